```python
import math
import jax, jax.numpy as jnp
from jax import lax
import numpy as np

D_MODEL = 1024
BATCH = 8
SEQ = 8192
DEPTH = 2

N_MIXERS = 2
D_FF = 2816
RMS_EPS = 1e-6
LN_EPS = 1e-5
DN_HEAD_DIM = 128
DN_HEADS = D_MODEL // DN_HEAD_DIM
DN_WIDTH = DN_HEADS * DN_HEAD_DIM
DN_CONV = 4
DN_CHUNK = 64
SG_WIDTH = 2 * D_MODEL
SG_GROUPS = 8
SG_CHUNK = 128
N_A = (DEPTH + 1) // 2
N_B = DEPTH // 2

kernel_name = 'hybrid_deltanet_spatialgate_macaron'


def rmsnorm(x, g, eps=RMS_EPS):
    xf = x.astype(jnp.float32)
    y = xf * lax.rsqrt(jnp.mean(xf * xf, axis=-1, keepdims=True) + eps)
    return (y * g.astype(jnp.float32)).astype(x.dtype)


def layernorm(x, g, b, eps=LN_EPS):
    xf = x.astype(jnp.float32)
    mu = jnp.mean(xf, axis=-1, keepdims=True)
    xc = xf - mu
    y = xc * lax.rsqrt(jnp.mean(xc * xc, axis=-1, keepdims=True) + eps)
    return (y * g.astype(jnp.float32) + b.astype(jnp.float32)).astype(x.dtype)


def l2norm(x, eps=1e-6):
    return x * lax.rsqrt(jnp.sum(x * x, axis=-1, keepdims=True) + eps)


def swiglu(h, w_gate, w_up, w_down):
    return (jax.nn.silu(h @ w_gate) * (h @ w_up)) @ w_down


def causal_short_conv(x, w):
    K = w.shape[0]
    S = x.shape[1]
    xp = jnp.pad(x, ((0, 0), (K - 1, 0), (0, 0)))
    return sum(xp[:, j:j + S, :] * w[j] for j in range(K))


def gated_delta_rule(q, k, v, g, beta):
    B, H, S, Dk = q.shape
    Dv = v.shape[-1]
    C = DN_CHUNK
    N = S // C
    q = q * (Dk ** -0.5)
    q = q.reshape(B, H, N, C, Dk)
    k = k.reshape(B, H, N, C, Dk)
    v = v.reshape(B, H, N, C, Dv)
    g = g.reshape(B, H, N, C)
    beta = beta.reshape(B, H, N, C)
    gc = jnp.cumsum(g, axis=-1)
    causal = jnp.tril(jnp.ones((C, C), dtype=bool))
    strict = jnp.tril(jnp.ones((C, C), dtype=bool), -1)
    diff = gc[..., :, None] - gc[..., None, :]
    decay = jnp.where(causal, jnp.exp(jnp.where(causal, diff, 0.0)), 0.0)
    k_beta = k * beta[..., None]
    v_beta = v * beta[..., None]
    L = jnp.where(strict, jnp.einsum('bhnid,bhnjd->bhnij', k_beta, k) * decay, 0.0)
    A = L + jnp.eye(C, dtype=jnp.float32)
    rhs = jnp.concatenate([v_beta, k_beta * jnp.exp(gc)[..., None]], axis=-1)
    sol = lax.linalg.triangular_solve(A, rhs, left_side=True, lower=True, unit_diagonal=True)
    u = sol[..., :Dv]
    w = sol[..., Dv:]
    attn = jnp.where(causal, jnp.einsum('bhnid,bhnjd->bhnij', q, k) * decay, 0.0)
    q_dec = q * jnp.exp(gc)[..., None]
    k_dec = k * jnp.exp(gc[..., -1:] - gc)[..., None]
    g_last = jnp.exp(gc[..., -1])

    def step(state, xs):
        u_i, w_i, attn_i, qd_i, kd_i, gl_i = xs
        v_new = u_i - jnp.einsum('bhck,bhkv->bhcv', w_i, state)
        o = jnp.einsum('bhck,bhkv->bhcv', qd_i, state) + jnp.einsum('bhij,bhjv->bhiv', attn_i, v_new)
        state = state * gl_i[..., None, None] + jnp.einsum('bhck,bhcv->bhkv', kd_i, v_new)
        return state, o

    xs = tuple(jnp.moveaxis(t, 2, 0) for t in (u, w, attn, q_dec, k_dec, g_last))
    s0 = jnp.zeros((B, H, Dk, Dv), jnp.float32)
    _, o = lax.scan(step, s0, xs)
    return jnp.moveaxis(o, 0, 2).reshape(B, H, S, Dv)


def gated_deltanet(h, w_in, conv_w, a_log, dt_bias, norm_g, w_out):
    B, S, _ = h.shape
    H, Dh, W = DN_HEADS, DN_HEAD_DIM, DN_WIDTH
    f32 = jnp.float32
    proj = h @ w_in
    qkv = proj[..., :3 * W]
    z = proj[..., 3 * W:4 * W]
    b_raw = proj[..., 4 * W:4 * W + H]
    a_raw = proj[..., 4 * W + H:]
    qkv = jax.nn.silu(causal_short_conv(qkv, conv_w))
    q, k, v = jnp.split(qkv, 3, axis=-1)
    to_heads = lambda t: t.reshape(B, S, H, Dh).transpose(0, 2, 1, 3).astype(f32)
    q = l2norm(to_heads(q))
    k = l2norm(to_heads(k))
    v = to_heads(v)
    beta = jax.nn.sigmoid(b_raw.astype(f32)).transpose(0, 2, 1)
    g = (-jnp.exp(a_log.astype(f32)) *
         jax.nn.softplus(a_raw.astype(f32) + dt_bias.astype(f32))).transpose(0, 2, 1)
    o = gated_delta_rule(q, k, v, g, beta).transpose(0, 2, 1, 3)
    o = rmsnorm(o, norm_g) * jax.nn.silu(z.reshape(B, S, H, Dh).astype(f32))
    return o.reshape(B, S, W).astype(h.dtype) @ w_out


def spatial_gating(h, w_in, b_in, ln_g, ln_b, w_s, b_s, w_out):
    B, S, _ = h.shape
    E, G, C = SG_WIDTH, SG_GROUPS, SG_CHUNK
    N = S // C
    zz = jax.nn.gelu(h @ w_in + b_in, approximate=False)
    u = zz[..., :E]
    v = layernorm(zz[..., E:], ln_g, ln_b)
    mask = jnp.tril(jnp.ones((C, C), dtype=bool))
    w_c = jnp.where(mask, w_s, 0.0)
    vg = v.reshape(B, N, C, G, E // G)
    mixed = jnp.einsum('gts,bnsgc->bntgc', w_c, vg) + b_s.T[None, None, :, :, None]
    return (u * mixed.reshape(B, S, E)) @ w_out


def _fwd_setup_inputs(seed: int = 0) -> dict:
    key = jax.random.key(seed)
    ks = jax.random.split(key, 20)
    D, F, H, Dh, W = D_MODEL, D_FF, DN_HEADS, DN_HEAD_DIM, DN_WIDTH
    E, G, C = SG_WIDTH, SG_GROUPS, SG_CHUNK
    nrm = jax.random.normal
    x = nrm(ks[0], (BATCH, SEQ, D), jnp.float32)
    norm_g = 1.0 + 0.02 * nrm(ks[1], (DEPTH, 6, D), jnp.float32)
    ffn_w_gate = nrm(ks[2], (DEPTH, 2, D, F), jnp.float32) * D ** -0.5
    ffn_w_up = nrm(ks[3], (DEPTH, 2, D, F), jnp.float32) * D ** -0.5
    ffn_w_down = nrm(ks[4], (DEPTH, 2, F, D), jnp.float32) * F ** -0.5
    dn_w_in = nrm(ks[5], (N_A, D, 4 * W + 2 * H), jnp.float32) * D ** -0.5
    dn_conv_w = nrm(ks[6], (N_A, DN_CONV, 3 * W), jnp.float32) * DN_CONV ** -0.5
    dn_a_log = jnp.log(jax.random.uniform(ks[7], (N_A, H), jnp.float32, minval=1.0, maxval=16.0))
    dt = jnp.exp(jax.random.uniform(ks[8], (N_A, H), jnp.float32,
                                    minval=math.log(1e-3), maxval=math.log(1e-1)))
    dn_dt_bias = dt + jnp.log(-jnp.expm1(-dt))
    dn_norm_g = 1.0 + 0.02 * nrm(ks[9], (N_A, Dh), jnp.float32)
    dn_w_out = nrm(ks[10], (N_A, W, D), jnp.float32) * W ** -0.5
    sg_w_in = nrm(ks[11], (N_B, D, 2 * E), jnp.float32) * D ** -0.5
    sg_b_in = 0.02 * nrm(ks[12], (N_B, 2 * E), jnp.float32)
    sg_ln_g = 1.0 + 0.02 * nrm(ks[13], (N_B, E), jnp.float32)
    sg_ln_b = 0.02 * nrm(ks[14], (N_B, E), jnp.float32)
    sg_w_s = nrm(ks[15], (N_B, G, C, C), jnp.float32) * C ** -0.5
    sg_b_s = 1.0 + 0.02 * nrm(ks[16], (N_B, G, C), jnp.float32)
    sg_w_out = nrm(ks[17], (N_B, E, D), jnp.float32) * E ** -0.5
    return {'x': x, 'norm_g': norm_g, 'ffn_w_gate': ffn_w_gate, 'ffn_w_up': ffn_w_up,
            'ffn_w_down': ffn_w_down, 'dn_w_in': dn_w_in, 'dn_conv_w': dn_conv_w,
            'dn_a_log': dn_a_log, 'dn_dt_bias': dn_dt_bias, 'dn_norm_g': dn_norm_g,
            'dn_w_out': dn_w_out, 'sg_w_in': sg_w_in, 'sg_b_in': sg_b_in, 'sg_ln_g': sg_ln_g,
            'sg_ln_b': sg_ln_b, 'sg_w_s': sg_w_s, 'sg_b_s': sg_b_s, 'sg_w_out': sg_w_out}


def _fwd_reference(x, norm_g, ffn_w_gate, ffn_w_up, ffn_w_down, dn_w_in, dn_conv_w, dn_a_log,
              dn_dt_bias, dn_norm_g, dn_w_out, sg_w_in, sg_b_in, sg_ln_g, sg_ln_b, sg_w_s,
              sg_b_s, sg_w_out):
    for i in range(DEPTH):
        ng = norm_g[i]
        h = rmsnorm(x, ng[0])
        x = x + 0.5 * rmsnorm(swiglu(h, ffn_w_gate[i, 0], ffn_w_up[i, 0], ffn_w_down[i, 0]), ng[1])
        h = rmsnorm(x, ng[2])
        j = i // N_MIXERS
        if i % N_MIXERS == 0:
            m = gated_deltanet(h, dn_w_in[j], dn_conv_w[j], dn_a_log[j], dn_dt_bias[j],
                               dn_norm_g[j], dn_w_out[j])
        else:
            m = spatial_gating(h, sg_w_in[j], sg_b_in[j], sg_ln_g[j], sg_ln_b[j], sg_w_s[j],
                               sg_b_s[j], sg_w_out[j])
        x = x + rmsnorm(m, ng[3])
        h = rmsnorm(x, ng[4])
        x = x + 0.5 * rmsnorm(swiglu(h, ffn_w_gate[i, 1], ffn_w_up[i, 1], ffn_w_down[i, 1]), ng[5])
    return x


import jax as _jax
import jax.numpy as _jnp

TWIN_FORMAT = 'train_step'
FWD_PARAMS = ['x', 'norm_g', 'ffn_w_gate', 'ffn_w_up', 'ffn_w_down', 'dn_w_in', 'dn_conv_w', 'dn_a_log', 'dn_dt_bias', 'dn_norm_g', 'dn_w_out', 'sg_w_in', 'sg_b_in', 'sg_ln_g', 'sg_ln_b', 'sg_w_s', 'sg_b_s', 'sg_w_out']
TWIN_WEIGHTS = ['norm_g', 'ffn_w_gate', 'ffn_w_up', 'ffn_w_down', 'dn_w_in', 'dn_conv_w', 'dn_a_log', 'dn_dt_bias', 'dn_norm_g', 'dn_w_out', 'sg_w_in', 'sg_b_in', 'sg_ln_g', 'sg_ln_b', 'sg_w_s', 'sg_b_s', 'sg_w_out']
TWIN_DIFF_INPUT = 'x'
TWIN_INPUTS = ['x', 'norm_g', 'ffn_w_gate', 'ffn_w_up', 'ffn_w_down', 'dn_w_in', 'dn_conv_w', 'dn_a_log', 'dn_dt_bias', 'dn_norm_g', 'dn_w_out', 'sg_w_in', 'sg_b_in', 'sg_ln_g', 'sg_ln_b', 'sg_w_s', 'sg_b_s', 'sg_w_out', 'loss_target', 'm_norm_g', 'm_ffn_w_gate', 'm_ffn_w_up', 'm_ffn_w_down', 'm_dn_w_in', 'm_dn_conv_w', 'm_dn_a_log', 'm_dn_dt_bias', 'm_dn_norm_g', 'm_dn_w_out', 'm_sg_w_in', 'm_sg_b_in', 'm_sg_ln_g', 'm_sg_ln_b', 'm_sg_w_s', 'm_sg_b_s', 'm_sg_w_out', 'v_norm_g', 'v_ffn_w_gate', 'v_ffn_w_up', 'v_ffn_w_down', 'v_dn_w_in', 'v_dn_conv_w', 'v_dn_a_log', 'v_dn_dt_bias', 'v_dn_norm_g', 'v_dn_w_out', 'v_sg_w_in', 'v_sg_b_in', 'v_sg_ln_g', 'v_sg_ln_b', 'v_sg_w_s', 'v_sg_b_s', 'v_sg_w_out']
TWIN_OUTPUTS = ['loss', 'grad_x', 'grad_norm_g', 'grad_ffn_w_gate', 'grad_ffn_w_up', 'grad_ffn_w_down', 'grad_dn_w_in', 'grad_dn_conv_w', 'grad_dn_a_log', 'grad_dn_dt_bias', 'grad_dn_norm_g', 'grad_dn_w_out', 'grad_sg_w_in', 'grad_sg_b_in', 'grad_sg_ln_g', 'grad_sg_ln_b', 'grad_sg_w_s', 'grad_sg_b_s', 'grad_sg_w_out', 'delta_norm_g', 'delta_ffn_w_gate', 'delta_ffn_w_up', 'delta_ffn_w_down', 'delta_dn_w_in', 'delta_dn_conv_w', 'delta_dn_a_log', 'delta_dn_dt_bias', 'delta_dn_norm_g', 'delta_dn_w_out', 'delta_sg_w_in', 'delta_sg_b_in', 'delta_sg_ln_g', 'delta_sg_ln_b', 'delta_sg_w_s', 'delta_sg_b_s', 'delta_sg_w_out', 'new_m_norm_g', 'new_m_ffn_w_gate', 'new_m_ffn_w_up', 'new_m_ffn_w_down', 'new_m_dn_w_in', 'new_m_dn_conv_w', 'new_m_dn_a_log', 'new_m_dn_dt_bias', 'new_m_dn_norm_g', 'new_m_dn_w_out', 'new_m_sg_w_in', 'new_m_sg_b_in', 'new_m_sg_ln_g', 'new_m_sg_ln_b', 'new_m_sg_w_s', 'new_m_sg_b_s', 'new_m_sg_w_out', 'new_v_norm_g', 'new_v_ffn_w_gate', 'new_v_ffn_w_up', 'new_v_ffn_w_down', 'new_v_dn_w_in', 'new_v_dn_conv_w', 'new_v_dn_a_log', 'new_v_dn_dt_bias', 'new_v_dn_norm_g', 'new_v_dn_w_out', 'new_v_sg_w_in', 'new_v_sg_b_in', 'new_v_sg_ln_g', 'new_v_sg_ln_b', 'new_v_sg_w_s', 'new_v_sg_b_s', 'new_v_sg_w_out']
TWIN_LEAF_KINDS = {'loss': 'loss', 'grad_x': 'grad_x', 'grad_norm_g': 'grad_w', 'grad_ffn_w_gate': 'grad_w', 'grad_ffn_w_up': 'grad_w', 'grad_ffn_w_down': 'grad_w', 'grad_dn_w_in': 'grad_w', 'grad_dn_conv_w': 'grad_w', 'grad_dn_a_log': 'grad_w', 'grad_dn_dt_bias': 'grad_w', 'grad_dn_norm_g': 'grad_w', 'grad_dn_w_out': 'grad_w', 'grad_sg_w_in': 'grad_w', 'grad_sg_b_in': 'grad_w', 'grad_sg_ln_g': 'grad_w', 'grad_sg_ln_b': 'grad_w', 'grad_sg_w_s': 'grad_w', 'grad_sg_b_s': 'grad_w', 'grad_sg_w_out': 'grad_w', 'delta_norm_g': 'delta_w', 'delta_ffn_w_gate': 'delta_w', 'delta_ffn_w_up': 'delta_w', 'delta_ffn_w_down': 'delta_w', 'delta_dn_w_in': 'delta_w', 'delta_dn_conv_w': 'delta_w', 'delta_dn_a_log': 'delta_w', 'delta_dn_dt_bias': 'delta_w', 'delta_dn_norm_g': 'delta_w', 'delta_dn_w_out': 'delta_w', 'delta_sg_w_in': 'delta_w', 'delta_sg_b_in': 'delta_w', 'delta_sg_ln_g': 'delta_w', 'delta_sg_ln_b': 'delta_w', 'delta_sg_w_s': 'delta_w', 'delta_sg_b_s': 'delta_w', 'delta_sg_w_out': 'delta_w', 'new_m_norm_g': 'new_m', 'new_m_ffn_w_gate': 'new_m', 'new_m_ffn_w_up': 'new_m', 'new_m_ffn_w_down': 'new_m', 'new_m_dn_w_in': 'new_m', 'new_m_dn_conv_w': 'new_m', 'new_m_dn_a_log': 'new_m', 'new_m_dn_dt_bias': 'new_m', 'new_m_dn_norm_g': 'new_m', 'new_m_dn_w_out': 'new_m', 'new_m_sg_w_in': 'new_m', 'new_m_sg_b_in': 'new_m', 'new_m_sg_ln_g': 'new_m', 'new_m_sg_ln_b': 'new_m', 'new_m_sg_w_s': 'new_m', 'new_m_sg_b_s': 'new_m', 'new_m_sg_w_out': 'new_m', 'new_v_norm_g': 'new_v', 'new_v_ffn_w_gate': 'new_v', 'new_v_ffn_w_up': 'new_v', 'new_v_ffn_w_down': 'new_v', 'new_v_dn_w_in': 'new_v', 'new_v_dn_conv_w': 'new_v', 'new_v_dn_a_log': 'new_v', 'new_v_dn_dt_bias': 'new_v', 'new_v_dn_norm_g': 'new_v', 'new_v_dn_w_out': 'new_v', 'new_v_sg_w_in': 'new_v', 'new_v_sg_b_in': 'new_v', 'new_v_sg_ln_g': 'new_v', 'new_v_sg_ln_b': 'new_v', 'new_v_sg_w_s': 'new_v', 'new_v_sg_b_s': 'new_v', 'new_v_sg_w_out': 'new_v'}


def _forward(args):
    return _fwd_reference(*[args[k] for k in FWD_PARAMS])


def _output_shape():
    def fwd():
        inp = _fwd_setup_inputs(0)
        return _fwd_reference(*[inp[k] for k in FWD_PARAMS])
    out = _jax.eval_shape(fwd)
    return out.shape, out.dtype

N_MICROBATCH = 1
ADAM_LR = 0.001
ADAM_B1 = 0.9
ADAM_B2 = 0.999
ADAM_EPS = 1e-08
ADAM_WD = 0.01
ADAM_STEP = 10
PER_EXAMPLE_BATCH_AXIS = {'x': 0, 'loss_target': 0}
SHARED_INPUTS = []
_WEIGHT_DTYPES = {'norm_g': _jnp.float32, 'ffn_w_gate': _jnp.float32, 'ffn_w_up': _jnp.float32, 'ffn_w_down': _jnp.float32, 'dn_w_in': _jnp.float32, 'dn_conv_w': _jnp.float32, 'dn_a_log': _jnp.float32, 'dn_dt_bias': _jnp.float32, 'dn_norm_g': _jnp.float32, 'dn_w_out': _jnp.float32, 'sg_w_in': _jnp.float32, 'sg_b_in': _jnp.float32, 'sg_ln_g': _jnp.float32, 'sg_ln_b': _jnp.float32, 'sg_w_s': _jnp.float32, 'sg_b_s': _jnp.float32, 'sg_w_out': _jnp.float32}
MOMENT_SCALE = {'norm_g': 2.794200e+01, 'ffn_w_gate': 3.898424e-01, 'ffn_w_up': 5.609837e-01, 'ffn_w_down': 9.385510e-01, 'dn_w_in': 8.376115e-01, 'dn_conv_w': 1.920625e+00, 'dn_a_log': 4.855552e+00, 'dn_dt_bias': 4.881912e+00, 'dn_norm_g': 1.700824e+01, 'dn_w_out': 5.486721e+00, 'sg_w_in': 1.111106e+00, 'sg_b_in': 4.787765e+00, 'sg_ln_g': 2.198948e-01, 'sg_ln_b': 2.356387e-01, 'sg_w_s': 3.023552e-01, 'sg_b_s': 5.273542e-01, 'sg_w_out': 5.949960e+00}


def _to_microbatches(a, axis):
    t = _jnp.moveaxis(a, axis, 0)
    t = t.reshape((N_MICROBATCH, t.shape[0] // N_MICROBATCH) + t.shape[1:])
    return _jnp.moveaxis(t, 1, axis + 1)


def setup_inputs(seed: int = 0) -> dict:
    inp = _fwd_setup_inputs(seed)
    key = _jax.random.fold_in(_jax.random.key(seed), 7919)
    shape, _ = _output_shape()
    out = dict(inp)
    out["loss_target"] = _jax.random.normal(_jax.random.fold_in(key, 0), shape, _jnp.float32)
    for i, name in enumerate(TWIN_WEIGHTS):
        w = inp[name].astype(_jnp.float32)
        if MOMENT_SCALE is None:
            s = _jnp.sqrt(_jnp.mean(_jnp.square(w)) + 1e-30)
        else:
            s = MOMENT_SCALE[name]
        km, kv = _jax.random.split(_jax.random.fold_in(key, i + 1))
        out[name] = w
        out["m_" + name] = s * _jax.random.normal(km, w.shape, _jnp.float32)
        out["v_" + name] = (s * s) * _jax.random.uniform(kv, w.shape, _jnp.float32, 0.5, 1.5)
    if N_MICROBATCH > 1:
        for name, axis in PER_EXAMPLE_BATCH_AXIS.items():
            out[name] = _to_microbatches(out[name], axis)
    return {'x': out['x'], 'norm_g': out['norm_g'], 'ffn_w_gate': out['ffn_w_gate'], 'ffn_w_up': out['ffn_w_up'], 'ffn_w_down': out['ffn_w_down'], 'dn_w_in': out['dn_w_in'], 'dn_conv_w': out['dn_conv_w'], 'dn_a_log': out['dn_a_log'], 'dn_dt_bias': out['dn_dt_bias'], 'dn_norm_g': out['dn_norm_g'], 'dn_w_out': out['dn_w_out'], 'sg_w_in': out['sg_w_in'], 'sg_b_in': out['sg_b_in'], 'sg_ln_g': out['sg_ln_g'], 'sg_ln_b': out['sg_ln_b'], 'sg_w_s': out['sg_w_s'], 'sg_b_s': out['sg_b_s'], 'sg_w_out': out['sg_w_out'], 'loss_target': out['loss_target'], 'm_norm_g': out['m_norm_g'], 'm_ffn_w_gate': out['m_ffn_w_gate'], 'm_ffn_w_up': out['m_ffn_w_up'], 'm_ffn_w_down': out['m_ffn_w_down'], 'm_dn_w_in': out['m_dn_w_in'], 'm_dn_conv_w': out['m_dn_conv_w'], 'm_dn_a_log': out['m_dn_a_log'], 'm_dn_dt_bias': out['m_dn_dt_bias'], 'm_dn_norm_g': out['m_dn_norm_g'], 'm_dn_w_out': out['m_dn_w_out'], 'm_sg_w_in': out['m_sg_w_in'], 'm_sg_b_in': out['m_sg_b_in'], 'm_sg_ln_g': out['m_sg_ln_g'], 'm_sg_ln_b': out['m_sg_ln_b'], 'm_sg_w_s': out['m_sg_w_s'], 'm_sg_b_s': out['m_sg_b_s'], 'm_sg_w_out': out['m_sg_w_out'], 'v_norm_g': out['v_norm_g'], 'v_ffn_w_gate': out['v_ffn_w_gate'], 'v_ffn_w_up': out['v_ffn_w_up'], 'v_ffn_w_down': out['v_ffn_w_down'], 'v_dn_w_in': out['v_dn_w_in'], 'v_dn_conv_w': out['v_dn_conv_w'], 'v_dn_a_log': out['v_dn_a_log'], 'v_dn_dt_bias': out['v_dn_dt_bias'], 'v_dn_norm_g': out['v_dn_norm_g'], 'v_dn_w_out': out['v_dn_w_out'], 'v_sg_w_in': out['v_sg_w_in'], 'v_sg_b_in': out['v_sg_b_in'], 'v_sg_ln_g': out['v_sg_ln_g'], 'v_sg_ln_b': out['v_sg_ln_b'], 'v_sg_w_s': out['v_sg_w_s'], 'v_sg_b_s': out['v_sg_b_s'], 'v_sg_w_out': out['v_sg_w_out']}


def _loss(weights, diff, rest, loss_target):
    with _jax.named_scope("forward"):
        args = {**rest, TWIN_DIFF_INPUT: diff, **{k: w.astype(_WEIGHT_DTYPES[k]) for k, w in weights.items()}}
        y = _forward(args)
    with _jax.named_scope("loss_head"):
        err = _jnp.square(y.astype(_jnp.float32) - loss_target)
        return 0.5 * _jnp.sum(_jnp.mean(err, axis=-1)) if err.ndim else 0.5 * err


def _adamw(w, g, m, v):
    m = ADAM_B1 * m + (1.0 - ADAM_B1) * g
    v = ADAM_B2 * v + (1.0 - ADAM_B2) * _jnp.square(g)
    m_hat = m / (1.0 - ADAM_B1 ** ADAM_STEP)
    v_hat = v / (1.0 - ADAM_B2 ** ADAM_STEP)
    delta = -ADAM_LR * (m_hat / (_jnp.sqrt(v_hat) + ADAM_EPS) + ADAM_WD * w)
    return delta, m, v


def reference(x, norm_g, ffn_w_gate, ffn_w_up, ffn_w_down, dn_w_in, dn_conv_w, dn_a_log, dn_dt_bias, dn_norm_g, dn_w_out, sg_w_in, sg_b_in, sg_ln_g, sg_ln_b, sg_w_s, sg_b_s, sg_w_out, loss_target, m_norm_g, m_ffn_w_gate, m_ffn_w_up, m_ffn_w_down, m_dn_w_in, m_dn_conv_w, m_dn_a_log, m_dn_dt_bias, m_dn_norm_g, m_dn_w_out, m_sg_w_in, m_sg_b_in, m_sg_ln_g, m_sg_ln_b, m_sg_w_s, m_sg_b_s, m_sg_w_out, v_norm_g, v_ffn_w_gate, v_ffn_w_up, v_ffn_w_down, v_dn_w_in, v_dn_conv_w, v_dn_a_log, v_dn_dt_bias, v_dn_norm_g, v_dn_w_out, v_sg_w_in, v_sg_b_in, v_sg_ln_g, v_sg_ln_b, v_sg_w_s, v_sg_b_s, v_sg_w_out):
    given = dict(x=x, norm_g=norm_g, ffn_w_gate=ffn_w_gate, ffn_w_up=ffn_w_up, ffn_w_down=ffn_w_down, dn_w_in=dn_w_in, dn_conv_w=dn_conv_w, dn_a_log=dn_a_log, dn_dt_bias=dn_dt_bias, dn_norm_g=dn_norm_g, dn_w_out=dn_w_out, sg_w_in=sg_w_in, sg_b_in=sg_b_in, sg_ln_g=sg_ln_g, sg_ln_b=sg_ln_b, sg_w_s=sg_w_s, sg_b_s=sg_b_s, sg_w_out=sg_w_out, loss_target=loss_target, m_norm_g=m_norm_g, m_ffn_w_gate=m_ffn_w_gate, m_ffn_w_up=m_ffn_w_up, m_ffn_w_down=m_ffn_w_down, m_dn_w_in=m_dn_w_in, m_dn_conv_w=m_dn_conv_w, m_dn_a_log=m_dn_a_log, m_dn_dt_bias=m_dn_dt_bias, m_dn_norm_g=m_dn_norm_g, m_dn_w_out=m_dn_w_out, m_sg_w_in=m_sg_w_in, m_sg_b_in=m_sg_b_in, m_sg_ln_g=m_sg_ln_g, m_sg_ln_b=m_sg_ln_b, m_sg_w_s=m_sg_w_s, m_sg_b_s=m_sg_b_s, m_sg_w_out=m_sg_w_out, v_norm_g=v_norm_g, v_ffn_w_gate=v_ffn_w_gate, v_ffn_w_up=v_ffn_w_up, v_ffn_w_down=v_ffn_w_down, v_dn_w_in=v_dn_w_in, v_dn_conv_w=v_dn_conv_w, v_dn_a_log=v_dn_a_log, v_dn_dt_bias=v_dn_dt_bias, v_dn_norm_g=v_dn_norm_g, v_dn_w_out=v_dn_w_out, v_sg_w_in=v_sg_w_in, v_sg_b_in=v_sg_b_in, v_sg_ln_g=v_sg_ln_g, v_sg_ln_b=v_sg_ln_b, v_sg_w_s=v_sg_w_s, v_sg_b_s=v_sg_b_s, v_sg_w_out=v_sg_w_out)
    weights = {n: given[n] for n in TWIN_WEIGHTS}
    shared = {n: given[n] for n in SHARED_INPUTS}
    per_example = {n: given[n] for n in ['x']}
    grad_fn = _jax.value_and_grad(_loss, argnums=(0, 1))

    def one_microbatch(ex, loss_target):
        ex = dict(ex)
        diff = ex.pop(TWIN_DIFF_INPUT)
        return grad_fn(weights, diff, {**shared, **ex}, loss_target)

    if N_MICROBATCH == 1:
        loss, (grad_w, grad_x) = one_microbatch(per_example, given["loss_target"])
    else:
        def body(carry, xs):
            loss_sum, grad_sum = carry
            l_k, (gw_k, gx_k) = one_microbatch(xs[0], xs[1])
            with _jax.named_scope("update"):
                return (loss_sum + l_k, _jax.tree.map(_jnp.add, grad_sum, gw_k)), gx_k

        init = (_jnp.zeros((), _jnp.float32), _jax.tree.map(_jnp.zeros_like, weights))
        (loss, grad_w), grad_x = _jax.lax.scan(body, init, (per_example, given["loss_target"]))
    with _jax.named_scope("update"):
        delta_w, new_m, new_v = {}, {}, {}
        for n in TWIN_WEIGHTS:
            delta_w[n], new_m[n], new_v[n] = _adamw(weights[n], grad_w[n], given["m_" + n], given["v_" + n])
    return (loss, grad_x, *[grad_w[n] for n in TWIN_WEIGHTS], *[delta_w[n] for n in TWIN_WEIGHTS],
            *[new_m[n] for n in TWIN_WEIGHTS], *[new_v[n] for n in TWIN_WEIGHTS])
```

```python
import functools
import math

import jax
import jax.numpy as jnp
from jax import lax
from jax.experimental import pallas as pl
from jax.experimental.pallas import tpu as pltpu

F32 = jnp.float32
MXU_DTYPE = jnp.bfloat16
HI = lax.Precision.HIGHEST

D_MODEL = 1024
D_FF = 2816
RMS_EPS = 1e-6
LN_EPS = 1e-5
L2_EPS = 1e-6
DN_HEADS = 8
DN_HEAD_DIM = 128
DN_CONV = 4
DN_CHUNK = 64
SG_WIDTH = 2048
SG_GROUPS = 8
SG_CHUNK = 128
SG_GROUP_W = SG_WIDTH // SG_GROUPS
N_CHIPS = 4
N_DEV = 8
LANES = 128
SUBLANES = 8
VMEM_LIMIT = 56 * 1024 * 1024

ADAM_LR = 0.001
ADAM_B1 = 0.9
ADAM_B2 = 0.999
ADAM_EPS = 1e-08
ADAM_WD = 0.01
ADAM_STEP = 10

MESH = pl.DeviceIdType.MESH
ANY = pl.BlockSpec(memory_space=pl.ANY)


def _cp(*sem):
    return pltpu.CompilerParams(dimension_semantics=sem, vmem_limit_bytes=VMEM_LIMIT)


def _pick(n, pref, mult=LANES):
    best = None
    d = mult
    while d <= min(n, pref):
        if n % d == 0:
            best = d
        d += mult
    return best if best is not None else n


def _full(shape):
    nd = len(shape)
    return pl.BlockSpec(shape, lambda *_: (0,) * nd)


def _sigmoid(x):
    return 1.0 / (1.0 + jnp.exp(-x))


def _dot(a, b, dims, prec=None):
    return lax.dot_general(a, b, (dims, ((), ())), preferred_element_type=F32, precision=prec)


NN = ((1,), (0,))
NT = ((1,), (1,))
TN = ((0,), (0,))


def _mx(a):
    return a.astype(MXU_DTYPE)


def _rms_stat(x):
    return lax.rsqrt(jnp.mean(x * x, axis=-1, keepdims=True) + RMS_EPS)


def _rms_bwd(x, r, g, dy):
    xh = x * r
    dxh = dy * g
    dx = r * (dxh - xh * jnp.mean(dxh * xh, axis=-1, keepdims=True))
    return dx, jnp.sum(dy * xh, axis=0, keepdims=True)


def _mm(a, b, mode, name, out_dtype=F32, add=None):
    if mode == "tn":
        K, M = a.shape
        N = b.shape[1]
    elif mode == "nt":
        M, K = a.shape
        N = b.shape[0]
    else:
        M, K = a.shape
        N = b.shape[1]
    tn = _pick(N, 1024)
    if mode == "tn":
        tm = _pick(M, 1024 if tn <= 512 else 1408)
        tk = _pick(K, 1024, SUBLANES)
    else:
        tm = _pick(M, max(512, min(2048, (512 * 1024) // tn)), SUBLANES)
        tk = _pick(K, 2048)
    nk = K // tk
    grid = (N // tn, M // tm, nk)
    if mode == "nn":
        a_spec = pl.BlockSpec((tm, tk), lambda j, i, k: (i, k))
        b_spec = pl.BlockSpec((tk, tn), lambda j, i, k: (k, j))
        dims = NN
    elif mode == "nt":
        a_spec = pl.BlockSpec((tm, tk), lambda j, i, k: (i, k))
        b_spec = pl.BlockSpec((tn, tk), lambda j, i, k: (j, k))
        dims = NT
    else:
        a_spec = pl.BlockSpec((tk, tm), lambda j, i, k: (k, i))
        b_spec = pl.BlockSpec((tk, tn), lambda j, i, k: (k, j))
        dims = TN
    o_spec = pl.BlockSpec((tm, tn), lambda j, i, k: (i, j))
    has_add = add is not None

    def body(*refs):
        if has_add:
            a_ref, b_ref, add_ref, o_ref, acc = refs
        else:
            a_ref, b_ref, o_ref, acc = refs
        k = pl.program_id(2)

        @pl.when(k == 0)
        def _():
            acc[...] = add_ref[...] if has_add else jnp.zeros_like(acc)

        acc[...] += _dot(a_ref[...], b_ref[...], dims)

        @pl.when(k == nk - 1)
        def _():
            o_ref[...] = acc[...].astype(o_ref.dtype)

    ins = [a, b] + ([add] if has_add else [])
    specs = [a_spec, b_spec] + ([o_spec] if has_add else [])
    return pl.pallas_call(
        body, name=name, grid=grid, in_specs=specs, out_specs=o_spec,
        out_shape=jax.ShapeDtypeStruct((M, N), out_dtype),
        scratch_shapes=[pltpu.VMEM((tm, tn), F32)],
        compiler_params=_cp("parallel", "parallel", "arbitrary"),
    )(*ins)


def _load_resident(pairs, sem):
    @pl.when(pl.program_id(0) == 0)
    def _():
        cps = [pltpu.make_async_copy(src, dst, sem.at[i]) for i, (src, dst) in enumerate(pairs)]
        for c in cps:
            c.start()
        for c in cps:
            c.wait()


def _ffn_fwd(x, g0, g1, wgu, wd, name):
    T, D = x.shape
    F2 = wgu.shape[1]
    F = F2 // 2
    tm = _pick(T, 256, SUBLANES)

    def body(x_ref, g0_ref, g1_ref, wgu_hbm, wd_hbm, xo_ref, h_ref, gu_ref, y_ref, wgu_v, wd_v, sem):
        _load_resident([(wgu_hbm, wgu_v), (wd_hbm, wd_v)], sem)
        xv = x_ref[...]
        hb = _mx(xv * _rms_stat(xv) * g0_ref[...])
        h_ref[...] = hb
        gu = _dot(hb, wgu_v[...], NN)
        gu_ref[...] = gu
        g = gu[:, :F]
        u = gu[:, F:]
        a = _mx(g * _sigmoid(g) * u)
        y = _dot(a, wd_v[...], NN)
        y_ref[...] = y
        xo_ref[...] = xv + 0.5 * (y * _rms_stat(y) * g1_ref[...])

    row = lambda w: pl.BlockSpec((tm, w), lambda i: (i, 0))
    return pl.pallas_call(
        body, name=name, grid=(T // tm,),
        in_specs=[row(D), _full((1, D)), _full((1, D)), ANY, ANY],
        out_specs=[row(D), row(D), row(F2), row(D)],
        out_shape=[jax.ShapeDtypeStruct((T, D), F32), jax.ShapeDtypeStruct((T, D), MXU_DTYPE),
                   jax.ShapeDtypeStruct((T, F2), F32), jax.ShapeDtypeStruct((T, D), F32)],
        scratch_shapes=[pltpu.VMEM(wgu.shape, wgu.dtype), pltpu.VMEM(wd.shape, wd.dtype),
                        pltpu.SemaphoreType.DMA((2,))],
        compiler_params=_cp("arbitrary"),
    )(x, g0, g1, wgu, wd)


def _ffn_bwd_down(dxo, y, gu, g1, wd, name):
    T, D = y.shape
    F2 = gu.shape[1]
    F = F2 // 2
    tm = _pick(T, 256, SUBLANES)

    def body(dxo_ref, y_ref, gu_ref, g1_ref, wd_hbm, dy_ref, a_ref, dgu_ref, dg1_ref, wd_v, sem):
        _load_resident([(wd_hbm, wd_v)], sem)

        @pl.when(pl.program_id(0) == 0)
        def _():
            dg1_ref[...] = jnp.zeros_like(dg1_ref)

        yv = y_ref[...]
        dy, dg1 = _rms_bwd(yv, _rms_stat(yv), g1_ref[...], 0.5 * dxo_ref[...])
        dg1_ref[...] += dg1
        dyb = _mx(dy)
        dy_ref[...] = dyb
        da = _dot(dyb, wd_v[...], NT)
        gu_v = gu_ref[...]
        g = gu_v[:, :F]
        u = gu_v[:, F:]
        s = _sigmoid(g)
        sg = g * s
        a_ref[...] = _mx(sg * u)
        dgu_ref[:, :F] = _mx(da * u * (s * (1.0 + g * (1.0 - s))))
        dgu_ref[:, F:] = _mx(da * sg)

    row = lambda w: pl.BlockSpec((tm, w), lambda i: (i, 0))
    return pl.pallas_call(
        body, name=name, grid=(T // tm,),
        in_specs=[row(D), row(D), row(F2), _full((1, D)), ANY],
        out_specs=[row(D), row(F), row(F2), _full((1, D))],
        out_shape=[jax.ShapeDtypeStruct((T, D), MXU_DTYPE), jax.ShapeDtypeStruct((T, F), MXU_DTYPE),
                   jax.ShapeDtypeStruct((T, F2), MXU_DTYPE), jax.ShapeDtypeStruct((1, D), F32)],
        scratch_shapes=[pltpu.VMEM(wd.shape, wd.dtype), pltpu.SemaphoreType.DMA((1,))],
        compiler_params=_cp("arbitrary"),
    )(dxo, y, gu, g1, wd)


def _ffn_bwd_up(dgu, x, dxo, g0, wgu, name):
    T, D = x.shape
    F2 = dgu.shape[1]
    tm = _pick(T, 256, SUBLANES)

    def body(dgu_ref, x_ref, dxo_ref, g0_ref, wgu_hbm, dx_ref, dg0_ref, wgu_v, sem):
        _load_resident([(wgu_hbm, wgu_v)], sem)

        @pl.when(pl.program_id(0) == 0)
        def _():
            dg0_ref[...] = jnp.zeros_like(dg0_ref)

        dh = _dot(dgu_ref[...], wgu_v[...], NT)
        xv = x_ref[...]
        dx, dg0 = _rms_bwd(xv, _rms_stat(xv), g0_ref[...], dh)
        dg0_ref[...] += dg0
        dx_ref[...] = dxo_ref[...] + dx

    row = lambda w: pl.BlockSpec((tm, w), lambda i: (i, 0))
    return pl.pallas_call(
        body, name=name, grid=(T // tm,),
        in_specs=[row(F2), row(D), row(D), _full((1, D)), ANY],
        out_specs=[row(D), _full((1, D))],
        out_shape=[jax.ShapeDtypeStruct((T, D), F32), jax.ShapeDtypeStruct((1, D), F32)],
        scratch_shapes=[pltpu.VMEM(wgu.shape, wgu.dtype), pltpu.SemaphoreType.DMA((1,))],
        compiler_params=_cp("arbitrary"),
    )(dgu, x, dxo, g0, wgu)


def _norm_fwd(x, g, name):
    T, D = x.shape
    tm = _pick(T, 512, SUBLANES)

    def body(x_ref, g_ref, h_ref):
        xv = x_ref[...]
        h_ref[...] = _mx(xv * _rms_stat(xv) * g_ref[...])

    row = pl.BlockSpec((tm, D), lambda i: (i, 0))
    return pl.pallas_call(body, name=name, grid=(T // tm,), in_specs=[row, _full((1, D))], out_specs=row,
                          out_shape=jax.ShapeDtypeStruct((T, D), MXU_DTYPE), compiler_params=_cp("parallel"))(x, g)


def _postnorm_fwd(x, m, g, name):
    T, D = x.shape
    tm = _pick(T, 512, SUBLANES)

    def body(x_ref, m_ref, g_ref, o_ref):
        mv = m_ref[...]
        o_ref[...] = x_ref[...] + mv * _rms_stat(mv) * g_ref[...]

    row = pl.BlockSpec((tm, D), lambda i: (i, 0))
    return pl.pallas_call(body, name=name, grid=(T // tm,), in_specs=[row, row, _full((1, D))], out_specs=row,
                          out_shape=jax.ShapeDtypeStruct((T, D), F32), compiler_params=_cp("parallel"))(x, m, g)


def _postnorm_bwd(dxo, m, g, name):
    T, D = m.shape
    tm = _pick(T, 512, SUBLANES)

    def body(dxo_ref, m_ref, g_ref, dm_ref, dg_ref):
        @pl.when(pl.program_id(0) == 0)
        def _():
            dg_ref[...] = jnp.zeros_like(dg_ref)

        mv = m_ref[...]
        dm, dg = _rms_bwd(mv, _rms_stat(mv), g_ref[...], dxo_ref[...])
        dg_ref[...] += dg
        dm_ref[...] = _mx(dm)

    row = pl.BlockSpec((tm, D), lambda i: (i, 0))
    return pl.pallas_call(body, name=name, grid=(T // tm,), in_specs=[row, row, _full((1, D))],
                          out_specs=[row, _full((1, D))],
                          out_shape=[jax.ShapeDtypeStruct((T, D), MXU_DTYPE), jax.ShapeDtypeStruct((1, D), F32)],
                          compiler_params=_cp("arbitrary"))(dxo, m, g)


def _prenorm_bwd(dxo, dh, x, g, name):
    T, D = x.shape
    tm = _pick(T, 512, SUBLANES)

    def body(dxo_ref, dh_ref, x_ref, g_ref, dx_ref, dg_ref):
        @pl.when(pl.program_id(0) == 0)
        def _():
            dg_ref[...] = jnp.zeros_like(dg_ref)

        xv = x_ref[...]
        dx, dg = _rms_bwd(xv, _rms_stat(xv), g_ref[...], dh_ref[...])
        dg_ref[...] += dg
        dx_ref[...] = dxo_ref[...] + dx

    row = pl.BlockSpec((tm, D), lambda i: (i, 0))
    return pl.pallas_call(body, name=name, grid=(T // tm,), in_specs=[row, row, row, _full((1, D))],
                          out_specs=[row, _full((1, D))],
                          out_shape=[jax.ShapeDtypeStruct((T, D), F32), jax.ShapeDtypeStruct((1, D), F32)],
                          compiler_params=_cp("arbitrary"))(dxo, dh, x, g)


def _loss_fwd_bwd(y, target, name):
    T, D = y.shape
    tm = _pick(T, 512, SUBLANES)

    def body(y_ref, t_ref, l_ref, dy_ref):
        @pl.when(pl.program_id(0) == 0)
        def _():
            l_ref[...] = jnp.zeros_like(l_ref)

        e = y_ref[...] - t_ref[...]
        dy_ref[...] = e * (1.0 / D)
        l_ref[...] += 0.5 * jnp.sum(jnp.mean(e * e, axis=-1, keepdims=True), axis=0, keepdims=True)

    row = pl.BlockSpec((tm, D), lambda i: (i, 0))
    return pl.pallas_call(body, name=name, grid=(T // tm,), in_specs=[row, row],
                          out_specs=[_full((SUBLANES, LANES)), row],
                          out_shape=[jax.ShapeDtypeStruct((SUBLANES, LANES), F32), jax.ShapeDtypeStruct((T, D), F32)],
                          compiler_params=_cp("arbitrary"))(y, target)


DN_ROWS = 512


def _shift_down(prev8, cur, s):
    n = cur.shape[0]
    xx = jnp.concatenate([prev8, cur], axis=0)
    return pltpu.roll(xx, s, 0)[SUBLANES:SUBLANES + n, :]


def _shift_up(cur, next8, s):
    n = cur.shape[0]
    xx = jnp.concatenate([cur, next8], axis=0)
    return pltpu.roll(xx, n + SUBLANES - s, 0)[:n, :]


def _conv_tile(x_ref, w, r, rows):
    start = pl.multiple_of(r * rows, SUBLANES)
    cur = x_ref[pl.ds(start, rows), :]
    pstart = pl.multiple_of(jnp.maximum(start - SUBLANES, 0), SUBLANES)
    prev8 = jnp.where(r == 0, 0.0, x_ref[pl.ds(pstart, SUBLANES), :])
    taps = [_shift_down(prev8, cur, DN_CONV - 1 - j) if j < DN_CONV - 1 else cur for j in range(DN_CONV)]
    c = taps[0] * w[0:1, :]
    for j in range(1, DN_CONV):
        c = c + taps[j] * w[j:j + 1, :]
    return c, taps


def _dn_prep_fwd(proj, conv_w, name):
    T = proj.shape[0]
    W = DN_HEADS * DN_HEAD_DIM
    rows = min(DN_ROWS, T)
    n_inner = T // rows
    scale = DN_HEAD_DIM ** -0.5

    def body(x_ref, w_ref, o_ref):
        cb = pl.program_id(0)
        w = w_ref[...]
        is_qk = cb < 2 * DN_HEADS
        post = jnp.where(cb < DN_HEADS, scale, 1.0)

        def step(r, carry):
            c, _ = _conv_tile(x_ref, w, r, rows)
            s = c * _sigmoid(c)
            rinv = lax.rsqrt(jnp.sum(s * s, axis=-1, keepdims=True) + L2_EPS)
            o_ref[pl.ds(pl.multiple_of(r * rows, SUBLANES), rows), :] = jnp.where(is_qk, s * rinv * post, s)
            return carry

        lax.fori_loop(0, n_inner, step, 0)

    col = pl.BlockSpec((T, LANES), lambda j: (0, j))
    return pl.pallas_call(body, name=name, grid=(3 * W // LANES,),
                          in_specs=[col, pl.BlockSpec((DN_CONV, LANES), lambda j: (0, j))], out_specs=col,
                          out_shape=jax.ShapeDtypeStruct((T, 3 * W), F32), compiler_params=_cp("parallel"))(proj, conv_w)


def _dn_prep_bwd(proj, conv_w, dqkv, name):
    T = proj.shape[0]
    W = DN_HEADS * DN_HEAD_DIM
    rows = min(DN_ROWS, T)
    n_inner = T // rows
    scale = DN_HEAD_DIM ** -0.5

    def body(x_ref, w_ref, dy_ref, dx_ref, dw_ref, dc_scr):
        cb = pl.program_id(0)
        w = w_ref[...]
        is_qk = cb < 2 * DN_HEADS
        post = jnp.where(cb < DN_HEADS, scale, 1.0)

        def step1(r, dws):
            c, taps = _conv_tile(x_ref, w, r, rows)
            sg = _sigmoid(c)
            s = c * sg
            rinv = lax.rsqrt(jnp.sum(s * s, axis=-1, keepdims=True) + L2_EPS)
            dy = dy_ref[pl.ds(pl.multiple_of(r * rows, SUBLANES), rows), :]
            yn = s * rinv
            dyn = dy * post
            ds_qk = rinv * (dyn - yn * jnp.sum(dyn * yn, axis=-1, keepdims=True))
            ds = jnp.where(is_qk, ds_qk, dy)
            dc = ds * (sg * (1.0 + c * (1.0 - sg)))
            dc_scr[pl.ds(pl.multiple_of(r * rows, SUBLANES), rows), :] = dc
            return tuple(dws[j] + jnp.sum(dc * taps[j], axis=0, keepdims=True) for j in range(DN_CONV))

        zero = jnp.zeros((1, LANES), F32)
        dws = lax.fori_loop(0, n_inner, step1, (zero,) * DN_CONV)
        for j in range(DN_CONV):
            dw_ref[j:j + 1, :] = dws[j]

        def step2(r, carry):
            start = pl.multiple_of(r * rows, SUBLANES)
            cur = dc_scr[pl.ds(start, rows), :]
            nstart = pl.multiple_of(jnp.minimum(start + rows, T - SUBLANES), SUBLANES)
            next8 = jnp.where(r == n_inner - 1, 0.0, dc_scr[pl.ds(nstart, SUBLANES), :])
            dx = cur * w[DN_CONV - 1:DN_CONV, :]
            for j in range(DN_CONV - 1):
                dx = dx + _shift_up(cur, next8, DN_CONV - 1 - j) * w[j:j + 1, :]
            dx_ref[pl.ds(start, rows), :] = _mx(dx)
            return carry

        lax.fori_loop(0, n_inner, step2, 0)

    col = pl.BlockSpec((T, LANES), lambda j: (0, j))
    wspec = pl.BlockSpec((DN_CONV, LANES), lambda j: (0, j))
    return pl.pallas_call(body, name=name, grid=(3 * W // LANES,), in_specs=[col, wspec, col], out_specs=[col, wspec],
                          out_shape=[jax.ShapeDtypeStruct((T, 3 * W), MXU_DTYPE), jax.ShapeDtypeStruct((DN_CONV, 3 * W), F32)],
                          scratch_shapes=[pltpu.VMEM((T, LANES), F32)], compiler_params=_cp("parallel"))(proj, conv_w, dqkv)


def _softplus(x):
    return jnp.maximum(x, 0.0) + jnp.log(1.0 + jnp.exp(-jnp.abs(x)))


def _dn_gate_fwd(ba, a_log, dt_bias, name):
    T = ba.shape[0]
    tm = _pick(T, 1024, SUBLANES)

    def body(ba_ref, al_ref, dt_ref, beta_ref, g_ref):
        beta_ref[...] = _sigmoid(ba_ref[:, :LANES])
        g_ref[...] = -jnp.exp(al_ref[...]) * _softplus(ba_ref[:, LANES:] + dt_ref[...])

    row = lambda w: pl.BlockSpec((tm, w), lambda i: (i, 0))
    return pl.pallas_call(body, name=name, grid=(T // tm,), in_specs=[row(2 * LANES), _full((1, LANES)), _full((1, LANES))],
                          out_specs=[row(LANES), row(LANES)],
                          out_shape=[jax.ShapeDtypeStruct((T, LANES), F32)] * 2, compiler_params=_cp("parallel"))(ba, a_log, dt_bias)


def _dn_gate_bwd(ba, a_log, dt_bias, dbeta, dg, name):
    T = ba.shape[0]
    tm = _pick(T, 1024, SUBLANES)

    def body(ba_ref, al_ref, dt_ref, dbeta_ref, dg_ref, dba_ref, dal_ref, ddt_ref):
        @pl.when(pl.program_id(0) == 0)
        def _():
            dal_ref[...] = jnp.zeros_like(dal_ref)
            ddt_ref[...] = jnp.zeros_like(ddt_ref)

        beta = _sigmoid(ba_ref[:, :LANES])
        dba_ref[:, :LANES] = _mx(dbeta_ref[...] * beta * (1.0 - beta))
        pre = ba_ref[:, LANES:] + dt_ref[...]
        ea = jnp.exp(al_ref[...])
        dgv = dg_ref[...]
        da = dgv * (-ea) * _sigmoid(pre)
        dba_ref[:, LANES:] = _mx(da)
        ddt_ref[...] += jnp.sum(da, axis=0, keepdims=True)
        dal_ref[...] += jnp.sum(dgv * (-ea) * _softplus(pre), axis=0, keepdims=True)

    row = lambda w: pl.BlockSpec((tm, w), lambda i: (i, 0))
    one = _full((1, LANES))
    return pl.pallas_call(body, name=name, grid=(T // tm,), in_specs=[row(2 * LANES), one, one, row(LANES), row(LANES)],
                          out_specs=[row(2 * LANES), one, one],
                          out_shape=[jax.ShapeDtypeStruct((T, 2 * LANES), MXU_DTYPE), jax.ShapeDtypeStruct((1, LANES), F32),
                                     jax.ShapeDtypeStruct((1, LANES), F32)],
                          compiler_params=_cp("arbitrary"))(ba, a_log, dt_bias, dbeta, dg)


def _tri(c, strict):
    i = lax.broadcasted_iota(jnp.int32, (c, c), 0)
    j = lax.broadcasted_iota(jnp.int32, (c, c), 1)
    return (i > j) if strict else (i >= j)


def _inv_unit_lower(L):
    c = L.shape[0]
    i = lax.broadcasted_iota(jnp.int32, (c, c), 0)
    j = lax.broadcasted_iota(jnp.int32, (c, c), 1)
    x = jnp.where(i == j, 1.0, 0.0) - L
    p = L
    for _ in range(int(math.log2(c)) - 1):
        p = _dot(p, p, NN, HI)
        x = x + _dot(x, p, NN, HI)
    return x


def _chunk_gates(g_blk):
    c = g_blk.shape[0]
    gcs = _dot(jnp.where(_tri(c, False), 1.0, 0.0), g_blk, NN, HI)
    return gcs, gcs.T


def _head_chunk(h, qh, kh, vh, beta_blk, gcs, gcs_t):
    c = qh.shape[0]
    incl = _tri(c, False)
    gc_col = gcs[:, h:h + 1]
    gc_row = gcs_t[h:h + 1, :]
    gc_last = gcs_t[h:h + 1, c - 1:c]
    dec = jnp.where(incl, jnp.exp(jnp.where(incl, gc_col - gc_row, 0.0)), 0.0)
    gam = jnp.exp(gc_col)
    rr = jnp.exp(gc_last - gc_col)
    gl = jnp.exp(gc_last)
    b = beta_blk[:, h:h + 1]
    kb = kh * b
    vb = vh * b
    kk = _dot(_mx(kb), _mx(kh), NT)
    lmat = jnp.where(_tri(c, True), kk * dec, 0.0)
    qk = _dot(_mx(qh), _mx(kh), NT)
    pmat = jnp.where(incl, qk * dec, 0.0)
    return dict(dec=dec, gam=gam, rr=rr, gl=gl, b=b, kb=kb, vb=vb, lmat=lmat, pmat=pmat)


def _dn_scan_fwd(qkv, beta, g, proj, norm_g, name):
    T = qkv.shape[0]
    C, H, Dh = DN_CHUNK, DN_HEADS, DN_HEAD_DIM
    W = H * Dh
    N = T // C

    def body(q_ref, k_ref, v_ref, beta_ref, g_ref, z_ref, ng_ref, og_ref, o_ref, tinv_ref, s_ref, state):
        @pl.when(pl.program_id(0) == 0)
        def _():
            state[...] = jnp.zeros_like(state)

        gcs, gcs_t = _chunk_gates(g_ref[...])
        beta_blk = beta_ref[...]
        ng = ng_ref[...]
        for h in range(H):
            cs = slice(h * Dh, (h + 1) * Dh)
            qh, kh, vh = q_ref[:, cs], k_ref[:, cs], v_ref[:, cs]
            q = _head_chunk(h, qh, kh, vh, beta_blk, gcs, gcs_t)
            tinv = _inv_unit_lower(q["lmat"])
            tinv_ref[h] = tinv
            u = _dot(tinv, q["vb"], NN, HI)
            w = _dot(tinv, q["kb"] * q["gam"], NN, HI)
            s = state[h]
            s_ref[0, h] = s
            sb = _mx(s)
            vnew = u - _dot(_mx(w), sb, NN)
            vnb = _mx(vnew)
            o = _dot(_mx(qh * q["gam"]), sb, NN) + _dot(_mx(q["pmat"]), vnb, NN)
            state[h] = s * q["gl"] + _dot(_mx((kh * q["rr"]).T), vnb, NN)
            o_ref[:, cs] = o
            zh = z_ref[:, cs]
            og_ref[:, cs] = _mx(o * _rms_stat(o) * ng * (zh * _sigmoid(zh)))

    blk = lambda j: pl.BlockSpec((C, W), lambda n: (n, j))
    small = pl.BlockSpec((C, LANES), lambda n: (n, 0))
    return pl.pallas_call(
        body, name=name, grid=(N,),
        in_specs=[blk(0), blk(1), blk(2), small, small, blk(3), _full((1, Dh))],
        out_specs=[blk(0), blk(0), pl.BlockSpec((H, C, C), lambda n: (0, n, 0)),
                   pl.BlockSpec((1, H, Dh, Dh), lambda n: (n, 0, 0, 0))],
        out_shape=[jax.ShapeDtypeStruct((T, W), MXU_DTYPE), jax.ShapeDtypeStruct((T, W), F32),
                   jax.ShapeDtypeStruct((H, T, C), F32), jax.ShapeDtypeStruct((N, H, Dh, Dh), F32)],
        scratch_shapes=[pltpu.VMEM((H, Dh, Dh), F32)],
        compiler_params=_cp("arbitrary"),
    )(qkv, qkv, qkv, beta, g, proj, norm_g)


def _dn_scan_bwd(qkv, beta, g, proj, norm_g, o, tinv, s_all, dog, name):
    T = qkv.shape[0]
    C, H, Dh = DN_CHUNK, DN_HEADS, DN_HEAD_DIM
    W = H * Dh
    N = T // C

    def body(q_ref, k_ref, v_ref, beta_ref, g_ref, z_ref, ng_ref, o_ref, tinv_ref, s_ref, dog_ref,
             dqkv_ref, dbeta_ref, dg_ref, dz_ref, dng_ref, dstate):
        @pl.when(pl.program_id(0) == 0)
        def _():
            dstate[...] = jnp.zeros_like(dstate)
            dng_ref[...] = jnp.zeros_like(dng_ref)

        gcs, gcs_t = _chunk_gates(g_ref[...])
        beta_blk = beta_ref[...]
        ng = ng_ref[...]
        incl = _tri(C, False)
        strict = _tri(C, True)
        lane = lax.broadcasted_iota(jnp.int32, (C, LANES), 1)
        rowi = lax.broadcasted_iota(jnp.int32, (C, 1), 0)
        ones = jnp.ones((C, LANES), F32)
        dbeta_acc = jnp.zeros((C, LANES), F32)
        dgc_acc = jnp.zeros((C, LANES), F32)
        dng_acc = jnp.zeros((1, Dh), F32)
        for h in range(H):
            cs = slice(h * Dh, (h + 1) * Dh)
            qh, kh, vh = q_ref[:, cs], k_ref[:, cs], v_ref[:, cs]
            oh = o_ref[:, cs]
            zh = z_ref[:, cs]
            dogh = dog_ref[:, cs]
            rstat = _rms_stat(oh)
            sz = _sigmoid(zh)
            dz_ref[:, cs] = _mx(dogh * (oh * rstat * ng) * (sz * (1.0 + zh * (1.0 - sz))))
            do, dng = _rms_bwd(oh, rstat, ng, dogh * (zh * sz))
            dng_acc = dng_acc + dng
            q = _head_chunk(h, qh, kh, vh, beta_blk, gcs, gcs_t)
            gam, rr, gl, b, kb, vb = q["gam"], q["rr"], q["gl"], q["b"], q["kb"], q["vb"]
            tm = tinv_ref[h]
            u = _dot(tm, vb, NN, HI)
            w = _dot(tm, kb * gam, NN, HI)
            s = s_ref[0, h]
            sb = _mx(s)
            wb = _mx(w)
            vnew = u - _dot(wb, sb, NN)
            vnb = _mx(vnew)
            dsn = dstate[h]
            dsb = _mx(dsn)
            dob = _mx(do)
            pb = _mx(q["pmat"])
            kd = kh * rr
            qd = qh * gam
            dvnew = _dot(pb, dob, TN) + _dot(_mx(kd), dsb, NN)
            dvb16 = _mx(dvnew)
            dp = jnp.where(incl, _dot(dob, vnb, NT), 0.0)
            dqd = _dot(dob, sb, NT)
            dkd = _dot(vnb, dsb, NT)
            dgl = jnp.sum(jnp.sum(s * dsn, axis=1, keepdims=True), axis=0, keepdims=True)
            dw = -_dot(dvb16, sb, NT)
            dstate[h] = _dot(_mx(qd), dob, TN) + gl * dsn - _dot(wb, dvb16, TN)
            dvb = _dot(tm, dvnew, TN, HI)
            dkbg = _dot(tm, dw, TN, HI)
            dl = jnp.where(strict, -(_dot(dvb, u, NT, HI) + _dot(dkbg, w, NT, HI)), 0.0)
            dkk = dl * q["dec"]
            dqk = dp * q["dec"]
            mmat = dl * q["lmat"] + dp * q["pmat"]
            dgc = jnp.sum(mmat, axis=1, keepdims=True) - _dot(mmat, ones, TN, HI)[:, :1]
            dkk16 = _mx(dkk)
            dqk16 = _mx(dqk)
            dkb = _dot(dkk16, _mx(kh), NN) + dkbg * gam
            dk = _dot(dkk16, _mx(kb), TN) + _dot(dqk16, _mx(qh), TN) + dkb * b + dkd * rr
            dq = _dot(dqk16, _mx(kh), NN) + dqd * gam
            dgam = jnp.sum(dkbg * kb, axis=1, keepdims=True) + jnp.sum(dqd * qh, axis=1, keepdims=True)
            dr = jnp.sum(dkd * kh, axis=1, keepdims=True)
            dgc = dgc + dgam * gam - dr * rr
            dgc_last = jnp.sum(dr * rr, axis=0, keepdims=True) + dgl * gl
            dgc = dgc + jnp.where(rowi == C - 1, dgc_last, 0.0)
            dbeta = jnp.sum(dvb * vh, axis=1, keepdims=True) + jnp.sum(dkb * kh, axis=1, keepdims=True)
            dqkv_ref[:, cs] = dq
            dqkv_ref[:, W + h * Dh:W + (h + 1) * Dh] = dk
            dqkv_ref[:, 2 * W + h * Dh:2 * W + (h + 1) * Dh] = dvb * b
            dbeta_acc = jnp.where(lane == h, dbeta, dbeta_acc)
            dgc_acc = jnp.where(lane == h, dgc, dgc_acc)
        dbeta_ref[...] = dbeta_acc
        dg_ref[...] = _dot(jnp.where(incl, 1.0, 0.0), dgc_acc, TN, HI)
        dng_ref[...] += dng_acc

    rev = lambda n: N - 1 - n
    blk = lambda j: pl.BlockSpec((C, W), lambda n: (rev(n), j))
    small = pl.BlockSpec((C, LANES), lambda n: (rev(n), 0))
    return pl.pallas_call(
        body, name=name, grid=(N,),
        in_specs=[blk(0), blk(1), blk(2), small, small, blk(3), _full((1, Dh)), blk(0),
                  pl.BlockSpec((H, C, C), lambda n: (0, rev(n), 0)),
                  pl.BlockSpec((1, H, Dh, Dh), lambda n: (rev(n), 0, 0, 0)), blk(0)],
        out_specs=[pl.BlockSpec((C, 3 * W), lambda n: (rev(n), 0)), small, small, blk(0), _full((1, Dh))],
        out_shape=[jax.ShapeDtypeStruct((T, 3 * W), F32), jax.ShapeDtypeStruct((T, LANES), F32),
                   jax.ShapeDtypeStruct((T, LANES), F32), jax.ShapeDtypeStruct((T, W), MXU_DTYPE),
                   jax.ShapeDtypeStruct((1, Dh), F32)],
        scratch_shapes=[pltpu.VMEM((H, Dh, Dh), F32)],
        compiler_params=_cp("arbitrary"),
    )(qkv, qkv, qkv, beta, g, proj, norm_g, o, tinv, s_all, dog)


_INV_SQRT2 = 0.7071067811865476
_INV_SQRT_2PI = 0.3989422804014327


def _sg_recompute(zp_ref, bin_ref, lng_ref, lnb_ref):
    E = SG_WIDTH
    zin = zp_ref[...] + bin_ref[...]
    cdf = 0.5 * (1.0 + lax.erf(zin * _INV_SQRT2))
    zz = zin * cdf
    u = zz[:, :E]
    vp = zz[:, E:]
    mu = jnp.mean(vp, axis=-1, keepdims=True)
    xc = vp - mu
    rstd = lax.rsqrt(jnp.mean(xc * xc, axis=-1, keepdims=True) + LN_EPS)
    xhat = xc * rstd
    v = xhat * lng_ref[...] + lnb_ref[...]
    return zin, cdf, u, xhat, rstd, v


def _sg_masked_ws(ws_ref, g):
    return _mx(jnp.where(_tri(SG_CHUNK, False), ws_ref[g], 0.0))


def _sg_fwd(zpre, b_in, ln_g, ln_b, w_s, b_s_t, name):
    T = zpre.shape[0]
    E, G, C, GW = SG_WIDTH, SG_GROUPS, SG_CHUNK, SG_GROUP_W

    def body(zp_ref, bin_ref, lng_ref, lnb_ref, ws_ref, bst_ref, um_ref):
        _, _, u, _, _, v = _sg_recompute(zp_ref, bin_ref, lng_ref, lnb_ref)
        bst = bst_ref[...]
        for g in range(G):
            cs = slice(g * GW, (g + 1) * GW)
            mixed = _dot(_sg_masked_ws(ws_ref, g), _mx(v[:, cs]), NN) + bst[:, g:g + 1]
            um_ref[:, cs] = _mx(u[:, cs] * mixed)

    return pl.pallas_call(
        body, name=name, grid=(T // C,),
        in_specs=[pl.BlockSpec((C, 2 * E), lambda n: (n, 0)), _full((1, 2 * E)), _full((1, E)), _full((1, E)),
                  _full((G, C, C)), _full((C, LANES))],
        out_specs=pl.BlockSpec((C, E), lambda n: (n, 0)),
        out_shape=jax.ShapeDtypeStruct((T, E), MXU_DTYPE), compiler_params=_cp("parallel"),
    )(zpre, b_in, ln_g, ln_b, w_s, b_s_t)


def _sg_bwd(zpre, b_in, ln_g, ln_b, w_s, b_s_t, dum, name):
    T = zpre.shape[0]
    E, G, C, GW = SG_WIDTH, SG_GROUPS, SG_CHUNK, SG_GROUP_W

    def body(zp_ref, bin_ref, lng_ref, lnb_ref, ws_ref, bst_ref, dum_ref,
             dz_ref, dbin_ref, dlng_ref, dlnb_ref, dws_ref, dbst_ref):
        @pl.when(pl.program_id(0) == 0)
        def _():
            for r in (dbin_ref, dlng_ref, dlnb_ref, dws_ref, dbst_ref):
                r[...] = jnp.zeros_like(r)

        zin, cdf, u, xhat, rstd, v = _sg_recompute(zp_ref, bin_ref, lng_ref, lnb_ref)
        bst = bst_ref[...]
        lane = lax.broadcasted_iota(jnp.int32, (C, LANES), 1)
        dum_v = dum_ref[...]
        dbst = jnp.zeros((C, LANES), F32)
        du_parts, dv_parts = [], []
        for g in range(G):
            cs = slice(g * GW, (g + 1) * GW)
            wsm = _sg_masked_ws(ws_ref, g)
            vg = _mx(v[:, cs])
            mixed = _dot(wsm, vg, NN) + bst[:, g:g + 1]
            dumg = dum_v[:, cs]
            du_parts.append(dumg * mixed)
            dmixed = dumg * u[:, cs]
            dmb = _mx(dmixed)
            dv_parts.append(_dot(wsm, dmb, TN))
            dws_ref[g] += _dot(dmb, vg, NT)
            dbst = jnp.where(lane == g, jnp.sum(dmixed, axis=1, keepdims=True), dbst)
        dbst_ref[...] += dbst
        du = jnp.concatenate(du_parts, axis=1)
        dv = jnp.concatenate(dv_parts, axis=1)
        dlng_ref[...] += jnp.sum(dv * xhat, axis=0, keepdims=True)
        dlnb_ref[...] += jnp.sum(dv, axis=0, keepdims=True)
        dxh = dv * lng_ref[...]
        dvp = rstd * (dxh - jnp.mean(dxh, axis=-1, keepdims=True) - xhat * jnp.mean(dxh * xhat, axis=-1, keepdims=True))
        dzz = jnp.concatenate([du, dvp], axis=1)
        dzin = dzz * (cdf + zin * (_INV_SQRT_2PI * jnp.exp(-0.5 * zin * zin)))
        dz_ref[...] = _mx(dzin)
        dbin_ref[...] += jnp.sum(dzin, axis=0, keepdims=True)

    return pl.pallas_call(
        body, name=name, grid=(T // C,),
        in_specs=[pl.BlockSpec((C, 2 * E), lambda n: (n, 0)), _full((1, 2 * E)), _full((1, E)), _full((1, E)),
                  _full((G, C, C)), _full((C, LANES)), pl.BlockSpec((C, E), lambda n: (n, 0))],
        out_specs=[pl.BlockSpec((C, 2 * E), lambda n: (n, 0)), _full((1, 2 * E)), _full((1, E)), _full((1, E)),
                   _full((G, C, C)), _full((C, LANES))],
        out_shape=[jax.ShapeDtypeStruct((T, 2 * E), MXU_DTYPE), jax.ShapeDtypeStruct((1, 2 * E), F32),
                   jax.ShapeDtypeStruct((1, E), F32), jax.ShapeDtypeStruct((1, E), F32),
                   jax.ShapeDtypeStruct((G, C, C), F32), jax.ShapeDtypeStruct((C, LANES), F32)],
        compiler_params=_cp("arbitrary"),
    )(zpre, b_in, ln_g, ln_b, w_s, b_s_t, dum)


def _row(v):
    return v.reshape(1, -1)


def _pad_lanes(v):
    v = v.reshape(1, -1)
    return jnp.pad(v, ((0, 0), (0, LANES - v.shape[1])))


def _local_step(x, target, p):
    ng = p["norm_g"]
    grads = {}
    dng = [[None] * 6 for _ in range(2)]
    saved = []

    def ffn_f(xin, i, j, tag):
        xo, h, gu, y = _ffn_fwd(xin, _row(ng[i, 4 * j]), _row(ng[i, 4 * j + 1]), p["wgu"][i][j], p["wd"][i][j], "ffn_fwd_" + tag)
        return xo, (xin, h, gu, y)

    x1, sv_f00 = ffn_f(x, 0, 0, "00")
    hn0 = _norm_fwd(x1, _row(ng[0, 2]), "dn_prenorm")
    proj = _mm(hn0, p["dn_wqkvz"], "nn", "dn_proj")
    ba = _mm(hn0, p["dn_wba"], "nn", "dn_proj_ba")
    a_log = _pad_lanes(p["dn_a_log"])
    dt_bias = _pad_lanes(p["dn_dt_bias"])
    dn_ng = _row(p["dn_norm_g"])
    qkv = _dn_prep_fwd(proj, p["dn_conv_w"], "dn_prep_fwd")
    beta, gdec = _dn_gate_fwd(ba, a_log, dt_bias, "dn_gate_fwd")
    og, o_raw, tinv, s_all = _dn_scan_fwd(qkv, beta, gdec, proj, dn_ng, "dn_scan_fwd")
    m0 = _mm(og, p["dn_wout"], "nn", "dn_out")
    x2 = _postnorm_fwd(x1, m0, _row(ng[0, 3]), "dn_postnorm")
    x3, sv_f01 = ffn_f(x2, 0, 1, "01")
    x4, sv_f10 = ffn_f(x3, 1, 0, "10")
    hn1 = _norm_fwd(x4, _row(ng[1, 2]), "sg_prenorm")
    zpre = _mm(hn1, p["sg_win"], "nn", "sg_proj")
    sg_bin = _row(p["sg_b_in"])
    sg_lng = _row(p["sg_ln_g"])
    sg_lnb = _row(p["sg_ln_b"])
    sg_bst = jnp.pad(p["sg_b_s"].T, ((0, 0), (0, LANES - SG_GROUPS)))
    um = _sg_fwd(zpre, sg_bin, sg_lng, sg_lnb, p["sg_w_s"], sg_bst, "sg_fwd")
    m1 = _mm(um, p["sg_wout"], "nn", "sg_out")
    x5 = _postnorm_fwd(x4, m1, _row(ng[1, 3]), "sg_postnorm")
    x6, sv_f11 = ffn_f(x5, 1, 1, "11")
    loss_part, dx = _loss_fwd_bwd(x6, target, "loss")

    def ffn_b(dxo, sv, i, j, tag):
        xin, h, gu, y = sv
        dy, a, dgu, dg1 = _ffn_bwd_down(dxo, y, gu, _row(ng[i, 4 * j + 1]), p["wd"][i][j], "ffn_bwd_down_" + tag)
        grads["wd%d%d" % (i, j)] = _mm(a, dy, "tn", "ffn_wgrad_down_" + tag)
        grads["wgu%d%d" % (i, j)] = _mm(h, dgu, "tn", "ffn_wgrad_up_" + tag)
        dxi, dg0 = _ffn_bwd_up(dgu, xin, dxo, _row(ng[i, 4 * j]), p["wgu"][i][j], "ffn_bwd_up_" + tag)
        dng[i][4 * j] = dg0
        dng[i][4 * j + 1] = dg1
        return dxi

    dx = ffn_b(dx, sv_f11, 1, 1, "11")
    dm1, dng[1][3] = _postnorm_bwd(dx, m1, _row(ng[1, 3]), "sg_postnorm_bwd")
    grads["sg_w_out"] = _mm(um, dm1, "tn", "sg_wgrad_out")
    dum = _mm(dm1, p["sg_wout"], "nt", "sg_dgrad_out")
    dz1, dbin, dlng, dlnb, dws, dbst = _sg_bwd(zpre, sg_bin, sg_lng, sg_lnb, p["sg_w_s"], sg_bst, dum, "sg_bwd")
    grads["sg_w_in"] = _mm(hn1, dz1, "tn", "sg_wgrad_in")
    dh1 = _mm(dz1, p["sg_win"], "nt", "sg_dgrad_in")
    dx, dng[1][2] = _prenorm_bwd(dx, dh1, x4, _row(ng[1, 2]), "sg_prenorm_bwd")
    grads["sg_b_in"] = dbin.reshape(1, -1)
    grads["sg_ln_g"] = dlng.reshape(1, -1)
    grads["sg_ln_b"] = dlnb.reshape(1, -1)
    grads["sg_w_s"] = jnp.where(jnp.tril(jnp.ones((SG_CHUNK, SG_CHUNK), bool)), dws, 0.0)[None]
    grads["sg_b_s"] = dbst[:, :SG_GROUPS].T[None]
    dx = ffn_b(dx, sv_f10, 1, 0, "10")
    dx = ffn_b(dx, sv_f01, 0, 1, "01")
    dm0, dng[0][3] = _postnorm_bwd(dx, m0, _row(ng[0, 3]), "dn_postnorm_bwd")
    grads["dn_w_out"] = _mm(og, dm0, "tn", "dn_wgrad_out")
    dog = _mm(dm0, p["dn_wout"], "nt", "dn_dgrad_out")
    dqkv, dbeta, dgdec, dz0, dnng = _dn_scan_bwd(qkv, beta, gdec, proj, dn_ng, o_raw, tinv, s_all, dog, "dn_scan_bwd")
    dqkv_pre, dconv = _dn_prep_bwd(proj, p["dn_conv_w"], dqkv, "dn_prep_bwd")
    dba, dal, ddt = _dn_gate_bwd(ba, a_log, dt_bias, dbeta, dgdec, "dn_gate_bwd")
    W3 = 3 * DN_HEADS * DN_HEAD_DIM
    dw_qkv = _mm(hn0, dqkv_pre, "tn", "dn_wgrad_qkv")
    dw_z = _mm(hn0, dz0, "tn", "dn_wgrad_z")
    dw_ba = _mm(hn0, dba, "tn", "dn_wgrad_ba")
    grads["dn_w_in"] = jnp.concatenate(
        [dw_qkv, dw_z, dw_ba[:, :DN_HEADS], dw_ba[:, LANES:LANES + DN_HEADS]], axis=1)
    dh0 = _mm(dqkv_pre, p["dn_wqkvz"][:, :W3], "nt", "dn_dgrad_qkv")
    dh0 = _mm(dz0, p["dn_wqkvz"][:, W3:], "nt", "dn_dgrad_z", add=dh0)
    dh0 = _mm(dba, p["dn_wba"], "nt", "dn_dgrad_ba", add=dh0)
    dx, dng[0][2] = _prenorm_bwd(dx, dh0, x1, _row(ng[0, 2]), "dn_prenorm_bwd")
    grads["dn_conv_w"] = dconv[None]
    grads["dn_a_log"] = dal[:, :DN_HEADS]
    grads["dn_dt_bias"] = ddt[:, :DN_HEADS]
    grads["dn_norm_g"] = dnng
    dx = ffn_b(dx, sv_f00, 0, 0, "00")
    grads["norm_g"] = jnp.stack([jnp.concatenate(dng[i], axis=0) for i in range(2)])
    return loss_part, dx, grads


def _mesh_pos():
    return lax.axis_index("x"), lax.axis_index("y"), lax.axis_index("c")


def _other_chips(x, y):
    return [(1 - x, y), (x, 1 - y), (1 - x, 1 - y)]


def _allgather_chips(arrs, name):
    n = len(arrs)

    def body(*refs):
        ins, outs = refs[:n], refs[n:2 * n]
        send_sems, recv_sems, loc_sems = refs[2 * n:]
        x, y, c = _mesh_pos()
        me = 2 * x + y
        chips = _other_chips(x, y)
        started = []
        for i in range(n):
            lc = pltpu.make_async_copy(ins[i], outs[i].at[me], loc_sems.at[i])
            lc.start()
            started.append(lc)
            for j, (cx, cy) in enumerate(chips):
                cp = pltpu.make_async_remote_copy(src_ref=ins[i], dst_ref=outs[i].at[me], send_sem=send_sems.at[3 * i + j],
                                                  recv_sem=recv_sems.at[3 * i + j], device_id=(cx, cy, c), device_id_type=MESH)
                cp.start()
                started.append(cp)
        for i in range(n):
            for j, (cx, cy) in enumerate(chips):
                pltpu.make_async_remote_copy(src_ref=ins[i], dst_ref=outs[i].at[2 * cx + cy], send_sem=send_sems.at[3 * i + j],
                                             recv_sem=recv_sems.at[3 * i + j], device_id=(cx, cy, c),
                                             device_id_type=MESH).wait_recv()
        for i in range(n):
            started[4 * i].wait()
            for j in range(3):
                started[4 * i + 1 + j].wait_send()

    return pl.pallas_call(
        body, name=name, in_specs=[ANY] * n, out_specs=[ANY] * n,
        out_shape=[jax.ShapeDtypeStruct((N_CHIPS,) + a.shape, a.dtype) for a in arrs],
        scratch_shapes=[pltpu.SemaphoreType.DMA((3 * n,)), pltpu.SemaphoreType.DMA((3 * n,)), pltpu.SemaphoreType.DMA((n,))],
    )(*arrs)


def _swap_halves(arrs, name):
    n = len(arrs)

    def body(*refs):
        ins, outs = refs[:n], refs[n:2 * n]
        send_sems, recv_sems = refs[2 * n:]
        x, y, c = _mesh_pos()
        cps = [pltpu.make_async_remote_copy(src_ref=ins[i].at[1 - c], dst_ref=outs[i], send_sem=send_sems.at[i],
                                            recv_sem=recv_sems.at[i], device_id=(x, y, 1 - c), device_id_type=MESH)
               for i in range(n)]
        for cp in cps:
            cp.start()
        for cp in cps:
            cp.wait()

    return pl.pallas_call(
        body, name=name, in_specs=[ANY] * n, out_specs=[ANY] * n,
        out_shape=[jax.ShapeDtypeStruct(a.shape[1:], a.dtype) for a in arrs],
        scratch_shapes=[pltpu.SemaphoreType.DMA((n,)), pltpu.SemaphoreType.DMA((n,))],
    )(*arrs)


def _scatter_chips(arrs, name):
    n = len(arrs)

    def body(*refs):
        ins, outs = refs[:n], refs[n:2 * n]
        send_sems, recv_sems, loc_sems = refs[2 * n:]
        x, y, c = _mesh_pos()
        me = 2 * x + y
        chips = _other_chips(x, y)
        started = []
        for i in range(n):
            lc = pltpu.make_async_copy(ins[i].at[me], outs[i].at[me], loc_sems.at[i])
            lc.start()
            started.append(lc)
            for j, (cx, cy) in enumerate(chips):
                cp = pltpu.make_async_remote_copy(src_ref=ins[i].at[2 * cx + cy], dst_ref=outs[i].at[me],
                                                  send_sem=send_sems.at[3 * i + j], recv_sem=recv_sems.at[3 * i + j],
                                                  device_id=(cx, cy, c), device_id_type=MESH)
                cp.start()
                started.append(cp)
        for i in range(n):
            for j, (cx, cy) in enumerate(chips):
                pltpu.make_async_remote_copy(src_ref=ins[i].at[me], dst_ref=outs[i].at[2 * cx + cy],
                                             send_sem=send_sems.at[3 * i + j], recv_sem=recv_sems.at[3 * i + j],
                                             device_id=(cx, cy, c), device_id_type=MESH).wait_recv()
        for i in range(n):
            started[4 * i].wait()
            for j in range(3):
                started[4 * i + 1 + j].wait_send()

    return pl.pallas_call(
        body, name=name, in_specs=[ANY] * n, out_specs=[ANY] * n,
        out_shape=[jax.ShapeDtypeStruct(a.shape, a.dtype) for a in arrs],
        scratch_shapes=[pltpu.SemaphoreType.DMA((3 * n,)), pltpu.SemaphoreType.DMA((3 * n,)), pltpu.SemaphoreType.DMA((n,))],
    )(*arrs)


def _join_halves(arrs, name):
    n = len(arrs)

    def body(*refs):
        ins, outs = refs[:n], refs[n:2 * n]
        send_sems, recv_sems, loc_sems = refs[2 * n:]
        x, y, c = _mesh_pos()
        lcs = [pltpu.make_async_copy(ins[i], outs[i].at[c], loc_sems.at[i]) for i in range(n)]
        cps = [pltpu.make_async_remote_copy(src_ref=ins[i], dst_ref=outs[i].at[c], send_sem=send_sems.at[i],
                                            recv_sem=recv_sems.at[i], device_id=(x, y, 1 - c), device_id_type=MESH)
               for i in range(n)]
        for cp in lcs + cps:
            cp.start()
        for i in range(n):
            pltpu.make_async_remote_copy(src_ref=ins[i], dst_ref=outs[i].at[1 - c], send_sem=send_sems.at[i],
                                         recv_sem=recv_sems.at[i], device_id=(x, y, 1 - c), device_id_type=MESH).wait_recv()
        for i in range(n):
            cps[i].wait_send()
            lcs[i].wait()

    return pl.pallas_call(
        body, name=name, in_specs=[ANY] * n, out_specs=[ANY] * n,
        out_shape=[jax.ShapeDtypeStruct((2,) + a.shape, a.dtype) for a in arrs],
        scratch_shapes=[pltpu.SemaphoreType.DMA((n,)), pltpu.SemaphoreType.DMA((n,)), pltpu.SemaphoreType.DMA((n,))],
    )(*arrs)


def _allgather_devices(a, name):
    masks = [(mx, my, mc) for mx in (0, 1) for my in (0, 1) for mc in (0, 1)][1:]

    def body(in_ref, out_ref, send_sems, recv_sems, loc_sem):
        x, y, c = _mesh_pos()
        me = 4 * x + 2 * y + c
        lc = pltpu.make_async_copy(in_ref, out_ref.at[me], loc_sem.at[0])
        lc.start()
        peers = [(jnp.where(mx, 1 - x, x), jnp.where(my, 1 - y, y), jnp.where(mc, 1 - c, c)) for mx, my, mc in masks]
        cps = [pltpu.make_async_remote_copy(src_ref=in_ref, dst_ref=out_ref.at[me], send_sem=send_sems.at[j],
                                            recv_sem=recv_sems.at[j], device_id=peers[j], device_id_type=MESH)
               for j in range(len(masks))]
        for cp in cps:
            cp.start()
        for j, (px, py, pc) in enumerate(peers):
            pltpu.make_async_remote_copy(src_ref=in_ref, dst_ref=out_ref.at[4 * px + 2 * py + pc], send_sem=send_sems.at[j],
                                         recv_sem=recv_sems.at[j], device_id=peers[j], device_id_type=MESH).wait_recv()
        for cp in cps:
            cp.wait_send()
        lc.wait()

    return pl.pallas_call(
        body, name=name, in_specs=[ANY], out_specs=ANY,
        out_shape=jax.ShapeDtypeStruct((N_DEV,) + a.shape, a.dtype),
        scratch_shapes=[pltpu.SemaphoreType.DMA((N_DEV - 1,)), pltpu.SemaphoreType.DMA((N_DEV - 1,)),
                        pltpu.SemaphoreType.DMA((1,))],
    )(a)


def _as_rows(a, lead):
    shp = a.shape
    rows = 1
    for s in shp[lead:-1]:
        rows *= s
    return a.reshape(shp[:lead] + (rows, shp[-1]))


def _row_tile(rows, cols, n_bufs):
    budget = (24 * 1024 * 1024) // (n_bufs * 2 * 4 * cols)
    return _pick(rows, max(SUBLANES, budget), SUBLANES)


def _sum_leading(a, name):
    n = a.shape[0]
    v = _as_rows(a, 1)
    _, rows, cols = v.shape
    tr = _row_tile(rows, cols, n + 1)

    def body(a_ref, o_ref):
        acc = a_ref[0]
        for k in range(1, n):
            acc = acc + a_ref[k]
        o_ref[...] = acc

    out = pl.pallas_call(body, name=name, grid=(rows // tr,),
                         in_specs=[pl.BlockSpec((n, tr, cols), lambda i: (0, i, 0))],
                         out_specs=pl.BlockSpec((tr, cols), lambda i: (i, 0)),
                         out_shape=jax.ShapeDtypeStruct((rows, cols), F32), compiler_params=_cp("parallel"))(v)
    return out.reshape(a.shape[1:])


def _add_own_half(g, other, c, name):
    v = _as_rows(g, 1)
    ov = _as_rows(other, 0)
    rows, cols = ov.shape
    tr = _row_tile(rows, cols, 3)

    def body(c_ref, g_ref, o_ref, out_ref):
        out_ref[...] = g_ref[0] + o_ref[...]

    out = pl.pallas_call(
        body, name=name,
        grid_spec=pltpu.PrefetchScalarGridSpec(
            num_scalar_prefetch=1, grid=(rows // tr,),
            in_specs=[pl.BlockSpec((1, tr, cols), lambda i, c_ref: (c_ref[0], i, 0)),
                      pl.BlockSpec((tr, cols), lambda i, c_ref: (i, 0))],
            out_specs=pl.BlockSpec((tr, cols), lambda i, c_ref: (i, 0))),
        out_shape=jax.ShapeDtypeStruct((rows, cols), F32), compiler_params=_cp("parallel"),
    )(jnp.reshape(c, (1,)).astype(jnp.int32), v, ov)
    return out.reshape(other.shape)


def _adamw(w, g, m, v, name):
    shape = w.shape
    ws, gs, ms, vs = (_as_rows(t, 0) for t in (w, g, m, v))
    rows, cols = ws.shape
    tr = _row_tile(rows, cols, 7)

    def body(w_ref, g_ref, m_ref, v_ref, d_ref, nm_ref, nv_ref):
        gv = g_ref[...]
        nm = ADAM_B1 * m_ref[...] + (1.0 - ADAM_B1) * gv
        nv = ADAM_B2 * v_ref[...] + (1.0 - ADAM_B2) * (gv * gv)
        m_hat = nm / (1.0 - ADAM_B1 ** ADAM_STEP)
        v_hat = nv / (1.0 - ADAM_B2 ** ADAM_STEP)
        d_ref[...] = -ADAM_LR * (m_hat / (jnp.sqrt(v_hat) + ADAM_EPS) + ADAM_WD * w_ref[...])
        nm_ref[...] = nm
        nv_ref[...] = nv

    spec = pl.BlockSpec((tr, cols), lambda i: (i, 0))
    outs = pl.pallas_call(body, name=name, grid=(rows // tr,), in_specs=[spec] * 4, out_specs=[spec] * 3,
                          out_shape=[jax.ShapeDtypeStruct((rows, cols), F32)] * 3, compiler_params=_cp("parallel"))(ws, gs, ms, vs)
    return tuple(o.reshape(shape) for o in outs)


_BIG = ["ffn_w_gate", "ffn_w_up", "ffn_w_down", "dn_w_in", "dn_w_out", "sg_w_in", "sg_w_out"]
_SMALL_SHARDED = ["norm_g", "dn_conv_w", "sg_b_in", "sg_ln_g", "sg_ln_b"]
_SMALL_REPL = ["dn_a_log", "dn_dt_bias", "dn_norm_g", "sg_w_s", "sg_b_s"]
_WEIGHTS = ["norm_g", "ffn_w_gate", "ffn_w_up", "ffn_w_down", "dn_w_in", "dn_conv_w", "dn_a_log", "dn_dt_bias",
            "dn_norm_g", "dn_w_out", "sg_w_in", "sg_b_in", "sg_ln_g", "sg_ln_b", "sg_w_s", "sg_b_s", "sg_w_out"]
PACK_COLS = 1024


def _pack(arrs):
    flat = jnp.concatenate([a.reshape(-1) for a in arrs])
    pad = (-flat.shape[0]) % (SUBLANES * PACK_COLS)
    return jnp.pad(flat, (0, pad)).reshape(-1, PACK_COLS)


def _unpack(buf, shapes):
    flat = buf.reshape(-1)
    out, off = [], 0
    for s in shapes:
        n = math.prod(s)
        out.append(flat[off:off + n].reshape(s))
        off += n
    return out


def _cat_shards(g, axis):
    return jnp.concatenate([g[k] for k in range(N_CHIPS)], axis=axis)


def _full_weights(w, gathered, small_full):
    gate, up, down, dn_in, dn_out, sg_in, sg_out = gathered
    p = dict(small_full)
    p["wgu"] = [[jnp.concatenate([_cat_shards(gate[:, i, j], 1), _cat_shards(up[:, i, j], 1)], axis=1) for j in range(2)]
                for i in range(2)]
    p["wd"] = [[_cat_shards(down[:, i, j], 0) for j in range(2)] for i in range(2)]
    dn_full = _cat_shards(dn_in[:, 0], 1)
    W4 = 4 * DN_HEADS * DN_HEAD_DIM
    p["dn_wqkvz"] = dn_full[:, :W4]
    wba = jnp.zeros((D_MODEL, 2 * LANES), dn_full.dtype)
    wba = wba.at[:, :DN_HEADS].set(dn_full[:, W4:W4 + DN_HEADS])
    wba = wba.at[:, LANES:LANES + DN_HEADS].set(dn_full[:, W4 + DN_HEADS:])
    p["dn_wba"] = wba
    p["dn_wout"] = _cat_shards(dn_out[:, 0], 0)
    p["sg_win"] = _cat_shards(sg_in[:, 0], 1)
    p["sg_wout"] = _cat_shards(sg_out[:, 0], 0)
    for k in _SMALL_REPL:
        p[k] = w[k][0]
    return p


def _split_cols(a, n):
    w = a.shape[-1] // n
    return [a[..., k * w:(k + 1) * w] for k in range(n)]


def _split_rows(a, n):
    h = a.shape[-2] // n
    return [a[..., k * h:(k + 1) * h, :] for k in range(n)]


def _grads_by_half_and_chip(grads):
    F = D_FF
    gate = jnp.stack([jnp.stack([jnp.stack([_split_cols(grads["wgu%d%d" % (i, j)][:, :F], 4)[k] for j in range(2)])
                                 for k in range(4)]) for i in range(2)])
    up = jnp.stack([jnp.stack([jnp.stack([_split_cols(grads["wgu%d%d" % (i, j)][:, F:], 4)[k] for j in range(2)])
                               for k in range(4)]) for i in range(2)])
    down = jnp.stack([jnp.stack([jnp.stack([_split_rows(grads["wd%d%d" % (i, j)], 4)[k] for j in range(2)])
                                 for k in range(4)]) for i in range(2)])

    def col_sharded(a):
        return jnp.stack([jnp.stack(_split_cols(hf, 4)) for hf in _split_rows(a, 2)])

    def row_sharded(a):
        return jnp.stack([jnp.stack([_split_rows(s, 2)[hf] for s in _split_rows(a, 4)]) for hf in range(2)])

    return [gate, up, down, col_sharded(grads["dn_w_in"]), row_sharded(grads["dn_w_out"]),
            col_sharded(grads["sg_w_in"]), row_sharded(grads["sg_w_out"])]


def kernel(x, norm_g, ffn_w_gate, ffn_w_up, ffn_w_down, dn_w_in, dn_conv_w, dn_a_log, dn_dt_bias, dn_norm_g, dn_w_out, sg_w_in, sg_b_in, sg_ln_g, sg_ln_b, sg_w_s, sg_b_s, sg_w_out, loss_target, m_norm_g, m_ffn_w_gate, m_ffn_w_up, m_ffn_w_down, m_dn_w_in, m_dn_conv_w, m_dn_a_log, m_dn_dt_bias, m_dn_norm_g, m_dn_w_out, m_sg_w_in, m_sg_b_in, m_sg_ln_g, m_sg_ln_b, m_sg_w_s, m_sg_b_s, m_sg_w_out, v_norm_g, v_ffn_w_gate, v_ffn_w_up, v_ffn_w_down, v_dn_w_in, v_dn_conv_w, v_dn_a_log, v_dn_dt_bias, v_dn_norm_g, v_dn_w_out, v_sg_w_in, v_sg_b_in, v_sg_ln_g, v_sg_ln_b, v_sg_w_s, v_sg_b_s, v_sg_w_out):
    args = dict(locals())
    w = {k: args[k] for k in _WEIGHTS}
    mom = {k: args["m_" + k] for k in _WEIGHTS}
    var = {k: args["v_" + k] for k in _WEIGHTS}
    cx, cy, cc = _mesh_pos()
    chip = 2 * cx + cy

    small_shapes = [w[k].shape for k in _SMALL_SHARDED]
    gathered = _allgather_chips([_mx(w[k]) for k in _BIG] + [_pack([w[k] for k in _SMALL_SHARDED])], "gather_weights")
    small_k = [_unpack(gathered[-1][k], small_shapes) for k in range(N_CHIPS)]
    small_full = {name: jnp.concatenate([small_k[k][i] for k in range(N_CHIPS)], axis=-1)
                  for i, name in enumerate(_SMALL_SHARDED)}
    small_full = {k: (v if k == "norm_g" else v[0]) for k, v in small_full.items()}
    p = _full_weights(w, gathered[:-1], small_full)

    loss_part, grad_x, grads = _local_step(x[0], loss_target[0], p)

    halves = _grads_by_half_and_chip(grads)
    from_sibling = _swap_halves(halves, "reduce_core_pair")
    pair_sum = [_add_own_half(h, o, cc, "pair_sum_%d" % i) for i, (h, o) in enumerate(zip(halves, from_sibling))]
    from_chips = _scatter_chips(pair_sum, "reduce_chips")
    half_sum = [_sum_leading(b, "chip_sum_%d" % i) for i, b in enumerate(from_chips)]
    joined = _join_halves(half_sum, "gather_core_pair")
    big_grad = {k: joined[i].reshape(w[k].shape) for i, k in enumerate(_BIG)}

    small_names = _SMALL_SHARDED + _SMALL_REPL
    small_grads = [grads[k] for k in small_names]
    full_shapes = [g.shape for g in small_grads] + [(1,)]
    pack = _pack(small_grads + [loss_part[0, :1]])
    summed = _sum_leading(_allgather_devices(pack, "gather_small"), "small_sum")
    parts = _unpack(summed, full_shapes)
    loss = parts[-1][0]
    small_grad = {}
    for i, k in enumerate(small_names):
        g = parts[i]
        if k in _SMALL_SHARDED:
            n = w[k].shape[-1]
            g = lax.dynamic_slice_in_dim(g, chip * n, n, axis=g.ndim - 1)
        small_grad[k] = g

    grad = {**big_grad, **small_grad}
    delta, new_m, new_v = {}, {}, {}
    for k in _BIG:
        delta[k], new_m[k], new_v[k] = _adamw(w[k], grad[k], mom[k], var[k], "adamw_" + k)
    shapes = [w[k].shape for k in small_names]
    d, nm, nv = _adamw(_pack([w[k] for k in small_names]), _pack([grad[k] for k in small_names]),
                       _pack([mom[k] for k in small_names]), _pack([var[k] for k in small_names]), "adamw_small")
    for k, a, b, c_ in zip(small_names, _unpack(d, shapes), _unpack(nm, shapes), _unpack(nv, shapes)):
        delta[k], new_m[k], new_v[k] = a, b, c_

    return (loss, grad_x[None], *[grad[k] for k in _WEIGHTS], *[delta[k] for k in _WEIGHTS],
            *[new_m[k] for k in _WEIGHTS], *[new_v[k] for k in _WEIGHTS])
```

```python
import functools
import math

import jax
import jax.numpy as jnp
from jax import lax
from jax.experimental import pallas as pl
from jax.experimental.pallas import tpu as pltpu

F32 = jnp.float32
MXU_DTYPE = jnp.bfloat16
COMM_DTYPE = jnp.bfloat16
HI = lax.Precision.HIGHEST

D_MODEL = 1024
D_FF = 2816
RMS_EPS = 1e-6
LN_EPS = 1e-5
L2_EPS = 1e-6
DN_HEADS = 8
DN_HEAD_DIM = 128
DN_CONV = 4
DN_CHUNK = 64
SG_WIDTH = 2048
SG_GROUPS = 8
SG_CHUNK = 128
SG_GROUP_W = SG_WIDTH // SG_GROUPS
N_CHIPS = 4
N_DEV = 8
LANES = 128
SUBLANES = 8
VMEM_LIMIT = 56 * 1024 * 1024

ADAM_LR = 0.001
ADAM_B1 = 0.9
ADAM_B2 = 0.999
ADAM_EPS = 1e-08
ADAM_WD = 0.01
ADAM_STEP = 10

MESH = pl.DeviceIdType.MESH
ANY = pl.BlockSpec(memory_space=pl.ANY)


def _cp(*sem):
    return pltpu.CompilerParams(dimension_semantics=sem, vmem_limit_bytes=VMEM_LIMIT)


def _pick(n, pref, mult=LANES):
    best = None
    d = mult
    while d <= min(n, pref):
        if n % d == 0:
            best = d
        d += mult
    return best if best is not None else n


def _full(shape):
    nd = len(shape)
    return pl.BlockSpec(shape, lambda *_: (0,) * nd)


def _sigmoid(x):
    return 1.0 / (1.0 + jnp.exp(-x))


def _dot(a, b, dims, prec=None):
    return lax.dot_general(a, b, (dims, ((), ())), preferred_element_type=F32, precision=prec)


NN = ((1,), (0,))
NT = ((1,), (1,))
TN = ((0,), (0,))


def _mx(a):
    return a.astype(MXU_DTYPE)


def _rms_stat(x):
    return lax.rsqrt(jnp.mean(x * x, axis=-1, keepdims=True) + RMS_EPS)


def _rms_bwd(x, r, g, dy):
    xh = x * r
    dxh = dy * g
    dx = r * (dxh - xh * jnp.mean(dxh * xh, axis=-1, keepdims=True))
    return dx, jnp.sum(dy * xh, axis=0, keepdims=True)


def _mm(a, b, mode, name, out_dtype=F32, add=None):
    if mode == "tn":
        K, M = a.shape
        N = b.shape[1]
    elif mode == "nt":
        M, K = a.shape
        N = b.shape[0]
    else:
        M, K = a.shape
        N = b.shape[1]
    tn = _pick(N, 1024)
    if mode == "tn":
        tm = _pick(M, 1024 if tn <= 512 else 1408)
        tk = _pick(K, 1024, SUBLANES)
    else:
        tm = _pick(M, max(512, min(2048, (512 * 1024) // tn)), SUBLANES)
        tk = _pick(K, 2048)
    nk = K // tk
    grid = (N // tn, M // tm, nk)
    if mode == "nn":
        a_spec = pl.BlockSpec((tm, tk), lambda j, i, k: (i, k))
        b_spec = pl.BlockSpec((tk, tn), lambda j, i, k: (k, j))
        dims = NN
    elif mode == "nt":
        a_spec = pl.BlockSpec((tm, tk), lambda j, i, k: (i, k))
        b_spec = pl.BlockSpec((tn, tk), lambda j, i, k: (j, k))
        dims = NT
    else:
        a_spec = pl.BlockSpec((tk, tm), lambda j, i, k: (k, i))
        b_spec = pl.BlockSpec((tk, tn), lambda j, i, k: (k, j))
        dims = TN
    o_spec = pl.BlockSpec((tm, tn), lambda j, i, k: (i, j))
    has_add = add is not None

    def body(*refs):
        if has_add:
            a_ref, b_ref, add_ref, o_ref, acc = refs
        else:
            a_ref, b_ref, o_ref, acc = refs
        k = pl.program_id(2)

        @pl.when(k == 0)
        def _():
            acc[...] = add_ref[...] if has_add else jnp.zeros_like(acc)

        acc[...] += _dot(a_ref[...], b_ref[...], dims)

        @pl.when(k == nk - 1)
        def _():
            o_ref[...] = acc[...].astype(o_ref.dtype)

    ins = [a, b] + ([add] if has_add else [])
    specs = [a_spec, b_spec] + ([o_spec] if has_add else [])
    return pl.pallas_call(
        body, name=name, grid=grid, in_specs=specs, out_specs=o_spec,
        out_shape=jax.ShapeDtypeStruct((M, N), out_dtype),
        scratch_shapes=[pltpu.VMEM((tm, tn), F32)],
        compiler_params=_cp("parallel", "parallel", "arbitrary"),
    )(*ins)


def _load_resident(pairs, sem):
    @pl.when(pl.program_id(0) == 0)
    def _():
        cps = [pltpu.make_async_copy(src, dst, sem.at[i]) for i, (src, dst) in enumerate(pairs)]
        for c in cps:
            c.start()
        for c in cps:
            c.wait()


def _ffn_fwd(x, g0, g1, wgu, wd, name):
    T, D = x.shape
    F2 = wgu.shape[1]
    F = F2 // 2
    tm = _pick(T, 256, SUBLANES)

    def body(x_ref, g0_ref, g1_ref, wgu_hbm, wd_hbm, xo_ref, h_ref, gu_ref, y_ref, wgu_v, wd_v, sem):
        _load_resident([(wgu_hbm, wgu_v), (wd_hbm, wd_v)], sem)
        xv = x_ref[...]
        hb = _mx(xv * _rms_stat(xv) * g0_ref[...])
        h_ref[...] = hb
        gu = _dot(hb, wgu_v[...], NN)
        gu_ref[...] = gu
        g = gu[:, :F]
        u = gu[:, F:]
        a = _mx(g * _sigmoid(g) * u)
        y = _dot(a, wd_v[...], NN)
        y_ref[...] = y
        xo_ref[...] = xv + 0.5 * (y * _rms_stat(y) * g1_ref[...])

    row = lambda w: pl.BlockSpec((tm, w), lambda i: (i, 0))
    return pl.pallas_call(
        body, name=name, grid=(T // tm,),
        in_specs=[row(D), _full((1, D)), _full((1, D)), ANY, ANY],
        out_specs=[row(D), row(D), row(F2), row(D)],
        out_shape=[jax.ShapeDtypeStruct((T, D), F32), jax.ShapeDtypeStruct((T, D), MXU_DTYPE),
                   jax.ShapeDtypeStruct((T, F2), F32), jax.ShapeDtypeStruct((T, D), F32)],
        scratch_shapes=[pltpu.VMEM(wgu.shape, wgu.dtype), pltpu.VMEM(wd.shape, wd.dtype),
                        pltpu.SemaphoreType.DMA((2,))],
        compiler_params=_cp("arbitrary"),
    )(x, g0, g1, wgu, wd)


def _ffn_bwd_down(dxo, y, gu, g1, wd, name):
    T, D = y.shape
    F2 = gu.shape[1]
    F = F2 // 2
    tm = _pick(T, 256, SUBLANES)

    def body(dxo_ref, y_ref, gu_ref, g1_ref, wd_hbm, dy_ref, a_ref, dgu_ref, dg1_ref, wd_v, sem):
        _load_resident([(wd_hbm, wd_v)], sem)

        @pl.when(pl.program_id(0) == 0)
        def _():
            dg1_ref[...] = jnp.zeros_like(dg1_ref)

        yv = y_ref[...]
        dy, dg1 = _rms_bwd(yv, _rms_stat(yv), g1_ref[...], 0.5 * dxo_ref[...])
        dg1_ref[...] += dg1
        dyb = _mx(dy)
        dy_ref[...] = dyb
        da = _dot(dyb, wd_v[...], NT)
        gu_v = gu_ref[...]
        g = gu_v[:, :F]
        u = gu_v[:, F:]
        s = _sigmoid(g)
        sg = g * s
        a_ref[...] = _mx(sg * u)
        dgu_ref[:, :F] = _mx(da * u * (s * (1.0 + g * (1.0 - s))))
        dgu_ref[:, F:] = _mx(da * sg)

    row = lambda w: pl.BlockSpec((tm, w), lambda i: (i, 0))
    return pl.pallas_call(
        body, name=name, grid=(T // tm,),
        in_specs=[row(D), row(D), row(F2), _full((1, D)), ANY],
        out_specs=[row(D), row(F), row(F2), _full((1, D))],
        out_shape=[jax.ShapeDtypeStruct((T, D), MXU_DTYPE), jax.ShapeDtypeStruct((T, F), MXU_DTYPE),
                   jax.ShapeDtypeStruct((T, F2), MXU_DTYPE), jax.ShapeDtypeStruct((1, D), F32)],
        scratch_shapes=[pltpu.VMEM(wd.shape, wd.dtype), pltpu.SemaphoreType.DMA((1,))],
        compiler_params=_cp("arbitrary"),
    )(dxo, y, gu, g1, wd)


def _ffn_bwd_up(dgu, x, dxo, g0, wgu, name):
    T, D = x.shape
    F2 = dgu.shape[1]
    tm = _pick(T, 256, SUBLANES)

    def body(dgu_ref, x_ref, dxo_ref, g0_ref, wgu_hbm, dx_ref, dg0_ref, wgu_v, sem):
        _load_resident([(wgu_hbm, wgu_v)], sem)

        @pl.when(pl.program_id(0) == 0)
        def _():
            dg0_ref[...] = jnp.zeros_like(dg0_ref)

        dh = _dot(dgu_ref[...], wgu_v[...], NT)
        xv = x_ref[...]
        dx, dg0 = _rms_bwd(xv, _rms_stat(xv), g0_ref[...], dh)
        dg0_ref[...] += dg0
        dx_ref[...] = dxo_ref[...] + dx

    row = lambda w: pl.BlockSpec((tm, w), lambda i: (i, 0))
    return pl.pallas_call(
        body, name=name, grid=(T // tm,),
        in_specs=[row(F2), row(D), row(D), _full((1, D)), ANY],
        out_specs=[row(D), _full((1, D))],
        out_shape=[jax.ShapeDtypeStruct((T, D), F32), jax.ShapeDtypeStruct((1, D), F32)],
        scratch_shapes=[pltpu.VMEM(wgu.shape, wgu.dtype), pltpu.SemaphoreType.DMA((1,))],
        compiler_params=_cp("arbitrary"),
    )(dgu, x, dxo, g0, wgu)


def _norm_fwd(x, g, name):
    T, D = x.shape
    tm = _pick(T, 512, SUBLANES)

    def body(x_ref, g_ref, h_ref):
        xv = x_ref[...]
        h_ref[...] = _mx(xv * _rms_stat(xv) * g_ref[...])

    row = pl.BlockSpec((tm, D), lambda i: (i, 0))
    return pl.pallas_call(body, name=name, grid=(T // tm,), in_specs=[row, _full((1, D))], out_specs=row,
                          out_shape=jax.ShapeDtypeStruct((T, D), MXU_DTYPE), compiler_params=_cp("parallel"))(x, g)


def _postnorm_fwd(x, m, g, name):
    T, D = x.shape
    tm = _pick(T, 512, SUBLANES)

    def body(x_ref, m_ref, g_ref, o_ref):
        mv = m_ref[...]
        o_ref[...] = x_ref[...] + mv * _rms_stat(mv) * g_ref[...]

    row = pl.BlockSpec((tm, D), lambda i: (i, 0))
    return pl.pallas_call(body, name=name, grid=(T // tm,), in_specs=[row, row, _full((1, D))], out_specs=row,
                          out_shape=jax.ShapeDtypeStruct((T, D), F32), compiler_params=_cp("parallel"))(x, m, g)


def _postnorm_bwd(dxo, m, g, name):
    T, D = m.shape
    tm = _pick(T, 512, SUBLANES)

    def body(dxo_ref, m_ref, g_ref, dm_ref, dg_ref):
        @pl.when(pl.program_id(0) == 0)
        def _():
            dg_ref[...] = jnp.zeros_like(dg_ref)

        mv = m_ref[...]
        dm, dg = _rms_bwd(mv, _rms_stat(mv), g_ref[...], dxo_ref[...])
        dg_ref[...] += dg
        dm_ref[...] = _mx(dm)

    row = pl.BlockSpec((tm, D), lambda i: (i, 0))
    return pl.pallas_call(body, name=name, grid=(T // tm,), in_specs=[row, row, _full((1, D))],
                          out_specs=[row, _full((1, D))],
                          out_shape=[jax.ShapeDtypeStruct((T, D), MXU_DTYPE), jax.ShapeDtypeStruct((1, D), F32)],
                          compiler_params=_cp("arbitrary"))(dxo, m, g)


def _prenorm_bwd(dxo, dh, x, g, name):
    T, D = x.shape
    tm = _pick(T, 512, SUBLANES)

    def body(dxo_ref, dh_ref, x_ref, g_ref, dx_ref, dg_ref):
        @pl.when(pl.program_id(0) == 0)
        def _():
            dg_ref[...] = jnp.zeros_like(dg_ref)

        xv = x_ref[...]
        dx, dg = _rms_bwd(xv, _rms_stat(xv), g_ref[...], dh_ref[...])
        dg_ref[...] += dg
        dx_ref[...] = dxo_ref[...] + dx

    row = pl.BlockSpec((tm, D), lambda i: (i, 0))
    return pl.pallas_call(body, name=name, grid=(T // tm,), in_specs=[row, row, row, _full((1, D))],
                          out_specs=[row, _full((1, D))],
                          out_shape=[jax.ShapeDtypeStruct((T, D), F32), jax.ShapeDtypeStruct((1, D), F32)],
                          compiler_params=_cp("arbitrary"))(dxo, dh, x, g)


def _loss_fwd_bwd(y, target, name):
    T, D = y.shape
    tm = _pick(T, 512, SUBLANES)

    def body(y_ref, t_ref, l_ref, dy_ref):
        @pl.when(pl.program_id(0) == 0)
        def _():
            l_ref[...] = jnp.zeros_like(l_ref)

        e = y_ref[...] - t_ref[...]
        dy_ref[...] = e * (1.0 / D)
        l_ref[...] += 0.5 * jnp.sum(jnp.mean(e * e, axis=-1, keepdims=True), axis=0, keepdims=True)

    row = pl.BlockSpec((tm, D), lambda i: (i, 0))
    return pl.pallas_call(body, name=name, grid=(T // tm,), in_specs=[row, row],
                          out_specs=[_full((SUBLANES, LANES)), row],
                          out_shape=[jax.ShapeDtypeStruct((SUBLANES, LANES), F32), jax.ShapeDtypeStruct((T, D), F32)],
                          compiler_params=_cp("arbitrary"))(y, target)


DN_ROWS = 512


def _shift_down(prev8, cur, s):
    n = cur.shape[0]
    xx = jnp.concatenate([prev8, cur], axis=0)
    return pltpu.roll(xx, s, 0)[SUBLANES:SUBLANES + n, :]


def _shift_up(cur, next8, s):
    n = cur.shape[0]
    xx = jnp.concatenate([cur, next8], axis=0)
    return pltpu.roll(xx, n + SUBLANES - s, 0)[:n, :]


def _conv_tile(x_ref, w, r, rows):
    start = pl.multiple_of(r * rows, SUBLANES)
    cur = x_ref[pl.ds(start, rows), :]
    pstart = pl.multiple_of(jnp.maximum(start - SUBLANES, 0), SUBLANES)
    prev8 = jnp.where(r == 0, 0.0, x_ref[pl.ds(pstart, SUBLANES), :])
    taps = [_shift_down(prev8, cur, DN_CONV - 1 - j) if j < DN_CONV - 1 else cur for j in range(DN_CONV)]
    c = taps[0] * w[0:1, :]
    for j in range(1, DN_CONV):
        c = c + taps[j] * w[j:j + 1, :]
    return c, taps


def _dn_prep_fwd(proj, conv_w, name):
    T = proj.shape[0]
    W = DN_HEADS * DN_HEAD_DIM
    rows = min(DN_ROWS, T)
    n_inner = T // rows
    scale = DN_HEAD_DIM ** -0.5

    def body(x_ref, w_ref, o_ref):
        cb = pl.program_id(0)
        w = w_ref[...]
        is_qk = cb < 2 * DN_HEADS
        post = jnp.where(cb < DN_HEADS, scale, 1.0)

        def step(r, carry):
            c, _ = _conv_tile(x_ref, w, r, rows)
            s = c * _sigmoid(c)
            rinv = lax.rsqrt(jnp.sum(s * s, axis=-1, keepdims=True) + L2_EPS)
            o_ref[pl.ds(pl.multiple_of(r * rows, SUBLANES), rows), :] = jnp.where(is_qk, s * rinv * post, s)
            return carry

        lax.fori_loop(0, n_inner, step, 0)

    col = pl.BlockSpec((T, LANES), lambda j: (0, j))
    return pl.pallas_call(body, name=name, grid=(3 * W // LANES,),
                          in_specs=[col, pl.BlockSpec((DN_CONV, LANES), lambda j: (0, j))], out_specs=col,
                          out_shape=jax.ShapeDtypeStruct((T, 3 * W), F32), compiler_params=_cp("parallel"))(proj, conv_w)


def _dn_prep_bwd(proj, conv_w, dqkv, name):
    T = proj.shape[0]
    W = DN_HEADS * DN_HEAD_DIM
    rows = min(DN_ROWS, T)
    n_inner = T // rows
    scale = DN_HEAD_DIM ** -0.5

    def body(x_ref, w_ref, dy_ref, dx_ref, dw_ref, dc_scr):
        cb = pl.program_id(0)
        w = w_ref[...]
        is_qk = cb < 2 * DN_HEADS
        post = jnp.where(cb < DN_HEADS, scale, 1.0)

        def step1(r, dws):
            c, taps = _conv_tile(x_ref, w, r, rows)
            sg = _sigmoid(c)
            s = c * sg
            rinv = lax.rsqrt(jnp.sum(s * s, axis=-1, keepdims=True) + L2_EPS)
            dy = dy_ref[pl.ds(pl.multiple_of(r * rows, SUBLANES), rows), :]
            yn = s * rinv
            dyn = dy * post
            ds_qk = rinv * (dyn - yn * jnp.sum(dyn * yn, axis=-1, keepdims=True))
            ds = jnp.where(is_qk, ds_qk, dy)
            dc = ds * (sg * (1.0 + c * (1.0 - sg)))
            dc_scr[pl.ds(pl.multiple_of(r * rows, SUBLANES), rows), :] = dc
            return tuple(dws[j] + jnp.sum(dc * taps[j], axis=0, keepdims=True) for j in range(DN_CONV))

        zero = jnp.zeros((1, LANES), F32)
        dws = lax.fori_loop(0, n_inner, step1, (zero,) * DN_CONV)
        for j in range(DN_CONV):
            dw_ref[j:j + 1, :] = dws[j]

        def step2(r, carry):
            start = pl.multiple_of(r * rows, SUBLANES)
            cur = dc_scr[pl.ds(start, rows), :]
            nstart = pl.multiple_of(jnp.minimum(start + rows, T - SUBLANES), SUBLANES)
            next8 = jnp.where(r == n_inner - 1, 0.0, dc_scr[pl.ds(nstart, SUBLANES), :])
            dx = cur * w[DN_CONV - 1:DN_CONV, :]
            for j in range(DN_CONV - 1):
                dx = dx + _shift_up(cur, next8, DN_CONV - 1 - j) * w[j:j + 1, :]
            dx_ref[pl.ds(start, rows), :] = _mx(dx)
            return carry

        lax.fori_loop(0, n_inner, step2, 0)

    col = pl.BlockSpec((T, LANES), lambda j: (0, j))
    wspec = pl.BlockSpec((DN_CONV, LANES), lambda j: (0, j))
    return pl.pallas_call(body, name=name, grid=(3 * W // LANES,), in_specs=[col, wspec, col], out_specs=[col, wspec],
                          out_shape=[jax.ShapeDtypeStruct((T, 3 * W), MXU_DTYPE), jax.ShapeDtypeStruct((DN_CONV, 3 * W), F32)],
                          scratch_shapes=[pltpu.VMEM((T, LANES), F32)], compiler_params=_cp("parallel"))(proj, conv_w, dqkv)


def _softplus(x):
    return jnp.maximum(x, 0.0) + jnp.log(1.0 + jnp.exp(-jnp.abs(x)))


def _dn_gate_fwd(ba, a_log, dt_bias, name):
    T = ba.shape[0]
    tm = _pick(T, 1024, SUBLANES)

    def body(ba_ref, al_ref, dt_ref, beta_ref, g_ref):
        beta_ref[...] = _sigmoid(ba_ref[:, :LANES])
        g_ref[...] = -jnp.exp(al_ref[...]) * _softplus(ba_ref[:, LANES:] + dt_ref[...])

    row = lambda w: pl.BlockSpec((tm, w), lambda i: (i, 0))
    return pl.pallas_call(body, name=name, grid=(T // tm,), in_specs=[row(2 * LANES), _full((1, LANES)), _full((1, LANES))],
                          out_specs=[row(LANES), row(LANES)],
                          out_shape=[jax.ShapeDtypeStruct((T, LANES), F32)] * 2, compiler_params=_cp("parallel"))(ba, a_log, dt_bias)


def _dn_gate_bwd(ba, a_log, dt_bias, dbeta, dg, name):
    T = ba.shape[0]
    tm = _pick(T, 1024, SUBLANES)

    def body(ba_ref, al_ref, dt_ref, dbeta_ref, dg_ref, dba_ref, dal_ref, ddt_ref):
        @pl.when(pl.program_id(0) == 0)
        def _():
            dal_ref[...] = jnp.zeros_like(dal_ref)
            ddt_ref[...] = jnp.zeros_like(ddt_ref)

        beta = _sigmoid(ba_ref[:, :LANES])
        dba_ref[:, :LANES] = _mx(dbeta_ref[...] * beta * (1.0 - beta))
        pre = ba_ref[:, LANES:] + dt_ref[...]
        ea = jnp.exp(al_ref[...])
        dgv = dg_ref[...]
        da = dgv * (-ea) * _sigmoid(pre)
        dba_ref[:, LANES:] = _mx(da)
        ddt_ref[...] += jnp.sum(da, axis=0, keepdims=True)
        dal_ref[...] += jnp.sum(dgv * (-ea) * _softplus(pre), axis=0, keepdims=True)

    row = lambda w: pl.BlockSpec((tm, w), lambda i: (i, 0))
    one = _full((1, LANES))
    return pl.pallas_call(body, name=name, grid=(T // tm,), in_specs=[row(2 * LANES), one, one, row(LANES), row(LANES)],
                          out_specs=[row(2 * LANES), one, one],
                          out_shape=[jax.ShapeDtypeStruct((T, 2 * LANES), MXU_DTYPE), jax.ShapeDtypeStruct((1, LANES), F32),
                                     jax.ShapeDtypeStruct((1, LANES), F32)],
                          compiler_params=_cp("arbitrary"))(ba, a_log, dt_bias, dbeta, dg)


def _tri(c, strict):
    i = lax.broadcasted_iota(jnp.int32, (c, c), 0)
    j = lax.broadcasted_iota(jnp.int32, (c, c), 1)
    return (i > j) if strict else (i >= j)


def _inv_unit_lower(L):
    c = L.shape[0]
    i = lax.broadcasted_iota(jnp.int32, (c, c), 0)
    j = lax.broadcasted_iota(jnp.int32, (c, c), 1)
    x = jnp.where(i == j, 1.0, 0.0) - L
    p = L
    for _ in range(int(math.log2(c)) - 1):
        p = _dot(p, p, NN, HI)
        x = x + _dot(x, p, NN, HI)
    return x


def _chunk_gates(g_blk):
    c = g_blk.shape[0]
    gcs = _dot(jnp.where(_tri(c, False), 1.0, 0.0), g_blk, NN, HI)
    return gcs, gcs.T


def _head_chunk(h, qh, kh, vh, beta_blk, gcs, gcs_t):
    c = qh.shape[0]
    incl = _tri(c, False)
    gc_col = gcs[:, h:h + 1]
    gc_row = gcs_t[h:h + 1, :]
    gc_last = gcs_t[h:h + 1, c - 1:c]
    dec = jnp.where(incl, jnp.exp(jnp.where(incl, gc_col - gc_row, 0.0)), 0.0)
    gam = jnp.exp(gc_col)
    rr = jnp.exp(gc_last - gc_col)
    gl = jnp.exp(gc_last)
    b = beta_blk[:, h:h + 1]
    kb = kh * b
    vb = vh * b
    kk = _dot(_mx(kb), _mx(kh), NT)
    lmat = jnp.where(_tri(c, True), kk * dec, 0.0)
    qk = _dot(_mx(qh), _mx(kh), NT)
    pmat = jnp.where(incl, qk * dec, 0.0)
    return dict(dec=dec, gam=gam, rr=rr, gl=gl, b=b, kb=kb, vb=vb, lmat=lmat, pmat=pmat)


def _dn_scan_fwd(qkv, beta, g, proj, norm_g, name):
    T = qkv.shape[0]
    C, H, Dh = DN_CHUNK, DN_HEADS, DN_HEAD_DIM
    W = H * Dh
    N = T // C

    def body(q_ref, k_ref, v_ref, beta_ref, g_ref, z_ref, ng_ref, og_ref, o_ref, tinv_ref, s_ref, state):
        @pl.when(pl.program_id(0) == 0)
        def _():
            state[...] = jnp.zeros_like(state)

        gcs, gcs_t = _chunk_gates(g_ref[...])
        beta_blk = beta_ref[...]
        ng = ng_ref[...]
        for h in range(H):
            cs = slice(h * Dh, (h + 1) * Dh)
            qh, kh, vh = q_ref[:, cs], k_ref[:, cs], v_ref[:, cs]
            q = _head_chunk(h, qh, kh, vh, beta_blk, gcs, gcs_t)
            tinv = _inv_unit_lower(q["lmat"])
            tinv_ref[h] = tinv
            u = _dot(tinv, q["vb"], NN, HI)
            w = _dot(tinv, q["kb"] * q["gam"], NN, HI)
            s = state[h]
            s_ref[0, h] = s
            sb = _mx(s)
            vnew = u - _dot(_mx(w), sb, NN)
            vnb = _mx(vnew)
            o = _dot(_mx(qh * q["gam"]), sb, NN) + _dot(_mx(q["pmat"]), vnb, NN)
            state[h] = s * q["gl"] + _dot(_mx((kh * q["rr"]).T), vnb, NN)
            o_ref[:, cs] = o
            zh = z_ref[:, cs]
            og_ref[:, cs] = _mx(o * _rms_stat(o) * ng * (zh * _sigmoid(zh)))

    blk = lambda j: pl.BlockSpec((C, W), lambda n: (n, j))
    small = pl.BlockSpec((C, LANES), lambda n: (n, 0))
    return pl.pallas_call(
        body, name=name, grid=(N,),
        in_specs=[blk(0), blk(1), blk(2), small, small, blk(3), _full((1, Dh))],
        out_specs=[blk(0), blk(0), pl.BlockSpec((H, C, C), lambda n: (0, n, 0)),
                   pl.BlockSpec((1, H, Dh, Dh), lambda n: (n, 0, 0, 0))],
        out_shape=[jax.ShapeDtypeStruct((T, W), MXU_DTYPE), jax.ShapeDtypeStruct((T, W), F32),
                   jax.ShapeDtypeStruct((H, T, C), F32), jax.ShapeDtypeStruct((N, H, Dh, Dh), F32)],
        scratch_shapes=[pltpu.VMEM((H, Dh, Dh), F32)],
        compiler_params=_cp("arbitrary"),
    )(qkv, qkv, qkv, beta, g, proj, norm_g)


def _dn_scan_bwd(qkv, beta, g, proj, norm_g, o, tinv, s_all, dog, name):
    T = qkv.shape[0]
    C, H, Dh = DN_CHUNK, DN_HEADS, DN_HEAD_DIM
    W = H * Dh
    N = T // C

    def body(q_ref, k_ref, v_ref, beta_ref, g_ref, z_ref, ng_ref, o_ref, tinv_ref, s_ref, dog_ref,
             dqkv_ref, dbeta_ref, dg_ref, dz_ref, dng_ref, dstate):
        @pl.when(pl.program_id(0) == 0)
        def _():
            dstate[...] = jnp.zeros_like(dstate)
            dng_ref[...] = jnp.zeros_like(dng_ref)

        gcs, gcs_t = _chunk_gates(g_ref[...])
        beta_blk = beta_ref[...]
        ng = ng_ref[...]
        incl = _tri(C, False)
        strict = _tri(C, True)
        lane = lax.broadcasted_iota(jnp.int32, (C, LANES), 1)
        rowi = lax.broadcasted_iota(jnp.int32, (C, 1), 0)
        ones = jnp.ones((C, LANES), F32)
        dbeta_acc = jnp.zeros((C, LANES), F32)
        dgc_acc = jnp.zeros((C, LANES), F32)
        dng_acc = jnp.zeros((1, Dh), F32)
        for h in range(H):
            cs = slice(h * Dh, (h + 1) * Dh)
            qh, kh, vh = q_ref[:, cs], k_ref[:, cs], v_ref[:, cs]
            oh = o_ref[:, cs]
            zh = z_ref[:, cs]
            dogh = dog_ref[:, cs]
            rstat = _rms_stat(oh)
            sz = _sigmoid(zh)
            dz_ref[:, cs] = _mx(dogh * (oh * rstat * ng) * (sz * (1.0 + zh * (1.0 - sz))))
            do, dng = _rms_bwd(oh, rstat, ng, dogh * (zh * sz))
            dng_acc = dng_acc + dng
            q = _head_chunk(h, qh, kh, vh, beta_blk, gcs, gcs_t)
            gam, rr, gl, b, kb, vb = q["gam"], q["rr"], q["gl"], q["b"], q["kb"], q["vb"]
            tm = tinv_ref[h]
            u = _dot(tm, vb, NN, HI)
            w = _dot(tm, kb * gam, NN, HI)
            s = s_ref[0, h]
            sb = _mx(s)
            wb = _mx(w)
            vnew = u - _dot(wb, sb, NN)
            vnb = _mx(vnew)
            dsn = dstate[h]
            dsb = _mx(dsn)
            dob = _mx(do)
            pb = _mx(q["pmat"])
            kd = kh * rr
            qd = qh * gam
            dvnew = _dot(pb, dob, TN) + _dot(_mx(kd), dsb, NN)
            dvb16 = _mx(dvnew)
            dp = jnp.where(incl, _dot(dob, vnb, NT), 0.0)
            dqd = _dot(dob, sb, NT)
            dkd = _dot(vnb, dsb, NT)
            dgl = jnp.sum(jnp.sum(s * dsn, axis=1, keepdims=True), axis=0, keepdims=True)
            dw = -_dot(dvb16, sb, NT)
            dstate[h] = _dot(_mx(qd), dob, TN) + gl * dsn - _dot(wb, dvb16, TN)
            dvb = _dot(tm, dvnew, TN, HI)
            dkbg = _dot(tm, dw, TN, HI)
            dl = jnp.where(strict, -(_dot(dvb, u, NT, HI) + _dot(dkbg, w, NT, HI)), 0.0)
            dkk = dl * q["dec"]
            dqk = dp * q["dec"]
            mmat = dl * q["lmat"] + dp * q["pmat"]
            dgc = jnp.sum(mmat, axis=1, keepdims=True) - _dot(mmat, ones, TN, HI)[:, :1]
            dkk16 = _mx(dkk)
            dqk16 = _mx(dqk)
            dkb = _dot(dkk16, _mx(kh), NN) + dkbg * gam
            dk = _dot(dkk16, _mx(kb), TN) + _dot(dqk16, _mx(qh), TN) + dkb * b + dkd * rr
            dq = _dot(dqk16, _mx(kh), NN) + dqd * gam
            dgam = jnp.sum(dkbg * kb, axis=1, keepdims=True) + jnp.sum(dqd * qh, axis=1, keepdims=True)
            dr = jnp.sum(dkd * kh, axis=1, keepdims=True)
            dgc = dgc + dgam * gam - dr * rr
            dgc_last = jnp.sum(dr * rr, axis=0, keepdims=True) + dgl * gl
            dgc = dgc + jnp.where(rowi == C - 1, dgc_last, 0.0)
            dbeta = jnp.sum(dvb * vh, axis=1, keepdims=True) + jnp.sum(dkb * kh, axis=1, keepdims=True)
            dqkv_ref[:, cs] = dq
            dqkv_ref[:, W + h * Dh:W + (h + 1) * Dh] = dk
            dqkv_ref[:, 2 * W + h * Dh:2 * W + (h + 1) * Dh] = dvb * b
            dbeta_acc = jnp.where(lane == h, dbeta, dbeta_acc)
            dgc_acc = jnp.where(lane == h, dgc, dgc_acc)
        dbeta_ref[...] = dbeta_acc
        dg_ref[...] = _dot(jnp.where(incl, 1.0, 0.0), dgc_acc, TN, HI)
        dng_ref[...] += dng_acc

    rev = lambda n: N - 1 - n
    blk = lambda j: pl.BlockSpec((C, W), lambda n: (rev(n), j))
    small = pl.BlockSpec((C, LANES), lambda n: (rev(n), 0))
    return pl.pallas_call(
        body, name=name, grid=(N,),
        in_specs=[blk(0), blk(1), blk(2), small, small, blk(3), _full((1, Dh)), blk(0),
                  pl.BlockSpec((H, C, C), lambda n: (0, rev(n), 0)),
                  pl.BlockSpec((1, H, Dh, Dh), lambda n: (rev(n), 0, 0, 0)), blk(0)],
        out_specs=[pl.BlockSpec((C, 3 * W), lambda n: (rev(n), 0)), small, small, blk(0), _full((1, Dh))],
        out_shape=[jax.ShapeDtypeStruct((T, 3 * W), F32), jax.ShapeDtypeStruct((T, LANES), F32),
                   jax.ShapeDtypeStruct((T, LANES), F32), jax.ShapeDtypeStruct((T, W), MXU_DTYPE),
                   jax.ShapeDtypeStruct((1, Dh), F32)],
        scratch_shapes=[pltpu.VMEM((H, Dh, Dh), F32)],
        compiler_params=_cp("arbitrary"),
    )(qkv, qkv, qkv, beta, g, proj, norm_g, o, tinv, s_all, dog)


_INV_SQRT2 = 0.7071067811865476
_INV_SQRT_2PI = 0.3989422804014327


def _sg_recompute(zp_ref, bin_ref, lng_ref, lnb_ref):
    E = SG_WIDTH
    zin = zp_ref[...] + bin_ref[...]
    cdf = 0.5 * (1.0 + lax.erf(zin * _INV_SQRT2))
    zz = zin * cdf
    u = zz[:, :E]
    vp = zz[:, E:]
    mu = jnp.mean(vp, axis=-1, keepdims=True)
    xc = vp - mu
    rstd = lax.rsqrt(jnp.mean(xc * xc, axis=-1, keepdims=True) + LN_EPS)
    xhat = xc * rstd
    v = xhat * lng_ref[...] + lnb_ref[...]
    return zin, cdf, u, xhat, rstd, v


def _sg_masked_ws(ws_ref, g):
    return _mx(jnp.where(_tri(SG_CHUNK, False), ws_ref[g], 0.0))


def _sg_fwd(zpre, b_in, ln_g, ln_b, w_s, b_s_t, name):
    T = zpre.shape[0]
    E, G, C, GW = SG_WIDTH, SG_GROUPS, SG_CHUNK, SG_GROUP_W

    def body(zp_ref, bin_ref, lng_ref, lnb_ref, ws_ref, bst_ref, um_ref):
        _, _, u, _, _, v = _sg_recompute(zp_ref, bin_ref, lng_ref, lnb_ref)
        bst = bst_ref[...]
        for g in range(G):
            cs = slice(g * GW, (g + 1) * GW)
            mixed = _dot(_sg_masked_ws(ws_ref, g), _mx(v[:, cs]), NN) + bst[:, g:g + 1]
            um_ref[:, cs] = _mx(u[:, cs] * mixed)

    return pl.pallas_call(
        body, name=name, grid=(T // C,),
        in_specs=[pl.BlockSpec((C, 2 * E), lambda n: (n, 0)), _full((1, 2 * E)), _full((1, E)), _full((1, E)),
                  _full((G, C, C)), _full((C, LANES))],
        out_specs=pl.BlockSpec((C, E), lambda n: (n, 0)),
        out_shape=jax.ShapeDtypeStruct((T, E), MXU_DTYPE), compiler_params=_cp("parallel"),
    )(zpre, b_in, ln_g, ln_b, w_s, b_s_t)


def _sg_bwd(zpre, b_in, ln_g, ln_b, w_s, b_s_t, dum, name):
    T = zpre.shape[0]
    E, G, C, GW = SG_WIDTH, SG_GROUPS, SG_CHUNK, SG_GROUP_W

    def body(zp_ref, bin_ref, lng_ref, lnb_ref, ws_ref, bst_ref, dum_ref,
             dz_ref, dbin_ref, dlng_ref, dlnb_ref, dws_ref, dbst_ref):
        @pl.when(pl.program_id(0) == 0)
        def _():
            for r in (dbin_ref, dlng_ref, dlnb_ref, dws_ref, dbst_ref):
                r[...] = jnp.zeros_like(r)

        zin, cdf, u, xhat, rstd, v = _sg_recompute(zp_ref, bin_ref, lng_ref, lnb_ref)
        bst = bst_ref[...]
        lane = lax.broadcasted_iota(jnp.int32, (C, LANES), 1)
        dum_v = dum_ref[...]
        dbst = jnp.zeros((C, LANES), F32)
        du_parts, dv_parts = [], []
        for g in range(G):
            cs = slice(g * GW, (g + 1) * GW)
            wsm = _sg_masked_ws(ws_ref, g)
            vg = _mx(v[:, cs])
            mixed = _dot(wsm, vg, NN) + bst[:, g:g + 1]
            dumg = dum_v[:, cs]
            du_parts.append(dumg * mixed)
            dmixed = dumg * u[:, cs]
            dmb = _mx(dmixed)
            dv_parts.append(_dot(wsm, dmb, TN))
            dws_ref[g] += _dot(dmb, vg, NT)
            dbst = jnp.where(lane == g, jnp.sum(dmixed, axis=1, keepdims=True), dbst)
        dbst_ref[...] += dbst
        du = jnp.concatenate(du_parts, axis=1)
        dv = jnp.concatenate(dv_parts, axis=1)
        dlng_ref[...] += jnp.sum(dv * xhat, axis=0, keepdims=True)
        dlnb_ref[...] += jnp.sum(dv, axis=0, keepdims=True)
        dxh = dv * lng_ref[...]
        dvp = rstd * (dxh - jnp.mean(dxh, axis=-1, keepdims=True) - xhat * jnp.mean(dxh * xhat, axis=-1, keepdims=True))
        dzz = jnp.concatenate([du, dvp], axis=1)
        dzin = dzz * (cdf + zin * (_INV_SQRT_2PI * jnp.exp(-0.5 * zin * zin)))
        dz_ref[...] = _mx(dzin)
        dbin_ref[...] += jnp.sum(dzin, axis=0, keepdims=True)

    return pl.pallas_call(
        body, name=name, grid=(T // C,),
        in_specs=[pl.BlockSpec((C, 2 * E), lambda n: (n, 0)), _full((1, 2 * E)), _full((1, E)), _full((1, E)),
                  _full((G, C, C)), _full((C, LANES)), pl.BlockSpec((C, E), lambda n: (n, 0))],
        out_specs=[pl.BlockSpec((C, 2 * E), lambda n: (n, 0)), _full((1, 2 * E)), _full((1, E)), _full((1, E)),
                   _full((G, C, C)), _full((C, LANES))],
        out_shape=[jax.ShapeDtypeStruct((T, 2 * E), MXU_DTYPE), jax.ShapeDtypeStruct((1, 2 * E), F32),
                   jax.ShapeDtypeStruct((1, E), F32), jax.ShapeDtypeStruct((1, E), F32),
                   jax.ShapeDtypeStruct((G, C, C), F32), jax.ShapeDtypeStruct((C, LANES), F32)],
        compiler_params=_cp("arbitrary"),
    )(zpre, b_in, ln_g, ln_b, w_s, b_s_t, dum)


def _row(v):
    return v.reshape(1, -1)


def _pad_lanes(v):
    v = v.reshape(1, -1)
    return jnp.pad(v, ((0, 0), (0, LANES - v.shape[1])))


def _local_step(x, target, p):
    ng = p["norm_g"]
    grads = {}
    dng = [[None] * 6 for _ in range(2)]
    saved = []

    def ffn_f(xin, i, j, tag):
        xo, h, gu, y = _ffn_fwd(xin, _row(ng[i, 4 * j]), _row(ng[i, 4 * j + 1]), p["wgu"][i][j], p["wd"][i][j], "ffn_fwd_" + tag)
        return xo, (xin, h, gu, y)

    x1, sv_f00 = ffn_f(x, 0, 0, "00")
    hn0 = _norm_fwd(x1, _row(ng[0, 2]), "dn_prenorm")
    proj = _mm(hn0, p["dn_wqkvz"], "nn", "dn_proj")
    ba = _mm(hn0, p["dn_wba"], "nn", "dn_proj_ba")
    a_log = _pad_lanes(p["dn_a_log"])
    dt_bias = _pad_lanes(p["dn_dt_bias"])
    dn_ng = _row(p["dn_norm_g"])
    qkv = _dn_prep_fwd(proj, p["dn_conv_w"], "dn_prep_fwd")
    beta, gdec = _dn_gate_fwd(ba, a_log, dt_bias, "dn_gate_fwd")
    og, o_raw, tinv, s_all = _dn_scan_fwd(qkv, beta, gdec, proj, dn_ng, "dn_scan_fwd")
    m0 = _mm(og, p["dn_wout"], "nn", "dn_out")
    x2 = _postnorm_fwd(x1, m0, _row(ng[0, 3]), "dn_postnorm")
    x3, sv_f01 = ffn_f(x2, 0, 1, "01")
    x4, sv_f10 = ffn_f(x3, 1, 0, "10")
    hn1 = _norm_fwd(x4, _row(ng[1, 2]), "sg_prenorm")
    zpre = _mm(hn1, p["sg_win"], "nn", "sg_proj")
    sg_bin = _row(p["sg_b_in"])
    sg_lng = _row(p["sg_ln_g"])
    sg_lnb = _row(p["sg_ln_b"])
    sg_bst = jnp.pad(p["sg_b_s"].T, ((0, 0), (0, LANES - SG_GROUPS)))
    um = _sg_fwd(zpre, sg_bin, sg_lng, sg_lnb, p["sg_w_s"], sg_bst, "sg_fwd")
    m1 = _mm(um, p["sg_wout"], "nn", "sg_out")
    x5 = _postnorm_fwd(x4, m1, _row(ng[1, 3]), "sg_postnorm")
    x6, sv_f11 = ffn_f(x5, 1, 1, "11")
    loss_part, dx = _loss_fwd_bwd(x6, target, "loss")

    def ffn_b(dxo, sv, i, j, tag):
        xin, h, gu, y = sv
        dy, a, dgu, dg1 = _ffn_bwd_down(dxo, y, gu, _row(ng[i, 4 * j + 1]), p["wd"][i][j], "ffn_bwd_down_" + tag)
        grads["wd%d%d" % (i, j)] = _mm(a, dy, "tn", "ffn_wgrad_down_" + tag)
        grads["wgu%d%d" % (i, j)] = _mm(h, dgu, "tn", "ffn_wgrad_up_" + tag)
        dxi, dg0 = _ffn_bwd_up(dgu, xin, dxo, _row(ng[i, 4 * j]), p["wgu"][i][j], "ffn_bwd_up_" + tag)
        dng[i][4 * j] = dg0
        dng[i][4 * j + 1] = dg1
        return dxi

    dx = ffn_b(dx, sv_f11, 1, 1, "11")
    dm1, dng[1][3] = _postnorm_bwd(dx, m1, _row(ng[1, 3]), "sg_postnorm_bwd")
    grads["sg_w_out"] = _mm(um, dm1, "tn", "sg_wgrad_out")
    dum = _mm(dm1, p["sg_wout"], "nt", "sg_dgrad_out")
    dz1, dbin, dlng, dlnb, dws, dbst = _sg_bwd(zpre, sg_bin, sg_lng, sg_lnb, p["sg_w_s"], sg_bst, dum, "sg_bwd")
    grads["sg_w_in"] = _mm(hn1, dz1, "tn", "sg_wgrad_in")
    dh1 = _mm(dz1, p["sg_win"], "nt", "sg_dgrad_in")
    dx, dng[1][2] = _prenorm_bwd(dx, dh1, x4, _row(ng[1, 2]), "sg_prenorm_bwd")
    grads["sg_b_in"] = dbin.reshape(1, -1)
    grads["sg_ln_g"] = dlng.reshape(1, -1)
    grads["sg_ln_b"] = dlnb.reshape(1, -1)
    grads["sg_w_s"] = jnp.where(jnp.tril(jnp.ones((SG_CHUNK, SG_CHUNK), bool)), dws, 0.0)[None]
    grads["sg_b_s"] = dbst[:, :SG_GROUPS].T[None]
    dx = ffn_b(dx, sv_f10, 1, 0, "10")
    dx = ffn_b(dx, sv_f01, 0, 1, "01")
    dm0, dng[0][3] = _postnorm_bwd(dx, m0, _row(ng[0, 3]), "dn_postnorm_bwd")
    grads["dn_w_out"] = _mm(og, dm0, "tn", "dn_wgrad_out")
    dog = _mm(dm0, p["dn_wout"], "nt", "dn_dgrad_out")
    dqkv, dbeta, dgdec, dz0, dnng = _dn_scan_bwd(qkv, beta, gdec, proj, dn_ng, o_raw, tinv, s_all, dog, "dn_scan_bwd")
    dqkv_pre, dconv = _dn_prep_bwd(proj, p["dn_conv_w"], dqkv, "dn_prep_bwd")
    dba, dal, ddt = _dn_gate_bwd(ba, a_log, dt_bias, dbeta, dgdec, "dn_gate_bwd")
    W3 = 3 * DN_HEADS * DN_HEAD_DIM
    dw_qkv = _mm(hn0, dqkv_pre, "tn", "dn_wgrad_qkv")
    dw_z = _mm(hn0, dz0, "tn", "dn_wgrad_z")
    dw_ba = _mm(hn0, dba, "tn", "dn_wgrad_ba")
    grads["dn_w_in"] = jnp.concatenate(
        [dw_qkv, dw_z, dw_ba[:, :DN_HEADS], dw_ba[:, LANES:LANES + DN_HEADS]], axis=1)
    dh0 = _mm(dqkv_pre, p["dn_wqkvz"][:, :W3], "nt", "dn_dgrad_qkv")
    dh0 = _mm(dz0, p["dn_wqkvz"][:, W3:], "nt", "dn_dgrad_z", add=dh0)
    dh0 = _mm(dba, p["dn_wba"], "nt", "dn_dgrad_ba", add=dh0)
    dx, dng[0][2] = _prenorm_bwd(dx, dh0, x1, _row(ng[0, 2]), "dn_prenorm_bwd")
    grads["dn_conv_w"] = dconv[None]
    grads["dn_a_log"] = dal[:, :DN_HEADS]
    grads["dn_dt_bias"] = ddt[:, :DN_HEADS]
    grads["dn_norm_g"] = dnng
    dx = ffn_b(dx, sv_f00, 0, 0, "00")
    grads["norm_g"] = jnp.stack([jnp.concatenate(dng[i], axis=0) for i in range(2)])
    return loss_part, dx, grads


def _mesh_pos():
    return lax.axis_index("x"), lax.axis_index("y"), lax.axis_index("c")


def _other_chips(x, y):
    return [(1 - x, y), (x, 1 - y), (1 - x, 1 - y)]


def _allgather_chips(arrs, name):
    n = len(arrs)

    def body(*refs):
        ins, outs = refs[:n], refs[n:2 * n]
        ici_send, ici_recv, d2d_send, d2d_recv = refs[2 * n:]
        x, y, c = _mesh_pos()
        me = 2 * x + y
        chips = _other_chips(x, y)
        sibling = (x, y, 1 - c)

        def ici(i, j, k):
            cx, cy = chips[j]
            return pltpu.make_async_remote_copy(src_ref=ins[i].at[c], dst_ref=outs[i].at[k, c], send_sem=ici_send.at[3 * i + j],
                                                recv_sem=ici_recv.at[3 * i + j], device_id=(cx, cy, c), device_id_type=MESH)

        def d2d(i, j, h):
            cx, cy = chips[j]
            slot = outs[i].at[2 * cx + cy, h]
            return pltpu.make_async_remote_copy(src_ref=slot, dst_ref=slot, send_sem=d2d_send.at[3 * i + j],
                                                recv_sem=d2d_recv.at[3 * i + j], device_id=sibling, device_id_type=MESH)

        sends = [ici(i, j, me) for i in range(n) for j in range(3)]
        for cp in sends:
            cp.start()
        for i in range(n):
            for j, (cx, cy) in enumerate(chips):
                ici(i, j, 2 * cx + cy).wait_recv()
                fwd = d2d(i, j, c)
                fwd.start()
                sends.append(fwd)
        for i in range(n):
            for j in range(3):
                d2d(i, j, 1 - c).wait_recv()
        for cp in sends:
            cp.wait_send()

    return pl.pallas_call(
        body, name=name, in_specs=[ANY] * n, out_specs=[ANY] * n,
        out_shape=[jax.ShapeDtypeStruct((N_CHIPS,) + a.shape, a.dtype) for a in arrs],
        scratch_shapes=[pltpu.SemaphoreType.DMA((3 * n,))] * 4,
    )(*arrs)


def _swap_halves(arrs, name):
    n = len(arrs)

    def body(*refs):
        ins, outs = refs[:n], refs[n:2 * n]
        send_sems, recv_sems = refs[2 * n:]
        x, y, c = _mesh_pos()
        cps = [pltpu.make_async_remote_copy(src_ref=ins[i].at[1 - c], dst_ref=outs[i], send_sem=send_sems.at[i],
                                            recv_sem=recv_sems.at[i], device_id=(x, y, 1 - c), device_id_type=MESH)
               for i in range(n)]
        for cp in cps:
            cp.start()
        for cp in cps:
            cp.wait()

    return pl.pallas_call(
        body, name=name, in_specs=[ANY] * n, out_specs=[ANY] * n,
        out_shape=[jax.ShapeDtypeStruct(a.shape[1:], a.dtype) for a in arrs],
        scratch_shapes=[pltpu.SemaphoreType.DMA((n,)), pltpu.SemaphoreType.DMA((n,))],
    )(*arrs)


def _scatter_chips(arrs, name):
    n = len(arrs)

    def body(*refs):
        ins, outs = refs[:n], refs[n:2 * n]
        send_sems, recv_sems = refs[2 * n:]
        x, y, c = _mesh_pos()
        me = 2 * x + y
        chips = _other_chips(x, y)

        def copy(i, j, src_k, dst_k):
            cx, cy = chips[j]
            return pltpu.make_async_remote_copy(src_ref=ins[i].at[src_k], dst_ref=outs[i].at[dst_k],
                                                send_sem=send_sems.at[3 * i + j], recv_sem=recv_sems.at[3 * i + j],
                                                device_id=(cx, cy, c), device_id_type=MESH)

        sends = [copy(i, j, 2 * chips[j][0] + chips[j][1], me) for i in range(n) for j in range(3)]
        for cp in sends:
            cp.start()
        for i in range(n):
            for j, (cx, cy) in enumerate(chips):
                copy(i, j, me, 2 * cx + cy).wait_recv()
        for cp in sends:
            cp.wait_send()

    return pl.pallas_call(
        body, name=name, in_specs=[ANY] * n, out_specs=[ANY] * n,
        out_shape=[jax.ShapeDtypeStruct(a.shape, a.dtype) for a in arrs],
        scratch_shapes=[pltpu.SemaphoreType.DMA((3 * n,)), pltpu.SemaphoreType.DMA((3 * n,))],
    )(*arrs)


def _swap_whole(arrs, name):
    n = len(arrs)

    def body(*refs):
        ins, outs = refs[:n], refs[n:2 * n]
        send_sems, recv_sems = refs[2 * n:]
        x, y, c = _mesh_pos()
        cps = [pltpu.make_async_remote_copy(src_ref=ins[i], dst_ref=outs[i], send_sem=send_sems.at[i],
                                            recv_sem=recv_sems.at[i], device_id=(x, y, 1 - c), device_id_type=MESH)
               for i in range(n)]
        for cp in cps:
            cp.start()
        for cp in cps:
            cp.wait()

    return pl.pallas_call(
        body, name=name, in_specs=[ANY] * n, out_specs=[ANY] * n,
        out_shape=[jax.ShapeDtypeStruct(a.shape, a.dtype) for a in arrs],
        scratch_shapes=[pltpu.SemaphoreType.DMA((n,)), pltpu.SemaphoreType.DMA((n,))],
    )(*arrs)


def _allgather_devices(a, name):
    masks = [(mx, my, mc) for mx in (0, 1) for my in (0, 1) for mc in (0, 1)][1:]

    def body(in_ref, out_ref, send_sems, recv_sems, loc_sem):
        x, y, c = _mesh_pos()
        me = 4 * x + 2 * y + c
        lc = pltpu.make_async_copy(in_ref, out_ref.at[me], loc_sem.at[0])
        lc.start()
        peers = [(jnp.where(mx, 1 - x, x), jnp.where(my, 1 - y, y), jnp.where(mc, 1 - c, c)) for mx, my, mc in masks]
        cps = [pltpu.make_async_remote_copy(src_ref=in_ref, dst_ref=out_ref.at[me], send_sem=send_sems.at[j],
                                            recv_sem=recv_sems.at[j], device_id=peers[j], device_id_type=MESH)
               for j in range(len(masks))]
        for cp in cps:
            cp.start()
        for j, (px, py, pc) in enumerate(peers):
            pltpu.make_async_remote_copy(src_ref=in_ref, dst_ref=out_ref.at[4 * px + 2 * py + pc], send_sem=send_sems.at[j],
                                         recv_sem=recv_sems.at[j], device_id=peers[j], device_id_type=MESH).wait_recv()
        for cp in cps:
            cp.wait_send()
        lc.wait()

    return pl.pallas_call(
        body, name=name, in_specs=[ANY], out_specs=ANY,
        out_shape=jax.ShapeDtypeStruct((N_DEV,) + a.shape, a.dtype),
        scratch_shapes=[pltpu.SemaphoreType.DMA((N_DEV - 1,)), pltpu.SemaphoreType.DMA((N_DEV - 1,)),
                        pltpu.SemaphoreType.DMA((1,))],
    )(a)


def _as_rows(a, lead):
    shp = a.shape
    rows = 1
    for s in shp[lead:-1]:
        rows *= s
    return a.reshape(shp[:lead] + (rows, shp[-1]))


def _row_tile(rows, cols, n_bufs):
    budget = (24 * 1024 * 1024) // (n_bufs * 2 * 4 * cols)
    return _pick(rows, max(2 * SUBLANES, budget), 2 * SUBLANES)


def _sum_leading(a, name):
    n = a.shape[0]
    v = _as_rows(a, 1)
    _, rows, cols = v.shape
    tr = _row_tile(rows, cols, n + 1)

    def body(a_ref, o_ref):
        acc = a_ref[0]
        for k in range(1, n):
            acc = acc + a_ref[k]
        o_ref[...] = acc

    out = pl.pallas_call(body, name=name, grid=(rows // tr,),
                         in_specs=[pl.BlockSpec((n, tr, cols), lambda i: (0, i, 0))],
                         out_specs=pl.BlockSpec((tr, cols), lambda i: (i, 0)),
                         out_shape=jax.ShapeDtypeStruct((rows, cols), F32), compiler_params=_cp("parallel"))(v)
    return out.reshape(a.shape[1:])


def _scalar(i):
    return jnp.reshape(i, (1,)).astype(jnp.int32)


def _add_own_half(g, other, c, name):
    v = _as_rows(g, 1)
    ov = _as_rows(other, 0)
    rows, cols = ov.shape
    tr = _row_tile(rows, cols, 3)

    def body(c_ref, g_ref, o_ref, out_ref):
        out_ref[...] = (g_ref[0] + o_ref[...]).astype(out_ref.dtype)

    out = pl.pallas_call(
        body, name=name,
        grid_spec=pltpu.PrefetchScalarGridSpec(
            num_scalar_prefetch=1, grid=(rows // tr,),
            in_specs=[pl.BlockSpec((1, tr, cols), lambda i, c_ref: (c_ref[0], i, 0)),
                      pl.BlockSpec((tr, cols), lambda i, c_ref: (i, 0))],
            out_specs=pl.BlockSpec((tr, cols), lambda i, c_ref: (i, 0))),
        out_shape=jax.ShapeDtypeStruct((rows, cols), COMM_DTYPE), compiler_params=_cp("parallel"),
    )(_scalar(c), v, ov)
    return out.reshape(other.shape)


def _sum_chips(own, got, chip, name):
    pv = _as_rows(own, 1)
    bv = _as_rows(got, 1)
    _, rows, cols = pv.shape
    tr = _row_tile(rows, cols, N_CHIPS + 2)

    def body(chip_ref, p_ref, b_ref, o_ref):
        mine = p_ref[0].astype(F32)
        acc = jnp.where(chip_ref[0] == 0, mine, b_ref[0].astype(F32))
        for k in range(1, N_CHIPS):
            acc = acc + jnp.where(chip_ref[0] == k, mine, b_ref[k].astype(F32))
        o_ref[...] = acc

    out = pl.pallas_call(
        body, name=name,
        grid_spec=pltpu.PrefetchScalarGridSpec(
            num_scalar_prefetch=1, grid=(rows // tr,),
            in_specs=[pl.BlockSpec((1, tr, cols), lambda i, k_ref: (k_ref[0], i, 0)),
                      pl.BlockSpec((N_CHIPS, tr, cols), lambda i, k_ref: (0, i, 0))],
            out_specs=pl.BlockSpec((tr, cols), lambda i, k_ref: (i, 0))),
        out_shape=jax.ShapeDtypeStruct((rows, cols), F32), compiler_params=_cp("parallel"),
    )(_scalar(chip), pv, bv)
    return out.reshape(own.shape[1:])


def _adam_math(w, g, m, v):
    nm = ADAM_B1 * m + (1.0 - ADAM_B1) * g
    nv = ADAM_B2 * v + (1.0 - ADAM_B2) * (g * g)
    m_hat = nm / (1.0 - ADAM_B1 ** ADAM_STEP)
    v_hat = nv / (1.0 - ADAM_B2 ** ADAM_STEP)
    return -ADAM_LR * (m_hat / (jnp.sqrt(v_hat) + ADAM_EPS) + ADAM_WD * w), nm, nv


def _adamw_halves(w, mine, theirs, m, v, c, name):
    shape = w.shape
    ws, ms, vs = (_as_rows(t.reshape((2, -1) + t.shape[-1:]), 1) for t in (w, m, v))
    a, b = _as_rows(mine, 0), _as_rows(theirs, 0)
    rows, cols = a.shape
    tr = _row_tile(rows, cols, 9)

    def body(c_ref, w_ref, a_ref, b_ref, m_ref, v_ref, g_ref, d_ref, nm_ref, nv_ref):
        gv = jnp.where(pl.program_id(0) == c_ref[0], a_ref[...], b_ref[...])
        g_ref[0] = gv
        d_ref[0], nm_ref[0], nv_ref[0] = _adam_math(w_ref[0], gv, m_ref[0], v_ref[0])

    half = pl.BlockSpec((1, tr, cols), lambda h, i, c_ref: (h, i, 0))
    flat = pl.BlockSpec((tr, cols), lambda h, i, c_ref: (i, 0))
    outs = pl.pallas_call(
        body, name=name,
        grid_spec=pltpu.PrefetchScalarGridSpec(num_scalar_prefetch=1, grid=(2, rows // tr),
                                               in_specs=[half, flat, flat, half, half], out_specs=[half] * 4),
        out_shape=[jax.ShapeDtypeStruct((2, rows, cols), F32)] * 4, compiler_params=_cp("parallel", "parallel"),
    )(_scalar(c), ws, a, b, ms, vs)
    return tuple(o.reshape(shape) for o in outs)


def _adamw(w, g, m, v, name):
    shape = w.shape
    ws, gs, ms, vs = (_as_rows(t, 0) for t in (w, g, m, v))
    rows, cols = ws.shape
    tr = _row_tile(rows, cols, 7)

    def body(w_ref, g_ref, m_ref, v_ref, d_ref, nm_ref, nv_ref):
        d_ref[...], nm_ref[...], nv_ref[...] = _adam_math(w_ref[...], g_ref[...], m_ref[...], v_ref[...])

    spec = pl.BlockSpec((tr, cols), lambda i: (i, 0))
    outs = pl.pallas_call(body, name=name, grid=(rows // tr,), in_specs=[spec] * 4, out_specs=[spec] * 3,
                          out_shape=[jax.ShapeDtypeStruct((rows, cols), F32)] * 3, compiler_params=_cp("parallel"))(ws, gs, ms, vs)
    return tuple(o.reshape(shape) for o in outs)


_BIG = ["ffn_w_gate", "ffn_w_up", "ffn_w_down", "dn_w_in", "dn_w_out", "sg_w_in", "sg_w_out"]
_SMALL_SHARDED = ["norm_g", "dn_conv_w", "sg_b_in", "sg_ln_g", "sg_ln_b"]
_SMALL_REPL = ["dn_a_log", "dn_dt_bias", "dn_norm_g", "sg_w_s", "sg_b_s"]
_WEIGHTS = ["norm_g", "ffn_w_gate", "ffn_w_up", "ffn_w_down", "dn_w_in", "dn_conv_w", "dn_a_log", "dn_dt_bias",
            "dn_norm_g", "dn_w_out", "sg_w_in", "sg_b_in", "sg_ln_g", "sg_ln_b", "sg_w_s", "sg_b_s", "sg_w_out"]
PACK_COLS = 1024


def _pack(arrs):
    flat = jnp.concatenate([a.reshape(-1) for a in arrs])
    pad = (-flat.shape[0]) % (SUBLANES * PACK_COLS)
    return jnp.pad(flat, (0, pad)).reshape(-1, PACK_COLS)


def _unpack(buf, shapes):
    flat = buf.reshape(-1)
    out, off = [], 0
    for s in shapes:
        n = math.prod(s)
        out.append(flat[off:off + n].reshape(s))
        off += n
    return out


def _as_halves(a):
    if a.shape[0] == 2:
        return a
    if a.shape[0] == 1:
        return a.reshape((2, a.shape[1] // 2) + a.shape[2:])
    return a.reshape((2, a.shape[0] // 2) + a.shape[1:])


def _with_own(gathered, own, chip):
    g = gathered.reshape((N_CHIPS,) + own.shape)
    return [jnp.where(chip == k, own, g[k]) for k in range(N_CHIPS)]


def _cat_shards(g, axis):
    return jnp.concatenate(list(g), axis=axis)


def _full_weights(w, gathered, small_full):
    gate, up, down, dn_in, dn_out, sg_in, sg_out = gathered
    sel = lambda g, *idx: [s[idx] for s in g]
    p = dict(small_full)
    p["wgu"] = [[jnp.concatenate([_cat_shards(sel(gate, i, j), 1), _cat_shards(sel(up, i, j), 1)], axis=1) for j in range(2)]
                for i in range(2)]
    p["wd"] = [[_cat_shards(sel(down, i, j), 0) for j in range(2)] for i in range(2)]
    dn_in, dn_out, sg_in, sg_out = (sel(g, 0) for g in (dn_in, dn_out, sg_in, sg_out))
    dn_full = _cat_shards(dn_in, 1)
    W4 = 4 * DN_HEADS * DN_HEAD_DIM
    p["dn_wqkvz"] = dn_full[:, :W4]
    wba = jnp.zeros((D_MODEL, 2 * LANES), dn_full.dtype)
    wba = wba.at[:, :DN_HEADS].set(dn_full[:, W4:W4 + DN_HEADS])
    wba = wba.at[:, LANES:LANES + DN_HEADS].set(dn_full[:, W4 + DN_HEADS:])
    p["dn_wba"] = wba
    p["dn_wout"] = _cat_shards(dn_out, 0)
    p["sg_win"] = _cat_shards(sg_in, 1)
    p["sg_wout"] = _cat_shards(sg_out, 0)
    for k in _SMALL_REPL:
        p[k] = w[k][0]
    return p


def _split_cols(a, n):
    w = a.shape[-1] // n
    return [a[..., k * w:(k + 1) * w] for k in range(n)]


def _split_rows(a, n):
    h = a.shape[-2] // n
    return [a[..., k * h:(k + 1) * h, :] for k in range(n)]


def _grads_by_half_and_chip(grads):
    F = D_FF
    gate = jnp.stack([jnp.stack([jnp.stack([_split_cols(grads["wgu%d%d" % (i, j)][:, :F], 4)[k] for j in range(2)])
                                 for k in range(4)]) for i in range(2)])
    up = jnp.stack([jnp.stack([jnp.stack([_split_cols(grads["wgu%d%d" % (i, j)][:, F:], 4)[k] for j in range(2)])
                               for k in range(4)]) for i in range(2)])
    down = jnp.stack([jnp.stack([jnp.stack([_split_rows(grads["wd%d%d" % (i, j)], 4)[k] for j in range(2)])
                                 for k in range(4)]) for i in range(2)])

    def col_sharded(a):
        return jnp.stack([jnp.stack(_split_cols(hf, 4)) for hf in _split_rows(a, 2)])

    def row_sharded(a):
        return jnp.stack([jnp.stack([_split_rows(s, 2)[hf] for s in _split_rows(a, 4)]) for hf in range(2)])

    return [gate, up, down, col_sharded(grads["dn_w_in"]), row_sharded(grads["dn_w_out"]),
            col_sharded(grads["sg_w_in"]), row_sharded(grads["sg_w_out"])]


def kernel(x, norm_g, ffn_w_gate, ffn_w_up, ffn_w_down, dn_w_in, dn_conv_w, dn_a_log, dn_dt_bias, dn_norm_g, dn_w_out, sg_w_in, sg_b_in, sg_ln_g, sg_ln_b, sg_w_s, sg_b_s, sg_w_out, loss_target, m_norm_g, m_ffn_w_gate, m_ffn_w_up, m_ffn_w_down, m_dn_w_in, m_dn_conv_w, m_dn_a_log, m_dn_dt_bias, m_dn_norm_g, m_dn_w_out, m_sg_w_in, m_sg_b_in, m_sg_ln_g, m_sg_ln_b, m_sg_w_s, m_sg_b_s, m_sg_w_out, v_norm_g, v_ffn_w_gate, v_ffn_w_up, v_ffn_w_down, v_dn_w_in, v_dn_conv_w, v_dn_a_log, v_dn_dt_bias, v_dn_norm_g, v_dn_w_out, v_sg_w_in, v_sg_b_in, v_sg_ln_g, v_sg_ln_b, v_sg_w_s, v_sg_b_s, v_sg_w_out):
    args = dict(locals())
    w = {k: args[k] for k in _WEIGHTS}
    mom = {k: args["m_" + k] for k in _WEIGHTS}
    var = {k: args["v_" + k] for k in _WEIGHTS}
    cx, cy, cc = _mesh_pos()
    chip = 2 * cx + cy

    small_shapes = [w[k].shape for k in _SMALL_SHARDED]
    own = [_mx(w[k]) for k in _BIG] + [_pack([w[k] for k in _SMALL_SHARDED])]
    gathered = _allgather_chips([_as_halves(a) for a in own], "gather_weights")
    gathered = [_with_own(g, a, chip) for g, a in zip(gathered, own)]
    small_k = [_unpack(gathered[-1][k], small_shapes) for k in range(N_CHIPS)]
    small_full = {name: jnp.concatenate([small_k[k][i] for k in range(N_CHIPS)], axis=-1)
                  for i, name in enumerate(_SMALL_SHARDED)}
    small_full = {k: (v if k == "norm_g" else v[0]) for k, v in small_full.items()}
    p = _full_weights(w, gathered[:-1], small_full)

    loss_part, grad_x, grads = _local_step(x[0], loss_target[0], p)

    halves = _grads_by_half_and_chip(grads)
    from_sibling = _swap_halves(halves, "reduce_core_pair")
    pair_sum = [_add_own_half(h, o, cc, "pair_sum_%d" % i) for i, (h, o) in enumerate(zip(halves, from_sibling))]
    from_chips = _scatter_chips(pair_sum, "reduce_chips")
    half_sum = [_sum_chips(own_sum, got, chip, "chip_sum_%d" % i) for i, (own_sum, got) in enumerate(zip(pair_sum, from_chips))]
    other_half = _swap_whole(half_sum, "gather_core_pair")

    small_names = _SMALL_SHARDED + _SMALL_REPL
    small_grads = [grads[k] for k in small_names]
    full_shapes = [g.shape for g in small_grads] + [(1,)]
    pack = _pack(small_grads + [loss_part[0, :1]])
    summed = _sum_leading(_allgather_devices(pack, "gather_small"), "small_sum")
    parts = _unpack(summed, full_shapes)
    loss = parts[-1][0]
    small_grad = {}
    for i, k in enumerate(small_names):
        g = parts[i]
        if k in _SMALL_SHARDED:
            n = w[k].shape[-1]
            g = lax.dynamic_slice_in_dim(g, chip * n, n, axis=g.ndim - 1)
        small_grad[k] = g

    grad = dict(small_grad)
    delta, new_m, new_v = {}, {}, {}
    for i, k in enumerate(_BIG):
        grad[k], delta[k], new_m[k], new_v[k] = _adamw_halves(w[k], half_sum[i], other_half[i], mom[k], var[k], cc, "adamw_" + k)
    shapes = [w[k].shape for k in small_names]
    d, nm, nv = _adamw(_pack([w[k] for k in small_names]), _pack([grad[k] for k in small_names]),
                       _pack([mom[k] for k in small_names]), _pack([var[k] for k in small_names]), "adamw_small")
    for k, a, b, c_ in zip(small_names, _unpack(d, shapes), _unpack(nm, shapes), _unpack(nv, shapes)):
        delta[k], new_m[k], new_v[k] = a, b, c_

    return (loss, grad_x[None], *[grad[k] for k in _WEIGHTS], *[delta[k] for k in _WEIGHTS],
            *[new_m[k] for k in _WEIGHTS], *[new_v[k] for k in _WEIGHTS])
```

```python
import functools
import math

import jax
import jax.numpy as jnp
from jax import lax
from jax.experimental import pallas as pl
from jax.experimental.pallas import tpu as pltpu

F32 = jnp.float32
MXU_DTYPE = jnp.bfloat16
COMM_DTYPE = jnp.bfloat16
HI = lax.Precision.HIGHEST
TRI_PREC = lax.Precision.HIGH

D_MODEL = 1024
D_FF = 2816
RMS_EPS = 1e-6
LN_EPS = 1e-5
L2_EPS = 1e-6
DN_HEADS = 8
DN_HEAD_DIM = 128
DN_CONV = 4
DN_CHUNK = 64
SG_WIDTH = 2048
SG_GROUPS = 8
SG_CHUNK = 128
SG_GROUP_W = SG_WIDTH // SG_GROUPS
N_CHIPS = 4
N_DEV = 8
LANES = 128
SUBLANES = 8
VMEM_LIMIT = 56 * 1024 * 1024

ADAM_LR = 0.001
ADAM_B1 = 0.9
ADAM_B2 = 0.999
ADAM_EPS = 1e-08
ADAM_WD = 0.01
ADAM_STEP = 10

MESH = pl.DeviceIdType.MESH
ANY = pl.BlockSpec(memory_space=pl.ANY)


def _cp(*sem):
    return pltpu.CompilerParams(dimension_semantics=sem, vmem_limit_bytes=VMEM_LIMIT)


def _pick(n, pref, mult=LANES):
    best = None
    d = mult
    while d <= min(n, pref):
        if n % d == 0:
            best = d
        d += mult
    return best if best is not None else n


def _full(shape):
    nd = len(shape)
    return pl.BlockSpec(shape, lambda *_: (0,) * nd)


def _sigmoid(x):
    return 1.0 / (1.0 + jnp.exp(-x))


def _dot(a, b, dims, prec=None):
    return lax.dot_general(a, b, (dims, ((), ())), preferred_element_type=F32, precision=prec)


NN = ((1,), (0,))
NT = ((1,), (1,))
TN = ((0,), (0,))


def _mx(a):
    return a.astype(MXU_DTYPE)


def _rms_stat(x):
    return lax.rsqrt(jnp.mean(x * x, axis=-1, keepdims=True) + RMS_EPS)


def _rms_bwd(x, r, g, dy):
    xh = x * r
    dxh = dy * g
    dx = r * (dxh - xh * jnp.mean(dxh * xh, axis=-1, keepdims=True))
    return dx, jnp.sum(dy * xh, axis=0, keepdims=True)


def _mm(a, b, mode, name, out_dtype=F32, add=None):
    if mode == "tn":
        K, M = a.shape
        N = b.shape[1]
    elif mode == "nt":
        M, K = a.shape
        N = b.shape[0]
    else:
        M, K = a.shape
        N = b.shape[1]
    tn = _pick(N, 1024)
    if mode == "tn":
        tm = _pick(M, 1024 if tn <= 512 else 1408)
        tk = _pick(K, 1024, SUBLANES)
    else:
        tm = _pick(M, max(512, min(2048, (512 * 1024) // tn)), SUBLANES)
        tk = _pick(K, 2048)
    nk = K // tk
    grid = (N // tn, M // tm, nk)
    if mode == "nn":
        a_spec = pl.BlockSpec((tm, tk), lambda j, i, k: (i, k))
        b_spec = pl.BlockSpec((tk, tn), lambda j, i, k: (k, j))
        dims = NN
    elif mode == "nt":
        a_spec = pl.BlockSpec((tm, tk), lambda j, i, k: (i, k))
        b_spec = pl.BlockSpec((tn, tk), lambda j, i, k: (j, k))
        dims = NT
    else:
        a_spec = pl.BlockSpec((tk, tm), lambda j, i, k: (k, i))
        b_spec = pl.BlockSpec((tk, tn), lambda j, i, k: (k, j))
        dims = TN
    o_spec = pl.BlockSpec((tm, tn), lambda j, i, k: (i, j))
    has_add = add is not None

    def body(*refs):
        if has_add:
            a_ref, b_ref, add_ref, o_ref, acc = refs
        else:
            a_ref, b_ref, o_ref, acc = refs
        k = pl.program_id(2)

        @pl.when(k == 0)
        def _():
            acc[...] = add_ref[...] if has_add else jnp.zeros_like(acc)

        acc[...] += _dot(a_ref[...], b_ref[...], dims)

        @pl.when(k == nk - 1)
        def _():
            o_ref[...] = acc[...].astype(o_ref.dtype)

    ins = [a, b] + ([add] if has_add else [])
    specs = [a_spec, b_spec] + ([o_spec] if has_add else [])
    return pl.pallas_call(
        body, name=name, grid=grid, in_specs=specs, out_specs=o_spec,
        out_shape=jax.ShapeDtypeStruct((M, N), out_dtype),
        scratch_shapes=[pltpu.VMEM((tm, tn), F32)],
        compiler_params=_cp("parallel", "parallel", "arbitrary"),
    )(*ins)


def _load_resident(pairs, sem):
    @pl.when(pl.program_id(0) == 0)
    def _():
        cps = [pltpu.make_async_copy(src, dst, sem.at[i]) for i, (src, dst) in enumerate(pairs)]
        for c in cps:
            c.start()
        for c in cps:
            c.wait()


def _ffn_fwd(x, g0, g1, wgu, wd, name):
    T, D = x.shape
    F2 = wgu.shape[1]
    F = F2 // 2
    tm = _pick(T, 256, SUBLANES)

    def body(x_ref, g0_ref, g1_ref, wgu_hbm, wd_hbm, xo_ref, h_ref, gu_ref, y_ref, wgu_v, wd_v, sem):
        _load_resident([(wgu_hbm, wgu_v), (wd_hbm, wd_v)], sem)
        xv = x_ref[...]
        hb = _mx(xv * _rms_stat(xv) * g0_ref[...])
        h_ref[...] = hb
        gu = _dot(hb, wgu_v[...], NN)
        gu_ref[...] = gu
        g = gu[:, :F]
        u = gu[:, F:]
        a = _mx(g * _sigmoid(g) * u)
        y = _dot(a, wd_v[...], NN)
        y_ref[...] = y
        xo_ref[...] = xv + 0.5 * (y * _rms_stat(y) * g1_ref[...])

    row = lambda w: pl.BlockSpec((tm, w), lambda i: (i, 0))
    return pl.pallas_call(
        body, name=name, grid=(T // tm,),
        in_specs=[row(D), _full((1, D)), _full((1, D)), ANY, ANY],
        out_specs=[row(D), row(D), row(F2), row(D)],
        out_shape=[jax.ShapeDtypeStruct((T, D), F32), jax.ShapeDtypeStruct((T, D), MXU_DTYPE),
                   jax.ShapeDtypeStruct((T, F2), F32), jax.ShapeDtypeStruct((T, D), F32)],
        scratch_shapes=[pltpu.VMEM(wgu.shape, wgu.dtype), pltpu.VMEM(wd.shape, wd.dtype),
                        pltpu.SemaphoreType.DMA((2,))],
        compiler_params=_cp("arbitrary"),
    )(x, g0, g1, wgu, wd)


def _ffn_bwd_down(dxo, y, gu, g1, wd, name):
    T, D = y.shape
    F2 = gu.shape[1]
    F = F2 // 2
    tm = _pick(T, 256, SUBLANES)

    def body(dxo_ref, y_ref, gu_ref, g1_ref, wd_hbm, dy_ref, a_ref, dgu_ref, dg1_ref, wd_v, sem):
        _load_resident([(wd_hbm, wd_v)], sem)

        @pl.when(pl.program_id(0) == 0)
        def _():
            dg1_ref[...] = jnp.zeros_like(dg1_ref)

        yv = y_ref[...]
        dy, dg1 = _rms_bwd(yv, _rms_stat(yv), g1_ref[...], 0.5 * dxo_ref[...])
        dg1_ref[...] += dg1
        dyb = _mx(dy)
        dy_ref[...] = dyb
        da = _dot(dyb, wd_v[...], NT)
        gu_v = gu_ref[...]
        g = gu_v[:, :F]
        u = gu_v[:, F:]
        s = _sigmoid(g)
        sg = g * s
        a_ref[...] = _mx(sg * u)
        dgu_ref[:, :F] = _mx(da * u * (s * (1.0 + g * (1.0 - s))))
        dgu_ref[:, F:] = _mx(da * sg)

    row = lambda w: pl.BlockSpec((tm, w), lambda i: (i, 0))
    return pl.pallas_call(
        body, name=name, grid=(T // tm,),
        in_specs=[row(D), row(D), row(F2), _full((1, D)), ANY],
        out_specs=[row(D), row(F), row(F2), _full((1, D))],
        out_shape=[jax.ShapeDtypeStruct((T, D), MXU_DTYPE), jax.ShapeDtypeStruct((T, F), MXU_DTYPE),
                   jax.ShapeDtypeStruct((T, F2), MXU_DTYPE), jax.ShapeDtypeStruct((1, D), F32)],
        scratch_shapes=[pltpu.VMEM(wd.shape, wd.dtype), pltpu.SemaphoreType.DMA((1,))],
        compiler_params=_cp("arbitrary"),
    )(dxo, y, gu, g1, wd)


def _ffn_bwd_up(dgu, x, dxo, g0, wgu, name):
    T, D = x.shape
    F2 = dgu.shape[1]
    tm = _pick(T, 256, SUBLANES)

    def body(dgu_ref, x_ref, dxo_ref, g0_ref, wgu_hbm, dx_ref, dg0_ref, wgu_v, sem):
        _load_resident([(wgu_hbm, wgu_v)], sem)

        @pl.when(pl.program_id(0) == 0)
        def _():
            dg0_ref[...] = jnp.zeros_like(dg0_ref)

        dh = _dot(dgu_ref[...], wgu_v[...], NT)
        xv = x_ref[...]
        dx, dg0 = _rms_bwd(xv, _rms_stat(xv), g0_ref[...], dh)
        dg0_ref[...] += dg0
        dx_ref[...] = dxo_ref[...] + dx

    row = lambda w: pl.BlockSpec((tm, w), lambda i: (i, 0))
    return pl.pallas_call(
        body, name=name, grid=(T // tm,),
        in_specs=[row(F2), row(D), row(D), _full((1, D)), ANY],
        out_specs=[row(D), _full((1, D))],
        out_shape=[jax.ShapeDtypeStruct((T, D), F32), jax.ShapeDtypeStruct((1, D), F32)],
        scratch_shapes=[pltpu.VMEM(wgu.shape, wgu.dtype), pltpu.SemaphoreType.DMA((1,))],
        compiler_params=_cp("arbitrary"),
    )(dgu, x, dxo, g0, wgu)


def _norm_fwd(x, g, name):
    T, D = x.shape
    tm = _pick(T, 512, SUBLANES)

    def body(x_ref, g_ref, h_ref):
        xv = x_ref[...]
        h_ref[...] = _mx(xv * _rms_stat(xv) * g_ref[...])

    row = pl.BlockSpec((tm, D), lambda i: (i, 0))
    return pl.pallas_call(body, name=name, grid=(T // tm,), in_specs=[row, _full((1, D))], out_specs=row,
                          out_shape=jax.ShapeDtypeStruct((T, D), MXU_DTYPE), compiler_params=_cp("parallel"))(x, g)


def _postnorm_fwd(x, m, g, name):
    T, D = x.shape
    tm = _pick(T, 512, SUBLANES)

    def body(x_ref, m_ref, g_ref, o_ref):
        mv = m_ref[...]
        o_ref[...] = x_ref[...] + mv * _rms_stat(mv) * g_ref[...]

    row = pl.BlockSpec((tm, D), lambda i: (i, 0))
    return pl.pallas_call(body, name=name, grid=(T // tm,), in_specs=[row, row, _full((1, D))], out_specs=row,
                          out_shape=jax.ShapeDtypeStruct((T, D), F32), compiler_params=_cp("parallel"))(x, m, g)


def _postnorm_bwd(dxo, m, g, name):
    T, D = m.shape
    tm = _pick(T, 512, SUBLANES)

    def body(dxo_ref, m_ref, g_ref, dm_ref, dg_ref):
        @pl.when(pl.program_id(0) == 0)
        def _():
            dg_ref[...] = jnp.zeros_like(dg_ref)

        mv = m_ref[...]
        dm, dg = _rms_bwd(mv, _rms_stat(mv), g_ref[...], dxo_ref[...])
        dg_ref[...] += dg
        dm_ref[...] = _mx(dm)

    row = pl.BlockSpec((tm, D), lambda i: (i, 0))
    return pl.pallas_call(body, name=name, grid=(T // tm,), in_specs=[row, row, _full((1, D))],
                          out_specs=[row, _full((1, D))],
                          out_shape=[jax.ShapeDtypeStruct((T, D), MXU_DTYPE), jax.ShapeDtypeStruct((1, D), F32)],
                          compiler_params=_cp("arbitrary"))(dxo, m, g)


def _prenorm_bwd(dxo, dh, x, g, name):
    T, D = x.shape
    tm = _pick(T, 512, SUBLANES)

    def body(dxo_ref, dh_ref, x_ref, g_ref, dx_ref, dg_ref):
        @pl.when(pl.program_id(0) == 0)
        def _():
            dg_ref[...] = jnp.zeros_like(dg_ref)

        xv = x_ref[...]
        dx, dg = _rms_bwd(xv, _rms_stat(xv), g_ref[...], dh_ref[...])
        dg_ref[...] += dg
        dx_ref[...] = dxo_ref[...] + dx

    row = pl.BlockSpec((tm, D), lambda i: (i, 0))
    return pl.pallas_call(body, name=name, grid=(T // tm,), in_specs=[row, row, row, _full((1, D))],
                          out_specs=[row, _full((1, D))],
                          out_shape=[jax.ShapeDtypeStruct((T, D), F32), jax.ShapeDtypeStruct((1, D), F32)],
                          compiler_params=_cp("arbitrary"))(dxo, dh, x, g)


def _loss_fwd_bwd(y, target, name):
    T, D = y.shape
    tm = _pick(T, 512, SUBLANES)

    def body(y_ref, t_ref, l_ref, dy_ref):
        @pl.when(pl.program_id(0) == 0)
        def _():
            l_ref[...] = jnp.zeros_like(l_ref)

        e = y_ref[...] - t_ref[...]
        dy_ref[...] = e * (1.0 / D)
        l_ref[...] += 0.5 * jnp.sum(jnp.mean(e * e, axis=-1, keepdims=True), axis=0, keepdims=True)

    row = pl.BlockSpec((tm, D), lambda i: (i, 0))
    return pl.pallas_call(body, name=name, grid=(T // tm,), in_specs=[row, row],
                          out_specs=[_full((SUBLANES, LANES)), row],
                          out_shape=[jax.ShapeDtypeStruct((SUBLANES, LANES), F32), jax.ShapeDtypeStruct((T, D), F32)],
                          compiler_params=_cp("arbitrary"))(y, target)


DN_ROWS = 512


def _shift_down(prev8, cur, s):
    n = cur.shape[0]
    xx = jnp.concatenate([prev8, cur], axis=0)
    return pltpu.roll(xx, s, 0)[SUBLANES:SUBLANES + n, :]


def _shift_up(cur, next8, s):
    n = cur.shape[0]
    xx = jnp.concatenate([cur, next8], axis=0)
    return pltpu.roll(xx, n + SUBLANES - s, 0)[:n, :]


def _conv_tile(x_ref, w, r, rows):
    start = pl.multiple_of(r * rows, SUBLANES)
    cur = x_ref[pl.ds(start, rows), :]
    pstart = pl.multiple_of(jnp.maximum(start - SUBLANES, 0), SUBLANES)
    prev8 = jnp.where(r == 0, 0.0, x_ref[pl.ds(pstart, SUBLANES), :])
    taps = [_shift_down(prev8, cur, DN_CONV - 1 - j) if j < DN_CONV - 1 else cur for j in range(DN_CONV)]
    c = taps[0] * w[0:1, :]
    for j in range(1, DN_CONV):
        c = c + taps[j] * w[j:j + 1, :]
    return c, taps


def _dn_prep_fwd(proj, conv_w, name):
    T = proj.shape[0]
    W = DN_HEADS * DN_HEAD_DIM
    rows = min(DN_ROWS, T)
    n_inner = T // rows
    scale = DN_HEAD_DIM ** -0.5

    def body(x_ref, w_ref, o_ref):
        cb = pl.program_id(0)
        w = w_ref[...]
        is_qk = cb < 2 * DN_HEADS
        post = jnp.where(cb < DN_HEADS, scale, 1.0)

        def step(r, carry):
            c, _ = _conv_tile(x_ref, w, r, rows)
            s = c * _sigmoid(c)
            rinv = lax.rsqrt(jnp.sum(s * s, axis=-1, keepdims=True) + L2_EPS)
            o_ref[pl.ds(pl.multiple_of(r * rows, SUBLANES), rows), :] = jnp.where(is_qk, s * rinv * post, s)
            return carry

        lax.fori_loop(0, n_inner, step, 0)

    col = pl.BlockSpec((T, LANES), lambda j: (0, j))
    return pl.pallas_call(body, name=name, grid=(3 * W // LANES,),
                          in_specs=[col, pl.BlockSpec((DN_CONV, LANES), lambda j: (0, j))], out_specs=col,
                          out_shape=jax.ShapeDtypeStruct((T, 3 * W), F32), compiler_params=_cp("parallel"))(proj, conv_w)


def _dn_prep_bwd(proj, conv_w, dqkv, name):
    T = proj.shape[0]
    W = DN_HEADS * DN_HEAD_DIM
    rows = min(DN_ROWS, T)
    n_inner = T // rows
    scale = DN_HEAD_DIM ** -0.5

    def body(x_ref, w_ref, dy_ref, dx_ref, dw_ref, dc_scr):
        cb = pl.program_id(0)
        w = w_ref[...]
        is_qk = cb < 2 * DN_HEADS
        post = jnp.where(cb < DN_HEADS, scale, 1.0)

        def step1(r, dws):
            c, taps = _conv_tile(x_ref, w, r, rows)
            sg = _sigmoid(c)
            s = c * sg
            rinv = lax.rsqrt(jnp.sum(s * s, axis=-1, keepdims=True) + L2_EPS)
            dy = dy_ref[pl.ds(pl.multiple_of(r * rows, SUBLANES), rows), :]
            yn = s * rinv
            dyn = dy * post
            ds_qk = rinv * (dyn - yn * jnp.sum(dyn * yn, axis=-1, keepdims=True))
            ds = jnp.where(is_qk, ds_qk, dy)
            dc = ds * (sg * (1.0 + c * (1.0 - sg)))
            dc_scr[pl.ds(pl.multiple_of(r * rows, SUBLANES), rows), :] = dc
            return tuple(dws[j] + jnp.sum(dc * taps[j], axis=0, keepdims=True) for j in range(DN_CONV))

        zero = jnp.zeros((1, LANES), F32)
        dws = lax.fori_loop(0, n_inner, step1, (zero,) * DN_CONV)
        for j in range(DN_CONV):
            dw_ref[j:j + 1, :] = dws[j]

        def step2(r, carry):
            start = pl.multiple_of(r * rows, SUBLANES)
            cur = dc_scr[pl.ds(start, rows), :]
            nstart = pl.multiple_of(jnp.minimum(start + rows, T - SUBLANES), SUBLANES)
            next8 = jnp.where(r == n_inner - 1, 0.0, dc_scr[pl.ds(nstart, SUBLANES), :])
            dx = cur * w[DN_CONV - 1:DN_CONV, :]
            for j in range(DN_CONV - 1):
                dx = dx + _shift_up(cur, next8, DN_CONV - 1 - j) * w[j:j + 1, :]
            dx_ref[pl.ds(start, rows), :] = _mx(dx)
            return carry

        lax.fori_loop(0, n_inner, step2, 0)

    col = pl.BlockSpec((T, LANES), lambda j: (0, j))
    wspec = pl.BlockSpec((DN_CONV, LANES), lambda j: (0, j))
    return pl.pallas_call(body, name=name, grid=(3 * W // LANES,), in_specs=[col, wspec, col], out_specs=[col, wspec],
                          out_shape=[jax.ShapeDtypeStruct((T, 3 * W), MXU_DTYPE), jax.ShapeDtypeStruct((DN_CONV, 3 * W), F32)],
                          scratch_shapes=[pltpu.VMEM((T, LANES), F32)], compiler_params=_cp("parallel"))(proj, conv_w, dqkv)


def _softplus(x):
    return jnp.maximum(x, 0.0) + jnp.log(1.0 + jnp.exp(-jnp.abs(x)))


def _dn_gate_fwd(ba, a_log, dt_bias, name):
    T = ba.shape[0]
    tm = _pick(T, 1024, SUBLANES)

    def body(ba_ref, al_ref, dt_ref, beta_ref, g_ref):
        beta_ref[...] = _sigmoid(ba_ref[:, :LANES])
        g_ref[...] = -jnp.exp(al_ref[...]) * _softplus(ba_ref[:, LANES:] + dt_ref[...])

    row = lambda w: pl.BlockSpec((tm, w), lambda i: (i, 0))
    return pl.pallas_call(body, name=name, grid=(T // tm,), in_specs=[row(2 * LANES), _full((1, LANES)), _full((1, LANES))],
                          out_specs=[row(LANES), row(LANES)],
                          out_shape=[jax.ShapeDtypeStruct((T, LANES), F32)] * 2, compiler_params=_cp("parallel"))(ba, a_log, dt_bias)


def _dn_gate_bwd(ba, a_log, dt_bias, dbeta, dg, name):
    T = ba.shape[0]
    tm = _pick(T, 1024, SUBLANES)

    def body(ba_ref, al_ref, dt_ref, dbeta_ref, dg_ref, dba_ref, dal_ref, ddt_ref):
        @pl.when(pl.program_id(0) == 0)
        def _():
            dal_ref[...] = jnp.zeros_like(dal_ref)
            ddt_ref[...] = jnp.zeros_like(ddt_ref)

        beta = _sigmoid(ba_ref[:, :LANES])
        dba_ref[:, :LANES] = _mx(dbeta_ref[...] * beta * (1.0 - beta))
        pre = ba_ref[:, LANES:] + dt_ref[...]
        ea = jnp.exp(al_ref[...])
        dgv = dg_ref[...]
        da = dgv * (-ea) * _sigmoid(pre)
        dba_ref[:, LANES:] = _mx(da)
        ddt_ref[...] += jnp.sum(da, axis=0, keepdims=True)
        dal_ref[...] += jnp.sum(dgv * (-ea) * _softplus(pre), axis=0, keepdims=True)

    row = lambda w: pl.BlockSpec((tm, w), lambda i: (i, 0))
    one = _full((1, LANES))
    return pl.pallas_call(body, name=name, grid=(T // tm,), in_specs=[row(2 * LANES), one, one, row(LANES), row(LANES)],
                          out_specs=[row(2 * LANES), one, one],
                          out_shape=[jax.ShapeDtypeStruct((T, 2 * LANES), MXU_DTYPE), jax.ShapeDtypeStruct((1, LANES), F32),
                                     jax.ShapeDtypeStruct((1, LANES), F32)],
                          compiler_params=_cp("arbitrary"))(ba, a_log, dt_bias, dbeta, dg)


def _tri(c, strict):
    i = lax.broadcasted_iota(jnp.int32, (c, c), 0)
    j = lax.broadcasted_iota(jnp.int32, (c, c), 1)
    return (i > j) if strict else (i >= j)


def _inv_unit_lower(ls):
    c = ls[0].shape[0]
    i = lax.broadcasted_iota(jnp.int32, (c, c), 0)
    j = lax.broadcasted_iota(jnp.int32, (c, c), 1)
    eye = jnp.where(i == j, 1.0, 0.0)
    facs = [[eye - l for l in ls]]
    cur = ls
    for _ in range(int(math.log2(c)) - 1):
        cur = [_dot(p, p, NN, TRI_PREC) for p in cur]
        facs.append([eye + p for p in cur])
    while len(facs) > 1:
        nxt = [[_dot(a, b, NN, TRI_PREC) for a, b in zip(facs[t], facs[t + 1])] for t in range(0, len(facs) - 1, 2)]
        if len(facs) % 2:
            nxt.append(facs[-1])
        facs = nxt
    return facs[0]


def _chunk_gates(g_blk):
    c = g_blk.shape[0]
    gcs = _dot(jnp.where(_tri(c, False), 1.0, 0.0), g_blk, NN, HI)
    return gcs, gcs.T


def _head_chunk(h, qh, kh, vh, beta_blk, gcs, gcs_t):
    c = qh.shape[0]
    incl = _tri(c, False)
    gc_col = gcs[:, h:h + 1]
    gc_row = gcs_t[h:h + 1, :]
    gc_last = gcs_t[h:h + 1, c - 1:c]
    dec = jnp.where(incl, jnp.exp(jnp.where(incl, gc_col - gc_row, 0.0)), 0.0)
    gam = jnp.exp(gc_col)
    rr = jnp.exp(gc_last - gc_col)
    gl = jnp.exp(gc_last)
    b = beta_blk[:, h:h + 1]
    kb = kh * b
    vb = vh * b
    kk = _dot(_mx(kb), _mx(kh), NT)
    lmat = jnp.where(_tri(c, True), kk * dec, 0.0)
    qk = _dot(_mx(qh), _mx(kh), NT)
    pmat = jnp.where(incl, qk * dec, 0.0)
    return dict(dec=dec, gam=gam, rr=rr, gl=gl, b=b, kb=kb, vb=vb, lmat=lmat, pmat=pmat)


def _dn_scan_fwd(qkv, beta, g, proj, norm_g, name):
    T = qkv.shape[0]
    C, H, Dh = DN_CHUNK, DN_HEADS, DN_HEAD_DIM
    W = H * Dh
    N = T // C

    def body(q_ref, k_ref, v_ref, beta_ref, g_ref, z_ref, ng_ref, og_ref, o_ref, tinv_ref, s_ref, state):
        @pl.when(pl.program_id(0) == 0)
        def _():
            state[...] = jnp.zeros_like(state)

        gcs, gcs_t = _chunk_gates(g_ref[...])
        beta_blk = beta_ref[...]
        ng = ng_ref[...]
        heads = range(H)
        cs = [slice(h * Dh, (h + 1) * Dh) for h in heads]
        qs = [_head_chunk(h, q_ref[:, cs[h]], k_ref[:, cs[h]], v_ref[:, cs[h]], beta_blk, gcs, gcs_t) for h in heads]
        tinvs = _inv_unit_lower([q["lmat"] for q in qs])
        for h in heads:
            tinv_ref[h] = tinvs[h]
        us = [_dot(tinvs[h], qs[h]["vb"], NN, TRI_PREC) for h in heads]
        ws = [_dot(tinvs[h], qs[h]["kb"] * qs[h]["gam"], NN, TRI_PREC) for h in heads]
        ss = [state[h] for h in heads]
        for h in heads:
            s_ref[0, h] = ss[h]
        sbs = [_mx(s) for s in ss]
        vnbs = [_mx(us[h] - _dot(_mx(ws[h]), sbs[h], NN)) for h in heads]
        os_ = [_dot(_mx(q_ref[:, cs[h]] * qs[h]["gam"]), sbs[h], NN) + _dot(_mx(qs[h]["pmat"]), vnbs[h], NN) for h in heads]
        for h in heads:
            state[h] = ss[h] * qs[h]["gl"] + _dot(_mx((k_ref[:, cs[h]] * qs[h]["rr"]).T), vnbs[h], NN)
        for h in heads:
            o = os_[h]
            o_ref[:, cs[h]] = o
            zh = z_ref[:, cs[h]]
            og_ref[:, cs[h]] = _mx(o * _rms_stat(o) * ng * (zh * _sigmoid(zh)))

    blk = lambda j: pl.BlockSpec((C, W), lambda n: (n, j))
    small = pl.BlockSpec((C, LANES), lambda n: (n, 0))
    return pl.pallas_call(
        body, name=name, grid=(N,),
        in_specs=[blk(0), blk(1), blk(2), small, small, blk(3), _full((1, Dh))],
        out_specs=[blk(0), blk(0), pl.BlockSpec((H, C, C), lambda n: (0, n, 0)),
                   pl.BlockSpec((1, H, Dh, Dh), lambda n: (n, 0, 0, 0))],
        out_shape=[jax.ShapeDtypeStruct((T, W), MXU_DTYPE), jax.ShapeDtypeStruct((T, W), F32),
                   jax.ShapeDtypeStruct((H, T, C), F32), jax.ShapeDtypeStruct((N, H, Dh, Dh), F32)],
        scratch_shapes=[pltpu.VMEM((H, Dh, Dh), F32)],
        compiler_params=_cp("arbitrary"),
    )(qkv, qkv, qkv, beta, g, proj, norm_g)


def _dn_scan_bwd(qkv, beta, g, proj, norm_g, o, tinv, s_all, dog, name):
    T = qkv.shape[0]
    C, H, Dh = DN_CHUNK, DN_HEADS, DN_HEAD_DIM
    W = H * Dh
    N = T // C

    def body(q_ref, k_ref, v_ref, beta_ref, g_ref, z_ref, ng_ref, o_ref, tinv_ref, s_ref, dog_ref,
             dqkv_ref, dbeta_ref, dg_ref, dz_ref, dng_ref, dstate):
        @pl.when(pl.program_id(0) == 0)
        def _():
            dstate[...] = jnp.zeros_like(dstate)
            dng_ref[...] = jnp.zeros_like(dng_ref)

        gcs, gcs_t = _chunk_gates(g_ref[...])
        beta_blk = beta_ref[...]
        ng = ng_ref[...]
        incl = _tri(C, False)
        strict = _tri(C, True)
        lane = lax.broadcasted_iota(jnp.int32, (C, LANES), 1)
        rowi = lax.broadcasted_iota(jnp.int32, (C, 1), 0)
        ones = jnp.ones((C, LANES), F32)
        dbeta_acc = jnp.zeros((C, LANES), F32)
        dgc_acc = jnp.zeros((C, LANES), F32)
        dng_acc = jnp.zeros((1, Dh), F32)
        heads = range(H)
        cs = [slice(h * Dh, (h + 1) * Dh) for h in heads]
        rsum = lambda t: jnp.sum(t, axis=1, keepdims=True)
        dobs = []
        for h in heads:
            oh, zh, dogh = o_ref[:, cs[h]], z_ref[:, cs[h]], dog_ref[:, cs[h]]
            rstat = _rms_stat(oh)
            sz = _sigmoid(zh)
            dz_ref[:, cs[h]] = _mx(dogh * (oh * rstat * ng) * (sz * (1.0 + zh * (1.0 - sz))))
            do, dng = _rms_bwd(oh, rstat, ng, dogh * (zh * sz))
            dng_acc = dng_acc + dng
            dobs.append(_mx(do))
        qs = [_head_chunk(h, q_ref[:, cs[h]], k_ref[:, cs[h]], v_ref[:, cs[h]], beta_blk, gcs, gcs_t) for h in heads]
        tms = [tinv_ref[h] for h in heads]
        us = [_dot(tms[h], qs[h]["vb"], NN, TRI_PREC) for h in heads]
        ws = [_dot(tms[h], qs[h]["kb"] * qs[h]["gam"], NN, TRI_PREC) for h in heads]
        ss = [s_ref[0, h] for h in heads]
        sbs = [_mx(s) for s in ss]
        wbs = [_mx(w) for w in ws]
        vnbs = [_mx(us[h] - _dot(wbs[h], sbs[h], NN)) for h in heads]
        dsns = [dstate[h] for h in heads]
        dsbs = [_mx(d) for d in dsns]
        dvnews = [_dot(_mx(qs[h]["pmat"]), dobs[h], TN) + _dot(_mx(k_ref[:, cs[h]] * qs[h]["rr"]), dsbs[h], NN) for h in heads]
        dvb16s = [_mx(d) for d in dvnews]
        dps = [jnp.where(incl, _dot(dobs[h], vnbs[h], NT), 0.0) for h in heads]
        dqds = [_dot(dobs[h], sbs[h], NT) for h in heads]
        dkds = [_dot(vnbs[h], dsbs[h], NT) for h in heads]
        dgls = [jnp.sum(rsum(ss[h] * dsns[h]), axis=0, keepdims=True) for h in heads]
        dws = [-_dot(dvb16s[h], sbs[h], NT) for h in heads]
        for h in heads:
            dstate[h] = (_dot(_mx(q_ref[:, cs[h]] * qs[h]["gam"]), dobs[h], TN) + qs[h]["gl"] * dsns[h]
                         - _dot(wbs[h], dvb16s[h], TN))
        dvbs = [_dot(tms[h], dvnews[h], TN, TRI_PREC) for h in heads]
        dkbgs = [_dot(tms[h], dws[h], TN, TRI_PREC) for h in heads]
        dls = [jnp.where(strict, -(_dot(dvbs[h], us[h], NT, TRI_PREC) + _dot(dkbgs[h], ws[h], NT, TRI_PREC)), 0.0)
               for h in heads]
        mmats = [dls[h] * qs[h]["lmat"] + dps[h] * qs[h]["pmat"] for h in heads]
        dgcs = [rsum(mmats[h]) - _dot(mmats[h], ones, TN, HI)[:, :1] for h in heads]
        dkk16s = [_mx(dls[h] * qs[h]["dec"]) for h in heads]
        dqk16s = [_mx(dps[h] * qs[h]["dec"]) for h in heads]
        for h in heads:
            q = qs[h]
            qh, kh, vh = q_ref[:, cs[h]], k_ref[:, cs[h]], v_ref[:, cs[h]]
            gam, rr, b, kb = q["gam"], q["rr"], q["b"], q["kb"]
            dkb = _dot(dkk16s[h], _mx(kh), NN) + dkbgs[h] * gam
            dk = _dot(dkk16s[h], _mx(kb), TN) + _dot(dqk16s[h], _mx(qh), TN) + dkb * b + dkds[h] * rr
            dq = _dot(dqk16s[h], _mx(kh), NN) + dqds[h] * gam
            dgam = rsum(dkbgs[h] * kb) + rsum(dqds[h] * qh)
            dr = rsum(dkds[h] * kh)
            dgc_last = jnp.sum(dr * rr, axis=0, keepdims=True) + dgls[h] * q["gl"]
            dgc = dgcs[h] + dgam * gam - dr * rr + jnp.where(rowi == C - 1, dgc_last, 0.0)
            dbeta = rsum(dvbs[h] * vh) + rsum(dkb * kh)
            dqkv_ref[:, cs[h]] = dq
            dqkv_ref[:, W + h * Dh:W + (h + 1) * Dh] = dk
            dqkv_ref[:, 2 * W + h * Dh:2 * W + (h + 1) * Dh] = dvbs[h] * b
            dbeta_acc = jnp.where(lane == h, dbeta, dbeta_acc)
            dgc_acc = jnp.where(lane == h, dgc, dgc_acc)
        dbeta_ref[...] = dbeta_acc
        dg_ref[...] = _dot(jnp.where(incl, 1.0, 0.0), dgc_acc, TN, HI)
        dng_ref[...] += dng_acc

    rev = lambda n: N - 1 - n
    blk = lambda j: pl.BlockSpec((C, W), lambda n: (rev(n), j))
    small = pl.BlockSpec((C, LANES), lambda n: (rev(n), 0))
    return pl.pallas_call(
        body, name=name, grid=(N,),
        in_specs=[blk(0), blk(1), blk(2), small, small, blk(3), _full((1, Dh)), blk(0),
                  pl.BlockSpec((H, C, C), lambda n: (0, rev(n), 0)),
                  pl.BlockSpec((1, H, Dh, Dh), lambda n: (rev(n), 0, 0, 0)), blk(0)],
        out_specs=[pl.BlockSpec((C, 3 * W), lambda n: (rev(n), 0)), small, small, blk(0), _full((1, Dh))],
        out_shape=[jax.ShapeDtypeStruct((T, 3 * W), F32), jax.ShapeDtypeStruct((T, LANES), F32),
                   jax.ShapeDtypeStruct((T, LANES), F32), jax.ShapeDtypeStruct((T, W), MXU_DTYPE),
                   jax.ShapeDtypeStruct((1, Dh), F32)],
        scratch_shapes=[pltpu.VMEM((H, Dh, Dh), F32)],
        compiler_params=_cp("arbitrary"),
    )(qkv, qkv, qkv, beta, g, proj, norm_g, o, tinv, s_all, dog)


_INV_SQRT2 = 0.7071067811865476
_INV_SQRT_2PI = 0.3989422804014327


def _sg_recompute(zp_ref, bin_ref, lng_ref, lnb_ref):
    E = SG_WIDTH
    zin = zp_ref[...] + bin_ref[...]
    cdf = 0.5 * (1.0 + lax.erf(zin * _INV_SQRT2))
    zz = zin * cdf
    u = zz[:, :E]
    vp = zz[:, E:]
    mu = jnp.mean(vp, axis=-1, keepdims=True)
    xc = vp - mu
    rstd = lax.rsqrt(jnp.mean(xc * xc, axis=-1, keepdims=True) + LN_EPS)
    xhat = xc * rstd
    v = xhat * lng_ref[...] + lnb_ref[...]
    return zin, cdf, u, xhat, rstd, v


def _sg_masked_ws(ws_ref, g):
    return _mx(jnp.where(_tri(SG_CHUNK, False), ws_ref[g], 0.0))


def _sg_fwd(zpre, b_in, ln_g, ln_b, w_s, b_s_t, name):
    T = zpre.shape[0]
    E, G, C, GW = SG_WIDTH, SG_GROUPS, SG_CHUNK, SG_GROUP_W

    def body(zp_ref, bin_ref, lng_ref, lnb_ref, ws_ref, bst_ref, um_ref):
        _, _, u, _, _, v = _sg_recompute(zp_ref, bin_ref, lng_ref, lnb_ref)
        bst = bst_ref[...]
        for g in range(G):
            cs = slice(g * GW, (g + 1) * GW)
            mixed = _dot(_sg_masked_ws(ws_ref, g), _mx(v[:, cs]), NN) + bst[:, g:g + 1]
            um_ref[:, cs] = _mx(u[:, cs] * mixed)

    return pl.pallas_call(
        body, name=name, grid=(T // C,),
        in_specs=[pl.BlockSpec((C, 2 * E), lambda n: (n, 0)), _full((1, 2 * E)), _full((1, E)), _full((1, E)),
                  _full((G, C, C)), _full((C, LANES))],
        out_specs=pl.BlockSpec((C, E), lambda n: (n, 0)),
        out_shape=jax.ShapeDtypeStruct((T, E), MXU_DTYPE), compiler_params=_cp("parallel"),
    )(zpre, b_in, ln_g, ln_b, w_s, b_s_t)


def _sg_bwd(zpre, b_in, ln_g, ln_b, w_s, b_s_t, dum, name):
    T = zpre.shape[0]
    E, G, C, GW = SG_WIDTH, SG_GROUPS, SG_CHUNK, SG_GROUP_W

    def body(zp_ref, bin_ref, lng_ref, lnb_ref, ws_ref, bst_ref, dum_ref,
             dz_ref, dbin_ref, dlng_ref, dlnb_ref, dws_ref, dbst_ref):
        @pl.when(pl.program_id(0) == 0)
        def _():
            for r in (dbin_ref, dlng_ref, dlnb_ref, dws_ref, dbst_ref):
                r[...] = jnp.zeros_like(r)

        zin, cdf, u, xhat, rstd, v = _sg_recompute(zp_ref, bin_ref, lng_ref, lnb_ref)
        bst = bst_ref[...]
        lane = lax.broadcasted_iota(jnp.int32, (C, LANES), 1)
        dum_v = dum_ref[...]
        dbst = jnp.zeros((C, LANES), F32)
        du_parts, dv_parts = [], []
        for g in range(G):
            cs = slice(g * GW, (g + 1) * GW)
            wsm = _sg_masked_ws(ws_ref, g)
            vg = _mx(v[:, cs])
            mixed = _dot(wsm, vg, NN) + bst[:, g:g + 1]
            dumg = dum_v[:, cs]
            du_parts.append(dumg * mixed)
            dmixed = dumg * u[:, cs]
            dmb = _mx(dmixed)
            dv_parts.append(_dot(wsm, dmb, TN))
            dws_ref[g] += _dot(dmb, vg, NT)
            dbst = jnp.where(lane == g, jnp.sum(dmixed, axis=1, keepdims=True), dbst)
        dbst_ref[...] += dbst
        du = jnp.concatenate(du_parts, axis=1)
        dv = jnp.concatenate(dv_parts, axis=1)
        dlng_ref[...] += jnp.sum(dv * xhat, axis=0, keepdims=True)
        dlnb_ref[...] += jnp.sum(dv, axis=0, keepdims=True)
        dxh = dv * lng_ref[...]
        dvp = rstd * (dxh - jnp.mean(dxh, axis=-1, keepdims=True) - xhat * jnp.mean(dxh * xhat, axis=-1, keepdims=True))
        dzz = jnp.concatenate([du, dvp], axis=1)
        dzin = dzz * (cdf + zin * (_INV_SQRT_2PI * jnp.exp(-0.5 * zin * zin)))
        dz_ref[...] = _mx(dzin)
        dbin_ref[...] += jnp.sum(dzin, axis=0, keepdims=True)

    return pl.pallas_call(
        body, name=name, grid=(T // C,),
        in_specs=[pl.BlockSpec((C, 2 * E), lambda n: (n, 0)), _full((1, 2 * E)), _full((1, E)), _full((1, E)),
                  _full((G, C, C)), _full((C, LANES)), pl.BlockSpec((C, E), lambda n: (n, 0))],
        out_specs=[pl.BlockSpec((C, 2 * E), lambda n: (n, 0)), _full((1, 2 * E)), _full((1, E)), _full((1, E)),
                   _full((G, C, C)), _full((C, LANES))],
        out_shape=[jax.ShapeDtypeStruct((T, 2 * E), MXU_DTYPE), jax.ShapeDtypeStruct((1, 2 * E), F32),
                   jax.ShapeDtypeStruct((1, E), F32), jax.ShapeDtypeStruct((1, E), F32),
                   jax.ShapeDtypeStruct((G, C, C), F32), jax.ShapeDtypeStruct((C, LANES), F32)],
        compiler_params=_cp("arbitrary"),
    )(zpre, b_in, ln_g, ln_b, w_s, b_s_t, dum)


def _row(v):
    return v.reshape(1, -1)


def _pad_lanes(v):
    v = v.reshape(1, -1)
    return jnp.pad(v, ((0, 0), (0, LANES - v.shape[1])))


def _local_step(x, target, p):
    ng = p["norm_g"]
    grads = {}
    dng = [[None] * 6 for _ in range(2)]
    saved = []

    def ffn_f(xin, i, j, tag):
        xo, h, gu, y = _ffn_fwd(xin, _row(ng[i, 4 * j]), _row(ng[i, 4 * j + 1]), p["wgu"][i][j], p["wd"][i][j], "ffn_fwd_" + tag)
        return xo, (xin, h, gu, y)

    x1, sv_f00 = ffn_f(x, 0, 0, "00")
    hn0 = _norm_fwd(x1, _row(ng[0, 2]), "dn_prenorm")
    proj = _mm(hn0, p["dn_wqkvz"], "nn", "dn_proj")
    ba = _mm(hn0, p["dn_wba"], "nn", "dn_proj_ba")
    a_log = _pad_lanes(p["dn_a_log"])
    dt_bias = _pad_lanes(p["dn_dt_bias"])
    dn_ng = _row(p["dn_norm_g"])
    qkv = _dn_prep_fwd(proj, p["dn_conv_w"], "dn_prep_fwd")
    beta, gdec = _dn_gate_fwd(ba, a_log, dt_bias, "dn_gate_fwd")
    og, o_raw, tinv, s_all = _dn_scan_fwd(qkv, beta, gdec, proj, dn_ng, "dn_scan_fwd")
    m0 = _mm(og, p["dn_wout"], "nn", "dn_out")
    x2 = _postnorm_fwd(x1, m0, _row(ng[0, 3]), "dn_postnorm")
    x3, sv_f01 = ffn_f(x2, 0, 1, "01")
    x4, sv_f10 = ffn_f(x3, 1, 0, "10")
    hn1 = _norm_fwd(x4, _row(ng[1, 2]), "sg_prenorm")
    zpre = _mm(hn1, p["sg_win"], "nn", "sg_proj")
    sg_bin = _row(p["sg_b_in"])
    sg_lng = _row(p["sg_ln_g"])
    sg_lnb = _row(p["sg_ln_b"])
    sg_bst = jnp.pad(p["sg_b_s"].T, ((0, 0), (0, LANES - SG_GROUPS)))
    um = _sg_fwd(zpre, sg_bin, sg_lng, sg_lnb, p["sg_w_s"], sg_bst, "sg_fwd")
    m1 = _mm(um, p["sg_wout"], "nn", "sg_out")
    x5 = _postnorm_fwd(x4, m1, _row(ng[1, 3]), "sg_postnorm")
    x6, sv_f11 = ffn_f(x5, 1, 1, "11")
    loss_part, dx = _loss_fwd_bwd(x6, target, "loss")

    def ffn_b(dxo, sv, i, j, tag):
        xin, h, gu, y = sv
        dy, a, dgu, dg1 = _ffn_bwd_down(dxo, y, gu, _row(ng[i, 4 * j + 1]), p["wd"][i][j], "ffn_bwd_down_" + tag)
        grads["wd%d%d" % (i, j)] = _mm(a, dy, "tn", "ffn_wgrad_down_" + tag)
        grads["wgu%d%d" % (i, j)] = _mm(h, dgu, "tn", "ffn_wgrad_up_" + tag)
        dxi, dg0 = _ffn_bwd_up(dgu, xin, dxo, _row(ng[i, 4 * j]), p["wgu"][i][j], "ffn_bwd_up_" + tag)
        dng[i][4 * j] = dg0
        dng[i][4 * j + 1] = dg1
        return dxi

    dx = ffn_b(dx, sv_f11, 1, 1, "11")
    dm1, dng[1][3] = _postnorm_bwd(dx, m1, _row(ng[1, 3]), "sg_postnorm_bwd")
    grads["sg_w_out"] = _mm(um, dm1, "tn", "sg_wgrad_out")
    dum = _mm(dm1, p["sg_wout"], "nt", "sg_dgrad_out")
    dz1, dbin, dlng, dlnb, dws, dbst = _sg_bwd(zpre, sg_bin, sg_lng, sg_lnb, p["sg_w_s"], sg_bst, dum, "sg_bwd")
    grads["sg_w_in"] = _mm(hn1, dz1, "tn", "sg_wgrad_in")
    dh1 = _mm(dz1, p["sg_win"], "nt", "sg_dgrad_in")
    dx, dng[1][2] = _prenorm_bwd(dx, dh1, x4, _row(ng[1, 2]), "sg_prenorm_bwd")
    grads["sg_b_in"] = dbin.reshape(1, -1)
    grads["sg_ln_g"] = dlng.reshape(1, -1)
    grads["sg_ln_b"] = dlnb.reshape(1, -1)
    grads["sg_w_s"] = jnp.where(jnp.tril(jnp.ones((SG_CHUNK, SG_CHUNK), bool)), dws, 0.0)[None]
    grads["sg_b_s"] = dbst[:, :SG_GROUPS].T[None]
    dx = ffn_b(dx, sv_f10, 1, 0, "10")
    dx = ffn_b(dx, sv_f01, 0, 1, "01")
    dm0, dng[0][3] = _postnorm_bwd(dx, m0, _row(ng[0, 3]), "dn_postnorm_bwd")
    grads["dn_w_out"] = _mm(og, dm0, "tn", "dn_wgrad_out")
    dog = _mm(dm0, p["dn_wout"], "nt", "dn_dgrad_out")
    dqkv, dbeta, dgdec, dz0, dnng = _dn_scan_bwd(qkv, beta, gdec, proj, dn_ng, o_raw, tinv, s_all, dog, "dn_scan_bwd")
    dqkv_pre, dconv = _dn_prep_bwd(proj, p["dn_conv_w"], dqkv, "dn_prep_bwd")
    dba, dal, ddt = _dn_gate_bwd(ba, a_log, dt_bias, dbeta, dgdec, "dn_gate_bwd")
    W3 = 3 * DN_HEADS * DN_HEAD_DIM
    dw_qkv = _mm(hn0, dqkv_pre, "tn", "dn_wgrad_qkv")
    dw_z = _mm(hn0, dz0, "tn", "dn_wgrad_z")
    dw_ba = _mm(hn0, dba, "tn", "dn_wgrad_ba")
    grads["dn_w_in"] = jnp.concatenate(
        [dw_qkv, dw_z, dw_ba[:, :DN_HEADS], dw_ba[:, LANES:LANES + DN_HEADS]], axis=1)
    dh0 = _mm(dqkv_pre, p["dn_wqkvz"][:, :W3], "nt", "dn_dgrad_qkv")
    dh0 = _mm(dz0, p["dn_wqkvz"][:, W3:], "nt", "dn_dgrad_z", add=dh0)
    dh0 = _mm(dba, p["dn_wba"], "nt", "dn_dgrad_ba", add=dh0)
    dx, dng[0][2] = _prenorm_bwd(dx, dh0, x1, _row(ng[0, 2]), "dn_prenorm_bwd")
    grads["dn_conv_w"] = dconv[None]
    grads["dn_a_log"] = dal[:, :DN_HEADS]
    grads["dn_dt_bias"] = ddt[:, :DN_HEADS]
    grads["dn_norm_g"] = dnng
    dx = ffn_b(dx, sv_f00, 0, 0, "00")
    grads["norm_g"] = jnp.stack([jnp.concatenate(dng[i], axis=0) for i in range(2)])
    return loss_part, dx, grads


def _mesh_pos():
    return lax.axis_index("x"), lax.axis_index("y"), lax.axis_index("c")


def _other_chips(x, y):
    return [(1 - x, y), (x, 1 - y), (1 - x, 1 - y)]


def _allgather_chips(arrs, name):
    n = len(arrs)

    def body(*refs):
        ins, outs = refs[:n], refs[n:2 * n]
        ici_send, ici_recv, d2d_send, d2d_recv = refs[2 * n:]
        x, y, c = _mesh_pos()
        me = 2 * x + y
        chips = _other_chips(x, y)
        sibling = (x, y, 1 - c)

        def ici(i, j, k):
            cx, cy = chips[j]
            return pltpu.make_async_remote_copy(src_ref=ins[i].at[c], dst_ref=outs[i].at[k, c], send_sem=ici_send.at[3 * i + j],
                                                recv_sem=ici_recv.at[3 * i + j], device_id=(cx, cy, c), device_id_type=MESH)

        def d2d(i, j, h):
            cx, cy = chips[j]
            slot = outs[i].at[2 * cx + cy, h]
            return pltpu.make_async_remote_copy(src_ref=slot, dst_ref=slot, send_sem=d2d_send.at[3 * i + j],
                                                recv_sem=d2d_recv.at[3 * i + j], device_id=sibling, device_id_type=MESH)

        sends = [ici(i, j, me) for i in range(n) for j in range(3)]
        for cp in sends:
            cp.start()
        for i in range(n):
            for j, (cx, cy) in enumerate(chips):
                ici(i, j, 2 * cx + cy).wait_recv()
                fwd = d2d(i, j, c)
                fwd.start()
                sends.append(fwd)
        for i in range(n):
            for j in range(3):
                d2d(i, j, 1 - c).wait_recv()
        for cp in sends:
            cp.wait_send()

    return pl.pallas_call(
        body, name=name, in_specs=[ANY] * n, out_specs=[ANY] * n,
        out_shape=[jax.ShapeDtypeStruct((N_CHIPS,) + a.shape, a.dtype) for a in arrs],
        scratch_shapes=[pltpu.SemaphoreType.DMA((3 * n,))] * 4,
    )(*arrs)


def _swap_halves(arrs, name):
    n = len(arrs)

    def body(*refs):
        ins, outs = refs[:n], refs[n:2 * n]
        send_sems, recv_sems = refs[2 * n:]
        x, y, c = _mesh_pos()
        cps = [pltpu.make_async_remote_copy(src_ref=ins[i].at[1 - c], dst_ref=outs[i], send_sem=send_sems.at[i],
                                            recv_sem=recv_sems.at[i], device_id=(x, y, 1 - c), device_id_type=MESH)
               for i in range(n)]
        for cp in cps:
            cp.start()
        for cp in cps:
            cp.wait()

    return pl.pallas_call(
        body, name=name, in_specs=[ANY] * n, out_specs=[ANY] * n,
        out_shape=[jax.ShapeDtypeStruct(a.shape[1:], a.dtype) for a in arrs],
        scratch_shapes=[pltpu.SemaphoreType.DMA((n,)), pltpu.SemaphoreType.DMA((n,))],
    )(*arrs)


def _scatter_chips(arrs, name):
    n = len(arrs)

    def body(*refs):
        ins, outs = refs[:n], refs[n:2 * n]
        send_sems, recv_sems = refs[2 * n:]
        x, y, c = _mesh_pos()
        me = 2 * x + y
        chips = _other_chips(x, y)

        def copy(i, j, src_k, dst_k):
            cx, cy = chips[j]
            return pltpu.make_async_remote_copy(src_ref=ins[i].at[src_k], dst_ref=outs[i].at[dst_k],
                                                send_sem=send_sems.at[3 * i + j], recv_sem=recv_sems.at[3 * i + j],
                                                device_id=(cx, cy, c), device_id_type=MESH)

        sends = [copy(i, j, 2 * chips[j][0] + chips[j][1], me) for i in range(n) for j in range(3)]
        for cp in sends:
            cp.start()
        for i in range(n):
            for j, (cx, cy) in enumerate(chips):
                copy(i, j, me, 2 * cx + cy).wait_recv()
        for cp in sends:
            cp.wait_send()

    return pl.pallas_call(
        body, name=name, in_specs=[ANY] * n, out_specs=[ANY] * n,
        out_shape=[jax.ShapeDtypeStruct(a.shape, a.dtype) for a in arrs],
        scratch_shapes=[pltpu.SemaphoreType.DMA((3 * n,)), pltpu.SemaphoreType.DMA((3 * n,))],
    )(*arrs)


def _swap_whole(arrs, name):
    n = len(arrs)

    def body(*refs):
        ins, outs = refs[:n], refs[n:2 * n]
        send_sems, recv_sems = refs[2 * n:]
        x, y, c = _mesh_pos()
        cps = [pltpu.make_async_remote_copy(src_ref=ins[i], dst_ref=outs[i], send_sem=send_sems.at[i],
                                            recv_sem=recv_sems.at[i], device_id=(x, y, 1 - c), device_id_type=MESH)
               for i in range(n)]
        for cp in cps:
            cp.start()
        for cp in cps:
            cp.wait()

    return pl.pallas_call(
        body, name=name, in_specs=[ANY] * n, out_specs=[ANY] * n,
        out_shape=[jax.ShapeDtypeStruct(a.shape, a.dtype) for a in arrs],
        scratch_shapes=[pltpu.SemaphoreType.DMA((n,)), pltpu.SemaphoreType.DMA((n,))],
    )(*arrs)


def _allgather_devices(a, name):
    masks = [(mx, my, mc) for mx in (0, 1) for my in (0, 1) for mc in (0, 1)][1:]

    def body(in_ref, out_ref, send_sems, recv_sems, loc_sem):
        x, y, c = _mesh_pos()
        me = 4 * x + 2 * y + c
        lc = pltpu.make_async_copy(in_ref, out_ref.at[me], loc_sem.at[0])
        lc.start()
        peers = [(jnp.where(mx, 1 - x, x), jnp.where(my, 1 - y, y), jnp.where(mc, 1 - c, c)) for mx, my, mc in masks]
        cps = [pltpu.make_async_remote_copy(src_ref=in_ref, dst_ref=out_ref.at[me], send_sem=send_sems.at[j],
                                            recv_sem=recv_sems.at[j], device_id=peers[j], device_id_type=MESH)
               for j in range(len(masks))]
        for cp in cps:
            cp.start()
        for j, (px, py, pc) in enumerate(peers):
            pltpu.make_async_remote_copy(src_ref=in_ref, dst_ref=out_ref.at[4 * px + 2 * py + pc], send_sem=send_sems.at[j],
                                         recv_sem=recv_sems.at[j], device_id=peers[j], device_id_type=MESH).wait_recv()
        for cp in cps:
            cp.wait_send()
        lc.wait()

    return pl.pallas_call(
        body, name=name, in_specs=[ANY], out_specs=ANY,
        out_shape=jax.ShapeDtypeStruct((N_DEV,) + a.shape, a.dtype),
        scratch_shapes=[pltpu.SemaphoreType.DMA((N_DEV - 1,)), pltpu.SemaphoreType.DMA((N_DEV - 1,)),
                        pltpu.SemaphoreType.DMA((1,))],
    )(a)


def _as_rows(a, lead):
    shp = a.shape
    rows = 1
    for s in shp[lead:-1]:
        rows *= s
    return a.reshape(shp[:lead] + (rows, shp[-1]))


def _row_tile(rows, cols, n_bufs):
    budget = (24 * 1024 * 1024) // (n_bufs * 2 * 4 * cols)
    return _pick(rows, max(2 * SUBLANES, budget), 2 * SUBLANES)


def _sum_leading(a, name):
    n = a.shape[0]
    v = _as_rows(a, 1)
    _, rows, cols = v.shape
    tr = _row_tile(rows, cols, n + 1)

    def body(a_ref, o_ref):
        acc = a_ref[0]
        for k in range(1, n):
            acc = acc + a_ref[k]
        o_ref[...] = acc

    out = pl.pallas_call(body, name=name, grid=(rows // tr,),
                         in_specs=[pl.BlockSpec((n, tr, cols), lambda i: (0, i, 0))],
                         out_specs=pl.BlockSpec((tr, cols), lambda i: (i, 0)),
                         out_shape=jax.ShapeDtypeStruct((rows, cols), F32), compiler_params=_cp("parallel"))(v)
    return out.reshape(a.shape[1:])


def _scalar(i):
    return jnp.reshape(i, (1,)).astype(jnp.int32)


def _add_own_half(g, other, c, name):
    v = _as_rows(g, 1)
    ov = _as_rows(other, 0)
    rows, cols = ov.shape
    tr = _row_tile(rows, cols, 3)

    def body(c_ref, g_ref, o_ref, out_ref):
        out_ref[...] = (g_ref[0] + o_ref[...]).astype(out_ref.dtype)

    out = pl.pallas_call(
        body, name=name,
        grid_spec=pltpu.PrefetchScalarGridSpec(
            num_scalar_prefetch=1, grid=(rows // tr,),
            in_specs=[pl.BlockSpec((1, tr, cols), lambda i, c_ref: (c_ref[0], i, 0)),
                      pl.BlockSpec((tr, cols), lambda i, c_ref: (i, 0))],
            out_specs=pl.BlockSpec((tr, cols), lambda i, c_ref: (i, 0))),
        out_shape=jax.ShapeDtypeStruct((rows, cols), COMM_DTYPE), compiler_params=_cp("parallel"),
    )(_scalar(c), v, ov)
    return out.reshape(other.shape)


def _sum_chips(own, got, chip, name):
    pv = _as_rows(own, 1)
    bv = _as_rows(got, 1)
    _, rows, cols = pv.shape
    tr = _row_tile(rows, cols, N_CHIPS + 2)

    def body(chip_ref, p_ref, b_ref, o_ref):
        mine = p_ref[0].astype(F32)
        acc = jnp.where(chip_ref[0] == 0, mine, b_ref[0].astype(F32))
        for k in range(1, N_CHIPS):
            acc = acc + jnp.where(chip_ref[0] == k, mine, b_ref[k].astype(F32))
        o_ref[...] = acc

    out = pl.pallas_call(
        body, name=name,
        grid_spec=pltpu.PrefetchScalarGridSpec(
            num_scalar_prefetch=1, grid=(rows // tr,),
            in_specs=[pl.BlockSpec((1, tr, cols), lambda i, k_ref: (k_ref[0], i, 0)),
                      pl.BlockSpec((N_CHIPS, tr, cols), lambda i, k_ref: (0, i, 0))],
            out_specs=pl.BlockSpec((tr, cols), lambda i, k_ref: (i, 0))),
        out_shape=jax.ShapeDtypeStruct((rows, cols), F32), compiler_params=_cp("parallel"),
    )(_scalar(chip), pv, bv)
    return out.reshape(own.shape[1:])


def _adam_math(w, g, m, v):
    nm = ADAM_B1 * m + (1.0 - ADAM_B1) * g
    nv = ADAM_B2 * v + (1.0 - ADAM_B2) * (g * g)
    m_hat = nm / (1.0 - ADAM_B1 ** ADAM_STEP)
    v_hat = nv / (1.0 - ADAM_B2 ** ADAM_STEP)
    return -ADAM_LR * (m_hat / (jnp.sqrt(v_hat) + ADAM_EPS) + ADAM_WD * w), nm, nv


def _adamw_halves(w, mine, theirs, m, v, c, name):
    shape = w.shape
    ws, ms, vs = (_as_rows(t.reshape((2, -1) + t.shape[-1:]), 1) for t in (w, m, v))
    a, b = _as_rows(mine, 0), _as_rows(theirs, 0)
    rows, cols = a.shape
    tr = _row_tile(rows, cols, 9)

    def body(c_ref, w_ref, a_ref, b_ref, m_ref, v_ref, g_ref, d_ref, nm_ref, nv_ref):
        gv = jnp.where(pl.program_id(0) == c_ref[0], a_ref[...], b_ref[...])
        g_ref[0] = gv
        d_ref[0], nm_ref[0], nv_ref[0] = _adam_math(w_ref[0], gv, m_ref[0], v_ref[0])

    half = pl.BlockSpec((1, tr, cols), lambda h, i, c_ref: (h, i, 0))
    flat = pl.BlockSpec((tr, cols), lambda h, i, c_ref: (i, 0))
    outs = pl.pallas_call(
        body, name=name,
        grid_spec=pltpu.PrefetchScalarGridSpec(num_scalar_prefetch=1, grid=(2, rows // tr),
                                               in_specs=[half, flat, flat, half, half], out_specs=[half] * 4),
        out_shape=[jax.ShapeDtypeStruct((2, rows, cols), F32)] * 4, compiler_params=_cp("parallel", "parallel"),
    )(_scalar(c), ws, a, b, ms, vs)
    return tuple(o.reshape(shape) for o in outs)


def _adamw(w, g, m, v, name):
    shape = w.shape
    ws, gs, ms, vs = (_as_rows(t, 0) for t in (w, g, m, v))
    rows, cols = ws.shape
    tr = _row_tile(rows, cols, 7)

    def body(w_ref, g_ref, m_ref, v_ref, d_ref, nm_ref, nv_ref):
        d_ref[...], nm_ref[...], nv_ref[...] = _adam_math(w_ref[...], g_ref[...], m_ref[...], v_ref[...])

    spec = pl.BlockSpec((tr, cols), lambda i: (i, 0))
    outs = pl.pallas_call(body, name=name, grid=(rows // tr,), in_specs=[spec] * 4, out_specs=[spec] * 3,
                          out_shape=[jax.ShapeDtypeStruct((rows, cols), F32)] * 3, compiler_params=_cp("parallel"))(ws, gs, ms, vs)
    return tuple(o.reshape(shape) for o in outs)


_BIG = ["ffn_w_gate", "ffn_w_up", "ffn_w_down", "dn_w_in", "dn_w_out", "sg_w_in", "sg_w_out"]
_SMALL_SHARDED = ["norm_g", "dn_conv_w", "sg_b_in", "sg_ln_g", "sg_ln_b"]
_SMALL_REPL = ["dn_a_log", "dn_dt_bias", "dn_norm_g", "sg_w_s", "sg_b_s"]
_WEIGHTS = ["norm_g", "ffn_w_gate", "ffn_w_up", "ffn_w_down", "dn_w_in", "dn_conv_w", "dn_a_log", "dn_dt_bias",
            "dn_norm_g", "dn_w_out", "sg_w_in", "sg_b_in", "sg_ln_g", "sg_ln_b", "sg_w_s", "sg_b_s", "sg_w_out"]
PACK_COLS = 1024


def _pack(arrs):
    flat = jnp.concatenate([a.reshape(-1) for a in arrs])
    pad = (-flat.shape[0]) % (SUBLANES * PACK_COLS)
    return jnp.pad(flat, (0, pad)).reshape(-1, PACK_COLS)


def _unpack(buf, shapes):
    flat = buf.reshape(-1)
    out, off = [], 0
    for s in shapes:
        n = math.prod(s)
        out.append(flat[off:off + n].reshape(s))
        off += n
    return out


def _as_halves(a):
    if a.shape[0] == 2:
        return a
    if a.shape[0] == 1:
        return a.reshape((2, a.shape[1] // 2) + a.shape[2:])
    return a.reshape((2, a.shape[0] // 2) + a.shape[1:])


def _with_own(gathered, own, chip):
    g = gathered.reshape((N_CHIPS,) + own.shape)
    return [jnp.where(chip == k, own, g[k]) for k in range(N_CHIPS)]


def _cat_shards(g, axis):
    return jnp.concatenate(list(g), axis=axis)


def _full_weights(w, gathered, small_full):
    gate, up, down, dn_in, dn_out, sg_in, sg_out = gathered
    sel = lambda g, *idx: [s[idx] for s in g]
    p = dict(small_full)
    p["wgu"] = [[jnp.concatenate([_cat_shards(sel(gate, i, j), 1), _cat_shards(sel(up, i, j), 1)], axis=1) for j in range(2)]
                for i in range(2)]
    p["wd"] = [[_cat_shards(sel(down, i, j), 0) for j in range(2)] for i in range(2)]
    dn_in, dn_out, sg_in, sg_out = (sel(g, 0) for g in (dn_in, dn_out, sg_in, sg_out))
    dn_full = _cat_shards(dn_in, 1)
    W4 = 4 * DN_HEADS * DN_HEAD_DIM
    p["dn_wqkvz"] = dn_full[:, :W4]
    wba = jnp.zeros((D_MODEL, 2 * LANES), dn_full.dtype)
    wba = wba.at[:, :DN_HEADS].set(dn_full[:, W4:W4 + DN_HEADS])
    wba = wba.at[:, LANES:LANES + DN_HEADS].set(dn_full[:, W4 + DN_HEADS:])
    p["dn_wba"] = wba
    p["dn_wout"] = _cat_shards(dn_out, 0)
    p["sg_win"] = _cat_shards(sg_in, 1)
    p["sg_wout"] = _cat_shards(sg_out, 0)
    for k in _SMALL_REPL:
        p[k] = w[k][0]
    return p


def _split_cols(a, n):
    w = a.shape[-1] // n
    return [a[..., k * w:(k + 1) * w] for k in range(n)]


def _split_rows(a, n):
    h = a.shape[-2] // n
    return [a[..., k * h:(k + 1) * h, :] for k in range(n)]


def _grads_by_half_and_chip(grads):
    F = D_FF
    gate = jnp.stack([jnp.stack([jnp.stack([_split_cols(grads["wgu%d%d" % (i, j)][:, :F], 4)[k] for j in range(2)])
                                 for k in range(4)]) for i in range(2)])
    up = jnp.stack([jnp.stack([jnp.stack([_split_cols(grads["wgu%d%d" % (i, j)][:, F:], 4)[k] for j in range(2)])
                               for k in range(4)]) for i in range(2)])
    down = jnp.stack([jnp.stack([jnp.stack([_split_rows(grads["wd%d%d" % (i, j)], 4)[k] for j in range(2)])
                                 for k in range(4)]) for i in range(2)])

    def col_sharded(a):
        return jnp.stack([jnp.stack(_split_cols(hf, 4)) for hf in _split_rows(a, 2)])

    def row_sharded(a):
        return jnp.stack([jnp.stack([_split_rows(s, 2)[hf] for s in _split_rows(a, 4)]) for hf in range(2)])

    return [gate, up, down, col_sharded(grads["dn_w_in"]), row_sharded(grads["dn_w_out"]),
            col_sharded(grads["sg_w_in"]), row_sharded(grads["sg_w_out"])]


def kernel(x, norm_g, ffn_w_gate, ffn_w_up, ffn_w_down, dn_w_in, dn_conv_w, dn_a_log, dn_dt_bias, dn_norm_g, dn_w_out, sg_w_in, sg_b_in, sg_ln_g, sg_ln_b, sg_w_s, sg_b_s, sg_w_out, loss_target, m_norm_g, m_ffn_w_gate, m_ffn_w_up, m_ffn_w_down, m_dn_w_in, m_dn_conv_w, m_dn_a_log, m_dn_dt_bias, m_dn_norm_g, m_dn_w_out, m_sg_w_in, m_sg_b_in, m_sg_ln_g, m_sg_ln_b, m_sg_w_s, m_sg_b_s, m_sg_w_out, v_norm_g, v_ffn_w_gate, v_ffn_w_up, v_ffn_w_down, v_dn_w_in, v_dn_conv_w, v_dn_a_log, v_dn_dt_bias, v_dn_norm_g, v_dn_w_out, v_sg_w_in, v_sg_b_in, v_sg_ln_g, v_sg_ln_b, v_sg_w_s, v_sg_b_s, v_sg_w_out):
    args = dict(locals())
    w = {k: args[k] for k in _WEIGHTS}
    mom = {k: args["m_" + k] for k in _WEIGHTS}
    var = {k: args["v_" + k] for k in _WEIGHTS}
    cx, cy, cc = _mesh_pos()
    chip = 2 * cx + cy

    small_shapes = [w[k].shape for k in _SMALL_SHARDED]
    own = [_mx(w[k]) for k in _BIG] + [_pack([w[k] for k in _SMALL_SHARDED])]
    gathered = _allgather_chips([_as_halves(a) for a in own], "gather_weights")
    gathered = [_with_own(g, a, chip) for g, a in zip(gathered, own)]
    small_k = [_unpack(gathered[-1][k], small_shapes) for k in range(N_CHIPS)]
    small_full = {name: jnp.concatenate([small_k[k][i] for k in range(N_CHIPS)], axis=-1)
                  for i, name in enumerate(_SMALL_SHARDED)}
    small_full = {k: (v if k == "norm_g" else v[0]) for k, v in small_full.items()}
    p = _full_weights(w, gathered[:-1], small_full)

    loss_part, grad_x, grads = _local_step(x[0], loss_target[0], p)

    halves = _grads_by_half_and_chip(grads)
    from_sibling = _swap_halves(halves, "reduce_core_pair")
    pair_sum = [_add_own_half(h, o, cc, "pair_sum_%d" % i) for i, (h, o) in enumerate(zip(halves, from_sibling))]
    from_chips = _scatter_chips(pair_sum, "reduce_chips")
    half_sum = [_sum_chips(own_sum, got, chip, "chip_sum_%d" % i) for i, (own_sum, got) in enumerate(zip(pair_sum, from_chips))]
    other_half = _swap_whole(half_sum, "gather_core_pair")

    small_names = _SMALL_SHARDED + _SMALL_REPL
    small_grads = [grads[k] for k in small_names]
    full_shapes = [g.shape for g in small_grads] + [(1,)]
    pack = _pack(small_grads + [loss_part[0, :1]])
    summed = _sum_leading(_allgather_devices(pack, "gather_small"), "small_sum")
    parts = _unpack(summed, full_shapes)
    loss = parts[-1][0]
    small_grad = {}
    for i, k in enumerate(small_names):
        g = parts[i]
        if k in _SMALL_SHARDED:
            n = w[k].shape[-1]
            g = lax.dynamic_slice_in_dim(g, chip * n, n, axis=g.ndim - 1)
        small_grad[k] = g

    grad = dict(small_grad)
    delta, new_m, new_v = {}, {}, {}
    for i, k in enumerate(_BIG):
        grad[k], delta[k], new_m[k], new_v[k] = _adamw_halves(w[k], half_sum[i], other_half[i], mom[k], var[k], cc, "adamw_" + k)
    shapes = [w[k].shape for k in small_names]
    d, nm, nv = _adamw(_pack([w[k] for k in small_names]), _pack([grad[k] for k in small_names]),
                       _pack([mom[k] for k in small_names]), _pack([var[k] for k in small_names]), "adamw_small")
    for k, a, b, c_ in zip(small_names, _unpack(d, shapes), _unpack(nm, shapes), _unpack(nv, shapes)):
        delta[k], new_m[k], new_v[k] = a, b, c_

    return (loss, grad_x[None], *[grad[k] for k in _WEIGHTS], *[delta[k] for k in _WEIGHTS],
            *[new_m[k] for k in _WEIGHTS], *[new_v[k] for k in _WEIGHTS])
```

```python
import functools
import math

import jax
import jax.numpy as jnp
from jax import lax
from jax.experimental import pallas as pl
from jax.experimental.pallas import tpu as pltpu

F32 = jnp.float32
MXU_DTYPE = jnp.bfloat16
COMM_DTYPE = jnp.bfloat16
HI = lax.Precision.HIGHEST
TRI_PREC = lax.Precision.HIGH

D_MODEL = 1024
D_FF = 2816
RMS_EPS = 1e-6
LN_EPS = 1e-5
L2_EPS = 1e-6
DN_HEADS = 8
DN_HEAD_DIM = 128
DN_CONV = 4
DN_CHUNK = 64
SG_WIDTH = 2048
SG_GROUPS = 8
SG_CHUNK = 128
SG_GROUP_W = SG_WIDTH // SG_GROUPS
N_CHIPS = 4
N_DEV = 8
LANES = 128
SUBLANES = 8
VMEM_LIMIT = 56 * 1024 * 1024

ADAM_LR = 0.001
ADAM_B1 = 0.9
ADAM_B2 = 0.999
ADAM_EPS = 1e-08
ADAM_WD = 0.01
ADAM_STEP = 10

MESH = pl.DeviceIdType.MESH
ANY = pl.BlockSpec(memory_space=pl.ANY)


def _cp(*sem):
    return pltpu.CompilerParams(dimension_semantics=sem, vmem_limit_bytes=VMEM_LIMIT)


def _pick(n, pref, mult=LANES):
    best = None
    d = mult
    while d <= min(n, pref):
        if n % d == 0:
            best = d
        d += mult
    return best if best is not None else n


def _full(shape):
    nd = len(shape)
    return pl.BlockSpec(shape, lambda *_: (0,) * nd)


def _sigmoid(x):
    return 1.0 / (1.0 + jnp.exp(-x))


def _dot(a, b, dims, prec=None):
    return lax.dot_general(a, b, (dims, ((), ())), preferred_element_type=F32, precision=prec)


NN = ((1,), (0,))
NT = ((1,), (1,))
TN = ((0,), (0,))


def _mx(a):
    return a.astype(MXU_DTYPE)


def _rms_stat(x):
    return lax.rsqrt(jnp.mean(x * x, axis=-1, keepdims=True) + RMS_EPS)


def _rms_bwd(x, r, g, dy):
    xh = x * r
    dxh = dy * g
    dx = r * (dxh - xh * jnp.mean(dxh * xh, axis=-1, keepdims=True))
    return dx, jnp.sum(dy * xh, axis=0, keepdims=True)


def _mm(a, b, mode, name, out_dtype=F32, add=None):
    if mode == "tn":
        K, M = a.shape
        N = b.shape[1]
    elif mode == "nt":
        M, K = a.shape
        N = b.shape[0]
    else:
        M, K = a.shape
        N = b.shape[1]
    tn = _pick(N, 1024)
    if mode == "tn":
        tm = _pick(M, 1024 if tn <= 512 else 1408)
        tk = _pick(K, 1024, SUBLANES)
    else:
        tm = _pick(M, max(512, min(2048, (512 * 1024) // tn)), SUBLANES)
        tk = _pick(K, 2048)
    nk = K // tk
    grid = (N // tn, M // tm, nk)
    if mode == "nn":
        a_spec = pl.BlockSpec((tm, tk), lambda j, i, k: (i, k))
        b_spec = pl.BlockSpec((tk, tn), lambda j, i, k: (k, j))
        dims = NN
    elif mode == "nt":
        a_spec = pl.BlockSpec((tm, tk), lambda j, i, k: (i, k))
        b_spec = pl.BlockSpec((tn, tk), lambda j, i, k: (j, k))
        dims = NT
    else:
        a_spec = pl.BlockSpec((tk, tm), lambda j, i, k: (k, i))
        b_spec = pl.BlockSpec((tk, tn), lambda j, i, k: (k, j))
        dims = TN
    o_spec = pl.BlockSpec((tm, tn), lambda j, i, k: (i, j))
    has_add = add is not None

    def body(*refs):
        if has_add:
            a_ref, b_ref, add_ref, o_ref, acc = refs
        else:
            a_ref, b_ref, o_ref, acc = refs
        k = pl.program_id(2)

        @pl.when(k == 0)
        def _():
            acc[...] = add_ref[...] if has_add else jnp.zeros_like(acc)

        acc[...] += _dot(a_ref[...], b_ref[...], dims)

        @pl.when(k == nk - 1)
        def _():
            o_ref[...] = acc[...].astype(o_ref.dtype)

    ins = [a, b] + ([add] if has_add else [])
    specs = [a_spec, b_spec] + ([o_spec] if has_add else [])
    return pl.pallas_call(
        body, name=name, grid=grid, in_specs=specs, out_specs=o_spec,
        out_shape=jax.ShapeDtypeStruct((M, N), out_dtype),
        scratch_shapes=[pltpu.VMEM((tm, tn), F32)],
        compiler_params=_cp("parallel", "parallel", "arbitrary"),
    )(*ins)


def _load_resident(pairs, sem):
    @pl.when(pl.program_id(0) == 0)
    def _():
        cps = [pltpu.make_async_copy(src, dst, sem.at[i]) for i, (src, dst) in enumerate(pairs)]
        for c in cps:
            c.start()
        for c in cps:
            c.wait()


def _ffn_fwd(x, g0, g1, wgu, wd, name):
    T, D = x.shape
    F2 = wgu.shape[1]
    F = F2 // 2
    tm = _pick(T, 256, SUBLANES)

    def body(x_ref, g0_ref, g1_ref, wgu_hbm, wd_hbm, xo_ref, h_ref, gu_ref, y_ref, wgu_v, wd_v, sem):
        _load_resident([(wgu_hbm, wgu_v), (wd_hbm, wd_v)], sem)
        xv = x_ref[...]
        hb = _mx(xv * _rms_stat(xv) * g0_ref[...])
        h_ref[...] = hb
        gu = _dot(hb, wgu_v[...], NN)
        gu_ref[...] = gu
        g = gu[:, :F]
        u = gu[:, F:]
        a = _mx(g * _sigmoid(g) * u)
        y = _dot(a, wd_v[...], NN)
        y_ref[...] = y
        xo_ref[...] = xv + 0.5 * (y * _rms_stat(y) * g1_ref[...])

    row = lambda w: pl.BlockSpec((tm, w), lambda i: (i, 0))
    return pl.pallas_call(
        body, name=name, grid=(T // tm,),
        in_specs=[row(D), _full((1, D)), _full((1, D)), ANY, ANY],
        out_specs=[row(D), row(D), row(F2), row(D)],
        out_shape=[jax.ShapeDtypeStruct((T, D), F32), jax.ShapeDtypeStruct((T, D), MXU_DTYPE),
                   jax.ShapeDtypeStruct((T, F2), F32), jax.ShapeDtypeStruct((T, D), F32)],
        scratch_shapes=[pltpu.VMEM(wgu.shape, wgu.dtype), pltpu.VMEM(wd.shape, wd.dtype),
                        pltpu.SemaphoreType.DMA((2,))],
        compiler_params=_cp("arbitrary"),
    )(x, g0, g1, wgu, wd)


def _ffn_bwd_down(dxo, y, gu, g1, wd, name):
    T, D = y.shape
    F2 = gu.shape[1]
    F = F2 // 2
    tm = _pick(T, 256, SUBLANES)

    def body(dxo_ref, y_ref, gu_ref, g1_ref, wd_hbm, dy_ref, a_ref, dgu_ref, dg1_ref, wd_v, sem):
        _load_resident([(wd_hbm, wd_v)], sem)

        @pl.when(pl.program_id(0) == 0)
        def _():
            dg1_ref[...] = jnp.zeros_like(dg1_ref)

        yv = y_ref[...]
        dy, dg1 = _rms_bwd(yv, _rms_stat(yv), g1_ref[...], 0.5 * dxo_ref[...])
        dg1_ref[...] += dg1
        dyb = _mx(dy)
        dy_ref[...] = dyb
        da = _dot(dyb, wd_v[...], NT)
        gu_v = gu_ref[...]
        g = gu_v[:, :F]
        u = gu_v[:, F:]
        s = _sigmoid(g)
        sg = g * s
        a_ref[...] = _mx(sg * u)
        dgu_ref[:, :F] = _mx(da * u * (s * (1.0 + g * (1.0 - s))))
        dgu_ref[:, F:] = _mx(da * sg)

    row = lambda w: pl.BlockSpec((tm, w), lambda i: (i, 0))
    return pl.pallas_call(
        body, name=name, grid=(T // tm,),
        in_specs=[row(D), row(D), row(F2), _full((1, D)), ANY],
        out_specs=[row(D), row(F), row(F2), _full((1, D))],
        out_shape=[jax.ShapeDtypeStruct((T, D), MXU_DTYPE), jax.ShapeDtypeStruct((T, F), MXU_DTYPE),
                   jax.ShapeDtypeStruct((T, F2), MXU_DTYPE), jax.ShapeDtypeStruct((1, D), F32)],
        scratch_shapes=[pltpu.VMEM(wd.shape, wd.dtype), pltpu.SemaphoreType.DMA((1,))],
        compiler_params=_cp("arbitrary"),
    )(dxo, y, gu, g1, wd)


def _ffn_bwd_up(dgu, x, dxo, g0, wgu, name):
    T, D = x.shape
    F2 = dgu.shape[1]
    tm = _pick(T, 256, SUBLANES)

    def body(dgu_ref, x_ref, dxo_ref, g0_ref, wgu_hbm, dx_ref, dg0_ref, wgu_v, sem):
        _load_resident([(wgu_hbm, wgu_v)], sem)

        @pl.when(pl.program_id(0) == 0)
        def _():
            dg0_ref[...] = jnp.zeros_like(dg0_ref)

        dh = _dot(dgu_ref[...], wgu_v[...], NT)
        xv = x_ref[...]
        dx, dg0 = _rms_bwd(xv, _rms_stat(xv), g0_ref[...], dh)
        dg0_ref[...] += dg0
        dx_ref[...] = dxo_ref[...] + dx

    row = lambda w: pl.BlockSpec((tm, w), lambda i: (i, 0))
    return pl.pallas_call(
        body, name=name, grid=(T // tm,),
        in_specs=[row(F2), row(D), row(D), _full((1, D)), ANY],
        out_specs=[row(D), _full((1, D))],
        out_shape=[jax.ShapeDtypeStruct((T, D), F32), jax.ShapeDtypeStruct((1, D), F32)],
        scratch_shapes=[pltpu.VMEM(wgu.shape, wgu.dtype), pltpu.SemaphoreType.DMA((1,))],
        compiler_params=_cp("arbitrary"),
    )(dgu, x, dxo, g0, wgu)


def _norm_fwd(x, g, name):
    T, D = x.shape
    tm = _pick(T, 512, SUBLANES)

    def body(x_ref, g_ref, h_ref):
        xv = x_ref[...]
        h_ref[...] = _mx(xv * _rms_stat(xv) * g_ref[...])

    row = pl.BlockSpec((tm, D), lambda i: (i, 0))
    return pl.pallas_call(body, name=name, grid=(T // tm,), in_specs=[row, _full((1, D))], out_specs=row,
                          out_shape=jax.ShapeDtypeStruct((T, D), MXU_DTYPE), compiler_params=_cp("parallel"))(x, g)


def _postnorm_fwd(x, m, g, name):
    T, D = x.shape
    tm = _pick(T, 512, SUBLANES)

    def body(x_ref, m_ref, g_ref, o_ref):
        mv = m_ref[...]
        o_ref[...] = x_ref[...] + mv * _rms_stat(mv) * g_ref[...]

    row = pl.BlockSpec((tm, D), lambda i: (i, 0))
    return pl.pallas_call(body, name=name, grid=(T // tm,), in_specs=[row, row, _full((1, D))], out_specs=row,
                          out_shape=jax.ShapeDtypeStruct((T, D), F32), compiler_params=_cp("parallel"))(x, m, g)


def _postnorm_bwd(dxo, m, g, name):
    T, D = m.shape
    tm = _pick(T, 512, SUBLANES)

    def body(dxo_ref, m_ref, g_ref, dm_ref, dg_ref):
        @pl.when(pl.program_id(0) == 0)
        def _():
            dg_ref[...] = jnp.zeros_like(dg_ref)

        mv = m_ref[...]
        dm, dg = _rms_bwd(mv, _rms_stat(mv), g_ref[...], dxo_ref[...])
        dg_ref[...] += dg
        dm_ref[...] = _mx(dm)

    row = pl.BlockSpec((tm, D), lambda i: (i, 0))
    return pl.pallas_call(body, name=name, grid=(T // tm,), in_specs=[row, row, _full((1, D))],
                          out_specs=[row, _full((1, D))],
                          out_shape=[jax.ShapeDtypeStruct((T, D), MXU_DTYPE), jax.ShapeDtypeStruct((1, D), F32)],
                          compiler_params=_cp("arbitrary"))(dxo, m, g)


def _prenorm_bwd(dxo, dh, x, g, name):
    T, D = x.shape
    tm = _pick(T, 512, SUBLANES)

    def body(dxo_ref, dh_ref, x_ref, g_ref, dx_ref, dg_ref):
        @pl.when(pl.program_id(0) == 0)
        def _():
            dg_ref[...] = jnp.zeros_like(dg_ref)

        xv = x_ref[...]
        dx, dg = _rms_bwd(xv, _rms_stat(xv), g_ref[...], dh_ref[...])
        dg_ref[...] += dg
        dx_ref[...] = dxo_ref[...] + dx

    row = pl.BlockSpec((tm, D), lambda i: (i, 0))
    return pl.pallas_call(body, name=name, grid=(T // tm,), in_specs=[row, row, row, _full((1, D))],
                          out_specs=[row, _full((1, D))],
                          out_shape=[jax.ShapeDtypeStruct((T, D), F32), jax.ShapeDtypeStruct((1, D), F32)],
                          compiler_params=_cp("arbitrary"))(dxo, dh, x, g)


def _loss_fwd_bwd(y, target, name):
    T, D = y.shape
    tm = _pick(T, 512, SUBLANES)

    def body(y_ref, t_ref, l_ref, dy_ref):
        @pl.when(pl.program_id(0) == 0)
        def _():
            l_ref[...] = jnp.zeros_like(l_ref)

        e = y_ref[...] - t_ref[...]
        dy_ref[...] = e * (1.0 / D)
        l_ref[...] += 0.5 * jnp.sum(jnp.mean(e * e, axis=-1, keepdims=True), axis=0, keepdims=True)

    row = pl.BlockSpec((tm, D), lambda i: (i, 0))
    return pl.pallas_call(body, name=name, grid=(T // tm,), in_specs=[row, row],
                          out_specs=[_full((SUBLANES, LANES)), row],
                          out_shape=[jax.ShapeDtypeStruct((SUBLANES, LANES), F32), jax.ShapeDtypeStruct((T, D), F32)],
                          compiler_params=_cp("arbitrary"))(y, target)


DN_ROWS = 512


def _shift_down(prev8, cur, s):
    n = cur.shape[0]
    xx = jnp.concatenate([prev8, cur], axis=0)
    return pltpu.roll(xx, s, 0)[SUBLANES:SUBLANES + n, :]


def _shift_up(cur, next8, s):
    n = cur.shape[0]
    xx = jnp.concatenate([cur, next8], axis=0)
    return pltpu.roll(xx, n + SUBLANES - s, 0)[:n, :]


def _conv_tile(x_ref, w, r, rows):
    start = pl.multiple_of(r * rows, SUBLANES)
    cur = x_ref[pl.ds(start, rows), :]
    pstart = pl.multiple_of(jnp.maximum(start - SUBLANES, 0), SUBLANES)
    prev8 = jnp.where(r == 0, 0.0, x_ref[pl.ds(pstart, SUBLANES), :])
    taps = [_shift_down(prev8, cur, DN_CONV - 1 - j) if j < DN_CONV - 1 else cur for j in range(DN_CONV)]
    c = taps[0] * w[0:1, :]
    for j in range(1, DN_CONV):
        c = c + taps[j] * w[j:j + 1, :]
    return c, taps


def _dn_prep_fwd(proj, conv_w, name):
    T = proj.shape[0]
    W = DN_HEADS * DN_HEAD_DIM
    rows = min(DN_ROWS, T)
    n_inner = T // rows
    scale = DN_HEAD_DIM ** -0.5

    def body(x_ref, w_ref, o_ref):
        cb = pl.program_id(0)
        w = w_ref[...]
        is_qk = cb < 2 * DN_HEADS
        post = jnp.where(cb < DN_HEADS, scale, 1.0)

        def step(r, carry):
            c, _ = _conv_tile(x_ref, w, r, rows)
            s = c * _sigmoid(c)
            rinv = lax.rsqrt(jnp.sum(s * s, axis=-1, keepdims=True) + L2_EPS)
            o_ref[pl.ds(pl.multiple_of(r * rows, SUBLANES), rows), :] = jnp.where(is_qk, s * rinv * post, s)
            return carry

        lax.fori_loop(0, n_inner, step, 0)

    col = pl.BlockSpec((T, LANES), lambda j: (0, j))
    return pl.pallas_call(body, name=name, grid=(3 * W // LANES,),
                          in_specs=[col, pl.BlockSpec((DN_CONV, LANES), lambda j: (0, j))], out_specs=col,
                          out_shape=jax.ShapeDtypeStruct((T, 3 * W), F32), compiler_params=_cp("parallel"))(proj, conv_w)


def _dn_prep_bwd(proj, conv_w, dqkv, name):
    T = proj.shape[0]
    W = DN_HEADS * DN_HEAD_DIM
    rows = min(DN_ROWS, T)
    n_inner = T // rows
    scale = DN_HEAD_DIM ** -0.5

    def body(x_ref, w_ref, dy_ref, dx_ref, dw_ref, dc_scr):
        cb = pl.program_id(0)
        w = w_ref[...]
        is_qk = cb < 2 * DN_HEADS
        post = jnp.where(cb < DN_HEADS, scale, 1.0)

        def step1(r, dws):
            c, taps = _conv_tile(x_ref, w, r, rows)
            sg = _sigmoid(c)
            s = c * sg
            rinv = lax.rsqrt(jnp.sum(s * s, axis=-1, keepdims=True) + L2_EPS)
            dy = dy_ref[pl.ds(pl.multiple_of(r * rows, SUBLANES), rows), :]
            yn = s * rinv
            dyn = dy * post
            ds_qk = rinv * (dyn - yn * jnp.sum(dyn * yn, axis=-1, keepdims=True))
            ds = jnp.where(is_qk, ds_qk, dy)
            dc = ds * (sg * (1.0 + c * (1.0 - sg)))
            dc_scr[pl.ds(pl.multiple_of(r * rows, SUBLANES), rows), :] = dc
            return tuple(dws[j] + jnp.sum(dc * taps[j], axis=0, keepdims=True) for j in range(DN_CONV))

        zero = jnp.zeros((1, LANES), F32)
        dws = lax.fori_loop(0, n_inner, step1, (zero,) * DN_CONV)
        for j in range(DN_CONV):
            dw_ref[j:j + 1, :] = dws[j]

        def step2(r, carry):
            start = pl.multiple_of(r * rows, SUBLANES)
            cur = dc_scr[pl.ds(start, rows), :]
            nstart = pl.multiple_of(jnp.minimum(start + rows, T - SUBLANES), SUBLANES)
            next8 = jnp.where(r == n_inner - 1, 0.0, dc_scr[pl.ds(nstart, SUBLANES), :])
            dx = cur * w[DN_CONV - 1:DN_CONV, :]
            for j in range(DN_CONV - 1):
                dx = dx + _shift_up(cur, next8, DN_CONV - 1 - j) * w[j:j + 1, :]
            dx_ref[pl.ds(start, rows), :] = _mx(dx)
            return carry

        lax.fori_loop(0, n_inner, step2, 0)

    col = pl.BlockSpec((T, LANES), lambda j: (0, j))
    wspec = pl.BlockSpec((DN_CONV, LANES), lambda j: (0, j))
    return pl.pallas_call(body, name=name, grid=(3 * W // LANES,), in_specs=[col, wspec, col], out_specs=[col, wspec],
                          out_shape=[jax.ShapeDtypeStruct((T, 3 * W), MXU_DTYPE), jax.ShapeDtypeStruct((DN_CONV, 3 * W), F32)],
                          scratch_shapes=[pltpu.VMEM((T, LANES), F32)], compiler_params=_cp("parallel"))(proj, conv_w, dqkv)


def _softplus(x):
    return jnp.maximum(x, 0.0) + jnp.log(1.0 + jnp.exp(-jnp.abs(x)))


def _dn_gate_fwd(ba, a_log, dt_bias, name):
    T = ba.shape[0]
    tm = _pick(T, 1024, SUBLANES)

    def body(ba_ref, al_ref, dt_ref, beta_ref, g_ref):
        beta_ref[...] = _sigmoid(ba_ref[:, :LANES])
        g_ref[...] = -jnp.exp(al_ref[...]) * _softplus(ba_ref[:, LANES:] + dt_ref[...])

    row = lambda w: pl.BlockSpec((tm, w), lambda i: (i, 0))
    return pl.pallas_call(body, name=name, grid=(T // tm,), in_specs=[row(2 * LANES), _full((1, LANES)), _full((1, LANES))],
                          out_specs=[row(LANES), row(LANES)],
                          out_shape=[jax.ShapeDtypeStruct((T, LANES), F32)] * 2, compiler_params=_cp("parallel"))(ba, a_log, dt_bias)


def _dn_gate_bwd(ba, a_log, dt_bias, dbeta, dg, name):
    T = ba.shape[0]
    tm = _pick(T, 1024, SUBLANES)

    def body(ba_ref, al_ref, dt_ref, dbeta_ref, dg_ref, dba_ref, dal_ref, ddt_ref):
        @pl.when(pl.program_id(0) == 0)
        def _():
            dal_ref[...] = jnp.zeros_like(dal_ref)
            ddt_ref[...] = jnp.zeros_like(ddt_ref)

        beta = _sigmoid(ba_ref[:, :LANES])
        dba_ref[:, :LANES] = _mx(dbeta_ref[...] * beta * (1.0 - beta))
        pre = ba_ref[:, LANES:] + dt_ref[...]
        ea = jnp.exp(al_ref[...])
        dgv = dg_ref[...]
        da = dgv * (-ea) * _sigmoid(pre)
        dba_ref[:, LANES:] = _mx(da)
        ddt_ref[...] += jnp.sum(da, axis=0, keepdims=True)
        dal_ref[...] += jnp.sum(dgv * (-ea) * _softplus(pre), axis=0, keepdims=True)

    row = lambda w: pl.BlockSpec((tm, w), lambda i: (i, 0))
    one = _full((1, LANES))
    return pl.pallas_call(body, name=name, grid=(T // tm,), in_specs=[row(2 * LANES), one, one, row(LANES), row(LANES)],
                          out_specs=[row(2 * LANES), one, one],
                          out_shape=[jax.ShapeDtypeStruct((T, 2 * LANES), MXU_DTYPE), jax.ShapeDtypeStruct((1, LANES), F32),
                                     jax.ShapeDtypeStruct((1, LANES), F32)],
                          compiler_params=_cp("arbitrary"))(ba, a_log, dt_bias, dbeta, dg)


def _tri(c, strict):
    i = lax.broadcasted_iota(jnp.int32, (c, c), 0)
    j = lax.broadcasted_iota(jnp.int32, (c, c), 1)
    return (i > j) if strict else (i >= j)


def _inv_unit_lower(ls):
    c = ls[0].shape[0]
    i = lax.broadcasted_iota(jnp.int32, (c, c), 0)
    j = lax.broadcasted_iota(jnp.int32, (c, c), 1)
    eye = jnp.where(i == j, 1.0, 0.0)
    facs = [[eye - l for l in ls]]
    cur = ls
    for _ in range(int(math.log2(c)) - 1):
        cur = [_dot(p, p, NN, TRI_PREC) for p in cur]
        facs.append([eye + p for p in cur])
    while len(facs) > 1:
        nxt = [[_dot(a, b, NN, TRI_PREC) for a, b in zip(facs[t], facs[t + 1])] for t in range(0, len(facs) - 1, 2)]
        if len(facs) % 2:
            nxt.append(facs[-1])
        facs = nxt
    return facs[0]


def _chunk_gates(g_blk):
    c = g_blk.shape[0]
    gcs = _dot(jnp.where(_tri(c, False), 1.0, 0.0), g_blk, NN, HI)
    return gcs, gcs.T


def _head_chunk(h, qh, kh, vh, beta_blk, gcs, gcs_t):
    c = qh.shape[0]
    incl = _tri(c, False)
    gc_col = gcs[:, h:h + 1]
    gc_row = gcs_t[h:h + 1, :]
    gc_last = gcs_t[h:h + 1, c - 1:c]
    dec = jnp.where(incl, jnp.exp(jnp.where(incl, gc_col - gc_row, 0.0)), 0.0)
    gam = jnp.exp(gc_col)
    rr = jnp.exp(gc_last - gc_col)
    gl = jnp.exp(gc_last)
    b = beta_blk[:, h:h + 1]
    kb = kh * b
    vb = vh * b
    kk = _dot(_mx(kb), _mx(kh), NT)
    lmat = jnp.where(_tri(c, True), kk * dec, 0.0)
    qk = _dot(_mx(qh), _mx(kh), NT)
    pmat = jnp.where(incl, qk * dec, 0.0)
    return dict(dec=dec, gam=gam, rr=rr, gl=gl, b=b, kb=kb, vb=vb, lmat=lmat, pmat=pmat)


def _dn_scan_fwd(qkv, beta, g, proj, norm_g, name):
    T = qkv.shape[0]
    C, H, Dh = DN_CHUNK, DN_HEADS, DN_HEAD_DIM
    W = H * Dh
    N = T // C

    def body(q_ref, k_ref, v_ref, beta_ref, g_ref, z_ref, ng_ref, og_ref, o_ref, tinv_ref, s_ref, state):
        @pl.when(pl.program_id(0) == 0)
        def _():
            state[...] = jnp.zeros_like(state)

        gcs, gcs_t = _chunk_gates(g_ref[...])
        beta_blk = beta_ref[...]
        ng = ng_ref[...]
        heads = range(H)
        cs = [slice(h * Dh, (h + 1) * Dh) for h in heads]
        qs = [_head_chunk(h, q_ref[:, cs[h]], k_ref[:, cs[h]], v_ref[:, cs[h]], beta_blk, gcs, gcs_t) for h in heads]
        tinvs = _inv_unit_lower([q["lmat"] for q in qs])
        for h in heads:
            tinv_ref[h] = tinvs[h]
        us = [_dot(tinvs[h], qs[h]["vb"], NN, TRI_PREC) for h in heads]
        ws = [_dot(tinvs[h], qs[h]["kb"] * qs[h]["gam"], NN, TRI_PREC) for h in heads]
        ss = [state[h] for h in heads]
        for h in heads:
            s_ref[0, h] = ss[h]
        sbs = [_mx(s) for s in ss]
        vnbs = [_mx(us[h] - _dot(_mx(ws[h]), sbs[h], NN)) for h in heads]
        os_ = [_dot(_mx(q_ref[:, cs[h]] * qs[h]["gam"]), sbs[h], NN) + _dot(_mx(qs[h]["pmat"]), vnbs[h], NN) for h in heads]
        for h in heads:
            state[h] = ss[h] * qs[h]["gl"] + _dot(_mx((k_ref[:, cs[h]] * qs[h]["rr"]).T), vnbs[h], NN)
        for h in heads:
            o = os_[h]
            o_ref[:, cs[h]] = o
            zh = z_ref[:, cs[h]]
            og_ref[:, cs[h]] = _mx(o * _rms_stat(o) * ng * (zh * _sigmoid(zh)))

    blk = lambda j: pl.BlockSpec((C, W), lambda n: (n, j))
    small = pl.BlockSpec((C, LANES), lambda n: (n, 0))
    return pl.pallas_call(
        body, name=name, grid=(N,),
        in_specs=[blk(0), blk(1), blk(2), small, small, blk(3), _full((1, Dh))],
        out_specs=[blk(0), blk(0), pl.BlockSpec((H, C, C), lambda n: (0, n, 0)),
                   pl.BlockSpec((1, H, Dh, Dh), lambda n: (n, 0, 0, 0))],
        out_shape=[jax.ShapeDtypeStruct((T, W), MXU_DTYPE), jax.ShapeDtypeStruct((T, W), F32),
                   jax.ShapeDtypeStruct((H, T, C), F32), jax.ShapeDtypeStruct((N, H, Dh, Dh), F32)],
        scratch_shapes=[pltpu.VMEM((H, Dh, Dh), F32)],
        compiler_params=_cp("arbitrary"),
    )(qkv, qkv, qkv, beta, g, proj, norm_g)


def _dn_scan_bwd(qkv, beta, g, proj, norm_g, o, tinv, s_all, dog, name):
    T = qkv.shape[0]
    C, H, Dh = DN_CHUNK, DN_HEADS, DN_HEAD_DIM
    W = H * Dh
    N = T // C

    def body(q_ref, k_ref, v_ref, beta_ref, g_ref, z_ref, ng_ref, o_ref, tinv_ref, s_ref, dog_ref,
             dqkv_ref, dbeta_ref, dg_ref, dz_ref, dng_ref, dstate):
        @pl.when(pl.program_id(0) == 0)
        def _():
            dstate[...] = jnp.zeros_like(dstate)
            dng_ref[...] = jnp.zeros_like(dng_ref)

        gcs, gcs_t = _chunk_gates(g_ref[...])
        beta_blk = beta_ref[...]
        ng = ng_ref[...]
        incl = _tri(C, False)
        strict = _tri(C, True)
        lane = lax.broadcasted_iota(jnp.int32, (C, LANES), 1)
        rowi = lax.broadcasted_iota(jnp.int32, (C, 1), 0)
        ones = jnp.ones((C, LANES), F32)
        dbeta_acc = jnp.zeros((C, LANES), F32)
        dgc_acc = jnp.zeros((C, LANES), F32)
        dng_acc = jnp.zeros((1, Dh), F32)
        heads = range(H)
        cs = [slice(h * Dh, (h + 1) * Dh) for h in heads]
        rsum = lambda t: jnp.sum(t, axis=1, keepdims=True)
        dobs = []
        for h in heads:
            oh, zh, dogh = o_ref[:, cs[h]], z_ref[:, cs[h]], dog_ref[:, cs[h]]
            rstat = _rms_stat(oh)
            sz = _sigmoid(zh)
            dz_ref[:, cs[h]] = _mx(dogh * (oh * rstat * ng) * (sz * (1.0 + zh * (1.0 - sz))))
            do, dng = _rms_bwd(oh, rstat, ng, dogh * (zh * sz))
            dng_acc = dng_acc + dng
            dobs.append(_mx(do))
        qs = [_head_chunk(h, q_ref[:, cs[h]], k_ref[:, cs[h]], v_ref[:, cs[h]], beta_blk, gcs, gcs_t) for h in heads]
        tms = [tinv_ref[h] for h in heads]
        us = [_dot(tms[h], qs[h]["vb"], NN, TRI_PREC) for h in heads]
        ws = [_dot(tms[h], qs[h]["kb"] * qs[h]["gam"], NN, TRI_PREC) for h in heads]
        ss = [s_ref[0, h] for h in heads]
        sbs = [_mx(s) for s in ss]
        wbs = [_mx(w) for w in ws]
        vnbs = [_mx(us[h] - _dot(wbs[h], sbs[h], NN)) for h in heads]
        dsns = [dstate[h] for h in heads]
        dsbs = [_mx(d) for d in dsns]
        dvnews = [_dot(_mx(qs[h]["pmat"]), dobs[h], TN) + _dot(_mx(k_ref[:, cs[h]] * qs[h]["rr"]), dsbs[h], NN) for h in heads]
        dvb16s = [_mx(d) for d in dvnews]
        dps = [jnp.where(incl, _dot(dobs[h], vnbs[h], NT), 0.0) for h in heads]
        dqds = [_dot(dobs[h], sbs[h], NT) for h in heads]
        dkds = [_dot(vnbs[h], dsbs[h], NT) for h in heads]
        dgls = [jnp.sum(rsum(ss[h] * dsns[h]), axis=0, keepdims=True) for h in heads]
        dws = [-_dot(dvb16s[h], sbs[h], NT) for h in heads]
        for h in heads:
            dstate[h] = (_dot(_mx(q_ref[:, cs[h]] * qs[h]["gam"]), dobs[h], TN) + qs[h]["gl"] * dsns[h]
                         - _dot(wbs[h], dvb16s[h], TN))
        dvbs = [_dot(tms[h], dvnews[h], TN, TRI_PREC) for h in heads]
        dkbgs = [_dot(tms[h], dws[h], TN, TRI_PREC) for h in heads]
        dls = [jnp.where(strict, -(_dot(dvbs[h], us[h], NT, TRI_PREC) + _dot(dkbgs[h], ws[h], NT, TRI_PREC)), 0.0)
               for h in heads]
        mmats = [dls[h] * qs[h]["lmat"] + dps[h] * qs[h]["pmat"] for h in heads]
        dgcs = [rsum(mmats[h]) - _dot(mmats[h], ones, TN, HI)[:, :1] for h in heads]
        dkk16s = [_mx(dls[h] * qs[h]["dec"]) for h in heads]
        dqk16s = [_mx(dps[h] * qs[h]["dec"]) for h in heads]
        for h in heads:
            q = qs[h]
            qh, kh, vh = q_ref[:, cs[h]], k_ref[:, cs[h]], v_ref[:, cs[h]]
            gam, rr, b, kb = q["gam"], q["rr"], q["b"], q["kb"]
            dkb = _dot(dkk16s[h], _mx(kh), NN) + dkbgs[h] * gam
            dk = _dot(dkk16s[h], _mx(kb), TN) + _dot(dqk16s[h], _mx(qh), TN) + dkb * b + dkds[h] * rr
            dq = _dot(dqk16s[h], _mx(kh), NN) + dqds[h] * gam
            dgam = rsum(dkbgs[h] * kb) + rsum(dqds[h] * qh)
            dr = rsum(dkds[h] * kh)
            dgc_last = jnp.sum(dr * rr, axis=0, keepdims=True) + dgls[h] * q["gl"]
            dgc = dgcs[h] + dgam * gam - dr * rr + jnp.where(rowi == C - 1, dgc_last, 0.0)
            dbeta = rsum(dvbs[h] * vh) + rsum(dkb * kh)
            dqkv_ref[:, cs[h]] = dq
            dqkv_ref[:, W + h * Dh:W + (h + 1) * Dh] = dk
            dqkv_ref[:, 2 * W + h * Dh:2 * W + (h + 1) * Dh] = dvbs[h] * b
            dbeta_acc = jnp.where(lane == h, dbeta, dbeta_acc)
            dgc_acc = jnp.where(lane == h, dgc, dgc_acc)
        dbeta_ref[...] = dbeta_acc
        dg_ref[...] = _dot(jnp.where(incl, 1.0, 0.0), dgc_acc, TN, HI)
        dng_ref[...] += dng_acc

    rev = lambda n: N - 1 - n
    blk = lambda j: pl.BlockSpec((C, W), lambda n: (rev(n), j))
    small = pl.BlockSpec((C, LANES), lambda n: (rev(n), 0))
    return pl.pallas_call(
        body, name=name, grid=(N,),
        in_specs=[blk(0), blk(1), blk(2), small, small, blk(3), _full((1, Dh)), blk(0),
                  pl.BlockSpec((H, C, C), lambda n: (0, rev(n), 0)),
                  pl.BlockSpec((1, H, Dh, Dh), lambda n: (rev(n), 0, 0, 0)), blk(0)],
        out_specs=[pl.BlockSpec((C, 3 * W), lambda n: (rev(n), 0)), small, small, blk(0), _full((1, Dh))],
        out_shape=[jax.ShapeDtypeStruct((T, 3 * W), F32), jax.ShapeDtypeStruct((T, LANES), F32),
                   jax.ShapeDtypeStruct((T, LANES), F32), jax.ShapeDtypeStruct((T, W), MXU_DTYPE),
                   jax.ShapeDtypeStruct((1, Dh), F32)],
        scratch_shapes=[pltpu.VMEM((H, Dh, Dh), F32)],
        compiler_params=_cp("arbitrary"),
    )(qkv, qkv, qkv, beta, g, proj, norm_g, o, tinv, s_all, dog)


_INV_SQRT2 = 0.7071067811865476
_INV_SQRT_2PI = 0.3989422804014327


def _sg_recompute(zp_ref, bin_ref, lng_ref, lnb_ref):
    E = SG_WIDTH
    zin = zp_ref[...] + bin_ref[...]
    cdf = 0.5 * (1.0 + lax.erf(zin * _INV_SQRT2))
    zz = zin * cdf
    u = zz[:, :E]
    vp = zz[:, E:]
    mu = jnp.mean(vp, axis=-1, keepdims=True)
    xc = vp - mu
    rstd = lax.rsqrt(jnp.mean(xc * xc, axis=-1, keepdims=True) + LN_EPS)
    xhat = xc * rstd
    v = xhat * lng_ref[...] + lnb_ref[...]
    return zin, cdf, u, xhat, rstd, v


def _sg_masked_ws(ws_ref, g):
    return _mx(jnp.where(_tri(SG_CHUNK, False), ws_ref[g], 0.0))


def _sg_fwd(zpre, b_in, ln_g, ln_b, w_s, b_s_t, name):
    T = zpre.shape[0]
    E, G, C, GW = SG_WIDTH, SG_GROUPS, SG_CHUNK, SG_GROUP_W

    def body(zp_ref, bin_ref, lng_ref, lnb_ref, ws_ref, bst_ref, um_ref):
        _, _, u, _, _, v = _sg_recompute(zp_ref, bin_ref, lng_ref, lnb_ref)
        bst = bst_ref[...]
        for g in range(G):
            cs = slice(g * GW, (g + 1) * GW)
            mixed = _dot(_sg_masked_ws(ws_ref, g), _mx(v[:, cs]), NN) + bst[:, g:g + 1]
            um_ref[:, cs] = _mx(u[:, cs] * mixed)

    return pl.pallas_call(
        body, name=name, grid=(T // C,),
        in_specs=[pl.BlockSpec((C, 2 * E), lambda n: (n, 0)), _full((1, 2 * E)), _full((1, E)), _full((1, E)),
                  _full((G, C, C)), _full((C, LANES))],
        out_specs=pl.BlockSpec((C, E), lambda n: (n, 0)),
        out_shape=jax.ShapeDtypeStruct((T, E), MXU_DTYPE), compiler_params=_cp("parallel"),
    )(zpre, b_in, ln_g, ln_b, w_s, b_s_t)


def _sg_bwd(zpre, b_in, ln_g, ln_b, w_s, b_s_t, dum, name):
    T = zpre.shape[0]
    E, G, C, GW = SG_WIDTH, SG_GROUPS, SG_CHUNK, SG_GROUP_W

    def body(zp_ref, bin_ref, lng_ref, lnb_ref, ws_ref, bst_ref, dum_ref,
             dz_ref, dbin_ref, dlng_ref, dlnb_ref, dws_ref, dbst_ref):
        @pl.when(pl.program_id(0) == 0)
        def _():
            for r in (dbin_ref, dlng_ref, dlnb_ref, dws_ref, dbst_ref):
                r[...] = jnp.zeros_like(r)

        zin, cdf, u, xhat, rstd, v = _sg_recompute(zp_ref, bin_ref, lng_ref, lnb_ref)
        bst = bst_ref[...]
        lane = lax.broadcasted_iota(jnp.int32, (C, LANES), 1)
        dum_v = dum_ref[...]
        dbst = jnp.zeros((C, LANES), F32)
        du_parts, dv_parts = [], []
        for g in range(G):
            cs = slice(g * GW, (g + 1) * GW)
            wsm = _sg_masked_ws(ws_ref, g)
            vg = _mx(v[:, cs])
            mixed = _dot(wsm, vg, NN) + bst[:, g:g + 1]
            dumg = dum_v[:, cs]
            du_parts.append(dumg * mixed)
            dmixed = dumg * u[:, cs]
            dmb = _mx(dmixed)
            dv_parts.append(_dot(wsm, dmb, TN))
            dws_ref[g] += _dot(dmb, vg, NT)
            dbst = jnp.where(lane == g, jnp.sum(dmixed, axis=1, keepdims=True), dbst)
        dbst_ref[...] += dbst
        du = jnp.concatenate(du_parts, axis=1)
        dv = jnp.concatenate(dv_parts, axis=1)
        dlng_ref[...] += jnp.sum(dv * xhat, axis=0, keepdims=True)
        dlnb_ref[...] += jnp.sum(dv, axis=0, keepdims=True)
        dxh = dv * lng_ref[...]
        dvp = rstd * (dxh - jnp.mean(dxh, axis=-1, keepdims=True) - xhat * jnp.mean(dxh * xhat, axis=-1, keepdims=True))
        dzz = jnp.concatenate([du, dvp], axis=1)
        dzin = dzz * (cdf + zin * (_INV_SQRT_2PI * jnp.exp(-0.5 * zin * zin)))
        dz_ref[...] = _mx(dzin)
        dbin_ref[...] += jnp.sum(dzin, axis=0, keepdims=True)

    return pl.pallas_call(
        body, name=name, grid=(T // C,),
        in_specs=[pl.BlockSpec((C, 2 * E), lambda n: (n, 0)), _full((1, 2 * E)), _full((1, E)), _full((1, E)),
                  _full((G, C, C)), _full((C, LANES)), pl.BlockSpec((C, E), lambda n: (n, 0))],
        out_specs=[pl.BlockSpec((C, 2 * E), lambda n: (n, 0)), _full((1, 2 * E)), _full((1, E)), _full((1, E)),
                   _full((G, C, C)), _full((C, LANES))],
        out_shape=[jax.ShapeDtypeStruct((T, 2 * E), MXU_DTYPE), jax.ShapeDtypeStruct((1, 2 * E), F32),
                   jax.ShapeDtypeStruct((1, E), F32), jax.ShapeDtypeStruct((1, E), F32),
                   jax.ShapeDtypeStruct((G, C, C), F32), jax.ShapeDtypeStruct((C, LANES), F32)],
        compiler_params=_cp("arbitrary"),
    )(zpre, b_in, ln_g, ln_b, w_s, b_s_t, dum)


def _row(v):
    return v.reshape(1, -1)


def _pad_lanes(v):
    v = v.reshape(1, -1)
    return jnp.pad(v, ((0, 0), (0, LANES - v.shape[1])))


def _local_step(x, target, p):
    ng = p["norm_g"]
    grads = {}
    dng = [[None] * 6 for _ in range(2)]
    saved = []

    def ffn_f(xin, i, j, tag):
        xo, h, gu, y = _ffn_fwd(xin, _row(ng[i, 4 * j]), _row(ng[i, 4 * j + 1]), p["wgu"][i][j], p["wd"][i][j], "ffn_fwd_" + tag)
        return xo, (xin, h, gu, y)

    x1, sv_f00 = ffn_f(x, 0, 0, "00")
    hn0 = _norm_fwd(x1, _row(ng[0, 2]), "dn_prenorm")
    proj = _mm(hn0, p["dn_wqkvz"], "nn", "dn_proj")
    ba = _mm(hn0, p["dn_wba"], "nn", "dn_proj_ba")
    a_log = _pad_lanes(p["dn_a_log"])
    dt_bias = _pad_lanes(p["dn_dt_bias"])
    dn_ng = _row(p["dn_norm_g"])
    qkv = _dn_prep_fwd(proj, p["dn_conv_w"], "dn_prep_fwd")
    beta, gdec = _dn_gate_fwd(ba, a_log, dt_bias, "dn_gate_fwd")
    og, o_raw, tinv, s_all = _dn_scan_fwd(qkv, beta, gdec, proj, dn_ng, "dn_scan_fwd")
    m0 = _mm(og, p["dn_wout"], "nn", "dn_out")
    x2 = _postnorm_fwd(x1, m0, _row(ng[0, 3]), "dn_postnorm")
    x3, sv_f01 = ffn_f(x2, 0, 1, "01")
    x4, sv_f10 = ffn_f(x3, 1, 0, "10")
    hn1 = _norm_fwd(x4, _row(ng[1, 2]), "sg_prenorm")
    zpre = _mm(hn1, p["sg_win"], "nn", "sg_proj")
    sg_bin = _row(p["sg_b_in"])
    sg_lng = _row(p["sg_ln_g"])
    sg_lnb = _row(p["sg_ln_b"])
    sg_bst = jnp.pad(p["sg_b_s"].T, ((0, 0), (0, LANES - SG_GROUPS)))
    um = _sg_fwd(zpre, sg_bin, sg_lng, sg_lnb, p["sg_w_s"], sg_bst, "sg_fwd")
    m1 = _mm(um, p["sg_wout"], "nn", "sg_out")
    x5 = _postnorm_fwd(x4, m1, _row(ng[1, 3]), "sg_postnorm")
    x6, sv_f11 = ffn_f(x5, 1, 1, "11")
    loss_part, dx = _loss_fwd_bwd(x6, target, "loss")

    def ffn_b(dxo, sv, i, j, tag):
        xin, h, gu, y = sv
        dy, a, dgu, dg1 = _ffn_bwd_down(dxo, y, gu, _row(ng[i, 4 * j + 1]), p["wd"][i][j], "ffn_bwd_down_" + tag)
        grads["wd%d%d" % (i, j)] = _mm(a, dy, "tn", "ffn_wgrad_down_" + tag)
        grads["wguT%d%d" % (i, j)] = _mm(dgu, h, "tn", "ffn_wgrad_up_" + tag)
        dxi, dg0 = _ffn_bwd_up(dgu, xin, dxo, _row(ng[i, 4 * j]), p["wgu"][i][j], "ffn_bwd_up_" + tag)
        dng[i][4 * j] = dg0
        dng[i][4 * j + 1] = dg1
        return dxi

    dx = ffn_b(dx, sv_f11, 1, 1, "11")
    dm1, dng[1][3] = _postnorm_bwd(dx, m1, _row(ng[1, 3]), "sg_postnorm_bwd")
    grads["sg_w_out"] = _mm(um, dm1, "tn", "sg_wgrad_out")
    dum = _mm(dm1, p["sg_wout"], "nt", "sg_dgrad_out")
    dz1, dbin, dlng, dlnb, dws, dbst = _sg_bwd(zpre, sg_bin, sg_lng, sg_lnb, p["sg_w_s"], sg_bst, dum, "sg_bwd")
    grads["sg_w_inT"] = _mm(dz1, hn1, "tn", "sg_wgrad_in")
    dh1 = _mm(dz1, p["sg_win"], "nt", "sg_dgrad_in")
    dx, dng[1][2] = _prenorm_bwd(dx, dh1, x4, _row(ng[1, 2]), "sg_prenorm_bwd")
    grads["sg_b_in"] = dbin.reshape(1, -1)
    grads["sg_ln_g"] = dlng.reshape(1, -1)
    grads["sg_ln_b"] = dlnb.reshape(1, -1)
    grads["sg_w_s"] = jnp.where(jnp.tril(jnp.ones((SG_CHUNK, SG_CHUNK), bool)), dws, 0.0)[None]
    grads["sg_b_s"] = dbst[:, :SG_GROUPS].T[None]
    dx = ffn_b(dx, sv_f10, 1, 0, "10")
    dx = ffn_b(dx, sv_f01, 0, 1, "01")
    dm0, dng[0][3] = _postnorm_bwd(dx, m0, _row(ng[0, 3]), "dn_postnorm_bwd")
    grads["dn_w_out"] = _mm(og, dm0, "tn", "dn_wgrad_out")
    dog = _mm(dm0, p["dn_wout"], "nt", "dn_dgrad_out")
    dqkv, dbeta, dgdec, dz0, dnng = _dn_scan_bwd(qkv, beta, gdec, proj, dn_ng, o_raw, tinv, s_all, dog, "dn_scan_bwd")
    dqkv_pre, dconv = _dn_prep_bwd(proj, p["dn_conv_w"], dqkv, "dn_prep_bwd")
    dba, dal, ddt = _dn_gate_bwd(ba, a_log, dt_bias, dbeta, dgdec, "dn_gate_bwd")
    W3 = 3 * DN_HEADS * DN_HEAD_DIM
    dw_qkv = _mm(hn0, dqkv_pre, "tn", "dn_wgrad_qkv")
    dw_z = _mm(hn0, dz0, "tn", "dn_wgrad_z")
    dw_ba = _mm(hn0, dba, "tn", "dn_wgrad_ba")
    grads["dn_w_in"] = jnp.concatenate(
        [dw_qkv, dw_z, dw_ba[:, :DN_HEADS], dw_ba[:, LANES:LANES + DN_HEADS]], axis=1)
    dh0 = _mm(dqkv_pre, p["dn_wqkvz"][:, :W3], "nt", "dn_dgrad_qkv")
    dh0 = _mm(dz0, p["dn_wqkvz"][:, W3:], "nt", "dn_dgrad_z", add=dh0)
    dh0 = _mm(dba, p["dn_wba"], "nt", "dn_dgrad_ba", add=dh0)
    dx, dng[0][2] = _prenorm_bwd(dx, dh0, x1, _row(ng[0, 2]), "dn_prenorm_bwd")
    grads["dn_conv_w"] = dconv[None]
    grads["dn_a_log"] = dal[:, :DN_HEADS]
    grads["dn_dt_bias"] = ddt[:, :DN_HEADS]
    grads["dn_norm_g"] = dnng
    dx = ffn_b(dx, sv_f00, 0, 0, "00")
    grads["norm_g"] = jnp.stack([jnp.concatenate(dng[i], axis=0) for i in range(2)])
    return loss_part, dx, grads


def _mesh_pos():
    return lax.axis_index("x"), lax.axis_index("y"), lax.axis_index("c")


def _other_chips(x, y):
    return [(1 - x, y), (x, 1 - y), (1 - x, 1 - y)]


def _allgather_chips(arrs, name):
    n = len(arrs)

    def body(*refs):
        ins, outs = refs[:n], refs[n:2 * n]
        ici_send, ici_recv, d2d_send, d2d_recv = refs[2 * n:]
        x, y, c = _mesh_pos()
        me = 2 * x + y
        chips = _other_chips(x, y)
        sibling = (x, y, 1 - c)

        def ici(i, j, k):
            cx, cy = chips[j]
            return pltpu.make_async_remote_copy(src_ref=ins[i].at[c], dst_ref=outs[i].at[k, c], send_sem=ici_send.at[3 * i + j],
                                                recv_sem=ici_recv.at[3 * i + j], device_id=(cx, cy, c), device_id_type=MESH)

        def d2d(i, j, h):
            cx, cy = chips[j]
            slot = outs[i].at[2 * cx + cy, h]
            return pltpu.make_async_remote_copy(src_ref=slot, dst_ref=slot, send_sem=d2d_send.at[3 * i + j],
                                                recv_sem=d2d_recv.at[3 * i + j], device_id=sibling, device_id_type=MESH)

        sends = [ici(i, j, me) for i in range(n) for j in range(3)]
        for cp in sends:
            cp.start()
        for i in range(n):
            for j, (cx, cy) in enumerate(chips):
                ici(i, j, 2 * cx + cy).wait_recv()
                fwd = d2d(i, j, c)
                fwd.start()
                sends.append(fwd)
        for i in range(n):
            for j in range(3):
                d2d(i, j, 1 - c).wait_recv()
        for cp in sends:
            cp.wait_send()

    return pl.pallas_call(
        body, name=name, in_specs=[ANY] * n, out_specs=[ANY] * n,
        out_shape=[jax.ShapeDtypeStruct((N_CHIPS,) + a.shape, a.dtype) for a in arrs],
        scratch_shapes=[pltpu.SemaphoreType.DMA((3 * n,))] * 4,
    )(*arrs)


def _swap_halves(arrs, half_first, name):
    n = len(arrs)

    def body(*refs):
        ins, outs = refs[:n], refs[n:2 * n]
        send_sems, recv_sems = refs[2 * n:]
        x, y, c = _mesh_pos()
        cps = [pltpu.make_async_remote_copy(src_ref=ins[i].at[1 - c] if half_first[i] else ins[i].at[:, 1 - c],
                                            dst_ref=outs[i], send_sem=send_sems.at[i], recv_sem=recv_sems.at[i],
                                            device_id=(x, y, 1 - c), device_id_type=MESH)
               for i in range(n)]
        for cp in cps:
            cp.start()
        for cp in cps:
            cp.wait()

    return pl.pallas_call(
        body, name=name, in_specs=[ANY] * n, out_specs=[ANY] * n,
        out_shape=[jax.ShapeDtypeStruct((N_CHIPS,) + a.shape[2:], a.dtype) for a in arrs],
        scratch_shapes=[pltpu.SemaphoreType.DMA((n,)), pltpu.SemaphoreType.DMA((n,))],
    )(*arrs)


def _scatter_chips(arrs, name):
    n = len(arrs)

    def body(*refs):
        ins, outs = refs[:n], refs[n:2 * n]
        send_sems, recv_sems = refs[2 * n:]
        x, y, c = _mesh_pos()
        me = 2 * x + y
        chips = _other_chips(x, y)

        def copy(i, j, src_k, dst_k):
            cx, cy = chips[j]
            return pltpu.make_async_remote_copy(src_ref=ins[i].at[src_k], dst_ref=outs[i].at[dst_k],
                                                send_sem=send_sems.at[3 * i + j], recv_sem=recv_sems.at[3 * i + j],
                                                device_id=(cx, cy, c), device_id_type=MESH)

        sends = [copy(i, j, 2 * chips[j][0] + chips[j][1], me) for i in range(n) for j in range(3)]
        for cp in sends:
            cp.start()
        for i in range(n):
            for j, (cx, cy) in enumerate(chips):
                copy(i, j, me, 2 * cx + cy).wait_recv()
        for cp in sends:
            cp.wait_send()

    return pl.pallas_call(
        body, name=name, in_specs=[ANY] * n, out_specs=[ANY] * n,
        out_shape=[jax.ShapeDtypeStruct(a.shape, a.dtype) for a in arrs],
        scratch_shapes=[pltpu.SemaphoreType.DMA((3 * n,)), pltpu.SemaphoreType.DMA((3 * n,))],
    )(*arrs)


def _swap_whole(arrs, name):
    n = len(arrs)

    def body(*refs):
        ins, outs = refs[:n], refs[n:2 * n]
        send_sems, recv_sems = refs[2 * n:]
        x, y, c = _mesh_pos()
        cps = [pltpu.make_async_remote_copy(src_ref=ins[i], dst_ref=outs[i], send_sem=send_sems.at[i],
                                            recv_sem=recv_sems.at[i], device_id=(x, y, 1 - c), device_id_type=MESH)
               for i in range(n)]
        for cp in cps:
            cp.start()
        for cp in cps:
            cp.wait()

    return pl.pallas_call(
        body, name=name, in_specs=[ANY] * n, out_specs=[ANY] * n,
        out_shape=[jax.ShapeDtypeStruct(a.shape, a.dtype) for a in arrs],
        scratch_shapes=[pltpu.SemaphoreType.DMA((n,)), pltpu.SemaphoreType.DMA((n,))],
    )(*arrs)


def _allgather_devices(a, name):
    masks = [(mx, my, mc) for mx in (0, 1) for my in (0, 1) for mc in (0, 1)][1:]

    def body(in_ref, out_ref, send_sems, recv_sems, loc_sem):
        x, y, c = _mesh_pos()
        me = 4 * x + 2 * y + c
        lc = pltpu.make_async_copy(in_ref, out_ref.at[me], loc_sem.at[0])
        lc.start()
        peers = [(jnp.where(mx, 1 - x, x), jnp.where(my, 1 - y, y), jnp.where(mc, 1 - c, c)) for mx, my, mc in masks]
        cps = [pltpu.make_async_remote_copy(src_ref=in_ref, dst_ref=out_ref.at[me], send_sem=send_sems.at[j],
                                            recv_sem=recv_sems.at[j], device_id=peers[j], device_id_type=MESH)
               for j in range(len(masks))]
        for cp in cps:
            cp.start()
        for j, (px, py, pc) in enumerate(peers):
            pltpu.make_async_remote_copy(src_ref=in_ref, dst_ref=out_ref.at[4 * px + 2 * py + pc], send_sem=send_sems.at[j],
                                         recv_sem=recv_sems.at[j], device_id=peers[j], device_id_type=MESH).wait_recv()
        for cp in cps:
            cp.wait_send()
        lc.wait()

    return pl.pallas_call(
        body, name=name, in_specs=[ANY], out_specs=ANY,
        out_shape=jax.ShapeDtypeStruct((N_DEV,) + a.shape, a.dtype),
        scratch_shapes=[pltpu.SemaphoreType.DMA((N_DEV - 1,)), pltpu.SemaphoreType.DMA((N_DEV - 1,)),
                        pltpu.SemaphoreType.DMA((1,))],
    )(a)


def _as_rows(a, lead):
    shp = a.shape
    rows = 1
    for s in shp[lead:-1]:
        rows *= s
    return a.reshape(shp[:lead] + (rows, shp[-1]))


def _row_tile(rows, cols, n_bufs):
    budget = (24 * 1024 * 1024) // (n_bufs * 2 * 4 * cols)
    return _pick(rows, max(2 * SUBLANES, budget), 2 * SUBLANES)


def _sum_leading(a, name):
    n = a.shape[0]
    v = _as_rows(a, 1)
    _, rows, cols = v.shape
    tr = _row_tile(rows, cols, n + 1)

    def body(a_ref, o_ref):
        acc = a_ref[0]
        for k in range(1, n):
            acc = acc + a_ref[k]
        o_ref[...] = acc

    out = pl.pallas_call(body, name=name, grid=(rows // tr,),
                         in_specs=[pl.BlockSpec((n, tr, cols), lambda i: (0, i, 0))],
                         out_specs=pl.BlockSpec((tr, cols), lambda i: (i, 0)),
                         out_shape=jax.ShapeDtypeStruct((rows, cols), F32), compiler_params=_cp("parallel"))(v)
    return out.reshape(a.shape[1:])


def _scalar(i):
    return jnp.reshape(i, (1,)).astype(jnp.int32)


def _add_own_half(g, other, c, half_first, name):
    _, rows, cols = other.shape
    tr = _row_tile(rows, cols, 3)

    def body(c_ref, g_ref, o_ref, out_ref):
        out_ref[0] = (g_ref[0, 0] + o_ref[0]).astype(out_ref.dtype)

    if half_first:
        g_map = lambda k, i, c_ref: (c_ref[0], k, i, 0)
    else:
        g_map = lambda k, i, c_ref: (k, c_ref[0], i, 0)
    flat = pl.BlockSpec((1, tr, cols), lambda k, i, c_ref: (k, i, 0))
    return pl.pallas_call(
        body, name=name,
        grid_spec=pltpu.PrefetchScalarGridSpec(
            num_scalar_prefetch=1, grid=(N_CHIPS, rows // tr),
            in_specs=[pl.BlockSpec((1, 1, tr, cols), g_map), flat], out_specs=flat),
        out_shape=jax.ShapeDtypeStruct(other.shape, COMM_DTYPE), compiler_params=_cp("parallel", "parallel"),
    )(_scalar(c), g, other)


def _sum_chips(own, got, chip, name):
    pv = _as_rows(own, 1)
    bv = _as_rows(got, 1)
    _, rows, cols = pv.shape
    tr = _row_tile(rows, cols, N_CHIPS + 2)

    def body(chip_ref, p_ref, b_ref, o_ref):
        mine = p_ref[0].astype(F32)
        acc = jnp.where(chip_ref[0] == 0, mine, b_ref[0].astype(F32))
        for k in range(1, N_CHIPS):
            acc = acc + jnp.where(chip_ref[0] == k, mine, b_ref[k].astype(F32))
        o_ref[...] = acc

    out = pl.pallas_call(
        body, name=name,
        grid_spec=pltpu.PrefetchScalarGridSpec(
            num_scalar_prefetch=1, grid=(rows // tr,),
            in_specs=[pl.BlockSpec((1, tr, cols), lambda i, k_ref: (k_ref[0], i, 0)),
                      pl.BlockSpec((N_CHIPS, tr, cols), lambda i, k_ref: (0, i, 0))],
            out_specs=pl.BlockSpec((tr, cols), lambda i, k_ref: (i, 0))),
        out_shape=jax.ShapeDtypeStruct((rows, cols), F32), compiler_params=_cp("parallel"),
    )(_scalar(chip), pv, bv)
    return out.reshape(own.shape[1:])


def _adam_math(w, g, m, v):
    nm = ADAM_B1 * m + (1.0 - ADAM_B1) * g
    nv = ADAM_B2 * v + (1.0 - ADAM_B2) * (g * g)
    m_hat = nm / (1.0 - ADAM_B1 ** ADAM_STEP)
    v_hat = nv / (1.0 - ADAM_B2 ** ADAM_STEP)
    return -ADAM_LR * (m_hat / (jnp.sqrt(v_hat) + ADAM_EPS) + ADAM_WD * w), nm, nv


def _adamw_halves(w, mine, theirs, m, v, c, name):
    shape = w.shape
    ws, ms, vs = (_as_rows(t.reshape((2, -1) + t.shape[-1:]), 1) for t in (w, m, v))
    a, b = _as_rows(mine, 0), _as_rows(theirs, 0)
    rows, cols = a.shape
    tr = _row_tile(rows, cols, 9)

    def body(c_ref, w_ref, a_ref, b_ref, m_ref, v_ref, g_ref, d_ref, nm_ref, nv_ref):
        gv = jnp.where(pl.program_id(0) == c_ref[0], a_ref[...], b_ref[...])
        g_ref[0] = gv
        d_ref[0], nm_ref[0], nv_ref[0] = _adam_math(w_ref[0], gv, m_ref[0], v_ref[0])

    half = pl.BlockSpec((1, tr, cols), lambda h, i, c_ref: (h, i, 0))
    flat = pl.BlockSpec((tr, cols), lambda h, i, c_ref: (i, 0))
    outs = pl.pallas_call(
        body, name=name,
        grid_spec=pltpu.PrefetchScalarGridSpec(num_scalar_prefetch=1, grid=(2, rows // tr),
                                               in_specs=[half, flat, flat, half, half], out_specs=[half] * 4),
        out_shape=[jax.ShapeDtypeStruct((2, rows, cols), F32)] * 4, compiler_params=_cp("parallel", "parallel"),
    )(_scalar(c), ws, a, b, ms, vs)
    return tuple(o.reshape(shape) for o in outs)


def _adamw(w, g, m, v, name):
    shape = w.shape
    ws, gs, ms, vs = (_as_rows(t, 0) for t in (w, g, m, v))
    rows, cols = ws.shape
    tr = _row_tile(rows, cols, 7)

    def body(w_ref, g_ref, m_ref, v_ref, d_ref, nm_ref, nv_ref):
        d_ref[...], nm_ref[...], nv_ref[...] = _adam_math(w_ref[...], g_ref[...], m_ref[...], v_ref[...])

    spec = pl.BlockSpec((tr, cols), lambda i: (i, 0))
    outs = pl.pallas_call(body, name=name, grid=(rows // tr,), in_specs=[spec] * 4, out_specs=[spec] * 3,
                          out_shape=[jax.ShapeDtypeStruct((rows, cols), F32)] * 3, compiler_params=_cp("parallel"))(ws, gs, ms, vs)
    return tuple(o.reshape(shape) for o in outs)


_BIG = ["ffn_w_gate", "ffn_w_up", "ffn_w_down", "dn_w_in", "dn_w_out", "sg_w_in", "sg_w_out"]
_SMALL_SHARDED = ["norm_g", "dn_conv_w", "sg_b_in", "sg_ln_g", "sg_ln_b"]
_SMALL_REPL = ["dn_a_log", "dn_dt_bias", "dn_norm_g", "sg_w_s", "sg_b_s"]
_WEIGHTS = ["norm_g", "ffn_w_gate", "ffn_w_up", "ffn_w_down", "dn_w_in", "dn_conv_w", "dn_a_log", "dn_dt_bias",
            "dn_norm_g", "dn_w_out", "sg_w_in", "sg_b_in", "sg_ln_g", "sg_ln_b", "sg_w_s", "sg_b_s", "sg_w_out"]
PACK_COLS = 1024


def _pack(arrs):
    flat = jnp.concatenate([a.reshape(-1) for a in arrs])
    pad = (-flat.shape[0]) % (SUBLANES * PACK_COLS)
    return jnp.pad(flat, (0, pad)).reshape(-1, PACK_COLS)


def _unpack(buf, shapes):
    flat = buf.reshape(-1)
    out, off = [], 0
    for s in shapes:
        n = math.prod(s)
        out.append(flat[off:off + n].reshape(s))
        off += n
    return out


def _as_halves(a):
    if a.shape[0] == 2:
        return a
    if a.shape[0] == 1:
        return a.reshape((2, a.shape[1] // 2) + a.shape[2:])
    return a.reshape((2, a.shape[0] // 2) + a.shape[1:])


def _with_own(gathered, own, chip):
    g = gathered.reshape((N_CHIPS,) + own.shape)
    return [jnp.where(chip == k, own, g[k]) for k in range(N_CHIPS)]


def _cat_shards(g, axis):
    return jnp.concatenate(list(g), axis=axis)


def _full_weights(w, gathered, small_full):
    gate, up, down, dn_in, dn_out, sg_in, sg_out = gathered
    sel = lambda g, *idx: [s[idx] for s in g]
    p = dict(small_full)
    p["wgu"] = [[jnp.concatenate([_cat_shards(sel(gate, i, j), 1), _cat_shards(sel(up, i, j), 1)], axis=1) for j in range(2)]
                for i in range(2)]
    p["wd"] = [[_cat_shards(sel(down, i, j), 0) for j in range(2)] for i in range(2)]
    dn_in, dn_out, sg_in, sg_out = (sel(g, 0) for g in (dn_in, dn_out, sg_in, sg_out))
    dn_full = _cat_shards(dn_in, 1)
    W4 = 4 * DN_HEADS * DN_HEAD_DIM
    p["dn_wqkvz"] = dn_full[:, :W4]
    wba = jnp.zeros((D_MODEL, 2 * LANES), dn_full.dtype)
    wba = wba.at[:, :DN_HEADS].set(dn_full[:, W4:W4 + DN_HEADS])
    wba = wba.at[:, LANES:LANES + DN_HEADS].set(dn_full[:, W4 + DN_HEADS:])
    p["dn_wba"] = wba
    p["dn_wout"] = _cat_shards(dn_out, 0)
    p["sg_win"] = _cat_shards(sg_in, 1)
    p["sg_wout"] = _cat_shards(sg_out, 0)
    for k in _SMALL_REPL:
        p[k] = w[k][0]
    return p


def _split_cols(a, n):
    w = a.shape[-1] // n
    return [a[..., k * w:(k + 1) * w] for k in range(n)]


def _split_rows(a, n):
    h = a.shape[-2] // n
    return [a[..., k * h:(k + 1) * h, :] for k in range(n)]


_IJ = [(0, 0), (0, 1), (1, 0), (1, 1)]


def _grads_by_half_and_chip(grads):
    def rows_by_chip(a):
        return a.reshape(N_CHIPS, 2, a.shape[0] // (2 * N_CHIPS), a.shape[1])

    ops, half_first = [], []
    for i, j in _IJ:
        t = grads["wguT%d%d" % (i, j)]
        ops.append(t.reshape(2, N_CHIPS, t.shape[0] // (2 * N_CHIPS), t.shape[1]))
        half_first.append(True)
    for i, j in _IJ:
        ops.append(rows_by_chip(grads["wd%d%d" % (i, j)]))
        half_first.append(False)
    ops.append(jnp.stack([jnp.stack(_split_cols(hf, N_CHIPS)) for hf in _split_rows(grads["dn_w_in"], 2)]))
    half_first.append(True)
    for k in ("dn_w_out", "sg_w_inT", "sg_w_out"):
        ops.append(rows_by_chip(grads[k]))
        half_first.append(False)
    return ops, half_first


def _shard_grads(mine, theirs, c, w):
    lo = [jnp.where(c == 0, a, b) for a, b in zip(mine, theirs)]
    hi = [jnp.where(c == 0, b, a) for a, b in zip(mine, theirs)]
    rows = lambda t: jnp.concatenate([lo[t], hi[t]], axis=0)
    sq = lambda parts: jnp.stack(parts).reshape(2, 2, *parts[0].shape)
    g = {}
    g["ffn_w_gate"] = sq([lo[t].T for t in range(4)])
    g["ffn_w_up"] = sq([hi[t].T for t in range(4)])
    g["ffn_w_down"] = sq([rows(4 + t) for t in range(4)])
    g["dn_w_in"] = rows(8)[None]
    g["dn_w_out"] = rows(9)[None]
    g["sg_w_in"] = rows(10).T[None]
    g["sg_w_out"] = rows(11)[None]
    return {k: v.reshape(w[k].shape) for k, v in g.items()}


def kernel(x, norm_g, ffn_w_gate, ffn_w_up, ffn_w_down, dn_w_in, dn_conv_w, dn_a_log, dn_dt_bias, dn_norm_g, dn_w_out, sg_w_in, sg_b_in, sg_ln_g, sg_ln_b, sg_w_s, sg_b_s, sg_w_out, loss_target, m_norm_g, m_ffn_w_gate, m_ffn_w_up, m_ffn_w_down, m_dn_w_in, m_dn_conv_w, m_dn_a_log, m_dn_dt_bias, m_dn_norm_g, m_dn_w_out, m_sg_w_in, m_sg_b_in, m_sg_ln_g, m_sg_ln_b, m_sg_w_s, m_sg_b_s, m_sg_w_out, v_norm_g, v_ffn_w_gate, v_ffn_w_up, v_ffn_w_down, v_dn_w_in, v_dn_conv_w, v_dn_a_log, v_dn_dt_bias, v_dn_norm_g, v_dn_w_out, v_sg_w_in, v_sg_b_in, v_sg_ln_g, v_sg_ln_b, v_sg_w_s, v_sg_b_s, v_sg_w_out):
    args = dict(locals())
    w = {k: args[k] for k in _WEIGHTS}
    mom = {k: args["m_" + k] for k in _WEIGHTS}
    var = {k: args["v_" + k] for k in _WEIGHTS}
    cx, cy, cc = _mesh_pos()
    chip = 2 * cx + cy

    small_shapes = [w[k].shape for k in _SMALL_SHARDED]
    own = [_mx(w[k]) for k in _BIG] + [_pack([w[k] for k in _SMALL_SHARDED])]
    gathered = _allgather_chips([_as_halves(a) for a in own], "gather_weights")
    gathered = [_with_own(g, a, chip) for g, a in zip(gathered, own)]
    small_k = [_unpack(gathered[-1][k], small_shapes) for k in range(N_CHIPS)]
    small_full = {name: jnp.concatenate([small_k[k][i] for k in range(N_CHIPS)], axis=-1)
                  for i, name in enumerate(_SMALL_SHARDED)}
    small_full = {k: (v if k == "norm_g" else v[0]) for k, v in small_full.items()}
    p = _full_weights(w, gathered[:-1], small_full)

    loss_part, grad_x, grads = _local_step(x[0], loss_target[0], p)

    halves, half_first = _grads_by_half_and_chip(grads)
    from_sibling = _swap_halves(halves, half_first, "reduce_core_pair")
    pair_sum = [_add_own_half(h, o, cc, hf, "pair_sum_%d" % i)
                for i, (h, o, hf) in enumerate(zip(halves, from_sibling, half_first))]
    from_chips = _scatter_chips(pair_sum, "reduce_chips")
    half_sum = [_sum_chips(own_sum, got, chip, "chip_sum_%d" % i) for i, (own_sum, got) in enumerate(zip(pair_sum, from_chips))]
    other_half = _swap_whole(half_sum, "gather_core_pair")

    small_names = _SMALL_SHARDED + _SMALL_REPL
    small_grads = [grads[k] for k in small_names]
    full_shapes = [g.shape for g in small_grads] + [(1,)]
    pack = _pack(small_grads + [loss_part[0, :1]])
    summed = _sum_leading(_allgather_devices(pack, "gather_small"), "small_sum")
    parts = _unpack(summed, full_shapes)
    loss = parts[-1][0]
    small_grad = {}
    for i, k in enumerate(small_names):
        g = parts[i]
        if k in _SMALL_SHARDED:
            n = w[k].shape[-1]
            g = lax.dynamic_slice_in_dim(g, chip * n, n, axis=g.ndim - 1)
        small_grad[k] = g

    grad = {**small_grad, **_shard_grads(half_sum, other_half, cc, w)}
    delta, new_m, new_v = {}, {}, {}
    for k in _BIG:
        delta[k], new_m[k], new_v[k] = _adamw(w[k], grad[k], mom[k], var[k], "adamw_" + k)
    shapes = [w[k].shape for k in small_names]
    d, nm, nv = _adamw(_pack([w[k] for k in small_names]), _pack([grad[k] for k in small_names]),
                       _pack([mom[k] for k in small_names]), _pack([var[k] for k in small_names]), "adamw_small")
    for k, a, b, c_ in zip(small_names, _unpack(d, shapes), _unpack(nm, shapes), _unpack(nv, shapes)):
        delta[k], new_m[k], new_v[k] = a, b, c_

    return (loss, grad_x[None], *[grad[k] for k in _WEIGHTS], *[delta[k] for k in _WEIGHTS],
            *[new_m[k] for k in _WEIGHTS], *[new_v[k] for k in _WEIGHTS])
```

```python
import functools
import math

import jax
import jax.numpy as jnp
from jax import lax
from jax.experimental import pallas as pl
from jax.experimental.pallas import tpu as pltpu

F32 = jnp.float32
MXU_DTYPE = jnp.bfloat16
COMM_DTYPE = jnp.bfloat16
HI = lax.Precision.HIGHEST
TRI_PREC = lax.Precision.HIGH

D_MODEL = 1024
D_FF = 2816
RMS_EPS = 1e-6
LN_EPS = 1e-5
L2_EPS = 1e-6
DN_HEADS = 8
DN_HEAD_DIM = 128
DN_CONV = 4
DN_CHUNK = 64
SG_WIDTH = 2048
SG_GROUPS = 8
SG_CHUNK = 128
SG_GROUP_W = SG_WIDTH // SG_GROUPS
N_CHIPS = 4
N_DEV = 8
LANES = 128
SUBLANES = 8
VMEM_LIMIT = 56 * 1024 * 1024

ADAM_LR = 0.001
ADAM_B1 = 0.9
ADAM_B2 = 0.999
ADAM_EPS = 1e-08
ADAM_WD = 0.01
ADAM_STEP = 10

MESH = pl.DeviceIdType.MESH
ANY = pl.BlockSpec(memory_space=pl.ANY)


def _cp(*sem):
    return pltpu.CompilerParams(dimension_semantics=sem, vmem_limit_bytes=VMEM_LIMIT)


def _pick(n, pref, mult=LANES):
    best = None
    d = mult
    while d <= min(n, pref):
        if n % d == 0:
            best = d
        d += mult
    return best if best is not None else n


def _full(shape):
    nd = len(shape)
    return pl.BlockSpec(shape, lambda *_: (0,) * nd)


def _sigmoid(x):
    return 1.0 / (1.0 + jnp.exp(-x))


def _dot(a, b, dims, prec=None):
    return lax.dot_general(a, b, (dims, ((), ())), preferred_element_type=F32, precision=prec)


NN = ((1,), (0,))
NT = ((1,), (1,))
TN = ((0,), (0,))


def _mx(a):
    return a.astype(MXU_DTYPE)


def _rms_stat(x):
    return lax.rsqrt(jnp.mean(x * x, axis=-1, keepdims=True) + RMS_EPS)


def _rms_bwd(x, r, g, dy):
    xh = x * r
    dxh = dy * g
    dx = r * (dxh - xh * jnp.mean(dxh * xh, axis=-1, keepdims=True))
    return dx, jnp.sum(dy * xh, axis=0, keepdims=True)


def _mm(a, b, mode, name, out_dtype=F32, add=None):
    if mode == "tn":
        K, M = a.shape
        N = b.shape[1]
    elif mode == "nt":
        M, K = a.shape
        N = b.shape[0]
    else:
        M, K = a.shape
        N = b.shape[1]
    tn = _pick(N, 1024)
    if mode == "tn":
        tm = _pick(M, 1024 if tn <= 512 else 1408)
        tk = _pick(K, 1024, SUBLANES)
    else:
        tm = _pick(M, max(512, min(2048, (512 * 1024) // tn)), SUBLANES)
        tk = _pick(K, 2048)
    nk = K // tk
    grid = (N // tn, M // tm, nk)
    if mode == "nn":
        a_spec = pl.BlockSpec((tm, tk), lambda j, i, k: (i, k))
        b_spec = pl.BlockSpec((tk, tn), lambda j, i, k: (k, j))
        dims = NN
    elif mode == "nt":
        a_spec = pl.BlockSpec((tm, tk), lambda j, i, k: (i, k))
        b_spec = pl.BlockSpec((tn, tk), lambda j, i, k: (j, k))
        dims = NT
    else:
        a_spec = pl.BlockSpec((tk, tm), lambda j, i, k: (k, i))
        b_spec = pl.BlockSpec((tk, tn), lambda j, i, k: (k, j))
        dims = TN
    o_spec = pl.BlockSpec((tm, tn), lambda j, i, k: (i, j))
    has_add = add is not None

    def body(*refs):
        if has_add:
            a_ref, b_ref, add_ref, o_ref, acc = refs
        else:
            a_ref, b_ref, o_ref, acc = refs
        k = pl.program_id(2)

        @pl.when(k == 0)
        def _():
            acc[...] = add_ref[...] if has_add else jnp.zeros_like(acc)

        acc[...] += _dot(a_ref[...], b_ref[...], dims)

        @pl.when(k == nk - 1)
        def _():
            o_ref[...] = acc[...].astype(o_ref.dtype)

    ins = [a, b] + ([add] if has_add else [])
    specs = [a_spec, b_spec] + ([o_spec] if has_add else [])
    return pl.pallas_call(
        body, name=name, grid=grid, in_specs=specs, out_specs=o_spec,
        out_shape=jax.ShapeDtypeStruct((M, N), out_dtype),
        scratch_shapes=[pltpu.VMEM((tm, tn), F32)],
        compiler_params=_cp("parallel", "parallel", "arbitrary"),
    )(*ins)


def _load_resident(pairs, sem):
    @pl.when(pl.program_id(0) == 0)
    def _():
        cps = [pltpu.make_async_copy(src, dst, sem.at[i]) for i, (src, dst) in enumerate(pairs)]
        for c in cps:
            c.start()
        for c in cps:
            c.wait()


def _ffn_fwd(x, g0, g1, wgu, wd, name):
    T, D = x.shape
    F2 = wgu.shape[1]
    F = F2 // 2
    tm = _pick(T, 256, SUBLANES)

    def body(x_ref, g0_ref, g1_ref, wgu_hbm, wd_hbm, xo_ref, h_ref, gu_ref, y_ref, wgu_v, wd_v, sem):
        _load_resident([(wgu_hbm, wgu_v), (wd_hbm, wd_v)], sem)
        xv = x_ref[...]
        hb = _mx(xv * _rms_stat(xv) * g0_ref[...])
        h_ref[...] = hb
        gu = _dot(hb, wgu_v[...], NN)
        gu_ref[...] = gu
        g = gu[:, :F]
        u = gu[:, F:]
        a = _mx(g * _sigmoid(g) * u)
        y = _dot(a, wd_v[...], NN)
        y_ref[...] = y
        xo_ref[...] = xv + 0.5 * (y * _rms_stat(y) * g1_ref[...])

    row = lambda w: pl.BlockSpec((tm, w), lambda i: (i, 0))
    return pl.pallas_call(
        body, name=name, grid=(T // tm,),
        in_specs=[row(D), _full((1, D)), _full((1, D)), ANY, ANY],
        out_specs=[row(D), row(D), row(F2), row(D)],
        out_shape=[jax.ShapeDtypeStruct((T, D), F32), jax.ShapeDtypeStruct((T, D), MXU_DTYPE),
                   jax.ShapeDtypeStruct((T, F2), F32), jax.ShapeDtypeStruct((T, D), F32)],
        scratch_shapes=[pltpu.VMEM(wgu.shape, wgu.dtype), pltpu.VMEM(wd.shape, wd.dtype),
                        pltpu.SemaphoreType.DMA((2,))],
        compiler_params=_cp("arbitrary"),
    )(x, g0, g1, wgu, wd)


def _ffn_bwd_down(dxo, y, gu, g1, wd, name):
    T, D = y.shape
    F2 = gu.shape[1]
    F = F2 // 2
    tm = _pick(T, 256, SUBLANES)

    def body(dxo_ref, y_ref, gu_ref, g1_ref, wd_hbm, dy_ref, a_ref, dgu_ref, dg1_ref, wd_v, sem):
        _load_resident([(wd_hbm, wd_v)], sem)

        @pl.when(pl.program_id(0) == 0)
        def _():
            dg1_ref[...] = jnp.zeros_like(dg1_ref)

        yv = y_ref[...]
        dy, dg1 = _rms_bwd(yv, _rms_stat(yv), g1_ref[...], 0.5 * dxo_ref[...])
        dg1_ref[...] += dg1
        dyb = _mx(dy)
        dy_ref[...] = dyb
        da = _dot(dyb, wd_v[...], NT)
        gu_v = gu_ref[...]
        g = gu_v[:, :F]
        u = gu_v[:, F:]
        s = _sigmoid(g)
        sg = g * s
        a_ref[...] = _mx(sg * u)
        dgu_ref[:, :F] = _mx(da * u * (s * (1.0 + g * (1.0 - s))))
        dgu_ref[:, F:] = _mx(da * sg)

    row = lambda w: pl.BlockSpec((tm, w), lambda i: (i, 0))
    return pl.pallas_call(
        body, name=name, grid=(T // tm,),
        in_specs=[row(D), row(D), row(F2), _full((1, D)), ANY],
        out_specs=[row(D), row(F), row(F2), _full((1, D))],
        out_shape=[jax.ShapeDtypeStruct((T, D), MXU_DTYPE), jax.ShapeDtypeStruct((T, F), MXU_DTYPE),
                   jax.ShapeDtypeStruct((T, F2), MXU_DTYPE), jax.ShapeDtypeStruct((1, D), F32)],
        scratch_shapes=[pltpu.VMEM(wd.shape, wd.dtype), pltpu.SemaphoreType.DMA((1,))],
        compiler_params=_cp("arbitrary"),
    )(dxo, y, gu, g1, wd)


def _ffn_bwd_up(dgu, x, dxo, g0, wgu, name):
    T, D = x.shape
    F2 = dgu.shape[1]
    tm = _pick(T, 256, SUBLANES)

    def body(dgu_ref, x_ref, dxo_ref, g0_ref, wgu_hbm, dx_ref, dg0_ref, wgu_v, sem):
        _load_resident([(wgu_hbm, wgu_v)], sem)

        @pl.when(pl.program_id(0) == 0)
        def _():
            dg0_ref[...] = jnp.zeros_like(dg0_ref)

        dh = _dot(dgu_ref[...], wgu_v[...], NT)
        xv = x_ref[...]
        dx, dg0 = _rms_bwd(xv, _rms_stat(xv), g0_ref[...], dh)
        dg0_ref[...] += dg0
        dx_ref[...] = dxo_ref[...] + dx

    row = lambda w: pl.BlockSpec((tm, w), lambda i: (i, 0))
    return pl.pallas_call(
        body, name=name, grid=(T // tm,),
        in_specs=[row(F2), row(D), row(D), _full((1, D)), ANY],
        out_specs=[row(D), _full((1, D))],
        out_shape=[jax.ShapeDtypeStruct((T, D), F32), jax.ShapeDtypeStruct((1, D), F32)],
        scratch_shapes=[pltpu.VMEM(wgu.shape, wgu.dtype), pltpu.SemaphoreType.DMA((1,))],
        compiler_params=_cp("arbitrary"),
    )(dgu, x, dxo, g0, wgu)


def _norm_fwd(x, g, name):
    T, D = x.shape
    tm = _pick(T, 512, SUBLANES)

    def body(x_ref, g_ref, h_ref):
        xv = x_ref[...]
        h_ref[...] = _mx(xv * _rms_stat(xv) * g_ref[...])

    row = pl.BlockSpec((tm, D), lambda i: (i, 0))
    return pl.pallas_call(body, name=name, grid=(T // tm,), in_specs=[row, _full((1, D))], out_specs=row,
                          out_shape=jax.ShapeDtypeStruct((T, D), MXU_DTYPE), compiler_params=_cp("parallel"))(x, g)


def _postnorm_fwd(x, m, g, name):
    T, D = x.shape
    tm = _pick(T, 512, SUBLANES)

    def body(x_ref, m_ref, g_ref, o_ref):
        mv = m_ref[...]
        o_ref[...] = x_ref[...] + mv * _rms_stat(mv) * g_ref[...]

    row = pl.BlockSpec((tm, D), lambda i: (i, 0))
    return pl.pallas_call(body, name=name, grid=(T // tm,), in_specs=[row, row, _full((1, D))], out_specs=row,
                          out_shape=jax.ShapeDtypeStruct((T, D), F32), compiler_params=_cp("parallel"))(x, m, g)


def _postnorm_bwd(dxo, m, g, name):
    T, D = m.shape
    tm = _pick(T, 512, SUBLANES)

    def body(dxo_ref, m_ref, g_ref, dm_ref, dg_ref):
        @pl.when(pl.program_id(0) == 0)
        def _():
            dg_ref[...] = jnp.zeros_like(dg_ref)

        mv = m_ref[...]
        dm, dg = _rms_bwd(mv, _rms_stat(mv), g_ref[...], dxo_ref[...])
        dg_ref[...] += dg
        dm_ref[...] = _mx(dm)

    row = pl.BlockSpec((tm, D), lambda i: (i, 0))
    return pl.pallas_call(body, name=name, grid=(T // tm,), in_specs=[row, row, _full((1, D))],
                          out_specs=[row, _full((1, D))],
                          out_shape=[jax.ShapeDtypeStruct((T, D), MXU_DTYPE), jax.ShapeDtypeStruct((1, D), F32)],
                          compiler_params=_cp("arbitrary"))(dxo, m, g)


def _prenorm_bwd(dxo, dh, x, g, name):
    T, D = x.shape
    tm = _pick(T, 512, SUBLANES)

    def body(dxo_ref, dh_ref, x_ref, g_ref, dx_ref, dg_ref):
        @pl.when(pl.program_id(0) == 0)
        def _():
            dg_ref[...] = jnp.zeros_like(dg_ref)

        xv = x_ref[...]
        dx, dg = _rms_bwd(xv, _rms_stat(xv), g_ref[...], dh_ref[...])
        dg_ref[...] += dg
        dx_ref[...] = dxo_ref[...] + dx

    row = pl.BlockSpec((tm, D), lambda i: (i, 0))
    return pl.pallas_call(body, name=name, grid=(T // tm,), in_specs=[row, row, row, _full((1, D))],
                          out_specs=[row, _full((1, D))],
                          out_shape=[jax.ShapeDtypeStruct((T, D), F32), jax.ShapeDtypeStruct((1, D), F32)],
                          compiler_params=_cp("arbitrary"))(dxo, dh, x, g)


def _loss_fwd_bwd(y, target, name):
    T, D = y.shape
    tm = _pick(T, 512, SUBLANES)

    def body(y_ref, t_ref, l_ref, dy_ref):
        @pl.when(pl.program_id(0) == 0)
        def _():
            l_ref[...] = jnp.zeros_like(l_ref)

        e = y_ref[...] - t_ref[...]
        dy_ref[...] = e * (1.0 / D)
        l_ref[...] += 0.5 * jnp.sum(jnp.mean(e * e, axis=-1, keepdims=True), axis=0, keepdims=True)

    row = pl.BlockSpec((tm, D), lambda i: (i, 0))
    return pl.pallas_call(body, name=name, grid=(T // tm,), in_specs=[row, row],
                          out_specs=[_full((SUBLANES, LANES)), row],
                          out_shape=[jax.ShapeDtypeStruct((SUBLANES, LANES), F32), jax.ShapeDtypeStruct((T, D), F32)],
                          compiler_params=_cp("arbitrary"))(y, target)


DN_ROWS = 512


def _shift_down(prev8, cur, s):
    n = cur.shape[0]
    xx = jnp.concatenate([prev8, cur], axis=0)
    return pltpu.roll(xx, s, 0)[SUBLANES:SUBLANES + n, :]


def _shift_up(cur, next8, s):
    n = cur.shape[0]
    xx = jnp.concatenate([cur, next8], axis=0)
    return pltpu.roll(xx, n + SUBLANES - s, 0)[:n, :]


def _conv_tile(x_ref, w, r, rows):
    start = pl.multiple_of(r * rows, SUBLANES)
    cur = x_ref[pl.ds(start, rows), :]
    pstart = pl.multiple_of(jnp.maximum(start - SUBLANES, 0), SUBLANES)
    prev8 = jnp.where(r == 0, 0.0, x_ref[pl.ds(pstart, SUBLANES), :])
    taps = [_shift_down(prev8, cur, DN_CONV - 1 - j) if j < DN_CONV - 1 else cur for j in range(DN_CONV)]
    c = taps[0] * w[0:1, :]
    for j in range(1, DN_CONV):
        c = c + taps[j] * w[j:j + 1, :]
    return c, taps


def _dn_prep_fwd(proj, conv_w, name):
    T = proj.shape[0]
    W = DN_HEADS * DN_HEAD_DIM
    rows = min(DN_ROWS, T)
    n_inner = T // rows
    scale = DN_HEAD_DIM ** -0.5

    def body(x_ref, w_ref, o_ref):
        cb = pl.program_id(0)
        w = w_ref[...]
        is_qk = cb < 2 * DN_HEADS
        post = jnp.where(cb < DN_HEADS, scale, 1.0)

        def step(r, carry):
            c, _ = _conv_tile(x_ref, w, r, rows)
            s = c * _sigmoid(c)
            rinv = lax.rsqrt(jnp.sum(s * s, axis=-1, keepdims=True) + L2_EPS)
            o_ref[pl.ds(pl.multiple_of(r * rows, SUBLANES), rows), :] = jnp.where(is_qk, s * rinv * post, s)
            return carry

        lax.fori_loop(0, n_inner, step, 0)

    col = pl.BlockSpec((T, LANES), lambda j: (0, j))
    return pl.pallas_call(body, name=name, grid=(3 * W // LANES,),
                          in_specs=[col, pl.BlockSpec((DN_CONV, LANES), lambda j: (0, j))], out_specs=col,
                          out_shape=jax.ShapeDtypeStruct((T, 3 * W), F32), compiler_params=_cp("parallel"))(proj, conv_w)


def _dn_prep_bwd(proj, conv_w, dqkv, name):
    T = proj.shape[0]
    W = DN_HEADS * DN_HEAD_DIM
    rows = min(DN_ROWS, T)
    n_inner = T // rows
    scale = DN_HEAD_DIM ** -0.5

    def body(x_ref, w_ref, dy_ref, dx_ref, dw_ref, dc_scr):
        cb = pl.program_id(0)
        w = w_ref[...]
        is_qk = cb < 2 * DN_HEADS
        post = jnp.where(cb < DN_HEADS, scale, 1.0)

        def step1(r, dws):
            c, taps = _conv_tile(x_ref, w, r, rows)
            sg = _sigmoid(c)
            s = c * sg
            rinv = lax.rsqrt(jnp.sum(s * s, axis=-1, keepdims=True) + L2_EPS)
            dy = dy_ref[pl.ds(pl.multiple_of(r * rows, SUBLANES), rows), :]
            yn = s * rinv
            dyn = dy * post
            ds_qk = rinv * (dyn - yn * jnp.sum(dyn * yn, axis=-1, keepdims=True))
            ds = jnp.where(is_qk, ds_qk, dy)
            dc = ds * (sg * (1.0 + c * (1.0 - sg)))
            dc_scr[pl.ds(pl.multiple_of(r * rows, SUBLANES), rows), :] = dc
            return tuple(dws[j] + jnp.sum(dc * taps[j], axis=0, keepdims=True) for j in range(DN_CONV))

        zero = jnp.zeros((1, LANES), F32)
        dws = lax.fori_loop(0, n_inner, step1, (zero,) * DN_CONV)
        for j in range(DN_CONV):
            dw_ref[j:j + 1, :] = dws[j]

        def step2(r, carry):
            start = pl.multiple_of(r * rows, SUBLANES)
            cur = dc_scr[pl.ds(start, rows), :]
            nstart = pl.multiple_of(jnp.minimum(start + rows, T - SUBLANES), SUBLANES)
            next8 = jnp.where(r == n_inner - 1, 0.0, dc_scr[pl.ds(nstart, SUBLANES), :])
            dx = cur * w[DN_CONV - 1:DN_CONV, :]
            for j in range(DN_CONV - 1):
                dx = dx + _shift_up(cur, next8, DN_CONV - 1 - j) * w[j:j + 1, :]
            dx_ref[pl.ds(start, rows), :] = _mx(dx)
            return carry

        lax.fori_loop(0, n_inner, step2, 0)

    col = pl.BlockSpec((T, LANES), lambda j: (0, j))
    wspec = pl.BlockSpec((DN_CONV, LANES), lambda j: (0, j))
    return pl.pallas_call(body, name=name, grid=(3 * W // LANES,), in_specs=[col, wspec, col], out_specs=[col, wspec],
                          out_shape=[jax.ShapeDtypeStruct((T, 3 * W), MXU_DTYPE), jax.ShapeDtypeStruct((DN_CONV, 3 * W), F32)],
                          scratch_shapes=[pltpu.VMEM((T, LANES), F32)], compiler_params=_cp("parallel"))(proj, conv_w, dqkv)


def _softplus(x):
    return jnp.maximum(x, 0.0) + jnp.log(1.0 + jnp.exp(-jnp.abs(x)))


def _dn_gate_fwd(ba, a_log, dt_bias, name):
    T = ba.shape[0]
    tm = _pick(T, 1024, SUBLANES)

    def body(ba_ref, al_ref, dt_ref, beta_ref, g_ref):
        beta_ref[...] = _sigmoid(ba_ref[:, :LANES])
        g_ref[...] = -jnp.exp(al_ref[...]) * _softplus(ba_ref[:, LANES:] + dt_ref[...])

    row = lambda w: pl.BlockSpec((tm, w), lambda i: (i, 0))
    return pl.pallas_call(body, name=name, grid=(T // tm,), in_specs=[row(2 * LANES), _full((1, LANES)), _full((1, LANES))],
                          out_specs=[row(LANES), row(LANES)],
                          out_shape=[jax.ShapeDtypeStruct((T, LANES), F32)] * 2, compiler_params=_cp("parallel"))(ba, a_log, dt_bias)


def _dn_gate_bwd(ba, a_log, dt_bias, dbeta, dg, name):
    T = ba.shape[0]
    tm = _pick(T, 1024, SUBLANES)

    def body(ba_ref, al_ref, dt_ref, dbeta_ref, dg_ref, dba_ref, dal_ref, ddt_ref):
        @pl.when(pl.program_id(0) == 0)
        def _():
            dal_ref[...] = jnp.zeros_like(dal_ref)
            ddt_ref[...] = jnp.zeros_like(ddt_ref)

        beta = _sigmoid(ba_ref[:, :LANES])
        dba_ref[:, :LANES] = _mx(dbeta_ref[...] * beta * (1.0 - beta))
        pre = ba_ref[:, LANES:] + dt_ref[...]
        ea = jnp.exp(al_ref[...])
        dgv = dg_ref[...]
        da = dgv * (-ea) * _sigmoid(pre)
        dba_ref[:, LANES:] = _mx(da)
        ddt_ref[...] += jnp.sum(da, axis=0, keepdims=True)
        dal_ref[...] += jnp.sum(dgv * (-ea) * _softplus(pre), axis=0, keepdims=True)

    row = lambda w: pl.BlockSpec((tm, w), lambda i: (i, 0))
    one = _full((1, LANES))
    return pl.pallas_call(body, name=name, grid=(T // tm,), in_specs=[row(2 * LANES), one, one, row(LANES), row(LANES)],
                          out_specs=[row(2 * LANES), one, one],
                          out_shape=[jax.ShapeDtypeStruct((T, 2 * LANES), MXU_DTYPE), jax.ShapeDtypeStruct((1, LANES), F32),
                                     jax.ShapeDtypeStruct((1, LANES), F32)],
                          compiler_params=_cp("arbitrary"))(ba, a_log, dt_bias, dbeta, dg)


def _tri(c, strict):
    i = lax.broadcasted_iota(jnp.int32, (c, c), 0)
    j = lax.broadcasted_iota(jnp.int32, (c, c), 1)
    return (i > j) if strict else (i >= j)


def _inv_unit_lower(ls):
    c = ls[0].shape[0]
    i = lax.broadcasted_iota(jnp.int32, (c, c), 0)
    j = lax.broadcasted_iota(jnp.int32, (c, c), 1)
    eye = jnp.where(i == j, 1.0, 0.0)
    facs = [[eye - l for l in ls]]
    cur = ls
    for _ in range(int(math.log2(c)) - 1):
        cur = [_dot(p, p, NN, TRI_PREC) for p in cur]
        facs.append([eye + p for p in cur])
    while len(facs) > 1:
        nxt = [[_dot(a, b, NN, TRI_PREC) for a, b in zip(facs[t], facs[t + 1])] for t in range(0, len(facs) - 1, 2)]
        if len(facs) % 2:
            nxt.append(facs[-1])
        facs = nxt
    return facs[0]


def _chunk_gates(g_blk):
    c = g_blk.shape[0]
    gcs = _dot(jnp.where(_tri(c, False), 1.0, 0.0), g_blk, NN, HI)
    return gcs, gcs.T


def _head_chunk(h, qh, kh, vh, beta_blk, gcs, gcs_t):
    c = qh.shape[0]
    incl = _tri(c, False)
    gc_col = gcs[:, h:h + 1]
    gc_row = gcs_t[h:h + 1, :]
    gc_last = gcs_t[h:h + 1, c - 1:c]
    dec = jnp.where(incl, jnp.exp(jnp.where(incl, gc_col - gc_row, 0.0)), 0.0)
    gam = jnp.exp(gc_col)
    rr = jnp.exp(gc_last - gc_col)
    gl = jnp.exp(gc_last)
    b = beta_blk[:, h:h + 1]
    kb = kh * b
    vb = vh * b
    kk = _dot(_mx(kb), _mx(kh), NT)
    lmat = jnp.where(_tri(c, True), kk * dec, 0.0)
    qk = _dot(_mx(qh), _mx(kh), NT)
    pmat = jnp.where(incl, qk * dec, 0.0)
    return dict(dec=dec, gam=gam, rr=rr, gl=gl, b=b, kb=kb, vb=vb, lmat=lmat, pmat=pmat)


def _dn_scan_fwd(qkv, beta, g, proj, norm_g, name):
    T = qkv.shape[0]
    C, H, Dh = DN_CHUNK, DN_HEADS, DN_HEAD_DIM
    W = H * Dh
    N = T // C

    def body(q_ref, k_ref, v_ref, beta_ref, g_ref, z_ref, ng_ref, og_ref, o_ref, tinv_ref, s_ref, state):
        @pl.when(pl.program_id(0) == 0)
        def _():
            state[...] = jnp.zeros_like(state)

        gcs, gcs_t = _chunk_gates(g_ref[...])
        beta_blk = beta_ref[...]
        ng = ng_ref[...]
        heads = range(H)
        cs = [slice(h * Dh, (h + 1) * Dh) for h in heads]
        qs = [_head_chunk(h, q_ref[:, cs[h]], k_ref[:, cs[h]], v_ref[:, cs[h]], beta_blk, gcs, gcs_t) for h in heads]
        tinvs = _inv_unit_lower([q["lmat"] for q in qs])
        for h in heads:
            tinv_ref[h] = tinvs[h]
        us = [_dot(tinvs[h], qs[h]["vb"], NN, TRI_PREC) for h in heads]
        ws = [_dot(tinvs[h], qs[h]["kb"] * qs[h]["gam"], NN, TRI_PREC) for h in heads]
        ss = [state[h] for h in heads]
        for h in heads:
            s_ref[0, h] = ss[h]
        sbs = [_mx(s) for s in ss]
        vnbs = [_mx(us[h] - _dot(_mx(ws[h]), sbs[h], NN)) for h in heads]
        os_ = [_dot(_mx(q_ref[:, cs[h]] * qs[h]["gam"]), sbs[h], NN) + _dot(_mx(qs[h]["pmat"]), vnbs[h], NN) for h in heads]
        for h in heads:
            state[h] = ss[h] * qs[h]["gl"] + _dot(_mx((k_ref[:, cs[h]] * qs[h]["rr"]).T), vnbs[h], NN)
        for h in heads:
            o = os_[h]
            o_ref[:, cs[h]] = o
            zh = z_ref[:, cs[h]]
            og_ref[:, cs[h]] = _mx(o * _rms_stat(o) * ng * (zh * _sigmoid(zh)))

    blk = lambda j: pl.BlockSpec((C, W), lambda n: (n, j))
    small = pl.BlockSpec((C, LANES), lambda n: (n, 0))
    return pl.pallas_call(
        body, name=name, grid=(N,),
        in_specs=[blk(0), blk(1), blk(2), small, small, blk(3), _full((1, Dh))],
        out_specs=[blk(0), blk(0), pl.BlockSpec((H, C, C), lambda n: (0, n, 0)),
                   pl.BlockSpec((1, H, Dh, Dh), lambda n: (n, 0, 0, 0))],
        out_shape=[jax.ShapeDtypeStruct((T, W), MXU_DTYPE), jax.ShapeDtypeStruct((T, W), F32),
                   jax.ShapeDtypeStruct((H, T, C), F32), jax.ShapeDtypeStruct((N, H, Dh, Dh), F32)],
        scratch_shapes=[pltpu.VMEM((H, Dh, Dh), F32)],
        compiler_params=_cp("arbitrary"),
    )(qkv, qkv, qkv, beta, g, proj, norm_g)


def _dn_scan_bwd(qkv, beta, g, proj, norm_g, o, tinv, s_all, dog, name):
    T = qkv.shape[0]
    C, H, Dh = DN_CHUNK, DN_HEADS, DN_HEAD_DIM
    W = H * Dh
    N = T // C

    def body(q_ref, k_ref, v_ref, beta_ref, g_ref, z_ref, ng_ref, o_ref, tinv_ref, s_ref, dog_ref,
             dqkv_ref, dbeta_ref, dg_ref, dz_ref, dng_ref, dstate):
        @pl.when(pl.program_id(0) == 0)
        def _():
            dstate[...] = jnp.zeros_like(dstate)
            dng_ref[...] = jnp.zeros_like(dng_ref)

        gcs, gcs_t = _chunk_gates(g_ref[...])
        beta_blk = beta_ref[...]
        ng = ng_ref[...]
        incl = _tri(C, False)
        strict = _tri(C, True)
        lane = lax.broadcasted_iota(jnp.int32, (C, LANES), 1)
        rowi = lax.broadcasted_iota(jnp.int32, (C, 1), 0)
        ones = jnp.ones((C, LANES), F32)
        dbeta_acc = jnp.zeros((C, LANES), F32)
        dgc_acc = jnp.zeros((C, LANES), F32)
        dng_acc = jnp.zeros((1, Dh), F32)
        heads = range(H)
        cs = [slice(h * Dh, (h + 1) * Dh) for h in heads]
        rsum = lambda t: jnp.sum(t, axis=1, keepdims=True)
        dobs = []
        for h in heads:
            oh, zh, dogh = o_ref[:, cs[h]], z_ref[:, cs[h]], dog_ref[:, cs[h]]
            rstat = _rms_stat(oh)
            sz = _sigmoid(zh)
            dz_ref[:, cs[h]] = _mx(dogh * (oh * rstat * ng) * (sz * (1.0 + zh * (1.0 - sz))))
            do, dng = _rms_bwd(oh, rstat, ng, dogh * (zh * sz))
            dng_acc = dng_acc + dng
            dobs.append(_mx(do))
        qs = [_head_chunk(h, q_ref[:, cs[h]], k_ref[:, cs[h]], v_ref[:, cs[h]], beta_blk, gcs, gcs_t) for h in heads]
        tms = [tinv_ref[h] for h in heads]
        us = [_dot(tms[h], qs[h]["vb"], NN, TRI_PREC) for h in heads]
        ws = [_dot(tms[h], qs[h]["kb"] * qs[h]["gam"], NN, TRI_PREC) for h in heads]
        ss = [s_ref[0, h] for h in heads]
        sbs = [_mx(s) for s in ss]
        wbs = [_mx(w) for w in ws]
        vnbs = [_mx(us[h] - _dot(wbs[h], sbs[h], NN)) for h in heads]
        dsns = [dstate[h] for h in heads]
        dsbs = [_mx(d) for d in dsns]
        dvnews = [_dot(_mx(qs[h]["pmat"]), dobs[h], TN) + _dot(_mx(k_ref[:, cs[h]] * qs[h]["rr"]), dsbs[h], NN) for h in heads]
        dvb16s = [_mx(d) for d in dvnews]
        dps = [jnp.where(incl, _dot(dobs[h], vnbs[h], NT), 0.0) for h in heads]
        dqds = [_dot(dobs[h], sbs[h], NT) for h in heads]
        dkds = [_dot(vnbs[h], dsbs[h], NT) for h in heads]
        dgls = [jnp.sum(rsum(ss[h] * dsns[h]), axis=0, keepdims=True) for h in heads]
        dws = [-_dot(dvb16s[h], sbs[h], NT) for h in heads]
        for h in heads:
            dstate[h] = (_dot(_mx(q_ref[:, cs[h]] * qs[h]["gam"]), dobs[h], TN) + qs[h]["gl"] * dsns[h]
                         - _dot(wbs[h], dvb16s[h], TN))
        dvbs = [_dot(tms[h], dvnews[h], TN, TRI_PREC) for h in heads]
        dkbgs = [_dot(tms[h], dws[h], TN, TRI_PREC) for h in heads]
        dls = [jnp.where(strict, -(_dot(dvbs[h], us[h], NT, TRI_PREC) + _dot(dkbgs[h], ws[h], NT, TRI_PREC)), 0.0)
               for h in heads]
        mmats = [dls[h] * qs[h]["lmat"] + dps[h] * qs[h]["pmat"] for h in heads]
        dgcs = [rsum(mmats[h]) - _dot(mmats[h], ones, TN, HI)[:, :1] for h in heads]
        dkk16s = [_mx(dls[h] * qs[h]["dec"]) for h in heads]
        dqk16s = [_mx(dps[h] * qs[h]["dec"]) for h in heads]
        for h in heads:
            q = qs[h]
            qh, kh, vh = q_ref[:, cs[h]], k_ref[:, cs[h]], v_ref[:, cs[h]]
            gam, rr, b, kb = q["gam"], q["rr"], q["b"], q["kb"]
            dkb = _dot(dkk16s[h], _mx(kh), NN) + dkbgs[h] * gam
            dk = _dot(dkk16s[h], _mx(kb), TN) + _dot(dqk16s[h], _mx(qh), TN) + dkb * b + dkds[h] * rr
            dq = _dot(dqk16s[h], _mx(kh), NN) + dqds[h] * gam
            dgam = rsum(dkbgs[h] * kb) + rsum(dqds[h] * qh)
            dr = rsum(dkds[h] * kh)
            dgc_last = jnp.sum(dr * rr, axis=0, keepdims=True) + dgls[h] * q["gl"]
            dgc = dgcs[h] + dgam * gam - dr * rr + jnp.where(rowi == C - 1, dgc_last, 0.0)
            dbeta = rsum(dvbs[h] * vh) + rsum(dkb * kh)
            dqkv_ref[:, cs[h]] = dq
            dqkv_ref[:, W + h * Dh:W + (h + 1) * Dh] = dk
            dqkv_ref[:, 2 * W + h * Dh:2 * W + (h + 1) * Dh] = dvbs[h] * b
            dbeta_acc = jnp.where(lane == h, dbeta, dbeta_acc)
            dgc_acc = jnp.where(lane == h, dgc, dgc_acc)
        dbeta_ref[...] = dbeta_acc
        dg_ref[...] = _dot(jnp.where(incl, 1.0, 0.0), dgc_acc, TN, HI)
        dng_ref[...] += dng_acc

    rev = lambda n: N - 1 - n
    blk = lambda j: pl.BlockSpec((C, W), lambda n: (rev(n), j))
    small = pl.BlockSpec((C, LANES), lambda n: (rev(n), 0))
    return pl.pallas_call(
        body, name=name, grid=(N,),
        in_specs=[blk(0), blk(1), blk(2), small, small, blk(3), _full((1, Dh)), blk(0),
                  pl.BlockSpec((H, C, C), lambda n: (0, rev(n), 0)),
                  pl.BlockSpec((1, H, Dh, Dh), lambda n: (rev(n), 0, 0, 0)), blk(0)],
        out_specs=[pl.BlockSpec((C, 3 * W), lambda n: (rev(n), 0)), small, small, blk(0), _full((1, Dh))],
        out_shape=[jax.ShapeDtypeStruct((T, 3 * W), F32), jax.ShapeDtypeStruct((T, LANES), F32),
                   jax.ShapeDtypeStruct((T, LANES), F32), jax.ShapeDtypeStruct((T, W), MXU_DTYPE),
                   jax.ShapeDtypeStruct((1, Dh), F32)],
        scratch_shapes=[pltpu.VMEM((H, Dh, Dh), F32)],
        compiler_params=_cp("arbitrary"),
    )(qkv, qkv, qkv, beta, g, proj, norm_g, o, tinv, s_all, dog)


_INV_SQRT2 = 0.7071067811865476
_INV_SQRT_2PI = 0.3989422804014327


def _sg_recompute(zp_ref, bin_ref, lng_ref, lnb_ref):
    E = SG_WIDTH
    zin = zp_ref[...] + bin_ref[...]
    cdf = 0.5 * (1.0 + lax.erf(zin * _INV_SQRT2))
    zz = zin * cdf
    u = zz[:, :E]
    vp = zz[:, E:]
    mu = jnp.mean(vp, axis=-1, keepdims=True)
    xc = vp - mu
    rstd = lax.rsqrt(jnp.mean(xc * xc, axis=-1, keepdims=True) + LN_EPS)
    xhat = xc * rstd
    v = xhat * lng_ref[...] + lnb_ref[...]
    return zin, cdf, u, xhat, rstd, v


def _sg_masked_ws(ws_ref, g):
    return _mx(jnp.where(_tri(SG_CHUNK, False), ws_ref[g], 0.0))


def _sg_fwd(zpre, b_in, ln_g, ln_b, w_s, b_s_t, name):
    T = zpre.shape[0]
    E, G, C, GW = SG_WIDTH, SG_GROUPS, SG_CHUNK, SG_GROUP_W

    def body(zp_ref, bin_ref, lng_ref, lnb_ref, ws_ref, bst_ref, um_ref):
        _, _, u, _, _, v = _sg_recompute(zp_ref, bin_ref, lng_ref, lnb_ref)
        bst = bst_ref[...]
        for g in range(G):
            cs = slice(g * GW, (g + 1) * GW)
            mixed = _dot(_sg_masked_ws(ws_ref, g), _mx(v[:, cs]), NN) + bst[:, g:g + 1]
            um_ref[:, cs] = _mx(u[:, cs] * mixed)

    return pl.pallas_call(
        body, name=name, grid=(T // C,),
        in_specs=[pl.BlockSpec((C, 2 * E), lambda n: (n, 0)), _full((1, 2 * E)), _full((1, E)), _full((1, E)),
                  _full((G, C, C)), _full((C, LANES))],
        out_specs=pl.BlockSpec((C, E), lambda n: (n, 0)),
        out_shape=jax.ShapeDtypeStruct((T, E), MXU_DTYPE), compiler_params=_cp("parallel"),
    )(zpre, b_in, ln_g, ln_b, w_s, b_s_t)


def _sg_bwd(zpre, b_in, ln_g, ln_b, w_s, b_s_t, dum, name):
    T = zpre.shape[0]
    E, G, C, GW = SG_WIDTH, SG_GROUPS, SG_CHUNK, SG_GROUP_W

    def body(zp_ref, bin_ref, lng_ref, lnb_ref, ws_ref, bst_ref, dum_ref,
             dz_ref, dbin_ref, dlng_ref, dlnb_ref, dws_ref, dbst_ref):
        @pl.when(pl.program_id(0) == 0)
        def _():
            for r in (dbin_ref, dlng_ref, dlnb_ref, dws_ref, dbst_ref):
                r[...] = jnp.zeros_like(r)

        zin, cdf, u, xhat, rstd, v = _sg_recompute(zp_ref, bin_ref, lng_ref, lnb_ref)
        bst = bst_ref[...]
        lane = lax.broadcasted_iota(jnp.int32, (C, LANES), 1)
        dum_v = dum_ref[...]
        dbst = jnp.zeros((C, LANES), F32)
        du_parts, dv_parts = [], []
        for g in range(G):
            cs = slice(g * GW, (g + 1) * GW)
            wsm = _sg_masked_ws(ws_ref, g)
            vg = _mx(v[:, cs])
            mixed = _dot(wsm, vg, NN) + bst[:, g:g + 1]
            dumg = dum_v[:, cs]
            du_parts.append(dumg * mixed)
            dmixed = dumg * u[:, cs]
            dmb = _mx(dmixed)
            dv_parts.append(_dot(wsm, dmb, TN))
            dws_ref[g] += _dot(dmb, vg, NT)
            dbst = jnp.where(lane == g, jnp.sum(dmixed, axis=1, keepdims=True), dbst)
        dbst_ref[...] += dbst
        du = jnp.concatenate(du_parts, axis=1)
        dv = jnp.concatenate(dv_parts, axis=1)
        dlng_ref[...] += jnp.sum(dv * xhat, axis=0, keepdims=True)
        dlnb_ref[...] += jnp.sum(dv, axis=0, keepdims=True)
        dxh = dv * lng_ref[...]
        dvp = rstd * (dxh - jnp.mean(dxh, axis=-1, keepdims=True) - xhat * jnp.mean(dxh * xhat, axis=-1, keepdims=True))
        dzz = jnp.concatenate([du, dvp], axis=1)
        dzin = dzz * (cdf + zin * (_INV_SQRT_2PI * jnp.exp(-0.5 * zin * zin)))
        dz_ref[...] = _mx(dzin)
        dbin_ref[...] += jnp.sum(dzin, axis=0, keepdims=True)

    return pl.pallas_call(
        body, name=name, grid=(T // C,),
        in_specs=[pl.BlockSpec((C, 2 * E), lambda n: (n, 0)), _full((1, 2 * E)), _full((1, E)), _full((1, E)),
                  _full((G, C, C)), _full((C, LANES)), pl.BlockSpec((C, E), lambda n: (n, 0))],
        out_specs=[pl.BlockSpec((C, 2 * E), lambda n: (n, 0)), _full((1, 2 * E)), _full((1, E)), _full((1, E)),
                   _full((G, C, C)), _full((C, LANES))],
        out_shape=[jax.ShapeDtypeStruct((T, 2 * E), MXU_DTYPE), jax.ShapeDtypeStruct((1, 2 * E), F32),
                   jax.ShapeDtypeStruct((1, E), F32), jax.ShapeDtypeStruct((1, E), F32),
                   jax.ShapeDtypeStruct((G, C, C), F32), jax.ShapeDtypeStruct((C, LANES), F32)],
        compiler_params=_cp("arbitrary"),
    )(zpre, b_in, ln_g, ln_b, w_s, b_s_t, dum)


def _row(v):
    return v.reshape(1, -1)


def _pad_lanes(v):
    v = v.reshape(1, -1)
    return jnp.pad(v, ((0, 0), (0, LANES - v.shape[1])))


def _local_step(x, target, p, weights_for, grads_ready=None):
    ng = p["norm_g"]
    tell = grads_ready if grads_ready is not None else (lambda group, g: None)
    grads = {}
    dng = [[None] * 6 for _ in range(2)]
    saved = []

    def ffn_f(xin, i, j, tag):
        wt = weights_for("ffn" + tag, xin)
        xo, h, gu, y = _ffn_fwd(xin, _row(ng[i, 4 * j]), _row(ng[i, 4 * j + 1]), wt["wgu"], wt["wd"], "ffn_fwd_" + tag)
        return xo, (xin, h, gu, y, wt)

    x1, sv_f00 = ffn_f(x, 0, 0, "00")
    dnw = weights_for("dn", x1)
    hn0 = _norm_fwd(x1, _row(ng[0, 2]), "dn_prenorm")
    proj = _mm(hn0, dnw["dn_wqkvz"], "nn", "dn_proj")
    ba = _mm(hn0, dnw["dn_wba"], "nn", "dn_proj_ba")
    a_log = _pad_lanes(p["dn_a_log"])
    dt_bias = _pad_lanes(p["dn_dt_bias"])
    dn_ng = _row(p["dn_norm_g"])
    qkv = _dn_prep_fwd(proj, p["dn_conv_w"], "dn_prep_fwd")
    beta, gdec = _dn_gate_fwd(ba, a_log, dt_bias, "dn_gate_fwd")
    og, o_raw, tinv, s_all = _dn_scan_fwd(qkv, beta, gdec, proj, dn_ng, "dn_scan_fwd")
    m0 = _mm(og, dnw["dn_wout"], "nn", "dn_out")
    x2 = _postnorm_fwd(x1, m0, _row(ng[0, 3]), "dn_postnorm")
    x3, sv_f01 = ffn_f(x2, 0, 1, "01")
    x4, sv_f10 = ffn_f(x3, 1, 0, "10")
    sgw = weights_for("sg", x4)
    hn1 = _norm_fwd(x4, _row(ng[1, 2]), "sg_prenorm")
    zpre = _mm(hn1, sgw["sg_win"], "nn", "sg_proj")
    sg_bin = _row(p["sg_b_in"])
    sg_lng = _row(p["sg_ln_g"])
    sg_lnb = _row(p["sg_ln_b"])
    sg_bst = jnp.pad(p["sg_b_s"].T, ((0, 0), (0, LANES - SG_GROUPS)))
    um = _sg_fwd(zpre, sg_bin, sg_lng, sg_lnb, p["sg_w_s"], sg_bst, "sg_fwd")
    m1 = _mm(um, sgw["sg_wout"], "nn", "sg_out")
    x5 = _postnorm_fwd(x4, m1, _row(ng[1, 3]), "sg_postnorm")
    x6, sv_f11 = ffn_f(x5, 1, 1, "11")
    loss_part, dx = _loss_fwd_bwd(x6, target, "loss")

    def ffn_b(dxo, sv, i, j, tag):
        xin, h, gu, y, wt = sv
        dy, a, dgu, dg1 = _ffn_bwd_down(dxo, y, gu, _row(ng[i, 4 * j + 1]), wt["wd"], "ffn_bwd_down_" + tag)
        grads["wd" + tag] = _mm(a, dy, "tn", "ffn_wgrad_down_" + tag)
        grads["wguT" + tag] = _mm(dgu, h, "tn", "ffn_wgrad_up_" + tag)
        tell("ffn" + tag, grads)
        dxi, dg0 = _ffn_bwd_up(dgu, xin, dxo, _row(ng[i, 4 * j]), wt["wgu"], "ffn_bwd_up_" + tag)
        dng[i][4 * j] = dg0
        dng[i][4 * j + 1] = dg1
        return dxi

    dx = ffn_b(dx, sv_f11, 1, 1, "11")
    dm1, dng[1][3] = _postnorm_bwd(dx, m1, _row(ng[1, 3]), "sg_postnorm_bwd")
    grads["sg_w_out"] = _mm(um, dm1, "tn", "sg_wgrad_out")
    dum = _mm(dm1, sgw["sg_wout"], "nt", "sg_dgrad_out")
    dz1, dbin, dlng, dlnb, dws, dbst = _sg_bwd(zpre, sg_bin, sg_lng, sg_lnb, p["sg_w_s"], sg_bst, dum, "sg_bwd")
    grads["sg_w_inT"] = _mm(dz1, hn1, "tn", "sg_wgrad_in")
    tell("sg", grads)
    dh1 = _mm(dz1, sgw["sg_win"], "nt", "sg_dgrad_in")
    dx, dng[1][2] = _prenorm_bwd(dx, dh1, x4, _row(ng[1, 2]), "sg_prenorm_bwd")
    grads["sg_b_in"] = dbin.reshape(1, -1)
    grads["sg_ln_g"] = dlng.reshape(1, -1)
    grads["sg_ln_b"] = dlnb.reshape(1, -1)
    grads["sg_w_s"] = jnp.where(jnp.tril(jnp.ones((SG_CHUNK, SG_CHUNK), bool)), dws, 0.0)[None]
    grads["sg_b_s"] = dbst[:, :SG_GROUPS].T[None]
    dx = ffn_b(dx, sv_f10, 1, 0, "10")
    dx = ffn_b(dx, sv_f01, 0, 1, "01")
    dm0, dng[0][3] = _postnorm_bwd(dx, m0, _row(ng[0, 3]), "dn_postnorm_bwd")
    grads["dn_w_out"] = _mm(og, dm0, "tn", "dn_wgrad_out")
    dog = _mm(dm0, dnw["dn_wout"], "nt", "dn_dgrad_out")
    dqkv, dbeta, dgdec, dz0, dnng = _dn_scan_bwd(qkv, beta, gdec, proj, dn_ng, o_raw, tinv, s_all, dog, "dn_scan_bwd")
    dqkv_pre, dconv = _dn_prep_bwd(proj, p["dn_conv_w"], dqkv, "dn_prep_bwd")
    dba, dal, ddt = _dn_gate_bwd(ba, a_log, dt_bias, dbeta, dgdec, "dn_gate_bwd")
    W3 = 3 * DN_HEADS * DN_HEAD_DIM
    dw_qkv = _mm(hn0, dqkv_pre, "tn", "dn_wgrad_qkv")
    dw_z = _mm(hn0, dz0, "tn", "dn_wgrad_z")
    dw_ba = _mm(hn0, dba, "tn", "dn_wgrad_ba")
    grads["dn_w_in"] = jnp.concatenate(
        [dw_qkv, dw_z, dw_ba[:, :DN_HEADS], dw_ba[:, LANES:LANES + DN_HEADS]], axis=1)
    tell("dn", grads)
    dh0 = _mm(dqkv_pre, dnw["dn_wqkvz"][:, :W3], "nt", "dn_dgrad_qkv")
    dh0 = _mm(dz0, dnw["dn_wqkvz"][:, W3:], "nt", "dn_dgrad_z", add=dh0)
    dh0 = _mm(dba, dnw["dn_wba"], "nt", "dn_dgrad_ba", add=dh0)
    dx, dng[0][2] = _prenorm_bwd(dx, dh0, x1, _row(ng[0, 2]), "dn_prenorm_bwd")
    grads["dn_conv_w"] = dconv[None]
    grads["dn_a_log"] = dal[:, :DN_HEADS]
    grads["dn_dt_bias"] = ddt[:, :DN_HEADS]
    grads["dn_norm_g"] = dnng
    dx = ffn_b(dx, sv_f00, 0, 0, "00")
    grads["norm_g"] = jnp.stack([jnp.concatenate(dng[i], axis=0) for i in range(2)])
    return loss_part, dx, grads


def _mesh_pos():
    return lax.axis_index("x"), lax.axis_index("y"), lax.axis_index("c")


def _other_chips(x, y):
    return [(1 - x, y), (x, 1 - y), (1 - x, 1 - y)]


def _allgather_chips(arrs, name):
    n = len(arrs)

    def body(*refs):
        ins, outs = refs[:n], refs[n:2 * n]
        ici_send, ici_recv, d2d_send, d2d_recv = refs[2 * n:]
        x, y, c = _mesh_pos()
        me = 2 * x + y
        chips = _other_chips(x, y)
        sibling = (x, y, 1 - c)

        def ici(i, j, k):
            cx, cy = chips[j]
            return pltpu.make_async_remote_copy(src_ref=ins[i].at[c], dst_ref=outs[i].at[k, c], send_sem=ici_send.at[3 * i + j],
                                                recv_sem=ici_recv.at[3 * i + j], device_id=(cx, cy, c), device_id_type=MESH)

        def d2d(i, j, h):
            cx, cy = chips[j]
            slot = outs[i].at[2 * cx + cy, h]
            return pltpu.make_async_remote_copy(src_ref=slot, dst_ref=slot, send_sem=d2d_send.at[3 * i + j],
                                                recv_sem=d2d_recv.at[3 * i + j], device_id=sibling, device_id_type=MESH)

        sends = [ici(i, j, me) for i in range(n) for j in range(3)]
        for cp in sends:
            cp.start()
        for i in range(n):
            for j, (cx, cy) in enumerate(chips):
                ici(i, j, 2 * cx + cy).wait_recv()
                fwd = d2d(i, j, c)
                fwd.start()
                sends.append(fwd)
        for i in range(n):
            for j in range(3):
                d2d(i, j, 1 - c).wait_recv()
        for cp in sends:
            cp.wait_send()

    return pl.pallas_call(
        body, name=name, in_specs=[ANY] * n, out_specs=[ANY] * n,
        out_shape=[jax.ShapeDtypeStruct((N_CHIPS,) + a.shape, a.dtype) for a in arrs],
        scratch_shapes=[pltpu.SemaphoreType.DMA((3 * n,))] * 4,
    )(*arrs)


def _swap_halves(arrs, half_first, name):
    n = len(arrs)

    def body(*refs):
        ins, outs = refs[:n], refs[n:2 * n]
        send_sems, recv_sems = refs[2 * n:]
        x, y, c = _mesh_pos()
        cps = [pltpu.make_async_remote_copy(src_ref=ins[i].at[1 - c] if half_first[i] else ins[i].at[:, 1 - c],
                                            dst_ref=outs[i], send_sem=send_sems.at[i], recv_sem=recv_sems.at[i],
                                            device_id=(x, y, 1 - c), device_id_type=MESH)
               for i in range(n)]
        for cp in cps:
            cp.start()
        for cp in cps:
            cp.wait()

    return pl.pallas_call(
        body, name=name, in_specs=[ANY] * n, out_specs=[ANY] * n,
        out_shape=[jax.ShapeDtypeStruct((N_CHIPS,) + a.shape[2:], a.dtype) for a in arrs],
        scratch_shapes=[pltpu.SemaphoreType.DMA((n,)), pltpu.SemaphoreType.DMA((n,))],
    )(*arrs)


def _scatter_chips(arrs, name):
    n = len(arrs)

    def body(*refs):
        ins, outs = refs[:n], refs[n:2 * n]
        send_sems, recv_sems = refs[2 * n:]
        x, y, c = _mesh_pos()
        me = 2 * x + y
        chips = _other_chips(x, y)

        def copy(i, j, src_k, dst_k):
            cx, cy = chips[j]
            return pltpu.make_async_remote_copy(src_ref=ins[i].at[src_k], dst_ref=outs[i].at[dst_k],
                                                send_sem=send_sems.at[3 * i + j], recv_sem=recv_sems.at[3 * i + j],
                                                device_id=(cx, cy, c), device_id_type=MESH)

        sends = [copy(i, j, 2 * chips[j][0] + chips[j][1], me) for i in range(n) for j in range(3)]
        for cp in sends:
            cp.start()
        for i in range(n):
            for j, (cx, cy) in enumerate(chips):
                copy(i, j, me, 2 * cx + cy).wait_recv()
        for cp in sends:
            cp.wait_send()

    return pl.pallas_call(
        body, name=name, in_specs=[ANY] * n, out_specs=[ANY] * n,
        out_shape=[jax.ShapeDtypeStruct(a.shape, a.dtype) for a in arrs],
        scratch_shapes=[pltpu.SemaphoreType.DMA((3 * n,)), pltpu.SemaphoreType.DMA((3 * n,))],
    )(*arrs)


HBM = pl.BlockSpec(memory_space=pltpu.HBM)
SEM = pl.BlockSpec(memory_space=pltpu.SEMAPHORE)
TOKEN = jax.ShapeDtypeStruct((SUBLANES, LANES), F32)


def _chip_copies(src_refs, land_refs, send_sems, recv_sems, slice_by_chip, receiving):
    x, y, c = _mesh_pos()
    me = 2 * x + y
    cps = []
    for i, (src, land) in enumerate(zip(src_refs, land_refs)):
        for j, (cx, cy) in enumerate(_other_chips(x, y)):
            peer = 2 * cx + cy
            s = src.at[me if receiving else peer] if slice_by_chip else src
            cps.append(pltpu.make_async_remote_copy(
                src_ref=s, dst_ref=land.at[peer if receiving else me], send_sem=send_sems.at[3 * i + j],
                recv_sem=recv_sems.at[3 * i + j], device_id=(cx, cy, c), device_id_type=MESH))
    return cps


def _chips_start(srcs, slice_by_chip, after, name):
    n = len(srcs)
    lands = [lax.empty((N_CHIPS,) + (s.shape[1:] if slice_by_chip else s.shape), s.dtype) for s in srcs]

    def body(*refs):
        src_refs, land_refs = refs[:n], refs[n:2 * n]
        send_sems, recv_sems = refs[2 * n + 1], refs[2 * n + 2]
        token = refs[-1]
        for cp in _chip_copies(src_refs, land_refs, send_sems, recv_sems, slice_by_chip, False):
            cp.start()
        token[...] = jnp.zeros_like(token)

    outs = pl.pallas_call(
        body, name=name,
        in_specs=[HBM] * (2 * n) + [ANY],
        out_specs=(SEM, SEM) + (HBM,) * (2 * n) + (pl.BlockSpec(memory_space=pltpu.VMEM),),
        out_shape=(pltpu.SemaphoreType.DMA((3 * n,)), pltpu.SemaphoreType.DMA((3 * n,)))
        + tuple(pltpu.HBM(a.shape, a.dtype) for a in list(srcs) + lands) + (TOKEN,),
        input_output_aliases={i: 2 + i for i in range(2 * n)},
        compiler_params=pltpu.CompilerParams(has_side_effects=pltpu.SideEffectType.DATAFLOW_SIDE_EFFECTING),
    )(*[pltpu.with_memory_space_constraint(a, pltpu.HBM) for a in list(srcs) + lands], after)
    return dict(sems=outs[:2], srcs=outs[2:2 + n], lands=outs[2 + n:2 + 2 * n], token=outs[-1], slice_by_chip=slice_by_chip)


def _chips_wait(started, after, name):
    n = len(started["srcs"])
    slice_by_chip = started["slice_by_chip"]

    def body(*refs):
        src_refs, land_refs = refs[:n], refs[n:2 * n]
        send_sems, recv_sems = refs[2 * n], refs[2 * n + 1]
        for cp in _chip_copies(src_refs, land_refs, send_sems, recv_sems, slice_by_chip, True):
            cp.wait_send()
            cp.wait_recv()

    outs = pl.pallas_call(
        body, name=name,
        in_specs=[HBM] * (2 * n) + [SEM, SEM, ANY],
        out_specs=(HBM,) * (2 * n),
        out_shape=tuple(pltpu.HBM(a.shape, a.dtype) for a in list(started["srcs"]) + list(started["lands"])),
        input_output_aliases={i: i for i in range(2 * n)},
        compiler_params=pltpu.CompilerParams(has_side_effects=pltpu.SideEffectType.DATAFLOW_SIDE_EFFECTING),
    )(*started["srcs"], *started["lands"], *started["sems"], after)
    return outs[n:]


def _swap_whole(arrs, name):
    n = len(arrs)

    def body(*refs):
        ins, outs = refs[:n], refs[n:2 * n]
        send_sems, recv_sems = refs[2 * n:]
        x, y, c = _mesh_pos()
        cps = [pltpu.make_async_remote_copy(src_ref=ins[i], dst_ref=outs[i], send_sem=send_sems.at[i],
                                            recv_sem=recv_sems.at[i], device_id=(x, y, 1 - c), device_id_type=MESH)
               for i in range(n)]
        for cp in cps:
            cp.start()
        for cp in cps:
            cp.wait()

    return pl.pallas_call(
        body, name=name, in_specs=[ANY] * n, out_specs=[ANY] * n,
        out_shape=[jax.ShapeDtypeStruct(a.shape, a.dtype) for a in arrs],
        scratch_shapes=[pltpu.SemaphoreType.DMA((n,)), pltpu.SemaphoreType.DMA((n,))],
    )(*arrs)


def _allgather_devices(a, name):
    masks = [(mx, my, mc) for mx in (0, 1) for my in (0, 1) for mc in (0, 1)][1:]

    def body(in_ref, out_ref, send_sems, recv_sems, loc_sem):
        x, y, c = _mesh_pos()
        me = 4 * x + 2 * y + c
        lc = pltpu.make_async_copy(in_ref, out_ref.at[me], loc_sem.at[0])
        lc.start()
        peers = [(jnp.where(mx, 1 - x, x), jnp.where(my, 1 - y, y), jnp.where(mc, 1 - c, c)) for mx, my, mc in masks]
        cps = [pltpu.make_async_remote_copy(src_ref=in_ref, dst_ref=out_ref.at[me], send_sem=send_sems.at[j],
                                            recv_sem=recv_sems.at[j], device_id=peers[j], device_id_type=MESH)
               for j in range(len(masks))]
        for cp in cps:
            cp.start()
        for j, (px, py, pc) in enumerate(peers):
            pltpu.make_async_remote_copy(src_ref=in_ref, dst_ref=out_ref.at[4 * px + 2 * py + pc], send_sem=send_sems.at[j],
                                         recv_sem=recv_sems.at[j], device_id=peers[j], device_id_type=MESH).wait_recv()
        for cp in cps:
            cp.wait_send()
        lc.wait()

    return pl.pallas_call(
        body, name=name, in_specs=[ANY], out_specs=ANY,
        out_shape=jax.ShapeDtypeStruct((N_DEV,) + a.shape, a.dtype),
        scratch_shapes=[pltpu.SemaphoreType.DMA((N_DEV - 1,)), pltpu.SemaphoreType.DMA((N_DEV - 1,)),
                        pltpu.SemaphoreType.DMA((1,))],
    )(a)


def _as_rows(a, lead):
    shp = a.shape
    rows = 1
    for s in shp[lead:-1]:
        rows *= s
    return a.reshape(shp[:lead] + (rows, shp[-1]))


def _row_tile(rows, cols, n_bufs):
    budget = (24 * 1024 * 1024) // (n_bufs * 2 * 4 * cols)
    return _pick(rows, max(2 * SUBLANES, budget), 2 * SUBLANES)


def _sum_leading(a, name):
    n = a.shape[0]
    v = _as_rows(a, 1)
    _, rows, cols = v.shape
    tr = _row_tile(rows, cols, n + 1)

    def body(a_ref, o_ref):
        acc = a_ref[0]
        for k in range(1, n):
            acc = acc + a_ref[k]
        o_ref[...] = acc

    out = pl.pallas_call(body, name=name, grid=(rows // tr,),
                         in_specs=[pl.BlockSpec((n, tr, cols), lambda i: (0, i, 0))],
                         out_specs=pl.BlockSpec((tr, cols), lambda i: (i, 0)),
                         out_shape=jax.ShapeDtypeStruct((rows, cols), F32), compiler_params=_cp("parallel"))(v)
    return out.reshape(a.shape[1:])


def _scalar(i):
    return jnp.reshape(i, (1,)).astype(jnp.int32)


def _add_own_half(g, other, c, half_first, name):
    _, rows, cols = other.shape
    tr = _row_tile(rows, cols, 3)

    def body(c_ref, g_ref, o_ref, out_ref):
        out_ref[0] = (g_ref[0, 0] + o_ref[0]).astype(out_ref.dtype)

    if half_first:
        g_map = lambda k, i, c_ref: (c_ref[0], k, i, 0)
    else:
        g_map = lambda k, i, c_ref: (k, c_ref[0], i, 0)
    flat = pl.BlockSpec((1, tr, cols), lambda k, i, c_ref: (k, i, 0))
    return pl.pallas_call(
        body, name=name,
        grid_spec=pltpu.PrefetchScalarGridSpec(
            num_scalar_prefetch=1, grid=(N_CHIPS, rows // tr),
            in_specs=[pl.BlockSpec((1, 1, tr, cols), g_map), flat], out_specs=flat),
        out_shape=jax.ShapeDtypeStruct(other.shape, COMM_DTYPE), compiler_params=_cp("parallel", "parallel"),
    )(_scalar(c), g, other)


def _sum_chips(own, got, chip, name):
    pv = _as_rows(own, 1)
    bv = _as_rows(got, 1)
    _, rows, cols = pv.shape
    tr = _row_tile(rows, cols, N_CHIPS + 2)

    def body(chip_ref, p_ref, b_ref, o_ref):
        mine = p_ref[0].astype(F32)
        acc = jnp.where(chip_ref[0] == 0, mine, b_ref[0].astype(F32))
        for k in range(1, N_CHIPS):
            acc = acc + jnp.where(chip_ref[0] == k, mine, b_ref[k].astype(F32))
        o_ref[...] = acc

    out = pl.pallas_call(
        body, name=name,
        grid_spec=pltpu.PrefetchScalarGridSpec(
            num_scalar_prefetch=1, grid=(rows // tr,),
            in_specs=[pl.BlockSpec((1, tr, cols), lambda i, k_ref: (k_ref[0], i, 0)),
                      pl.BlockSpec((N_CHIPS, tr, cols), lambda i, k_ref: (0, i, 0))],
            out_specs=pl.BlockSpec((tr, cols), lambda i, k_ref: (i, 0))),
        out_shape=jax.ShapeDtypeStruct((rows, cols), F32), compiler_params=_cp("parallel"),
    )(_scalar(chip), pv, bv)
    return out.reshape(own.shape[1:])


def _adam_math(w, g, m, v):
    nm = ADAM_B1 * m + (1.0 - ADAM_B1) * g
    nv = ADAM_B2 * v + (1.0 - ADAM_B2) * (g * g)
    m_hat = nm / (1.0 - ADAM_B1 ** ADAM_STEP)
    v_hat = nv / (1.0 - ADAM_B2 ** ADAM_STEP)
    return -ADAM_LR * (m_hat / (jnp.sqrt(v_hat) + ADAM_EPS) + ADAM_WD * w), nm, nv


def _adamw_halves(w, mine, theirs, m, v, c, name):
    shape = w.shape
    ws, ms, vs = (_as_rows(t.reshape((2, -1) + t.shape[-1:]), 1) for t in (w, m, v))
    a, b = _as_rows(mine, 0), _as_rows(theirs, 0)
    rows, cols = a.shape
    tr = _row_tile(rows, cols, 9)

    def body(c_ref, w_ref, a_ref, b_ref, m_ref, v_ref, g_ref, d_ref, nm_ref, nv_ref):
        gv = jnp.where(pl.program_id(0) == c_ref[0], a_ref[...], b_ref[...])
        g_ref[0] = gv
        d_ref[0], nm_ref[0], nv_ref[0] = _adam_math(w_ref[0], gv, m_ref[0], v_ref[0])

    half = pl.BlockSpec((1, tr, cols), lambda h, i, c_ref: (h, i, 0))
    flat = pl.BlockSpec((tr, cols), lambda h, i, c_ref: (i, 0))
    outs = pl.pallas_call(
        body, name=name,
        grid_spec=pltpu.PrefetchScalarGridSpec(num_scalar_prefetch=1, grid=(2, rows // tr),
                                               in_specs=[half, flat, flat, half, half], out_specs=[half] * 4),
        out_shape=[jax.ShapeDtypeStruct((2, rows, cols), F32)] * 4, compiler_params=_cp("parallel", "parallel"),
    )(_scalar(c), ws, a, b, ms, vs)
    return tuple(o.reshape(shape) for o in outs)


def _adamw(w, g, m, v, name):
    shape = w.shape
    ws, gs, ms, vs = (_as_rows(t, 0) for t in (w, g, m, v))
    rows, cols = ws.shape
    tr = _row_tile(rows, cols, 7)

    def body(w_ref, g_ref, m_ref, v_ref, d_ref, nm_ref, nv_ref):
        d_ref[...], nm_ref[...], nv_ref[...] = _adam_math(w_ref[...], g_ref[...], m_ref[...], v_ref[...])

    spec = pl.BlockSpec((tr, cols), lambda i: (i, 0))
    outs = pl.pallas_call(body, name=name, grid=(rows // tr,), in_specs=[spec] * 4, out_specs=[spec] * 3,
                          out_shape=[jax.ShapeDtypeStruct((rows, cols), F32)] * 3, compiler_params=_cp("parallel"))(ws, gs, ms, vs)
    return tuple(o.reshape(shape) for o in outs)


_BIG = ["ffn_w_gate", "ffn_w_up", "ffn_w_down", "dn_w_in", "dn_w_out", "sg_w_in", "sg_w_out"]
_SMALL_SHARDED = ["norm_g", "dn_conv_w", "sg_b_in", "sg_ln_g", "sg_ln_b"]
_SMALL_REPL = ["dn_a_log", "dn_dt_bias", "dn_norm_g", "sg_w_s", "sg_b_s"]
_WEIGHTS = ["norm_g", "ffn_w_gate", "ffn_w_up", "ffn_w_down", "dn_w_in", "dn_conv_w", "dn_a_log", "dn_dt_bias",
            "dn_norm_g", "dn_w_out", "sg_w_in", "sg_b_in", "sg_ln_g", "sg_ln_b", "sg_w_s", "sg_b_s", "sg_w_out"]
PACK_COLS = 1024


def _pack(arrs):
    flat = jnp.concatenate([a.reshape(-1) for a in arrs])
    pad = (-flat.shape[0]) % (SUBLANES * PACK_COLS)
    return jnp.pad(flat, (0, pad)).reshape(-1, PACK_COLS)


def _unpack(buf, shapes):
    flat = buf.reshape(-1)
    out, off = [], 0
    for s in shapes:
        n = math.prod(s)
        out.append(flat[off:off + n].reshape(s))
        off += n
    return out


def _as_halves(a):
    if a.shape[0] == 2:
        return a
    if a.shape[0] == 1:
        return a.reshape((2, a.shape[1] // 2) + a.shape[2:])
    return a.reshape((2, a.shape[0] // 2) + a.shape[1:])


def _with_own(gathered, own, chip):
    g = gathered.reshape((N_CHIPS,) + own.shape)
    return [jnp.where(chip == k, own, g[k]) for k in range(N_CHIPS)]


def _cat_shards(g, axis):
    return jnp.concatenate(list(g), axis=axis)


_GROUP_ORDER = ["ffn00", "dn", "ffn01", "ffn10", "sg", "ffn11"]


def _weight_groups(w):
    cast = {k: _mx(w[k]) for k in _BIG}
    groups = {"ffn%d%d" % (i, j): [cast["ffn_w_gate"][i, j], cast["ffn_w_up"][i, j], cast["ffn_w_down"][i, j]]
              for i, j in [(0, 0), (0, 1), (1, 0), (1, 1)]}
    groups["dn"] = [cast["dn_w_in"][0], cast["dn_w_out"][0]]
    groups["sg"] = [cast["sg_w_in"][0], cast["sg_w_out"][0]]
    return groups


def _group_matrices(group, shards):
    if group.startswith("ffn"):
        gate, up, down = shards
        return {"wgu": jnp.concatenate([_cat_shards(gate, 1), _cat_shards(up, 1)], axis=1), "wd": _cat_shards(down, 0)}
    if group == "sg":
        return {"sg_win": _cat_shards(shards[0], 1), "sg_wout": _cat_shards(shards[1], 0)}
    dn_full = _cat_shards(shards[0], 1)
    W4 = 4 * DN_HEADS * DN_HEAD_DIM
    wba = jnp.zeros((D_MODEL, 2 * LANES), dn_full.dtype)
    wba = wba.at[:, :DN_HEADS].set(dn_full[:, W4:W4 + DN_HEADS])
    wba = wba.at[:, LANES:LANES + DN_HEADS].set(dn_full[:, W4 + DN_HEADS:])
    return {"dn_wqkvz": dn_full[:, :W4], "dn_wba": wba, "dn_wout": _cat_shards(shards[1], 0)}


def _split_cols(a, n):
    w = a.shape[-1] // n
    return [a[..., k * w:(k + 1) * w] for k in range(n)]


def _split_rows(a, n):
    h = a.shape[-2] // n
    return [a[..., k * h:(k + 1) * h, :] for k in range(n)]


_IJ = [(0, 0), (0, 1), (1, 0), (1, 1)]


def _grads_by_half_and_chip(grads):
    def rows_by_chip(a):
        return a.reshape(N_CHIPS, 2, a.shape[0] // (2 * N_CHIPS), a.shape[1])

    ops, half_first = [], []
    for i, j in _IJ:
        t = grads["wguT%d%d" % (i, j)]
        ops.append(t.reshape(2, N_CHIPS, t.shape[0] // (2 * N_CHIPS), t.shape[1]))
        half_first.append(True)
    for i, j in _IJ:
        ops.append(rows_by_chip(grads["wd%d%d" % (i, j)]))
        half_first.append(False)
    ops.append(jnp.stack([jnp.stack(_split_cols(hf, N_CHIPS)) for hf in _split_rows(grads["dn_w_in"], 2)]))
    half_first.append(True)
    for k in ("dn_w_out", "sg_w_inT", "sg_w_out"):
        ops.append(rows_by_chip(grads[k]))
        half_first.append(False)
    return ops, half_first


def _shard_grads(mine, theirs, c, w):
    lo = [jnp.where(c == 0, a, b) for a, b in zip(mine, theirs)]
    hi = [jnp.where(c == 0, b, a) for a, b in zip(mine, theirs)]
    rows = lambda t: jnp.concatenate([lo[t], hi[t]], axis=0)
    sq = lambda parts: jnp.stack(parts).reshape(2, 2, *parts[0].shape)
    g = {}
    g["ffn_w_gate"] = sq([lo[t].T for t in range(4)])
    g["ffn_w_up"] = sq([hi[t].T for t in range(4)])
    g["ffn_w_down"] = sq([rows(4 + t) for t in range(4)])
    g["dn_w_in"] = rows(8)[None]
    g["dn_w_out"] = rows(9)[None]
    g["sg_w_in"] = rows(10).T[None]
    g["sg_w_out"] = rows(11)[None]
    return {k: v.reshape(w[k].shape) for k, v in g.items()}


def kernel(x, norm_g, ffn_w_gate, ffn_w_up, ffn_w_down, dn_w_in, dn_conv_w, dn_a_log, dn_dt_bias, dn_norm_g, dn_w_out, sg_w_in, sg_b_in, sg_ln_g, sg_ln_b, sg_w_s, sg_b_s, sg_w_out, loss_target, m_norm_g, m_ffn_w_gate, m_ffn_w_up, m_ffn_w_down, m_dn_w_in, m_dn_conv_w, m_dn_a_log, m_dn_dt_bias, m_dn_norm_g, m_dn_w_out, m_sg_w_in, m_sg_b_in, m_sg_ln_g, m_sg_ln_b, m_sg_w_s, m_sg_b_s, m_sg_w_out, v_norm_g, v_ffn_w_gate, v_ffn_w_up, v_ffn_w_down, v_dn_w_in, v_dn_conv_w, v_dn_a_log, v_dn_dt_bias, v_dn_norm_g, v_dn_w_out, v_sg_w_in, v_sg_b_in, v_sg_ln_g, v_sg_ln_b, v_sg_w_s, v_sg_b_s, v_sg_w_out):
    args = dict(locals())
    w = {k: args[k] for k in _WEIGHTS}
    mom = {k: args["m_" + k] for k in _WEIGHTS}
    var = {k: args["v_" + k] for k in _WEIGHTS}
    cx, cy, cc = _mesh_pos()
    chip = 2 * cx + cy

    small_shapes = [w[k].shape for k in _SMALL_SHARDED]
    groups = _weight_groups(w)
    own = groups[_GROUP_ORDER[0]] + [_pack([w[k] for k in _SMALL_SHARDED])]
    first = _allgather_chips([_as_halves(a) for a in own], "gather_first")
    started, after = {}, first[0]
    for g in _GROUP_ORDER[1:]:
        started[g] = _chips_start(groups[g], False, after, "gather_start_" + g)
        after = started[g]["token"]
    first = [_with_own(g, a, chip) for g, a in zip(first, own)]
    small_k = [_unpack(first[-1][k], small_shapes) for k in range(N_CHIPS)]
    p = {name: jnp.concatenate([small_k[k][i] for k in range(N_CHIPS)], axis=-1) for i, name in enumerate(_SMALL_SHARDED)}
    p = {k: (v if k == "norm_g" else v[0]) for k, v in p.items()}
    for k in _SMALL_REPL:
        p[k] = w[k][0]

    def weights_for(group, after):
        if group == _GROUP_ORDER[0]:
            return _group_matrices(group, first[:-1])
        lands = _chips_wait(started[group], after, "gather_wait_" + group)
        return _group_matrices(group, [_with_own(l, a, chip) for l, a in zip(lands, groups[group])])

    loss_part, grad_x, grads = _local_step(x[0], loss_target[0], p, weights_for)

    halves, half_first = _grads_by_half_and_chip(grads)
    from_sibling = _swap_halves(halves, half_first, "reduce_core_pair")
    pair_sum = [_add_own_half(h, o, cc, hf, "pair_sum_%d" % i)
                for i, (h, o, hf) in enumerate(zip(halves, from_sibling, half_first))]
    from_chips = _scatter_chips(pair_sum, "reduce_chips")
    half_sum = [_sum_chips(own_sum, got, chip, "chip_sum_%d" % i) for i, (own_sum, got) in enumerate(zip(pair_sum, from_chips))]
    other_half = _swap_whole(half_sum, "gather_core_pair")

    small_names = _SMALL_SHARDED + _SMALL_REPL
    small_grads = [grads[k] for k in small_names]
    full_shapes = [g.shape for g in small_grads] + [(1,)]
    pack = _pack(small_grads + [loss_part[0, :1]])
    summed = _sum_leading(_allgather_devices(pack, "gather_small"), "small_sum")
    parts = _unpack(summed, full_shapes)
    loss = parts[-1][0]
    small_grad = {}
    for i, k in enumerate(small_names):
        g = parts[i]
        if k in _SMALL_SHARDED:
            n = w[k].shape[-1]
            g = lax.dynamic_slice_in_dim(g, chip * n, n, axis=g.ndim - 1)
        small_grad[k] = g

    grad = {**small_grad, **_shard_grads(half_sum, other_half, cc, w)}
    delta, new_m, new_v = {}, {}, {}
    for k in _BIG:
        delta[k], new_m[k], new_v[k] = _adamw(w[k], grad[k], mom[k], var[k], "adamw_" + k)
    shapes = [w[k].shape for k in small_names]
    d, nm, nv = _adamw(_pack([w[k] for k in small_names]), _pack([grad[k] for k in small_names]),
                       _pack([mom[k] for k in small_names]), _pack([var[k] for k in small_names]), "adamw_small")
    for k, a, b, c_ in zip(small_names, _unpack(d, shapes), _unpack(nm, shapes), _unpack(nv, shapes)):
        delta[k], new_m[k], new_v[k] = a, b, c_

    return (loss, grad_x[None], *[grad[k] for k in _WEIGHTS], *[delta[k] for k in _WEIGHTS],
            *[new_m[k] for k in _WEIGHTS], *[new_v[k] for k in _WEIGHTS])
```

```python
import functools
import math

import jax
import jax.numpy as jnp
from jax import lax
from jax.experimental import pallas as pl
from jax.experimental.pallas import tpu as pltpu

F32 = jnp.float32
MXU_DTYPE = jnp.bfloat16
COMM_DTYPE = jnp.bfloat16
HI = lax.Precision.HIGHEST
TRI_PREC = lax.Precision.HIGH

D_MODEL = 1024
D_FF = 2816
RMS_EPS = 1e-6
LN_EPS = 1e-5
L2_EPS = 1e-6
DN_HEADS = 8
DN_HEAD_DIM = 128
DN_CONV = 4
DN_CHUNK = 64
SG_WIDTH = 2048
SG_GROUPS = 8
SG_CHUNK = 128
SG_GROUP_W = SG_WIDTH // SG_GROUPS
N_CHIPS = 4
N_DEV = 8
LANES = 128
SUBLANES = 8
VMEM_LIMIT = 56 * 1024 * 1024

ADAM_LR = 0.001
ADAM_B1 = 0.9
ADAM_B2 = 0.999
ADAM_EPS = 1e-08
ADAM_WD = 0.01
ADAM_STEP = 10

MESH = pl.DeviceIdType.MESH
ANY = pl.BlockSpec(memory_space=pl.ANY)


def _cp(*sem):
    return pltpu.CompilerParams(dimension_semantics=sem, vmem_limit_bytes=VMEM_LIMIT)


def _pick(n, pref, mult=LANES):
    best = None
    d = mult
    while d <= min(n, pref):
        if n % d == 0:
            best = d
        d += mult
    return best if best is not None else n


def _full(shape):
    nd = len(shape)
    return pl.BlockSpec(shape, lambda *_: (0,) * nd)


def _sigmoid(x):
    return 1.0 / (1.0 + jnp.exp(-x))


def _dot(a, b, dims, prec=None):
    return lax.dot_general(a, b, (dims, ((), ())), preferred_element_type=F32, precision=prec)


NN = ((1,), (0,))
NT = ((1,), (1,))
TN = ((0,), (0,))


def _mx(a):
    return a.astype(MXU_DTYPE)


def _rms_stat(x):
    return lax.rsqrt(jnp.mean(x * x, axis=-1, keepdims=True) + RMS_EPS)


def _rms_bwd(x, r, g, dy):
    xh = x * r
    dxh = dy * g
    dx = r * (dxh - xh * jnp.mean(dxh * xh, axis=-1, keepdims=True))
    return dx, jnp.sum(dy * xh, axis=0, keepdims=True)


def _mm(a, b, mode, name, out_dtype=F32, add=None):
    if mode == "tn":
        K, M = a.shape
        N = b.shape[1]
    elif mode == "nt":
        M, K = a.shape
        N = b.shape[0]
    else:
        M, K = a.shape
        N = b.shape[1]
    tn = _pick(N, 1024)
    if mode == "tn":
        tm = _pick(M, 1024 if tn <= 512 else 1408)
        tk = _pick(K, 1024, SUBLANES)
    else:
        tm = _pick(M, max(512, min(2048, (512 * 1024) // tn)), SUBLANES)
        tk = _pick(K, 2048)
    nk = K // tk
    grid = (N // tn, M // tm, nk)
    if mode == "nn":
        a_spec = pl.BlockSpec((tm, tk), lambda j, i, k: (i, k))
        b_spec = pl.BlockSpec((tk, tn), lambda j, i, k: (k, j))
        dims = NN
    elif mode == "nt":
        a_spec = pl.BlockSpec((tm, tk), lambda j, i, k: (i, k))
        b_spec = pl.BlockSpec((tn, tk), lambda j, i, k: (j, k))
        dims = NT
    else:
        a_spec = pl.BlockSpec((tk, tm), lambda j, i, k: (k, i))
        b_spec = pl.BlockSpec((tk, tn), lambda j, i, k: (k, j))
        dims = TN
    o_spec = pl.BlockSpec((tm, tn), lambda j, i, k: (i, j))
    has_add = add is not None

    def body(*refs):
        if has_add:
            a_ref, b_ref, add_ref, o_ref, acc = refs
        else:
            a_ref, b_ref, o_ref, acc = refs
        k = pl.program_id(2)

        @pl.when(k == 0)
        def _():
            acc[...] = add_ref[...] if has_add else jnp.zeros_like(acc)

        acc[...] += _dot(a_ref[...], b_ref[...], dims)

        @pl.when(k == nk - 1)
        def _():
            o_ref[...] = acc[...].astype(o_ref.dtype)

    ins = [a, b] + ([add] if has_add else [])
    specs = [a_spec, b_spec] + ([o_spec] if has_add else [])
    return pl.pallas_call(
        body, name=name, grid=grid, in_specs=specs, out_specs=o_spec,
        out_shape=jax.ShapeDtypeStruct((M, N), out_dtype),
        scratch_shapes=[pltpu.VMEM((tm, tn), F32)],
        compiler_params=_cp("parallel", "parallel", "arbitrary"),
    )(*ins)


def _load_resident(pairs, sem):
    @pl.when(pl.program_id(0) == 0)
    def _():
        cps = [pltpu.make_async_copy(src, dst, sem.at[i]) for i, (src, dst) in enumerate(pairs)]
        for c in cps:
            c.start()
        for c in cps:
            c.wait()


def _ffn_fwd(x, g0, g1, wgu, wd, name):
    T, D = x.shape
    F2 = wgu.shape[1]
    F = F2 // 2
    tm = _pick(T, 256, SUBLANES)

    def body(x_ref, g0_ref, g1_ref, wgu_hbm, wd_hbm, xo_ref, h_ref, gu_ref, y_ref, wgu_v, wd_v, sem):
        _load_resident([(wgu_hbm, wgu_v), (wd_hbm, wd_v)], sem)
        xv = x_ref[...]
        hb = _mx(xv * _rms_stat(xv) * g0_ref[...])
        h_ref[...] = hb
        gu = _dot(hb, wgu_v[...], NN)
        gu_ref[...] = gu
        g = gu[:, :F]
        u = gu[:, F:]
        a = _mx(g * _sigmoid(g) * u)
        y = _dot(a, wd_v[...], NN)
        y_ref[...] = y
        xo_ref[...] = xv + 0.5 * (y * _rms_stat(y) * g1_ref[...])

    row = lambda w: pl.BlockSpec((tm, w), lambda i: (i, 0))
    return pl.pallas_call(
        body, name=name, grid=(T // tm,),
        in_specs=[row(D), _full((1, D)), _full((1, D)), ANY, ANY],
        out_specs=[row(D), row(D), row(F2), row(D)],
        out_shape=[jax.ShapeDtypeStruct((T, D), F32), jax.ShapeDtypeStruct((T, D), MXU_DTYPE),
                   jax.ShapeDtypeStruct((T, F2), F32), jax.ShapeDtypeStruct((T, D), F32)],
        scratch_shapes=[pltpu.VMEM(wgu.shape, wgu.dtype), pltpu.VMEM(wd.shape, wd.dtype),
                        pltpu.SemaphoreType.DMA((2,))],
        compiler_params=_cp("arbitrary"),
    )(x, g0, g1, wgu, wd)


def _ffn_bwd_down(dxo, y, gu, g1, wd, name):
    T, D = y.shape
    F2 = gu.shape[1]
    F = F2 // 2
    tm = _pick(T, 256, SUBLANES)

    def body(dxo_ref, y_ref, gu_ref, g1_ref, wd_hbm, dy_ref, a_ref, dgu_ref, dg1_ref, wd_v, sem):
        _load_resident([(wd_hbm, wd_v)], sem)

        @pl.when(pl.program_id(0) == 0)
        def _():
            dg1_ref[...] = jnp.zeros_like(dg1_ref)

        yv = y_ref[...]
        dy, dg1 = _rms_bwd(yv, _rms_stat(yv), g1_ref[...], 0.5 * dxo_ref[...])
        dg1_ref[...] += dg1
        dyb = _mx(dy)
        dy_ref[...] = dyb
        da = _dot(dyb, wd_v[...], NT)
        gu_v = gu_ref[...]
        g = gu_v[:, :F]
        u = gu_v[:, F:]
        s = _sigmoid(g)
        sg = g * s
        a_ref[...] = _mx(sg * u)
        dgu_ref[:, :F] = _mx(da * u * (s * (1.0 + g * (1.0 - s))))
        dgu_ref[:, F:] = _mx(da * sg)

    row = lambda w: pl.BlockSpec((tm, w), lambda i: (i, 0))
    return pl.pallas_call(
        body, name=name, grid=(T // tm,),
        in_specs=[row(D), row(D), row(F2), _full((1, D)), ANY],
        out_specs=[row(D), row(F), row(F2), _full((1, D))],
        out_shape=[jax.ShapeDtypeStruct((T, D), MXU_DTYPE), jax.ShapeDtypeStruct((T, F), MXU_DTYPE),
                   jax.ShapeDtypeStruct((T, F2), MXU_DTYPE), jax.ShapeDtypeStruct((1, D), F32)],
        scratch_shapes=[pltpu.VMEM(wd.shape, wd.dtype), pltpu.SemaphoreType.DMA((1,))],
        compiler_params=_cp("arbitrary"),
    )(dxo, y, gu, g1, wd)


def _ffn_bwd_up(dgu, x, dxo, g0, wgu, name):
    T, D = x.shape
    F2 = dgu.shape[1]
    tm = _pick(T, 256, SUBLANES)

    def body(dgu_ref, x_ref, dxo_ref, g0_ref, wgu_hbm, dx_ref, dg0_ref, wgu_v, sem):
        _load_resident([(wgu_hbm, wgu_v)], sem)

        @pl.when(pl.program_id(0) == 0)
        def _():
            dg0_ref[...] = jnp.zeros_like(dg0_ref)

        dh = _dot(dgu_ref[...], wgu_v[...], NT)
        xv = x_ref[...]
        dx, dg0 = _rms_bwd(xv, _rms_stat(xv), g0_ref[...], dh)
        dg0_ref[...] += dg0
        dx_ref[...] = dxo_ref[...] + dx

    row = lambda w: pl.BlockSpec((tm, w), lambda i: (i, 0))
    return pl.pallas_call(
        body, name=name, grid=(T // tm,),
        in_specs=[row(F2), row(D), row(D), _full((1, D)), ANY],
        out_specs=[row(D), _full((1, D))],
        out_shape=[jax.ShapeDtypeStruct((T, D), F32), jax.ShapeDtypeStruct((1, D), F32)],
        scratch_shapes=[pltpu.VMEM(wgu.shape, wgu.dtype), pltpu.SemaphoreType.DMA((1,))],
        compiler_params=_cp("arbitrary"),
    )(dgu, x, dxo, g0, wgu)


def _norm_fwd(x, g, name):
    T, D = x.shape
    tm = _pick(T, 512, SUBLANES)

    def body(x_ref, g_ref, h_ref):
        xv = x_ref[...]
        h_ref[...] = _mx(xv * _rms_stat(xv) * g_ref[...])

    row = pl.BlockSpec((tm, D), lambda i: (i, 0))
    return pl.pallas_call(body, name=name, grid=(T // tm,), in_specs=[row, _full((1, D))], out_specs=row,
                          out_shape=jax.ShapeDtypeStruct((T, D), MXU_DTYPE), compiler_params=_cp("parallel"))(x, g)


def _postnorm_fwd(x, m, g, name):
    T, D = x.shape
    tm = _pick(T, 512, SUBLANES)

    def body(x_ref, m_ref, g_ref, o_ref):
        mv = m_ref[...]
        o_ref[...] = x_ref[...] + mv * _rms_stat(mv) * g_ref[...]

    row = pl.BlockSpec((tm, D), lambda i: (i, 0))
    return pl.pallas_call(body, name=name, grid=(T // tm,), in_specs=[row, row, _full((1, D))], out_specs=row,
                          out_shape=jax.ShapeDtypeStruct((T, D), F32), compiler_params=_cp("parallel"))(x, m, g)


def _postnorm_bwd(dxo, m, g, name):
    T, D = m.shape
    tm = _pick(T, 512, SUBLANES)

    def body(dxo_ref, m_ref, g_ref, dm_ref, dg_ref):
        @pl.when(pl.program_id(0) == 0)
        def _():
            dg_ref[...] = jnp.zeros_like(dg_ref)

        mv = m_ref[...]
        dm, dg = _rms_bwd(mv, _rms_stat(mv), g_ref[...], dxo_ref[...])
        dg_ref[...] += dg
        dm_ref[...] = _mx(dm)

    row = pl.BlockSpec((tm, D), lambda i: (i, 0))
    return pl.pallas_call(body, name=name, grid=(T // tm,), in_specs=[row, row, _full((1, D))],
                          out_specs=[row, _full((1, D))],
                          out_shape=[jax.ShapeDtypeStruct((T, D), MXU_DTYPE), jax.ShapeDtypeStruct((1, D), F32)],
                          compiler_params=_cp("arbitrary"))(dxo, m, g)


def _prenorm_bwd(dxo, dh, x, g, name):
    T, D = x.shape
    tm = _pick(T, 512, SUBLANES)

    def body(dxo_ref, dh_ref, x_ref, g_ref, dx_ref, dg_ref):
        @pl.when(pl.program_id(0) == 0)
        def _():
            dg_ref[...] = jnp.zeros_like(dg_ref)

        xv = x_ref[...]
        dx, dg = _rms_bwd(xv, _rms_stat(xv), g_ref[...], dh_ref[...])
        dg_ref[...] += dg
        dx_ref[...] = dxo_ref[...] + dx

    row = pl.BlockSpec((tm, D), lambda i: (i, 0))
    return pl.pallas_call(body, name=name, grid=(T // tm,), in_specs=[row, row, row, _full((1, D))],
                          out_specs=[row, _full((1, D))],
                          out_shape=[jax.ShapeDtypeStruct((T, D), F32), jax.ShapeDtypeStruct((1, D), F32)],
                          compiler_params=_cp("arbitrary"))(dxo, dh, x, g)


def _loss_fwd_bwd(y, target, name):
    T, D = y.shape
    tm = _pick(T, 512, SUBLANES)

    def body(y_ref, t_ref, l_ref, dy_ref):
        @pl.when(pl.program_id(0) == 0)
        def _():
            l_ref[...] = jnp.zeros_like(l_ref)

        e = y_ref[...] - t_ref[...]
        dy_ref[...] = e * (1.0 / D)
        l_ref[...] += 0.5 * jnp.sum(jnp.mean(e * e, axis=-1, keepdims=True), axis=0, keepdims=True)

    row = pl.BlockSpec((tm, D), lambda i: (i, 0))
    return pl.pallas_call(body, name=name, grid=(T // tm,), in_specs=[row, row],
                          out_specs=[_full((SUBLANES, LANES)), row],
                          out_shape=[jax.ShapeDtypeStruct((SUBLANES, LANES), F32), jax.ShapeDtypeStruct((T, D), F32)],
                          compiler_params=_cp("arbitrary"))(y, target)


DN_ROWS = 512


def _shift_down(prev8, cur, s):
    n = cur.shape[0]
    xx = jnp.concatenate([prev8, cur], axis=0)
    return pltpu.roll(xx, s, 0)[SUBLANES:SUBLANES + n, :]


def _shift_up(cur, next8, s):
    n = cur.shape[0]
    xx = jnp.concatenate([cur, next8], axis=0)
    return pltpu.roll(xx, n + SUBLANES - s, 0)[:n, :]


def _conv_tile(x_ref, w, r, rows):
    start = pl.multiple_of(r * rows, SUBLANES)
    cur = x_ref[pl.ds(start, rows), :]
    pstart = pl.multiple_of(jnp.maximum(start - SUBLANES, 0), SUBLANES)
    prev8 = jnp.where(r == 0, 0.0, x_ref[pl.ds(pstart, SUBLANES), :])
    taps = [_shift_down(prev8, cur, DN_CONV - 1 - j) if j < DN_CONV - 1 else cur for j in range(DN_CONV)]
    c = taps[0] * w[0:1, :]
    for j in range(1, DN_CONV):
        c = c + taps[j] * w[j:j + 1, :]
    return c, taps


def _dn_prep_fwd(proj, conv_w, name):
    T = proj.shape[0]
    W = DN_HEADS * DN_HEAD_DIM
    rows = min(DN_ROWS, T)
    n_inner = T // rows
    scale = DN_HEAD_DIM ** -0.5

    def body(x_ref, w_ref, o_ref):
        cb = pl.program_id(0)
        w = w_ref[...]
        is_qk = cb < 2 * DN_HEADS
        post = jnp.where(cb < DN_HEADS, scale, 1.0)

        def step(r, carry):
            c, _ = _conv_tile(x_ref, w, r, rows)
            s = c * _sigmoid(c)
            rinv = lax.rsqrt(jnp.sum(s * s, axis=-1, keepdims=True) + L2_EPS)
            o_ref[pl.ds(pl.multiple_of(r * rows, SUBLANES), rows), :] = jnp.where(is_qk, s * rinv * post, s)
            return carry

        lax.fori_loop(0, n_inner, step, 0)

    col = pl.BlockSpec((T, LANES), lambda j: (0, j))
    return pl.pallas_call(body, name=name, grid=(3 * W // LANES,),
                          in_specs=[col, pl.BlockSpec((DN_CONV, LANES), lambda j: (0, j))], out_specs=col,
                          out_shape=jax.ShapeDtypeStruct((T, 3 * W), F32), compiler_params=_cp("parallel"))(proj, conv_w)


def _dn_prep_bwd(proj, conv_w, dqkv, name):
    T = proj.shape[0]
    W = DN_HEADS * DN_HEAD_DIM
    rows = min(DN_ROWS, T)
    n_inner = T // rows
    scale = DN_HEAD_DIM ** -0.5

    def body(x_ref, w_ref, dy_ref, dx_ref, dw_ref, dc_scr):
        cb = pl.program_id(0)
        w = w_ref[...]
        is_qk = cb < 2 * DN_HEADS
        post = jnp.where(cb < DN_HEADS, scale, 1.0)

        def step1(r, dws):
            c, taps = _conv_tile(x_ref, w, r, rows)
            sg = _sigmoid(c)
            s = c * sg
            rinv = lax.rsqrt(jnp.sum(s * s, axis=-1, keepdims=True) + L2_EPS)
            dy = dy_ref[pl.ds(pl.multiple_of(r * rows, SUBLANES), rows), :]
            yn = s * rinv
            dyn = dy * post
            ds_qk = rinv * (dyn - yn * jnp.sum(dyn * yn, axis=-1, keepdims=True))
            ds = jnp.where(is_qk, ds_qk, dy)
            dc = ds * (sg * (1.0 + c * (1.0 - sg)))
            dc_scr[pl.ds(pl.multiple_of(r * rows, SUBLANES), rows), :] = dc
            return tuple(dws[j] + jnp.sum(dc * taps[j], axis=0, keepdims=True) for j in range(DN_CONV))

        zero = jnp.zeros((1, LANES), F32)
        dws = lax.fori_loop(0, n_inner, step1, (zero,) * DN_CONV)
        for j in range(DN_CONV):
            dw_ref[j:j + 1, :] = dws[j]

        def step2(r, carry):
            start = pl.multiple_of(r * rows, SUBLANES)
            cur = dc_scr[pl.ds(start, rows), :]
            nstart = pl.multiple_of(jnp.minimum(start + rows, T - SUBLANES), SUBLANES)
            next8 = jnp.where(r == n_inner - 1, 0.0, dc_scr[pl.ds(nstart, SUBLANES), :])
            dx = cur * w[DN_CONV - 1:DN_CONV, :]
            for j in range(DN_CONV - 1):
                dx = dx + _shift_up(cur, next8, DN_CONV - 1 - j) * w[j:j + 1, :]
            dx_ref[pl.ds(start, rows), :] = _mx(dx)
            return carry

        lax.fori_loop(0, n_inner, step2, 0)

    col = pl.BlockSpec((T, LANES), lambda j: (0, j))
    wspec = pl.BlockSpec((DN_CONV, LANES), lambda j: (0, j))
    return pl.pallas_call(body, name=name, grid=(3 * W // LANES,), in_specs=[col, wspec, col], out_specs=[col, wspec],
                          out_shape=[jax.ShapeDtypeStruct((T, 3 * W), MXU_DTYPE), jax.ShapeDtypeStruct((DN_CONV, 3 * W), F32)],
                          scratch_shapes=[pltpu.VMEM((T, LANES), F32)], compiler_params=_cp("parallel"))(proj, conv_w, dqkv)


def _softplus(x):
    return jnp.maximum(x, 0.0) + jnp.log(1.0 + jnp.exp(-jnp.abs(x)))


def _dn_gate_fwd(ba, a_log, dt_bias, name):
    T = ba.shape[0]
    tm = _pick(T, 1024, SUBLANES)

    def body(ba_ref, al_ref, dt_ref, beta_ref, g_ref):
        beta_ref[...] = _sigmoid(ba_ref[:, :LANES])
        g_ref[...] = -jnp.exp(al_ref[...]) * _softplus(ba_ref[:, LANES:] + dt_ref[...])

    row = lambda w: pl.BlockSpec((tm, w), lambda i: (i, 0))
    return pl.pallas_call(body, name=name, grid=(T // tm,), in_specs=[row(2 * LANES), _full((1, LANES)), _full((1, LANES))],
                          out_specs=[row(LANES), row(LANES)],
                          out_shape=[jax.ShapeDtypeStruct((T, LANES), F32)] * 2, compiler_params=_cp("parallel"))(ba, a_log, dt_bias)


def _dn_gate_bwd(ba, a_log, dt_bias, dbeta, dg, name):
    T = ba.shape[0]
    tm = _pick(T, 1024, SUBLANES)

    def body(ba_ref, al_ref, dt_ref, dbeta_ref, dg_ref, dba_ref, dal_ref, ddt_ref):
        @pl.when(pl.program_id(0) == 0)
        def _():
            dal_ref[...] = jnp.zeros_like(dal_ref)
            ddt_ref[...] = jnp.zeros_like(ddt_ref)

        beta = _sigmoid(ba_ref[:, :LANES])
        dba_ref[:, :LANES] = _mx(dbeta_ref[...] * beta * (1.0 - beta))
        pre = ba_ref[:, LANES:] + dt_ref[...]
        ea = jnp.exp(al_ref[...])
        dgv = dg_ref[...]
        da = dgv * (-ea) * _sigmoid(pre)
        dba_ref[:, LANES:] = _mx(da)
        ddt_ref[...] += jnp.sum(da, axis=0, keepdims=True)
        dal_ref[...] += jnp.sum(dgv * (-ea) * _softplus(pre), axis=0, keepdims=True)

    row = lambda w: pl.BlockSpec((tm, w), lambda i: (i, 0))
    one = _full((1, LANES))
    return pl.pallas_call(body, name=name, grid=(T // tm,), in_specs=[row(2 * LANES), one, one, row(LANES), row(LANES)],
                          out_specs=[row(2 * LANES), one, one],
                          out_shape=[jax.ShapeDtypeStruct((T, 2 * LANES), MXU_DTYPE), jax.ShapeDtypeStruct((1, LANES), F32),
                                     jax.ShapeDtypeStruct((1, LANES), F32)],
                          compiler_params=_cp("arbitrary"))(ba, a_log, dt_bias, dbeta, dg)


def _tri(c, strict):
    i = lax.broadcasted_iota(jnp.int32, (c, c), 0)
    j = lax.broadcasted_iota(jnp.int32, (c, c), 1)
    return (i > j) if strict else (i >= j)


def _inv_unit_lower(ls):
    c = ls[0].shape[0]
    i = lax.broadcasted_iota(jnp.int32, (c, c), 0)
    j = lax.broadcasted_iota(jnp.int32, (c, c), 1)
    eye = jnp.where(i == j, 1.0, 0.0)
    facs = [[eye - l for l in ls]]
    cur = ls
    for _ in range(int(math.log2(c)) - 1):
        cur = [_dot(p, p, NN, TRI_PREC) for p in cur]
        facs.append([eye + p for p in cur])
    while len(facs) > 1:
        nxt = [[_dot(a, b, NN, TRI_PREC) for a, b in zip(facs[t], facs[t + 1])] for t in range(0, len(facs) - 1, 2)]
        if len(facs) % 2:
            nxt.append(facs[-1])
        facs = nxt
    return facs[0]


def _chunk_gates(g_blk):
    c = g_blk.shape[0]
    gcs = _dot(jnp.where(_tri(c, False), 1.0, 0.0), g_blk, NN, HI)
    return gcs, gcs.T


def _head_chunk(h, qh, kh, vh, beta_blk, gcs, gcs_t):
    c = qh.shape[0]
    incl = _tri(c, False)
    gc_col = gcs[:, h:h + 1]
    gc_row = gcs_t[h:h + 1, :]
    gc_last = gcs_t[h:h + 1, c - 1:c]
    dec = jnp.where(incl, jnp.exp(jnp.where(incl, gc_col - gc_row, 0.0)), 0.0)
    gam = jnp.exp(gc_col)
    rr = jnp.exp(gc_last - gc_col)
    gl = jnp.exp(gc_last)
    b = beta_blk[:, h:h + 1]
    kb = kh * b
    vb = vh * b
    kk = _dot(_mx(kb), _mx(kh), NT)
    lmat = jnp.where(_tri(c, True), kk * dec, 0.0)
    qk = _dot(_mx(qh), _mx(kh), NT)
    pmat = jnp.where(incl, qk * dec, 0.0)
    return dict(dec=dec, gam=gam, rr=rr, gl=gl, b=b, kb=kb, vb=vb, lmat=lmat, pmat=pmat)


def _dn_scan_fwd(qkv, beta, g, proj, norm_g, name):
    T = qkv.shape[0]
    C, H, Dh = DN_CHUNK, DN_HEADS, DN_HEAD_DIM
    W = H * Dh
    N = T // C

    def body(q_ref, k_ref, v_ref, beta_ref, g_ref, z_ref, ng_ref, og_ref, o_ref, tinv_ref, s_ref, state):
        @pl.when(pl.program_id(0) == 0)
        def _():
            state[...] = jnp.zeros_like(state)

        gcs, gcs_t = _chunk_gates(g_ref[...])
        beta_blk = beta_ref[...]
        ng = ng_ref[...]
        heads = range(H)
        cs = [slice(h * Dh, (h + 1) * Dh) for h in heads]
        qs = [_head_chunk(h, q_ref[:, cs[h]], k_ref[:, cs[h]], v_ref[:, cs[h]], beta_blk, gcs, gcs_t) for h in heads]
        tinvs = _inv_unit_lower([q["lmat"] for q in qs])
        for h in heads:
            tinv_ref[h] = tinvs[h]
        us = [_dot(tinvs[h], qs[h]["vb"], NN, TRI_PREC) for h in heads]
        ws = [_dot(tinvs[h], qs[h]["kb"] * qs[h]["gam"], NN, TRI_PREC) for h in heads]
        ss = [state[h] for h in heads]
        for h in heads:
            s_ref[0, h] = ss[h]
        sbs = [_mx(s) for s in ss]
        vnbs = [_mx(us[h] - _dot(_mx(ws[h]), sbs[h], NN)) for h in heads]
        os_ = [_dot(_mx(q_ref[:, cs[h]] * qs[h]["gam"]), sbs[h], NN) + _dot(_mx(qs[h]["pmat"]), vnbs[h], NN) for h in heads]
        for h in heads:
            state[h] = ss[h] * qs[h]["gl"] + _dot(_mx((k_ref[:, cs[h]] * qs[h]["rr"]).T), vnbs[h], NN)
        for h in heads:
            o = os_[h]
            o_ref[:, cs[h]] = o
            zh = z_ref[:, cs[h]]
            og_ref[:, cs[h]] = _mx(o * _rms_stat(o) * ng * (zh * _sigmoid(zh)))

    blk = lambda j: pl.BlockSpec((C, W), lambda n: (n, j))
    small = pl.BlockSpec((C, LANES), lambda n: (n, 0))
    return pl.pallas_call(
        body, name=name, grid=(N,),
        in_specs=[blk(0), blk(1), blk(2), small, small, blk(3), _full((1, Dh))],
        out_specs=[blk(0), blk(0), pl.BlockSpec((H, C, C), lambda n: (0, n, 0)),
                   pl.BlockSpec((1, H, Dh, Dh), lambda n: (n, 0, 0, 0))],
        out_shape=[jax.ShapeDtypeStruct((T, W), MXU_DTYPE), jax.ShapeDtypeStruct((T, W), F32),
                   jax.ShapeDtypeStruct((H, T, C), F32), jax.ShapeDtypeStruct((N, H, Dh, Dh), F32)],
        scratch_shapes=[pltpu.VMEM((H, Dh, Dh), F32)],
        compiler_params=_cp("arbitrary"),
    )(qkv, qkv, qkv, beta, g, proj, norm_g)


def _dn_scan_bwd(qkv, beta, g, proj, norm_g, o, tinv, s_all, dog, name):
    T = qkv.shape[0]
    C, H, Dh = DN_CHUNK, DN_HEADS, DN_HEAD_DIM
    W = H * Dh
    N = T // C

    def body(q_ref, k_ref, v_ref, beta_ref, g_ref, z_ref, ng_ref, o_ref, tinv_ref, s_ref, dog_ref,
             dqkv_ref, dbeta_ref, dg_ref, dz_ref, dng_ref, dstate):
        @pl.when(pl.program_id(0) == 0)
        def _():
            dstate[...] = jnp.zeros_like(dstate)
            dng_ref[...] = jnp.zeros_like(dng_ref)

        gcs, gcs_t = _chunk_gates(g_ref[...])
        beta_blk = beta_ref[...]
        ng = ng_ref[...]
        incl = _tri(C, False)
        strict = _tri(C, True)
        lane = lax.broadcasted_iota(jnp.int32, (C, LANES), 1)
        rowi = lax.broadcasted_iota(jnp.int32, (C, 1), 0)
        ones = jnp.ones((C, LANES), F32)
        dbeta_acc = jnp.zeros((C, LANES), F32)
        dgc_acc = jnp.zeros((C, LANES), F32)
        dng_acc = jnp.zeros((1, Dh), F32)
        heads = range(H)
        cs = [slice(h * Dh, (h + 1) * Dh) for h in heads]
        rsum = lambda t: jnp.sum(t, axis=1, keepdims=True)
        dobs = []
        for h in heads:
            oh, zh, dogh = o_ref[:, cs[h]], z_ref[:, cs[h]], dog_ref[:, cs[h]]
            rstat = _rms_stat(oh)
            sz = _sigmoid(zh)
            dz_ref[:, cs[h]] = _mx(dogh * (oh * rstat * ng) * (sz * (1.0 + zh * (1.0 - sz))))
            do, dng = _rms_bwd(oh, rstat, ng, dogh * (zh * sz))
            dng_acc = dng_acc + dng
            dobs.append(_mx(do))
        qs = [_head_chunk(h, q_ref[:, cs[h]], k_ref[:, cs[h]], v_ref[:, cs[h]], beta_blk, gcs, gcs_t) for h in heads]
        tms = [tinv_ref[h] for h in heads]
        us = [_dot(tms[h], qs[h]["vb"], NN, TRI_PREC) for h in heads]
        ws = [_dot(tms[h], qs[h]["kb"] * qs[h]["gam"], NN, TRI_PREC) for h in heads]
        ss = [s_ref[0, h] for h in heads]
        sbs = [_mx(s) for s in ss]
        wbs = [_mx(w) for w in ws]
        vnbs = [_mx(us[h] - _dot(wbs[h], sbs[h], NN)) for h in heads]
        dsns = [dstate[h] for h in heads]
        dsbs = [_mx(d) for d in dsns]
        dvnews = [_dot(_mx(qs[h]["pmat"]), dobs[h], TN) + _dot(_mx(k_ref[:, cs[h]] * qs[h]["rr"]), dsbs[h], NN) for h in heads]
        dvb16s = [_mx(d) for d in dvnews]
        dps = [jnp.where(incl, _dot(dobs[h], vnbs[h], NT), 0.0) for h in heads]
        dqds = [_dot(dobs[h], sbs[h], NT) for h in heads]
        dkds = [_dot(vnbs[h], dsbs[h], NT) for h in heads]
        dgls = [jnp.sum(rsum(ss[h] * dsns[h]), axis=0, keepdims=True) for h in heads]
        dws = [-_dot(dvb16s[h], sbs[h], NT) for h in heads]
        for h in heads:
            dstate[h] = (_dot(_mx(q_ref[:, cs[h]] * qs[h]["gam"]), dobs[h], TN) + qs[h]["gl"] * dsns[h]
                         - _dot(wbs[h], dvb16s[h], TN))
        dvbs = [_dot(tms[h], dvnews[h], TN, TRI_PREC) for h in heads]
        dkbgs = [_dot(tms[h], dws[h], TN, TRI_PREC) for h in heads]
        dls = [jnp.where(strict, -(_dot(dvbs[h], us[h], NT, TRI_PREC) + _dot(dkbgs[h], ws[h], NT, TRI_PREC)), 0.0)
               for h in heads]
        mmats = [dls[h] * qs[h]["lmat"] + dps[h] * qs[h]["pmat"] for h in heads]
        dgcs = [rsum(mmats[h]) - _dot(mmats[h], ones, TN, HI)[:, :1] for h in heads]
        dkk16s = [_mx(dls[h] * qs[h]["dec"]) for h in heads]
        dqk16s = [_mx(dps[h] * qs[h]["dec"]) for h in heads]
        for h in heads:
            q = qs[h]
            qh, kh, vh = q_ref[:, cs[h]], k_ref[:, cs[h]], v_ref[:, cs[h]]
            gam, rr, b, kb = q["gam"], q["rr"], q["b"], q["kb"]
            dkb = _dot(dkk16s[h], _mx(kh), NN) + dkbgs[h] * gam
            dk = _dot(dkk16s[h], _mx(kb), TN) + _dot(dqk16s[h], _mx(qh), TN) + dkb * b + dkds[h] * rr
            dq = _dot(dqk16s[h], _mx(kh), NN) + dqds[h] * gam
            dgam = rsum(dkbgs[h] * kb) + rsum(dqds[h] * qh)
            dr = rsum(dkds[h] * kh)
            dgc_last = jnp.sum(dr * rr, axis=0, keepdims=True) + dgls[h] * q["gl"]
            dgc = dgcs[h] + dgam * gam - dr * rr + jnp.where(rowi == C - 1, dgc_last, 0.0)
            dbeta = rsum(dvbs[h] * vh) + rsum(dkb * kh)
            dqkv_ref[:, cs[h]] = dq
            dqkv_ref[:, W + h * Dh:W + (h + 1) * Dh] = dk
            dqkv_ref[:, 2 * W + h * Dh:2 * W + (h + 1) * Dh] = dvbs[h] * b
            dbeta_acc = jnp.where(lane == h, dbeta, dbeta_acc)
            dgc_acc = jnp.where(lane == h, dgc, dgc_acc)
        dbeta_ref[...] = dbeta_acc
        dg_ref[...] = _dot(jnp.where(incl, 1.0, 0.0), dgc_acc, TN, HI)
        dng_ref[...] += dng_acc

    rev = lambda n: N - 1 - n
    blk = lambda j: pl.BlockSpec((C, W), lambda n: (rev(n), j))
    small = pl.BlockSpec((C, LANES), lambda n: (rev(n), 0))
    return pl.pallas_call(
        body, name=name, grid=(N,),
        in_specs=[blk(0), blk(1), blk(2), small, small, blk(3), _full((1, Dh)), blk(0),
                  pl.BlockSpec((H, C, C), lambda n: (0, rev(n), 0)),
                  pl.BlockSpec((1, H, Dh, Dh), lambda n: (rev(n), 0, 0, 0)), blk(0)],
        out_specs=[pl.BlockSpec((C, 3 * W), lambda n: (rev(n), 0)), small, small, blk(0), _full((1, Dh))],
        out_shape=[jax.ShapeDtypeStruct((T, 3 * W), F32), jax.ShapeDtypeStruct((T, LANES), F32),
                   jax.ShapeDtypeStruct((T, LANES), F32), jax.ShapeDtypeStruct((T, W), MXU_DTYPE),
                   jax.ShapeDtypeStruct((1, Dh), F32)],
        scratch_shapes=[pltpu.VMEM((H, Dh, Dh), F32)],
        compiler_params=_cp("arbitrary"),
    )(qkv, qkv, qkv, beta, g, proj, norm_g, o, tinv, s_all, dog)


_INV_SQRT2 = 0.7071067811865476
_INV_SQRT_2PI = 0.3989422804014327


def _sg_recompute(zp_ref, bin_ref, lng_ref, lnb_ref):
    E = SG_WIDTH
    zin = zp_ref[...] + bin_ref[...]
    cdf = 0.5 * (1.0 + lax.erf(zin * _INV_SQRT2))
    zz = zin * cdf
    u = zz[:, :E]
    vp = zz[:, E:]
    mu = jnp.mean(vp, axis=-1, keepdims=True)
    xc = vp - mu
    rstd = lax.rsqrt(jnp.mean(xc * xc, axis=-1, keepdims=True) + LN_EPS)
    xhat = xc * rstd
    v = xhat * lng_ref[...] + lnb_ref[...]
    return zin, cdf, u, xhat, rstd, v


def _sg_masked_ws(ws_ref, g):
    return _mx(jnp.where(_tri(SG_CHUNK, False), ws_ref[g], 0.0))


def _sg_fwd(zpre, b_in, ln_g, ln_b, w_s, b_s_t, name):
    T = zpre.shape[0]
    E, G, C, GW = SG_WIDTH, SG_GROUPS, SG_CHUNK, SG_GROUP_W

    def body(zp_ref, bin_ref, lng_ref, lnb_ref, ws_ref, bst_ref, um_ref):
        _, _, u, _, _, v = _sg_recompute(zp_ref, bin_ref, lng_ref, lnb_ref)
        bst = bst_ref[...]
        for g in range(G):
            cs = slice(g * GW, (g + 1) * GW)
            mixed = _dot(_sg_masked_ws(ws_ref, g), _mx(v[:, cs]), NN) + bst[:, g:g + 1]
            um_ref[:, cs] = _mx(u[:, cs] * mixed)

    return pl.pallas_call(
        body, name=name, grid=(T // C,),
        in_specs=[pl.BlockSpec((C, 2 * E), lambda n: (n, 0)), _full((1, 2 * E)), _full((1, E)), _full((1, E)),
                  _full((G, C, C)), _full((C, LANES))],
        out_specs=pl.BlockSpec((C, E), lambda n: (n, 0)),
        out_shape=jax.ShapeDtypeStruct((T, E), MXU_DTYPE), compiler_params=_cp("parallel"),
    )(zpre, b_in, ln_g, ln_b, w_s, b_s_t)


def _sg_bwd(zpre, b_in, ln_g, ln_b, w_s, b_s_t, dum, name):
    T = zpre.shape[0]
    E, G, C, GW = SG_WIDTH, SG_GROUPS, SG_CHUNK, SG_GROUP_W

    def body(zp_ref, bin_ref, lng_ref, lnb_ref, ws_ref, bst_ref, dum_ref,
             dz_ref, dbin_ref, dlng_ref, dlnb_ref, dws_ref, dbst_ref):
        @pl.when(pl.program_id(0) == 0)
        def _():
            for r in (dbin_ref, dlng_ref, dlnb_ref, dws_ref, dbst_ref):
                r[...] = jnp.zeros_like(r)

        zin, cdf, u, xhat, rstd, v = _sg_recompute(zp_ref, bin_ref, lng_ref, lnb_ref)
        bst = bst_ref[...]
        lane = lax.broadcasted_iota(jnp.int32, (C, LANES), 1)
        dum_v = dum_ref[...]
        dbst = jnp.zeros((C, LANES), F32)
        du_parts, dv_parts = [], []
        for g in range(G):
            cs = slice(g * GW, (g + 1) * GW)
            wsm = _sg_masked_ws(ws_ref, g)
            vg = _mx(v[:, cs])
            mixed = _dot(wsm, vg, NN) + bst[:, g:g + 1]
            dumg = dum_v[:, cs]
            du_parts.append(dumg * mixed)
            dmixed = dumg * u[:, cs]
            dmb = _mx(dmixed)
            dv_parts.append(_dot(wsm, dmb, TN))
            dws_ref[g] += _dot(dmb, vg, NT)
            dbst = jnp.where(lane == g, jnp.sum(dmixed, axis=1, keepdims=True), dbst)
        dbst_ref[...] += dbst
        du = jnp.concatenate(du_parts, axis=1)
        dv = jnp.concatenate(dv_parts, axis=1)
        dlng_ref[...] += jnp.sum(dv * xhat, axis=0, keepdims=True)
        dlnb_ref[...] += jnp.sum(dv, axis=0, keepdims=True)
        dxh = dv * lng_ref[...]
        dvp = rstd * (dxh - jnp.mean(dxh, axis=-1, keepdims=True) - xhat * jnp.mean(dxh * xhat, axis=-1, keepdims=True))
        dzz = jnp.concatenate([du, dvp], axis=1)
        dzin = dzz * (cdf + zin * (_INV_SQRT_2PI * jnp.exp(-0.5 * zin * zin)))
        dz_ref[...] = _mx(dzin)
        dbin_ref[...] += jnp.sum(dzin, axis=0, keepdims=True)

    return pl.pallas_call(
        body, name=name, grid=(T // C,),
        in_specs=[pl.BlockSpec((C, 2 * E), lambda n: (n, 0)), _full((1, 2 * E)), _full((1, E)), _full((1, E)),
                  _full((G, C, C)), _full((C, LANES)), pl.BlockSpec((C, E), lambda n: (n, 0))],
        out_specs=[pl.BlockSpec((C, 2 * E), lambda n: (n, 0)), _full((1, 2 * E)), _full((1, E)), _full((1, E)),
                   _full((G, C, C)), _full((C, LANES))],
        out_shape=[jax.ShapeDtypeStruct((T, 2 * E), MXU_DTYPE), jax.ShapeDtypeStruct((1, 2 * E), F32),
                   jax.ShapeDtypeStruct((1, E), F32), jax.ShapeDtypeStruct((1, E), F32),
                   jax.ShapeDtypeStruct((G, C, C), F32), jax.ShapeDtypeStruct((C, LANES), F32)],
        compiler_params=_cp("arbitrary"),
    )(zpre, b_in, ln_g, ln_b, w_s, b_s_t, dum)


def _row(v):
    return v.reshape(1, -1)


def _pad_lanes(v):
    v = v.reshape(1, -1)
    return jnp.pad(v, ((0, 0), (0, LANES - v.shape[1])))


def _local_step(x, target, p, weights_for, grads_ready=None):
    ng = p["norm_g"]
    tell = grads_ready if grads_ready is not None else (lambda group, g: None)
    grads = {}
    dng = [[None] * 6 for _ in range(2)]
    saved = []

    def ffn_f(xin, i, j, tag):
        wt = weights_for("ffn" + tag, xin)
        xo, h, gu, y = _ffn_fwd(xin, _row(ng[i, 4 * j]), _row(ng[i, 4 * j + 1]), wt["wgu"], wt["wd"], "ffn_fwd_" + tag)
        return xo, (xin, h, gu, y, wt)

    x1, sv_f00 = ffn_f(x, 0, 0, "00")
    dnw = weights_for("dn", x1)
    hn0 = _norm_fwd(x1, _row(ng[0, 2]), "dn_prenorm")
    proj = _mm(hn0, dnw["dn_wqkvz"], "nn", "dn_proj")
    ba = _mm(hn0, dnw["dn_wba"], "nn", "dn_proj_ba")
    a_log = _pad_lanes(p["dn_a_log"])
    dt_bias = _pad_lanes(p["dn_dt_bias"])
    dn_ng = _row(p["dn_norm_g"])
    qkv = _dn_prep_fwd(proj, p["dn_conv_w"], "dn_prep_fwd")
    beta, gdec = _dn_gate_fwd(ba, a_log, dt_bias, "dn_gate_fwd")
    og, o_raw, tinv, s_all = _dn_scan_fwd(qkv, beta, gdec, proj, dn_ng, "dn_scan_fwd")
    m0 = _mm(og, dnw["dn_wout"], "nn", "dn_out")
    x2 = _postnorm_fwd(x1, m0, _row(ng[0, 3]), "dn_postnorm")
    x3, sv_f01 = ffn_f(x2, 0, 1, "01")
    x4, sv_f10 = ffn_f(x3, 1, 0, "10")
    sgw = weights_for("sg", x4)
    hn1 = _norm_fwd(x4, _row(ng[1, 2]), "sg_prenorm")
    zpre = _mm(hn1, sgw["sg_win"], "nn", "sg_proj")
    sg_bin = _row(p["sg_b_in"])
    sg_lng = _row(p["sg_ln_g"])
    sg_lnb = _row(p["sg_ln_b"])
    sg_bst = jnp.pad(p["sg_b_s"].T, ((0, 0), (0, LANES - SG_GROUPS)))
    um = _sg_fwd(zpre, sg_bin, sg_lng, sg_lnb, p["sg_w_s"], sg_bst, "sg_fwd")
    m1 = _mm(um, sgw["sg_wout"], "nn", "sg_out")
    x5 = _postnorm_fwd(x4, m1, _row(ng[1, 3]), "sg_postnorm")
    x6, sv_f11 = ffn_f(x5, 1, 1, "11")
    loss_part, dx = _loss_fwd_bwd(x6, target, "loss")

    def ffn_b(dxo, sv, i, j, tag):
        xin, h, gu, y, wt = sv
        dy, a, dgu, dg1 = _ffn_bwd_down(dxo, y, gu, _row(ng[i, 4 * j + 1]), wt["wd"], "ffn_bwd_down_" + tag)
        grads["wd" + tag] = _mm(a, dy, "tn", "ffn_wgrad_down_" + tag)
        grads["wguT" + tag] = _mm(dgu, h, "tn", "ffn_wgrad_up_" + tag)
        tell("ffn" + tag, grads)
        dxi, dg0 = _ffn_bwd_up(dgu, xin, dxo, _row(ng[i, 4 * j]), wt["wgu"], "ffn_bwd_up_" + tag)
        dng[i][4 * j] = dg0
        dng[i][4 * j + 1] = dg1
        return dxi

    dx = ffn_b(dx, sv_f11, 1, 1, "11")
    dm1, dng[1][3] = _postnorm_bwd(dx, m1, _row(ng[1, 3]), "sg_postnorm_bwd")
    grads["sg_w_out"] = _mm(um, dm1, "tn", "sg_wgrad_out")
    dum = _mm(dm1, sgw["sg_wout"], "nt", "sg_dgrad_out")
    dz1, dbin, dlng, dlnb, dws, dbst = _sg_bwd(zpre, sg_bin, sg_lng, sg_lnb, p["sg_w_s"], sg_bst, dum, "sg_bwd")
    grads["sg_w_inT"] = _mm(dz1, hn1, "tn", "sg_wgrad_in")
    tell("sg", grads)
    dh1 = _mm(dz1, sgw["sg_win"], "nt", "sg_dgrad_in")
    dx, dng[1][2] = _prenorm_bwd(dx, dh1, x4, _row(ng[1, 2]), "sg_prenorm_bwd")
    grads["sg_b_in"] = dbin.reshape(1, -1)
    grads["sg_ln_g"] = dlng.reshape(1, -1)
    grads["sg_ln_b"] = dlnb.reshape(1, -1)
    grads["sg_w_s"] = jnp.where(jnp.tril(jnp.ones((SG_CHUNK, SG_CHUNK), bool)), dws, 0.0)[None]
    grads["sg_b_s"] = dbst[:, :SG_GROUPS].T[None]
    dx = ffn_b(dx, sv_f10, 1, 0, "10")
    dx = ffn_b(dx, sv_f01, 0, 1, "01")
    dm0, dng[0][3] = _postnorm_bwd(dx, m0, _row(ng[0, 3]), "dn_postnorm_bwd")
    grads["dn_w_out"] = _mm(og, dm0, "tn", "dn_wgrad_out")
    dog = _mm(dm0, dnw["dn_wout"], "nt", "dn_dgrad_out")
    dqkv, dbeta, dgdec, dz0, dnng = _dn_scan_bwd(qkv, beta, gdec, proj, dn_ng, o_raw, tinv, s_all, dog, "dn_scan_bwd")
    dqkv_pre, dconv = _dn_prep_bwd(proj, p["dn_conv_w"], dqkv, "dn_prep_bwd")
    dba, dal, ddt = _dn_gate_bwd(ba, a_log, dt_bias, dbeta, dgdec, "dn_gate_bwd")
    W3 = 3 * DN_HEADS * DN_HEAD_DIM
    dw_qkv = _mm(hn0, dqkv_pre, "tn", "dn_wgrad_qkv")
    dw_z = _mm(hn0, dz0, "tn", "dn_wgrad_z")
    dw_ba = _mm(hn0, dba, "tn", "dn_wgrad_ba")
    grads["dn_w_in"] = jnp.concatenate(
        [dw_qkv, dw_z, dw_ba[:, :DN_HEADS], dw_ba[:, LANES:LANES + DN_HEADS]], axis=1)
    tell("dn", grads)
    dh0 = _mm(dqkv_pre, dnw["dn_wqkvz"][:, :W3], "nt", "dn_dgrad_qkv")
    dh0 = _mm(dz0, dnw["dn_wqkvz"][:, W3:], "nt", "dn_dgrad_z", add=dh0)
    dh0 = _mm(dba, dnw["dn_wba"], "nt", "dn_dgrad_ba", add=dh0)
    dx, dng[0][2] = _prenorm_bwd(dx, dh0, x1, _row(ng[0, 2]), "dn_prenorm_bwd")
    grads["dn_conv_w"] = dconv[None]
    grads["dn_a_log"] = dal[:, :DN_HEADS]
    grads["dn_dt_bias"] = ddt[:, :DN_HEADS]
    grads["dn_norm_g"] = dnng
    dx = ffn_b(dx, sv_f00, 0, 0, "00")
    grads["norm_g"] = jnp.stack([jnp.concatenate(dng[i], axis=0) for i in range(2)])
    return loss_part, dx, grads


def _mesh_pos():
    return lax.axis_index("x"), lax.axis_index("y"), lax.axis_index("c")


def _other_chips(x, y):
    return [(1 - x, y), (x, 1 - y), (1 - x, 1 - y)]


def _allgather_chips(arrs, name):
    n = len(arrs)

    def body(*refs):
        ins, outs = refs[:n], refs[n:2 * n]
        ici_send, ici_recv, d2d_send, d2d_recv = refs[2 * n:]
        x, y, c = _mesh_pos()
        me = 2 * x + y
        chips = _other_chips(x, y)
        sibling = (x, y, 1 - c)

        def ici(i, j, k):
            cx, cy = chips[j]
            return pltpu.make_async_remote_copy(src_ref=ins[i].at[c], dst_ref=outs[i].at[k, c], send_sem=ici_send.at[3 * i + j],
                                                recv_sem=ici_recv.at[3 * i + j], device_id=(cx, cy, c), device_id_type=MESH)

        def d2d(i, j, h):
            cx, cy = chips[j]
            slot = outs[i].at[2 * cx + cy, h]
            return pltpu.make_async_remote_copy(src_ref=slot, dst_ref=slot, send_sem=d2d_send.at[3 * i + j],
                                                recv_sem=d2d_recv.at[3 * i + j], device_id=sibling, device_id_type=MESH)

        sends = [ici(i, j, me) for i in range(n) for j in range(3)]
        for cp in sends:
            cp.start()
        for i in range(n):
            for j, (cx, cy) in enumerate(chips):
                ici(i, j, 2 * cx + cy).wait_recv()
                fwd = d2d(i, j, c)
                fwd.start()
                sends.append(fwd)
        for i in range(n):
            for j in range(3):
                d2d(i, j, 1 - c).wait_recv()
        for cp in sends:
            cp.wait_send()

    return pl.pallas_call(
        body, name=name, in_specs=[ANY] * n, out_specs=[ANY] * n,
        out_shape=[jax.ShapeDtypeStruct((N_CHIPS,) + a.shape, a.dtype) for a in arrs],
        scratch_shapes=[pltpu.SemaphoreType.DMA((3 * n,))] * 4,
    )(*arrs)


def _swap_halves(arrs, half_first, name):
    n = len(arrs)

    def body(*refs):
        ins, outs = refs[:n], refs[n:2 * n]
        send_sems, recv_sems = refs[2 * n:]
        x, y, c = _mesh_pos()
        cps = [pltpu.make_async_remote_copy(src_ref=ins[i].at[1 - c] if half_first[i] else ins[i].at[:, 1 - c],
                                            dst_ref=outs[i], send_sem=send_sems.at[i], recv_sem=recv_sems.at[i],
                                            device_id=(x, y, 1 - c), device_id_type=MESH)
               for i in range(n)]
        for cp in cps:
            cp.start()
        for cp in cps:
            cp.wait()

    return pl.pallas_call(
        body, name=name, in_specs=[ANY] * n, out_specs=[ANY] * n,
        out_shape=[jax.ShapeDtypeStruct((N_CHIPS,) + a.shape[2:], a.dtype) for a in arrs],
        scratch_shapes=[pltpu.SemaphoreType.DMA((n,)), pltpu.SemaphoreType.DMA((n,))],
    )(*arrs)


def _scatter_chips(arrs, name):
    n = len(arrs)

    def body(*refs):
        ins, outs = refs[:n], refs[n:2 * n]
        send_sems, recv_sems = refs[2 * n:]
        x, y, c = _mesh_pos()
        me = 2 * x + y
        chips = _other_chips(x, y)

        def copy(i, j, src_k, dst_k):
            cx, cy = chips[j]
            return pltpu.make_async_remote_copy(src_ref=ins[i].at[src_k], dst_ref=outs[i].at[dst_k],
                                                send_sem=send_sems.at[3 * i + j], recv_sem=recv_sems.at[3 * i + j],
                                                device_id=(cx, cy, c), device_id_type=MESH)

        sends = [copy(i, j, 2 * chips[j][0] + chips[j][1], me) for i in range(n) for j in range(3)]
        for cp in sends:
            cp.start()
        for i in range(n):
            for j, (cx, cy) in enumerate(chips):
                copy(i, j, me, 2 * cx + cy).wait_recv()
        for cp in sends:
            cp.wait_send()

    return pl.pallas_call(
        body, name=name, in_specs=[ANY] * n, out_specs=[ANY] * n,
        out_shape=[jax.ShapeDtypeStruct(a.shape, a.dtype) for a in arrs],
        scratch_shapes=[pltpu.SemaphoreType.DMA((3 * n,)), pltpu.SemaphoreType.DMA((3 * n,))],
    )(*arrs)


HBM = pl.BlockSpec(memory_space=pltpu.HBM)
SEM = pl.BlockSpec(memory_space=pltpu.SEMAPHORE)
TOKEN = jax.ShapeDtypeStruct((SUBLANES, LANES), F32)


def _chip_copies(src_refs, land_refs, send_sems, recv_sems, slice_by_chip, receiving):
    x, y, c = _mesh_pos()
    me = 2 * x + y
    cps = []
    for i, (src, land) in enumerate(zip(src_refs, land_refs)):
        for j, (cx, cy) in enumerate(_other_chips(x, y)):
            peer = 2 * cx + cy
            s = src.at[me if receiving else peer] if slice_by_chip else src
            cps.append(pltpu.make_async_remote_copy(
                src_ref=s, dst_ref=land.at[peer if receiving else me], send_sem=send_sems.at[3 * i + j],
                recv_sem=recv_sems.at[3 * i + j], device_id=(cx, cy, c), device_id_type=MESH))
    return cps


def _chips_start(srcs, slice_by_chip, after, name):
    n = len(srcs)
    lands = [lax.empty((N_CHIPS,) + (s.shape[1:] if slice_by_chip else s.shape), s.dtype) for s in srcs]

    def body(*refs):
        src_refs, land_refs = refs[:n], refs[n:2 * n]
        send_sems, recv_sems = refs[2 * n + 1], refs[2 * n + 2]
        token = refs[-1]
        for cp in _chip_copies(src_refs, land_refs, send_sems, recv_sems, slice_by_chip, False):
            cp.start()
        token[...] = jnp.zeros_like(token)

    outs = pl.pallas_call(
        body, name=name,
        in_specs=[HBM] * (2 * n) + [ANY],
        out_specs=(SEM, SEM) + (HBM,) * (2 * n) + (pl.BlockSpec(memory_space=pltpu.VMEM),),
        out_shape=(pltpu.SemaphoreType.DMA((3 * n,)), pltpu.SemaphoreType.DMA((3 * n,)))
        + tuple(pltpu.HBM(a.shape, a.dtype) for a in list(srcs) + lands) + (TOKEN,),
        input_output_aliases={i: 2 + i for i in range(2 * n)},
        compiler_params=pltpu.CompilerParams(has_side_effects=pltpu.SideEffectType.DATAFLOW_SIDE_EFFECTING),
    )(*[pltpu.with_memory_space_constraint(a, pltpu.HBM) for a in list(srcs) + lands], after)
    return dict(sems=outs[:2], srcs=outs[2:2 + n], lands=outs[2 + n:2 + 2 * n], token=outs[-1], slice_by_chip=slice_by_chip)


def _chips_wait(started, after, name):
    n = len(started["srcs"])
    slice_by_chip = started["slice_by_chip"]

    def body(*refs):
        src_refs, land_refs = refs[:n], refs[n:2 * n]
        send_sems, recv_sems = refs[2 * n], refs[2 * n + 1]
        for cp in _chip_copies(src_refs, land_refs, send_sems, recv_sems, slice_by_chip, True):
            cp.wait_send()
            cp.wait_recv()

    outs = pl.pallas_call(
        body, name=name,
        in_specs=[HBM] * (2 * n) + [SEM, SEM, ANY],
        out_specs=(HBM,) * (2 * n),
        out_shape=tuple(pltpu.HBM(a.shape, a.dtype) for a in list(started["srcs"]) + list(started["lands"])),
        input_output_aliases={i: i for i in range(2 * n)},
        compiler_params=pltpu.CompilerParams(has_side_effects=pltpu.SideEffectType.DATAFLOW_SIDE_EFFECTING),
    )(*started["srcs"], *started["lands"], *started["sems"], after)
    return outs[:n], outs[n:]


def _swap_whole(arrs, name):
    n = len(arrs)

    def body(*refs):
        ins, outs = refs[:n], refs[n:2 * n]
        send_sems, recv_sems = refs[2 * n:]
        x, y, c = _mesh_pos()
        cps = [pltpu.make_async_remote_copy(src_ref=ins[i], dst_ref=outs[i], send_sem=send_sems.at[i],
                                            recv_sem=recv_sems.at[i], device_id=(x, y, 1 - c), device_id_type=MESH)
               for i in range(n)]
        for cp in cps:
            cp.start()
        for cp in cps:
            cp.wait()

    return pl.pallas_call(
        body, name=name, in_specs=[ANY] * n, out_specs=[ANY] * n,
        out_shape=[jax.ShapeDtypeStruct(a.shape, a.dtype) for a in arrs],
        scratch_shapes=[pltpu.SemaphoreType.DMA((n,)), pltpu.SemaphoreType.DMA((n,))],
    )(*arrs)


def _allgather_devices(a, name):
    masks = [(mx, my, mc) for mx in (0, 1) for my in (0, 1) for mc in (0, 1)][1:]

    def body(in_ref, out_ref, send_sems, recv_sems, loc_sem):
        x, y, c = _mesh_pos()
        me = 4 * x + 2 * y + c
        lc = pltpu.make_async_copy(in_ref, out_ref.at[me], loc_sem.at[0])
        lc.start()
        peers = [(jnp.where(mx, 1 - x, x), jnp.where(my, 1 - y, y), jnp.where(mc, 1 - c, c)) for mx, my, mc in masks]
        cps = [pltpu.make_async_remote_copy(src_ref=in_ref, dst_ref=out_ref.at[me], send_sem=send_sems.at[j],
                                            recv_sem=recv_sems.at[j], device_id=peers[j], device_id_type=MESH)
               for j in range(len(masks))]
        for cp in cps:
            cp.start()
        for j, (px, py, pc) in enumerate(peers):
            pltpu.make_async_remote_copy(src_ref=in_ref, dst_ref=out_ref.at[4 * px + 2 * py + pc], send_sem=send_sems.at[j],
                                         recv_sem=recv_sems.at[j], device_id=peers[j], device_id_type=MESH).wait_recv()
        for cp in cps:
            cp.wait_send()
        lc.wait()

    return pl.pallas_call(
        body, name=name, in_specs=[ANY], out_specs=ANY,
        out_shape=jax.ShapeDtypeStruct((N_DEV,) + a.shape, a.dtype),
        scratch_shapes=[pltpu.SemaphoreType.DMA((N_DEV - 1,)), pltpu.SemaphoreType.DMA((N_DEV - 1,)),
                        pltpu.SemaphoreType.DMA((1,))],
    )(a)


def _as_rows(a, lead):
    shp = a.shape
    rows = 1
    for s in shp[lead:-1]:
        rows *= s
    return a.reshape(shp[:lead] + (rows, shp[-1]))


def _row_tile(rows, cols, n_bufs):
    budget = (24 * 1024 * 1024) // (n_bufs * 2 * 4 * cols)
    return _pick(rows, max(2 * SUBLANES, budget), 2 * SUBLANES)


def _sum_leading(a, name):
    n = a.shape[0]
    v = _as_rows(a, 1)
    _, rows, cols = v.shape
    tr = _row_tile(rows, cols, n + 1)

    def body(a_ref, o_ref):
        acc = a_ref[0]
        for k in range(1, n):
            acc = acc + a_ref[k]
        o_ref[...] = acc

    out = pl.pallas_call(body, name=name, grid=(rows // tr,),
                         in_specs=[pl.BlockSpec((n, tr, cols), lambda i: (0, i, 0))],
                         out_specs=pl.BlockSpec((tr, cols), lambda i: (i, 0)),
                         out_shape=jax.ShapeDtypeStruct((rows, cols), F32), compiler_params=_cp("parallel"))(v)
    return out.reshape(a.shape[1:])


def _scalar(i):
    return jnp.reshape(i, (1,)).astype(jnp.int32)


def _add_own_half(g, other, c, half_first, name):
    _, rows, cols = other.shape
    tr = _row_tile(rows, cols, 3)

    def body(c_ref, g_ref, o_ref, out_ref):
        out_ref[0] = (g_ref[0, 0] + o_ref[0]).astype(out_ref.dtype)

    if half_first:
        g_map = lambda k, i, c_ref: (c_ref[0], k, i, 0)
    else:
        g_map = lambda k, i, c_ref: (k, c_ref[0], i, 0)
    flat = pl.BlockSpec((1, tr, cols), lambda k, i, c_ref: (k, i, 0))
    return pl.pallas_call(
        body, name=name,
        grid_spec=pltpu.PrefetchScalarGridSpec(
            num_scalar_prefetch=1, grid=(N_CHIPS, rows // tr),
            in_specs=[pl.BlockSpec((1, 1, tr, cols), g_map), flat], out_specs=flat),
        out_shape=jax.ShapeDtypeStruct(other.shape, COMM_DTYPE), compiler_params=_cp("parallel", "parallel"),
    )(_scalar(c), g, other)


def _sum_chips(own, got, chip, name):
    pv = _as_rows(own, 1)
    bv = _as_rows(got, 1)
    _, rows, cols = pv.shape
    tr = _row_tile(rows, cols, N_CHIPS + 2)

    def body(chip_ref, p_ref, b_ref, o_ref):
        mine = p_ref[0].astype(F32)
        acc = jnp.where(chip_ref[0] == 0, mine, b_ref[0].astype(F32))
        for k in range(1, N_CHIPS):
            acc = acc + jnp.where(chip_ref[0] == k, mine, b_ref[k].astype(F32))
        o_ref[...] = acc

    out = pl.pallas_call(
        body, name=name,
        grid_spec=pltpu.PrefetchScalarGridSpec(
            num_scalar_prefetch=1, grid=(rows // tr,),
            in_specs=[pl.BlockSpec((1, tr, cols), lambda i, k_ref: (k_ref[0], i, 0)),
                      pl.BlockSpec((N_CHIPS, tr, cols), lambda i, k_ref: (0, i, 0))],
            out_specs=pl.BlockSpec((tr, cols), lambda i, k_ref: (i, 0))),
        out_shape=jax.ShapeDtypeStruct((rows, cols), F32), compiler_params=_cp("parallel"),
    )(_scalar(chip), pv, bv)
    return out.reshape(own.shape[1:])


def _adam_math(w, g, m, v):
    nm = ADAM_B1 * m + (1.0 - ADAM_B1) * g
    nv = ADAM_B2 * v + (1.0 - ADAM_B2) * (g * g)
    m_hat = nm / (1.0 - ADAM_B1 ** ADAM_STEP)
    v_hat = nv / (1.0 - ADAM_B2 ** ADAM_STEP)
    return -ADAM_LR * (m_hat / (jnp.sqrt(v_hat) + ADAM_EPS) + ADAM_WD * w), nm, nv


def _adamw_halves(w, mine, theirs, m, v, c, name):
    shape = w.shape
    ws, ms, vs = (_as_rows(t.reshape((2, -1) + t.shape[-1:]), 1) for t in (w, m, v))
    a, b = _as_rows(mine, 0), _as_rows(theirs, 0)
    rows, cols = a.shape
    tr = _row_tile(rows, cols, 9)

    def body(c_ref, w_ref, a_ref, b_ref, m_ref, v_ref, g_ref, d_ref, nm_ref, nv_ref):
        gv = jnp.where(pl.program_id(0) == c_ref[0], a_ref[...], b_ref[...])
        g_ref[0] = gv
        d_ref[0], nm_ref[0], nv_ref[0] = _adam_math(w_ref[0], gv, m_ref[0], v_ref[0])

    half = pl.BlockSpec((1, tr, cols), lambda h, i, c_ref: (h, i, 0))
    flat = pl.BlockSpec((tr, cols), lambda h, i, c_ref: (i, 0))
    outs = pl.pallas_call(
        body, name=name,
        grid_spec=pltpu.PrefetchScalarGridSpec(num_scalar_prefetch=1, grid=(2, rows // tr),
                                               in_specs=[half, flat, flat, half, half], out_specs=[half] * 4),
        out_shape=[jax.ShapeDtypeStruct((2, rows, cols), F32)] * 4, compiler_params=_cp("parallel", "parallel"),
    )(_scalar(c), ws, a, b, ms, vs)
    return tuple(o.reshape(shape) for o in outs)


def _adamw(w, g, m, v, name):
    shape = w.shape
    ws, gs, ms, vs = (_as_rows(t, 0) for t in (w, g, m, v))
    rows, cols = ws.shape
    tr = _row_tile(rows, cols, 7)

    def body(w_ref, g_ref, m_ref, v_ref, d_ref, nm_ref, nv_ref):
        d_ref[...], nm_ref[...], nv_ref[...] = _adam_math(w_ref[...], g_ref[...], m_ref[...], v_ref[...])

    spec = pl.BlockSpec((tr, cols), lambda i: (i, 0))
    outs = pl.pallas_call(body, name=name, grid=(rows // tr,), in_specs=[spec] * 4, out_specs=[spec] * 3,
                          out_shape=[jax.ShapeDtypeStruct((rows, cols), F32)] * 3, compiler_params=_cp("parallel"))(ws, gs, ms, vs)
    return tuple(o.reshape(shape) for o in outs)


_BIG = ["ffn_w_gate", "ffn_w_up", "ffn_w_down", "dn_w_in", "dn_w_out", "sg_w_in", "sg_w_out"]
_SMALL_SHARDED = ["norm_g", "dn_conv_w", "sg_b_in", "sg_ln_g", "sg_ln_b"]
_SMALL_REPL = ["dn_a_log", "dn_dt_bias", "dn_norm_g", "sg_w_s", "sg_b_s"]
_WEIGHTS = ["norm_g", "ffn_w_gate", "ffn_w_up", "ffn_w_down", "dn_w_in", "dn_conv_w", "dn_a_log", "dn_dt_bias",
            "dn_norm_g", "dn_w_out", "sg_w_in", "sg_b_in", "sg_ln_g", "sg_ln_b", "sg_w_s", "sg_b_s", "sg_w_out"]
PACK_COLS = 1024


def _pack(arrs):
    flat = jnp.concatenate([a.reshape(-1) for a in arrs])
    pad = (-flat.shape[0]) % (SUBLANES * PACK_COLS)
    return jnp.pad(flat, (0, pad)).reshape(-1, PACK_COLS)


def _unpack(buf, shapes):
    flat = buf.reshape(-1)
    out, off = [], 0
    for s in shapes:
        n = math.prod(s)
        out.append(flat[off:off + n].reshape(s))
        off += n
    return out


def _as_halves(a):
    if a.shape[0] == 2:
        return a
    if a.shape[0] == 1:
        return a.reshape((2, a.shape[1] // 2) + a.shape[2:])
    return a.reshape((2, a.shape[0] // 2) + a.shape[1:])


def _with_own(gathered, own, chip):
    g = gathered.reshape((N_CHIPS,) + own.shape)
    return [jnp.where(chip == k, own, g[k]) for k in range(N_CHIPS)]


def _cat_shards(g, axis):
    return jnp.concatenate(list(g), axis=axis)


_GROUP_ORDER = ["ffn00", "dn", "ffn01", "ffn10", "sg", "ffn11"]


def _weight_groups(w):
    cast = {k: _mx(w[k]) for k in _BIG}
    groups = {"ffn%d%d" % (i, j): [cast["ffn_w_gate"][i, j], cast["ffn_w_up"][i, j], cast["ffn_w_down"][i, j]]
              for i, j in [(0, 0), (0, 1), (1, 0), (1, 1)]}
    groups["dn"] = [cast["dn_w_in"][0], cast["dn_w_out"][0]]
    groups["sg"] = [cast["sg_w_in"][0], cast["sg_w_out"][0]]
    return groups


def _group_matrices(group, shards):
    if group.startswith("ffn"):
        gate, up, down = shards
        return {"wgu": jnp.concatenate([_cat_shards(gate, 1), _cat_shards(up, 1)], axis=1), "wd": _cat_shards(down, 0)}
    if group == "sg":
        return {"sg_win": _cat_shards(shards[0], 1), "sg_wout": _cat_shards(shards[1], 0)}
    dn_full = _cat_shards(shards[0], 1)
    W4 = 4 * DN_HEADS * DN_HEAD_DIM
    wba = jnp.zeros((D_MODEL, 2 * LANES), dn_full.dtype)
    wba = wba.at[:, :DN_HEADS].set(dn_full[:, W4:W4 + DN_HEADS])
    wba = wba.at[:, LANES:LANES + DN_HEADS].set(dn_full[:, W4 + DN_HEADS:])
    return {"dn_wqkvz": dn_full[:, :W4], "dn_wba": wba, "dn_wout": _cat_shards(shards[1], 0)}


def _split_cols(a, n):
    w = a.shape[-1] // n
    return [a[..., k * w:(k + 1) * w] for k in range(n)]


def _split_rows(a, n):
    h = a.shape[-2] // n
    return [a[..., k * h:(k + 1) * h, :] for k in range(n)]


_IJ = [(0, 0), (0, 1), (1, 0), (1, 1)]


_REDUCED = ["wguT%d%d" % ij for ij in _IJ] + ["wd%d%d" % ij for ij in _IJ] + ["dn_w_in", "dn_w_out", "sg_w_inT", "sg_w_out"]


def _group_grads(group, grads):
    def rows_by_chip(a):
        return a.reshape(N_CHIPS, 2, a.shape[0] // (2 * N_CHIPS), a.shape[1])

    if group.startswith("ffn"):
        tag = group[3:]
        t = grads["wguT" + tag]
        return (["wguT" + tag, "wd" + tag],
                [t.reshape(2, N_CHIPS, t.shape[0] // (2 * N_CHIPS), t.shape[1]), rows_by_chip(grads["wd" + tag])], [True, False])
    if group == "sg":
        return ["sg_w_inT", "sg_w_out"], [rows_by_chip(grads["sg_w_inT"]), rows_by_chip(grads["sg_w_out"])], [False, False]
    dn_in = jnp.stack([jnp.stack(_split_cols(hf, N_CHIPS)) for hf in _split_rows(grads["dn_w_in"], 2)])
    return ["dn_w_in", "dn_w_out"], [dn_in, rows_by_chip(grads["dn_w_out"])], [True, False]


def _shard_grads(mine, theirs, c, w):
    lo = [jnp.where(c == 0, a, b) for a, b in zip(mine, theirs)]
    hi = [jnp.where(c == 0, b, a) for a, b in zip(mine, theirs)]
    rows = lambda t: jnp.concatenate([lo[t], hi[t]], axis=0)
    sq = lambda parts: jnp.stack(parts).reshape(2, 2, *parts[0].shape)
    g = {}
    g["ffn_w_gate"] = sq([lo[t].T for t in range(4)])
    g["ffn_w_up"] = sq([hi[t].T for t in range(4)])
    g["ffn_w_down"] = sq([rows(4 + t) for t in range(4)])
    g["dn_w_in"] = rows(8)[None]
    g["dn_w_out"] = rows(9)[None]
    g["sg_w_in"] = rows(10).T[None]
    g["sg_w_out"] = rows(11)[None]
    return {k: v.reshape(w[k].shape) for k, v in g.items()}


def kernel(x, norm_g, ffn_w_gate, ffn_w_up, ffn_w_down, dn_w_in, dn_conv_w, dn_a_log, dn_dt_bias, dn_norm_g, dn_w_out, sg_w_in, sg_b_in, sg_ln_g, sg_ln_b, sg_w_s, sg_b_s, sg_w_out, loss_target, m_norm_g, m_ffn_w_gate, m_ffn_w_up, m_ffn_w_down, m_dn_w_in, m_dn_conv_w, m_dn_a_log, m_dn_dt_bias, m_dn_norm_g, m_dn_w_out, m_sg_w_in, m_sg_b_in, m_sg_ln_g, m_sg_ln_b, m_sg_w_s, m_sg_b_s, m_sg_w_out, v_norm_g, v_ffn_w_gate, v_ffn_w_up, v_ffn_w_down, v_dn_w_in, v_dn_conv_w, v_dn_a_log, v_dn_dt_bias, v_dn_norm_g, v_dn_w_out, v_sg_w_in, v_sg_b_in, v_sg_ln_g, v_sg_ln_b, v_sg_w_s, v_sg_b_s, v_sg_w_out):
    args = dict(locals())
    w = {k: args[k] for k in _WEIGHTS}
    mom = {k: args["m_" + k] for k in _WEIGHTS}
    var = {k: args["v_" + k] for k in _WEIGHTS}
    cx, cy, cc = _mesh_pos()
    chip = 2 * cx + cy

    small_shapes = [w[k].shape for k in _SMALL_SHARDED]
    groups = _weight_groups(w)
    own = groups[_GROUP_ORDER[0]] + [_pack([w[k] for k in _SMALL_SHARDED])]
    first = _allgather_chips([_as_halves(a) for a in own], "gather_first")
    started, after = {}, first[0]
    for g in _GROUP_ORDER[1:]:
        started[g] = _chips_start(groups[g], False, after, "gather_start_" + g)
        after = started[g]["token"]
    first = [_with_own(g, a, chip) for g, a in zip(first, own)]
    small_k = [_unpack(first[-1][k], small_shapes) for k in range(N_CHIPS)]
    p = {name: jnp.concatenate([small_k[k][i] for k in range(N_CHIPS)], axis=-1) for i, name in enumerate(_SMALL_SHARDED)}
    p = {k: (v if k == "norm_g" else v[0]) for k, v in p.items()}
    p["norm_g"] = p["norm_g"] + after[0, 0]
    for k in _SMALL_REPL:
        p[k] = w[k][0]

    def weights_for(group, after):
        if group == _GROUP_ORDER[0]:
            return _group_matrices(group, first[:-1])
        srcs, lands = _chips_wait(started[group], after, "gather_wait_" + group)
        return _group_matrices(group, [_with_own(l, a, chip) for l, a in zip(lands, srcs)])

    mine, theirs, pending = {}, {}, []

    def finish(group, names, scatter, after):
        pair_sum, got = _chips_wait(scatter, after, "reduce_wait_" + group)
        half_sum = [_sum_chips(a, b, chip, "chip_sum_" + n) for n, a, b in zip(names, pair_sum, got)]
        other = _swap_whole(half_sum, "gather_core_pair_" + group)
        mine.update(zip(names, half_sum))
        theirs.update(zip(names, other))

    def grads_ready(group, grads):
        names, halves, half_first = _group_grads(group, grads)
        from_sibling = _swap_halves(halves, half_first, "reduce_core_pair_" + group)
        pair_sum = [_add_own_half(h, o, cc, hf, "pair_sum_" + n) for n, h, o, hf in zip(names, halves, from_sibling, half_first)]
        scatter = _chips_start(pair_sum, True, from_sibling[0], "reduce_start_" + group)
        if pending:
            finish(*pending.pop(), scatter["token"])
        pending.append((group, names, scatter))

    loss_part, grad_x, grads = _local_step(x[0], loss_target[0], p, weights_for, grads_ready)
    last = pending.pop()
    finish(*last, last[2]["token"])
    half_sum = [mine[n] for n in _REDUCED]
    other_half = [theirs[n] for n in _REDUCED]

    small_names = _SMALL_SHARDED + _SMALL_REPL
    small_grads = [grads[k] for k in small_names]
    full_shapes = [g.shape for g in small_grads] + [(1,)]
    pack = _pack(small_grads + [loss_part[0, :1]])
    summed = _sum_leading(_allgather_devices(pack, "gather_small"), "small_sum")
    parts = _unpack(summed, full_shapes)
    loss = parts[-1][0]
    small_grad = {}
    for i, k in enumerate(small_names):
        g = parts[i]
        if k in _SMALL_SHARDED:
            n = w[k].shape[-1]
            g = lax.dynamic_slice_in_dim(g, chip * n, n, axis=g.ndim - 1)
        small_grad[k] = g

    grad = {**small_grad, **_shard_grads(half_sum, other_half, cc, w)}
    delta, new_m, new_v = {}, {}, {}
    for k in _BIG:
        delta[k], new_m[k], new_v[k] = _adamw(w[k], grad[k], mom[k], var[k], "adamw_" + k)
    shapes = [w[k].shape for k in small_names]
    d, nm, nv = _adamw(_pack([w[k] for k in small_names]), _pack([grad[k] for k in small_names]),
                       _pack([mom[k] for k in small_names]), _pack([var[k] for k in small_names]), "adamw_small")
    for k, a, b, c_ in zip(small_names, _unpack(d, shapes), _unpack(nm, shapes), _unpack(nv, shapes)):
        delta[k], new_m[k], new_v[k] = a, b, c_

    return (loss, grad_x[None], *[grad[k] for k in _WEIGHTS], *[delta[k] for k in _WEIGHTS],
            *[new_m[k] for k in _WEIGHTS], *[new_v[k] for k in _WEIGHTS])
```

```python
import functools
import math

import jax
import jax.numpy as jnp
from jax import lax
from jax.experimental import pallas as pl
from jax.experimental.pallas import tpu as pltpu

F32 = jnp.float32
MXU_DTYPE = jnp.bfloat16
COMM_DTYPE = jnp.bfloat16
HI = lax.Precision.HIGHEST
TRI_PREC = lax.Precision.HIGH

D_MODEL = 1024
D_FF = 2816
RMS_EPS = 1e-6
LN_EPS = 1e-5
L2_EPS = 1e-6
DN_HEADS = 8
DN_HEAD_DIM = 128
DN_CONV = 4
DN_CHUNK = 64
SG_WIDTH = 2048
SG_GROUPS = 8
SG_CHUNK = 128
SG_GROUP_W = SG_WIDTH // SG_GROUPS
N_CHIPS = 4
N_DEV = 8
LANES = 128
SUBLANES = 8
VMEM_LIMIT = 56 * 1024 * 1024

ADAM_LR = 0.001
ADAM_B1 = 0.9
ADAM_B2 = 0.999
ADAM_EPS = 1e-08
ADAM_WD = 0.01
ADAM_STEP = 10

MESH = pl.DeviceIdType.MESH
ANY = pl.BlockSpec(memory_space=pl.ANY)


def _cp(*sem):
    return pltpu.CompilerParams(dimension_semantics=sem, vmem_limit_bytes=VMEM_LIMIT)


def _pick(n, pref, mult=LANES):
    best = None
    d = mult
    while d <= min(n, pref):
        if n % d == 0:
            best = d
        d += mult
    return best if best is not None else n


def _full(shape):
    nd = len(shape)
    return pl.BlockSpec(shape, lambda *_: (0,) * nd)


def _sigmoid(x):
    return 1.0 / (1.0 + jnp.exp(-x))


def _dot(a, b, dims, prec=None):
    return lax.dot_general(a, b, (dims, ((), ())), preferred_element_type=F32, precision=prec)


NN = ((1,), (0,))
NT = ((1,), (1,))
TN = ((0,), (0,))


def _mx(a):
    return a.astype(MXU_DTYPE)


def _rms_stat(x):
    return lax.rsqrt(jnp.mean(x * x, axis=-1, keepdims=True) + RMS_EPS)


def _rms_bwd(x, r, g, dy):
    xh = x * r
    dxh = dy * g
    dx = r * (dxh - xh * jnp.mean(dxh * xh, axis=-1, keepdims=True))
    return dx, jnp.sum(dy * xh, axis=0, keepdims=True)


def _mm(a, b, mode, name, out_dtype=F32, add=None):
    if mode == "tn":
        K, M = a.shape
        N = b.shape[1]
    elif mode == "nt":
        M, K = a.shape
        N = b.shape[0]
    else:
        M, K = a.shape
        N = b.shape[1]
    tn = _pick(N, 1024)
    if mode == "tn":
        tm = _pick(M, 1024 if tn <= 512 else 1408)
        tk = _pick(K, 1024, SUBLANES)
    else:
        tm = _pick(M, max(512, min(2048, (512 * 1024) // tn)), SUBLANES)
        tk = _pick(K, 2048)
    nk = K // tk
    grid = (N // tn, M // tm, nk)
    if mode == "nn":
        a_spec = pl.BlockSpec((tm, tk), lambda j, i, k: (i, k))
        b_spec = pl.BlockSpec((tk, tn), lambda j, i, k: (k, j))
        dims = NN
    elif mode == "nt":
        a_spec = pl.BlockSpec((tm, tk), lambda j, i, k: (i, k))
        b_spec = pl.BlockSpec((tn, tk), lambda j, i, k: (j, k))
        dims = NT
    else:
        a_spec = pl.BlockSpec((tk, tm), lambda j, i, k: (k, i))
        b_spec = pl.BlockSpec((tk, tn), lambda j, i, k: (k, j))
        dims = TN
    o_spec = pl.BlockSpec((tm, tn), lambda j, i, k: (i, j))
    has_add = add is not None

    def body(*refs):
        if has_add:
            a_ref, b_ref, add_ref, o_ref, acc = refs
        else:
            a_ref, b_ref, o_ref, acc = refs
        k = pl.program_id(2)

        @pl.when(k == 0)
        def _():
            acc[...] = add_ref[...] if has_add else jnp.zeros_like(acc)

        acc[...] += _dot(a_ref[...], b_ref[...], dims)

        @pl.when(k == nk - 1)
        def _():
            o_ref[...] = acc[...].astype(o_ref.dtype)

    ins = [a, b] + ([add] if has_add else [])
    specs = [a_spec, b_spec] + ([o_spec] if has_add else [])
    return pl.pallas_call(
        body, name=name, grid=grid, in_specs=specs, out_specs=o_spec,
        out_shape=jax.ShapeDtypeStruct((M, N), out_dtype),
        scratch_shapes=[pltpu.VMEM((tm, tn), F32)],
        compiler_params=_cp("parallel", "parallel", "arbitrary"),
    )(*ins)


def _load_resident(pairs, sem):
    @pl.when(pl.program_id(0) == 0)
    def _():
        cps = [pltpu.make_async_copy(src, dst, sem.at[i]) for i, (src, dst) in enumerate(pairs)]
        for c in cps:
            c.start()
        for c in cps:
            c.wait()


def _ffn_fwd(x, g0, g1, wgu, wd, name):
    T, D = x.shape
    F2 = wgu.shape[1]
    F = F2 // 2
    tm = _pick(T, 256, SUBLANES)

    def body(x_ref, g0_ref, g1_ref, wgu_hbm, wd_hbm, xo_ref, h_ref, gu_ref, y_ref, wgu_v, wd_v, sem):
        _load_resident([(wgu_hbm, wgu_v), (wd_hbm, wd_v)], sem)
        xv = x_ref[...]
        hb = _mx(xv * _rms_stat(xv) * g0_ref[...])
        h_ref[...] = hb
        gu = _dot(hb, wgu_v[...], NN)
        gu_ref[...] = gu
        g = gu[:, :F]
        u = gu[:, F:]
        a = _mx(g * _sigmoid(g) * u)
        y = _dot(a, wd_v[...], NN)
        y_ref[...] = y
        xo_ref[...] = xv + 0.5 * (y * _rms_stat(y) * g1_ref[...])

    row = lambda w: pl.BlockSpec((tm, w), lambda i: (i, 0))
    return pl.pallas_call(
        body, name=name, grid=(T // tm,),
        in_specs=[row(D), _full((1, D)), _full((1, D)), ANY, ANY],
        out_specs=[row(D), row(D), row(F2), row(D)],
        out_shape=[jax.ShapeDtypeStruct((T, D), F32), jax.ShapeDtypeStruct((T, D), MXU_DTYPE),
                   jax.ShapeDtypeStruct((T, F2), F32), jax.ShapeDtypeStruct((T, D), F32)],
        scratch_shapes=[pltpu.VMEM(wgu.shape, wgu.dtype), pltpu.VMEM(wd.shape, wd.dtype),
                        pltpu.SemaphoreType.DMA((2,))],
        compiler_params=_cp("arbitrary"),
    )(x, g0, g1, wgu, wd)


def _ffn_bwd_down(dxo, y, gu, g1, wd, name):
    T, D = y.shape
    F2 = gu.shape[1]
    F = F2 // 2
    tm = _pick(T, 256, SUBLANES)

    def body(dxo_ref, y_ref, gu_ref, g1_ref, wd_hbm, dy_ref, a_ref, dgu_ref, dg1_ref, wd_v, sem):
        _load_resident([(wd_hbm, wd_v)], sem)

        @pl.when(pl.program_id(0) == 0)
        def _():
            dg1_ref[...] = jnp.zeros_like(dg1_ref)

        yv = y_ref[...]
        dy, dg1 = _rms_bwd(yv, _rms_stat(yv), g1_ref[...], 0.5 * dxo_ref[...])
        dg1_ref[...] += dg1
        dyb = _mx(dy)
        dy_ref[...] = dyb
        da = _dot(dyb, wd_v[...], NT)
        gu_v = gu_ref[...]
        g = gu_v[:, :F]
        u = gu_v[:, F:]
        s = _sigmoid(g)
        sg = g * s
        a_ref[...] = _mx(sg * u)
        dgu_ref[:, :F] = _mx(da * u * (s * (1.0 + g * (1.0 - s))))
        dgu_ref[:, F:] = _mx(da * sg)

    row = lambda w: pl.BlockSpec((tm, w), lambda i: (i, 0))
    return pl.pallas_call(
        body, name=name, grid=(T // tm,),
        in_specs=[row(D), row(D), row(F2), _full((1, D)), ANY],
        out_specs=[row(D), row(F), row(F2), _full((1, D))],
        out_shape=[jax.ShapeDtypeStruct((T, D), MXU_DTYPE), jax.ShapeDtypeStruct((T, F), MXU_DTYPE),
                   jax.ShapeDtypeStruct((T, F2), MXU_DTYPE), jax.ShapeDtypeStruct((1, D), F32)],
        scratch_shapes=[pltpu.VMEM(wd.shape, wd.dtype), pltpu.SemaphoreType.DMA((1,))],
        compiler_params=_cp("arbitrary"),
    )(dxo, y, gu, g1, wd)


def _ffn_bwd_up(dgu, x, dxo, g0, wgu, name):
    T, D = x.shape
    F2 = dgu.shape[1]
    tm = _pick(T, 256, SUBLANES)

    def body(dgu_ref, x_ref, dxo_ref, g0_ref, wgu_hbm, dx_ref, dg0_ref, wgu_v, sem):
        _load_resident([(wgu_hbm, wgu_v)], sem)

        @pl.when(pl.program_id(0) == 0)
        def _():
            dg0_ref[...] = jnp.zeros_like(dg0_ref)

        dh = _dot(dgu_ref[...], wgu_v[...], NT)
        xv = x_ref[...]
        dx, dg0 = _rms_bwd(xv, _rms_stat(xv), g0_ref[...], dh)
        dg0_ref[...] += dg0
        dx_ref[...] = dxo_ref[...] + dx

    row = lambda w: pl.BlockSpec((tm, w), lambda i: (i, 0))
    return pl.pallas_call(
        body, name=name, grid=(T // tm,),
        in_specs=[row(F2), row(D), row(D), _full((1, D)), ANY],
        out_specs=[row(D), _full((1, D))],
        out_shape=[jax.ShapeDtypeStruct((T, D), F32), jax.ShapeDtypeStruct((1, D), F32)],
        scratch_shapes=[pltpu.VMEM(wgu.shape, wgu.dtype), pltpu.SemaphoreType.DMA((1,))],
        compiler_params=_cp("arbitrary"),
    )(dgu, x, dxo, g0, wgu)


def _norm_fwd(x, g, name):
    T, D = x.shape
    tm = _pick(T, 512, SUBLANES)

    def body(x_ref, g_ref, h_ref):
        xv = x_ref[...]
        h_ref[...] = _mx(xv * _rms_stat(xv) * g_ref[...])

    row = pl.BlockSpec((tm, D), lambda i: (i, 0))
    return pl.pallas_call(body, name=name, grid=(T // tm,), in_specs=[row, _full((1, D))], out_specs=row,
                          out_shape=jax.ShapeDtypeStruct((T, D), MXU_DTYPE), compiler_params=_cp("parallel"))(x, g)


def _postnorm_fwd(x, m, g, name):
    T, D = x.shape
    tm = _pick(T, 512, SUBLANES)

    def body(x_ref, m_ref, g_ref, o_ref):
        mv = m_ref[...]
        o_ref[...] = x_ref[...] + mv * _rms_stat(mv) * g_ref[...]

    row = pl.BlockSpec((tm, D), lambda i: (i, 0))
    return pl.pallas_call(body, name=name, grid=(T // tm,), in_specs=[row, row, _full((1, D))], out_specs=row,
                          out_shape=jax.ShapeDtypeStruct((T, D), F32), compiler_params=_cp("parallel"))(x, m, g)


def _postnorm_bwd(dxo, m, g, name):
    T, D = m.shape
    tm = _pick(T, 512, SUBLANES)

    def body(dxo_ref, m_ref, g_ref, dm_ref, dg_ref):
        @pl.when(pl.program_id(0) == 0)
        def _():
            dg_ref[...] = jnp.zeros_like(dg_ref)

        mv = m_ref[...]
        dm, dg = _rms_bwd(mv, _rms_stat(mv), g_ref[...], dxo_ref[...])
        dg_ref[...] += dg
        dm_ref[...] = _mx(dm)

    row = pl.BlockSpec((tm, D), lambda i: (i, 0))
    return pl.pallas_call(body, name=name, grid=(T // tm,), in_specs=[row, row, _full((1, D))],
                          out_specs=[row, _full((1, D))],
                          out_shape=[jax.ShapeDtypeStruct((T, D), MXU_DTYPE), jax.ShapeDtypeStruct((1, D), F32)],
                          compiler_params=_cp("arbitrary"))(dxo, m, g)


def _prenorm_bwd(dxo, dh, x, g, name):
    T, D = x.shape
    tm = _pick(T, 512, SUBLANES)

    def body(dxo_ref, dh_ref, x_ref, g_ref, dx_ref, dg_ref):
        @pl.when(pl.program_id(0) == 0)
        def _():
            dg_ref[...] = jnp.zeros_like(dg_ref)

        xv = x_ref[...]
        dx, dg = _rms_bwd(xv, _rms_stat(xv), g_ref[...], dh_ref[...])
        dg_ref[...] += dg
        dx_ref[...] = dxo_ref[...] + dx

    row = pl.BlockSpec((tm, D), lambda i: (i, 0))
    return pl.pallas_call(body, name=name, grid=(T // tm,), in_specs=[row, row, row, _full((1, D))],
                          out_specs=[row, _full((1, D))],
                          out_shape=[jax.ShapeDtypeStruct((T, D), F32), jax.ShapeDtypeStruct((1, D), F32)],
                          compiler_params=_cp("arbitrary"))(dxo, dh, x, g)


def _loss_fwd_bwd(y, target, name):
    T, D = y.shape
    tm = _pick(T, 512, SUBLANES)

    def body(y_ref, t_ref, l_ref, dy_ref):
        @pl.when(pl.program_id(0) == 0)
        def _():
            l_ref[...] = jnp.zeros_like(l_ref)

        e = y_ref[...] - t_ref[...]
        dy_ref[...] = e * (1.0 / D)
        l_ref[...] += 0.5 * jnp.sum(jnp.mean(e * e, axis=-1, keepdims=True), axis=0, keepdims=True)

    row = pl.BlockSpec((tm, D), lambda i: (i, 0))
    return pl.pallas_call(body, name=name, grid=(T // tm,), in_specs=[row, row],
                          out_specs=[_full((SUBLANES, LANES)), row],
                          out_shape=[jax.ShapeDtypeStruct((SUBLANES, LANES), F32), jax.ShapeDtypeStruct((T, D), F32)],
                          compiler_params=_cp("arbitrary"))(y, target)


DN_ROWS = 512


def _shift_down(prev8, cur, s):
    n = cur.shape[0]
    xx = jnp.concatenate([prev8, cur], axis=0)
    return pltpu.roll(xx, s, 0)[SUBLANES:SUBLANES + n, :]


def _shift_up(cur, next8, s):
    n = cur.shape[0]
    xx = jnp.concatenate([cur, next8], axis=0)
    return pltpu.roll(xx, n + SUBLANES - s, 0)[:n, :]


def _conv_tile(x_ref, w, r, rows):
    start = pl.multiple_of(r * rows, SUBLANES)
    cur = x_ref[pl.ds(start, rows), :]
    pstart = pl.multiple_of(jnp.maximum(start - SUBLANES, 0), SUBLANES)
    prev8 = jnp.where(r == 0, 0.0, x_ref[pl.ds(pstart, SUBLANES), :])
    taps = [_shift_down(prev8, cur, DN_CONV - 1 - j) if j < DN_CONV - 1 else cur for j in range(DN_CONV)]
    c = taps[0] * w[0:1, :]
    for j in range(1, DN_CONV):
        c = c + taps[j] * w[j:j + 1, :]
    return c, taps


def _dn_prep_fwd(proj, conv_w, name):
    T = proj.shape[0]
    W = DN_HEADS * DN_HEAD_DIM
    rows = min(DN_ROWS, T)
    n_inner = T // rows
    scale = DN_HEAD_DIM ** -0.5

    def body(x_ref, w_ref, o_ref):
        cb = pl.program_id(0)
        w = w_ref[...]
        is_qk = cb < 2 * DN_HEADS
        post = jnp.where(cb < DN_HEADS, scale, 1.0)

        def step(r, carry):
            c, _ = _conv_tile(x_ref, w, r, rows)
            s = c * _sigmoid(c)
            rinv = lax.rsqrt(jnp.sum(s * s, axis=-1, keepdims=True) + L2_EPS)
            o_ref[pl.ds(pl.multiple_of(r * rows, SUBLANES), rows), :] = jnp.where(is_qk, s * rinv * post, s)
            return carry

        lax.fori_loop(0, n_inner, step, 0)

    col = pl.BlockSpec((T, LANES), lambda j: (0, j))
    return pl.pallas_call(body, name=name, grid=(3 * W // LANES,),
                          in_specs=[col, pl.BlockSpec((DN_CONV, LANES), lambda j: (0, j))], out_specs=col,
                          out_shape=jax.ShapeDtypeStruct((T, 3 * W), F32), compiler_params=_cp("parallel"))(proj, conv_w)


def _dn_prep_bwd(proj, conv_w, dqkv, name):
    T = proj.shape[0]
    W = DN_HEADS * DN_HEAD_DIM
    rows = min(DN_ROWS, T)
    n_inner = T // rows
    scale = DN_HEAD_DIM ** -0.5

    def body(x_ref, w_ref, dy_ref, dx_ref, dw_ref, dc_scr):
        cb = pl.program_id(0)
        w = w_ref[...]
        is_qk = cb < 2 * DN_HEADS
        post = jnp.where(cb < DN_HEADS, scale, 1.0)

        def step1(r, dws):
            c, taps = _conv_tile(x_ref, w, r, rows)
            sg = _sigmoid(c)
            s = c * sg
            rinv = lax.rsqrt(jnp.sum(s * s, axis=-1, keepdims=True) + L2_EPS)
            dy = dy_ref[pl.ds(pl.multiple_of(r * rows, SUBLANES), rows), :]
            yn = s * rinv
            dyn = dy * post
            ds_qk = rinv * (dyn - yn * jnp.sum(dyn * yn, axis=-1, keepdims=True))
            ds = jnp.where(is_qk, ds_qk, dy)
            dc = ds * (sg * (1.0 + c * (1.0 - sg)))
            dc_scr[pl.ds(pl.multiple_of(r * rows, SUBLANES), rows), :] = dc
            return tuple(dws[j] + jnp.sum(dc * taps[j], axis=0, keepdims=True) for j in range(DN_CONV))

        zero = jnp.zeros((1, LANES), F32)
        dws = lax.fori_loop(0, n_inner, step1, (zero,) * DN_CONV)
        for j in range(DN_CONV):
            dw_ref[j:j + 1, :] = dws[j]

        def step2(r, carry):
            start = pl.multiple_of(r * rows, SUBLANES)
            cur = dc_scr[pl.ds(start, rows), :]
            nstart = pl.multiple_of(jnp.minimum(start + rows, T - SUBLANES), SUBLANES)
            next8 = jnp.where(r == n_inner - 1, 0.0, dc_scr[pl.ds(nstart, SUBLANES), :])
            dx = cur * w[DN_CONV - 1:DN_CONV, :]
            for j in range(DN_CONV - 1):
                dx = dx + _shift_up(cur, next8, DN_CONV - 1 - j) * w[j:j + 1, :]
            dx_ref[pl.ds(start, rows), :] = _mx(dx)
            return carry

        lax.fori_loop(0, n_inner, step2, 0)

    col = pl.BlockSpec((T, LANES), lambda j: (0, j))
    wspec = pl.BlockSpec((DN_CONV, LANES), lambda j: (0, j))
    return pl.pallas_call(body, name=name, grid=(3 * W // LANES,), in_specs=[col, wspec, col], out_specs=[col, wspec],
                          out_shape=[jax.ShapeDtypeStruct((T, 3 * W), MXU_DTYPE), jax.ShapeDtypeStruct((DN_CONV, 3 * W), F32)],
                          scratch_shapes=[pltpu.VMEM((T, LANES), F32)], compiler_params=_cp("parallel"))(proj, conv_w, dqkv)


def _softplus(x):
    return jnp.maximum(x, 0.0) + jnp.log(1.0 + jnp.exp(-jnp.abs(x)))


def _dn_gate_fwd(ba, a_log, dt_bias, name):
    T = ba.shape[0]
    tm = _pick(T, 1024, SUBLANES)

    def body(ba_ref, al_ref, dt_ref, beta_ref, g_ref):
        beta_ref[...] = _sigmoid(ba_ref[:, :LANES])
        g_ref[...] = -jnp.exp(al_ref[...]) * _softplus(ba_ref[:, LANES:] + dt_ref[...])

    row = lambda w: pl.BlockSpec((tm, w), lambda i: (i, 0))
    return pl.pallas_call(body, name=name, grid=(T // tm,), in_specs=[row(2 * LANES), _full((1, LANES)), _full((1, LANES))],
                          out_specs=[row(LANES), row(LANES)],
                          out_shape=[jax.ShapeDtypeStruct((T, LANES), F32)] * 2, compiler_params=_cp("parallel"))(ba, a_log, dt_bias)


def _dn_gate_bwd(ba, a_log, dt_bias, dbeta, dg, name):
    T = ba.shape[0]
    tm = _pick(T, 1024, SUBLANES)

    def body(ba_ref, al_ref, dt_ref, dbeta_ref, dg_ref, dba_ref, dal_ref, ddt_ref):
        @pl.when(pl.program_id(0) == 0)
        def _():
            dal_ref[...] = jnp.zeros_like(dal_ref)
            ddt_ref[...] = jnp.zeros_like(ddt_ref)

        beta = _sigmoid(ba_ref[:, :LANES])
        dba_ref[:, :LANES] = _mx(dbeta_ref[...] * beta * (1.0 - beta))
        pre = ba_ref[:, LANES:] + dt_ref[...]
        ea = jnp.exp(al_ref[...])
        dgv = dg_ref[...]
        da = dgv * (-ea) * _sigmoid(pre)
        dba_ref[:, LANES:] = _mx(da)
        ddt_ref[...] += jnp.sum(da, axis=0, keepdims=True)
        dal_ref[...] += jnp.sum(dgv * (-ea) * _softplus(pre), axis=0, keepdims=True)

    row = lambda w: pl.BlockSpec((tm, w), lambda i: (i, 0))
    one = _full((1, LANES))
    return pl.pallas_call(body, name=name, grid=(T // tm,), in_specs=[row(2 * LANES), one, one, row(LANES), row(LANES)],
                          out_specs=[row(2 * LANES), one, one],
                          out_shape=[jax.ShapeDtypeStruct((T, 2 * LANES), MXU_DTYPE), jax.ShapeDtypeStruct((1, LANES), F32),
                                     jax.ShapeDtypeStruct((1, LANES), F32)],
                          compiler_params=_cp("arbitrary"))(ba, a_log, dt_bias, dbeta, dg)


def _tri(c, strict):
    i = lax.broadcasted_iota(jnp.int32, (c, c), 0)
    j = lax.broadcasted_iota(jnp.int32, (c, c), 1)
    return (i > j) if strict else (i >= j)


def _inv_unit_lower(ls):
    c = ls[0].shape[0]
    i = lax.broadcasted_iota(jnp.int32, (c, c), 0)
    j = lax.broadcasted_iota(jnp.int32, (c, c), 1)
    eye = jnp.where(i == j, 1.0, 0.0)
    facs = [[eye - l for l in ls]]
    cur = ls
    for _ in range(int(math.log2(c)) - 1):
        cur = [_dot(p, p, NN, TRI_PREC) for p in cur]
        facs.append([eye + p for p in cur])
    while len(facs) > 1:
        nxt = [[_dot(a, b, NN, TRI_PREC) for a, b in zip(facs[t], facs[t + 1])] for t in range(0, len(facs) - 1, 2)]
        if len(facs) % 2:
            nxt.append(facs[-1])
        facs = nxt
    return facs[0]


def _chunk_gates(g_blk):
    c = g_blk.shape[0]
    gcs = _dot(jnp.where(_tri(c, False), 1.0, 0.0), g_blk, NN, HI)
    return gcs, gcs.T


def _head_chunk(h, qh, kh, vh, beta_blk, gcs, gcs_t):
    c = qh.shape[0]
    incl = _tri(c, False)
    gc_col = gcs[:, h:h + 1]
    gc_row = gcs_t[h:h + 1, :]
    gc_last = gcs_t[h:h + 1, c - 1:c]
    dec = jnp.where(incl, jnp.exp(jnp.where(incl, gc_col - gc_row, 0.0)), 0.0)
    gam = jnp.exp(gc_col)
    rr = jnp.exp(gc_last - gc_col)
    gl = jnp.exp(gc_last)
    b = beta_blk[:, h:h + 1]
    kb = kh * b
    vb = vh * b
    kk = _dot(_mx(kb), _mx(kh), NT)
    lmat = jnp.where(_tri(c, True), kk * dec, 0.0)
    qk = _dot(_mx(qh), _mx(kh), NT)
    pmat = jnp.where(incl, qk * dec, 0.0)
    return dict(dec=dec, gam=gam, rr=rr, gl=gl, b=b, kb=kb, vb=vb, lmat=lmat, pmat=pmat)


def _dn_scan_fwd(qkv, beta, g, proj, norm_g, name):
    T = qkv.shape[0]
    C, H, Dh = DN_CHUNK, DN_HEADS, DN_HEAD_DIM
    W = H * Dh
    N = T // C

    def body(q_ref, k_ref, v_ref, beta_ref, g_ref, z_ref, ng_ref, og_ref, o_ref, tinv_ref, s_ref, state):
        @pl.when(pl.program_id(0) == 0)
        def _():
            state[...] = jnp.zeros_like(state)

        gcs, gcs_t = _chunk_gates(g_ref[...])
        beta_blk = beta_ref[...]
        ng = ng_ref[...]
        heads = range(H)
        cs = [slice(h * Dh, (h + 1) * Dh) for h in heads]
        qs = [_head_chunk(h, q_ref[:, cs[h]], k_ref[:, cs[h]], v_ref[:, cs[h]], beta_blk, gcs, gcs_t) for h in heads]
        tinvs = _inv_unit_lower([q["lmat"] for q in qs])
        for h in heads:
            tinv_ref[h] = tinvs[h]
        us = [_dot(tinvs[h], qs[h]["vb"], NN, TRI_PREC) for h in heads]
        ws = [_dot(tinvs[h], qs[h]["kb"] * qs[h]["gam"], NN, TRI_PREC) for h in heads]
        ss = [state[h] for h in heads]
        for h in heads:
            s_ref[0, h] = ss[h]
        sbs = [_mx(s) for s in ss]
        vnbs = [_mx(us[h] - _dot(_mx(ws[h]), sbs[h], NN)) for h in heads]
        os_ = [_dot(_mx(q_ref[:, cs[h]] * qs[h]["gam"]), sbs[h], NN) + _dot(_mx(qs[h]["pmat"]), vnbs[h], NN) for h in heads]
        for h in heads:
            state[h] = ss[h] * qs[h]["gl"] + _dot(_mx((k_ref[:, cs[h]] * qs[h]["rr"]).T), vnbs[h], NN)
        for h in heads:
            o = os_[h]
            o_ref[:, cs[h]] = o
            zh = z_ref[:, cs[h]]
            og_ref[:, cs[h]] = _mx(o * _rms_stat(o) * ng * (zh * _sigmoid(zh)))

    blk = lambda j: pl.BlockSpec((C, W), lambda n: (n, j))
    small = pl.BlockSpec((C, LANES), lambda n: (n, 0))
    return pl.pallas_call(
        body, name=name, grid=(N,),
        in_specs=[blk(0), blk(1), blk(2), small, small, blk(3), _full((1, Dh))],
        out_specs=[blk(0), blk(0), pl.BlockSpec((H, C, C), lambda n: (0, n, 0)),
                   pl.BlockSpec((1, H, Dh, Dh), lambda n: (n, 0, 0, 0))],
        out_shape=[jax.ShapeDtypeStruct((T, W), MXU_DTYPE), jax.ShapeDtypeStruct((T, W), F32),
                   jax.ShapeDtypeStruct((H, T, C), F32), jax.ShapeDtypeStruct((N, H, Dh, Dh), F32)],
        scratch_shapes=[pltpu.VMEM((H, Dh, Dh), F32)],
        compiler_params=_cp("arbitrary"),
    )(qkv, qkv, qkv, beta, g, proj, norm_g)


def _dn_scan_bwd(qkv, beta, g, proj, norm_g, o, tinv, s_all, dog, name):
    T = qkv.shape[0]
    C, H, Dh = DN_CHUNK, DN_HEADS, DN_HEAD_DIM
    W = H * Dh
    N = T // C

    def body(q_ref, k_ref, v_ref, beta_ref, g_ref, z_ref, ng_ref, o_ref, tinv_ref, s_ref, dog_ref,
             dqkv_ref, dbeta_ref, dg_ref, dz_ref, dng_ref, dstate):
        @pl.when(pl.program_id(0) == 0)
        def _():
            dstate[...] = jnp.zeros_like(dstate)
            dng_ref[...] = jnp.zeros_like(dng_ref)

        gcs, gcs_t = _chunk_gates(g_ref[...])
        beta_blk = beta_ref[...]
        ng = ng_ref[...]
        incl = _tri(C, False)
        strict = _tri(C, True)
        lane = lax.broadcasted_iota(jnp.int32, (C, LANES), 1)
        rowi = lax.broadcasted_iota(jnp.int32, (C, 1), 0)
        ones = jnp.ones((C, LANES), F32)
        dbeta_acc = jnp.zeros((C, LANES), F32)
        dgc_acc = jnp.zeros((C, LANES), F32)
        dng_acc = jnp.zeros((1, Dh), F32)
        heads = range(H)
        cs = [slice(h * Dh, (h + 1) * Dh) for h in heads]
        rsum = lambda t: jnp.sum(t, axis=1, keepdims=True)
        dobs = []
        for h in heads:
            oh, zh, dogh = o_ref[:, cs[h]], z_ref[:, cs[h]], dog_ref[:, cs[h]]
            rstat = _rms_stat(oh)
            sz = _sigmoid(zh)
            dz_ref[:, cs[h]] = _mx(dogh * (oh * rstat * ng) * (sz * (1.0 + zh * (1.0 - sz))))
            do, dng = _rms_bwd(oh, rstat, ng, dogh * (zh * sz))
            dng_acc = dng_acc + dng
            dobs.append(_mx(do))
        qs = [_head_chunk(h, q_ref[:, cs[h]], k_ref[:, cs[h]], v_ref[:, cs[h]], beta_blk, gcs, gcs_t) for h in heads]
        tms = [tinv_ref[h] for h in heads]
        us = [_dot(tms[h], qs[h]["vb"], NN, TRI_PREC) for h in heads]
        ws = [_dot(tms[h], qs[h]["kb"] * qs[h]["gam"], NN, TRI_PREC) for h in heads]
        ss = [s_ref[0, h] for h in heads]
        sbs = [_mx(s) for s in ss]
        wbs = [_mx(w) for w in ws]
        vnbs = [_mx(us[h] - _dot(wbs[h], sbs[h], NN)) for h in heads]
        dsns = [dstate[h] for h in heads]
        dsbs = [_mx(d) for d in dsns]
        dvnews = [_dot(_mx(qs[h]["pmat"]), dobs[h], TN) + _dot(_mx(k_ref[:, cs[h]] * qs[h]["rr"]), dsbs[h], NN) for h in heads]
        dvb16s = [_mx(d) for d in dvnews]
        dps = [jnp.where(incl, _dot(dobs[h], vnbs[h], NT), 0.0) for h in heads]
        dqds = [_dot(dobs[h], sbs[h], NT) for h in heads]
        dkds = [_dot(vnbs[h], dsbs[h], NT) for h in heads]
        dgls = [jnp.sum(rsum(ss[h] * dsns[h]), axis=0, keepdims=True) for h in heads]
        dws = [-_dot(dvb16s[h], sbs[h], NT) for h in heads]
        for h in heads:
            dstate[h] = (_dot(_mx(q_ref[:, cs[h]] * qs[h]["gam"]), dobs[h], TN) + qs[h]["gl"] * dsns[h]
                         - _dot(wbs[h], dvb16s[h], TN))
        dvbs = [_dot(tms[h], dvnews[h], TN, TRI_PREC) for h in heads]
        dkbgs = [_dot(tms[h], dws[h], TN, TRI_PREC) for h in heads]
        dls = [jnp.where(strict, -(_dot(dvbs[h], us[h], NT, TRI_PREC) + _dot(dkbgs[h], ws[h], NT, TRI_PREC)), 0.0)
               for h in heads]
        mmats = [dls[h] * qs[h]["lmat"] + dps[h] * qs[h]["pmat"] for h in heads]
        dgcs = [rsum(mmats[h]) - _dot(mmats[h], ones, TN, HI)[:, :1] for h in heads]
        dkk16s = [_mx(dls[h] * qs[h]["dec"]) for h in heads]
        dqk16s = [_mx(dps[h] * qs[h]["dec"]) for h in heads]
        for h in heads:
            q = qs[h]
            qh, kh, vh = q_ref[:, cs[h]], k_ref[:, cs[h]], v_ref[:, cs[h]]
            gam, rr, b, kb = q["gam"], q["rr"], q["b"], q["kb"]
            dkb = _dot(dkk16s[h], _mx(kh), NN) + dkbgs[h] * gam
            dk = _dot(dkk16s[h], _mx(kb), TN) + _dot(dqk16s[h], _mx(qh), TN) + dkb * b + dkds[h] * rr
            dq = _dot(dqk16s[h], _mx(kh), NN) + dqds[h] * gam
            dgam = rsum(dkbgs[h] * kb) + rsum(dqds[h] * qh)
            dr = rsum(dkds[h] * kh)
            dgc_last = jnp.sum(dr * rr, axis=0, keepdims=True) + dgls[h] * q["gl"]
            dgc = dgcs[h] + dgam * gam - dr * rr + jnp.where(rowi == C - 1, dgc_last, 0.0)
            dbeta = rsum(dvbs[h] * vh) + rsum(dkb * kh)
            dqkv_ref[:, cs[h]] = dq
            dqkv_ref[:, W + h * Dh:W + (h + 1) * Dh] = dk
            dqkv_ref[:, 2 * W + h * Dh:2 * W + (h + 1) * Dh] = dvbs[h] * b
            dbeta_acc = jnp.where(lane == h, dbeta, dbeta_acc)
            dgc_acc = jnp.where(lane == h, dgc, dgc_acc)
        dbeta_ref[...] = dbeta_acc
        dg_ref[...] = _dot(jnp.where(incl, 1.0, 0.0), dgc_acc, TN, HI)
        dng_ref[...] += dng_acc

    rev = lambda n: N - 1 - n
    blk = lambda j: pl.BlockSpec((C, W), lambda n: (rev(n), j))
    small = pl.BlockSpec((C, LANES), lambda n: (rev(n), 0))
    return pl.pallas_call(
        body, name=name, grid=(N,),
        in_specs=[blk(0), blk(1), blk(2), small, small, blk(3), _full((1, Dh)), blk(0),
                  pl.BlockSpec((H, C, C), lambda n: (0, rev(n), 0)),
                  pl.BlockSpec((1, H, Dh, Dh), lambda n: (rev(n), 0, 0, 0)), blk(0)],
        out_specs=[pl.BlockSpec((C, 3 * W), lambda n: (rev(n), 0)), small, small, blk(0), _full((1, Dh))],
        out_shape=[jax.ShapeDtypeStruct((T, 3 * W), F32), jax.ShapeDtypeStruct((T, LANES), F32),
                   jax.ShapeDtypeStruct((T, LANES), F32), jax.ShapeDtypeStruct((T, W), MXU_DTYPE),
                   jax.ShapeDtypeStruct((1, Dh), F32)],
        scratch_shapes=[pltpu.VMEM((H, Dh, Dh), F32)],
        compiler_params=_cp("arbitrary"),
    )(qkv, qkv, qkv, beta, g, proj, norm_g, o, tinv, s_all, dog)


_INV_SQRT2 = 0.7071067811865476
_INV_SQRT_2PI = 0.3989422804014327


def _sg_recompute(zp_ref, bin_ref, lng_ref, lnb_ref):
    E = SG_WIDTH
    zin = zp_ref[...] + bin_ref[...]
    cdf = 0.5 * (1.0 + lax.erf(zin * _INV_SQRT2))
    zz = zin * cdf
    u = zz[:, :E]
    vp = zz[:, E:]
    mu = jnp.mean(vp, axis=-1, keepdims=True)
    xc = vp - mu
    rstd = lax.rsqrt(jnp.mean(xc * xc, axis=-1, keepdims=True) + LN_EPS)
    xhat = xc * rstd
    v = xhat * lng_ref[...] + lnb_ref[...]
    return zin, cdf, u, xhat, rstd, v


def _sg_masked_ws(ws_ref, g):
    return _mx(jnp.where(_tri(SG_CHUNK, False), ws_ref[g], 0.0))


def _sg_fwd(zpre, b_in, ln_g, ln_b, w_s, b_s_t, name):
    T = zpre.shape[0]
    E, G, C, GW = SG_WIDTH, SG_GROUPS, SG_CHUNK, SG_GROUP_W

    def body(zp_ref, bin_ref, lng_ref, lnb_ref, ws_ref, bst_ref, um_ref):
        _, _, u, _, _, v = _sg_recompute(zp_ref, bin_ref, lng_ref, lnb_ref)
        bst = bst_ref[...]
        for g in range(G):
            cs = slice(g * GW, (g + 1) * GW)
            mixed = _dot(_sg_masked_ws(ws_ref, g), _mx(v[:, cs]), NN) + bst[:, g:g + 1]
            um_ref[:, cs] = _mx(u[:, cs] * mixed)

    return pl.pallas_call(
        body, name=name, grid=(T // C,),
        in_specs=[pl.BlockSpec((C, 2 * E), lambda n: (n, 0)), _full((1, 2 * E)), _full((1, E)), _full((1, E)),
                  _full((G, C, C)), _full((C, LANES))],
        out_specs=pl.BlockSpec((C, E), lambda n: (n, 0)),
        out_shape=jax.ShapeDtypeStruct((T, E), MXU_DTYPE), compiler_params=_cp("parallel"),
    )(zpre, b_in, ln_g, ln_b, w_s, b_s_t)


def _sg_bwd(zpre, b_in, ln_g, ln_b, w_s, b_s_t, dum, name):
    T = zpre.shape[0]
    E, G, C, GW = SG_WIDTH, SG_GROUPS, SG_CHUNK, SG_GROUP_W

    def body(zp_ref, bin_ref, lng_ref, lnb_ref, ws_ref, bst_ref, dum_ref,
             dz_ref, dbin_ref, dlng_ref, dlnb_ref, dws_ref, dbst_ref):
        @pl.when(pl.program_id(0) == 0)
        def _():
            for r in (dbin_ref, dlng_ref, dlnb_ref, dws_ref, dbst_ref):
                r[...] = jnp.zeros_like(r)

        zin, cdf, u, xhat, rstd, v = _sg_recompute(zp_ref, bin_ref, lng_ref, lnb_ref)
        bst = bst_ref[...]
        lane = lax.broadcasted_iota(jnp.int32, (C, LANES), 1)
        dum_v = dum_ref[...]
        dbst = jnp.zeros((C, LANES), F32)
        du_parts, dv_parts = [], []
        for g in range(G):
            cs = slice(g * GW, (g + 1) * GW)
            wsm = _sg_masked_ws(ws_ref, g)
            vg = _mx(v[:, cs])
            mixed = _dot(wsm, vg, NN) + bst[:, g:g + 1]
            dumg = dum_v[:, cs]
            du_parts.append(dumg * mixed)
            dmixed = dumg * u[:, cs]
            dmb = _mx(dmixed)
            dv_parts.append(_dot(wsm, dmb, TN))
            dws_ref[g] += _dot(dmb, vg, NT)
            dbst = jnp.where(lane == g, jnp.sum(dmixed, axis=1, keepdims=True), dbst)
        dbst_ref[...] += dbst
        du = jnp.concatenate(du_parts, axis=1)
        dv = jnp.concatenate(dv_parts, axis=1)
        dlng_ref[...] += jnp.sum(dv * xhat, axis=0, keepdims=True)
        dlnb_ref[...] += jnp.sum(dv, axis=0, keepdims=True)
        dxh = dv * lng_ref[...]
        dvp = rstd * (dxh - jnp.mean(dxh, axis=-1, keepdims=True) - xhat * jnp.mean(dxh * xhat, axis=-1, keepdims=True))
        dzz = jnp.concatenate([du, dvp], axis=1)
        dzin = dzz * (cdf + zin * (_INV_SQRT_2PI * jnp.exp(-0.5 * zin * zin)))
        dz_ref[...] = _mx(dzin)
        dbin_ref[...] += jnp.sum(dzin, axis=0, keepdims=True)

    return pl.pallas_call(
        body, name=name, grid=(T // C,),
        in_specs=[pl.BlockSpec((C, 2 * E), lambda n: (n, 0)), _full((1, 2 * E)), _full((1, E)), _full((1, E)),
                  _full((G, C, C)), _full((C, LANES)), pl.BlockSpec((C, E), lambda n: (n, 0))],
        out_specs=[pl.BlockSpec((C, 2 * E), lambda n: (n, 0)), _full((1, 2 * E)), _full((1, E)), _full((1, E)),
                   _full((G, C, C)), _full((C, LANES))],
        out_shape=[jax.ShapeDtypeStruct((T, 2 * E), MXU_DTYPE), jax.ShapeDtypeStruct((1, 2 * E), F32),
                   jax.ShapeDtypeStruct((1, E), F32), jax.ShapeDtypeStruct((1, E), F32),
                   jax.ShapeDtypeStruct((G, C, C), F32), jax.ShapeDtypeStruct((C, LANES), F32)],
        compiler_params=_cp("arbitrary"),
    )(zpre, b_in, ln_g, ln_b, w_s, b_s_t, dum)


def _row(v):
    return v.reshape(1, -1)


def _pad_lanes(v):
    v = v.reshape(1, -1)
    return jnp.pad(v, ((0, 0), (0, LANES - v.shape[1])))


def _local_step(x, target, p, weights_for, grads_ready=None):
    ng = p["norm_g"]
    grads = {}
    dng = [[None] * 6 for _ in range(2)]
    order = [jnp.zeros((), F32)]

    def tell(group):
        zero = grads_ready(group, grads) if grads_ready is not None else None
        if zero is not None:
            order[0] = zero

    def gain(i, s):
        return _row(ng[i, s]) + order[0]

    def ffn_f(xin, i, j, tag):
        wt = weights_for("ffn" + tag, xin)
        xo, h, gu, y = _ffn_fwd(xin, _row(ng[i, 4 * j]), _row(ng[i, 4 * j + 1]), wt["wgu"], wt["wd"], "ffn_fwd_" + tag)
        return xo, (xin, h, gu, y, wt)

    x1, sv_f00 = ffn_f(x, 0, 0, "00")
    dnw = weights_for("dn", x1)
    hn0 = _norm_fwd(x1, _row(ng[0, 2]), "dn_prenorm")
    proj = _mm(hn0, dnw["dn_wqkvz"], "nn", "dn_proj")
    ba = _mm(hn0, dnw["dn_wba"], "nn", "dn_proj_ba")
    a_log = _pad_lanes(p["dn_a_log"])
    dt_bias = _pad_lanes(p["dn_dt_bias"])
    dn_ng = _row(p["dn_norm_g"])
    qkv = _dn_prep_fwd(proj, p["dn_conv_w"], "dn_prep_fwd")
    beta, gdec = _dn_gate_fwd(ba, a_log, dt_bias, "dn_gate_fwd")
    og, o_raw, tinv, s_all = _dn_scan_fwd(qkv, beta, gdec, proj, dn_ng, "dn_scan_fwd")
    m0 = _mm(og, dnw["dn_wout"], "nn", "dn_out")
    x2 = _postnorm_fwd(x1, m0, _row(ng[0, 3]), "dn_postnorm")
    x3, sv_f01 = ffn_f(x2, 0, 1, "01")
    x4, sv_f10 = ffn_f(x3, 1, 0, "10")
    sgw = weights_for("sg", x4)
    hn1 = _norm_fwd(x4, _row(ng[1, 2]), "sg_prenorm")
    zpre = _mm(hn1, sgw["sg_win"], "nn", "sg_proj")
    sg_bin = _row(p["sg_b_in"])
    sg_lng = _row(p["sg_ln_g"])
    sg_lnb = _row(p["sg_ln_b"])
    sg_bst = jnp.pad(p["sg_b_s"].T, ((0, 0), (0, LANES - SG_GROUPS)))
    um = _sg_fwd(zpre, sg_bin, sg_lng, sg_lnb, p["sg_w_s"], sg_bst, "sg_fwd")
    m1 = _mm(um, sgw["sg_wout"], "nn", "sg_out")
    x5 = _postnorm_fwd(x4, m1, _row(ng[1, 3]), "sg_postnorm")
    x6, sv_f11 = ffn_f(x5, 1, 1, "11")
    loss_part, dx = _loss_fwd_bwd(x6, target, "loss")

    def ffn_b(dxo, sv, i, j, tag):
        xin, h, gu, y, wt = sv
        dy, a, dgu, dg1 = _ffn_bwd_down(dxo, y, gu, gain(i, 4 * j + 1), wt["wd"], "ffn_bwd_down_" + tag)
        grads["wd" + tag] = _mm(a, dy, "tn", "ffn_wgrad_down_" + tag)
        grads["wguT" + tag] = _mm(dgu, h, "tn", "ffn_wgrad_up_" + tag)
        tell("ffn" + tag)
        dxi, dg0 = _ffn_bwd_up(dgu, xin, dxo, gain(i, 4 * j), wt["wgu"], "ffn_bwd_up_" + tag)
        dng[i][4 * j] = dg0
        dng[i][4 * j + 1] = dg1
        return dxi

    dx = ffn_b(dx, sv_f11, 1, 1, "11")
    dm1, dng[1][3] = _postnorm_bwd(dx, m1, gain(1, 3), "sg_postnorm_bwd")
    grads["sg_w_out"] = _mm(um, dm1, "tn", "sg_wgrad_out")
    dum = _mm(dm1, sgw["sg_wout"], "nt", "sg_dgrad_out")
    dz1, dbin, dlng, dlnb, dws, dbst = _sg_bwd(zpre, sg_bin, sg_lng, sg_lnb, p["sg_w_s"], sg_bst, dum, "sg_bwd")
    grads["sg_w_inT"] = _mm(dz1, hn1, "tn", "sg_wgrad_in")
    tell("sg")
    dh1 = _mm(dz1, sgw["sg_win"], "nt", "sg_dgrad_in")
    dx, dng[1][2] = _prenorm_bwd(dx, dh1, x4, gain(1, 2), "sg_prenorm_bwd")
    grads["sg_b_in"] = dbin.reshape(1, -1)
    grads["sg_ln_g"] = dlng.reshape(1, -1)
    grads["sg_ln_b"] = dlnb.reshape(1, -1)
    grads["sg_w_s"] = jnp.where(jnp.tril(jnp.ones((SG_CHUNK, SG_CHUNK), bool)), dws, 0.0)[None]
    grads["sg_b_s"] = dbst[:, :SG_GROUPS].T[None]
    dx = ffn_b(dx, sv_f10, 1, 0, "10")
    dx = ffn_b(dx, sv_f01, 0, 1, "01")
    dm0, dng[0][3] = _postnorm_bwd(dx, m0, gain(0, 3), "dn_postnorm_bwd")
    grads["dn_w_out"] = _mm(og, dm0, "tn", "dn_wgrad_out")
    dog = _mm(dm0, dnw["dn_wout"], "nt", "dn_dgrad_out")
    dqkv, dbeta, dgdec, dz0, dnng = _dn_scan_bwd(qkv, beta, gdec, proj, dn_ng, o_raw, tinv, s_all, dog, "dn_scan_bwd")
    dqkv_pre, dconv = _dn_prep_bwd(proj, p["dn_conv_w"], dqkv, "dn_prep_bwd")
    dba, dal, ddt = _dn_gate_bwd(ba, a_log, dt_bias, dbeta, dgdec, "dn_gate_bwd")
    W3 = 3 * DN_HEADS * DN_HEAD_DIM
    dw_qkv = _mm(hn0, dqkv_pre, "tn", "dn_wgrad_qkv")
    dw_z = _mm(hn0, dz0, "tn", "dn_wgrad_z")
    dw_ba = _mm(hn0, dba, "tn", "dn_wgrad_ba")
    grads["dn_w_in"] = jnp.concatenate(
        [dw_qkv, dw_z, dw_ba[:, :DN_HEADS], dw_ba[:, LANES:LANES + DN_HEADS]], axis=1)
    tell("dn")
    dh0 = _mm(dqkv_pre, dnw["dn_wqkvz"][:, :W3], "nt", "dn_dgrad_qkv")
    dh0 = _mm(dz0, dnw["dn_wqkvz"][:, W3:], "nt", "dn_dgrad_z", add=dh0)
    dh0 = _mm(dba, dnw["dn_wba"], "nt", "dn_dgrad_ba", add=dh0)
    dx, dng[0][2] = _prenorm_bwd(dx, dh0, x1, gain(0, 2), "dn_prenorm_bwd")
    grads["dn_conv_w"] = dconv[None]
    grads["dn_a_log"] = dal[:, :DN_HEADS]
    grads["dn_dt_bias"] = ddt[:, :DN_HEADS]
    grads["dn_norm_g"] = dnng
    dx = ffn_b(dx, sv_f00, 0, 0, "00")
    grads["norm_g"] = jnp.stack([jnp.concatenate(dng[i], axis=0) for i in range(2)])
    return loss_part, dx, grads


def _mesh_pos():
    return lax.axis_index("x"), lax.axis_index("y"), lax.axis_index("c")


def _other_chips(x, y):
    return [(1 - x, y), (x, 1 - y), (1 - x, 1 - y)]


def _allgather_chips(arrs, name):
    n = len(arrs)

    def body(*refs):
        ins, outs = refs[:n], refs[n:2 * n]
        ici_send, ici_recv, d2d_send, d2d_recv = refs[2 * n:]
        x, y, c = _mesh_pos()
        me = 2 * x + y
        chips = _other_chips(x, y)
        sibling = (x, y, 1 - c)

        def ici(i, j, k):
            cx, cy = chips[j]
            return pltpu.make_async_remote_copy(src_ref=ins[i].at[c], dst_ref=outs[i].at[k, c], send_sem=ici_send.at[3 * i + j],
                                                recv_sem=ici_recv.at[3 * i + j], device_id=(cx, cy, c), device_id_type=MESH)

        def d2d(i, j, h):
            cx, cy = chips[j]
            slot = outs[i].at[2 * cx + cy, h]
            return pltpu.make_async_remote_copy(src_ref=slot, dst_ref=slot, send_sem=d2d_send.at[3 * i + j],
                                                recv_sem=d2d_recv.at[3 * i + j], device_id=sibling, device_id_type=MESH)

        sends = [ici(i, j, me) for i in range(n) for j in range(3)]
        for cp in sends:
            cp.start()
        for i in range(n):
            for j, (cx, cy) in enumerate(chips):
                ici(i, j, 2 * cx + cy).wait_recv()
                fwd = d2d(i, j, c)
                fwd.start()
                sends.append(fwd)
        for i in range(n):
            for j in range(3):
                d2d(i, j, 1 - c).wait_recv()
        for cp in sends:
            cp.wait_send()

    return pl.pallas_call(
        body, name=name, in_specs=[ANY] * n, out_specs=[ANY] * n,
        out_shape=[jax.ShapeDtypeStruct((N_CHIPS,) + a.shape, a.dtype) for a in arrs],
        scratch_shapes=[pltpu.SemaphoreType.DMA((3 * n,))] * 4,
    )(*arrs)


def _swap_halves(arrs, half_first, name):
    n = len(arrs)

    def body(*refs):
        ins, outs = refs[:n], refs[n:2 * n]
        send_sems, recv_sems = refs[2 * n:]
        x, y, c = _mesh_pos()
        cps = [pltpu.make_async_remote_copy(src_ref=ins[i].at[1 - c] if half_first[i] else ins[i].at[:, 1 - c],
                                            dst_ref=outs[i], send_sem=send_sems.at[i], recv_sem=recv_sems.at[i],
                                            device_id=(x, y, 1 - c), device_id_type=MESH)
               for i in range(n)]
        for cp in cps:
            cp.start()
        for cp in cps:
            cp.wait()

    return pl.pallas_call(
        body, name=name, in_specs=[ANY] * n, out_specs=[ANY] * n,
        out_shape=[jax.ShapeDtypeStruct((N_CHIPS,) + a.shape[2:], a.dtype) for a in arrs],
        scratch_shapes=[pltpu.SemaphoreType.DMA((n,)), pltpu.SemaphoreType.DMA((n,))],
    )(*arrs)


HBM = pl.BlockSpec(memory_space=pltpu.HBM)
SEM = pl.BlockSpec(memory_space=pltpu.SEMAPHORE)
TOKEN = jax.ShapeDtypeStruct((SUBLANES, LANES), F32)


def _chip_copies(src_refs, land_refs, send_sems, recv_sems, slice_by_chip, receiving):
    x, y, c = _mesh_pos()
    me = 2 * x + y
    cps = []
    for i, (src, land) in enumerate(zip(src_refs, land_refs)):
        for j, (cx, cy) in enumerate(_other_chips(x, y)):
            peer = 2 * cx + cy
            s = src.at[me if receiving else peer] if slice_by_chip else src
            cps.append(pltpu.make_async_remote_copy(
                src_ref=s, dst_ref=land.at[peer if receiving else me], send_sem=send_sems.at[3 * i + j],
                recv_sem=recv_sems.at[3 * i + j], device_id=(cx, cy, c), device_id_type=MESH))
    return cps


def _chips_start(srcs, slice_by_chip, after, name):
    n = len(srcs)
    lands = [lax.empty((N_CHIPS,) + (s.shape[1:] if slice_by_chip else s.shape), s.dtype) for s in srcs]

    def body(*refs):
        src_refs, land_refs = refs[:n], refs[n:2 * n]
        send_sems, recv_sems = refs[2 * n + 1], refs[2 * n + 2]
        token = refs[-1]
        for cp in _chip_copies(src_refs, land_refs, send_sems, recv_sems, slice_by_chip, False):
            cp.start()
        token[...] = jnp.zeros_like(token)

    outs = pl.pallas_call(
        body, name=name,
        in_specs=[HBM] * (2 * n) + [ANY],
        out_specs=(SEM, SEM) + (HBM,) * (2 * n) + (pl.BlockSpec(memory_space=pltpu.VMEM),),
        out_shape=(pltpu.SemaphoreType.DMA((3 * n,)), pltpu.SemaphoreType.DMA((3 * n,)))
        + tuple(pltpu.HBM(a.shape, a.dtype) for a in list(srcs) + lands) + (TOKEN,),
        input_output_aliases={i: 2 + i for i in range(2 * n)},
        compiler_params=pltpu.CompilerParams(has_side_effects=pltpu.SideEffectType.DATAFLOW_SIDE_EFFECTING),
    )(*[pltpu.with_memory_space_constraint(a, pltpu.HBM) for a in list(srcs) + lands], after)
    return dict(sems=outs[:2], srcs=outs[2:2 + n], lands=outs[2 + n:2 + 2 * n], token=outs[-1], slice_by_chip=slice_by_chip)


def _chips_wait(started, after, name):
    n = len(started["srcs"])
    slice_by_chip = started["slice_by_chip"]

    def body(*refs):
        src_refs, land_refs = refs[:n], refs[n:2 * n]
        send_sems, recv_sems = refs[2 * n], refs[2 * n + 1]
        for cp in _chip_copies(src_refs, land_refs, send_sems, recv_sems, slice_by_chip, True):
            cp.wait_send()
            cp.wait_recv()

    outs = pl.pallas_call(
        body, name=name,
        in_specs=[HBM] * (2 * n) + [SEM, SEM, ANY],
        out_specs=(HBM,) * (2 * n),
        out_shape=tuple(pltpu.HBM(a.shape, a.dtype) for a in list(started["srcs"]) + list(started["lands"])),
        input_output_aliases={i: i for i in range(2 * n)},
        compiler_params=pltpu.CompilerParams(has_side_effects=pltpu.SideEffectType.DATAFLOW_SIDE_EFFECTING),
    )(*started["srcs"], *started["lands"], *started["sems"], after)
    return outs[:n], outs[n:]


def _swap_whole(arrs, name):
    n = len(arrs)

    def body(*refs):
        ins, outs = refs[:n], refs[n:2 * n]
        send_sems, recv_sems = refs[2 * n:]
        x, y, c = _mesh_pos()
        cps = [pltpu.make_async_remote_copy(src_ref=ins[i], dst_ref=outs[i], send_sem=send_sems.at[i],
                                            recv_sem=recv_sems.at[i], device_id=(x, y, 1 - c), device_id_type=MESH)
               for i in range(n)]
        for cp in cps:
            cp.start()
        for cp in cps:
            cp.wait()

    return pl.pallas_call(
        body, name=name, in_specs=[ANY] * n, out_specs=[ANY] * n,
        out_shape=[jax.ShapeDtypeStruct(a.shape, a.dtype) for a in arrs],
        scratch_shapes=[pltpu.SemaphoreType.DMA((n,)), pltpu.SemaphoreType.DMA((n,))],
    )(*arrs)


def _allgather_devices(a, name):
    masks = [(mx, my, mc) for mx in (0, 1) for my in (0, 1) for mc in (0, 1)][1:]

    def body(in_ref, out_ref, send_sems, recv_sems, loc_sem):
        x, y, c = _mesh_pos()
        me = 4 * x + 2 * y + c
        lc = pltpu.make_async_copy(in_ref, out_ref.at[me], loc_sem.at[0])
        lc.start()
        peers = [(jnp.where(mx, 1 - x, x), jnp.where(my, 1 - y, y), jnp.where(mc, 1 - c, c)) for mx, my, mc in masks]
        cps = [pltpu.make_async_remote_copy(src_ref=in_ref, dst_ref=out_ref.at[me], send_sem=send_sems.at[j],
                                            recv_sem=recv_sems.at[j], device_id=peers[j], device_id_type=MESH)
               for j in range(len(masks))]
        for cp in cps:
            cp.start()
        for j, (px, py, pc) in enumerate(peers):
            pltpu.make_async_remote_copy(src_ref=in_ref, dst_ref=out_ref.at[4 * px + 2 * py + pc], send_sem=send_sems.at[j],
                                         recv_sem=recv_sems.at[j], device_id=peers[j], device_id_type=MESH).wait_recv()
        for cp in cps:
            cp.wait_send()
        lc.wait()

    return pl.pallas_call(
        body, name=name, in_specs=[ANY], out_specs=ANY,
        out_shape=jax.ShapeDtypeStruct((N_DEV,) + a.shape, a.dtype),
        scratch_shapes=[pltpu.SemaphoreType.DMA((N_DEV - 1,)), pltpu.SemaphoreType.DMA((N_DEV - 1,)),
                        pltpu.SemaphoreType.DMA((1,))],
    )(a)


def _as_rows(a, lead):
    shp = a.shape
    rows = 1
    for s in shp[lead:-1]:
        rows *= s
    return a.reshape(shp[:lead] + (rows, shp[-1]))


def _row_tile(rows, cols, n_bufs):
    budget = (24 * 1024 * 1024) // (n_bufs * 2 * 4 * cols)
    return _pick(rows, max(2 * SUBLANES, budget), 2 * SUBLANES)


def _sum_leading(a, name):
    n = a.shape[0]
    v = _as_rows(a, 1)
    _, rows, cols = v.shape
    tr = _row_tile(rows, cols, n + 1)

    def body(a_ref, o_ref):
        acc = a_ref[0]
        for k in range(1, n):
            acc = acc + a_ref[k]
        o_ref[...] = acc

    out = pl.pallas_call(body, name=name, grid=(rows // tr,),
                         in_specs=[pl.BlockSpec((n, tr, cols), lambda i: (0, i, 0))],
                         out_specs=pl.BlockSpec((tr, cols), lambda i: (i, 0)),
                         out_shape=jax.ShapeDtypeStruct((rows, cols), F32), compiler_params=_cp("parallel"))(v)
    return out.reshape(a.shape[1:])


def _scalar(i):
    return jnp.reshape(i, (1,)).astype(jnp.int32)


def _add_own_half(g, other, c, half_first, name):
    _, rows, cols = other.shape
    tr = _row_tile(rows, cols, 3)

    def body(c_ref, g_ref, o_ref, out_ref):
        out_ref[0] = (g_ref[0, 0] + o_ref[0]).astype(out_ref.dtype)

    if half_first:
        g_map = lambda k, i, c_ref: (c_ref[0], k, i, 0)
    else:
        g_map = lambda k, i, c_ref: (k, c_ref[0], i, 0)
    flat = pl.BlockSpec((1, tr, cols), lambda k, i, c_ref: (k, i, 0))
    return pl.pallas_call(
        body, name=name,
        grid_spec=pltpu.PrefetchScalarGridSpec(
            num_scalar_prefetch=1, grid=(N_CHIPS, rows // tr),
            in_specs=[pl.BlockSpec((1, 1, tr, cols), g_map), flat], out_specs=flat),
        out_shape=jax.ShapeDtypeStruct(other.shape, COMM_DTYPE), compiler_params=_cp("parallel", "parallel"),
    )(_scalar(c), g, other)


def _sum_chips(own, got, chip, name):
    pv = _as_rows(own, 1)
    bv = _as_rows(got, 1)
    _, rows, cols = pv.shape
    tr = _row_tile(rows, cols, N_CHIPS + 2)

    def body(chip_ref, p_ref, b_ref, o_ref):
        mine = p_ref[0].astype(F32)
        acc = jnp.where(chip_ref[0] == 0, mine, b_ref[0].astype(F32))
        for k in range(1, N_CHIPS):
            acc = acc + jnp.where(chip_ref[0] == k, mine, b_ref[k].astype(F32))
        o_ref[...] = acc

    out = pl.pallas_call(
        body, name=name,
        grid_spec=pltpu.PrefetchScalarGridSpec(
            num_scalar_prefetch=1, grid=(rows // tr,),
            in_specs=[pl.BlockSpec((1, tr, cols), lambda i, k_ref: (k_ref[0], i, 0)),
                      pl.BlockSpec((N_CHIPS, tr, cols), lambda i, k_ref: (0, i, 0))],
            out_specs=pl.BlockSpec((tr, cols), lambda i, k_ref: (i, 0))),
        out_shape=jax.ShapeDtypeStruct((rows, cols), F32), compiler_params=_cp("parallel"),
    )(_scalar(chip), pv, bv)
    return out.reshape(own.shape[1:])


def _adam_math(w, g, m, v):
    nm = ADAM_B1 * m + (1.0 - ADAM_B1) * g
    nv = ADAM_B2 * v + (1.0 - ADAM_B2) * (g * g)
    m_hat = nm / (1.0 - ADAM_B1 ** ADAM_STEP)
    v_hat = nv / (1.0 - ADAM_B2 ** ADAM_STEP)
    return -ADAM_LR * (m_hat / (jnp.sqrt(v_hat) + ADAM_EPS) + ADAM_WD * w), nm, nv


def _adamw_halves(w, mine, theirs, m, v, c, name):
    shape = w.shape
    ws, ms, vs = (_as_rows(t.reshape((2, -1) + t.shape[-1:]), 1) for t in (w, m, v))
    a, b = _as_rows(mine, 0), _as_rows(theirs, 0)
    rows, cols = a.shape
    tr = _row_tile(rows, cols, 9)

    def body(c_ref, w_ref, a_ref, b_ref, m_ref, v_ref, g_ref, d_ref, nm_ref, nv_ref):
        gv = jnp.where(pl.program_id(0) == c_ref[0], a_ref[...], b_ref[...])
        g_ref[0] = gv
        d_ref[0], nm_ref[0], nv_ref[0] = _adam_math(w_ref[0], gv, m_ref[0], v_ref[0])

    half = pl.BlockSpec((1, tr, cols), lambda h, i, c_ref: (h, i, 0))
    flat = pl.BlockSpec((tr, cols), lambda h, i, c_ref: (i, 0))
    outs = pl.pallas_call(
        body, name=name,
        grid_spec=pltpu.PrefetchScalarGridSpec(num_scalar_prefetch=1, grid=(2, rows // tr),
                                               in_specs=[half, flat, flat, half, half], out_specs=[half] * 4),
        out_shape=[jax.ShapeDtypeStruct((2, rows, cols), F32)] * 4, compiler_params=_cp("parallel", "parallel"),
    )(_scalar(c), ws, a, b, ms, vs)
    return tuple(o.reshape(shape) for o in outs)


def _adamw(w, g, m, v, name):
    shape = w.shape
    ws, gs, ms, vs = (_as_rows(t, 0) for t in (w, g, m, v))
    rows, cols = ws.shape
    tr = _row_tile(rows, cols, 7)

    def body(w_ref, g_ref, m_ref, v_ref, d_ref, nm_ref, nv_ref):
        d_ref[...], nm_ref[...], nv_ref[...] = _adam_math(w_ref[...], g_ref[...], m_ref[...], v_ref[...])

    spec = pl.BlockSpec((tr, cols), lambda i: (i, 0))
    outs = pl.pallas_call(body, name=name, grid=(rows // tr,), in_specs=[spec] * 4, out_specs=[spec] * 3,
                          out_shape=[jax.ShapeDtypeStruct((rows, cols), F32)] * 3, compiler_params=_cp("parallel"))(ws, gs, ms, vs)
    return tuple(o.reshape(shape) for o in outs)


_BIG = ["ffn_w_gate", "ffn_w_up", "ffn_w_down", "dn_w_in", "dn_w_out", "sg_w_in", "sg_w_out"]
_SMALL_SHARDED = ["norm_g", "dn_conv_w", "sg_b_in", "sg_ln_g", "sg_ln_b"]
_SMALL_REPL = ["dn_a_log", "dn_dt_bias", "dn_norm_g", "sg_w_s", "sg_b_s"]
_WEIGHTS = ["norm_g", "ffn_w_gate", "ffn_w_up", "ffn_w_down", "dn_w_in", "dn_conv_w", "dn_a_log", "dn_dt_bias",
            "dn_norm_g", "dn_w_out", "sg_w_in", "sg_b_in", "sg_ln_g", "sg_ln_b", "sg_w_s", "sg_b_s", "sg_w_out"]
PACK_COLS = 1024


def _pack(arrs):
    flat = jnp.concatenate([a.reshape(-1) for a in arrs])
    pad = (-flat.shape[0]) % (SUBLANES * PACK_COLS)
    return jnp.pad(flat, (0, pad)).reshape(-1, PACK_COLS)


def _unpack(buf, shapes):
    flat = buf.reshape(-1)
    out, off = [], 0
    for s in shapes:
        n = math.prod(s)
        out.append(flat[off:off + n].reshape(s))
        off += n
    return out


def _as_halves(a):
    if a.shape[0] == 2:
        return a
    if a.shape[0] == 1:
        return a.reshape((2, a.shape[1] // 2) + a.shape[2:])
    return a.reshape((2, a.shape[0] // 2) + a.shape[1:])


def _with_own(gathered, own, chip):
    g = gathered.reshape((N_CHIPS,) + own.shape)
    return [jnp.where(chip == k, own, g[k]) for k in range(N_CHIPS)]


def _cat_shards(g, axis):
    return jnp.concatenate(list(g), axis=axis)


_GROUP_ORDER = ["ffn00", "dn", "ffn01", "ffn10", "sg", "ffn11"]


def _weight_groups(w):
    cast = {k: _mx(w[k]) for k in _BIG}
    groups = {"ffn%d%d" % (i, j): [cast["ffn_w_gate"][i, j], cast["ffn_w_up"][i, j], cast["ffn_w_down"][i, j]]
              for i, j in [(0, 0), (0, 1), (1, 0), (1, 1)]}
    groups["dn"] = [cast["dn_w_in"][0], cast["dn_w_out"][0]]
    groups["sg"] = [cast["sg_w_in"][0], cast["sg_w_out"][0]]
    return groups


def _group_matrices(group, shards):
    if group.startswith("ffn"):
        gate, up, down = shards
        return {"wgu": jnp.concatenate([_cat_shards(gate, 1), _cat_shards(up, 1)], axis=1), "wd": _cat_shards(down, 0)}
    if group == "sg":
        return {"sg_win": _cat_shards(shards[0], 1), "sg_wout": _cat_shards(shards[1], 0)}
    dn_full = _cat_shards(shards[0], 1)
    W4 = 4 * DN_HEADS * DN_HEAD_DIM
    wba = jnp.zeros((D_MODEL, 2 * LANES), dn_full.dtype)
    wba = wba.at[:, :DN_HEADS].set(dn_full[:, W4:W4 + DN_HEADS])
    wba = wba.at[:, LANES:LANES + DN_HEADS].set(dn_full[:, W4 + DN_HEADS:])
    return {"dn_wqkvz": dn_full[:, :W4], "dn_wba": wba, "dn_wout": _cat_shards(shards[1], 0)}


def _split_cols(a, n):
    w = a.shape[-1] // n
    return [a[..., k * w:(k + 1) * w] for k in range(n)]


def _split_rows(a, n):
    h = a.shape[-2] // n
    return [a[..., k * h:(k + 1) * h, :] for k in range(n)]


_IJ = [(0, 0), (0, 1), (1, 0), (1, 1)]


_REDUCED = ["wguT%d%d" % ij for ij in _IJ] + ["wd%d%d" % ij for ij in _IJ] + ["dn_w_in", "dn_w_out", "sg_w_inT", "sg_w_out"]


def _group_grads(group, grads):
    def rows_by_chip(a):
        return a.reshape(N_CHIPS, 2, a.shape[0] // (2 * N_CHIPS), a.shape[1])

    if group.startswith("ffn"):
        tag = group[3:]
        t = grads["wguT" + tag]
        return (["wguT" + tag, "wd" + tag],
                [t.reshape(2, N_CHIPS, t.shape[0] // (2 * N_CHIPS), t.shape[1]), rows_by_chip(grads["wd" + tag])], [True, False])
    if group == "sg":
        return ["sg_w_inT", "sg_w_out"], [rows_by_chip(grads["sg_w_inT"]), rows_by_chip(grads["sg_w_out"])], [False, False]
    dn_in = jnp.stack([jnp.stack(_split_cols(hf, N_CHIPS)) for hf in _split_rows(grads["dn_w_in"], 2)])
    return ["dn_w_in", "dn_w_out"], [dn_in, rows_by_chip(grads["dn_w_out"])], [True, False]


def _shard_grads(mine, theirs, c, w):
    lo = [jnp.where(c == 0, a, b) for a, b in zip(mine, theirs)]
    hi = [jnp.where(c == 0, b, a) for a, b in zip(mine, theirs)]
    rows = lambda t: jnp.concatenate([lo[t], hi[t]], axis=0)
    sq = lambda parts: jnp.stack(parts).reshape(2, 2, *parts[0].shape)
    g = {}
    g["ffn_w_gate"] = sq([lo[t].T for t in range(4)])
    g["ffn_w_up"] = sq([hi[t].T for t in range(4)])
    g["ffn_w_down"] = sq([rows(4 + t) for t in range(4)])
    g["dn_w_in"] = rows(8)[None]
    g["dn_w_out"] = rows(9)[None]
    g["sg_w_in"] = rows(10).T[None]
    g["sg_w_out"] = rows(11)[None]
    return {k: v.reshape(w[k].shape) for k, v in g.items()}


def kernel(x, norm_g, ffn_w_gate, ffn_w_up, ffn_w_down, dn_w_in, dn_conv_w, dn_a_log, dn_dt_bias, dn_norm_g, dn_w_out, sg_w_in, sg_b_in, sg_ln_g, sg_ln_b, sg_w_s, sg_b_s, sg_w_out, loss_target, m_norm_g, m_ffn_w_gate, m_ffn_w_up, m_ffn_w_down, m_dn_w_in, m_dn_conv_w, m_dn_a_log, m_dn_dt_bias, m_dn_norm_g, m_dn_w_out, m_sg_w_in, m_sg_b_in, m_sg_ln_g, m_sg_ln_b, m_sg_w_s, m_sg_b_s, m_sg_w_out, v_norm_g, v_ffn_w_gate, v_ffn_w_up, v_ffn_w_down, v_dn_w_in, v_dn_conv_w, v_dn_a_log, v_dn_dt_bias, v_dn_norm_g, v_dn_w_out, v_sg_w_in, v_sg_b_in, v_sg_ln_g, v_sg_ln_b, v_sg_w_s, v_sg_b_s, v_sg_w_out):
    args = dict(locals())
    w = {k: args[k] for k in _WEIGHTS}
    mom = {k: args["m_" + k] for k in _WEIGHTS}
    var = {k: args["v_" + k] for k in _WEIGHTS}
    cx, cy, cc = _mesh_pos()
    chip = 2 * cx + cy

    small_shapes = [w[k].shape for k in _SMALL_SHARDED]
    groups = _weight_groups(w)
    own = groups[_GROUP_ORDER[0]] + [_pack([w[k] for k in _SMALL_SHARDED])]
    first = _allgather_chips([_as_halves(a) for a in own], "gather_first")
    started, after = {}, first[0]
    for g in _GROUP_ORDER[1:]:
        started[g] = _chips_start(groups[g], False, after, "gather_start_" + g)
        after = started[g]["token"]
    first = [_with_own(g, a, chip) for g, a in zip(first, own)]
    small_k = [_unpack(first[-1][k], small_shapes) for k in range(N_CHIPS)]
    p = {name: jnp.concatenate([small_k[k][i] for k in range(N_CHIPS)], axis=-1) for i, name in enumerate(_SMALL_SHARDED)}
    p = {k: (v if k == "norm_g" else v[0]) for k, v in p.items()}
    p["norm_g"] = p["norm_g"] + after[0, 0]
    for k in _SMALL_REPL:
        p[k] = w[k][0]

    def weights_for(group, after):
        if group == _GROUP_ORDER[0]:
            return _group_matrices(group, first[:-1])
        srcs, lands = _chips_wait(started[group], after, "gather_wait_" + group)
        return _group_matrices(group, [_with_own(l, a, chip) for l, a in zip(lands, srcs)])

    mine, theirs, pending = {}, {}, []

    def finish(group, names, scatter, after):
        pair_sum, got = _chips_wait(scatter, after, "reduce_wait_" + group)
        half_sum = [_sum_chips(a, b, chip, "chip_sum_" + n) for n, a, b in zip(names, pair_sum, got)]
        other = _swap_whole(half_sum, "gather_core_pair_" + group)
        mine.update(zip(names, half_sum))
        theirs.update(zip(names, other))

    def grads_ready(group, grads):
        names, halves, half_first = _group_grads(group, grads)
        from_sibling = _swap_halves(halves, half_first, "reduce_core_pair_" + group)
        pair_sum = [_add_own_half(h, o, cc, hf, "pair_sum_" + n) for n, h, o, hf in zip(names, halves, from_sibling, half_first)]
        scatter = _chips_start(pair_sum, True, from_sibling[0], "reduce_start_" + group)
        if pending:
            finish(*pending.pop(), scatter["token"])
        pending.append((group, names, scatter))
        return scatter["token"][0, 0]

    loss_part, grad_x, grads = _local_step(x[0], loss_target[0], p, weights_for, grads_ready)
    last = pending.pop()
    finish(*last, last[2]["token"])
    half_sum = [mine[n] for n in _REDUCED]
    other_half = [theirs[n] for n in _REDUCED]

    small_names = _SMALL_SHARDED + _SMALL_REPL
    small_grads = [grads[k] for k in small_names]
    full_shapes = [g.shape for g in small_grads] + [(1,)]
    pack = _pack(small_grads + [loss_part[0, :1]])
    summed = _sum_leading(_allgather_devices(pack, "gather_small"), "small_sum")
    parts = _unpack(summed, full_shapes)
    loss = parts[-1][0]
    small_grad = {}
    for i, k in enumerate(small_names):
        g = parts[i]
        if k in _SMALL_SHARDED:
            n = w[k].shape[-1]
            g = lax.dynamic_slice_in_dim(g, chip * n, n, axis=g.ndim - 1)
        small_grad[k] = g

    grad = {**small_grad, **_shard_grads(half_sum, other_half, cc, w)}
    delta, new_m, new_v = {}, {}, {}
    for k in _BIG:
        delta[k], new_m[k], new_v[k] = _adamw(w[k], grad[k], mom[k], var[k], "adamw_" + k)
    shapes = [w[k].shape for k in small_names]
    d, nm, nv = _adamw(_pack([w[k] for k in small_names]), _pack([grad[k] for k in small_names]),
                       _pack([mom[k] for k in small_names]), _pack([var[k] for k in small_names]), "adamw_small")
    for k, a, b, c_ in zip(small_names, _unpack(d, shapes), _unpack(nm, shapes), _unpack(nv, shapes)):
        delta[k], new_m[k], new_v[k] = a, b, c_

    return (loss, grad_x[None], *[grad[k] for k in _WEIGHTS], *[delta[k] for k in _WEIGHTS],
            *[new_m[k] for k in _WEIGHTS], *[new_v[k] for k in _WEIGHTS])
```

```python
import functools
import math

import jax
import jax.numpy as jnp
from jax import lax
from jax.experimental import pallas as pl
from jax.experimental.pallas import tpu as pltpu

F32 = jnp.float32
MXU_DTYPE = jnp.bfloat16
COMM_DTYPE = jnp.bfloat16
HI = lax.Precision.HIGHEST
TRI_PREC = lax.Precision.HIGH

D_MODEL = 1024
D_FF = 2816
RMS_EPS = 1e-6
LN_EPS = 1e-5
L2_EPS = 1e-6
DN_HEADS = 8
DN_HEAD_DIM = 128
DN_CONV = 4
DN_CHUNK = 64
SG_WIDTH = 2048
SG_GROUPS = 8
SG_CHUNK = 128
SG_GROUP_W = SG_WIDTH // SG_GROUPS
N_CHIPS = 4
N_DEV = 8
LANES = 128
SUBLANES = 8
VMEM_LIMIT = 56 * 1024 * 1024

ADAM_LR = 0.001
ADAM_B1 = 0.9
ADAM_B2 = 0.999
ADAM_EPS = 1e-08
ADAM_WD = 0.01
ADAM_STEP = 10

MESH = pl.DeviceIdType.MESH
ANY = pl.BlockSpec(memory_space=pl.ANY)


def _cp(*sem):
    return pltpu.CompilerParams(dimension_semantics=sem, vmem_limit_bytes=VMEM_LIMIT)


def _pick(n, pref, mult=LANES):
    best = None
    d = mult
    while d <= min(n, pref):
        if n % d == 0:
            best = d
        d += mult
    return best if best is not None else n


def _full(shape):
    nd = len(shape)
    return pl.BlockSpec(shape, lambda *_: (0,) * nd)


def _sigmoid(x):
    return 1.0 / (1.0 + jnp.exp(-x))


def _dot(a, b, dims, prec=None):
    return lax.dot_general(a, b, (dims, ((), ())), preferred_element_type=F32, precision=prec)


NN = ((1,), (0,))
NT = ((1,), (1,))
TN = ((0,), (0,))


def _mx(a):
    return a.astype(MXU_DTYPE)


def _rms_stat(x):
    return lax.rsqrt(jnp.mean(x * x, axis=-1, keepdims=True) + RMS_EPS)


def _rms_bwd(x, r, g, dy):
    xh = x * r
    dxh = dy * g
    dx = r * (dxh - xh * jnp.mean(dxh * xh, axis=-1, keepdims=True))
    return dx, jnp.sum(dy * xh, axis=0, keepdims=True)


def _mm(a, b, mode, name, out_dtype=F32, add=None):
    if mode == "tn":
        K, M = a.shape
        N = b.shape[1]
    elif mode == "nt":
        M, K = a.shape
        N = b.shape[0]
    else:
        M, K = a.shape
        N = b.shape[1]
    tn = _pick(N, 1024)
    if mode == "tn":
        tm = _pick(M, 1024 if tn <= 512 else 1408)
        tk = _pick(K, 1024, SUBLANES)
    else:
        tm = _pick(M, max(512, min(2048, (512 * 1024) // tn)), SUBLANES)
        tk = _pick(K, 2048)
    nk = K // tk
    grid = (N // tn, M // tm, nk)
    if mode == "nn":
        a_spec = pl.BlockSpec((tm, tk), lambda j, i, k: (i, k))
        b_spec = pl.BlockSpec((tk, tn), lambda j, i, k: (k, j))
        dims = NN
    elif mode == "nt":
        a_spec = pl.BlockSpec((tm, tk), lambda j, i, k: (i, k))
        b_spec = pl.BlockSpec((tn, tk), lambda j, i, k: (j, k))
        dims = NT
    else:
        a_spec = pl.BlockSpec((tk, tm), lambda j, i, k: (k, i))
        b_spec = pl.BlockSpec((tk, tn), lambda j, i, k: (k, j))
        dims = TN
    o_spec = pl.BlockSpec((tm, tn), lambda j, i, k: (i, j))
    has_add = add is not None

    def body(*refs):
        if has_add:
            a_ref, b_ref, add_ref, o_ref, acc = refs
        else:
            a_ref, b_ref, o_ref, acc = refs
        k = pl.program_id(2)

        @pl.when(k == 0)
        def _():
            acc[...] = add_ref[...] if has_add else jnp.zeros_like(acc)

        acc[...] += _dot(a_ref[...], b_ref[...], dims)

        @pl.when(k == nk - 1)
        def _():
            o_ref[...] = acc[...].astype(o_ref.dtype)

    ins = [a, b] + ([add] if has_add else [])
    specs = [a_spec, b_spec] + ([o_spec] if has_add else [])
    return pl.pallas_call(
        body, name=name, grid=grid, in_specs=specs, out_specs=o_spec,
        out_shape=jax.ShapeDtypeStruct((M, N), out_dtype),
        scratch_shapes=[pltpu.VMEM((tm, tn), F32)],
        compiler_params=_cp("parallel", "parallel", "arbitrary"),
    )(*ins)


def _load_resident(pairs, sem):
    @pl.when(pl.program_id(0) == 0)
    def _():
        cps = [pltpu.make_async_copy(src, dst, sem.at[i]) for i, (src, dst) in enumerate(pairs)]
        for c in cps:
            c.start()
        for c in cps:
            c.wait()


def _ffn_fwd(x, g0, g1, wgu, wd, name):
    T, D = x.shape
    F2 = wgu.shape[1]
    F = F2 // 2
    tm = _pick(T, 256, SUBLANES)

    def body(x_ref, g0_ref, g1_ref, wgu_hbm, wd_hbm, xo_ref, h_ref, gu_ref, y_ref, wgu_v, wd_v, sem):
        _load_resident([(wgu_hbm, wgu_v), (wd_hbm, wd_v)], sem)
        xv = x_ref[...]
        hb = _mx(xv * _rms_stat(xv) * g0_ref[...])
        h_ref[...] = hb
        gu = _dot(hb, wgu_v[...], NN)
        gu_ref[...] = gu.astype(gu_ref.dtype)
        g = gu[:, :F]
        u = gu[:, F:]
        a = _mx(g * _sigmoid(g) * u)
        y = _dot(a, wd_v[...], NN)
        y_ref[...] = y
        xo_ref[...] = xv + 0.5 * (y * _rms_stat(y) * g1_ref[...])

    row = lambda w: pl.BlockSpec((tm, w), lambda i: (i, 0))
    return pl.pallas_call(
        body, name=name, grid=(T // tm,),
        in_specs=[row(D), _full((1, D)), _full((1, D)), ANY, ANY],
        out_specs=[row(D), row(D), row(F2), row(D)],
        out_shape=[jax.ShapeDtypeStruct((T, D), F32), jax.ShapeDtypeStruct((T, D), MXU_DTYPE),
                   jax.ShapeDtypeStruct((T, F2), MXU_DTYPE), jax.ShapeDtypeStruct((T, D), F32)],
        scratch_shapes=[pltpu.VMEM(wgu.shape, wgu.dtype), pltpu.VMEM(wd.shape, wd.dtype),
                        pltpu.SemaphoreType.DMA((2,))],
        compiler_params=_cp("arbitrary"),
    )(x, g0, g1, wgu, wd)


FFN_BWD_CHUNK = 1408


def _ffn_bwd(dxo, x, y, gu, g0, g1, wgu, wd, name):
    T, D = x.shape
    F2 = gu.shape[1]
    F = F2 // 2
    tm = _pick(T, 256, SUBLANES)
    fc = _pick(F, FFN_BWD_CHUNK)

    def body(dxo_ref, x_ref, y_ref, gu_ref, g0_ref, g1_ref, wgu_hbm, wd_hbm,
             dx_ref, dy_ref, a_ref, dgu_ref, dg0_ref, dg1_ref, wgu_v, wd_v, sem):
        _load_resident([(wgu_hbm, wgu_v), (wd_hbm, wd_v)], sem)

        @pl.when(pl.program_id(0) == 0)
        def _():
            dg0_ref[...] = jnp.zeros_like(dg0_ref)
            dg1_ref[...] = jnp.zeros_like(dg1_ref)

        dxo_v = dxo_ref[...]
        yv = y_ref[...]
        dy, dg1 = _rms_bwd(yv, _rms_stat(yv), g1_ref[...], 0.5 * dxo_v)
        dg1_ref[...] += dg1
        dyb = _mx(dy)
        dy_ref[...] = dyb
        dh = jnp.zeros((tm, D), F32)
        for c in range(F // fc):
            lo, hi = c * fc, (c + 1) * fc
            da = _dot(dyb, wd_v[lo:hi, :], NT)
            g = gu_ref[:, lo:hi].astype(F32)
            u = gu_ref[:, F + lo:F + hi].astype(F32)
            s = _sigmoid(g)
            sg = g * s
            a_ref[:, lo:hi] = _mx(sg * u)
            dg = _mx(da * u * (s * (1.0 + g * (1.0 - s))))
            du = _mx(da * sg)
            dgu_ref[:, lo:hi] = dg
            dgu_ref[:, F + lo:F + hi] = du
            dh = dh + _dot(dg, wgu_v[:, lo:hi], NT) + _dot(du, wgu_v[:, F + lo:F + hi], NT)
        xv = x_ref[...]
        dx, dg0 = _rms_bwd(xv, _rms_stat(xv), g0_ref[...], dh)
        dg0_ref[...] += dg0
        dx_ref[...] = dxo_v + dx

    row = lambda w: pl.BlockSpec((tm, w), lambda i: (i, 0))
    one = _full((1, D))
    return pl.pallas_call(
        body, name=name, grid=(T // tm,),
        in_specs=[row(D), row(D), row(D), row(F2), one, one, ANY, ANY],
        out_specs=[row(D), row(D), row(F), row(F2), one, one],
        out_shape=[jax.ShapeDtypeStruct((T, D), F32), jax.ShapeDtypeStruct((T, D), MXU_DTYPE),
                   jax.ShapeDtypeStruct((T, F), MXU_DTYPE), jax.ShapeDtypeStruct((T, F2), MXU_DTYPE),
                   jax.ShapeDtypeStruct((1, D), F32), jax.ShapeDtypeStruct((1, D), F32)],
        scratch_shapes=[pltpu.VMEM(wgu.shape, wgu.dtype), pltpu.VMEM(wd.shape, wd.dtype), pltpu.SemaphoreType.DMA((2,))],
        compiler_params=_cp("arbitrary"),
    )(dxo, x, y, gu, g0, g1, wgu, wd)


def _norm_fwd(x, g, name):
    T, D = x.shape
    tm = _pick(T, 512, SUBLANES)

    def body(x_ref, g_ref, h_ref):
        xv = x_ref[...]
        h_ref[...] = _mx(xv * _rms_stat(xv) * g_ref[...])

    row = pl.BlockSpec((tm, D), lambda i: (i, 0))
    return pl.pallas_call(body, name=name, grid=(T // tm,), in_specs=[row, _full((1, D))], out_specs=row,
                          out_shape=jax.ShapeDtypeStruct((T, D), MXU_DTYPE), compiler_params=_cp("parallel"))(x, g)


def _postnorm_fwd(x, m, g, name):
    T, D = x.shape
    tm = _pick(T, 512, SUBLANES)

    def body(x_ref, m_ref, g_ref, o_ref):
        mv = m_ref[...]
        o_ref[...] = x_ref[...] + mv * _rms_stat(mv) * g_ref[...]

    row = pl.BlockSpec((tm, D), lambda i: (i, 0))
    return pl.pallas_call(body, name=name, grid=(T // tm,), in_specs=[row, row, _full((1, D))], out_specs=row,
                          out_shape=jax.ShapeDtypeStruct((T, D), F32), compiler_params=_cp("parallel"))(x, m, g)


def _postnorm_bwd(dxo, m, g, name):
    T, D = m.shape
    tm = _pick(T, 512, SUBLANES)

    def body(dxo_ref, m_ref, g_ref, dm_ref, dg_ref):
        @pl.when(pl.program_id(0) == 0)
        def _():
            dg_ref[...] = jnp.zeros_like(dg_ref)

        mv = m_ref[...]
        dm, dg = _rms_bwd(mv, _rms_stat(mv), g_ref[...], dxo_ref[...])
        dg_ref[...] += dg
        dm_ref[...] = _mx(dm)

    row = pl.BlockSpec((tm, D), lambda i: (i, 0))
    return pl.pallas_call(body, name=name, grid=(T // tm,), in_specs=[row, row, _full((1, D))],
                          out_specs=[row, _full((1, D))],
                          out_shape=[jax.ShapeDtypeStruct((T, D), MXU_DTYPE), jax.ShapeDtypeStruct((1, D), F32)],
                          compiler_params=_cp("arbitrary"))(dxo, m, g)


def _prenorm_bwd(dxo, dh, x, g, name):
    T, D = x.shape
    tm = _pick(T, 512, SUBLANES)

    def body(dxo_ref, dh_ref, x_ref, g_ref, dx_ref, dg_ref):
        @pl.when(pl.program_id(0) == 0)
        def _():
            dg_ref[...] = jnp.zeros_like(dg_ref)

        xv = x_ref[...]
        dx, dg = _rms_bwd(xv, _rms_stat(xv), g_ref[...], dh_ref[...])
        dg_ref[...] += dg
        dx_ref[...] = dxo_ref[...] + dx

    row = pl.BlockSpec((tm, D), lambda i: (i, 0))
    return pl.pallas_call(body, name=name, grid=(T // tm,), in_specs=[row, row, row, _full((1, D))],
                          out_specs=[row, _full((1, D))],
                          out_shape=[jax.ShapeDtypeStruct((T, D), F32), jax.ShapeDtypeStruct((1, D), F32)],
                          compiler_params=_cp("arbitrary"))(dxo, dh, x, g)


def _loss_fwd_bwd(y, target, name):
    T, D = y.shape
    tm = _pick(T, 512, SUBLANES)

    def body(y_ref, t_ref, l_ref, dy_ref):
        @pl.when(pl.program_id(0) == 0)
        def _():
            l_ref[...] = jnp.zeros_like(l_ref)

        e = y_ref[...] - t_ref[...]
        dy_ref[...] = e * (1.0 / D)
        l_ref[...] += 0.5 * jnp.sum(jnp.mean(e * e, axis=-1, keepdims=True), axis=0, keepdims=True)

    row = pl.BlockSpec((tm, D), lambda i: (i, 0))
    return pl.pallas_call(body, name=name, grid=(T // tm,), in_specs=[row, row],
                          out_specs=[_full((SUBLANES, LANES)), row],
                          out_shape=[jax.ShapeDtypeStruct((SUBLANES, LANES), F32), jax.ShapeDtypeStruct((T, D), F32)],
                          compiler_params=_cp("arbitrary"))(y, target)


DN_ROWS = 512


def _shift_down(prev8, cur, s):
    n = cur.shape[0]
    xx = jnp.concatenate([prev8, cur], axis=0)
    return pltpu.roll(xx, s, 0)[SUBLANES:SUBLANES + n, :]


def _shift_up(cur, next8, s):
    n = cur.shape[0]
    xx = jnp.concatenate([cur, next8], axis=0)
    return pltpu.roll(xx, n + SUBLANES - s, 0)[:n, :]


def _conv_tile(x_ref, w, r, rows):
    start = pl.multiple_of(r * rows, SUBLANES)
    cur = x_ref[pl.ds(start, rows), :]
    pstart = pl.multiple_of(jnp.maximum(start - SUBLANES, 0), SUBLANES)
    prev8 = jnp.where(r == 0, 0.0, x_ref[pl.ds(pstart, SUBLANES), :])
    taps = [_shift_down(prev8, cur, DN_CONV - 1 - j) if j < DN_CONV - 1 else cur for j in range(DN_CONV)]
    c = taps[0] * w[0:1, :]
    for j in range(1, DN_CONV):
        c = c + taps[j] * w[j:j + 1, :]
    return c, taps


def _dn_prep_fwd(proj, conv_w, name):
    T = proj.shape[0]
    W = DN_HEADS * DN_HEAD_DIM
    rows = min(DN_ROWS, T)
    n_inner = T // rows
    scale = DN_HEAD_DIM ** -0.5

    def body(x_ref, w_ref, o_ref):
        cb = pl.program_id(0)
        w = w_ref[...]
        is_qk = cb < 2 * DN_HEADS
        post = jnp.where(cb < DN_HEADS, scale, 1.0)

        def step(r, carry):
            c, _ = _conv_tile(x_ref, w, r, rows)
            s = c * _sigmoid(c)
            rinv = lax.rsqrt(jnp.sum(s * s, axis=-1, keepdims=True) + L2_EPS)
            o_ref[pl.ds(pl.multiple_of(r * rows, SUBLANES), rows), :] = jnp.where(is_qk, s * rinv * post, s)
            return carry

        lax.fori_loop(0, n_inner, step, 0)

    col = pl.BlockSpec((T, LANES), lambda j: (0, j))
    return pl.pallas_call(body, name=name, grid=(3 * W // LANES,),
                          in_specs=[col, pl.BlockSpec((DN_CONV, LANES), lambda j: (0, j))], out_specs=col,
                          out_shape=jax.ShapeDtypeStruct((T, 3 * W), F32), compiler_params=_cp("parallel"))(proj, conv_w)


def _dn_prep_bwd(proj, conv_w, dqkv, name):
    T = proj.shape[0]
    W = DN_HEADS * DN_HEAD_DIM
    rows = min(DN_ROWS, T)
    n_inner = T // rows
    scale = DN_HEAD_DIM ** -0.5

    def body(x_ref, w_ref, dy_ref, dx_ref, dw_ref, dc_scr):
        cb = pl.program_id(0)
        w = w_ref[...]
        is_qk = cb < 2 * DN_HEADS
        post = jnp.where(cb < DN_HEADS, scale, 1.0)

        def step1(r, dws):
            c, taps = _conv_tile(x_ref, w, r, rows)
            sg = _sigmoid(c)
            s = c * sg
            rinv = lax.rsqrt(jnp.sum(s * s, axis=-1, keepdims=True) + L2_EPS)
            dy = dy_ref[pl.ds(pl.multiple_of(r * rows, SUBLANES), rows), :]
            yn = s * rinv
            dyn = dy * post
            ds_qk = rinv * (dyn - yn * jnp.sum(dyn * yn, axis=-1, keepdims=True))
            ds = jnp.where(is_qk, ds_qk, dy)
            dc = ds * (sg * (1.0 + c * (1.0 - sg)))
            dc_scr[pl.ds(pl.multiple_of(r * rows, SUBLANES), rows), :] = dc
            return tuple(dws[j] + jnp.sum(dc * taps[j], axis=0, keepdims=True) for j in range(DN_CONV))

        zero = jnp.zeros((1, LANES), F32)
        dws = lax.fori_loop(0, n_inner, step1, (zero,) * DN_CONV)
        for j in range(DN_CONV):
            dw_ref[j:j + 1, :] = dws[j]

        def step2(r, carry):
            start = pl.multiple_of(r * rows, SUBLANES)
            cur = dc_scr[pl.ds(start, rows), :]
            nstart = pl.multiple_of(jnp.minimum(start + rows, T - SUBLANES), SUBLANES)
            next8 = jnp.where(r == n_inner - 1, 0.0, dc_scr[pl.ds(nstart, SUBLANES), :])
            dx = cur * w[DN_CONV - 1:DN_CONV, :]
            for j in range(DN_CONV - 1):
                dx = dx + _shift_up(cur, next8, DN_CONV - 1 - j) * w[j:j + 1, :]
            dx_ref[pl.ds(start, rows), :] = _mx(dx)
            return carry

        lax.fori_loop(0, n_inner, step2, 0)

    col = pl.BlockSpec((T, LANES), lambda j: (0, j))
    wspec = pl.BlockSpec((DN_CONV, LANES), lambda j: (0, j))
    return pl.pallas_call(body, name=name, grid=(3 * W // LANES,), in_specs=[col, wspec, col], out_specs=[col, wspec],
                          out_shape=[jax.ShapeDtypeStruct((T, 3 * W), MXU_DTYPE), jax.ShapeDtypeStruct((DN_CONV, 3 * W), F32)],
                          scratch_shapes=[pltpu.VMEM((T, LANES), F32)], compiler_params=_cp("parallel"))(proj, conv_w, dqkv)


def _softplus(x):
    return jnp.maximum(x, 0.0) + jnp.log(1.0 + jnp.exp(-jnp.abs(x)))


def _dn_gate_fwd(ba, a_log, dt_bias, name):
    T = ba.shape[0]
    tm = _pick(T, 1024, SUBLANES)

    def body(ba_ref, al_ref, dt_ref, beta_ref, g_ref):
        beta_ref[...] = _sigmoid(ba_ref[:, :LANES])
        g_ref[...] = -jnp.exp(al_ref[...]) * _softplus(ba_ref[:, LANES:] + dt_ref[...])

    row = lambda w: pl.BlockSpec((tm, w), lambda i: (i, 0))
    return pl.pallas_call(body, name=name, grid=(T // tm,), in_specs=[row(2 * LANES), _full((1, LANES)), _full((1, LANES))],
                          out_specs=[row(LANES), row(LANES)],
                          out_shape=[jax.ShapeDtypeStruct((T, LANES), F32)] * 2, compiler_params=_cp("parallel"))(ba, a_log, dt_bias)


def _dn_gate_bwd(ba, a_log, dt_bias, dbeta, dg, name):
    T = ba.shape[0]
    tm = _pick(T, 1024, SUBLANES)

    def body(ba_ref, al_ref, dt_ref, dbeta_ref, dg_ref, dba_ref, dal_ref, ddt_ref):
        @pl.when(pl.program_id(0) == 0)
        def _():
            dal_ref[...] = jnp.zeros_like(dal_ref)
            ddt_ref[...] = jnp.zeros_like(ddt_ref)

        beta = _sigmoid(ba_ref[:, :LANES])
        dba_ref[:, :LANES] = _mx(dbeta_ref[...] * beta * (1.0 - beta))
        pre = ba_ref[:, LANES:] + dt_ref[...]
        ea = jnp.exp(al_ref[...])
        dgv = dg_ref[...]
        da = dgv * (-ea) * _sigmoid(pre)
        dba_ref[:, LANES:] = _mx(da)
        ddt_ref[...] += jnp.sum(da, axis=0, keepdims=True)
        dal_ref[...] += jnp.sum(dgv * (-ea) * _softplus(pre), axis=0, keepdims=True)

    row = lambda w: pl.BlockSpec((tm, w), lambda i: (i, 0))
    one = _full((1, LANES))
    return pl.pallas_call(body, name=name, grid=(T // tm,), in_specs=[row(2 * LANES), one, one, row(LANES), row(LANES)],
                          out_specs=[row(2 * LANES), one, one],
                          out_shape=[jax.ShapeDtypeStruct((T, 2 * LANES), MXU_DTYPE), jax.ShapeDtypeStruct((1, LANES), F32),
                                     jax.ShapeDtypeStruct((1, LANES), F32)],
                          compiler_params=_cp("arbitrary"))(ba, a_log, dt_bias, dbeta, dg)


def _tri(c, strict):
    i = lax.broadcasted_iota(jnp.int32, (c, c), 0)
    j = lax.broadcasted_iota(jnp.int32, (c, c), 1)
    return (i > j) if strict else (i >= j)


def _inv_unit_lower(ls):
    c = ls[0].shape[0]
    i = lax.broadcasted_iota(jnp.int32, (c, c), 0)
    j = lax.broadcasted_iota(jnp.int32, (c, c), 1)
    eye = jnp.where(i == j, 1.0, 0.0)
    facs = [[eye - l for l in ls]]
    cur = ls
    for _ in range(int(math.log2(c)) - 1):
        cur = [_dot(p, p, NN, TRI_PREC) for p in cur]
        facs.append([eye + p for p in cur])
    while len(facs) > 1:
        nxt = [[_dot(a, b, NN, TRI_PREC) for a, b in zip(facs[t], facs[t + 1])] for t in range(0, len(facs) - 1, 2)]
        if len(facs) % 2:
            nxt.append(facs[-1])
        facs = nxt
    return facs[0]


def _chunk_gates(g_blk):
    c = g_blk.shape[0]
    gcs = _dot(jnp.where(_tri(c, False), 1.0, 0.0), g_blk, NN, HI)
    return gcs, gcs.T


def _head_chunk(h, qh, kh, vh, beta_blk, gcs, gcs_t):
    c = qh.shape[0]
    incl = _tri(c, False)
    gc_col = gcs[:, h:h + 1]
    gc_row = gcs_t[h:h + 1, :]
    gc_last = gcs_t[h:h + 1, c - 1:c]
    dec = jnp.where(incl, jnp.exp(jnp.where(incl, gc_col - gc_row, 0.0)), 0.0)
    gam = jnp.exp(gc_col)
    rr = jnp.exp(gc_last - gc_col)
    gl = jnp.exp(gc_last)
    b = beta_blk[:, h:h + 1]
    kb = kh * b
    vb = vh * b
    kk = _dot(_mx(kb), _mx(kh), NT)
    lmat = jnp.where(_tri(c, True), kk * dec, 0.0)
    qk = _dot(_mx(qh), _mx(kh), NT)
    pmat = jnp.where(incl, qk * dec, 0.0)
    return dict(dec=dec, gam=gam, rr=rr, gl=gl, b=b, kb=kb, vb=vb, lmat=lmat, pmat=pmat)


def _dn_scan_fwd(qkv, beta, g, proj, norm_g, name):
    T = qkv.shape[0]
    C, H, Dh = DN_CHUNK, DN_HEADS, DN_HEAD_DIM
    W = H * Dh
    N = T // C

    def body(q_ref, k_ref, v_ref, beta_ref, g_ref, z_ref, ng_ref, og_ref, o_ref, tinv_ref, s_ref, state):
        @pl.when(pl.program_id(0) == 0)
        def _():
            state[...] = jnp.zeros_like(state)

        gcs, gcs_t = _chunk_gates(g_ref[...])
        beta_blk = beta_ref[...]
        ng = ng_ref[...]
        heads = range(H)
        cs = [slice(h * Dh, (h + 1) * Dh) for h in heads]
        qs = [_head_chunk(h, q_ref[:, cs[h]], k_ref[:, cs[h]], v_ref[:, cs[h]], beta_blk, gcs, gcs_t) for h in heads]
        tinvs = _inv_unit_lower([q["lmat"] for q in qs])
        for h in heads:
            tinv_ref[h] = tinvs[h]
        us = [_dot(tinvs[h], qs[h]["vb"], NN, TRI_PREC) for h in heads]
        ws = [_dot(tinvs[h], qs[h]["kb"] * qs[h]["gam"], NN, TRI_PREC) for h in heads]
        ss = [state[h] for h in heads]
        for h in heads:
            s_ref[0, h] = ss[h]
        sbs = [_mx(s) for s in ss]
        vnbs = [_mx(us[h] - _dot(_mx(ws[h]), sbs[h], NN)) for h in heads]
        os_ = [_dot(_mx(q_ref[:, cs[h]] * qs[h]["gam"]), sbs[h], NN) + _dot(_mx(qs[h]["pmat"]), vnbs[h], NN) for h in heads]
        for h in heads:
            state[h] = ss[h] * qs[h]["gl"] + _dot(_mx((k_ref[:, cs[h]] * qs[h]["rr"]).T), vnbs[h], NN)
        for h in heads:
            o = os_[h]
            o_ref[:, cs[h]] = o
            zh = z_ref[:, cs[h]]
            og_ref[:, cs[h]] = _mx(o * _rms_stat(o) * ng * (zh * _sigmoid(zh)))

    blk = lambda j: pl.BlockSpec((C, W), lambda n: (n, j))
    small = pl.BlockSpec((C, LANES), lambda n: (n, 0))
    return pl.pallas_call(
        body, name=name, grid=(N,),
        in_specs=[blk(0), blk(1), blk(2), small, small, blk(3), _full((1, Dh))],
        out_specs=[blk(0), blk(0), pl.BlockSpec((H, C, C), lambda n: (0, n, 0)),
                   pl.BlockSpec((1, H, Dh, Dh), lambda n: (n, 0, 0, 0))],
        out_shape=[jax.ShapeDtypeStruct((T, W), MXU_DTYPE), jax.ShapeDtypeStruct((T, W), F32),
                   jax.ShapeDtypeStruct((H, T, C), F32), jax.ShapeDtypeStruct((N, H, Dh, Dh), F32)],
        scratch_shapes=[pltpu.VMEM((H, Dh, Dh), F32)],
        compiler_params=_cp("arbitrary"),
    )(qkv, qkv, qkv, beta, g, proj, norm_g)


def _dn_scan_bwd(qkv, beta, g, proj, norm_g, o, tinv, s_all, dog, name):
    T = qkv.shape[0]
    C, H, Dh = DN_CHUNK, DN_HEADS, DN_HEAD_DIM
    W = H * Dh
    N = T // C

    def body(q_ref, k_ref, v_ref, beta_ref, g_ref, z_ref, ng_ref, o_ref, tinv_ref, s_ref, dog_ref,
             dqkv_ref, dbeta_ref, dg_ref, dz_ref, dng_ref, dstate):
        @pl.when(pl.program_id(0) == 0)
        def _():
            dstate[...] = jnp.zeros_like(dstate)
            dng_ref[...] = jnp.zeros_like(dng_ref)

        gcs, gcs_t = _chunk_gates(g_ref[...])
        beta_blk = beta_ref[...]
        ng = ng_ref[...]
        incl = _tri(C, False)
        strict = _tri(C, True)
        lane = lax.broadcasted_iota(jnp.int32, (C, LANES), 1)
        rowi = lax.broadcasted_iota(jnp.int32, (C, 1), 0)
        ones = jnp.ones((C, LANES), F32)
        dbeta_acc = jnp.zeros((C, LANES), F32)
        dgc_acc = jnp.zeros((C, LANES), F32)
        dng_acc = jnp.zeros((1, Dh), F32)
        heads = range(H)
        cs = [slice(h * Dh, (h + 1) * Dh) for h in heads]
        rsum = lambda t: jnp.sum(t, axis=1, keepdims=True)
        dobs = []
        for h in heads:
            oh, zh, dogh = o_ref[:, cs[h]], z_ref[:, cs[h]], dog_ref[:, cs[h]]
            rstat = _rms_stat(oh)
            sz = _sigmoid(zh)
            dz_ref[:, cs[h]] = _mx(dogh * (oh * rstat * ng) * (sz * (1.0 + zh * (1.0 - sz))))
            do, dng = _rms_bwd(oh, rstat, ng, dogh * (zh * sz))
            dng_acc = dng_acc + dng
            dobs.append(_mx(do))
        qs = [_head_chunk(h, q_ref[:, cs[h]], k_ref[:, cs[h]], v_ref[:, cs[h]], beta_blk, gcs, gcs_t) for h in heads]
        tms = [tinv_ref[h] for h in heads]
        us = [_dot(tms[h], qs[h]["vb"], NN, TRI_PREC) for h in heads]
        ws = [_dot(tms[h], qs[h]["kb"] * qs[h]["gam"], NN, TRI_PREC) for h in heads]
        ss = [s_ref[0, h] for h in heads]
        sbs = [_mx(s) for s in ss]
        wbs = [_mx(w) for w in ws]
        vnbs = [_mx(us[h] - _dot(wbs[h], sbs[h], NN)) for h in heads]
        dsns = [dstate[h] for h in heads]
        dsbs = [_mx(d) for d in dsns]
        dvnews = [_dot(_mx(qs[h]["pmat"]), dobs[h], TN) + _dot(_mx(k_ref[:, cs[h]] * qs[h]["rr"]), dsbs[h], NN) for h in heads]
        dvb16s = [_mx(d) for d in dvnews]
        dps = [jnp.where(incl, _dot(dobs[h], vnbs[h], NT), 0.0) for h in heads]
        dqds = [_dot(dobs[h], sbs[h], NT) for h in heads]
        dkds = [_dot(vnbs[h], dsbs[h], NT) for h in heads]
        dgls = [jnp.sum(rsum(ss[h] * dsns[h]), axis=0, keepdims=True) for h in heads]
        dws = [-_dot(dvb16s[h], sbs[h], NT) for h in heads]
        for h in heads:
            dstate[h] = (_dot(_mx(q_ref[:, cs[h]] * qs[h]["gam"]), dobs[h], TN) + qs[h]["gl"] * dsns[h]
                         - _dot(wbs[h], dvb16s[h], TN))
        dvbs = [_dot(tms[h], dvnews[h], TN, TRI_PREC) for h in heads]
        dkbgs = [_dot(tms[h], dws[h], TN, TRI_PREC) for h in heads]
        dls = [jnp.where(strict, -(_dot(dvbs[h], us[h], NT, TRI_PREC) + _dot(dkbgs[h], ws[h], NT, TRI_PREC)), 0.0)
               for h in heads]
        mmats = [dls[h] * qs[h]["lmat"] + dps[h] * qs[h]["pmat"] for h in heads]
        dgcs = [rsum(mmats[h]) - _dot(mmats[h], ones, TN, HI)[:, :1] for h in heads]
        dkk16s = [_mx(dls[h] * qs[h]["dec"]) for h in heads]
        dqk16s = [_mx(dps[h] * qs[h]["dec"]) for h in heads]
        for h in heads:
            q = qs[h]
            qh, kh, vh = q_ref[:, cs[h]], k_ref[:, cs[h]], v_ref[:, cs[h]]
            gam, rr, b, kb = q["gam"], q["rr"], q["b"], q["kb"]
            dkb = _dot(dkk16s[h], _mx(kh), NN) + dkbgs[h] * gam
            dk = _dot(dkk16s[h], _mx(kb), TN) + _dot(dqk16s[h], _mx(qh), TN) + dkb * b + dkds[h] * rr
            dq = _dot(dqk16s[h], _mx(kh), NN) + dqds[h] * gam
            dgam = rsum(dkbgs[h] * kb) + rsum(dqds[h] * qh)
            dr = rsum(dkds[h] * kh)
            dgc_last = jnp.sum(dr * rr, axis=0, keepdims=True) + dgls[h] * q["gl"]
            dgc = dgcs[h] + dgam * gam - dr * rr + jnp.where(rowi == C - 1, dgc_last, 0.0)
            dbeta = rsum(dvbs[h] * vh) + rsum(dkb * kh)
            dqkv_ref[:, cs[h]] = dq
            dqkv_ref[:, W + h * Dh:W + (h + 1) * Dh] = dk
            dqkv_ref[:, 2 * W + h * Dh:2 * W + (h + 1) * Dh] = dvbs[h] * b
            dbeta_acc = jnp.where(lane == h, dbeta, dbeta_acc)
            dgc_acc = jnp.where(lane == h, dgc, dgc_acc)
        dbeta_ref[...] = dbeta_acc
        dg_ref[...] = _dot(jnp.where(incl, 1.0, 0.0), dgc_acc, TN, HI)
        dng_ref[...] += dng_acc

    rev = lambda n: N - 1 - n
    blk = lambda j: pl.BlockSpec((C, W), lambda n: (rev(n), j))
    small = pl.BlockSpec((C, LANES), lambda n: (rev(n), 0))
    return pl.pallas_call(
        body, name=name, grid=(N,),
        in_specs=[blk(0), blk(1), blk(2), small, small, blk(3), _full((1, Dh)), blk(0),
                  pl.BlockSpec((H, C, C), lambda n: (0, rev(n), 0)),
                  pl.BlockSpec((1, H, Dh, Dh), lambda n: (rev(n), 0, 0, 0)), blk(0)],
        out_specs=[pl.BlockSpec((C, 3 * W), lambda n: (rev(n), 0)), small, small, blk(0), _full((1, Dh))],
        out_shape=[jax.ShapeDtypeStruct((T, 3 * W), F32), jax.ShapeDtypeStruct((T, LANES), F32),
                   jax.ShapeDtypeStruct((T, LANES), F32), jax.ShapeDtypeStruct((T, W), MXU_DTYPE),
                   jax.ShapeDtypeStruct((1, Dh), F32)],
        scratch_shapes=[pltpu.VMEM((H, Dh, Dh), F32)],
        compiler_params=_cp("arbitrary"),
    )(qkv, qkv, qkv, beta, g, proj, norm_g, o, tinv, s_all, dog)


_INV_SQRT2 = 0.7071067811865476
_INV_SQRT_2PI = 0.3989422804014327


def _sg_recompute(zp_ref, bin_ref, lng_ref, lnb_ref):
    E = SG_WIDTH
    zin = zp_ref[...] + bin_ref[...]
    cdf = 0.5 * (1.0 + lax.erf(zin * _INV_SQRT2))
    zz = zin * cdf
    u = zz[:, :E]
    vp = zz[:, E:]
    mu = jnp.mean(vp, axis=-1, keepdims=True)
    xc = vp - mu
    rstd = lax.rsqrt(jnp.mean(xc * xc, axis=-1, keepdims=True) + LN_EPS)
    xhat = xc * rstd
    v = xhat * lng_ref[...] + lnb_ref[...]
    return zin, cdf, u, xhat, rstd, v


def _sg_masked_ws(ws_ref, g):
    return _mx(jnp.where(_tri(SG_CHUNK, False), ws_ref[g], 0.0))


def _sg_fwd(zpre, b_in, ln_g, ln_b, w_s, b_s_t, name):
    T = zpre.shape[0]
    E, G, C, GW = SG_WIDTH, SG_GROUPS, SG_CHUNK, SG_GROUP_W

    def body(zp_ref, bin_ref, lng_ref, lnb_ref, ws_ref, bst_ref, um_ref):
        _, _, u, _, _, v = _sg_recompute(zp_ref, bin_ref, lng_ref, lnb_ref)
        bst = bst_ref[...]
        for g in range(G):
            cs = slice(g * GW, (g + 1) * GW)
            mixed = _dot(_sg_masked_ws(ws_ref, g), _mx(v[:, cs]), NN) + bst[:, g:g + 1]
            um_ref[:, cs] = _mx(u[:, cs] * mixed)

    return pl.pallas_call(
        body, name=name, grid=(T // C,),
        in_specs=[pl.BlockSpec((C, 2 * E), lambda n: (n, 0)), _full((1, 2 * E)), _full((1, E)), _full((1, E)),
                  _full((G, C, C)), _full((C, LANES))],
        out_specs=pl.BlockSpec((C, E), lambda n: (n, 0)),
        out_shape=jax.ShapeDtypeStruct((T, E), MXU_DTYPE), compiler_params=_cp("parallel"),
    )(zpre, b_in, ln_g, ln_b, w_s, b_s_t)


def _sg_bwd(zpre, b_in, ln_g, ln_b, w_s, b_s_t, dum, name):
    T = zpre.shape[0]
    E, G, C, GW = SG_WIDTH, SG_GROUPS, SG_CHUNK, SG_GROUP_W

    def body(zp_ref, bin_ref, lng_ref, lnb_ref, ws_ref, bst_ref, dum_ref,
             dz_ref, dbin_ref, dlng_ref, dlnb_ref, dws_ref, dbst_ref):
        @pl.when(pl.program_id(0) == 0)
        def _():
            for r in (dbin_ref, dlng_ref, dlnb_ref, dws_ref, dbst_ref):
                r[...] = jnp.zeros_like(r)

        zin, cdf, u, xhat, rstd, v = _sg_recompute(zp_ref, bin_ref, lng_ref, lnb_ref)
        bst = bst_ref[...]
        lane = lax.broadcasted_iota(jnp.int32, (C, LANES), 1)
        dum_v = dum_ref[...]
        dbst = jnp.zeros((C, LANES), F32)
        du_parts, dv_parts = [], []
        for g in range(G):
            cs = slice(g * GW, (g + 1) * GW)
            wsm = _sg_masked_ws(ws_ref, g)
            vg = _mx(v[:, cs])
            mixed = _dot(wsm, vg, NN) + bst[:, g:g + 1]
            dumg = dum_v[:, cs]
            du_parts.append(dumg * mixed)
            dmixed = dumg * u[:, cs]
            dmb = _mx(dmixed)
            dv_parts.append(_dot(wsm, dmb, TN))
            dws_ref[g] += _dot(dmb, vg, NT)
            dbst = jnp.where(lane == g, jnp.sum(dmixed, axis=1, keepdims=True), dbst)
        dbst_ref[...] += dbst
        du = jnp.concatenate(du_parts, axis=1)
        dv = jnp.concatenate(dv_parts, axis=1)
        dlng_ref[...] += jnp.sum(dv * xhat, axis=0, keepdims=True)
        dlnb_ref[...] += jnp.sum(dv, axis=0, keepdims=True)
        dxh = dv * lng_ref[...]
        dvp = rstd * (dxh - jnp.mean(dxh, axis=-1, keepdims=True) - xhat * jnp.mean(dxh * xhat, axis=-1, keepdims=True))
        dzz = jnp.concatenate([du, dvp], axis=1)
        dzin = dzz * (cdf + zin * (_INV_SQRT_2PI * jnp.exp(-0.5 * zin * zin)))
        dz_ref[...] = _mx(dzin)
        dbin_ref[...] += jnp.sum(dzin, axis=0, keepdims=True)

    return pl.pallas_call(
        body, name=name, grid=(T // C,),
        in_specs=[pl.BlockSpec((C, 2 * E), lambda n: (n, 0)), _full((1, 2 * E)), _full((1, E)), _full((1, E)),
                  _full((G, C, C)), _full((C, LANES)), pl.BlockSpec((C, E), lambda n: (n, 0))],
        out_specs=[pl.BlockSpec((C, 2 * E), lambda n: (n, 0)), _full((1, 2 * E)), _full((1, E)), _full((1, E)),
                   _full((G, C, C)), _full((C, LANES))],
        out_shape=[jax.ShapeDtypeStruct((T, 2 * E), MXU_DTYPE), jax.ShapeDtypeStruct((1, 2 * E), F32),
                   jax.ShapeDtypeStruct((1, E), F32), jax.ShapeDtypeStruct((1, E), F32),
                   jax.ShapeDtypeStruct((G, C, C), F32), jax.ShapeDtypeStruct((C, LANES), F32)],
        compiler_params=_cp("arbitrary"),
    )(zpre, b_in, ln_g, ln_b, w_s, b_s_t, dum)


def _row(v):
    return v.reshape(1, -1)


def _pad_lanes(v):
    v = v.reshape(1, -1)
    return jnp.pad(v, ((0, 0), (0, LANES - v.shape[1])))


def _local_step(x, target, p, weights_for, grads_ready=None):
    ng = p["norm_g"]
    grads = {}
    dng = [[None] * 6 for _ in range(2)]
    order = [jnp.zeros((), F32)]

    def tell(group):
        zero = grads_ready(group, grads) if grads_ready is not None else None
        if zero is not None:
            order[0] = zero

    def gain(i, s):
        return _row(ng[i, s]) + order[0]

    def ffn_f(xin, i, j, tag):
        wt = weights_for("ffn" + tag, xin)
        xo, h, gu, y = _ffn_fwd(xin, _row(ng[i, 4 * j]), _row(ng[i, 4 * j + 1]), wt["wgu"], wt["wd"], "ffn_fwd_" + tag)
        return xo, (xin, h, gu, y, wt)

    x1, sv_f00 = ffn_f(x, 0, 0, "00")
    dnw = weights_for("dn", x1)
    hn0 = _norm_fwd(x1, _row(ng[0, 2]), "dn_prenorm")
    proj = _mm(hn0, dnw["dn_wqkvz"], "nn", "dn_proj")
    ba = _mm(hn0, dnw["dn_wba"], "nn", "dn_proj_ba")
    a_log = _pad_lanes(p["dn_a_log"])
    dt_bias = _pad_lanes(p["dn_dt_bias"])
    dn_ng = _row(p["dn_norm_g"])
    qkv = _dn_prep_fwd(proj, p["dn_conv_w"], "dn_prep_fwd")
    beta, gdec = _dn_gate_fwd(ba, a_log, dt_bias, "dn_gate_fwd")
    og, o_raw, tinv, s_all = _dn_scan_fwd(qkv, beta, gdec, proj, dn_ng, "dn_scan_fwd")
    m0 = _mm(og, dnw["dn_wout"], "nn", "dn_out")
    x2 = _postnorm_fwd(x1, m0, _row(ng[0, 3]), "dn_postnorm")
    x3, sv_f01 = ffn_f(x2, 0, 1, "01")
    x4, sv_f10 = ffn_f(x3, 1, 0, "10")
    sgw = weights_for("sg", x4)
    hn1 = _norm_fwd(x4, _row(ng[1, 2]), "sg_prenorm")
    zpre = _mm(hn1, sgw["sg_win"], "nn", "sg_proj")
    sg_bin = _row(p["sg_b_in"])
    sg_lng = _row(p["sg_ln_g"])
    sg_lnb = _row(p["sg_ln_b"])
    sg_bst = jnp.pad(p["sg_b_s"].T, ((0, 0), (0, LANES - SG_GROUPS)))
    um = _sg_fwd(zpre, sg_bin, sg_lng, sg_lnb, p["sg_w_s"], sg_bst, "sg_fwd")
    m1 = _mm(um, sgw["sg_wout"], "nn", "sg_out")
    x5 = _postnorm_fwd(x4, m1, _row(ng[1, 3]), "sg_postnorm")
    x6, sv_f11 = ffn_f(x5, 1, 1, "11")
    loss_part, dx = _loss_fwd_bwd(x6, target, "loss")

    def ffn_b(dxo, sv, i, j, tag):
        xin, h, gu, y, wt = sv
        dxi, dy, a, dgu, dg0, dg1 = _ffn_bwd(dxo, xin, y, gu, gain(i, 4 * j), gain(i, 4 * j + 1), wt["wgu"], wt["wd"],
                                             "ffn_bwd_" + tag)
        grads["wd" + tag] = _mm(a, dy, "tn", "ffn_wgrad_down_" + tag)
        grads["wguT" + tag] = _mm(dgu, h, "tn", "ffn_wgrad_up_" + tag)
        tell("ffn" + tag)
        dng[i][4 * j] = dg0
        dng[i][4 * j + 1] = dg1
        return dxi

    dx = ffn_b(dx, sv_f11, 1, 1, "11")
    dm1, dng[1][3] = _postnorm_bwd(dx, m1, gain(1, 3), "sg_postnorm_bwd")
    grads["sg_w_out"] = _mm(um, dm1, "tn", "sg_wgrad_out")
    dum = _mm(dm1, sgw["sg_wout"], "nt", "sg_dgrad_out")
    dz1, dbin, dlng, dlnb, dws, dbst = _sg_bwd(zpre, sg_bin, sg_lng, sg_lnb, p["sg_w_s"], sg_bst, dum, "sg_bwd")
    grads["sg_w_inT"] = _mm(dz1, hn1, "tn", "sg_wgrad_in")
    tell("sg")
    dh1 = _mm(dz1, sgw["sg_win"], "nt", "sg_dgrad_in")
    dx, dng[1][2] = _prenorm_bwd(dx, dh1, x4, gain(1, 2), "sg_prenorm_bwd")
    grads["sg_b_in"] = dbin.reshape(1, -1)
    grads["sg_ln_g"] = dlng.reshape(1, -1)
    grads["sg_ln_b"] = dlnb.reshape(1, -1)
    grads["sg_w_s"] = jnp.where(jnp.tril(jnp.ones((SG_CHUNK, SG_CHUNK), bool)), dws, 0.0)[None]
    grads["sg_b_s"] = dbst[:, :SG_GROUPS].T[None]
    dx = ffn_b(dx, sv_f10, 1, 0, "10")
    dx = ffn_b(dx, sv_f01, 0, 1, "01")
    dm0, dng[0][3] = _postnorm_bwd(dx, m0, gain(0, 3), "dn_postnorm_bwd")
    grads["dn_w_out"] = _mm(og, dm0, "tn", "dn_wgrad_out")
    dog = _mm(dm0, dnw["dn_wout"], "nt", "dn_dgrad_out")
    dqkv, dbeta, dgdec, dz0, dnng = _dn_scan_bwd(qkv, beta, gdec, proj, dn_ng, o_raw, tinv, s_all, dog, "dn_scan_bwd")
    dqkv_pre, dconv = _dn_prep_bwd(proj, p["dn_conv_w"], dqkv, "dn_prep_bwd")
    dba, dal, ddt = _dn_gate_bwd(ba, a_log, dt_bias, dbeta, dgdec, "dn_gate_bwd")
    W3 = 3 * DN_HEADS * DN_HEAD_DIM
    dw_qkv = _mm(hn0, dqkv_pre, "tn", "dn_wgrad_qkv")
    dw_z = _mm(hn0, dz0, "tn", "dn_wgrad_z")
    dw_ba = _mm(hn0, dba, "tn", "dn_wgrad_ba")
    grads["dn_w_in"] = jnp.concatenate(
        [dw_qkv, dw_z, dw_ba[:, :DN_HEADS], dw_ba[:, LANES:LANES + DN_HEADS]], axis=1)
    tell("dn")
    dh0 = _mm(dqkv_pre, dnw["dn_wqkvz"][:, :W3], "nt", "dn_dgrad_qkv")
    dh0 = _mm(dz0, dnw["dn_wqkvz"][:, W3:], "nt", "dn_dgrad_z", add=dh0)
    dh0 = _mm(dba, dnw["dn_wba"], "nt", "dn_dgrad_ba", add=dh0)
    dx, dng[0][2] = _prenorm_bwd(dx, dh0, x1, gain(0, 2), "dn_prenorm_bwd")
    grads["dn_conv_w"] = dconv[None]
    grads["dn_a_log"] = dal[:, :DN_HEADS]
    grads["dn_dt_bias"] = ddt[:, :DN_HEADS]
    grads["dn_norm_g"] = dnng
    dx = ffn_b(dx, sv_f00, 0, 0, "00")
    grads["norm_g"] = jnp.stack([jnp.concatenate(dng[i], axis=0) for i in range(2)])
    return loss_part, dx, grads


def _mesh_pos():
    return lax.axis_index("x"), lax.axis_index("y"), lax.axis_index("c")


def _other_chips(x, y):
    return [(1 - x, y), (x, 1 - y), (1 - x, 1 - y)]


def _allgather_chips(arrs, name):
    n = len(arrs)

    def body(*refs):
        ins, outs = refs[:n], refs[n:2 * n]
        ici_send, ici_recv, d2d_send, d2d_recv = refs[2 * n:]
        x, y, c = _mesh_pos()
        me = 2 * x + y
        chips = _other_chips(x, y)
        sibling = (x, y, 1 - c)

        def ici(i, j, k):
            cx, cy = chips[j]
            return pltpu.make_async_remote_copy(src_ref=ins[i].at[c], dst_ref=outs[i].at[k, c], send_sem=ici_send.at[3 * i + j],
                                                recv_sem=ici_recv.at[3 * i + j], device_id=(cx, cy, c), device_id_type=MESH)

        def d2d(i, j, h):
            cx, cy = chips[j]
            slot = outs[i].at[2 * cx + cy, h]
            return pltpu.make_async_remote_copy(src_ref=slot, dst_ref=slot, send_sem=d2d_send.at[3 * i + j],
                                                recv_sem=d2d_recv.at[3 * i + j], device_id=sibling, device_id_type=MESH)

        sends = [ici(i, j, me) for i in range(n) for j in range(3)]
        for cp in sends:
            cp.start()
        for i in range(n):
            for j, (cx, cy) in enumerate(chips):
                ici(i, j, 2 * cx + cy).wait_recv()
                fwd = d2d(i, j, c)
                fwd.start()
                sends.append(fwd)
        for i in range(n):
            for j in range(3):
                d2d(i, j, 1 - c).wait_recv()
        for cp in sends:
            cp.wait_send()

    return pl.pallas_call(
        body, name=name, in_specs=[ANY] * n, out_specs=[ANY] * n,
        out_shape=[jax.ShapeDtypeStruct((N_CHIPS,) + a.shape, a.dtype) for a in arrs],
        scratch_shapes=[pltpu.SemaphoreType.DMA((3 * n,))] * 4,
    )(*arrs)


def _swap_halves(arrs, half_first, name):
    n = len(arrs)

    def body(*refs):
        ins, outs = refs[:n], refs[n:2 * n]
        send_sems, recv_sems = refs[2 * n:]
        x, y, c = _mesh_pos()
        cps = [pltpu.make_async_remote_copy(src_ref=ins[i].at[1 - c] if half_first[i] else ins[i].at[:, 1 - c],
                                            dst_ref=outs[i], send_sem=send_sems.at[i], recv_sem=recv_sems.at[i],
                                            device_id=(x, y, 1 - c), device_id_type=MESH)
               for i in range(n)]
        for cp in cps:
            cp.start()
        for cp in cps:
            cp.wait()

    return pl.pallas_call(
        body, name=name, in_specs=[ANY] * n, out_specs=[ANY] * n,
        out_shape=[jax.ShapeDtypeStruct((N_CHIPS,) + a.shape[2:], a.dtype) for a in arrs],
        scratch_shapes=[pltpu.SemaphoreType.DMA((n,)), pltpu.SemaphoreType.DMA((n,))],
    )(*arrs)


HBM = pl.BlockSpec(memory_space=pltpu.HBM)
SEM = pl.BlockSpec(memory_space=pltpu.SEMAPHORE)
TOKEN = jax.ShapeDtypeStruct((SUBLANES, LANES), F32)


def _chip_copies(src_refs, land_refs, send_sems, recv_sems, slice_by_chip, receiving):
    x, y, c = _mesh_pos()
    me = 2 * x + y
    cps = []
    for i, (src, land) in enumerate(zip(src_refs, land_refs)):
        for j, (cx, cy) in enumerate(_other_chips(x, y)):
            peer = 2 * cx + cy
            s = src.at[me if receiving else peer] if slice_by_chip else src
            cps.append(pltpu.make_async_remote_copy(
                src_ref=s, dst_ref=land.at[peer if receiving else me], send_sem=send_sems.at[3 * i + j],
                recv_sem=recv_sems.at[3 * i + j], device_id=(cx, cy, c), device_id_type=MESH))
    return cps


def _chips_start(srcs, slice_by_chip, after, name):
    n = len(srcs)
    lands = [lax.empty((N_CHIPS,) + (s.shape[1:] if slice_by_chip else s.shape), s.dtype) for s in srcs]

    def body(*refs):
        src_refs, land_refs = refs[:n], refs[n:2 * n]
        send_sems, recv_sems = refs[2 * n + 1], refs[2 * n + 2]
        token = refs[-1]
        for cp in _chip_copies(src_refs, land_refs, send_sems, recv_sems, slice_by_chip, False):
            cp.start()
        token[...] = jnp.zeros_like(token)

    outs = pl.pallas_call(
        body, name=name,
        in_specs=[HBM] * (2 * n) + [ANY],
        out_specs=(SEM, SEM) + (HBM,) * (2 * n) + (pl.BlockSpec(memory_space=pltpu.VMEM),),
        out_shape=(pltpu.SemaphoreType.DMA((3 * n,)), pltpu.SemaphoreType.DMA((3 * n,)))
        + tuple(pltpu.HBM(a.shape, a.dtype) for a in list(srcs) + lands) + (TOKEN,),
        input_output_aliases={i: 2 + i for i in range(2 * n)},
        compiler_params=pltpu.CompilerParams(has_side_effects=pltpu.SideEffectType.DATAFLOW_SIDE_EFFECTING),
    )(*[pltpu.with_memory_space_constraint(a, pltpu.HBM) for a in list(srcs) + lands], after)
    return dict(sems=outs[:2], srcs=outs[2:2 + n], lands=outs[2 + n:2 + 2 * n], token=outs[-1], slice_by_chip=slice_by_chip)


def _chips_wait(started, after, name):
    n = len(started["srcs"])
    slice_by_chip = started["slice_by_chip"]

    def body(*refs):
        src_refs, land_refs = refs[:n], refs[n:2 * n]
        send_sems, recv_sems = refs[2 * n], refs[2 * n + 1]
        for cp in _chip_copies(src_refs, land_refs, send_sems, recv_sems, slice_by_chip, True):
            cp.wait_send()
            cp.wait_recv()

    outs = pl.pallas_call(
        body, name=name,
        in_specs=[HBM] * (2 * n) + [SEM, SEM, ANY],
        out_specs=(HBM,) * (2 * n),
        out_shape=tuple(pltpu.HBM(a.shape, a.dtype) for a in list(started["srcs"]) + list(started["lands"])),
        input_output_aliases={i: i for i in range(2 * n)},
        compiler_params=pltpu.CompilerParams(has_side_effects=pltpu.SideEffectType.DATAFLOW_SIDE_EFFECTING),
    )(*started["srcs"], *started["lands"], *started["sems"], after)
    return outs[:n], outs[n:]


def _swap_whole(arrs, name):
    n = len(arrs)

    def body(*refs):
        ins, outs = refs[:n], refs[n:2 * n]
        send_sems, recv_sems = refs[2 * n:]
        x, y, c = _mesh_pos()
        cps = [pltpu.make_async_remote_copy(src_ref=ins[i], dst_ref=outs[i], send_sem=send_sems.at[i],
                                            recv_sem=recv_sems.at[i], device_id=(x, y, 1 - c), device_id_type=MESH)
               for i in range(n)]
        for cp in cps:
            cp.start()
        for cp in cps:
            cp.wait()

    return pl.pallas_call(
        body, name=name, in_specs=[ANY] * n, out_specs=[ANY] * n,
        out_shape=[jax.ShapeDtypeStruct(a.shape, a.dtype) for a in arrs],
        scratch_shapes=[pltpu.SemaphoreType.DMA((n,)), pltpu.SemaphoreType.DMA((n,))],
    )(*arrs)


def _allgather_devices(a, name):
    masks = [(mx, my, mc) for mx in (0, 1) for my in (0, 1) for mc in (0, 1)][1:]

    def body(in_ref, out_ref, send_sems, recv_sems, loc_sem):
        x, y, c = _mesh_pos()
        me = 4 * x + 2 * y + c
        lc = pltpu.make_async_copy(in_ref, out_ref.at[me], loc_sem.at[0])
        lc.start()
        peers = [(jnp.where(mx, 1 - x, x), jnp.where(my, 1 - y, y), jnp.where(mc, 1 - c, c)) for mx, my, mc in masks]
        cps = [pltpu.make_async_remote_copy(src_ref=in_ref, dst_ref=out_ref.at[me], send_sem=send_sems.at[j],
                                            recv_sem=recv_sems.at[j], device_id=peers[j], device_id_type=MESH)
               for j in range(len(masks))]
        for cp in cps:
            cp.start()
        for j, (px, py, pc) in enumerate(peers):
            pltpu.make_async_remote_copy(src_ref=in_ref, dst_ref=out_ref.at[4 * px + 2 * py + pc], send_sem=send_sems.at[j],
                                         recv_sem=recv_sems.at[j], device_id=peers[j], device_id_type=MESH).wait_recv()
        for cp in cps:
            cp.wait_send()
        lc.wait()

    return pl.pallas_call(
        body, name=name, in_specs=[ANY], out_specs=ANY,
        out_shape=jax.ShapeDtypeStruct((N_DEV,) + a.shape, a.dtype),
        scratch_shapes=[pltpu.SemaphoreType.DMA((N_DEV - 1,)), pltpu.SemaphoreType.DMA((N_DEV - 1,)),
                        pltpu.SemaphoreType.DMA((1,))],
    )(a)


def _as_rows(a, lead):
    shp = a.shape
    rows = 1
    for s in shp[lead:-1]:
        rows *= s
    return a.reshape(shp[:lead] + (rows, shp[-1]))


def _row_tile(rows, cols, n_bufs):
    budget = (24 * 1024 * 1024) // (n_bufs * 2 * 4 * cols)
    return _pick(rows, max(2 * SUBLANES, budget), 2 * SUBLANES)


def _sum_leading(a, name):
    n = a.shape[0]
    v = _as_rows(a, 1)
    _, rows, cols = v.shape
    tr = _row_tile(rows, cols, n + 1)

    def body(a_ref, o_ref):
        acc = a_ref[0]
        for k in range(1, n):
            acc = acc + a_ref[k]
        o_ref[...] = acc

    out = pl.pallas_call(body, name=name, grid=(rows // tr,),
                         in_specs=[pl.BlockSpec((n, tr, cols), lambda i: (0, i, 0))],
                         out_specs=pl.BlockSpec((tr, cols), lambda i: (i, 0)),
                         out_shape=jax.ShapeDtypeStruct((rows, cols), F32), compiler_params=_cp("parallel"))(v)
    return out.reshape(a.shape[1:])


def _scalar(i):
    return jnp.reshape(i, (1,)).astype(jnp.int32)


def _add_own_half(g, other, c, half_first, name):
    _, rows, cols = other.shape
    tr = _row_tile(rows, cols, 3)

    def body(c_ref, g_ref, o_ref, out_ref):
        out_ref[0] = (g_ref[0, 0] + o_ref[0]).astype(out_ref.dtype)

    if half_first:
        g_map = lambda k, i, c_ref: (c_ref[0], k, i, 0)
    else:
        g_map = lambda k, i, c_ref: (k, c_ref[0], i, 0)
    flat = pl.BlockSpec((1, tr, cols), lambda k, i, c_ref: (k, i, 0))
    return pl.pallas_call(
        body, name=name,
        grid_spec=pltpu.PrefetchScalarGridSpec(
            num_scalar_prefetch=1, grid=(N_CHIPS, rows // tr),
            in_specs=[pl.BlockSpec((1, 1, tr, cols), g_map), flat], out_specs=flat),
        out_shape=jax.ShapeDtypeStruct(other.shape, COMM_DTYPE), compiler_params=_cp("parallel", "parallel"),
    )(_scalar(c), g, other)


def _sum_chips(own, got, chip, name):
    pv = _as_rows(own, 1)
    bv = _as_rows(got, 1)
    _, rows, cols = pv.shape
    tr = _row_tile(rows, cols, N_CHIPS + 2)

    def body(chip_ref, p_ref, b_ref, o_ref):
        mine = p_ref[0].astype(F32)
        acc = jnp.where(chip_ref[0] == 0, mine, b_ref[0].astype(F32))
        for k in range(1, N_CHIPS):
            acc = acc + jnp.where(chip_ref[0] == k, mine, b_ref[k].astype(F32))
        o_ref[...] = acc

    out = pl.pallas_call(
        body, name=name,
        grid_spec=pltpu.PrefetchScalarGridSpec(
            num_scalar_prefetch=1, grid=(rows // tr,),
            in_specs=[pl.BlockSpec((1, tr, cols), lambda i, k_ref: (k_ref[0], i, 0)),
                      pl.BlockSpec((N_CHIPS, tr, cols), lambda i, k_ref: (0, i, 0))],
            out_specs=pl.BlockSpec((tr, cols), lambda i, k_ref: (i, 0))),
        out_shape=jax.ShapeDtypeStruct((rows, cols), F32), compiler_params=_cp("parallel"),
    )(_scalar(chip), pv, bv)
    return out.reshape(own.shape[1:])


def _adam_math(w, g, m, v):
    nm = ADAM_B1 * m + (1.0 - ADAM_B1) * g
    nv = ADAM_B2 * v + (1.0 - ADAM_B2) * (g * g)
    m_hat = nm / (1.0 - ADAM_B1 ** ADAM_STEP)
    v_hat = nv / (1.0 - ADAM_B2 ** ADAM_STEP)
    return -ADAM_LR * (m_hat / (jnp.sqrt(v_hat) + ADAM_EPS) + ADAM_WD * w), nm, nv


def _adamw_halves(w, mine, theirs, m, v, c, name):
    shape = w.shape
    ws, ms, vs = (_as_rows(t.reshape((2, -1) + t.shape[-1:]), 1) for t in (w, m, v))
    a, b = _as_rows(mine, 0), _as_rows(theirs, 0)
    rows, cols = a.shape
    tr = _row_tile(rows, cols, 9)

    def body(c_ref, w_ref, a_ref, b_ref, m_ref, v_ref, g_ref, d_ref, nm_ref, nv_ref):
        gv = jnp.where(pl.program_id(0) == c_ref[0], a_ref[...], b_ref[...])
        g_ref[0] = gv
        d_ref[0], nm_ref[0], nv_ref[0] = _adam_math(w_ref[0], gv, m_ref[0], v_ref[0])

    half = pl.BlockSpec((1, tr, cols), lambda h, i, c_ref: (h, i, 0))
    flat = pl.BlockSpec((tr, cols), lambda h, i, c_ref: (i, 0))
    outs = pl.pallas_call(
        body, name=name,
        grid_spec=pltpu.PrefetchScalarGridSpec(num_scalar_prefetch=1, grid=(2, rows // tr),
                                               in_specs=[half, flat, flat, half, half], out_specs=[half] * 4),
        out_shape=[jax.ShapeDtypeStruct((2, rows, cols), F32)] * 4, compiler_params=_cp("parallel", "parallel"),
    )(_scalar(c), ws, a, b, ms, vs)
    return tuple(o.reshape(shape) for o in outs)


def _adamw(w, g, m, v, name):
    shape = w.shape
    ws, gs, ms, vs = (_as_rows(t, 0) for t in (w, g, m, v))
    rows, cols = ws.shape
    tr = _row_tile(rows, cols, 7)

    def body(w_ref, g_ref, m_ref, v_ref, d_ref, nm_ref, nv_ref):
        d_ref[...], nm_ref[...], nv_ref[...] = _adam_math(w_ref[...], g_ref[...], m_ref[...], v_ref[...])

    spec = pl.BlockSpec((tr, cols), lambda i: (i, 0))
    outs = pl.pallas_call(body, name=name, grid=(rows // tr,), in_specs=[spec] * 4, out_specs=[spec] * 3,
                          out_shape=[jax.ShapeDtypeStruct((rows, cols), F32)] * 3, compiler_params=_cp("parallel"))(ws, gs, ms, vs)
    return tuple(o.reshape(shape) for o in outs)


_BIG = ["ffn_w_gate", "ffn_w_up", "ffn_w_down", "dn_w_in", "dn_w_out", "sg_w_in", "sg_w_out"]
_SMALL_SHARDED = ["norm_g", "dn_conv_w", "sg_b_in", "sg_ln_g", "sg_ln_b"]
_SMALL_REPL = ["dn_a_log", "dn_dt_bias", "dn_norm_g", "sg_w_s", "sg_b_s"]
_WEIGHTS = ["norm_g", "ffn_w_gate", "ffn_w_up", "ffn_w_down", "dn_w_in", "dn_conv_w", "dn_a_log", "dn_dt_bias",
            "dn_norm_g", "dn_w_out", "sg_w_in", "sg_b_in", "sg_ln_g", "sg_ln_b", "sg_w_s", "sg_b_s", "sg_w_out"]
PACK_COLS = 1024


def _pack(arrs):
    flat = jnp.concatenate([a.reshape(-1) for a in arrs])
    pad = (-flat.shape[0]) % (SUBLANES * PACK_COLS)
    return jnp.pad(flat, (0, pad)).reshape(-1, PACK_COLS)


def _unpack(buf, shapes):
    flat = buf.reshape(-1)
    out, off = [], 0
    for s in shapes:
        n = math.prod(s)
        out.append(flat[off:off + n].reshape(s))
        off += n
    return out


def _as_halves(a):
    if a.shape[0] == 2:
        return a
    if a.shape[0] == 1:
        return a.reshape((2, a.shape[1] // 2) + a.shape[2:])
    return a.reshape((2, a.shape[0] // 2) + a.shape[1:])


def _with_own(gathered, own, chip):
    g = gathered.reshape((N_CHIPS,) + own.shape)
    return [jnp.where(chip == k, own, g[k]) for k in range(N_CHIPS)]


def _cat_shards(g, axis):
    return jnp.concatenate(list(g), axis=axis)


_GROUP_ORDER = ["ffn00", "dn", "ffn01", "ffn10", "sg", "ffn11"]


def _weight_groups(w):
    cast = {k: _mx(w[k]) for k in _BIG}
    groups = {"ffn%d%d" % (i, j): [cast["ffn_w_gate"][i, j], cast["ffn_w_up"][i, j], cast["ffn_w_down"][i, j]]
              for i, j in [(0, 0), (0, 1), (1, 0), (1, 1)]}
    groups["dn"] = [cast["dn_w_in"][0], cast["dn_w_out"][0]]
    groups["sg"] = [cast["sg_w_in"][0], cast["sg_w_out"][0]]
    return groups


def _group_matrices(group, shards):
    if group.startswith("ffn"):
        gate, up, down = shards
        return {"wgu": jnp.concatenate([_cat_shards(gate, 1), _cat_shards(up, 1)], axis=1), "wd": _cat_shards(down, 0)}
    if group == "sg":
        return {"sg_win": _cat_shards(shards[0], 1), "sg_wout": _cat_shards(shards[1], 0)}
    dn_full = _cat_shards(shards[0], 1)
    W4 = 4 * DN_HEADS * DN_HEAD_DIM
    wba = jnp.zeros((D_MODEL, 2 * LANES), dn_full.dtype)
    wba = wba.at[:, :DN_HEADS].set(dn_full[:, W4:W4 + DN_HEADS])
    wba = wba.at[:, LANES:LANES + DN_HEADS].set(dn_full[:, W4 + DN_HEADS:])
    return {"dn_wqkvz": dn_full[:, :W4], "dn_wba": wba, "dn_wout": _cat_shards(shards[1], 0)}


def _split_cols(a, n):
    w = a.shape[-1] // n
    return [a[..., k * w:(k + 1) * w] for k in range(n)]


def _split_rows(a, n):
    h = a.shape[-2] // n
    return [a[..., k * h:(k + 1) * h, :] for k in range(n)]


_IJ = [(0, 0), (0, 1), (1, 0), (1, 1)]


_REDUCED = ["wguT%d%d" % ij for ij in _IJ] + ["wd%d%d" % ij for ij in _IJ] + ["dn_w_in", "dn_w_out", "sg_w_inT", "sg_w_out"]


def _group_grads(group, grads):
    def rows_by_chip(a):
        return a.reshape(N_CHIPS, 2, a.shape[0] // (2 * N_CHIPS), a.shape[1])

    if group.startswith("ffn"):
        tag = group[3:]
        t = grads["wguT" + tag]
        return (["wguT" + tag, "wd" + tag],
                [t.reshape(2, N_CHIPS, t.shape[0] // (2 * N_CHIPS), t.shape[1]), rows_by_chip(grads["wd" + tag])], [True, False])
    if group == "sg":
        return ["sg_w_inT", "sg_w_out"], [rows_by_chip(grads["sg_w_inT"]), rows_by_chip(grads["sg_w_out"])], [False, False]
    dn_in = jnp.stack([jnp.stack(_split_cols(hf, N_CHIPS)) for hf in _split_rows(grads["dn_w_in"], 2)])
    return ["dn_w_in", "dn_w_out"], [dn_in, rows_by_chip(grads["dn_w_out"])], [True, False]


def _shard_grads(mine, theirs, c, w):
    lo = [jnp.where(c == 0, a, b) for a, b in zip(mine, theirs)]
    hi = [jnp.where(c == 0, b, a) for a, b in zip(mine, theirs)]
    rows = lambda t: jnp.concatenate([lo[t], hi[t]], axis=0)
    sq = lambda parts: jnp.stack(parts).reshape(2, 2, *parts[0].shape)
    g = {}
    g["ffn_w_gate"] = sq([lo[t].T for t in range(4)])
    g["ffn_w_up"] = sq([hi[t].T for t in range(4)])
    g["ffn_w_down"] = sq([rows(4 + t) for t in range(4)])
    g["dn_w_in"] = rows(8)[None]
    g["dn_w_out"] = rows(9)[None]
    g["sg_w_in"] = rows(10).T[None]
    g["sg_w_out"] = rows(11)[None]
    return {k: v.reshape(w[k].shape) for k, v in g.items()}


def kernel(x, norm_g, ffn_w_gate, ffn_w_up, ffn_w_down, dn_w_in, dn_conv_w, dn_a_log, dn_dt_bias, dn_norm_g, dn_w_out, sg_w_in, sg_b_in, sg_ln_g, sg_ln_b, sg_w_s, sg_b_s, sg_w_out, loss_target, m_norm_g, m_ffn_w_gate, m_ffn_w_up, m_ffn_w_down, m_dn_w_in, m_dn_conv_w, m_dn_a_log, m_dn_dt_bias, m_dn_norm_g, m_dn_w_out, m_sg_w_in, m_sg_b_in, m_sg_ln_g, m_sg_ln_b, m_sg_w_s, m_sg_b_s, m_sg_w_out, v_norm_g, v_ffn_w_gate, v_ffn_w_up, v_ffn_w_down, v_dn_w_in, v_dn_conv_w, v_dn_a_log, v_dn_dt_bias, v_dn_norm_g, v_dn_w_out, v_sg_w_in, v_sg_b_in, v_sg_ln_g, v_sg_ln_b, v_sg_w_s, v_sg_b_s, v_sg_w_out):
    args = dict(locals())
    w = {k: args[k] for k in _WEIGHTS}
    mom = {k: args["m_" + k] for k in _WEIGHTS}
    var = {k: args["v_" + k] for k in _WEIGHTS}
    cx, cy, cc = _mesh_pos()
    chip = 2 * cx + cy

    small_shapes = [w[k].shape for k in _SMALL_SHARDED]
    groups = _weight_groups(w)
    own = groups[_GROUP_ORDER[0]] + [_pack([w[k] for k in _SMALL_SHARDED])]
    first = _allgather_chips([_as_halves(a) for a in own], "gather_first")
    started, after = {}, first[0]
    for g in _GROUP_ORDER[1:]:
        started[g] = _chips_start(groups[g], False, after, "gather_start_" + g)
        after = started[g]["token"]
    first = [_with_own(g, a, chip) for g, a in zip(first, own)]
    small_k = [_unpack(first[-1][k], small_shapes) for k in range(N_CHIPS)]
    p = {name: jnp.concatenate([small_k[k][i] for k in range(N_CHIPS)], axis=-1) for i, name in enumerate(_SMALL_SHARDED)}
    p = {k: (v if k == "norm_g" else v[0]) for k, v in p.items()}
    p["norm_g"] = p["norm_g"] + after[0, 0]
    for k in _SMALL_REPL:
        p[k] = w[k][0]

    def weights_for(group, after):
        if group == _GROUP_ORDER[0]:
            return _group_matrices(group, first[:-1])
        srcs, lands = _chips_wait(started[group], after, "gather_wait_" + group)
        return _group_matrices(group, [_with_own(l, a, chip) for l, a in zip(lands, srcs)])

    mine, theirs, pending = {}, {}, []

    def finish(group, names, scatter, after):
        pair_sum, got = _chips_wait(scatter, after, "reduce_wait_" + group)
        half_sum = [_sum_chips(a, b, chip, "chip_sum_" + n) for n, a, b in zip(names, pair_sum, got)]
        other = _swap_whole(half_sum, "gather_core_pair_" + group)
        mine.update(zip(names, half_sum))
        theirs.update(zip(names, other))

    def grads_ready(group, grads):
        names, halves, half_first = _group_grads(group, grads)
        from_sibling = _swap_halves(halves, half_first, "reduce_core_pair_" + group)
        pair_sum = [_add_own_half(h, o, cc, hf, "pair_sum_" + n) for n, h, o, hf in zip(names, halves, from_sibling, half_first)]
        scatter = _chips_start(pair_sum, True, from_sibling[0], "reduce_start_" + group)
        if pending:
            finish(*pending.pop(), scatter["token"])
        pending.append((group, names, scatter))
        return scatter["token"][0, 0]

    loss_part, grad_x, grads = _local_step(x[0], loss_target[0], p, weights_for, grads_ready)
    last = pending.pop()
    finish(*last, last[2]["token"])
    half_sum = [mine[n] for n in _REDUCED]
    other_half = [theirs[n] for n in _REDUCED]

    small_names = _SMALL_SHARDED + _SMALL_REPL
    small_grads = [grads[k] for k in small_names]
    full_shapes = [g.shape for g in small_grads] + [(1,)]
    pack = _pack(small_grads + [loss_part[0, :1]])
    summed = _sum_leading(_allgather_devices(pack, "gather_small"), "small_sum")
    parts = _unpack(summed, full_shapes)
    loss = parts[-1][0]
    small_grad = {}
    for i, k in enumerate(small_names):
        g = parts[i]
        if k in _SMALL_SHARDED:
            n = w[k].shape[-1]
            g = lax.dynamic_slice_in_dim(g, chip * n, n, axis=g.ndim - 1)
        small_grad[k] = g

    grad = {**small_grad, **_shard_grads(half_sum, other_half, cc, w)}
    delta, new_m, new_v = {}, {}, {}
    for k in _BIG:
        delta[k], new_m[k], new_v[k] = _adamw(w[k], grad[k], mom[k], var[k], "adamw_" + k)
    shapes = [w[k].shape for k in small_names]
    d, nm, nv = _adamw(_pack([w[k] for k in small_names]), _pack([grad[k] for k in small_names]),
                       _pack([mom[k] for k in small_names]), _pack([var[k] for k in small_names]), "adamw_small")
    for k, a, b, c_ in zip(small_names, _unpack(d, shapes), _unpack(nm, shapes), _unpack(nv, shapes)):
        delta[k], new_m[k], new_v[k] = a, b, c_

    return (loss, grad_x[None], *[grad[k] for k in _WEIGHTS], *[delta[k] for k in _WEIGHTS],
            *[new_m[k] for k in _WEIGHTS], *[new_v[k] for k in _WEIGHTS])
```

```python
import functools
import math

import jax
import jax.numpy as jnp
from jax import lax
from jax.experimental import pallas as pl
from jax.experimental.pallas import tpu as pltpu

F32 = jnp.float32
MXU_DTYPE = jnp.bfloat16
COMM_DTYPE = jnp.bfloat16
HI = lax.Precision.HIGHEST
TRI_PREC = lax.Precision.HIGH

D_MODEL = 1024
D_FF = 2816
RMS_EPS = 1e-6
LN_EPS = 1e-5
L2_EPS = 1e-6
DN_HEADS = 8
DN_HEAD_DIM = 128
DN_CONV = 4
DN_CHUNK = 64
SG_WIDTH = 2048
SG_GROUPS = 8
SG_CHUNK = 128
SG_GROUP_W = SG_WIDTH // SG_GROUPS
N_CHIPS = 4
N_DEV = 8
LANES = 128
SUBLANES = 8
VMEM_LIMIT = 56 * 1024 * 1024

ADAM_LR = 0.001
ADAM_B1 = 0.9
ADAM_B2 = 0.999
ADAM_EPS = 1e-08
ADAM_WD = 0.01
ADAM_STEP = 10

MESH = pl.DeviceIdType.MESH
ANY = pl.BlockSpec(memory_space=pl.ANY)


def _cp(*sem):
    return pltpu.CompilerParams(dimension_semantics=sem, vmem_limit_bytes=VMEM_LIMIT)


def _pick(n, pref, mult=LANES):
    best = None
    d = mult
    while d <= min(n, pref):
        if n % d == 0:
            best = d
        d += mult
    return best if best is not None else n


def _full(shape):
    nd = len(shape)
    return pl.BlockSpec(shape, lambda *_: (0,) * nd)


def _sigmoid(x):
    return 1.0 / (1.0 + jnp.exp(-x))


def _dot(a, b, dims, prec=None):
    return lax.dot_general(a, b, (dims, ((), ())), preferred_element_type=F32, precision=prec)


NN = ((1,), (0,))
NT = ((1,), (1,))
TN = ((0,), (0,))


def _mx(a):
    return a.astype(MXU_DTYPE)


def _rms_stat(x):
    return lax.rsqrt(jnp.mean(x * x, axis=-1, keepdims=True) + RMS_EPS)


def _rms_bwd(x, r, g, dy):
    xh = x * r
    dxh = dy * g
    dx = r * (dxh - xh * jnp.mean(dxh * xh, axis=-1, keepdims=True))
    return dx, jnp.sum(dy * xh, axis=0, keepdims=True)


def _mm(a, b, mode, name, out_dtype=F32, add=None):
    if mode == "tn":
        K, M = a.shape
        N = b.shape[1]
    elif mode == "nt":
        M, K = a.shape
        N = b.shape[0]
    else:
        M, K = a.shape
        N = b.shape[1]
    tn = _pick(N, 1024)
    if mode == "tn":
        tm = _pick(M, 1024 if tn <= 512 else 1408)
        tk = _pick(K, 1024, SUBLANES)
    else:
        tm = _pick(M, max(512, min(2048, (512 * 1024) // tn)), SUBLANES)
        tk = _pick(K, 2048)
    nk = K // tk
    grid = (N // tn, M // tm, nk)
    if mode == "nn":
        a_spec = pl.BlockSpec((tm, tk), lambda j, i, k: (i, k))
        b_spec = pl.BlockSpec((tk, tn), lambda j, i, k: (k, j))
        dims = NN
    elif mode == "nt":
        a_spec = pl.BlockSpec((tm, tk), lambda j, i, k: (i, k))
        b_spec = pl.BlockSpec((tn, tk), lambda j, i, k: (j, k))
        dims = NT
    else:
        a_spec = pl.BlockSpec((tk, tm), lambda j, i, k: (k, i))
        b_spec = pl.BlockSpec((tk, tn), lambda j, i, k: (k, j))
        dims = TN
    o_spec = pl.BlockSpec((tm, tn), lambda j, i, k: (i, j))
    has_add = add is not None

    def body(*refs):
        if has_add:
            a_ref, b_ref, add_ref, o_ref, acc = refs
        else:
            a_ref, b_ref, o_ref, acc = refs
        k = pl.program_id(2)

        @pl.when(k == 0)
        def _():
            acc[...] = add_ref[...] if has_add else jnp.zeros_like(acc)

        acc[...] += _dot(a_ref[...], b_ref[...], dims)

        @pl.when(k == nk - 1)
        def _():
            o_ref[...] = acc[...].astype(o_ref.dtype)

    ins = [a, b] + ([add] if has_add else [])
    specs = [a_spec, b_spec] + ([o_spec] if has_add else [])
    return pl.pallas_call(
        body, name=name, grid=grid, in_specs=specs, out_specs=o_spec,
        out_shape=jax.ShapeDtypeStruct((M, N), out_dtype),
        scratch_shapes=[pltpu.VMEM((tm, tn), F32)],
        compiler_params=_cp("parallel", "parallel", "arbitrary"),
    )(*ins)


def _load_resident(pairs, sem):
    @pl.when(pl.program_id(0) == 0)
    def _():
        cps = [pltpu.make_async_copy(src, dst, sem.at[i]) for i, (src, dst) in enumerate(pairs)]
        for c in cps:
            c.start()
        for c in cps:
            c.wait()


def _ffn_fwd(x, g0, g1, wgu, wd, name):
    T, D = x.shape
    F2 = wgu.shape[1]
    F = F2 // 2
    tm = _pick(T, 256, SUBLANES)

    def body(x_ref, g0_ref, g1_ref, wgu_hbm, wd_hbm, xo_ref, h_ref, gu_ref, y_ref, wgu_v, wd_v, sem):
        _load_resident([(wgu_hbm, wgu_v), (wd_hbm, wd_v)], sem)
        xv = x_ref[...]
        hb = _mx(xv * _rms_stat(xv) * g0_ref[...])
        h_ref[...] = hb
        gu = _dot(hb, wgu_v[...], NN)
        gu_ref[...] = gu.astype(gu_ref.dtype)
        g = gu[:, :F]
        u = gu[:, F:]
        a = _mx(g * _sigmoid(g) * u)
        y = _dot(a, wd_v[...], NN)
        y_ref[...] = y
        xo_ref[...] = xv + 0.5 * (y * _rms_stat(y) * g1_ref[...])

    row = lambda w: pl.BlockSpec((tm, w), lambda i: (i, 0))
    return pl.pallas_call(
        body, name=name, grid=(T // tm,),
        in_specs=[row(D), _full((1, D)), _full((1, D)), ANY, ANY],
        out_specs=[row(D), row(D), row(F2), row(D)],
        out_shape=[jax.ShapeDtypeStruct((T, D), F32), jax.ShapeDtypeStruct((T, D), MXU_DTYPE),
                   jax.ShapeDtypeStruct((T, F2), MXU_DTYPE), jax.ShapeDtypeStruct((T, D), F32)],
        scratch_shapes=[pltpu.VMEM(wgu.shape, wgu.dtype), pltpu.VMEM(wd.shape, wd.dtype),
                        pltpu.SemaphoreType.DMA((2,))],
        compiler_params=_cp("arbitrary"),
    )(x, g0, g1, wgu, wd)


FFN_BWD_CHUNK = 1408


def _ffn_bwd(dxo, x, y, gu, g0, g1, wgu, wd, name):
    T, D = x.shape
    F2 = gu.shape[1]
    F = F2 // 2
    tm = _pick(T, 256, SUBLANES)
    fc = _pick(F, FFN_BWD_CHUNK)

    def body(dxo_ref, x_ref, y_ref, gu_ref, g0_ref, g1_ref, wgu_hbm, wd_hbm,
             dx_ref, dy_ref, a_ref, dgu_ref, dg0_ref, dg1_ref, wgu_v, wd_v, sem):
        _load_resident([(wgu_hbm, wgu_v), (wd_hbm, wd_v)], sem)

        @pl.when(pl.program_id(0) == 0)
        def _():
            dg0_ref[...] = jnp.zeros_like(dg0_ref)
            dg1_ref[...] = jnp.zeros_like(dg1_ref)

        dxo_v = dxo_ref[...]
        yv = y_ref[...]
        dy, dg1 = _rms_bwd(yv, _rms_stat(yv), g1_ref[...], 0.5 * dxo_v)
        dg1_ref[...] += dg1
        dyb = _mx(dy)
        dy_ref[...] = dyb
        dh = jnp.zeros((tm, D), F32)
        for c in range(F // fc):
            lo, hi = c * fc, (c + 1) * fc
            da = _dot(dyb, wd_v[lo:hi, :], NT)
            g = gu_ref[:, lo:hi].astype(F32)
            u = gu_ref[:, F + lo:F + hi].astype(F32)
            s = _sigmoid(g)
            sg = g * s
            a_ref[:, lo:hi] = _mx(sg * u)
            dg = _mx(da * u * (s * (1.0 + g * (1.0 - s))))
            du = _mx(da * sg)
            dgu_ref[:, lo:hi] = dg
            dgu_ref[:, F + lo:F + hi] = du
            dh = dh + _dot(dg, wgu_v[:, lo:hi], NT) + _dot(du, wgu_v[:, F + lo:F + hi], NT)
        xv = x_ref[...]
        dx, dg0 = _rms_bwd(xv, _rms_stat(xv), g0_ref[...], dh)
        dg0_ref[...] += dg0
        dx_ref[...] = dxo_v + dx

    row = lambda w: pl.BlockSpec((tm, w), lambda i: (i, 0))
    one = _full((1, D))
    return pl.pallas_call(
        body, name=name, grid=(T // tm,),
        in_specs=[row(D), row(D), row(D), row(F2), one, one, ANY, ANY],
        out_specs=[row(D), row(D), row(F), row(F2), one, one],
        out_shape=[jax.ShapeDtypeStruct((T, D), F32), jax.ShapeDtypeStruct((T, D), MXU_DTYPE),
                   jax.ShapeDtypeStruct((T, F), MXU_DTYPE), jax.ShapeDtypeStruct((T, F2), MXU_DTYPE),
                   jax.ShapeDtypeStruct((1, D), F32), jax.ShapeDtypeStruct((1, D), F32)],
        scratch_shapes=[pltpu.VMEM(wgu.shape, wgu.dtype), pltpu.VMEM(wd.shape, wd.dtype), pltpu.SemaphoreType.DMA((2,))],
        compiler_params=_cp("arbitrary"),
    )(dxo, x, y, gu, g0, g1, wgu, wd)


def _norm_fwd(x, g, name):
    T, D = x.shape
    tm = _pick(T, 512, SUBLANES)

    def body(x_ref, g_ref, h_ref):
        xv = x_ref[...]
        h_ref[...] = _mx(xv * _rms_stat(xv) * g_ref[...])

    row = pl.BlockSpec((tm, D), lambda i: (i, 0))
    return pl.pallas_call(body, name=name, grid=(T // tm,), in_specs=[row, _full((1, D))], out_specs=row,
                          out_shape=jax.ShapeDtypeStruct((T, D), MXU_DTYPE), compiler_params=_cp("parallel"))(x, g)


def _postnorm_fwd(x, m, g, name):
    T, D = x.shape
    tm = _pick(T, 512, SUBLANES)

    def body(x_ref, m_ref, g_ref, o_ref):
        mv = m_ref[...]
        o_ref[...] = x_ref[...] + mv * _rms_stat(mv) * g_ref[...]

    row = pl.BlockSpec((tm, D), lambda i: (i, 0))
    return pl.pallas_call(body, name=name, grid=(T // tm,), in_specs=[row, row, _full((1, D))], out_specs=row,
                          out_shape=jax.ShapeDtypeStruct((T, D), F32), compiler_params=_cp("parallel"))(x, m, g)


def _postnorm_bwd(dxo, m, g, name):
    T, D = m.shape
    tm = _pick(T, 512, SUBLANES)

    def body(dxo_ref, m_ref, g_ref, dm_ref, dg_ref):
        @pl.when(pl.program_id(0) == 0)
        def _():
            dg_ref[...] = jnp.zeros_like(dg_ref)

        mv = m_ref[...]
        dm, dg = _rms_bwd(mv, _rms_stat(mv), g_ref[...], dxo_ref[...])
        dg_ref[...] += dg
        dm_ref[...] = _mx(dm)

    row = pl.BlockSpec((tm, D), lambda i: (i, 0))
    return pl.pallas_call(body, name=name, grid=(T // tm,), in_specs=[row, row, _full((1, D))],
                          out_specs=[row, _full((1, D))],
                          out_shape=[jax.ShapeDtypeStruct((T, D), MXU_DTYPE), jax.ShapeDtypeStruct((1, D), F32)],
                          compiler_params=_cp("arbitrary"))(dxo, m, g)


def _prenorm_bwd(dxo, dh, x, g, name):
    T, D = x.shape
    tm = _pick(T, 512, SUBLANES)

    def body(dxo_ref, dh_ref, x_ref, g_ref, dx_ref, dg_ref):
        @pl.when(pl.program_id(0) == 0)
        def _():
            dg_ref[...] = jnp.zeros_like(dg_ref)

        xv = x_ref[...]
        dx, dg = _rms_bwd(xv, _rms_stat(xv), g_ref[...], dh_ref[...])
        dg_ref[...] += dg
        dx_ref[...] = dxo_ref[...] + dx

    row = pl.BlockSpec((tm, D), lambda i: (i, 0))
    return pl.pallas_call(body, name=name, grid=(T // tm,), in_specs=[row, row, row, _full((1, D))],
                          out_specs=[row, _full((1, D))],
                          out_shape=[jax.ShapeDtypeStruct((T, D), F32), jax.ShapeDtypeStruct((1, D), F32)],
                          compiler_params=_cp("arbitrary"))(dxo, dh, x, g)


def _loss_fwd_bwd(y, target, name):
    T, D = y.shape
    tm = _pick(T, 512, SUBLANES)

    def body(y_ref, t_ref, l_ref, dy_ref):
        @pl.when(pl.program_id(0) == 0)
        def _():
            l_ref[...] = jnp.zeros_like(l_ref)

        e = y_ref[...] - t_ref[...]
        dy_ref[...] = e * (1.0 / D)
        l_ref[...] += 0.5 * jnp.sum(jnp.mean(e * e, axis=-1, keepdims=True), axis=0, keepdims=True)

    row = pl.BlockSpec((tm, D), lambda i: (i, 0))
    return pl.pallas_call(body, name=name, grid=(T // tm,), in_specs=[row, row],
                          out_specs=[_full((SUBLANES, LANES)), row],
                          out_shape=[jax.ShapeDtypeStruct((SUBLANES, LANES), F32), jax.ShapeDtypeStruct((T, D), F32)],
                          compiler_params=_cp("arbitrary"))(y, target)


DN_ROWS = 512


def _shift_down(prev8, cur, s):
    n = cur.shape[0]
    xx = jnp.concatenate([prev8, cur], axis=0)
    return pltpu.roll(xx, s, 0)[SUBLANES:SUBLANES + n, :]


def _shift_up(cur, next8, s):
    n = cur.shape[0]
    xx = jnp.concatenate([cur, next8], axis=0)
    return pltpu.roll(xx, n + SUBLANES - s, 0)[:n, :]


def _conv_tile(x_ref, w, r, rows):
    start = pl.multiple_of(r * rows, SUBLANES)
    cur = x_ref[pl.ds(start, rows), :]
    pstart = pl.multiple_of(jnp.maximum(start - SUBLANES, 0), SUBLANES)
    prev8 = jnp.where(r == 0, 0.0, x_ref[pl.ds(pstart, SUBLANES), :])
    taps = [_shift_down(prev8, cur, DN_CONV - 1 - j) if j < DN_CONV - 1 else cur for j in range(DN_CONV)]
    c = taps[0] * w[0:1, :]
    for j in range(1, DN_CONV):
        c = c + taps[j] * w[j:j + 1, :]
    return c, taps


def _dn_prep_fwd(proj, conv_w, name):
    T = proj.shape[0]
    W = DN_HEADS * DN_HEAD_DIM
    rows = min(DN_ROWS, T)
    n_inner = T // rows
    scale = DN_HEAD_DIM ** -0.5

    def body(x_ref, w_ref, o_ref):
        cb = pl.program_id(0)
        w = w_ref[...]
        is_qk = cb < 2 * DN_HEADS
        post = jnp.where(cb < DN_HEADS, scale, 1.0)

        def step(r, carry):
            c, _ = _conv_tile(x_ref, w, r, rows)
            s = c * _sigmoid(c)
            rinv = lax.rsqrt(jnp.sum(s * s, axis=-1, keepdims=True) + L2_EPS)
            o_ref[pl.ds(pl.multiple_of(r * rows, SUBLANES), rows), :] = jnp.where(is_qk, s * rinv * post, s)
            return carry

        lax.fori_loop(0, n_inner, step, 0)

    col = pl.BlockSpec((T, LANES), lambda j: (0, j))
    return pl.pallas_call(body, name=name, grid=(3 * W // LANES,),
                          in_specs=[col, pl.BlockSpec((DN_CONV, LANES), lambda j: (0, j))], out_specs=col,
                          out_shape=jax.ShapeDtypeStruct((T, 3 * W), F32), compiler_params=_cp("parallel"))(proj, conv_w)


def _dn_prep_bwd(proj, conv_w, dqkv, name):
    T = proj.shape[0]
    W = DN_HEADS * DN_HEAD_DIM
    rows = min(DN_ROWS, T)
    n_inner = T // rows
    scale = DN_HEAD_DIM ** -0.5

    def body(x_ref, w_ref, dy_ref, dx_ref, dw_ref, dc_scr):
        cb = pl.program_id(0)
        w = w_ref[...]
        is_qk = cb < 2 * DN_HEADS
        post = jnp.where(cb < DN_HEADS, scale, 1.0)

        def step1(r, dws):
            c, taps = _conv_tile(x_ref, w, r, rows)
            sg = _sigmoid(c)
            s = c * sg
            rinv = lax.rsqrt(jnp.sum(s * s, axis=-1, keepdims=True) + L2_EPS)
            dy = dy_ref[pl.ds(pl.multiple_of(r * rows, SUBLANES), rows), :]
            yn = s * rinv
            dyn = dy * post
            ds_qk = rinv * (dyn - yn * jnp.sum(dyn * yn, axis=-1, keepdims=True))
            ds = jnp.where(is_qk, ds_qk, dy)
            dc = ds * (sg * (1.0 + c * (1.0 - sg)))
            dc_scr[pl.ds(pl.multiple_of(r * rows, SUBLANES), rows), :] = dc
            return tuple(dws[j] + jnp.sum(dc * taps[j], axis=0, keepdims=True) for j in range(DN_CONV))

        zero = jnp.zeros((1, LANES), F32)
        dws = lax.fori_loop(0, n_inner, step1, (zero,) * DN_CONV)
        for j in range(DN_CONV):
            dw_ref[j:j + 1, :] = dws[j]

        def step2(r, carry):
            start = pl.multiple_of(r * rows, SUBLANES)
            cur = dc_scr[pl.ds(start, rows), :]
            nstart = pl.multiple_of(jnp.minimum(start + rows, T - SUBLANES), SUBLANES)
            next8 = jnp.where(r == n_inner - 1, 0.0, dc_scr[pl.ds(nstart, SUBLANES), :])
            dx = cur * w[DN_CONV - 1:DN_CONV, :]
            for j in range(DN_CONV - 1):
                dx = dx + _shift_up(cur, next8, DN_CONV - 1 - j) * w[j:j + 1, :]
            dx_ref[pl.ds(start, rows), :] = _mx(dx)
            return carry

        lax.fori_loop(0, n_inner, step2, 0)

    col = pl.BlockSpec((T, LANES), lambda j: (0, j))
    wspec = pl.BlockSpec((DN_CONV, LANES), lambda j: (0, j))
    return pl.pallas_call(body, name=name, grid=(3 * W // LANES,), in_specs=[col, wspec, col], out_specs=[col, wspec],
                          out_shape=[jax.ShapeDtypeStruct((T, 3 * W), MXU_DTYPE), jax.ShapeDtypeStruct((DN_CONV, 3 * W), F32)],
                          scratch_shapes=[pltpu.VMEM((T, LANES), F32)], compiler_params=_cp("parallel"))(proj, conv_w, dqkv)


def _softplus(x):
    return jnp.maximum(x, 0.0) + jnp.log(1.0 + jnp.exp(-jnp.abs(x)))


def _dn_gate_fwd(ba, a_log, dt_bias, name):
    T = ba.shape[0]
    tm = _pick(T, 1024, SUBLANES)

    def body(ba_ref, al_ref, dt_ref, beta_ref, g_ref):
        beta_ref[...] = _sigmoid(ba_ref[:, :LANES])
        g_ref[...] = -jnp.exp(al_ref[...]) * _softplus(ba_ref[:, LANES:] + dt_ref[...])

    row = lambda w: pl.BlockSpec((tm, w), lambda i: (i, 0))
    return pl.pallas_call(body, name=name, grid=(T // tm,), in_specs=[row(2 * LANES), _full((1, LANES)), _full((1, LANES))],
                          out_specs=[row(LANES), row(LANES)],
                          out_shape=[jax.ShapeDtypeStruct((T, LANES), F32)] * 2, compiler_params=_cp("parallel"))(ba, a_log, dt_bias)


def _dn_gate_bwd(ba, a_log, dt_bias, dbeta, dg, name):
    T = ba.shape[0]
    tm = _pick(T, 1024, SUBLANES)

    def body(ba_ref, al_ref, dt_ref, dbeta_ref, dg_ref, dba_ref, dal_ref, ddt_ref):
        @pl.when(pl.program_id(0) == 0)
        def _():
            dal_ref[...] = jnp.zeros_like(dal_ref)
            ddt_ref[...] = jnp.zeros_like(ddt_ref)

        beta = _sigmoid(ba_ref[:, :LANES])
        dba_ref[:, :LANES] = _mx(dbeta_ref[...] * beta * (1.0 - beta))
        pre = ba_ref[:, LANES:] + dt_ref[...]
        ea = jnp.exp(al_ref[...])
        dgv = dg_ref[...]
        da = dgv * (-ea) * _sigmoid(pre)
        dba_ref[:, LANES:] = _mx(da)
        ddt_ref[...] += jnp.sum(da, axis=0, keepdims=True)
        dal_ref[...] += jnp.sum(dgv * (-ea) * _softplus(pre), axis=0, keepdims=True)

    row = lambda w: pl.BlockSpec((tm, w), lambda i: (i, 0))
    one = _full((1, LANES))
    return pl.pallas_call(body, name=name, grid=(T // tm,), in_specs=[row(2 * LANES), one, one, row(LANES), row(LANES)],
                          out_specs=[row(2 * LANES), one, one],
                          out_shape=[jax.ShapeDtypeStruct((T, 2 * LANES), MXU_DTYPE), jax.ShapeDtypeStruct((1, LANES), F32),
                                     jax.ShapeDtypeStruct((1, LANES), F32)],
                          compiler_params=_cp("arbitrary"))(ba, a_log, dt_bias, dbeta, dg)


def _tri(c, strict):
    i = lax.broadcasted_iota(jnp.int32, (c, c), 0)
    j = lax.broadcasted_iota(jnp.int32, (c, c), 1)
    return (i > j) if strict else (i >= j)


def _inv_unit_lower(ls):
    c = ls[0].shape[0]
    i = lax.broadcasted_iota(jnp.int32, (c, c), 0)
    j = lax.broadcasted_iota(jnp.int32, (c, c), 1)
    eye = jnp.where(i == j, 1.0, 0.0)
    facs = [[eye - l for l in ls]]
    cur = ls
    for _ in range(int(math.log2(c)) - 1):
        cur = [_dot(p, p, NN, TRI_PREC) for p in cur]
        facs.append([eye + p for p in cur])
    while len(facs) > 1:
        nxt = [[_dot(a, b, NN, TRI_PREC) for a, b in zip(facs[t], facs[t + 1])] for t in range(0, len(facs) - 1, 2)]
        if len(facs) % 2:
            nxt.append(facs[-1])
        facs = nxt
    return facs[0]


def _chunk_gates(g_blk):
    c = g_blk.shape[0]
    gcs = _dot(jnp.where(_tri(c, False), 1.0, 0.0), g_blk, NN, HI)
    return gcs, gcs.T


def _head_chunk(h, qh, kh, vh, beta_blk, gcs, gcs_t):
    c = qh.shape[0]
    incl = _tri(c, False)
    gc_col = gcs[:, h:h + 1]
    gc_row = gcs_t[h:h + 1, :]
    gc_last = gcs_t[h:h + 1, c - 1:c]
    dec = jnp.where(incl, jnp.exp(jnp.where(incl, gc_col - gc_row, 0.0)), 0.0)
    gam = jnp.exp(gc_col)
    rr = jnp.exp(gc_last - gc_col)
    gl = jnp.exp(gc_last)
    b = beta_blk[:, h:h + 1]
    kb = kh * b
    vb = vh * b
    kk = _dot(_mx(kb), _mx(kh), NT)
    lmat = jnp.where(_tri(c, True), kk * dec, 0.0)
    qk = _dot(_mx(qh), _mx(kh), NT)
    pmat = jnp.where(incl, qk * dec, 0.0)
    return dict(dec=dec, gam=gam, rr=rr, gl=gl, b=b, kb=kb, vb=vb, lmat=lmat, pmat=pmat)


def _dn_scan_fwd(qkv, beta, g, proj, norm_g, name):
    T = qkv.shape[0]
    C, H, Dh = DN_CHUNK, DN_HEADS, DN_HEAD_DIM
    W = H * Dh
    N = T // C

    def body(q_ref, k_ref, v_ref, beta_ref, g_ref, z_ref, ng_ref, og_ref, o_ref, tinv_ref, s_ref, state):
        @pl.when(pl.program_id(0) == 0)
        def _():
            state[...] = jnp.zeros_like(state)

        gcs, gcs_t = _chunk_gates(g_ref[...])
        beta_blk = beta_ref[...]
        ng = ng_ref[...]
        heads = range(H)
        cs = [slice(h * Dh, (h + 1) * Dh) for h in heads]
        qs = [_head_chunk(h, q_ref[:, cs[h]], k_ref[:, cs[h]], v_ref[:, cs[h]], beta_blk, gcs, gcs_t) for h in heads]
        tinvs = _inv_unit_lower([q["lmat"] for q in qs])
        for h in heads:
            tinv_ref[h] = tinvs[h]
        us = [_dot(tinvs[h], qs[h]["vb"], NN, TRI_PREC) for h in heads]
        ws = [_dot(tinvs[h], qs[h]["kb"] * qs[h]["gam"], NN, TRI_PREC) for h in heads]
        ss = [state[h] for h in heads]
        for h in heads:
            s_ref[0, h] = ss[h]
        sbs = [_mx(s) for s in ss]
        vnbs = [_mx(us[h] - _dot(_mx(ws[h]), sbs[h], NN)) for h in heads]
        os_ = [_dot(_mx(q_ref[:, cs[h]] * qs[h]["gam"]), sbs[h], NN) + _dot(_mx(qs[h]["pmat"]), vnbs[h], NN) for h in heads]
        for h in heads:
            state[h] = ss[h] * qs[h]["gl"] + _dot(_mx((k_ref[:, cs[h]] * qs[h]["rr"]).T), vnbs[h], NN)
        for h in heads:
            o = os_[h]
            o_ref[:, cs[h]] = o
            zh = z_ref[:, cs[h]]
            og_ref[:, cs[h]] = _mx(o * _rms_stat(o) * ng * (zh * _sigmoid(zh)))

    blk = lambda j: pl.BlockSpec((C, W), lambda n: (n, j))
    small = pl.BlockSpec((C, LANES), lambda n: (n, 0))
    return pl.pallas_call(
        body, name=name, grid=(N,),
        in_specs=[blk(0), blk(1), blk(2), small, small, blk(3), _full((1, Dh))],
        out_specs=[blk(0), blk(0), pl.BlockSpec((H, C, C), lambda n: (0, n, 0)),
                   pl.BlockSpec((1, H, Dh, Dh), lambda n: (n, 0, 0, 0))],
        out_shape=[jax.ShapeDtypeStruct((T, W), MXU_DTYPE), jax.ShapeDtypeStruct((T, W), F32),
                   jax.ShapeDtypeStruct((H, T, C), F32), jax.ShapeDtypeStruct((N, H, Dh, Dh), F32)],
        scratch_shapes=[pltpu.VMEM((H, Dh, Dh), F32)],
        compiler_params=_cp("arbitrary"),
    )(qkv, qkv, qkv, beta, g, proj, norm_g)


def _dn_scan_bwd(qkv, beta, g, proj, norm_g, o, tinv, s_all, dog, name):
    T = qkv.shape[0]
    C, H, Dh = DN_CHUNK, DN_HEADS, DN_HEAD_DIM
    W = H * Dh
    N = T // C

    def body(q_ref, k_ref, v_ref, beta_ref, g_ref, z_ref, ng_ref, o_ref, tinv_ref, s_ref, dog_ref,
             dqkv_ref, dbeta_ref, dg_ref, dz_ref, dng_ref, dstate):
        @pl.when(pl.program_id(0) == 0)
        def _():
            dstate[...] = jnp.zeros_like(dstate)
            dng_ref[...] = jnp.zeros_like(dng_ref)

        gcs, gcs_t = _chunk_gates(g_ref[...])
        beta_blk = beta_ref[...]
        ng = ng_ref[...]
        incl = _tri(C, False)
        strict = _tri(C, True)
        lane = lax.broadcasted_iota(jnp.int32, (C, LANES), 1)
        rowi = lax.broadcasted_iota(jnp.int32, (C, 1), 0)
        ones = jnp.ones((C, LANES), F32)
        dbeta_acc = jnp.zeros((C, LANES), F32)
        dgc_acc = jnp.zeros((C, LANES), F32)
        dng_acc = jnp.zeros((1, Dh), F32)
        heads = range(H)
        cs = [slice(h * Dh, (h + 1) * Dh) for h in heads]
        rsum = lambda t: jnp.sum(t, axis=1, keepdims=True)
        dobs = []
        for h in heads:
            oh, zh, dogh = o_ref[:, cs[h]], z_ref[:, cs[h]], dog_ref[:, cs[h]]
            rstat = _rms_stat(oh)
            sz = _sigmoid(zh)
            dz_ref[:, cs[h]] = _mx(dogh * (oh * rstat * ng) * (sz * (1.0 + zh * (1.0 - sz))))
            do, dng = _rms_bwd(oh, rstat, ng, dogh * (zh * sz))
            dng_acc = dng_acc + dng
            dobs.append(_mx(do))
        qs = [_head_chunk(h, q_ref[:, cs[h]], k_ref[:, cs[h]], v_ref[:, cs[h]], beta_blk, gcs, gcs_t) for h in heads]
        tms = [tinv_ref[h] for h in heads]
        us = [_dot(tms[h], qs[h]["vb"], NN, TRI_PREC) for h in heads]
        ws = [_dot(tms[h], qs[h]["kb"] * qs[h]["gam"], NN, TRI_PREC) for h in heads]
        ss = [s_ref[0, h] for h in heads]
        sbs = [_mx(s) for s in ss]
        wbs = [_mx(w) for w in ws]
        vnbs = [_mx(us[h] - _dot(wbs[h], sbs[h], NN)) for h in heads]
        dsns = [dstate[h] for h in heads]
        dsbs = [_mx(d) for d in dsns]
        dvnews = [_dot(_mx(qs[h]["pmat"]), dobs[h], TN) + _dot(_mx(k_ref[:, cs[h]] * qs[h]["rr"]), dsbs[h], NN) for h in heads]
        dvb16s = [_mx(d) for d in dvnews]
        dps = [jnp.where(incl, _dot(dobs[h], vnbs[h], NT), 0.0) for h in heads]
        dqds = [_dot(dobs[h], sbs[h], NT) for h in heads]
        dkds = [_dot(vnbs[h], dsbs[h], NT) for h in heads]
        dgls = [jnp.sum(rsum(ss[h] * dsns[h]), axis=0, keepdims=True) for h in heads]
        dws = [-_dot(dvb16s[h], sbs[h], NT) for h in heads]
        for h in heads:
            dstate[h] = (_dot(_mx(q_ref[:, cs[h]] * qs[h]["gam"]), dobs[h], TN) + qs[h]["gl"] * dsns[h]
                         - _dot(wbs[h], dvb16s[h], TN))
        dvbs = [_dot(tms[h], dvnews[h], TN, TRI_PREC) for h in heads]
        dkbgs = [_dot(tms[h], dws[h], TN, TRI_PREC) for h in heads]
        dls = [jnp.where(strict, -(_dot(dvbs[h], us[h], NT, TRI_PREC) + _dot(dkbgs[h], ws[h], NT, TRI_PREC)), 0.0)
               for h in heads]
        mmats = [dls[h] * qs[h]["lmat"] + dps[h] * qs[h]["pmat"] for h in heads]
        dgcs = [rsum(mmats[h]) - _dot(mmats[h], ones, TN, HI)[:, :1] for h in heads]
        dkk16s = [_mx(dls[h] * qs[h]["dec"]) for h in heads]
        dqk16s = [_mx(dps[h] * qs[h]["dec"]) for h in heads]
        for h in heads:
            q = qs[h]
            qh, kh, vh = q_ref[:, cs[h]], k_ref[:, cs[h]], v_ref[:, cs[h]]
            gam, rr, b, kb = q["gam"], q["rr"], q["b"], q["kb"]
            dkb = _dot(dkk16s[h], _mx(kh), NN) + dkbgs[h] * gam
            dk = _dot(dkk16s[h], _mx(kb), TN) + _dot(dqk16s[h], _mx(qh), TN) + dkb * b + dkds[h] * rr
            dq = _dot(dqk16s[h], _mx(kh), NN) + dqds[h] * gam
            dgam = rsum(dkbgs[h] * kb) + rsum(dqds[h] * qh)
            dr = rsum(dkds[h] * kh)
            dgc_last = jnp.sum(dr * rr, axis=0, keepdims=True) + dgls[h] * q["gl"]
            dgc = dgcs[h] + dgam * gam - dr * rr + jnp.where(rowi == C - 1, dgc_last, 0.0)
            dbeta = rsum(dvbs[h] * vh) + rsum(dkb * kh)
            dqkv_ref[:, cs[h]] = dq
            dqkv_ref[:, W + h * Dh:W + (h + 1) * Dh] = dk
            dqkv_ref[:, 2 * W + h * Dh:2 * W + (h + 1) * Dh] = dvbs[h] * b
            dbeta_acc = jnp.where(lane == h, dbeta, dbeta_acc)
            dgc_acc = jnp.where(lane == h, dgc, dgc_acc)
        dbeta_ref[...] = dbeta_acc
        dg_ref[...] = _dot(jnp.where(incl, 1.0, 0.0), dgc_acc, TN, HI)
        dng_ref[...] += dng_acc

    rev = lambda n: N - 1 - n
    blk = lambda j: pl.BlockSpec((C, W), lambda n: (rev(n), j))
    small = pl.BlockSpec((C, LANES), lambda n: (rev(n), 0))
    return pl.pallas_call(
        body, name=name, grid=(N,),
        in_specs=[blk(0), blk(1), blk(2), small, small, blk(3), _full((1, Dh)), blk(0),
                  pl.BlockSpec((H, C, C), lambda n: (0, rev(n), 0)),
                  pl.BlockSpec((1, H, Dh, Dh), lambda n: (rev(n), 0, 0, 0)), blk(0)],
        out_specs=[pl.BlockSpec((C, 3 * W), lambda n: (rev(n), 0)), small, small, blk(0), _full((1, Dh))],
        out_shape=[jax.ShapeDtypeStruct((T, 3 * W), F32), jax.ShapeDtypeStruct((T, LANES), F32),
                   jax.ShapeDtypeStruct((T, LANES), F32), jax.ShapeDtypeStruct((T, W), MXU_DTYPE),
                   jax.ShapeDtypeStruct((1, Dh), F32)],
        scratch_shapes=[pltpu.VMEM((H, Dh, Dh), F32)],
        compiler_params=_cp("arbitrary"),
    )(qkv, qkv, qkv, beta, g, proj, norm_g, o, tinv, s_all, dog)


_INV_SQRT2 = 0.7071067811865476
_INV_SQRT_2PI = 0.3989422804014327


def _sg_recompute(zp_ref, bin_ref, lng_ref, lnb_ref):
    E = SG_WIDTH
    zin = zp_ref[...] + bin_ref[...]
    cdf = 0.5 * (1.0 + lax.erf(zin * _INV_SQRT2))
    zz = zin * cdf
    u = zz[:, :E]
    vp = zz[:, E:]
    mu = jnp.mean(vp, axis=-1, keepdims=True)
    xc = vp - mu
    rstd = lax.rsqrt(jnp.mean(xc * xc, axis=-1, keepdims=True) + LN_EPS)
    xhat = xc * rstd
    v = xhat * lng_ref[...] + lnb_ref[...]
    return zin, cdf, u, xhat, rstd, v


def _sg_masked_ws(ws_ref, g):
    return _mx(jnp.where(_tri(SG_CHUNK, False), ws_ref[g], 0.0))


def _sg_fwd(zpre, b_in, ln_g, ln_b, w_s, b_s_t, name):
    T = zpre.shape[0]
    E, G, C, GW = SG_WIDTH, SG_GROUPS, SG_CHUNK, SG_GROUP_W

    def body(zp_ref, bin_ref, lng_ref, lnb_ref, ws_ref, bst_ref, um_ref):
        _, _, u, _, _, v = _sg_recompute(zp_ref, bin_ref, lng_ref, lnb_ref)
        bst = bst_ref[...]
        for g in range(G):
            cs = slice(g * GW, (g + 1) * GW)
            mixed = _dot(_sg_masked_ws(ws_ref, g), _mx(v[:, cs]), NN) + bst[:, g:g + 1]
            um_ref[:, cs] = _mx(u[:, cs] * mixed)

    return pl.pallas_call(
        body, name=name, grid=(T // C,),
        in_specs=[pl.BlockSpec((C, 2 * E), lambda n: (n, 0)), _full((1, 2 * E)), _full((1, E)), _full((1, E)),
                  _full((G, C, C)), _full((C, LANES))],
        out_specs=pl.BlockSpec((C, E), lambda n: (n, 0)),
        out_shape=jax.ShapeDtypeStruct((T, E), MXU_DTYPE), compiler_params=_cp("parallel"),
    )(zpre, b_in, ln_g, ln_b, w_s, b_s_t)


def _sg_bwd(zpre, b_in, ln_g, ln_b, w_s, b_s_t, dum, name):
    T = zpre.shape[0]
    E, G, C, GW = SG_WIDTH, SG_GROUPS, SG_CHUNK, SG_GROUP_W

    def body(zp_ref, bin_ref, lng_ref, lnb_ref, ws_ref, bst_ref, dum_ref,
             dz_ref, dbin_ref, dlng_ref, dlnb_ref, dws_ref, dbst_ref):
        @pl.when(pl.program_id(0) == 0)
        def _():
            for r in (dbin_ref, dlng_ref, dlnb_ref, dws_ref, dbst_ref):
                r[...] = jnp.zeros_like(r)

        zin, cdf, u, xhat, rstd, v = _sg_recompute(zp_ref, bin_ref, lng_ref, lnb_ref)
        bst = bst_ref[...]
        lane = lax.broadcasted_iota(jnp.int32, (C, LANES), 1)
        dum_v = dum_ref[...]
        dbst = jnp.zeros((C, LANES), F32)
        du_parts, dv_parts = [], []
        for g in range(G):
            cs = slice(g * GW, (g + 1) * GW)
            wsm = _sg_masked_ws(ws_ref, g)
            vg = _mx(v[:, cs])
            mixed = _dot(wsm, vg, NN) + bst[:, g:g + 1]
            dumg = dum_v[:, cs]
            du_parts.append(dumg * mixed)
            dmixed = dumg * u[:, cs]
            dmb = _mx(dmixed)
            dv_parts.append(_dot(wsm, dmb, TN))
            dws_ref[g] += _dot(dmb, vg, NT)
            dbst = jnp.where(lane == g, jnp.sum(dmixed, axis=1, keepdims=True), dbst)
        dbst_ref[...] += dbst
        du = jnp.concatenate(du_parts, axis=1)
        dv = jnp.concatenate(dv_parts, axis=1)
        dlng_ref[...] += jnp.sum(dv * xhat, axis=0, keepdims=True)
        dlnb_ref[...] += jnp.sum(dv, axis=0, keepdims=True)
        dxh = dv * lng_ref[...]
        dvp = rstd * (dxh - jnp.mean(dxh, axis=-1, keepdims=True) - xhat * jnp.mean(dxh * xhat, axis=-1, keepdims=True))
        dzz = jnp.concatenate([du, dvp], axis=1)
        dzin = dzz * (cdf + zin * (_INV_SQRT_2PI * jnp.exp(-0.5 * zin * zin)))
        dz_ref[...] = _mx(dzin)
        dbin_ref[...] += jnp.sum(dzin, axis=0, keepdims=True)

    return pl.pallas_call(
        body, name=name, grid=(T // C,),
        in_specs=[pl.BlockSpec((C, 2 * E), lambda n: (n, 0)), _full((1, 2 * E)), _full((1, E)), _full((1, E)),
                  _full((G, C, C)), _full((C, LANES)), pl.BlockSpec((C, E), lambda n: (n, 0))],
        out_specs=[pl.BlockSpec((C, 2 * E), lambda n: (n, 0)), _full((1, 2 * E)), _full((1, E)), _full((1, E)),
                   _full((G, C, C)), _full((C, LANES))],
        out_shape=[jax.ShapeDtypeStruct((T, 2 * E), MXU_DTYPE), jax.ShapeDtypeStruct((1, 2 * E), F32),
                   jax.ShapeDtypeStruct((1, E), F32), jax.ShapeDtypeStruct((1, E), F32),
                   jax.ShapeDtypeStruct((G, C, C), F32), jax.ShapeDtypeStruct((C, LANES), F32)],
        compiler_params=_cp("arbitrary"),
    )(zpre, b_in, ln_g, ln_b, w_s, b_s_t, dum)


def _row(v):
    return v.reshape(1, -1)


def _pad_lanes(v):
    v = v.reshape(1, -1)
    return jnp.pad(v, ((0, 0), (0, LANES - v.shape[1])))


def _local_step(x, target, p, weights_for, grads_ready=None):
    ng = p["norm_g"]
    grads = {}
    dng = [[None] * 6 for _ in range(2)]
    order = [jnp.zeros((), F32)]

    def tell(group):
        zero = grads_ready(group, grads) if grads_ready is not None else None
        if zero is not None:
            order[0] = zero

    def gain(i, s):
        return _row(ng[i, s]) + order[0]

    def ffn_f(xin, i, j, tag):
        wt = weights_for("ffn" + tag, xin)
        xo, h, gu, y = _ffn_fwd(xin, _row(ng[i, 4 * j]), _row(ng[i, 4 * j + 1]), wt["wgu"], wt["wd"], "ffn_fwd_" + tag)
        return xo, (xin, h, gu, y, wt)

    x1, sv_f00 = ffn_f(x, 0, 0, "00")
    dnw = weights_for("dn", x1)
    hn0 = _norm_fwd(x1, _row(ng[0, 2]), "dn_prenorm")
    proj = _mm(hn0, dnw["dn_wqkvz"], "nn", "dn_proj")
    ba = _mm(hn0, dnw["dn_wba"], "nn", "dn_proj_ba")
    a_log = _pad_lanes(p["dn_a_log"])
    dt_bias = _pad_lanes(p["dn_dt_bias"])
    dn_ng = _row(p["dn_norm_g"])
    qkv = _dn_prep_fwd(proj, p["dn_conv_w"], "dn_prep_fwd")
    beta, gdec = _dn_gate_fwd(ba, a_log, dt_bias, "dn_gate_fwd")
    og, o_raw, tinv, s_all = _dn_scan_fwd(qkv, beta, gdec, proj, dn_ng, "dn_scan_fwd")
    m0 = _mm(og, dnw["dn_wout"], "nn", "dn_out")
    x2 = _postnorm_fwd(x1, m0, _row(ng[0, 3]), "dn_postnorm")
    x3, sv_f01 = ffn_f(x2, 0, 1, "01")
    x4, sv_f10 = ffn_f(x3, 1, 0, "10")
    sgw = weights_for("sg", x4)
    hn1 = _norm_fwd(x4, _row(ng[1, 2]), "sg_prenorm")
    zpre = _mm(hn1, sgw["sg_win"], "nn", "sg_proj")
    sg_bin = _row(p["sg_b_in"])
    sg_lng = _row(p["sg_ln_g"])
    sg_lnb = _row(p["sg_ln_b"])
    sg_bst = jnp.pad(p["sg_b_s"].T, ((0, 0), (0, LANES - SG_GROUPS)))
    um = _sg_fwd(zpre, sg_bin, sg_lng, sg_lnb, p["sg_w_s"], sg_bst, "sg_fwd")
    m1 = _mm(um, sgw["sg_wout"], "nn", "sg_out")
    x5 = _postnorm_fwd(x4, m1, _row(ng[1, 3]), "sg_postnorm")
    x6, sv_f11 = ffn_f(x5, 1, 1, "11")
    loss_part, dx = _loss_fwd_bwd(x6, target, "loss")

    def ffn_b(dxo, sv, i, j, tag):
        xin, h, gu, y, wt = sv
        dxi, dy, a, dgu, dg0, dg1 = _ffn_bwd(dxo, xin, y, gu, gain(i, 4 * j), gain(i, 4 * j + 1), wt["wgu"], wt["wd"],
                                             "ffn_bwd_" + tag)
        grads["wd" + tag] = _mm(a, dy, "tn", "ffn_wgrad_down_" + tag)
        grads["wguT" + tag] = _mm(dgu, h, "tn", "ffn_wgrad_up_" + tag)
        tell("ffn" + tag)
        dng[i][4 * j] = dg0
        dng[i][4 * j + 1] = dg1
        return dxi

    dx = ffn_b(dx, sv_f11, 1, 1, "11")
    dm1, dng[1][3] = _postnorm_bwd(dx, m1, gain(1, 3), "sg_postnorm_bwd")
    grads["sg_w_out"] = _mm(um, dm1, "tn", "sg_wgrad_out")
    dum = _mm(dm1, sgw["sg_wout"], "nt", "sg_dgrad_out")
    dz1, dbin, dlng, dlnb, dws, dbst = _sg_bwd(zpre, sg_bin, sg_lng, sg_lnb, p["sg_w_s"], sg_bst, dum, "sg_bwd")
    grads["sg_w_inT"] = _mm(dz1, hn1, "tn", "sg_wgrad_in")
    tell("sg")
    dh1 = _mm(dz1, sgw["sg_win"], "nt", "sg_dgrad_in")
    dx, dng[1][2] = _prenorm_bwd(dx, dh1, x4, gain(1, 2), "sg_prenorm_bwd")
    grads["sg_b_in"] = dbin.reshape(1, -1)
    grads["sg_ln_g"] = dlng.reshape(1, -1)
    grads["sg_ln_b"] = dlnb.reshape(1, -1)
    grads["sg_w_s"] = jnp.where(jnp.tril(jnp.ones((SG_CHUNK, SG_CHUNK), bool)), dws, 0.0)[None]
    grads["sg_b_s"] = dbst[:, :SG_GROUPS].T[None]
    dx = ffn_b(dx, sv_f10, 1, 0, "10")
    dx = ffn_b(dx, sv_f01, 0, 1, "01")
    dm0, dng[0][3] = _postnorm_bwd(dx, m0, gain(0, 3), "dn_postnorm_bwd")
    grads["dn_w_out"] = _mm(og, dm0, "tn", "dn_wgrad_out")
    dog = _mm(dm0, dnw["dn_wout"], "nt", "dn_dgrad_out")
    dqkv, dbeta, dgdec, dz0, dnng = _dn_scan_bwd(qkv, beta, gdec, proj, dn_ng, o_raw, tinv, s_all, dog, "dn_scan_bwd")
    dqkv_pre, dconv = _dn_prep_bwd(proj, p["dn_conv_w"], dqkv, "dn_prep_bwd")
    dba, dal, ddt = _dn_gate_bwd(ba, a_log, dt_bias, dbeta, dgdec, "dn_gate_bwd")
    W3 = 3 * DN_HEADS * DN_HEAD_DIM
    dw_qkv = _mm(hn0, dqkv_pre, "tn", "dn_wgrad_qkv")
    dw_z = _mm(hn0, dz0, "tn", "dn_wgrad_z")
    dw_ba = _mm(hn0, dba, "tn", "dn_wgrad_ba")
    grads["dn_w_in"] = jnp.concatenate(
        [dw_qkv, dw_z, dw_ba[:, :DN_HEADS], dw_ba[:, LANES:LANES + DN_HEADS]], axis=1)
    tell("dn")
    dh0 = _mm(dqkv_pre, dnw["dn_wqkvz"][:, :W3], "nt", "dn_dgrad_qkv")
    dh0 = _mm(dz0, dnw["dn_wqkvz"][:, W3:], "nt", "dn_dgrad_z", add=dh0)
    dh0 = _mm(dba, dnw["dn_wba"], "nt", "dn_dgrad_ba", add=dh0)
    dx, dng[0][2] = _prenorm_bwd(dx, dh0, x1, gain(0, 2), "dn_prenorm_bwd")
    grads["dn_conv_w"] = dconv[None]
    grads["dn_a_log"] = dal[:, :DN_HEADS]
    grads["dn_dt_bias"] = ddt[:, :DN_HEADS]
    grads["dn_norm_g"] = dnng
    dx = ffn_b(dx, sv_f00, 0, 0, "00")
    grads["norm_g"] = jnp.stack([jnp.concatenate(dng[i], axis=0) for i in range(2)])
    return loss_part, dx, grads


def _mesh_pos():
    return lax.axis_index("x"), lax.axis_index("y"), lax.axis_index("c")


def _other_chips(x, y):
    return [(1 - x, y), (x, 1 - y), (1 - x, 1 - y)]


def _allgather_chips(arrs, name):
    n = len(arrs)

    def body(*refs):
        ins, outs = refs[:n], refs[n:2 * n]
        ici_send, ici_recv, d2d_send, d2d_recv = refs[2 * n:]
        x, y, c = _mesh_pos()
        me = 2 * x + y
        chips = _other_chips(x, y)
        sibling = (x, y, 1 - c)

        def ici(i, j, k):
            cx, cy = chips[j]
            return pltpu.make_async_remote_copy(src_ref=ins[i].at[c], dst_ref=outs[i].at[k, c], send_sem=ici_send.at[3 * i + j],
                                                recv_sem=ici_recv.at[3 * i + j], device_id=(cx, cy, c), device_id_type=MESH)

        def d2d(i, j, h):
            cx, cy = chips[j]
            slot = outs[i].at[2 * cx + cy, h]
            return pltpu.make_async_remote_copy(src_ref=slot, dst_ref=slot, send_sem=d2d_send.at[3 * i + j],
                                                recv_sem=d2d_recv.at[3 * i + j], device_id=sibling, device_id_type=MESH)

        sends = [ici(i, j, me) for i in range(n) for j in range(3)]
        for cp in sends:
            cp.start()
        for i in range(n):
            for j, (cx, cy) in enumerate(chips):
                ici(i, j, 2 * cx + cy).wait_recv()
                fwd = d2d(i, j, c)
                fwd.start()
                sends.append(fwd)
        for i in range(n):
            for j in range(3):
                d2d(i, j, 1 - c).wait_recv()
        for cp in sends:
            cp.wait_send()

    return pl.pallas_call(
        body, name=name, in_specs=[ANY] * n, out_specs=[ANY] * n,
        out_shape=[jax.ShapeDtypeStruct((N_CHIPS,) + a.shape, a.dtype) for a in arrs],
        scratch_shapes=[pltpu.SemaphoreType.DMA((3 * n,))] * 4,
    )(*arrs)


def _swap_halves(arrs, half_first, name):
    n = len(arrs)

    def body(*refs):
        ins, outs = refs[:n], refs[n:2 * n]
        send_sems, recv_sems = refs[2 * n:]
        x, y, c = _mesh_pos()
        cps = [pltpu.make_async_remote_copy(src_ref=ins[i].at[1 - c] if half_first[i] else ins[i].at[:, 1 - c],
                                            dst_ref=outs[i], send_sem=send_sems.at[i], recv_sem=recv_sems.at[i],
                                            device_id=(x, y, 1 - c), device_id_type=MESH)
               for i in range(n)]
        for cp in cps:
            cp.start()
        for cp in cps:
            cp.wait()

    return pl.pallas_call(
        body, name=name, in_specs=[ANY] * n, out_specs=[ANY] * n,
        out_shape=[jax.ShapeDtypeStruct((N_CHIPS,) + a.shape[2:], a.dtype) for a in arrs],
        scratch_shapes=[pltpu.SemaphoreType.DMA((n,)), pltpu.SemaphoreType.DMA((n,))],
    )(*arrs)


HBM = pl.BlockSpec(memory_space=pltpu.HBM)
SEM = pl.BlockSpec(memory_space=pltpu.SEMAPHORE)
TOKEN = jax.ShapeDtypeStruct((SUBLANES, LANES), F32)


def _chip_copies(src_refs, land_refs, send_sems, recv_sems, slice_by_chip, receiving):
    x, y, c = _mesh_pos()
    me = 2 * x + y
    cps = []
    for i, (src, land) in enumerate(zip(src_refs, land_refs)):
        for j, (cx, cy) in enumerate(_other_chips(x, y)):
            peer = 2 * cx + cy
            s = src.at[me if receiving else peer] if slice_by_chip else src
            cps.append(pltpu.make_async_remote_copy(
                src_ref=s, dst_ref=land.at[peer if receiving else me], send_sem=send_sems.at[3 * i + j],
                recv_sem=recv_sems.at[3 * i + j], device_id=(cx, cy, c), device_id_type=MESH))
    return cps


def _chips_start(srcs, slice_by_chip, after, name):
    n = len(srcs)
    lands = [lax.empty((N_CHIPS,) + (s.shape[1:] if slice_by_chip else s.shape), s.dtype) for s in srcs]

    def body(*refs):
        src_refs, land_refs = refs[:n], refs[n:2 * n]
        send_sems, recv_sems = refs[2 * n + 1], refs[2 * n + 2]
        token = refs[-1]
        for cp in _chip_copies(src_refs, land_refs, send_sems, recv_sems, slice_by_chip, False):
            cp.start()
        token[...] = jnp.zeros_like(token)

    outs = pl.pallas_call(
        body, name=name,
        in_specs=[HBM] * (2 * n) + [ANY],
        out_specs=(SEM, SEM) + (HBM,) * (2 * n) + (pl.BlockSpec(memory_space=pltpu.VMEM),),
        out_shape=(pltpu.SemaphoreType.DMA((3 * n,)), pltpu.SemaphoreType.DMA((3 * n,)))
        + tuple(pltpu.HBM(a.shape, a.dtype) for a in list(srcs) + lands) + (TOKEN,),
        input_output_aliases={i: 2 + i for i in range(2 * n)},
        compiler_params=pltpu.CompilerParams(has_side_effects=pltpu.SideEffectType.DATAFLOW_SIDE_EFFECTING),
    )(*[pltpu.with_memory_space_constraint(a, pltpu.HBM) for a in list(srcs) + lands], after)
    return dict(sems=outs[:2], srcs=outs[2:2 + n], lands=outs[2 + n:2 + 2 * n], token=outs[-1], slice_by_chip=slice_by_chip)


def _chips_wait(started, after, name):
    n = len(started["srcs"])
    slice_by_chip = started["slice_by_chip"]
    after = list(after) if isinstance(after, (list, tuple)) else [after]

    def body(*refs):
        src_refs, land_refs = refs[:n], refs[n:2 * n]
        send_sems, recv_sems = refs[2 * n], refs[2 * n + 1]
        for cp in _chip_copies(src_refs, land_refs, send_sems, recv_sems, slice_by_chip, True):
            cp.wait_send()
            cp.wait_recv()

    outs = pl.pallas_call(
        body, name=name,
        in_specs=[HBM] * (2 * n) + [SEM, SEM] + [ANY] * len(after),
        out_specs=(HBM,) * (2 * n),
        out_shape=tuple(pltpu.HBM(a.shape, a.dtype) for a in list(started["srcs"]) + list(started["lands"])),
        input_output_aliases={i: i for i in range(2 * n)},
        compiler_params=pltpu.CompilerParams(has_side_effects=pltpu.SideEffectType.DATAFLOW_SIDE_EFFECTING),
    )(*started["srcs"], *started["lands"], *started["sems"], *after)
    return outs[:n], outs[n:]


def _swap_whole(arrs, name):
    n = len(arrs)

    def body(*refs):
        ins, outs = refs[:n], refs[n:2 * n]
        send_sems, recv_sems = refs[2 * n:]
        x, y, c = _mesh_pos()
        cps = [pltpu.make_async_remote_copy(src_ref=ins[i], dst_ref=outs[i], send_sem=send_sems.at[i],
                                            recv_sem=recv_sems.at[i], device_id=(x, y, 1 - c), device_id_type=MESH)
               for i in range(n)]
        for cp in cps:
            cp.start()
        for cp in cps:
            cp.wait()

    return pl.pallas_call(
        body, name=name, in_specs=[ANY] * n, out_specs=[ANY] * n,
        out_shape=[jax.ShapeDtypeStruct(a.shape, a.dtype) for a in arrs],
        scratch_shapes=[pltpu.SemaphoreType.DMA((n,)), pltpu.SemaphoreType.DMA((n,))],
    )(*arrs)


def _allgather_devices(a, name):
    masks = [(mx, my, mc) for mx in (0, 1) for my in (0, 1) for mc in (0, 1)][1:]

    def body(in_ref, out_ref, send_sems, recv_sems, loc_sem):
        x, y, c = _mesh_pos()
        me = 4 * x + 2 * y + c
        lc = pltpu.make_async_copy(in_ref, out_ref.at[me], loc_sem.at[0])
        lc.start()
        peers = [(jnp.where(mx, 1 - x, x), jnp.where(my, 1 - y, y), jnp.where(mc, 1 - c, c)) for mx, my, mc in masks]
        cps = [pltpu.make_async_remote_copy(src_ref=in_ref, dst_ref=out_ref.at[me], send_sem=send_sems.at[j],
                                            recv_sem=recv_sems.at[j], device_id=peers[j], device_id_type=MESH)
               for j in range(len(masks))]
        for cp in cps:
            cp.start()
        for j, (px, py, pc) in enumerate(peers):
            pltpu.make_async_remote_copy(src_ref=in_ref, dst_ref=out_ref.at[4 * px + 2 * py + pc], send_sem=send_sems.at[j],
                                         recv_sem=recv_sems.at[j], device_id=peers[j], device_id_type=MESH).wait_recv()
        for cp in cps:
            cp.wait_send()
        lc.wait()

    return pl.pallas_call(
        body, name=name, in_specs=[ANY], out_specs=ANY,
        out_shape=jax.ShapeDtypeStruct((N_DEV,) + a.shape, a.dtype),
        scratch_shapes=[pltpu.SemaphoreType.DMA((N_DEV - 1,)), pltpu.SemaphoreType.DMA((N_DEV - 1,)),
                        pltpu.SemaphoreType.DMA((1,))],
    )(a)


def _as_rows(a, lead):
    shp = a.shape
    rows = 1
    for s in shp[lead:-1]:
        rows *= s
    return a.reshape(shp[:lead] + (rows, shp[-1]))


def _row_tile(rows, cols, n_bufs):
    budget = (24 * 1024 * 1024) // (n_bufs * 2 * 4 * cols)
    return _pick(rows, max(2 * SUBLANES, budget), 2 * SUBLANES)


def _sum_leading(a, name):
    n = a.shape[0]
    v = _as_rows(a, 1)
    _, rows, cols = v.shape
    tr = _row_tile(rows, cols, n + 1)

    def body(a_ref, o_ref):
        acc = a_ref[0]
        for k in range(1, n):
            acc = acc + a_ref[k]
        o_ref[...] = acc

    out = pl.pallas_call(body, name=name, grid=(rows // tr,),
                         in_specs=[pl.BlockSpec((n, tr, cols), lambda i: (0, i, 0))],
                         out_specs=pl.BlockSpec((tr, cols), lambda i: (i, 0)),
                         out_shape=jax.ShapeDtypeStruct((rows, cols), F32), compiler_params=_cp("parallel"))(v)
    return out.reshape(a.shape[1:])


def _scalar(i):
    return jnp.reshape(i, (1,)).astype(jnp.int32)


def _add_own_half(g, other, c, half_first, name):
    _, rows, cols = other.shape
    tr = _row_tile(rows, cols, 3)

    def body(c_ref, g_ref, o_ref, out_ref):
        out_ref[0] = (g_ref[0, 0] + o_ref[0]).astype(out_ref.dtype)

    if half_first:
        g_map = lambda k, i, c_ref: (c_ref[0], k, i, 0)
    else:
        g_map = lambda k, i, c_ref: (k, c_ref[0], i, 0)
    flat = pl.BlockSpec((1, tr, cols), lambda k, i, c_ref: (k, i, 0))
    return pl.pallas_call(
        body, name=name,
        grid_spec=pltpu.PrefetchScalarGridSpec(
            num_scalar_prefetch=1, grid=(N_CHIPS, rows // tr),
            in_specs=[pl.BlockSpec((1, 1, tr, cols), g_map), flat], out_specs=flat),
        out_shape=jax.ShapeDtypeStruct(other.shape, COMM_DTYPE), compiler_params=_cp("parallel", "parallel"),
    )(_scalar(c), g, other)


def _sum_chips(own, got, chip, name):
    pv = _as_rows(own, 1)
    bv = _as_rows(got, 1)
    _, rows, cols = pv.shape
    tr = _row_tile(rows, cols, N_CHIPS + 2)

    def body(chip_ref, p_ref, b_ref, o_ref):
        mine = p_ref[0].astype(F32)
        acc = jnp.where(chip_ref[0] == 0, mine, b_ref[0].astype(F32))
        for k in range(1, N_CHIPS):
            acc = acc + jnp.where(chip_ref[0] == k, mine, b_ref[k].astype(F32))
        o_ref[...] = acc

    out = pl.pallas_call(
        body, name=name,
        grid_spec=pltpu.PrefetchScalarGridSpec(
            num_scalar_prefetch=1, grid=(rows // tr,),
            in_specs=[pl.BlockSpec((1, tr, cols), lambda i, k_ref: (k_ref[0], i, 0)),
                      pl.BlockSpec((N_CHIPS, tr, cols), lambda i, k_ref: (0, i, 0))],
            out_specs=pl.BlockSpec((tr, cols), lambda i, k_ref: (i, 0))),
        out_shape=jax.ShapeDtypeStruct((rows, cols), F32), compiler_params=_cp("parallel"),
    )(_scalar(chip), pv, bv)
    return out.reshape(own.shape[1:])


def _adam_math(w, g, m, v):
    nm = ADAM_B1 * m + (1.0 - ADAM_B1) * g
    nv = ADAM_B2 * v + (1.0 - ADAM_B2) * (g * g)
    m_hat = nm / (1.0 - ADAM_B1 ** ADAM_STEP)
    v_hat = nv / (1.0 - ADAM_B2 ** ADAM_STEP)
    return -ADAM_LR * (m_hat / (jnp.sqrt(v_hat) + ADAM_EPS) + ADAM_WD * w), nm, nv


def _adamw_halves(w, mine, theirs, m, v, c, name):
    shape = w.shape
    ws, ms, vs = (_as_rows(t.reshape((2, -1) + t.shape[-1:]), 1) for t in (w, m, v))
    a, b = _as_rows(mine, 0), _as_rows(theirs, 0)
    rows, cols = a.shape
    tr = _row_tile(rows, cols, 9)

    def body(c_ref, w_ref, a_ref, b_ref, m_ref, v_ref, g_ref, d_ref, nm_ref, nv_ref):
        gv = jnp.where(pl.program_id(0) == c_ref[0], a_ref[...], b_ref[...])
        g_ref[0] = gv
        d_ref[0], nm_ref[0], nv_ref[0] = _adam_math(w_ref[0], gv, m_ref[0], v_ref[0])

    half = pl.BlockSpec((1, tr, cols), lambda h, i, c_ref: (h, i, 0))
    flat = pl.BlockSpec((tr, cols), lambda h, i, c_ref: (i, 0))
    outs = pl.pallas_call(
        body, name=name,
        grid_spec=pltpu.PrefetchScalarGridSpec(num_scalar_prefetch=1, grid=(2, rows // tr),
                                               in_specs=[half, flat, flat, half, half], out_specs=[half] * 4),
        out_shape=[jax.ShapeDtypeStruct((2, rows, cols), F32)] * 4, compiler_params=_cp("parallel", "parallel"),
    )(_scalar(c), ws, a, b, ms, vs)
    return tuple(o.reshape(shape) for o in outs)


def _adamw(w, g, m, v, name):
    shape = w.shape
    ws, gs, ms, vs = (_as_rows(t, 0) for t in (w, g, m, v))
    rows, cols = ws.shape
    tr = _row_tile(rows, cols, 7)

    def body(w_ref, g_ref, m_ref, v_ref, d_ref, nm_ref, nv_ref):
        d_ref[...], nm_ref[...], nv_ref[...] = _adam_math(w_ref[...], g_ref[...], m_ref[...], v_ref[...])

    spec = pl.BlockSpec((tr, cols), lambda i: (i, 0))
    outs = pl.pallas_call(body, name=name, grid=(rows // tr,), in_specs=[spec] * 4, out_specs=[spec] * 3,
                          out_shape=[jax.ShapeDtypeStruct((rows, cols), F32)] * 3, compiler_params=_cp("parallel"))(ws, gs, ms, vs)
    return tuple(o.reshape(shape) for o in outs)


_BIG = ["ffn_w_gate", "ffn_w_up", "ffn_w_down", "dn_w_in", "dn_w_out", "sg_w_in", "sg_w_out"]
_SMALL_SHARDED = ["norm_g", "dn_conv_w", "sg_b_in", "sg_ln_g", "sg_ln_b"]
_SMALL_REPL = ["dn_a_log", "dn_dt_bias", "dn_norm_g", "sg_w_s", "sg_b_s"]
_WEIGHTS = ["norm_g", "ffn_w_gate", "ffn_w_up", "ffn_w_down", "dn_w_in", "dn_conv_w", "dn_a_log", "dn_dt_bias",
            "dn_norm_g", "dn_w_out", "sg_w_in", "sg_b_in", "sg_ln_g", "sg_ln_b", "sg_w_s", "sg_b_s", "sg_w_out"]
PACK_COLS = 1024


def _pack(arrs):
    flat = jnp.concatenate([a.reshape(-1) for a in arrs])
    pad = (-flat.shape[0]) % (SUBLANES * PACK_COLS)
    return jnp.pad(flat, (0, pad)).reshape(-1, PACK_COLS)


def _unpack(buf, shapes):
    flat = buf.reshape(-1)
    out, off = [], 0
    for s in shapes:
        n = math.prod(s)
        out.append(flat[off:off + n].reshape(s))
        off += n
    return out


def _as_halves(a):
    if a.shape[0] == 2:
        return a
    if a.shape[0] == 1:
        return a.reshape((2, a.shape[1] // 2) + a.shape[2:])
    return a.reshape((2, a.shape[0] // 2) + a.shape[1:])


def _with_own(gathered, own, chip):
    g = gathered.reshape((N_CHIPS,) + own.shape)
    return [jnp.where(chip == k, own, g[k]) for k in range(N_CHIPS)]


def _cat_shards(g, axis):
    return jnp.concatenate(list(g), axis=axis)


_GROUP_ORDER = ["ffn00", "dn", "ffn01", "ffn10", "sg", "ffn11"]


def _weight_groups(w):
    cast = {k: _mx(w[k]) for k in _BIG}
    groups = {"ffn%d%d" % (i, j): [cast["ffn_w_gate"][i, j], cast["ffn_w_up"][i, j], cast["ffn_w_down"][i, j]]
              for i, j in [(0, 0), (0, 1), (1, 0), (1, 1)]}
    groups["dn"] = [cast["dn_w_in"][0], cast["dn_w_out"][0]]
    groups["sg"] = [cast["sg_w_in"][0], cast["sg_w_out"][0]]
    return groups


def _group_matrices(group, shards):
    if group.startswith("ffn"):
        gate, up, down = shards
        return {"wgu": jnp.concatenate([_cat_shards(gate, 1), _cat_shards(up, 1)], axis=1), "wd": _cat_shards(down, 0)}
    if group == "sg":
        return {"sg_win": _cat_shards(shards[0], 1), "sg_wout": _cat_shards(shards[1], 0)}
    dn_full = _cat_shards(shards[0], 1)
    W4 = 4 * DN_HEADS * DN_HEAD_DIM
    wba = jnp.zeros((D_MODEL, 2 * LANES), dn_full.dtype)
    wba = wba.at[:, :DN_HEADS].set(dn_full[:, W4:W4 + DN_HEADS])
    wba = wba.at[:, LANES:LANES + DN_HEADS].set(dn_full[:, W4 + DN_HEADS:])
    return {"dn_wqkvz": dn_full[:, :W4], "dn_wba": wba, "dn_wout": _cat_shards(shards[1], 0)}


def _split_cols(a, n):
    w = a.shape[-1] // n
    return [a[..., k * w:(k + 1) * w] for k in range(n)]


def _split_rows(a, n):
    h = a.shape[-2] // n
    return [a[..., k * h:(k + 1) * h, :] for k in range(n)]


_IJ = [(0, 0), (0, 1), (1, 0), (1, 1)]


_REDUCED = ["wguT%d%d" % ij for ij in _IJ] + ["wd%d%d" % ij for ij in _IJ] + ["dn_w_in", "dn_w_out", "sg_w_inT", "sg_w_out"]


def _group_grads(group, grads):
    def rows_by_chip(a):
        return a.reshape(N_CHIPS, 2, a.shape[0] // (2 * N_CHIPS), a.shape[1])

    if group.startswith("ffn"):
        tag = group[3:]
        t = grads["wguT" + tag]
        return (["wguT" + tag, "wd" + tag],
                [t.reshape(2, N_CHIPS, t.shape[0] // (2 * N_CHIPS), t.shape[1]), rows_by_chip(grads["wd" + tag])], [True, False])
    if group == "sg":
        return ["sg_w_inT", "sg_w_out"], [rows_by_chip(grads["sg_w_inT"]), rows_by_chip(grads["sg_w_out"])], [False, False]
    dn_in = jnp.stack([jnp.stack(_split_cols(hf, N_CHIPS)) for hf in _split_rows(grads["dn_w_in"], 2)])
    return ["dn_w_in", "dn_w_out"], [dn_in, rows_by_chip(grads["dn_w_out"])], [True, False]


def _shard_grads(mine, theirs, c, w):
    lo = [jnp.where(c == 0, a, b) for a, b in zip(mine, theirs)]
    hi = [jnp.where(c == 0, b, a) for a, b in zip(mine, theirs)]
    rows = lambda t: jnp.concatenate([lo[t], hi[t]], axis=0)
    sq = lambda parts: jnp.stack(parts).reshape(2, 2, *parts[0].shape)
    g = {}
    g["ffn_w_gate"] = sq([lo[t].T for t in range(4)])
    g["ffn_w_up"] = sq([hi[t].T for t in range(4)])
    g["ffn_w_down"] = sq([rows(4 + t) for t in range(4)])
    g["dn_w_in"] = rows(8)[None]
    g["dn_w_out"] = rows(9)[None]
    g["sg_w_in"] = rows(10).T[None]
    g["sg_w_out"] = rows(11)[None]
    return {k: v.reshape(w[k].shape) for k, v in g.items()}


def kernel(x, norm_g, ffn_w_gate, ffn_w_up, ffn_w_down, dn_w_in, dn_conv_w, dn_a_log, dn_dt_bias, dn_norm_g, dn_w_out, sg_w_in, sg_b_in, sg_ln_g, sg_ln_b, sg_w_s, sg_b_s, sg_w_out, loss_target, m_norm_g, m_ffn_w_gate, m_ffn_w_up, m_ffn_w_down, m_dn_w_in, m_dn_conv_w, m_dn_a_log, m_dn_dt_bias, m_dn_norm_g, m_dn_w_out, m_sg_w_in, m_sg_b_in, m_sg_ln_g, m_sg_ln_b, m_sg_w_s, m_sg_b_s, m_sg_w_out, v_norm_g, v_ffn_w_gate, v_ffn_w_up, v_ffn_w_down, v_dn_w_in, v_dn_conv_w, v_dn_a_log, v_dn_dt_bias, v_dn_norm_g, v_dn_w_out, v_sg_w_in, v_sg_b_in, v_sg_ln_g, v_sg_ln_b, v_sg_w_s, v_sg_b_s, v_sg_w_out):
    args = dict(locals())
    w = {k: args[k] for k in _WEIGHTS}
    mom = {k: args["m_" + k] for k in _WEIGHTS}
    var = {k: args["v_" + k] for k in _WEIGHTS}
    cx, cy, cc = _mesh_pos()
    chip = 2 * cx + cy

    small_shapes = [w[k].shape for k in _SMALL_SHARDED]
    groups = _weight_groups(w)
    own = groups[_GROUP_ORDER[0]] + [_pack([w[k] for k in _SMALL_SHARDED])]
    first = _allgather_chips([_as_halves(a) for a in own], "gather_first")
    started, after = {}, first[0]
    for g in _GROUP_ORDER[1:]:
        started[g] = _chips_start(groups[g], False, after, "gather_start_" + g)
        after = started[g]["token"]
    first = [_with_own(g, a, chip) for g, a in zip(first, own)]
    small_k = [_unpack(first[-1][k], small_shapes) for k in range(N_CHIPS)]
    p = {name: jnp.concatenate([small_k[k][i] for k in range(N_CHIPS)], axis=-1) for i, name in enumerate(_SMALL_SHARDED)}
    p = {k: (v if k == "norm_g" else v[0]) for k, v in p.items()}
    p["norm_g"] = p["norm_g"] + after[0, 0]
    for k in _SMALL_REPL:
        p[k] = w[k][0]

    def weights_for(group, after):
        if group == _GROUP_ORDER[0]:
            return _group_matrices(group, first[:-1])
        srcs, lands = _chips_wait(started[group], after, "gather_wait_" + group)
        return _group_matrices(group, [_with_own(l, a, chip) for l, a in zip(lands, srcs)])

    mine, theirs, pending = {}, {}, []

    def finish(group, names, scatter, after):
        pair_sum, got = _chips_wait(scatter, after, "reduce_wait_" + group)
        half_sum = [_sum_chips(a, b, chip, "chip_sum_" + n) for n, a, b in zip(names, pair_sum, got)]
        other = _swap_whole(half_sum, "gather_core_pair_" + group)
        mine.update(zip(names, half_sum))
        theirs.update(zip(names, other))

    def grads_ready(group, grads):
        names, halves, half_first = _group_grads(group, grads)
        from_sibling = _swap_halves(halves, half_first, "reduce_core_pair_" + group)
        pair_sum = [_add_own_half(h, o, cc, hf, "pair_sum_" + n) for n, h, o, hf in zip(names, halves, from_sibling, half_first)]
        scatter = _chips_start(pair_sum, True, from_sibling[0], "reduce_start_" + group)
        if pending:
            finish(*pending.pop(), scatter["token"])
        pending.append((group, names, scatter))
        return scatter["token"][0, 0]

    loss_part, grad_x, grads = _local_step(x[0], loss_target[0], p, weights_for, grads_ready)

    small_names = _SMALL_SHARDED + _SMALL_REPL
    small_grads = [grads[k] for k in small_names]
    full_shapes = [g.shape for g in small_grads] + [(1,)]
    pack = _pack(small_grads + [loss_part[0, :1]])
    summed = _sum_leading(_allgather_devices(pack, "gather_small"), "small_sum")
    finish(*pending.pop(), [summed] + list(theirs.values()))
    half_sum = [mine[n] for n in _REDUCED]
    other_half = [theirs[n] for n in _REDUCED]
    parts = _unpack(summed, full_shapes)
    loss = parts[-1][0]
    small_grad = {}
    for i, k in enumerate(small_names):
        g = parts[i]
        if k in _SMALL_SHARDED:
            n = w[k].shape[-1]
            g = lax.dynamic_slice_in_dim(g, chip * n, n, axis=g.ndim - 1)
        small_grad[k] = g

    grad = {**small_grad, **_shard_grads(half_sum, other_half, cc, w)}
    delta, new_m, new_v = {}, {}, {}
    for k in _BIG:
        delta[k], new_m[k], new_v[k] = _adamw(w[k], grad[k], mom[k], var[k], "adamw_" + k)
    shapes = [w[k].shape for k in small_names]
    d, nm, nv = _adamw(_pack([w[k] for k in small_names]), _pack([grad[k] for k in small_names]),
                       _pack([mom[k] for k in small_names]), _pack([var[k] for k in small_names]), "adamw_small")
    for k, a, b, c_ in zip(small_names, _unpack(d, shapes), _unpack(nm, shapes), _unpack(nv, shapes)):
        delta[k], new_m[k], new_v[k] = a, b, c_

    return (loss, grad_x[None], *[grad[k] for k in _WEIGHTS], *[delta[k] for k in _WEIGHTS],
            *[new_m[k] for k in _WEIGHTS], *[new_v[k] for k in _WEIGHTS])
```

```python
import functools
import math

import jax
import jax.numpy as jnp
from jax import lax
from jax.experimental import pallas as pl
from jax.experimental.pallas import tpu as pltpu

F32 = jnp.float32
MXU_DTYPE = jnp.bfloat16
COMM_DTYPE = jnp.bfloat16
HI = lax.Precision.HIGHEST
TRI_PREC = lax.Precision.HIGH

D_MODEL = 1024
D_FF = 2816
RMS_EPS = 1e-6
LN_EPS = 1e-5
L2_EPS = 1e-6
DN_HEADS = 8
DN_HEAD_DIM = 128
DN_CONV = 4
DN_CHUNK = 64
SG_WIDTH = 2048
SG_GROUPS = 8
SG_CHUNK = 128
SG_GROUP_W = SG_WIDTH // SG_GROUPS
N_CHIPS = 4
N_DEV = 8
LANES = 128
SUBLANES = 8
VMEM_LIMIT = 56 * 1024 * 1024

ADAM_LR = 0.001
ADAM_B1 = 0.9
ADAM_B2 = 0.999
ADAM_EPS = 1e-08
ADAM_WD = 0.01
ADAM_STEP = 10

MESH = pl.DeviceIdType.MESH
ANY = pl.BlockSpec(memory_space=pl.ANY)


def _cp(*sem):
    return pltpu.CompilerParams(dimension_semantics=sem, vmem_limit_bytes=VMEM_LIMIT)


def _pick(n, pref, mult=LANES):
    best = None
    d = mult
    while d <= min(n, pref):
        if n % d == 0:
            best = d
        d += mult
    return best if best is not None else n


def _full(shape):
    nd = len(shape)
    return pl.BlockSpec(shape, lambda *_: (0,) * nd)


def _sigmoid(x):
    return 1.0 / (1.0 + jnp.exp(-x))


def _dot(a, b, dims, prec=None):
    return lax.dot_general(a, b, (dims, ((), ())), preferred_element_type=F32, precision=prec)


NN = ((1,), (0,))
NT = ((1,), (1,))
TN = ((0,), (0,))


def _mx(a):
    return a.astype(MXU_DTYPE)


def _rms_stat(x):
    return lax.rsqrt(jnp.mean(x * x, axis=-1, keepdims=True) + RMS_EPS)


def _rms_bwd(x, r, g, dy):
    xh = x * r
    dxh = dy * g
    dx = r * (dxh - xh * jnp.mean(dxh * xh, axis=-1, keepdims=True))
    return dx, jnp.sum(dy * xh, axis=0, keepdims=True)


def _mm(a, b, mode, name, out_dtype=F32, add=None):
    if mode == "tn":
        K, M = a.shape
        N = b.shape[1]
    elif mode == "nt":
        M, K = a.shape
        N = b.shape[0]
    else:
        M, K = a.shape
        N = b.shape[1]
    tn = _pick(N, 1024)
    if mode == "tn":
        tm = _pick(M, 1024 if tn <= 512 else 1408)
        tk = _pick(K, 1024, SUBLANES)
    else:
        tm = _pick(M, max(512, min(2048, (512 * 1024) // tn)), SUBLANES)
        tk = _pick(K, 2048)
    nk = K // tk
    grid = (N // tn, M // tm, nk)
    if mode == "nn":
        a_spec = pl.BlockSpec((tm, tk), lambda j, i, k: (i, k))
        b_spec = pl.BlockSpec((tk, tn), lambda j, i, k: (k, j))
        dims = NN
    elif mode == "nt":
        a_spec = pl.BlockSpec((tm, tk), lambda j, i, k: (i, k))
        b_spec = pl.BlockSpec((tn, tk), lambda j, i, k: (j, k))
        dims = NT
    else:
        a_spec = pl.BlockSpec((tk, tm), lambda j, i, k: (k, i))
        b_spec = pl.BlockSpec((tk, tn), lambda j, i, k: (k, j))
        dims = TN
    o_spec = pl.BlockSpec((tm, tn), lambda j, i, k: (i, j))
    has_add = add is not None

    def body(*refs):
        if has_add:
            a_ref, b_ref, add_ref, o_ref, acc = refs
        else:
            a_ref, b_ref, o_ref, acc = refs
        k = pl.program_id(2)

        @pl.when(k == 0)
        def _():
            acc[...] = add_ref[...] if has_add else jnp.zeros_like(acc)

        acc[...] += _dot(a_ref[...], b_ref[...], dims)

        @pl.when(k == nk - 1)
        def _():
            o_ref[...] = acc[...].astype(o_ref.dtype)

    ins = [a, b] + ([add] if has_add else [])
    specs = [a_spec, b_spec] + ([o_spec] if has_add else [])
    return pl.pallas_call(
        body, name=name, grid=grid, in_specs=specs, out_specs=o_spec,
        out_shape=jax.ShapeDtypeStruct((M, N), out_dtype),
        scratch_shapes=[pltpu.VMEM((tm, tn), F32)],
        compiler_params=_cp("parallel", "parallel", "arbitrary"),
    )(*ins)


def _ffn_weight_operands(wt):
    return [_scalar(wt["chip"])] , [wt["gate"][0], wt["gate"][1], wt["up"][0], wt["up"][1], wt["down"][0], wt["down"][1]]


def _load_ffn_weights(chip_ref, shard_refs, wgu_v, wd_v, sem):
    fs = wd_v.shape[0] // N_CHIPS

    @pl.when(pl.program_id(0) == 0)
    def _():
        me = chip_ref[0]
        waits = []
        for t, (dst, base) in enumerate([(wgu_v, 0), (wgu_v, wd_v.shape[0]), (wd_v, 0)]):
            own, gathered = shard_refs[2 * t], shard_refs[2 * t + 1]
            for k in range(N_CHIPS):
                slot = dst.at[pl.ds(base + k * fs, fs), :]
                s = sem.at[t * N_CHIPS + k]

                @pl.when(me == k)
                def _(own=own, slot=slot, s=s):
                    pltpu.make_async_copy(own, slot, s).start()

                @pl.when(me != k)
                def _(gathered=gathered, k=k, slot=slot, s=s):
                    pltpu.make_async_copy(gathered.at[k], slot, s).start()

                waits.append(pltpu.make_async_copy(own, slot, s))
        for cp in waits:
            cp.wait()


def _ffn_fwd(x, g0, g1, wt, name):
    T, D = x.shape
    F = N_CHIPS * wt["down"][0].shape[0]
    F2 = 2 * F
    tm = _pick(T, 256, SUBLANES)
    prefetch, shards = _ffn_weight_operands(wt)

    def body(chip_ref, x_ref, g0_ref, g1_ref, *refs):
        shard_refs = refs[:6]
        xo_ref, h_ref, gu_ref, y_ref, wgu_v, wd_v, sem = refs[6:]
        _load_ffn_weights(chip_ref, shard_refs, wgu_v, wd_v, sem)
        xv = x_ref[...]
        hb = _mx(xv * _rms_stat(xv) * g0_ref[...])
        h_ref[...] = hb
        gu = _dot(hb, wgu_v[...], NT)
        gu_ref[...] = gu.astype(gu_ref.dtype)
        g = gu[:, :F]
        u = gu[:, F:]
        a = _mx(g * _sigmoid(g) * u)
        y = _dot(a, wd_v[...], NN)
        y_ref[...] = y
        xo_ref[...] = xv + 0.5 * (y * _rms_stat(y) * g1_ref[...])

    row = lambda w: pl.BlockSpec((tm, w), lambda i, c: (i, 0))
    one = pl.BlockSpec((1, D), lambda i, c: (0, 0))
    return pl.pallas_call(
        body, name=name,
        grid_spec=pltpu.PrefetchScalarGridSpec(
            num_scalar_prefetch=1, grid=(T // tm,),
            in_specs=[row(D), one, one] + [ANY] * 6,
            out_specs=[row(D), row(D), row(F2), row(D)],
            scratch_shapes=[pltpu.VMEM((F2, D), MXU_DTYPE), pltpu.VMEM((F, D), MXU_DTYPE),
                            pltpu.SemaphoreType.DMA((3 * N_CHIPS,))]),
        out_shape=[jax.ShapeDtypeStruct((T, D), F32), jax.ShapeDtypeStruct((T, D), MXU_DTYPE),
                   jax.ShapeDtypeStruct((T, F2), MXU_DTYPE), jax.ShapeDtypeStruct((T, D), F32)],
        compiler_params=_cp("arbitrary"),
    )(*prefetch, x, g0, g1, *shards)


FFN_BWD_CHUNK = 1408


def _ffn_bwd(dxo, x, y, gu, g0, g1, wt, name):
    T, D = x.shape
    F2 = gu.shape[1]
    F = F2 // 2
    tm = _pick(T, 256, SUBLANES)
    fc = _pick(F, FFN_BWD_CHUNK)
    prefetch, shards = _ffn_weight_operands(wt)

    def body(chip_ref, dxo_ref, x_ref, y_ref, gu_ref, g0_ref, g1_ref, *refs):
        shard_refs = refs[:6]
        dx_ref, dy_ref, a_ref, dgu_ref, dg0_ref, dg1_ref, wgu_v, wd_v, sem = refs[6:]
        _load_ffn_weights(chip_ref, shard_refs, wgu_v, wd_v, sem)

        @pl.when(pl.program_id(0) == 0)
        def _():
            dg0_ref[...] = jnp.zeros_like(dg0_ref)
            dg1_ref[...] = jnp.zeros_like(dg1_ref)

        dxo_v = dxo_ref[...]
        yv = y_ref[...]
        dy, dg1 = _rms_bwd(yv, _rms_stat(yv), g1_ref[...], 0.5 * dxo_v)
        dg1_ref[...] += dg1
        dyb = _mx(dy)
        dy_ref[...] = dyb
        dh = jnp.zeros((tm, D), F32)
        for c in range(F // fc):
            lo, hi = c * fc, (c + 1) * fc
            da = _dot(dyb, wd_v[lo:hi, :], NT)
            g = gu_ref[:, lo:hi].astype(F32)
            u = gu_ref[:, F + lo:F + hi].astype(F32)
            s = _sigmoid(g)
            sg = g * s
            a_ref[:, lo:hi] = _mx(sg * u)
            dg = _mx(da * u * (s * (1.0 + g * (1.0 - s))))
            du = _mx(da * sg)
            dgu_ref[:, lo:hi] = dg
            dgu_ref[:, F + lo:F + hi] = du
            dh = dh + _dot(dg, wgu_v[lo:hi, :], NN) + _dot(du, wgu_v[F + lo:F + hi, :], NN)
        xv = x_ref[...]
        dx, dg0 = _rms_bwd(xv, _rms_stat(xv), g0_ref[...], dh)
        dg0_ref[...] += dg0
        dx_ref[...] = dxo_v + dx

    row = lambda w: pl.BlockSpec((tm, w), lambda i, c: (i, 0))
    one = pl.BlockSpec((1, D), lambda i, c: (0, 0))
    return pl.pallas_call(
        body, name=name,
        grid_spec=pltpu.PrefetchScalarGridSpec(
            num_scalar_prefetch=1, grid=(T // tm,),
            in_specs=[row(D), row(D), row(D), row(F2), one, one] + [ANY] * 6,
            out_specs=[row(D), row(D), row(F), row(F2), one, one],
            scratch_shapes=[pltpu.VMEM((F2, D), MXU_DTYPE), pltpu.VMEM((F, D), MXU_DTYPE),
                            pltpu.SemaphoreType.DMA((3 * N_CHIPS,))]),
        out_shape=[jax.ShapeDtypeStruct((T, D), F32), jax.ShapeDtypeStruct((T, D), MXU_DTYPE),
                   jax.ShapeDtypeStruct((T, F), MXU_DTYPE), jax.ShapeDtypeStruct((T, F2), MXU_DTYPE),
                   jax.ShapeDtypeStruct((1, D), F32), jax.ShapeDtypeStruct((1, D), F32)],
        compiler_params=_cp("arbitrary"),
    )(*prefetch, dxo, x, y, gu, g0, g1, *shards)


def _norm_fwd(x, g, name):
    T, D = x.shape
    tm = _pick(T, 512, SUBLANES)

    def body(x_ref, g_ref, h_ref):
        xv = x_ref[...]
        h_ref[...] = _mx(xv * _rms_stat(xv) * g_ref[...])

    row = pl.BlockSpec((tm, D), lambda i: (i, 0))
    return pl.pallas_call(body, name=name, grid=(T // tm,), in_specs=[row, _full((1, D))], out_specs=row,
                          out_shape=jax.ShapeDtypeStruct((T, D), MXU_DTYPE), compiler_params=_cp("parallel"))(x, g)


def _postnorm_fwd(x, m, g, name):
    T, D = x.shape
    tm = _pick(T, 512, SUBLANES)

    def body(x_ref, m_ref, g_ref, o_ref):
        mv = m_ref[...]
        o_ref[...] = x_ref[...] + mv * _rms_stat(mv) * g_ref[...]

    row = pl.BlockSpec((tm, D), lambda i: (i, 0))
    return pl.pallas_call(body, name=name, grid=(T // tm,), in_specs=[row, row, _full((1, D))], out_specs=row,
                          out_shape=jax.ShapeDtypeStruct((T, D), F32), compiler_params=_cp("parallel"))(x, m, g)


def _postnorm_bwd(dxo, m, g, name):
    T, D = m.shape
    tm = _pick(T, 512, SUBLANES)

    def body(dxo_ref, m_ref, g_ref, dm_ref, dg_ref):
        @pl.when(pl.program_id(0) == 0)
        def _():
            dg_ref[...] = jnp.zeros_like(dg_ref)

        mv = m_ref[...]
        dm, dg = _rms_bwd(mv, _rms_stat(mv), g_ref[...], dxo_ref[...])
        dg_ref[...] += dg
        dm_ref[...] = _mx(dm)

    row = pl.BlockSpec((tm, D), lambda i: (i, 0))
    return pl.pallas_call(body, name=name, grid=(T // tm,), in_specs=[row, row, _full((1, D))],
                          out_specs=[row, _full((1, D))],
                          out_shape=[jax.ShapeDtypeStruct((T, D), MXU_DTYPE), jax.ShapeDtypeStruct((1, D), F32)],
                          compiler_params=_cp("arbitrary"))(dxo, m, g)


def _prenorm_bwd(dxo, dh, x, g, name):
    T, D = x.shape
    tm = _pick(T, 512, SUBLANES)

    def body(dxo_ref, dh_ref, x_ref, g_ref, dx_ref, dg_ref):
        @pl.when(pl.program_id(0) == 0)
        def _():
            dg_ref[...] = jnp.zeros_like(dg_ref)

        xv = x_ref[...]
        dx, dg = _rms_bwd(xv, _rms_stat(xv), g_ref[...], dh_ref[...])
        dg_ref[...] += dg
        dx_ref[...] = dxo_ref[...] + dx

    row = pl.BlockSpec((tm, D), lambda i: (i, 0))
    return pl.pallas_call(body, name=name, grid=(T // tm,), in_specs=[row, row, row, _full((1, D))],
                          out_specs=[row, _full((1, D))],
                          out_shape=[jax.ShapeDtypeStruct((T, D), F32), jax.ShapeDtypeStruct((1, D), F32)],
                          compiler_params=_cp("arbitrary"))(dxo, dh, x, g)


def _loss_fwd_bwd(y, target, name):
    T, D = y.shape
    tm = _pick(T, 512, SUBLANES)

    def body(y_ref, t_ref, l_ref, dy_ref):
        @pl.when(pl.program_id(0) == 0)
        def _():
            l_ref[...] = jnp.zeros_like(l_ref)

        e = y_ref[...] - t_ref[...]
        dy_ref[...] = e * (1.0 / D)
        l_ref[...] += 0.5 * jnp.sum(jnp.mean(e * e, axis=-1, keepdims=True), axis=0, keepdims=True)

    row = pl.BlockSpec((tm, D), lambda i: (i, 0))
    return pl.pallas_call(body, name=name, grid=(T // tm,), in_specs=[row, row],
                          out_specs=[_full((SUBLANES, LANES)), row],
                          out_shape=[jax.ShapeDtypeStruct((SUBLANES, LANES), F32), jax.ShapeDtypeStruct((T, D), F32)],
                          compiler_params=_cp("arbitrary"))(y, target)


DN_ROWS = 512


def _shift_down(prev8, cur, s):
    n = cur.shape[0]
    xx = jnp.concatenate([prev8, cur], axis=0)
    return pltpu.roll(xx, s, 0)[SUBLANES:SUBLANES + n, :]


def _shift_up(cur, next8, s):
    n = cur.shape[0]
    xx = jnp.concatenate([cur, next8], axis=0)
    return pltpu.roll(xx, n + SUBLANES - s, 0)[:n, :]


def _conv_tile(x_ref, w, r, rows):
    start = pl.multiple_of(r * rows, SUBLANES)
    cur = x_ref[pl.ds(start, rows), :]
    pstart = pl.multiple_of(jnp.maximum(start - SUBLANES, 0), SUBLANES)
    prev8 = jnp.where(r == 0, 0.0, x_ref[pl.ds(pstart, SUBLANES), :])
    taps = [_shift_down(prev8, cur, DN_CONV - 1 - j) if j < DN_CONV - 1 else cur for j in range(DN_CONV)]
    c = taps[0] * w[0:1, :]
    for j in range(1, DN_CONV):
        c = c + taps[j] * w[j:j + 1, :]
    return c, taps


def _dn_prep_fwd(proj, conv_w, name):
    T = proj.shape[0]
    W = DN_HEADS * DN_HEAD_DIM
    rows = min(DN_ROWS, T)
    n_inner = T // rows
    scale = DN_HEAD_DIM ** -0.5

    def body(x_ref, w_ref, o_ref):
        cb = pl.program_id(0)
        w = w_ref[...]
        is_qk = cb < 2 * DN_HEADS
        post = jnp.where(cb < DN_HEADS, scale, 1.0)

        def step(r, carry):
            c, _ = _conv_tile(x_ref, w, r, rows)
            s = c * _sigmoid(c)
            rinv = lax.rsqrt(jnp.sum(s * s, axis=-1, keepdims=True) + L2_EPS)
            o_ref[pl.ds(pl.multiple_of(r * rows, SUBLANES), rows), :] = jnp.where(is_qk, s * rinv * post, s)
            return carry

        lax.fori_loop(0, n_inner, step, 0)

    col = pl.BlockSpec((T, LANES), lambda j: (0, j))
    return pl.pallas_call(body, name=name, grid=(3 * W // LANES,),
                          in_specs=[col, pl.BlockSpec((DN_CONV, LANES), lambda j: (0, j))], out_specs=col,
                          out_shape=jax.ShapeDtypeStruct((T, 3 * W), F32), compiler_params=_cp("parallel"))(proj, conv_w)


def _dn_prep_bwd(proj, conv_w, dqkv, name):
    T = proj.shape[0]
    W = DN_HEADS * DN_HEAD_DIM
    rows = min(DN_ROWS, T)
    n_inner = T // rows
    scale = DN_HEAD_DIM ** -0.5

    def body(x_ref, w_ref, dy_ref, dx_ref, dw_ref, dc_scr):
        cb = pl.program_id(0)
        w = w_ref[...]
        is_qk = cb < 2 * DN_HEADS
        post = jnp.where(cb < DN_HEADS, scale, 1.0)

        def step1(r, dws):
            c, taps = _conv_tile(x_ref, w, r, rows)
            sg = _sigmoid(c)
            s = c * sg
            rinv = lax.rsqrt(jnp.sum(s * s, axis=-1, keepdims=True) + L2_EPS)
            dy = dy_ref[pl.ds(pl.multiple_of(r * rows, SUBLANES), rows), :]
            yn = s * rinv
            dyn = dy * post
            ds_qk = rinv * (dyn - yn * jnp.sum(dyn * yn, axis=-1, keepdims=True))
            ds = jnp.where(is_qk, ds_qk, dy)
            dc = ds * (sg * (1.0 + c * (1.0 - sg)))
            dc_scr[pl.ds(pl.multiple_of(r * rows, SUBLANES), rows), :] = dc
            return tuple(dws[j] + jnp.sum(dc * taps[j], axis=0, keepdims=True) for j in range(DN_CONV))

        zero = jnp.zeros((1, LANES), F32)
        dws = lax.fori_loop(0, n_inner, step1, (zero,) * DN_CONV)
        for j in range(DN_CONV):
            dw_ref[j:j + 1, :] = dws[j]

        def step2(r, carry):
            start = pl.multiple_of(r * rows, SUBLANES)
            cur = dc_scr[pl.ds(start, rows), :]
            nstart = pl.multiple_of(jnp.minimum(start + rows, T - SUBLANES), SUBLANES)
            next8 = jnp.where(r == n_inner - 1, 0.0, dc_scr[pl.ds(nstart, SUBLANES), :])
            dx = cur * w[DN_CONV - 1:DN_CONV, :]
            for j in range(DN_CONV - 1):
                dx = dx + _shift_up(cur, next8, DN_CONV - 1 - j) * w[j:j + 1, :]
            dx_ref[pl.ds(start, rows), :] = _mx(dx)
            return carry

        lax.fori_loop(0, n_inner, step2, 0)

    col = pl.BlockSpec((T, LANES), lambda j: (0, j))
    wspec = pl.BlockSpec((DN_CONV, LANES), lambda j: (0, j))
    return pl.pallas_call(body, name=name, grid=(3 * W // LANES,), in_specs=[col, wspec, col], out_specs=[col, wspec],
                          out_shape=[jax.ShapeDtypeStruct((T, 3 * W), MXU_DTYPE), jax.ShapeDtypeStruct((DN_CONV, 3 * W), F32)],
                          scratch_shapes=[pltpu.VMEM((T, LANES), F32)], compiler_params=_cp("parallel"))(proj, conv_w, dqkv)


def _softplus(x):
    return jnp.maximum(x, 0.0) + jnp.log(1.0 + jnp.exp(-jnp.abs(x)))


def _dn_gate_fwd(ba, a_log, dt_bias, name):
    T = ba.shape[0]
    tm = _pick(T, 1024, SUBLANES)

    def body(ba_ref, al_ref, dt_ref, beta_ref, g_ref):
        beta_ref[...] = _sigmoid(ba_ref[:, :LANES])
        g_ref[...] = -jnp.exp(al_ref[...]) * _softplus(ba_ref[:, LANES:] + dt_ref[...])

    row = lambda w: pl.BlockSpec((tm, w), lambda i: (i, 0))
    return pl.pallas_call(body, name=name, grid=(T // tm,), in_specs=[row(2 * LANES), _full((1, LANES)), _full((1, LANES))],
                          out_specs=[row(LANES), row(LANES)],
                          out_shape=[jax.ShapeDtypeStruct((T, LANES), F32)] * 2, compiler_params=_cp("parallel"))(ba, a_log, dt_bias)


def _dn_gate_bwd(ba, a_log, dt_bias, dbeta, dg, name):
    T = ba.shape[0]
    tm = _pick(T, 1024, SUBLANES)

    def body(ba_ref, al_ref, dt_ref, dbeta_ref, dg_ref, dba_ref, dal_ref, ddt_ref):
        @pl.when(pl.program_id(0) == 0)
        def _():
            dal_ref[...] = jnp.zeros_like(dal_ref)
            ddt_ref[...] = jnp.zeros_like(ddt_ref)

        beta = _sigmoid(ba_ref[:, :LANES])
        dba_ref[:, :LANES] = _mx(dbeta_ref[...] * beta * (1.0 - beta))
        pre = ba_ref[:, LANES:] + dt_ref[...]
        ea = jnp.exp(al_ref[...])
        dgv = dg_ref[...]
        da = dgv * (-ea) * _sigmoid(pre)
        dba_ref[:, LANES:] = _mx(da)
        ddt_ref[...] += jnp.sum(da, axis=0, keepdims=True)
        dal_ref[...] += jnp.sum(dgv * (-ea) * _softplus(pre), axis=0, keepdims=True)

    row = lambda w: pl.BlockSpec((tm, w), lambda i: (i, 0))
    one = _full((1, LANES))
    return pl.pallas_call(body, name=name, grid=(T // tm,), in_specs=[row(2 * LANES), one, one, row(LANES), row(LANES)],
                          out_specs=[row(2 * LANES), one, one],
                          out_shape=[jax.ShapeDtypeStruct((T, 2 * LANES), MXU_DTYPE), jax.ShapeDtypeStruct((1, LANES), F32),
                                     jax.ShapeDtypeStruct((1, LANES), F32)],
                          compiler_params=_cp("arbitrary"))(ba, a_log, dt_bias, dbeta, dg)


def _tri(c, strict):
    i = lax.broadcasted_iota(jnp.int32, (c, c), 0)
    j = lax.broadcasted_iota(jnp.int32, (c, c), 1)
    return (i > j) if strict else (i >= j)


def _inv_unit_lower(ls):
    c = ls[0].shape[0]
    i = lax.broadcasted_iota(jnp.int32, (c, c), 0)
    j = lax.broadcasted_iota(jnp.int32, (c, c), 1)
    eye = jnp.where(i == j, 1.0, 0.0)
    facs = [[eye - l for l in ls]]
    cur = ls
    for _ in range(int(math.log2(c)) - 1):
        cur = [_dot(p, p, NN, TRI_PREC) for p in cur]
        facs.append([eye + p for p in cur])
    while len(facs) > 1:
        nxt = [[_dot(a, b, NN, TRI_PREC) for a, b in zip(facs[t], facs[t + 1])] for t in range(0, len(facs) - 1, 2)]
        if len(facs) % 2:
            nxt.append(facs[-1])
        facs = nxt
    return facs[0]


def _chunk_gates(g_blk):
    c = g_blk.shape[0]
    gcs = _dot(jnp.where(_tri(c, False), 1.0, 0.0), g_blk, NN, HI)
    return gcs, gcs.T


def _head_chunk(h, qh, kh, vh, beta_blk, gcs, gcs_t):
    c = qh.shape[0]
    incl = _tri(c, False)
    gc_col = gcs[:, h:h + 1]
    gc_row = gcs_t[h:h + 1, :]
    gc_last = gcs_t[h:h + 1, c - 1:c]
    dec = jnp.where(incl, jnp.exp(jnp.where(incl, gc_col - gc_row, 0.0)), 0.0)
    gam = jnp.exp(gc_col)
    rr = jnp.exp(gc_last - gc_col)
    gl = jnp.exp(gc_last)
    b = beta_blk[:, h:h + 1]
    kb = kh * b
    vb = vh * b
    kk = _dot(_mx(kb), _mx(kh), NT)
    lmat = jnp.where(_tri(c, True), kk * dec, 0.0)
    qk = _dot(_mx(qh), _mx(kh), NT)
    pmat = jnp.where(incl, qk * dec, 0.0)
    return dict(dec=dec, gam=gam, rr=rr, gl=gl, b=b, kb=kb, vb=vb, lmat=lmat, pmat=pmat)


def _dn_scan_fwd(qkv, beta, g, proj, norm_g, name):
    T = qkv.shape[0]
    C, H, Dh = DN_CHUNK, DN_HEADS, DN_HEAD_DIM
    W = H * Dh
    N = T // C

    def body(q_ref, k_ref, v_ref, beta_ref, g_ref, z_ref, ng_ref, og_ref, o_ref, tinv_ref, s_ref, state):
        @pl.when(pl.program_id(0) == 0)
        def _():
            state[...] = jnp.zeros_like(state)

        gcs, gcs_t = _chunk_gates(g_ref[...])
        beta_blk = beta_ref[...]
        ng = ng_ref[...]
        heads = range(H)
        cs = [slice(h * Dh, (h + 1) * Dh) for h in heads]
        qs = [_head_chunk(h, q_ref[:, cs[h]], k_ref[:, cs[h]], v_ref[:, cs[h]], beta_blk, gcs, gcs_t) for h in heads]
        tinvs = _inv_unit_lower([q["lmat"] for q in qs])
        for h in heads:
            tinv_ref[h] = tinvs[h]
        us = [_dot(tinvs[h], qs[h]["vb"], NN, TRI_PREC) for h in heads]
        ws = [_dot(tinvs[h], qs[h]["kb"] * qs[h]["gam"], NN, TRI_PREC) for h in heads]
        ss = [state[h] for h in heads]
        for h in heads:
            s_ref[0, h] = ss[h]
        sbs = [_mx(s) for s in ss]
        vnbs = [_mx(us[h] - _dot(_mx(ws[h]), sbs[h], NN)) for h in heads]
        os_ = [_dot(_mx(q_ref[:, cs[h]] * qs[h]["gam"]), sbs[h], NN) + _dot(_mx(qs[h]["pmat"]), vnbs[h], NN) for h in heads]
        for h in heads:
            state[h] = ss[h] * qs[h]["gl"] + _dot(_mx((k_ref[:, cs[h]] * qs[h]["rr"]).T), vnbs[h], NN)
        for h in heads:
            o = os_[h]
            o_ref[:, cs[h]] = o
            zh = z_ref[:, cs[h]]
            og_ref[:, cs[h]] = _mx(o * _rms_stat(o) * ng * (zh * _sigmoid(zh)))

    blk = lambda j: pl.BlockSpec((C, W), lambda n: (n, j))
    small = pl.BlockSpec((C, LANES), lambda n: (n, 0))
    return pl.pallas_call(
        body, name=name, grid=(N,),
        in_specs=[blk(0), blk(1), blk(2), small, small, blk(3), _full((1, Dh))],
        out_specs=[blk(0), blk(0), pl.BlockSpec((H, C, C), lambda n: (0, n, 0)),
                   pl.BlockSpec((1, H, Dh, Dh), lambda n: (n, 0, 0, 0))],
        out_shape=[jax.ShapeDtypeStruct((T, W), MXU_DTYPE), jax.ShapeDtypeStruct((T, W), F32),
                   jax.ShapeDtypeStruct((H, T, C), F32), jax.ShapeDtypeStruct((N, H, Dh, Dh), F32)],
        scratch_shapes=[pltpu.VMEM((H, Dh, Dh), F32)],
        compiler_params=_cp("arbitrary"),
    )(qkv, qkv, qkv, beta, g, proj, norm_g)


def _dn_scan_bwd(qkv, beta, g, proj, norm_g, o, tinv, s_all, dog, name):
    T = qkv.shape[0]
    C, H, Dh = DN_CHUNK, DN_HEADS, DN_HEAD_DIM
    W = H * Dh
    N = T // C

    def body(q_ref, k_ref, v_ref, beta_ref, g_ref, z_ref, ng_ref, o_ref, tinv_ref, s_ref, dog_ref,
             dqkv_ref, dbeta_ref, dg_ref, dz_ref, dng_ref, dstate):
        @pl.when(pl.program_id(0) == 0)
        def _():
            dstate[...] = jnp.zeros_like(dstate)
            dng_ref[...] = jnp.zeros_like(dng_ref)

        gcs, gcs_t = _chunk_gates(g_ref[...])
        beta_blk = beta_ref[...]
        ng = ng_ref[...]
        incl = _tri(C, False)
        strict = _tri(C, True)
        lane = lax.broadcasted_iota(jnp.int32, (C, LANES), 1)
        rowi = lax.broadcasted_iota(jnp.int32, (C, 1), 0)
        ones = jnp.ones((C, LANES), F32)
        dbeta_acc = jnp.zeros((C, LANES), F32)
        dgc_acc = jnp.zeros((C, LANES), F32)
        dng_acc = jnp.zeros((1, Dh), F32)
        heads = range(H)
        cs = [slice(h * Dh, (h + 1) * Dh) for h in heads]
        rsum = lambda t: jnp.sum(t, axis=1, keepdims=True)
        dobs = []
        for h in heads:
            oh, zh, dogh = o_ref[:, cs[h]], z_ref[:, cs[h]], dog_ref[:, cs[h]]
            rstat = _rms_stat(oh)
            sz = _sigmoid(zh)
            dz_ref[:, cs[h]] = _mx(dogh * (oh * rstat * ng) * (sz * (1.0 + zh * (1.0 - sz))))
            do, dng = _rms_bwd(oh, rstat, ng, dogh * (zh * sz))
            dng_acc = dng_acc + dng
            dobs.append(_mx(do))
        qs = [_head_chunk(h, q_ref[:, cs[h]], k_ref[:, cs[h]], v_ref[:, cs[h]], beta_blk, gcs, gcs_t) for h in heads]
        tms = [tinv_ref[h] for h in heads]
        us = [_dot(tms[h], qs[h]["vb"], NN, TRI_PREC) for h in heads]
        ws = [_dot(tms[h], qs[h]["kb"] * qs[h]["gam"], NN, TRI_PREC) for h in heads]
        ss = [s_ref[0, h] for h in heads]
        sbs = [_mx(s) for s in ss]
        wbs = [_mx(w) for w in ws]
        vnbs = [_mx(us[h] - _dot(wbs[h], sbs[h], NN)) for h in heads]
        dsns = [dstate[h] for h in heads]
        dsbs = [_mx(d) for d in dsns]
        dvnews = [_dot(_mx(qs[h]["pmat"]), dobs[h], TN) + _dot(_mx(k_ref[:, cs[h]] * qs[h]["rr"]), dsbs[h], NN) for h in heads]
        dvb16s = [_mx(d) for d in dvnews]
        dps = [jnp.where(incl, _dot(dobs[h], vnbs[h], NT), 0.0) for h in heads]
        dqds = [_dot(dobs[h], sbs[h], NT) for h in heads]
        dkds = [_dot(vnbs[h], dsbs[h], NT) for h in heads]
        dgls = [jnp.sum(rsum(ss[h] * dsns[h]), axis=0, keepdims=True) for h in heads]
        dws = [-_dot(dvb16s[h], sbs[h], NT) for h in heads]
        for h in heads:
            dstate[h] = (_dot(_mx(q_ref[:, cs[h]] * qs[h]["gam"]), dobs[h], TN) + qs[h]["gl"] * dsns[h]
                         - _dot(wbs[h], dvb16s[h], TN))
        dvbs = [_dot(tms[h], dvnews[h], TN, TRI_PREC) for h in heads]
        dkbgs = [_dot(tms[h], dws[h], TN, TRI_PREC) for h in heads]
        dls = [jnp.where(strict, -(_dot(dvbs[h], us[h], NT, TRI_PREC) + _dot(dkbgs[h], ws[h], NT, TRI_PREC)), 0.0)
               for h in heads]
        mmats = [dls[h] * qs[h]["lmat"] + dps[h] * qs[h]["pmat"] for h in heads]
        dgcs = [rsum(mmats[h]) - _dot(mmats[h], ones, TN, HI)[:, :1] for h in heads]
        dkk16s = [_mx(dls[h] * qs[h]["dec"]) for h in heads]
        dqk16s = [_mx(dps[h] * qs[h]["dec"]) for h in heads]
        for h in heads:
            q = qs[h]
            qh, kh, vh = q_ref[:, cs[h]], k_ref[:, cs[h]], v_ref[:, cs[h]]
            gam, rr, b, kb = q["gam"], q["rr"], q["b"], q["kb"]
            dkb = _dot(dkk16s[h], _mx(kh), NN) + dkbgs[h] * gam
            dk = _dot(dkk16s[h], _mx(kb), TN) + _dot(dqk16s[h], _mx(qh), TN) + dkb * b + dkds[h] * rr
            dq = _dot(dqk16s[h], _mx(kh), NN) + dqds[h] * gam
            dgam = rsum(dkbgs[h] * kb) + rsum(dqds[h] * qh)
            dr = rsum(dkds[h] * kh)
            dgc_last = jnp.sum(dr * rr, axis=0, keepdims=True) + dgls[h] * q["gl"]
            dgc = dgcs[h] + dgam * gam - dr * rr + jnp.where(rowi == C - 1, dgc_last, 0.0)
            dbeta = rsum(dvbs[h] * vh) + rsum(dkb * kh)
            dqkv_ref[:, cs[h]] = dq
            dqkv_ref[:, W + h * Dh:W + (h + 1) * Dh] = dk
            dqkv_ref[:, 2 * W + h * Dh:2 * W + (h + 1) * Dh] = dvbs[h] * b
            dbeta_acc = jnp.where(lane == h, dbeta, dbeta_acc)
            dgc_acc = jnp.where(lane == h, dgc, dgc_acc)
        dbeta_ref[...] = dbeta_acc
        dg_ref[...] = _dot(jnp.where(incl, 1.0, 0.0), dgc_acc, TN, HI)
        dng_ref[...] += dng_acc

    rev = lambda n: N - 1 - n
    blk = lambda j: pl.BlockSpec((C, W), lambda n: (rev(n), j))
    small = pl.BlockSpec((C, LANES), lambda n: (rev(n), 0))
    return pl.pallas_call(
        body, name=name, grid=(N,),
        in_specs=[blk(0), blk(1), blk(2), small, small, blk(3), _full((1, Dh)), blk(0),
                  pl.BlockSpec((H, C, C), lambda n: (0, rev(n), 0)),
                  pl.BlockSpec((1, H, Dh, Dh), lambda n: (rev(n), 0, 0, 0)), blk(0)],
        out_specs=[pl.BlockSpec((C, 3 * W), lambda n: (rev(n), 0)), small, small, blk(0), _full((1, Dh))],
        out_shape=[jax.ShapeDtypeStruct((T, 3 * W), F32), jax.ShapeDtypeStruct((T, LANES), F32),
                   jax.ShapeDtypeStruct((T, LANES), F32), jax.ShapeDtypeStruct((T, W), MXU_DTYPE),
                   jax.ShapeDtypeStruct((1, Dh), F32)],
        scratch_shapes=[pltpu.VMEM((H, Dh, Dh), F32)],
        compiler_params=_cp("arbitrary"),
    )(qkv, qkv, qkv, beta, g, proj, norm_g, o, tinv, s_all, dog)


_INV_SQRT2 = 0.7071067811865476
_INV_SQRT_2PI = 0.3989422804014327


def _sg_recompute(zp_ref, bin_ref, lng_ref, lnb_ref):
    E = SG_WIDTH
    zin = zp_ref[...] + bin_ref[...]
    cdf = 0.5 * (1.0 + lax.erf(zin * _INV_SQRT2))
    zz = zin * cdf
    u = zz[:, :E]
    vp = zz[:, E:]
    mu = jnp.mean(vp, axis=-1, keepdims=True)
    xc = vp - mu
    rstd = lax.rsqrt(jnp.mean(xc * xc, axis=-1, keepdims=True) + LN_EPS)
    xhat = xc * rstd
    v = xhat * lng_ref[...] + lnb_ref[...]
    return zin, cdf, u, xhat, rstd, v


def _sg_masked_ws(ws_ref, g):
    return _mx(jnp.where(_tri(SG_CHUNK, False), ws_ref[g], 0.0))


def _sg_fwd(zpre, b_in, ln_g, ln_b, w_s, b_s_t, name):
    T = zpre.shape[0]
    E, G, C, GW = SG_WIDTH, SG_GROUPS, SG_CHUNK, SG_GROUP_W

    def body(zp_ref, bin_ref, lng_ref, lnb_ref, ws_ref, bst_ref, um_ref):
        _, _, u, _, _, v = _sg_recompute(zp_ref, bin_ref, lng_ref, lnb_ref)
        bst = bst_ref[...]
        for g in range(G):
            cs = slice(g * GW, (g + 1) * GW)
            mixed = _dot(_sg_masked_ws(ws_ref, g), _mx(v[:, cs]), NN) + bst[:, g:g + 1]
            um_ref[:, cs] = _mx(u[:, cs] * mixed)

    return pl.pallas_call(
        body, name=name, grid=(T // C,),
        in_specs=[pl.BlockSpec((C, 2 * E), lambda n: (n, 0)), _full((1, 2 * E)), _full((1, E)), _full((1, E)),
                  _full((G, C, C)), _full((C, LANES))],
        out_specs=pl.BlockSpec((C, E), lambda n: (n, 0)),
        out_shape=jax.ShapeDtypeStruct((T, E), MXU_DTYPE), compiler_params=_cp("parallel"),
    )(zpre, b_in, ln_g, ln_b, w_s, b_s_t)


def _sg_bwd(zpre, b_in, ln_g, ln_b, w_s, b_s_t, dum, name):
    T = zpre.shape[0]
    E, G, C, GW = SG_WIDTH, SG_GROUPS, SG_CHUNK, SG_GROUP_W

    def body(zp_ref, bin_ref, lng_ref, lnb_ref, ws_ref, bst_ref, dum_ref,
             dz_ref, dbin_ref, dlng_ref, dlnb_ref, dws_ref, dbst_ref):
        @pl.when(pl.program_id(0) == 0)
        def _():
            for r in (dbin_ref, dlng_ref, dlnb_ref, dws_ref, dbst_ref):
                r[...] = jnp.zeros_like(r)

        zin, cdf, u, xhat, rstd, v = _sg_recompute(zp_ref, bin_ref, lng_ref, lnb_ref)
        bst = bst_ref[...]
        lane = lax.broadcasted_iota(jnp.int32, (C, LANES), 1)
        dum_v = dum_ref[...]
        dbst = jnp.zeros((C, LANES), F32)
        du_parts, dv_parts = [], []
        for g in range(G):
            cs = slice(g * GW, (g + 1) * GW)
            wsm = _sg_masked_ws(ws_ref, g)
            vg = _mx(v[:, cs])
            mixed = _dot(wsm, vg, NN) + bst[:, g:g + 1]
            dumg = dum_v[:, cs]
            du_parts.append(dumg * mixed)
            dmixed = dumg * u[:, cs]
            dmb = _mx(dmixed)
            dv_parts.append(_dot(wsm, dmb, TN))
            dws_ref[g] += _dot(dmb, vg, NT)
            dbst = jnp.where(lane == g, jnp.sum(dmixed, axis=1, keepdims=True), dbst)
        dbst_ref[...] += dbst
        du = jnp.concatenate(du_parts, axis=1)
        dv = jnp.concatenate(dv_parts, axis=1)
        dlng_ref[...] += jnp.sum(dv * xhat, axis=0, keepdims=True)
        dlnb_ref[...] += jnp.sum(dv, axis=0, keepdims=True)
        dxh = dv * lng_ref[...]
        dvp = rstd * (dxh - jnp.mean(dxh, axis=-1, keepdims=True) - xhat * jnp.mean(dxh * xhat, axis=-1, keepdims=True))
        dzz = jnp.concatenate([du, dvp], axis=1)
        dzin = dzz * (cdf + zin * (_INV_SQRT_2PI * jnp.exp(-0.5 * zin * zin)))
        dz_ref[...] = _mx(dzin)
        dbin_ref[...] += jnp.sum(dzin, axis=0, keepdims=True)

    return pl.pallas_call(
        body, name=name, grid=(T // C,),
        in_specs=[pl.BlockSpec((C, 2 * E), lambda n: (n, 0)), _full((1, 2 * E)), _full((1, E)), _full((1, E)),
                  _full((G, C, C)), _full((C, LANES)), pl.BlockSpec((C, E), lambda n: (n, 0))],
        out_specs=[pl.BlockSpec((C, 2 * E), lambda n: (n, 0)), _full((1, 2 * E)), _full((1, E)), _full((1, E)),
                   _full((G, C, C)), _full((C, LANES))],
        out_shape=[jax.ShapeDtypeStruct((T, 2 * E), MXU_DTYPE), jax.ShapeDtypeStruct((1, 2 * E), F32),
                   jax.ShapeDtypeStruct((1, E), F32), jax.ShapeDtypeStruct((1, E), F32),
                   jax.ShapeDtypeStruct((G, C, C), F32), jax.ShapeDtypeStruct((C, LANES), F32)],
        compiler_params=_cp("arbitrary"),
    )(zpre, b_in, ln_g, ln_b, w_s, b_s_t, dum)


def _row(v):
    return v.reshape(1, -1)


def _pad_lanes(v):
    v = v.reshape(1, -1)
    return jnp.pad(v, ((0, 0), (0, LANES - v.shape[1])))


def _local_step(x, target, p, weights_for, grads_ready=None):
    ng = p["norm_g"]
    grads = {}
    dng = [[None] * 6 for _ in range(2)]
    order = [jnp.zeros((), F32)]

    def tell(group):
        zero = grads_ready(group, grads) if grads_ready is not None else None
        if zero is not None:
            order[0] = zero

    def gain(i, s):
        return _row(ng[i, s]) + order[0]

    def ffn_f(xin, i, j, tag):
        wt = weights_for("ffn" + tag, xin)
        xo, h, gu, y = _ffn_fwd(xin, _row(ng[i, 4 * j]), _row(ng[i, 4 * j + 1]), wt, "ffn_fwd_" + tag)
        return xo, (xin, h, gu, y, wt)

    x1, sv_f00 = ffn_f(x, 0, 0, "00")
    dnw = weights_for("dn", x1)
    hn0 = _norm_fwd(x1, _row(ng[0, 2]), "dn_prenorm")
    proj = _mm(hn0, dnw["dn_wqkvz"], "nn", "dn_proj")
    ba = _mm(hn0, dnw["dn_wba"], "nn", "dn_proj_ba")
    a_log = _pad_lanes(p["dn_a_log"])
    dt_bias = _pad_lanes(p["dn_dt_bias"])
    dn_ng = _row(p["dn_norm_g"])
    qkv = _dn_prep_fwd(proj, p["dn_conv_w"], "dn_prep_fwd")
    beta, gdec = _dn_gate_fwd(ba, a_log, dt_bias, "dn_gate_fwd")
    og, o_raw, tinv, s_all = _dn_scan_fwd(qkv, beta, gdec, proj, dn_ng, "dn_scan_fwd")
    m0 = _mm(og, dnw["dn_wout"], "nn", "dn_out")
    x2 = _postnorm_fwd(x1, m0, _row(ng[0, 3]), "dn_postnorm")
    x3, sv_f01 = ffn_f(x2, 0, 1, "01")
    x4, sv_f10 = ffn_f(x3, 1, 0, "10")
    sgw = weights_for("sg", x4)
    hn1 = _norm_fwd(x4, _row(ng[1, 2]), "sg_prenorm")
    zpre = _mm(hn1, sgw["sg_win"], "nn", "sg_proj")
    sg_bin = _row(p["sg_b_in"])
    sg_lng = _row(p["sg_ln_g"])
    sg_lnb = _row(p["sg_ln_b"])
    sg_bst = jnp.pad(p["sg_b_s"].T, ((0, 0), (0, LANES - SG_GROUPS)))
    um = _sg_fwd(zpre, sg_bin, sg_lng, sg_lnb, p["sg_w_s"], sg_bst, "sg_fwd")
    m1 = _mm(um, sgw["sg_wout"], "nn", "sg_out")
    x5 = _postnorm_fwd(x4, m1, _row(ng[1, 3]), "sg_postnorm")
    x6, sv_f11 = ffn_f(x5, 1, 1, "11")
    loss_part, dx = _loss_fwd_bwd(x6, target, "loss")

    def ffn_b(dxo, sv, i, j, tag):
        xin, h, gu, y, wt = sv
        dxi, dy, a, dgu, dg0, dg1 = _ffn_bwd(dxo, xin, y, gu, gain(i, 4 * j), gain(i, 4 * j + 1), wt, "ffn_bwd_" + tag)
        grads["wd" + tag] = _mm(a, dy, "tn", "ffn_wgrad_down_" + tag)
        grads["wguT" + tag] = _mm(dgu, h, "tn", "ffn_wgrad_up_" + tag)
        tell("ffn" + tag)
        dng[i][4 * j] = dg0
        dng[i][4 * j + 1] = dg1
        return dxi

    dx = ffn_b(dx, sv_f11, 1, 1, "11")
    dm1, dng[1][3] = _postnorm_bwd(dx, m1, gain(1, 3), "sg_postnorm_bwd")
    grads["sg_w_out"] = _mm(um, dm1, "tn", "sg_wgrad_out")
    dum = _mm(dm1, sgw["sg_wout"], "nt", "sg_dgrad_out")
    dz1, dbin, dlng, dlnb, dws, dbst = _sg_bwd(zpre, sg_bin, sg_lng, sg_lnb, p["sg_w_s"], sg_bst, dum, "sg_bwd")
    grads["sg_w_inT"] = _mm(dz1, hn1, "tn", "sg_wgrad_in")
    tell("sg")
    dh1 = _mm(dz1, sgw["sg_win"], "nt", "sg_dgrad_in")
    dx, dng[1][2] = _prenorm_bwd(dx, dh1, x4, gain(1, 2), "sg_prenorm_bwd")
    grads["sg_b_in"] = dbin.reshape(1, -1)
    grads["sg_ln_g"] = dlng.reshape(1, -1)
    grads["sg_ln_b"] = dlnb.reshape(1, -1)
    grads["sg_w_s"] = jnp.where(jnp.tril(jnp.ones((SG_CHUNK, SG_CHUNK), bool)), dws, 0.0)[None]
    grads["sg_b_s"] = dbst[:, :SG_GROUPS].T[None]
    dx = ffn_b(dx, sv_f10, 1, 0, "10")
    dx = ffn_b(dx, sv_f01, 0, 1, "01")
    dm0, dng[0][3] = _postnorm_bwd(dx, m0, gain(0, 3), "dn_postnorm_bwd")
    grads["dn_w_out"] = _mm(og, dm0, "tn", "dn_wgrad_out")
    dog = _mm(dm0, dnw["dn_wout"], "nt", "dn_dgrad_out")
    dqkv, dbeta, dgdec, dz0, dnng = _dn_scan_bwd(qkv, beta, gdec, proj, dn_ng, o_raw, tinv, s_all, dog, "dn_scan_bwd")
    dqkv_pre, dconv = _dn_prep_bwd(proj, p["dn_conv_w"], dqkv, "dn_prep_bwd")
    dba, dal, ddt = _dn_gate_bwd(ba, a_log, dt_bias, dbeta, dgdec, "dn_gate_bwd")
    W3 = 3 * DN_HEADS * DN_HEAD_DIM
    dw_qkv = _mm(hn0, dqkv_pre, "tn", "dn_wgrad_qkv")
    dw_z = _mm(hn0, dz0, "tn", "dn_wgrad_z")
    dw_ba = _mm(hn0, dba, "tn", "dn_wgrad_ba")
    grads["dn_w_in"] = jnp.concatenate(
        [dw_qkv, dw_z, dw_ba[:, :DN_HEADS], dw_ba[:, LANES:LANES + DN_HEADS]], axis=1)
    tell("dn")
    dh0 = _mm(dqkv_pre, dnw["dn_wqkvz"][:, :W3], "nt", "dn_dgrad_qkv")
    dh0 = _mm(dz0, dnw["dn_wqkvz"][:, W3:], "nt", "dn_dgrad_z", add=dh0)
    dh0 = _mm(dba, dnw["dn_wba"], "nt", "dn_dgrad_ba", add=dh0)
    dx, dng[0][2] = _prenorm_bwd(dx, dh0, x1, gain(0, 2), "dn_prenorm_bwd")
    grads["dn_conv_w"] = dconv[None]
    grads["dn_a_log"] = dal[:, :DN_HEADS]
    grads["dn_dt_bias"] = ddt[:, :DN_HEADS]
    grads["dn_norm_g"] = dnng
    dx = ffn_b(dx, sv_f00, 0, 0, "00")
    grads["norm_g"] = jnp.stack([jnp.concatenate(dng[i], axis=0) for i in range(2)])
    return loss_part + order[0], dx, grads


def _mesh_pos():
    return lax.axis_index("x"), lax.axis_index("y"), lax.axis_index("c")


def _other_chips(x, y):
    return [(1 - x, y), (x, 1 - y), (1 - x, 1 - y)]


def _allgather_chips(arrs, name):
    n = len(arrs)

    def body(*refs):
        ins, outs = refs[:n], refs[n:2 * n]
        ici_send, ici_recv, d2d_send, d2d_recv = refs[2 * n:]
        x, y, c = _mesh_pos()
        me = 2 * x + y
        chips = _other_chips(x, y)
        sibling = (x, y, 1 - c)

        def ici(i, j, k):
            cx, cy = chips[j]
            return pltpu.make_async_remote_copy(src_ref=ins[i].at[c], dst_ref=outs[i].at[k, c], send_sem=ici_send.at[3 * i + j],
                                                recv_sem=ici_recv.at[3 * i + j], device_id=(cx, cy, c), device_id_type=MESH)

        def d2d(i, j, h):
            cx, cy = chips[j]
            slot = outs[i].at[2 * cx + cy, h]
            return pltpu.make_async_remote_copy(src_ref=slot, dst_ref=slot, send_sem=d2d_send.at[3 * i + j],
                                                recv_sem=d2d_recv.at[3 * i + j], device_id=sibling, device_id_type=MESH)

        sends = [ici(i, j, me) for i in range(n) for j in range(3)]
        for cp in sends:
            cp.start()
        for i in range(n):
            for j, (cx, cy) in enumerate(chips):
                ici(i, j, 2 * cx + cy).wait_recv()
                fwd = d2d(i, j, c)
                fwd.start()
                sends.append(fwd)
        for i in range(n):
            for j in range(3):
                d2d(i, j, 1 - c).wait_recv()
        for cp in sends:
            cp.wait_send()

    return pl.pallas_call(
        body, name=name, in_specs=[ANY] * n, out_specs=[ANY] * n,
        out_shape=[jax.ShapeDtypeStruct((N_CHIPS,) + a.shape, a.dtype) for a in arrs],
        scratch_shapes=[pltpu.SemaphoreType.DMA((3 * n,))] * 4,
    )(*arrs)


def _swap_halves(arrs, half_first, name):
    n = len(arrs)

    def body(*refs):
        ins, outs = refs[:n], refs[n:2 * n]
        send_sems, recv_sems = refs[2 * n:]
        x, y, c = _mesh_pos()
        cps = [pltpu.make_async_remote_copy(src_ref=ins[i].at[1 - c] if half_first[i] else ins[i].at[:, 1 - c],
                                            dst_ref=outs[i], send_sem=send_sems.at[i], recv_sem=recv_sems.at[i],
                                            device_id=(x, y, 1 - c), device_id_type=MESH)
               for i in range(n)]
        for cp in cps:
            cp.start()
        for cp in cps:
            cp.wait()

    return pl.pallas_call(
        body, name=name, in_specs=[ANY] * n, out_specs=[ANY] * n,
        out_shape=[jax.ShapeDtypeStruct((N_CHIPS,) + a.shape[2:], a.dtype) for a in arrs],
        scratch_shapes=[pltpu.SemaphoreType.DMA((n,)), pltpu.SemaphoreType.DMA((n,))],
    )(*arrs)


HBM = pl.BlockSpec(memory_space=pltpu.HBM)
SEM = pl.BlockSpec(memory_space=pltpu.SEMAPHORE)
TOKEN = jax.ShapeDtypeStruct((SUBLANES, LANES), F32)


def _chip_copies(src_refs, land_refs, send_sems, recv_sems, slice_by_chip, receiving):
    x, y, c = _mesh_pos()
    me = 2 * x + y
    cps = []
    for i, (src, land) in enumerate(zip(src_refs, land_refs)):
        for j, (cx, cy) in enumerate(_other_chips(x, y)):
            peer = 2 * cx + cy
            s = src.at[me if receiving else peer] if slice_by_chip else src
            cps.append(pltpu.make_async_remote_copy(
                src_ref=s, dst_ref=land.at[peer if receiving else me], send_sem=send_sems.at[3 * i + j],
                recv_sem=recv_sems.at[3 * i + j], device_id=(cx, cy, c), device_id_type=MESH))
    return cps


def _chips_start(srcs, slice_by_chip, after, name):
    n = len(srcs)
    lands = [lax.empty((N_CHIPS,) + (s.shape[1:] if slice_by_chip else s.shape), s.dtype) for s in srcs]

    def body(*refs):
        src_refs, land_refs = refs[:n], refs[n:2 * n]
        send_sems, recv_sems = refs[2 * n + 1], refs[2 * n + 2]
        token = refs[-1]
        for cp in _chip_copies(src_refs, land_refs, send_sems, recv_sems, slice_by_chip, False):
            cp.start()
        token[...] = jnp.zeros_like(token)

    outs = pl.pallas_call(
        body, name=name,
        in_specs=[HBM] * (2 * n) + [ANY],
        out_specs=(SEM, SEM) + (HBM,) * (2 * n) + (pl.BlockSpec(memory_space=pltpu.VMEM),),
        out_shape=(pltpu.SemaphoreType.DMA((3 * n,)), pltpu.SemaphoreType.DMA((3 * n,)))
        + tuple(pltpu.HBM(a.shape, a.dtype) for a in list(srcs) + lands) + (TOKEN,),
        input_output_aliases={i: 2 + i for i in range(2 * n)},
        compiler_params=pltpu.CompilerParams(has_side_effects=pltpu.SideEffectType.DATAFLOW_SIDE_EFFECTING),
    )(*[pltpu.with_memory_space_constraint(a, pltpu.HBM) for a in list(srcs) + lands], after)
    return dict(sems=outs[:2], srcs=outs[2:2 + n], lands=outs[2 + n:2 + 2 * n], token=outs[-1], slice_by_chip=slice_by_chip)


def _chips_wait(started, after, name):
    n = len(started["srcs"])
    slice_by_chip = started["slice_by_chip"]
    after = list(after) if isinstance(after, (list, tuple)) else [after]

    def body(*refs):
        src_refs, land_refs = refs[:n], refs[n:2 * n]
        send_sems, recv_sems = refs[2 * n], refs[2 * n + 1]
        for cp in _chip_copies(src_refs, land_refs, send_sems, recv_sems, slice_by_chip, True):
            cp.wait_send()
            cp.wait_recv()

    outs = pl.pallas_call(
        body, name=name,
        in_specs=[HBM] * (2 * n) + [SEM, SEM] + [ANY] * len(after),
        out_specs=(HBM,) * (2 * n),
        out_shape=tuple(pltpu.HBM(a.shape, a.dtype) for a in list(started["srcs"]) + list(started["lands"])),
        input_output_aliases={i: i for i in range(2 * n)},
        compiler_params=pltpu.CompilerParams(has_side_effects=pltpu.SideEffectType.DATAFLOW_SIDE_EFFECTING),
    )(*started["srcs"], *started["lands"], *started["sems"], *after)
    return outs[:n], outs[n:]


def _swap_whole(arrs, name):
    n = len(arrs)

    def body(*refs):
        ins, outs = refs[:n], refs[n:2 * n]
        send_sems, recv_sems = refs[2 * n:]
        x, y, c = _mesh_pos()
        cps = [pltpu.make_async_remote_copy(src_ref=ins[i], dst_ref=outs[i], send_sem=send_sems.at[i],
                                            recv_sem=recv_sems.at[i], device_id=(x, y, 1 - c), device_id_type=MESH)
               for i in range(n)]
        for cp in cps:
            cp.start()
        for cp in cps:
            cp.wait()

    return pl.pallas_call(
        body, name=name, in_specs=[ANY] * n, out_specs=[ANY] * n,
        out_shape=[jax.ShapeDtypeStruct(a.shape, a.dtype) for a in arrs],
        scratch_shapes=[pltpu.SemaphoreType.DMA((n,)), pltpu.SemaphoreType.DMA((n,))],
    )(*arrs)


def _allgather_devices(a, name):
    masks = [(mx, my, mc) for mx in (0, 1) for my in (0, 1) for mc in (0, 1)][1:]

    def body(in_ref, out_ref, send_sems, recv_sems, loc_sem):
        x, y, c = _mesh_pos()
        me = 4 * x + 2 * y + c
        lc = pltpu.make_async_copy(in_ref, out_ref.at[me], loc_sem.at[0])
        lc.start()
        peers = [(jnp.where(mx, 1 - x, x), jnp.where(my, 1 - y, y), jnp.where(mc, 1 - c, c)) for mx, my, mc in masks]
        cps = [pltpu.make_async_remote_copy(src_ref=in_ref, dst_ref=out_ref.at[me], send_sem=send_sems.at[j],
                                            recv_sem=recv_sems.at[j], device_id=peers[j], device_id_type=MESH)
               for j in range(len(masks))]
        for cp in cps:
            cp.start()
        for j, (px, py, pc) in enumerate(peers):
            pltpu.make_async_remote_copy(src_ref=in_ref, dst_ref=out_ref.at[4 * px + 2 * py + pc], send_sem=send_sems.at[j],
                                         recv_sem=recv_sems.at[j], device_id=peers[j], device_id_type=MESH).wait_recv()
        for cp in cps:
            cp.wait_send()
        lc.wait()

    return pl.pallas_call(
        body, name=name, in_specs=[ANY], out_specs=ANY,
        out_shape=jax.ShapeDtypeStruct((N_DEV,) + a.shape, a.dtype),
        scratch_shapes=[pltpu.SemaphoreType.DMA((N_DEV - 1,)), pltpu.SemaphoreType.DMA((N_DEV - 1,)),
                        pltpu.SemaphoreType.DMA((1,))],
    )(a)


def _as_rows(a, lead):
    shp = a.shape
    rows = 1
    for s in shp[lead:-1]:
        rows *= s
    return a.reshape(shp[:lead] + (rows, shp[-1]))


def _row_tile(rows, cols, n_bufs):
    budget = (24 * 1024 * 1024) // (n_bufs * 2 * 4 * cols)
    return _pick(rows, max(2 * SUBLANES, budget), 2 * SUBLANES)


def _sum_leading(a, name):
    n = a.shape[0]
    v = _as_rows(a, 1)
    _, rows, cols = v.shape
    tr = _row_tile(rows, cols, n + 1)

    def body(a_ref, o_ref):
        acc = a_ref[0]
        for k in range(1, n):
            acc = acc + a_ref[k]
        o_ref[...] = acc

    out = pl.pallas_call(body, name=name, grid=(rows // tr,),
                         in_specs=[pl.BlockSpec((n, tr, cols), lambda i: (0, i, 0))],
                         out_specs=pl.BlockSpec((tr, cols), lambda i: (i, 0)),
                         out_shape=jax.ShapeDtypeStruct((rows, cols), F32), compiler_params=_cp("parallel"))(v)
    return out.reshape(a.shape[1:])


def _scalar(i):
    return jnp.reshape(i, (1,)).astype(jnp.int32)


def _add_own_half(g, other, c, half_first, name):
    _, rows, cols = other.shape
    tr = _row_tile(rows, cols, 3)

    def body(c_ref, g_ref, o_ref, out_ref):
        out_ref[0] = (g_ref[0, 0] + o_ref[0]).astype(out_ref.dtype)

    if half_first:
        g_map = lambda k, i, c_ref: (c_ref[0], k, i, 0)
    else:
        g_map = lambda k, i, c_ref: (k, c_ref[0], i, 0)
    flat = pl.BlockSpec((1, tr, cols), lambda k, i, c_ref: (k, i, 0))
    return pl.pallas_call(
        body, name=name,
        grid_spec=pltpu.PrefetchScalarGridSpec(
            num_scalar_prefetch=1, grid=(N_CHIPS, rows // tr),
            in_specs=[pl.BlockSpec((1, 1, tr, cols), g_map), flat], out_specs=flat),
        out_shape=jax.ShapeDtypeStruct(other.shape, COMM_DTYPE), compiler_params=_cp("parallel", "parallel"),
    )(_scalar(c), g, other)


def _sum_chips(own, got, chip, name):
    pv = _as_rows(own, 1)
    bv = _as_rows(got, 1)
    _, rows, cols = pv.shape
    tr = _row_tile(rows, cols, N_CHIPS + 2)

    def body(chip_ref, p_ref, b_ref, o_ref):
        mine = p_ref[0].astype(F32)
        acc = jnp.where(chip_ref[0] == 0, mine, b_ref[0].astype(F32))
        for k in range(1, N_CHIPS):
            acc = acc + jnp.where(chip_ref[0] == k, mine, b_ref[k].astype(F32))
        o_ref[...] = acc

    out = pl.pallas_call(
        body, name=name,
        grid_spec=pltpu.PrefetchScalarGridSpec(
            num_scalar_prefetch=1, grid=(rows // tr,),
            in_specs=[pl.BlockSpec((1, tr, cols), lambda i, k_ref: (k_ref[0], i, 0)),
                      pl.BlockSpec((N_CHIPS, tr, cols), lambda i, k_ref: (0, i, 0))],
            out_specs=pl.BlockSpec((tr, cols), lambda i, k_ref: (i, 0))),
        out_shape=jax.ShapeDtypeStruct((rows, cols), F32), compiler_params=_cp("parallel"),
    )(_scalar(chip), pv, bv)
    return out.reshape(own.shape[1:])


def _adam_math(w, g, m, v):
    nm = ADAM_B1 * m + (1.0 - ADAM_B1) * g
    nv = ADAM_B2 * v + (1.0 - ADAM_B2) * (g * g)
    m_hat = nm / (1.0 - ADAM_B1 ** ADAM_STEP)
    v_hat = nv / (1.0 - ADAM_B2 ** ADAM_STEP)
    return -ADAM_LR * (m_hat / (jnp.sqrt(v_hat) + ADAM_EPS) + ADAM_WD * w), nm, nv


def _adamw_halves(w, mine, theirs, m, v, c, name):
    shape = w.shape
    ws, ms, vs = (_as_rows(t.reshape((2, -1) + t.shape[-1:]), 1) for t in (w, m, v))
    a, b = _as_rows(mine, 0), _as_rows(theirs, 0)
    rows, cols = a.shape
    tr = _row_tile(rows, cols, 9)

    def body(c_ref, w_ref, a_ref, b_ref, m_ref, v_ref, g_ref, d_ref, nm_ref, nv_ref):
        gv = jnp.where(pl.program_id(0) == c_ref[0], a_ref[...], b_ref[...])
        g_ref[0] = gv
        d_ref[0], nm_ref[0], nv_ref[0] = _adam_math(w_ref[0], gv, m_ref[0], v_ref[0])

    half = pl.BlockSpec((1, tr, cols), lambda h, i, c_ref: (h, i, 0))
    flat = pl.BlockSpec((tr, cols), lambda h, i, c_ref: (i, 0))
    outs = pl.pallas_call(
        body, name=name,
        grid_spec=pltpu.PrefetchScalarGridSpec(num_scalar_prefetch=1, grid=(2, rows // tr),
                                               in_specs=[half, flat, flat, half, half], out_specs=[half] * 4),
        out_shape=[jax.ShapeDtypeStruct((2, rows, cols), F32)] * 4, compiler_params=_cp("parallel", "parallel"),
    )(_scalar(c), ws, a, b, ms, vs)
    return tuple(o.reshape(shape) for o in outs)


def _adamw(w, g, m, v, name):
    shape = w.shape
    ws, gs, ms, vs = (_as_rows(t, 0) for t in (w, g, m, v))
    rows, cols = ws.shape
    tr = _row_tile(rows, cols, 7)

    def body(w_ref, g_ref, m_ref, v_ref, d_ref, nm_ref, nv_ref):
        d_ref[...], nm_ref[...], nv_ref[...] = _adam_math(w_ref[...], g_ref[...], m_ref[...], v_ref[...])

    spec = pl.BlockSpec((tr, cols), lambda i: (i, 0))
    outs = pl.pallas_call(body, name=name, grid=(rows // tr,), in_specs=[spec] * 4, out_specs=[spec] * 3,
                          out_shape=[jax.ShapeDtypeStruct((rows, cols), F32)] * 3, compiler_params=_cp("parallel"))(ws, gs, ms, vs)
    return tuple(o.reshape(shape) for o in outs)


_BIG = ["ffn_w_gate", "ffn_w_up", "ffn_w_down", "dn_w_in", "dn_w_out", "sg_w_in", "sg_w_out"]
_SMALL_SHARDED = ["norm_g", "dn_conv_w", "sg_b_in", "sg_ln_g", "sg_ln_b"]
_SMALL_REPL = ["dn_a_log", "dn_dt_bias", "dn_norm_g", "sg_w_s", "sg_b_s"]
_WEIGHTS = ["norm_g", "ffn_w_gate", "ffn_w_up", "ffn_w_down", "dn_w_in", "dn_conv_w", "dn_a_log", "dn_dt_bias",
            "dn_norm_g", "dn_w_out", "sg_w_in", "sg_b_in", "sg_ln_g", "sg_ln_b", "sg_w_s", "sg_b_s", "sg_w_out"]
PACK_COLS = 1024


def _pack(arrs):
    flat = jnp.concatenate([a.reshape(-1) for a in arrs])
    pad = (-flat.shape[0]) % (SUBLANES * PACK_COLS)
    return jnp.pad(flat, (0, pad)).reshape(-1, PACK_COLS)


def _unpack(buf, shapes):
    flat = buf.reshape(-1)
    out, off = [], 0
    for s in shapes:
        n = math.prod(s)
        out.append(flat[off:off + n].reshape(s))
        off += n
    return out


def _as_halves(a):
    if a.shape[0] == 2:
        return a
    if a.shape[0] == 1:
        return a.reshape((2, a.shape[1] // 2) + a.shape[2:])
    return a.reshape((2, a.shape[0] // 2) + a.shape[1:])


def _with_own(gathered, own, chip):
    g = gathered.reshape((N_CHIPS,) + own.shape)
    return [jnp.where(chip == k, own, g[k]) for k in range(N_CHIPS)]


def _cat_shards(g, axis):
    return jnp.concatenate(list(g), axis=axis)


_GROUP_ORDER = ["ffn00", "dn", "ffn01", "ffn10", "sg", "ffn11"]


def _weight_groups(w):
    cast = {k: _mx(w[k]) for k in _BIG}
    groups = {"ffn%d%d" % (i, j): [cast["ffn_w_gate"][i, j].T, cast["ffn_w_up"][i, j].T, cast["ffn_w_down"][i, j]]
              for i, j in [(0, 0), (0, 1), (1, 0), (1, 1)]}
    groups["dn"] = [cast["dn_w_in"][0], cast["dn_w_out"][0]]
    groups["sg"] = [cast["sg_w_in"][0], cast["sg_w_out"][0]]
    return groups


def _ffn_weights(chip, own, gathered):
    pairs = [(a, g.reshape((N_CHIPS,) + a.shape)) for a, g in zip(own, gathered)]
    return {"chip": chip, "gate": pairs[0], "up": pairs[1], "down": pairs[2]}


def _group_matrices(group, shards):
    if group == "sg":
        return {"sg_win": _cat_shards(shards[0], 1), "sg_wout": _cat_shards(shards[1], 0)}
    dn_full = _cat_shards(shards[0], 1)
    W4 = 4 * DN_HEADS * DN_HEAD_DIM
    wba = jnp.zeros((D_MODEL, 2 * LANES), dn_full.dtype)
    wba = wba.at[:, :DN_HEADS].set(dn_full[:, W4:W4 + DN_HEADS])
    wba = wba.at[:, LANES:LANES + DN_HEADS].set(dn_full[:, W4 + DN_HEADS:])
    return {"dn_wqkvz": dn_full[:, :W4], "dn_wba": wba, "dn_wout": _cat_shards(shards[1], 0)}


def _split_cols(a, n):
    w = a.shape[-1] // n
    return [a[..., k * w:(k + 1) * w] for k in range(n)]


def _split_rows(a, n):
    h = a.shape[-2] // n
    return [a[..., k * h:(k + 1) * h, :] for k in range(n)]


_IJ = [(0, 0), (0, 1), (1, 0), (1, 1)]


_REDUCED = ["wguT%d%d" % ij for ij in _IJ] + ["wd%d%d" % ij for ij in _IJ] + ["dn_w_in", "dn_w_out", "sg_w_inT", "sg_w_out"]


def _group_grads(group, grads):
    def rows_by_chip(a):
        return a.reshape(N_CHIPS, 2, a.shape[0] // (2 * N_CHIPS), a.shape[1])

    if group.startswith("ffn"):
        tag = group[3:]
        t = grads["wguT" + tag]
        return (["wguT" + tag, "wd" + tag],
                [t.reshape(2, N_CHIPS, t.shape[0] // (2 * N_CHIPS), t.shape[1]), rows_by_chip(grads["wd" + tag])], [True, False])
    if group == "sg":
        return ["sg_w_inT", "sg_w_out"], [rows_by_chip(grads["sg_w_inT"]), rows_by_chip(grads["sg_w_out"])], [False, False]
    dn_in = jnp.stack([jnp.stack(_split_cols(hf, N_CHIPS)) for hf in _split_rows(grads["dn_w_in"], 2)])
    return ["dn_w_in", "dn_w_out"], [dn_in, rows_by_chip(grads["dn_w_out"])], [True, False]


def _shard_grads(mine, theirs, c, w):
    lo = [jnp.where(c == 0, a, b) for a, b in zip(mine, theirs)]
    hi = [jnp.where(c == 0, b, a) for a, b in zip(mine, theirs)]
    rows = lambda t: jnp.concatenate([lo[t], hi[t]], axis=0)
    sq = lambda parts: jnp.stack(parts).reshape(2, 2, *parts[0].shape)
    g = {}
    g["ffn_w_gate"] = sq([lo[t].T for t in range(4)])
    g["ffn_w_up"] = sq([hi[t].T for t in range(4)])
    g["ffn_w_down"] = sq([rows(4 + t) for t in range(4)])
    g["dn_w_in"] = rows(8)[None]
    g["dn_w_out"] = rows(9)[None]
    g["sg_w_in"] = rows(10).T[None]
    g["sg_w_out"] = rows(11)[None]
    return {k: v.reshape(w[k].shape) for k, v in g.items()}


def kernel(x, norm_g, ffn_w_gate, ffn_w_up, ffn_w_down, dn_w_in, dn_conv_w, dn_a_log, dn_dt_bias, dn_norm_g, dn_w_out, sg_w_in, sg_b_in, sg_ln_g, sg_ln_b, sg_w_s, sg_b_s, sg_w_out, loss_target, m_norm_g, m_ffn_w_gate, m_ffn_w_up, m_ffn_w_down, m_dn_w_in, m_dn_conv_w, m_dn_a_log, m_dn_dt_bias, m_dn_norm_g, m_dn_w_out, m_sg_w_in, m_sg_b_in, m_sg_ln_g, m_sg_ln_b, m_sg_w_s, m_sg_b_s, m_sg_w_out, v_norm_g, v_ffn_w_gate, v_ffn_w_up, v_ffn_w_down, v_dn_w_in, v_dn_conv_w, v_dn_a_log, v_dn_dt_bias, v_dn_norm_g, v_dn_w_out, v_sg_w_in, v_sg_b_in, v_sg_ln_g, v_sg_ln_b, v_sg_w_s, v_sg_b_s, v_sg_w_out):
    args = dict(locals())
    w = {k: args[k] for k in _WEIGHTS}
    mom = {k: args["m_" + k] for k in _WEIGHTS}
    var = {k: args["v_" + k] for k in _WEIGHTS}
    cx, cy, cc = _mesh_pos()
    chip = 2 * cx + cy

    small_shapes = [w[k].shape for k in _SMALL_SHARDED]
    groups = _weight_groups(w)
    own = groups[_GROUP_ORDER[0]] + [_pack([w[k] for k in _SMALL_SHARDED])]
    first = _allgather_chips([_as_halves(a) for a in own], "gather_first")
    started, after = {}, first[0]
    for g in _GROUP_ORDER[1:]:
        started[g] = _chips_start(groups[g], False, after, "gather_start_" + g)
        after = started[g]["token"]
    small_k = [_unpack(pack, small_shapes) for pack in _with_own(first[-1], own[-1], chip)]
    p = {name: jnp.concatenate([small_k[k][i] for k in range(N_CHIPS)], axis=-1) for i, name in enumerate(_SMALL_SHARDED)}
    p = {k: (v if k == "norm_g" else v[0]) for k, v in p.items()}
    p["norm_g"] = p["norm_g"] + after[0, 0]
    for k in _SMALL_REPL:
        p[k] = w[k][0]

    def weights_for(group, after):
        if group == _GROUP_ORDER[0]:
            return _ffn_weights(chip, own[:-1], first[:-1])
        srcs, lands = _chips_wait(started[group], after, "gather_wait_" + group)
        if group.startswith("ffn"):
            return _ffn_weights(chip, srcs, lands)
        return _group_matrices(group, [_with_own(l, a, chip) for l, a in zip(lands, srcs)])

    mine, theirs, pending = {}, {}, []

    def finish(group, names, scatter, after):
        pair_sum, got = _chips_wait(scatter, after, "reduce_wait_" + group)
        half_sum = [_sum_chips(a, b, chip, "chip_sum_" + n) for n, a, b in zip(names, pair_sum, got)]
        other = _swap_whole(half_sum, "gather_core_pair_" + group)
        mine.update(zip(names, half_sum))
        theirs.update(zip(names, other))

    def grads_ready(group, grads):
        names, halves, half_first = _group_grads(group, grads)
        from_sibling = _swap_halves(halves, half_first, "reduce_core_pair_" + group)
        pair_sum = [_add_own_half(h, o, cc, hf, "pair_sum_" + n) for n, h, o, hf in zip(names, halves, from_sibling, half_first)]
        scatter = _chips_start(pair_sum, True, from_sibling[0], "reduce_start_" + group)
        if pending:
            finish(*pending.pop(), scatter["token"])
        pending.append((group, names, scatter))
        return scatter["token"][0, 0]

    loss_part, grad_x, grads = _local_step(x[0], loss_target[0], p, weights_for, grads_ready)

    small_names = _SMALL_SHARDED + _SMALL_REPL
    small_grads = [grads[k] for k in small_names]
    full_shapes = [g.shape for g in small_grads] + [(1,)]
    pack = _pack(small_grads + [loss_part[0, :1]])
    summed = _sum_leading(_allgather_devices(pack, "gather_small"), "small_sum")
    finish(*pending.pop(), [summed] + list(theirs.values()))
    half_sum = [mine[n] for n in _REDUCED]
    other_half = [theirs[n] for n in _REDUCED]
    parts = _unpack(summed, full_shapes)
    loss = parts[-1][0]
    small_grad = {}
    for i, k in enumerate(small_names):
        g = parts[i]
        if k in _SMALL_SHARDED:
            n = w[k].shape[-1]
            g = lax.dynamic_slice_in_dim(g, chip * n, n, axis=g.ndim - 1)
        small_grad[k] = g

    grad = {**small_grad, **_shard_grads(half_sum, other_half, cc, w)}
    delta, new_m, new_v = {}, {}, {}
    for k in _BIG:
        delta[k], new_m[k], new_v[k] = _adamw(w[k], grad[k], mom[k], var[k], "adamw_" + k)
    shapes = [w[k].shape for k in small_names]
    d, nm, nv = _adamw(_pack([w[k] for k in small_names]), _pack([grad[k] for k in small_names]),
                       _pack([mom[k] for k in small_names]), _pack([var[k] for k in small_names]), "adamw_small")
    for k, a, b, c_ in zip(small_names, _unpack(d, shapes), _unpack(nm, shapes), _unpack(nv, shapes)):
        delta[k], new_m[k], new_v[k] = a, b, c_

    return (loss, grad_x[None], *[grad[k] for k in _WEIGHTS], *[delta[k] for k in _WEIGHTS],
            *[new_m[k] for k in _WEIGHTS], *[new_v[k] for k in _WEIGHTS])
```

```python
import functools
import math

import jax
import jax.numpy as jnp
from jax import lax
from jax.experimental import pallas as pl
from jax.experimental.pallas import tpu as pltpu

F32 = jnp.float32
MXU_DTYPE = jnp.bfloat16
COMM_DTYPE = jnp.bfloat16
HI = lax.Precision.HIGHEST
TRI_PREC = lax.Precision.HIGH

D_MODEL = 1024
D_FF = 2816
RMS_EPS = 1e-6
LN_EPS = 1e-5
L2_EPS = 1e-6
DN_HEADS = 8
DN_HEAD_DIM = 128
DN_CONV = 4
DN_CHUNK = 64
SG_WIDTH = 2048
SG_GROUPS = 8
SG_CHUNK = 128
SG_GROUP_W = SG_WIDTH // SG_GROUPS
N_CHIPS = 4
N_DEV = 8
LANES = 128
SUBLANES = 8
VMEM_LIMIT = 56 * 1024 * 1024

ADAM_LR = 0.001
ADAM_B1 = 0.9
ADAM_B2 = 0.999
ADAM_EPS = 1e-08
ADAM_WD = 0.01
ADAM_STEP = 10

MESH = pl.DeviceIdType.MESH
ANY = pl.BlockSpec(memory_space=pl.ANY)


def _cp(*sem):
    return pltpu.CompilerParams(dimension_semantics=sem, vmem_limit_bytes=VMEM_LIMIT)


def _pick(n, pref, mult=LANES):
    best = None
    d = mult
    while d <= min(n, pref):
        if n % d == 0:
            best = d
        d += mult
    return best if best is not None else n


def _full(shape):
    nd = len(shape)
    return pl.BlockSpec(shape, lambda *_: (0,) * nd)


def _sigmoid(x):
    return 1.0 / (1.0 + jnp.exp(-x))


def _dot(a, b, dims, prec=None):
    return lax.dot_general(a, b, (dims, ((), ())), preferred_element_type=F32, precision=prec)


NN = ((1,), (0,))
NT = ((1,), (1,))
TN = ((0,), (0,))


def _mx(a):
    return a.astype(MXU_DTYPE)


def _rms_stat(x):
    return lax.rsqrt(jnp.mean(x * x, axis=-1, keepdims=True) + RMS_EPS)


def _rms_bwd(x, r, g, dy):
    xh = x * r
    dxh = dy * g
    dx = r * (dxh - xh * jnp.mean(dxh * xh, axis=-1, keepdims=True))
    return dx, jnp.sum(dy * xh, axis=0, keepdims=True)


def _mm(a, b, mode, name, out_dtype=F32, add=None, after=None):
    if mode == "tn":
        K, M = a.shape
        N = b.shape[1]
    elif mode == "nt":
        M, K = a.shape
        N = b.shape[0]
    else:
        M, K = a.shape
        N = b.shape[1]
    tn = _pick(N, 1024)
    if mode == "tn":
        tm = _pick(M, 1024 if tn <= 512 else 1408)
        tk = _pick(K, 1024, SUBLANES)
    else:
        tm = _pick(M, max(512, min(2048, (512 * 1024) // tn)), SUBLANES)
        tk = _pick(K, 2048)
    nk = K // tk
    grid = (N // tn, M // tm, nk)
    if mode == "nn":
        a_spec = pl.BlockSpec((tm, tk), lambda j, i, k: (i, k))
        b_spec = pl.BlockSpec((tk, tn), lambda j, i, k: (k, j))
        dims = NN
    elif mode == "nt":
        a_spec = pl.BlockSpec((tm, tk), lambda j, i, k: (i, k))
        b_spec = pl.BlockSpec((tn, tk), lambda j, i, k: (j, k))
        dims = NT
    else:
        a_spec = pl.BlockSpec((tk, tm), lambda j, i, k: (k, i))
        b_spec = pl.BlockSpec((tk, tn), lambda j, i, k: (k, j))
        dims = TN
    o_spec = pl.BlockSpec((tm, tn), lambda j, i, k: (i, j))
    has_add = add is not None

    def body(*refs):
        a_ref, b_ref = refs[:2]
        add_ref = refs[2] if has_add else None
        o_ref, acc = refs[-2:]
        k = pl.program_id(2)

        @pl.when(k == 0)
        def _():
            acc[...] = add_ref[...] if has_add else jnp.zeros_like(acc)

        acc[...] += _dot(a_ref[...], b_ref[...], dims)

        @pl.when(k == nk - 1)
        def _():
            o_ref[...] = acc[...].astype(o_ref.dtype)

    ins = [a, b] + ([add] if has_add else []) + ([after] if after is not None else [])
    specs = [a_spec, b_spec] + ([o_spec] if has_add else []) + ([ANY] if after is not None else [])
    return pl.pallas_call(
        body, name=name, grid=grid, in_specs=specs, out_specs=o_spec,
        out_shape=jax.ShapeDtypeStruct((M, N), out_dtype),
        scratch_shapes=[pltpu.VMEM((tm, tn), F32)],
        compiler_params=_cp("parallel", "parallel", "arbitrary"),
    )(*ins)


def _ffn_weight_operands(wt):
    return [_scalar(wt["chip"])] , [wt["gate"][0], wt["gate"][1], wt["up"][0], wt["up"][1], wt["down"][0], wt["down"][1]]


def _load_ffn_weights(chip_ref, shard_refs, wgu_v, wd_v, sem):
    fs = wd_v.shape[0] // N_CHIPS

    @pl.when(pl.program_id(0) == 0)
    def _():
        me = chip_ref[0]
        waits = []
        for t, (dst, base) in enumerate([(wgu_v, 0), (wgu_v, wd_v.shape[0]), (wd_v, 0)]):
            own, gathered = shard_refs[2 * t], shard_refs[2 * t + 1]
            for k in range(N_CHIPS):
                slot = dst.at[pl.ds(base + k * fs, fs), :]
                s = sem.at[t * N_CHIPS + k]

                @pl.when(me == k)
                def _(own=own, slot=slot, s=s):
                    pltpu.make_async_copy(own, slot, s).start()

                @pl.when(me != k)
                def _(gathered=gathered, k=k, slot=slot, s=s):
                    pltpu.make_async_copy(gathered.at[k], slot, s).start()

                waits.append(pltpu.make_async_copy(own, slot, s))
        for cp in waits:
            cp.wait()


def _ffn_fwd(x, g0, g1, wt, name):
    T, D = x.shape
    F = N_CHIPS * wt["down"][0].shape[0]
    F2 = 2 * F
    tm = _pick(T, 256, SUBLANES)
    prefetch, shards = _ffn_weight_operands(wt)

    def body(chip_ref, x_ref, g0_ref, g1_ref, *refs):
        shard_refs = refs[:6]
        xo_ref, h_ref, gu_ref, y_ref, wgu_v, wd_v, sem = refs[6:]
        _load_ffn_weights(chip_ref, shard_refs, wgu_v, wd_v, sem)
        xv = x_ref[...]
        hb = _mx(xv * _rms_stat(xv) * g0_ref[...])
        h_ref[...] = hb
        gu = _dot(hb, wgu_v[...], NT)
        gu_ref[...] = gu.astype(gu_ref.dtype)
        g = gu[:, :F]
        u = gu[:, F:]
        a = _mx(g * _sigmoid(g) * u)
        y = _dot(a, wd_v[...], NN)
        y_ref[...] = y
        xo_ref[...] = xv + 0.5 * (y * _rms_stat(y) * g1_ref[...])

    row = lambda w: pl.BlockSpec((tm, w), lambda i, c: (i, 0))
    one = pl.BlockSpec((1, D), lambda i, c: (0, 0))
    return pl.pallas_call(
        body, name=name,
        grid_spec=pltpu.PrefetchScalarGridSpec(
            num_scalar_prefetch=1, grid=(T // tm,),
            in_specs=[row(D), one, one] + [ANY] * 6,
            out_specs=[row(D), row(D), row(F2), row(D)],
            scratch_shapes=[pltpu.VMEM((F2, D), MXU_DTYPE), pltpu.VMEM((F, D), MXU_DTYPE),
                            pltpu.SemaphoreType.DMA((3 * N_CHIPS,))]),
        out_shape=[jax.ShapeDtypeStruct((T, D), F32), jax.ShapeDtypeStruct((T, D), MXU_DTYPE),
                   jax.ShapeDtypeStruct((T, F2), MXU_DTYPE), jax.ShapeDtypeStruct((T, D), F32)],
        compiler_params=_cp("arbitrary"),
    )(*prefetch, x, g0, g1, *shards)


FFN_BWD_CHUNK = 1408


def _ffn_bwd(dxo, x, y, gu, g0, g1, wt, name):
    T, D = x.shape
    F2 = gu.shape[1]
    F = F2 // 2
    tm = _pick(T, 256, SUBLANES)
    fc = _pick(F, FFN_BWD_CHUNK)
    prefetch, shards = _ffn_weight_operands(wt)

    def body(chip_ref, dxo_ref, x_ref, y_ref, gu_ref, g0_ref, g1_ref, *refs):
        shard_refs = refs[:6]
        dx_ref, dy_ref, a_ref, dgu_ref, dg0_ref, dg1_ref, wgu_v, wd_v, sem = refs[6:]
        _load_ffn_weights(chip_ref, shard_refs, wgu_v, wd_v, sem)

        @pl.when(pl.program_id(0) == 0)
        def _():
            dg0_ref[...] = jnp.zeros_like(dg0_ref)
            dg1_ref[...] = jnp.zeros_like(dg1_ref)

        dxo_v = dxo_ref[...]
        yv = y_ref[...]
        dy, dg1 = _rms_bwd(yv, _rms_stat(yv), g1_ref[...], 0.5 * dxo_v)
        dg1_ref[...] += dg1
        dyb = _mx(dy)
        dy_ref[...] = dyb
        dh = jnp.zeros((tm, D), F32)
        for c in range(F // fc):
            lo, hi = c * fc, (c + 1) * fc
            da = _dot(dyb, wd_v[lo:hi, :], NT)
            g = gu_ref[:, lo:hi].astype(F32)
            u = gu_ref[:, F + lo:F + hi].astype(F32)
            s = _sigmoid(g)
            sg = g * s
            a_ref[:, lo:hi] = _mx(sg * u)
            dg = _mx(da * u * (s * (1.0 + g * (1.0 - s))))
            du = _mx(da * sg)
            dgu_ref[:, lo:hi] = dg
            dgu_ref[:, F + lo:F + hi] = du
            dh = dh + _dot(dg, wgu_v[lo:hi, :], NN) + _dot(du, wgu_v[F + lo:F + hi, :], NN)
        xv = x_ref[...]
        dx, dg0 = _rms_bwd(xv, _rms_stat(xv), g0_ref[...], dh)
        dg0_ref[...] += dg0
        dx_ref[...] = dxo_v + dx

    row = lambda w: pl.BlockSpec((tm, w), lambda i, c: (i, 0))
    one = pl.BlockSpec((1, D), lambda i, c: (0, 0))
    return pl.pallas_call(
        body, name=name,
        grid_spec=pltpu.PrefetchScalarGridSpec(
            num_scalar_prefetch=1, grid=(T // tm,),
            in_specs=[row(D), row(D), row(D), row(F2), one, one] + [ANY] * 6,
            out_specs=[row(D), row(D), row(F), row(F2), one, one],
            scratch_shapes=[pltpu.VMEM((F2, D), MXU_DTYPE), pltpu.VMEM((F, D), MXU_DTYPE),
                            pltpu.SemaphoreType.DMA((3 * N_CHIPS,))]),
        out_shape=[jax.ShapeDtypeStruct((T, D), F32), jax.ShapeDtypeStruct((T, D), MXU_DTYPE),
                   jax.ShapeDtypeStruct((T, F), MXU_DTYPE), jax.ShapeDtypeStruct((T, F2), MXU_DTYPE),
                   jax.ShapeDtypeStruct((1, D), F32), jax.ShapeDtypeStruct((1, D), F32)],
        compiler_params=_cp("arbitrary"),
    )(*prefetch, dxo, x, y, gu, g0, g1, *shards)


def _norm_fwd(x, g, name):
    T, D = x.shape
    tm = _pick(T, 512, SUBLANES)

    def body(x_ref, g_ref, h_ref):
        xv = x_ref[...]
        h_ref[...] = _mx(xv * _rms_stat(xv) * g_ref[...])

    row = pl.BlockSpec((tm, D), lambda i: (i, 0))
    return pl.pallas_call(body, name=name, grid=(T // tm,), in_specs=[row, _full((1, D))], out_specs=row,
                          out_shape=jax.ShapeDtypeStruct((T, D), MXU_DTYPE), compiler_params=_cp("parallel"))(x, g)


def _postnorm_fwd(x, m, g, name):
    T, D = x.shape
    tm = _pick(T, 512, SUBLANES)

    def body(x_ref, m_ref, g_ref, o_ref):
        mv = m_ref[...]
        o_ref[...] = x_ref[...] + mv * _rms_stat(mv) * g_ref[...]

    row = pl.BlockSpec((tm, D), lambda i: (i, 0))
    return pl.pallas_call(body, name=name, grid=(T // tm,), in_specs=[row, row, _full((1, D))], out_specs=row,
                          out_shape=jax.ShapeDtypeStruct((T, D), F32), compiler_params=_cp("parallel"))(x, m, g)


def _postnorm_bwd(dxo, m, g, name):
    T, D = m.shape
    tm = _pick(T, 512, SUBLANES)

    def body(dxo_ref, m_ref, g_ref, dm_ref, dg_ref):
        @pl.when(pl.program_id(0) == 0)
        def _():
            dg_ref[...] = jnp.zeros_like(dg_ref)

        mv = m_ref[...]
        dm, dg = _rms_bwd(mv, _rms_stat(mv), g_ref[...], dxo_ref[...])
        dg_ref[...] += dg
        dm_ref[...] = _mx(dm)

    row = pl.BlockSpec((tm, D), lambda i: (i, 0))
    return pl.pallas_call(body, name=name, grid=(T // tm,), in_specs=[row, row, _full((1, D))],
                          out_specs=[row, _full((1, D))],
                          out_shape=[jax.ShapeDtypeStruct((T, D), MXU_DTYPE), jax.ShapeDtypeStruct((1, D), F32)],
                          compiler_params=_cp("arbitrary"))(dxo, m, g)


def _prenorm_bwd(dxo, dh, x, g, name):
    T, D = x.shape
    tm = _pick(T, 512, SUBLANES)

    def body(dxo_ref, dh_ref, x_ref, g_ref, dx_ref, dg_ref):
        @pl.when(pl.program_id(0) == 0)
        def _():
            dg_ref[...] = jnp.zeros_like(dg_ref)

        xv = x_ref[...]
        dx, dg = _rms_bwd(xv, _rms_stat(xv), g_ref[...], dh_ref[...])
        dg_ref[...] += dg
        dx_ref[...] = dxo_ref[...] + dx

    row = pl.BlockSpec((tm, D), lambda i: (i, 0))
    return pl.pallas_call(body, name=name, grid=(T // tm,), in_specs=[row, row, row, _full((1, D))],
                          out_specs=[row, _full((1, D))],
                          out_shape=[jax.ShapeDtypeStruct((T, D), F32), jax.ShapeDtypeStruct((1, D), F32)],
                          compiler_params=_cp("arbitrary"))(dxo, dh, x, g)


def _loss_fwd_bwd(y, target, name):
    T, D = y.shape
    tm = _pick(T, 512, SUBLANES)

    def body(y_ref, t_ref, l_ref, dy_ref):
        @pl.when(pl.program_id(0) == 0)
        def _():
            l_ref[...] = jnp.zeros_like(l_ref)

        e = y_ref[...] - t_ref[...]
        dy_ref[...] = e * (1.0 / D)
        l_ref[...] += 0.5 * jnp.sum(jnp.mean(e * e, axis=-1, keepdims=True), axis=0, keepdims=True)

    row = pl.BlockSpec((tm, D), lambda i: (i, 0))
    return pl.pallas_call(body, name=name, grid=(T // tm,), in_specs=[row, row],
                          out_specs=[_full((SUBLANES, LANES)), row],
                          out_shape=[jax.ShapeDtypeStruct((SUBLANES, LANES), F32), jax.ShapeDtypeStruct((T, D), F32)],
                          compiler_params=_cp("arbitrary"))(y, target)


DN_ROWS = 512


def _shift_down(prev8, cur, s):
    n = cur.shape[0]
    xx = jnp.concatenate([prev8, cur], axis=0)
    return pltpu.roll(xx, s, 0)[SUBLANES:SUBLANES + n, :]


def _shift_up(cur, next8, s):
    n = cur.shape[0]
    xx = jnp.concatenate([cur, next8], axis=0)
    return pltpu.roll(xx, n + SUBLANES - s, 0)[:n, :]


def _conv_tile(x_ref, w, r, rows):
    start = pl.multiple_of(r * rows, SUBLANES)
    cur = x_ref[pl.ds(start, rows), :]
    pstart = pl.multiple_of(jnp.maximum(start - SUBLANES, 0), SUBLANES)
    prev8 = jnp.where(r == 0, 0.0, x_ref[pl.ds(pstart, SUBLANES), :])
    taps = [_shift_down(prev8, cur, DN_CONV - 1 - j) if j < DN_CONV - 1 else cur for j in range(DN_CONV)]
    c = taps[0] * w[0:1, :]
    for j in range(1, DN_CONV):
        c = c + taps[j] * w[j:j + 1, :]
    return c, taps


def _dn_prep_fwd(proj, conv_w, name):
    T = proj.shape[0]
    W = DN_HEADS * DN_HEAD_DIM
    rows = min(DN_ROWS, T)
    n_inner = T // rows
    scale = DN_HEAD_DIM ** -0.5

    def body(x_ref, w_ref, o_ref):
        cb = pl.program_id(0)
        w = w_ref[...]
        is_qk = cb < 2 * DN_HEADS
        post = jnp.where(cb < DN_HEADS, scale, 1.0)

        def step(r, carry):
            c, _ = _conv_tile(x_ref, w, r, rows)
            s = c * _sigmoid(c)
            rinv = lax.rsqrt(jnp.sum(s * s, axis=-1, keepdims=True) + L2_EPS)
            o_ref[pl.ds(pl.multiple_of(r * rows, SUBLANES), rows), :] = jnp.where(is_qk, s * rinv * post, s)
            return carry

        lax.fori_loop(0, n_inner, step, 0)

    col = pl.BlockSpec((T, LANES), lambda j: (0, j))
    return pl.pallas_call(body, name=name, grid=(3 * W // LANES,),
                          in_specs=[col, pl.BlockSpec((DN_CONV, LANES), lambda j: (0, j))], out_specs=col,
                          out_shape=jax.ShapeDtypeStruct((T, 3 * W), F32), compiler_params=_cp("parallel"))(proj, conv_w)


def _dn_prep_bwd(proj, conv_w, dqkv, name):
    T = proj.shape[0]
    W = DN_HEADS * DN_HEAD_DIM
    rows = min(DN_ROWS, T)
    n_inner = T // rows
    scale = DN_HEAD_DIM ** -0.5

    def body(x_ref, w_ref, dy_ref, dx_ref, dw_ref, dc_scr):
        cb = pl.program_id(0)
        w = w_ref[...]
        is_qk = cb < 2 * DN_HEADS
        post = jnp.where(cb < DN_HEADS, scale, 1.0)

        def step1(r, dws):
            c, taps = _conv_tile(x_ref, w, r, rows)
            sg = _sigmoid(c)
            s = c * sg
            rinv = lax.rsqrt(jnp.sum(s * s, axis=-1, keepdims=True) + L2_EPS)
            dy = dy_ref[pl.ds(pl.multiple_of(r * rows, SUBLANES), rows), :]
            yn = s * rinv
            dyn = dy * post
            ds_qk = rinv * (dyn - yn * jnp.sum(dyn * yn, axis=-1, keepdims=True))
            ds = jnp.where(is_qk, ds_qk, dy)
            dc = ds * (sg * (1.0 + c * (1.0 - sg)))
            dc_scr[pl.ds(pl.multiple_of(r * rows, SUBLANES), rows), :] = dc
            return tuple(dws[j] + jnp.sum(dc * taps[j], axis=0, keepdims=True) for j in range(DN_CONV))

        zero = jnp.zeros((1, LANES), F32)
        dws = lax.fori_loop(0, n_inner, step1, (zero,) * DN_CONV)
        for j in range(DN_CONV):
            dw_ref[j:j + 1, :] = dws[j]

        def step2(r, carry):
            start = pl.multiple_of(r * rows, SUBLANES)
            cur = dc_scr[pl.ds(start, rows), :]
            nstart = pl.multiple_of(jnp.minimum(start + rows, T - SUBLANES), SUBLANES)
            next8 = jnp.where(r == n_inner - 1, 0.0, dc_scr[pl.ds(nstart, SUBLANES), :])
            dx = cur * w[DN_CONV - 1:DN_CONV, :]
            for j in range(DN_CONV - 1):
                dx = dx + _shift_up(cur, next8, DN_CONV - 1 - j) * w[j:j + 1, :]
            dx_ref[pl.ds(start, rows), :] = _mx(dx)
            return carry

        lax.fori_loop(0, n_inner, step2, 0)

    col = pl.BlockSpec((T, LANES), lambda j: (0, j))
    wspec = pl.BlockSpec((DN_CONV, LANES), lambda j: (0, j))
    return pl.pallas_call(body, name=name, grid=(3 * W // LANES,), in_specs=[col, wspec, col], out_specs=[col, wspec],
                          out_shape=[jax.ShapeDtypeStruct((T, 3 * W), MXU_DTYPE), jax.ShapeDtypeStruct((DN_CONV, 3 * W), F32)],
                          scratch_shapes=[pltpu.VMEM((T, LANES), F32)], compiler_params=_cp("parallel"))(proj, conv_w, dqkv)


def _softplus(x):
    return jnp.maximum(x, 0.0) + jnp.log(1.0 + jnp.exp(-jnp.abs(x)))


def _dn_gate_fwd(ba, a_log, dt_bias, name):
    T = ba.shape[0]
    tm = _pick(T, 1024, SUBLANES)

    def body(ba_ref, al_ref, dt_ref, beta_ref, g_ref):
        beta_ref[...] = _sigmoid(ba_ref[:, :LANES])
        g_ref[...] = -jnp.exp(al_ref[...]) * _softplus(ba_ref[:, LANES:] + dt_ref[...])

    row = lambda w: pl.BlockSpec((tm, w), lambda i: (i, 0))
    return pl.pallas_call(body, name=name, grid=(T // tm,), in_specs=[row(2 * LANES), _full((1, LANES)), _full((1, LANES))],
                          out_specs=[row(LANES), row(LANES)],
                          out_shape=[jax.ShapeDtypeStruct((T, LANES), F32)] * 2, compiler_params=_cp("parallel"))(ba, a_log, dt_bias)


def _dn_gate_bwd(ba, a_log, dt_bias, dbeta, dg, name):
    T = ba.shape[0]
    tm = _pick(T, 1024, SUBLANES)

    def body(ba_ref, al_ref, dt_ref, dbeta_ref, dg_ref, dba_ref, dal_ref, ddt_ref):
        @pl.when(pl.program_id(0) == 0)
        def _():
            dal_ref[...] = jnp.zeros_like(dal_ref)
            ddt_ref[...] = jnp.zeros_like(ddt_ref)

        beta = _sigmoid(ba_ref[:, :LANES])
        dba_ref[:, :LANES] = _mx(dbeta_ref[...] * beta * (1.0 - beta))
        pre = ba_ref[:, LANES:] + dt_ref[...]
        ea = jnp.exp(al_ref[...])
        dgv = dg_ref[...]
        da = dgv * (-ea) * _sigmoid(pre)
        dba_ref[:, LANES:] = _mx(da)
        ddt_ref[...] += jnp.sum(da, axis=0, keepdims=True)
        dal_ref[...] += jnp.sum(dgv * (-ea) * _softplus(pre), axis=0, keepdims=True)

    row = lambda w: pl.BlockSpec((tm, w), lambda i: (i, 0))
    one = _full((1, LANES))
    return pl.pallas_call(body, name=name, grid=(T // tm,), in_specs=[row(2 * LANES), one, one, row(LANES), row(LANES)],
                          out_specs=[row(2 * LANES), one, one],
                          out_shape=[jax.ShapeDtypeStruct((T, 2 * LANES), MXU_DTYPE), jax.ShapeDtypeStruct((1, LANES), F32),
                                     jax.ShapeDtypeStruct((1, LANES), F32)],
                          compiler_params=_cp("arbitrary"))(ba, a_log, dt_bias, dbeta, dg)


def _tri(c, strict):
    i = lax.broadcasted_iota(jnp.int32, (c, c), 0)
    j = lax.broadcasted_iota(jnp.int32, (c, c), 1)
    return (i > j) if strict else (i >= j)


def _inv_unit_lower(ls):
    c = ls[0].shape[0]
    i = lax.broadcasted_iota(jnp.int32, (c, c), 0)
    j = lax.broadcasted_iota(jnp.int32, (c, c), 1)
    eye = jnp.where(i == j, 1.0, 0.0)
    facs = [[eye - l for l in ls]]
    cur = ls
    for _ in range(int(math.log2(c)) - 1):
        cur = [_dot(p, p, NN, TRI_PREC) for p in cur]
        facs.append([eye + p for p in cur])
    while len(facs) > 1:
        nxt = [[_dot(a, b, NN, TRI_PREC) for a, b in zip(facs[t], facs[t + 1])] for t in range(0, len(facs) - 1, 2)]
        if len(facs) % 2:
            nxt.append(facs[-1])
        facs = nxt
    return facs[0]


def _chunk_gates(g_blk):
    c = g_blk.shape[0]
    gcs = _dot(jnp.where(_tri(c, False), 1.0, 0.0), g_blk, NN, HI)
    return gcs, gcs.T


def _head_chunk(h, qh, kh, vh, beta_blk, gcs, gcs_t):
    c = qh.shape[0]
    incl = _tri(c, False)
    gc_col = gcs[:, h:h + 1]
    gc_row = gcs_t[h:h + 1, :]
    gc_last = gcs_t[h:h + 1, c - 1:c]
    dec = jnp.where(incl, jnp.exp(jnp.where(incl, gc_col - gc_row, 0.0)), 0.0)
    gam = jnp.exp(gc_col)
    rr = jnp.exp(gc_last - gc_col)
    gl = jnp.exp(gc_last)
    b = beta_blk[:, h:h + 1]
    kb = kh * b
    vb = vh * b
    kk = _dot(_mx(kb), _mx(kh), NT)
    lmat = jnp.where(_tri(c, True), kk * dec, 0.0)
    qk = _dot(_mx(qh), _mx(kh), NT)
    pmat = jnp.where(incl, qk * dec, 0.0)
    return dict(dec=dec, gam=gam, rr=rr, gl=gl, b=b, kb=kb, vb=vb, lmat=lmat, pmat=pmat)


def _dn_scan_fwd(qkv, beta, g, proj, norm_g, name):
    T = qkv.shape[0]
    C, H, Dh = DN_CHUNK, DN_HEADS, DN_HEAD_DIM
    W = H * Dh
    N = T // C

    def body(q_ref, k_ref, v_ref, beta_ref, g_ref, z_ref, ng_ref, og_ref, o_ref, tinv_ref, s_ref, state):
        @pl.when(pl.program_id(0) == 0)
        def _():
            state[...] = jnp.zeros_like(state)

        gcs, gcs_t = _chunk_gates(g_ref[...])
        beta_blk = beta_ref[...]
        ng = ng_ref[...]
        heads = range(H)
        cs = [slice(h * Dh, (h + 1) * Dh) for h in heads]
        qs = [_head_chunk(h, q_ref[:, cs[h]], k_ref[:, cs[h]], v_ref[:, cs[h]], beta_blk, gcs, gcs_t) for h in heads]
        tinvs = _inv_unit_lower([q["lmat"] for q in qs])
        for h in heads:
            tinv_ref[h] = tinvs[h]
        us = [_dot(tinvs[h], qs[h]["vb"], NN, TRI_PREC) for h in heads]
        ws = [_dot(tinvs[h], qs[h]["kb"] * qs[h]["gam"], NN, TRI_PREC) for h in heads]
        ss = [state[h] for h in heads]
        for h in heads:
            s_ref[0, h] = ss[h]
        sbs = [_mx(s) for s in ss]
        vnbs = [_mx(us[h] - _dot(_mx(ws[h]), sbs[h], NN)) for h in heads]
        os_ = [_dot(_mx(q_ref[:, cs[h]] * qs[h]["gam"]), sbs[h], NN) + _dot(_mx(qs[h]["pmat"]), vnbs[h], NN) for h in heads]
        for h in heads:
            state[h] = ss[h] * qs[h]["gl"] + _dot(_mx((k_ref[:, cs[h]] * qs[h]["rr"]).T), vnbs[h], NN)
        for h in heads:
            o = os_[h]
            o_ref[:, cs[h]] = o
            zh = z_ref[:, cs[h]]
            og_ref[:, cs[h]] = _mx(o * _rms_stat(o) * ng * (zh * _sigmoid(zh)))

    blk = lambda j: pl.BlockSpec((C, W), lambda n: (n, j))
    small = pl.BlockSpec((C, LANES), lambda n: (n, 0))
    return pl.pallas_call(
        body, name=name, grid=(N,),
        in_specs=[blk(0), blk(1), blk(2), small, small, blk(3), _full((1, Dh))],
        out_specs=[blk(0), blk(0), pl.BlockSpec((H, C, C), lambda n: (0, n, 0)),
                   pl.BlockSpec((1, H, Dh, Dh), lambda n: (n, 0, 0, 0))],
        out_shape=[jax.ShapeDtypeStruct((T, W), MXU_DTYPE), jax.ShapeDtypeStruct((T, W), F32),
                   jax.ShapeDtypeStruct((H, T, C), F32), jax.ShapeDtypeStruct((N, H, Dh, Dh), F32)],
        scratch_shapes=[pltpu.VMEM((H, Dh, Dh), F32)],
        compiler_params=_cp("arbitrary"),
    )(qkv, qkv, qkv, beta, g, proj, norm_g)


def _dn_scan_bwd(qkv, beta, g, proj, norm_g, o, tinv, s_all, dog, name):
    T = qkv.shape[0]
    C, H, Dh = DN_CHUNK, DN_HEADS, DN_HEAD_DIM
    W = H * Dh
    N = T // C

    def body(q_ref, k_ref, v_ref, beta_ref, g_ref, z_ref, ng_ref, o_ref, tinv_ref, s_ref, dog_ref,
             dqkv_ref, dbeta_ref, dg_ref, dz_ref, dng_ref, dstate):
        @pl.when(pl.program_id(0) == 0)
        def _():
            dstate[...] = jnp.zeros_like(dstate)
            dng_ref[...] = jnp.zeros_like(dng_ref)

        gcs, gcs_t = _chunk_gates(g_ref[...])
        beta_blk = beta_ref[...]
        ng = ng_ref[...]
        incl = _tri(C, False)
        strict = _tri(C, True)
        lane = lax.broadcasted_iota(jnp.int32, (C, LANES), 1)
        rowi = lax.broadcasted_iota(jnp.int32, (C, 1), 0)
        ones = jnp.ones((C, LANES), F32)
        dbeta_acc = jnp.zeros((C, LANES), F32)
        dgc_acc = jnp.zeros((C, LANES), F32)
        dng_acc = jnp.zeros((1, Dh), F32)
        heads = range(H)
        cs = [slice(h * Dh, (h + 1) * Dh) for h in heads]
        rsum = lambda t: jnp.sum(t, axis=1, keepdims=True)
        dobs = []
        for h in heads:
            oh, zh, dogh = o_ref[:, cs[h]], z_ref[:, cs[h]], dog_ref[:, cs[h]]
            rstat = _rms_stat(oh)
            sz = _sigmoid(zh)
            dz_ref[:, cs[h]] = _mx(dogh * (oh * rstat * ng) * (sz * (1.0 + zh * (1.0 - sz))))
            do, dng = _rms_bwd(oh, rstat, ng, dogh * (zh * sz))
            dng_acc = dng_acc + dng
            dobs.append(_mx(do))
        qs = [_head_chunk(h, q_ref[:, cs[h]], k_ref[:, cs[h]], v_ref[:, cs[h]], beta_blk, gcs, gcs_t) for h in heads]
        tms = [tinv_ref[h] for h in heads]
        us = [_dot(tms[h], qs[h]["vb"], NN, TRI_PREC) for h in heads]
        ws = [_dot(tms[h], qs[h]["kb"] * qs[h]["gam"], NN, TRI_PREC) for h in heads]
        ss = [s_ref[0, h] for h in heads]
        sbs = [_mx(s) for s in ss]
        wbs = [_mx(w) for w in ws]
        vnbs = [_mx(us[h] - _dot(wbs[h], sbs[h], NN)) for h in heads]
        dsns = [dstate[h] for h in heads]
        dsbs = [_mx(d) for d in dsns]
        dvnews = [_dot(_mx(qs[h]["pmat"]), dobs[h], TN) + _dot(_mx(k_ref[:, cs[h]] * qs[h]["rr"]), dsbs[h], NN) for h in heads]
        dvb16s = [_mx(d) for d in dvnews]
        dps = [jnp.where(incl, _dot(dobs[h], vnbs[h], NT), 0.0) for h in heads]
        dqds = [_dot(dobs[h], sbs[h], NT) for h in heads]
        dkds = [_dot(vnbs[h], dsbs[h], NT) for h in heads]
        dgls = [jnp.sum(rsum(ss[h] * dsns[h]), axis=0, keepdims=True) for h in heads]
        dws = [-_dot(dvb16s[h], sbs[h], NT) for h in heads]
        for h in heads:
            dstate[h] = (_dot(_mx(q_ref[:, cs[h]] * qs[h]["gam"]), dobs[h], TN) + qs[h]["gl"] * dsns[h]
                         - _dot(wbs[h], dvb16s[h], TN))
        dvbs = [_dot(tms[h], dvnews[h], TN, TRI_PREC) for h in heads]
        dkbgs = [_dot(tms[h], dws[h], TN, TRI_PREC) for h in heads]
        dls = [jnp.where(strict, -(_dot(dvbs[h], us[h], NT, TRI_PREC) + _dot(dkbgs[h], ws[h], NT, TRI_PREC)), 0.0)
               for h in heads]
        mmats = [dls[h] * qs[h]["lmat"] + dps[h] * qs[h]["pmat"] for h in heads]
        dgcs = [rsum(mmats[h]) - _dot(mmats[h], ones, TN, HI)[:, :1] for h in heads]
        dkk16s = [_mx(dls[h] * qs[h]["dec"]) for h in heads]
        dqk16s = [_mx(dps[h] * qs[h]["dec"]) for h in heads]
        for h in heads:
            q = qs[h]
            qh, kh, vh = q_ref[:, cs[h]], k_ref[:, cs[h]], v_ref[:, cs[h]]
            gam, rr, b, kb = q["gam"], q["rr"], q["b"], q["kb"]
            dkb = _dot(dkk16s[h], _mx(kh), NN) + dkbgs[h] * gam
            dk = _dot(dkk16s[h], _mx(kb), TN) + _dot(dqk16s[h], _mx(qh), TN) + dkb * b + dkds[h] * rr
            dq = _dot(dqk16s[h], _mx(kh), NN) + dqds[h] * gam
            dgam = rsum(dkbgs[h] * kb) + rsum(dqds[h] * qh)
            dr = rsum(dkds[h] * kh)
            dgc_last = jnp.sum(dr * rr, axis=0, keepdims=True) + dgls[h] * q["gl"]
            dgc = dgcs[h] + dgam * gam - dr * rr + jnp.where(rowi == C - 1, dgc_last, 0.0)
            dbeta = rsum(dvbs[h] * vh) + rsum(dkb * kh)
            dqkv_ref[:, cs[h]] = dq
            dqkv_ref[:, W + h * Dh:W + (h + 1) * Dh] = dk
            dqkv_ref[:, 2 * W + h * Dh:2 * W + (h + 1) * Dh] = dvbs[h] * b
            dbeta_acc = jnp.where(lane == h, dbeta, dbeta_acc)
            dgc_acc = jnp.where(lane == h, dgc, dgc_acc)
        dbeta_ref[...] = dbeta_acc
        dg_ref[...] = _dot(jnp.where(incl, 1.0, 0.0), dgc_acc, TN, HI)
        dng_ref[...] += dng_acc

    rev = lambda n: N - 1 - n
    blk = lambda j: pl.BlockSpec((C, W), lambda n: (rev(n), j))
    small = pl.BlockSpec((C, LANES), lambda n: (rev(n), 0))
    return pl.pallas_call(
        body, name=name, grid=(N,),
        in_specs=[blk(0), blk(1), blk(2), small, small, blk(3), _full((1, Dh)), blk(0),
                  pl.BlockSpec((H, C, C), lambda n: (0, rev(n), 0)),
                  pl.BlockSpec((1, H, Dh, Dh), lambda n: (rev(n), 0, 0, 0)), blk(0)],
        out_specs=[pl.BlockSpec((C, 3 * W), lambda n: (rev(n), 0)), small, small, blk(0), _full((1, Dh))],
        out_shape=[jax.ShapeDtypeStruct((T, 3 * W), F32), jax.ShapeDtypeStruct((T, LANES), F32),
                   jax.ShapeDtypeStruct((T, LANES), F32), jax.ShapeDtypeStruct((T, W), MXU_DTYPE),
                   jax.ShapeDtypeStruct((1, Dh), F32)],
        scratch_shapes=[pltpu.VMEM((H, Dh, Dh), F32)],
        compiler_params=_cp("arbitrary"),
    )(qkv, qkv, qkv, beta, g, proj, norm_g, o, tinv, s_all, dog)


_INV_SQRT2 = 0.7071067811865476
_INV_SQRT_2PI = 0.3989422804014327


def _sg_recompute(zp_ref, bin_ref, lng_ref, lnb_ref):
    E = SG_WIDTH
    zin = zp_ref[...] + bin_ref[...]
    cdf = 0.5 * (1.0 + lax.erf(zin * _INV_SQRT2))
    zz = zin * cdf
    u = zz[:, :E]
    vp = zz[:, E:]
    mu = jnp.mean(vp, axis=-1, keepdims=True)
    xc = vp - mu
    rstd = lax.rsqrt(jnp.mean(xc * xc, axis=-1, keepdims=True) + LN_EPS)
    xhat = xc * rstd
    v = xhat * lng_ref[...] + lnb_ref[...]
    return zin, cdf, u, xhat, rstd, v


def _sg_masked_ws(ws_ref, g):
    return _mx(jnp.where(_tri(SG_CHUNK, False), ws_ref[g], 0.0))


def _sg_fwd(zpre, b_in, ln_g, ln_b, w_s, b_s_t, name):
    T = zpre.shape[0]
    E, G, C, GW = SG_WIDTH, SG_GROUPS, SG_CHUNK, SG_GROUP_W

    def body(zp_ref, bin_ref, lng_ref, lnb_ref, ws_ref, bst_ref, um_ref):
        _, _, u, _, _, v = _sg_recompute(zp_ref, bin_ref, lng_ref, lnb_ref)
        bst = bst_ref[...]
        for g in range(G):
            cs = slice(g * GW, (g + 1) * GW)
            mixed = _dot(_sg_masked_ws(ws_ref, g), _mx(v[:, cs]), NN) + bst[:, g:g + 1]
            um_ref[:, cs] = _mx(u[:, cs] * mixed)

    return pl.pallas_call(
        body, name=name, grid=(T // C,),
        in_specs=[pl.BlockSpec((C, 2 * E), lambda n: (n, 0)), _full((1, 2 * E)), _full((1, E)), _full((1, E)),
                  _full((G, C, C)), _full((C, LANES))],
        out_specs=pl.BlockSpec((C, E), lambda n: (n, 0)),
        out_shape=jax.ShapeDtypeStruct((T, E), MXU_DTYPE), compiler_params=_cp("parallel"),
    )(zpre, b_in, ln_g, ln_b, w_s, b_s_t)


def _sg_bwd(zpre, b_in, ln_g, ln_b, w_s, b_s_t, dum, name):
    T = zpre.shape[0]
    E, G, C, GW = SG_WIDTH, SG_GROUPS, SG_CHUNK, SG_GROUP_W

    def body(zp_ref, bin_ref, lng_ref, lnb_ref, ws_ref, bst_ref, dum_ref,
             dz_ref, dbin_ref, dlng_ref, dlnb_ref, dws_ref, dbst_ref):
        @pl.when(pl.program_id(0) == 0)
        def _():
            for r in (dbin_ref, dlng_ref, dlnb_ref, dws_ref, dbst_ref):
                r[...] = jnp.zeros_like(r)

        zin, cdf, u, xhat, rstd, v = _sg_recompute(zp_ref, bin_ref, lng_ref, lnb_ref)
        bst = bst_ref[...]
        lane = lax.broadcasted_iota(jnp.int32, (C, LANES), 1)
        dum_v = dum_ref[...]
        dbst = jnp.zeros((C, LANES), F32)
        du_parts, dv_parts = [], []
        for g in range(G):
            cs = slice(g * GW, (g + 1) * GW)
            wsm = _sg_masked_ws(ws_ref, g)
            vg = _mx(v[:, cs])
            mixed = _dot(wsm, vg, NN) + bst[:, g:g + 1]
            dumg = dum_v[:, cs]
            du_parts.append(dumg * mixed)
            dmixed = dumg * u[:, cs]
            dmb = _mx(dmixed)
            dv_parts.append(_dot(wsm, dmb, TN))
            dws_ref[g] += _dot(dmb, vg, NT)
            dbst = jnp.where(lane == g, jnp.sum(dmixed, axis=1, keepdims=True), dbst)
        dbst_ref[...] += dbst
        du = jnp.concatenate(du_parts, axis=1)
        dv = jnp.concatenate(dv_parts, axis=1)
        dlng_ref[...] += jnp.sum(dv * xhat, axis=0, keepdims=True)
        dlnb_ref[...] += jnp.sum(dv, axis=0, keepdims=True)
        dxh = dv * lng_ref[...]
        dvp = rstd * (dxh - jnp.mean(dxh, axis=-1, keepdims=True) - xhat * jnp.mean(dxh * xhat, axis=-1, keepdims=True))
        dzz = jnp.concatenate([du, dvp], axis=1)
        dzin = dzz * (cdf + zin * (_INV_SQRT_2PI * jnp.exp(-0.5 * zin * zin)))
        dz_ref[...] = _mx(dzin)
        dbin_ref[...] += jnp.sum(dzin, axis=0, keepdims=True)

    return pl.pallas_call(
        body, name=name, grid=(T // C,),
        in_specs=[pl.BlockSpec((C, 2 * E), lambda n: (n, 0)), _full((1, 2 * E)), _full((1, E)), _full((1, E)),
                  _full((G, C, C)), _full((C, LANES)), pl.BlockSpec((C, E), lambda n: (n, 0))],
        out_specs=[pl.BlockSpec((C, 2 * E), lambda n: (n, 0)), _full((1, 2 * E)), _full((1, E)), _full((1, E)),
                   _full((G, C, C)), _full((C, LANES))],
        out_shape=[jax.ShapeDtypeStruct((T, 2 * E), MXU_DTYPE), jax.ShapeDtypeStruct((1, 2 * E), F32),
                   jax.ShapeDtypeStruct((1, E), F32), jax.ShapeDtypeStruct((1, E), F32),
                   jax.ShapeDtypeStruct((G, C, C), F32), jax.ShapeDtypeStruct((C, LANES), F32)],
        compiler_params=_cp("arbitrary"),
    )(zpre, b_in, ln_g, ln_b, w_s, b_s_t, dum)


def _row(v):
    return v.reshape(1, -1)


def _pad_lanes(v):
    v = v.reshape(1, -1)
    return jnp.pad(v, ((0, 0), (0, LANES - v.shape[1])))


def _local_step(x, target, p, weights_for, grads_ready=None, small_ready=None):
    ng = p["norm_g"]
    grads = {}
    dng = [[None] * 6 for _ in range(2)]
    order = [jnp.zeros((), F32)]

    def tell(group):
        zero = grads_ready(group, grads) if grads_ready is not None else None
        if zero is not None:
            order[0] = zero

    def gain(i, s):
        return _row(ng[i, s]) + order[0]

    def ffn_f(xin, i, j, tag):
        wt = weights_for("ffn" + tag, xin)
        xo, h, gu, y = _ffn_fwd(xin, _row(ng[i, 4 * j]), _row(ng[i, 4 * j + 1]), wt, "ffn_fwd_" + tag)
        return xo, (xin, h, gu, y, wt)

    x1, sv_f00 = ffn_f(x, 0, 0, "00")
    dnw = weights_for("dn", x1)
    hn0 = _norm_fwd(x1, _row(ng[0, 2]), "dn_prenorm")
    proj = _mm(hn0, dnw["dn_wqkvz"], "nn", "dn_proj")
    ba = _mm(hn0, dnw["dn_wba"], "nn", "dn_proj_ba")
    a_log = _pad_lanes(p["dn_a_log"])
    dt_bias = _pad_lanes(p["dn_dt_bias"])
    dn_ng = _row(p["dn_norm_g"])
    qkv = _dn_prep_fwd(proj, p["dn_conv_w"], "dn_prep_fwd")
    beta, gdec = _dn_gate_fwd(ba, a_log, dt_bias, "dn_gate_fwd")
    og, o_raw, tinv, s_all = _dn_scan_fwd(qkv, beta, gdec, proj, dn_ng, "dn_scan_fwd")
    m0 = _mm(og, dnw["dn_wout"], "nn", "dn_out")
    x2 = _postnorm_fwd(x1, m0, _row(ng[0, 3]), "dn_postnorm")
    x3, sv_f01 = ffn_f(x2, 0, 1, "01")
    x4, sv_f10 = ffn_f(x3, 1, 0, "10")
    sgw = weights_for("sg", x4)
    hn1 = _norm_fwd(x4, _row(ng[1, 2]), "sg_prenorm")
    zpre = _mm(hn1, sgw["sg_win"], "nn", "sg_proj")
    sg_bin = _row(p["sg_b_in"])
    sg_lng = _row(p["sg_ln_g"])
    sg_lnb = _row(p["sg_ln_b"])
    sg_bst = jnp.pad(p["sg_b_s"].T, ((0, 0), (0, LANES - SG_GROUPS)))
    um = _sg_fwd(zpre, sg_bin, sg_lng, sg_lnb, p["sg_w_s"], sg_bst, "sg_fwd")
    m1 = _mm(um, sgw["sg_wout"], "nn", "sg_out")
    x5 = _postnorm_fwd(x4, m1, _row(ng[1, 3]), "sg_postnorm")
    x6, sv_f11 = ffn_f(x5, 1, 1, "11")
    loss_part, dx = _loss_fwd_bwd(x6, target, "loss")

    def ffn_b(dxo, sv, i, j, tag, last=False):
        xin, h, gu, y, wt = sv
        dxi, dy, a, dgu, dg0, dg1 = _ffn_bwd(dxo, xin, y, gu, gain(i, 4 * j), gain(i, 4 * j + 1), wt, "ffn_bwd_" + tag)
        dng[i][4 * j] = dg0
        dng[i][4 * j + 1] = dg1
        after = None
        if last:
            grads["norm_g"] = jnp.stack([jnp.concatenate(dng[t], axis=0) for t in range(2)])
            after = small_ready(grads, loss_part) if small_ready is not None else None
        grads["wd" + tag] = _mm(a, dy, "tn", "ffn_wgrad_down_" + tag, after=after)
        grads["wguT" + tag] = _mm(dgu, h, "tn", "ffn_wgrad_up_" + tag, after=after)
        tell("ffn" + tag)
        return dxi

    dx = ffn_b(dx, sv_f11, 1, 1, "11")
    dm1, dng[1][3] = _postnorm_bwd(dx, m1, gain(1, 3), "sg_postnorm_bwd")
    grads["sg_w_out"] = _mm(um, dm1, "tn", "sg_wgrad_out")
    dum = _mm(dm1, sgw["sg_wout"], "nt", "sg_dgrad_out")
    dz1, dbin, dlng, dlnb, dws, dbst = _sg_bwd(zpre, sg_bin, sg_lng, sg_lnb, p["sg_w_s"], sg_bst, dum, "sg_bwd")
    grads["sg_w_inT"] = _mm(dz1, hn1, "tn", "sg_wgrad_in")
    tell("sg")
    dh1 = _mm(dz1, sgw["sg_win"], "nt", "sg_dgrad_in")
    dx, dng[1][2] = _prenorm_bwd(dx, dh1, x4, gain(1, 2), "sg_prenorm_bwd")
    grads["sg_b_in"] = dbin.reshape(1, -1)
    grads["sg_ln_g"] = dlng.reshape(1, -1)
    grads["sg_ln_b"] = dlnb.reshape(1, -1)
    grads["sg_w_s"] = jnp.where(jnp.tril(jnp.ones((SG_CHUNK, SG_CHUNK), bool)), dws, 0.0)[None]
    grads["sg_b_s"] = dbst[:, :SG_GROUPS].T[None]
    dx = ffn_b(dx, sv_f10, 1, 0, "10")
    dx = ffn_b(dx, sv_f01, 0, 1, "01")
    dm0, dng[0][3] = _postnorm_bwd(dx, m0, gain(0, 3), "dn_postnorm_bwd")
    grads["dn_w_out"] = _mm(og, dm0, "tn", "dn_wgrad_out")
    dog = _mm(dm0, dnw["dn_wout"], "nt", "dn_dgrad_out")
    dqkv, dbeta, dgdec, dz0, dnng = _dn_scan_bwd(qkv, beta, gdec, proj, dn_ng, o_raw, tinv, s_all, dog, "dn_scan_bwd")
    dqkv_pre, dconv = _dn_prep_bwd(proj, p["dn_conv_w"], dqkv, "dn_prep_bwd")
    dba, dal, ddt = _dn_gate_bwd(ba, a_log, dt_bias, dbeta, dgdec, "dn_gate_bwd")
    W3 = 3 * DN_HEADS * DN_HEAD_DIM
    dw_qkv = _mm(hn0, dqkv_pre, "tn", "dn_wgrad_qkv")
    dw_z = _mm(hn0, dz0, "tn", "dn_wgrad_z")
    dw_ba = _mm(hn0, dba, "tn", "dn_wgrad_ba")
    grads["dn_w_in"] = jnp.concatenate(
        [dw_qkv, dw_z, dw_ba[:, :DN_HEADS], dw_ba[:, LANES:LANES + DN_HEADS]], axis=1)
    tell("dn")
    dh0 = _mm(dqkv_pre, dnw["dn_wqkvz"][:, :W3], "nt", "dn_dgrad_qkv")
    dh0 = _mm(dz0, dnw["dn_wqkvz"][:, W3:], "nt", "dn_dgrad_z", add=dh0)
    dh0 = _mm(dba, dnw["dn_wba"], "nt", "dn_dgrad_ba", add=dh0)
    dx, dng[0][2] = _prenorm_bwd(dx, dh0, x1, gain(0, 2), "dn_prenorm_bwd")
    grads["dn_conv_w"] = dconv[None]
    grads["dn_a_log"] = dal[:, :DN_HEADS]
    grads["dn_dt_bias"] = ddt[:, :DN_HEADS]
    grads["dn_norm_g"] = dnng
    dx = ffn_b(dx, sv_f00, 0, 0, "00", last=True)
    return loss_part, dx, grads


def _mesh_pos():
    return lax.axis_index("x"), lax.axis_index("y"), lax.axis_index("c")


def _other_chips(x, y):
    return [(1 - x, y), (x, 1 - y), (1 - x, 1 - y)]


def _allgather_chips(arrs, name):
    n = len(arrs)

    def body(*refs):
        ins, outs = refs[:n], refs[n:2 * n]
        ici_send, ici_recv, d2d_send, d2d_recv = refs[2 * n:]
        x, y, c = _mesh_pos()
        me = 2 * x + y
        chips = _other_chips(x, y)
        sibling = (x, y, 1 - c)

        def ici(i, j, k):
            cx, cy = chips[j]
            return pltpu.make_async_remote_copy(src_ref=ins[i].at[c], dst_ref=outs[i].at[k, c], send_sem=ici_send.at[3 * i + j],
                                                recv_sem=ici_recv.at[3 * i + j], device_id=(cx, cy, c), device_id_type=MESH)

        def d2d(i, j, h):
            cx, cy = chips[j]
            slot = outs[i].at[2 * cx + cy, h]
            return pltpu.make_async_remote_copy(src_ref=slot, dst_ref=slot, send_sem=d2d_send.at[3 * i + j],
                                                recv_sem=d2d_recv.at[3 * i + j], device_id=sibling, device_id_type=MESH)

        sends = [ici(i, j, me) for i in range(n) for j in range(3)]
        for cp in sends:
            cp.start()
        for i in range(n):
            for j, (cx, cy) in enumerate(chips):
                ici(i, j, 2 * cx + cy).wait_recv()
                fwd = d2d(i, j, c)
                fwd.start()
                sends.append(fwd)
        for i in range(n):
            for j in range(3):
                d2d(i, j, 1 - c).wait_recv()
        for cp in sends:
            cp.wait_send()

    return pl.pallas_call(
        body, name=name, in_specs=[ANY] * n, out_specs=[ANY] * n,
        out_shape=[jax.ShapeDtypeStruct((N_CHIPS,) + a.shape, a.dtype) for a in arrs],
        scratch_shapes=[pltpu.SemaphoreType.DMA((3 * n,))] * 4,
    )(*arrs)


HBM = pl.BlockSpec(memory_space=pltpu.HBM)
SEM = pl.BlockSpec(memory_space=pltpu.SEMAPHORE)
TOKEN = jax.ShapeDtypeStruct((SUBLANES, LANES), F32)


_PEERS = {"gather": 3, "scatter": 3, "swap": 1, "all": N_DEV - 1}


def _land_shape(kind, shape):
    if kind == "gather":
        return (N_CHIPS,) + shape
    if kind == "all":
        return (N_DEV,) + shape
    return (N_CHIPS,) + shape[2:] if kind == "swap" else shape


def _peer_copies(kind, flags, src_refs, land_refs, send_sems, recv_sems, receiving):
    x, y, c = _mesh_pos()
    me4, me8 = 2 * x + y, 4 * x + 2 * y + c
    np_ = _PEERS[kind]
    cps = []
    for i, (src, land) in enumerate(zip(src_refs, land_refs)):
        if kind == "swap":
            half = src.at[1 - c] if flags[i] else src.at[:, 1 - c]
            plan = [((x, y, 1 - c), half, land)]
        elif kind == "all":
            masks = [(mx, my, mc) for mx in (0, 1) for my in (0, 1) for mc in (0, 1)][1:]
            peers = [(jnp.where(mx, 1 - x, x), jnp.where(my, 1 - y, y), jnp.where(mc, 1 - c, c)) for mx, my, mc in masks]
            plan = [(p, src, land.at[4 * p[0] + 2 * p[1] + p[2] if receiving else me8]) for p in peers]
        else:
            plan = []
            for cx, cy in _other_chips(x, y):
                k = 2 * cx + cy
                s = src.at[me4 if receiving else k] if kind == "scatter" else src
                plan.append(((cx, cy, c), s, land.at[k if receiving else me4]))
        for j, (peer, s, d) in enumerate(plan):
            cps.append(pltpu.make_async_remote_copy(src_ref=s, dst_ref=d, send_sem=send_sems.at[np_ * i + j],
                                                    recv_sem=recv_sems.at[np_ * i + j], device_id=peer, device_id_type=MESH))
    return cps


def _copies_start(kind, srcs, after, name, flags=None):
    n = len(srcs)
    ns = _PEERS[kind] * n
    lands = [lax.empty(_land_shape(kind, s.shape), s.dtype) for s in srcs]

    def body(*refs):
        src_refs, land_refs = refs[:n], refs[n:2 * n]
        send_sems, recv_sems = refs[2 * n + 1], refs[2 * n + 2]
        token = refs[-1]
        for cp in _peer_copies(kind, flags, src_refs, land_refs, send_sems, recv_sems, False):
            cp.start()
        token[...] = jnp.zeros_like(token)

    outs = pl.pallas_call(
        body, name=name,
        in_specs=[HBM] * (2 * n) + [ANY],
        out_specs=(SEM, SEM) + (HBM,) * (2 * n) + (pl.BlockSpec(memory_space=pltpu.VMEM),),
        out_shape=(pltpu.SemaphoreType.DMA((ns,)), pltpu.SemaphoreType.DMA((ns,)))
        + tuple(pltpu.HBM(a.shape, a.dtype) for a in list(srcs) + lands) + (TOKEN,),
        input_output_aliases={i: 2 + i for i in range(2 * n)},
        compiler_params=pltpu.CompilerParams(has_side_effects=pltpu.SideEffectType.DATAFLOW_SIDE_EFFECTING),
    )(*[pltpu.with_memory_space_constraint(a, pltpu.HBM) for a in list(srcs) + lands], after)
    return dict(sems=outs[:2], srcs=outs[2:2 + n], lands=outs[2 + n:2 + 2 * n], token=outs[-1], kind=kind, flags=flags)


def _copies_wait(started, after, name):
    n = len(started["srcs"])
    kind, flags = started["kind"], started["flags"]
    after = list(after) if isinstance(after, (list, tuple)) else [after]

    def body(*refs):
        src_refs, land_refs = refs[:n], refs[n:2 * n]
        send_sems, recv_sems = refs[2 * n], refs[2 * n + 1]
        for cp in _peer_copies(kind, flags, src_refs, land_refs, send_sems, recv_sems, True):
            cp.wait_send()
            cp.wait_recv()

    outs = pl.pallas_call(
        body, name=name,
        in_specs=[HBM] * (2 * n) + [SEM, SEM] + [ANY] * len(after),
        out_specs=(HBM,) * (2 * n),
        out_shape=tuple(pltpu.HBM(a.shape, a.dtype) for a in list(started["srcs"]) + list(started["lands"])),
        input_output_aliases={i: i for i in range(2 * n)},
        compiler_params=pltpu.CompilerParams(has_side_effects=pltpu.SideEffectType.DATAFLOW_SIDE_EFFECTING),
    )(*started["srcs"], *started["lands"], *started["sems"], *after)
    return outs[:n], outs[n:]


def _swap_whole(arrs, name):
    n = len(arrs)

    def body(*refs):
        ins, outs = refs[:n], refs[n:2 * n]
        send_sems, recv_sems = refs[2 * n:]
        x, y, c = _mesh_pos()
        cps = [pltpu.make_async_remote_copy(src_ref=ins[i], dst_ref=outs[i], send_sem=send_sems.at[i],
                                            recv_sem=recv_sems.at[i], device_id=(x, y, 1 - c), device_id_type=MESH)
               for i in range(n)]
        for cp in cps:
            cp.start()
        for cp in cps:
            cp.wait()

    return pl.pallas_call(
        body, name=name, in_specs=[ANY] * n, out_specs=[ANY] * n,
        out_shape=[jax.ShapeDtypeStruct(a.shape, a.dtype) for a in arrs],
        scratch_shapes=[pltpu.SemaphoreType.DMA((n,)), pltpu.SemaphoreType.DMA((n,))],
    )(*arrs)


def _as_rows(a, lead):
    shp = a.shape
    rows = 1
    for s in shp[lead:-1]:
        rows *= s
    return a.reshape(shp[:lead] + (rows, shp[-1]))


def _row_tile(rows, cols, n_bufs):
    budget = (24 * 1024 * 1024) // (n_bufs * 2 * 4 * cols)
    return _pick(rows, max(2 * SUBLANES, budget), 2 * SUBLANES)


def _sum_devices(own, got, dev, name):
    n, rows, cols = got.shape
    tr = _row_tile(rows, cols, n + 2)

    def body(dev_ref, own_ref, got_ref, o_ref):
        mine = own_ref[...]
        acc = jnp.where(dev_ref[0] == 0, mine, got_ref[0])
        for k in range(1, n):
            acc = acc + jnp.where(dev_ref[0] == k, mine, got_ref[k])
        o_ref[...] = acc

    return pl.pallas_call(
        body, name=name,
        grid_spec=pltpu.PrefetchScalarGridSpec(
            num_scalar_prefetch=1, grid=(rows // tr,),
            in_specs=[pl.BlockSpec((tr, cols), lambda i, d: (i, 0)), pl.BlockSpec((n, tr, cols), lambda i, d: (0, i, 0))],
            out_specs=pl.BlockSpec((tr, cols), lambda i, d: (i, 0))),
        out_shape=jax.ShapeDtypeStruct((rows, cols), F32), compiler_params=_cp("parallel"),
    )(_scalar(dev), own, got)


def _scalar(i):
    return jnp.reshape(i, (1,)).astype(jnp.int32)


def _add_own_half(g, other, c, half_first, name):
    _, rows, cols = other.shape
    tr = _row_tile(rows, cols, 3)

    def body(c_ref, g_ref, o_ref, out_ref):
        out_ref[0] = (g_ref[0, 0] + o_ref[0]).astype(out_ref.dtype)

    if half_first:
        g_map = lambda k, i, c_ref: (c_ref[0], k, i, 0)
    else:
        g_map = lambda k, i, c_ref: (k, c_ref[0], i, 0)
    flat = pl.BlockSpec((1, tr, cols), lambda k, i, c_ref: (k, i, 0))
    return pl.pallas_call(
        body, name=name,
        grid_spec=pltpu.PrefetchScalarGridSpec(
            num_scalar_prefetch=1, grid=(N_CHIPS, rows // tr),
            in_specs=[pl.BlockSpec((1, 1, tr, cols), g_map), flat], out_specs=flat),
        out_shape=jax.ShapeDtypeStruct(other.shape, COMM_DTYPE), compiler_params=_cp("parallel", "parallel"),
    )(_scalar(c), g, other)


def _sum_chips(own, got, chip, name):
    pv = _as_rows(own, 1)
    bv = _as_rows(got, 1)
    _, rows, cols = pv.shape
    tr = _row_tile(rows, cols, N_CHIPS + 2)

    def body(chip_ref, p_ref, b_ref, o_ref):
        mine = p_ref[0].astype(F32)
        acc = jnp.where(chip_ref[0] == 0, mine, b_ref[0].astype(F32))
        for k in range(1, N_CHIPS):
            acc = acc + jnp.where(chip_ref[0] == k, mine, b_ref[k].astype(F32))
        o_ref[...] = acc

    out = pl.pallas_call(
        body, name=name,
        grid_spec=pltpu.PrefetchScalarGridSpec(
            num_scalar_prefetch=1, grid=(rows // tr,),
            in_specs=[pl.BlockSpec((1, tr, cols), lambda i, k_ref: (k_ref[0], i, 0)),
                      pl.BlockSpec((N_CHIPS, tr, cols), lambda i, k_ref: (0, i, 0))],
            out_specs=pl.BlockSpec((tr, cols), lambda i, k_ref: (i, 0))),
        out_shape=jax.ShapeDtypeStruct((rows, cols), F32), compiler_params=_cp("parallel"),
    )(_scalar(chip), pv, bv)
    return out.reshape(own.shape[1:])


def _adam_math(w, g, m, v):
    nm = ADAM_B1 * m + (1.0 - ADAM_B1) * g
    nv = ADAM_B2 * v + (1.0 - ADAM_B2) * (g * g)
    m_hat = nm / (1.0 - ADAM_B1 ** ADAM_STEP)
    v_hat = nv / (1.0 - ADAM_B2 ** ADAM_STEP)
    return -ADAM_LR * (m_hat / (jnp.sqrt(v_hat) + ADAM_EPS) + ADAM_WD * w), nm, nv


def _adamw_halves(w, mine, theirs, m, v, c, name):
    shape = w.shape
    ws, ms, vs = (_as_rows(t.reshape((2, -1) + t.shape[-1:]), 1) for t in (w, m, v))
    a, b = _as_rows(mine, 0), _as_rows(theirs, 0)
    rows, cols = a.shape
    tr = _row_tile(rows, cols, 9)

    def body(c_ref, w_ref, a_ref, b_ref, m_ref, v_ref, g_ref, d_ref, nm_ref, nv_ref):
        gv = jnp.where(pl.program_id(0) == c_ref[0], a_ref[...], b_ref[...])
        g_ref[0] = gv
        d_ref[0], nm_ref[0], nv_ref[0] = _adam_math(w_ref[0], gv, m_ref[0], v_ref[0])

    half = pl.BlockSpec((1, tr, cols), lambda h, i, c_ref: (h, i, 0))
    flat = pl.BlockSpec((tr, cols), lambda h, i, c_ref: (i, 0))
    outs = pl.pallas_call(
        body, name=name,
        grid_spec=pltpu.PrefetchScalarGridSpec(num_scalar_prefetch=1, grid=(2, rows // tr),
                                               in_specs=[half, flat, flat, half, half], out_specs=[half] * 4),
        out_shape=[jax.ShapeDtypeStruct((2, rows, cols), F32)] * 4, compiler_params=_cp("parallel", "parallel"),
    )(_scalar(c), ws, a, b, ms, vs)
    return tuple(o.reshape(shape) for o in outs)


def _adamw(w, g, m, v, name):
    shape = w.shape
    ws, gs, ms, vs = (_as_rows(t, 0) for t in (w, g, m, v))
    rows, cols = ws.shape
    tr = _row_tile(rows, cols, 7)

    def body(w_ref, g_ref, m_ref, v_ref, d_ref, nm_ref, nv_ref):
        d_ref[...], nm_ref[...], nv_ref[...] = _adam_math(w_ref[...], g_ref[...], m_ref[...], v_ref[...])

    spec = pl.BlockSpec((tr, cols), lambda i: (i, 0))
    outs = pl.pallas_call(body, name=name, grid=(rows // tr,), in_specs=[spec] * 4, out_specs=[spec] * 3,
                          out_shape=[jax.ShapeDtypeStruct((rows, cols), F32)] * 3, compiler_params=_cp("parallel"))(ws, gs, ms, vs)
    return tuple(o.reshape(shape) for o in outs)


_BIG = ["ffn_w_gate", "ffn_w_up", "ffn_w_down", "dn_w_in", "dn_w_out", "sg_w_in", "sg_w_out"]
_SMALL_SHARDED = ["norm_g", "dn_conv_w", "sg_b_in", "sg_ln_g", "sg_ln_b"]
_SMALL_REPL = ["dn_a_log", "dn_dt_bias", "dn_norm_g", "sg_w_s", "sg_b_s"]
_WEIGHTS = ["norm_g", "ffn_w_gate", "ffn_w_up", "ffn_w_down", "dn_w_in", "dn_conv_w", "dn_a_log", "dn_dt_bias",
            "dn_norm_g", "dn_w_out", "sg_w_in", "sg_b_in", "sg_ln_g", "sg_ln_b", "sg_w_s", "sg_b_s", "sg_w_out"]
PACK_COLS = 1024


def _pack(arrs):
    flat = jnp.concatenate([a.reshape(-1) for a in arrs])
    pad = (-flat.shape[0]) % (SUBLANES * PACK_COLS)
    return jnp.pad(flat, (0, pad)).reshape(-1, PACK_COLS)


def _unpack(buf, shapes):
    flat = buf.reshape(-1)
    out, off = [], 0
    for s in shapes:
        n = math.prod(s)
        out.append(flat[off:off + n].reshape(s))
        off += n
    return out


def _as_halves(a):
    if a.shape[0] == 2:
        return a
    if a.shape[0] == 1:
        return a.reshape((2, a.shape[1] // 2) + a.shape[2:])
    return a.reshape((2, a.shape[0] // 2) + a.shape[1:])


def _with_own(gathered, own, chip):
    g = gathered.reshape((N_CHIPS,) + own.shape)
    return [jnp.where(chip == k, own, g[k]) for k in range(N_CHIPS)]


def _cat_shards(g, axis):
    return jnp.concatenate(list(g), axis=axis)


_GROUP_ORDER = ["ffn00", "dn", "ffn01", "ffn10", "sg", "ffn11"]


def _weight_groups(w):
    cast = {k: _mx(w[k]) for k in _BIG}
    groups = {"ffn%d%d" % (i, j): [cast["ffn_w_gate"][i, j].T, cast["ffn_w_up"][i, j].T, cast["ffn_w_down"][i, j]]
              for i, j in [(0, 0), (0, 1), (1, 0), (1, 1)]}
    groups["dn"] = [cast["dn_w_in"][0], cast["dn_w_out"][0]]
    groups["sg"] = [cast["sg_w_in"][0], cast["sg_w_out"][0]]
    return groups


def _ffn_weights(chip, own, gathered):
    pairs = [(a, g.reshape((N_CHIPS,) + a.shape)) for a, g in zip(own, gathered)]
    return {"chip": chip, "gate": pairs[0], "up": pairs[1], "down": pairs[2]}


def _group_matrices(group, shards):
    if group == "sg":
        return {"sg_win": _cat_shards(shards[0], 1), "sg_wout": _cat_shards(shards[1], 0)}
    dn_full = _cat_shards(shards[0], 1)
    W4 = 4 * DN_HEADS * DN_HEAD_DIM
    wba = jnp.zeros((D_MODEL, 2 * LANES), dn_full.dtype)
    wba = wba.at[:, :DN_HEADS].set(dn_full[:, W4:W4 + DN_HEADS])
    wba = wba.at[:, LANES:LANES + DN_HEADS].set(dn_full[:, W4 + DN_HEADS:])
    return {"dn_wqkvz": dn_full[:, :W4], "dn_wba": wba, "dn_wout": _cat_shards(shards[1], 0)}


def _split_cols(a, n):
    w = a.shape[-1] // n
    return [a[..., k * w:(k + 1) * w] for k in range(n)]


def _split_rows(a, n):
    h = a.shape[-2] // n
    return [a[..., k * h:(k + 1) * h, :] for k in range(n)]


_IJ = [(0, 0), (0, 1), (1, 0), (1, 1)]


_REDUCED = ["wguT%d%d" % ij for ij in _IJ] + ["wd%d%d" % ij for ij in _IJ] + ["dn_w_in", "dn_w_out", "sg_w_inT", "sg_w_out"]


def _group_grads(group, grads):
    def rows_by_chip(a):
        return a.reshape(N_CHIPS, 2, a.shape[0] // (2 * N_CHIPS), a.shape[1])

    if group.startswith("ffn"):
        tag = group[3:]
        t = grads["wguT" + tag]
        return (["wguT" + tag, "wd" + tag],
                [t.reshape(2, N_CHIPS, t.shape[0] // (2 * N_CHIPS), t.shape[1]), rows_by_chip(grads["wd" + tag])], [True, False])
    if group == "sg":
        return ["sg_w_inT", "sg_w_out"], [rows_by_chip(grads["sg_w_inT"]), rows_by_chip(grads["sg_w_out"])], [False, False]
    dn_in = jnp.stack([jnp.stack(_split_cols(hf, N_CHIPS)) for hf in _split_rows(grads["dn_w_in"], 2)])
    return ["dn_w_in", "dn_w_out"], [dn_in, rows_by_chip(grads["dn_w_out"])], [True, False]


def _shard_grads(mine, theirs, c, w):
    lo = [jnp.where(c == 0, a, b) for a, b in zip(mine, theirs)]
    hi = [jnp.where(c == 0, b, a) for a, b in zip(mine, theirs)]
    rows = lambda t: jnp.concatenate([lo[t], hi[t]], axis=0)
    sq = lambda parts: jnp.stack(parts).reshape(2, 2, *parts[0].shape)
    g = {}
    g["ffn_w_gate"] = sq([lo[t].T for t in range(4)])
    g["ffn_w_up"] = sq([hi[t].T for t in range(4)])
    g["ffn_w_down"] = sq([rows(4 + t) for t in range(4)])
    g["dn_w_in"] = rows(8)[None]
    g["dn_w_out"] = rows(9)[None]
    g["sg_w_in"] = rows(10).T[None]
    g["sg_w_out"] = rows(11)[None]
    return {k: v.reshape(w[k].shape) for k, v in g.items()}


def kernel(x, norm_g, ffn_w_gate, ffn_w_up, ffn_w_down, dn_w_in, dn_conv_w, dn_a_log, dn_dt_bias, dn_norm_g, dn_w_out, sg_w_in, sg_b_in, sg_ln_g, sg_ln_b, sg_w_s, sg_b_s, sg_w_out, loss_target, m_norm_g, m_ffn_w_gate, m_ffn_w_up, m_ffn_w_down, m_dn_w_in, m_dn_conv_w, m_dn_a_log, m_dn_dt_bias, m_dn_norm_g, m_dn_w_out, m_sg_w_in, m_sg_b_in, m_sg_ln_g, m_sg_ln_b, m_sg_w_s, m_sg_b_s, m_sg_w_out, v_norm_g, v_ffn_w_gate, v_ffn_w_up, v_ffn_w_down, v_dn_w_in, v_dn_conv_w, v_dn_a_log, v_dn_dt_bias, v_dn_norm_g, v_dn_w_out, v_sg_w_in, v_sg_b_in, v_sg_ln_g, v_sg_ln_b, v_sg_w_s, v_sg_b_s, v_sg_w_out):
    args = dict(locals())
    w = {k: args[k] for k in _WEIGHTS}
    mom = {k: args["m_" + k] for k in _WEIGHTS}
    var = {k: args["v_" + k] for k in _WEIGHTS}
    cx, cy, cc = _mesh_pos()
    chip = 2 * cx + cy

    small_shapes = [w[k].shape for k in _SMALL_SHARDED]
    groups = _weight_groups(w)
    own = groups[_GROUP_ORDER[0]] + [_pack([w[k] for k in _SMALL_SHARDED])]
    first = _allgather_chips([_as_halves(a) for a in own], "gather_first")
    started, after = {}, first[0]
    for g in _GROUP_ORDER[1:]:
        started[g] = _copies_start("gather", groups[g], after, "gather_start_" + g)
        after = started[g]["token"]
    small_k = [_unpack(pack, small_shapes) for pack in _with_own(first[-1], own[-1], chip)]
    p = {name: jnp.concatenate([small_k[k][i] for k in range(N_CHIPS)], axis=-1) for i, name in enumerate(_SMALL_SHARDED)}
    p = {k: (v if k == "norm_g" else v[0]) for k, v in p.items()}
    p["norm_g"] = p["norm_g"] + after[0, 0]
    for k in _SMALL_REPL:
        p[k] = w[k][0]

    def weights_for(group, after):
        if group == _GROUP_ORDER[0]:
            return _ffn_weights(chip, own[:-1], first[:-1])
        srcs, lands = _copies_wait(started[group], after, "gather_wait_" + group)
        if group.startswith("ffn"):
            return _ffn_weights(chip, srcs, lands)
        return _group_matrices(group, [_with_own(l, a, chip) for l, a in zip(lands, srcs)])

    mine, theirs, to_core, to_chips = {}, {}, [], []

    def send_to_chips(after):
        group, names, flags, swap = to_core.pop(0)
        halves, got = _copies_wait(swap, after, "swap_wait_" + group)
        pair_sum = [_add_own_half(h, o, cc, hf, "pair_sum_" + n) for n, h, o, hf in zip(names, halves, got, flags)]
        scatter = _copies_start("scatter", pair_sum, got[0], "reduce_start_" + group)
        to_chips.append((group, names, scatter))
        return scatter["token"]

    def finish(after):
        group, names, scatter = to_chips.pop(0)
        pair_sum, got = _copies_wait(scatter, after, "reduce_wait_" + group)
        half_sum = [_sum_chips(a, b, chip, "chip_sum_" + n) for n, a, b in zip(names, pair_sum, got)]
        other = _swap_whole(half_sum, "gather_core_pair_" + group)
        mine.update(zip(names, half_sum))
        theirs.update(zip(names, other))

    def grads_ready(group, grads):
        names, halves, flags = _group_grads(group, grads)
        swap = _copies_start("swap", halves, halves[0], "swap_start_" + group, flags)
        token = swap["token"]
        if to_core:
            token = send_to_chips(token)
            if len(to_chips) > 1:
                finish(token)
        to_core.append((group, names, flags, swap))
        return token[0, 0]

    small_names = _SMALL_SHARDED + _SMALL_REPL
    small = {}

    def small_ready(grads, loss_part):
        parts = [grads[k] for k in small_names]
        small["shapes"] = [g.shape for g in parts] + [(1,)]
        pack = _pack(parts + [loss_part[0, :1]])
        small["exchange"] = _copies_start("all", [pack], pack, "small_start")
        return small["exchange"]["token"]

    loss_part, grad_x, grads = _local_step(x[0], loss_target[0], p, weights_for, grads_ready, small_ready)
    token = send_to_chips(to_core[0][3]["token"])
    finish(token)
    (pack,), (packs,) = _copies_wait(small["exchange"], list(theirs.values()), "small_wait")
    summed = _sum_devices(pack, packs, 4 * cx + 2 * cy + cc, "small_sum")
    finish([summed] + list(theirs.values()))
    half_sum = [mine[n] for n in _REDUCED]
    other_half = [theirs[n] for n in _REDUCED]
    full_shapes = small["shapes"]
    parts = _unpack(summed, full_shapes)
    loss = parts[-1][0]
    small_grad = {}
    for i, k in enumerate(small_names):
        g = parts[i]
        if k in _SMALL_SHARDED:
            n = w[k].shape[-1]
            g = lax.dynamic_slice_in_dim(g, chip * n, n, axis=g.ndim - 1)
        small_grad[k] = g

    grad = {**small_grad, **_shard_grads(half_sum, other_half, cc, w)}
    delta, new_m, new_v = {}, {}, {}
    for k in _BIG:
        delta[k], new_m[k], new_v[k] = _adamw(w[k], grad[k], mom[k], var[k], "adamw_" + k)
    shapes = [w[k].shape for k in small_names]
    d, nm, nv = _adamw(_pack([w[k] for k in small_names]), _pack([grad[k] for k in small_names]),
                       _pack([mom[k] for k in small_names]), _pack([var[k] for k in small_names]), "adamw_small")
    for k, a, b, c_ in zip(small_names, _unpack(d, shapes), _unpack(nm, shapes), _unpack(nv, shapes)):
        delta[k], new_m[k], new_v[k] = a, b, c_

    return (loss, grad_x[None], *[grad[k] for k in _WEIGHTS], *[delta[k] for k in _WEIGHTS],
            *[new_m[k] for k in _WEIGHTS], *[new_v[k] for k in _WEIGHTS])
```

```python
import functools
import math

import jax
import jax.numpy as jnp
from jax import lax
from jax.experimental import pallas as pl
from jax.experimental.pallas import tpu as pltpu

F32 = jnp.float32
MXU_DTYPE = jnp.bfloat16
COMM_DTYPE = jnp.bfloat16
HI = lax.Precision.HIGHEST
TRI_PREC = lax.Precision.HIGH

D_MODEL = 1024
D_FF = 2816
RMS_EPS = 1e-6
LN_EPS = 1e-5
L2_EPS = 1e-6
DN_HEADS = 8
DN_HEAD_DIM = 128
DN_CONV = 4
DN_CHUNK = 64
SG_WIDTH = 2048
SG_GROUPS = 8
SG_CHUNK = 128
SG_GROUP_W = SG_WIDTH // SG_GROUPS
N_CHIPS = 4
N_DEV = 8
LANES = 128
SUBLANES = 8
VMEM_LIMIT = 56 * 1024 * 1024

ADAM_LR = 0.001
ADAM_B1 = 0.9
ADAM_B2 = 0.999
ADAM_EPS = 1e-08
ADAM_WD = 0.01
ADAM_STEP = 10

MESH = pl.DeviceIdType.MESH
ANY = pl.BlockSpec(memory_space=pl.ANY)


def _cp(*sem):
    return pltpu.CompilerParams(dimension_semantics=sem, vmem_limit_bytes=VMEM_LIMIT)


def _pick(n, pref, mult=LANES):
    best = None
    d = mult
    while d <= min(n, pref):
        if n % d == 0:
            best = d
        d += mult
    return best if best is not None else n


def _full(shape):
    nd = len(shape)
    return pl.BlockSpec(shape, lambda *_: (0,) * nd)


def _sigmoid(x):
    return 1.0 / (1.0 + jnp.exp(-x))


def _dot(a, b, dims, prec=None):
    return lax.dot_general(a, b, (dims, ((), ())), preferred_element_type=F32, precision=prec)


NN = ((1,), (0,))
NT = ((1,), (1,))
TN = ((0,), (0,))


def _mx(a):
    return a.astype(MXU_DTYPE)


def _rms_stat(x):
    return lax.rsqrt(jnp.mean(x * x, axis=-1, keepdims=True) + RMS_EPS)


def _rms_bwd(x, r, g, dy):
    xh = x * r
    dxh = dy * g
    dx = r * (dxh - xh * jnp.mean(dxh * xh, axis=-1, keepdims=True))
    return dx, jnp.sum(dy * xh, axis=0, keepdims=True)


def _mm(a, b, mode, name, out_dtype=F32, add=None, after=None):
    if mode == "tn":
        K, M = a.shape
        N = b.shape[1]
    elif mode == "nt":
        M, K = a.shape
        N = b.shape[0]
    else:
        M, K = a.shape
        N = b.shape[1]
    tn = _pick(N, 1024)
    if mode == "tn":
        tm = _pick(M, 1024 if tn <= 512 else 1408)
        tk = _pick(K, 1024, SUBLANES)
    else:
        tm = _pick(M, max(512, min(2048, (512 * 1024) // tn)), SUBLANES)
        tk = _pick(K, 2048)
    nk = K // tk
    grid = (N // tn, M // tm, nk)
    if mode == "nn":
        a_spec = pl.BlockSpec((tm, tk), lambda j, i, k: (i, k))
        b_spec = pl.BlockSpec((tk, tn), lambda j, i, k: (k, j))
        dims = NN
    elif mode == "nt":
        a_spec = pl.BlockSpec((tm, tk), lambda j, i, k: (i, k))
        b_spec = pl.BlockSpec((tn, tk), lambda j, i, k: (j, k))
        dims = NT
    else:
        a_spec = pl.BlockSpec((tk, tm), lambda j, i, k: (k, i))
        b_spec = pl.BlockSpec((tk, tn), lambda j, i, k: (k, j))
        dims = TN
    o_spec = pl.BlockSpec((tm, tn), lambda j, i, k: (i, j))
    has_add = add is not None

    def body(*refs):
        a_ref, b_ref = refs[:2]
        add_ref = refs[2] if has_add else None
        o_ref, acc = refs[-2:]
        k = pl.program_id(2)

        @pl.when(k == 0)
        def _():
            acc[...] = add_ref[...] if has_add else jnp.zeros_like(acc)

        acc[...] += _dot(a_ref[...], b_ref[...], dims)

        @pl.when(k == nk - 1)
        def _():
            o_ref[...] = acc[...].astype(o_ref.dtype)

    ins = [a, b] + ([add] if has_add else []) + ([after] if after is not None else [])
    specs = [a_spec, b_spec] + ([o_spec] if has_add else []) + ([ANY] if after is not None else [])
    return pl.pallas_call(
        body, name=name, grid=grid, in_specs=specs, out_specs=o_spec,
        out_shape=jax.ShapeDtypeStruct((M, N), out_dtype),
        scratch_shapes=[pltpu.VMEM((tm, tn), F32)],
        compiler_params=_cp("parallel", "parallel", "arbitrary"),
    )(*ins)


def _ffn_weight_operands(wt):
    return [_scalar(wt["chip"])] , [wt["gate"][0], wt["gate"][1], wt["up"][0], wt["up"][1], wt["down"][0], wt["down"][1]]


def _load_ffn_weights(chip_ref, shard_refs, wgu_v, wd_v, sem):
    fs = wd_v.shape[0] // N_CHIPS

    @pl.when(pl.program_id(0) == 0)
    def _():
        me = chip_ref[0]
        waits = []
        for t, (dst, base) in enumerate([(wgu_v, 0), (wgu_v, wd_v.shape[0]), (wd_v, 0)]):
            own, gathered = shard_refs[2 * t], shard_refs[2 * t + 1]
            for k in range(N_CHIPS):
                slot = dst.at[pl.ds(base + k * fs, fs), :]
                s = sem.at[t * N_CHIPS + k]

                @pl.when(me == k)
                def _(own=own, slot=slot, s=s):
                    pltpu.make_async_copy(own, slot, s).start()

                @pl.when(me != k)
                def _(gathered=gathered, k=k, slot=slot, s=s):
                    pltpu.make_async_copy(gathered.at[k], slot, s).start()

                waits.append(pltpu.make_async_copy(own, slot, s))
        for cp in waits:
            cp.wait()


def _ffn_fwd(x, g0, g1, wt, name):
    T, D = x.shape
    F = N_CHIPS * wt["down"][0].shape[0]
    F2 = 2 * F
    tm = _pick(T, 256, SUBLANES)
    prefetch, shards = _ffn_weight_operands(wt)

    def body(chip_ref, x_ref, g0_ref, g1_ref, *refs):
        shard_refs = refs[:6]
        xo_ref, h_ref, gu_ref, y_ref, wgu_v, wd_v, sem = refs[6:]
        _load_ffn_weights(chip_ref, shard_refs, wgu_v, wd_v, sem)
        xv = x_ref[...]
        hb = _mx(xv * _rms_stat(xv) * g0_ref[...])
        h_ref[...] = hb
        gu = _dot(hb, wgu_v[...], NT)
        gu_ref[...] = gu.astype(gu_ref.dtype)
        g = gu[:, :F]
        u = gu[:, F:]
        a = _mx(g * _sigmoid(g) * u)
        y = _dot(a, wd_v[...], NN)
        y_ref[...] = y
        xo_ref[...] = xv + 0.5 * (y * _rms_stat(y) * g1_ref[...])

    row = lambda w: pl.BlockSpec((tm, w), lambda i, c: (i, 0))
    one = pl.BlockSpec((1, D), lambda i, c: (0, 0))
    return pl.pallas_call(
        body, name=name,
        grid_spec=pltpu.PrefetchScalarGridSpec(
            num_scalar_prefetch=1, grid=(T // tm,),
            in_specs=[row(D), one, one] + [ANY] * 6,
            out_specs=[row(D), row(D), row(F2), row(D)],
            scratch_shapes=[pltpu.VMEM((F2, D), MXU_DTYPE), pltpu.VMEM((F, D), MXU_DTYPE),
                            pltpu.SemaphoreType.DMA((3 * N_CHIPS,))]),
        out_shape=[jax.ShapeDtypeStruct((T, D), F32), jax.ShapeDtypeStruct((T, D), MXU_DTYPE),
                   jax.ShapeDtypeStruct((T, F2), MXU_DTYPE), jax.ShapeDtypeStruct((T, D), F32)],
        compiler_params=_cp("arbitrary"),
    )(*prefetch, x, g0, g1, *shards)


FFN_BWD_CHUNK = 1408


def _ffn_bwd(dxo, x, y, gu, g0, g1, wt, name):
    T, D = x.shape
    F2 = gu.shape[1]
    F = F2 // 2
    tm = _pick(T, 256, SUBLANES)
    fc = _pick(F, FFN_BWD_CHUNK)
    prefetch, shards = _ffn_weight_operands(wt)

    def body(chip_ref, dxo_ref, x_ref, y_ref, gu_ref, g0_ref, g1_ref, *refs):
        shard_refs = refs[:6]
        dx_ref, dy_ref, a_ref, dgu_ref, dg0_ref, dg1_ref, wgu_v, wd_v, sem = refs[6:]
        _load_ffn_weights(chip_ref, shard_refs, wgu_v, wd_v, sem)

        @pl.when(pl.program_id(0) == 0)
        def _():
            dg0_ref[...] = jnp.zeros_like(dg0_ref)
            dg1_ref[...] = jnp.zeros_like(dg1_ref)

        dxo_v = dxo_ref[...]
        yv = y_ref[...]
        dy, dg1 = _rms_bwd(yv, _rms_stat(yv), g1_ref[...], 0.5 * dxo_v)
        dg1_ref[...] += dg1
        dyb = _mx(dy)
        dy_ref[...] = dyb
        dh = jnp.zeros((tm, D), F32)
        for c in range(F // fc):
            lo, hi = c * fc, (c + 1) * fc
            da = _dot(dyb, wd_v[lo:hi, :], NT)
            g = gu_ref[:, lo:hi].astype(F32)
            u = gu_ref[:, F + lo:F + hi].astype(F32)
            s = _sigmoid(g)
            sg = g * s
            a_ref[:, lo:hi] = _mx(sg * u)
            dg = _mx(da * u * (s * (1.0 + g * (1.0 - s))))
            du = _mx(da * sg)
            dgu_ref[:, lo:hi] = dg
            dgu_ref[:, F + lo:F + hi] = du
            dh = dh + _dot(dg, wgu_v[lo:hi, :], NN) + _dot(du, wgu_v[F + lo:F + hi, :], NN)
        xv = x_ref[...]
        dx, dg0 = _rms_bwd(xv, _rms_stat(xv), g0_ref[...], dh)
        dg0_ref[...] += dg0
        dx_ref[...] = dxo_v + dx

    row = lambda w: pl.BlockSpec((tm, w), lambda i, c: (i, 0))
    one = pl.BlockSpec((1, D), lambda i, c: (0, 0))
    return pl.pallas_call(
        body, name=name,
        grid_spec=pltpu.PrefetchScalarGridSpec(
            num_scalar_prefetch=1, grid=(T // tm,),
            in_specs=[row(D), row(D), row(D), row(F2), one, one] + [ANY] * 6,
            out_specs=[row(D), row(D), row(F), row(F2), one, one],
            scratch_shapes=[pltpu.VMEM((F2, D), MXU_DTYPE), pltpu.VMEM((F, D), MXU_DTYPE),
                            pltpu.SemaphoreType.DMA((3 * N_CHIPS,))]),
        out_shape=[jax.ShapeDtypeStruct((T, D), F32), jax.ShapeDtypeStruct((T, D), MXU_DTYPE),
                   jax.ShapeDtypeStruct((T, F), MXU_DTYPE), jax.ShapeDtypeStruct((T, F2), MXU_DTYPE),
                   jax.ShapeDtypeStruct((1, D), F32), jax.ShapeDtypeStruct((1, D), F32)],
        compiler_params=_cp("arbitrary"),
    )(*prefetch, dxo, x, y, gu, g0, g1, *shards)


def _norm_fwd(x, g, name):
    T, D = x.shape
    tm = _pick(T, 512, SUBLANES)

    def body(x_ref, g_ref, h_ref):
        xv = x_ref[...]
        h_ref[...] = _mx(xv * _rms_stat(xv) * g_ref[...])

    row = pl.BlockSpec((tm, D), lambda i: (i, 0))
    return pl.pallas_call(body, name=name, grid=(T // tm,), in_specs=[row, _full((1, D))], out_specs=row,
                          out_shape=jax.ShapeDtypeStruct((T, D), MXU_DTYPE), compiler_params=_cp("parallel"))(x, g)


def _postnorm_fwd(x, m, g, name):
    T, D = x.shape
    tm = _pick(T, 512, SUBLANES)

    def body(x_ref, m_ref, g_ref, o_ref):
        mv = m_ref[...]
        o_ref[...] = x_ref[...] + mv * _rms_stat(mv) * g_ref[...]

    row = pl.BlockSpec((tm, D), lambda i: (i, 0))
    return pl.pallas_call(body, name=name, grid=(T // tm,), in_specs=[row, row, _full((1, D))], out_specs=row,
                          out_shape=jax.ShapeDtypeStruct((T, D), F32), compiler_params=_cp("parallel"))(x, m, g)


def _postnorm_bwd(dxo, m, g, name):
    T, D = m.shape
    tm = _pick(T, 512, SUBLANES)

    def body(dxo_ref, m_ref, g_ref, dm_ref, dg_ref):
        @pl.when(pl.program_id(0) == 0)
        def _():
            dg_ref[...] = jnp.zeros_like(dg_ref)

        mv = m_ref[...]
        dm, dg = _rms_bwd(mv, _rms_stat(mv), g_ref[...], dxo_ref[...])
        dg_ref[...] += dg
        dm_ref[...] = _mx(dm)

    row = pl.BlockSpec((tm, D), lambda i: (i, 0))
    return pl.pallas_call(body, name=name, grid=(T // tm,), in_specs=[row, row, _full((1, D))],
                          out_specs=[row, _full((1, D))],
                          out_shape=[jax.ShapeDtypeStruct((T, D), MXU_DTYPE), jax.ShapeDtypeStruct((1, D), F32)],
                          compiler_params=_cp("arbitrary"))(dxo, m, g)


def _prenorm_bwd(dxo, dh, x, g, name):
    T, D = x.shape
    tm = _pick(T, 512, SUBLANES)

    def body(dxo_ref, dh_ref, x_ref, g_ref, dx_ref, dg_ref):
        @pl.when(pl.program_id(0) == 0)
        def _():
            dg_ref[...] = jnp.zeros_like(dg_ref)

        xv = x_ref[...]
        dx, dg = _rms_bwd(xv, _rms_stat(xv), g_ref[...], dh_ref[...])
        dg_ref[...] += dg
        dx_ref[...] = dxo_ref[...] + dx

    row = pl.BlockSpec((tm, D), lambda i: (i, 0))
    return pl.pallas_call(body, name=name, grid=(T // tm,), in_specs=[row, row, row, _full((1, D))],
                          out_specs=[row, _full((1, D))],
                          out_shape=[jax.ShapeDtypeStruct((T, D), F32), jax.ShapeDtypeStruct((1, D), F32)],
                          compiler_params=_cp("arbitrary"))(dxo, dh, x, g)


def _loss_fwd_bwd(y, target, name):
    T, D = y.shape
    tm = _pick(T, 512, SUBLANES)

    def body(y_ref, t_ref, l_ref, dy_ref):
        @pl.when(pl.program_id(0) == 0)
        def _():
            l_ref[...] = jnp.zeros_like(l_ref)

        e = y_ref[...] - t_ref[...]
        dy_ref[...] = e * (1.0 / D)
        l_ref[...] += 0.5 * jnp.sum(jnp.mean(e * e, axis=-1, keepdims=True), axis=0, keepdims=True)

    row = pl.BlockSpec((tm, D), lambda i: (i, 0))
    return pl.pallas_call(body, name=name, grid=(T // tm,), in_specs=[row, row],
                          out_specs=[_full((SUBLANES, LANES)), row],
                          out_shape=[jax.ShapeDtypeStruct((SUBLANES, LANES), F32), jax.ShapeDtypeStruct((T, D), F32)],
                          compiler_params=_cp("arbitrary"))(y, target)


DN_ROWS = 512


def _shift_down(prev8, cur, s):
    n = cur.shape[0]
    xx = jnp.concatenate([prev8, cur], axis=0)
    return pltpu.roll(xx, s, 0)[SUBLANES:SUBLANES + n, :]


def _shift_up(cur, next8, s):
    n = cur.shape[0]
    xx = jnp.concatenate([cur, next8], axis=0)
    return pltpu.roll(xx, n + SUBLANES - s, 0)[:n, :]


def _conv_tile(x_ref, w, r, rows):
    start = pl.multiple_of(r * rows, SUBLANES)
    cur = x_ref[pl.ds(start, rows), :]
    pstart = pl.multiple_of(jnp.maximum(start - SUBLANES, 0), SUBLANES)
    prev8 = jnp.where(r == 0, 0.0, x_ref[pl.ds(pstart, SUBLANES), :])
    taps = [_shift_down(prev8, cur, DN_CONV - 1 - j) if j < DN_CONV - 1 else cur for j in range(DN_CONV)]
    c = taps[0] * w[0:1, :]
    for j in range(1, DN_CONV):
        c = c + taps[j] * w[j:j + 1, :]
    return c, taps


def _dn_prep_fwd(proj, conv_w, name):
    T = proj.shape[0]
    W = DN_HEADS * DN_HEAD_DIM
    rows = min(DN_ROWS, T)
    n_inner = T // rows
    scale = DN_HEAD_DIM ** -0.5

    def body(x_ref, w_ref, o_ref):
        cb = pl.program_id(0)
        w = w_ref[...]
        is_qk = cb < 2 * DN_HEADS
        post = jnp.where(cb < DN_HEADS, scale, 1.0)

        def step(r, carry):
            c, _ = _conv_tile(x_ref, w, r, rows)
            s = c * _sigmoid(c)
            rinv = lax.rsqrt(jnp.sum(s * s, axis=-1, keepdims=True) + L2_EPS)
            o_ref[pl.ds(pl.multiple_of(r * rows, SUBLANES), rows), :] = jnp.where(is_qk, s * rinv * post, s)
            return carry

        lax.fori_loop(0, n_inner, step, 0)

    col = pl.BlockSpec((T, LANES), lambda j: (0, j))
    return pl.pallas_call(body, name=name, grid=(3 * W // LANES,),
                          in_specs=[col, pl.BlockSpec((DN_CONV, LANES), lambda j: (0, j))], out_specs=col,
                          out_shape=jax.ShapeDtypeStruct((T, 3 * W), F32), compiler_params=_cp("parallel"))(proj, conv_w)


def _dn_prep_bwd(proj, conv_w, dqkv, name):
    T = proj.shape[0]
    W = DN_HEADS * DN_HEAD_DIM
    rows = min(DN_ROWS, T)
    n_inner = T // rows
    scale = DN_HEAD_DIM ** -0.5

    def body(x_ref, w_ref, dy_ref, dx_ref, dw_ref, dc_scr):
        cb = pl.program_id(0)
        w = w_ref[...]
        is_qk = cb < 2 * DN_HEADS
        post = jnp.where(cb < DN_HEADS, scale, 1.0)

        def step1(r, dws):
            c, taps = _conv_tile(x_ref, w, r, rows)
            sg = _sigmoid(c)
            s = c * sg
            rinv = lax.rsqrt(jnp.sum(s * s, axis=-1, keepdims=True) + L2_EPS)
            dy = dy_ref[pl.ds(pl.multiple_of(r * rows, SUBLANES), rows), :]
            yn = s * rinv
            dyn = dy * post
            ds_qk = rinv * (dyn - yn * jnp.sum(dyn * yn, axis=-1, keepdims=True))
            ds = jnp.where(is_qk, ds_qk, dy)
            dc = ds * (sg * (1.0 + c * (1.0 - sg)))
            dc_scr[pl.ds(pl.multiple_of(r * rows, SUBLANES), rows), :] = dc
            return tuple(dws[j] + jnp.sum(dc * taps[j], axis=0, keepdims=True) for j in range(DN_CONV))

        zero = jnp.zeros((1, LANES), F32)
        dws = lax.fori_loop(0, n_inner, step1, (zero,) * DN_CONV)
        for j in range(DN_CONV):
            dw_ref[j:j + 1, :] = dws[j]

        def step2(r, carry):
            start = pl.multiple_of(r * rows, SUBLANES)
            cur = dc_scr[pl.ds(start, rows), :]
            nstart = pl.multiple_of(jnp.minimum(start + rows, T - SUBLANES), SUBLANES)
            next8 = jnp.where(r == n_inner - 1, 0.0, dc_scr[pl.ds(nstart, SUBLANES), :])
            dx = cur * w[DN_CONV - 1:DN_CONV, :]
            for j in range(DN_CONV - 1):
                dx = dx + _shift_up(cur, next8, DN_CONV - 1 - j) * w[j:j + 1, :]
            dx_ref[pl.ds(start, rows), :] = _mx(dx)
            return carry

        lax.fori_loop(0, n_inner, step2, 0)

    col = pl.BlockSpec((T, LANES), lambda j: (0, j))
    wspec = pl.BlockSpec((DN_CONV, LANES), lambda j: (0, j))
    return pl.pallas_call(body, name=name, grid=(3 * W // LANES,), in_specs=[col, wspec, col], out_specs=[col, wspec],
                          out_shape=[jax.ShapeDtypeStruct((T, 3 * W), MXU_DTYPE), jax.ShapeDtypeStruct((DN_CONV, 3 * W), F32)],
                          scratch_shapes=[pltpu.VMEM((T, LANES), F32)], compiler_params=_cp("parallel"))(proj, conv_w, dqkv)


def _softplus(x):
    return jnp.maximum(x, 0.0) + jnp.log(1.0 + jnp.exp(-jnp.abs(x)))


def _dn_gate_fwd(ba, a_log, dt_bias, name):
    T = ba.shape[0]
    tm = _pick(T, 1024, SUBLANES)

    def body(ba_ref, al_ref, dt_ref, beta_ref, g_ref):
        beta_ref[...] = _sigmoid(ba_ref[:, :LANES])
        g_ref[...] = -jnp.exp(al_ref[...]) * _softplus(ba_ref[:, LANES:] + dt_ref[...])

    row = lambda w: pl.BlockSpec((tm, w), lambda i: (i, 0))
    return pl.pallas_call(body, name=name, grid=(T // tm,), in_specs=[row(2 * LANES), _full((1, LANES)), _full((1, LANES))],
                          out_specs=[row(LANES), row(LANES)],
                          out_shape=[jax.ShapeDtypeStruct((T, LANES), F32)] * 2, compiler_params=_cp("parallel"))(ba, a_log, dt_bias)


def _dn_gate_bwd(ba, a_log, dt_bias, dbeta, dg, name):
    T = ba.shape[0]
    tm = _pick(T, 1024, SUBLANES)

    def body(ba_ref, al_ref, dt_ref, dbeta_ref, dg_ref, dba_ref, dal_ref, ddt_ref):
        @pl.when(pl.program_id(0) == 0)
        def _():
            dal_ref[...] = jnp.zeros_like(dal_ref)
            ddt_ref[...] = jnp.zeros_like(ddt_ref)

        beta = _sigmoid(ba_ref[:, :LANES])
        dba_ref[:, :LANES] = _mx(dbeta_ref[...] * beta * (1.0 - beta))
        pre = ba_ref[:, LANES:] + dt_ref[...]
        ea = jnp.exp(al_ref[...])
        dgv = dg_ref[...]
        da = dgv * (-ea) * _sigmoid(pre)
        dba_ref[:, LANES:] = _mx(da)
        ddt_ref[...] += jnp.sum(da, axis=0, keepdims=True)
        dal_ref[...] += jnp.sum(dgv * (-ea) * _softplus(pre), axis=0, keepdims=True)

    row = lambda w: pl.BlockSpec((tm, w), lambda i: (i, 0))
    one = _full((1, LANES))
    return pl.pallas_call(body, name=name, grid=(T // tm,), in_specs=[row(2 * LANES), one, one, row(LANES), row(LANES)],
                          out_specs=[row(2 * LANES), one, one],
                          out_shape=[jax.ShapeDtypeStruct((T, 2 * LANES), MXU_DTYPE), jax.ShapeDtypeStruct((1, LANES), F32),
                                     jax.ShapeDtypeStruct((1, LANES), F32)],
                          compiler_params=_cp("arbitrary"))(ba, a_log, dt_bias, dbeta, dg)


def _tri(c, strict):
    i = lax.broadcasted_iota(jnp.int32, (c, c), 0)
    j = lax.broadcasted_iota(jnp.int32, (c, c), 1)
    return (i > j) if strict else (i >= j)


def _inv_unit_lower(ls):
    c = ls[0].shape[0]
    i = lax.broadcasted_iota(jnp.int32, (c, c), 0)
    j = lax.broadcasted_iota(jnp.int32, (c, c), 1)
    eye = jnp.where(i == j, 1.0, 0.0)
    facs = [[eye - l for l in ls]]
    cur = ls
    for _ in range(int(math.log2(c)) - 1):
        cur = [_dot(p, p, NN, TRI_PREC) for p in cur]
        facs.append([eye + p for p in cur])
    while len(facs) > 1:
        nxt = [[_dot(a, b, NN, TRI_PREC) for a, b in zip(facs[t], facs[t + 1])] for t in range(0, len(facs) - 1, 2)]
        if len(facs) % 2:
            nxt.append(facs[-1])
        facs = nxt
    return facs[0]


def _chunk_gates(g_blk):
    c = g_blk.shape[0]
    gcs = _dot(jnp.where(_tri(c, False), 1.0, 0.0), g_blk, NN, HI)
    return gcs, gcs.T


def _head_chunk(h, qh, kh, vh, beta_blk, gcs, gcs_t):
    c = qh.shape[0]
    incl = _tri(c, False)
    gc_col = gcs[:, h:h + 1]
    gc_row = gcs_t[h:h + 1, :]
    gc_last = gcs_t[h:h + 1, c - 1:c]
    dec = jnp.where(incl, jnp.exp(jnp.where(incl, gc_col - gc_row, 0.0)), 0.0)
    gam = jnp.exp(gc_col)
    rr = jnp.exp(gc_last - gc_col)
    gl = jnp.exp(gc_last)
    b = beta_blk[:, h:h + 1]
    kb = kh * b
    vb = vh * b
    kk = _dot(_mx(kb), _mx(kh), NT)
    lmat = jnp.where(_tri(c, True), kk * dec, 0.0)
    qk = _dot(_mx(qh), _mx(kh), NT)
    pmat = jnp.where(incl, qk * dec, 0.0)
    return dict(dec=dec, gam=gam, rr=rr, gl=gl, b=b, kb=kb, vb=vb, lmat=lmat, pmat=pmat)


def _dn_scan_fwd(qkv, beta, g, proj, norm_g, name):
    T = qkv.shape[0]
    C, H, Dh = DN_CHUNK, DN_HEADS, DN_HEAD_DIM
    W = H * Dh
    N = T // C

    def body(q_ref, k_ref, v_ref, beta_ref, g_ref, z_ref, ng_ref, og_ref, o_ref, tinv_ref, s_ref, state):
        @pl.when(pl.program_id(0) == 0)
        def _():
            state[...] = jnp.zeros_like(state)

        gcs, gcs_t = _chunk_gates(g_ref[...])
        beta_blk = beta_ref[...]
        ng = ng_ref[...]
        heads = range(H)
        cs = [slice(h * Dh, (h + 1) * Dh) for h in heads]
        qs = [_head_chunk(h, q_ref[:, cs[h]], k_ref[:, cs[h]], v_ref[:, cs[h]], beta_blk, gcs, gcs_t) for h in heads]
        tinvs = _inv_unit_lower([q["lmat"] for q in qs])
        for h in heads:
            tinv_ref[h] = tinvs[h]
        us = [_dot(tinvs[h], qs[h]["vb"], NN, TRI_PREC) for h in heads]
        ws = [_dot(tinvs[h], qs[h]["kb"] * qs[h]["gam"], NN, TRI_PREC) for h in heads]
        ss = [state[h] for h in heads]
        for h in heads:
            s_ref[0, h] = ss[h]
        sbs = [_mx(s) for s in ss]
        vnbs = [_mx(us[h] - _dot(_mx(ws[h]), sbs[h], NN)) for h in heads]
        os_ = [_dot(_mx(q_ref[:, cs[h]] * qs[h]["gam"]), sbs[h], NN) + _dot(_mx(qs[h]["pmat"]), vnbs[h], NN) for h in heads]
        for h in heads:
            state[h] = ss[h] * qs[h]["gl"] + _dot(_mx((k_ref[:, cs[h]] * qs[h]["rr"]).T), vnbs[h], NN)
        for h in heads:
            o = os_[h]
            o_ref[:, cs[h]] = o
            zh = z_ref[:, cs[h]]
            og_ref[:, cs[h]] = _mx(o * _rms_stat(o) * ng * (zh * _sigmoid(zh)))

    blk = lambda j: pl.BlockSpec((C, W), lambda n: (n, j))
    small = pl.BlockSpec((C, LANES), lambda n: (n, 0))
    return pl.pallas_call(
        body, name=name, grid=(N,),
        in_specs=[blk(0), blk(1), blk(2), small, small, blk(3), _full((1, Dh))],
        out_specs=[blk(0), blk(0), pl.BlockSpec((H, C, C), lambda n: (0, n, 0)),
                   pl.BlockSpec((1, H, Dh, Dh), lambda n: (n, 0, 0, 0))],
        out_shape=[jax.ShapeDtypeStruct((T, W), MXU_DTYPE), jax.ShapeDtypeStruct((T, W), F32),
                   jax.ShapeDtypeStruct((H, T, C), F32), jax.ShapeDtypeStruct((N, H, Dh, Dh), F32)],
        scratch_shapes=[pltpu.VMEM((H, Dh, Dh), F32)],
        compiler_params=_cp("arbitrary"),
    )(qkv, qkv, qkv, beta, g, proj, norm_g)


def _dn_scan_bwd(qkv, beta, g, proj, norm_g, o, tinv, s_all, dog, name):
    T = qkv.shape[0]
    C, H, Dh = DN_CHUNK, DN_HEADS, DN_HEAD_DIM
    W = H * Dh
    N = T // C

    def body(q_ref, k_ref, v_ref, beta_ref, g_ref, z_ref, ng_ref, o_ref, tinv_ref, s_ref, dog_ref,
             dqkv_ref, dbeta_ref, dg_ref, dz_ref, dng_ref, dstate):
        @pl.when(pl.program_id(0) == 0)
        def _():
            dstate[...] = jnp.zeros_like(dstate)
            dng_ref[...] = jnp.zeros_like(dng_ref)

        gcs, gcs_t = _chunk_gates(g_ref[...])
        beta_blk = beta_ref[...]
        ng = ng_ref[...]
        incl = _tri(C, False)
        strict = _tri(C, True)
        lane = lax.broadcasted_iota(jnp.int32, (C, LANES), 1)
        rowi = lax.broadcasted_iota(jnp.int32, (C, 1), 0)
        ones = jnp.ones((C, LANES), F32)
        dbeta_acc = jnp.zeros((C, LANES), F32)
        dgc_acc = jnp.zeros((C, LANES), F32)
        dng_acc = jnp.zeros((1, Dh), F32)
        cs = [slice(h * Dh, (h + 1) * Dh) for h in range(H)]
        rsum = lambda t: jnp.sum(t, axis=1, keepdims=True)
        for heads in (range(0, H // 2), range(H // 2, H)):
            dobs = {}
            for h in heads:
                oh, zh, dogh = o_ref[:, cs[h]], z_ref[:, cs[h]], dog_ref[:, cs[h]]
                rstat = _rms_stat(oh)
                sz = _sigmoid(zh)
                dz_ref[:, cs[h]] = _mx(dogh * (oh * rstat * ng) * (sz * (1.0 + zh * (1.0 - sz))))
                do, dng = _rms_bwd(oh, rstat, ng, dogh * (zh * sz))
                dng_acc = dng_acc + dng
                dobs[h] = _mx(do)
            qs = {h: _head_chunk(h, q_ref[:, cs[h]], k_ref[:, cs[h]], v_ref[:, cs[h]], beta_blk, gcs, gcs_t) for h in heads}
            tms = {h: tinv_ref[h] for h in heads}
            us = {h: _dot(tms[h], qs[h]["vb"], NN, TRI_PREC) for h in heads}
            ws = {h: _dot(tms[h], qs[h]["kb"] * qs[h]["gam"], NN, TRI_PREC) for h in heads}
            ss = {h: s_ref[0, h] for h in heads}
            sbs = {h: _mx(ss[h]) for h in heads}
            wbs = {h: _mx(ws[h]) for h in heads}
            vnbs = {h: _mx(us[h] - _dot(wbs[h], sbs[h], NN)) for h in heads}
            dsns = {h: dstate[h] for h in heads}
            dsbs = {h: _mx(dsns[h]) for h in heads}
            dvnews = {h: _dot(_mx(qs[h]["pmat"]), dobs[h], TN) + _dot(_mx(k_ref[:, cs[h]] * qs[h]["rr"]), dsbs[h], NN)
                      for h in heads}
            dvb16s = {h: _mx(dvnews[h]) for h in heads}
            dps = {h: jnp.where(incl, _dot(dobs[h], vnbs[h], NT), 0.0) for h in heads}
            dqds = {h: _dot(dobs[h], sbs[h], NT) for h in heads}
            dkds = {h: _dot(vnbs[h], dsbs[h], NT) for h in heads}
            dgls = {h: jnp.sum(rsum(ss[h] * dsns[h]), axis=0, keepdims=True) for h in heads}
            dws = {h: -_dot(dvb16s[h], sbs[h], NT) for h in heads}
            for h in heads:
                dstate[h] = (_dot(_mx(q_ref[:, cs[h]] * qs[h]["gam"]), dobs[h], TN) + qs[h]["gl"] * dsns[h]
                             - _dot(wbs[h], dvb16s[h], TN))
            dvbs = {h: _dot(tms[h], dvnews[h], TN, TRI_PREC) for h in heads}
            dkbgs = {h: _dot(tms[h], dws[h], TN, TRI_PREC) for h in heads}
            dls = {h: jnp.where(strict, -(_dot(dvbs[h], us[h], NT, TRI_PREC) + _dot(dkbgs[h], ws[h], NT, TRI_PREC)), 0.0)
                   for h in heads}
            mmats = {h: dls[h] * qs[h]["lmat"] + dps[h] * qs[h]["pmat"] for h in heads}
            dgcs = {h: rsum(mmats[h]) - _dot(mmats[h], ones, TN, HI)[:, :1] for h in heads}
            dkk16s = {h: _mx(dls[h] * qs[h]["dec"]) for h in heads}
            dqk16s = {h: _mx(dps[h] * qs[h]["dec"]) for h in heads}
            for h in heads:
                q = qs[h]
                qh, kh, vh = q_ref[:, cs[h]], k_ref[:, cs[h]], v_ref[:, cs[h]]
                gam, rr, b, kb = q["gam"], q["rr"], q["b"], q["kb"]
                dkb = _dot(dkk16s[h], _mx(kh), NN) + dkbgs[h] * gam
                dk = _dot(dkk16s[h], _mx(kb), TN) + _dot(dqk16s[h], _mx(qh), TN) + dkb * b + dkds[h] * rr
                dq = _dot(dqk16s[h], _mx(kh), NN) + dqds[h] * gam
                dgam = rsum(dkbgs[h] * kb) + rsum(dqds[h] * qh)
                dr = rsum(dkds[h] * kh)
                dgc_last = jnp.sum(dr * rr, axis=0, keepdims=True) + dgls[h] * q["gl"]
                dgc = dgcs[h] + dgam * gam - dr * rr + jnp.where(rowi == C - 1, dgc_last, 0.0)
                dbeta = rsum(dvbs[h] * vh) + rsum(dkb * kh)
                dqkv_ref[:, cs[h]] = dq
                dqkv_ref[:, W + h * Dh:W + (h + 1) * Dh] = dk
                dqkv_ref[:, 2 * W + h * Dh:2 * W + (h + 1) * Dh] = dvbs[h] * b
                dbeta_acc = jnp.where(lane == h, dbeta, dbeta_acc)
                dgc_acc = jnp.where(lane == h, dgc, dgc_acc)
        dbeta_ref[...] = dbeta_acc
        dg_ref[...] = _dot(jnp.where(incl, 1.0, 0.0), dgc_acc, TN, HI)
        dng_ref[...] += dng_acc

    rev = lambda n: N - 1 - n
    blk = lambda j: pl.BlockSpec((C, W), lambda n: (rev(n), j))
    small = pl.BlockSpec((C, LANES), lambda n: (rev(n), 0))
    return pl.pallas_call(
        body, name=name, grid=(N,),
        in_specs=[blk(0), blk(1), blk(2), small, small, blk(3), _full((1, Dh)), blk(0),
                  pl.BlockSpec((H, C, C), lambda n: (0, rev(n), 0)),
                  pl.BlockSpec((1, H, Dh, Dh), lambda n: (rev(n), 0, 0, 0)), blk(0)],
        out_specs=[pl.BlockSpec((C, 3 * W), lambda n: (rev(n), 0)), small, small, blk(0), _full((1, Dh))],
        out_shape=[jax.ShapeDtypeStruct((T, 3 * W), F32), jax.ShapeDtypeStruct((T, LANES), F32),
                   jax.ShapeDtypeStruct((T, LANES), F32), jax.ShapeDtypeStruct((T, W), MXU_DTYPE),
                   jax.ShapeDtypeStruct((1, Dh), F32)],
        scratch_shapes=[pltpu.VMEM((H, Dh, Dh), F32)],
        compiler_params=_cp("arbitrary"),
    )(qkv, qkv, qkv, beta, g, proj, norm_g, o, tinv, s_all, dog)


_INV_SQRT2 = 0.7071067811865476
_INV_SQRT_2PI = 0.3989422804014327


def _sg_recompute(zp_ref, bin_ref, lng_ref, lnb_ref):
    E = SG_WIDTH
    zin = zp_ref[...] + bin_ref[...]
    cdf = 0.5 * (1.0 + lax.erf(zin * _INV_SQRT2))
    zz = zin * cdf
    u = zz[:, :E]
    vp = zz[:, E:]
    mu = jnp.mean(vp, axis=-1, keepdims=True)
    xc = vp - mu
    rstd = lax.rsqrt(jnp.mean(xc * xc, axis=-1, keepdims=True) + LN_EPS)
    xhat = xc * rstd
    v = xhat * lng_ref[...] + lnb_ref[...]
    return zin, cdf, u, xhat, rstd, v


def _sg_masked_ws(ws_ref, g):
    return _mx(jnp.where(_tri(SG_CHUNK, False), ws_ref[g], 0.0))


def _sg_fwd(zpre, b_in, ln_g, ln_b, w_s, b_s_t, name):
    T = zpre.shape[0]
    E, G, C, GW = SG_WIDTH, SG_GROUPS, SG_CHUNK, SG_GROUP_W

    def body(zp_ref, bin_ref, lng_ref, lnb_ref, ws_ref, bst_ref, um_ref):
        _, _, u, _, _, v = _sg_recompute(zp_ref, bin_ref, lng_ref, lnb_ref)
        bst = bst_ref[...]
        for g in range(G):
            cs = slice(g * GW, (g + 1) * GW)
            mixed = _dot(_sg_masked_ws(ws_ref, g), _mx(v[:, cs]), NN) + bst[:, g:g + 1]
            um_ref[:, cs] = _mx(u[:, cs] * mixed)

    return pl.pallas_call(
        body, name=name, grid=(T // C,),
        in_specs=[pl.BlockSpec((C, 2 * E), lambda n: (n, 0)), _full((1, 2 * E)), _full((1, E)), _full((1, E)),
                  _full((G, C, C)), _full((C, LANES))],
        out_specs=pl.BlockSpec((C, E), lambda n: (n, 0)),
        out_shape=jax.ShapeDtypeStruct((T, E), MXU_DTYPE), compiler_params=_cp("parallel"),
    )(zpre, b_in, ln_g, ln_b, w_s, b_s_t)


def _sg_bwd(zpre, b_in, ln_g, ln_b, w_s, b_s_t, dum, name):
    T = zpre.shape[0]
    E, G, C, GW = SG_WIDTH, SG_GROUPS, SG_CHUNK, SG_GROUP_W

    def body(zp_ref, bin_ref, lng_ref, lnb_ref, ws_ref, bst_ref, dum_ref,
             dz_ref, dbin_ref, dlng_ref, dlnb_ref, dws_ref, dbst_ref):
        @pl.when(pl.program_id(0) == 0)
        def _():
            for r in (dbin_ref, dlng_ref, dlnb_ref, dws_ref, dbst_ref):
                r[...] = jnp.zeros_like(r)

        zin, cdf, u, xhat, rstd, v = _sg_recompute(zp_ref, bin_ref, lng_ref, lnb_ref)
        bst = bst_ref[...]
        lane = lax.broadcasted_iota(jnp.int32, (C, LANES), 1)
        dum_v = dum_ref[...]
        dbst = jnp.zeros((C, LANES), F32)
        du_parts, dv_parts = [], []
        for g in range(G):
            cs = slice(g * GW, (g + 1) * GW)
            wsm = _sg_masked_ws(ws_ref, g)
            vg = _mx(v[:, cs])
            mixed = _dot(wsm, vg, NN) + bst[:, g:g + 1]
            dumg = dum_v[:, cs]
            du_parts.append(dumg * mixed)
            dmixed = dumg * u[:, cs]
            dmb = _mx(dmixed)
            dv_parts.append(_dot(wsm, dmb, TN))
            dws_ref[g] += _dot(dmb, vg, NT)
            dbst = jnp.where(lane == g, jnp.sum(dmixed, axis=1, keepdims=True), dbst)
        dbst_ref[...] += dbst
        du = jnp.concatenate(du_parts, axis=1)
        dv = jnp.concatenate(dv_parts, axis=1)
        dlng_ref[...] += jnp.sum(dv * xhat, axis=0, keepdims=True)
        dlnb_ref[...] += jnp.sum(dv, axis=0, keepdims=True)
        dxh = dv * lng_ref[...]
        dvp = rstd * (dxh - jnp.mean(dxh, axis=-1, keepdims=True) - xhat * jnp.mean(dxh * xhat, axis=-1, keepdims=True))
        dzz = jnp.concatenate([du, dvp], axis=1)
        dzin = dzz * (cdf + zin * (_INV_SQRT_2PI * jnp.exp(-0.5 * zin * zin)))
        dz_ref[...] = _mx(dzin)
        dbin_ref[...] += jnp.sum(dzin, axis=0, keepdims=True)

    return pl.pallas_call(
        body, name=name, grid=(T // C,),
        in_specs=[pl.BlockSpec((C, 2 * E), lambda n: (n, 0)), _full((1, 2 * E)), _full((1, E)), _full((1, E)),
                  _full((G, C, C)), _full((C, LANES)), pl.BlockSpec((C, E), lambda n: (n, 0))],
        out_specs=[pl.BlockSpec((C, 2 * E), lambda n: (n, 0)), _full((1, 2 * E)), _full((1, E)), _full((1, E)),
                   _full((G, C, C)), _full((C, LANES))],
        out_shape=[jax.ShapeDtypeStruct((T, 2 * E), MXU_DTYPE), jax.ShapeDtypeStruct((1, 2 * E), F32),
                   jax.ShapeDtypeStruct((1, E), F32), jax.ShapeDtypeStruct((1, E), F32),
                   jax.ShapeDtypeStruct((G, C, C), F32), jax.ShapeDtypeStruct((C, LANES), F32)],
        compiler_params=_cp("arbitrary"),
    )(zpre, b_in, ln_g, ln_b, w_s, b_s_t, dum)


def _row(v):
    return v.reshape(1, -1)


def _pad_lanes(v):
    v = v.reshape(1, -1)
    return jnp.pad(v, ((0, 0), (0, LANES - v.shape[1])))


def _local_step(x, target, p, weights_for, grads_ready=None, small_ready=None):
    ng = p["norm_g"]
    grads = {}
    dng = [[None] * 6 for _ in range(2)]
    order = [jnp.zeros((), F32)]

    def tell(group):
        zero = grads_ready(group, grads) if grads_ready is not None else None
        if zero is not None:
            order[0] = zero

    def gain(i, s):
        return _row(ng[i, s]) + order[0]

    def ffn_f(xin, i, j, tag):
        wt = weights_for("ffn" + tag, xin)
        xo, h, gu, y = _ffn_fwd(xin, _row(ng[i, 4 * j]), _row(ng[i, 4 * j + 1]), wt, "ffn_fwd_" + tag)
        return xo, (xin, h, gu, y, wt)

    x1, sv_f00 = ffn_f(x, 0, 0, "00")
    dnw = weights_for("dn", x1)
    hn0 = _norm_fwd(x1, _row(ng[0, 2]), "dn_prenorm")
    proj = _mm(hn0, dnw["dn_wqkvz"], "nn", "dn_proj")
    ba = _mm(hn0, dnw["dn_wba"], "nn", "dn_proj_ba")
    a_log = _pad_lanes(p["dn_a_log"])
    dt_bias = _pad_lanes(p["dn_dt_bias"])
    dn_ng = _row(p["dn_norm_g"])
    qkv = _dn_prep_fwd(proj, p["dn_conv_w"], "dn_prep_fwd")
    beta, gdec = _dn_gate_fwd(ba, a_log, dt_bias, "dn_gate_fwd")
    og, o_raw, tinv, s_all = _dn_scan_fwd(qkv, beta, gdec, proj, dn_ng, "dn_scan_fwd")
    m0 = _mm(og, dnw["dn_wout"], "nn", "dn_out")
    x2 = _postnorm_fwd(x1, m0, _row(ng[0, 3]), "dn_postnorm")
    x3, sv_f01 = ffn_f(x2, 0, 1, "01")
    x4, sv_f10 = ffn_f(x3, 1, 0, "10")
    sgw = weights_for("sg", x4)
    hn1 = _norm_fwd(x4, _row(ng[1, 2]), "sg_prenorm")
    zpre = _mm(hn1, sgw["sg_win"], "nn", "sg_proj")
    sg_bin = _row(p["sg_b_in"])
    sg_lng = _row(p["sg_ln_g"])
    sg_lnb = _row(p["sg_ln_b"])
    sg_bst = jnp.pad(p["sg_b_s"].T, ((0, 0), (0, LANES - SG_GROUPS)))
    um = _sg_fwd(zpre, sg_bin, sg_lng, sg_lnb, p["sg_w_s"], sg_bst, "sg_fwd")
    m1 = _mm(um, sgw["sg_wout"], "nn", "sg_out")
    x5 = _postnorm_fwd(x4, m1, _row(ng[1, 3]), "sg_postnorm")
    x6, sv_f11 = ffn_f(x5, 1, 1, "11")
    loss_part, dx = _loss_fwd_bwd(x6, target, "loss")

    def ffn_b(dxo, sv, i, j, tag, last=False):
        xin, h, gu, y, wt = sv
        dxi, dy, a, dgu, dg0, dg1 = _ffn_bwd(dxo, xin, y, gu, gain(i, 4 * j), gain(i, 4 * j + 1), wt, "ffn_bwd_" + tag)
        dng[i][4 * j] = dg0
        dng[i][4 * j + 1] = dg1
        after = None
        if last:
            grads["norm_g"] = jnp.stack([jnp.concatenate(dng[t], axis=0) for t in range(2)])
            after = small_ready(grads, loss_part) if small_ready is not None else None
        grads["wd" + tag] = _mm(a, dy, "tn", "ffn_wgrad_down_" + tag, after=after)
        grads["wguT" + tag] = _mm(dgu, h, "tn", "ffn_wgrad_up_" + tag, after=after)
        tell("ffn" + tag)
        return dxi

    dx = ffn_b(dx, sv_f11, 1, 1, "11")
    dm1, dng[1][3] = _postnorm_bwd(dx, m1, gain(1, 3), "sg_postnorm_bwd")
    grads["sg_w_out"] = _mm(um, dm1, "tn", "sg_wgrad_out")
    dum = _mm(dm1, sgw["sg_wout"], "nt", "sg_dgrad_out")
    dz1, dbin, dlng, dlnb, dws, dbst = _sg_bwd(zpre, sg_bin, sg_lng, sg_lnb, p["sg_w_s"], sg_bst, dum, "sg_bwd")
    grads["sg_w_inT"] = _mm(dz1, hn1, "tn", "sg_wgrad_in")
    tell("sg")
    dh1 = _mm(dz1, sgw["sg_win"], "nt", "sg_dgrad_in")
    dx, dng[1][2] = _prenorm_bwd(dx, dh1, x4, gain(1, 2), "sg_prenorm_bwd")
    grads["sg_b_in"] = dbin.reshape(1, -1)
    grads["sg_ln_g"] = dlng.reshape(1, -1)
    grads["sg_ln_b"] = dlnb.reshape(1, -1)
    grads["sg_w_s"] = jnp.where(jnp.tril(jnp.ones((SG_CHUNK, SG_CHUNK), bool)), dws, 0.0)[None]
    grads["sg_b_s"] = dbst[:, :SG_GROUPS].T[None]
    dx = ffn_b(dx, sv_f10, 1, 0, "10")
    dx = ffn_b(dx, sv_f01, 0, 1, "01")
    dm0, dng[0][3] = _postnorm_bwd(dx, m0, gain(0, 3), "dn_postnorm_bwd")
    grads["dn_w_out"] = _mm(og, dm0, "tn", "dn_wgrad_out")
    dog = _mm(dm0, dnw["dn_wout"], "nt", "dn_dgrad_out")
    dqkv, dbeta, dgdec, dz0, dnng = _dn_scan_bwd(qkv, beta, gdec, proj, dn_ng, o_raw, tinv, s_all, dog, "dn_scan_bwd")
    dqkv_pre, dconv = _dn_prep_bwd(proj, p["dn_conv_w"], dqkv, "dn_prep_bwd")
    dba, dal, ddt = _dn_gate_bwd(ba, a_log, dt_bias, dbeta, dgdec, "dn_gate_bwd")
    W3 = 3 * DN_HEADS * DN_HEAD_DIM
    dw_qkv = _mm(hn0, dqkv_pre, "tn", "dn_wgrad_qkv")
    dw_z = _mm(hn0, dz0, "tn", "dn_wgrad_z")
    dw_ba = _mm(hn0, dba, "tn", "dn_wgrad_ba")
    grads["dn_w_in"] = jnp.concatenate(
        [dw_qkv, dw_z, dw_ba[:, :DN_HEADS], dw_ba[:, LANES:LANES + DN_HEADS]], axis=1)
    tell("dn")
    dh0 = _mm(dqkv_pre, dnw["dn_wqkvz"][:, :W3], "nt", "dn_dgrad_qkv")
    dh0 = _mm(dz0, dnw["dn_wqkvz"][:, W3:], "nt", "dn_dgrad_z", add=dh0)
    dh0 = _mm(dba, dnw["dn_wba"], "nt", "dn_dgrad_ba", add=dh0)
    dx, dng[0][2] = _prenorm_bwd(dx, dh0, x1, gain(0, 2), "dn_prenorm_bwd")
    grads["dn_conv_w"] = dconv[None]
    grads["dn_a_log"] = dal[:, :DN_HEADS]
    grads["dn_dt_bias"] = ddt[:, :DN_HEADS]
    grads["dn_norm_g"] = dnng
    dx = ffn_b(dx, sv_f00, 0, 0, "00", last=True)
    return loss_part, dx, grads


def _mesh_pos():
    return lax.axis_index("x"), lax.axis_index("y"), lax.axis_index("c")


def _other_chips(x, y):
    return [(1 - x, y), (x, 1 - y), (1 - x, 1 - y)]


def _allgather_chips(arrs, name):
    n = len(arrs)

    def body(*refs):
        ins, outs = refs[:n], refs[n:2 * n]
        ici_send, ici_recv, d2d_send, d2d_recv = refs[2 * n:]
        x, y, c = _mesh_pos()
        me = 2 * x + y
        chips = _other_chips(x, y)
        sibling = (x, y, 1 - c)

        def ici(i, j, k):
            cx, cy = chips[j]
            return pltpu.make_async_remote_copy(src_ref=ins[i].at[c], dst_ref=outs[i].at[k, c], send_sem=ici_send.at[3 * i + j],
                                                recv_sem=ici_recv.at[3 * i + j], device_id=(cx, cy, c), device_id_type=MESH)

        def d2d(i, j, h):
            cx, cy = chips[j]
            slot = outs[i].at[2 * cx + cy, h]
            return pltpu.make_async_remote_copy(src_ref=slot, dst_ref=slot, send_sem=d2d_send.at[3 * i + j],
                                                recv_sem=d2d_recv.at[3 * i + j], device_id=sibling, device_id_type=MESH)

        sends = [ici(i, j, me) for i in range(n) for j in range(3)]
        for cp in sends:
            cp.start()
        for i in range(n):
            for j, (cx, cy) in enumerate(chips):
                ici(i, j, 2 * cx + cy).wait_recv()
                fwd = d2d(i, j, c)
                fwd.start()
                sends.append(fwd)
        for i in range(n):
            for j in range(3):
                d2d(i, j, 1 - c).wait_recv()
        for cp in sends:
            cp.wait_send()

    return pl.pallas_call(
        body, name=name, in_specs=[ANY] * n, out_specs=[ANY] * n,
        out_shape=[jax.ShapeDtypeStruct((N_CHIPS,) + a.shape, a.dtype) for a in arrs],
        scratch_shapes=[pltpu.SemaphoreType.DMA((3 * n,))] * 4,
    )(*arrs)


HBM = pl.BlockSpec(memory_space=pltpu.HBM)
SEM = pl.BlockSpec(memory_space=pltpu.SEMAPHORE)
TOKEN = jax.ShapeDtypeStruct((SUBLANES, LANES), F32)


_PEERS = {"gather": 3, "scatter": 3, "swap": 1, "all": N_DEV - 1}


def _land_shape(kind, shape):
    if kind == "gather":
        return (N_CHIPS,) + shape
    if kind == "all":
        return (N_DEV,) + shape
    return (N_CHIPS,) + shape[2:] if kind == "swap" else shape


def _peer_copies(kind, flags, src_refs, land_refs, send_sems, recv_sems, receiving):
    x, y, c = _mesh_pos()
    me4, me8 = 2 * x + y, 4 * x + 2 * y + c
    np_ = _PEERS[kind]
    cps = []
    for i, (src, land) in enumerate(zip(src_refs, land_refs)):
        if kind == "swap":
            half = src.at[1 - c] if flags[i] else src.at[:, 1 - c]
            plan = [((x, y, 1 - c), half, land)]
        elif kind == "all":
            masks = [(mx, my, mc) for mx in (0, 1) for my in (0, 1) for mc in (0, 1)][1:]
            peers = [(jnp.where(mx, 1 - x, x), jnp.where(my, 1 - y, y), jnp.where(mc, 1 - c, c)) for mx, my, mc in masks]
            plan = [(p, src, land.at[4 * p[0] + 2 * p[1] + p[2] if receiving else me8]) for p in peers]
        else:
            plan = []
            for cx, cy in _other_chips(x, y):
                k = 2 * cx + cy
                s = src.at[me4 if receiving else k] if kind == "scatter" else src
                plan.append(((cx, cy, c), s, land.at[k if receiving else me4]))
        for j, (peer, s, d) in enumerate(plan):
            cps.append(pltpu.make_async_remote_copy(src_ref=s, dst_ref=d, send_sem=send_sems.at[np_ * i + j],
                                                    recv_sem=recv_sems.at[np_ * i + j], device_id=peer, device_id_type=MESH))
    return cps


def _copies_start(kind, srcs, after, name, flags=None):
    n = len(srcs)
    ns = _PEERS[kind] * n
    lands = [lax.empty(_land_shape(kind, s.shape), s.dtype) for s in srcs]
    after = [] if after is None else [after]

    def body(*refs):
        src_refs, land_refs = refs[:n], refs[n:2 * n]
        send_sems, recv_sems = refs[2 * n + len(after)], refs[2 * n + len(after) + 1]
        token = refs[-1]
        for cp in _peer_copies(kind, flags, src_refs, land_refs, send_sems, recv_sems, False):
            cp.start()
        token[...] = jnp.zeros_like(token)

    outs = pl.pallas_call(
        body, name=name,
        in_specs=[HBM] * (2 * n) + [ANY] * len(after),
        out_specs=(SEM, SEM) + (HBM,) * (2 * n) + (pl.BlockSpec(memory_space=pltpu.VMEM),),
        out_shape=(pltpu.SemaphoreType.DMA((ns,)), pltpu.SemaphoreType.DMA((ns,)))
        + tuple(pltpu.HBM(a.shape, a.dtype) for a in list(srcs) + lands) + (TOKEN,),
        input_output_aliases={i: 2 + i for i in range(2 * n)},
        compiler_params=pltpu.CompilerParams(has_side_effects=pltpu.SideEffectType.DATAFLOW_SIDE_EFFECTING),
    )(*[pltpu.with_memory_space_constraint(a, pltpu.HBM) for a in list(srcs) + lands], *after)
    return dict(sems=outs[:2], srcs=outs[2:2 + n], lands=outs[2 + n:2 + 2 * n], token=outs[-1], kind=kind, flags=flags)


def _copies_wait(started, after, name):
    n = len(started["srcs"])
    kind, flags = started["kind"], started["flags"]
    after = list(after) if isinstance(after, (list, tuple)) else [after]

    def body(*refs):
        src_refs, land_refs = refs[:n], refs[n:2 * n]
        send_sems, recv_sems = refs[2 * n], refs[2 * n + 1]
        for cp in _peer_copies(kind, flags, src_refs, land_refs, send_sems, recv_sems, True):
            cp.wait_send()
            cp.wait_recv()

    outs = pl.pallas_call(
        body, name=name,
        in_specs=[HBM] * (2 * n) + [SEM, SEM] + [ANY] * len(after),
        out_specs=(HBM,) * (2 * n),
        out_shape=tuple(pltpu.HBM(a.shape, a.dtype) for a in list(started["srcs"]) + list(started["lands"])),
        input_output_aliases={i: i for i in range(2 * n)},
        compiler_params=pltpu.CompilerParams(has_side_effects=pltpu.SideEffectType.DATAFLOW_SIDE_EFFECTING),
    )(*started["srcs"], *started["lands"], *started["sems"], *after)
    return outs[:n], outs[n:]


def _swap_whole(arrs, name):
    n = len(arrs)

    def body(*refs):
        ins, outs = refs[:n], refs[n:2 * n]
        send_sems, recv_sems = refs[2 * n:]
        x, y, c = _mesh_pos()
        cps = [pltpu.make_async_remote_copy(src_ref=ins[i], dst_ref=outs[i], send_sem=send_sems.at[i],
                                            recv_sem=recv_sems.at[i], device_id=(x, y, 1 - c), device_id_type=MESH)
               for i in range(n)]
        for cp in cps:
            cp.start()
        for cp in cps:
            cp.wait()

    return pl.pallas_call(
        body, name=name, in_specs=[ANY] * n, out_specs=[ANY] * n,
        out_shape=[jax.ShapeDtypeStruct(a.shape, a.dtype) for a in arrs],
        scratch_shapes=[pltpu.SemaphoreType.DMA((n,)), pltpu.SemaphoreType.DMA((n,))],
    )(*arrs)


def _as_rows(a, lead):
    shp = a.shape
    rows = 1
    for s in shp[lead:-1]:
        rows *= s
    return a.reshape(shp[:lead] + (rows, shp[-1]))


def _row_tile(rows, cols, n_bufs):
    budget = (24 * 1024 * 1024) // (n_bufs * 2 * 4 * cols)
    return _pick(rows, max(2 * SUBLANES, budget), 2 * SUBLANES)


def _sum_devices(own, got, dev, name):
    n, rows, cols = got.shape
    tr = _row_tile(rows, cols, n + 2)

    def body(dev_ref, own_ref, got_ref, o_ref):
        mine = own_ref[...]
        acc = jnp.where(dev_ref[0] == 0, mine, got_ref[0])
        for k in range(1, n):
            acc = acc + jnp.where(dev_ref[0] == k, mine, got_ref[k])
        o_ref[...] = acc

    return pl.pallas_call(
        body, name=name,
        grid_spec=pltpu.PrefetchScalarGridSpec(
            num_scalar_prefetch=1, grid=(rows // tr,),
            in_specs=[pl.BlockSpec((tr, cols), lambda i, d: (i, 0)), pl.BlockSpec((n, tr, cols), lambda i, d: (0, i, 0))],
            out_specs=pl.BlockSpec((tr, cols), lambda i, d: (i, 0))),
        out_shape=jax.ShapeDtypeStruct((rows, cols), F32), compiler_params=_cp("parallel"),
    )(_scalar(dev), own, got)


def _scalar(i):
    return jnp.reshape(i, (1,)).astype(jnp.int32)


def _add_own_half(g, other, c, half_first, name):
    _, rows, cols = other.shape
    tr = _row_tile(rows, cols, 3)

    def body(c_ref, g_ref, o_ref, out_ref):
        out_ref[0] = (g_ref[0, 0] + o_ref[0]).astype(out_ref.dtype)

    if half_first:
        g_map = lambda k, i, c_ref: (c_ref[0], k, i, 0)
    else:
        g_map = lambda k, i, c_ref: (k, c_ref[0], i, 0)
    flat = pl.BlockSpec((1, tr, cols), lambda k, i, c_ref: (k, i, 0))
    return pl.pallas_call(
        body, name=name,
        grid_spec=pltpu.PrefetchScalarGridSpec(
            num_scalar_prefetch=1, grid=(N_CHIPS, rows // tr),
            in_specs=[pl.BlockSpec((1, 1, tr, cols), g_map), flat], out_specs=flat),
        out_shape=jax.ShapeDtypeStruct(other.shape, COMM_DTYPE), compiler_params=_cp("parallel", "parallel"),
    )(_scalar(c), g, other)


def _sum_chips(own, got, chip, name):
    pv = _as_rows(own, 1)
    bv = _as_rows(got, 1)
    _, rows, cols = pv.shape
    tr = _row_tile(rows, cols, N_CHIPS + 2)

    def body(chip_ref, p_ref, b_ref, o_ref):
        mine = p_ref[0].astype(F32)
        acc = jnp.where(chip_ref[0] == 0, mine, b_ref[0].astype(F32))
        for k in range(1, N_CHIPS):
            acc = acc + jnp.where(chip_ref[0] == k, mine, b_ref[k].astype(F32))
        o_ref[...] = acc

    out = pl.pallas_call(
        body, name=name,
        grid_spec=pltpu.PrefetchScalarGridSpec(
            num_scalar_prefetch=1, grid=(rows // tr,),
            in_specs=[pl.BlockSpec((1, tr, cols), lambda i, k_ref: (k_ref[0], i, 0)),
                      pl.BlockSpec((N_CHIPS, tr, cols), lambda i, k_ref: (0, i, 0))],
            out_specs=pl.BlockSpec((tr, cols), lambda i, k_ref: (i, 0))),
        out_shape=jax.ShapeDtypeStruct((rows, cols), F32), compiler_params=_cp("parallel"),
    )(_scalar(chip), pv, bv)
    return out.reshape(own.shape[1:])


def _adam_math(w, g, m, v):
    nm = ADAM_B1 * m + (1.0 - ADAM_B1) * g
    nv = ADAM_B2 * v + (1.0 - ADAM_B2) * (g * g)
    m_hat = nm / (1.0 - ADAM_B1 ** ADAM_STEP)
    v_hat = nv / (1.0 - ADAM_B2 ** ADAM_STEP)
    return -ADAM_LR * (m_hat / (jnp.sqrt(v_hat) + ADAM_EPS) + ADAM_WD * w), nm, nv


def _adamw_halves(w, mine, theirs, m, v, c, name):
    shape = w.shape
    ws, ms, vs = (_as_rows(t.reshape((2, -1) + t.shape[-1:]), 1) for t in (w, m, v))
    a, b = _as_rows(mine, 0), _as_rows(theirs, 0)
    rows, cols = a.shape
    tr = _row_tile(rows, cols, 9)

    def body(c_ref, w_ref, a_ref, b_ref, m_ref, v_ref, g_ref, d_ref, nm_ref, nv_ref):
        gv = jnp.where(pl.program_id(0) == c_ref[0], a_ref[...], b_ref[...])
        g_ref[0] = gv
        d_ref[0], nm_ref[0], nv_ref[0] = _adam_math(w_ref[0], gv, m_ref[0], v_ref[0])

    half = pl.BlockSpec((1, tr, cols), lambda h, i, c_ref: (h, i, 0))
    flat = pl.BlockSpec((tr, cols), lambda h, i, c_ref: (i, 0))
    outs = pl.pallas_call(
        body, name=name,
        grid_spec=pltpu.PrefetchScalarGridSpec(num_scalar_prefetch=1, grid=(2, rows // tr),
                                               in_specs=[half, flat, flat, half, half], out_specs=[half] * 4),
        out_shape=[jax.ShapeDtypeStruct((2, rows, cols), F32)] * 4, compiler_params=_cp("parallel", "parallel"),
    )(_scalar(c), ws, a, b, ms, vs)
    return tuple(o.reshape(shape) for o in outs)


def _adamw(w, g, m, v, name):
    shape = w.shape
    ws, gs, ms, vs = (_as_rows(t, 0) for t in (w, g, m, v))
    rows, cols = ws.shape
    tr = _row_tile(rows, cols, 7)

    def body(w_ref, g_ref, m_ref, v_ref, d_ref, nm_ref, nv_ref):
        d_ref[...], nm_ref[...], nv_ref[...] = _adam_math(w_ref[...], g_ref[...], m_ref[...], v_ref[...])

    spec = pl.BlockSpec((tr, cols), lambda i: (i, 0))
    outs = pl.pallas_call(body, name=name, grid=(rows // tr,), in_specs=[spec] * 4, out_specs=[spec] * 3,
                          out_shape=[jax.ShapeDtypeStruct((rows, cols), F32)] * 3, compiler_params=_cp("parallel"))(ws, gs, ms, vs)
    return tuple(o.reshape(shape) for o in outs)


_BIG = ["ffn_w_gate", "ffn_w_up", "ffn_w_down", "dn_w_in", "dn_w_out", "sg_w_in", "sg_w_out"]
_SMALL_SHARDED = ["norm_g", "dn_conv_w", "sg_b_in", "sg_ln_g", "sg_ln_b"]
_SMALL_REPL = ["dn_a_log", "dn_dt_bias", "dn_norm_g", "sg_w_s", "sg_b_s"]
_WEIGHTS = ["norm_g", "ffn_w_gate", "ffn_w_up", "ffn_w_down", "dn_w_in", "dn_conv_w", "dn_a_log", "dn_dt_bias",
            "dn_norm_g", "dn_w_out", "sg_w_in", "sg_b_in", "sg_ln_g", "sg_ln_b", "sg_w_s", "sg_b_s", "sg_w_out"]
PACK_COLS = 1024


def _pack(arrs):
    flat = jnp.concatenate([a.reshape(-1) for a in arrs])
    pad = (-flat.shape[0]) % (SUBLANES * PACK_COLS)
    return jnp.pad(flat, (0, pad)).reshape(-1, PACK_COLS)


def _unpack(buf, shapes):
    flat = buf.reshape(-1)
    out, off = [], 0
    for s in shapes:
        n = math.prod(s)
        out.append(flat[off:off + n].reshape(s))
        off += n
    return out


def _as_halves(a):
    if a.shape[0] == 2:
        return a
    if a.shape[0] == 1:
        return a.reshape((2, a.shape[1] // 2) + a.shape[2:])
    return a.reshape((2, a.shape[0] // 2) + a.shape[1:])


def _with_own(gathered, own, chip):
    g = gathered.reshape((N_CHIPS,) + own.shape)
    return [jnp.where(chip == k, own, g[k]) for k in range(N_CHIPS)]


def _cat_shards(g, axis):
    return jnp.concatenate(list(g), axis=axis)


_GROUP_ORDER = ["ffn00", "dn", "ffn01", "ffn10", "sg", "ffn11"]


def _weight_groups(w):
    cast = {k: _mx(w[k]) for k in _BIG}
    groups = {"ffn%d%d" % (i, j): [cast["ffn_w_gate"][i, j].T, cast["ffn_w_up"][i, j].T, cast["ffn_w_down"][i, j]]
              for i, j in [(0, 0), (0, 1), (1, 0), (1, 1)]}
    groups["dn"] = [cast["dn_w_in"][0], cast["dn_w_out"][0]]
    groups["sg"] = [cast["sg_w_in"][0], cast["sg_w_out"][0]]
    return groups


def _ffn_weights(chip, own, gathered):
    pairs = [(a, g.reshape((N_CHIPS,) + a.shape)) for a, g in zip(own, gathered)]
    return {"chip": chip, "gate": pairs[0], "up": pairs[1], "down": pairs[2]}


def _group_matrices(group, shards):
    if group == "sg":
        return {"sg_win": _cat_shards(shards[0], 1), "sg_wout": _cat_shards(shards[1], 0)}
    dn_full = _cat_shards(shards[0], 1)
    W4 = 4 * DN_HEADS * DN_HEAD_DIM
    wba = jnp.zeros((D_MODEL, 2 * LANES), dn_full.dtype)
    wba = wba.at[:, :DN_HEADS].set(dn_full[:, W4:W4 + DN_HEADS])
    wba = wba.at[:, LANES:LANES + DN_HEADS].set(dn_full[:, W4 + DN_HEADS:])
    return {"dn_wqkvz": dn_full[:, :W4], "dn_wba": wba, "dn_wout": _cat_shards(shards[1], 0)}


def _split_cols(a, n):
    w = a.shape[-1] // n
    return [a[..., k * w:(k + 1) * w] for k in range(n)]


def _split_rows(a, n):
    h = a.shape[-2] // n
    return [a[..., k * h:(k + 1) * h, :] for k in range(n)]


_IJ = [(0, 0), (0, 1), (1, 0), (1, 1)]


_REDUCED = ["wguT%d%d" % ij for ij in _IJ] + ["wd%d%d" % ij for ij in _IJ] + ["dn_w_in", "dn_w_out", "sg_w_inT", "sg_w_out"]


def _group_grads(group, grads):
    def rows_by_chip(a):
        return a.reshape(N_CHIPS, 2, a.shape[0] // (2 * N_CHIPS), a.shape[1])

    if group.startswith("ffn"):
        tag = group[3:]
        t = grads["wguT" + tag]
        return (["wguT" + tag, "wd" + tag],
                [t.reshape(2, N_CHIPS, t.shape[0] // (2 * N_CHIPS), t.shape[1]), rows_by_chip(grads["wd" + tag])], [True, False])
    if group == "sg":
        return ["sg_w_inT", "sg_w_out"], [rows_by_chip(grads["sg_w_inT"]), rows_by_chip(grads["sg_w_out"])], [False, False]
    dn_in = jnp.stack([jnp.stack(_split_cols(hf, N_CHIPS)) for hf in _split_rows(grads["dn_w_in"], 2)])
    return ["dn_w_in", "dn_w_out"], [dn_in, rows_by_chip(grads["dn_w_out"])], [True, False]


def _shard_grads(mine, theirs, c, w, keys):
    lo = lambda n: jnp.where(c == 0, mine[n], theirs[n])
    hi = lambda n: jnp.where(c == 0, theirs[n], mine[n])
    rows = lambda n: jnp.concatenate([lo(n), hi(n)], axis=0)
    sq = lambda parts: jnp.stack(parts).reshape(2, 2, *parts[0].shape)
    tags = ["%d%d" % ij for ij in _IJ]
    make = {
        "ffn_w_gate": lambda: sq([lo("wguT" + t).T for t in tags]),
        "ffn_w_up": lambda: sq([hi("wguT" + t).T for t in tags]),
        "ffn_w_down": lambda: sq([rows("wd" + t) for t in tags]),
        "dn_w_in": lambda: rows("dn_w_in"),
        "dn_w_out": lambda: rows("dn_w_out"),
        "sg_w_in": lambda: rows("sg_w_inT").T,
        "sg_w_out": lambda: rows("sg_w_out"),
    }
    g = {k: make[k]() for k in keys}
    return {k: v.reshape(w[k].shape) for k, v in g.items()}


def kernel(x, norm_g, ffn_w_gate, ffn_w_up, ffn_w_down, dn_w_in, dn_conv_w, dn_a_log, dn_dt_bias, dn_norm_g, dn_w_out, sg_w_in, sg_b_in, sg_ln_g, sg_ln_b, sg_w_s, sg_b_s, sg_w_out, loss_target, m_norm_g, m_ffn_w_gate, m_ffn_w_up, m_ffn_w_down, m_dn_w_in, m_dn_conv_w, m_dn_a_log, m_dn_dt_bias, m_dn_norm_g, m_dn_w_out, m_sg_w_in, m_sg_b_in, m_sg_ln_g, m_sg_ln_b, m_sg_w_s, m_sg_b_s, m_sg_w_out, v_norm_g, v_ffn_w_gate, v_ffn_w_up, v_ffn_w_down, v_dn_w_in, v_dn_conv_w, v_dn_a_log, v_dn_dt_bias, v_dn_norm_g, v_dn_w_out, v_sg_w_in, v_sg_b_in, v_sg_ln_g, v_sg_ln_b, v_sg_w_s, v_sg_b_s, v_sg_w_out):
    args = dict(locals())
    w = {k: args[k] for k in _WEIGHTS}
    mom = {k: args["m_" + k] for k in _WEIGHTS}
    var = {k: args["v_" + k] for k in _WEIGHTS}
    cx, cy, cc = _mesh_pos()
    chip = 2 * cx + cy

    small_shapes = [w[k].shape for k in _SMALL_SHARDED]
    groups = _weight_groups(w)
    own = groups[_GROUP_ORDER[0]] + [_pack([w[k] for k in _SMALL_SHARDED])]
    first = _allgather_chips([_as_halves(a) for a in own], "gather_first")
    started, after = {}, first[0]
    for g in _GROUP_ORDER[1:]:
        started[g] = _copies_start("gather", groups[g], after, "gather_start_" + g)
        after = started[g]["token"]
    small_k = [_unpack(pack, small_shapes) for pack in _with_own(first[-1], own[-1], chip)]
    p = {name: jnp.concatenate([small_k[k][i] for k in range(N_CHIPS)], axis=-1) for i, name in enumerate(_SMALL_SHARDED)}
    p = {k: (v if k == "norm_g" else v[0]) for k, v in p.items()}
    p["norm_g"] = p["norm_g"] + after[0, 0]
    for k in _SMALL_REPL:
        p[k] = w[k][0]

    def weights_for(group, after):
        if group == _GROUP_ORDER[0]:
            return _ffn_weights(chip, own[:-1], first[:-1])
        srcs, lands = _copies_wait(started[group], after, "gather_wait_" + group)
        if group.startswith("ffn"):
            return _ffn_weights(chip, srcs, lands)
        return _group_matrices(group, [_with_own(l, a, chip) for l, a in zip(lands, srcs)])

    mine, theirs, to_core, to_chips = {}, {}, [], []

    def send_to_chips(after):
        group, names, flags, swap = to_core.pop(0)
        halves, got = _copies_wait(swap, after, "swap_wait_" + group)
        pair_sum = [_add_own_half(h, o, cc, hf, "pair_sum_" + n) for n, h, o, hf in zip(names, halves, got, flags)]
        scatter = _copies_start("scatter", pair_sum, got[0], "reduce_start_" + group)
        to_chips.append((group, names, scatter))
        return scatter["token"]

    def finish(after):
        group, names, scatter = to_chips.pop(0)
        pair_sum, got = _copies_wait(scatter, after, "reduce_wait_" + group)
        half_sum = [_sum_chips(a, b, chip, "chip_sum_" + n) for n, a, b in zip(names, pair_sum, got)]
        other = _swap_whole(half_sum, "gather_core_pair_" + group)
        mine.update(zip(names, half_sum))
        theirs.update(zip(names, other))

    def grads_ready(group, grads):
        names, halves, flags = _group_grads(group, grads)
        swap = _copies_start("swap", halves, None, "swap_start_" + group, flags)
        token = swap["token"]
        if to_core:
            token = send_to_chips(token)
            if len(to_chips) > 1:
                finish(token)
        to_core.append((group, names, flags, swap))
        return token[0, 0]

    small_names = _SMALL_SHARDED + _SMALL_REPL
    small = {}

    def small_ready(grads, loss_part):
        parts = [grads[k] for k in small_names]
        small["shapes"] = [g.shape for g in parts] + [(1,)]
        pack = _pack(parts + [loss_part[0, :1]])
        small["exchange"] = _copies_start("all", [pack], None, "small_start")
        return small["exchange"]["token"]

    loss_part, grad_x, grads = _local_step(x[0], loss_target[0], p, weights_for, grads_ready, small_ready)
    token = send_to_chips(to_core[0][3]["token"])
    finish(token)
    (pack,), (packs,) = _copies_wait(small["exchange"], list(theirs.values()), "small_wait")
    summed = _sum_devices(pack, packs, 4 * cx + 2 * cy + cc, "small_sum")
    parts = _unpack(summed, small["shapes"])
    loss = parts[-1][0]
    grad = {}
    for i, k in enumerate(small_names):
        g = parts[i]
        if k in _SMALL_SHARDED:
            n = w[k].shape[-1]
            g = lax.dynamic_slice_in_dim(g, chip * n, n, axis=g.ndim - 1)
        grad[k] = g

    delta, new_m, new_v = {}, {}, {}

    def update(keys):
        grad.update(_shard_grads(mine, theirs, cc, w, keys))
        for k in keys:
            delta[k], new_m[k], new_v[k] = _adamw(w[k], grad[k], mom[k], var[k], "adamw_" + k)

    shapes = [w[k].shape for k in small_names]
    d, nm, nv = _adamw(_pack([w[k] for k in small_names]), _pack([grad[k] for k in small_names]),
                       _pack([mom[k] for k in small_names]), _pack([var[k] for k in small_names]), "adamw_small")
    for k, a, b, c_ in zip(small_names, _unpack(d, shapes), _unpack(nm, shapes), _unpack(nv, shapes)):
        delta[k], new_m[k], new_v[k] = a, b, c_
    mixers = [k for k in _BIG if not k.startswith("ffn")]
    update(mixers)
    finish([d] + [delta[k] for k in mixers] + list(theirs.values()))
    update([k for k in _BIG if k.startswith("ffn")])

    return (loss, grad_x[None], *[grad[k] for k in _WEIGHTS], *[delta[k] for k in _WEIGHTS],
            *[new_m[k] for k in _WEIGHTS], *[new_v[k] for k in _WEIGHTS])
```

```python
import functools
import math

import jax
import jax.numpy as jnp
from jax import lax
from jax.experimental import pallas as pl
from jax.experimental.pallas import tpu as pltpu

F32 = jnp.float32
MXU_DTYPE = jnp.bfloat16
COMM_DTYPE = jnp.bfloat16
HI = lax.Precision.HIGHEST
TRI_PREC = lax.Precision.HIGH

D_MODEL = 1024
D_FF = 2816
RMS_EPS = 1e-6
LN_EPS = 1e-5
L2_EPS = 1e-6
DN_HEADS = 8
DN_HEAD_DIM = 128
DN_CONV = 4
DN_CHUNK = 64
SG_WIDTH = 2048
SG_GROUPS = 8
SG_CHUNK = 128
SG_GROUP_W = SG_WIDTH // SG_GROUPS
N_CHIPS = 4
N_DEV = 8
LANES = 128
SUBLANES = 8
VMEM_LIMIT = 56 * 1024 * 1024

ADAM_LR = 0.001
ADAM_B1 = 0.9
ADAM_B2 = 0.999
ADAM_EPS = 1e-08
ADAM_WD = 0.01
ADAM_STEP = 10

MESH = pl.DeviceIdType.MESH
ANY = pl.BlockSpec(memory_space=pl.ANY)


def _cp(*sem):
    return pltpu.CompilerParams(dimension_semantics=sem, vmem_limit_bytes=VMEM_LIMIT)


def _pick(n, pref, mult=LANES):
    best = None
    d = mult
    while d <= min(n, pref):
        if n % d == 0:
            best = d
        d += mult
    return best if best is not None else n


def _full(shape):
    nd = len(shape)
    return pl.BlockSpec(shape, lambda *_: (0,) * nd)


def _sigmoid(x):
    return 1.0 / (1.0 + jnp.exp(-x))


def _dot(a, b, dims, prec=None):
    return lax.dot_general(a, b, (dims, ((), ())), preferred_element_type=F32, precision=prec)


NN = ((1,), (0,))
NT = ((1,), (1,))
TN = ((0,), (0,))


def _mx(a):
    return a.astype(MXU_DTYPE)


def _rms_stat(x):
    return lax.rsqrt(jnp.mean(x * x, axis=-1, keepdims=True) + RMS_EPS)


def _rms_bwd(x, r, g, dy):
    xh = x * r
    dxh = dy * g
    dx = r * (dxh - xh * jnp.mean(dxh * xh, axis=-1, keepdims=True))
    return dx, jnp.sum(dy * xh, axis=0, keepdims=True)


def _mm(a, b, mode, name, out_dtype=F32, add=None, after=None):
    if mode == "tn":
        K, M = a.shape
        N = b.shape[1]
    elif mode == "nt":
        M, K = a.shape
        N = b.shape[0]
    else:
        M, K = a.shape
        N = b.shape[1]
    tn = _pick(N, 1024)
    if mode == "tn":
        tm = _pick(M, 1024 if tn <= 512 else 1408)
        tk = _pick(K, 1024, SUBLANES)
    else:
        tm = _pick(M, max(512, min(2048, (1024 * 1024) // tn)), SUBLANES)
        tk = _pick(K, 2048)
    nk = K // tk
    grid = (N // tn, M // tm, nk)
    if mode == "nn":
        a_spec = pl.BlockSpec((tm, tk), lambda j, i, k: (i, k))
        b_spec = pl.BlockSpec((tk, tn), lambda j, i, k: (k, j))
        dims = NN
    elif mode == "nt":
        a_spec = pl.BlockSpec((tm, tk), lambda j, i, k: (i, k))
        b_spec = pl.BlockSpec((tn, tk), lambda j, i, k: (j, k))
        dims = NT
    else:
        a_spec = pl.BlockSpec((tk, tm), lambda j, i, k: (k, i))
        b_spec = pl.BlockSpec((tk, tn), lambda j, i, k: (k, j))
        dims = TN
    o_spec = pl.BlockSpec((tm, tn), lambda j, i, k: (i, j))
    has_add = add is not None

    def body(*refs):
        a_ref, b_ref = refs[:2]
        add_ref = refs[2] if has_add else None
        o_ref, acc = refs[-2:]
        k = pl.program_id(2)

        @pl.when(k == 0)
        def _():
            acc[...] = add_ref[...] if has_add else jnp.zeros_like(acc)

        acc[...] += _dot(a_ref[...], b_ref[...], dims)

        @pl.when(k == nk - 1)
        def _():
            o_ref[...] = acc[...].astype(o_ref.dtype)

    ins = [a, b] + ([add] if has_add else []) + ([after] if after is not None else [])
    specs = [a_spec, b_spec] + ([o_spec] if has_add else []) + ([ANY] if after is not None else [])
    return pl.pallas_call(
        body, name=name, grid=grid, in_specs=specs, out_specs=o_spec,
        out_shape=jax.ShapeDtypeStruct((M, N), out_dtype),
        scratch_shapes=[pltpu.VMEM((tm, tn), F32)],
        compiler_params=_cp("parallel", "parallel", "arbitrary"),
    )(*ins)


def _ffn_weight_operands(wt):
    return [_scalar(wt["chip"])] , [wt["gate"][0], wt["gate"][1], wt["up"][0], wt["up"][1], wt["down"][0], wt["down"][1]]


def _load_ffn_weights(chip_ref, shard_refs, wgu_v, wd_v, sem):
    fs = wd_v.shape[0] // N_CHIPS

    @pl.when(pl.program_id(0) == 0)
    def _():
        me = chip_ref[0]
        waits = []
        for t, (dst, base) in enumerate([(wgu_v, 0), (wgu_v, wd_v.shape[0]), (wd_v, 0)]):
            own, gathered = shard_refs[2 * t], shard_refs[2 * t + 1]
            for k in range(N_CHIPS):
                slot = dst.at[pl.ds(base + k * fs, fs), :]
                s = sem.at[t * N_CHIPS + k]

                @pl.when(me == k)
                def _(own=own, slot=slot, s=s):
                    pltpu.make_async_copy(own, slot, s).start()

                @pl.when(me != k)
                def _(gathered=gathered, k=k, slot=slot, s=s):
                    pltpu.make_async_copy(gathered.at[k], slot, s).start()

                waits.append(pltpu.make_async_copy(own, slot, s))
        for cp in waits:
            cp.wait()


def _ffn_fwd(x, g0, g1, wt, name):
    T, D = x.shape
    F = N_CHIPS * wt["down"][0].shape[0]
    F2 = 2 * F
    tm = _pick(T, 256, SUBLANES)
    prefetch, shards = _ffn_weight_operands(wt)

    def body(chip_ref, x_ref, g0_ref, g1_ref, *refs):
        shard_refs = refs[:6]
        xo_ref, h_ref, gu_ref, y_ref, wgu_v, wd_v, sem = refs[6:]
        _load_ffn_weights(chip_ref, shard_refs, wgu_v, wd_v, sem)
        xv = x_ref[...]
        hb = _mx(xv * _rms_stat(xv) * g0_ref[...])
        h_ref[...] = hb
        gu = _dot(hb, wgu_v[...], NT)
        gu_ref[...] = gu.astype(gu_ref.dtype)
        g = gu[:, :F]
        u = gu[:, F:]
        a = _mx(g * _sigmoid(g) * u)
        y = _dot(a, wd_v[...], NN)
        y_ref[...] = y
        xo_ref[...] = xv + 0.5 * (y * _rms_stat(y) * g1_ref[...])

    row = lambda w: pl.BlockSpec((tm, w), lambda i, c: (i, 0))
    one = pl.BlockSpec((1, D), lambda i, c: (0, 0))
    return pl.pallas_call(
        body, name=name,
        grid_spec=pltpu.PrefetchScalarGridSpec(
            num_scalar_prefetch=1, grid=(T // tm,),
            in_specs=[row(D), one, one] + [ANY] * 6,
            out_specs=[row(D), row(D), row(F2), row(D)],
            scratch_shapes=[pltpu.VMEM((F2, D), MXU_DTYPE), pltpu.VMEM((F, D), MXU_DTYPE),
                            pltpu.SemaphoreType.DMA((3 * N_CHIPS,))]),
        out_shape=[jax.ShapeDtypeStruct((T, D), F32), jax.ShapeDtypeStruct((T, D), MXU_DTYPE),
                   jax.ShapeDtypeStruct((T, F2), MXU_DTYPE), jax.ShapeDtypeStruct((T, D), F32)],
        compiler_params=_cp("arbitrary"),
    )(*prefetch, x, g0, g1, *shards)


FFN_BWD_CHUNK = 1408


def _ffn_bwd(dxo, x, y, gu, g0, g1, wt, name):
    T, D = x.shape
    F2 = gu.shape[1]
    F = F2 // 2
    tm = _pick(T, 256, SUBLANES)
    fc = _pick(F, FFN_BWD_CHUNK)
    prefetch, shards = _ffn_weight_operands(wt)

    def body(chip_ref, dxo_ref, x_ref, y_ref, gu_ref, g0_ref, g1_ref, *refs):
        shard_refs = refs[:6]
        dx_ref, dy_ref, a_ref, dgu_ref, dg0_ref, dg1_ref, wgu_v, wd_v, sem = refs[6:]
        _load_ffn_weights(chip_ref, shard_refs, wgu_v, wd_v, sem)

        @pl.when(pl.program_id(0) == 0)
        def _():
            dg0_ref[...] = jnp.zeros_like(dg0_ref)
            dg1_ref[...] = jnp.zeros_like(dg1_ref)

        dxo_v = dxo_ref[...]
        yv = y_ref[...]
        dy, dg1 = _rms_bwd(yv, _rms_stat(yv), g1_ref[...], 0.5 * dxo_v)
        dg1_ref[...] += dg1
        dyb = _mx(dy)
        dy_ref[...] = dyb
        dh = jnp.zeros((tm, D), F32)
        for c in range(F // fc):
            lo, hi = c * fc, (c + 1) * fc
            da = _dot(dyb, wd_v[lo:hi, :], NT)
            g = gu_ref[:, lo:hi].astype(F32)
            u = gu_ref[:, F + lo:F + hi].astype(F32)
            s = _sigmoid(g)
            sg = g * s
            a_ref[:, lo:hi] = _mx(sg * u)
            dg = _mx(da * u * (s * (1.0 + g * (1.0 - s))))
            du = _mx(da * sg)
            dgu_ref[:, lo:hi] = dg
            dgu_ref[:, F + lo:F + hi] = du
            dh = dh + _dot(dg, wgu_v[lo:hi, :], NN) + _dot(du, wgu_v[F + lo:F + hi, :], NN)
        xv = x_ref[...]
        dx, dg0 = _rms_bwd(xv, _rms_stat(xv), g0_ref[...], dh)
        dg0_ref[...] += dg0
        dx_ref[...] = dxo_v + dx

    row = lambda w: pl.BlockSpec((tm, w), lambda i, c: (i, 0))
    one = pl.BlockSpec((1, D), lambda i, c: (0, 0))
    return pl.pallas_call(
        body, name=name,
        grid_spec=pltpu.PrefetchScalarGridSpec(
            num_scalar_prefetch=1, grid=(T // tm,),
            in_specs=[row(D), row(D), row(D), row(F2), one, one] + [ANY] * 6,
            out_specs=[row(D), row(D), row(F), row(F2), one, one],
            scratch_shapes=[pltpu.VMEM((F2, D), MXU_DTYPE), pltpu.VMEM((F, D), MXU_DTYPE),
                            pltpu.SemaphoreType.DMA((3 * N_CHIPS,))]),
        out_shape=[jax.ShapeDtypeStruct((T, D), F32), jax.ShapeDtypeStruct((T, D), MXU_DTYPE),
                   jax.ShapeDtypeStruct((T, F), MXU_DTYPE), jax.ShapeDtypeStruct((T, F2), MXU_DTYPE),
                   jax.ShapeDtypeStruct((1, D), F32), jax.ShapeDtypeStruct((1, D), F32)],
        compiler_params=_cp("arbitrary"),
    )(*prefetch, dxo, x, y, gu, g0, g1, *shards)


def _norm_fwd(x, g, name):
    T, D = x.shape
    tm = _pick(T, 512, SUBLANES)

    def body(x_ref, g_ref, h_ref):
        xv = x_ref[...]
        h_ref[...] = _mx(xv * _rms_stat(xv) * g_ref[...])

    row = pl.BlockSpec((tm, D), lambda i: (i, 0))
    return pl.pallas_call(body, name=name, grid=(T // tm,), in_specs=[row, _full((1, D))], out_specs=row,
                          out_shape=jax.ShapeDtypeStruct((T, D), MXU_DTYPE), compiler_params=_cp("parallel"))(x, g)


def _postnorm_fwd(x, m, g, name):
    T, D = x.shape
    tm = _pick(T, 512, SUBLANES)

    def body(x_ref, m_ref, g_ref, o_ref):
        mv = m_ref[...]
        o_ref[...] = x_ref[...] + mv * _rms_stat(mv) * g_ref[...]

    row = pl.BlockSpec((tm, D), lambda i: (i, 0))
    return pl.pallas_call(body, name=name, grid=(T // tm,), in_specs=[row, row, _full((1, D))], out_specs=row,
                          out_shape=jax.ShapeDtypeStruct((T, D), F32), compiler_params=_cp("parallel"))(x, m, g)


def _postnorm_bwd(dxo, m, g, name):
    T, D = m.shape
    tm = _pick(T, 512, SUBLANES)

    def body(dxo_ref, m_ref, g_ref, dm_ref, dg_ref):
        @pl.when(pl.program_id(0) == 0)
        def _():
            dg_ref[...] = jnp.zeros_like(dg_ref)

        mv = m_ref[...]
        dm, dg = _rms_bwd(mv, _rms_stat(mv), g_ref[...], dxo_ref[...])
        dg_ref[...] += dg
        dm_ref[...] = _mx(dm)

    row = pl.BlockSpec((tm, D), lambda i: (i, 0))
    return pl.pallas_call(body, name=name, grid=(T // tm,), in_specs=[row, row, _full((1, D))],
                          out_specs=[row, _full((1, D))],
                          out_shape=[jax.ShapeDtypeStruct((T, D), MXU_DTYPE), jax.ShapeDtypeStruct((1, D), F32)],
                          compiler_params=_cp("arbitrary"))(dxo, m, g)


def _prenorm_bwd(dxo, dh, x, g, name):
    T, D = x.shape
    tm = _pick(T, 512, SUBLANES)

    def body(dxo_ref, dh_ref, x_ref, g_ref, dx_ref, dg_ref):
        @pl.when(pl.program_id(0) == 0)
        def _():
            dg_ref[...] = jnp.zeros_like(dg_ref)

        xv = x_ref[...]
        dx, dg = _rms_bwd(xv, _rms_stat(xv), g_ref[...], dh_ref[...])
        dg_ref[...] += dg
        dx_ref[...] = dxo_ref[...] + dx

    row = pl.BlockSpec((tm, D), lambda i: (i, 0))
    return pl.pallas_call(body, name=name, grid=(T // tm,), in_specs=[row, row, row, _full((1, D))],
                          out_specs=[row, _full((1, D))],
                          out_shape=[jax.ShapeDtypeStruct((T, D), F32), jax.ShapeDtypeStruct((1, D), F32)],
                          compiler_params=_cp("arbitrary"))(dxo, dh, x, g)


def _loss_fwd_bwd(y, target, name):
    T, D = y.shape
    tm = _pick(T, 512, SUBLANES)

    def body(y_ref, t_ref, l_ref, dy_ref):
        @pl.when(pl.program_id(0) == 0)
        def _():
            l_ref[...] = jnp.zeros_like(l_ref)

        e = y_ref[...] - t_ref[...]
        dy_ref[...] = e * (1.0 / D)
        l_ref[...] += 0.5 * jnp.sum(jnp.mean(e * e, axis=-1, keepdims=True), axis=0, keepdims=True)

    row = pl.BlockSpec((tm, D), lambda i: (i, 0))
    return pl.pallas_call(body, name=name, grid=(T // tm,), in_specs=[row, row],
                          out_specs=[_full((SUBLANES, LANES)), row],
                          out_shape=[jax.ShapeDtypeStruct((SUBLANES, LANES), F32), jax.ShapeDtypeStruct((T, D), F32)],
                          compiler_params=_cp("arbitrary"))(y, target)


DN_ROWS = 512


def _shift_down(prev8, cur, s):
    n = cur.shape[0]
    xx = jnp.concatenate([prev8, cur], axis=0)
    return pltpu.roll(xx, s, 0)[SUBLANES:SUBLANES + n, :]


def _shift_up(cur, next8, s):
    n = cur.shape[0]
    xx = jnp.concatenate([cur, next8], axis=0)
    return pltpu.roll(xx, n + SUBLANES - s, 0)[:n, :]


def _conv_tile(x_ref, w, r, rows):
    start = pl.multiple_of(r * rows, SUBLANES)
    cur = x_ref[pl.ds(start, rows), :]
    pstart = pl.multiple_of(jnp.maximum(start - SUBLANES, 0), SUBLANES)
    prev8 = jnp.where(r == 0, 0.0, x_ref[pl.ds(pstart, SUBLANES), :])
    taps = [_shift_down(prev8, cur, DN_CONV - 1 - j) if j < DN_CONV - 1 else cur for j in range(DN_CONV)]
    c = taps[0] * w[0:1, :]
    for j in range(1, DN_CONV):
        c = c + taps[j] * w[j:j + 1, :]
    return c, taps


def _dn_prep_fwd(proj, conv_w, name):
    T = proj.shape[0]
    W = DN_HEADS * DN_HEAD_DIM
    rows = min(DN_ROWS, T)
    n_inner = T // rows
    scale = DN_HEAD_DIM ** -0.5

    def body(x_ref, w_ref, o_ref):
        cb = pl.program_id(0)
        w = w_ref[...]
        is_qk = cb < 2 * DN_HEADS
        post = jnp.where(cb < DN_HEADS, scale, 1.0)

        def step(r, carry):
            c, _ = _conv_tile(x_ref, w, r, rows)
            s = c * _sigmoid(c)
            rinv = lax.rsqrt(jnp.sum(s * s, axis=-1, keepdims=True) + L2_EPS)
            o_ref[pl.ds(pl.multiple_of(r * rows, SUBLANES), rows), :] = jnp.where(is_qk, s * rinv * post, s)
            return carry

        lax.fori_loop(0, n_inner, step, 0)

    col = pl.BlockSpec((T, LANES), lambda j: (0, j))
    return pl.pallas_call(body, name=name, grid=(3 * W // LANES,),
                          in_specs=[col, pl.BlockSpec((DN_CONV, LANES), lambda j: (0, j))], out_specs=col,
                          out_shape=jax.ShapeDtypeStruct((T, 3 * W), F32), compiler_params=_cp("parallel"))(proj, conv_w)


def _dn_prep_bwd(proj, conv_w, dqkv, name):
    T = proj.shape[0]
    W = DN_HEADS * DN_HEAD_DIM
    rows = min(DN_ROWS, T)
    n_inner = T // rows
    scale = DN_HEAD_DIM ** -0.5

    def body(x_ref, w_ref, dy_ref, dx_ref, dw_ref, dc_scr):
        cb = pl.program_id(0)
        w = w_ref[...]
        is_qk = cb < 2 * DN_HEADS
        post = jnp.where(cb < DN_HEADS, scale, 1.0)

        def step1(r, dws):
            c, taps = _conv_tile(x_ref, w, r, rows)
            sg = _sigmoid(c)
            s = c * sg
            rinv = lax.rsqrt(jnp.sum(s * s, axis=-1, keepdims=True) + L2_EPS)
            dy = dy_ref[pl.ds(pl.multiple_of(r * rows, SUBLANES), rows), :]
            yn = s * rinv
            dyn = dy * post
            ds_qk = rinv * (dyn - yn * jnp.sum(dyn * yn, axis=-1, keepdims=True))
            ds = jnp.where(is_qk, ds_qk, dy)
            dc = ds * (sg * (1.0 + c * (1.0 - sg)))
            dc_scr[pl.ds(pl.multiple_of(r * rows, SUBLANES), rows), :] = dc
            return tuple(dws[j] + jnp.sum(dc * taps[j], axis=0, keepdims=True) for j in range(DN_CONV))

        zero = jnp.zeros((1, LANES), F32)
        dws = lax.fori_loop(0, n_inner, step1, (zero,) * DN_CONV)
        for j in range(DN_CONV):
            dw_ref[j:j + 1, :] = dws[j]

        def step2(r, carry):
            start = pl.multiple_of(r * rows, SUBLANES)
            cur = dc_scr[pl.ds(start, rows), :]
            nstart = pl.multiple_of(jnp.minimum(start + rows, T - SUBLANES), SUBLANES)
            next8 = jnp.where(r == n_inner - 1, 0.0, dc_scr[pl.ds(nstart, SUBLANES), :])
            dx = cur * w[DN_CONV - 1:DN_CONV, :]
            for j in range(DN_CONV - 1):
                dx = dx + _shift_up(cur, next8, DN_CONV - 1 - j) * w[j:j + 1, :]
            dx_ref[pl.ds(start, rows), :] = _mx(dx)
            return carry

        lax.fori_loop(0, n_inner, step2, 0)

    col = pl.BlockSpec((T, LANES), lambda j: (0, j))
    wspec = pl.BlockSpec((DN_CONV, LANES), lambda j: (0, j))
    return pl.pallas_call(body, name=name, grid=(3 * W // LANES,), in_specs=[col, wspec, col], out_specs=[col, wspec],
                          out_shape=[jax.ShapeDtypeStruct((T, 3 * W), MXU_DTYPE), jax.ShapeDtypeStruct((DN_CONV, 3 * W), F32)],
                          scratch_shapes=[pltpu.VMEM((T, LANES), F32)], compiler_params=_cp("parallel"))(proj, conv_w, dqkv)


def _softplus(x):
    return jnp.maximum(x, 0.0) + jnp.log(1.0 + jnp.exp(-jnp.abs(x)))


def _dn_gate_fwd(ba, a_log, dt_bias, name):
    T = ba.shape[0]
    tm = _pick(T, 1024, SUBLANES)

    def body(ba_ref, al_ref, dt_ref, beta_ref, g_ref):
        beta_ref[...] = _sigmoid(ba_ref[:, :LANES])
        g_ref[...] = -jnp.exp(al_ref[...]) * _softplus(ba_ref[:, LANES:] + dt_ref[...])

    row = lambda w: pl.BlockSpec((tm, w), lambda i: (i, 0))
    return pl.pallas_call(body, name=name, grid=(T // tm,), in_specs=[row(2 * LANES), _full((1, LANES)), _full((1, LANES))],
                          out_specs=[row(LANES), row(LANES)],
                          out_shape=[jax.ShapeDtypeStruct((T, LANES), F32)] * 2, compiler_params=_cp("parallel"))(ba, a_log, dt_bias)


def _dn_gate_bwd(ba, a_log, dt_bias, dbeta, dg, name):
    T = ba.shape[0]
    tm = _pick(T, 1024, SUBLANES)

    def body(ba_ref, al_ref, dt_ref, dbeta_ref, dg_ref, dba_ref, dal_ref, ddt_ref):
        @pl.when(pl.program_id(0) == 0)
        def _():
            dal_ref[...] = jnp.zeros_like(dal_ref)
            ddt_ref[...] = jnp.zeros_like(ddt_ref)

        beta = _sigmoid(ba_ref[:, :LANES])
        dba_ref[:, :LANES] = _mx(dbeta_ref[...] * beta * (1.0 - beta))
        pre = ba_ref[:, LANES:] + dt_ref[...]
        ea = jnp.exp(al_ref[...])
        dgv = dg_ref[...]
        da = dgv * (-ea) * _sigmoid(pre)
        dba_ref[:, LANES:] = _mx(da)
        ddt_ref[...] += jnp.sum(da, axis=0, keepdims=True)
        dal_ref[...] += jnp.sum(dgv * (-ea) * _softplus(pre), axis=0, keepdims=True)

    row = lambda w: pl.BlockSpec((tm, w), lambda i: (i, 0))
    one = _full((1, LANES))
    return pl.pallas_call(body, name=name, grid=(T // tm,), in_specs=[row(2 * LANES), one, one, row(LANES), row(LANES)],
                          out_specs=[row(2 * LANES), one, one],
                          out_shape=[jax.ShapeDtypeStruct((T, 2 * LANES), MXU_DTYPE), jax.ShapeDtypeStruct((1, LANES), F32),
                                     jax.ShapeDtypeStruct((1, LANES), F32)],
                          compiler_params=_cp("arbitrary"))(ba, a_log, dt_bias, dbeta, dg)


def _tri(c, strict):
    i = lax.broadcasted_iota(jnp.int32, (c, c), 0)
    j = lax.broadcasted_iota(jnp.int32, (c, c), 1)
    return (i > j) if strict else (i >= j)


def _inv_unit_lower(ls):
    c = ls[0].shape[0]
    i = lax.broadcasted_iota(jnp.int32, (c, c), 0)
    j = lax.broadcasted_iota(jnp.int32, (c, c), 1)
    eye = jnp.where(i == j, 1.0, 0.0)
    facs = [[eye - l for l in ls]]
    cur = ls
    for _ in range(int(math.log2(c)) - 1):
        cur = [_dot(p, p, NN, TRI_PREC) for p in cur]
        facs.append([eye + p for p in cur])
    while len(facs) > 1:
        nxt = [[_dot(a, b, NN, TRI_PREC) for a, b in zip(facs[t], facs[t + 1])] for t in range(0, len(facs) - 1, 2)]
        if len(facs) % 2:
            nxt.append(facs[-1])
        facs = nxt
    return facs[0]


def _chunk_gates(g_blk):
    c = g_blk.shape[0]
    gcs = _dot(jnp.where(_tri(c, False), 1.0, 0.0), g_blk, NN, HI)
    return gcs, gcs.T


def _head_chunk(h, qh, kh, vh, beta_blk, gcs, gcs_t):
    c = qh.shape[0]
    incl = _tri(c, False)
    gc_col = gcs[:, h:h + 1]
    gc_row = gcs_t[h:h + 1, :]
    gc_last = gcs_t[h:h + 1, c - 1:c]
    dec = jnp.where(incl, jnp.exp(jnp.where(incl, gc_col - gc_row, 0.0)), 0.0)
    gam = jnp.exp(gc_col)
    rr = jnp.exp(gc_last - gc_col)
    gl = jnp.exp(gc_last)
    b = beta_blk[:, h:h + 1]
    kb = kh * b
    vb = vh * b
    kk = _dot(_mx(kb), _mx(kh), NT)
    lmat = jnp.where(_tri(c, True), kk * dec, 0.0)
    qk = _dot(_mx(qh), _mx(kh), NT)
    pmat = jnp.where(incl, qk * dec, 0.0)
    return dict(dec=dec, gam=gam, rr=rr, gl=gl, b=b, kb=kb, vb=vb, lmat=lmat, pmat=pmat)


def _dn_scan_fwd(qkv, beta, g, proj, norm_g, name):
    T = qkv.shape[0]
    C, H, Dh = DN_CHUNK, DN_HEADS, DN_HEAD_DIM
    W = H * Dh
    N = T // C

    def body(q_ref, k_ref, v_ref, beta_ref, g_ref, z_ref, ng_ref, og_ref, o_ref, tinv_ref, s_ref, state):
        @pl.when(pl.program_id(0) == 0)
        def _():
            state[...] = jnp.zeros_like(state)

        gcs, gcs_t = _chunk_gates(g_ref[...])
        beta_blk = beta_ref[...]
        ng = ng_ref[...]
        heads = range(H)
        cs = [slice(h * Dh, (h + 1) * Dh) for h in heads]
        qs = [_head_chunk(h, q_ref[:, cs[h]], k_ref[:, cs[h]], v_ref[:, cs[h]], beta_blk, gcs, gcs_t) for h in heads]
        tinvs = _inv_unit_lower([q["lmat"] for q in qs])
        for h in heads:
            tinv_ref[h] = tinvs[h]
        us = [_dot(tinvs[h], qs[h]["vb"], NN, TRI_PREC) for h in heads]
        ws = [_dot(tinvs[h], qs[h]["kb"] * qs[h]["gam"], NN, TRI_PREC) for h in heads]
        ss = [state[h] for h in heads]
        for h in heads:
            s_ref[0, h] = ss[h]
        sbs = [_mx(s) for s in ss]
        vnbs = [_mx(us[h] - _dot(_mx(ws[h]), sbs[h], NN)) for h in heads]
        os_ = [_dot(_mx(q_ref[:, cs[h]] * qs[h]["gam"]), sbs[h], NN) + _dot(_mx(qs[h]["pmat"]), vnbs[h], NN) for h in heads]
        for h in heads:
            state[h] = ss[h] * qs[h]["gl"] + _dot(_mx((k_ref[:, cs[h]] * qs[h]["rr"]).T), vnbs[h], NN)
        for h in heads:
            o = os_[h]
            o_ref[:, cs[h]] = o
            zh = z_ref[:, cs[h]]
            og_ref[:, cs[h]] = _mx(o * _rms_stat(o) * ng * (zh * _sigmoid(zh)))

    blk = lambda j: pl.BlockSpec((C, W), lambda n: (n, j))
    small = pl.BlockSpec((C, LANES), lambda n: (n, 0))
    return pl.pallas_call(
        body, name=name, grid=(N,),
        in_specs=[blk(0), blk(1), blk(2), small, small, blk(3), _full((1, Dh))],
        out_specs=[blk(0), blk(0), pl.BlockSpec((H, C, C), lambda n: (0, n, 0)),
                   pl.BlockSpec((1, H, Dh, Dh), lambda n: (n, 0, 0, 0))],
        out_shape=[jax.ShapeDtypeStruct((T, W), MXU_DTYPE), jax.ShapeDtypeStruct((T, W), F32),
                   jax.ShapeDtypeStruct((H, T, C), F32), jax.ShapeDtypeStruct((N, H, Dh, Dh), F32)],
        scratch_shapes=[pltpu.VMEM((H, Dh, Dh), F32)],
        compiler_params=_cp("arbitrary"),
    )(qkv, qkv, qkv, beta, g, proj, norm_g)


def _dn_scan_bwd(qkv, beta, g, proj, norm_g, o, tinv, s_all, dog, name):
    T = qkv.shape[0]
    C, H, Dh = DN_CHUNK, DN_HEADS, DN_HEAD_DIM
    W = H * Dh
    N = T // C

    def body(q_ref, k_ref, v_ref, beta_ref, g_ref, z_ref, ng_ref, o_ref, tinv_ref, s_ref, dog_ref,
             dqkv_ref, dbeta_ref, dg_ref, dz_ref, dng_ref, dstate):
        @pl.when(pl.program_id(0) == 0)
        def _():
            dstate[...] = jnp.zeros_like(dstate)
            dng_ref[...] = jnp.zeros_like(dng_ref)

        gcs, gcs_t = _chunk_gates(g_ref[...])
        beta_blk = beta_ref[...]
        ng = ng_ref[...]
        incl = _tri(C, False)
        strict = _tri(C, True)
        lane = lax.broadcasted_iota(jnp.int32, (C, LANES), 1)
        rowi = lax.broadcasted_iota(jnp.int32, (C, 1), 0)
        ones = jnp.ones((C, LANES), F32)
        dbeta_acc = jnp.zeros((C, LANES), F32)
        dgc_acc = jnp.zeros((C, LANES), F32)
        dng_acc = jnp.zeros((1, Dh), F32)
        cs = [slice(h * Dh, (h + 1) * Dh) for h in range(H)]
        rsum = lambda t: jnp.sum(t, axis=1, keepdims=True)
        for heads in (range(0, H // 2), range(H // 2, H)):
            dobs = {}
            for h in heads:
                oh, zh, dogh = o_ref[:, cs[h]], z_ref[:, cs[h]], dog_ref[:, cs[h]]
                rstat = _rms_stat(oh)
                sz = _sigmoid(zh)
                dz_ref[:, cs[h]] = _mx(dogh * (oh * rstat * ng) * (sz * (1.0 + zh * (1.0 - sz))))
                do, dng = _rms_bwd(oh, rstat, ng, dogh * (zh * sz))
                dng_acc = dng_acc + dng
                dobs[h] = _mx(do)
            qs = {h: _head_chunk(h, q_ref[:, cs[h]], k_ref[:, cs[h]], v_ref[:, cs[h]], beta_blk, gcs, gcs_t) for h in heads}
            tms = {h: tinv_ref[h] for h in heads}
            us = {h: _dot(tms[h], qs[h]["vb"], NN, TRI_PREC) for h in heads}
            ws = {h: _dot(tms[h], qs[h]["kb"] * qs[h]["gam"], NN, TRI_PREC) for h in heads}
            ss = {h: s_ref[0, h] for h in heads}
            sbs = {h: _mx(ss[h]) for h in heads}
            wbs = {h: _mx(ws[h]) for h in heads}
            vnbs = {h: _mx(us[h] - _dot(wbs[h], sbs[h], NN)) for h in heads}
            dsns = {h: dstate[h] for h in heads}
            dsbs = {h: _mx(dsns[h]) for h in heads}
            dvnews = {h: _dot(_mx(qs[h]["pmat"]), dobs[h], TN) + _dot(_mx(k_ref[:, cs[h]] * qs[h]["rr"]), dsbs[h], NN)
                      for h in heads}
            dvb16s = {h: _mx(dvnews[h]) for h in heads}
            dps = {h: jnp.where(incl, _dot(dobs[h], vnbs[h], NT), 0.0) for h in heads}
            dqds = {h: _dot(dobs[h], sbs[h], NT) for h in heads}
            dkds = {h: _dot(vnbs[h], dsbs[h], NT) for h in heads}
            dgls = {h: jnp.sum(rsum(ss[h] * dsns[h]), axis=0, keepdims=True) for h in heads}
            dws = {h: -_dot(dvb16s[h], sbs[h], NT) for h in heads}
            for h in heads:
                dstate[h] = (_dot(_mx(q_ref[:, cs[h]] * qs[h]["gam"]), dobs[h], TN) + qs[h]["gl"] * dsns[h]
                             - _dot(wbs[h], dvb16s[h], TN))
            dvbs = {h: _dot(tms[h], dvnews[h], TN, TRI_PREC) for h in heads}
            dkbgs = {h: _dot(tms[h], dws[h], TN, TRI_PREC) for h in heads}
            dls = {h: jnp.where(strict, -(_dot(dvbs[h], us[h], NT, TRI_PREC) + _dot(dkbgs[h], ws[h], NT, TRI_PREC)), 0.0)
                   for h in heads}
            mmats = {h: dls[h] * qs[h]["lmat"] + dps[h] * qs[h]["pmat"] for h in heads}
            dgcs = {h: rsum(mmats[h]) - _dot(mmats[h], ones, TN, HI)[:, :1] for h in heads}
            dkk16s = {h: _mx(dls[h] * qs[h]["dec"]) for h in heads}
            dqk16s = {h: _mx(dps[h] * qs[h]["dec"]) for h in heads}
            for h in heads:
                q = qs[h]
                qh, kh, vh = q_ref[:, cs[h]], k_ref[:, cs[h]], v_ref[:, cs[h]]
                gam, rr, b, kb = q["gam"], q["rr"], q["b"], q["kb"]
                dkb = _dot(dkk16s[h], _mx(kh), NN) + dkbgs[h] * gam
                dk = _dot(dkk16s[h], _mx(kb), TN) + _dot(dqk16s[h], _mx(qh), TN) + dkb * b + dkds[h] * rr
                dq = _dot(dqk16s[h], _mx(kh), NN) + dqds[h] * gam
                dgam = rsum(dkbgs[h] * kb) + rsum(dqds[h] * qh)
                dr = rsum(dkds[h] * kh)
                dgc_last = jnp.sum(dr * rr, axis=0, keepdims=True) + dgls[h] * q["gl"]
                dgc = dgcs[h] + dgam * gam - dr * rr + jnp.where(rowi == C - 1, dgc_last, 0.0)
                dbeta = rsum(dvbs[h] * vh) + rsum(dkb * kh)
                dqkv_ref[:, cs[h]] = dq
                dqkv_ref[:, W + h * Dh:W + (h + 1) * Dh] = dk
                dqkv_ref[:, 2 * W + h * Dh:2 * W + (h + 1) * Dh] = dvbs[h] * b
                dbeta_acc = jnp.where(lane == h, dbeta, dbeta_acc)
                dgc_acc = jnp.where(lane == h, dgc, dgc_acc)
        dbeta_ref[...] = dbeta_acc
        dg_ref[...] = _dot(jnp.where(incl, 1.0, 0.0), dgc_acc, TN, HI)
        dng_ref[...] += dng_acc

    rev = lambda n: N - 1 - n
    blk = lambda j: pl.BlockSpec((C, W), lambda n: (rev(n), j))
    small = pl.BlockSpec((C, LANES), lambda n: (rev(n), 0))
    return pl.pallas_call(
        body, name=name, grid=(N,),
        in_specs=[blk(0), blk(1), blk(2), small, small, blk(3), _full((1, Dh)), blk(0),
                  pl.BlockSpec((H, C, C), lambda n: (0, rev(n), 0)),
                  pl.BlockSpec((1, H, Dh, Dh), lambda n: (rev(n), 0, 0, 0)), blk(0)],
        out_specs=[pl.BlockSpec((C, 3 * W), lambda n: (rev(n), 0)), small, small, blk(0), _full((1, Dh))],
        out_shape=[jax.ShapeDtypeStruct((T, 3 * W), F32), jax.ShapeDtypeStruct((T, LANES), F32),
                   jax.ShapeDtypeStruct((T, LANES), F32), jax.ShapeDtypeStruct((T, W), MXU_DTYPE),
                   jax.ShapeDtypeStruct((1, Dh), F32)],
        scratch_shapes=[pltpu.VMEM((H, Dh, Dh), F32)],
        compiler_params=_cp("arbitrary"),
    )(qkv, qkv, qkv, beta, g, proj, norm_g, o, tinv, s_all, dog)


_INV_SQRT2 = 0.7071067811865476
_INV_SQRT_2PI = 0.3989422804014327


def _sg_recompute(zp_ref, bin_ref, lng_ref, lnb_ref):
    E = SG_WIDTH
    zin = zp_ref[...] + bin_ref[...]
    cdf = 0.5 * (1.0 + lax.erf(zin * _INV_SQRT2))
    zz = zin * cdf
    u = zz[:, :E]
    vp = zz[:, E:]
    mu = jnp.mean(vp, axis=-1, keepdims=True)
    xc = vp - mu
    rstd = lax.rsqrt(jnp.mean(xc * xc, axis=-1, keepdims=True) + LN_EPS)
    xhat = xc * rstd
    v = xhat * lng_ref[...] + lnb_ref[...]
    return zin, cdf, u, xhat, rstd, v


def _sg_masked_ws(ws_ref, g):
    return _mx(jnp.where(_tri(SG_CHUNK, False), ws_ref[g], 0.0))


def _sg_fwd(zpre, b_in, ln_g, ln_b, w_s, b_s_t, name):
    T = zpre.shape[0]
    E, G, C, GW = SG_WIDTH, SG_GROUPS, SG_CHUNK, SG_GROUP_W

    def body(zp_ref, bin_ref, lng_ref, lnb_ref, ws_ref, bst_ref, um_ref):
        _, _, u, _, _, v = _sg_recompute(zp_ref, bin_ref, lng_ref, lnb_ref)
        bst = bst_ref[...]
        for g in range(G):
            cs = slice(g * GW, (g + 1) * GW)
            mixed = _dot(_sg_masked_ws(ws_ref, g), _mx(v[:, cs]), NN) + bst[:, g:g + 1]
            um_ref[:, cs] = _mx(u[:, cs] * mixed)

    return pl.pallas_call(
        body, name=name, grid=(T // C,),
        in_specs=[pl.BlockSpec((C, 2 * E), lambda n: (n, 0)), _full((1, 2 * E)), _full((1, E)), _full((1, E)),
                  _full((G, C, C)), _full((C, LANES))],
        out_specs=pl.BlockSpec((C, E), lambda n: (n, 0)),
        out_shape=jax.ShapeDtypeStruct((T, E), MXU_DTYPE), compiler_params=_cp("parallel"),
    )(zpre, b_in, ln_g, ln_b, w_s, b_s_t)


def _sg_bwd(zpre, b_in, ln_g, ln_b, w_s, b_s_t, dum, name):
    T = zpre.shape[0]
    E, G, C, GW = SG_WIDTH, SG_GROUPS, SG_CHUNK, SG_GROUP_W

    def body(zp_ref, bin_ref, lng_ref, lnb_ref, ws_ref, bst_ref, dum_ref,
             dz_ref, dbin_ref, dlng_ref, dlnb_ref, dws_ref, dbst_ref):
        @pl.when(pl.program_id(0) == 0)
        def _():
            for r in (dbin_ref, dlng_ref, dlnb_ref, dws_ref, dbst_ref):
                r[...] = jnp.zeros_like(r)

        zin, cdf, u, xhat, rstd, v = _sg_recompute(zp_ref, bin_ref, lng_ref, lnb_ref)
        bst = bst_ref[...]
        lane = lax.broadcasted_iota(jnp.int32, (C, LANES), 1)
        dum_v = dum_ref[...]
        dbst = jnp.zeros((C, LANES), F32)
        du_parts, dv_parts = [], []
        for g in range(G):
            cs = slice(g * GW, (g + 1) * GW)
            wsm = _sg_masked_ws(ws_ref, g)
            vg = _mx(v[:, cs])
            mixed = _dot(wsm, vg, NN) + bst[:, g:g + 1]
            dumg = dum_v[:, cs]
            du_parts.append(dumg * mixed)
            dmixed = dumg * u[:, cs]
            dmb = _mx(dmixed)
            dv_parts.append(_dot(wsm, dmb, TN))
            dws_ref[g] += _dot(dmb, vg, NT)
            dbst = jnp.where(lane == g, jnp.sum(dmixed, axis=1, keepdims=True), dbst)
        dbst_ref[...] += dbst
        du = jnp.concatenate(du_parts, axis=1)
        dv = jnp.concatenate(dv_parts, axis=1)
        dlng_ref[...] += jnp.sum(dv * xhat, axis=0, keepdims=True)
        dlnb_ref[...] += jnp.sum(dv, axis=0, keepdims=True)
        dxh = dv * lng_ref[...]
        dvp = rstd * (dxh - jnp.mean(dxh, axis=-1, keepdims=True) - xhat * jnp.mean(dxh * xhat, axis=-1, keepdims=True))
        dzz = jnp.concatenate([du, dvp], axis=1)
        dzin = dzz * (cdf + zin * (_INV_SQRT_2PI * jnp.exp(-0.5 * zin * zin)))
        dz_ref[...] = _mx(dzin)
        dbin_ref[...] += jnp.sum(dzin, axis=0, keepdims=True)

    return pl.pallas_call(
        body, name=name, grid=(T // C,),
        in_specs=[pl.BlockSpec((C, 2 * E), lambda n: (n, 0)), _full((1, 2 * E)), _full((1, E)), _full((1, E)),
                  _full((G, C, C)), _full((C, LANES)), pl.BlockSpec((C, E), lambda n: (n, 0))],
        out_specs=[pl.BlockSpec((C, 2 * E), lambda n: (n, 0)), _full((1, 2 * E)), _full((1, E)), _full((1, E)),
                   _full((G, C, C)), _full((C, LANES))],
        out_shape=[jax.ShapeDtypeStruct((T, 2 * E), MXU_DTYPE), jax.ShapeDtypeStruct((1, 2 * E), F32),
                   jax.ShapeDtypeStruct((1, E), F32), jax.ShapeDtypeStruct((1, E), F32),
                   jax.ShapeDtypeStruct((G, C, C), F32), jax.ShapeDtypeStruct((C, LANES), F32)],
        compiler_params=_cp("arbitrary"),
    )(zpre, b_in, ln_g, ln_b, w_s, b_s_t, dum)


def _row(v):
    return v.reshape(1, -1)


def _pad_lanes(v):
    v = v.reshape(1, -1)
    return jnp.pad(v, ((0, 0), (0, LANES - v.shape[1])))


def _local_step(x, target, p, weights_for, grads_ready=None, small_ready=None):
    ng = p["norm_g"]
    grads = {}
    dng = [[None] * 6 for _ in range(2)]
    order = [jnp.zeros((), F32)]

    def tell(group):
        zero = grads_ready(group, grads) if grads_ready is not None else None
        if zero is not None:
            order[0] = zero

    def gain(i, s):
        return _row(ng[i, s]) + order[0]

    def ffn_f(xin, i, j, tag):
        wt = weights_for("ffn" + tag, xin)
        xo, h, gu, y = _ffn_fwd(xin, _row(ng[i, 4 * j]), _row(ng[i, 4 * j + 1]), wt, "ffn_fwd_" + tag)
        return xo, (xin, h, gu, y, wt)

    x1, sv_f00 = ffn_f(x, 0, 0, "00")
    dnw = weights_for("dn", x1)
    hn0 = _norm_fwd(x1, _row(ng[0, 2]), "dn_prenorm")
    proj = _mm(hn0, dnw["dn_wqkvz"], "nn", "dn_proj")
    ba = _mm(hn0, dnw["dn_wba"], "nn", "dn_proj_ba")
    a_log = _pad_lanes(p["dn_a_log"])
    dt_bias = _pad_lanes(p["dn_dt_bias"])
    dn_ng = _row(p["dn_norm_g"])
    qkv = _dn_prep_fwd(proj, p["dn_conv_w"], "dn_prep_fwd")
    beta, gdec = _dn_gate_fwd(ba, a_log, dt_bias, "dn_gate_fwd")
    og, o_raw, tinv, s_all = _dn_scan_fwd(qkv, beta, gdec, proj, dn_ng, "dn_scan_fwd")
    m0 = _mm(og, dnw["dn_wout"], "nn", "dn_out")
    x2 = _postnorm_fwd(x1, m0, _row(ng[0, 3]), "dn_postnorm")
    x3, sv_f01 = ffn_f(x2, 0, 1, "01")
    x4, sv_f10 = ffn_f(x3, 1, 0, "10")
    sgw = weights_for("sg", x4)
    hn1 = _norm_fwd(x4, _row(ng[1, 2]), "sg_prenorm")
    zpre = _mm(hn1, sgw["sg_win"], "nn", "sg_proj")
    sg_bin = _row(p["sg_b_in"])
    sg_lng = _row(p["sg_ln_g"])
    sg_lnb = _row(p["sg_ln_b"])
    sg_bst = jnp.pad(p["sg_b_s"].T, ((0, 0), (0, LANES - SG_GROUPS)))
    um = _sg_fwd(zpre, sg_bin, sg_lng, sg_lnb, p["sg_w_s"], sg_bst, "sg_fwd")
    m1 = _mm(um, sgw["sg_wout"], "nn", "sg_out")
    x5 = _postnorm_fwd(x4, m1, _row(ng[1, 3]), "sg_postnorm")
    x6, sv_f11 = ffn_f(x5, 1, 1, "11")
    loss_part, dx = _loss_fwd_bwd(x6, target, "loss")

    def ffn_b(dxo, sv, i, j, tag, last=False):
        xin, h, gu, y, wt = sv
        dxi, dy, a, dgu, dg0, dg1 = _ffn_bwd(dxo, xin, y, gu, gain(i, 4 * j), gain(i, 4 * j + 1), wt, "ffn_bwd_" + tag)
        dng[i][4 * j] = dg0
        dng[i][4 * j + 1] = dg1
        after = None
        if last:
            grads["norm_g"] = jnp.stack([jnp.concatenate(dng[t], axis=0) for t in range(2)])
            after = small_ready(grads, loss_part) if small_ready is not None else None
        grads["wd" + tag] = _mm(a, dy, "tn", "ffn_wgrad_down_" + tag, after=after)
        grads["wguT" + tag] = _mm(dgu, h, "tn", "ffn_wgrad_up_" + tag, after=after)
        tell("ffn" + tag)
        return dxi

    dx = ffn_b(dx, sv_f11, 1, 1, "11")
    dm1, dng[1][3] = _postnorm_bwd(dx, m1, gain(1, 3), "sg_postnorm_bwd")
    grads["sg_w_out"] = _mm(um, dm1, "tn", "sg_wgrad_out")
    dum = _mm(dm1, sgw["sg_wout"], "nt", "sg_dgrad_out")
    dz1, dbin, dlng, dlnb, dws, dbst = _sg_bwd(zpre, sg_bin, sg_lng, sg_lnb, p["sg_w_s"], sg_bst, dum, "sg_bwd")
    grads["sg_w_inT"] = _mm(dz1, hn1, "tn", "sg_wgrad_in")
    tell("sg")
    dh1 = _mm(dz1, sgw["sg_win"], "nt", "sg_dgrad_in")
    dx, dng[1][2] = _prenorm_bwd(dx, dh1, x4, gain(1, 2), "sg_prenorm_bwd")
    grads["sg_b_in"] = dbin.reshape(1, -1)
    grads["sg_ln_g"] = dlng.reshape(1, -1)
    grads["sg_ln_b"] = dlnb.reshape(1, -1)
    grads["sg_w_s"] = jnp.where(jnp.tril(jnp.ones((SG_CHUNK, SG_CHUNK), bool)), dws, 0.0)[None]
    grads["sg_b_s"] = dbst[:, :SG_GROUPS].T[None]
    dx = ffn_b(dx, sv_f10, 1, 0, "10")
    dx = ffn_b(dx, sv_f01, 0, 1, "01")
    dm0, dng[0][3] = _postnorm_bwd(dx, m0, gain(0, 3), "dn_postnorm_bwd")
    grads["dn_w_out"] = _mm(og, dm0, "tn", "dn_wgrad_out")
    dog = _mm(dm0, dnw["dn_wout"], "nt", "dn_dgrad_out")
    dqkv, dbeta, dgdec, dz0, dnng = _dn_scan_bwd(qkv, beta, gdec, proj, dn_ng, o_raw, tinv, s_all, dog, "dn_scan_bwd")
    dqkv_pre, dconv = _dn_prep_bwd(proj, p["dn_conv_w"], dqkv, "dn_prep_bwd")
    dba, dal, ddt = _dn_gate_bwd(ba, a_log, dt_bias, dbeta, dgdec, "dn_gate_bwd")
    W3 = 3 * DN_HEADS * DN_HEAD_DIM
    dw_qkv = _mm(hn0, dqkv_pre, "tn", "dn_wgrad_qkv")
    dw_z = _mm(hn0, dz0, "tn", "dn_wgrad_z")
    dw_ba = _mm(hn0, dba, "tn", "dn_wgrad_ba")
    grads["dn_w_in"] = jnp.concatenate(
        [dw_qkv, dw_z, dw_ba[:, :DN_HEADS], dw_ba[:, LANES:LANES + DN_HEADS]], axis=1)
    tell("dn")
    dh0 = _mm(dqkv_pre, dnw["dn_wqkvz"][:, :W3], "nt", "dn_dgrad_qkv")
    dh0 = _mm(dz0, dnw["dn_wqkvz"][:, W3:], "nt", "dn_dgrad_z", add=dh0)
    dh0 = _mm(dba, dnw["dn_wba"], "nt", "dn_dgrad_ba", add=dh0)
    dx, dng[0][2] = _prenorm_bwd(dx, dh0, x1, gain(0, 2), "dn_prenorm_bwd")
    grads["dn_conv_w"] = dconv[None]
    grads["dn_a_log"] = dal[:, :DN_HEADS]
    grads["dn_dt_bias"] = ddt[:, :DN_HEADS]
    grads["dn_norm_g"] = dnng
    dx = ffn_b(dx, sv_f00, 0, 0, "00", last=True)
    return loss_part, dx, grads


def _mesh_pos():
    return lax.axis_index("x"), lax.axis_index("y"), lax.axis_index("c")


def _other_chips(x, y):
    return [(1 - x, y), (x, 1 - y), (1 - x, 1 - y)]


def _allgather_chips(arrs, name):
    n = len(arrs)

    def body(*refs):
        ins, outs = refs[:n], refs[n:2 * n]
        ici_send, ici_recv, d2d_send, d2d_recv = refs[2 * n:]
        x, y, c = _mesh_pos()
        me = 2 * x + y
        chips = _other_chips(x, y)
        sibling = (x, y, 1 - c)

        def ici(i, j, k):
            cx, cy = chips[j]
            return pltpu.make_async_remote_copy(src_ref=ins[i].at[c], dst_ref=outs[i].at[k, c], send_sem=ici_send.at[3 * i + j],
                                                recv_sem=ici_recv.at[3 * i + j], device_id=(cx, cy, c), device_id_type=MESH)

        def d2d(i, j, h):
            cx, cy = chips[j]
            slot = outs[i].at[2 * cx + cy, h]
            return pltpu.make_async_remote_copy(src_ref=slot, dst_ref=slot, send_sem=d2d_send.at[3 * i + j],
                                                recv_sem=d2d_recv.at[3 * i + j], device_id=sibling, device_id_type=MESH)

        sends = [ici(i, j, me) for i in range(n) for j in range(3)]
        for cp in sends:
            cp.start()
        for i in range(n):
            for j, (cx, cy) in enumerate(chips):
                ici(i, j, 2 * cx + cy).wait_recv()
                fwd = d2d(i, j, c)
                fwd.start()
                sends.append(fwd)
        for i in range(n):
            for j in range(3):
                d2d(i, j, 1 - c).wait_recv()
        for cp in sends:
            cp.wait_send()

    return pl.pallas_call(
        body, name=name, in_specs=[ANY] * n, out_specs=[ANY] * n,
        out_shape=[jax.ShapeDtypeStruct((N_CHIPS,) + a.shape, a.dtype) for a in arrs],
        scratch_shapes=[pltpu.SemaphoreType.DMA((3 * n,))] * 4,
    )(*arrs)


HBM = pl.BlockSpec(memory_space=pltpu.HBM)
SEM = pl.BlockSpec(memory_space=pltpu.SEMAPHORE)
TOKEN = jax.ShapeDtypeStruct((SUBLANES, LANES), F32)


_PEERS = {"gather": 3, "scatter": 3, "swap": 1, "all": N_DEV - 1}


def _land_shape(kind, shape):
    if kind == "gather":
        return (N_CHIPS,) + shape
    if kind == "all":
        return (N_DEV,) + shape
    return (N_CHIPS,) + shape[2:] if kind == "swap" else shape


def _peer_copies(kind, flags, src_refs, land_refs, send_sems, recv_sems, receiving):
    x, y, c = _mesh_pos()
    me4, me8 = 2 * x + y, 4 * x + 2 * y + c
    np_ = _PEERS[kind]
    cps = []
    for i, (src, land) in enumerate(zip(src_refs, land_refs)):
        if kind == "swap":
            half = src.at[1 - c] if flags[i] else src.at[:, 1 - c]
            plan = [((x, y, 1 - c), half, land)]
        elif kind == "all":
            masks = [(mx, my, mc) for mx in (0, 1) for my in (0, 1) for mc in (0, 1)][1:]
            peers = [(jnp.where(mx, 1 - x, x), jnp.where(my, 1 - y, y), jnp.where(mc, 1 - c, c)) for mx, my, mc in masks]
            plan = [(p, src, land.at[4 * p[0] + 2 * p[1] + p[2] if receiving else me8]) for p in peers]
        else:
            plan = []
            for cx, cy in _other_chips(x, y):
                k = 2 * cx + cy
                s = src.at[me4 if receiving else k] if kind == "scatter" else src
                plan.append(((cx, cy, c), s, land.at[k if receiving else me4]))
        for j, (peer, s, d) in enumerate(plan):
            cps.append(pltpu.make_async_remote_copy(src_ref=s, dst_ref=d, send_sem=send_sems.at[np_ * i + j],
                                                    recv_sem=recv_sems.at[np_ * i + j], device_id=peer, device_id_type=MESH))
    return cps


def _copies_start(kind, srcs, after, name, flags=None):
    n = len(srcs)
    ns = _PEERS[kind] * n
    lands = [lax.empty(_land_shape(kind, s.shape), s.dtype) for s in srcs]
    after = [] if after is None else [after]

    def body(*refs):
        src_refs, land_refs = refs[:n], refs[n:2 * n]
        send_sems, recv_sems = refs[2 * n + len(after)], refs[2 * n + len(after) + 1]
        token = refs[-1]
        for cp in _peer_copies(kind, flags, src_refs, land_refs, send_sems, recv_sems, False):
            cp.start()
        token[...] = jnp.zeros_like(token)

    outs = pl.pallas_call(
        body, name=name,
        in_specs=[HBM] * (2 * n) + [ANY] * len(after),
        out_specs=(SEM, SEM) + (HBM,) * (2 * n) + (pl.BlockSpec(memory_space=pltpu.VMEM),),
        out_shape=(pltpu.SemaphoreType.DMA((ns,)), pltpu.SemaphoreType.DMA((ns,)))
        + tuple(pltpu.HBM(a.shape, a.dtype) for a in list(srcs) + lands) + (TOKEN,),
        input_output_aliases={i: 2 + i for i in range(2 * n)},
        compiler_params=pltpu.CompilerParams(has_side_effects=pltpu.SideEffectType.DATAFLOW_SIDE_EFFECTING),
    )(*[pltpu.with_memory_space_constraint(a, pltpu.HBM) for a in list(srcs) + lands], *after)
    return dict(sems=outs[:2], srcs=outs[2:2 + n], lands=outs[2 + n:2 + 2 * n], token=outs[-1], kind=kind, flags=flags)


def _copies_wait(started, after, name):
    n = len(started["srcs"])
    kind, flags = started["kind"], started["flags"]
    after = list(after) if isinstance(after, (list, tuple)) else [after]

    def body(*refs):
        src_refs, land_refs = refs[:n], refs[n:2 * n]
        send_sems, recv_sems = refs[2 * n], refs[2 * n + 1]
        for cp in _peer_copies(kind, flags, src_refs, land_refs, send_sems, recv_sems, True):
            cp.wait_send()
            cp.wait_recv()

    outs = pl.pallas_call(
        body, name=name,
        in_specs=[HBM] * (2 * n) + [SEM, SEM] + [ANY] * len(after),
        out_specs=(HBM,) * (2 * n),
        out_shape=tuple(pltpu.HBM(a.shape, a.dtype) for a in list(started["srcs"]) + list(started["lands"])),
        input_output_aliases={i: i for i in range(2 * n)},
        compiler_params=pltpu.CompilerParams(has_side_effects=pltpu.SideEffectType.DATAFLOW_SIDE_EFFECTING),
    )(*started["srcs"], *started["lands"], *started["sems"], *after)
    return outs[:n], outs[n:]


def _swap_whole(arrs, name):
    n = len(arrs)

    def body(*refs):
        ins, outs = refs[:n], refs[n:2 * n]
        send_sems, recv_sems = refs[2 * n:]
        x, y, c = _mesh_pos()
        cps = [pltpu.make_async_remote_copy(src_ref=ins[i], dst_ref=outs[i], send_sem=send_sems.at[i],
                                            recv_sem=recv_sems.at[i], device_id=(x, y, 1 - c), device_id_type=MESH)
               for i in range(n)]
        for cp in cps:
            cp.start()
        for cp in cps:
            cp.wait()

    return pl.pallas_call(
        body, name=name, in_specs=[ANY] * n, out_specs=[ANY] * n,
        out_shape=[jax.ShapeDtypeStruct(a.shape, a.dtype) for a in arrs],
        scratch_shapes=[pltpu.SemaphoreType.DMA((n,)), pltpu.SemaphoreType.DMA((n,))],
    )(*arrs)


def _as_rows(a, lead):
    shp = a.shape
    rows = 1
    for s in shp[lead:-1]:
        rows *= s
    return a.reshape(shp[:lead] + (rows, shp[-1]))


def _row_tile(rows, cols, n_bufs):
    budget = (24 * 1024 * 1024) // (n_bufs * 2 * 4 * cols)
    return _pick(rows, max(2 * SUBLANES, budget), 2 * SUBLANES)


def _sum_devices(own, got, dev, name):
    n, rows, cols = got.shape
    tr = _row_tile(rows, cols, n + 2)

    def body(dev_ref, own_ref, got_ref, o_ref):
        mine = own_ref[...]
        acc = jnp.where(dev_ref[0] == 0, mine, got_ref[0])
        for k in range(1, n):
            acc = acc + jnp.where(dev_ref[0] == k, mine, got_ref[k])
        o_ref[...] = acc

    return pl.pallas_call(
        body, name=name,
        grid_spec=pltpu.PrefetchScalarGridSpec(
            num_scalar_prefetch=1, grid=(rows // tr,),
            in_specs=[pl.BlockSpec((tr, cols), lambda i, d: (i, 0)), pl.BlockSpec((n, tr, cols), lambda i, d: (0, i, 0))],
            out_specs=pl.BlockSpec((tr, cols), lambda i, d: (i, 0))),
        out_shape=jax.ShapeDtypeStruct((rows, cols), F32), compiler_params=_cp("parallel"),
    )(_scalar(dev), own, got)


def _scalar(i):
    return jnp.reshape(i, (1,)).astype(jnp.int32)


def _add_own_half(g, other, c, half_first, name):
    _, rows, cols = other.shape
    tr = _row_tile(rows, cols, 3)

    def body(c_ref, g_ref, o_ref, out_ref):
        out_ref[0] = (g_ref[0, 0] + o_ref[0]).astype(out_ref.dtype)

    if half_first:
        g_map = lambda k, i, c_ref: (c_ref[0], k, i, 0)
    else:
        g_map = lambda k, i, c_ref: (k, c_ref[0], i, 0)
    flat = pl.BlockSpec((1, tr, cols), lambda k, i, c_ref: (k, i, 0))
    return pl.pallas_call(
        body, name=name,
        grid_spec=pltpu.PrefetchScalarGridSpec(
            num_scalar_prefetch=1, grid=(N_CHIPS, rows // tr),
            in_specs=[pl.BlockSpec((1, 1, tr, cols), g_map), flat], out_specs=flat),
        out_shape=jax.ShapeDtypeStruct(other.shape, COMM_DTYPE), compiler_params=_cp("parallel", "parallel"),
    )(_scalar(c), g, other)


def _sum_chips(own, got, chip, name, transpose=False):
    _, rows, cols = own.shape
    tr = rows if transpose else _row_tile(rows, cols, N_CHIPS + 2)

    def body(chip_ref, p_ref, b_ref, o_ref):
        mine = p_ref[0].astype(F32)
        acc = jnp.where(chip_ref[0] == 0, mine, b_ref[0].astype(F32))
        for k in range(1, N_CHIPS):
            acc = acc + jnp.where(chip_ref[0] == k, mine, b_ref[k].astype(F32))
        o_ref[...] = acc.T if transpose else acc

    if transpose:
        out_spec, out_shape = pl.BlockSpec((cols, rows), lambda i, k_ref: (0, 0)), (cols, rows)
    else:
        out_spec, out_shape = pl.BlockSpec((tr, cols), lambda i, k_ref: (i, 0)), (rows, cols)
    return pl.pallas_call(
        body, name=name,
        grid_spec=pltpu.PrefetchScalarGridSpec(
            num_scalar_prefetch=1, grid=(rows // tr,),
            in_specs=[pl.BlockSpec((1, tr, cols), lambda i, k_ref: (k_ref[0], i, 0)),
                      pl.BlockSpec((N_CHIPS, tr, cols), lambda i, k_ref: (0, i, 0))],
            out_specs=out_spec),
        out_shape=jax.ShapeDtypeStruct(out_shape, F32), compiler_params=_cp("parallel"),
    )(_scalar(chip), own, got)


def _adam_math(w, g, m, v):
    nm = ADAM_B1 * m + (1.0 - ADAM_B1) * g
    nv = ADAM_B2 * v + (1.0 - ADAM_B2) * (g * g)
    m_hat = nm / (1.0 - ADAM_B1 ** ADAM_STEP)
    v_hat = nv / (1.0 - ADAM_B2 ** ADAM_STEP)
    return -ADAM_LR * (m_hat / (jnp.sqrt(v_hat) + ADAM_EPS) + ADAM_WD * w), nm, nv


def _adamw_halves(w, mine, theirs, m, v, c, name):
    shape = w.shape
    ws, ms, vs = (_as_rows(t.reshape((2, -1) + t.shape[-1:]), 1) for t in (w, m, v))
    a, b = _as_rows(mine, 0), _as_rows(theirs, 0)
    rows, cols = a.shape
    tr = _row_tile(rows, cols, 9)

    def body(c_ref, w_ref, a_ref, b_ref, m_ref, v_ref, g_ref, d_ref, nm_ref, nv_ref):
        gv = jnp.where(pl.program_id(0) == c_ref[0], a_ref[...], b_ref[...])
        g_ref[0] = gv
        d_ref[0], nm_ref[0], nv_ref[0] = _adam_math(w_ref[0], gv, m_ref[0], v_ref[0])

    half = pl.BlockSpec((1, tr, cols), lambda h, i, c_ref: (h, i, 0))
    flat = pl.BlockSpec((tr, cols), lambda h, i, c_ref: (i, 0))
    outs = pl.pallas_call(
        body, name=name,
        grid_spec=pltpu.PrefetchScalarGridSpec(num_scalar_prefetch=1, grid=(2, rows // tr),
                                               in_specs=[half, flat, flat, half, half], out_specs=[half] * 4),
        out_shape=[jax.ShapeDtypeStruct((2, rows, cols), F32)] * 4, compiler_params=_cp("parallel", "parallel"),
    )(_scalar(c), ws, a, b, ms, vs)
    return tuple(o.reshape(shape) for o in outs)


def _adamw(w, g, m, v, name):
    shape = w.shape
    ws, gs, ms, vs = (_as_rows(t, 0) for t in (w, g, m, v))
    rows, cols = ws.shape
    tr = _row_tile(rows, cols, 7)

    def body(w_ref, g_ref, m_ref, v_ref, d_ref, nm_ref, nv_ref):
        d_ref[...], nm_ref[...], nv_ref[...] = _adam_math(w_ref[...], g_ref[...], m_ref[...], v_ref[...])

    spec = pl.BlockSpec((tr, cols), lambda i: (i, 0))
    outs = pl.pallas_call(body, name=name, grid=(rows // tr,), in_specs=[spec] * 4, out_specs=[spec] * 3,
                          out_shape=[jax.ShapeDtypeStruct((rows, cols), F32)] * 3, compiler_params=_cp("parallel"))(ws, gs, ms, vs)
    return tuple(o.reshape(shape) for o in outs)


_BIG = ["ffn_w_gate", "ffn_w_up", "ffn_w_down", "dn_w_in", "dn_w_out", "sg_w_in", "sg_w_out"]
_SMALL_SHARDED = ["norm_g", "dn_conv_w", "sg_b_in", "sg_ln_g", "sg_ln_b"]
_SMALL_REPL = ["dn_a_log", "dn_dt_bias", "dn_norm_g", "sg_w_s", "sg_b_s"]
_WEIGHTS = ["norm_g", "ffn_w_gate", "ffn_w_up", "ffn_w_down", "dn_w_in", "dn_conv_w", "dn_a_log", "dn_dt_bias",
            "dn_norm_g", "dn_w_out", "sg_w_in", "sg_b_in", "sg_ln_g", "sg_ln_b", "sg_w_s", "sg_b_s", "sg_w_out"]
PACK_COLS = 1024


def _pack(arrs):
    flat = jnp.concatenate([a.reshape(-1) for a in arrs])
    pad = (-flat.shape[0]) % (SUBLANES * PACK_COLS)
    return jnp.pad(flat, (0, pad)).reshape(-1, PACK_COLS)


def _unpack(buf, shapes):
    flat = buf.reshape(-1)
    out, off = [], 0
    for s in shapes:
        n = math.prod(s)
        out.append(flat[off:off + n].reshape(s))
        off += n
    return out


def _as_halves(a):
    if a.shape[0] == 2:
        return a
    if a.shape[0] == 1:
        return a.reshape((2, a.shape[1] // 2) + a.shape[2:])
    return a.reshape((2, a.shape[0] // 2) + a.shape[1:])


def _with_own(gathered, own, chip):
    g = gathered.reshape((N_CHIPS,) + own.shape)
    return [jnp.where(chip == k, own, g[k]) for k in range(N_CHIPS)]


def _cat_shards(g, axis):
    return jnp.concatenate(list(g), axis=axis)


_GROUP_ORDER = ["ffn00", "dn", "ffn01", "ffn10", "sg", "ffn11"]


def _weight_groups(w):
    cast = {k: _mx(w[k]) for k in _BIG}
    groups = {"ffn%d%d" % (i, j): [cast["ffn_w_gate"][i, j].T, cast["ffn_w_up"][i, j].T, cast["ffn_w_down"][i, j]]
              for i, j in [(0, 0), (0, 1), (1, 0), (1, 1)]}
    groups["dn"] = [cast["dn_w_in"][0], cast["dn_w_out"][0]]
    groups["sg"] = [cast["sg_w_in"][0], cast["sg_w_out"][0]]
    return groups


def _ffn_weights(chip, own, gathered):
    pairs = [(a, g.reshape((N_CHIPS,) + a.shape)) for a, g in zip(own, gathered)]
    return {"chip": chip, "gate": pairs[0], "up": pairs[1], "down": pairs[2]}


def _group_matrices(group, shards):
    if group == "sg":
        return {"sg_win": _cat_shards(shards[0], 1), "sg_wout": _cat_shards(shards[1], 0)}
    dn_full = _cat_shards(shards[0], 1)
    W4 = 4 * DN_HEADS * DN_HEAD_DIM
    wba = jnp.zeros((D_MODEL, 2 * LANES), dn_full.dtype)
    wba = wba.at[:, :DN_HEADS].set(dn_full[:, W4:W4 + DN_HEADS])
    wba = wba.at[:, LANES:LANES + DN_HEADS].set(dn_full[:, W4 + DN_HEADS:])
    return {"dn_wqkvz": dn_full[:, :W4], "dn_wba": wba, "dn_wout": _cat_shards(shards[1], 0)}


def _split_cols(a, n):
    w = a.shape[-1] // n
    return [a[..., k * w:(k + 1) * w] for k in range(n)]


def _split_rows(a, n):
    h = a.shape[-2] // n
    return [a[..., k * h:(k + 1) * h, :] for k in range(n)]


_IJ = [(0, 0), (0, 1), (1, 0), (1, 1)]


_REDUCED = ["wguT%d%d" % ij for ij in _IJ] + ["wd%d%d" % ij for ij in _IJ] + ["dn_w_in", "dn_w_out", "sg_w_inT", "sg_w_out"]


def _group_grads(group, grads):
    def rows_by_chip(a):
        return a.reshape(N_CHIPS, 2, a.shape[0] // (2 * N_CHIPS), a.shape[1])

    if group.startswith("ffn"):
        tag = group[3:]
        t = grads["wguT" + tag]
        return (["wguT" + tag, "wd" + tag],
                [t.reshape(2, N_CHIPS, t.shape[0] // (2 * N_CHIPS), t.shape[1]), rows_by_chip(grads["wd" + tag])], [True, False])
    if group == "sg":
        return ["sg_w_inT", "sg_w_out"], [rows_by_chip(grads["sg_w_inT"]), rows_by_chip(grads["sg_w_out"])], [False, False]
    dn_in = jnp.stack([jnp.stack(_split_cols(hf, N_CHIPS)) for hf in _split_rows(grads["dn_w_in"], 2)])
    return ["dn_w_in", "dn_w_out"], [dn_in, rows_by_chip(grads["dn_w_out"])], [True, False]


def _shard_grads(mine, theirs, c, w, keys):
    lo = lambda n: jnp.where(c == 0, mine[n], theirs[n])
    hi = lambda n: jnp.where(c == 0, theirs[n], mine[n])
    rows = lambda n: jnp.concatenate([lo(n), hi(n)], axis=0)
    sq = lambda parts: jnp.stack(parts).reshape(2, 2, *parts[0].shape)
    tags = ["%d%d" % ij for ij in _IJ]
    make = {
        "ffn_w_gate": lambda: sq([lo("wguT" + t) for t in tags]),
        "ffn_w_up": lambda: sq([hi("wguT" + t) for t in tags]),
        "ffn_w_down": lambda: sq([rows("wd" + t) for t in tags]),
        "dn_w_in": lambda: rows("dn_w_in"),
        "dn_w_out": lambda: rows("dn_w_out"),
        "sg_w_in": lambda: jnp.concatenate([lo("sg_w_inT"), hi("sg_w_inT")], axis=1),
        "sg_w_out": lambda: rows("sg_w_out"),
    }
    g = {k: make[k]() for k in keys}
    return {k: v.reshape(w[k].shape) for k, v in g.items()}


def kernel(x, norm_g, ffn_w_gate, ffn_w_up, ffn_w_down, dn_w_in, dn_conv_w, dn_a_log, dn_dt_bias, dn_norm_g, dn_w_out, sg_w_in, sg_b_in, sg_ln_g, sg_ln_b, sg_w_s, sg_b_s, sg_w_out, loss_target, m_norm_g, m_ffn_w_gate, m_ffn_w_up, m_ffn_w_down, m_dn_w_in, m_dn_conv_w, m_dn_a_log, m_dn_dt_bias, m_dn_norm_g, m_dn_w_out, m_sg_w_in, m_sg_b_in, m_sg_ln_g, m_sg_ln_b, m_sg_w_s, m_sg_b_s, m_sg_w_out, v_norm_g, v_ffn_w_gate, v_ffn_w_up, v_ffn_w_down, v_dn_w_in, v_dn_conv_w, v_dn_a_log, v_dn_dt_bias, v_dn_norm_g, v_dn_w_out, v_sg_w_in, v_sg_b_in, v_sg_ln_g, v_sg_ln_b, v_sg_w_s, v_sg_b_s, v_sg_w_out):
    args = dict(locals())
    w = {k: args[k] for k in _WEIGHTS}
    mom = {k: args["m_" + k] for k in _WEIGHTS}
    var = {k: args["v_" + k] for k in _WEIGHTS}
    cx, cy, cc = _mesh_pos()
    chip = 2 * cx + cy

    small_shapes = [w[k].shape for k in _SMALL_SHARDED]
    groups = _weight_groups(w)
    own = groups[_GROUP_ORDER[0]] + [_pack([w[k] for k in _SMALL_SHARDED])]
    first = _allgather_chips([_as_halves(a) for a in own], "gather_first")
    started, after = {}, first[0]
    for g in _GROUP_ORDER[1:]:
        started[g] = _copies_start("gather", groups[g], after, "gather_start_" + g)
        after = started[g]["token"]
    small_k = [_unpack(pack, small_shapes) for pack in _with_own(first[-1], own[-1], chip)]
    p = {name: jnp.concatenate([small_k[k][i] for k in range(N_CHIPS)], axis=-1) for i, name in enumerate(_SMALL_SHARDED)}
    p = {k: (v if k == "norm_g" else v[0]) for k, v in p.items()}
    p["norm_g"] = p["norm_g"] + after[0, 0]
    for k in _SMALL_REPL:
        p[k] = w[k][0]

    def weights_for(group, after):
        if group == _GROUP_ORDER[0]:
            return _ffn_weights(chip, own[:-1], first[:-1])
        srcs, lands = _copies_wait(started[group], after, "gather_wait_" + group)
        if group.startswith("ffn"):
            return _ffn_weights(chip, srcs, lands)
        return _group_matrices(group, [_with_own(l, a, chip) for l, a in zip(lands, srcs)])

    mine, theirs, to_core, to_chips = {}, {}, [], []

    def send_to_chips(after):
        group, names, flags, swap = to_core.pop(0)
        halves, got = _copies_wait(swap, after, "swap_wait_" + group)
        pair_sum = [_add_own_half(h, o, cc, hf, "pair_sum_" + n) for n, h, o, hf in zip(names, halves, got, flags)]
        scatter = _copies_start("scatter", pair_sum, got[0], "reduce_start_" + group)
        to_chips.append((group, names, scatter))
        return scatter["token"]

    def finish(after):
        group, names, scatter = to_chips.pop(0)
        pair_sum, got = _copies_wait(scatter, after, "reduce_wait_" + group)
        half_sum = [_sum_chips(a, b, chip, "chip_sum_" + n, transpose="T" in n)
                    for n, a, b in zip(names, pair_sum, got)]
        other = _swap_whole(half_sum, "gather_core_pair_" + group)
        mine.update(zip(names, half_sum))
        theirs.update(zip(names, other))

    def grads_ready(group, grads):
        names, halves, flags = _group_grads(group, grads)
        swap = _copies_start("swap", halves, None, "swap_start_" + group, flags)
        token = swap["token"]
        if to_core:
            token = send_to_chips(token)
            if len(to_chips) > 1:
                finish(token)
        to_core.append((group, names, flags, swap))
        return token[0, 0]

    small_names = _SMALL_SHARDED + _SMALL_REPL
    small = {}

    def small_ready(grads, loss_part):
        parts = [grads[k] for k in small_names]
        small["shapes"] = [g.shape for g in parts] + [(1,)]
        pack = _pack(parts + [loss_part[0, :1]])
        small["exchange"] = _copies_start("all", [pack], None, "small_start")
        return small["exchange"]["token"]

    loss_part, grad_x, grads = _local_step(x[0], loss_target[0], p, weights_for, grads_ready, small_ready)
    token = send_to_chips(to_core[0][3]["token"])
    finish(token)
    (pack,), (packs,) = _copies_wait(small["exchange"], list(theirs.values()), "small_wait")
    summed = _sum_devices(pack, packs, 4 * cx + 2 * cy + cc, "small_sum")
    parts = _unpack(summed, small["shapes"])
    loss = parts[-1][0]
    grad = {}
    for i, k in enumerate(small_names):
        g = parts[i]
        if k in _SMALL_SHARDED:
            n = w[k].shape[-1]
            g = lax.dynamic_slice_in_dim(g, chip * n, n, axis=g.ndim - 1)
        grad[k] = g

    delta, new_m, new_v = {}, {}, {}

    def update(keys):
        grad.update(_shard_grads(mine, theirs, cc, w, keys))
        for k in keys:
            delta[k], new_m[k], new_v[k] = _adamw(w[k], grad[k], mom[k], var[k], "adamw_" + k)

    shapes = [w[k].shape for k in small_names]
    d, nm, nv = _adamw(_pack([w[k] for k in small_names]), _pack([grad[k] for k in small_names]),
                       _pack([mom[k] for k in small_names]), _pack([var[k] for k in small_names]), "adamw_small")
    for k, a, b, c_ in zip(small_names, _unpack(d, shapes), _unpack(nm, shapes), _unpack(nv, shapes)):
        delta[k], new_m[k], new_v[k] = a, b, c_
    mixers = [k for k in _BIG if not k.startswith("ffn")]
    update(mixers)
    finish([d] + [delta[k] for k in mixers] + list(theirs.values()))
    update([k for k in _BIG if k.startswith("ffn")])

    return (loss, grad_x[None], *[grad[k] for k in _WEIGHTS], *[delta[k] for k in _WEIGHTS],
            *[new_m[k] for k in _WEIGHTS], *[new_v[k] for k in _WEIGHTS])
```

```python
import functools
import math

import jax
import jax.numpy as jnp
from jax import lax
from jax.experimental import pallas as pl
from jax.experimental.pallas import tpu as pltpu

F32 = jnp.float32
MXU_DTYPE = jnp.bfloat16
COMM_DTYPE = jnp.bfloat16
HI = lax.Precision.HIGHEST
TRI_PREC = lax.Precision.HIGH

D_MODEL = 1024
D_FF = 2816
RMS_EPS = 1e-6
LN_EPS = 1e-5
L2_EPS = 1e-6
DN_HEADS = 8
DN_HEAD_DIM = 128
DN_CONV = 4
DN_CHUNK = 64
SG_WIDTH = 2048
SG_GROUPS = 8
SG_CHUNK = 128
SG_GROUP_W = SG_WIDTH // SG_GROUPS
N_CHIPS = 4
N_DEV = 8
LANES = 128
SUBLANES = 8
VMEM_LIMIT = 56 * 1024 * 1024

ADAM_LR = 0.001
ADAM_B1 = 0.9
ADAM_B2 = 0.999
ADAM_EPS = 1e-08
ADAM_WD = 0.01
ADAM_STEP = 10

MESH = pl.DeviceIdType.MESH
ANY = pl.BlockSpec(memory_space=pl.ANY)


def _cp(*sem):
    return pltpu.CompilerParams(dimension_semantics=sem, vmem_limit_bytes=VMEM_LIMIT)


def _pick(n, pref, mult=LANES):
    best = None
    d = mult
    while d <= min(n, pref):
        if n % d == 0:
            best = d
        d += mult
    return best if best is not None else n


def _full(shape):
    nd = len(shape)
    return pl.BlockSpec(shape, lambda *_: (0,) * nd)


def _sigmoid(x):
    return 1.0 / (1.0 + jnp.exp(-x))


def _dot(a, b, dims, prec=None):
    return lax.dot_general(a, b, (dims, ((), ())), preferred_element_type=F32, precision=prec)


NN = ((1,), (0,))
NT = ((1,), (1,))
TN = ((0,), (0,))


def _mx(a):
    return a.astype(MXU_DTYPE)


def _rms_stat(x):
    return lax.rsqrt(jnp.mean(x * x, axis=-1, keepdims=True) + RMS_EPS)


def _rms_bwd(x, r, g, dy):
    xh = x * r
    dxh = dy * g
    dx = r * (dxh - xh * jnp.mean(dxh * xh, axis=-1, keepdims=True))
    return dx, jnp.sum(dy * xh, axis=0, keepdims=True)


def _mm(a, b, mode, name, out_dtype=F32, add=None, after=None):
    if mode == "tn":
        K, M = a.shape
        N = b.shape[1]
    elif mode == "nt":
        M, K = a.shape
        N = b.shape[0]
    else:
        M, K = a.shape
        N = b.shape[1]
    tn = _pick(N, 1024)
    if mode == "tn":
        tm = _pick(M, 1024 if tn <= 512 else 1408)
        tk = _pick(K, 1024, SUBLANES)
    else:
        tm = _pick(M, max(512, min(2048, (1024 * 1024) // tn)), SUBLANES)
        tk = _pick(K, 2048)
    nk = K // tk
    grid = (N // tn, M // tm, nk)
    if mode == "nn":
        a_spec = pl.BlockSpec((tm, tk), lambda j, i, k: (i, k))
        b_spec = pl.BlockSpec((tk, tn), lambda j, i, k: (k, j))
        dims = NN
    elif mode == "nt":
        a_spec = pl.BlockSpec((tm, tk), lambda j, i, k: (i, k))
        b_spec = pl.BlockSpec((tn, tk), lambda j, i, k: (j, k))
        dims = NT
    else:
        a_spec = pl.BlockSpec((tk, tm), lambda j, i, k: (k, i))
        b_spec = pl.BlockSpec((tk, tn), lambda j, i, k: (k, j))
        dims = TN
    o_spec = pl.BlockSpec((tm, tn), lambda j, i, k: (i, j))
    has_add = add is not None

    def body(*refs):
        a_ref, b_ref = refs[:2]
        add_ref = refs[2] if has_add else None
        o_ref, acc = refs[-2:]
        k = pl.program_id(2)

        @pl.when(k == 0)
        def _():
            acc[...] = add_ref[...] if has_add else jnp.zeros_like(acc)

        acc[...] += _dot(a_ref[...], b_ref[...], dims)

        @pl.when(k == nk - 1)
        def _():
            o_ref[...] = acc[...].astype(o_ref.dtype)

    ins = [a, b] + ([add] if has_add else []) + ([after] if after is not None else [])
    specs = [a_spec, b_spec] + ([o_spec] if has_add else []) + ([ANY] if after is not None else [])
    return pl.pallas_call(
        body, name=name, grid=grid, in_specs=specs, out_specs=o_spec,
        out_shape=jax.ShapeDtypeStruct((M, N), out_dtype),
        scratch_shapes=[pltpu.VMEM((tm, tn), F32)],
        compiler_params=_cp("parallel", "parallel", "arbitrary"),
    )(*ins)


def _ffn_weight_operands(wt):
    return [_scalar(wt["chip"])] , [wt["gate"][0], wt["gate"][1], wt["up"][0], wt["up"][1], wt["down"][0], wt["down"][1]]


def _load_ffn_weights(chip_ref, shard_refs, wgu_v, wd_v, sem):
    fs = wd_v.shape[0] // N_CHIPS

    @pl.when(pl.program_id(0) == 0)
    def _():
        me = chip_ref[0]
        waits = []
        for t, (dst, base) in enumerate([(wgu_v, 0), (wgu_v, wd_v.shape[0]), (wd_v, 0)]):
            own, gathered = shard_refs[2 * t], shard_refs[2 * t + 1]
            for k in range(N_CHIPS):
                slot = dst.at[pl.ds(base + k * fs, fs), :]
                s = sem.at[t * N_CHIPS + k]

                @pl.when(me == k)
                def _(own=own, slot=slot, s=s):
                    pltpu.make_async_copy(own, slot, s).start()

                @pl.when(me != k)
                def _(gathered=gathered, k=k, slot=slot, s=s):
                    pltpu.make_async_copy(gathered.at[k], slot, s).start()

                waits.append(pltpu.make_async_copy(own, slot, s))
        for cp in waits:
            cp.wait()


def _ffn_fwd(x, g0, g1, wt, name):
    T, D = x.shape
    F = N_CHIPS * wt["down"][0].shape[0]
    F2 = 2 * F
    tm = _pick(T, 256, SUBLANES)
    prefetch, shards = _ffn_weight_operands(wt)

    def body(chip_ref, x_ref, g0_ref, g1_ref, *refs):
        shard_refs = refs[:6]
        xo_ref, h_ref, gu_ref, y_ref, wgu_v, wd_v, sem = refs[6:]
        _load_ffn_weights(chip_ref, shard_refs, wgu_v, wd_v, sem)
        xv = x_ref[...]
        hb = _mx(xv * _rms_stat(xv) * g0_ref[...])
        h_ref[...] = hb
        gu = _dot(hb, wgu_v[...], NT)
        gu_ref[...] = gu.astype(gu_ref.dtype)
        g = gu[:, :F]
        u = gu[:, F:]
        a = _mx(g * _sigmoid(g) * u)
        y = _dot(a, wd_v[...], NN)
        y_ref[...] = y
        xo_ref[...] = xv + 0.5 * (y * _rms_stat(y) * g1_ref[...])

    row = lambda w: pl.BlockSpec((tm, w), lambda i, c: (i, 0))
    one = pl.BlockSpec((1, D), lambda i, c: (0, 0))
    return pl.pallas_call(
        body, name=name,
        grid_spec=pltpu.PrefetchScalarGridSpec(
            num_scalar_prefetch=1, grid=(T // tm,),
            in_specs=[row(D), one, one] + [ANY] * 6,
            out_specs=[row(D), row(D), row(F2), row(D)],
            scratch_shapes=[pltpu.VMEM((F2, D), MXU_DTYPE), pltpu.VMEM((F, D), MXU_DTYPE),
                            pltpu.SemaphoreType.DMA((3 * N_CHIPS,))]),
        out_shape=[jax.ShapeDtypeStruct((T, D), F32), jax.ShapeDtypeStruct((T, D), MXU_DTYPE),
                   jax.ShapeDtypeStruct((T, F2), MXU_DTYPE), jax.ShapeDtypeStruct((T, D), F32)],
        compiler_params=_cp("arbitrary"),
    )(*prefetch, x, g0, g1, *shards)


FFN_BWD_CHUNK = 2816


def _ffn_bwd(dxo, x, y, gu, g0, g1, wt, name):
    T, D = x.shape
    F2 = gu.shape[1]
    F = F2 // 2
    tm = _pick(T, 256, SUBLANES)
    fc = _pick(F, FFN_BWD_CHUNK)
    prefetch, shards = _ffn_weight_operands(wt)

    def body(chip_ref, dxo_ref, x_ref, y_ref, gu_ref, g0_ref, g1_ref, *refs):
        shard_refs = refs[:6]
        dx_ref, dy_ref, a_ref, dgu_ref, dg0_ref, dg1_ref, wgu_v, wd_v, sem = refs[6:]
        _load_ffn_weights(chip_ref, shard_refs, wgu_v, wd_v, sem)

        @pl.when(pl.program_id(0) == 0)
        def _():
            dg0_ref[...] = jnp.zeros_like(dg0_ref)
            dg1_ref[...] = jnp.zeros_like(dg1_ref)

        dxo_v = dxo_ref[...]
        yv = y_ref[...]
        dy, dg1 = _rms_bwd(yv, _rms_stat(yv), g1_ref[...], 0.5 * dxo_v)
        dg1_ref[...] += dg1
        dyb = _mx(dy)
        dy_ref[...] = dyb
        dh = jnp.zeros((tm, D), F32)
        for c in range(F // fc):
            lo, hi = c * fc, (c + 1) * fc
            da = _dot(dyb, wd_v[lo:hi, :], NT)
            g = gu_ref[:, lo:hi].astype(F32)
            u = gu_ref[:, F + lo:F + hi].astype(F32)
            s = _sigmoid(g)
            sg = g * s
            a_ref[:, lo:hi] = _mx(sg * u)
            dg = _mx(da * u * (s * (1.0 + g * (1.0 - s))))
            du = _mx(da * sg)
            dgu_ref[:, lo:hi] = dg
            dgu_ref[:, F + lo:F + hi] = du
            dh = dh + _dot(dg, wgu_v[lo:hi, :], NN) + _dot(du, wgu_v[F + lo:F + hi, :], NN)
        xv = x_ref[...]
        dx, dg0 = _rms_bwd(xv, _rms_stat(xv), g0_ref[...], dh)
        dg0_ref[...] += dg0
        dx_ref[...] = dxo_v + dx

    row = lambda w: pl.BlockSpec((tm, w), lambda i, c: (i, 0))
    one = pl.BlockSpec((1, D), lambda i, c: (0, 0))
    return pl.pallas_call(
        body, name=name,
        grid_spec=pltpu.PrefetchScalarGridSpec(
            num_scalar_prefetch=1, grid=(T // tm,),
            in_specs=[row(D), row(D), row(D), row(F2), one, one] + [ANY] * 6,
            out_specs=[row(D), row(D), row(F), row(F2), one, one],
            scratch_shapes=[pltpu.VMEM((F2, D), MXU_DTYPE), pltpu.VMEM((F, D), MXU_DTYPE),
                            pltpu.SemaphoreType.DMA((3 * N_CHIPS,))]),
        out_shape=[jax.ShapeDtypeStruct((T, D), F32), jax.ShapeDtypeStruct((T, D), MXU_DTYPE),
                   jax.ShapeDtypeStruct((T, F), MXU_DTYPE), jax.ShapeDtypeStruct((T, F2), MXU_DTYPE),
                   jax.ShapeDtypeStruct((1, D), F32), jax.ShapeDtypeStruct((1, D), F32)],
        compiler_params=_cp("arbitrary"),
    )(*prefetch, dxo, x, y, gu, g0, g1, *shards)


def _norm_fwd(x, g, name):
    T, D = x.shape
    tm = _pick(T, 512, SUBLANES)

    def body(x_ref, g_ref, h_ref):
        xv = x_ref[...]
        h_ref[...] = _mx(xv * _rms_stat(xv) * g_ref[...])

    row = pl.BlockSpec((tm, D), lambda i: (i, 0))
    return pl.pallas_call(body, name=name, grid=(T // tm,), in_specs=[row, _full((1, D))], out_specs=row,
                          out_shape=jax.ShapeDtypeStruct((T, D), MXU_DTYPE), compiler_params=_cp("parallel"))(x, g)


def _postnorm_fwd(x, m, g, name):
    T, D = x.shape
    tm = _pick(T, 512, SUBLANES)

    def body(x_ref, m_ref, g_ref, o_ref):
        mv = m_ref[...]
        o_ref[...] = x_ref[...] + mv * _rms_stat(mv) * g_ref[...]

    row = pl.BlockSpec((tm, D), lambda i: (i, 0))
    return pl.pallas_call(body, name=name, grid=(T // tm,), in_specs=[row, row, _full((1, D))], out_specs=row,
                          out_shape=jax.ShapeDtypeStruct((T, D), F32), compiler_params=_cp("parallel"))(x, m, g)


def _postnorm_bwd(dxo, m, g, name):
    T, D = m.shape
    tm = _pick(T, 512, SUBLANES)

    def body(dxo_ref, m_ref, g_ref, dm_ref, dg_ref):
        @pl.when(pl.program_id(0) == 0)
        def _():
            dg_ref[...] = jnp.zeros_like(dg_ref)

        mv = m_ref[...]
        dm, dg = _rms_bwd(mv, _rms_stat(mv), g_ref[...], dxo_ref[...])
        dg_ref[...] += dg
        dm_ref[...] = _mx(dm)

    row = pl.BlockSpec((tm, D), lambda i: (i, 0))
    return pl.pallas_call(body, name=name, grid=(T // tm,), in_specs=[row, row, _full((1, D))],
                          out_specs=[row, _full((1, D))],
                          out_shape=[jax.ShapeDtypeStruct((T, D), MXU_DTYPE), jax.ShapeDtypeStruct((1, D), F32)],
                          compiler_params=_cp("arbitrary"))(dxo, m, g)


def _prenorm_bwd(dxo, dh, x, g, name):
    T, D = x.shape
    tm = _pick(T, 512, SUBLANES)

    def body(dxo_ref, dh_ref, x_ref, g_ref, dx_ref, dg_ref):
        @pl.when(pl.program_id(0) == 0)
        def _():
            dg_ref[...] = jnp.zeros_like(dg_ref)

        xv = x_ref[...]
        dx, dg = _rms_bwd(xv, _rms_stat(xv), g_ref[...], dh_ref[...])
        dg_ref[...] += dg
        dx_ref[...] = dxo_ref[...] + dx

    row = pl.BlockSpec((tm, D), lambda i: (i, 0))
    return pl.pallas_call(body, name=name, grid=(T // tm,), in_specs=[row, row, row, _full((1, D))],
                          out_specs=[row, _full((1, D))],
                          out_shape=[jax.ShapeDtypeStruct((T, D), F32), jax.ShapeDtypeStruct((1, D), F32)],
                          compiler_params=_cp("arbitrary"))(dxo, dh, x, g)


def _loss_fwd_bwd(y, target, name):
    T, D = y.shape
    tm = _pick(T, 512, SUBLANES)

    def body(y_ref, t_ref, l_ref, dy_ref):
        @pl.when(pl.program_id(0) == 0)
        def _():
            l_ref[...] = jnp.zeros_like(l_ref)

        e = y_ref[...] - t_ref[...]
        dy_ref[...] = e * (1.0 / D)
        l_ref[...] += 0.5 * jnp.sum(jnp.mean(e * e, axis=-1, keepdims=True), axis=0, keepdims=True)

    row = pl.BlockSpec((tm, D), lambda i: (i, 0))
    return pl.pallas_call(body, name=name, grid=(T // tm,), in_specs=[row, row],
                          out_specs=[_full((SUBLANES, LANES)), row],
                          out_shape=[jax.ShapeDtypeStruct((SUBLANES, LANES), F32), jax.ShapeDtypeStruct((T, D), F32)],
                          compiler_params=_cp("arbitrary"))(y, target)


DN_ROWS = 512


def _shift_down(prev8, cur, s):
    n = cur.shape[0]
    xx = jnp.concatenate([prev8, cur], axis=0)
    return pltpu.roll(xx, s, 0)[SUBLANES:SUBLANES + n, :]


def _shift_up(cur, next8, s):
    n = cur.shape[0]
    xx = jnp.concatenate([cur, next8], axis=0)
    return pltpu.roll(xx, n + SUBLANES - s, 0)[:n, :]


def _conv_tile(x_ref, w, r, rows):
    start = pl.multiple_of(r * rows, SUBLANES)
    cur = x_ref[pl.ds(start, rows), :]
    pstart = pl.multiple_of(jnp.maximum(start - SUBLANES, 0), SUBLANES)
    prev8 = jnp.where(r == 0, 0.0, x_ref[pl.ds(pstart, SUBLANES), :])
    taps = [_shift_down(prev8, cur, DN_CONV - 1 - j) if j < DN_CONV - 1 else cur for j in range(DN_CONV)]
    c = taps[0] * w[0:1, :]
    for j in range(1, DN_CONV):
        c = c + taps[j] * w[j:j + 1, :]
    return c, taps


def _dn_prep_fwd(proj, conv_w, name):
    T = proj.shape[0]
    W = DN_HEADS * DN_HEAD_DIM
    rows = min(DN_ROWS, T)
    n_inner = T // rows
    scale = DN_HEAD_DIM ** -0.5

    def body(x_ref, w_ref, o_ref):
        cb = pl.program_id(0)
        w = w_ref[...]
        is_qk = cb < 2 * DN_HEADS
        post = jnp.where(cb < DN_HEADS, scale, 1.0)

        def step(r, carry):
            c, _ = _conv_tile(x_ref, w, r, rows)
            s = c * _sigmoid(c)
            rinv = lax.rsqrt(jnp.sum(s * s, axis=-1, keepdims=True) + L2_EPS)
            o_ref[pl.ds(pl.multiple_of(r * rows, SUBLANES), rows), :] = jnp.where(is_qk, s * rinv * post, s)
            return carry

        lax.fori_loop(0, n_inner, step, 0)

    col = pl.BlockSpec((T, LANES), lambda j: (0, j))
    return pl.pallas_call(body, name=name, grid=(3 * W // LANES,),
                          in_specs=[col, pl.BlockSpec((DN_CONV, LANES), lambda j: (0, j))], out_specs=col,
                          out_shape=jax.ShapeDtypeStruct((T, 3 * W), F32), compiler_params=_cp("parallel"))(proj, conv_w)


def _dn_prep_bwd(proj, conv_w, dqkv, name):
    T = proj.shape[0]
    W = DN_HEADS * DN_HEAD_DIM
    rows = min(DN_ROWS, T)
    n_inner = T // rows
    scale = DN_HEAD_DIM ** -0.5

    def body(x_ref, w_ref, dy_ref, dx_ref, dw_ref, dc_scr):
        cb = pl.program_id(0)
        w = w_ref[...]
        is_qk = cb < 2 * DN_HEADS
        post = jnp.where(cb < DN_HEADS, scale, 1.0)

        def step1(r, dws):
            c, taps = _conv_tile(x_ref, w, r, rows)
            sg = _sigmoid(c)
            s = c * sg
            rinv = lax.rsqrt(jnp.sum(s * s, axis=-1, keepdims=True) + L2_EPS)
            dy = dy_ref[pl.ds(pl.multiple_of(r * rows, SUBLANES), rows), :]
            yn = s * rinv
            dyn = dy * post
            ds_qk = rinv * (dyn - yn * jnp.sum(dyn * yn, axis=-1, keepdims=True))
            ds = jnp.where(is_qk, ds_qk, dy)
            dc = ds * (sg * (1.0 + c * (1.0 - sg)))
            dc_scr[pl.ds(pl.multiple_of(r * rows, SUBLANES), rows), :] = dc
            return tuple(dws[j] + jnp.sum(dc * taps[j], axis=0, keepdims=True) for j in range(DN_CONV))

        zero = jnp.zeros((1, LANES), F32)
        dws = lax.fori_loop(0, n_inner, step1, (zero,) * DN_CONV)
        for j in range(DN_CONV):
            dw_ref[j:j + 1, :] = dws[j]

        def step2(r, carry):
            start = pl.multiple_of(r * rows, SUBLANES)
            cur = dc_scr[pl.ds(start, rows), :]
            nstart = pl.multiple_of(jnp.minimum(start + rows, T - SUBLANES), SUBLANES)
            next8 = jnp.where(r == n_inner - 1, 0.0, dc_scr[pl.ds(nstart, SUBLANES), :])
            dx = cur * w[DN_CONV - 1:DN_CONV, :]
            for j in range(DN_CONV - 1):
                dx = dx + _shift_up(cur, next8, DN_CONV - 1 - j) * w[j:j + 1, :]
            dx_ref[pl.ds(start, rows), :] = _mx(dx)
            return carry

        lax.fori_loop(0, n_inner, step2, 0)

    col = pl.BlockSpec((T, LANES), lambda j: (0, j))
    wspec = pl.BlockSpec((DN_CONV, LANES), lambda j: (0, j))
    return pl.pallas_call(body, name=name, grid=(3 * W // LANES,), in_specs=[col, wspec, col], out_specs=[col, wspec],
                          out_shape=[jax.ShapeDtypeStruct((T, 3 * W), MXU_DTYPE), jax.ShapeDtypeStruct((DN_CONV, 3 * W), F32)],
                          scratch_shapes=[pltpu.VMEM((T, LANES), F32)], compiler_params=_cp("parallel"))(proj, conv_w, dqkv)


def _softplus(x):
    return jnp.maximum(x, 0.0) + jnp.log(1.0 + jnp.exp(-jnp.abs(x)))


def _dn_gate_fwd(ba, a_log, dt_bias, name):
    T = ba.shape[0]
    tm = _pick(T, 1024, SUBLANES)

    def body(ba_ref, al_ref, dt_ref, beta_ref, g_ref):
        beta_ref[...] = _sigmoid(ba_ref[:, :LANES])
        g_ref[...] = -jnp.exp(al_ref[...]) * _softplus(ba_ref[:, LANES:] + dt_ref[...])

    row = lambda w: pl.BlockSpec((tm, w), lambda i: (i, 0))
    return pl.pallas_call(body, name=name, grid=(T // tm,), in_specs=[row(2 * LANES), _full((1, LANES)), _full((1, LANES))],
                          out_specs=[row(LANES), row(LANES)],
                          out_shape=[jax.ShapeDtypeStruct((T, LANES), F32)] * 2, compiler_params=_cp("parallel"))(ba, a_log, dt_bias)


def _dn_gate_bwd(ba, a_log, dt_bias, dbeta, dg, name):
    T = ba.shape[0]
    tm = _pick(T, 1024, SUBLANES)

    def body(ba_ref, al_ref, dt_ref, dbeta_ref, dg_ref, dba_ref, dal_ref, ddt_ref):
        @pl.when(pl.program_id(0) == 0)
        def _():
            dal_ref[...] = jnp.zeros_like(dal_ref)
            ddt_ref[...] = jnp.zeros_like(ddt_ref)

        beta = _sigmoid(ba_ref[:, :LANES])
        dba_ref[:, :LANES] = _mx(dbeta_ref[...] * beta * (1.0 - beta))
        pre = ba_ref[:, LANES:] + dt_ref[...]
        ea = jnp.exp(al_ref[...])
        dgv = dg_ref[...]
        da = dgv * (-ea) * _sigmoid(pre)
        dba_ref[:, LANES:] = _mx(da)
        ddt_ref[...] += jnp.sum(da, axis=0, keepdims=True)
        dal_ref[...] += jnp.sum(dgv * (-ea) * _softplus(pre), axis=0, keepdims=True)

    row = lambda w: pl.BlockSpec((tm, w), lambda i: (i, 0))
    one = _full((1, LANES))
    return pl.pallas_call(body, name=name, grid=(T // tm,), in_specs=[row(2 * LANES), one, one, row(LANES), row(LANES)],
                          out_specs=[row(2 * LANES), one, one],
                          out_shape=[jax.ShapeDtypeStruct((T, 2 * LANES), MXU_DTYPE), jax.ShapeDtypeStruct((1, LANES), F32),
                                     jax.ShapeDtypeStruct((1, LANES), F32)],
                          compiler_params=_cp("arbitrary"))(ba, a_log, dt_bias, dbeta, dg)


def _tri(c, strict):
    i = lax.broadcasted_iota(jnp.int32, (c, c), 0)
    j = lax.broadcasted_iota(jnp.int32, (c, c), 1)
    return (i > j) if strict else (i >= j)


def _inv_unit_lower(ls):
    c = ls[0].shape[0]
    i = lax.broadcasted_iota(jnp.int32, (c, c), 0)
    j = lax.broadcasted_iota(jnp.int32, (c, c), 1)
    eye = jnp.where(i == j, 1.0, 0.0)
    facs = [[eye - l for l in ls]]
    cur = ls
    for _ in range(int(math.log2(c)) - 1):
        cur = [_dot(p, p, NN, TRI_PREC) for p in cur]
        facs.append([eye + p for p in cur])
    while len(facs) > 1:
        nxt = [[_dot(a, b, NN, TRI_PREC) for a, b in zip(facs[t], facs[t + 1])] for t in range(0, len(facs) - 1, 2)]
        if len(facs) % 2:
            nxt.append(facs[-1])
        facs = nxt
    return facs[0]


def _chunk_gates(g_blk):
    c = g_blk.shape[0]
    gcs = _dot(jnp.where(_tri(c, False), 1.0, 0.0), g_blk, NN, HI)
    return gcs, gcs.T


def _head_chunk(h, qh, kh, vh, beta_blk, gcs, gcs_t):
    c = qh.shape[0]
    incl = _tri(c, False)
    gc_col = gcs[:, h:h + 1]
    gc_row = gcs_t[h:h + 1, :]
    gc_last = gcs_t[h:h + 1, c - 1:c]
    dec = jnp.where(incl, jnp.exp(jnp.where(incl, gc_col - gc_row, 0.0)), 0.0)
    gam = jnp.exp(gc_col)
    rr = jnp.exp(gc_last - gc_col)
    gl = jnp.exp(gc_last)
    b = beta_blk[:, h:h + 1]
    kb = kh * b
    vb = vh * b
    kk = _dot(_mx(kb), _mx(kh), NT)
    lmat = jnp.where(_tri(c, True), kk * dec, 0.0)
    qk = _dot(_mx(qh), _mx(kh), NT)
    pmat = jnp.where(incl, qk * dec, 0.0)
    return dict(dec=dec, gam=gam, rr=rr, gl=gl, b=b, kb=kb, vb=vb, lmat=lmat, pmat=pmat)


def _dn_scan_fwd(qkv, beta, g, proj, norm_g, name):
    T = qkv.shape[0]
    C, H, Dh = DN_CHUNK, DN_HEADS, DN_HEAD_DIM
    W = H * Dh
    N = T // C

    def body(q_ref, k_ref, v_ref, beta_ref, g_ref, z_ref, ng_ref, og_ref, o_ref, tinv_ref, s_ref, state):
        @pl.when(pl.program_id(0) == 0)
        def _():
            state[...] = jnp.zeros_like(state)

        gcs, gcs_t = _chunk_gates(g_ref[...])
        beta_blk = beta_ref[...]
        ng = ng_ref[...]
        heads = range(H)
        cs = [slice(h * Dh, (h + 1) * Dh) for h in heads]
        qs = [_head_chunk(h, q_ref[:, cs[h]], k_ref[:, cs[h]], v_ref[:, cs[h]], beta_blk, gcs, gcs_t) for h in heads]
        tinvs = _inv_unit_lower([q["lmat"] for q in qs])
        for h in heads:
            tinv_ref[h] = tinvs[h]
        us = [_dot(tinvs[h], qs[h]["vb"], NN, TRI_PREC) for h in heads]
        ws = [_dot(tinvs[h], qs[h]["kb"] * qs[h]["gam"], NN, TRI_PREC) for h in heads]
        ss = [state[h] for h in heads]
        for h in heads:
            s_ref[0, h] = ss[h]
        sbs = [_mx(s) for s in ss]
        vnbs = [_mx(us[h] - _dot(_mx(ws[h]), sbs[h], NN)) for h in heads]
        os_ = [_dot(_mx(q_ref[:, cs[h]] * qs[h]["gam"]), sbs[h], NN) + _dot(_mx(qs[h]["pmat"]), vnbs[h], NN) for h in heads]
        for h in heads:
            state[h] = ss[h] * qs[h]["gl"] + _dot(_mx((k_ref[:, cs[h]] * qs[h]["rr"]).T), vnbs[h], NN)
        for h in heads:
            o = os_[h]
            o_ref[:, cs[h]] = o
            zh = z_ref[:, cs[h]]
            og_ref[:, cs[h]] = _mx(o * _rms_stat(o) * ng * (zh * _sigmoid(zh)))

    blk = lambda j: pl.BlockSpec((C, W), lambda n: (n, j))
    small = pl.BlockSpec((C, LANES), lambda n: (n, 0))
    return pl.pallas_call(
        body, name=name, grid=(N,),
        in_specs=[blk(0), blk(1), blk(2), small, small, blk(3), _full((1, Dh))],
        out_specs=[blk(0), blk(0), pl.BlockSpec((H, C, C), lambda n: (0, n, 0)),
                   pl.BlockSpec((1, H, Dh, Dh), lambda n: (n, 0, 0, 0))],
        out_shape=[jax.ShapeDtypeStruct((T, W), MXU_DTYPE), jax.ShapeDtypeStruct((T, W), F32),
                   jax.ShapeDtypeStruct((H, T, C), F32), jax.ShapeDtypeStruct((N, H, Dh, Dh), F32)],
        scratch_shapes=[pltpu.VMEM((H, Dh, Dh), F32)],
        compiler_params=_cp("arbitrary"),
    )(qkv, qkv, qkv, beta, g, proj, norm_g)


def _dn_scan_bwd(qkv, beta, g, proj, norm_g, o, tinv, s_all, dog, name):
    T = qkv.shape[0]
    C, H, Dh = DN_CHUNK, DN_HEADS, DN_HEAD_DIM
    W = H * Dh
    N = T // C

    def body(q_ref, k_ref, v_ref, beta_ref, g_ref, z_ref, ng_ref, o_ref, tinv_ref, s_ref, dog_ref,
             dqkv_ref, dbeta_ref, dg_ref, dz_ref, dng_ref, dstate):
        @pl.when(pl.program_id(0) == 0)
        def _():
            dstate[...] = jnp.zeros_like(dstate)
            dng_ref[...] = jnp.zeros_like(dng_ref)

        gcs, gcs_t = _chunk_gates(g_ref[...])
        beta_blk = beta_ref[...]
        ng = ng_ref[...]
        incl = _tri(C, False)
        strict = _tri(C, True)
        lane = lax.broadcasted_iota(jnp.int32, (C, LANES), 1)
        rowi = lax.broadcasted_iota(jnp.int32, (C, 1), 0)
        ones = jnp.ones((C, LANES), F32)
        dbeta_acc = jnp.zeros((C, LANES), F32)
        dgc_acc = jnp.zeros((C, LANES), F32)
        dng_acc = jnp.zeros((1, Dh), F32)
        cs = [slice(h * Dh, (h + 1) * Dh) for h in range(H)]
        rsum = lambda t: jnp.sum(t, axis=1, keepdims=True)
        for heads in (range(0, H // 2), range(H // 2, H)):
            dobs = {}
            for h in heads:
                oh, zh, dogh = o_ref[:, cs[h]], z_ref[:, cs[h]], dog_ref[:, cs[h]]
                rstat = _rms_stat(oh)
                sz = _sigmoid(zh)
                dz_ref[:, cs[h]] = _mx(dogh * (oh * rstat * ng) * (sz * (1.0 + zh * (1.0 - sz))))
                do, dng = _rms_bwd(oh, rstat, ng, dogh * (zh * sz))
                dng_acc = dng_acc + dng
                dobs[h] = _mx(do)
            qs = {h: _head_chunk(h, q_ref[:, cs[h]], k_ref[:, cs[h]], v_ref[:, cs[h]], beta_blk, gcs, gcs_t) for h in heads}
            tms = {h: tinv_ref[h] for h in heads}
            us = {h: _dot(tms[h], qs[h]["vb"], NN, TRI_PREC) for h in heads}
            ws = {h: _dot(tms[h], qs[h]["kb"] * qs[h]["gam"], NN, TRI_PREC) for h in heads}
            ss = {h: s_ref[0, h] for h in heads}
            sbs = {h: _mx(ss[h]) for h in heads}
            wbs = {h: _mx(ws[h]) for h in heads}
            vnbs = {h: _mx(us[h] - _dot(wbs[h], sbs[h], NN)) for h in heads}
            dsns = {h: dstate[h] for h in heads}
            dsbs = {h: _mx(dsns[h]) for h in heads}
            dvnews = {h: _dot(_mx(qs[h]["pmat"]), dobs[h], TN) + _dot(_mx(k_ref[:, cs[h]] * qs[h]["rr"]), dsbs[h], NN)
                      for h in heads}
            dvb16s = {h: _mx(dvnews[h]) for h in heads}
            dps = {h: jnp.where(incl, _dot(dobs[h], vnbs[h], NT), 0.0) for h in heads}
            dqds = {h: _dot(dobs[h], sbs[h], NT) for h in heads}
            dkds = {h: _dot(vnbs[h], dsbs[h], NT) for h in heads}
            dgls = {h: jnp.sum(rsum(ss[h] * dsns[h]), axis=0, keepdims=True) for h in heads}
            dws = {h: -_dot(dvb16s[h], sbs[h], NT) for h in heads}
            for h in heads:
                dstate[h] = (_dot(_mx(q_ref[:, cs[h]] * qs[h]["gam"]), dobs[h], TN) + qs[h]["gl"] * dsns[h]
                             - _dot(wbs[h], dvb16s[h], TN))
            dvbs = {h: _dot(tms[h], dvnews[h], TN, TRI_PREC) for h in heads}
            dkbgs = {h: _dot(tms[h], dws[h], TN, TRI_PREC) for h in heads}
            dls = {h: jnp.where(strict, -(_dot(dvbs[h], us[h], NT, TRI_PREC) + _dot(dkbgs[h], ws[h], NT, TRI_PREC)), 0.0)
                   for h in heads}
            mmats = {h: dls[h] * qs[h]["lmat"] + dps[h] * qs[h]["pmat"] for h in heads}
            dgcs = {h: rsum(mmats[h]) - _dot(mmats[h], ones, TN, HI)[:, :1] for h in heads}
            dkk16s = {h: _mx(dls[h] * qs[h]["dec"]) for h in heads}
            dqk16s = {h: _mx(dps[h] * qs[h]["dec"]) for h in heads}
            for h in heads:
                q = qs[h]
                qh, kh, vh = q_ref[:, cs[h]], k_ref[:, cs[h]], v_ref[:, cs[h]]
                gam, rr, b, kb = q["gam"], q["rr"], q["b"], q["kb"]
                dkb = _dot(dkk16s[h], _mx(kh), NN) + dkbgs[h] * gam
                dk = _dot(dkk16s[h], _mx(kb), TN) + _dot(dqk16s[h], _mx(qh), TN) + dkb * b + dkds[h] * rr
                dq = _dot(dqk16s[h], _mx(kh), NN) + dqds[h] * gam
                dgam = rsum(dkbgs[h] * kb) + rsum(dqds[h] * qh)
                dr = rsum(dkds[h] * kh)
                dgc_last = jnp.sum(dr * rr, axis=0, keepdims=True) + dgls[h] * q["gl"]
                dgc = dgcs[h] + dgam * gam - dr * rr + jnp.where(rowi == C - 1, dgc_last, 0.0)
                dbeta = rsum(dvbs[h] * vh) + rsum(dkb * kh)
                dqkv_ref[:, cs[h]] = dq
                dqkv_ref[:, W + h * Dh:W + (h + 1) * Dh] = dk
                dqkv_ref[:, 2 * W + h * Dh:2 * W + (h + 1) * Dh] = dvbs[h] * b
                dbeta_acc = jnp.where(lane == h, dbeta, dbeta_acc)
                dgc_acc = jnp.where(lane == h, dgc, dgc_acc)
        dbeta_ref[...] = dbeta_acc
        dg_ref[...] = _dot(jnp.where(incl, 1.0, 0.0), dgc_acc, TN, HI)
        dng_ref[...] += dng_acc

    rev = lambda n: N - 1 - n
    blk = lambda j: pl.BlockSpec((C, W), lambda n: (rev(n), j))
    small = pl.BlockSpec((C, LANES), lambda n: (rev(n), 0))
    return pl.pallas_call(
        body, name=name, grid=(N,),
        in_specs=[blk(0), blk(1), blk(2), small, small, blk(3), _full((1, Dh)), blk(0),
                  pl.BlockSpec((H, C, C), lambda n: (0, rev(n), 0)),
                  pl.BlockSpec((1, H, Dh, Dh), lambda n: (rev(n), 0, 0, 0)), blk(0)],
        out_specs=[pl.BlockSpec((C, 3 * W), lambda n: (rev(n), 0)), small, small, blk(0), _full((1, Dh))],
        out_shape=[jax.ShapeDtypeStruct((T, 3 * W), F32), jax.ShapeDtypeStruct((T, LANES), F32),
                   jax.ShapeDtypeStruct((T, LANES), F32), jax.ShapeDtypeStruct((T, W), MXU_DTYPE),
                   jax.ShapeDtypeStruct((1, Dh), F32)],
        scratch_shapes=[pltpu.VMEM((H, Dh, Dh), F32)],
        compiler_params=_cp("arbitrary"),
    )(qkv, qkv, qkv, beta, g, proj, norm_g, o, tinv, s_all, dog)


_INV_SQRT2 = 0.7071067811865476
_INV_SQRT_2PI = 0.3989422804014327


def _sg_recompute(zp_ref, bin_ref, lng_ref, lnb_ref):
    E = SG_WIDTH
    zin = zp_ref[...] + bin_ref[...]
    cdf = 0.5 * (1.0 + lax.erf(zin * _INV_SQRT2))
    zz = zin * cdf
    u = zz[:, :E]
    vp = zz[:, E:]
    mu = jnp.mean(vp, axis=-1, keepdims=True)
    xc = vp - mu
    rstd = lax.rsqrt(jnp.mean(xc * xc, axis=-1, keepdims=True) + LN_EPS)
    xhat = xc * rstd
    v = xhat * lng_ref[...] + lnb_ref[...]
    return zin, cdf, u, xhat, rstd, v


def _sg_masked_ws(ws_ref, g):
    return _mx(jnp.where(_tri(SG_CHUNK, False), ws_ref[g], 0.0))


def _sg_fwd(zpre, b_in, ln_g, ln_b, w_s, b_s_t, name):
    T = zpre.shape[0]
    E, G, C, GW = SG_WIDTH, SG_GROUPS, SG_CHUNK, SG_GROUP_W

    def body(zp_ref, bin_ref, lng_ref, lnb_ref, ws_ref, bst_ref, um_ref):
        _, _, u, _, _, v = _sg_recompute(zp_ref, bin_ref, lng_ref, lnb_ref)
        bst = bst_ref[...]
        for g in range(G):
            cs = slice(g * GW, (g + 1) * GW)
            mixed = _dot(_sg_masked_ws(ws_ref, g), _mx(v[:, cs]), NN) + bst[:, g:g + 1]
            um_ref[:, cs] = _mx(u[:, cs] * mixed)

    return pl.pallas_call(
        body, name=name, grid=(T // C,),
        in_specs=[pl.BlockSpec((C, 2 * E), lambda n: (n, 0)), _full((1, 2 * E)), _full((1, E)), _full((1, E)),
                  _full((G, C, C)), _full((C, LANES))],
        out_specs=pl.BlockSpec((C, E), lambda n: (n, 0)),
        out_shape=jax.ShapeDtypeStruct((T, E), MXU_DTYPE), compiler_params=_cp("parallel"),
    )(zpre, b_in, ln_g, ln_b, w_s, b_s_t)


def _sg_bwd(zpre, b_in, ln_g, ln_b, w_s, b_s_t, dum, name):
    T = zpre.shape[0]
    E, G, C, GW = SG_WIDTH, SG_GROUPS, SG_CHUNK, SG_GROUP_W

    def body(zp_ref, bin_ref, lng_ref, lnb_ref, ws_ref, bst_ref, dum_ref,
             dz_ref, dbin_ref, dlng_ref, dlnb_ref, dws_ref, dbst_ref):
        @pl.when(pl.program_id(0) == 0)
        def _():
            for r in (dbin_ref, dlng_ref, dlnb_ref, dws_ref, dbst_ref):
                r[...] = jnp.zeros_like(r)

        zin, cdf, u, xhat, rstd, v = _sg_recompute(zp_ref, bin_ref, lng_ref, lnb_ref)
        bst = bst_ref[...]
        lane = lax.broadcasted_iota(jnp.int32, (C, LANES), 1)
        dum_v = dum_ref[...]
        dbst = jnp.zeros((C, LANES), F32)
        du_parts, dv_parts = [], []
        for g in range(G):
            cs = slice(g * GW, (g + 1) * GW)
            wsm = _sg_masked_ws(ws_ref, g)
            vg = _mx(v[:, cs])
            mixed = _dot(wsm, vg, NN) + bst[:, g:g + 1]
            dumg = dum_v[:, cs]
            du_parts.append(dumg * mixed)
            dmixed = dumg * u[:, cs]
            dmb = _mx(dmixed)
            dv_parts.append(_dot(wsm, dmb, TN))
            dws_ref[g] += _dot(dmb, vg, NT)
            dbst = jnp.where(lane == g, jnp.sum(dmixed, axis=1, keepdims=True), dbst)
        dbst_ref[...] += dbst
        du = jnp.concatenate(du_parts, axis=1)
        dv = jnp.concatenate(dv_parts, axis=1)
        dlng_ref[...] += jnp.sum(dv * xhat, axis=0, keepdims=True)
        dlnb_ref[...] += jnp.sum(dv, axis=0, keepdims=True)
        dxh = dv * lng_ref[...]
        dvp = rstd * (dxh - jnp.mean(dxh, axis=-1, keepdims=True) - xhat * jnp.mean(dxh * xhat, axis=-1, keepdims=True))
        dzz = jnp.concatenate([du, dvp], axis=1)
        dzin = dzz * (cdf + zin * (_INV_SQRT_2PI * jnp.exp(-0.5 * zin * zin)))
        dz_ref[...] = _mx(dzin)
        dbin_ref[...] += jnp.sum(dzin, axis=0, keepdims=True)

    return pl.pallas_call(
        body, name=name, grid=(T // C,),
        in_specs=[pl.BlockSpec((C, 2 * E), lambda n: (n, 0)), _full((1, 2 * E)), _full((1, E)), _full((1, E)),
                  _full((G, C, C)), _full((C, LANES)), pl.BlockSpec((C, E), lambda n: (n, 0))],
        out_specs=[pl.BlockSpec((C, 2 * E), lambda n: (n, 0)), _full((1, 2 * E)), _full((1, E)), _full((1, E)),
                   _full((G, C, C)), _full((C, LANES))],
        out_shape=[jax.ShapeDtypeStruct((T, 2 * E), MXU_DTYPE), jax.ShapeDtypeStruct((1, 2 * E), F32),
                   jax.ShapeDtypeStruct((1, E), F32), jax.ShapeDtypeStruct((1, E), F32),
                   jax.ShapeDtypeStruct((G, C, C), F32), jax.ShapeDtypeStruct((C, LANES), F32)],
        compiler_params=_cp("arbitrary"),
    )(zpre, b_in, ln_g, ln_b, w_s, b_s_t, dum)


def _row(v):
    return v.reshape(1, -1)


def _pad_lanes(v):
    v = v.reshape(1, -1)
    return jnp.pad(v, ((0, 0), (0, LANES - v.shape[1])))


def _local_step(x, target, p, weights_for, grads_ready=None, small_ready=None):
    ng = p["norm_g"]
    grads = {}
    dng = [[None] * 6 for _ in range(2)]
    order = [jnp.zeros((), F32)]

    def tell(group):
        zero = grads_ready(group, grads) if grads_ready is not None else None
        if zero is not None:
            order[0] = zero

    def gain(i, s):
        return _row(ng[i, s]) + order[0]

    def ffn_f(xin, i, j, tag):
        wt = weights_for("ffn" + tag, xin)
        xo, h, gu, y = _ffn_fwd(xin, _row(ng[i, 4 * j]), _row(ng[i, 4 * j + 1]), wt, "ffn_fwd_" + tag)
        return xo, (xin, h, gu, y, wt)

    x1, sv_f00 = ffn_f(x, 0, 0, "00")
    dnw = weights_for("dn", x1)
    hn0 = _norm_fwd(x1, _row(ng[0, 2]), "dn_prenorm")
    proj = _mm(hn0, dnw["dn_wqkvz"], "nn", "dn_proj")
    ba = _mm(hn0, dnw["dn_wba"], "nn", "dn_proj_ba")
    a_log = _pad_lanes(p["dn_a_log"])
    dt_bias = _pad_lanes(p["dn_dt_bias"])
    dn_ng = _row(p["dn_norm_g"])
    qkv = _dn_prep_fwd(proj, p["dn_conv_w"], "dn_prep_fwd")
    beta, gdec = _dn_gate_fwd(ba, a_log, dt_bias, "dn_gate_fwd")
    og, o_raw, tinv, s_all = _dn_scan_fwd(qkv, beta, gdec, proj, dn_ng, "dn_scan_fwd")
    m0 = _mm(og, dnw["dn_wout"], "nn", "dn_out")
    x2 = _postnorm_fwd(x1, m0, _row(ng[0, 3]), "dn_postnorm")
    x3, sv_f01 = ffn_f(x2, 0, 1, "01")
    x4, sv_f10 = ffn_f(x3, 1, 0, "10")
    sgw = weights_for("sg", x4)
    hn1 = _norm_fwd(x4, _row(ng[1, 2]), "sg_prenorm")
    zpre = _mm(hn1, sgw["sg_win"], "nn", "sg_proj")
    sg_bin = _row(p["sg_b_in"])
    sg_lng = _row(p["sg_ln_g"])
    sg_lnb = _row(p["sg_ln_b"])
    sg_bst = jnp.pad(p["sg_b_s"].T, ((0, 0), (0, LANES - SG_GROUPS)))
    um = _sg_fwd(zpre, sg_bin, sg_lng, sg_lnb, p["sg_w_s"], sg_bst, "sg_fwd")
    m1 = _mm(um, sgw["sg_wout"], "nn", "sg_out")
    x5 = _postnorm_fwd(x4, m1, _row(ng[1, 3]), "sg_postnorm")
    x6, sv_f11 = ffn_f(x5, 1, 1, "11")
    loss_part, dx = _loss_fwd_bwd(x6, target, "loss")

    def ffn_b(dxo, sv, i, j, tag, last=False):
        xin, h, gu, y, wt = sv
        dxi, dy, a, dgu, dg0, dg1 = _ffn_bwd(dxo, xin, y, gu, gain(i, 4 * j), gain(i, 4 * j + 1), wt, "ffn_bwd_" + tag)
        dng[i][4 * j] = dg0
        dng[i][4 * j + 1] = dg1
        after = None
        if last:
            grads["norm_g"] = jnp.stack([jnp.concatenate(dng[t], axis=0) for t in range(2)])
            after = small_ready(grads, loss_part) if small_ready is not None else None
        grads["wd" + tag] = _mm(a, dy, "tn", "ffn_wgrad_down_" + tag, after=after)
        grads["wguT" + tag] = _mm(dgu, h, "tn", "ffn_wgrad_up_" + tag, after=after)
        tell("ffn" + tag)
        return dxi

    dx = ffn_b(dx, sv_f11, 1, 1, "11")
    dm1, dng[1][3] = _postnorm_bwd(dx, m1, gain(1, 3), "sg_postnorm_bwd")
    grads["sg_w_out"] = _mm(um, dm1, "tn", "sg_wgrad_out")
    dum = _mm(dm1, sgw["sg_wout"], "nt", "sg_dgrad_out")
    dz1, dbin, dlng, dlnb, dws, dbst = _sg_bwd(zpre, sg_bin, sg_lng, sg_lnb, p["sg_w_s"], sg_bst, dum, "sg_bwd")
    grads["sg_w_inT"] = _mm(dz1, hn1, "tn", "sg_wgrad_in")
    tell("sg")
    dh1 = _mm(dz1, sgw["sg_win"], "nt", "sg_dgrad_in")
    dx, dng[1][2] = _prenorm_bwd(dx, dh1, x4, gain(1, 2), "sg_prenorm_bwd")
    grads["sg_b_in"] = dbin.reshape(1, -1)
    grads["sg_ln_g"] = dlng.reshape(1, -1)
    grads["sg_ln_b"] = dlnb.reshape(1, -1)
    grads["sg_w_s"] = jnp.where(jnp.tril(jnp.ones((SG_CHUNK, SG_CHUNK), bool)), dws, 0.0)[None]
    grads["sg_b_s"] = dbst[:, :SG_GROUPS].T[None]
    dx = ffn_b(dx, sv_f10, 1, 0, "10")
    dx = ffn_b(dx, sv_f01, 0, 1, "01")
    dm0, dng[0][3] = _postnorm_bwd(dx, m0, gain(0, 3), "dn_postnorm_bwd")
    grads["dn_w_out"] = _mm(og, dm0, "tn", "dn_wgrad_out")
    dog = _mm(dm0, dnw["dn_wout"], "nt", "dn_dgrad_out")
    dqkv, dbeta, dgdec, dz0, dnng = _dn_scan_bwd(qkv, beta, gdec, proj, dn_ng, o_raw, tinv, s_all, dog, "dn_scan_bwd")
    dqkv_pre, dconv = _dn_prep_bwd(proj, p["dn_conv_w"], dqkv, "dn_prep_bwd")
    dba, dal, ddt = _dn_gate_bwd(ba, a_log, dt_bias, dbeta, dgdec, "dn_gate_bwd")
    W3 = 3 * DN_HEADS * DN_HEAD_DIM
    dw_qkv = _mm(hn0, dqkv_pre, "tn", "dn_wgrad_qkv")
    dw_z = _mm(hn0, dz0, "tn", "dn_wgrad_z")
    dw_ba = _mm(hn0, dba, "tn", "dn_wgrad_ba")
    grads["dn_w_in"] = jnp.concatenate(
        [dw_qkv, dw_z, dw_ba[:, :DN_HEADS], dw_ba[:, LANES:LANES + DN_HEADS]], axis=1)
    tell("dn")
    dh0 = _mm(dqkv_pre, dnw["dn_wqkvz"][:, :W3], "nt", "dn_dgrad_qkv")
    dh0 = _mm(dz0, dnw["dn_wqkvz"][:, W3:], "nt", "dn_dgrad_z", add=dh0)
    dh0 = _mm(dba, dnw["dn_wba"], "nt", "dn_dgrad_ba", add=dh0)
    dx, dng[0][2] = _prenorm_bwd(dx, dh0, x1, gain(0, 2), "dn_prenorm_bwd")
    grads["dn_conv_w"] = dconv[None]
    grads["dn_a_log"] = dal[:, :DN_HEADS]
    grads["dn_dt_bias"] = ddt[:, :DN_HEADS]
    grads["dn_norm_g"] = dnng
    dx = ffn_b(dx, sv_f00, 0, 0, "00", last=True)
    return loss_part, dx, grads


def _mesh_pos():
    return lax.axis_index("x"), lax.axis_index("y"), lax.axis_index("c")


def _other_chips(x, y):
    return [(1 - x, y), (x, 1 - y), (1 - x, 1 - y)]


def _allgather_chips(arrs, name):
    n = len(arrs)

    def body(*refs):
        ins, outs = refs[:n], refs[n:2 * n]
        ici_send, ici_recv, d2d_send, d2d_recv = refs[2 * n:]
        x, y, c = _mesh_pos()
        me = 2 * x + y
        chips = _other_chips(x, y)
        sibling = (x, y, 1 - c)

        def ici(i, j, k):
            cx, cy = chips[j]
            return pltpu.make_async_remote_copy(src_ref=ins[i].at[c], dst_ref=outs[i].at[k, c], send_sem=ici_send.at[3 * i + j],
                                                recv_sem=ici_recv.at[3 * i + j], device_id=(cx, cy, c), device_id_type=MESH)

        def d2d(i, j, h):
            cx, cy = chips[j]
            slot = outs[i].at[2 * cx + cy, h]
            return pltpu.make_async_remote_copy(src_ref=slot, dst_ref=slot, send_sem=d2d_send.at[3 * i + j],
                                                recv_sem=d2d_recv.at[3 * i + j], device_id=sibling, device_id_type=MESH)

        sends = [ici(i, j, me) for i in range(n) for j in range(3)]
        for cp in sends:
            cp.start()
        for i in range(n):
            for j, (cx, cy) in enumerate(chips):
                ici(i, j, 2 * cx + cy).wait_recv()
                fwd = d2d(i, j, c)
                fwd.start()
                sends.append(fwd)
        for i in range(n):
            for j in range(3):
                d2d(i, j, 1 - c).wait_recv()
        for cp in sends:
            cp.wait_send()

    return pl.pallas_call(
        body, name=name, in_specs=[ANY] * n, out_specs=[ANY] * n,
        out_shape=[jax.ShapeDtypeStruct((N_CHIPS,) + a.shape, a.dtype) for a in arrs],
        scratch_shapes=[pltpu.SemaphoreType.DMA((3 * n,))] * 4,
    )(*arrs)


HBM = pl.BlockSpec(memory_space=pltpu.HBM)
SEM = pl.BlockSpec(memory_space=pltpu.SEMAPHORE)
TOKEN = jax.ShapeDtypeStruct((SUBLANES, LANES), F32)


_PEERS = {"gather": 3, "scatter": 3, "swap": 1, "all": N_DEV - 1}


def _land_shape(kind, shape):
    if kind == "gather":
        return (N_CHIPS,) + shape
    if kind == "all":
        return (N_DEV,) + shape
    return (N_CHIPS,) + shape[2:] if kind == "swap" else shape


def _peer_copies(kind, flags, src_refs, land_refs, send_sems, recv_sems, receiving):
    x, y, c = _mesh_pos()
    me4, me8 = 2 * x + y, 4 * x + 2 * y + c
    np_ = _PEERS[kind]
    cps = []
    for i, (src, land) in enumerate(zip(src_refs, land_refs)):
        if kind == "swap":
            half = src.at[1 - c] if flags[i] else src.at[:, 1 - c]
            plan = [((x, y, 1 - c), half, land)]
        elif kind == "all":
            masks = [(mx, my, mc) for mx in (0, 1) for my in (0, 1) for mc in (0, 1)][1:]
            peers = [(jnp.where(mx, 1 - x, x), jnp.where(my, 1 - y, y), jnp.where(mc, 1 - c, c)) for mx, my, mc in masks]
            plan = [(p, src, land.at[4 * p[0] + 2 * p[1] + p[2] if receiving else me8]) for p in peers]
        else:
            plan = []
            for cx, cy in _other_chips(x, y):
                k = 2 * cx + cy
                s = src.at[me4 if receiving else k] if kind == "scatter" else src
                plan.append(((cx, cy, c), s, land.at[k if receiving else me4]))
        for j, (peer, s, d) in enumerate(plan):
            cps.append(pltpu.make_async_remote_copy(src_ref=s, dst_ref=d, send_sem=send_sems.at[np_ * i + j],
                                                    recv_sem=recv_sems.at[np_ * i + j], device_id=peer, device_id_type=MESH))
    return cps


def _copies_start(kind, srcs, after, name, flags=None):
    n = len(srcs)
    ns = _PEERS[kind] * n
    lands = [lax.empty(_land_shape(kind, s.shape), s.dtype) for s in srcs]
    after = [] if after is None else [after]

    def body(*refs):
        src_refs, land_refs = refs[:n], refs[n:2 * n]
        send_sems, recv_sems = refs[2 * n + len(after)], refs[2 * n + len(after) + 1]
        token = refs[-1]
        for cp in _peer_copies(kind, flags, src_refs, land_refs, send_sems, recv_sems, False):
            cp.start()
        token[...] = jnp.zeros_like(token)

    outs = pl.pallas_call(
        body, name=name,
        in_specs=[HBM] * (2 * n) + [ANY] * len(after),
        out_specs=(SEM, SEM) + (HBM,) * (2 * n) + (pl.BlockSpec(memory_space=pltpu.VMEM),),
        out_shape=(pltpu.SemaphoreType.DMA((ns,)), pltpu.SemaphoreType.DMA((ns,)))
        + tuple(pltpu.HBM(a.shape, a.dtype) for a in list(srcs) + lands) + (TOKEN,),
        input_output_aliases={i: 2 + i for i in range(2 * n)},
        compiler_params=pltpu.CompilerParams(has_side_effects=pltpu.SideEffectType.DATAFLOW_SIDE_EFFECTING),
    )(*[pltpu.with_memory_space_constraint(a, pltpu.HBM) for a in list(srcs) + lands], *after)
    return dict(sems=outs[:2], srcs=outs[2:2 + n], lands=outs[2 + n:2 + 2 * n], token=outs[-1], kind=kind, flags=flags)


def _copies_wait(started, after, name):
    n = len(started["srcs"])
    kind, flags = started["kind"], started["flags"]
    after = list(after) if isinstance(after, (list, tuple)) else [after]

    def body(*refs):
        src_refs, land_refs = refs[:n], refs[n:2 * n]
        send_sems, recv_sems = refs[2 * n], refs[2 * n + 1]
        for cp in _peer_copies(kind, flags, src_refs, land_refs, send_sems, recv_sems, True):
            cp.wait_send()
            cp.wait_recv()

    outs = pl.pallas_call(
        body, name=name,
        in_specs=[HBM] * (2 * n) + [SEM, SEM] + [ANY] * len(after),
        out_specs=(HBM,) * (2 * n),
        out_shape=tuple(pltpu.HBM(a.shape, a.dtype) for a in list(started["srcs"]) + list(started["lands"])),
        input_output_aliases={i: i for i in range(2 * n)},
        compiler_params=pltpu.CompilerParams(has_side_effects=pltpu.SideEffectType.DATAFLOW_SIDE_EFFECTING),
    )(*started["srcs"], *started["lands"], *started["sems"], *after)
    return outs[:n], outs[n:]


def _swap_whole(arrs, name):
    n = len(arrs)

    def body(*refs):
        ins, outs = refs[:n], refs[n:2 * n]
        send_sems, recv_sems = refs[2 * n:]
        x, y, c = _mesh_pos()
        cps = [pltpu.make_async_remote_copy(src_ref=ins[i], dst_ref=outs[i], send_sem=send_sems.at[i],
                                            recv_sem=recv_sems.at[i], device_id=(x, y, 1 - c), device_id_type=MESH)
               for i in range(n)]
        for cp in cps:
            cp.start()
        for cp in cps:
            cp.wait()

    return pl.pallas_call(
        body, name=name, in_specs=[ANY] * n, out_specs=[ANY] * n,
        out_shape=[jax.ShapeDtypeStruct(a.shape, a.dtype) for a in arrs],
        scratch_shapes=[pltpu.SemaphoreType.DMA((n,)), pltpu.SemaphoreType.DMA((n,))],
    )(*arrs)


def _as_rows(a, lead):
    shp = a.shape
    rows = 1
    for s in shp[lead:-1]:
        rows *= s
    return a.reshape(shp[:lead] + (rows, shp[-1]))


def _row_tile(rows, cols, n_bufs):
    budget = (24 * 1024 * 1024) // (n_bufs * 2 * 4 * cols)
    return _pick(rows, max(2 * SUBLANES, budget), 2 * SUBLANES)


def _sum_devices(own, got, dev, name):
    n, rows, cols = got.shape
    tr = _row_tile(rows, cols, n + 2)

    def body(dev_ref, own_ref, got_ref, o_ref):
        mine = own_ref[...]
        acc = jnp.where(dev_ref[0] == 0, mine, got_ref[0])
        for k in range(1, n):
            acc = acc + jnp.where(dev_ref[0] == k, mine, got_ref[k])
        o_ref[...] = acc

    return pl.pallas_call(
        body, name=name,
        grid_spec=pltpu.PrefetchScalarGridSpec(
            num_scalar_prefetch=1, grid=(rows // tr,),
            in_specs=[pl.BlockSpec((tr, cols), lambda i, d: (i, 0)), pl.BlockSpec((n, tr, cols), lambda i, d: (0, i, 0))],
            out_specs=pl.BlockSpec((tr, cols), lambda i, d: (i, 0))),
        out_shape=jax.ShapeDtypeStruct((rows, cols), F32), compiler_params=_cp("parallel"),
    )(_scalar(dev), own, got)


def _scalar(i):
    return jnp.reshape(i, (1,)).astype(jnp.int32)


def _add_own_half(g, other, c, half_first, name):
    _, rows, cols = other.shape
    tr = _row_tile(rows, cols, 3)

    def body(c_ref, g_ref, o_ref, out_ref):
        out_ref[0] = (g_ref[0, 0] + o_ref[0]).astype(out_ref.dtype)

    if half_first:
        g_map = lambda k, i, c_ref: (c_ref[0], k, i, 0)
    else:
        g_map = lambda k, i, c_ref: (k, c_ref[0], i, 0)
    flat = pl.BlockSpec((1, tr, cols), lambda k, i, c_ref: (k, i, 0))
    return pl.pallas_call(
        body, name=name,
        grid_spec=pltpu.PrefetchScalarGridSpec(
            num_scalar_prefetch=1, grid=(N_CHIPS, rows // tr),
            in_specs=[pl.BlockSpec((1, 1, tr, cols), g_map), flat], out_specs=flat),
        out_shape=jax.ShapeDtypeStruct(other.shape, COMM_DTYPE), compiler_params=_cp("parallel", "parallel"),
    )(_scalar(c), g, other)


def _sum_chips(own, got, chip, name, transpose=False):
    _, rows, cols = own.shape
    tr = rows if transpose else _row_tile(rows, cols, N_CHIPS + 2)

    def body(chip_ref, p_ref, b_ref, o_ref):
        mine = p_ref[0].astype(F32)
        acc = jnp.where(chip_ref[0] == 0, mine, b_ref[0].astype(F32))
        for k in range(1, N_CHIPS):
            acc = acc + jnp.where(chip_ref[0] == k, mine, b_ref[k].astype(F32))
        o_ref[...] = acc.T if transpose else acc

    if transpose:
        out_spec, out_shape = pl.BlockSpec((cols, rows), lambda i, k_ref: (0, 0)), (cols, rows)
    else:
        out_spec, out_shape = pl.BlockSpec((tr, cols), lambda i, k_ref: (i, 0)), (rows, cols)
    return pl.pallas_call(
        body, name=name,
        grid_spec=pltpu.PrefetchScalarGridSpec(
            num_scalar_prefetch=1, grid=(rows // tr,),
            in_specs=[pl.BlockSpec((1, tr, cols), lambda i, k_ref: (k_ref[0], i, 0)),
                      pl.BlockSpec((N_CHIPS, tr, cols), lambda i, k_ref: (0, i, 0))],
            out_specs=out_spec),
        out_shape=jax.ShapeDtypeStruct(out_shape, F32), compiler_params=_cp("parallel"),
    )(_scalar(chip), own, got)


def _adam_math(w, g, m, v):
    nm = ADAM_B1 * m + (1.0 - ADAM_B1) * g
    nv = ADAM_B2 * v + (1.0 - ADAM_B2) * (g * g)
    m_hat = nm / (1.0 - ADAM_B1 ** ADAM_STEP)
    v_hat = nv / (1.0 - ADAM_B2 ** ADAM_STEP)
    return -ADAM_LR * (m_hat / (jnp.sqrt(v_hat) + ADAM_EPS) + ADAM_WD * w), nm, nv


def _adamw_pieces(w, m, v, mine, theirs, c, kind, name):
    shape = w.shape
    P = len(mine)
    ws, ms, vs = (t.reshape((P, -1, t.shape[-1])) for t in (w, m, v))
    _, R, C = ws.shape
    if kind == "rows":
        tr = _pick(R // 2, 512, SUBLANES)
    else:
        tr = _pick(R, 256 if kind in ("lo", "hi") else 512, SUBLANES)
    nt = R // tr
    nh = nt // 2

    def body(c_ref, w_ref, m_ref, v_ref, *refs):
        mine_refs, theirs_refs = refs[:P], refs[P:2 * P]
        g_ref, d_ref, nm_ref, nv_ref = refs[2 * P:]
        p, i, core = pl.program_id(0), pl.program_id(1), c_ref[0]

        def pick(refs_):
            out = refs_[0][...]
            for q in range(1, P):
                out = jnp.where(p == q, refs_[q][...], out)
            return out

        a, b = pick(mine_refs), pick(theirs_refs)
        if kind == "cols":
            gv = jnp.where(core == 0, jnp.concatenate([a, b], axis=1), jnp.concatenate([b, a], axis=1))
        else:
            own = {"lo": core == 0, "hi": core == 1, "rows": (i >= nh) == (core == 1)}[kind]
            gv = jnp.where(own, a, b)
        g_ref[0] = gv
        d_ref[0], nm_ref[0], nv_ref[0] = _adam_math(w_ref[0], gv, m_ref[0], v_ref[0])

    def piece_spec(q):
        tile = (lambda i: i - jnp.where(i >= nh, nh, 0)) if kind == "rows" else (lambda i: i)
        return pl.BlockSpec((tr, mine[q].shape[1]), lambda p, i, c_ref: (jnp.where(p == q, tile(i), 0), 0))

    full = pl.BlockSpec((1, tr, C), lambda p, i, c_ref: (p, i, 0))
    outs = pl.pallas_call(
        body, name=name,
        grid_spec=pltpu.PrefetchScalarGridSpec(num_scalar_prefetch=1, grid=(P, nt),
                                               in_specs=[full] * 3 + [piece_spec(q) for q in range(P)] * 2,
                                               out_specs=[full] * 4),
        out_shape=[jax.ShapeDtypeStruct((P, R, C), F32)] * 4, compiler_params=_cp("parallel", "arbitrary"),
    )(_scalar(c), ws, ms, vs, *mine, *theirs)
    return tuple(o.reshape(shape) for o in outs)


def _adamw(w, g, m, v, name):
    shape = w.shape
    ws, gs, ms, vs = (_as_rows(t, 0) for t in (w, g, m, v))
    rows, cols = ws.shape
    tr = _row_tile(rows, cols, 7)

    def body(w_ref, g_ref, m_ref, v_ref, d_ref, nm_ref, nv_ref):
        d_ref[...], nm_ref[...], nv_ref[...] = _adam_math(w_ref[...], g_ref[...], m_ref[...], v_ref[...])

    spec = pl.BlockSpec((tr, cols), lambda i: (i, 0))
    outs = pl.pallas_call(body, name=name, grid=(rows // tr,), in_specs=[spec] * 4, out_specs=[spec] * 3,
                          out_shape=[jax.ShapeDtypeStruct((rows, cols), F32)] * 3, compiler_params=_cp("parallel"))(ws, gs, ms, vs)
    return tuple(o.reshape(shape) for o in outs)


_BIG = ["ffn_w_gate", "ffn_w_up", "ffn_w_down", "dn_w_in", "dn_w_out", "sg_w_in", "sg_w_out"]
_SMALL_SHARDED = ["norm_g", "dn_conv_w", "sg_b_in", "sg_ln_g", "sg_ln_b"]
_SMALL_REPL = ["dn_a_log", "dn_dt_bias", "dn_norm_g", "sg_w_s", "sg_b_s"]
_WEIGHTS = ["norm_g", "ffn_w_gate", "ffn_w_up", "ffn_w_down", "dn_w_in", "dn_conv_w", "dn_a_log", "dn_dt_bias",
            "dn_norm_g", "dn_w_out", "sg_w_in", "sg_b_in", "sg_ln_g", "sg_ln_b", "sg_w_s", "sg_b_s", "sg_w_out"]
PACK_COLS = 1024


def _pack(arrs):
    flat = jnp.concatenate([a.reshape(-1) for a in arrs])
    pad = (-flat.shape[0]) % (SUBLANES * PACK_COLS)
    return jnp.pad(flat, (0, pad)).reshape(-1, PACK_COLS)


def _unpack(buf, shapes):
    flat = buf.reshape(-1)
    out, off = [], 0
    for s in shapes:
        n = math.prod(s)
        out.append(flat[off:off + n].reshape(s))
        off += n
    return out


def _as_halves(a):
    if a.shape[0] == 2:
        return a
    if a.shape[0] == 1:
        return a.reshape((2, a.shape[1] // 2) + a.shape[2:])
    return a.reshape((2, a.shape[0] // 2) + a.shape[1:])


def _with_own(gathered, own, chip):
    g = gathered.reshape((N_CHIPS,) + own.shape)
    return [jnp.where(chip == k, own, g[k]) for k in range(N_CHIPS)]


def _cat_shards(g, axis):
    return jnp.concatenate(list(g), axis=axis)


_GROUP_ORDER = ["ffn00", "dn", "ffn01", "ffn10", "sg", "ffn11"]


def _weight_groups(w):
    cast = {k: _mx(w[k]) for k in _BIG}
    groups = {"ffn%d%d" % (i, j): [cast["ffn_w_gate"][i, j].T, cast["ffn_w_up"][i, j].T, cast["ffn_w_down"][i, j]]
              for i, j in [(0, 0), (0, 1), (1, 0), (1, 1)]}
    groups["dn"] = [cast["dn_w_in"][0], cast["dn_w_out"][0]]
    groups["sg"] = [cast["sg_w_in"][0], cast["sg_w_out"][0]]
    return groups


def _ffn_weights(chip, own, gathered):
    pairs = [(a, g.reshape((N_CHIPS,) + a.shape)) for a, g in zip(own, gathered)]
    return {"chip": chip, "gate": pairs[0], "up": pairs[1], "down": pairs[2]}


def _group_matrices(group, shards):
    if group == "sg":
        return {"sg_win": _cat_shards(shards[0], 1), "sg_wout": _cat_shards(shards[1], 0)}
    dn_full = _cat_shards(shards[0], 1)
    W4 = 4 * DN_HEADS * DN_HEAD_DIM
    wba = jnp.zeros((D_MODEL, 2 * LANES), dn_full.dtype)
    wba = wba.at[:, :DN_HEADS].set(dn_full[:, W4:W4 + DN_HEADS])
    wba = wba.at[:, LANES:LANES + DN_HEADS].set(dn_full[:, W4 + DN_HEADS:])
    return {"dn_wqkvz": dn_full[:, :W4], "dn_wba": wba, "dn_wout": _cat_shards(shards[1], 0)}


def _split_cols(a, n):
    w = a.shape[-1] // n
    return [a[..., k * w:(k + 1) * w] for k in range(n)]


def _split_rows(a, n):
    h = a.shape[-2] // n
    return [a[..., k * h:(k + 1) * h, :] for k in range(n)]


_IJ = [(0, 0), (0, 1), (1, 0), (1, 1)]


def _group_grads(group, grads):
    def rows_by_chip(a):
        return a.reshape(N_CHIPS, 2, a.shape[0] // (2 * N_CHIPS), a.shape[1])

    if group.startswith("ffn"):
        tag = group[3:]
        t = grads["wguT" + tag]
        return (["wguT" + tag, "wd" + tag],
                [t.reshape(2, N_CHIPS, t.shape[0] // (2 * N_CHIPS), t.shape[1]), rows_by_chip(grads["wd" + tag])], [True, False])
    if group == "sg":
        return ["sg_w_inT", "sg_w_out"], [rows_by_chip(grads["sg_w_inT"]), rows_by_chip(grads["sg_w_out"])], [False, False]
    dn_in = jnp.stack([jnp.stack(_split_cols(hf, N_CHIPS)) for hf in _split_rows(grads["dn_w_in"], 2)])
    return ["dn_w_in", "dn_w_out"], [dn_in, rows_by_chip(grads["dn_w_out"])], [True, False]


_SHARD_PIECES = {
    "ffn_w_gate": (["wguT%d%d" % ij for ij in _IJ], "lo"),
    "ffn_w_up": (["wguT%d%d" % ij for ij in _IJ], "hi"),
    "ffn_w_down": (["wd%d%d" % ij for ij in _IJ], "rows"),
    "dn_w_in": (["dn_w_in"], "rows"),
    "dn_w_out": (["dn_w_out"], "rows"),
    "sg_w_in": (["sg_w_inT"], "cols"),
    "sg_w_out": (["sg_w_out"], "rows"),
}


def kernel(x, norm_g, ffn_w_gate, ffn_w_up, ffn_w_down, dn_w_in, dn_conv_w, dn_a_log, dn_dt_bias, dn_norm_g, dn_w_out, sg_w_in, sg_b_in, sg_ln_g, sg_ln_b, sg_w_s, sg_b_s, sg_w_out, loss_target, m_norm_g, m_ffn_w_gate, m_ffn_w_up, m_ffn_w_down, m_dn_w_in, m_dn_conv_w, m_dn_a_log, m_dn_dt_bias, m_dn_norm_g, m_dn_w_out, m_sg_w_in, m_sg_b_in, m_sg_ln_g, m_sg_ln_b, m_sg_w_s, m_sg_b_s, m_sg_w_out, v_norm_g, v_ffn_w_gate, v_ffn_w_up, v_ffn_w_down, v_dn_w_in, v_dn_conv_w, v_dn_a_log, v_dn_dt_bias, v_dn_norm_g, v_dn_w_out, v_sg_w_in, v_sg_b_in, v_sg_ln_g, v_sg_ln_b, v_sg_w_s, v_sg_b_s, v_sg_w_out):
    args = dict(locals())
    w = {k: args[k] for k in _WEIGHTS}
    mom = {k: args["m_" + k] for k in _WEIGHTS}
    var = {k: args["v_" + k] for k in _WEIGHTS}
    cx, cy, cc = _mesh_pos()
    chip = 2 * cx + cy

    small_shapes = [w[k].shape for k in _SMALL_SHARDED]
    groups = _weight_groups(w)
    own = groups[_GROUP_ORDER[0]] + [_pack([w[k] for k in _SMALL_SHARDED])]
    first = _allgather_chips([_as_halves(a) for a in own], "gather_first")
    started, after = {}, first[0]
    for g in _GROUP_ORDER[1:]:
        started[g] = _copies_start("gather", groups[g], after, "gather_start_" + g)
        after = started[g]["token"]
    small_k = [_unpack(pack, small_shapes) for pack in _with_own(first[-1], own[-1], chip)]
    p = {name: jnp.concatenate([small_k[k][i] for k in range(N_CHIPS)], axis=-1) for i, name in enumerate(_SMALL_SHARDED)}
    p = {k: (v if k == "norm_g" else v[0]) for k, v in p.items()}
    p["norm_g"] = p["norm_g"] + after[0, 0]
    for k in _SMALL_REPL:
        p[k] = w[k][0]

    def weights_for(group, after):
        if group == _GROUP_ORDER[0]:
            return _ffn_weights(chip, own[:-1], first[:-1])
        srcs, lands = _copies_wait(started[group], after, "gather_wait_" + group)
        if group.startswith("ffn"):
            return _ffn_weights(chip, srcs, lands)
        return _group_matrices(group, [_with_own(l, a, chip) for l, a in zip(lands, srcs)])

    mine, theirs, to_core, to_chips = {}, {}, [], []

    def send_to_chips(after):
        group, names, flags, swap = to_core.pop(0)
        halves, got = _copies_wait(swap, after, "swap_wait_" + group)
        pair_sum = [_add_own_half(h, o, cc, hf, "pair_sum_" + n) for n, h, o, hf in zip(names, halves, got, flags)]
        scatter = _copies_start("scatter", pair_sum, got[0], "reduce_start_" + group)
        to_chips.append((group, names, scatter))
        return scatter["token"]

    def finish(after):
        group, names, scatter = to_chips.pop(0)
        pair_sum, got = _copies_wait(scatter, after, "reduce_wait_" + group)
        half_sum = [_sum_chips(a, b, chip, "chip_sum_" + n, transpose="T" in n)
                    for n, a, b in zip(names, pair_sum, got)]
        other = _swap_whole(half_sum, "gather_core_pair_" + group)
        mine.update(zip(names, half_sum))
        theirs.update(zip(names, other))

    def grads_ready(group, grads):
        names, halves, flags = _group_grads(group, grads)
        swap = _copies_start("swap", halves, None, "swap_start_" + group, flags)
        token = swap["token"]
        if to_core:
            token = send_to_chips(token)
            if len(to_chips) > 1:
                finish(token)
        to_core.append((group, names, flags, swap))
        return token[0, 0]

    small_names = _SMALL_SHARDED + _SMALL_REPL
    small = {}

    def small_ready(grads, loss_part):
        parts = [grads[k] for k in small_names]
        small["shapes"] = [g.shape for g in parts] + [(1,)]
        pack = _pack(parts + [loss_part[0, :1]])
        small["exchange"] = _copies_start("all", [pack], None, "small_start")
        return small["exchange"]["token"]

    loss_part, grad_x, grads = _local_step(x[0], loss_target[0], p, weights_for, grads_ready, small_ready)
    token = send_to_chips(to_core[0][3]["token"])
    finish(token)
    (pack,), (packs,) = _copies_wait(small["exchange"], list(theirs.values()), "small_wait")
    summed = _sum_devices(pack, packs, 4 * cx + 2 * cy + cc, "small_sum")
    parts = _unpack(summed, small["shapes"])
    loss = parts[-1][0]
    grad = {}
    for i, k in enumerate(small_names):
        g = parts[i]
        if k in _SMALL_SHARDED:
            n = w[k].shape[-1]
            g = lax.dynamic_slice_in_dim(g, chip * n, n, axis=g.ndim - 1)
        grad[k] = g

    delta, new_m, new_v = {}, {}, {}

    def update(keys):
        for k in keys:
            names, kind = _SHARD_PIECES[k]
            grad[k], delta[k], new_m[k], new_v[k] = _adamw_pieces(
                w[k], mom[k], var[k], [mine[n] for n in names], [theirs[n] for n in names], cc, kind, "adamw_" + k)

    shapes = [w[k].shape for k in small_names]
    d, nm, nv = _adamw(_pack([w[k] for k in small_names]), _pack([grad[k] for k in small_names]),
                       _pack([mom[k] for k in small_names]), _pack([var[k] for k in small_names]), "adamw_small")
    for k, a, b, c_ in zip(small_names, _unpack(d, shapes), _unpack(nm, shapes), _unpack(nv, shapes)):
        delta[k], new_m[k], new_v[k] = a, b, c_
    mixers = [k for k in _BIG if not k.startswith("ffn")]
    update(mixers)
    finish([d] + [delta[k] for k in mixers] + list(theirs.values()))
    update([k for k in _BIG if k.startswith("ffn")])

    return (loss, grad_x[None], *[grad[k] for k in _WEIGHTS], *[delta[k] for k in _WEIGHTS],
            *[new_m[k] for k in _WEIGHTS], *[new_v[k] for k in _WEIGHTS])
```

```python
import functools
import math

import jax
import jax.numpy as jnp
from jax import lax
from jax.experimental import pallas as pl
from jax.experimental.pallas import tpu as pltpu

F32 = jnp.float32
MXU_DTYPE = jnp.bfloat16
COMM_DTYPE = jnp.bfloat16
HI = lax.Precision.HIGHEST
TRI_PREC = lax.Precision.HIGH

D_MODEL = 1024
D_FF = 2816
RMS_EPS = 1e-6
LN_EPS = 1e-5
L2_EPS = 1e-6
DN_HEADS = 8
DN_HEAD_DIM = 128
DN_CONV = 4
DN_CHUNK = 64
SG_WIDTH = 2048
SG_GROUPS = 8
SG_CHUNK = 128
SG_GROUP_W = SG_WIDTH // SG_GROUPS
N_CHIPS = 4
N_DEV = 8
LANES = 128
SUBLANES = 8
VMEM_LIMIT = 56 * 1024 * 1024

ADAM_LR = 0.001
ADAM_B1 = 0.9
ADAM_B2 = 0.999
ADAM_EPS = 1e-08
ADAM_WD = 0.01
ADAM_STEP = 10

MESH = pl.DeviceIdType.MESH
ANY = pl.BlockSpec(memory_space=pl.ANY)


def _cp(*sem):
    return pltpu.CompilerParams(dimension_semantics=sem, vmem_limit_bytes=VMEM_LIMIT)


def _pick(n, pref, mult=LANES):
    best = None
    d = mult
    while d <= min(n, pref):
        if n % d == 0:
            best = d
        d += mult
    return best if best is not None else n


def _full(shape):
    nd = len(shape)
    return pl.BlockSpec(shape, lambda *_: (0,) * nd)


def _sigmoid(x):
    return 1.0 / (1.0 + jnp.exp(-x))


def _dot(a, b, dims, prec=None):
    return lax.dot_general(a, b, (dims, ((), ())), preferred_element_type=F32, precision=prec)


NN = ((1,), (0,))
NT = ((1,), (1,))
TN = ((0,), (0,))


def _mx(a):
    return a.astype(MXU_DTYPE)


def _rms_stat(x):
    return lax.rsqrt(jnp.mean(x * x, axis=-1, keepdims=True) + RMS_EPS)


def _rms_bwd(x, r, g, dy):
    xh = x * r
    dxh = dy * g
    dx = r * (dxh - xh * jnp.mean(dxh * xh, axis=-1, keepdims=True))
    return dx, jnp.sum(dy * xh, axis=0, keepdims=True)


def _mm(a, b, mode, name, out_dtype=F32, add=None, after=None):
    if mode == "tn":
        K, M = a.shape
        N = b.shape[1]
    elif mode == "nt":
        M, K = a.shape
        N = b.shape[0]
    else:
        M, K = a.shape
        N = b.shape[1]
    tn = _pick(N, 1024)
    if mode == "tn":
        tm = _pick(M, 1024 if tn <= 512 else 1408)
        tk = _pick(K, 1024, SUBLANES)
    else:
        tm = _pick(M, max(512, min(2048, (1024 * 1024) // tn)), SUBLANES)
        tk = _pick(K, 2048)
    nk = K // tk
    grid = (N // tn, M // tm, nk)
    if mode == "nn":
        a_spec = pl.BlockSpec((tm, tk), lambda j, i, k: (i, k))
        b_spec = pl.BlockSpec((tk, tn), lambda j, i, k: (k, j))
        dims = NN
    elif mode == "nt":
        a_spec = pl.BlockSpec((tm, tk), lambda j, i, k: (i, k))
        b_spec = pl.BlockSpec((tn, tk), lambda j, i, k: (j, k))
        dims = NT
    else:
        a_spec = pl.BlockSpec((tk, tm), lambda j, i, k: (k, i))
        b_spec = pl.BlockSpec((tk, tn), lambda j, i, k: (k, j))
        dims = TN
    o_spec = pl.BlockSpec((tm, tn), lambda j, i, k: (i, j))
    has_add = add is not None

    def body(*refs):
        a_ref, b_ref = refs[:2]
        add_ref = refs[2] if has_add else None
        o_ref, acc = refs[-2:]
        k = pl.program_id(2)

        @pl.when(k == 0)
        def _():
            acc[...] = add_ref[...] if has_add else jnp.zeros_like(acc)

        acc[...] += _dot(a_ref[...], b_ref[...], dims)

        @pl.when(k == nk - 1)
        def _():
            o_ref[...] = acc[...].astype(o_ref.dtype)

    ins = [a, b] + ([add] if has_add else []) + ([after] if after is not None else [])
    specs = [a_spec, b_spec] + ([o_spec] if has_add else []) + ([ANY] if after is not None else [])
    return pl.pallas_call(
        body, name=name, grid=grid, in_specs=specs, out_specs=o_spec,
        out_shape=jax.ShapeDtypeStruct((M, N), out_dtype),
        scratch_shapes=[pltpu.VMEM((tm, tn), F32)],
        compiler_params=_cp("parallel", "parallel", "arbitrary"),
    )(*ins)


def _ffn_weight_operands(wt):
    return [_scalar(wt["chip"])] , [wt["gate"][0], wt["gate"][1], wt["up"][0], wt["up"][1], wt["down"][0], wt["down"][1]]


def _load_ffn_weights(chip_ref, shard_refs, wgu_v, wd_v, sem):
    fs = wd_v.shape[0] // N_CHIPS

    @pl.when(pl.program_id(0) == 0)
    def _():
        me = chip_ref[0]
        waits = []
        for t, (dst, base) in enumerate([(wgu_v, 0), (wgu_v, wd_v.shape[0]), (wd_v, 0)]):
            own, gathered = shard_refs[2 * t], shard_refs[2 * t + 1]
            for k in range(N_CHIPS):
                slot = dst.at[pl.ds(base + k * fs, fs), :]
                s = sem.at[t * N_CHIPS + k]

                @pl.when(me == k)
                def _(own=own, slot=slot, s=s):
                    pltpu.make_async_copy(own, slot, s).start()

                @pl.when(me != k)
                def _(gathered=gathered, k=k, slot=slot, s=s):
                    pltpu.make_async_copy(gathered.at[k], slot, s).start()

                waits.append(pltpu.make_async_copy(own, slot, s))
        for cp in waits:
            cp.wait()


def _ffn_fwd(x, g0, g1, wt, name):
    T, D = x.shape
    F = N_CHIPS * wt["down"][0].shape[0]
    F2 = 2 * F
    tm = _pick(T, 256, SUBLANES)
    prefetch, shards = _ffn_weight_operands(wt)

    def body(chip_ref, x_ref, g0_ref, g1_ref, *refs):
        shard_refs = refs[:6]
        xo_ref, h_ref, gu_ref, y_ref, wgu_v, wd_v, sem = refs[6:]
        _load_ffn_weights(chip_ref, shard_refs, wgu_v, wd_v, sem)
        xv = x_ref[...]
        hb = _mx(xv * _rms_stat(xv) * g0_ref[...])
        h_ref[...] = hb
        gu = _dot(hb, wgu_v[...], NT)
        gu_ref[...] = gu.astype(gu_ref.dtype)
        g = gu[:, :F]
        u = gu[:, F:]
        a = _mx(g * _sigmoid(g) * u)
        y = _dot(a, wd_v[...], NN)
        y_ref[...] = y
        xo_ref[...] = xv + 0.5 * (y * _rms_stat(y) * g1_ref[...])

    row = lambda w: pl.BlockSpec((tm, w), lambda i, c: (i, 0))
    one = pl.BlockSpec((1, D), lambda i, c: (0, 0))
    return pl.pallas_call(
        body, name=name,
        grid_spec=pltpu.PrefetchScalarGridSpec(
            num_scalar_prefetch=1, grid=(T // tm,),
            in_specs=[row(D), one, one] + [ANY] * 6,
            out_specs=[row(D), row(D), row(F2), row(D)],
            scratch_shapes=[pltpu.VMEM((F2, D), MXU_DTYPE), pltpu.VMEM((F, D), MXU_DTYPE),
                            pltpu.SemaphoreType.DMA((3 * N_CHIPS,))]),
        out_shape=[jax.ShapeDtypeStruct((T, D), F32), jax.ShapeDtypeStruct((T, D), MXU_DTYPE),
                   jax.ShapeDtypeStruct((T, F2), MXU_DTYPE), jax.ShapeDtypeStruct((T, D), F32)],
        compiler_params=_cp("arbitrary"),
    )(*prefetch, x, g0, g1, *shards)


FFN_BWD_CHUNK = 2816


def _ffn_bwd(dxo, x, y, gu, g0, g1, wt, name):
    T, D = x.shape
    F2 = gu.shape[1]
    F = F2 // 2
    tm = _pick(T, 256, SUBLANES)
    fc = _pick(F, FFN_BWD_CHUNK)
    prefetch, shards = _ffn_weight_operands(wt)

    def body(chip_ref, dxo_ref, x_ref, y_ref, gu_ref, g0_ref, g1_ref, *refs):
        shard_refs = refs[:6]
        dx_ref, dy_ref, a_ref, dgu_ref, dg0_ref, dg1_ref, wgu_v, wd_v, sem = refs[6:]
        _load_ffn_weights(chip_ref, shard_refs, wgu_v, wd_v, sem)

        @pl.when(pl.program_id(0) == 0)
        def _():
            dg0_ref[...] = jnp.zeros_like(dg0_ref)
            dg1_ref[...] = jnp.zeros_like(dg1_ref)

        dxo_v = dxo_ref[...]
        yv = y_ref[...]
        dy, dg1 = _rms_bwd(yv, _rms_stat(yv), g1_ref[...], 0.5 * dxo_v)
        dg1_ref[...] += dg1
        dyb = _mx(dy)
        dy_ref[...] = dyb
        dh = jnp.zeros((tm, D), F32)
        for c in range(F // fc):
            lo, hi = c * fc, (c + 1) * fc
            da = _dot(dyb, wd_v[lo:hi, :], NT)
            g = gu_ref[:, lo:hi].astype(F32)
            u = gu_ref[:, F + lo:F + hi].astype(F32)
            s = _sigmoid(g)
            sg = g * s
            a_ref[:, lo:hi] = _mx(sg * u)
            dg = _mx(da * u * (s * (1.0 + g * (1.0 - s))))
            du = _mx(da * sg)
            dgu_ref[:, lo:hi] = dg
            dgu_ref[:, F + lo:F + hi] = du
            dh = dh + _dot(dg, wgu_v[lo:hi, :], NN) + _dot(du, wgu_v[F + lo:F + hi, :], NN)
        xv = x_ref[...]
        dx, dg0 = _rms_bwd(xv, _rms_stat(xv), g0_ref[...], dh)
        dg0_ref[...] += dg0
        dx_ref[...] = dxo_v + dx

    row = lambda w: pl.BlockSpec((tm, w), lambda i, c: (i, 0))
    one = pl.BlockSpec((1, D), lambda i, c: (0, 0))
    return pl.pallas_call(
        body, name=name,
        grid_spec=pltpu.PrefetchScalarGridSpec(
            num_scalar_prefetch=1, grid=(T // tm,),
            in_specs=[row(D), row(D), row(D), row(F2), one, one] + [ANY] * 6,
            out_specs=[row(D), row(D), row(F), row(F2), one, one],
            scratch_shapes=[pltpu.VMEM((F2, D), MXU_DTYPE), pltpu.VMEM((F, D), MXU_DTYPE),
                            pltpu.SemaphoreType.DMA((3 * N_CHIPS,))]),
        out_shape=[jax.ShapeDtypeStruct((T, D), F32), jax.ShapeDtypeStruct((T, D), MXU_DTYPE),
                   jax.ShapeDtypeStruct((T, F), MXU_DTYPE), jax.ShapeDtypeStruct((T, F2), MXU_DTYPE),
                   jax.ShapeDtypeStruct((1, D), F32), jax.ShapeDtypeStruct((1, D), F32)],
        compiler_params=_cp("arbitrary"),
    )(*prefetch, dxo, x, y, gu, g0, g1, *shards)


def _norm_fwd(x, g, name):
    T, D = x.shape
    tm = _pick(T, 512, SUBLANES)

    def body(x_ref, g_ref, h_ref):
        xv = x_ref[...]
        h_ref[...] = _mx(xv * _rms_stat(xv) * g_ref[...])

    row = pl.BlockSpec((tm, D), lambda i: (i, 0))
    return pl.pallas_call(body, name=name, grid=(T // tm,), in_specs=[row, _full((1, D))], out_specs=row,
                          out_shape=jax.ShapeDtypeStruct((T, D), MXU_DTYPE), compiler_params=_cp("parallel"))(x, g)


def _postnorm_fwd(x, m, g, name):
    T, D = x.shape
    tm = _pick(T, 512, SUBLANES)

    def body(x_ref, m_ref, g_ref, o_ref):
        mv = m_ref[...]
        o_ref[...] = x_ref[...] + mv * _rms_stat(mv) * g_ref[...]

    row = pl.BlockSpec((tm, D), lambda i: (i, 0))
    return pl.pallas_call(body, name=name, grid=(T // tm,), in_specs=[row, row, _full((1, D))], out_specs=row,
                          out_shape=jax.ShapeDtypeStruct((T, D), F32), compiler_params=_cp("parallel"))(x, m, g)


def _postnorm_bwd(dxo, m, g, name):
    T, D = m.shape
    tm = _pick(T, 512, SUBLANES)

    def body(dxo_ref, m_ref, g_ref, dm_ref, dg_ref):
        @pl.when(pl.program_id(0) == 0)
        def _():
            dg_ref[...] = jnp.zeros_like(dg_ref)

        mv = m_ref[...]
        dm, dg = _rms_bwd(mv, _rms_stat(mv), g_ref[...], dxo_ref[...])
        dg_ref[...] += dg
        dm_ref[...] = _mx(dm)

    row = pl.BlockSpec((tm, D), lambda i: (i, 0))
    return pl.pallas_call(body, name=name, grid=(T // tm,), in_specs=[row, row, _full((1, D))],
                          out_specs=[row, _full((1, D))],
                          out_shape=[jax.ShapeDtypeStruct((T, D), MXU_DTYPE), jax.ShapeDtypeStruct((1, D), F32)],
                          compiler_params=_cp("arbitrary"))(dxo, m, g)


def _prenorm_bwd(dxo, dh, x, g, name):
    T, D = x.shape
    tm = _pick(T, 512, SUBLANES)

    def body(dxo_ref, dh_ref, x_ref, g_ref, dx_ref, dg_ref):
        @pl.when(pl.program_id(0) == 0)
        def _():
            dg_ref[...] = jnp.zeros_like(dg_ref)

        xv = x_ref[...]
        dx, dg = _rms_bwd(xv, _rms_stat(xv), g_ref[...], dh_ref[...])
        dg_ref[...] += dg
        dx_ref[...] = dxo_ref[...] + dx

    row = pl.BlockSpec((tm, D), lambda i: (i, 0))
    return pl.pallas_call(body, name=name, grid=(T // tm,), in_specs=[row, row, row, _full((1, D))],
                          out_specs=[row, _full((1, D))],
                          out_shape=[jax.ShapeDtypeStruct((T, D), F32), jax.ShapeDtypeStruct((1, D), F32)],
                          compiler_params=_cp("arbitrary"))(dxo, dh, x, g)


def _loss_fwd_bwd(y, target, name):
    T, D = y.shape
    tm = _pick(T, 512, SUBLANES)

    def body(y_ref, t_ref, l_ref, dy_ref):
        @pl.when(pl.program_id(0) == 0)
        def _():
            l_ref[...] = jnp.zeros_like(l_ref)

        e = y_ref[...] - t_ref[...]
        dy_ref[...] = e * (1.0 / D)
        l_ref[...] += 0.5 * jnp.sum(jnp.mean(e * e, axis=-1, keepdims=True), axis=0, keepdims=True)

    row = pl.BlockSpec((tm, D), lambda i: (i, 0))
    return pl.pallas_call(body, name=name, grid=(T // tm,), in_specs=[row, row],
                          out_specs=[_full((SUBLANES, LANES)), row],
                          out_shape=[jax.ShapeDtypeStruct((SUBLANES, LANES), F32), jax.ShapeDtypeStruct((T, D), F32)],
                          compiler_params=_cp("arbitrary"))(y, target)


DN_ROWS = 512


def _shift_down(prev8, cur, s):
    n = cur.shape[0]
    xx = jnp.concatenate([prev8, cur], axis=0)
    return pltpu.roll(xx, s, 0)[SUBLANES:SUBLANES + n, :]


def _shift_up(cur, next8, s):
    n = cur.shape[0]
    xx = jnp.concatenate([cur, next8], axis=0)
    return pltpu.roll(xx, n + SUBLANES - s, 0)[:n, :]


def _conv_tile(x_ref, w, r, rows):
    start = pl.multiple_of(r * rows, SUBLANES)
    cur = x_ref[pl.ds(start, rows), :]
    pstart = pl.multiple_of(jnp.maximum(start - SUBLANES, 0), SUBLANES)
    prev8 = jnp.where(r == 0, 0.0, x_ref[pl.ds(pstart, SUBLANES), :])
    taps = [_shift_down(prev8, cur, DN_CONV - 1 - j) if j < DN_CONV - 1 else cur for j in range(DN_CONV)]
    c = taps[0] * w[0:1, :]
    for j in range(1, DN_CONV):
        c = c + taps[j] * w[j:j + 1, :]
    return c, taps


def _dn_prep_fwd(proj, conv_w, name):
    T = proj.shape[0]
    W = DN_HEADS * DN_HEAD_DIM
    rows = min(DN_ROWS, T)
    n_inner = T // rows
    scale = DN_HEAD_DIM ** -0.5

    def body(x_ref, w_ref, o_ref):
        cb = pl.program_id(0)
        w = w_ref[...]
        is_qk = cb < 2 * DN_HEADS
        post = jnp.where(cb < DN_HEADS, scale, 1.0)

        def step(r, carry):
            c, _ = _conv_tile(x_ref, w, r, rows)
            s = c * _sigmoid(c)
            rinv = lax.rsqrt(jnp.sum(s * s, axis=-1, keepdims=True) + L2_EPS)
            o_ref[pl.ds(pl.multiple_of(r * rows, SUBLANES), rows), :] = jnp.where(is_qk, s * rinv * post, s)
            return carry

        lax.fori_loop(0, n_inner, step, 0)

    col = pl.BlockSpec((T, LANES), lambda j: (0, j))
    return pl.pallas_call(body, name=name, grid=(3 * W // LANES,),
                          in_specs=[col, pl.BlockSpec((DN_CONV, LANES), lambda j: (0, j))], out_specs=col,
                          out_shape=jax.ShapeDtypeStruct((T, 3 * W), F32), compiler_params=_cp("parallel"))(proj, conv_w)


def _dn_prep_bwd(proj, conv_w, dqkv, name):
    T = proj.shape[0]
    W = DN_HEADS * DN_HEAD_DIM
    rows = min(DN_ROWS, T)
    n_inner = T // rows
    scale = DN_HEAD_DIM ** -0.5

    def body(x_ref, w_ref, dy_ref, dx_ref, dw_ref, dc_scr):
        cb = pl.program_id(0)
        w = w_ref[...]
        is_qk = cb < 2 * DN_HEADS
        post = jnp.where(cb < DN_HEADS, scale, 1.0)

        def step1(r, dws):
            c, taps = _conv_tile(x_ref, w, r, rows)
            sg = _sigmoid(c)
            s = c * sg
            rinv = lax.rsqrt(jnp.sum(s * s, axis=-1, keepdims=True) + L2_EPS)
            dy = dy_ref[pl.ds(pl.multiple_of(r * rows, SUBLANES), rows), :]
            yn = s * rinv
            dyn = dy * post
            ds_qk = rinv * (dyn - yn * jnp.sum(dyn * yn, axis=-1, keepdims=True))
            ds = jnp.where(is_qk, ds_qk, dy)
            dc = ds * (sg * (1.0 + c * (1.0 - sg)))
            dc_scr[pl.ds(pl.multiple_of(r * rows, SUBLANES), rows), :] = dc
            return tuple(dws[j] + jnp.sum(dc * taps[j], axis=0, keepdims=True) for j in range(DN_CONV))

        zero = jnp.zeros((1, LANES), F32)
        dws = lax.fori_loop(0, n_inner, step1, (zero,) * DN_CONV)
        for j in range(DN_CONV):
            dw_ref[j:j + 1, :] = dws[j]

        def step2(r, carry):
            start = pl.multiple_of(r * rows, SUBLANES)
            cur = dc_scr[pl.ds(start, rows), :]
            nstart = pl.multiple_of(jnp.minimum(start + rows, T - SUBLANES), SUBLANES)
            next8 = jnp.where(r == n_inner - 1, 0.0, dc_scr[pl.ds(nstart, SUBLANES), :])
            dx = cur * w[DN_CONV - 1:DN_CONV, :]
            for j in range(DN_CONV - 1):
                dx = dx + _shift_up(cur, next8, DN_CONV - 1 - j) * w[j:j + 1, :]
            dx_ref[pl.ds(start, rows), :] = _mx(dx)
            return carry

        lax.fori_loop(0, n_inner, step2, 0)

    col = pl.BlockSpec((T, LANES), lambda j: (0, j))
    wspec = pl.BlockSpec((DN_CONV, LANES), lambda j: (0, j))
    return pl.pallas_call(body, name=name, grid=(3 * W // LANES,), in_specs=[col, wspec, col], out_specs=[col, wspec],
                          out_shape=[jax.ShapeDtypeStruct((T, 3 * W), MXU_DTYPE), jax.ShapeDtypeStruct((DN_CONV, 3 * W), F32)],
                          scratch_shapes=[pltpu.VMEM((T, LANES), F32)], compiler_params=_cp("parallel"))(proj, conv_w, dqkv)


def _softplus(x):
    return jnp.maximum(x, 0.0) + jnp.log(1.0 + jnp.exp(-jnp.abs(x)))


def _dn_gate_fwd(ba, a_log, dt_bias, name):
    T = ba.shape[0]
    tm = _pick(T, 1024, SUBLANES)

    def body(ba_ref, al_ref, dt_ref, beta_ref, g_ref):
        beta_ref[...] = _sigmoid(ba_ref[:, :LANES])
        g_ref[...] = -jnp.exp(al_ref[...]) * _softplus(ba_ref[:, LANES:] + dt_ref[...])

    row = lambda w: pl.BlockSpec((tm, w), lambda i: (i, 0))
    return pl.pallas_call(body, name=name, grid=(T // tm,), in_specs=[row(2 * LANES), _full((1, LANES)), _full((1, LANES))],
                          out_specs=[row(LANES), row(LANES)],
                          out_shape=[jax.ShapeDtypeStruct((T, LANES), F32)] * 2, compiler_params=_cp("parallel"))(ba, a_log, dt_bias)


def _dn_gate_bwd(ba, a_log, dt_bias, dbeta, dg, name):
    T = ba.shape[0]
    tm = _pick(T, 1024, SUBLANES)

    def body(ba_ref, al_ref, dt_ref, dbeta_ref, dg_ref, dba_ref, dal_ref, ddt_ref):
        @pl.when(pl.program_id(0) == 0)
        def _():
            dal_ref[...] = jnp.zeros_like(dal_ref)
            ddt_ref[...] = jnp.zeros_like(ddt_ref)

        beta = _sigmoid(ba_ref[:, :LANES])
        dba_ref[:, :LANES] = _mx(dbeta_ref[...] * beta * (1.0 - beta))
        pre = ba_ref[:, LANES:] + dt_ref[...]
        ea = jnp.exp(al_ref[...])
        dgv = dg_ref[...]
        da = dgv * (-ea) * _sigmoid(pre)
        dba_ref[:, LANES:] = _mx(da)
        ddt_ref[...] += jnp.sum(da, axis=0, keepdims=True)
        dal_ref[...] += jnp.sum(dgv * (-ea) * _softplus(pre), axis=0, keepdims=True)

    row = lambda w: pl.BlockSpec((tm, w), lambda i: (i, 0))
    one = _full((1, LANES))
    return pl.pallas_call(body, name=name, grid=(T // tm,), in_specs=[row(2 * LANES), one, one, row(LANES), row(LANES)],
                          out_specs=[row(2 * LANES), one, one],
                          out_shape=[jax.ShapeDtypeStruct((T, 2 * LANES), MXU_DTYPE), jax.ShapeDtypeStruct((1, LANES), F32),
                                     jax.ShapeDtypeStruct((1, LANES), F32)],
                          compiler_params=_cp("arbitrary"))(ba, a_log, dt_bias, dbeta, dg)


def _tri(c, strict):
    i = lax.broadcasted_iota(jnp.int32, (c, c), 0)
    j = lax.broadcasted_iota(jnp.int32, (c, c), 1)
    return (i > j) if strict else (i >= j)


def _inv_unit_lower(ls):
    c = ls[0].shape[0]
    i = lax.broadcasted_iota(jnp.int32, (c, c), 0)
    j = lax.broadcasted_iota(jnp.int32, (c, c), 1)
    eye = jnp.where(i == j, 1.0, 0.0)
    facs = [[eye - l for l in ls]]
    cur = ls
    for _ in range(int(math.log2(c)) - 1):
        cur = [_dot(p, p, NN, TRI_PREC) for p in cur]
        facs.append([eye + p for p in cur])
    while len(facs) > 1:
        nxt = [[_dot(a, b, NN, TRI_PREC) for a, b in zip(facs[t], facs[t + 1])] for t in range(0, len(facs) - 1, 2)]
        if len(facs) % 2:
            nxt.append(facs[-1])
        facs = nxt
    return facs[0]


def _chunk_gates(g_blk):
    c = g_blk.shape[0]
    gcs = _dot(jnp.where(_tri(c, False), 1.0, 0.0), g_blk, NN, HI)
    return gcs, gcs.T


def _head_chunk(h, qh, kh, vh, beta_blk, gcs, gcs_t):
    c = qh.shape[0]
    incl = _tri(c, False)
    gc_col = gcs[:, h:h + 1]
    gc_row = gcs_t[h:h + 1, :]
    gc_last = gcs_t[h:h + 1, c - 1:c]
    dec = jnp.where(incl, jnp.exp(jnp.where(incl, gc_col - gc_row, 0.0)), 0.0)
    gam = jnp.exp(gc_col)
    rr = jnp.exp(gc_last - gc_col)
    gl = jnp.exp(gc_last)
    b = beta_blk[:, h:h + 1]
    kb = kh * b
    vb = vh * b
    kk = _dot(_mx(kb), _mx(kh), NT)
    lmat = jnp.where(_tri(c, True), kk * dec, 0.0)
    qk = _dot(_mx(qh), _mx(kh), NT)
    pmat = jnp.where(incl, qk * dec, 0.0)
    return dict(dec=dec, gam=gam, rr=rr, gl=gl, b=b, kb=kb, vb=vb, lmat=lmat, pmat=pmat)


def _dn_scan_fwd(qkv, beta, g, proj, norm_g, name):
    T = qkv.shape[0]
    C, H, Dh = DN_CHUNK, DN_HEADS, DN_HEAD_DIM
    W = H * Dh
    N = T // C

    def body(q_ref, k_ref, v_ref, beta_ref, g_ref, z_ref, ng_ref, og_ref, o_ref, tinv_ref, s_ref, state):
        @pl.when(pl.program_id(0) == 0)
        def _():
            state[...] = jnp.zeros_like(state)

        gcs, gcs_t = _chunk_gates(g_ref[...])
        beta_blk = beta_ref[...]
        ng = ng_ref[...]
        heads = range(H)
        cs = [slice(h * Dh, (h + 1) * Dh) for h in heads]
        qs = [_head_chunk(h, q_ref[:, cs[h]], k_ref[:, cs[h]], v_ref[:, cs[h]], beta_blk, gcs, gcs_t) for h in heads]
        tinvs = _inv_unit_lower([q["lmat"] for q in qs])
        for h in heads:
            tinv_ref[h] = tinvs[h]
        us = [_dot(tinvs[h], qs[h]["vb"], NN, TRI_PREC) for h in heads]
        ws = [_dot(tinvs[h], qs[h]["kb"] * qs[h]["gam"], NN, TRI_PREC) for h in heads]
        ss = [state[h] for h in heads]
        for h in heads:
            s_ref[0, h] = ss[h]
        sbs = [_mx(s) for s in ss]
        vnbs = [_mx(us[h] - _dot(_mx(ws[h]), sbs[h], NN)) for h in heads]
        os_ = [_dot(_mx(q_ref[:, cs[h]] * qs[h]["gam"]), sbs[h], NN) + _dot(_mx(qs[h]["pmat"]), vnbs[h], NN) for h in heads]
        for h in heads:
            state[h] = ss[h] * qs[h]["gl"] + _dot(_mx((k_ref[:, cs[h]] * qs[h]["rr"]).T), vnbs[h], NN)
        for h in heads:
            o = os_[h]
            o_ref[:, cs[h]] = o
            zh = z_ref[:, cs[h]]
            og_ref[:, cs[h]] = _mx(o * _rms_stat(o) * ng * (zh * _sigmoid(zh)))

    blk = lambda j: pl.BlockSpec((C, W), lambda n: (n, j))
    small = pl.BlockSpec((C, LANES), lambda n: (n, 0))
    return pl.pallas_call(
        body, name=name, grid=(N,),
        in_specs=[blk(0), blk(1), blk(2), small, small, blk(3), _full((1, Dh))],
        out_specs=[blk(0), blk(0), pl.BlockSpec((H, C, C), lambda n: (0, n, 0)),
                   pl.BlockSpec((1, H, Dh, Dh), lambda n: (n, 0, 0, 0))],
        out_shape=[jax.ShapeDtypeStruct((T, W), MXU_DTYPE), jax.ShapeDtypeStruct((T, W), F32),
                   jax.ShapeDtypeStruct((H, T, C), F32), jax.ShapeDtypeStruct((N, H, Dh, Dh), F32)],
        scratch_shapes=[pltpu.VMEM((H, Dh, Dh), F32)],
        compiler_params=_cp("arbitrary"),
    )(qkv, qkv, qkv, beta, g, proj, norm_g)


def _dn_scan_bwd(qkv, beta, g, proj, norm_g, o, tinv, s_all, dog, name):
    T = qkv.shape[0]
    C, H, Dh = DN_CHUNK, DN_HEADS, DN_HEAD_DIM
    W = H * Dh
    N = T // C

    def body(q_ref, k_ref, v_ref, beta_ref, g_ref, z_ref, ng_ref, o_ref, tinv_ref, s_ref, dog_ref,
             dqkv_ref, dbeta_ref, dg_ref, dz_ref, dng_ref, dstate):
        @pl.when(pl.program_id(0) == 0)
        def _():
            dstate[...] = jnp.zeros_like(dstate)
            dng_ref[...] = jnp.zeros_like(dng_ref)

        gcs, gcs_t = _chunk_gates(g_ref[...])
        beta_blk = beta_ref[...]
        ng = ng_ref[...]
        incl = _tri(C, False)
        strict = _tri(C, True)
        lane = lax.broadcasted_iota(jnp.int32, (C, LANES), 1)
        rowi = lax.broadcasted_iota(jnp.int32, (C, 1), 0)
        ones = jnp.ones((C, LANES), F32)
        dbeta_acc = jnp.zeros((C, LANES), F32)
        dgc_acc = jnp.zeros((C, LANES), F32)
        dng_acc = jnp.zeros((1, Dh), F32)
        cs = [slice(h * Dh, (h + 1) * Dh) for h in range(H)]
        rsum = lambda t: jnp.sum(t, axis=1, keepdims=True)
        for heads in (range(0, H // 2), range(H // 2, H)):
            dobs = {}
            for h in heads:
                oh, zh, dogh = o_ref[:, cs[h]], z_ref[:, cs[h]], dog_ref[:, cs[h]]
                rstat = _rms_stat(oh)
                sz = _sigmoid(zh)
                dz_ref[:, cs[h]] = _mx(dogh * (oh * rstat * ng) * (sz * (1.0 + zh * (1.0 - sz))))
                do, dng = _rms_bwd(oh, rstat, ng, dogh * (zh * sz))
                dng_acc = dng_acc + dng
                dobs[h] = _mx(do)
            qs = {h: _head_chunk(h, q_ref[:, cs[h]], k_ref[:, cs[h]], v_ref[:, cs[h]], beta_blk, gcs, gcs_t) for h in heads}
            tms = {h: tinv_ref[h] for h in heads}
            us = {h: _dot(tms[h], qs[h]["vb"], NN, TRI_PREC) for h in heads}
            ws = {h: _dot(tms[h], qs[h]["kb"] * qs[h]["gam"], NN, TRI_PREC) for h in heads}
            ss = {h: s_ref[0, h] for h in heads}
            sbs = {h: _mx(ss[h]) for h in heads}
            wbs = {h: _mx(ws[h]) for h in heads}
            vnbs = {h: _mx(us[h] - _dot(wbs[h], sbs[h], NN)) for h in heads}
            dsns = {h: dstate[h] for h in heads}
            dsbs = {h: _mx(dsns[h]) for h in heads}
            dvnews = {h: _dot(_mx(qs[h]["pmat"]), dobs[h], TN) + _dot(_mx(k_ref[:, cs[h]] * qs[h]["rr"]), dsbs[h], NN)
                      for h in heads}
            dvb16s = {h: _mx(dvnews[h]) for h in heads}
            dps = {h: jnp.where(incl, _dot(dobs[h], vnbs[h], NT), 0.0) for h in heads}
            dqds = {h: _dot(dobs[h], sbs[h], NT) for h in heads}
            dkds = {h: _dot(vnbs[h], dsbs[h], NT) for h in heads}
            dgls = {h: jnp.sum(rsum(ss[h] * dsns[h]), axis=0, keepdims=True) for h in heads}
            dws = {h: -_dot(dvb16s[h], sbs[h], NT) for h in heads}
            for h in heads:
                dstate[h] = (_dot(_mx(q_ref[:, cs[h]] * qs[h]["gam"]), dobs[h], TN) + qs[h]["gl"] * dsns[h]
                             - _dot(wbs[h], dvb16s[h], TN))
            dvbs = {h: _dot(tms[h], dvnews[h], TN, TRI_PREC) for h in heads}
            dkbgs = {h: _dot(tms[h], dws[h], TN, TRI_PREC) for h in heads}
            dls = {h: jnp.where(strict, -(_dot(dvbs[h], us[h], NT, TRI_PREC) + _dot(dkbgs[h], ws[h], NT, TRI_PREC)), 0.0)
                   for h in heads}
            mmats = {h: dls[h] * qs[h]["lmat"] + dps[h] * qs[h]["pmat"] for h in heads}
            dgcs = {h: rsum(mmats[h]) - _dot(mmats[h], ones, TN, HI)[:, :1] for h in heads}
            dkk16s = {h: _mx(dls[h] * qs[h]["dec"]) for h in heads}
            dqk16s = {h: _mx(dps[h] * qs[h]["dec"]) for h in heads}
            for h in heads:
                q = qs[h]
                qh, kh, vh = q_ref[:, cs[h]], k_ref[:, cs[h]], v_ref[:, cs[h]]
                gam, rr, b, kb = q["gam"], q["rr"], q["b"], q["kb"]
                dkb = _dot(dkk16s[h], _mx(kh), NN) + dkbgs[h] * gam
                dk = _dot(dkk16s[h], _mx(kb), TN) + _dot(dqk16s[h], _mx(qh), TN) + dkb * b + dkds[h] * rr
                dq = _dot(dqk16s[h], _mx(kh), NN) + dqds[h] * gam
                dgam = rsum(dkbgs[h] * kb) + rsum(dqds[h] * qh)
                dr = rsum(dkds[h] * kh)
                dgc_last = jnp.sum(dr * rr, axis=0, keepdims=True) + dgls[h] * q["gl"]
                dgc = dgcs[h] + dgam * gam - dr * rr + jnp.where(rowi == C - 1, dgc_last, 0.0)
                dbeta = rsum(dvbs[h] * vh) + rsum(dkb * kh)
                dqkv_ref[:, cs[h]] = dq
                dqkv_ref[:, W + h * Dh:W + (h + 1) * Dh] = dk
                dqkv_ref[:, 2 * W + h * Dh:2 * W + (h + 1) * Dh] = dvbs[h] * b
                dbeta_acc = jnp.where(lane == h, dbeta, dbeta_acc)
                dgc_acc = jnp.where(lane == h, dgc, dgc_acc)
        dbeta_ref[...] = dbeta_acc
        dg_ref[...] = _dot(jnp.where(incl, 1.0, 0.0), dgc_acc, TN, HI)
        dng_ref[...] += dng_acc

    rev = lambda n: N - 1 - n
    blk = lambda j: pl.BlockSpec((C, W), lambda n: (rev(n), j))
    small = pl.BlockSpec((C, LANES), lambda n: (rev(n), 0))
    return pl.pallas_call(
        body, name=name, grid=(N,),
        in_specs=[blk(0), blk(1), blk(2), small, small, blk(3), _full((1, Dh)), blk(0),
                  pl.BlockSpec((H, C, C), lambda n: (0, rev(n), 0)),
                  pl.BlockSpec((1, H, Dh, Dh), lambda n: (rev(n), 0, 0, 0)), blk(0)],
        out_specs=[pl.BlockSpec((C, 3 * W), lambda n: (rev(n), 0)), small, small, blk(0), _full((1, Dh))],
        out_shape=[jax.ShapeDtypeStruct((T, 3 * W), F32), jax.ShapeDtypeStruct((T, LANES), F32),
                   jax.ShapeDtypeStruct((T, LANES), F32), jax.ShapeDtypeStruct((T, W), MXU_DTYPE),
                   jax.ShapeDtypeStruct((1, Dh), F32)],
        scratch_shapes=[pltpu.VMEM((H, Dh, Dh), F32)],
        compiler_params=_cp("arbitrary"),
    )(qkv, qkv, qkv, beta, g, proj, norm_g, o, tinv, s_all, dog)


_INV_SQRT2 = 0.7071067811865476
_INV_SQRT_2PI = 0.3989422804014327


def _sg_recompute(zp_ref, bin_ref, lng_ref, lnb_ref):
    E = SG_WIDTH
    zin = zp_ref[...] + bin_ref[...]
    cdf = 0.5 * (1.0 + lax.erf(zin * _INV_SQRT2))
    zz = zin * cdf
    u = zz[:, :E]
    vp = zz[:, E:]
    mu = jnp.mean(vp, axis=-1, keepdims=True)
    xc = vp - mu
    rstd = lax.rsqrt(jnp.mean(xc * xc, axis=-1, keepdims=True) + LN_EPS)
    xhat = xc * rstd
    v = xhat * lng_ref[...] + lnb_ref[...]
    return zin, cdf, u, xhat, rstd, v


def _sg_masked_ws(ws_ref, g):
    return _mx(jnp.where(_tri(SG_CHUNK, False), ws_ref[g], 0.0))


def _sg_fwd(zpre, b_in, ln_g, ln_b, w_s, b_s_t, name):
    T = zpre.shape[0]
    E, G, C, GW = SG_WIDTH, SG_GROUPS, SG_CHUNK, SG_GROUP_W

    def body(zp_ref, bin_ref, lng_ref, lnb_ref, ws_ref, bst_ref, um_ref):
        _, _, u, _, _, v = _sg_recompute(zp_ref, bin_ref, lng_ref, lnb_ref)
        bst = bst_ref[...]
        for g in range(G):
            cs = slice(g * GW, (g + 1) * GW)
            mixed = _dot(_sg_masked_ws(ws_ref, g), _mx(v[:, cs]), NN) + bst[:, g:g + 1]
            um_ref[:, cs] = _mx(u[:, cs] * mixed)

    return pl.pallas_call(
        body, name=name, grid=(T // C,),
        in_specs=[pl.BlockSpec((C, 2 * E), lambda n: (n, 0)), _full((1, 2 * E)), _full((1, E)), _full((1, E)),
                  _full((G, C, C)), _full((C, LANES))],
        out_specs=pl.BlockSpec((C, E), lambda n: (n, 0)),
        out_shape=jax.ShapeDtypeStruct((T, E), MXU_DTYPE), compiler_params=_cp("parallel"),
    )(zpre, b_in, ln_g, ln_b, w_s, b_s_t)


def _sg_bwd(zpre, b_in, ln_g, ln_b, w_s, b_s_t, dum, name):
    T = zpre.shape[0]
    E, G, C, GW = SG_WIDTH, SG_GROUPS, SG_CHUNK, SG_GROUP_W

    def body(zp_ref, bin_ref, lng_ref, lnb_ref, ws_ref, bst_ref, dum_ref,
             dz_ref, dbin_ref, dlng_ref, dlnb_ref, dws_ref, dbst_ref):
        @pl.when(pl.program_id(0) == 0)
        def _():
            for r in (dbin_ref, dlng_ref, dlnb_ref, dws_ref, dbst_ref):
                r[...] = jnp.zeros_like(r)

        zin, cdf, u, xhat, rstd, v = _sg_recompute(zp_ref, bin_ref, lng_ref, lnb_ref)
        bst = bst_ref[...]
        lane = lax.broadcasted_iota(jnp.int32, (C, LANES), 1)
        dum_v = dum_ref[...]
        dbst = jnp.zeros((C, LANES), F32)
        du_parts, dv_parts = [], []
        for g in range(G):
            cs = slice(g * GW, (g + 1) * GW)
            wsm = _sg_masked_ws(ws_ref, g)
            vg = _mx(v[:, cs])
            mixed = _dot(wsm, vg, NN) + bst[:, g:g + 1]
            dumg = dum_v[:, cs]
            du_parts.append(dumg * mixed)
            dmixed = dumg * u[:, cs]
            dmb = _mx(dmixed)
            dv_parts.append(_dot(wsm, dmb, TN))
            dws_ref[g] += _dot(dmb, vg, NT)
            dbst = jnp.where(lane == g, jnp.sum(dmixed, axis=1, keepdims=True), dbst)
        dbst_ref[...] += dbst
        du = jnp.concatenate(du_parts, axis=1)
        dv = jnp.concatenate(dv_parts, axis=1)
        dlng_ref[...] += jnp.sum(dv * xhat, axis=0, keepdims=True)
        dlnb_ref[...] += jnp.sum(dv, axis=0, keepdims=True)
        dxh = dv * lng_ref[...]
        dvp = rstd * (dxh - jnp.mean(dxh, axis=-1, keepdims=True) - xhat * jnp.mean(dxh * xhat, axis=-1, keepdims=True))
        dzz = jnp.concatenate([du, dvp], axis=1)
        dzin = dzz * (cdf + zin * (_INV_SQRT_2PI * jnp.exp(-0.5 * zin * zin)))
        dz_ref[...] = _mx(dzin)
        dbin_ref[...] += jnp.sum(dzin, axis=0, keepdims=True)

    return pl.pallas_call(
        body, name=name, grid=(T // C,),
        in_specs=[pl.BlockSpec((C, 2 * E), lambda n: (n, 0)), _full((1, 2 * E)), _full((1, E)), _full((1, E)),
                  _full((G, C, C)), _full((C, LANES)), pl.BlockSpec((C, E), lambda n: (n, 0))],
        out_specs=[pl.BlockSpec((C, 2 * E), lambda n: (n, 0)), _full((1, 2 * E)), _full((1, E)), _full((1, E)),
                   _full((G, C, C)), _full((C, LANES))],
        out_shape=[jax.ShapeDtypeStruct((T, 2 * E), MXU_DTYPE), jax.ShapeDtypeStruct((1, 2 * E), F32),
                   jax.ShapeDtypeStruct((1, E), F32), jax.ShapeDtypeStruct((1, E), F32),
                   jax.ShapeDtypeStruct((G, C, C), F32), jax.ShapeDtypeStruct((C, LANES), F32)],
        compiler_params=_cp("arbitrary"),
    )(zpre, b_in, ln_g, ln_b, w_s, b_s_t, dum)


def _row(v):
    return v.reshape(1, -1)


def _pad_lanes(v):
    v = v.reshape(1, -1)
    return jnp.pad(v, ((0, 0), (0, LANES - v.shape[1])))


def _local_step(x, target, p, weights_for, grads_ready=None, small_ready=None):
    ng = p["norm_g"]
    grads = {}
    dng = [[None] * 6 for _ in range(2)]
    order = [jnp.zeros((), F32)]

    def tell(group):
        zero = grads_ready(group, grads) if grads_ready is not None else None
        if zero is not None:
            order[0] = zero

    def gain(i, s):
        return _row(ng[i, s]) + order[0]

    def ffn_f(xin, i, j, tag):
        wt = weights_for("ffn" + tag, xin)
        xo, h, gu, y = _ffn_fwd(xin, _row(ng[i, 4 * j]), _row(ng[i, 4 * j + 1]), wt, "ffn_fwd_" + tag)
        return xo, (xin, h, gu, y, wt)

    x1, sv_f00 = ffn_f(x, 0, 0, "00")
    dnw = weights_for("dn", x1)
    hn0 = _norm_fwd(x1, _row(ng[0, 2]), "dn_prenorm")
    proj = _mm(hn0, dnw["dn_wqkvz"], "nn", "dn_proj")
    ba = _mm(hn0, dnw["dn_wba"], "nn", "dn_proj_ba")
    a_log = _pad_lanes(p["dn_a_log"])
    dt_bias = _pad_lanes(p["dn_dt_bias"])
    dn_ng = _row(p["dn_norm_g"])
    qkv = _dn_prep_fwd(proj, p["dn_conv_w"], "dn_prep_fwd")
    beta, gdec = _dn_gate_fwd(ba, a_log, dt_bias, "dn_gate_fwd")
    og, o_raw, tinv, s_all = _dn_scan_fwd(qkv, beta, gdec, proj, dn_ng, "dn_scan_fwd")
    m0 = _mm(og, dnw["dn_wout"], "nn", "dn_out")
    x2 = _postnorm_fwd(x1, m0, _row(ng[0, 3]), "dn_postnorm")
    x3, sv_f01 = ffn_f(x2, 0, 1, "01")
    x4, sv_f10 = ffn_f(x3, 1, 0, "10")
    sgw = weights_for("sg", x4)
    hn1 = _norm_fwd(x4, _row(ng[1, 2]), "sg_prenorm")
    zpre = _mm(hn1, sgw["sg_win"], "nn", "sg_proj")
    sg_bin = _row(p["sg_b_in"])
    sg_lng = _row(p["sg_ln_g"])
    sg_lnb = _row(p["sg_ln_b"])
    sg_bst = jnp.pad(p["sg_b_s"].T, ((0, 0), (0, LANES - SG_GROUPS)))
    um = _sg_fwd(zpre, sg_bin, sg_lng, sg_lnb, p["sg_w_s"], sg_bst, "sg_fwd")
    m1 = _mm(um, sgw["sg_wout"], "nn", "sg_out")
    x5 = _postnorm_fwd(x4, m1, _row(ng[1, 3]), "sg_postnorm")
    x6, sv_f11 = ffn_f(x5, 1, 1, "11")
    loss_part, dx = _loss_fwd_bwd(x6, target, "loss")

    def ffn_b(dxo, sv, i, j, tag, last=False):
        xin, h, gu, y, wt = sv
        dxi, dy, a, dgu, dg0, dg1 = _ffn_bwd(dxo, xin, y, gu, gain(i, 4 * j), gain(i, 4 * j + 1), wt, "ffn_bwd_" + tag)
        dng[i][4 * j] = dg0
        dng[i][4 * j + 1] = dg1
        after = None
        if last:
            grads["norm_g"] = jnp.stack([jnp.concatenate(dng[t], axis=0) for t in range(2)])
            after = small_ready(grads, loss_part) if small_ready is not None else None
        grads["wd" + tag] = _mm(a, dy, "tn", "ffn_wgrad_down_" + tag, after=after)
        grads["wguT" + tag] = _mm(dgu, h, "tn", "ffn_wgrad_up_" + tag, after=after)
        tell("ffn" + tag)
        return dxi

    dx = ffn_b(dx, sv_f11, 1, 1, "11")
    dm1, dng[1][3] = _postnorm_bwd(dx, m1, gain(1, 3), "sg_postnorm_bwd")
    grads["sg_w_out"] = _mm(um, dm1, "tn", "sg_wgrad_out")
    dum = _mm(dm1, sgw["sg_wout"], "nt", "sg_dgrad_out")
    dz1, dbin, dlng, dlnb, dws, dbst = _sg_bwd(zpre, sg_bin, sg_lng, sg_lnb, p["sg_w_s"], sg_bst, dum, "sg_bwd")
    grads["sg_w_inT"] = _mm(dz1, hn1, "tn", "sg_wgrad_in")
    tell("sg")
    dh1 = _mm(dz1, sgw["sg_win"], "nt", "sg_dgrad_in")
    dx, dng[1][2] = _prenorm_bwd(dx, dh1, x4, gain(1, 2), "sg_prenorm_bwd")
    grads["sg_b_in"] = dbin.reshape(1, -1)
    grads["sg_ln_g"] = dlng.reshape(1, -1)
    grads["sg_ln_b"] = dlnb.reshape(1, -1)
    grads["sg_w_s"] = jnp.where(jnp.tril(jnp.ones((SG_CHUNK, SG_CHUNK), bool)), dws, 0.0)[None]
    grads["sg_b_s"] = dbst[:, :SG_GROUPS].T[None]
    dx = ffn_b(dx, sv_f10, 1, 0, "10")
    dx = ffn_b(dx, sv_f01, 0, 1, "01")
    dm0, dng[0][3] = _postnorm_bwd(dx, m0, gain(0, 3), "dn_postnorm_bwd")
    grads["dn_w_out"] = _mm(og, dm0, "tn", "dn_wgrad_out")
    dog = _mm(dm0, dnw["dn_wout"], "nt", "dn_dgrad_out")
    dqkv, dbeta, dgdec, dz0, dnng = _dn_scan_bwd(qkv, beta, gdec, proj, dn_ng, o_raw, tinv, s_all, dog, "dn_scan_bwd")
    dqkv_pre, dconv = _dn_prep_bwd(proj, p["dn_conv_w"], dqkv, "dn_prep_bwd")
    dba, dal, ddt = _dn_gate_bwd(ba, a_log, dt_bias, dbeta, dgdec, "dn_gate_bwd")
    W3 = 3 * DN_HEADS * DN_HEAD_DIM
    dw_qkv = _mm(hn0, dqkv_pre, "tn", "dn_wgrad_qkv")
    dw_z = _mm(hn0, dz0, "tn", "dn_wgrad_z")
    dw_ba = _mm(hn0, dba, "tn", "dn_wgrad_ba")
    grads["dn_w_in"] = jnp.concatenate(
        [dw_qkv, dw_z, dw_ba[:, :DN_HEADS], dw_ba[:, LANES:LANES + DN_HEADS]], axis=1)
    tell("dn")
    dh0 = _mm(dqkv_pre, dnw["dn_wqkvz"][:, :W3], "nt", "dn_dgrad_qkv")
    dh0 = _mm(dz0, dnw["dn_wqkvz"][:, W3:], "nt", "dn_dgrad_z", add=dh0)
    dh0 = _mm(dba, dnw["dn_wba"], "nt", "dn_dgrad_ba", add=dh0)
    dx, dng[0][2] = _prenorm_bwd(dx, dh0, x1, gain(0, 2), "dn_prenorm_bwd")
    grads["dn_conv_w"] = dconv[None]
    grads["dn_a_log"] = dal[:, :DN_HEADS]
    grads["dn_dt_bias"] = ddt[:, :DN_HEADS]
    grads["dn_norm_g"] = dnng
    dx = ffn_b(dx, sv_f00, 0, 0, "00", last=True)
    return loss_part, dx, grads


def _mesh_pos():
    return lax.axis_index("x"), lax.axis_index("y"), lax.axis_index("c")


def _other_chips(x, y):
    return [(1 - x, y), (x, 1 - y), (1 - x, 1 - y)]


def _allgather_chips(arrs, name):
    n = len(arrs)

    def body(*refs):
        ins, outs = refs[:n], refs[n:2 * n]
        ici_send, ici_recv, d2d_send, d2d_recv = refs[2 * n:]
        x, y, c = _mesh_pos()
        me = 2 * x + y
        chips = _other_chips(x, y)
        sibling = (x, y, 1 - c)

        def ici(i, j, k):
            cx, cy = chips[j]
            return pltpu.make_async_remote_copy(src_ref=ins[i].at[c], dst_ref=outs[i].at[k, c], send_sem=ici_send.at[3 * i + j],
                                                recv_sem=ici_recv.at[3 * i + j], device_id=(cx, cy, c), device_id_type=MESH)

        def d2d(i, j, h):
            cx, cy = chips[j]
            slot = outs[i].at[2 * cx + cy, h]
            return pltpu.make_async_remote_copy(src_ref=slot, dst_ref=slot, send_sem=d2d_send.at[3 * i + j],
                                                recv_sem=d2d_recv.at[3 * i + j], device_id=sibling, device_id_type=MESH)

        sends = [ici(i, j, me) for i in range(n) for j in range(3)]
        for cp in sends:
            cp.start()
        for i in range(n):
            for j, (cx, cy) in enumerate(chips):
                ici(i, j, 2 * cx + cy).wait_recv()
                fwd = d2d(i, j, c)
                fwd.start()
                sends.append(fwd)
        for i in range(n):
            for j in range(3):
                d2d(i, j, 1 - c).wait_recv()
        for cp in sends:
            cp.wait_send()

    return pl.pallas_call(
        body, name=name, in_specs=[ANY] * n, out_specs=[ANY] * n,
        out_shape=[jax.ShapeDtypeStruct((N_CHIPS,) + a.shape, a.dtype) for a in arrs],
        scratch_shapes=[pltpu.SemaphoreType.DMA((3 * n,))] * 4,
    )(*arrs)


HBM = pl.BlockSpec(memory_space=pltpu.HBM)
SEM = pl.BlockSpec(memory_space=pltpu.SEMAPHORE)
TOKEN = jax.ShapeDtypeStruct((SUBLANES, LANES), F32)


_PEERS = {"gather": 3, "scatter": 3, "swap": 1, "all": N_DEV - 1}


def _land_shape(kind, shape):
    if kind == "gather":
        return (N_CHIPS,) + shape
    if kind == "all":
        return (N_DEV,) + shape
    return (N_CHIPS,) + shape[2:] if kind == "swap" else shape


def _peer_copies(kind, flags, src_refs, land_refs, send_sems, recv_sems, receiving):
    x, y, c = _mesh_pos()
    me4, me8 = 2 * x + y, 4 * x + 2 * y + c
    np_ = _PEERS[kind]
    cps = []
    for i, (src, land) in enumerate(zip(src_refs, land_refs)):
        if kind == "swap":
            half = src.at[1 - c] if flags[i] else src.at[:, 1 - c]
            plan = [((x, y, 1 - c), half, land)]
        elif kind == "all":
            masks = [(mx, my, mc) for mx in (0, 1) for my in (0, 1) for mc in (0, 1)][1:]
            peers = [(jnp.where(mx, 1 - x, x), jnp.where(my, 1 - y, y), jnp.where(mc, 1 - c, c)) for mx, my, mc in masks]
            plan = [(p, src, land.at[4 * p[0] + 2 * p[1] + p[2] if receiving else me8]) for p in peers]
        else:
            plan = []
            for cx, cy in _other_chips(x, y):
                k = 2 * cx + cy
                s = src.at[me4 if receiving else k] if kind == "scatter" else src
                plan.append(((cx, cy, c), s, land.at[k if receiving else me4]))
        for j, (peer, s, d) in enumerate(plan):
            cps.append(pltpu.make_async_remote_copy(src_ref=s, dst_ref=d, send_sem=send_sems.at[np_ * i + j],
                                                    recv_sem=recv_sems.at[np_ * i + j], device_id=peer, device_id_type=MESH))
    return cps


def _copies_start(kind, srcs, after, name, flags=None):
    n = len(srcs)
    ns = _PEERS[kind] * n
    lands = [lax.empty(_land_shape(kind, s.shape), s.dtype) for s in srcs]
    after = [] if after is None else [after]

    def body(*refs):
        src_refs, land_refs = refs[:n], refs[n:2 * n]
        send_sems, recv_sems = refs[2 * n + len(after)], refs[2 * n + len(after) + 1]
        token = refs[-1]
        for cp in _peer_copies(kind, flags, src_refs, land_refs, send_sems, recv_sems, False):
            cp.start()
        token[...] = jnp.zeros_like(token)

    outs = pl.pallas_call(
        body, name=name,
        in_specs=[HBM] * (2 * n) + [ANY] * len(after),
        out_specs=(SEM, SEM) + (HBM,) * (2 * n) + (pl.BlockSpec(memory_space=pltpu.VMEM),),
        out_shape=(pltpu.SemaphoreType.DMA((ns,)), pltpu.SemaphoreType.DMA((ns,)))
        + tuple(pltpu.HBM(a.shape, a.dtype) for a in list(srcs) + lands) + (TOKEN,),
        input_output_aliases={i: 2 + i for i in range(2 * n)},
        compiler_params=pltpu.CompilerParams(has_side_effects=pltpu.SideEffectType.DATAFLOW_SIDE_EFFECTING),
    )(*[pltpu.with_memory_space_constraint(a, pltpu.HBM) for a in list(srcs) + lands], *after)
    return dict(sems=outs[:2], srcs=outs[2:2 + n], lands=outs[2 + n:2 + 2 * n], token=outs[-1], kind=kind, flags=flags)


def _copies_wait(started, after, name):
    n = len(started["srcs"])
    kind, flags = started["kind"], started["flags"]
    after = list(after) if isinstance(after, (list, tuple)) else [after]

    def body(*refs):
        src_refs, land_refs = refs[:n], refs[n:2 * n]
        send_sems, recv_sems = refs[2 * n], refs[2 * n + 1]
        for cp in _peer_copies(kind, flags, src_refs, land_refs, send_sems, recv_sems, True):
            cp.wait_send()
            cp.wait_recv()

    outs = pl.pallas_call(
        body, name=name,
        in_specs=[HBM] * (2 * n) + [SEM, SEM] + [ANY] * len(after),
        out_specs=(HBM,) * (2 * n),
        out_shape=tuple(pltpu.HBM(a.shape, a.dtype) for a in list(started["srcs"]) + list(started["lands"])),
        input_output_aliases={i: i for i in range(2 * n)},
        compiler_params=pltpu.CompilerParams(has_side_effects=pltpu.SideEffectType.DATAFLOW_SIDE_EFFECTING),
    )(*started["srcs"], *started["lands"], *started["sems"], *after)
    return outs[:n], outs[n:]


def _swap_whole(arrs, name):
    n = len(arrs)

    def body(*refs):
        ins, outs = refs[:n], refs[n:2 * n]
        send_sems, recv_sems = refs[2 * n:]
        x, y, c = _mesh_pos()
        cps = [pltpu.make_async_remote_copy(src_ref=ins[i], dst_ref=outs[i], send_sem=send_sems.at[i],
                                            recv_sem=recv_sems.at[i], device_id=(x, y, 1 - c), device_id_type=MESH)
               for i in range(n)]
        for cp in cps:
            cp.start()
        for cp in cps:
            cp.wait()

    return pl.pallas_call(
        body, name=name, in_specs=[ANY] * n, out_specs=[ANY] * n,
        out_shape=[jax.ShapeDtypeStruct(a.shape, a.dtype) for a in arrs],
        scratch_shapes=[pltpu.SemaphoreType.DMA((n,)), pltpu.SemaphoreType.DMA((n,))],
    )(*arrs)


def _as_rows(a, lead):
    shp = a.shape
    rows = 1
    for s in shp[lead:-1]:
        rows *= s
    return a.reshape(shp[:lead] + (rows, shp[-1]))


def _row_tile(rows, cols, n_bufs):
    budget = (24 * 1024 * 1024) // (n_bufs * 2 * 4 * cols)
    return _pick(rows, max(2 * SUBLANES, budget), 2 * SUBLANES)


def _sum_devices(own, got, dev, name):
    n, rows, cols = got.shape
    tr = _row_tile(rows, cols, n + 2)

    def body(dev_ref, own_ref, got_ref, o_ref):
        mine = own_ref[...]
        acc = jnp.where(dev_ref[0] == 0, mine, got_ref[0])
        for k in range(1, n):
            acc = acc + jnp.where(dev_ref[0] == k, mine, got_ref[k])
        o_ref[...] = acc

    return pl.pallas_call(
        body, name=name,
        grid_spec=pltpu.PrefetchScalarGridSpec(
            num_scalar_prefetch=1, grid=(rows // tr,),
            in_specs=[pl.BlockSpec((tr, cols), lambda i, d: (i, 0)), pl.BlockSpec((n, tr, cols), lambda i, d: (0, i, 0))],
            out_specs=pl.BlockSpec((tr, cols), lambda i, d: (i, 0))),
        out_shape=jax.ShapeDtypeStruct((rows, cols), F32), compiler_params=_cp("parallel"),
    )(_scalar(dev), own, got)


def _scalar(i):
    return jnp.reshape(i, (1,)).astype(jnp.int32)


def _add_own_half(g, other, c, half_first, name):
    _, rows, cols = other.shape
    tr = _row_tile(rows, cols, 3)

    def body(c_ref, g_ref, o_ref, out_ref):
        out_ref[0] = (g_ref[0, 0] + o_ref[0]).astype(out_ref.dtype)

    if half_first:
        g_map = lambda k, i, c_ref: (c_ref[0], k, i, 0)
    else:
        g_map = lambda k, i, c_ref: (k, c_ref[0], i, 0)
    flat = pl.BlockSpec((1, tr, cols), lambda k, i, c_ref: (k, i, 0))
    return pl.pallas_call(
        body, name=name,
        grid_spec=pltpu.PrefetchScalarGridSpec(
            num_scalar_prefetch=1, grid=(N_CHIPS, rows // tr),
            in_specs=[pl.BlockSpec((1, 1, tr, cols), g_map), flat], out_specs=flat),
        out_shape=jax.ShapeDtypeStruct(other.shape, COMM_DTYPE), compiler_params=_cp("parallel", "parallel"),
    )(_scalar(c), g, other)


def _sum_chips(own, got, chip, name, transpose=False):
    _, rows, cols = own.shape
    tr = rows if transpose else _row_tile(rows, cols, N_CHIPS + 2)

    def body(chip_ref, p_ref, b_ref, o_ref):
        mine = p_ref[0].astype(F32)
        acc = jnp.where(chip_ref[0] == 0, mine, b_ref[0].astype(F32))
        for k in range(1, N_CHIPS):
            acc = acc + jnp.where(chip_ref[0] == k, mine, b_ref[k].astype(F32))
        o_ref[...] = acc.T if transpose else acc

    if transpose:
        out_spec, out_shape = pl.BlockSpec((cols, rows), lambda i, k_ref: (0, 0)), (cols, rows)
    else:
        out_spec, out_shape = pl.BlockSpec((tr, cols), lambda i, k_ref: (i, 0)), (rows, cols)
    return pl.pallas_call(
        body, name=name,
        grid_spec=pltpu.PrefetchScalarGridSpec(
            num_scalar_prefetch=1, grid=(rows // tr,),
            in_specs=[pl.BlockSpec((1, tr, cols), lambda i, k_ref: (k_ref[0], i, 0)),
                      pl.BlockSpec((N_CHIPS, tr, cols), lambda i, k_ref: (0, i, 0))],
            out_specs=out_spec),
        out_shape=jax.ShapeDtypeStruct(out_shape, F32), compiler_params=_cp("parallel"),
    )(_scalar(chip), own, got)


def _adam_math(w, g, m, v):
    nm = ADAM_B1 * m + (1.0 - ADAM_B1) * g
    nv = ADAM_B2 * v + (1.0 - ADAM_B2) * (g * g)
    m_hat = nm / (1.0 - ADAM_B1 ** ADAM_STEP)
    v_hat = nv / (1.0 - ADAM_B2 ** ADAM_STEP)
    return -ADAM_LR * (m_hat / (jnp.sqrt(v_hat) + ADAM_EPS) + ADAM_WD * w), nm, nv


def _adamw_pieces(w, m, v, mine, theirs, c, kind, name):
    shape = w.shape
    P = len(mine)
    ws, ms, vs = (t.reshape((P, -1, t.shape[-1])) for t in (w, m, v))
    _, R, C = ws.shape
    if kind == "rows":
        tr = _pick(R // 2, 512, SUBLANES)
    else:
        tr = _pick(R, 256 if kind in ("lo", "hi") else 512, SUBLANES)
    nt = R // tr
    nh = nt // 2

    def body(c_ref, w_ref, m_ref, v_ref, *refs):
        mine_refs, theirs_refs = refs[:P], refs[P:2 * P]
        g_ref, d_ref, nm_ref, nv_ref = refs[2 * P:]
        p, i, core = pl.program_id(0), pl.program_id(1), c_ref[0]

        def pick(refs_):
            out = refs_[0][...]
            for q in range(1, P):
                out = jnp.where(p == q, refs_[q][...], out)
            return out

        a, b = pick(mine_refs), pick(theirs_refs)
        if kind == "cols":
            gv = jnp.where(core == 0, jnp.concatenate([a, b], axis=1), jnp.concatenate([b, a], axis=1))
        else:
            own = {"lo": core == 0, "hi": core == 1, "rows": (i >= nh) == (core == 1)}[kind]
            gv = jnp.where(own, a, b)
        g_ref[0] = gv
        d_ref[0], nm_ref[0], nv_ref[0] = _adam_math(w_ref[0], gv, m_ref[0], v_ref[0])

    def piece_spec(q):
        tile = (lambda i: i - jnp.where(i >= nh, nh, 0)) if kind == "rows" else (lambda i: i)
        return pl.BlockSpec((tr, mine[q].shape[1]), lambda p, i, c_ref: (jnp.where(p == q, tile(i), 0), 0))

    full = pl.BlockSpec((1, tr, C), lambda p, i, c_ref: (p, i, 0))
    outs = pl.pallas_call(
        body, name=name,
        grid_spec=pltpu.PrefetchScalarGridSpec(num_scalar_prefetch=1, grid=(P, nt),
                                               in_specs=[full] * 3 + [piece_spec(q) for q in range(P)] * 2,
                                               out_specs=[full] * 4),
        out_shape=[jax.ShapeDtypeStruct((P, R, C), F32)] * 4, compiler_params=_cp("parallel", "arbitrary"),
    )(_scalar(c), ws, ms, vs, *mine, *theirs)
    return tuple(o.reshape(shape) for o in outs)


def _adamw(w, g, m, v, name):
    shape = w.shape
    ws, gs, ms, vs = (_as_rows(t, 0) for t in (w, g, m, v))
    rows, cols = ws.shape
    tr = _row_tile(rows, cols, 7)

    def body(w_ref, g_ref, m_ref, v_ref, d_ref, nm_ref, nv_ref):
        d_ref[...], nm_ref[...], nv_ref[...] = _adam_math(w_ref[...], g_ref[...], m_ref[...], v_ref[...])

    spec = pl.BlockSpec((tr, cols), lambda i: (i, 0))
    outs = pl.pallas_call(body, name=name, grid=(rows // tr,), in_specs=[spec] * 4, out_specs=[spec] * 3,
                          out_shape=[jax.ShapeDtypeStruct((rows, cols), F32)] * 3, compiler_params=_cp("parallel"))(ws, gs, ms, vs)
    return tuple(o.reshape(shape) for o in outs)


_BIG = ["ffn_w_gate", "ffn_w_up", "ffn_w_down", "dn_w_in", "dn_w_out", "sg_w_in", "sg_w_out"]
_SMALL_SHARDED = ["norm_g", "dn_conv_w", "sg_b_in", "sg_ln_g", "sg_ln_b"]
_SMALL_REPL = ["dn_a_log", "dn_dt_bias", "dn_norm_g", "sg_w_s", "sg_b_s"]
_WEIGHTS = ["norm_g", "ffn_w_gate", "ffn_w_up", "ffn_w_down", "dn_w_in", "dn_conv_w", "dn_a_log", "dn_dt_bias",
            "dn_norm_g", "dn_w_out", "sg_w_in", "sg_b_in", "sg_ln_g", "sg_ln_b", "sg_w_s", "sg_b_s", "sg_w_out"]
PACK_COLS = 1024


def _pack(arrs):
    flat = jnp.concatenate([a.reshape(-1) for a in arrs])
    pad = (-flat.shape[0]) % (SUBLANES * PACK_COLS)
    return jnp.pad(flat, (0, pad)).reshape(-1, PACK_COLS)


def _unpack(buf, shapes):
    flat = buf.reshape(-1)
    out, off = [], 0
    for s in shapes:
        n = math.prod(s)
        out.append(flat[off:off + n].reshape(s))
        off += n
    return out


def _as_halves(a):
    if a.shape[0] == 2:
        return a
    if a.shape[0] == 1:
        return a.reshape((2, a.shape[1] // 2) + a.shape[2:])
    return a.reshape((2, a.shape[0] // 2) + a.shape[1:])


def _with_own(gathered, own, chip):
    g = gathered.reshape((N_CHIPS,) + own.shape)
    return [jnp.where(chip == k, own, g[k]) for k in range(N_CHIPS)]


def _cat_shards(g, axis):
    return jnp.concatenate(list(g), axis=axis)


_GROUP_ORDER = ["ffn00", "dn", "ffn01", "ffn10", "sg", "ffn11"]


def _weight_groups(w):
    cast = {k: _mx(w[k]) for k in _BIG}
    groups = {"ffn%d%d" % (i, j): [cast["ffn_w_gate"][i, j].T, cast["ffn_w_up"][i, j].T, cast["ffn_w_down"][i, j]]
              for i, j in [(0, 0), (0, 1), (1, 0), (1, 1)]}
    groups["dn"] = [cast["dn_w_in"][0], cast["dn_w_out"][0]]
    groups["sg"] = [cast["sg_w_in"][0], cast["sg_w_out"][0]]
    return groups


def _ffn_weights(chip, own, gathered):
    pairs = [(a, g.reshape((N_CHIPS,) + a.shape)) for a, g in zip(own, gathered)]
    return {"chip": chip, "gate": pairs[0], "up": pairs[1], "down": pairs[2]}


def _group_matrices(group, shards):
    if group == "sg":
        return {"sg_win": _cat_shards(shards[0], 1), "sg_wout": _cat_shards(shards[1], 0)}
    dn_full = _cat_shards(shards[0], 1)
    W4 = 4 * DN_HEADS * DN_HEAD_DIM
    wba = jnp.zeros((D_MODEL, 2 * LANES), dn_full.dtype)
    wba = wba.at[:, :DN_HEADS].set(dn_full[:, W4:W4 + DN_HEADS])
    wba = wba.at[:, LANES:LANES + DN_HEADS].set(dn_full[:, W4 + DN_HEADS:])
    return {"dn_wqkvz": dn_full[:, :W4], "dn_wba": wba, "dn_wout": _cat_shards(shards[1], 0)}


def _split_cols(a, n):
    w = a.shape[-1] // n
    return [a[..., k * w:(k + 1) * w] for k in range(n)]


def _split_rows(a, n):
    h = a.shape[-2] // n
    return [a[..., k * h:(k + 1) * h, :] for k in range(n)]


_IJ = [(0, 0), (0, 1), (1, 0), (1, 1)]


def _group_grads(group, grads):
    def rows_by_chip(a):
        return a.reshape(N_CHIPS, 2, a.shape[0] // (2 * N_CHIPS), a.shape[1])

    if group.startswith("ffn"):
        tag = group[3:]
        t = grads["wguT" + tag]
        return (["wguT" + tag, "wd" + tag],
                [t.reshape(2, N_CHIPS, t.shape[0] // (2 * N_CHIPS), t.shape[1]), rows_by_chip(grads["wd" + tag])], [True, False])
    if group == "sg":
        return ["sg_w_inT", "sg_w_out"], [rows_by_chip(grads["sg_w_inT"]), rows_by_chip(grads["sg_w_out"])], [False, False]
    dn_in = jnp.stack([jnp.stack(_split_cols(hf, N_CHIPS)) for hf in _split_rows(grads["dn_w_in"], 2)])
    return ["dn_w_in", "dn_w_out"], [dn_in, rows_by_chip(grads["dn_w_out"])], [True, False]


_SHARD_PIECES = {
    "ffn_w_gate": (["wguT%d%d" % ij for ij in _IJ], "lo"),
    "ffn_w_up": (["wguT%d%d" % ij for ij in _IJ], "hi"),
    "ffn_w_down": (["wd%d%d" % ij for ij in _IJ], "rows"),
    "dn_w_in": (["dn_w_in"], "rows"),
    "dn_w_out": (["dn_w_out"], "rows"),
    "sg_w_in": (["sg_w_inT"], "cols"),
    "sg_w_out": (["sg_w_out"], "rows"),
}


def kernel(x, norm_g, ffn_w_gate, ffn_w_up, ffn_w_down, dn_w_in, dn_conv_w, dn_a_log, dn_dt_bias, dn_norm_g, dn_w_out, sg_w_in, sg_b_in, sg_ln_g, sg_ln_b, sg_w_s, sg_b_s, sg_w_out, loss_target, m_norm_g, m_ffn_w_gate, m_ffn_w_up, m_ffn_w_down, m_dn_w_in, m_dn_conv_w, m_dn_a_log, m_dn_dt_bias, m_dn_norm_g, m_dn_w_out, m_sg_w_in, m_sg_b_in, m_sg_ln_g, m_sg_ln_b, m_sg_w_s, m_sg_b_s, m_sg_w_out, v_norm_g, v_ffn_w_gate, v_ffn_w_up, v_ffn_w_down, v_dn_w_in, v_dn_conv_w, v_dn_a_log, v_dn_dt_bias, v_dn_norm_g, v_dn_w_out, v_sg_w_in, v_sg_b_in, v_sg_ln_g, v_sg_ln_b, v_sg_w_s, v_sg_b_s, v_sg_w_out):
    args = dict(locals())
    w = {k: args[k] for k in _WEIGHTS}
    mom = {k: args["m_" + k] for k in _WEIGHTS}
    var = {k: args["v_" + k] for k in _WEIGHTS}
    cx, cy, cc = _mesh_pos()
    chip = 2 * cx + cy

    small_shapes = [w[k].shape for k in _SMALL_SHARDED]
    groups = _weight_groups(w)
    own = groups[_GROUP_ORDER[0]] + [_pack([w[k] for k in _SMALL_SHARDED])]
    first = _allgather_chips([_as_halves(a) for a in own], "gather_first")
    started, after = {}, first[0]
    for g in _GROUP_ORDER[1:]:
        started[g] = _copies_start("gather", groups[g], after, "gather_start_" + g)
        after = started[g]["token"]
    small_k = [_unpack(pack, small_shapes) for pack in _with_own(first[-1], own[-1], chip)]
    p = {name: jnp.concatenate([small_k[k][i] for k in range(N_CHIPS)], axis=-1) for i, name in enumerate(_SMALL_SHARDED)}
    p = {k: (v if k == "norm_g" else v[0]) for k, v in p.items()}
    p["norm_g"] = p["norm_g"] + after[0, 0]
    for k in _SMALL_REPL:
        p[k] = w[k][0]

    def weights_for(group, after):
        if group == _GROUP_ORDER[0]:
            return _ffn_weights(chip, own[:-1], first[:-1])
        srcs, lands = _copies_wait(started[group], after, "gather_wait_" + group)
        if group.startswith("ffn"):
            return _ffn_weights(chip, srcs, lands)
        return _group_matrices(group, [_with_own(l, a, chip) for l, a in zip(lands, srcs)])

    mine, theirs, to_core, to_chips = {}, {}, [], []

    def send_to_chips(after):
        group, names, flags, swap = to_core.pop(0)
        halves, got = _copies_wait(swap, after, "swap_wait_" + group)
        pair_sum = [_add_own_half(h, o, cc, hf, "pair_sum_" + n) for n, h, o, hf in zip(names, halves, got, flags)]
        scatter = _copies_start("scatter", pair_sum, got[0], "reduce_start_" + group)
        to_chips.append((group, names, scatter))
        return scatter["token"]

    def finish(after):
        group, names, scatter = to_chips.pop(0)
        pair_sum, got = _copies_wait(scatter, after, "reduce_wait_" + group)
        half_sum = [_sum_chips(a, b, chip, "chip_sum_" + n, transpose=n == "sg_w_inT")
                    for n, a, b in zip(names, pair_sum, got)]
        other = _swap_whole(half_sum, "gather_core_pair_" + group)
        mine.update(zip(names, half_sum))
        theirs.update(zip(names, other))

    def grads_ready(group, grads):
        names, halves, flags = _group_grads(group, grads)
        swap = _copies_start("swap", halves, None, "swap_start_" + group, flags)
        token = swap["token"]
        if to_core:
            token = send_to_chips(token)
            if len(to_chips) > 1:
                finish(token)
        to_core.append((group, names, flags, swap))
        return token[0, 0]

    small_names = _SMALL_SHARDED + _SMALL_REPL
    small = {}

    def small_ready(grads, loss_part):
        parts = [grads[k] for k in small_names]
        small["shapes"] = [g.shape for g in parts] + [(1,)]
        pack = _pack(parts + [loss_part[0, :1]])
        small["exchange"] = _copies_start("all", [pack], None, "small_start")
        return small["exchange"]["token"]

    loss_part, grad_x, grads = _local_step(x[0], loss_target[0], p, weights_for, grads_ready, small_ready)
    token = send_to_chips(to_core[0][3]["token"])
    finish(token)
    (pack,), (packs,) = _copies_wait(small["exchange"], list(theirs.values()), "small_wait")
    summed = _sum_devices(pack, packs, 4 * cx + 2 * cy + cc, "small_sum")
    parts = _unpack(summed, small["shapes"])
    loss = parts[-1][0]
    grad = {}
    for i, k in enumerate(small_names):
        g = parts[i]
        if k in _SMALL_SHARDED:
            n = w[k].shape[-1]
            g = lax.dynamic_slice_in_dim(g, chip * n, n, axis=g.ndim - 1)
        grad[k] = g

    delta, new_m, new_v = {}, {}, {}

    def update(keys):
        for k in keys:
            names, kind = _SHARD_PIECES[k]
            turn = (lambda a: jnp.swapaxes(a, -1, -2)) if names[0].startswith("wguT") else (lambda a: a)
            outs = _adamw_pieces(turn(w[k]), turn(mom[k]), turn(var[k]), [mine[n] for n in names], [theirs[n] for n in names],
                                 cc, kind, "adamw_" + k)
            grad[k], delta[k], new_m[k], new_v[k] = (turn(o) for o in outs)

    shapes = [w[k].shape for k in small_names]
    d, nm, nv = _adamw(_pack([w[k] for k in small_names]), _pack([grad[k] for k in small_names]),
                       _pack([mom[k] for k in small_names]), _pack([var[k] for k in small_names]), "adamw_small")
    for k, a, b, c_ in zip(small_names, _unpack(d, shapes), _unpack(nm, shapes), _unpack(nv, shapes)):
        delta[k], new_m[k], new_v[k] = a, b, c_
    mixers = [k for k in _BIG if not k.startswith("ffn")]
    update(mixers)
    finish([d] + [delta[k] for k in mixers] + list(theirs.values()))
    update([k for k in _BIG if k.startswith("ffn")])

    return (loss, grad_x[None], *[grad[k] for k in _WEIGHTS], *[delta[k] for k in _WEIGHTS],
            *[new_m[k] for k in _WEIGHTS], *[new_v[k] for k in _WEIGHTS])
```

```python
import functools
import math

import jax
import jax.numpy as jnp
from jax import lax
from jax.experimental import pallas as pl
from jax.experimental.pallas import tpu as pltpu

F32 = jnp.float32
MXU_DTYPE = jnp.bfloat16
COMM_DTYPE = jnp.bfloat16
HI = lax.Precision.HIGHEST
TRI_PREC = lax.Precision.HIGH

D_MODEL = 1024
D_FF = 2816
RMS_EPS = 1e-6
LN_EPS = 1e-5
L2_EPS = 1e-6
DN_HEADS = 8
DN_HEAD_DIM = 128
DN_CONV = 4
DN_CHUNK = 64
SG_WIDTH = 2048
SG_GROUPS = 8
SG_CHUNK = 128
SG_GROUP_W = SG_WIDTH // SG_GROUPS
N_CHIPS = 4
N_DEV = 8
LANES = 128
SUBLANES = 8
VMEM_LIMIT = 56 * 1024 * 1024

ADAM_LR = 0.001
ADAM_B1 = 0.9
ADAM_B2 = 0.999
ADAM_EPS = 1e-08
ADAM_WD = 0.01
ADAM_STEP = 10

MESH = pl.DeviceIdType.MESH
ANY = pl.BlockSpec(memory_space=pl.ANY)


def _cp(*sem):
    return pltpu.CompilerParams(dimension_semantics=sem, vmem_limit_bytes=VMEM_LIMIT)


def _pick(n, pref, mult=LANES):
    best = None
    d = mult
    while d <= min(n, pref):
        if n % d == 0:
            best = d
        d += mult
    return best if best is not None else n


def _full(shape):
    nd = len(shape)
    return pl.BlockSpec(shape, lambda *_: (0,) * nd)


def _sigmoid(x):
    return 1.0 / (1.0 + jnp.exp(-x))


def _dot(a, b, dims, prec=None):
    return lax.dot_general(a, b, (dims, ((), ())), preferred_element_type=F32, precision=prec)


NN = ((1,), (0,))
NT = ((1,), (1,))
TN = ((0,), (0,))


def _mx(a):
    return a.astype(MXU_DTYPE)


def _rms_stat(x):
    return lax.rsqrt(jnp.mean(x * x, axis=-1, keepdims=True) + RMS_EPS)


def _rms_bwd(x, r, g, dy):
    xh = x * r
    dxh = dy * g
    dx = r * (dxh - xh * jnp.mean(dxh * xh, axis=-1, keepdims=True))
    return dx, jnp.sum(dy * xh, axis=0, keepdims=True)


def _mm(a, b, mode, name, out_dtype=F32, add=None, after=None):
    if mode == "tn":
        K, M = a.shape
        N = b.shape[1]
    elif mode == "nt":
        M, K = a.shape
        N = b.shape[0]
    else:
        M, K = a.shape
        N = b.shape[1]
    tn = _pick(N, 1024)
    if mode == "tn":
        tm = _pick(M, 1024 if tn <= 512 else 1408)
        tk = _pick(K, 1024, SUBLANES)
    else:
        tm = _pick(M, max(512, min(2048, (1024 * 1024) // tn)), SUBLANES)
        tk = _pick(K, 2048)
    nk = K // tk
    grid = (N // tn, M // tm, nk)
    if mode == "nn":
        a_spec = pl.BlockSpec((tm, tk), lambda j, i, k: (i, k))
        b_spec = pl.BlockSpec((tk, tn), lambda j, i, k: (k, j))
        dims = NN
    elif mode == "nt":
        a_spec = pl.BlockSpec((tm, tk), lambda j, i, k: (i, k))
        b_spec = pl.BlockSpec((tn, tk), lambda j, i, k: (j, k))
        dims = NT
    else:
        a_spec = pl.BlockSpec((tk, tm), lambda j, i, k: (k, i))
        b_spec = pl.BlockSpec((tk, tn), lambda j, i, k: (k, j))
        dims = TN
    o_spec = pl.BlockSpec((tm, tn), lambda j, i, k: (i, j))
    has_add = add is not None

    def body(*refs):
        a_ref, b_ref = refs[:2]
        add_ref = refs[2] if has_add else None
        o_ref, acc = refs[-2:]
        k = pl.program_id(2)

        @pl.when(k == 0)
        def _():
            acc[...] = add_ref[...] if has_add else jnp.zeros_like(acc)

        acc[...] += _dot(a_ref[...], b_ref[...], dims)

        @pl.when(k == nk - 1)
        def _():
            o_ref[...] = acc[...].astype(o_ref.dtype)

    ins = [a, b] + ([add] if has_add else []) + ([after] if after is not None else [])
    specs = [a_spec, b_spec] + ([o_spec] if has_add else []) + ([ANY] if after is not None else [])
    return pl.pallas_call(
        body, name=name, grid=grid, in_specs=specs, out_specs=o_spec,
        out_shape=jax.ShapeDtypeStruct((M, N), out_dtype),
        scratch_shapes=[pltpu.VMEM((tm, tn), F32)],
        compiler_params=_cp("parallel", "parallel", "arbitrary"),
    )(*ins)


def _ffn_weight_operands(wt):
    return [_scalar(wt["chip"])] , [wt["gate"][0], wt["gate"][1], wt["up"][0], wt["up"][1], wt["down"][0], wt["down"][1]]


def _load_ffn_weights(chip_ref, shard_refs, wgu_v, wd_v, sem):
    fs = wd_v.shape[0] // N_CHIPS

    @pl.when(pl.program_id(0) == 0)
    def _():
        me = chip_ref[0]
        waits = []
        for t, (dst, base) in enumerate([(wgu_v, 0), (wgu_v, wd_v.shape[0]), (wd_v, 0)]):
            own, gathered = shard_refs[2 * t], shard_refs[2 * t + 1]
            for k in range(N_CHIPS):
                slot = dst.at[pl.ds(base + k * fs, fs), :]
                s = sem.at[t * N_CHIPS + k]

                @pl.when(me == k)
                def _(own=own, slot=slot, s=s):
                    pltpu.make_async_copy(own, slot, s).start()

                @pl.when(me != k)
                def _(gathered=gathered, k=k, slot=slot, s=s):
                    pltpu.make_async_copy(gathered.at[k], slot, s).start()

                waits.append(pltpu.make_async_copy(own, slot, s))
        for cp in waits:
            cp.wait()


def _ffn_fwd(x, g0, g1, wt, name):
    T, D = x.shape
    F = N_CHIPS * wt["down"][0].shape[0]
    F2 = 2 * F
    tm = _pick(T, 256, SUBLANES)
    prefetch, shards = _ffn_weight_operands(wt)

    def body(chip_ref, x_ref, g0_ref, g1_ref, *refs):
        shard_refs = refs[:6]
        xo_ref, h_ref, gu_ref, y_ref, wgu_v, wd_v, sem = refs[6:]
        _load_ffn_weights(chip_ref, shard_refs, wgu_v, wd_v, sem)
        xv = x_ref[...]
        hb = _mx(xv * _rms_stat(xv) * g0_ref[...])
        h_ref[...] = hb
        gu = _dot(hb, wgu_v[...], NT)
        gu_ref[...] = gu.astype(gu_ref.dtype)
        g = gu[:, :F]
        u = gu[:, F:]
        a = _mx(g * _sigmoid(g) * u)
        y = _dot(a, wd_v[...], NN)
        y_ref[...] = y
        xo_ref[...] = xv + 0.5 * (y * _rms_stat(y) * g1_ref[...])

    row = lambda w: pl.BlockSpec((tm, w), lambda i, c: (i, 0))
    one = pl.BlockSpec((1, D), lambda i, c: (0, 0))
    return pl.pallas_call(
        body, name=name,
        grid_spec=pltpu.PrefetchScalarGridSpec(
            num_scalar_prefetch=1, grid=(T // tm,),
            in_specs=[row(D), one, one] + [ANY] * 6,
            out_specs=[row(D), row(D), row(F2), row(D)],
            scratch_shapes=[pltpu.VMEM((F2, D), MXU_DTYPE), pltpu.VMEM((F, D), MXU_DTYPE),
                            pltpu.SemaphoreType.DMA((3 * N_CHIPS,))]),
        out_shape=[jax.ShapeDtypeStruct((T, D), F32), jax.ShapeDtypeStruct((T, D), MXU_DTYPE),
                   jax.ShapeDtypeStruct((T, F2), MXU_DTYPE), jax.ShapeDtypeStruct((T, D), F32)],
        compiler_params=_cp("arbitrary"),
    )(*prefetch, x, g0, g1, *shards)


FFN_BWD_CHUNK = 2816


def _ffn_bwd(dxo, x, y, gu, g0, g1, wt, name):
    T, D = x.shape
    F2 = gu.shape[1]
    F = F2 // 2
    tm = _pick(T, 256, SUBLANES)
    fc = _pick(F, FFN_BWD_CHUNK)
    prefetch, shards = _ffn_weight_operands(wt)

    def body(chip_ref, dxo_ref, x_ref, y_ref, gu_ref, g0_ref, g1_ref, *refs):
        shard_refs = refs[:6]
        dx_ref, dy_ref, a_ref, dgu_ref, dg0_ref, dg1_ref, wgu_v, wd_v, sem = refs[6:]
        _load_ffn_weights(chip_ref, shard_refs, wgu_v, wd_v, sem)

        @pl.when(pl.program_id(0) == 0)
        def _():
            dg0_ref[...] = jnp.zeros_like(dg0_ref)
            dg1_ref[...] = jnp.zeros_like(dg1_ref)

        dxo_v = dxo_ref[...]
        yv = y_ref[...]
        dy, dg1 = _rms_bwd(yv, _rms_stat(yv), g1_ref[...], 0.5 * dxo_v)
        dg1_ref[...] += dg1
        dyb = _mx(dy)
        dy_ref[...] = dyb
        dh = jnp.zeros((tm, D), F32)
        for c in range(F // fc):
            lo, hi = c * fc, (c + 1) * fc
            da = _dot(dyb, wd_v[lo:hi, :], NT)
            g = gu_ref[:, lo:hi].astype(F32)
            u = gu_ref[:, F + lo:F + hi].astype(F32)
            s = _sigmoid(g)
            sg = g * s
            a_ref[:, lo:hi] = _mx(sg * u)
            dg = _mx(da * u * (s * (1.0 + g * (1.0 - s))))
            du = _mx(da * sg)
            dgu_ref[:, lo:hi] = dg
            dgu_ref[:, F + lo:F + hi] = du
            dh = dh + _dot(dg, wgu_v[lo:hi, :], NN) + _dot(du, wgu_v[F + lo:F + hi, :], NN)
        xv = x_ref[...]
        dx, dg0 = _rms_bwd(xv, _rms_stat(xv), g0_ref[...], dh)
        dg0_ref[...] += dg0
        dx_ref[...] = dxo_v + dx

    row = lambda w: pl.BlockSpec((tm, w), lambda i, c: (i, 0))
    one = pl.BlockSpec((1, D), lambda i, c: (0, 0))
    return pl.pallas_call(
        body, name=name,
        grid_spec=pltpu.PrefetchScalarGridSpec(
            num_scalar_prefetch=1, grid=(T // tm,),
            in_specs=[row(D), row(D), row(D), row(F2), one, one] + [ANY] * 6,
            out_specs=[row(D), row(D), row(F), row(F2), one, one],
            scratch_shapes=[pltpu.VMEM((F2, D), MXU_DTYPE), pltpu.VMEM((F, D), MXU_DTYPE),
                            pltpu.SemaphoreType.DMA((3 * N_CHIPS,))]),
        out_shape=[jax.ShapeDtypeStruct((T, D), F32), jax.ShapeDtypeStruct((T, D), MXU_DTYPE),
                   jax.ShapeDtypeStruct((T, F), MXU_DTYPE), jax.ShapeDtypeStruct((T, F2), MXU_DTYPE),
                   jax.ShapeDtypeStruct((1, D), F32), jax.ShapeDtypeStruct((1, D), F32)],
        compiler_params=_cp("arbitrary"),
    )(*prefetch, dxo, x, y, gu, g0, g1, *shards)


def _norm_fwd(x, g, name):
    T, D = x.shape
    tm = _pick(T, 512, SUBLANES)

    def body(x_ref, g_ref, h_ref):
        xv = x_ref[...]
        h_ref[...] = _mx(xv * _rms_stat(xv) * g_ref[...])

    row = pl.BlockSpec((tm, D), lambda i: (i, 0))
    return pl.pallas_call(body, name=name, grid=(T // tm,), in_specs=[row, _full((1, D))], out_specs=row,
                          out_shape=jax.ShapeDtypeStruct((T, D), MXU_DTYPE), compiler_params=_cp("parallel"))(x, g)


def _postnorm_fwd(x, m, g, name):
    T, D = x.shape
    tm = _pick(T, 512, SUBLANES)

    def body(x_ref, m_ref, g_ref, o_ref):
        mv = m_ref[...]
        o_ref[...] = x_ref[...] + mv * _rms_stat(mv) * g_ref[...]

    row = pl.BlockSpec((tm, D), lambda i: (i, 0))
    return pl.pallas_call(body, name=name, grid=(T // tm,), in_specs=[row, row, _full((1, D))], out_specs=row,
                          out_shape=jax.ShapeDtypeStruct((T, D), F32), compiler_params=_cp("parallel"))(x, m, g)


def _postnorm_bwd(dxo, m, g, name):
    T, D = m.shape
    tm = _pick(T, 512, SUBLANES)

    def body(dxo_ref, m_ref, g_ref, dm_ref, dg_ref):
        @pl.when(pl.program_id(0) == 0)
        def _():
            dg_ref[...] = jnp.zeros_like(dg_ref)

        mv = m_ref[...]
        dm, dg = _rms_bwd(mv, _rms_stat(mv), g_ref[...], dxo_ref[...])
        dg_ref[...] += dg
        dm_ref[...] = _mx(dm)

    row = pl.BlockSpec((tm, D), lambda i: (i, 0))
    return pl.pallas_call(body, name=name, grid=(T // tm,), in_specs=[row, row, _full((1, D))],
                          out_specs=[row, _full((1, D))],
                          out_shape=[jax.ShapeDtypeStruct((T, D), MXU_DTYPE), jax.ShapeDtypeStruct((1, D), F32)],
                          compiler_params=_cp("arbitrary"))(dxo, m, g)


def _prenorm_bwd(dxo, dh, x, g, name):
    T, D = x.shape
    tm = _pick(T, 512, SUBLANES)

    def body(dxo_ref, dh_ref, x_ref, g_ref, dx_ref, dg_ref):
        @pl.when(pl.program_id(0) == 0)
        def _():
            dg_ref[...] = jnp.zeros_like(dg_ref)

        xv = x_ref[...]
        dx, dg = _rms_bwd(xv, _rms_stat(xv), g_ref[...], dh_ref[...])
        dg_ref[...] += dg
        dx_ref[...] = dxo_ref[...] + dx

    row = pl.BlockSpec((tm, D), lambda i: (i, 0))
    return pl.pallas_call(body, name=name, grid=(T // tm,), in_specs=[row, row, row, _full((1, D))],
                          out_specs=[row, _full((1, D))],
                          out_shape=[jax.ShapeDtypeStruct((T, D), F32), jax.ShapeDtypeStruct((1, D), F32)],
                          compiler_params=_cp("arbitrary"))(dxo, dh, x, g)


def _loss_fwd_bwd(y, target, name):
    T, D = y.shape
    tm = _pick(T, 512, SUBLANES)

    def body(y_ref, t_ref, l_ref, dy_ref):
        @pl.when(pl.program_id(0) == 0)
        def _():
            l_ref[...] = jnp.zeros_like(l_ref)

        e = y_ref[...] - t_ref[...]
        dy_ref[...] = e * (1.0 / D)
        l_ref[...] += 0.5 * jnp.sum(jnp.mean(e * e, axis=-1, keepdims=True), axis=0, keepdims=True)

    row = pl.BlockSpec((tm, D), lambda i: (i, 0))
    return pl.pallas_call(body, name=name, grid=(T // tm,), in_specs=[row, row],
                          out_specs=[_full((SUBLANES, LANES)), row],
                          out_shape=[jax.ShapeDtypeStruct((SUBLANES, LANES), F32), jax.ShapeDtypeStruct((T, D), F32)],
                          compiler_params=_cp("arbitrary"))(y, target)


DN_ROWS = 512


def _shift_down(prev8, cur, s):
    n = cur.shape[0]
    xx = jnp.concatenate([prev8, cur], axis=0)
    return pltpu.roll(xx, s, 0)[SUBLANES:SUBLANES + n, :]


def _shift_up(cur, next8, s):
    n = cur.shape[0]
    xx = jnp.concatenate([cur, next8], axis=0)
    return pltpu.roll(xx, n + SUBLANES - s, 0)[:n, :]


def _conv_tile(x_ref, w, r, rows):
    start = pl.multiple_of(r * rows, SUBLANES)
    cur = x_ref[pl.ds(start, rows), :]
    pstart = pl.multiple_of(jnp.maximum(start - SUBLANES, 0), SUBLANES)
    prev8 = jnp.where(r == 0, 0.0, x_ref[pl.ds(pstart, SUBLANES), :])
    taps = [_shift_down(prev8, cur, DN_CONV - 1 - j) if j < DN_CONV - 1 else cur for j in range(DN_CONV)]
    c = taps[0] * w[0:1, :]
    for j in range(1, DN_CONV):
        c = c + taps[j] * w[j:j + 1, :]
    return c, taps


def _dn_prep_fwd(proj, conv_w, name):
    T = proj.shape[0]
    W = DN_HEADS * DN_HEAD_DIM
    rows = min(DN_ROWS, T)
    n_inner = T // rows
    scale = DN_HEAD_DIM ** -0.5

    def body(x_ref, w_ref, o_ref):
        cb = pl.program_id(0)
        w = w_ref[...]
        is_qk = cb < 2 * DN_HEADS
        post = jnp.where(cb < DN_HEADS, scale, 1.0)

        def step(r, carry):
            c, _ = _conv_tile(x_ref, w, r, rows)
            s = c * _sigmoid(c)
            rinv = lax.rsqrt(jnp.sum(s * s, axis=-1, keepdims=True) + L2_EPS)
            o_ref[pl.ds(pl.multiple_of(r * rows, SUBLANES), rows), :] = jnp.where(is_qk, s * rinv * post, s)
            return carry

        lax.fori_loop(0, n_inner, step, 0)

    col = pl.BlockSpec((T, LANES), lambda j: (0, j))
    return pl.pallas_call(body, name=name, grid=(3 * W // LANES,),
                          in_specs=[col, pl.BlockSpec((DN_CONV, LANES), lambda j: (0, j))], out_specs=col,
                          out_shape=jax.ShapeDtypeStruct((T, 3 * W), F32), compiler_params=_cp("parallel"))(proj, conv_w)


def _dn_prep_bwd(proj, conv_w, dqkv, name):
    T = proj.shape[0]
    W = DN_HEADS * DN_HEAD_DIM
    rows = min(DN_ROWS, T)
    n_inner = T // rows
    scale = DN_HEAD_DIM ** -0.5

    def body(x_ref, w_ref, dy_ref, dx_ref, dw_ref, dc_scr):
        cb = pl.program_id(0)
        w = w_ref[...]
        is_qk = cb < 2 * DN_HEADS
        post = jnp.where(cb < DN_HEADS, scale, 1.0)

        def step1(r, dws):
            c, taps = _conv_tile(x_ref, w, r, rows)
            sg = _sigmoid(c)
            s = c * sg
            rinv = lax.rsqrt(jnp.sum(s * s, axis=-1, keepdims=True) + L2_EPS)
            dy = dy_ref[pl.ds(pl.multiple_of(r * rows, SUBLANES), rows), :]
            yn = s * rinv
            dyn = dy * post
            ds_qk = rinv * (dyn - yn * jnp.sum(dyn * yn, axis=-1, keepdims=True))
            ds = jnp.where(is_qk, ds_qk, dy)
            dc = ds * (sg * (1.0 + c * (1.0 - sg)))
            dc_scr[pl.ds(pl.multiple_of(r * rows, SUBLANES), rows), :] = dc
            return tuple(dws[j] + jnp.sum(dc * taps[j], axis=0, keepdims=True) for j in range(DN_CONV))

        zero = jnp.zeros((1, LANES), F32)
        dws = lax.fori_loop(0, n_inner, step1, (zero,) * DN_CONV)
        for j in range(DN_CONV):
            dw_ref[j:j + 1, :] = dws[j]

        def step2(r, carry):
            start = pl.multiple_of(r * rows, SUBLANES)
            cur = dc_scr[pl.ds(start, rows), :]
            nstart = pl.multiple_of(jnp.minimum(start + rows, T - SUBLANES), SUBLANES)
            next8 = jnp.where(r == n_inner - 1, 0.0, dc_scr[pl.ds(nstart, SUBLANES), :])
            dx = cur * w[DN_CONV - 1:DN_CONV, :]
            for j in range(DN_CONV - 1):
                dx = dx + _shift_up(cur, next8, DN_CONV - 1 - j) * w[j:j + 1, :]
            dx_ref[pl.ds(start, rows), :] = _mx(dx)
            return carry

        lax.fori_loop(0, n_inner, step2, 0)

    col = pl.BlockSpec((T, LANES), lambda j: (0, j))
    wspec = pl.BlockSpec((DN_CONV, LANES), lambda j: (0, j))
    return pl.pallas_call(body, name=name, grid=(3 * W // LANES,), in_specs=[col, wspec, col], out_specs=[col, wspec],
                          out_shape=[jax.ShapeDtypeStruct((T, 3 * W), MXU_DTYPE), jax.ShapeDtypeStruct((DN_CONV, 3 * W), F32)],
                          scratch_shapes=[pltpu.VMEM((T, LANES), F32)], compiler_params=_cp("parallel"))(proj, conv_w, dqkv)


def _softplus(x):
    return jnp.maximum(x, 0.0) + jnp.log(1.0 + jnp.exp(-jnp.abs(x)))


def _dn_gate_fwd(ba, a_log, dt_bias, name):
    T = ba.shape[0]
    tm = _pick(T, 1024, SUBLANES)

    def body(ba_ref, al_ref, dt_ref, beta_ref, g_ref):
        beta_ref[...] = _sigmoid(ba_ref[:, :LANES])
        g_ref[...] = -jnp.exp(al_ref[...]) * _softplus(ba_ref[:, LANES:] + dt_ref[...])

    row = lambda w: pl.BlockSpec((tm, w), lambda i: (i, 0))
    return pl.pallas_call(body, name=name, grid=(T // tm,), in_specs=[row(2 * LANES), _full((1, LANES)), _full((1, LANES))],
                          out_specs=[row(LANES), row(LANES)],
                          out_shape=[jax.ShapeDtypeStruct((T, LANES), F32)] * 2, compiler_params=_cp("parallel"))(ba, a_log, dt_bias)


def _dn_gate_bwd(ba, a_log, dt_bias, dbeta, dg, name):
    T = ba.shape[0]
    tm = _pick(T, 1024, SUBLANES)

    def body(ba_ref, al_ref, dt_ref, dbeta_ref, dg_ref, dba_ref, dal_ref, ddt_ref):
        @pl.when(pl.program_id(0) == 0)
        def _():
            dal_ref[...] = jnp.zeros_like(dal_ref)
            ddt_ref[...] = jnp.zeros_like(ddt_ref)

        beta = _sigmoid(ba_ref[:, :LANES])
        dba_ref[:, :LANES] = _mx(dbeta_ref[...] * beta * (1.0 - beta))
        pre = ba_ref[:, LANES:] + dt_ref[...]
        ea = jnp.exp(al_ref[...])
        dgv = dg_ref[...]
        da = dgv * (-ea) * _sigmoid(pre)
        dba_ref[:, LANES:] = _mx(da)
        ddt_ref[...] += jnp.sum(da, axis=0, keepdims=True)
        dal_ref[...] += jnp.sum(dgv * (-ea) * _softplus(pre), axis=0, keepdims=True)

    row = lambda w: pl.BlockSpec((tm, w), lambda i: (i, 0))
    one = _full((1, LANES))
    return pl.pallas_call(body, name=name, grid=(T // tm,), in_specs=[row(2 * LANES), one, one, row(LANES), row(LANES)],
                          out_specs=[row(2 * LANES), one, one],
                          out_shape=[jax.ShapeDtypeStruct((T, 2 * LANES), MXU_DTYPE), jax.ShapeDtypeStruct((1, LANES), F32),
                                     jax.ShapeDtypeStruct((1, LANES), F32)],
                          compiler_params=_cp("arbitrary"))(ba, a_log, dt_bias, dbeta, dg)


def _tri(c, strict):
    i = lax.broadcasted_iota(jnp.int32, (c, c), 0)
    j = lax.broadcasted_iota(jnp.int32, (c, c), 1)
    return (i > j) if strict else (i >= j)


def _inv_unit_lower(ls):
    c = ls[0].shape[0]
    i = lax.broadcasted_iota(jnp.int32, (c, c), 0)
    j = lax.broadcasted_iota(jnp.int32, (c, c), 1)
    eye = jnp.where(i == j, 1.0, 0.0)
    facs = [[eye - l for l in ls]]
    cur = ls
    for _ in range(int(math.log2(c)) - 1):
        cur = [_dot(p, p, NN, TRI_PREC) for p in cur]
        facs.append([eye + p for p in cur])
    while len(facs) > 1:
        nxt = [[_dot(a, b, NN, TRI_PREC) for a, b in zip(facs[t], facs[t + 1])] for t in range(0, len(facs) - 1, 2)]
        if len(facs) % 2:
            nxt.append(facs[-1])
        facs = nxt
    return facs[0]


def _chunk_gates(g_blk):
    c = g_blk.shape[0]
    gcs = _dot(jnp.where(_tri(c, False), 1.0, 0.0), g_blk, NN, HI)
    return gcs, gcs.T


def _head_chunk(h, qh, kh, vh, beta_blk, gcs, gcs_t):
    c = qh.shape[0]
    incl = _tri(c, False)
    gc_col = gcs[:, h:h + 1]
    gc_row = gcs_t[h:h + 1, :]
    gc_last = gcs_t[h:h + 1, c - 1:c]
    dec = jnp.where(incl, jnp.exp(jnp.where(incl, gc_col - gc_row, 0.0)), 0.0)
    gam = jnp.exp(gc_col)
    rr = jnp.exp(gc_last - gc_col)
    gl = jnp.exp(gc_last)
    b = beta_blk[:, h:h + 1]
    kb = kh * b
    vb = vh * b
    both = _dot(jnp.concatenate([_mx(kb), _mx(qh)], axis=0), _mx(kh), NT)
    lmat = jnp.where(_tri(c, True), both[:c] * dec, 0.0)
    pmat = jnp.where(incl, both[c:] * dec, 0.0)
    return dict(dec=dec, gam=gam, rr=rr, gl=gl, b=b, kb=kb, vb=vb, lmat=lmat, pmat=pmat)


def _solve_uw(tinv, q):
    return _dot(tinv, jnp.concatenate([q["vb"], q["kb"] * q["gam"]], axis=1), NN, TRI_PREC)


def _dn_scan_fwd(qkv, beta, g, proj, norm_g, name):
    T = qkv.shape[0]
    C, H, Dh = DN_CHUNK, DN_HEADS, DN_HEAD_DIM
    W = H * Dh
    N = T // C

    def body(q_ref, k_ref, v_ref, beta_ref, g_ref, z_ref, ng_ref, og_ref, o_ref, tinv_ref, s_ref, state):
        @pl.when(pl.program_id(0) == 0)
        def _():
            state[...] = jnp.zeros_like(state)

        gcs, gcs_t = _chunk_gates(g_ref[...])
        beta_blk = beta_ref[...]
        ng = ng_ref[...]
        heads = range(H)
        cs = [slice(h * Dh, (h + 1) * Dh) for h in heads]
        qs = [_head_chunk(h, q_ref[:, cs[h]], k_ref[:, cs[h]], v_ref[:, cs[h]], beta_blk, gcs, gcs_t) for h in heads]
        tinvs = _inv_unit_lower([q["lmat"] for q in qs])
        for h in heads:
            tinv_ref[h] = tinvs[h]
        uws = [_solve_uw(tinvs[h], qs[h]) for h in heads]
        ss = [state[h] for h in heads]
        for h in heads:
            s_ref[0, h] = ss[h]
        sbs = [_mx(s) for s in ss]
        vnbs = [_mx(uws[h][:, :Dh] - _dot(_mx(uws[h][:, Dh:]), sbs[h], NN)) for h in heads]
        os_ = [_dot(jnp.concatenate([_mx(q_ref[:, cs[h]] * qs[h]["gam"]), _mx(qs[h]["pmat"])], axis=1),
                    jnp.concatenate([sbs[h], vnbs[h]], axis=0), NN) for h in heads]
        for h in heads:
            state[h] = ss[h] * qs[h]["gl"] + _dot(_mx((k_ref[:, cs[h]] * qs[h]["rr"]).T), vnbs[h], NN)
        for h in heads:
            o = os_[h]
            o_ref[:, cs[h]] = o
            zh = z_ref[:, cs[h]]
            og_ref[:, cs[h]] = _mx(o * _rms_stat(o) * ng * (zh * _sigmoid(zh)))

    blk = lambda j: pl.BlockSpec((C, W), lambda n: (n, j))
    small = pl.BlockSpec((C, LANES), lambda n: (n, 0))
    return pl.pallas_call(
        body, name=name, grid=(N,),
        in_specs=[blk(0), blk(1), blk(2), small, small, blk(3), _full((1, Dh))],
        out_specs=[blk(0), blk(0), pl.BlockSpec((H, C, C), lambda n: (0, n, 0)),
                   pl.BlockSpec((1, H, Dh, Dh), lambda n: (n, 0, 0, 0))],
        out_shape=[jax.ShapeDtypeStruct((T, W), MXU_DTYPE), jax.ShapeDtypeStruct((T, W), F32),
                   jax.ShapeDtypeStruct((H, T, C), F32), jax.ShapeDtypeStruct((N, H, Dh, Dh), F32)],
        scratch_shapes=[pltpu.VMEM((H, Dh, Dh), F32)],
        compiler_params=_cp("arbitrary"),
    )(qkv, qkv, qkv, beta, g, proj, norm_g)


def _dn_scan_bwd(qkv, beta, g, proj, norm_g, o, tinv, s_all, dog, name):
    T = qkv.shape[0]
    C, H, Dh = DN_CHUNK, DN_HEADS, DN_HEAD_DIM
    W = H * Dh
    N = T // C

    def body(q_ref, k_ref, v_ref, beta_ref, g_ref, z_ref, ng_ref, o_ref, tinv_ref, s_ref, dog_ref,
             dqkv_ref, dbeta_ref, dg_ref, dz_ref, dng_ref, dstate):
        @pl.when(pl.program_id(0) == 0)
        def _():
            dstate[...] = jnp.zeros_like(dstate)
            dng_ref[...] = jnp.zeros_like(dng_ref)

        gcs, gcs_t = _chunk_gates(g_ref[...])
        beta_blk = beta_ref[...]
        ng = ng_ref[...]
        incl = _tri(C, False)
        strict = _tri(C, True)
        lane = lax.broadcasted_iota(jnp.int32, (C, LANES), 1)
        rowi = lax.broadcasted_iota(jnp.int32, (C, 1), 0)
        ones = jnp.ones((C, LANES), F32)
        dbeta_acc = jnp.zeros((C, LANES), F32)
        dgc_acc = jnp.zeros((C, LANES), F32)
        dng_acc = jnp.zeros((1, Dh), F32)
        cs = [slice(h * Dh, (h + 1) * Dh) for h in range(H)]
        rsum = lambda t: jnp.sum(t, axis=1, keepdims=True)
        for heads in (range(0, H // 2), range(H // 2, H)):
            dobs = {}
            for h in heads:
                oh, zh, dogh = o_ref[:, cs[h]], z_ref[:, cs[h]], dog_ref[:, cs[h]]
                rstat = _rms_stat(oh)
                sz = _sigmoid(zh)
                dz_ref[:, cs[h]] = _mx(dogh * (oh * rstat * ng) * (sz * (1.0 + zh * (1.0 - sz))))
                do, dng = _rms_bwd(oh, rstat, ng, dogh * (zh * sz))
                dng_acc = dng_acc + dng
                dobs[h] = _mx(do)
            qs = {h: _head_chunk(h, q_ref[:, cs[h]], k_ref[:, cs[h]], v_ref[:, cs[h]], beta_blk, gcs, gcs_t) for h in heads}
            tms = {h: tinv_ref[h] for h in heads}
            uws = {h: _solve_uw(tms[h], qs[h]) for h in heads}
            ss = {h: s_ref[0, h] for h in heads}
            sbs = {h: _mx(ss[h]) for h in heads}
            wbs = {h: _mx(uws[h][:, Dh:]) for h in heads}
            vnbs = {h: _mx(uws[h][:, :Dh] - _dot(wbs[h], sbs[h], NN)) for h in heads}
            dsns = {h: dstate[h] for h in heads}
            dsbs = {h: _mx(dsns[h]) for h in heads}
            dvnews = {h: _dot(_mx(qs[h]["pmat"]), dobs[h], TN) + _dot(_mx(k_ref[:, cs[h]] * qs[h]["rr"]), dsbs[h], NN)
                      for h in heads}
            dvb16s = {h: _mx(dvnews[h]) for h in heads}
            dps = {h: jnp.where(incl, _dot(dobs[h], vnbs[h], NT), 0.0) for h in heads}
            dqds = {h: _dot(dobs[h], sbs[h], NT) for h in heads}
            dkds = {h: _dot(vnbs[h], dsbs[h], NT) for h in heads}
            dgls = {h: jnp.sum(rsum(ss[h] * dsns[h]), axis=0, keepdims=True) for h in heads}
            dws = {h: -_dot(dvb16s[h], sbs[h], NT) for h in heads}
            for h in heads:
                dstate[h] = qs[h]["gl"] * dsns[h] + _dot(
                    jnp.concatenate([_mx(q_ref[:, cs[h]] * qs[h]["gam"]), -wbs[h]], axis=0),
                    jnp.concatenate([dobs[h], dvb16s[h]], axis=0), TN)
            dsols = {h: _dot(tms[h], jnp.concatenate([dvnews[h], dws[h]], axis=1), TN, TRI_PREC) for h in heads}
            dvbs = {h: dsols[h][:, :Dh] for h in heads}
            dkbgs = {h: dsols[h][:, Dh:] for h in heads}
            dls = {h: jnp.where(strict, -_dot(dsols[h], uws[h], NT, TRI_PREC), 0.0) for h in heads}
            mmats = {h: dls[h] * qs[h]["lmat"] + dps[h] * qs[h]["pmat"] for h in heads}
            dgcs = {h: rsum(mmats[h]) - _dot(mmats[h], ones, TN, TRI_PREC)[:, :1] for h in heads}
            dboth = {h: jnp.concatenate([_mx(dls[h] * qs[h]["dec"]), _mx(dps[h] * qs[h]["dec"])], axis=0) for h in heads}
            for h in heads:
                q = qs[h]
                qh, kh, vh = q_ref[:, cs[h]], k_ref[:, cs[h]], v_ref[:, cs[h]]
                gam, rr, b, kb = q["gam"], q["rr"], q["b"], q["kb"]
                on_k = _dot(dboth[h], _mx(kh), NN)
                dkb = on_k[:C] + dkbgs[h] * gam
                dk = _dot(dboth[h], jnp.concatenate([_mx(kb), _mx(qh)], axis=0), TN) + dkb * b + dkds[h] * rr
                dq = on_k[C:] + dqds[h] * gam
                dgam = rsum(dkbgs[h] * kb) + rsum(dqds[h] * qh)
                dr = rsum(dkds[h] * kh)
                dgc_last = jnp.sum(dr * rr, axis=0, keepdims=True) + dgls[h] * q["gl"]
                dgc = dgcs[h] + dgam * gam - dr * rr + jnp.where(rowi == C - 1, dgc_last, 0.0)
                dbeta = rsum(dvbs[h] * vh) + rsum(dkb * kh)
                dqkv_ref[:, cs[h]] = dq
                dqkv_ref[:, W + h * Dh:W + (h + 1) * Dh] = dk
                dqkv_ref[:, 2 * W + h * Dh:2 * W + (h + 1) * Dh] = dvbs[h] * b
                dbeta_acc = jnp.where(lane == h, dbeta, dbeta_acc)
                dgc_acc = jnp.where(lane == h, dgc, dgc_acc)
        dbeta_ref[...] = dbeta_acc
        dg_ref[...] = _dot(jnp.where(incl, 1.0, 0.0), dgc_acc, TN, HI)
        dng_ref[...] += dng_acc

    rev = lambda n: N - 1 - n
    blk = lambda j: pl.BlockSpec((C, W), lambda n: (rev(n), j))
    small = pl.BlockSpec((C, LANES), lambda n: (rev(n), 0))
    return pl.pallas_call(
        body, name=name, grid=(N,),
        in_specs=[blk(0), blk(1), blk(2), small, small, blk(3), _full((1, Dh)), blk(0),
                  pl.BlockSpec((H, C, C), lambda n: (0, rev(n), 0)),
                  pl.BlockSpec((1, H, Dh, Dh), lambda n: (rev(n), 0, 0, 0)), blk(0)],
        out_specs=[pl.BlockSpec((C, 3 * W), lambda n: (rev(n), 0)), small, small, blk(0), _full((1, Dh))],
        out_shape=[jax.ShapeDtypeStruct((T, 3 * W), F32), jax.ShapeDtypeStruct((T, LANES), F32),
                   jax.ShapeDtypeStruct((T, LANES), F32), jax.ShapeDtypeStruct((T, W), MXU_DTYPE),
                   jax.ShapeDtypeStruct((1, Dh), F32)],
        scratch_shapes=[pltpu.VMEM((H, Dh, Dh), F32)],
        compiler_params=_cp("arbitrary"),
    )(qkv, qkv, qkv, beta, g, proj, norm_g, o, tinv, s_all, dog)


_INV_SQRT2 = 0.7071067811865476
_INV_SQRT_2PI = 0.3989422804014327


def _sg_recompute(zp_ref, bin_ref, lng_ref, lnb_ref):
    E = SG_WIDTH
    zin = zp_ref[...] + bin_ref[...]
    cdf = 0.5 * (1.0 + lax.erf(zin * _INV_SQRT2))
    zz = zin * cdf
    u = zz[:, :E]
    vp = zz[:, E:]
    mu = jnp.mean(vp, axis=-1, keepdims=True)
    xc = vp - mu
    rstd = lax.rsqrt(jnp.mean(xc * xc, axis=-1, keepdims=True) + LN_EPS)
    xhat = xc * rstd
    v = xhat * lng_ref[...] + lnb_ref[...]
    return zin, cdf, u, xhat, rstd, v


def _sg_masked_ws(ws_ref, g):
    return _mx(jnp.where(_tri(SG_CHUNK, False), ws_ref[g], 0.0))


def _sg_fwd(zpre, b_in, ln_g, ln_b, w_s, b_s_t, name):
    T = zpre.shape[0]
    E, G, C, GW = SG_WIDTH, SG_GROUPS, SG_CHUNK, SG_GROUP_W

    def body(zp_ref, bin_ref, lng_ref, lnb_ref, ws_ref, bst_ref, um_ref):
        _, _, u, _, _, v = _sg_recompute(zp_ref, bin_ref, lng_ref, lnb_ref)
        bst = bst_ref[...]
        for g in range(G):
            cs = slice(g * GW, (g + 1) * GW)
            mixed = _dot(_sg_masked_ws(ws_ref, g), _mx(v[:, cs]), NN) + bst[:, g:g + 1]
            um_ref[:, cs] = _mx(u[:, cs] * mixed)

    return pl.pallas_call(
        body, name=name, grid=(T // C,),
        in_specs=[pl.BlockSpec((C, 2 * E), lambda n: (n, 0)), _full((1, 2 * E)), _full((1, E)), _full((1, E)),
                  _full((G, C, C)), _full((C, LANES))],
        out_specs=pl.BlockSpec((C, E), lambda n: (n, 0)),
        out_shape=jax.ShapeDtypeStruct((T, E), MXU_DTYPE), compiler_params=_cp("parallel"),
    )(zpre, b_in, ln_g, ln_b, w_s, b_s_t)


def _sg_bwd(zpre, b_in, ln_g, ln_b, w_s, b_s_t, dum, name):
    T = zpre.shape[0]
    E, G, C, GW = SG_WIDTH, SG_GROUPS, SG_CHUNK, SG_GROUP_W

    def body(zp_ref, bin_ref, lng_ref, lnb_ref, ws_ref, bst_ref, dum_ref,
             dz_ref, dbin_ref, dlng_ref, dlnb_ref, dws_ref, dbst_ref):
        @pl.when(pl.program_id(0) == 0)
        def _():
            for r in (dbin_ref, dlng_ref, dlnb_ref, dws_ref, dbst_ref):
                r[...] = jnp.zeros_like(r)

        zin, cdf, u, xhat, rstd, v = _sg_recompute(zp_ref, bin_ref, lng_ref, lnb_ref)
        bst = bst_ref[...]
        lane = lax.broadcasted_iota(jnp.int32, (C, LANES), 1)
        dum_v = dum_ref[...]
        dbst = jnp.zeros((C, LANES), F32)
        du_parts, dv_parts = [], []
        for g in range(G):
            cs = slice(g * GW, (g + 1) * GW)
            wsm = _sg_masked_ws(ws_ref, g)
            vg = _mx(v[:, cs])
            mixed = _dot(wsm, vg, NN) + bst[:, g:g + 1]
            dumg = dum_v[:, cs]
            du_parts.append(dumg * mixed)
            dmixed = dumg * u[:, cs]
            dmb = _mx(dmixed)
            dv_parts.append(_dot(wsm, dmb, TN))
            dws_ref[g] += _dot(dmb, vg, NT)
            dbst = jnp.where(lane == g, jnp.sum(dmixed, axis=1, keepdims=True), dbst)
        dbst_ref[...] += dbst
        du = jnp.concatenate(du_parts, axis=1)
        dv = jnp.concatenate(dv_parts, axis=1)
        dlng_ref[...] += jnp.sum(dv * xhat, axis=0, keepdims=True)
        dlnb_ref[...] += jnp.sum(dv, axis=0, keepdims=True)
        dxh = dv * lng_ref[...]
        dvp = rstd * (dxh - jnp.mean(dxh, axis=-1, keepdims=True) - xhat * jnp.mean(dxh * xhat, axis=-1, keepdims=True))
        dzz = jnp.concatenate([du, dvp], axis=1)
        dzin = dzz * (cdf + zin * (_INV_SQRT_2PI * jnp.exp(-0.5 * zin * zin)))
        dz_ref[...] = _mx(dzin)
        dbin_ref[...] += jnp.sum(dzin, axis=0, keepdims=True)

    return pl.pallas_call(
        body, name=name, grid=(T // C,),
        in_specs=[pl.BlockSpec((C, 2 * E), lambda n: (n, 0)), _full((1, 2 * E)), _full((1, E)), _full((1, E)),
                  _full((G, C, C)), _full((C, LANES)), pl.BlockSpec((C, E), lambda n: (n, 0))],
        out_specs=[pl.BlockSpec((C, 2 * E), lambda n: (n, 0)), _full((1, 2 * E)), _full((1, E)), _full((1, E)),
                   _full((G, C, C)), _full((C, LANES))],
        out_shape=[jax.ShapeDtypeStruct((T, 2 * E), MXU_DTYPE), jax.ShapeDtypeStruct((1, 2 * E), F32),
                   jax.ShapeDtypeStruct((1, E), F32), jax.ShapeDtypeStruct((1, E), F32),
                   jax.ShapeDtypeStruct((G, C, C), F32), jax.ShapeDtypeStruct((C, LANES), F32)],
        compiler_params=_cp("arbitrary"),
    )(zpre, b_in, ln_g, ln_b, w_s, b_s_t, dum)


def _row(v):
    return v.reshape(1, -1)


def _pad_lanes(v):
    v = v.reshape(1, -1)
    return jnp.pad(v, ((0, 0), (0, LANES - v.shape[1])))


def _local_step(x, target, p, weights_for, grads_ready=None, small_ready=None):
    ng = p["norm_g"]
    grads = {}
    dng = [[None] * 6 for _ in range(2)]
    order = [jnp.zeros((), F32)]

    def tell(group):
        zero = grads_ready(group, grads) if grads_ready is not None else None
        if zero is not None:
            order[0] = zero

    def gain(i, s):
        return _row(ng[i, s]) + order[0]

    def ffn_f(xin, i, j, tag):
        wt = weights_for("ffn" + tag, xin)
        xo, h, gu, y = _ffn_fwd(xin, _row(ng[i, 4 * j]), _row(ng[i, 4 * j + 1]), wt, "ffn_fwd_" + tag)
        return xo, (xin, h, gu, y, wt)

    x1, sv_f00 = ffn_f(x, 0, 0, "00")
    dnw = weights_for("dn", x1)
    hn0 = _norm_fwd(x1, _row(ng[0, 2]), "dn_prenorm")
    proj = _mm(hn0, dnw["dn_wqkvz"], "nn", "dn_proj")
    ba = _mm(hn0, dnw["dn_wba"], "nn", "dn_proj_ba")
    a_log = _pad_lanes(p["dn_a_log"])
    dt_bias = _pad_lanes(p["dn_dt_bias"])
    dn_ng = _row(p["dn_norm_g"])
    qkv = _dn_prep_fwd(proj, p["dn_conv_w"], "dn_prep_fwd")
    beta, gdec = _dn_gate_fwd(ba, a_log, dt_bias, "dn_gate_fwd")
    og, o_raw, tinv, s_all = _dn_scan_fwd(qkv, beta, gdec, proj, dn_ng, "dn_scan_fwd")
    m0 = _mm(og, dnw["dn_wout"], "nn", "dn_out")
    x2 = _postnorm_fwd(x1, m0, _row(ng[0, 3]), "dn_postnorm")
    x3, sv_f01 = ffn_f(x2, 0, 1, "01")
    x4, sv_f10 = ffn_f(x3, 1, 0, "10")
    sgw = weights_for("sg", x4)
    hn1 = _norm_fwd(x4, _row(ng[1, 2]), "sg_prenorm")
    zpre = _mm(hn1, sgw["sg_win"], "nn", "sg_proj")
    sg_bin = _row(p["sg_b_in"])
    sg_lng = _row(p["sg_ln_g"])
    sg_lnb = _row(p["sg_ln_b"])
    sg_bst = jnp.pad(p["sg_b_s"].T, ((0, 0), (0, LANES - SG_GROUPS)))
    um = _sg_fwd(zpre, sg_bin, sg_lng, sg_lnb, p["sg_w_s"], sg_bst, "sg_fwd")
    m1 = _mm(um, sgw["sg_wout"], "nn", "sg_out")
    x5 = _postnorm_fwd(x4, m1, _row(ng[1, 3]), "sg_postnorm")
    x6, sv_f11 = ffn_f(x5, 1, 1, "11")
    loss_part, dx = _loss_fwd_bwd(x6, target, "loss")

    def ffn_b(dxo, sv, i, j, tag, last=False):
        xin, h, gu, y, wt = sv
        dxi, dy, a, dgu, dg0, dg1 = _ffn_bwd(dxo, xin, y, gu, gain(i, 4 * j), gain(i, 4 * j + 1), wt, "ffn_bwd_" + tag)
        dng[i][4 * j] = dg0
        dng[i][4 * j + 1] = dg1
        after = None
        if last:
            grads["norm_g"] = jnp.stack([jnp.concatenate(dng[t], axis=0) for t in range(2)])
            after = small_ready(grads, loss_part) if small_ready is not None else None
        grads["wd" + tag] = _mm(a, dy, "tn", "ffn_wgrad_down_" + tag, after=after)
        grads["wguT" + tag] = _mm(dgu, h, "tn", "ffn_wgrad_up_" + tag, after=after)
        tell("ffn" + tag)
        return dxi

    dx = ffn_b(dx, sv_f11, 1, 1, "11")
    dm1, dng[1][3] = _postnorm_bwd(dx, m1, gain(1, 3), "sg_postnorm_bwd")
    grads["sg_w_out"] = _mm(um, dm1, "tn", "sg_wgrad_out")
    dum = _mm(dm1, sgw["sg_wout"], "nt", "sg_dgrad_out")
    dz1, dbin, dlng, dlnb, dws, dbst = _sg_bwd(zpre, sg_bin, sg_lng, sg_lnb, p["sg_w_s"], sg_bst, dum, "sg_bwd")
    grads["sg_w_inT"] = _mm(dz1, hn1, "tn", "sg_wgrad_in")
    tell("sg")
    dh1 = _mm(dz1, sgw["sg_win"], "nt", "sg_dgrad_in")
    dx, dng[1][2] = _prenorm_bwd(dx, dh1, x4, gain(1, 2), "sg_prenorm_bwd")
    grads["sg_b_in"] = dbin.reshape(1, -1)
    grads["sg_ln_g"] = dlng.reshape(1, -1)
    grads["sg_ln_b"] = dlnb.reshape(1, -1)
    grads["sg_w_s"] = jnp.where(jnp.tril(jnp.ones((SG_CHUNK, SG_CHUNK), bool)), dws, 0.0)[None]
    grads["sg_b_s"] = dbst[:, :SG_GROUPS].T[None]
    dx = ffn_b(dx, sv_f10, 1, 0, "10")
    dx = ffn_b(dx, sv_f01, 0, 1, "01")
    dm0, dng[0][3] = _postnorm_bwd(dx, m0, gain(0, 3), "dn_postnorm_bwd")
    grads["dn_w_out"] = _mm(og, dm0, "tn", "dn_wgrad_out")
    dog = _mm(dm0, dnw["dn_wout"], "nt", "dn_dgrad_out")
    dqkv, dbeta, dgdec, dz0, dnng = _dn_scan_bwd(qkv, beta, gdec, proj, dn_ng, o_raw, tinv, s_all, dog, "dn_scan_bwd")
    dqkv_pre, dconv = _dn_prep_bwd(proj, p["dn_conv_w"], dqkv, "dn_prep_bwd")
    dba, dal, ddt = _dn_gate_bwd(ba, a_log, dt_bias, dbeta, dgdec, "dn_gate_bwd")
    W3 = 3 * DN_HEADS * DN_HEAD_DIM
    dw_qkv = _mm(hn0, dqkv_pre, "tn", "dn_wgrad_qkv")
    dw_z = _mm(hn0, dz0, "tn", "dn_wgrad_z")
    dw_ba = _mm(hn0, dba, "tn", "dn_wgrad_ba")
    grads["dn_w_in"] = jnp.concatenate(
        [dw_qkv, dw_z, dw_ba[:, :DN_HEADS], dw_ba[:, LANES:LANES + DN_HEADS]], axis=1)
    tell("dn")
    dh0 = _mm(dqkv_pre, dnw["dn_wqkvz"][:, :W3], "nt", "dn_dgrad_qkv")
    dh0 = _mm(dz0, dnw["dn_wqkvz"][:, W3:], "nt", "dn_dgrad_z", add=dh0)
    dh0 = _mm(dba, dnw["dn_wba"], "nt", "dn_dgrad_ba", add=dh0)
    dx, dng[0][2] = _prenorm_bwd(dx, dh0, x1, gain(0, 2), "dn_prenorm_bwd")
    grads["dn_conv_w"] = dconv[None]
    grads["dn_a_log"] = dal[:, :DN_HEADS]
    grads["dn_dt_bias"] = ddt[:, :DN_HEADS]
    grads["dn_norm_g"] = dnng
    dx = ffn_b(dx, sv_f00, 0, 0, "00", last=True)
    return loss_part, dx, grads


def _mesh_pos():
    return lax.axis_index("x"), lax.axis_index("y"), lax.axis_index("c")


def _other_chips(x, y):
    return [(1 - x, y), (x, 1 - y), (1 - x, 1 - y)]


def _allgather_chips(arrs, name):
    n = len(arrs)

    def body(*refs):
        ins, outs = refs[:n], refs[n:2 * n]
        ici_send, ici_recv, d2d_send, d2d_recv = refs[2 * n:]
        x, y, c = _mesh_pos()
        me = 2 * x + y
        chips = _other_chips(x, y)
        sibling = (x, y, 1 - c)

        def ici(i, j, k):
            cx, cy = chips[j]
            return pltpu.make_async_remote_copy(src_ref=ins[i].at[c], dst_ref=outs[i].at[k, c], send_sem=ici_send.at[3 * i + j],
                                                recv_sem=ici_recv.at[3 * i + j], device_id=(cx, cy, c), device_id_type=MESH)

        def d2d(i, j, h):
            cx, cy = chips[j]
            slot = outs[i].at[2 * cx + cy, h]
            return pltpu.make_async_remote_copy(src_ref=slot, dst_ref=slot, send_sem=d2d_send.at[3 * i + j],
                                                recv_sem=d2d_recv.at[3 * i + j], device_id=sibling, device_id_type=MESH)

        sends = [ici(i, j, me) for i in range(n) for j in range(3)]
        for cp in sends:
            cp.start()
        for i in range(n):
            for j, (cx, cy) in enumerate(chips):
                ici(i, j, 2 * cx + cy).wait_recv()
                fwd = d2d(i, j, c)
                fwd.start()
                sends.append(fwd)
        for i in range(n):
            for j in range(3):
                d2d(i, j, 1 - c).wait_recv()
        for cp in sends:
            cp.wait_send()

    return pl.pallas_call(
        body, name=name, in_specs=[ANY] * n, out_specs=[ANY] * n,
        out_shape=[jax.ShapeDtypeStruct((N_CHIPS,) + a.shape, a.dtype) for a in arrs],
        scratch_shapes=[pltpu.SemaphoreType.DMA((3 * n,))] * 4,
    )(*arrs)


HBM = pl.BlockSpec(memory_space=pltpu.HBM)
SEM = pl.BlockSpec(memory_space=pltpu.SEMAPHORE)
TOKEN = jax.ShapeDtypeStruct((SUBLANES, LANES), F32)


_PEERS = {"gather": 3, "scatter": 3, "swap": 1, "all": N_DEV - 1}


def _land_shape(kind, shape):
    if kind == "gather":
        return (N_CHIPS,) + shape
    if kind == "all":
        return (N_DEV,) + shape
    return (N_CHIPS,) + shape[2:] if kind == "swap" else shape


def _peer_copies(kind, flags, src_refs, land_refs, send_sems, recv_sems, receiving):
    x, y, c = _mesh_pos()
    me4, me8 = 2 * x + y, 4 * x + 2 * y + c
    np_ = _PEERS[kind]
    cps = []
    for i, (src, land) in enumerate(zip(src_refs, land_refs)):
        if kind == "swap":
            half = src.at[1 - c] if flags[i] else src.at[:, 1 - c]
            plan = [((x, y, 1 - c), half, land)]
        elif kind == "all":
            masks = [(mx, my, mc) for mx in (0, 1) for my in (0, 1) for mc in (0, 1)][1:]
            peers = [(jnp.where(mx, 1 - x, x), jnp.where(my, 1 - y, y), jnp.where(mc, 1 - c, c)) for mx, my, mc in masks]
            plan = [(p, src, land.at[4 * p[0] + 2 * p[1] + p[2] if receiving else me8]) for p in peers]
        else:
            plan = []
            for cx, cy in _other_chips(x, y):
                k = 2 * cx + cy
                s = src.at[me4 if receiving else k] if kind == "scatter" else src
                plan.append(((cx, cy, c), s, land.at[k if receiving else me4]))
        for j, (peer, s, d) in enumerate(plan):
            cps.append(pltpu.make_async_remote_copy(src_ref=s, dst_ref=d, send_sem=send_sems.at[np_ * i + j],
                                                    recv_sem=recv_sems.at[np_ * i + j], device_id=peer, device_id_type=MESH))
    return cps


def _copies_start(kind, srcs, after, name, flags=None):
    n = len(srcs)
    ns = _PEERS[kind] * n
    lands = [lax.empty(_land_shape(kind, s.shape), s.dtype) for s in srcs]
    after = [] if after is None else [after]

    def body(*refs):
        src_refs, land_refs = refs[:n], refs[n:2 * n]
        send_sems, recv_sems = refs[2 * n + len(after)], refs[2 * n + len(after) + 1]
        token = refs[-1]
        for cp in _peer_copies(kind, flags, src_refs, land_refs, send_sems, recv_sems, False):
            cp.start()
        token[...] = jnp.zeros_like(token)

    outs = pl.pallas_call(
        body, name=name,
        in_specs=[HBM] * (2 * n) + [ANY] * len(after),
        out_specs=(SEM, SEM) + (HBM,) * (2 * n) + (pl.BlockSpec(memory_space=pltpu.VMEM),),
        out_shape=(pltpu.SemaphoreType.DMA((ns,)), pltpu.SemaphoreType.DMA((ns,)))
        + tuple(pltpu.HBM(a.shape, a.dtype) for a in list(srcs) + lands) + (TOKEN,),
        input_output_aliases={i: 2 + i for i in range(2 * n)},
        compiler_params=pltpu.CompilerParams(has_side_effects=pltpu.SideEffectType.DATAFLOW_SIDE_EFFECTING),
    )(*[pltpu.with_memory_space_constraint(a, pltpu.HBM) for a in list(srcs) + lands], *after)
    return dict(sems=outs[:2], srcs=outs[2:2 + n], lands=outs[2 + n:2 + 2 * n], token=outs[-1], kind=kind, flags=flags)


def _copies_wait(started, after, name):
    n = len(started["srcs"])
    kind, flags = started["kind"], started["flags"]
    after = list(after) if isinstance(after, (list, tuple)) else [after]

    def body(*refs):
        src_refs, land_refs = refs[:n], refs[n:2 * n]
        send_sems, recv_sems = refs[2 * n], refs[2 * n + 1]
        for cp in _peer_copies(kind, flags, src_refs, land_refs, send_sems, recv_sems, True):
            cp.wait_send()
            cp.wait_recv()

    outs = pl.pallas_call(
        body, name=name,
        in_specs=[HBM] * (2 * n) + [SEM, SEM] + [ANY] * len(after),
        out_specs=(HBM,) * (2 * n),
        out_shape=tuple(pltpu.HBM(a.shape, a.dtype) for a in list(started["srcs"]) + list(started["lands"])),
        input_output_aliases={i: i for i in range(2 * n)},
        compiler_params=pltpu.CompilerParams(has_side_effects=pltpu.SideEffectType.DATAFLOW_SIDE_EFFECTING),
    )(*started["srcs"], *started["lands"], *started["sems"], *after)
    return outs[:n], outs[n:]


def _swap_whole(arrs, name):
    n = len(arrs)

    def body(*refs):
        ins, outs = refs[:n], refs[n:2 * n]
        send_sems, recv_sems = refs[2 * n:]
        x, y, c = _mesh_pos()
        cps = [pltpu.make_async_remote_copy(src_ref=ins[i], dst_ref=outs[i], send_sem=send_sems.at[i],
                                            recv_sem=recv_sems.at[i], device_id=(x, y, 1 - c), device_id_type=MESH)
               for i in range(n)]
        for cp in cps:
            cp.start()
        for cp in cps:
            cp.wait()

    return pl.pallas_call(
        body, name=name, in_specs=[ANY] * n, out_specs=[ANY] * n,
        out_shape=[jax.ShapeDtypeStruct(a.shape, a.dtype) for a in arrs],
        scratch_shapes=[pltpu.SemaphoreType.DMA((n,)), pltpu.SemaphoreType.DMA((n,))],
    )(*arrs)


def _as_rows(a, lead):
    shp = a.shape
    rows = 1
    for s in shp[lead:-1]:
        rows *= s
    return a.reshape(shp[:lead] + (rows, shp[-1]))


def _row_tile(rows, cols, n_bufs):
    budget = (24 * 1024 * 1024) // (n_bufs * 2 * 4 * cols)
    return _pick(rows, max(2 * SUBLANES, budget), 2 * SUBLANES)


def _sum_devices(own, got, dev, name):
    n, rows, cols = got.shape
    tr = _row_tile(rows, cols, n + 2)

    def body(dev_ref, own_ref, got_ref, o_ref):
        mine = own_ref[...]
        acc = jnp.where(dev_ref[0] == 0, mine, got_ref[0])
        for k in range(1, n):
            acc = acc + jnp.where(dev_ref[0] == k, mine, got_ref[k])
        o_ref[...] = acc

    return pl.pallas_call(
        body, name=name,
        grid_spec=pltpu.PrefetchScalarGridSpec(
            num_scalar_prefetch=1, grid=(rows // tr,),
            in_specs=[pl.BlockSpec((tr, cols), lambda i, d: (i, 0)), pl.BlockSpec((n, tr, cols), lambda i, d: (0, i, 0))],
            out_specs=pl.BlockSpec((tr, cols), lambda i, d: (i, 0))),
        out_shape=jax.ShapeDtypeStruct((rows, cols), F32), compiler_params=_cp("parallel"),
    )(_scalar(dev), own, got)


def _scalar(i):
    return jnp.reshape(i, (1,)).astype(jnp.int32)


def _add_own_half(g, other, c, half_first, name):
    _, rows, cols = other.shape
    tr = _row_tile(rows, cols, 3)

    def body(c_ref, g_ref, o_ref, out_ref):
        out_ref[0] = (g_ref[0, 0] + o_ref[0]).astype(out_ref.dtype)

    if half_first:
        g_map = lambda k, i, c_ref: (c_ref[0], k, i, 0)
    else:
        g_map = lambda k, i, c_ref: (k, c_ref[0], i, 0)
    flat = pl.BlockSpec((1, tr, cols), lambda k, i, c_ref: (k, i, 0))
    return pl.pallas_call(
        body, name=name,
        grid_spec=pltpu.PrefetchScalarGridSpec(
            num_scalar_prefetch=1, grid=(N_CHIPS, rows // tr),
            in_specs=[pl.BlockSpec((1, 1, tr, cols), g_map), flat], out_specs=flat),
        out_shape=jax.ShapeDtypeStruct(other.shape, COMM_DTYPE), compiler_params=_cp("parallel", "parallel"),
    )(_scalar(c), g, other)


def _sum_chips(own, got, chip, name, transpose=False):
    _, rows, cols = own.shape
    tr = rows if transpose else _row_tile(rows, cols, N_CHIPS + 2)

    def body(chip_ref, p_ref, b_ref, o_ref):
        mine = p_ref[0].astype(F32)
        acc = jnp.where(chip_ref[0] == 0, mine, b_ref[0].astype(F32))
        for k in range(1, N_CHIPS):
            acc = acc + jnp.where(chip_ref[0] == k, mine, b_ref[k].astype(F32))
        o_ref[...] = acc.T if transpose else acc

    if transpose:
        out_spec, out_shape = pl.BlockSpec((cols, rows), lambda i, k_ref: (0, 0)), (cols, rows)
    else:
        out_spec, out_shape = pl.BlockSpec((tr, cols), lambda i, k_ref: (i, 0)), (rows, cols)
    return pl.pallas_call(
        body, name=name,
        grid_spec=pltpu.PrefetchScalarGridSpec(
            num_scalar_prefetch=1, grid=(rows // tr,),
            in_specs=[pl.BlockSpec((1, tr, cols), lambda i, k_ref: (k_ref[0], i, 0)),
                      pl.BlockSpec((N_CHIPS, tr, cols), lambda i, k_ref: (0, i, 0))],
            out_specs=out_spec),
        out_shape=jax.ShapeDtypeStruct(out_shape, F32), compiler_params=_cp("parallel"),
    )(_scalar(chip), own, got)


def _adam_math(w, g, m, v):
    nm = ADAM_B1 * m + (1.0 - ADAM_B1) * g
    nv = ADAM_B2 * v + (1.0 - ADAM_B2) * (g * g)
    m_hat = nm / (1.0 - ADAM_B1 ** ADAM_STEP)
    v_hat = nv / (1.0 - ADAM_B2 ** ADAM_STEP)
    return -ADAM_LR * (m_hat / (jnp.sqrt(v_hat) + ADAM_EPS) + ADAM_WD * w), nm, nv


def _adamw_pieces(w, m, v, mine, theirs, c, kind, name):
    shape = w.shape
    P = len(mine)
    ws, ms, vs = (t.reshape((P, -1, t.shape[-1])) for t in (w, m, v))
    _, R, C = ws.shape
    if kind == "rows":
        tr = _pick(R // 2, 512, SUBLANES)
    else:
        tr = _pick(R, 256 if kind in ("lo", "hi") else 512, SUBLANES)
    nt = R // tr
    nh = nt // 2

    def body(c_ref, w_ref, m_ref, v_ref, *refs):
        mine_refs, theirs_refs = refs[:P], refs[P:2 * P]
        g_ref, d_ref, nm_ref, nv_ref = refs[2 * P:]
        p, i, core = pl.program_id(0), pl.program_id(1), c_ref[0]

        def pick(refs_):
            out = refs_[0][...]
            for q in range(1, P):
                out = jnp.where(p == q, refs_[q][...], out)
            return out

        a, b = pick(mine_refs), pick(theirs_refs)
        if kind == "cols":
            gv = jnp.where(core == 0, jnp.concatenate([a, b], axis=1), jnp.concatenate([b, a], axis=1))
        else:
            own = {"lo": core == 0, "hi": core == 1, "rows": (i >= nh) == (core == 1)}[kind]
            gv = jnp.where(own, a, b)
        g_ref[0] = gv
        d_ref[0], nm_ref[0], nv_ref[0] = _adam_math(w_ref[0], gv, m_ref[0], v_ref[0])

    def piece_spec(q):
        tile = (lambda i: i - jnp.where(i >= nh, nh, 0)) if kind == "rows" else (lambda i: i)
        return pl.BlockSpec((tr, mine[q].shape[1]), lambda p, i, c_ref: (jnp.where(p == q, tile(i), 0), 0))

    full = pl.BlockSpec((1, tr, C), lambda p, i, c_ref: (p, i, 0))
    outs = pl.pallas_call(
        body, name=name,
        grid_spec=pltpu.PrefetchScalarGridSpec(num_scalar_prefetch=1, grid=(P, nt),
                                               in_specs=[full] * 3 + [piece_spec(q) for q in range(P)] * 2,
                                               out_specs=[full] * 4),
        out_shape=[jax.ShapeDtypeStruct((P, R, C), F32)] * 4, compiler_params=_cp("parallel", "arbitrary"),
    )(_scalar(c), ws, ms, vs, *mine, *theirs)
    return tuple(o.reshape(shape) for o in outs)


def _adamw(w, g, m, v, name):
    shape = w.shape
    ws, gs, ms, vs = (_as_rows(t, 0) for t in (w, g, m, v))
    rows, cols = ws.shape
    tr = _row_tile(rows, cols, 7)

    def body(w_ref, g_ref, m_ref, v_ref, d_ref, nm_ref, nv_ref):
        d_ref[...], nm_ref[...], nv_ref[...] = _adam_math(w_ref[...], g_ref[...], m_ref[...], v_ref[...])

    spec = pl.BlockSpec((tr, cols), lambda i: (i, 0))
    outs = pl.pallas_call(body, name=name, grid=(rows // tr,), in_specs=[spec] * 4, out_specs=[spec] * 3,
                          out_shape=[jax.ShapeDtypeStruct((rows, cols), F32)] * 3, compiler_params=_cp("parallel"))(ws, gs, ms, vs)
    return tuple(o.reshape(shape) for o in outs)


_BIG = ["ffn_w_gate", "ffn_w_up", "ffn_w_down", "dn_w_in", "dn_w_out", "sg_w_in", "sg_w_out"]
_SMALL_SHARDED = ["norm_g", "dn_conv_w", "sg_b_in", "sg_ln_g", "sg_ln_b"]
_SMALL_REPL = ["dn_a_log", "dn_dt_bias", "dn_norm_g", "sg_w_s", "sg_b_s"]
_WEIGHTS = ["norm_g", "ffn_w_gate", "ffn_w_up", "ffn_w_down", "dn_w_in", "dn_conv_w", "dn_a_log", "dn_dt_bias",
            "dn_norm_g", "dn_w_out", "sg_w_in", "sg_b_in", "sg_ln_g", "sg_ln_b", "sg_w_s", "sg_b_s", "sg_w_out"]
PACK_COLS = 1024


def _pack(arrs):
    flat = jnp.concatenate([a.reshape(-1) for a in arrs])
    pad = (-flat.shape[0]) % (SUBLANES * PACK_COLS)
    return jnp.pad(flat, (0, pad)).reshape(-1, PACK_COLS)


def _unpack(buf, shapes):
    flat = buf.reshape(-1)
    out, off = [], 0
    for s in shapes:
        n = math.prod(s)
        out.append(flat[off:off + n].reshape(s))
        off += n
    return out


def _as_halves(a):
    if a.shape[0] == 2:
        return a
    if a.shape[0] == 1:
        return a.reshape((2, a.shape[1] // 2) + a.shape[2:])
    return a.reshape((2, a.shape[0] // 2) + a.shape[1:])


def _with_own(gathered, own, chip):
    g = gathered.reshape((N_CHIPS,) + own.shape)
    return [jnp.where(chip == k, own, g[k]) for k in range(N_CHIPS)]


def _cat_shards(g, axis):
    return jnp.concatenate(list(g), axis=axis)


_GROUP_ORDER = ["ffn00", "dn", "ffn01", "ffn10", "sg", "ffn11"]


def _weight_groups(w):
    cast = {k: _mx(w[k]) for k in _BIG}
    groups = {"ffn%d%d" % (i, j): [cast["ffn_w_gate"][i, j].T, cast["ffn_w_up"][i, j].T, cast["ffn_w_down"][i, j]]
              for i, j in [(0, 0), (0, 1), (1, 0), (1, 1)]}
    groups["dn"] = [cast["dn_w_in"][0], cast["dn_w_out"][0]]
    groups["sg"] = [cast["sg_w_in"][0], cast["sg_w_out"][0]]
    return groups


def _ffn_weights(chip, own, gathered):
    pairs = [(a, g.reshape((N_CHIPS,) + a.shape)) for a, g in zip(own, gathered)]
    return {"chip": chip, "gate": pairs[0], "up": pairs[1], "down": pairs[2]}


def _group_matrices(group, shards):
    if group == "sg":
        return {"sg_win": _cat_shards(shards[0], 1), "sg_wout": _cat_shards(shards[1], 0)}
    dn_full = _cat_shards(shards[0], 1)
    W4 = 4 * DN_HEADS * DN_HEAD_DIM
    wba = jnp.zeros((D_MODEL, 2 * LANES), dn_full.dtype)
    wba = wba.at[:, :DN_HEADS].set(dn_full[:, W4:W4 + DN_HEADS])
    wba = wba.at[:, LANES:LANES + DN_HEADS].set(dn_full[:, W4 + DN_HEADS:])
    return {"dn_wqkvz": dn_full[:, :W4], "dn_wba": wba, "dn_wout": _cat_shards(shards[1], 0)}


def _split_cols(a, n):
    w = a.shape[-1] // n
    return [a[..., k * w:(k + 1) * w] for k in range(n)]


def _split_rows(a, n):
    h = a.shape[-2] // n
    return [a[..., k * h:(k + 1) * h, :] for k in range(n)]


_IJ = [(0, 0), (0, 1), (1, 0), (1, 1)]


def _group_grads(group, grads):
    def rows_by_chip(a):
        return a.reshape(N_CHIPS, 2, a.shape[0] // (2 * N_CHIPS), a.shape[1])

    if group.startswith("ffn"):
        tag = group[3:]
        t = grads["wguT" + tag]
        return (["wguT" + tag, "wd" + tag],
                [t.reshape(2, N_CHIPS, t.shape[0] // (2 * N_CHIPS), t.shape[1]), rows_by_chip(grads["wd" + tag])], [True, False])
    if group == "sg":
        return ["sg_w_inT", "sg_w_out"], [rows_by_chip(grads["sg_w_inT"]), rows_by_chip(grads["sg_w_out"])], [False, False]
    dn_in = jnp.stack([jnp.stack(_split_cols(hf, N_CHIPS)) for hf in _split_rows(grads["dn_w_in"], 2)])
    return ["dn_w_in", "dn_w_out"], [dn_in, rows_by_chip(grads["dn_w_out"])], [True, False]


_SHARD_PIECES = {
    "ffn_w_gate": (["wguT%d%d" % ij for ij in _IJ], "lo"),
    "ffn_w_up": (["wguT%d%d" % ij for ij in _IJ], "hi"),
    "ffn_w_down": (["wd%d%d" % ij for ij in _IJ], "rows"),
    "dn_w_in": (["dn_w_in"], "rows"),
    "dn_w_out": (["dn_w_out"], "rows"),
    "sg_w_in": (["sg_w_inT"], "cols"),
    "sg_w_out": (["sg_w_out"], "rows"),
}


def kernel(x, norm_g, ffn_w_gate, ffn_w_up, ffn_w_down, dn_w_in, dn_conv_w, dn_a_log, dn_dt_bias, dn_norm_g, dn_w_out, sg_w_in, sg_b_in, sg_ln_g, sg_ln_b, sg_w_s, sg_b_s, sg_w_out, loss_target, m_norm_g, m_ffn_w_gate, m_ffn_w_up, m_ffn_w_down, m_dn_w_in, m_dn_conv_w, m_dn_a_log, m_dn_dt_bias, m_dn_norm_g, m_dn_w_out, m_sg_w_in, m_sg_b_in, m_sg_ln_g, m_sg_ln_b, m_sg_w_s, m_sg_b_s, m_sg_w_out, v_norm_g, v_ffn_w_gate, v_ffn_w_up, v_ffn_w_down, v_dn_w_in, v_dn_conv_w, v_dn_a_log, v_dn_dt_bias, v_dn_norm_g, v_dn_w_out, v_sg_w_in, v_sg_b_in, v_sg_ln_g, v_sg_ln_b, v_sg_w_s, v_sg_b_s, v_sg_w_out):
    args = dict(locals())
    w = {k: args[k] for k in _WEIGHTS}
    mom = {k: args["m_" + k] for k in _WEIGHTS}
    var = {k: args["v_" + k] for k in _WEIGHTS}
    cx, cy, cc = _mesh_pos()
    chip = 2 * cx + cy

    small_shapes = [w[k].shape for k in _SMALL_SHARDED]
    groups = _weight_groups(w)
    own = groups[_GROUP_ORDER[0]] + [_pack([w[k] for k in _SMALL_SHARDED])]
    first = _allgather_chips([_as_halves(a) for a in own], "gather_first")
    started, after = {}, first[0]
    for g in _GROUP_ORDER[1:]:
        started[g] = _copies_start("gather", groups[g], after, "gather_start_" + g)
        after = started[g]["token"]
    small_k = [_unpack(pack, small_shapes) for pack in _with_own(first[-1], own[-1], chip)]
    p = {name: jnp.concatenate([small_k[k][i] for k in range(N_CHIPS)], axis=-1) for i, name in enumerate(_SMALL_SHARDED)}
    p = {k: (v if k == "norm_g" else v[0]) for k, v in p.items()}
    p["norm_g"] = p["norm_g"] + after[0, 0]
    for k in _SMALL_REPL:
        p[k] = w[k][0]

    def weights_for(group, after):
        if group == _GROUP_ORDER[0]:
            return _ffn_weights(chip, own[:-1], first[:-1])
        srcs, lands = _copies_wait(started[group], after, "gather_wait_" + group)
        if group.startswith("ffn"):
            return _ffn_weights(chip, srcs, lands)
        return _group_matrices(group, [_with_own(l, a, chip) for l, a in zip(lands, srcs)])

    mine, theirs, to_core, to_chips = {}, {}, [], []

    def send_to_chips(after):
        group, names, flags, swap = to_core.pop(0)
        halves, got = _copies_wait(swap, after, "swap_wait_" + group)
        pair_sum = [_add_own_half(h, o, cc, hf, "pair_sum_" + n) for n, h, o, hf in zip(names, halves, got, flags)]
        scatter = _copies_start("scatter", pair_sum, got[0], "reduce_start_" + group)
        to_chips.append((group, names, scatter))
        return scatter["token"]

    def finish(after):
        group, names, scatter = to_chips.pop(0)
        pair_sum, got = _copies_wait(scatter, after, "reduce_wait_" + group)
        half_sum = [_sum_chips(a, b, chip, "chip_sum_" + n, transpose=n == "sg_w_inT")
                    for n, a, b in zip(names, pair_sum, got)]
        other = _swap_whole(half_sum, "gather_core_pair_" + group)
        mine.update(zip(names, half_sum))
        theirs.update(zip(names, other))

    def grads_ready(group, grads):
        names, halves, flags = _group_grads(group, grads)
        swap = _copies_start("swap", halves, None, "swap_start_" + group, flags)
        token = swap["token"]
        if to_core:
            token = send_to_chips(token)
            if len(to_chips) > 1:
                finish(token)
        to_core.append((group, names, flags, swap))
        return token[0, 0]

    small_names = _SMALL_SHARDED + _SMALL_REPL
    small = {}

    def small_ready(grads, loss_part):
        parts = [grads[k] for k in small_names]
        small["shapes"] = [g.shape for g in parts] + [(1,)]
        pack = _pack(parts + [loss_part[0, :1]])
        small["exchange"] = _copies_start("all", [pack], None, "small_start")
        return small["exchange"]["token"]

    loss_part, grad_x, grads = _local_step(x[0], loss_target[0], p, weights_for, grads_ready, small_ready)
    token = send_to_chips(to_core[0][3]["token"])
    finish(token)
    (pack,), (packs,) = _copies_wait(small["exchange"], list(theirs.values()), "small_wait")
    summed = _sum_devices(pack, packs, 4 * cx + 2 * cy + cc, "small_sum")
    parts = _unpack(summed, small["shapes"])
    loss = parts[-1][0]
    grad = {}
    for i, k in enumerate(small_names):
        g = parts[i]
        if k in _SMALL_SHARDED:
            n = w[k].shape[-1]
            g = lax.dynamic_slice_in_dim(g, chip * n, n, axis=g.ndim - 1)
        grad[k] = g

    delta, new_m, new_v = {}, {}, {}

    def update(keys):
        for k in keys:
            names, kind = _SHARD_PIECES[k]
            turn = (lambda a: jnp.swapaxes(a, -1, -2)) if names[0].startswith("wguT") else (lambda a: a)
            outs = _adamw_pieces(turn(w[k]), turn(mom[k]), turn(var[k]), [mine[n] for n in names], [theirs[n] for n in names],
                                 cc, kind, "adamw_" + k)
            grad[k], delta[k], new_m[k], new_v[k] = (turn(o) for o in outs)

    shapes = [w[k].shape for k in small_names]
    d, nm, nv = _adamw(_pack([w[k] for k in small_names]), _pack([grad[k] for k in small_names]),
                       _pack([mom[k] for k in small_names]), _pack([var[k] for k in small_names]), "adamw_small")
    for k, a, b, c_ in zip(small_names, _unpack(d, shapes), _unpack(nm, shapes), _unpack(nv, shapes)):
        delta[k], new_m[k], new_v[k] = a, b, c_
    mixers = [k for k in _BIG if not k.startswith("ffn")]
    update(mixers)
    finish([d] + [delta[k] for k in mixers] + list(theirs.values()))
    update([k for k in _BIG if k.startswith("ffn")])

    return (loss, grad_x[None], *[grad[k] for k in _WEIGHTS], *[delta[k] for k in _WEIGHTS],
            *[new_m[k] for k in _WEIGHTS], *[new_v[k] for k in _WEIGHTS])
```

```python
import functools
import math

import jax
import jax.numpy as jnp
from jax import lax
from jax.experimental import pallas as pl
from jax.experimental.pallas import tpu as pltpu

F32 = jnp.float32
MXU_DTYPE = jnp.bfloat16
COMM_DTYPE = jnp.bfloat16
HI = lax.Precision.HIGHEST
TRI_PREC = lax.Precision.HIGH

D_MODEL = 1024
D_FF = 2816
RMS_EPS = 1e-6
LN_EPS = 1e-5
L2_EPS = 1e-6
DN_HEADS = 8
DN_HEAD_DIM = 128
DN_CONV = 4
DN_CHUNK = 64
SG_WIDTH = 2048
SG_GROUPS = 8
SG_CHUNK = 128
SG_GROUP_W = SG_WIDTH // SG_GROUPS
N_CHIPS = 4
N_DEV = 8
LANES = 128
SUBLANES = 8
VMEM_LIMIT = 56 * 1024 * 1024

ADAM_LR = 0.001
ADAM_B1 = 0.9
ADAM_B2 = 0.999
ADAM_EPS = 1e-08
ADAM_WD = 0.01
ADAM_STEP = 10

MESH = pl.DeviceIdType.MESH
ANY = pl.BlockSpec(memory_space=pl.ANY)


def _cp(*sem):
    return pltpu.CompilerParams(dimension_semantics=sem, vmem_limit_bytes=VMEM_LIMIT)


def _pick(n, pref, mult=LANES):
    best = None
    d = mult
    while d <= min(n, pref):
        if n % d == 0:
            best = d
        d += mult
    return best if best is not None else n


def _full(shape):
    nd = len(shape)
    return pl.BlockSpec(shape, lambda *_: (0,) * nd)


def _sigmoid(x):
    return 1.0 / (1.0 + jnp.exp(-x))


def _dot(a, b, dims, prec=None):
    return lax.dot_general(a, b, (dims, ((), ())), preferred_element_type=F32, precision=prec)


NN = ((1,), (0,))
NT = ((1,), (1,))
TN = ((0,), (0,))


def _mx(a):
    return a.astype(MXU_DTYPE)


def _rms_stat(x):
    return lax.rsqrt(jnp.mean(x * x, axis=-1, keepdims=True) + RMS_EPS)


def _rms_bwd(x, r, g, dy):
    xh = x * r
    dxh = dy * g
    dx = r * (dxh - xh * jnp.mean(dxh * xh, axis=-1, keepdims=True))
    return dx, jnp.sum(dy * xh, axis=0, keepdims=True)


def _mm(a, b, mode, name, out_dtype=F32, add=None, after=None):
    if mode == "tn":
        K, M = a.shape
        N = b.shape[1]
    elif mode == "nt":
        M, K = a.shape
        N = b.shape[0]
    else:
        M, K = a.shape
        N = b.shape[1]
    tn = _pick(N, 1024)
    if mode == "tn":
        tm = _pick(M, 1024 if tn <= 512 else 1408)
        tk = _pick(K, 1024, SUBLANES)
    else:
        tm = _pick(M, max(512, min(2048, (1024 * 1024) // tn)), SUBLANES)
        tk = _pick(K, 2048)
    nk = K // tk
    grid = (N // tn, M // tm, nk)
    if mode == "nn":
        a_spec = pl.BlockSpec((tm, tk), lambda j, i, k: (i, k))
        b_spec = pl.BlockSpec((tk, tn), lambda j, i, k: (k, j))
        dims = NN
    elif mode == "nt":
        a_spec = pl.BlockSpec((tm, tk), lambda j, i, k: (i, k))
        b_spec = pl.BlockSpec((tn, tk), lambda j, i, k: (j, k))
        dims = NT
    else:
        a_spec = pl.BlockSpec((tk, tm), lambda j, i, k: (k, i))
        b_spec = pl.BlockSpec((tk, tn), lambda j, i, k: (k, j))
        dims = TN
    o_spec = pl.BlockSpec((tm, tn), lambda j, i, k: (i, j))
    has_add = add is not None

    def body(*refs):
        a_ref, b_ref = refs[:2]
        add_ref = refs[2] if has_add else None
        o_ref, acc = refs[-2:]
        k = pl.program_id(2)

        @pl.when(k == 0)
        def _():
            acc[...] = add_ref[...] if has_add else jnp.zeros_like(acc)

        acc[...] += _dot(a_ref[...], b_ref[...], dims)

        @pl.when(k == nk - 1)
        def _():
            o_ref[...] = acc[...].astype(o_ref.dtype)

    ins = [a, b] + ([add] if has_add else []) + ([after] if after is not None else [])
    specs = [a_spec, b_spec] + ([o_spec] if has_add else []) + ([ANY] if after is not None else [])
    return pl.pallas_call(
        body, name=name, grid=grid, in_specs=specs, out_specs=o_spec,
        out_shape=jax.ShapeDtypeStruct((M, N), out_dtype),
        scratch_shapes=[pltpu.VMEM((tm, tn), F32)],
        compiler_params=_cp("parallel", "parallel", "arbitrary"),
    )(*ins)


def _ffn_weight_operands(wt):
    return [_scalar(wt["chip"])] , [wt["gate"][0], wt["gate"][1], wt["up"][0], wt["up"][1], wt["down"][0], wt["down"][1]]


def _load_ffn_weights(chip_ref, shard_refs, wgu_v, wd_v, sem):
    fs = wd_v.shape[0] // N_CHIPS

    @pl.when(pl.program_id(0) == 0)
    def _():
        me = chip_ref[0]
        waits = []
        for t, (dst, base) in enumerate([(wgu_v, 0), (wgu_v, wd_v.shape[0]), (wd_v, 0)]):
            own, gathered = shard_refs[2 * t], shard_refs[2 * t + 1]
            for k in range(N_CHIPS):
                slot = dst.at[pl.ds(base + k * fs, fs), :]
                s = sem.at[t * N_CHIPS + k]

                @pl.when(me == k)
                def _(own=own, slot=slot, s=s):
                    pltpu.make_async_copy(own, slot, s).start()

                @pl.when(me != k)
                def _(gathered=gathered, k=k, slot=slot, s=s):
                    pltpu.make_async_copy(gathered.at[k], slot, s).start()

                waits.append(pltpu.make_async_copy(own, slot, s))
        for cp in waits:
            cp.wait()


def _ffn_fwd(x, g0, g1, wt, name):
    T, D = x.shape
    F = N_CHIPS * wt["down"][0].shape[0]
    F2 = 2 * F
    tm = _pick(T, 256, SUBLANES)
    prefetch, shards = _ffn_weight_operands(wt)

    def body(chip_ref, x_ref, g0_ref, g1_ref, *refs):
        shard_refs = refs[:6]
        xo_ref, h_ref, gu_ref, y_ref, wgu_v, wd_v, sem = refs[6:]
        _load_ffn_weights(chip_ref, shard_refs, wgu_v, wd_v, sem)
        xv = x_ref[...]
        hb = _mx(xv * _rms_stat(xv) * g0_ref[...])
        h_ref[...] = hb
        gu = _dot(hb, wgu_v[...], NT)
        gu_ref[...] = gu.astype(gu_ref.dtype)
        g = gu[:, :F]
        u = gu[:, F:]
        a = _mx(g * _sigmoid(g) * u)
        y = _dot(a, wd_v[...], NN)
        y_ref[...] = y
        xo_ref[...] = xv + 0.5 * (y * _rms_stat(y) * g1_ref[...])

    row = lambda w: pl.BlockSpec((tm, w), lambda i, c: (i, 0))
    one = pl.BlockSpec((1, D), lambda i, c: (0, 0))
    return pl.pallas_call(
        body, name=name,
        grid_spec=pltpu.PrefetchScalarGridSpec(
            num_scalar_prefetch=1, grid=(T // tm,),
            in_specs=[row(D), one, one] + [ANY] * 6,
            out_specs=[row(D), row(D), row(F2), row(D)],
            scratch_shapes=[pltpu.VMEM((F2, D), MXU_DTYPE), pltpu.VMEM((F, D), MXU_DTYPE),
                            pltpu.SemaphoreType.DMA((3 * N_CHIPS,))]),
        out_shape=[jax.ShapeDtypeStruct((T, D), F32), jax.ShapeDtypeStruct((T, D), MXU_DTYPE),
                   jax.ShapeDtypeStruct((T, F2), MXU_DTYPE), jax.ShapeDtypeStruct((T, D), F32)],
        compiler_params=_cp("arbitrary"),
    )(*prefetch, x, g0, g1, *shards)


FFN_BWD_CHUNK = 2816


def _ffn_bwd(dxo, x, y, gu, g0, g1, wt, name):
    T, D = x.shape
    F2 = gu.shape[1]
    F = F2 // 2
    tm = _pick(T, 256, SUBLANES)
    fc = _pick(F, FFN_BWD_CHUNK)
    prefetch, shards = _ffn_weight_operands(wt)

    def body(chip_ref, dxo_ref, x_ref, y_ref, gu_ref, g0_ref, g1_ref, *refs):
        shard_refs = refs[:6]
        dx_ref, dy_ref, a_ref, dgu_ref, dg0_ref, dg1_ref, wgu_v, wd_v, sem = refs[6:]
        _load_ffn_weights(chip_ref, shard_refs, wgu_v, wd_v, sem)

        @pl.when(pl.program_id(0) == 0)
        def _():
            dg0_ref[...] = jnp.zeros_like(dg0_ref)
            dg1_ref[...] = jnp.zeros_like(dg1_ref)

        dxo_v = dxo_ref[...]
        yv = y_ref[...]
        dy, dg1 = _rms_bwd(yv, _rms_stat(yv), g1_ref[...], 0.5 * dxo_v)
        dg1_ref[...] += dg1
        dyb = _mx(dy)
        dy_ref[...] = dyb
        dh = jnp.zeros((tm, D), F32)
        for c in range(F // fc):
            lo, hi = c * fc, (c + 1) * fc
            da = _dot(dyb, wd_v[lo:hi, :], NT)
            g = gu_ref[:, lo:hi].astype(F32)
            u = gu_ref[:, F + lo:F + hi].astype(F32)
            s = _sigmoid(g)
            sg = g * s
            a_ref[:, lo:hi] = _mx(sg * u)
            dg = _mx(da * u * (s * (1.0 + g * (1.0 - s))))
            du = _mx(da * sg)
            dgu_ref[:, lo:hi] = dg
            dgu_ref[:, F + lo:F + hi] = du
            dh = dh + _dot(dg, wgu_v[lo:hi, :], NN) + _dot(du, wgu_v[F + lo:F + hi, :], NN)
        xv = x_ref[...]
        dx, dg0 = _rms_bwd(xv, _rms_stat(xv), g0_ref[...], dh)
        dg0_ref[...] += dg0
        dx_ref[...] = dxo_v + dx

    row = lambda w: pl.BlockSpec((tm, w), lambda i, c: (i, 0))
    one = pl.BlockSpec((1, D), lambda i, c: (0, 0))
    return pl.pallas_call(
        body, name=name,
        grid_spec=pltpu.PrefetchScalarGridSpec(
            num_scalar_prefetch=1, grid=(T // tm,),
            in_specs=[row(D), row(D), row(D), row(F2), one, one] + [ANY] * 6,
            out_specs=[row(D), row(D), row(F), row(F2), one, one],
            scratch_shapes=[pltpu.VMEM((F2, D), MXU_DTYPE), pltpu.VMEM((F, D), MXU_DTYPE),
                            pltpu.SemaphoreType.DMA((3 * N_CHIPS,))]),
        out_shape=[jax.ShapeDtypeStruct((T, D), F32), jax.ShapeDtypeStruct((T, D), MXU_DTYPE),
                   jax.ShapeDtypeStruct((T, F), MXU_DTYPE), jax.ShapeDtypeStruct((T, F2), MXU_DTYPE),
                   jax.ShapeDtypeStruct((1, D), F32), jax.ShapeDtypeStruct((1, D), F32)],
        compiler_params=_cp("arbitrary"),
    )(*prefetch, dxo, x, y, gu, g0, g1, *shards)


def _norm_fwd(x, g, name):
    T, D = x.shape
    tm = _pick(T, 512, SUBLANES)

    def body(x_ref, g_ref, h_ref):
        xv = x_ref[...]
        h_ref[...] = _mx(xv * _rms_stat(xv) * g_ref[...])

    row = pl.BlockSpec((tm, D), lambda i: (i, 0))
    return pl.pallas_call(body, name=name, grid=(T // tm,), in_specs=[row, _full((1, D))], out_specs=row,
                          out_shape=jax.ShapeDtypeStruct((T, D), MXU_DTYPE), compiler_params=_cp("parallel"))(x, g)


def _out_proj_postnorm(a, b, x, g, name):
    T, K = a.shape
    D = b.shape[1]
    tm = _pick(T, 512, SUBLANES)

    def body(a_ref, b_ref, x_ref, g_ref, m_ref, o_ref):
        mv = _dot(a_ref[...], b_ref[...], NN)
        m_ref[...] = mv
        o_ref[...] = x_ref[...] + mv * _rms_stat(mv) * g_ref[...]

    row = lambda w: pl.BlockSpec((tm, w), lambda i: (i, 0))
    return pl.pallas_call(body, name=name, grid=(T // tm,),
                          in_specs=[row(K), _full((K, D)), row(D), _full((1, D))], out_specs=[row(D), row(D)],
                          out_shape=[jax.ShapeDtypeStruct((T, D), F32)] * 2, compiler_params=_cp("parallel"))(a, b, x, g)


def _postnorm_bwd_dgrad(dxo, m, g, b, name):
    T, D = m.shape
    K = b.shape[0]
    tm = _pick(T, 512, SUBLANES)

    def body(dxo_ref, m_ref, g_ref, b_ref, dm_ref, dg_ref, da_ref):
        @pl.when(pl.program_id(0) == 0)
        def _():
            dg_ref[...] = jnp.zeros_like(dg_ref)

        mv = m_ref[...]
        dm, dg = _rms_bwd(mv, _rms_stat(mv), g_ref[...], dxo_ref[...])
        dg_ref[...] += dg
        dmb = _mx(dm)
        dm_ref[...] = dmb
        da_ref[...] = _dot(dmb, b_ref[...], NT)

    row = lambda w: pl.BlockSpec((tm, w), lambda i: (i, 0))
    return pl.pallas_call(body, name=name, grid=(T // tm,), in_specs=[row(D), row(D), _full((1, D)), _full((K, D))],
                          out_specs=[row(D), _full((1, D)), row(K)],
                          out_shape=[jax.ShapeDtypeStruct((T, D), MXU_DTYPE), jax.ShapeDtypeStruct((1, D), F32),
                                     jax.ShapeDtypeStruct((T, K), F32)],
                          compiler_params=_cp("arbitrary"))(dxo, m, g, b)


def _dgrad_prenorm_bwd(a, b, add, dxo, x, g, name):
    T, K = a.shape
    D = b.shape[0]
    tm = _pick(T, 512, SUBLANES)
    tk = _pick(K, 2048)
    nk = K // tk
    has_add = add is not None

    def body(*refs):
        a_ref, b_ref = refs[:2]
        add_ref = refs[2] if has_add else None
        dxo_ref, x_ref, g_ref, dx_ref, dg_ref, acc = refs[-6:]
        i, k = pl.program_id(0), pl.program_id(1)

        @pl.when((i == 0) & (k == 0))
        def _():
            dg_ref[...] = jnp.zeros_like(dg_ref)

        @pl.when(k == 0)
        def _():
            acc[...] = add_ref[...] if has_add else jnp.zeros_like(acc)

        acc[...] += _dot(a_ref[...], b_ref[...], NT)

        @pl.when(k == nk - 1)
        def _():
            xv = x_ref[...]
            dx, dg = _rms_bwd(xv, _rms_stat(xv), g_ref[...], acc[...])
            dg_ref[...] += dg
            dx_ref[...] = dxo_ref[...] + dx

    row = pl.BlockSpec((tm, D), lambda i, k: (i, 0))
    one = pl.BlockSpec((1, D), lambda i, k: (0, 0))
    ins = [a, b] + ([add] if has_add else []) + [dxo, x, g]
    specs = ([pl.BlockSpec((tm, tk), lambda i, k: (i, k)), pl.BlockSpec((D, tk), lambda i, k: (0, k))]
             + ([row] if has_add else []) + [row, row, one])
    return pl.pallas_call(body, name=name, grid=(T // tm, nk), in_specs=specs, out_specs=[row, one],
                          out_shape=[jax.ShapeDtypeStruct((T, D), F32), jax.ShapeDtypeStruct((1, D), F32)],
                          scratch_shapes=[pltpu.VMEM((tm, D), F32)],
                          compiler_params=_cp("arbitrary", "arbitrary"))(*ins)


def _loss_fwd_bwd(y, target, name):
    T, D = y.shape
    tm = _pick(T, 512, SUBLANES)

    def body(y_ref, t_ref, l_ref, dy_ref):
        @pl.when(pl.program_id(0) == 0)
        def _():
            l_ref[...] = jnp.zeros_like(l_ref)

        e = y_ref[...] - t_ref[...]
        dy_ref[...] = e * (1.0 / D)
        l_ref[...] += 0.5 * jnp.sum(jnp.mean(e * e, axis=-1, keepdims=True), axis=0, keepdims=True)

    row = pl.BlockSpec((tm, D), lambda i: (i, 0))
    return pl.pallas_call(body, name=name, grid=(T // tm,), in_specs=[row, row],
                          out_specs=[_full((SUBLANES, LANES)), row],
                          out_shape=[jax.ShapeDtypeStruct((SUBLANES, LANES), F32), jax.ShapeDtypeStruct((T, D), F32)],
                          compiler_params=_cp("arbitrary"))(y, target)


DN_ROWS = 512


def _shift_down(prev8, cur, s):
    n = cur.shape[0]
    xx = jnp.concatenate([prev8, cur], axis=0)
    return pltpu.roll(xx, s, 0)[SUBLANES:SUBLANES + n, :]


def _shift_up(cur, next8, s):
    n = cur.shape[0]
    xx = jnp.concatenate([cur, next8], axis=0)
    return pltpu.roll(xx, n + SUBLANES - s, 0)[:n, :]


def _conv_tile(x_ref, w, r, rows):
    start = pl.multiple_of(r * rows, SUBLANES)
    cur = x_ref[pl.ds(start, rows), :]
    pstart = pl.multiple_of(jnp.maximum(start - SUBLANES, 0), SUBLANES)
    prev8 = jnp.where(r == 0, 0.0, x_ref[pl.ds(pstart, SUBLANES), :])
    taps = [_shift_down(prev8, cur, DN_CONV - 1 - j) if j < DN_CONV - 1 else cur for j in range(DN_CONV)]
    c = taps[0] * w[0:1, :]
    for j in range(1, DN_CONV):
        c = c + taps[j] * w[j:j + 1, :]
    return c, taps


def _dn_prep_fwd(proj, conv_w, name):
    T = proj.shape[0]
    W = DN_HEADS * DN_HEAD_DIM
    rows = min(DN_ROWS, T)
    n_inner = T // rows
    scale = DN_HEAD_DIM ** -0.5

    def body(x_ref, w_ref, o_ref):
        cb = pl.program_id(0)
        w = w_ref[...]
        is_qk = cb < 2 * DN_HEADS
        post = jnp.where(cb < DN_HEADS, scale, 1.0)

        def step(r, carry):
            c, _ = _conv_tile(x_ref, w, r, rows)
            s = c * _sigmoid(c)
            rinv = lax.rsqrt(jnp.sum(s * s, axis=-1, keepdims=True) + L2_EPS)
            o_ref[pl.ds(pl.multiple_of(r * rows, SUBLANES), rows), :] = jnp.where(is_qk, s * rinv * post, s)
            return carry

        lax.fori_loop(0, n_inner, step, 0)

    col = pl.BlockSpec((T, LANES), lambda j: (0, j))
    return pl.pallas_call(body, name=name, grid=(3 * W // LANES,),
                          in_specs=[col, pl.BlockSpec((DN_CONV, LANES), lambda j: (0, j))], out_specs=col,
                          out_shape=jax.ShapeDtypeStruct((T, 3 * W), F32), compiler_params=_cp("parallel"))(proj, conv_w)


def _dn_prep_bwd(proj, conv_w, dqkv, name):
    T = proj.shape[0]
    W = DN_HEADS * DN_HEAD_DIM
    rows = min(DN_ROWS, T)
    n_inner = T // rows
    scale = DN_HEAD_DIM ** -0.5

    def body(x_ref, w_ref, dy_ref, dx_ref, dw_ref, dc_scr):
        cb = pl.program_id(0)
        w = w_ref[...]
        is_qk = cb < 2 * DN_HEADS
        post = jnp.where(cb < DN_HEADS, scale, 1.0)

        def step1(r, dws):
            c, taps = _conv_tile(x_ref, w, r, rows)
            sg = _sigmoid(c)
            s = c * sg
            rinv = lax.rsqrt(jnp.sum(s * s, axis=-1, keepdims=True) + L2_EPS)
            dy = dy_ref[pl.ds(pl.multiple_of(r * rows, SUBLANES), rows), :]
            yn = s * rinv
            dyn = dy * post
            ds_qk = rinv * (dyn - yn * jnp.sum(dyn * yn, axis=-1, keepdims=True))
            ds = jnp.where(is_qk, ds_qk, dy)
            dc = ds * (sg * (1.0 + c * (1.0 - sg)))
            dc_scr[pl.ds(pl.multiple_of(r * rows, SUBLANES), rows), :] = dc
            return tuple(dws[j] + jnp.sum(dc * taps[j], axis=0, keepdims=True) for j in range(DN_CONV))

        zero = jnp.zeros((1, LANES), F32)
        dws = lax.fori_loop(0, n_inner, step1, (zero,) * DN_CONV)
        for j in range(DN_CONV):
            dw_ref[j:j + 1, :] = dws[j]

        def step2(r, carry):
            start = pl.multiple_of(r * rows, SUBLANES)
            cur = dc_scr[pl.ds(start, rows), :]
            nstart = pl.multiple_of(jnp.minimum(start + rows, T - SUBLANES), SUBLANES)
            next8 = jnp.where(r == n_inner - 1, 0.0, dc_scr[pl.ds(nstart, SUBLANES), :])
            dx = cur * w[DN_CONV - 1:DN_CONV, :]
            for j in range(DN_CONV - 1):
                dx = dx + _shift_up(cur, next8, DN_CONV - 1 - j) * w[j:j + 1, :]
            dx_ref[pl.ds(start, rows), :] = _mx(dx)
            return carry

        lax.fori_loop(0, n_inner, step2, 0)

    col = pl.BlockSpec((T, LANES), lambda j: (0, j))
    wspec = pl.BlockSpec((DN_CONV, LANES), lambda j: (0, j))
    return pl.pallas_call(body, name=name, grid=(3 * W // LANES,), in_specs=[col, wspec, col], out_specs=[col, wspec],
                          out_shape=[jax.ShapeDtypeStruct((T, 3 * W), MXU_DTYPE), jax.ShapeDtypeStruct((DN_CONV, 3 * W), F32)],
                          scratch_shapes=[pltpu.VMEM((T, LANES), F32)], compiler_params=_cp("parallel"))(proj, conv_w, dqkv)


def _softplus(x):
    return jnp.maximum(x, 0.0) + jnp.log(1.0 + jnp.exp(-jnp.abs(x)))


def _dn_gate_fwd(ba, a_log, dt_bias, name):
    T = ba.shape[0]
    tm = _pick(T, 1024, SUBLANES)

    def body(ba_ref, al_ref, dt_ref, beta_ref, g_ref):
        beta_ref[...] = _sigmoid(ba_ref[:, :LANES])
        g_ref[...] = -jnp.exp(al_ref[...]) * _softplus(ba_ref[:, LANES:] + dt_ref[...])

    row = lambda w: pl.BlockSpec((tm, w), lambda i: (i, 0))
    return pl.pallas_call(body, name=name, grid=(T // tm,), in_specs=[row(2 * LANES), _full((1, LANES)), _full((1, LANES))],
                          out_specs=[row(LANES), row(LANES)],
                          out_shape=[jax.ShapeDtypeStruct((T, LANES), F32)] * 2, compiler_params=_cp("parallel"))(ba, a_log, dt_bias)


def _dn_gate_bwd(ba, a_log, dt_bias, dbeta, dg, name):
    T = ba.shape[0]
    tm = _pick(T, 1024, SUBLANES)

    def body(ba_ref, al_ref, dt_ref, dbeta_ref, dg_ref, dba_ref, dal_ref, ddt_ref):
        @pl.when(pl.program_id(0) == 0)
        def _():
            dal_ref[...] = jnp.zeros_like(dal_ref)
            ddt_ref[...] = jnp.zeros_like(ddt_ref)

        beta = _sigmoid(ba_ref[:, :LANES])
        dba_ref[:, :LANES] = _mx(dbeta_ref[...] * beta * (1.0 - beta))
        pre = ba_ref[:, LANES:] + dt_ref[...]
        ea = jnp.exp(al_ref[...])
        dgv = dg_ref[...]
        da = dgv * (-ea) * _sigmoid(pre)
        dba_ref[:, LANES:] = _mx(da)
        ddt_ref[...] += jnp.sum(da, axis=0, keepdims=True)
        dal_ref[...] += jnp.sum(dgv * (-ea) * _softplus(pre), axis=0, keepdims=True)

    row = lambda w: pl.BlockSpec((tm, w), lambda i: (i, 0))
    one = _full((1, LANES))
    return pl.pallas_call(body, name=name, grid=(T // tm,), in_specs=[row(2 * LANES), one, one, row(LANES), row(LANES)],
                          out_specs=[row(2 * LANES), one, one],
                          out_shape=[jax.ShapeDtypeStruct((T, 2 * LANES), MXU_DTYPE), jax.ShapeDtypeStruct((1, LANES), F32),
                                     jax.ShapeDtypeStruct((1, LANES), F32)],
                          compiler_params=_cp("arbitrary"))(ba, a_log, dt_bias, dbeta, dg)


def _tri(c, strict):
    i = lax.broadcasted_iota(jnp.int32, (c, c), 0)
    j = lax.broadcasted_iota(jnp.int32, (c, c), 1)
    return (i > j) if strict else (i >= j)


def _inv_unit_lower(ls):
    c = ls[0].shape[0]
    i = lax.broadcasted_iota(jnp.int32, (c, c), 0)
    j = lax.broadcasted_iota(jnp.int32, (c, c), 1)
    eye = jnp.where(i == j, 1.0, 0.0)
    facs = [[eye - l for l in ls]]
    cur = ls
    for _ in range(int(math.log2(c)) - 1):
        cur = [_dot(p, p, NN, TRI_PREC) for p in cur]
        facs.append([eye + p for p in cur])
    while len(facs) > 1:
        nxt = [[_dot(a, b, NN, TRI_PREC) for a, b in zip(facs[t], facs[t + 1])] for t in range(0, len(facs) - 1, 2)]
        if len(facs) % 2:
            nxt.append(facs[-1])
        facs = nxt
    return facs[0]


def _chunk_gates(g_blk):
    c = g_blk.shape[0]
    gcs = _dot(jnp.where(_tri(c, False), 1.0, 0.0), g_blk, NN, HI)
    return gcs, gcs.T


def _head_chunk(h, qh, kh, vh, beta_blk, gcs, gcs_t):
    c = qh.shape[0]
    incl = _tri(c, False)
    gc_col = gcs[:, h:h + 1]
    gc_row = gcs_t[h:h + 1, :]
    gc_last = gcs_t[h:h + 1, c - 1:c]
    dec = jnp.where(incl, jnp.exp(jnp.where(incl, gc_col - gc_row, 0.0)), 0.0)
    gam = jnp.exp(gc_col)
    rr = jnp.exp(gc_last - gc_col)
    gl = jnp.exp(gc_last)
    b = beta_blk[:, h:h + 1]
    kb = kh * b
    vb = vh * b
    both = _dot(jnp.concatenate([_mx(kb), _mx(qh)], axis=0), _mx(kh), NT)
    lmat = jnp.where(_tri(c, True), both[:c] * dec, 0.0)
    pmat = jnp.where(incl, both[c:] * dec, 0.0)
    return dict(dec=dec, gam=gam, rr=rr, gl=gl, b=b, kb=kb, vb=vb, lmat=lmat, pmat=pmat)


def _solve_uw(tinv, q):
    return _dot(tinv, jnp.concatenate([q["vb"], q["kb"] * q["gam"]], axis=1), NN, TRI_PREC)


def _dn_scan_fwd(qkv, beta, g, proj, norm_g, name):
    T = qkv.shape[0]
    C, H, Dh = DN_CHUNK, DN_HEADS, DN_HEAD_DIM
    W = H * Dh
    N = T // C

    def body(q_ref, k_ref, v_ref, beta_ref, g_ref, z_ref, ng_ref, og_ref, o_ref, tinv_ref, s_ref, state):
        @pl.when(pl.program_id(0) == 0)
        def _():
            state[...] = jnp.zeros_like(state)

        gcs, gcs_t = _chunk_gates(g_ref[...])
        beta_blk = beta_ref[...]
        ng = ng_ref[...]
        heads = range(H)
        cs = [slice(h * Dh, (h + 1) * Dh) for h in heads]
        qs = [_head_chunk(h, q_ref[:, cs[h]], k_ref[:, cs[h]], v_ref[:, cs[h]], beta_blk, gcs, gcs_t) for h in heads]
        tinvs = _inv_unit_lower([q["lmat"] for q in qs])
        for h in heads:
            tinv_ref[h] = tinvs[h]
        uws = [_solve_uw(tinvs[h], qs[h]) for h in heads]
        ss = [state[h] for h in heads]
        for h in heads:
            s_ref[0, h] = ss[h]
        sbs = [_mx(s) for s in ss]
        vnbs = [_mx(uws[h][:, :Dh] - _dot(_mx(uws[h][:, Dh:]), sbs[h], NN)) for h in heads]
        os_ = [_dot(jnp.concatenate([_mx(q_ref[:, cs[h]] * qs[h]["gam"]), _mx(qs[h]["pmat"])], axis=1),
                    jnp.concatenate([sbs[h], vnbs[h]], axis=0), NN) for h in heads]
        for h in heads:
            state[h] = ss[h] * qs[h]["gl"] + _dot(_mx((k_ref[:, cs[h]] * qs[h]["rr"]).T), vnbs[h], NN)
        for h in heads:
            o = os_[h]
            o_ref[:, cs[h]] = o
            zh = z_ref[:, cs[h]]
            og_ref[:, cs[h]] = _mx(o * _rms_stat(o) * ng * (zh * _sigmoid(zh)))

    blk = lambda j: pl.BlockSpec((C, W), lambda n: (n, j))
    small = pl.BlockSpec((C, LANES), lambda n: (n, 0))
    return pl.pallas_call(
        body, name=name, grid=(N,),
        in_specs=[blk(0), blk(1), blk(2), small, small, blk(3), _full((1, Dh))],
        out_specs=[blk(0), blk(0), pl.BlockSpec((H, C, C), lambda n: (0, n, 0)),
                   pl.BlockSpec((1, H, Dh, Dh), lambda n: (n, 0, 0, 0))],
        out_shape=[jax.ShapeDtypeStruct((T, W), MXU_DTYPE), jax.ShapeDtypeStruct((T, W), F32),
                   jax.ShapeDtypeStruct((H, T, C), F32), jax.ShapeDtypeStruct((N, H, Dh, Dh), F32)],
        scratch_shapes=[pltpu.VMEM((H, Dh, Dh), F32)],
        compiler_params=_cp("arbitrary"),
    )(qkv, qkv, qkv, beta, g, proj, norm_g)


def _dn_scan_bwd(qkv, beta, g, proj, norm_g, o, tinv, s_all, dog, name):
    T = qkv.shape[0]
    C, H, Dh = DN_CHUNK, DN_HEADS, DN_HEAD_DIM
    W = H * Dh
    N = T // C

    def body(q_ref, k_ref, v_ref, beta_ref, g_ref, z_ref, ng_ref, o_ref, tinv_ref, s_ref, dog_ref,
             dqkv_ref, dbeta_ref, dg_ref, dz_ref, dng_ref, dstate):
        @pl.when(pl.program_id(0) == 0)
        def _():
            dstate[...] = jnp.zeros_like(dstate)
            dng_ref[...] = jnp.zeros_like(dng_ref)

        gcs, gcs_t = _chunk_gates(g_ref[...])
        beta_blk = beta_ref[...]
        ng = ng_ref[...]
        incl = _tri(C, False)
        strict = _tri(C, True)
        lane = lax.broadcasted_iota(jnp.int32, (C, LANES), 1)
        rowi = lax.broadcasted_iota(jnp.int32, (C, 1), 0)
        ones = jnp.ones((C, LANES), F32)
        dbeta_acc = jnp.zeros((C, LANES), F32)
        dgc_acc = jnp.zeros((C, LANES), F32)
        dng_acc = jnp.zeros((1, Dh), F32)
        cs = [slice(h * Dh, (h + 1) * Dh) for h in range(H)]
        rsum = lambda t: jnp.sum(t, axis=1, keepdims=True)
        for heads in (range(0, H // 2), range(H // 2, H)):
            dobs = {}
            for h in heads:
                oh, zh, dogh = o_ref[:, cs[h]], z_ref[:, cs[h]], dog_ref[:, cs[h]]
                rstat = _rms_stat(oh)
                sz = _sigmoid(zh)
                dz_ref[:, cs[h]] = _mx(dogh * (oh * rstat * ng) * (sz * (1.0 + zh * (1.0 - sz))))
                do, dng = _rms_bwd(oh, rstat, ng, dogh * (zh * sz))
                dng_acc = dng_acc + dng
                dobs[h] = _mx(do)
            qs = {h: _head_chunk(h, q_ref[:, cs[h]], k_ref[:, cs[h]], v_ref[:, cs[h]], beta_blk, gcs, gcs_t) for h in heads}
            tms = {h: tinv_ref[h] for h in heads}
            uws = {h: _solve_uw(tms[h], qs[h]) for h in heads}
            ss = {h: s_ref[0, h] for h in heads}
            sbs = {h: _mx(ss[h]) for h in heads}
            wbs = {h: _mx(uws[h][:, Dh:]) for h in heads}
            vnbs = {h: _mx(uws[h][:, :Dh] - _dot(wbs[h], sbs[h], NN)) for h in heads}
            dsns = {h: dstate[h] for h in heads}
            dsbs = {h: _mx(dsns[h]) for h in heads}
            dvnews = {h: _dot(_mx(qs[h]["pmat"]), dobs[h], TN) + _dot(_mx(k_ref[:, cs[h]] * qs[h]["rr"]), dsbs[h], NN)
                      for h in heads}
            dvb16s = {h: _mx(dvnews[h]) for h in heads}
            dps = {h: jnp.where(incl, _dot(dobs[h], vnbs[h], NT), 0.0) for h in heads}
            dqds = {h: _dot(dobs[h], sbs[h], NT) for h in heads}
            dkds = {h: _dot(vnbs[h], dsbs[h], NT) for h in heads}
            dgls = {h: jnp.sum(rsum(ss[h] * dsns[h]), axis=0, keepdims=True) for h in heads}
            dws = {h: -_dot(dvb16s[h], sbs[h], NT) for h in heads}
            for h in heads:
                dstate[h] = qs[h]["gl"] * dsns[h] + _dot(
                    jnp.concatenate([_mx(q_ref[:, cs[h]] * qs[h]["gam"]), -wbs[h]], axis=0),
                    jnp.concatenate([dobs[h], dvb16s[h]], axis=0), TN)
            dsols = {h: _dot(tms[h], jnp.concatenate([dvnews[h], dws[h]], axis=1), TN, TRI_PREC) for h in heads}
            dvbs = {h: dsols[h][:, :Dh] for h in heads}
            dkbgs = {h: dsols[h][:, Dh:] for h in heads}
            dls = {h: jnp.where(strict, -_dot(dsols[h], uws[h], NT, TRI_PREC), 0.0) for h in heads}
            mmats = {h: dls[h] * qs[h]["lmat"] + dps[h] * qs[h]["pmat"] for h in heads}
            dgcs = {h: rsum(mmats[h]) - _dot(mmats[h], ones, TN, TRI_PREC)[:, :1] for h in heads}
            dboth = {h: jnp.concatenate([_mx(dls[h] * qs[h]["dec"]), _mx(dps[h] * qs[h]["dec"])], axis=0) for h in heads}
            for h in heads:
                q = qs[h]
                qh, kh, vh = q_ref[:, cs[h]], k_ref[:, cs[h]], v_ref[:, cs[h]]
                gam, rr, b, kb = q["gam"], q["rr"], q["b"], q["kb"]
                on_k = _dot(dboth[h], _mx(kh), NN)
                dkb = on_k[:C] + dkbgs[h] * gam
                dk = _dot(dboth[h], jnp.concatenate([_mx(kb), _mx(qh)], axis=0), TN) + dkb * b + dkds[h] * rr
                dq = on_k[C:] + dqds[h] * gam
                dgam = rsum(dkbgs[h] * kb) + rsum(dqds[h] * qh)
                dr = rsum(dkds[h] * kh)
                dgc_last = jnp.sum(dr * rr, axis=0, keepdims=True) + dgls[h] * q["gl"]
                dgc = dgcs[h] + dgam * gam - dr * rr + jnp.where(rowi == C - 1, dgc_last, 0.0)
                dbeta = rsum(dvbs[h] * vh) + rsum(dkb * kh)
                dqkv_ref[:, cs[h]] = dq
                dqkv_ref[:, W + h * Dh:W + (h + 1) * Dh] = dk
                dqkv_ref[:, 2 * W + h * Dh:2 * W + (h + 1) * Dh] = dvbs[h] * b
                dbeta_acc = jnp.where(lane == h, dbeta, dbeta_acc)
                dgc_acc = jnp.where(lane == h, dgc, dgc_acc)
        dbeta_ref[...] = dbeta_acc
        dg_ref[...] = _dot(jnp.where(incl, 1.0, 0.0), dgc_acc, TN, HI)
        dng_ref[...] += dng_acc

    rev = lambda n: N - 1 - n
    blk = lambda j: pl.BlockSpec((C, W), lambda n: (rev(n), j))
    small = pl.BlockSpec((C, LANES), lambda n: (rev(n), 0))
    return pl.pallas_call(
        body, name=name, grid=(N,),
        in_specs=[blk(0), blk(1), blk(2), small, small, blk(3), _full((1, Dh)), blk(0),
                  pl.BlockSpec((H, C, C), lambda n: (0, rev(n), 0)),
                  pl.BlockSpec((1, H, Dh, Dh), lambda n: (rev(n), 0, 0, 0)), blk(0)],
        out_specs=[pl.BlockSpec((C, 3 * W), lambda n: (rev(n), 0)), small, small, blk(0), _full((1, Dh))],
        out_shape=[jax.ShapeDtypeStruct((T, 3 * W), F32), jax.ShapeDtypeStruct((T, LANES), F32),
                   jax.ShapeDtypeStruct((T, LANES), F32), jax.ShapeDtypeStruct((T, W), MXU_DTYPE),
                   jax.ShapeDtypeStruct((1, Dh), F32)],
        scratch_shapes=[pltpu.VMEM((H, Dh, Dh), F32)],
        compiler_params=_cp("arbitrary"),
    )(qkv, qkv, qkv, beta, g, proj, norm_g, o, tinv, s_all, dog)


_INV_SQRT2 = 0.7071067811865476
_INV_SQRT_2PI = 0.3989422804014327


def _sg_recompute(zp_ref, bin_ref, lng_ref, lnb_ref):
    E = SG_WIDTH
    zin = zp_ref[...] + bin_ref[...]
    cdf = 0.5 * (1.0 + lax.erf(zin * _INV_SQRT2))
    zz = zin * cdf
    u = zz[:, :E]
    vp = zz[:, E:]
    mu = jnp.mean(vp, axis=-1, keepdims=True)
    xc = vp - mu
    rstd = lax.rsqrt(jnp.mean(xc * xc, axis=-1, keepdims=True) + LN_EPS)
    xhat = xc * rstd
    v = xhat * lng_ref[...] + lnb_ref[...]
    return zin, cdf, u, xhat, rstd, v


def _sg_masked_ws(ws_ref, g):
    return _mx(jnp.where(_tri(SG_CHUNK, False), ws_ref[g], 0.0))


def _sg_fwd(zpre, b_in, ln_g, ln_b, w_s, b_s_t, name):
    T = zpre.shape[0]
    E, G, C, GW = SG_WIDTH, SG_GROUPS, SG_CHUNK, SG_GROUP_W

    def body(zp_ref, bin_ref, lng_ref, lnb_ref, ws_ref, bst_ref, um_ref):
        _, _, u, _, _, v = _sg_recompute(zp_ref, bin_ref, lng_ref, lnb_ref)
        bst = bst_ref[...]
        for g in range(G):
            cs = slice(g * GW, (g + 1) * GW)
            mixed = _dot(_sg_masked_ws(ws_ref, g), _mx(v[:, cs]), NN) + bst[:, g:g + 1]
            um_ref[:, cs] = _mx(u[:, cs] * mixed)

    return pl.pallas_call(
        body, name=name, grid=(T // C,),
        in_specs=[pl.BlockSpec((C, 2 * E), lambda n: (n, 0)), _full((1, 2 * E)), _full((1, E)), _full((1, E)),
                  _full((G, C, C)), _full((C, LANES))],
        out_specs=pl.BlockSpec((C, E), lambda n: (n, 0)),
        out_shape=jax.ShapeDtypeStruct((T, E), MXU_DTYPE), compiler_params=_cp("parallel"),
    )(zpre, b_in, ln_g, ln_b, w_s, b_s_t)


def _sg_bwd(zpre, b_in, ln_g, ln_b, w_s, b_s_t, dum, name):
    T = zpre.shape[0]
    E, G, C, GW = SG_WIDTH, SG_GROUPS, SG_CHUNK, SG_GROUP_W

    def body(zp_ref, bin_ref, lng_ref, lnb_ref, ws_ref, bst_ref, dum_ref,
             dz_ref, dbin_ref, dlng_ref, dlnb_ref, dws_ref, dbst_ref):
        @pl.when(pl.program_id(0) == 0)
        def _():
            for r in (dbin_ref, dlng_ref, dlnb_ref, dws_ref, dbst_ref):
                r[...] = jnp.zeros_like(r)

        zin, cdf, u, xhat, rstd, v = _sg_recompute(zp_ref, bin_ref, lng_ref, lnb_ref)
        bst = bst_ref[...]
        lane = lax.broadcasted_iota(jnp.int32, (C, LANES), 1)
        dum_v = dum_ref[...]
        dbst = jnp.zeros((C, LANES), F32)
        du_parts, dv_parts = [], []
        for g in range(G):
            cs = slice(g * GW, (g + 1) * GW)
            wsm = _sg_masked_ws(ws_ref, g)
            vg = _mx(v[:, cs])
            mixed = _dot(wsm, vg, NN) + bst[:, g:g + 1]
            dumg = dum_v[:, cs]
            du_parts.append(dumg * mixed)
            dmixed = dumg * u[:, cs]
            dmb = _mx(dmixed)
            dv_parts.append(_dot(wsm, dmb, TN))
            dws_ref[g] += _dot(dmb, vg, NT)
            dbst = jnp.where(lane == g, jnp.sum(dmixed, axis=1, keepdims=True), dbst)
        dbst_ref[...] += dbst
        du = jnp.concatenate(du_parts, axis=1)
        dv = jnp.concatenate(dv_parts, axis=1)
        dlng_ref[...] += jnp.sum(dv * xhat, axis=0, keepdims=True)
        dlnb_ref[...] += jnp.sum(dv, axis=0, keepdims=True)
        dxh = dv * lng_ref[...]
        dvp = rstd * (dxh - jnp.mean(dxh, axis=-1, keepdims=True) - xhat * jnp.mean(dxh * xhat, axis=-1, keepdims=True))
        dzz = jnp.concatenate([du, dvp], axis=1)
        dzin = dzz * (cdf + zin * (_INV_SQRT_2PI * jnp.exp(-0.5 * zin * zin)))
        dz_ref[...] = _mx(dzin)
        dbin_ref[...] += jnp.sum(dzin, axis=0, keepdims=True)

    return pl.pallas_call(
        body, name=name, grid=(T // C,),
        in_specs=[pl.BlockSpec((C, 2 * E), lambda n: (n, 0)), _full((1, 2 * E)), _full((1, E)), _full((1, E)),
                  _full((G, C, C)), _full((C, LANES)), pl.BlockSpec((C, E), lambda n: (n, 0))],
        out_specs=[pl.BlockSpec((C, 2 * E), lambda n: (n, 0)), _full((1, 2 * E)), _full((1, E)), _full((1, E)),
                   _full((G, C, C)), _full((C, LANES))],
        out_shape=[jax.ShapeDtypeStruct((T, 2 * E), MXU_DTYPE), jax.ShapeDtypeStruct((1, 2 * E), F32),
                   jax.ShapeDtypeStruct((1, E), F32), jax.ShapeDtypeStruct((1, E), F32),
                   jax.ShapeDtypeStruct((G, C, C), F32), jax.ShapeDtypeStruct((C, LANES), F32)],
        compiler_params=_cp("arbitrary"),
    )(zpre, b_in, ln_g, ln_b, w_s, b_s_t, dum)


def _row(v):
    return v.reshape(1, -1)


def _pad_lanes(v):
    v = v.reshape(1, -1)
    return jnp.pad(v, ((0, 0), (0, LANES - v.shape[1])))


def _local_step(x, target, p, weights_for, grads_ready=None, small_ready=None):
    ng = p["norm_g"]
    grads = {}
    dng = [[None] * 6 for _ in range(2)]
    order = [jnp.zeros((), F32)]

    def tell(group):
        zero = grads_ready(group, grads) if grads_ready is not None else None
        if zero is not None:
            order[0] = zero

    def gain(i, s):
        return _row(ng[i, s]) + order[0]

    def ffn_f(xin, i, j, tag):
        wt = weights_for("ffn" + tag, xin)
        xo, h, gu, y = _ffn_fwd(xin, _row(ng[i, 4 * j]), _row(ng[i, 4 * j + 1]), wt, "ffn_fwd_" + tag)
        return xo, (xin, h, gu, y, wt)

    x1, sv_f00 = ffn_f(x, 0, 0, "00")
    dnw = weights_for("dn", x1)
    hn0 = _norm_fwd(x1, _row(ng[0, 2]), "dn_prenorm")
    proj = _mm(hn0, dnw["dn_wqkvz"], "nn", "dn_proj")
    ba = _mm(hn0, dnw["dn_wba"], "nn", "dn_proj_ba")
    a_log = _pad_lanes(p["dn_a_log"])
    dt_bias = _pad_lanes(p["dn_dt_bias"])
    dn_ng = _row(p["dn_norm_g"])
    qkv = _dn_prep_fwd(proj, p["dn_conv_w"], "dn_prep_fwd")
    beta, gdec = _dn_gate_fwd(ba, a_log, dt_bias, "dn_gate_fwd")
    og, o_raw, tinv, s_all = _dn_scan_fwd(qkv, beta, gdec, proj, dn_ng, "dn_scan_fwd")
    m0, x2 = _out_proj_postnorm(og, dnw["dn_wout"], x1, _row(ng[0, 3]), "dn_out")
    x3, sv_f01 = ffn_f(x2, 0, 1, "01")
    x4, sv_f10 = ffn_f(x3, 1, 0, "10")
    sgw = weights_for("sg", x4)
    hn1 = _norm_fwd(x4, _row(ng[1, 2]), "sg_prenorm")
    zpre = _mm(hn1, sgw["sg_win"], "nn", "sg_proj")
    sg_bin = _row(p["sg_b_in"])
    sg_lng = _row(p["sg_ln_g"])
    sg_lnb = _row(p["sg_ln_b"])
    sg_bst = jnp.pad(p["sg_b_s"].T, ((0, 0), (0, LANES - SG_GROUPS)))
    um = _sg_fwd(zpre, sg_bin, sg_lng, sg_lnb, p["sg_w_s"], sg_bst, "sg_fwd")
    m1, x5 = _out_proj_postnorm(um, sgw["sg_wout"], x4, _row(ng[1, 3]), "sg_out")
    x6, sv_f11 = ffn_f(x5, 1, 1, "11")
    loss_part, dx = _loss_fwd_bwd(x6, target, "loss")

    def ffn_b(dxo, sv, i, j, tag, last=False):
        xin, h, gu, y, wt = sv
        dxi, dy, a, dgu, dg0, dg1 = _ffn_bwd(dxo, xin, y, gu, gain(i, 4 * j), gain(i, 4 * j + 1), wt, "ffn_bwd_" + tag)
        dng[i][4 * j] = dg0
        dng[i][4 * j + 1] = dg1
        after = None
        if last:
            grads["norm_g"] = jnp.stack([jnp.concatenate(dng[t], axis=0) for t in range(2)])
            after = small_ready(grads, loss_part) if small_ready is not None else None
        grads["wd" + tag] = _mm(a, dy, "tn", "ffn_wgrad_down_" + tag, after=after)
        grads["wguT" + tag] = _mm(dgu, h, "tn", "ffn_wgrad_up_" + tag, after=after)
        tell("ffn" + tag)
        return dxi

    dx = ffn_b(dx, sv_f11, 1, 1, "11")
    dm1, dng[1][3], dum = _postnorm_bwd_dgrad(dx, m1, gain(1, 3), sgw["sg_wout"], "sg_dgrad_out")
    grads["sg_w_out"] = _mm(um, dm1, "tn", "sg_wgrad_out")
    dz1, dbin, dlng, dlnb, dws, dbst = _sg_bwd(zpre, sg_bin, sg_lng, sg_lnb, p["sg_w_s"], sg_bst, dum, "sg_bwd")
    grads["sg_w_inT"] = _mm(dz1, hn1, "tn", "sg_wgrad_in")
    tell("sg")
    dx, dng[1][2] = _dgrad_prenorm_bwd(dz1, sgw["sg_win"], None, dx, x4, gain(1, 2), "sg_dgrad_in")
    grads["sg_b_in"] = dbin.reshape(1, -1)
    grads["sg_ln_g"] = dlng.reshape(1, -1)
    grads["sg_ln_b"] = dlnb.reshape(1, -1)
    grads["sg_w_s"] = jnp.where(jnp.tril(jnp.ones((SG_CHUNK, SG_CHUNK), bool)), dws, 0.0)[None]
    grads["sg_b_s"] = dbst[:, :SG_GROUPS].T[None]
    dx = ffn_b(dx, sv_f10, 1, 0, "10")
    dx = ffn_b(dx, sv_f01, 0, 1, "01")
    dm0, dng[0][3], dog = _postnorm_bwd_dgrad(dx, m0, gain(0, 3), dnw["dn_wout"], "dn_dgrad_out")
    grads["dn_w_out"] = _mm(og, dm0, "tn", "dn_wgrad_out")
    dqkv, dbeta, dgdec, dz0, dnng = _dn_scan_bwd(qkv, beta, gdec, proj, dn_ng, o_raw, tinv, s_all, dog, "dn_scan_bwd")
    dqkv_pre, dconv = _dn_prep_bwd(proj, p["dn_conv_w"], dqkv, "dn_prep_bwd")
    dba, dal, ddt = _dn_gate_bwd(ba, a_log, dt_bias, dbeta, dgdec, "dn_gate_bwd")
    W3 = 3 * DN_HEADS * DN_HEAD_DIM
    dw_qkv = _mm(hn0, dqkv_pre, "tn", "dn_wgrad_qkv")
    dw_z = _mm(hn0, dz0, "tn", "dn_wgrad_z")
    dw_ba = _mm(hn0, dba, "tn", "dn_wgrad_ba")
    grads["dn_w_in"] = jnp.concatenate(
        [dw_qkv, dw_z, dw_ba[:, :DN_HEADS], dw_ba[:, LANES:LANES + DN_HEADS]], axis=1)
    tell("dn")
    dh0 = _mm(dqkv_pre, dnw["dn_wqkvz"][:, :W3], "nt", "dn_dgrad_qkv")
    dh0 = _mm(dz0, dnw["dn_wqkvz"][:, W3:], "nt", "dn_dgrad_z", add=dh0)
    dx, dng[0][2] = _dgrad_prenorm_bwd(dba, dnw["dn_wba"], dh0, dx, x1, gain(0, 2), "dn_dgrad_ba")
    grads["dn_conv_w"] = dconv[None]
    grads["dn_a_log"] = dal[:, :DN_HEADS]
    grads["dn_dt_bias"] = ddt[:, :DN_HEADS]
    grads["dn_norm_g"] = dnng
    dx = ffn_b(dx, sv_f00, 0, 0, "00", last=True)
    return loss_part, dx, grads


def _mesh_pos():
    return lax.axis_index("x"), lax.axis_index("y"), lax.axis_index("c")


def _other_chips(x, y):
    return [(1 - x, y), (x, 1 - y), (1 - x, 1 - y)]


def _allgather_chips(arrs, name):
    n = len(arrs)

    def body(*refs):
        ins, outs = refs[:n], refs[n:2 * n]
        ici_send, ici_recv, d2d_send, d2d_recv = refs[2 * n:]
        x, y, c = _mesh_pos()
        me = 2 * x + y
        chips = _other_chips(x, y)
        sibling = (x, y, 1 - c)

        def ici(i, j, k):
            cx, cy = chips[j]
            return pltpu.make_async_remote_copy(src_ref=ins[i].at[c], dst_ref=outs[i].at[k, c], send_sem=ici_send.at[3 * i + j],
                                                recv_sem=ici_recv.at[3 * i + j], device_id=(cx, cy, c), device_id_type=MESH)

        def d2d(i, j, h):
            cx, cy = chips[j]
            slot = outs[i].at[2 * cx + cy, h]
            return pltpu.make_async_remote_copy(src_ref=slot, dst_ref=slot, send_sem=d2d_send.at[3 * i + j],
                                                recv_sem=d2d_recv.at[3 * i + j], device_id=sibling, device_id_type=MESH)

        sends = [ici(i, j, me) for i in range(n) for j in range(3)]
        for cp in sends:
            cp.start()
        for i in range(n):
            for j, (cx, cy) in enumerate(chips):
                ici(i, j, 2 * cx + cy).wait_recv()
                fwd = d2d(i, j, c)
                fwd.start()
                sends.append(fwd)
        for i in range(n):
            for j in range(3):
                d2d(i, j, 1 - c).wait_recv()
        for cp in sends:
            cp.wait_send()

    return pl.pallas_call(
        body, name=name, in_specs=[ANY] * n, out_specs=[ANY] * n,
        out_shape=[jax.ShapeDtypeStruct((N_CHIPS,) + a.shape, a.dtype) for a in arrs],
        scratch_shapes=[pltpu.SemaphoreType.DMA((3 * n,))] * 4,
    )(*arrs)


HBM = pl.BlockSpec(memory_space=pltpu.HBM)
SEM = pl.BlockSpec(memory_space=pltpu.SEMAPHORE)
TOKEN = jax.ShapeDtypeStruct((SUBLANES, LANES), F32)


_PEERS = {"gather": 3, "scatter": 3, "swap": 1, "all": N_DEV - 1}


def _land_shape(kind, shape):
    if kind == "gather":
        return (N_CHIPS,) + shape
    if kind == "all":
        return (N_DEV,) + shape
    return (N_CHIPS,) + shape[2:] if kind == "swap" else shape


def _peer_copies(kind, flags, src_refs, land_refs, send_sems, recv_sems, receiving):
    x, y, c = _mesh_pos()
    me4, me8 = 2 * x + y, 4 * x + 2 * y + c
    np_ = _PEERS[kind]
    cps = []
    for i, (src, land) in enumerate(zip(src_refs, land_refs)):
        if kind == "swap":
            half = src.at[1 - c] if flags[i] else src.at[:, 1 - c]
            plan = [((x, y, 1 - c), half, land)]
        elif kind == "all":
            masks = [(mx, my, mc) for mx in (0, 1) for my in (0, 1) for mc in (0, 1)][1:]
            peers = [(jnp.where(mx, 1 - x, x), jnp.where(my, 1 - y, y), jnp.where(mc, 1 - c, c)) for mx, my, mc in masks]
            plan = [(p, src, land.at[4 * p[0] + 2 * p[1] + p[2] if receiving else me8]) for p in peers]
        else:
            plan = []
            for cx, cy in _other_chips(x, y):
                k = 2 * cx + cy
                s = src.at[me4 if receiving else k] if kind == "scatter" else src
                plan.append(((cx, cy, c), s, land.at[k if receiving else me4]))
        for j, (peer, s, d) in enumerate(plan):
            cps.append(pltpu.make_async_remote_copy(src_ref=s, dst_ref=d, send_sem=send_sems.at[np_ * i + j],
                                                    recv_sem=recv_sems.at[np_ * i + j], device_id=peer, device_id_type=MESH))
    return cps


def _copies_start(kind, srcs, after, name, flags=None):
    n = len(srcs)
    ns = _PEERS[kind] * n
    lands = [lax.empty(_land_shape(kind, s.shape), s.dtype) for s in srcs]
    after = [] if after is None else [after]

    def body(*refs):
        src_refs, land_refs = refs[:n], refs[n:2 * n]
        send_sems, recv_sems = refs[2 * n + len(after)], refs[2 * n + len(after) + 1]
        token = refs[-1]
        for cp in _peer_copies(kind, flags, src_refs, land_refs, send_sems, recv_sems, False):
            cp.start()
        token[...] = jnp.zeros_like(token)

    outs = pl.pallas_call(
        body, name=name,
        in_specs=[HBM] * (2 * n) + [ANY] * len(after),
        out_specs=(SEM, SEM) + (HBM,) * (2 * n) + (pl.BlockSpec(memory_space=pltpu.VMEM),),
        out_shape=(pltpu.SemaphoreType.DMA((ns,)), pltpu.SemaphoreType.DMA((ns,)))
        + tuple(pltpu.HBM(a.shape, a.dtype) for a in list(srcs) + lands) + (TOKEN,),
        input_output_aliases={i: 2 + i for i in range(2 * n)},
        compiler_params=pltpu.CompilerParams(has_side_effects=pltpu.SideEffectType.DATAFLOW_SIDE_EFFECTING),
    )(*[pltpu.with_memory_space_constraint(a, pltpu.HBM) for a in list(srcs) + lands], *after)
    return dict(sems=outs[:2], srcs=outs[2:2 + n], lands=outs[2 + n:2 + 2 * n], token=outs[-1], kind=kind, flags=flags)


def _copies_wait(started, after, name):
    n = len(started["srcs"])
    kind, flags = started["kind"], started["flags"]
    after = list(after) if isinstance(after, (list, tuple)) else [after]

    def body(*refs):
        src_refs, land_refs = refs[:n], refs[n:2 * n]
        send_sems, recv_sems = refs[2 * n], refs[2 * n + 1]
        for cp in _peer_copies(kind, flags, src_refs, land_refs, send_sems, recv_sems, True):
            cp.wait_send()
            cp.wait_recv()

    outs = pl.pallas_call(
        body, name=name,
        in_specs=[HBM] * (2 * n) + [SEM, SEM] + [ANY] * len(after),
        out_specs=(HBM,) * (2 * n),
        out_shape=tuple(pltpu.HBM(a.shape, a.dtype) for a in list(started["srcs"]) + list(started["lands"])),
        input_output_aliases={i: i for i in range(2 * n)},
        compiler_params=pltpu.CompilerParams(has_side_effects=pltpu.SideEffectType.DATAFLOW_SIDE_EFFECTING),
    )(*started["srcs"], *started["lands"], *started["sems"], *after)
    return outs[:n], outs[n:]


def _swap_whole(arrs, name):
    n = len(arrs)

    def body(*refs):
        ins, outs = refs[:n], refs[n:2 * n]
        send_sems, recv_sems = refs[2 * n:]
        x, y, c = _mesh_pos()
        cps = [pltpu.make_async_remote_copy(src_ref=ins[i], dst_ref=outs[i], send_sem=send_sems.at[i],
                                            recv_sem=recv_sems.at[i], device_id=(x, y, 1 - c), device_id_type=MESH)
               for i in range(n)]
        for cp in cps:
            cp.start()
        for cp in cps:
            cp.wait()

    return pl.pallas_call(
        body, name=name, in_specs=[ANY] * n, out_specs=[ANY] * n,
        out_shape=[jax.ShapeDtypeStruct(a.shape, a.dtype) for a in arrs],
        scratch_shapes=[pltpu.SemaphoreType.DMA((n,)), pltpu.SemaphoreType.DMA((n,))],
    )(*arrs)


def _as_rows(a, lead):
    shp = a.shape
    rows = 1
    for s in shp[lead:-1]:
        rows *= s
    return a.reshape(shp[:lead] + (rows, shp[-1]))


def _row_tile(rows, cols, n_bufs):
    budget = (24 * 1024 * 1024) // (n_bufs * 2 * 4 * cols)
    return _pick(rows, max(2 * SUBLANES, budget), 2 * SUBLANES)


def _sum_devices(own, got, dev, name):
    n, rows, cols = got.shape
    tr = _row_tile(rows, cols, n + 2)

    def body(dev_ref, own_ref, got_ref, o_ref):
        mine = own_ref[...]
        acc = jnp.where(dev_ref[0] == 0, mine, got_ref[0])
        for k in range(1, n):
            acc = acc + jnp.where(dev_ref[0] == k, mine, got_ref[k])
        o_ref[...] = acc

    return pl.pallas_call(
        body, name=name,
        grid_spec=pltpu.PrefetchScalarGridSpec(
            num_scalar_prefetch=1, grid=(rows // tr,),
            in_specs=[pl.BlockSpec((tr, cols), lambda i, d: (i, 0)), pl.BlockSpec((n, tr, cols), lambda i, d: (0, i, 0))],
            out_specs=pl.BlockSpec((tr, cols), lambda i, d: (i, 0))),
        out_shape=jax.ShapeDtypeStruct((rows, cols), F32), compiler_params=_cp("parallel"),
    )(_scalar(dev), own, got)


def _scalar(i):
    return jnp.reshape(i, (1,)).astype(jnp.int32)


def _add_own_half(g, other, c, half_first, name):
    _, rows, cols = other.shape
    tr = _row_tile(rows, cols, 3)

    def body(c_ref, g_ref, o_ref, out_ref):
        out_ref[0] = (g_ref[0, 0] + o_ref[0]).astype(out_ref.dtype)

    if half_first:
        g_map = lambda k, i, c_ref: (c_ref[0], k, i, 0)
    else:
        g_map = lambda k, i, c_ref: (k, c_ref[0], i, 0)
    flat = pl.BlockSpec((1, tr, cols), lambda k, i, c_ref: (k, i, 0))
    return pl.pallas_call(
        body, name=name,
        grid_spec=pltpu.PrefetchScalarGridSpec(
            num_scalar_prefetch=1, grid=(N_CHIPS, rows // tr),
            in_specs=[pl.BlockSpec((1, 1, tr, cols), g_map), flat], out_specs=flat),
        out_shape=jax.ShapeDtypeStruct(other.shape, COMM_DTYPE), compiler_params=_cp("parallel", "parallel"),
    )(_scalar(c), g, other)


def _sum_chips(own, got, chip, name, transpose=False):
    _, rows, cols = own.shape
    tr = rows if transpose else _row_tile(rows, cols, N_CHIPS + 2)

    def body(chip_ref, p_ref, b_ref, o_ref):
        mine = p_ref[0].astype(F32)
        acc = jnp.where(chip_ref[0] == 0, mine, b_ref[0].astype(F32))
        for k in range(1, N_CHIPS):
            acc = acc + jnp.where(chip_ref[0] == k, mine, b_ref[k].astype(F32))
        o_ref[...] = acc.T if transpose else acc

    if transpose:
        out_spec, out_shape = pl.BlockSpec((cols, rows), lambda i, k_ref: (0, 0)), (cols, rows)
    else:
        out_spec, out_shape = pl.BlockSpec((tr, cols), lambda i, k_ref: (i, 0)), (rows, cols)
    return pl.pallas_call(
        body, name=name,
        grid_spec=pltpu.PrefetchScalarGridSpec(
            num_scalar_prefetch=1, grid=(rows // tr,),
            in_specs=[pl.BlockSpec((1, tr, cols), lambda i, k_ref: (k_ref[0], i, 0)),
                      pl.BlockSpec((N_CHIPS, tr, cols), lambda i, k_ref: (0, i, 0))],
            out_specs=out_spec),
        out_shape=jax.ShapeDtypeStruct(out_shape, F32), compiler_params=_cp("parallel"),
    )(_scalar(chip), own, got)


def _adam_math(w, g, m, v):
    nm = ADAM_B1 * m + (1.0 - ADAM_B1) * g
    nv = ADAM_B2 * v + (1.0 - ADAM_B2) * (g * g)
    m_hat = nm / (1.0 - ADAM_B1 ** ADAM_STEP)
    v_hat = nv / (1.0 - ADAM_B2 ** ADAM_STEP)
    return -ADAM_LR * (m_hat / (jnp.sqrt(v_hat) + ADAM_EPS) + ADAM_WD * w), nm, nv


def _adamw_pieces(w, m, v, mine, theirs, c, kind, name):
    shape = w.shape
    P = len(mine)
    ws, ms, vs = (t.reshape((P, -1, t.shape[-1])) for t in (w, m, v))
    _, R, C = ws.shape
    if kind == "rows":
        tr = _pick(R // 2, 512, SUBLANES)
    else:
        tr = _pick(R, 256 if kind in ("lo", "hi") else 512, SUBLANES)
    nt = R // tr
    nh = nt // 2

    def body(c_ref, w_ref, m_ref, v_ref, *refs):
        mine_refs, theirs_refs = refs[:P], refs[P:2 * P]
        g_ref, d_ref, nm_ref, nv_ref = refs[2 * P:]
        p, i, core = pl.program_id(0), pl.program_id(1), c_ref[0]

        def pick(refs_):
            out = refs_[0][...]
            for q in range(1, P):
                out = jnp.where(p == q, refs_[q][...], out)
            return out

        a, b = pick(mine_refs), pick(theirs_refs)
        if kind == "cols":
            gv = jnp.where(core == 0, jnp.concatenate([a, b], axis=1), jnp.concatenate([b, a], axis=1))
        else:
            own = {"lo": core == 0, "hi": core == 1, "rows": (i >= nh) == (core == 1)}[kind]
            gv = jnp.where(own, a, b)
        g_ref[0] = gv
        d_ref[0], nm_ref[0], nv_ref[0] = _adam_math(w_ref[0], gv, m_ref[0], v_ref[0])

    def piece_spec(q):
        tile = (lambda i: i - jnp.where(i >= nh, nh, 0)) if kind == "rows" else (lambda i: i)
        return pl.BlockSpec((tr, mine[q].shape[1]), lambda p, i, c_ref: (jnp.where(p == q, tile(i), 0), 0))

    full = pl.BlockSpec((1, tr, C), lambda p, i, c_ref: (p, i, 0))
    outs = pl.pallas_call(
        body, name=name,
        grid_spec=pltpu.PrefetchScalarGridSpec(num_scalar_prefetch=1, grid=(P, nt),
                                               in_specs=[full] * 3 + [piece_spec(q) for q in range(P)] * 2,
                                               out_specs=[full] * 4),
        out_shape=[jax.ShapeDtypeStruct((P, R, C), F32)] * 4, compiler_params=_cp("parallel", "arbitrary"),
    )(_scalar(c), ws, ms, vs, *mine, *theirs)
    return tuple(o.reshape(shape) for o in outs)


def _adamw(w, g, m, v, name):
    shape = w.shape
    ws, gs, ms, vs = (_as_rows(t, 0) for t in (w, g, m, v))
    rows, cols = ws.shape
    tr = _row_tile(rows, cols, 7)

    def body(w_ref, g_ref, m_ref, v_ref, d_ref, nm_ref, nv_ref):
        d_ref[...], nm_ref[...], nv_ref[...] = _adam_math(w_ref[...], g_ref[...], m_ref[...], v_ref[...])

    spec = pl.BlockSpec((tr, cols), lambda i: (i, 0))
    outs = pl.pallas_call(body, name=name, grid=(rows // tr,), in_specs=[spec] * 4, out_specs=[spec] * 3,
                          out_shape=[jax.ShapeDtypeStruct((rows, cols), F32)] * 3, compiler_params=_cp("parallel"))(ws, gs, ms, vs)
    return tuple(o.reshape(shape) for o in outs)


_BIG = ["ffn_w_gate", "ffn_w_up", "ffn_w_down", "dn_w_in", "dn_w_out", "sg_w_in", "sg_w_out"]
_SMALL_SHARDED = ["norm_g", "dn_conv_w", "sg_b_in", "sg_ln_g", "sg_ln_b"]
_SMALL_REPL = ["dn_a_log", "dn_dt_bias", "dn_norm_g", "sg_w_s", "sg_b_s"]
_WEIGHTS = ["norm_g", "ffn_w_gate", "ffn_w_up", "ffn_w_down", "dn_w_in", "dn_conv_w", "dn_a_log", "dn_dt_bias",
            "dn_norm_g", "dn_w_out", "sg_w_in", "sg_b_in", "sg_ln_g", "sg_ln_b", "sg_w_s", "sg_b_s", "sg_w_out"]
PACK_COLS = 1024


def _pack(arrs):
    flat = jnp.concatenate([a.reshape(-1) for a in arrs])
    pad = (-flat.shape[0]) % (SUBLANES * PACK_COLS)
    return jnp.pad(flat, (0, pad)).reshape(-1, PACK_COLS)


def _unpack(buf, shapes):
    flat = buf.reshape(-1)
    out, off = [], 0
    for s in shapes:
        n = math.prod(s)
        out.append(flat[off:off + n].reshape(s))
        off += n
    return out


def _as_halves(a):
    if a.shape[0] == 2:
        return a
    if a.shape[0] == 1:
        return a.reshape((2, a.shape[1] // 2) + a.shape[2:])
    return a.reshape((2, a.shape[0] // 2) + a.shape[1:])


def _with_own(gathered, own, chip):
    g = gathered.reshape((N_CHIPS,) + own.shape)
    return [jnp.where(chip == k, own, g[k]) for k in range(N_CHIPS)]


def _cat_shards(g, axis):
    return jnp.concatenate(list(g), axis=axis)


_GROUP_ORDER = ["ffn00", "dn", "ffn01", "ffn10", "sg", "ffn11"]


def _weight_groups(w):
    cast = {k: _mx(w[k]) for k in _BIG}
    groups = {"ffn%d%d" % (i, j): [cast["ffn_w_gate"][i, j].T, cast["ffn_w_up"][i, j].T, cast["ffn_w_down"][i, j]]
              for i, j in [(0, 0), (0, 1), (1, 0), (1, 1)]}
    groups["dn"] = [cast["dn_w_in"][0], cast["dn_w_out"][0]]
    groups["sg"] = [cast["sg_w_in"][0], cast["sg_w_out"][0]]
    return groups


def _ffn_weights(chip, own, gathered):
    pairs = [(a, g.reshape((N_CHIPS,) + a.shape)) for a, g in zip(own, gathered)]
    return {"chip": chip, "gate": pairs[0], "up": pairs[1], "down": pairs[2]}


def _group_matrices(group, shards):
    if group == "sg":
        return {"sg_win": _cat_shards(shards[0], 1), "sg_wout": _cat_shards(shards[1], 0)}
    dn_full = _cat_shards(shards[0], 1)
    W4 = 4 * DN_HEADS * DN_HEAD_DIM
    wba = jnp.zeros((D_MODEL, 2 * LANES), dn_full.dtype)
    wba = wba.at[:, :DN_HEADS].set(dn_full[:, W4:W4 + DN_HEADS])
    wba = wba.at[:, LANES:LANES + DN_HEADS].set(dn_full[:, W4 + DN_HEADS:])
    return {"dn_wqkvz": dn_full[:, :W4], "dn_wba": wba, "dn_wout": _cat_shards(shards[1], 0)}


def _split_cols(a, n):
    w = a.shape[-1] // n
    return [a[..., k * w:(k + 1) * w] for k in range(n)]


def _split_rows(a, n):
    h = a.shape[-2] // n
    return [a[..., k * h:(k + 1) * h, :] for k in range(n)]


_IJ = [(0, 0), (0, 1), (1, 0), (1, 1)]


def _group_grads(group, grads):
    def rows_by_chip(a):
        return a.reshape(N_CHIPS, 2, a.shape[0] // (2 * N_CHIPS), a.shape[1])

    if group.startswith("ffn"):
        tag = group[3:]
        t = grads["wguT" + tag]
        return (["wguT" + tag, "wd" + tag],
                [t.reshape(2, N_CHIPS, t.shape[0] // (2 * N_CHIPS), t.shape[1]), rows_by_chip(grads["wd" + tag])], [True, False])
    if group == "sg":
        return ["sg_w_inT", "sg_w_out"], [rows_by_chip(grads["sg_w_inT"]), rows_by_chip(grads["sg_w_out"])], [False, False]
    dn_in = jnp.stack([jnp.stack(_split_cols(hf, N_CHIPS)) for hf in _split_rows(grads["dn_w_in"], 2)])
    return ["dn_w_in", "dn_w_out"], [dn_in, rows_by_chip(grads["dn_w_out"])], [True, False]


_SHARD_PIECES = {
    "ffn_w_gate": (["wguT%d%d" % ij for ij in _IJ], "lo"),
    "ffn_w_up": (["wguT%d%d" % ij for ij in _IJ], "hi"),
    "ffn_w_down": (["wd%d%d" % ij for ij in _IJ], "rows"),
    "dn_w_in": (["dn_w_in"], "rows"),
    "dn_w_out": (["dn_w_out"], "rows"),
    "sg_w_in": (["sg_w_inT"], "cols"),
    "sg_w_out": (["sg_w_out"], "rows"),
}


def kernel(x, norm_g, ffn_w_gate, ffn_w_up, ffn_w_down, dn_w_in, dn_conv_w, dn_a_log, dn_dt_bias, dn_norm_g, dn_w_out, sg_w_in, sg_b_in, sg_ln_g, sg_ln_b, sg_w_s, sg_b_s, sg_w_out, loss_target, m_norm_g, m_ffn_w_gate, m_ffn_w_up, m_ffn_w_down, m_dn_w_in, m_dn_conv_w, m_dn_a_log, m_dn_dt_bias, m_dn_norm_g, m_dn_w_out, m_sg_w_in, m_sg_b_in, m_sg_ln_g, m_sg_ln_b, m_sg_w_s, m_sg_b_s, m_sg_w_out, v_norm_g, v_ffn_w_gate, v_ffn_w_up, v_ffn_w_down, v_dn_w_in, v_dn_conv_w, v_dn_a_log, v_dn_dt_bias, v_dn_norm_g, v_dn_w_out, v_sg_w_in, v_sg_b_in, v_sg_ln_g, v_sg_ln_b, v_sg_w_s, v_sg_b_s, v_sg_w_out):
    args = dict(locals())
    w = {k: args[k] for k in _WEIGHTS}
    mom = {k: args["m_" + k] for k in _WEIGHTS}
    var = {k: args["v_" + k] for k in _WEIGHTS}
    cx, cy, cc = _mesh_pos()
    chip = 2 * cx + cy

    small_shapes = [w[k].shape for k in _SMALL_SHARDED]
    groups = _weight_groups(w)
    own = groups[_GROUP_ORDER[0]] + [_pack([w[k] for k in _SMALL_SHARDED])]
    first = _allgather_chips([_as_halves(a) for a in own], "gather_first")
    started, after = {}, first[0]
    for g in _GROUP_ORDER[1:]:
        started[g] = _copies_start("gather", groups[g], after, "gather_start_" + g)
        after = started[g]["token"]
    small_k = [_unpack(pack, small_shapes) for pack in _with_own(first[-1], own[-1], chip)]
    p = {name: jnp.concatenate([small_k[k][i] for k in range(N_CHIPS)], axis=-1) for i, name in enumerate(_SMALL_SHARDED)}
    p = {k: (v if k == "norm_g" else v[0]) for k, v in p.items()}
    p["norm_g"] = p["norm_g"] + after[0, 0]
    for k in _SMALL_REPL:
        p[k] = w[k][0]

    def weights_for(group, after):
        if group == _GROUP_ORDER[0]:
            return _ffn_weights(chip, own[:-1], first[:-1])
        srcs, lands = _copies_wait(started[group], after, "gather_wait_" + group)
        if group.startswith("ffn"):
            return _ffn_weights(chip, srcs, lands)
        return _group_matrices(group, [_with_own(l, a, chip) for l, a in zip(lands, srcs)])

    mine, theirs, to_core, to_chips = {}, {}, [], []

    def send_to_chips(after):
        group, names, flags, swap = to_core.pop(0)
        halves, got = _copies_wait(swap, after, "swap_wait_" + group)
        pair_sum = [_add_own_half(h, o, cc, hf, "pair_sum_" + n) for n, h, o, hf in zip(names, halves, got, flags)]
        scatter = _copies_start("scatter", pair_sum, got[0], "reduce_start_" + group)
        to_chips.append((group, names, scatter))
        return scatter["token"]

    def finish(after):
        group, names, scatter = to_chips.pop(0)
        pair_sum, got = _copies_wait(scatter, after, "reduce_wait_" + group)
        half_sum = [_sum_chips(a, b, chip, "chip_sum_" + n, transpose=n == "sg_w_inT")
                    for n, a, b in zip(names, pair_sum, got)]
        other = _swap_whole(half_sum, "gather_core_pair_" + group)
        mine.update(zip(names, half_sum))
        theirs.update(zip(names, other))

    def grads_ready(group, grads):
        names, halves, flags = _group_grads(group, grads)
        swap = _copies_start("swap", halves, None, "swap_start_" + group, flags)
        token = swap["token"]
        if to_core:
            token = send_to_chips(token)
            if len(to_chips) > 1:
                finish(token)
        to_core.append((group, names, flags, swap))
        return token[0, 0]

    small_names = _SMALL_SHARDED + _SMALL_REPL
    small = {}

    def small_ready(grads, loss_part):
        parts = [grads[k] for k in small_names]
        small["shapes"] = [g.shape for g in parts] + [(1,)]
        pack = _pack(parts + [loss_part[0, :1]])
        small["exchange"] = _copies_start("all", [pack], None, "small_start")
        return small["exchange"]["token"]

    loss_part, grad_x, grads = _local_step(x[0], loss_target[0], p, weights_for, grads_ready, small_ready)
    token = send_to_chips(to_core[0][3]["token"])
    finish(token)
    (pack,), (packs,) = _copies_wait(small["exchange"], list(theirs.values()), "small_wait")
    summed = _sum_devices(pack, packs, 4 * cx + 2 * cy + cc, "small_sum")
    parts = _unpack(summed, small["shapes"])
    loss = parts[-1][0]
    grad = {}
    for i, k in enumerate(small_names):
        g = parts[i]
        if k in _SMALL_SHARDED:
            n = w[k].shape[-1]
            g = lax.dynamic_slice_in_dim(g, chip * n, n, axis=g.ndim - 1)
        grad[k] = g

    delta, new_m, new_v = {}, {}, {}

    def update(keys):
        for k in keys:
            names, kind = _SHARD_PIECES[k]
            turn = (lambda a: jnp.swapaxes(a, -1, -2)) if names[0].startswith("wguT") else (lambda a: a)
            outs = _adamw_pieces(turn(w[k]), turn(mom[k]), turn(var[k]), [mine[n] for n in names], [theirs[n] for n in names],
                                 cc, kind, "adamw_" + k)
            grad[k], delta[k], new_m[k], new_v[k] = (turn(o) for o in outs)

    shapes = [w[k].shape for k in small_names]
    d, nm, nv = _adamw(_pack([w[k] for k in small_names]), _pack([grad[k] for k in small_names]),
                       _pack([mom[k] for k in small_names]), _pack([var[k] for k in small_names]), "adamw_small")
    for k, a, b, c_ in zip(small_names, _unpack(d, shapes), _unpack(nm, shapes), _unpack(nv, shapes)):
        delta[k], new_m[k], new_v[k] = a, b, c_
    mixers = [k for k in _BIG if not k.startswith("ffn")]
    update(mixers)
    finish([d] + [delta[k] for k in mixers] + list(theirs.values()))
    update([k for k in _BIG if k.startswith("ffn")])

    return (loss, grad_x[None], *[grad[k] for k in _WEIGHTS], *[delta[k] for k in _WEIGHTS],
            *[new_m[k] for k in _WEIGHTS], *[new_v[k] for k in _WEIGHTS])
```

```python
import functools
import math

import jax
import jax.numpy as jnp
from jax import lax
from jax.experimental import pallas as pl
from jax.experimental.pallas import tpu as pltpu

F32 = jnp.float32
MXU_DTYPE = jnp.bfloat16
COMM_DTYPE = jnp.bfloat16
HI = lax.Precision.HIGHEST
TRI_PREC = lax.Precision.HIGH

D_MODEL = 1024
D_FF = 2816
RMS_EPS = 1e-6
LN_EPS = 1e-5
L2_EPS = 1e-6
DN_HEADS = 8
DN_HEAD_DIM = 128
DN_CONV = 4
DN_CHUNK = 64
SG_WIDTH = 2048
SG_GROUPS = 8
SG_CHUNK = 128
SG_GROUP_W = SG_WIDTH // SG_GROUPS
N_CHIPS = 4
N_DEV = 8
LANES = 128
SUBLANES = 8
VMEM_LIMIT = 56 * 1024 * 1024

ADAM_LR = 0.001
ADAM_B1 = 0.9
ADAM_B2 = 0.999
ADAM_EPS = 1e-08
ADAM_WD = 0.01
ADAM_STEP = 10

MESH = pl.DeviceIdType.MESH
ANY = pl.BlockSpec(memory_space=pl.ANY)


def _cp(*sem):
    return pltpu.CompilerParams(dimension_semantics=sem, vmem_limit_bytes=VMEM_LIMIT)


def _pick(n, pref, mult=LANES):
    best = None
    d = mult
    while d <= min(n, pref):
        if n % d == 0:
            best = d
        d += mult
    return best if best is not None else n


def _full(shape):
    nd = len(shape)
    return pl.BlockSpec(shape, lambda *_: (0,) * nd)


def _sigmoid(x):
    return 1.0 / (1.0 + jnp.exp(-x))


def _dot(a, b, dims, prec=None):
    return lax.dot_general(a, b, (dims, ((), ())), preferred_element_type=F32, precision=prec)


NN = ((1,), (0,))
NT = ((1,), (1,))
TN = ((0,), (0,))


def _mx(a):
    return a.astype(MXU_DTYPE)


def _rms_stat(x):
    return lax.rsqrt(jnp.mean(x * x, axis=-1, keepdims=True) + RMS_EPS)


def _rms_bwd(x, r, g, dy):
    xh = x * r
    dxh = dy * g
    dx = r * (dxh - xh * jnp.mean(dxh * xh, axis=-1, keepdims=True))
    return dx, jnp.sum(dy * xh, axis=0, keepdims=True)


def _mm(a, b, mode, name, out_dtype=F32, add=None, after=None):
    if mode == "tn":
        K, M = a.shape
        N = b.shape[1]
    elif mode == "nt":
        M, K = a.shape
        N = b.shape[0]
    else:
        M, K = a.shape
        N = b.shape[1]
    tn = _pick(N, 1024)
    if mode == "tn":
        tm = _pick(M, 1024 if tn <= 512 else 1408)
        tk = _pick(K, 1024, SUBLANES)
    else:
        tm = _pick(M, max(512, min(2048, (1024 * 1024) // tn)), SUBLANES)
        tk = _pick(K, 2048)
    nk = K // tk
    grid = (N // tn, M // tm, nk)
    if mode == "nn":
        a_spec = pl.BlockSpec((tm, tk), lambda j, i, k: (i, k))
        b_spec = pl.BlockSpec((tk, tn), lambda j, i, k: (k, j))
        dims = NN
    elif mode == "nt":
        a_spec = pl.BlockSpec((tm, tk), lambda j, i, k: (i, k))
        b_spec = pl.BlockSpec((tn, tk), lambda j, i, k: (j, k))
        dims = NT
    else:
        a_spec = pl.BlockSpec((tk, tm), lambda j, i, k: (k, i))
        b_spec = pl.BlockSpec((tk, tn), lambda j, i, k: (k, j))
        dims = TN
    o_spec = pl.BlockSpec((tm, tn), lambda j, i, k: (i, j))
    has_add = add is not None

    def body(*refs):
        a_ref, b_ref = refs[:2]
        add_ref = refs[2] if has_add else None
        o_ref, acc = refs[-2:]
        k = pl.program_id(2)

        @pl.when(k == 0)
        def _():
            acc[...] = add_ref[...] if has_add else jnp.zeros_like(acc)

        acc[...] += _dot(a_ref[...], b_ref[...], dims)

        @pl.when(k == nk - 1)
        def _():
            o_ref[...] = acc[...].astype(o_ref.dtype)

    ins = [a, b] + ([add] if has_add else []) + ([after] if after is not None else [])
    specs = [a_spec, b_spec] + ([o_spec] if has_add else []) + ([ANY] if after is not None else [])
    return pl.pallas_call(
        body, name=name, grid=grid, in_specs=specs, out_specs=o_spec,
        out_shape=jax.ShapeDtypeStruct((M, N), out_dtype),
        scratch_shapes=[pltpu.VMEM((tm, tn), F32)],
        compiler_params=_cp("parallel", "parallel", "arbitrary"),
    )(*ins)


def _ffn_weight_operands(wt):
    return [_scalar(wt["chip"])] , [wt["gate"][0], wt["gate"][1], wt["up"][0], wt["up"][1], wt["down"][0], wt["down"][1]]


def _load_ffn_weights(chip_ref, shard_refs, wgu_v, wd_v, sem):
    fs = wd_v.shape[0] // N_CHIPS

    @pl.when(pl.program_id(0) == 0)
    def _():
        me = chip_ref[0]
        waits = []
        for t, (dst, base) in enumerate([(wgu_v, 0), (wgu_v, wd_v.shape[0]), (wd_v, 0)]):
            own, gathered = shard_refs[2 * t], shard_refs[2 * t + 1]
            for k in range(N_CHIPS):
                slot = dst.at[pl.ds(base + k * fs, fs), :]
                s = sem.at[t * N_CHIPS + k]

                @pl.when(me == k)
                def _(own=own, slot=slot, s=s):
                    pltpu.make_async_copy(own, slot, s).start()

                @pl.when(me != k)
                def _(gathered=gathered, k=k, slot=slot, s=s):
                    pltpu.make_async_copy(gathered.at[k], slot, s).start()

                waits.append(pltpu.make_async_copy(own, slot, s))
        for cp in waits:
            cp.wait()


def _ffn_fwd(x, g0, g1, wt, name):
    T, D = x.shape
    F = N_CHIPS * wt["down"][0].shape[0]
    F2 = 2 * F
    tm = _pick(T, 256, SUBLANES)
    prefetch, shards = _ffn_weight_operands(wt)

    def body(chip_ref, x_ref, g0_ref, g1_ref, *refs):
        shard_refs = refs[:6]
        xo_ref, h_ref, gu_ref, y_ref, wgu_v, wd_v, sem = refs[6:]
        _load_ffn_weights(chip_ref, shard_refs, wgu_v, wd_v, sem)
        xv = x_ref[...]
        hb = _mx(xv * _rms_stat(xv) * g0_ref[...])
        h_ref[...] = hb
        gu = _dot(hb, wgu_v[...], NT)
        gu_ref[...] = gu.astype(gu_ref.dtype)
        g = gu[:, :F]
        u = gu[:, F:]
        a = _mx(g * _sigmoid(g) * u)
        y = _dot(a, wd_v[...], NN)
        y_ref[...] = y
        xo_ref[...] = xv + 0.5 * (y * _rms_stat(y) * g1_ref[...])

    row = lambda w: pl.BlockSpec((tm, w), lambda i, c: (i, 0))
    one = pl.BlockSpec((1, D), lambda i, c: (0, 0))
    return pl.pallas_call(
        body, name=name,
        grid_spec=pltpu.PrefetchScalarGridSpec(
            num_scalar_prefetch=1, grid=(T // tm,),
            in_specs=[row(D), one, one] + [ANY] * 6,
            out_specs=[row(D), row(D), row(F2), row(D)],
            scratch_shapes=[pltpu.VMEM((F2, D), MXU_DTYPE), pltpu.VMEM((F, D), MXU_DTYPE),
                            pltpu.SemaphoreType.DMA((3 * N_CHIPS,))]),
        out_shape=[jax.ShapeDtypeStruct((T, D), F32), jax.ShapeDtypeStruct((T, D), MXU_DTYPE),
                   jax.ShapeDtypeStruct((T, F2), MXU_DTYPE), jax.ShapeDtypeStruct((T, D), F32)],
        compiler_params=_cp("arbitrary"),
    )(*prefetch, x, g0, g1, *shards)


FFN_BWD_CHUNK = 2816


def _ffn_bwd(dxo, x, y, gu, g0, g1, wt, name):
    T, D = x.shape
    F2 = gu.shape[1]
    F = F2 // 2
    tm = _pick(T, 256, SUBLANES)
    fc = _pick(F, FFN_BWD_CHUNK)
    prefetch, shards = _ffn_weight_operands(wt)

    def body(chip_ref, dxo_ref, x_ref, y_ref, gu_ref, g0_ref, g1_ref, *refs):
        shard_refs = refs[:6]
        dx_ref, dy_ref, a_ref, dgu_ref, dg0_ref, dg1_ref, wgu_v, wd_v, sem = refs[6:]
        _load_ffn_weights(chip_ref, shard_refs, wgu_v, wd_v, sem)

        @pl.when(pl.program_id(0) == 0)
        def _():
            dg0_ref[...] = jnp.zeros_like(dg0_ref)
            dg1_ref[...] = jnp.zeros_like(dg1_ref)

        dxo_v = dxo_ref[...]
        yv = y_ref[...]
        dy, dg1 = _rms_bwd(yv, _rms_stat(yv), g1_ref[...], 0.5 * dxo_v)
        dg1_ref[...] += dg1
        dyb = _mx(dy)
        dy_ref[...] = dyb
        dh = jnp.zeros((tm, D), F32)
        for c in range(F // fc):
            lo, hi = c * fc, (c + 1) * fc
            da = _dot(dyb, wd_v[lo:hi, :], NT)
            g = gu_ref[:, lo:hi].astype(F32)
            u = gu_ref[:, F + lo:F + hi].astype(F32)
            s = _sigmoid(g)
            sg = g * s
            a_ref[:, lo:hi] = _mx(sg * u)
            dg = _mx(da * u * (s * (1.0 + g * (1.0 - s))))
            du = _mx(da * sg)
            dgu_ref[:, lo:hi] = dg
            dgu_ref[:, F + lo:F + hi] = du
            dh = dh + _dot(dg, wgu_v[lo:hi, :], NN) + _dot(du, wgu_v[F + lo:F + hi, :], NN)
        xv = x_ref[...]
        dx, dg0 = _rms_bwd(xv, _rms_stat(xv), g0_ref[...], dh)
        dg0_ref[...] += dg0
        dx_ref[...] = dxo_v + dx

    row = lambda w: pl.BlockSpec((tm, w), lambda i, c: (i, 0))
    one = pl.BlockSpec((1, D), lambda i, c: (0, 0))
    return pl.pallas_call(
        body, name=name,
        grid_spec=pltpu.PrefetchScalarGridSpec(
            num_scalar_prefetch=1, grid=(T // tm,),
            in_specs=[row(D), row(D), row(D), row(F2), one, one] + [ANY] * 6,
            out_specs=[row(D), row(D), row(F), row(F2), one, one],
            scratch_shapes=[pltpu.VMEM((F2, D), MXU_DTYPE), pltpu.VMEM((F, D), MXU_DTYPE),
                            pltpu.SemaphoreType.DMA((3 * N_CHIPS,))]),
        out_shape=[jax.ShapeDtypeStruct((T, D), F32), jax.ShapeDtypeStruct((T, D), MXU_DTYPE),
                   jax.ShapeDtypeStruct((T, F), MXU_DTYPE), jax.ShapeDtypeStruct((T, F2), MXU_DTYPE),
                   jax.ShapeDtypeStruct((1, D), F32), jax.ShapeDtypeStruct((1, D), F32)],
        compiler_params=_cp("arbitrary"),
    )(*prefetch, dxo, x, y, gu, g0, g1, *shards)


def _norm_fwd(x, g, name):
    T, D = x.shape
    tm = _pick(T, 512, SUBLANES)

    def body(x_ref, g_ref, h_ref):
        xv = x_ref[...]
        h_ref[...] = _mx(xv * _rms_stat(xv) * g_ref[...])

    row = pl.BlockSpec((tm, D), lambda i: (i, 0))
    return pl.pallas_call(body, name=name, grid=(T // tm,), in_specs=[row, _full((1, D))], out_specs=row,
                          out_shape=jax.ShapeDtypeStruct((T, D), MXU_DTYPE), compiler_params=_cp("parallel"))(x, g)


def _out_proj_postnorm(a, b, x, g, name):
    T, K = a.shape
    D = b.shape[1]
    tm = _pick(T, 512, SUBLANES)

    def body(a_ref, b_ref, x_ref, g_ref, m_ref, o_ref):
        mv = _dot(a_ref[...], b_ref[...], NN)
        m_ref[...] = mv
        o_ref[...] = x_ref[...] + mv * _rms_stat(mv) * g_ref[...]

    row = lambda w: pl.BlockSpec((tm, w), lambda i: (i, 0))
    return pl.pallas_call(body, name=name, grid=(T // tm,),
                          in_specs=[row(K), _full((K, D)), row(D), _full((1, D))], out_specs=[row(D), row(D)],
                          out_shape=[jax.ShapeDtypeStruct((T, D), F32)] * 2, compiler_params=_cp("parallel"))(a, b, x, g)


def _postnorm_bwd_dgrad(dxo, m, g, b, name):
    T, D = m.shape
    K = b.shape[0]
    tm = _pick(T, 512, SUBLANES)

    def body(dxo_ref, m_ref, g_ref, b_ref, dm_ref, dg_ref, da_ref):
        @pl.when(pl.program_id(0) == 0)
        def _():
            dg_ref[...] = jnp.zeros_like(dg_ref)

        mv = m_ref[...]
        dm, dg = _rms_bwd(mv, _rms_stat(mv), g_ref[...], dxo_ref[...])
        dg_ref[...] += dg
        dmb = _mx(dm)
        dm_ref[...] = dmb
        da_ref[...] = _dot(dmb, b_ref[...], NT)

    row = lambda w: pl.BlockSpec((tm, w), lambda i: (i, 0))
    return pl.pallas_call(body, name=name, grid=(T // tm,), in_specs=[row(D), row(D), _full((1, D)), _full((K, D))],
                          out_specs=[row(D), _full((1, D)), row(K)],
                          out_shape=[jax.ShapeDtypeStruct((T, D), MXU_DTYPE), jax.ShapeDtypeStruct((1, D), F32),
                                     jax.ShapeDtypeStruct((T, K), F32)],
                          compiler_params=_cp("arbitrary"))(dxo, m, g, b)


def _dgrad_prenorm_bwd(a, b, add, dxo, x, g, name):
    T, K = a.shape
    D = b.shape[0]
    tm = _pick(T, 512, SUBLANES)
    tk = _pick(K, 2048)
    nk = K // tk
    has_add = add is not None

    def body(*refs):
        a_ref, b_ref = refs[:2]
        add_ref = refs[2] if has_add else None
        dxo_ref, x_ref, g_ref, dx_ref, dg_ref, acc = refs[-6:]
        i, k = pl.program_id(0), pl.program_id(1)

        @pl.when((i == 0) & (k == 0))
        def _():
            dg_ref[...] = jnp.zeros_like(dg_ref)

        @pl.when(k == 0)
        def _():
            acc[...] = add_ref[...] if has_add else jnp.zeros_like(acc)

        acc[...] += _dot(a_ref[...], b_ref[...], NT)

        @pl.when(k == nk - 1)
        def _():
            xv = x_ref[...]
            dx, dg = _rms_bwd(xv, _rms_stat(xv), g_ref[...], acc[...])
            dg_ref[...] += dg
            dx_ref[...] = dxo_ref[...] + dx

    row = pl.BlockSpec((tm, D), lambda i, k: (i, 0))
    one = pl.BlockSpec((1, D), lambda i, k: (0, 0))
    ins = [a, b] + ([add] if has_add else []) + [dxo, x, g]
    specs = ([pl.BlockSpec((tm, tk), lambda i, k: (i, k)), pl.BlockSpec((D, tk), lambda i, k: (0, k))]
             + ([row] if has_add else []) + [row, row, one])
    return pl.pallas_call(body, name=name, grid=(T // tm, nk), in_specs=specs, out_specs=[row, one],
                          out_shape=[jax.ShapeDtypeStruct((T, D), F32), jax.ShapeDtypeStruct((1, D), F32)],
                          scratch_shapes=[pltpu.VMEM((tm, D), F32)],
                          compiler_params=_cp("arbitrary", "arbitrary"))(*ins)


def _loss_fwd_bwd(y, target, name):
    T, D = y.shape
    tm = _pick(T, 512, SUBLANES)

    def body(y_ref, t_ref, l_ref, dy_ref):
        @pl.when(pl.program_id(0) == 0)
        def _():
            l_ref[...] = jnp.zeros_like(l_ref)

        e = y_ref[...] - t_ref[...]
        dy_ref[...] = e * (1.0 / D)
        l_ref[...] += 0.5 * jnp.sum(jnp.mean(e * e, axis=-1, keepdims=True), axis=0, keepdims=True)

    row = pl.BlockSpec((tm, D), lambda i: (i, 0))
    return pl.pallas_call(body, name=name, grid=(T // tm,), in_specs=[row, row],
                          out_specs=[_full((SUBLANES, LANES)), row],
                          out_shape=[jax.ShapeDtypeStruct((SUBLANES, LANES), F32), jax.ShapeDtypeStruct((T, D), F32)],
                          compiler_params=_cp("arbitrary"))(y, target)


DN_ROWS = 512


def _shift_down(prev8, cur, s):
    n = cur.shape[0]
    xx = jnp.concatenate([prev8, cur], axis=0)
    return pltpu.roll(xx, s, 0)[SUBLANES:SUBLANES + n, :]


def _shift_up(cur, next8, s):
    n = cur.shape[0]
    xx = jnp.concatenate([cur, next8], axis=0)
    return pltpu.roll(xx, n + SUBLANES - s, 0)[:n, :]


def _tile_start(r, rows):
    return r * rows if isinstance(r, int) else pl.multiple_of(r * rows, SUBLANES)


def _conv_tile(x_ref, w, r, rows):
    start = _tile_start(r, rows)
    cur = x_ref[pl.ds(start, rows), :]
    if isinstance(r, int):
        prev8 = jnp.zeros((SUBLANES, cur.shape[1]), cur.dtype)
        taps = [_shift_down(prev8, cur, DN_CONV - 1 - j) if j < DN_CONV - 1 else cur for j in range(DN_CONV)]
    else:
        taps = [x_ref[pl.ds(start - (DN_CONV - 1 - j), rows), :] if j < DN_CONV - 1 else cur for j in range(DN_CONV)]
    c = taps[0] * w[0:1, :]
    for j in range(1, DN_CONV):
        c = c + taps[j] * w[j:j + 1, :]
    return c, taps


def _dn_prep_fwd(proj, conv_w, name):
    T = proj.shape[0]
    W = DN_HEADS * DN_HEAD_DIM
    rows = min(DN_ROWS, T)
    n_inner = T // rows
    scale = DN_HEAD_DIM ** -0.5

    def body(x_ref, w_ref, o_ref):
        cb = pl.program_id(0)
        w = w_ref[...]
        is_qk = cb < 2 * DN_HEADS
        post = jnp.where(cb < DN_HEADS, scale, 1.0)

        def step(r, carry):
            c, _ = _conv_tile(x_ref, w, r, rows)
            s = c * _sigmoid(c)
            rinv = lax.rsqrt(jnp.sum(s * s, axis=-1, keepdims=True) + L2_EPS)
            o_ref[pl.ds(_tile_start(r, rows), rows), :] = jnp.where(is_qk, s * rinv * post, s)
            return carry

        step(0, 0)
        lax.fori_loop(1, n_inner, step, 0)

    col = pl.BlockSpec((T, LANES), lambda j: (0, j))
    return pl.pallas_call(body, name=name, grid=(3 * W // LANES,),
                          in_specs=[col, pl.BlockSpec((DN_CONV, LANES), lambda j: (0, j))], out_specs=col,
                          out_shape=jax.ShapeDtypeStruct((T, 3 * W), F32), compiler_params=_cp("parallel"))(proj, conv_w)


def _dn_prep_bwd(proj, conv_w, dqkv, name):
    T = proj.shape[0]
    W = DN_HEADS * DN_HEAD_DIM
    rows = min(DN_ROWS, T)
    n_inner = T // rows
    scale = DN_HEAD_DIM ** -0.5

    def body(x_ref, w_ref, dy_ref, dx_ref, dw_ref, dc_scr):
        cb = pl.program_id(0)
        w = w_ref[...]
        is_qk = cb < 2 * DN_HEADS
        post = jnp.where(cb < DN_HEADS, scale, 1.0)

        def step1(r, dws):
            c, taps = _conv_tile(x_ref, w, r, rows)
            sg = _sigmoid(c)
            s = c * sg
            rinv = lax.rsqrt(jnp.sum(s * s, axis=-1, keepdims=True) + L2_EPS)
            dy = dy_ref[pl.ds(_tile_start(r, rows), rows), :]
            yn = s * rinv
            dyn = dy * post
            ds_qk = rinv * (dyn - yn * jnp.sum(dyn * yn, axis=-1, keepdims=True))
            ds = jnp.where(is_qk, ds_qk, dy)
            dc = ds * (sg * (1.0 + c * (1.0 - sg)))
            dc_scr[pl.ds(_tile_start(r, rows), rows), :] = dc
            return tuple(dws[j] + jnp.sum(dc * taps[j], axis=0, keepdims=True) for j in range(DN_CONV))

        zero = jnp.zeros((1, LANES), F32)
        dws = lax.fori_loop(1, n_inner, step1, step1(0, (zero,) * DN_CONV))
        for j in range(DN_CONV):
            dw_ref[j:j + 1, :] = dws[j]

        def step2(r, carry):
            start = _tile_start(r, rows)
            cur = dc_scr[pl.ds(start, rows), :]
            dx = cur * w[DN_CONV - 1:DN_CONV, :]
            for j in range(DN_CONV - 1):
                s = DN_CONV - 1 - j
                if isinstance(r, int):
                    up = _shift_up(cur, jnp.zeros((SUBLANES, LANES), F32), s)
                else:
                    up = dc_scr[pl.ds(start + s, rows), :]
                dx = dx + up * w[j:j + 1, :]
            dx_ref[pl.ds(start, rows), :] = _mx(dx)
            return carry

        lax.fori_loop(0, n_inner - 1, step2, 0)
        step2(n_inner - 1, 0)

    col = pl.BlockSpec((T, LANES), lambda j: (0, j))
    wspec = pl.BlockSpec((DN_CONV, LANES), lambda j: (0, j))
    return pl.pallas_call(body, name=name, grid=(3 * W // LANES,), in_specs=[col, wspec, col], out_specs=[col, wspec],
                          out_shape=[jax.ShapeDtypeStruct((T, 3 * W), MXU_DTYPE), jax.ShapeDtypeStruct((DN_CONV, 3 * W), F32)],
                          scratch_shapes=[pltpu.VMEM((T, LANES), F32)], compiler_params=_cp("parallel"))(proj, conv_w, dqkv)


def _softplus(x):
    return jnp.maximum(x, 0.0) + jnp.log(1.0 + jnp.exp(-jnp.abs(x)))


def _dn_gate_fwd(ba, a_log, dt_bias, name):
    T = ba.shape[0]
    tm = _pick(T, 1024, SUBLANES)

    def body(ba_ref, al_ref, dt_ref, beta_ref, g_ref):
        beta_ref[...] = _sigmoid(ba_ref[:, :LANES])
        g_ref[...] = -jnp.exp(al_ref[...]) * _softplus(ba_ref[:, LANES:] + dt_ref[...])

    row = lambda w: pl.BlockSpec((tm, w), lambda i: (i, 0))
    return pl.pallas_call(body, name=name, grid=(T // tm,), in_specs=[row(2 * LANES), _full((1, LANES)), _full((1, LANES))],
                          out_specs=[row(LANES), row(LANES)],
                          out_shape=[jax.ShapeDtypeStruct((T, LANES), F32)] * 2, compiler_params=_cp("parallel"))(ba, a_log, dt_bias)


def _dn_gate_bwd(ba, a_log, dt_bias, dbeta, dg, name):
    T = ba.shape[0]
    tm = _pick(T, 1024, SUBLANES)

    def body(ba_ref, al_ref, dt_ref, dbeta_ref, dg_ref, dba_ref, dal_ref, ddt_ref):
        @pl.when(pl.program_id(0) == 0)
        def _():
            dal_ref[...] = jnp.zeros_like(dal_ref)
            ddt_ref[...] = jnp.zeros_like(ddt_ref)

        beta = _sigmoid(ba_ref[:, :LANES])
        dba_ref[:, :LANES] = _mx(dbeta_ref[...] * beta * (1.0 - beta))
        pre = ba_ref[:, LANES:] + dt_ref[...]
        ea = jnp.exp(al_ref[...])
        dgv = dg_ref[...]
        da = dgv * (-ea) * _sigmoid(pre)
        dba_ref[:, LANES:] = _mx(da)
        ddt_ref[...] += jnp.sum(da, axis=0, keepdims=True)
        dal_ref[...] += jnp.sum(dgv * (-ea) * _softplus(pre), axis=0, keepdims=True)

    row = lambda w: pl.BlockSpec((tm, w), lambda i: (i, 0))
    one = _full((1, LANES))
    return pl.pallas_call(body, name=name, grid=(T // tm,), in_specs=[row(2 * LANES), one, one, row(LANES), row(LANES)],
                          out_specs=[row(2 * LANES), one, one],
                          out_shape=[jax.ShapeDtypeStruct((T, 2 * LANES), MXU_DTYPE), jax.ShapeDtypeStruct((1, LANES), F32),
                                     jax.ShapeDtypeStruct((1, LANES), F32)],
                          compiler_params=_cp("arbitrary"))(ba, a_log, dt_bias, dbeta, dg)


def _tri(c, strict):
    i = lax.broadcasted_iota(jnp.int32, (c, c), 0)
    j = lax.broadcasted_iota(jnp.int32, (c, c), 1)
    return (i > j) if strict else (i >= j)


def _inv_unit_lower(ls):
    c = ls[0].shape[0]
    i = lax.broadcasted_iota(jnp.int32, (c, c), 0)
    j = lax.broadcasted_iota(jnp.int32, (c, c), 1)
    eye = jnp.where(i == j, 1.0, 0.0)
    facs = [[eye - l for l in ls]]
    cur = ls
    for _ in range(int(math.log2(c)) - 1):
        cur = [_dot(p, p, NN, TRI_PREC) for p in cur]
        facs.append([eye + p for p in cur])
    while len(facs) > 1:
        nxt = [[_dot(a, b, NN, TRI_PREC) for a, b in zip(facs[t], facs[t + 1])] for t in range(0, len(facs) - 1, 2)]
        if len(facs) % 2:
            nxt.append(facs[-1])
        facs = nxt
    return facs[0]


def _chunk_gates(g_blk):
    c = g_blk.shape[0]
    gcs = _dot(jnp.where(_tri(c, False), 1.0, 0.0), g_blk, NN, HI)
    return gcs, gcs.T


def _head_chunk(h, qh, kh, vh, beta_blk, gcs, gcs_t):
    c = qh.shape[0]
    incl = _tri(c, False)
    gc_col = gcs[:, h:h + 1]
    gc_row = gcs_t[h:h + 1, :]
    gc_last = gcs_t[h:h + 1, c - 1:c]
    dec = jnp.where(incl, jnp.exp(jnp.where(incl, gc_col - gc_row, 0.0)), 0.0)
    gam = jnp.exp(gc_col)
    rr = jnp.exp(gc_last - gc_col)
    gl = jnp.exp(gc_last)
    b = beta_blk[:, h:h + 1]
    kb = kh * b
    vb = vh * b
    both = _dot(jnp.concatenate([_mx(kb), _mx(qh)], axis=0), _mx(kh), NT)
    lmat = jnp.where(_tri(c, True), both[:c] * dec, 0.0)
    pmat = jnp.where(incl, both[c:] * dec, 0.0)
    return dict(dec=dec, gam=gam, rr=rr, gl=gl, b=b, kb=kb, vb=vb, lmat=lmat, pmat=pmat)


def _solve_uw(tinv, q):
    return _dot(tinv, jnp.concatenate([q["vb"], q["kb"] * q["gam"]], axis=1), NN, TRI_PREC)


def _dn_scan_fwd(qkv, beta, g, proj, norm_g, name):
    T = qkv.shape[0]
    C, H, Dh = DN_CHUNK, DN_HEADS, DN_HEAD_DIM
    W = H * Dh
    N = T // C

    def body(q_ref, k_ref, v_ref, beta_ref, g_ref, z_ref, ng_ref, og_ref, o_ref, tinv_ref, s_ref, state):
        @pl.when(pl.program_id(0) == 0)
        def _():
            state[...] = jnp.zeros_like(state)

        gcs, gcs_t = _chunk_gates(g_ref[...])
        beta_blk = beta_ref[...]
        ng = ng_ref[...]
        heads = range(H)
        cs = [slice(h * Dh, (h + 1) * Dh) for h in heads]
        qs = [_head_chunk(h, q_ref[:, cs[h]], k_ref[:, cs[h]], v_ref[:, cs[h]], beta_blk, gcs, gcs_t) for h in heads]
        tinvs = _inv_unit_lower([q["lmat"] for q in qs])
        for h in heads:
            tinv_ref[h] = tinvs[h]
        uws = [_solve_uw(tinvs[h], qs[h]) for h in heads]
        ss = [state[h] for h in heads]
        for h in heads:
            s_ref[0, h] = ss[h]
        sbs = [_mx(s) for s in ss]
        vnbs = [_mx(uws[h][:, :Dh] - _dot(_mx(uws[h][:, Dh:]), sbs[h], NN)) for h in heads]
        os_ = [_dot(jnp.concatenate([_mx(q_ref[:, cs[h]] * qs[h]["gam"]), _mx(qs[h]["pmat"])], axis=1),
                    jnp.concatenate([sbs[h], vnbs[h]], axis=0), NN) for h in heads]
        for h in heads:
            state[h] = ss[h] * qs[h]["gl"] + _dot(_mx((k_ref[:, cs[h]] * qs[h]["rr"]).T), vnbs[h], NN)
        for h in heads:
            o = os_[h]
            o_ref[:, cs[h]] = o
            zh = z_ref[:, cs[h]]
            og_ref[:, cs[h]] = _mx(o * _rms_stat(o) * ng * (zh * _sigmoid(zh)))

    blk = lambda j: pl.BlockSpec((C, W), lambda n: (n, j))
    small = pl.BlockSpec((C, LANES), lambda n: (n, 0))
    return pl.pallas_call(
        body, name=name, grid=(N,),
        in_specs=[blk(0), blk(1), blk(2), small, small, blk(3), _full((1, Dh))],
        out_specs=[blk(0), blk(0), pl.BlockSpec((H, C, C), lambda n: (0, n, 0)),
                   pl.BlockSpec((1, H, Dh, Dh), lambda n: (n, 0, 0, 0))],
        out_shape=[jax.ShapeDtypeStruct((T, W), MXU_DTYPE), jax.ShapeDtypeStruct((T, W), F32),
                   jax.ShapeDtypeStruct((H, T, C), F32), jax.ShapeDtypeStruct((N, H, Dh, Dh), F32)],
        scratch_shapes=[pltpu.VMEM((H, Dh, Dh), F32)],
        compiler_params=_cp("arbitrary"),
    )(qkv, qkv, qkv, beta, g, proj, norm_g)


def _dn_scan_bwd(qkv, beta, g, proj, norm_g, o, tinv, s_all, dog, name):
    T = qkv.shape[0]
    C, H, Dh = DN_CHUNK, DN_HEADS, DN_HEAD_DIM
    W = H * Dh
    N = T // C

    def body(q_ref, k_ref, v_ref, beta_ref, g_ref, z_ref, ng_ref, o_ref, tinv_ref, s_ref, dog_ref,
             dqkv_ref, dbeta_ref, dg_ref, dz_ref, dng_ref, dstate):
        @pl.when(pl.program_id(0) == 0)
        def _():
            dstate[...] = jnp.zeros_like(dstate)
            dng_ref[...] = jnp.zeros_like(dng_ref)

        gcs, gcs_t = _chunk_gates(g_ref[...])
        beta_blk = beta_ref[...]
        ng = ng_ref[...]
        incl = _tri(C, False)
        strict = _tri(C, True)
        lane = lax.broadcasted_iota(jnp.int32, (C, LANES), 1)
        rowi = lax.broadcasted_iota(jnp.int32, (C, 1), 0)
        ones = jnp.ones((C, LANES), F32)
        dbeta_acc = jnp.zeros((C, LANES), F32)
        dgc_acc = jnp.zeros((C, LANES), F32)
        dng_acc = jnp.zeros((1, Dh), F32)
        cs = [slice(h * Dh, (h + 1) * Dh) for h in range(H)]
        rsum = lambda t: jnp.sum(t, axis=1, keepdims=True)
        for heads in (range(0, H // 2), range(H // 2, H)):
            dobs = {}
            for h in heads:
                oh, zh, dogh = o_ref[:, cs[h]], z_ref[:, cs[h]], dog_ref[:, cs[h]]
                rstat = _rms_stat(oh)
                sz = _sigmoid(zh)
                dz_ref[:, cs[h]] = _mx(dogh * (oh * rstat * ng) * (sz * (1.0 + zh * (1.0 - sz))))
                do, dng = _rms_bwd(oh, rstat, ng, dogh * (zh * sz))
                dng_acc = dng_acc + dng
                dobs[h] = _mx(do)
            qs = {h: _head_chunk(h, q_ref[:, cs[h]], k_ref[:, cs[h]], v_ref[:, cs[h]], beta_blk, gcs, gcs_t) for h in heads}
            tms = {h: tinv_ref[h] for h in heads}
            uws = {h: _solve_uw(tms[h], qs[h]) for h in heads}
            ss = {h: s_ref[0, h] for h in heads}
            sbs = {h: _mx(ss[h]) for h in heads}
            wbs = {h: _mx(uws[h][:, Dh:]) for h in heads}
            vnbs = {h: _mx(uws[h][:, :Dh] - _dot(wbs[h], sbs[h], NN)) for h in heads}
            dsns = {h: dstate[h] for h in heads}
            dsbs = {h: _mx(dsns[h]) for h in heads}
            dvnews = {h: _dot(_mx(qs[h]["pmat"]), dobs[h], TN) + _dot(_mx(k_ref[:, cs[h]] * qs[h]["rr"]), dsbs[h], NN)
                      for h in heads}
            dvb16s = {h: _mx(dvnews[h]) for h in heads}
            dps = {h: jnp.where(incl, _dot(dobs[h], vnbs[h], NT), 0.0) for h in heads}
            dqds = {h: _dot(dobs[h], sbs[h], NT) for h in heads}
            dkds = {h: _dot(vnbs[h], dsbs[h], NT) for h in heads}
            dgls = {h: jnp.sum(rsum(ss[h] * dsns[h]), axis=0, keepdims=True) for h in heads}
            dws = {h: -_dot(dvb16s[h], sbs[h], NT) for h in heads}
            for h in heads:
                dstate[h] = qs[h]["gl"] * dsns[h] + _dot(
                    jnp.concatenate([_mx(q_ref[:, cs[h]] * qs[h]["gam"]), -wbs[h]], axis=0),
                    jnp.concatenate([dobs[h], dvb16s[h]], axis=0), TN)
            dsols = {h: _dot(tms[h], jnp.concatenate([dvnews[h], dws[h]], axis=1), TN, TRI_PREC) for h in heads}
            dvbs = {h: dsols[h][:, :Dh] for h in heads}
            dkbgs = {h: dsols[h][:, Dh:] for h in heads}
            dls = {h: jnp.where(strict, -_dot(dsols[h], uws[h], NT, TRI_PREC), 0.0) for h in heads}
            mmats = {h: dls[h] * qs[h]["lmat"] + dps[h] * qs[h]["pmat"] for h in heads}
            dgcs = {h: rsum(mmats[h]) - _dot(mmats[h], ones, TN, TRI_PREC)[:, :1] for h in heads}
            dboth = {h: jnp.concatenate([_mx(dls[h] * qs[h]["dec"]), _mx(dps[h] * qs[h]["dec"])], axis=0) for h in heads}
            for h in heads:
                q = qs[h]
                qh, kh, vh = q_ref[:, cs[h]], k_ref[:, cs[h]], v_ref[:, cs[h]]
                gam, rr, b, kb = q["gam"], q["rr"], q["b"], q["kb"]
                on_k = _dot(dboth[h], _mx(kh), NN)
                dkb = on_k[:C] + dkbgs[h] * gam
                dk = _dot(dboth[h], jnp.concatenate([_mx(kb), _mx(qh)], axis=0), TN) + dkb * b + dkds[h] * rr
                dq = on_k[C:] + dqds[h] * gam
                dgam = rsum(dkbgs[h] * kb) + rsum(dqds[h] * qh)
                dr = rsum(dkds[h] * kh)
                dgc_last = jnp.sum(dr * rr, axis=0, keepdims=True) + dgls[h] * q["gl"]
                dgc = dgcs[h] + dgam * gam - dr * rr + jnp.where(rowi == C - 1, dgc_last, 0.0)
                dbeta = rsum(dvbs[h] * vh) + rsum(dkb * kh)
                dqkv_ref[:, cs[h]] = dq
                dqkv_ref[:, W + h * Dh:W + (h + 1) * Dh] = dk
                dqkv_ref[:, 2 * W + h * Dh:2 * W + (h + 1) * Dh] = dvbs[h] * b
                dbeta_acc = jnp.where(lane == h, dbeta, dbeta_acc)
                dgc_acc = jnp.where(lane == h, dgc, dgc_acc)
        dbeta_ref[...] = dbeta_acc
        dg_ref[...] = _dot(jnp.where(incl, 1.0, 0.0), dgc_acc, TN, HI)
        dng_ref[...] += dng_acc

    rev = lambda n: N - 1 - n
    blk = lambda j: pl.BlockSpec((C, W), lambda n: (rev(n), j))
    small = pl.BlockSpec((C, LANES), lambda n: (rev(n), 0))
    return pl.pallas_call(
        body, name=name, grid=(N,),
        in_specs=[blk(0), blk(1), blk(2), small, small, blk(3), _full((1, Dh)), blk(0),
                  pl.BlockSpec((H, C, C), lambda n: (0, rev(n), 0)),
                  pl.BlockSpec((1, H, Dh, Dh), lambda n: (rev(n), 0, 0, 0)), blk(0)],
        out_specs=[pl.BlockSpec((C, 3 * W), lambda n: (rev(n), 0)), small, small, blk(0), _full((1, Dh))],
        out_shape=[jax.ShapeDtypeStruct((T, 3 * W), F32), jax.ShapeDtypeStruct((T, LANES), F32),
                   jax.ShapeDtypeStruct((T, LANES), F32), jax.ShapeDtypeStruct((T, W), MXU_DTYPE),
                   jax.ShapeDtypeStruct((1, Dh), F32)],
        scratch_shapes=[pltpu.VMEM((H, Dh, Dh), F32)],
        compiler_params=_cp("arbitrary"),
    )(qkv, qkv, qkv, beta, g, proj, norm_g, o, tinv, s_all, dog)


_INV_SQRT2 = 0.7071067811865476
_INV_SQRT_2PI = 0.3989422804014327


def _sg_recompute(zp_ref, bin_ref, lng_ref, lnb_ref):
    E = SG_WIDTH
    zin = zp_ref[...] + bin_ref[...]
    cdf = 0.5 * (1.0 + lax.erf(zin * _INV_SQRT2))
    zz = zin * cdf
    u = zz[:, :E]
    vp = zz[:, E:]
    mu = jnp.mean(vp, axis=-1, keepdims=True)
    xc = vp - mu
    rstd = lax.rsqrt(jnp.mean(xc * xc, axis=-1, keepdims=True) + LN_EPS)
    xhat = xc * rstd
    v = xhat * lng_ref[...] + lnb_ref[...]
    return zin, cdf, u, xhat, rstd, v


def _sg_masked_ws(ws_ref, g):
    return _mx(jnp.where(_tri(SG_CHUNK, False), ws_ref[g], 0.0))


def _sg_fwd(zpre, b_in, ln_g, ln_b, w_s, b_s_t, name):
    T = zpre.shape[0]
    E, G, C, GW = SG_WIDTH, SG_GROUPS, SG_CHUNK, SG_GROUP_W

    def body(zp_ref, bin_ref, lng_ref, lnb_ref, ws_ref, bst_ref, um_ref):
        _, _, u, _, _, v = _sg_recompute(zp_ref, bin_ref, lng_ref, lnb_ref)
        bst = bst_ref[...]
        for g in range(G):
            cs = slice(g * GW, (g + 1) * GW)
            mixed = _dot(_sg_masked_ws(ws_ref, g), _mx(v[:, cs]), NN) + bst[:, g:g + 1]
            um_ref[:, cs] = _mx(u[:, cs] * mixed)

    return pl.pallas_call(
        body, name=name, grid=(T // C,),
        in_specs=[pl.BlockSpec((C, 2 * E), lambda n: (n, 0)), _full((1, 2 * E)), _full((1, E)), _full((1, E)),
                  _full((G, C, C)), _full((C, LANES))],
        out_specs=pl.BlockSpec((C, E), lambda n: (n, 0)),
        out_shape=jax.ShapeDtypeStruct((T, E), MXU_DTYPE), compiler_params=_cp("parallel"),
    )(zpre, b_in, ln_g, ln_b, w_s, b_s_t)


def _sg_bwd(zpre, b_in, ln_g, ln_b, w_s, b_s_t, dum, name):
    T = zpre.shape[0]
    E, G, C, GW = SG_WIDTH, SG_GROUPS, SG_CHUNK, SG_GROUP_W

    def body(zp_ref, bin_ref, lng_ref, lnb_ref, ws_ref, bst_ref, dum_ref,
             dz_ref, dbin_ref, dlng_ref, dlnb_ref, dws_ref, dbst_ref):
        @pl.when(pl.program_id(0) == 0)
        def _():
            for r in (dbin_ref, dlng_ref, dlnb_ref, dws_ref, dbst_ref):
                r[...] = jnp.zeros_like(r)

        zin, cdf, u, xhat, rstd, v = _sg_recompute(zp_ref, bin_ref, lng_ref, lnb_ref)
        bst = bst_ref[...]
        lane = lax.broadcasted_iota(jnp.int32, (C, LANES), 1)
        dum_v = dum_ref[...]
        dbst = jnp.zeros((C, LANES), F32)
        du_parts, dv_parts = [], []
        for g in range(G):
            cs = slice(g * GW, (g + 1) * GW)
            wsm = _sg_masked_ws(ws_ref, g)
            vg = _mx(v[:, cs])
            mixed = _dot(wsm, vg, NN) + bst[:, g:g + 1]
            dumg = dum_v[:, cs]
            du_parts.append(dumg * mixed)
            dmixed = dumg * u[:, cs]
            dmb = _mx(dmixed)
            dv_parts.append(_dot(wsm, dmb, TN))
            dws_ref[g] += _dot(dmb, vg, NT)
            dbst = jnp.where(lane == g, jnp.sum(dmixed, axis=1, keepdims=True), dbst)
        dbst_ref[...] += dbst
        du = jnp.concatenate(du_parts, axis=1)
        dv = jnp.concatenate(dv_parts, axis=1)
        dlng_ref[...] += jnp.sum(dv * xhat, axis=0, keepdims=True)
        dlnb_ref[...] += jnp.sum(dv, axis=0, keepdims=True)
        dxh = dv * lng_ref[...]
        dvp = rstd * (dxh - jnp.mean(dxh, axis=-1, keepdims=True) - xhat * jnp.mean(dxh * xhat, axis=-1, keepdims=True))
        dzz = jnp.concatenate([du, dvp], axis=1)
        dzin = dzz * (cdf + zin * (_INV_SQRT_2PI * jnp.exp(-0.5 * zin * zin)))
        dz_ref[...] = _mx(dzin)
        dbin_ref[...] += jnp.sum(dzin, axis=0, keepdims=True)

    return pl.pallas_call(
        body, name=name, grid=(T // C,),
        in_specs=[pl.BlockSpec((C, 2 * E), lambda n: (n, 0)), _full((1, 2 * E)), _full((1, E)), _full((1, E)),
                  _full((G, C, C)), _full((C, LANES)), pl.BlockSpec((C, E), lambda n: (n, 0))],
        out_specs=[pl.BlockSpec((C, 2 * E), lambda n: (n, 0)), _full((1, 2 * E)), _full((1, E)), _full((1, E)),
                   _full((G, C, C)), _full((C, LANES))],
        out_shape=[jax.ShapeDtypeStruct((T, 2 * E), MXU_DTYPE), jax.ShapeDtypeStruct((1, 2 * E), F32),
                   jax.ShapeDtypeStruct((1, E), F32), jax.ShapeDtypeStruct((1, E), F32),
                   jax.ShapeDtypeStruct((G, C, C), F32), jax.ShapeDtypeStruct((C, LANES), F32)],
        compiler_params=_cp("arbitrary"),
    )(zpre, b_in, ln_g, ln_b, w_s, b_s_t, dum)


def _row(v):
    return v.reshape(1, -1)


def _pad_lanes(v):
    v = v.reshape(1, -1)
    return jnp.pad(v, ((0, 0), (0, LANES - v.shape[1])))


def _local_step(x, target, p, weights_for, grads_ready=None, small_ready=None):
    ng = p["norm_g"]
    grads = {}
    dng = [[None] * 6 for _ in range(2)]
    order = [jnp.zeros((), F32)]

    def tell(group):
        zero = grads_ready(group, grads) if grads_ready is not None else None
        if zero is not None:
            order[0] = zero

    def gain(i, s):
        return _row(ng[i, s]) + order[0]

    def ffn_f(xin, i, j, tag):
        wt = weights_for("ffn" + tag, xin)
        xo, h, gu, y = _ffn_fwd(xin, _row(ng[i, 4 * j]), _row(ng[i, 4 * j + 1]), wt, "ffn_fwd_" + tag)
        return xo, (xin, h, gu, y, wt)

    x1, sv_f00 = ffn_f(x, 0, 0, "00")
    dnw = weights_for("dn", x1)
    hn0 = _norm_fwd(x1, _row(ng[0, 2]), "dn_prenorm")
    proj = _mm(hn0, dnw["dn_wqkvz"], "nn", "dn_proj")
    ba = _mm(hn0, dnw["dn_wba"], "nn", "dn_proj_ba")
    a_log = _pad_lanes(p["dn_a_log"])
    dt_bias = _pad_lanes(p["dn_dt_bias"])
    dn_ng = _row(p["dn_norm_g"])
    qkv = _dn_prep_fwd(proj, p["dn_conv_w"], "dn_prep_fwd")
    beta, gdec = _dn_gate_fwd(ba, a_log, dt_bias, "dn_gate_fwd")
    og, o_raw, tinv, s_all = _dn_scan_fwd(qkv, beta, gdec, proj, dn_ng, "dn_scan_fwd")
    m0, x2 = _out_proj_postnorm(og, dnw["dn_wout"], x1, _row(ng[0, 3]), "dn_out")
    x3, sv_f01 = ffn_f(x2, 0, 1, "01")
    x4, sv_f10 = ffn_f(x3, 1, 0, "10")
    sgw = weights_for("sg", x4)
    hn1 = _norm_fwd(x4, _row(ng[1, 2]), "sg_prenorm")
    zpre = _mm(hn1, sgw["sg_win"], "nn", "sg_proj")
    sg_bin = _row(p["sg_b_in"])
    sg_lng = _row(p["sg_ln_g"])
    sg_lnb = _row(p["sg_ln_b"])
    sg_bst = jnp.pad(p["sg_b_s"].T, ((0, 0), (0, LANES - SG_GROUPS)))
    um = _sg_fwd(zpre, sg_bin, sg_lng, sg_lnb, p["sg_w_s"], sg_bst, "sg_fwd")
    m1, x5 = _out_proj_postnorm(um, sgw["sg_wout"], x4, _row(ng[1, 3]), "sg_out")
    x6, sv_f11 = ffn_f(x5, 1, 1, "11")
    loss_part, dx = _loss_fwd_bwd(x6, target, "loss")

    def ffn_b(dxo, sv, i, j, tag, last=False):
        xin, h, gu, y, wt = sv
        dxi, dy, a, dgu, dg0, dg1 = _ffn_bwd(dxo, xin, y, gu, gain(i, 4 * j), gain(i, 4 * j + 1), wt, "ffn_bwd_" + tag)
        dng[i][4 * j] = dg0
        dng[i][4 * j + 1] = dg1
        after = None
        if last:
            grads["norm_g"] = jnp.stack([jnp.concatenate(dng[t], axis=0) for t in range(2)])
            after = small_ready(grads, loss_part) if small_ready is not None else None
        grads["wd" + tag] = _mm(a, dy, "tn", "ffn_wgrad_down_" + tag, after=after)
        grads["wguT" + tag] = _mm(dgu, h, "tn", "ffn_wgrad_up_" + tag, after=after)
        tell("ffn" + tag)
        return dxi

    dx = ffn_b(dx, sv_f11, 1, 1, "11")
    dm1, dng[1][3], dum = _postnorm_bwd_dgrad(dx, m1, gain(1, 3), sgw["sg_wout"], "sg_dgrad_out")
    grads["sg_w_out"] = _mm(um, dm1, "tn", "sg_wgrad_out")
    dz1, dbin, dlng, dlnb, dws, dbst = _sg_bwd(zpre, sg_bin, sg_lng, sg_lnb, p["sg_w_s"], sg_bst, dum, "sg_bwd")
    grads["sg_w_inT"] = _mm(dz1, hn1, "tn", "sg_wgrad_in")
    tell("sg")
    dx, dng[1][2] = _dgrad_prenorm_bwd(dz1, sgw["sg_win"], None, dx, x4, gain(1, 2), "sg_dgrad_in")
    grads["sg_b_in"] = dbin.reshape(1, -1)
    grads["sg_ln_g"] = dlng.reshape(1, -1)
    grads["sg_ln_b"] = dlnb.reshape(1, -1)
    grads["sg_w_s"] = jnp.where(jnp.tril(jnp.ones((SG_CHUNK, SG_CHUNK), bool)), dws, 0.0)[None]
    grads["sg_b_s"] = dbst[:, :SG_GROUPS].T[None]
    dx = ffn_b(dx, sv_f10, 1, 0, "10")
    dx = ffn_b(dx, sv_f01, 0, 1, "01")
    dm0, dng[0][3], dog = _postnorm_bwd_dgrad(dx, m0, gain(0, 3), dnw["dn_wout"], "dn_dgrad_out")
    grads["dn_w_out"] = _mm(og, dm0, "tn", "dn_wgrad_out")
    dqkv, dbeta, dgdec, dz0, dnng = _dn_scan_bwd(qkv, beta, gdec, proj, dn_ng, o_raw, tinv, s_all, dog, "dn_scan_bwd")
    dqkv_pre, dconv = _dn_prep_bwd(proj, p["dn_conv_w"], dqkv, "dn_prep_bwd")
    dba, dal, ddt = _dn_gate_bwd(ba, a_log, dt_bias, dbeta, dgdec, "dn_gate_bwd")
    W3 = 3 * DN_HEADS * DN_HEAD_DIM
    dw_qkv = _mm(hn0, dqkv_pre, "tn", "dn_wgrad_qkv")
    dw_z = _mm(hn0, dz0, "tn", "dn_wgrad_z")
    dw_ba = _mm(hn0, dba, "tn", "dn_wgrad_ba")
    grads["dn_w_in"] = jnp.concatenate(
        [dw_qkv, dw_z, dw_ba[:, :DN_HEADS], dw_ba[:, LANES:LANES + DN_HEADS]], axis=1)
    tell("dn")
    dh0 = _mm(dqkv_pre, dnw["dn_wqkvz"][:, :W3], "nt", "dn_dgrad_qkv")
    dh0 = _mm(dz0, dnw["dn_wqkvz"][:, W3:], "nt", "dn_dgrad_z", add=dh0)
    dx, dng[0][2] = _dgrad_prenorm_bwd(dba, dnw["dn_wba"], dh0, dx, x1, gain(0, 2), "dn_dgrad_ba")
    grads["dn_conv_w"] = dconv[None]
    grads["dn_a_log"] = dal[:, :DN_HEADS]
    grads["dn_dt_bias"] = ddt[:, :DN_HEADS]
    grads["dn_norm_g"] = dnng
    dx = ffn_b(dx, sv_f00, 0, 0, "00", last=True)
    return loss_part, dx, grads


def _mesh_pos():
    return lax.axis_index("x"), lax.axis_index("y"), lax.axis_index("c")


def _other_chips(x, y):
    return [(1 - x, y), (x, 1 - y), (1 - x, 1 - y)]


def _allgather_chips(arrs, name):
    n = len(arrs)

    def body(*refs):
        ins, outs = refs[:n], refs[n:2 * n]
        ici_send, ici_recv, d2d_send, d2d_recv = refs[2 * n:]
        x, y, c = _mesh_pos()
        me = 2 * x + y
        chips = _other_chips(x, y)
        sibling = (x, y, 1 - c)

        def ici(i, j, k):
            cx, cy = chips[j]
            return pltpu.make_async_remote_copy(src_ref=ins[i].at[c], dst_ref=outs[i].at[k, c], send_sem=ici_send.at[3 * i + j],
                                                recv_sem=ici_recv.at[3 * i + j], device_id=(cx, cy, c), device_id_type=MESH)

        def d2d(i, j, h):
            cx, cy = chips[j]
            slot = outs[i].at[2 * cx + cy, h]
            return pltpu.make_async_remote_copy(src_ref=slot, dst_ref=slot, send_sem=d2d_send.at[3 * i + j],
                                                recv_sem=d2d_recv.at[3 * i + j], device_id=sibling, device_id_type=MESH)

        sends = [ici(i, j, me) for i in range(n) for j in range(3)]
        for cp in sends:
            cp.start()
        for i in range(n):
            for j, (cx, cy) in enumerate(chips):
                ici(i, j, 2 * cx + cy).wait_recv()
                fwd = d2d(i, j, c)
                fwd.start()
                sends.append(fwd)
        for i in range(n):
            for j in range(3):
                d2d(i, j, 1 - c).wait_recv()
        for cp in sends:
            cp.wait_send()

    return pl.pallas_call(
        body, name=name, in_specs=[ANY] * n, out_specs=[ANY] * n,
        out_shape=[jax.ShapeDtypeStruct((N_CHIPS,) + a.shape, a.dtype) for a in arrs],
        scratch_shapes=[pltpu.SemaphoreType.DMA((3 * n,))] * 4,
    )(*arrs)


HBM = pl.BlockSpec(memory_space=pltpu.HBM)
SEM = pl.BlockSpec(memory_space=pltpu.SEMAPHORE)
TOKEN = jax.ShapeDtypeStruct((SUBLANES, LANES), F32)


_PEERS = {"gather": 3, "scatter": 3, "swap": 1, "all": N_DEV - 1}


def _land_shape(kind, shape):
    if kind == "gather":
        return (N_CHIPS,) + shape
    if kind == "all":
        return (N_DEV,) + shape
    return (N_CHIPS,) + shape[2:] if kind == "swap" else shape


def _peer_copies(kind, flags, src_refs, land_refs, send_sems, recv_sems, receiving):
    x, y, c = _mesh_pos()
    me4, me8 = 2 * x + y, 4 * x + 2 * y + c
    np_ = _PEERS[kind]
    cps = []
    for i, (src, land) in enumerate(zip(src_refs, land_refs)):
        if kind == "swap":
            half = src.at[1 - c] if flags[i] else src.at[:, 1 - c]
            plan = [((x, y, 1 - c), half, land)]
        elif kind == "all":
            masks = [(mx, my, mc) for mx in (0, 1) for my in (0, 1) for mc in (0, 1)][1:]
            peers = [(jnp.where(mx, 1 - x, x), jnp.where(my, 1 - y, y), jnp.where(mc, 1 - c, c)) for mx, my, mc in masks]
            plan = [(p, src, land.at[4 * p[0] + 2 * p[1] + p[2] if receiving else me8]) for p in peers]
        else:
            plan = []
            for cx, cy in _other_chips(x, y):
                k = 2 * cx + cy
                s = src.at[me4 if receiving else k] if kind == "scatter" else src
                plan.append(((cx, cy, c), s, land.at[k if receiving else me4]))
        for j, (peer, s, d) in enumerate(plan):
            cps.append(pltpu.make_async_remote_copy(src_ref=s, dst_ref=d, send_sem=send_sems.at[np_ * i + j],
                                                    recv_sem=recv_sems.at[np_ * i + j], device_id=peer, device_id_type=MESH))
    return cps


def _copies_start(kind, srcs, after, name, flags=None):
    n = len(srcs)
    ns = _PEERS[kind] * n
    lands = [lax.empty(_land_shape(kind, s.shape), s.dtype) for s in srcs]
    after = [] if after is None else [after]

    def body(*refs):
        src_refs, land_refs = refs[:n], refs[n:2 * n]
        send_sems, recv_sems = refs[2 * n + len(after)], refs[2 * n + len(after) + 1]
        token = refs[-1]
        for cp in _peer_copies(kind, flags, src_refs, land_refs, send_sems, recv_sems, False):
            cp.start()
        token[...] = jnp.zeros_like(token)

    outs = pl.pallas_call(
        body, name=name,
        in_specs=[HBM] * (2 * n) + [ANY] * len(after),
        out_specs=(SEM, SEM) + (HBM,) * (2 * n) + (pl.BlockSpec(memory_space=pltpu.VMEM),),
        out_shape=(pltpu.SemaphoreType.DMA((ns,)), pltpu.SemaphoreType.DMA((ns,)))
        + tuple(pltpu.HBM(a.shape, a.dtype) for a in list(srcs) + lands) + (TOKEN,),
        input_output_aliases={i: 2 + i for i in range(2 * n)},
        compiler_params=pltpu.CompilerParams(has_side_effects=pltpu.SideEffectType.DATAFLOW_SIDE_EFFECTING),
    )(*[pltpu.with_memory_space_constraint(a, pltpu.HBM) for a in list(srcs) + lands], *after)
    return dict(sems=outs[:2], srcs=outs[2:2 + n], lands=outs[2 + n:2 + 2 * n], token=outs[-1], kind=kind, flags=flags)


def _copies_wait(started, after, name):
    n = len(started["srcs"])
    kind, flags = started["kind"], started["flags"]
    after = list(after) if isinstance(after, (list, tuple)) else [after]

    def body(*refs):
        src_refs, land_refs = refs[:n], refs[n:2 * n]
        send_sems, recv_sems = refs[2 * n], refs[2 * n + 1]
        for cp in _peer_copies(kind, flags, src_refs, land_refs, send_sems, recv_sems, True):
            cp.wait_send()
            cp.wait_recv()

    outs = pl.pallas_call(
        body, name=name,
        in_specs=[HBM] * (2 * n) + [SEM, SEM] + [ANY] * len(after),
        out_specs=(HBM,) * (2 * n),
        out_shape=tuple(pltpu.HBM(a.shape, a.dtype) for a in list(started["srcs"]) + list(started["lands"])),
        input_output_aliases={i: i for i in range(2 * n)},
        compiler_params=pltpu.CompilerParams(has_side_effects=pltpu.SideEffectType.DATAFLOW_SIDE_EFFECTING),
    )(*started["srcs"], *started["lands"], *started["sems"], *after)
    return outs[:n], outs[n:]


def _swap_whole(arrs, name):
    n = len(arrs)

    def body(*refs):
        ins, outs = refs[:n], refs[n:2 * n]
        send_sems, recv_sems = refs[2 * n:]
        x, y, c = _mesh_pos()
        cps = [pltpu.make_async_remote_copy(src_ref=ins[i], dst_ref=outs[i], send_sem=send_sems.at[i],
                                            recv_sem=recv_sems.at[i], device_id=(x, y, 1 - c), device_id_type=MESH)
               for i in range(n)]
        for cp in cps:
            cp.start()
        for cp in cps:
            cp.wait()

    return pl.pallas_call(
        body, name=name, in_specs=[ANY] * n, out_specs=[ANY] * n,
        out_shape=[jax.ShapeDtypeStruct(a.shape, a.dtype) for a in arrs],
        scratch_shapes=[pltpu.SemaphoreType.DMA((n,)), pltpu.SemaphoreType.DMA((n,))],
    )(*arrs)


def _as_rows(a, lead):
    shp = a.shape
    rows = 1
    for s in shp[lead:-1]:
        rows *= s
    return a.reshape(shp[:lead] + (rows, shp[-1]))


def _row_tile(rows, cols, n_bufs):
    budget = (24 * 1024 * 1024) // (n_bufs * 2 * 4 * cols)
    return _pick(rows, max(2 * SUBLANES, budget), 2 * SUBLANES)


def _sum_devices(own, got, dev, name):
    n, rows, cols = got.shape
    tr = _row_tile(rows, cols, n + 2)

    def body(dev_ref, own_ref, got_ref, o_ref):
        mine = own_ref[...]
        acc = jnp.where(dev_ref[0] == 0, mine, got_ref[0])
        for k in range(1, n):
            acc = acc + jnp.where(dev_ref[0] == k, mine, got_ref[k])
        o_ref[...] = acc

    return pl.pallas_call(
        body, name=name,
        grid_spec=pltpu.PrefetchScalarGridSpec(
            num_scalar_prefetch=1, grid=(rows // tr,),
            in_specs=[pl.BlockSpec((tr, cols), lambda i, d: (i, 0)), pl.BlockSpec((n, tr, cols), lambda i, d: (0, i, 0))],
            out_specs=pl.BlockSpec((tr, cols), lambda i, d: (i, 0))),
        out_shape=jax.ShapeDtypeStruct((rows, cols), F32), compiler_params=_cp("parallel"),
    )(_scalar(dev), own, got)


def _scalar(i):
    return jnp.reshape(i, (1,)).astype(jnp.int32)


def _add_own_half(g, other, c, half_first, name):
    _, rows, cols = other.shape
    tr = _row_tile(rows, cols, 3)

    def body(c_ref, g_ref, o_ref, out_ref):
        out_ref[0] = (g_ref[0, 0] + o_ref[0]).astype(out_ref.dtype)

    if half_first:
        g_map = lambda k, i, c_ref: (c_ref[0], k, i, 0)
    else:
        g_map = lambda k, i, c_ref: (k, c_ref[0], i, 0)
    flat = pl.BlockSpec((1, tr, cols), lambda k, i, c_ref: (k, i, 0))
    return pl.pallas_call(
        body, name=name,
        grid_spec=pltpu.PrefetchScalarGridSpec(
            num_scalar_prefetch=1, grid=(N_CHIPS, rows // tr),
            in_specs=[pl.BlockSpec((1, 1, tr, cols), g_map), flat], out_specs=flat),
        out_shape=jax.ShapeDtypeStruct(other.shape, COMM_DTYPE), compiler_params=_cp("parallel", "parallel"),
    )(_scalar(c), g, other)


def _sum_chips(own, got, chip, name, transpose=False):
    _, rows, cols = own.shape
    tr = rows if transpose else _row_tile(rows, cols, N_CHIPS + 2)

    def body(chip_ref, p_ref, b_ref, o_ref):
        mine = p_ref[0].astype(F32)
        acc = jnp.where(chip_ref[0] == 0, mine, b_ref[0].astype(F32))
        for k in range(1, N_CHIPS):
            acc = acc + jnp.where(chip_ref[0] == k, mine, b_ref[k].astype(F32))
        o_ref[...] = acc.T if transpose else acc

    if transpose:
        out_spec, out_shape = pl.BlockSpec((cols, rows), lambda i, k_ref: (0, 0)), (cols, rows)
    else:
        out_spec, out_shape = pl.BlockSpec((tr, cols), lambda i, k_ref: (i, 0)), (rows, cols)
    return pl.pallas_call(
        body, name=name,
        grid_spec=pltpu.PrefetchScalarGridSpec(
            num_scalar_prefetch=1, grid=(rows // tr,),
            in_specs=[pl.BlockSpec((1, tr, cols), lambda i, k_ref: (k_ref[0], i, 0)),
                      pl.BlockSpec((N_CHIPS, tr, cols), lambda i, k_ref: (0, i, 0))],
            out_specs=out_spec),
        out_shape=jax.ShapeDtypeStruct(out_shape, F32), compiler_params=_cp("parallel"),
    )(_scalar(chip), own, got)


def _adam_math(w, g, m, v):
    nm = ADAM_B1 * m + (1.0 - ADAM_B1) * g
    nv = ADAM_B2 * v + (1.0 - ADAM_B2) * (g * g)
    m_hat = nm / (1.0 - ADAM_B1 ** ADAM_STEP)
    v_hat = nv / (1.0 - ADAM_B2 ** ADAM_STEP)
    return -ADAM_LR * (m_hat / (jnp.sqrt(v_hat) + ADAM_EPS) + ADAM_WD * w), nm, nv


def _adamw_pieces(w, m, v, mine, theirs, c, kind, name):
    shape = w.shape
    P = len(mine)
    ws, ms, vs = (t.reshape((P, -1, t.shape[-1])) for t in (w, m, v))
    _, R, C = ws.shape
    if kind == "rows":
        tr = _pick(R // 2, 512, SUBLANES)
    else:
        tr = _pick(R, 256 if kind in ("lo", "hi") else 512, SUBLANES)
    nt = R // tr
    nh = nt // 2

    def body(c_ref, w_ref, m_ref, v_ref, *refs):
        mine_refs, theirs_refs = refs[:P], refs[P:2 * P]
        g_ref, d_ref, nm_ref, nv_ref = refs[2 * P:]
        p, i, core = pl.program_id(0), pl.program_id(1), c_ref[0]

        def pick(refs_):
            out = refs_[0][...]
            for q in range(1, P):
                out = jnp.where(p == q, refs_[q][...], out)
            return out

        a, b = pick(mine_refs), pick(theirs_refs)
        if kind == "cols":
            gv = jnp.where(core == 0, jnp.concatenate([a, b], axis=1), jnp.concatenate([b, a], axis=1))
        else:
            own = {"lo": core == 0, "hi": core == 1, "rows": (i >= nh) == (core == 1)}[kind]
            gv = jnp.where(own, a, b)
        g_ref[0] = gv
        d_ref[0], nm_ref[0], nv_ref[0] = _adam_math(w_ref[0], gv, m_ref[0], v_ref[0])

    def piece_spec(q):
        tile = (lambda i: i - jnp.where(i >= nh, nh, 0)) if kind == "rows" else (lambda i: i)
        return pl.BlockSpec((tr, mine[q].shape[1]), lambda p, i, c_ref: (jnp.where(p == q, tile(i), 0), 0))

    full = pl.BlockSpec((1, tr, C), lambda p, i, c_ref: (p, i, 0))
    outs = pl.pallas_call(
        body, name=name,
        grid_spec=pltpu.PrefetchScalarGridSpec(num_scalar_prefetch=1, grid=(P, nt),
                                               in_specs=[full] * 3 + [piece_spec(q) for q in range(P)] * 2,
                                               out_specs=[full] * 4),
        out_shape=[jax.ShapeDtypeStruct((P, R, C), F32)] * 4, compiler_params=_cp("parallel", "arbitrary"),
    )(_scalar(c), ws, ms, vs, *mine, *theirs)
    return tuple(o.reshape(shape) for o in outs)


def _adamw(w, g, m, v, name):
    shape = w.shape
    ws, gs, ms, vs = (_as_rows(t, 0) for t in (w, g, m, v))
    rows, cols = ws.shape
    tr = _row_tile(rows, cols, 7)

    def body(w_ref, g_ref, m_ref, v_ref, d_ref, nm_ref, nv_ref):
        d_ref[...], nm_ref[...], nv_ref[...] = _adam_math(w_ref[...], g_ref[...], m_ref[...], v_ref[...])

    spec = pl.BlockSpec((tr, cols), lambda i: (i, 0))
    outs = pl.pallas_call(body, name=name, grid=(rows // tr,), in_specs=[spec] * 4, out_specs=[spec] * 3,
                          out_shape=[jax.ShapeDtypeStruct((rows, cols), F32)] * 3, compiler_params=_cp("parallel"))(ws, gs, ms, vs)
    return tuple(o.reshape(shape) for o in outs)


_BIG = ["ffn_w_gate", "ffn_w_up", "ffn_w_down", "dn_w_in", "dn_w_out", "sg_w_in", "sg_w_out"]
_SMALL_SHARDED = ["norm_g", "dn_conv_w", "sg_b_in", "sg_ln_g", "sg_ln_b"]
_SMALL_REPL = ["dn_a_log", "dn_dt_bias", "dn_norm_g", "sg_w_s", "sg_b_s"]
_WEIGHTS = ["norm_g", "ffn_w_gate", "ffn_w_up", "ffn_w_down", "dn_w_in", "dn_conv_w", "dn_a_log", "dn_dt_bias",
            "dn_norm_g", "dn_w_out", "sg_w_in", "sg_b_in", "sg_ln_g", "sg_ln_b", "sg_w_s", "sg_b_s", "sg_w_out"]
PACK_COLS = 1024


def _pack(arrs):
    flat = jnp.concatenate([a.reshape(-1) for a in arrs])
    pad = (-flat.shape[0]) % (SUBLANES * PACK_COLS)
    return jnp.pad(flat, (0, pad)).reshape(-1, PACK_COLS)


def _unpack(buf, shapes):
    flat = buf.reshape(-1)
    out, off = [], 0
    for s in shapes:
        n = math.prod(s)
        out.append(flat[off:off + n].reshape(s))
        off += n
    return out


def _as_halves(a):
    if a.shape[0] == 2:
        return a
    if a.shape[0] == 1:
        return a.reshape((2, a.shape[1] // 2) + a.shape[2:])
    return a.reshape((2, a.shape[0] // 2) + a.shape[1:])


def _with_own(gathered, own, chip):
    g = gathered.reshape((N_CHIPS,) + own.shape)
    return [jnp.where(chip == k, own, g[k]) for k in range(N_CHIPS)]


def _cat_shards(g, axis):
    return jnp.concatenate(list(g), axis=axis)


_GROUP_ORDER = ["ffn00", "dn", "ffn01", "ffn10", "sg", "ffn11"]


def _weight_groups(w):
    cast = {k: _mx(w[k]) for k in _BIG}
    groups = {"ffn%d%d" % (i, j): [cast["ffn_w_gate"][i, j].T, cast["ffn_w_up"][i, j].T, cast["ffn_w_down"][i, j]]
              for i, j in [(0, 0), (0, 1), (1, 0), (1, 1)]}
    groups["dn"] = [cast["dn_w_in"][0], cast["dn_w_out"][0]]
    groups["sg"] = [cast["sg_w_in"][0], cast["sg_w_out"][0]]
    return groups


def _ffn_weights(chip, own, gathered):
    pairs = [(a, g.reshape((N_CHIPS,) + a.shape)) for a, g in zip(own, gathered)]
    return {"chip": chip, "gate": pairs[0], "up": pairs[1], "down": pairs[2]}


def _group_matrices(group, shards):
    if group == "sg":
        return {"sg_win": _cat_shards(shards[0], 1), "sg_wout": _cat_shards(shards[1], 0)}
    dn_full = _cat_shards(shards[0], 1)
    W4 = 4 * DN_HEADS * DN_HEAD_DIM
    wba = jnp.zeros((D_MODEL, 2 * LANES), dn_full.dtype)
    wba = wba.at[:, :DN_HEADS].set(dn_full[:, W4:W4 + DN_HEADS])
    wba = wba.at[:, LANES:LANES + DN_HEADS].set(dn_full[:, W4 + DN_HEADS:])
    return {"dn_wqkvz": dn_full[:, :W4], "dn_wba": wba, "dn_wout": _cat_shards(shards[1], 0)}


def _split_cols(a, n):
    w = a.shape[-1] // n
    return [a[..., k * w:(k + 1) * w] for k in range(n)]


def _split_rows(a, n):
    h = a.shape[-2] // n
    return [a[..., k * h:(k + 1) * h, :] for k in range(n)]


_IJ = [(0, 0), (0, 1), (1, 0), (1, 1)]


def _group_grads(group, grads):
    def rows_by_chip(a):
        return a.reshape(N_CHIPS, 2, a.shape[0] // (2 * N_CHIPS), a.shape[1])

    if group.startswith("ffn"):
        tag = group[3:]
        t = grads["wguT" + tag]
        return (["wguT" + tag, "wd" + tag],
                [t.reshape(2, N_CHIPS, t.shape[0] // (2 * N_CHIPS), t.shape[1]), rows_by_chip(grads["wd" + tag])], [True, False])
    if group == "sg":
        return ["sg_w_inT", "sg_w_out"], [rows_by_chip(grads["sg_w_inT"]), rows_by_chip(grads["sg_w_out"])], [False, False]
    dn_in = jnp.stack([jnp.stack(_split_cols(hf, N_CHIPS)) for hf in _split_rows(grads["dn_w_in"], 2)])
    return ["dn_w_in", "dn_w_out"], [dn_in, rows_by_chip(grads["dn_w_out"])], [True, False]


_SHARD_PIECES = {
    "ffn_w_gate": (["wguT%d%d" % ij for ij in _IJ], "lo"),
    "ffn_w_up": (["wguT%d%d" % ij for ij in _IJ], "hi"),
    "ffn_w_down": (["wd%d%d" % ij for ij in _IJ], "rows"),
    "dn_w_in": (["dn_w_in"], "rows"),
    "dn_w_out": (["dn_w_out"], "rows"),
    "sg_w_in": (["sg_w_inT"], "cols"),
    "sg_w_out": (["sg_w_out"], "rows"),
}


def kernel(x, norm_g, ffn_w_gate, ffn_w_up, ffn_w_down, dn_w_in, dn_conv_w, dn_a_log, dn_dt_bias, dn_norm_g, dn_w_out, sg_w_in, sg_b_in, sg_ln_g, sg_ln_b, sg_w_s, sg_b_s, sg_w_out, loss_target, m_norm_g, m_ffn_w_gate, m_ffn_w_up, m_ffn_w_down, m_dn_w_in, m_dn_conv_w, m_dn_a_log, m_dn_dt_bias, m_dn_norm_g, m_dn_w_out, m_sg_w_in, m_sg_b_in, m_sg_ln_g, m_sg_ln_b, m_sg_w_s, m_sg_b_s, m_sg_w_out, v_norm_g, v_ffn_w_gate, v_ffn_w_up, v_ffn_w_down, v_dn_w_in, v_dn_conv_w, v_dn_a_log, v_dn_dt_bias, v_dn_norm_g, v_dn_w_out, v_sg_w_in, v_sg_b_in, v_sg_ln_g, v_sg_ln_b, v_sg_w_s, v_sg_b_s, v_sg_w_out):
    args = dict(locals())
    w = {k: args[k] for k in _WEIGHTS}
    mom = {k: args["m_" + k] for k in _WEIGHTS}
    var = {k: args["v_" + k] for k in _WEIGHTS}
    cx, cy, cc = _mesh_pos()
    chip = 2 * cx + cy

    small_shapes = [w[k].shape for k in _SMALL_SHARDED]
    groups = _weight_groups(w)
    own = groups[_GROUP_ORDER[0]] + [_pack([w[k] for k in _SMALL_SHARDED])]
    first = _allgather_chips([_as_halves(a) for a in own], "gather_first")
    started, after = {}, first[0]
    for g in _GROUP_ORDER[1:]:
        started[g] = _copies_start("gather", groups[g], after, "gather_start_" + g)
        after = started[g]["token"]
    small_k = [_unpack(pack, small_shapes) for pack in _with_own(first[-1], own[-1], chip)]
    p = {name: jnp.concatenate([small_k[k][i] for k in range(N_CHIPS)], axis=-1) for i, name in enumerate(_SMALL_SHARDED)}
    p = {k: (v if k == "norm_g" else v[0]) for k, v in p.items()}
    p["norm_g"] = p["norm_g"] + after[0, 0]
    for k in _SMALL_REPL:
        p[k] = w[k][0]

    def weights_for(group, after):
        if group == _GROUP_ORDER[0]:
            return _ffn_weights(chip, own[:-1], first[:-1])
        srcs, lands = _copies_wait(started[group], after, "gather_wait_" + group)
        if group.startswith("ffn"):
            return _ffn_weights(chip, srcs, lands)
        return _group_matrices(group, [_with_own(l, a, chip) for l, a in zip(lands, srcs)])

    mine, theirs, to_core, to_chips = {}, {}, [], []

    def send_to_chips(after):
        group, names, flags, swap = to_core.pop(0)
        halves, got = _copies_wait(swap, after, "swap_wait_" + group)
        pair_sum = [_add_own_half(h, o, cc, hf, "pair_sum_" + n) for n, h, o, hf in zip(names, halves, got, flags)]
        scatter = _copies_start("scatter", pair_sum, got[0], "reduce_start_" + group)
        to_chips.append((group, names, scatter))
        return scatter["token"]

    def finish(after):
        group, names, scatter = to_chips.pop(0)
        pair_sum, got = _copies_wait(scatter, after, "reduce_wait_" + group)
        half_sum = [_sum_chips(a, b, chip, "chip_sum_" + n, transpose=n == "sg_w_inT")
                    for n, a, b in zip(names, pair_sum, got)]
        other = _swap_whole(half_sum, "gather_core_pair_" + group)
        mine.update(zip(names, half_sum))
        theirs.update(zip(names, other))

    def grads_ready(group, grads):
        names, halves, flags = _group_grads(group, grads)
        swap = _copies_start("swap", halves, None, "swap_start_" + group, flags)
        token = swap["token"]
        if to_core:
            token = send_to_chips(token)
            if len(to_chips) > 1:
                finish(token)
        to_core.append((group, names, flags, swap))
        return token[0, 0]

    small_names = _SMALL_SHARDED + _SMALL_REPL
    small = {}

    def small_ready(grads, loss_part):
        parts = [grads[k] for k in small_names]
        small["shapes"] = [g.shape for g in parts] + [(1,)]
        pack = _pack(parts + [loss_part[0, :1]])
        small["exchange"] = _copies_start("all", [pack], None, "small_start")
        return small["exchange"]["token"]

    loss_part, grad_x, grads = _local_step(x[0], loss_target[0], p, weights_for, grads_ready, small_ready)
    token = send_to_chips(to_core[0][3]["token"])
    finish(token)
    (pack,), (packs,) = _copies_wait(small["exchange"], list(theirs.values()), "small_wait")
    summed = _sum_devices(pack, packs, 4 * cx + 2 * cy + cc, "small_sum")
    parts = _unpack(summed, small["shapes"])
    loss = parts[-1][0]
    grad = {}
    for i, k in enumerate(small_names):
        g = parts[i]
        if k in _SMALL_SHARDED:
            n = w[k].shape[-1]
            g = lax.dynamic_slice_in_dim(g, chip * n, n, axis=g.ndim - 1)
        grad[k] = g

    delta, new_m, new_v = {}, {}, {}

    def update(keys):
        for k in keys:
            names, kind = _SHARD_PIECES[k]
            turn = (lambda a: jnp.swapaxes(a, -1, -2)) if names[0].startswith("wguT") else (lambda a: a)
            outs = _adamw_pieces(turn(w[k]), turn(mom[k]), turn(var[k]), [mine[n] for n in names], [theirs[n] for n in names],
                                 cc, kind, "adamw_" + k)
            grad[k], delta[k], new_m[k], new_v[k] = (turn(o) for o in outs)

    shapes = [w[k].shape for k in small_names]
    d, nm, nv = _adamw(_pack([w[k] for k in small_names]), _pack([grad[k] for k in small_names]),
                       _pack([mom[k] for k in small_names]), _pack([var[k] for k in small_names]), "adamw_small")
    for k, a, b, c_ in zip(small_names, _unpack(d, shapes), _unpack(nm, shapes), _unpack(nv, shapes)):
        delta[k], new_m[k], new_v[k] = a, b, c_
    mixers = [k for k in _BIG if not k.startswith("ffn")]
    update(mixers)
    finish([d] + [delta[k] for k in mixers] + list(theirs.values()))
    update([k for k in _BIG if k.startswith("ffn")])

    return (loss, grad_x[None], *[grad[k] for k in _WEIGHTS], *[delta[k] for k in _WEIGHTS],
            *[new_m[k] for k in _WEIGHTS], *[new_v[k] for k in _WEIGHTS])
```

```python
import functools
import math

import jax
import jax.numpy as jnp
from jax import lax
from jax.experimental import pallas as pl
from jax.experimental.pallas import tpu as pltpu

F32 = jnp.float32
MXU_DTYPE = jnp.bfloat16
COMM_DTYPE = jnp.bfloat16
HI = lax.Precision.HIGHEST
TRI_PREC = lax.Precision.HIGH

D_MODEL = 1024
D_FF = 2816
RMS_EPS = 1e-6
LN_EPS = 1e-5
L2_EPS = 1e-6
DN_HEADS = 8
DN_HEAD_DIM = 128
DN_CONV = 4
DN_CHUNK = 64
SG_WIDTH = 2048
SG_GROUPS = 8
SG_CHUNK = 128
SG_GROUP_W = SG_WIDTH // SG_GROUPS
N_CHIPS = 4
N_DEV = 8
LANES = 128
SUBLANES = 8
VMEM_LIMIT = 56 * 1024 * 1024

ADAM_LR = 0.001
ADAM_B1 = 0.9
ADAM_B2 = 0.999
ADAM_EPS = 1e-08
ADAM_WD = 0.01
ADAM_STEP = 10

MESH = pl.DeviceIdType.MESH
ANY = pl.BlockSpec(memory_space=pl.ANY)


def _cp(*sem):
    return pltpu.CompilerParams(dimension_semantics=sem, vmem_limit_bytes=VMEM_LIMIT)


def _pick(n, pref, mult=LANES):
    best = None
    d = mult
    while d <= min(n, pref):
        if n % d == 0:
            best = d
        d += mult
    return best if best is not None else n


def _full(shape):
    nd = len(shape)
    return pl.BlockSpec(shape, lambda *_: (0,) * nd)


def _sigmoid(x):
    return 1.0 / (1.0 + jnp.exp(-x))


def _dot(a, b, dims, prec=None):
    return lax.dot_general(a, b, (dims, ((), ())), preferred_element_type=F32, precision=prec)


NN = ((1,), (0,))
NT = ((1,), (1,))
TN = ((0,), (0,))


def _mx(a):
    return a.astype(MXU_DTYPE)


def _rms_stat(x):
    return lax.rsqrt(jnp.mean(x * x, axis=-1, keepdims=True) + RMS_EPS)


def _rms_bwd(x, r, g, dy):
    xh = x * r
    dxh = dy * g
    dx = r * (dxh - xh * jnp.mean(dxh * xh, axis=-1, keepdims=True))
    return dx, jnp.sum(dy * xh, axis=0, keepdims=True)


def _mm(a, b, mode, name, out_dtype=F32, add=None, after=None):
    if mode == "tn":
        K, M = a.shape
        N = b.shape[1]
    elif mode == "nt":
        M, K = a.shape
        N = b.shape[0]
    else:
        M, K = a.shape
        N = b.shape[1]
    tn = _pick(N, 1024)
    if mode == "tn":
        tm = _pick(M, 1024 if tn <= 512 else 1408)
        tk = _pick(K, 1024, SUBLANES)
    else:
        tm = _pick(M, max(512, min(2048, (1024 * 1024) // tn)), SUBLANES)
        tk = _pick(K, 2048)
    nk = K // tk
    grid = (N // tn, M // tm, nk)
    if mode == "nn":
        a_spec = pl.BlockSpec((tm, tk), lambda j, i, k: (i, k))
        b_spec = pl.BlockSpec((tk, tn), lambda j, i, k: (k, j))
        dims = NN
    elif mode == "nt":
        a_spec = pl.BlockSpec((tm, tk), lambda j, i, k: (i, k))
        b_spec = pl.BlockSpec((tn, tk), lambda j, i, k: (j, k))
        dims = NT
    else:
        a_spec = pl.BlockSpec((tk, tm), lambda j, i, k: (k, i))
        b_spec = pl.BlockSpec((tk, tn), lambda j, i, k: (k, j))
        dims = TN
    o_spec = pl.BlockSpec((tm, tn), lambda j, i, k: (i, j))
    has_add = add is not None

    def body(*refs):
        a_ref, b_ref = refs[:2]
        add_ref = refs[2] if has_add else None
        o_ref, acc = refs[-2:]
        k = pl.program_id(2)

        @pl.when(k == 0)
        def _():
            acc[...] = add_ref[...] if has_add else jnp.zeros_like(acc)

        acc[...] += _dot(a_ref[...], b_ref[...], dims)

        @pl.when(k == nk - 1)
        def _():
            o_ref[...] = acc[...].astype(o_ref.dtype)

    ins = [a, b] + ([add] if has_add else []) + ([after] if after is not None else [])
    specs = [a_spec, b_spec] + ([o_spec] if has_add else []) + ([ANY] if after is not None else [])
    return pl.pallas_call(
        body, name=name, grid=grid, in_specs=specs, out_specs=o_spec,
        out_shape=jax.ShapeDtypeStruct((M, N), out_dtype),
        scratch_shapes=[pltpu.VMEM((tm, tn), F32)],
        compiler_params=_cp("parallel", "parallel", "arbitrary"),
    )(*ins)


def _ffn_weight_operands(wt):
    return [_scalar(wt["chip"])] , [wt["gate"][0], wt["gate"][1], wt["up"][0], wt["up"][1], wt["down"][0], wt["down"][1]]


def _load_ffn_weights(chip_ref, shard_refs, wgu_v, wd_v, sem):
    fs = wd_v.shape[0] // N_CHIPS

    @pl.when(pl.program_id(0) == 0)
    def _():
        me = chip_ref[0]
        waits = []
        for t, (dst, base) in enumerate([(wgu_v, 0), (wgu_v, wd_v.shape[0]), (wd_v, 0)]):
            own, gathered = shard_refs[2 * t], shard_refs[2 * t + 1]
            for k in range(N_CHIPS):
                slot = dst.at[pl.ds(base + k * fs, fs), :]
                s = sem.at[t * N_CHIPS + k]

                @pl.when(me == k)
                def _(own=own, slot=slot, s=s):
                    pltpu.make_async_copy(own, slot, s).start()

                @pl.when(me != k)
                def _(gathered=gathered, k=k, slot=slot, s=s):
                    pltpu.make_async_copy(gathered.at[k], slot, s).start()

                waits.append(pltpu.make_async_copy(own, slot, s))
        for cp in waits:
            cp.wait()


def _ffn_fwd(x, g0, g1, wt, name, next_gain=None, target=None):
    T, D = x.shape
    F = N_CHIPS * wt["down"][0].shape[0]
    F2 = 2 * F
    tm = _pick(T, 256, SUBLANES)
    prefetch, shards = _ffn_weight_operands(wt)
    extra = [a for a in (next_gain, target) if a is not None]
    n_tail = 1 if next_gain is not None else (2 if target is not None else 0)

    def body(chip_ref, x_ref, g0_ref, g1_ref, *refs):
        extra_ref = refs[0] if extra else None
        refs = refs[len(extra):]
        shard_refs = refs[:6]
        xo_ref, h_ref, gu_ref, y_ref = refs[6:10]
        tail_refs = refs[10:10 + n_tail]
        wgu_v, wd_v, sem = refs[10 + n_tail:]
        _load_ffn_weights(chip_ref, shard_refs, wgu_v, wd_v, sem)
        xv = x_ref[...]
        hb = _mx(xv * _rms_stat(xv) * g0_ref[...])
        h_ref[...] = hb
        gu = _dot(hb, wgu_v[...], NT)
        gu_ref[...] = gu.astype(gu_ref.dtype)
        g = gu[:, :F]
        u = gu[:, F:]
        a = _mx(g * _sigmoid(g) * u)
        y = _dot(a, wd_v[...], NN)
        y_ref[...] = y
        xo = xv + 0.5 * (y * _rms_stat(y) * g1_ref[...])
        xo_ref[...] = xo
        if next_gain is not None:
            tail_refs[0][...] = _mx(xo * _rms_stat(xo) * extra_ref[...])
        if target is not None:
            loss_ref, dy_ref = tail_refs

            @pl.when(pl.program_id(0) == 0)
            def _():
                loss_ref[...] = jnp.zeros_like(loss_ref)

            e = xo - extra_ref[...]
            dy_ref[...] = e * (1.0 / D)
            loss_ref[...] += 0.5 * jnp.sum(jnp.mean(e * e, axis=-1, keepdims=True), axis=0, keepdims=True)

    row = lambda w: pl.BlockSpec((tm, w), lambda i, c: (i, 0))
    one = pl.BlockSpec((1, D), lambda i, c: (0, 0))
    tail_specs, tail_shapes, extra_specs = [], [], []
    if next_gain is not None:
        extra_specs, tail_specs, tail_shapes = [one], [row(D)], [jax.ShapeDtypeStruct((T, D), MXU_DTYPE)]
    if target is not None:
        extra_specs = [row(D)]
        tail_specs = [pl.BlockSpec((SUBLANES, LANES), lambda i, c: (0, 0)), row(D)]
        tail_shapes = [jax.ShapeDtypeStruct((SUBLANES, LANES), F32), jax.ShapeDtypeStruct((T, D), F32)]
    return pl.pallas_call(
        body, name=name,
        grid_spec=pltpu.PrefetchScalarGridSpec(
            num_scalar_prefetch=1, grid=(T // tm,),
            in_specs=[row(D), one, one] + extra_specs + [ANY] * 6,
            out_specs=[row(D), row(D), row(F2), row(D)] + tail_specs,
            scratch_shapes=[pltpu.VMEM((F2, D), MXU_DTYPE), pltpu.VMEM((F, D), MXU_DTYPE),
                            pltpu.SemaphoreType.DMA((3 * N_CHIPS,))]),
        out_shape=[jax.ShapeDtypeStruct((T, D), F32), jax.ShapeDtypeStruct((T, D), MXU_DTYPE),
                   jax.ShapeDtypeStruct((T, F2), MXU_DTYPE), jax.ShapeDtypeStruct((T, D), F32)] + tail_shapes,
        compiler_params=_cp("arbitrary"),
    )(*prefetch, x, g0, g1, *extra, *shards)


FFN_BWD_CHUNK = 2816


def _ffn_bwd(dxo, x, y, gu, g0, g1, wt, name):
    T, D = x.shape
    F2 = gu.shape[1]
    F = F2 // 2
    tm = _pick(T, 256, SUBLANES)
    fc = _pick(F, FFN_BWD_CHUNK)
    prefetch, shards = _ffn_weight_operands(wt)

    def body(chip_ref, dxo_ref, x_ref, y_ref, gu_ref, g0_ref, g1_ref, *refs):
        shard_refs = refs[:6]
        dx_ref, dy_ref, a_ref, dgu_ref, dg0_ref, dg1_ref, wgu_v, wd_v, sem = refs[6:]
        _load_ffn_weights(chip_ref, shard_refs, wgu_v, wd_v, sem)

        @pl.when(pl.program_id(0) == 0)
        def _():
            dg0_ref[...] = jnp.zeros_like(dg0_ref)
            dg1_ref[...] = jnp.zeros_like(dg1_ref)

        dxo_v = dxo_ref[...]
        yv = y_ref[...]
        dy, dg1 = _rms_bwd(yv, _rms_stat(yv), g1_ref[...], 0.5 * dxo_v)
        dg1_ref[...] += dg1
        dyb = _mx(dy)
        dy_ref[...] = dyb
        dh = jnp.zeros((tm, D), F32)
        for c in range(F // fc):
            lo, hi = c * fc, (c + 1) * fc
            da = _dot(dyb, wd_v[lo:hi, :], NT)
            g = gu_ref[:, lo:hi].astype(F32)
            u = gu_ref[:, F + lo:F + hi].astype(F32)
            s = _sigmoid(g)
            sg = g * s
            a_ref[:, lo:hi] = _mx(sg * u)
            dg = _mx(da * u * (s * (1.0 + g * (1.0 - s))))
            du = _mx(da * sg)
            dgu_ref[:, lo:hi] = dg
            dgu_ref[:, F + lo:F + hi] = du
            dh = dh + _dot(dg, wgu_v[lo:hi, :], NN) + _dot(du, wgu_v[F + lo:F + hi, :], NN)
        xv = x_ref[...]
        dx, dg0 = _rms_bwd(xv, _rms_stat(xv), g0_ref[...], dh)
        dg0_ref[...] += dg0
        dx_ref[...] = dxo_v + dx

    row = lambda w: pl.BlockSpec((tm, w), lambda i, c: (i, 0))
    one = pl.BlockSpec((1, D), lambda i, c: (0, 0))
    return pl.pallas_call(
        body, name=name,
        grid_spec=pltpu.PrefetchScalarGridSpec(
            num_scalar_prefetch=1, grid=(T // tm,),
            in_specs=[row(D), row(D), row(D), row(F2), one, one] + [ANY] * 6,
            out_specs=[row(D), row(D), row(F), row(F2), one, one],
            scratch_shapes=[pltpu.VMEM((F2, D), MXU_DTYPE), pltpu.VMEM((F, D), MXU_DTYPE),
                            pltpu.SemaphoreType.DMA((3 * N_CHIPS,))]),
        out_shape=[jax.ShapeDtypeStruct((T, D), F32), jax.ShapeDtypeStruct((T, D), MXU_DTYPE),
                   jax.ShapeDtypeStruct((T, F), MXU_DTYPE), jax.ShapeDtypeStruct((T, F2), MXU_DTYPE),
                   jax.ShapeDtypeStruct((1, D), F32), jax.ShapeDtypeStruct((1, D), F32)],
        compiler_params=_cp("arbitrary"),
    )(*prefetch, dxo, x, y, gu, g0, g1, *shards)


def _out_proj_postnorm(a, b, x, g, name):
    T, K = a.shape
    D = b.shape[1]
    tm = _pick(T, 512, SUBLANES)

    def body(a_ref, b_ref, x_ref, g_ref, m_ref, o_ref):
        mv = _dot(a_ref[...], b_ref[...], NN)
        m_ref[...] = mv
        o_ref[...] = x_ref[...] + mv * _rms_stat(mv) * g_ref[...]

    row = lambda w: pl.BlockSpec((tm, w), lambda i: (i, 0))
    return pl.pallas_call(body, name=name, grid=(T // tm,),
                          in_specs=[row(K), _full((K, D)), row(D), _full((1, D))], out_specs=[row(D), row(D)],
                          out_shape=[jax.ShapeDtypeStruct((T, D), F32)] * 2, compiler_params=_cp("parallel"))(a, b, x, g)


def _postnorm_bwd_dgrad(dxo, m, g, b, name):
    T, D = m.shape
    K = b.shape[0]
    tm = _pick(T, 512, SUBLANES)

    def body(dxo_ref, m_ref, g_ref, b_ref, dm_ref, dg_ref, da_ref):
        @pl.when(pl.program_id(0) == 0)
        def _():
            dg_ref[...] = jnp.zeros_like(dg_ref)

        mv = m_ref[...]
        dm, dg = _rms_bwd(mv, _rms_stat(mv), g_ref[...], dxo_ref[...])
        dg_ref[...] += dg
        dmb = _mx(dm)
        dm_ref[...] = dmb
        da_ref[...] = _dot(dmb, b_ref[...], NT)

    row = lambda w: pl.BlockSpec((tm, w), lambda i: (i, 0))
    return pl.pallas_call(body, name=name, grid=(T // tm,), in_specs=[row(D), row(D), _full((1, D)), _full((K, D))],
                          out_specs=[row(D), _full((1, D)), row(K)],
                          out_shape=[jax.ShapeDtypeStruct((T, D), MXU_DTYPE), jax.ShapeDtypeStruct((1, D), F32),
                                     jax.ShapeDtypeStruct((T, K), F32)],
                          compiler_params=_cp("arbitrary"))(dxo, m, g, b)


def _dgrad_prenorm_bwd(a, b, add, dxo, x, g, name):
    T, K = a.shape
    D = b.shape[0]
    tm = _pick(T, 512, SUBLANES)
    tk = _pick(K, 2048)
    nk = K // tk
    has_add = add is not None

    def body(*refs):
        a_ref, b_ref = refs[:2]
        add_ref = refs[2] if has_add else None
        dxo_ref, x_ref, g_ref, dx_ref, dg_ref, acc = refs[-6:]
        i, k = pl.program_id(0), pl.program_id(1)

        @pl.when((i == 0) & (k == 0))
        def _():
            dg_ref[...] = jnp.zeros_like(dg_ref)

        @pl.when(k == 0)
        def _():
            acc[...] = add_ref[...] if has_add else jnp.zeros_like(acc)

        acc[...] += _dot(a_ref[...], b_ref[...], NT)

        @pl.when(k == nk - 1)
        def _():
            xv = x_ref[...]
            dx, dg = _rms_bwd(xv, _rms_stat(xv), g_ref[...], acc[...])
            dg_ref[...] += dg
            dx_ref[...] = dxo_ref[...] + dx

    row = pl.BlockSpec((tm, D), lambda i, k: (i, 0))
    one = pl.BlockSpec((1, D), lambda i, k: (0, 0))
    ins = [a, b] + ([add] if has_add else []) + [dxo, x, g]
    specs = ([pl.BlockSpec((tm, tk), lambda i, k: (i, k)), pl.BlockSpec((D, tk), lambda i, k: (0, k))]
             + ([row] if has_add else []) + [row, row, one])
    return pl.pallas_call(body, name=name, grid=(T // tm, nk), in_specs=specs, out_specs=[row, one],
                          out_shape=[jax.ShapeDtypeStruct((T, D), F32), jax.ShapeDtypeStruct((1, D), F32)],
                          scratch_shapes=[pltpu.VMEM((tm, D), F32)],
                          compiler_params=_cp("arbitrary", "arbitrary"))(*ins)


DN_ROWS = 512


def _shift_down(prev8, cur, s):
    n = cur.shape[0]
    xx = jnp.concatenate([prev8, cur], axis=0)
    return pltpu.roll(xx, s, 0)[SUBLANES:SUBLANES + n, :]


def _shift_up(cur, next8, s):
    n = cur.shape[0]
    xx = jnp.concatenate([cur, next8], axis=0)
    return pltpu.roll(xx, n + SUBLANES - s, 0)[:n, :]


def _tile_start(r, rows):
    return r * rows if isinstance(r, int) else pl.multiple_of(r * rows, SUBLANES)


def _conv_tile(x_ref, w, r, rows):
    start = _tile_start(r, rows)
    cur = x_ref[pl.ds(start, rows), :]
    if isinstance(r, int):
        prev8 = jnp.zeros((SUBLANES, cur.shape[1]), cur.dtype)
        taps = [_shift_down(prev8, cur, DN_CONV - 1 - j) if j < DN_CONV - 1 else cur for j in range(DN_CONV)]
    else:
        taps = [x_ref[pl.ds(start - (DN_CONV - 1 - j), rows), :] if j < DN_CONV - 1 else cur for j in range(DN_CONV)]
    c = taps[0] * w[0:1, :]
    for j in range(1, DN_CONV):
        c = c + taps[j] * w[j:j + 1, :]
    return c, taps


def _dn_prep_fwd(proj, conv_w, name):
    T = proj.shape[0]
    W = DN_HEADS * DN_HEAD_DIM
    rows = min(DN_ROWS, T)
    n_inner = T // rows
    scale = DN_HEAD_DIM ** -0.5

    def body(x_ref, w_ref, o_ref):
        cb = pl.program_id(0)
        w = w_ref[...]
        is_qk = cb < 2 * DN_HEADS
        post = jnp.where(cb < DN_HEADS, scale, 1.0)

        def step(r, carry):
            c, _ = _conv_tile(x_ref, w, r, rows)
            s = c * _sigmoid(c)
            rinv = lax.rsqrt(jnp.sum(s * s, axis=-1, keepdims=True) + L2_EPS)
            o_ref[pl.ds(_tile_start(r, rows), rows), :] = jnp.where(is_qk, s * rinv * post, s)
            return carry

        step(0, 0)
        lax.fori_loop(1, n_inner, step, 0)

    col = pl.BlockSpec((T, LANES), lambda j: (0, j))
    return pl.pallas_call(body, name=name, grid=(3 * W // LANES,),
                          in_specs=[col, pl.BlockSpec((DN_CONV, LANES), lambda j: (0, j))], out_specs=col,
                          out_shape=jax.ShapeDtypeStruct((T, 3 * W), F32), compiler_params=_cp("parallel"))(proj, conv_w)


def _dn_prep_bwd(proj, conv_w, dqkv, name):
    T = proj.shape[0]
    W = DN_HEADS * DN_HEAD_DIM
    rows = min(DN_ROWS, T)
    n_inner = T // rows
    scale = DN_HEAD_DIM ** -0.5

    def body(x_ref, w_ref, dy_ref, dx_ref, dw_ref, dc_scr):
        cb = pl.program_id(0)
        w = w_ref[...]
        is_qk = cb < 2 * DN_HEADS
        post = jnp.where(cb < DN_HEADS, scale, 1.0)

        def step1(r, dws):
            c, taps = _conv_tile(x_ref, w, r, rows)
            sg = _sigmoid(c)
            s = c * sg
            rinv = lax.rsqrt(jnp.sum(s * s, axis=-1, keepdims=True) + L2_EPS)
            dy = dy_ref[pl.ds(_tile_start(r, rows), rows), :]
            yn = s * rinv
            dyn = dy * post
            ds_qk = rinv * (dyn - yn * jnp.sum(dyn * yn, axis=-1, keepdims=True))
            ds = jnp.where(is_qk, ds_qk, dy)
            dc = ds * (sg * (1.0 + c * (1.0 - sg)))
            dc_scr[pl.ds(_tile_start(r, rows), rows), :] = dc
            return tuple(dws[j] + jnp.sum(dc * taps[j], axis=0, keepdims=True) for j in range(DN_CONV))

        zero = jnp.zeros((1, LANES), F32)
        dws = lax.fori_loop(1, n_inner, step1, step1(0, (zero,) * DN_CONV))
        for j in range(DN_CONV):
            dw_ref[j:j + 1, :] = dws[j]

        def step2(r, carry):
            start = _tile_start(r, rows)
            cur = dc_scr[pl.ds(start, rows), :]
            dx = cur * w[DN_CONV - 1:DN_CONV, :]
            for j in range(DN_CONV - 1):
                s = DN_CONV - 1 - j
                if isinstance(r, int):
                    up = _shift_up(cur, jnp.zeros((SUBLANES, LANES), F32), s)
                else:
                    up = dc_scr[pl.ds(start + s, rows), :]
                dx = dx + up * w[j:j + 1, :]
            dx_ref[pl.ds(start, rows), :] = _mx(dx)
            return carry

        lax.fori_loop(0, n_inner - 1, step2, 0)
        step2(n_inner - 1, 0)

    col = pl.BlockSpec((T, LANES), lambda j: (0, j))
    wspec = pl.BlockSpec((DN_CONV, LANES), lambda j: (0, j))
    return pl.pallas_call(body, name=name, grid=(3 * W // LANES,), in_specs=[col, wspec, col], out_specs=[col, wspec],
                          out_shape=[jax.ShapeDtypeStruct((T, 3 * W), MXU_DTYPE), jax.ShapeDtypeStruct((DN_CONV, 3 * W), F32)],
                          scratch_shapes=[pltpu.VMEM((T, LANES), F32)], compiler_params=_cp("parallel"))(proj, conv_w, dqkv)


def _softplus(x):
    return jnp.maximum(x, 0.0) + jnp.log(1.0 + jnp.exp(-jnp.abs(x)))


def _dn_gate_fwd(ba, a_log, dt_bias, name):
    T = ba.shape[0]
    tm = _pick(T, 1024, SUBLANES)

    def body(ba_ref, al_ref, dt_ref, beta_ref, g_ref):
        beta_ref[...] = _sigmoid(ba_ref[:, :LANES])
        g_ref[...] = -jnp.exp(al_ref[...]) * _softplus(ba_ref[:, LANES:] + dt_ref[...])

    row = lambda w: pl.BlockSpec((tm, w), lambda i: (i, 0))
    return pl.pallas_call(body, name=name, grid=(T // tm,), in_specs=[row(2 * LANES), _full((1, LANES)), _full((1, LANES))],
                          out_specs=[row(LANES), row(LANES)],
                          out_shape=[jax.ShapeDtypeStruct((T, LANES), F32)] * 2, compiler_params=_cp("parallel"))(ba, a_log, dt_bias)


def _dn_gate_bwd(ba, a_log, dt_bias, dbeta, dg, name):
    T = ba.shape[0]
    tm = _pick(T, 1024, SUBLANES)

    def body(ba_ref, al_ref, dt_ref, dbeta_ref, dg_ref, dba_ref, dal_ref, ddt_ref):
        @pl.when(pl.program_id(0) == 0)
        def _():
            dal_ref[...] = jnp.zeros_like(dal_ref)
            ddt_ref[...] = jnp.zeros_like(ddt_ref)

        beta = _sigmoid(ba_ref[:, :LANES])
        dba_ref[:, :LANES] = _mx(dbeta_ref[...] * beta * (1.0 - beta))
        pre = ba_ref[:, LANES:] + dt_ref[...]
        ea = jnp.exp(al_ref[...])
        dgv = dg_ref[...]
        da = dgv * (-ea) * _sigmoid(pre)
        dba_ref[:, LANES:] = _mx(da)
        ddt_ref[...] += jnp.sum(da, axis=0, keepdims=True)
        dal_ref[...] += jnp.sum(dgv * (-ea) * _softplus(pre), axis=0, keepdims=True)

    row = lambda w: pl.BlockSpec((tm, w), lambda i: (i, 0))
    one = _full((1, LANES))
    return pl.pallas_call(body, name=name, grid=(T // tm,), in_specs=[row(2 * LANES), one, one, row(LANES), row(LANES)],
                          out_specs=[row(2 * LANES), one, one],
                          out_shape=[jax.ShapeDtypeStruct((T, 2 * LANES), MXU_DTYPE), jax.ShapeDtypeStruct((1, LANES), F32),
                                     jax.ShapeDtypeStruct((1, LANES), F32)],
                          compiler_params=_cp("arbitrary"))(ba, a_log, dt_bias, dbeta, dg)


def _tri(c, strict):
    i = lax.broadcasted_iota(jnp.int32, (c, c), 0)
    j = lax.broadcasted_iota(jnp.int32, (c, c), 1)
    return (i > j) if strict else (i >= j)


def _inv_unit_lower(ls):
    c = ls[0].shape[0]
    i = lax.broadcasted_iota(jnp.int32, (c, c), 0)
    j = lax.broadcasted_iota(jnp.int32, (c, c), 1)
    eye = jnp.where(i == j, 1.0, 0.0)
    facs = [[eye - l for l in ls]]
    cur = ls
    for _ in range(int(math.log2(c)) - 1):
        cur = [_dot(p, p, NN, TRI_PREC) for p in cur]
        facs.append([eye + p for p in cur])
    while len(facs) > 1:
        nxt = [[_dot(a, b, NN, TRI_PREC) for a, b in zip(facs[t], facs[t + 1])] for t in range(0, len(facs) - 1, 2)]
        if len(facs) % 2:
            nxt.append(facs[-1])
        facs = nxt
    return facs[0]


def _chunk_gates(g_blk):
    c = g_blk.shape[0]
    gcs = _dot(jnp.where(_tri(c, False), 1.0, 0.0), g_blk, NN, HI)
    return gcs, gcs.T


def _head_chunk(h, qh, kh, vh, beta_blk, gcs, gcs_t):
    c = qh.shape[0]
    incl = _tri(c, False)
    gc_col = gcs[:, h:h + 1]
    gc_row = gcs_t[h:h + 1, :]
    gc_last = gcs_t[h:h + 1, c - 1:c]
    dec = jnp.where(incl, jnp.exp(jnp.where(incl, gc_col - gc_row, 0.0)), 0.0)
    gam = jnp.exp(gc_col)
    rr = jnp.exp(gc_last - gc_col)
    gl = jnp.exp(gc_last)
    b = beta_blk[:, h:h + 1]
    kb = kh * b
    vb = vh * b
    both = _dot(jnp.concatenate([_mx(kb), _mx(qh)], axis=0), _mx(kh), NT)
    lmat = jnp.where(_tri(c, True), both[:c] * dec, 0.0)
    pmat = jnp.where(incl, both[c:] * dec, 0.0)
    return dict(dec=dec, gam=gam, rr=rr, gl=gl, b=b, kb=kb, vb=vb, lmat=lmat, pmat=pmat)


def _solve_uw(tinv, q):
    return _dot(tinv, jnp.concatenate([q["vb"], q["kb"] * q["gam"]], axis=1), NN, TRI_PREC)


def _dn_scan_fwd(qkv, beta, g, proj, norm_g, name):
    T = qkv.shape[0]
    C, H, Dh = DN_CHUNK, DN_HEADS, DN_HEAD_DIM
    W = H * Dh
    N = T // C

    def body(q_ref, k_ref, v_ref, beta_ref, g_ref, z_ref, ng_ref, og_ref, o_ref, tinv_ref, s_ref, state):
        @pl.when(pl.program_id(0) == 0)
        def _():
            state[...] = jnp.zeros_like(state)

        gcs, gcs_t = _chunk_gates(g_ref[...])
        beta_blk = beta_ref[...]
        ng = ng_ref[...]
        heads = range(H)
        cs = [slice(h * Dh, (h + 1) * Dh) for h in heads]
        qs = [_head_chunk(h, q_ref[:, cs[h]], k_ref[:, cs[h]], v_ref[:, cs[h]], beta_blk, gcs, gcs_t) for h in heads]
        tinvs = _inv_unit_lower([q["lmat"] for q in qs])
        for h in heads:
            tinv_ref[h] = tinvs[h]
        uws = [_solve_uw(tinvs[h], qs[h]) for h in heads]
        ss = [state[h] for h in heads]
        for h in heads:
            s_ref[0, h] = ss[h]
        sbs = [_mx(s) for s in ss]
        vnbs = [_mx(uws[h][:, :Dh] - _dot(_mx(uws[h][:, Dh:]), sbs[h], NN)) for h in heads]
        os_ = [_dot(jnp.concatenate([_mx(q_ref[:, cs[h]] * qs[h]["gam"]), _mx(qs[h]["pmat"])], axis=1),
                    jnp.concatenate([sbs[h], vnbs[h]], axis=0), NN) for h in heads]
        for h in heads:
            state[h] = ss[h] * qs[h]["gl"] + _dot(_mx((k_ref[:, cs[h]] * qs[h]["rr"]).T), vnbs[h], NN)
        for h in heads:
            o = os_[h]
            o_ref[:, cs[h]] = o
            zh = z_ref[:, cs[h]]
            og_ref[:, cs[h]] = _mx(o * _rms_stat(o) * ng * (zh * _sigmoid(zh)))

    blk = lambda j: pl.BlockSpec((C, W), lambda n: (n, j))
    small = pl.BlockSpec((C, LANES), lambda n: (n, 0))
    return pl.pallas_call(
        body, name=name, grid=(N,),
        in_specs=[blk(0), blk(1), blk(2), small, small, blk(3), _full((1, Dh))],
        out_specs=[blk(0), blk(0), pl.BlockSpec((H, C, C), lambda n: (0, n, 0)),
                   pl.BlockSpec((1, H, Dh, Dh), lambda n: (n, 0, 0, 0))],
        out_shape=[jax.ShapeDtypeStruct((T, W), MXU_DTYPE), jax.ShapeDtypeStruct((T, W), F32),
                   jax.ShapeDtypeStruct((H, T, C), F32), jax.ShapeDtypeStruct((N, H, Dh, Dh), F32)],
        scratch_shapes=[pltpu.VMEM((H, Dh, Dh), F32)],
        compiler_params=_cp("arbitrary"),
    )(qkv, qkv, qkv, beta, g, proj, norm_g)


def _dn_scan_bwd(qkv, beta, g, proj, norm_g, o, tinv, s_all, dog, name):
    T = qkv.shape[0]
    C, H, Dh = DN_CHUNK, DN_HEADS, DN_HEAD_DIM
    W = H * Dh
    N = T // C

    def body(q_ref, k_ref, v_ref, beta_ref, g_ref, z_ref, ng_ref, o_ref, tinv_ref, s_ref, dog_ref,
             dqkv_ref, dbeta_ref, dg_ref, dz_ref, dng_ref, dstate):
        @pl.when(pl.program_id(0) == 0)
        def _():
            dstate[...] = jnp.zeros_like(dstate)
            dng_ref[...] = jnp.zeros_like(dng_ref)

        gcs, gcs_t = _chunk_gates(g_ref[...])
        beta_blk = beta_ref[...]
        ng = ng_ref[...]
        incl = _tri(C, False)
        strict = _tri(C, True)
        lane = lax.broadcasted_iota(jnp.int32, (C, LANES), 1)
        rowi = lax.broadcasted_iota(jnp.int32, (C, 1), 0)
        ones = jnp.ones((C, LANES), F32)
        dbeta_acc = jnp.zeros((C, LANES), F32)
        dgc_acc = jnp.zeros((C, LANES), F32)
        dng_acc = jnp.zeros((1, Dh), F32)
        cs = [slice(h * Dh, (h + 1) * Dh) for h in range(H)]
        rsum = lambda t: jnp.sum(t, axis=1, keepdims=True)
        for heads in (range(0, H // 2), range(H // 2, H)):
            dobs = {}
            for h in heads:
                oh, zh, dogh = o_ref[:, cs[h]], z_ref[:, cs[h]], dog_ref[:, cs[h]]
                rstat = _rms_stat(oh)
                sz = _sigmoid(zh)
                dz_ref[:, cs[h]] = _mx(dogh * (oh * rstat * ng) * (sz * (1.0 + zh * (1.0 - sz))))
                do, dng = _rms_bwd(oh, rstat, ng, dogh * (zh * sz))
                dng_acc = dng_acc + dng
                dobs[h] = _mx(do)
            qs = {h: _head_chunk(h, q_ref[:, cs[h]], k_ref[:, cs[h]], v_ref[:, cs[h]], beta_blk, gcs, gcs_t) for h in heads}
            tms = {h: tinv_ref[h] for h in heads}
            uws = {h: _solve_uw(tms[h], qs[h]) for h in heads}
            ss = {h: s_ref[0, h] for h in heads}
            sbs = {h: _mx(ss[h]) for h in heads}
            wbs = {h: _mx(uws[h][:, Dh:]) for h in heads}
            vnbs = {h: _mx(uws[h][:, :Dh] - _dot(wbs[h], sbs[h], NN)) for h in heads}
            dsns = {h: dstate[h] for h in heads}
            dsbs = {h: _mx(dsns[h]) for h in heads}
            dvnews = {h: _dot(_mx(qs[h]["pmat"]), dobs[h], TN) + _dot(_mx(k_ref[:, cs[h]] * qs[h]["rr"]), dsbs[h], NN)
                      for h in heads}
            dvb16s = {h: _mx(dvnews[h]) for h in heads}
            dps = {h: jnp.where(incl, _dot(dobs[h], vnbs[h], NT), 0.0) for h in heads}
            dqds = {h: _dot(dobs[h], sbs[h], NT) for h in heads}
            dkds = {h: _dot(vnbs[h], dsbs[h], NT) for h in heads}
            dgls = {h: jnp.sum(rsum(ss[h] * dsns[h]), axis=0, keepdims=True) for h in heads}
            dws = {h: -_dot(dvb16s[h], sbs[h], NT) for h in heads}
            for h in heads:
                dstate[h] = qs[h]["gl"] * dsns[h] + _dot(
                    jnp.concatenate([_mx(q_ref[:, cs[h]] * qs[h]["gam"]), -wbs[h]], axis=0),
                    jnp.concatenate([dobs[h], dvb16s[h]], axis=0), TN)
            dsols = {h: _dot(tms[h], jnp.concatenate([dvnews[h], dws[h]], axis=1), TN, TRI_PREC) for h in heads}
            dvbs = {h: dsols[h][:, :Dh] for h in heads}
            dkbgs = {h: dsols[h][:, Dh:] for h in heads}
            dls = {h: jnp.where(strict, -_dot(dsols[h], uws[h], NT, TRI_PREC), 0.0) for h in heads}
            mmats = {h: dls[h] * qs[h]["lmat"] + dps[h] * qs[h]["pmat"] for h in heads}
            dgcs = {h: rsum(mmats[h]) - _dot(mmats[h], ones, TN, TRI_PREC)[:, :1] for h in heads}
            dboth = {h: jnp.concatenate([_mx(dls[h] * qs[h]["dec"]), _mx(dps[h] * qs[h]["dec"])], axis=0) for h in heads}
            for h in heads:
                q = qs[h]
                qh, kh, vh = q_ref[:, cs[h]], k_ref[:, cs[h]], v_ref[:, cs[h]]
                gam, rr, b, kb = q["gam"], q["rr"], q["b"], q["kb"]
                on_k = _dot(dboth[h], _mx(kh), NN)
                dkb = on_k[:C] + dkbgs[h] * gam
                dk = _dot(dboth[h], jnp.concatenate([_mx(kb), _mx(qh)], axis=0), TN) + dkb * b + dkds[h] * rr
                dq = on_k[C:] + dqds[h] * gam
                dgam = rsum(dkbgs[h] * kb) + rsum(dqds[h] * qh)
                dr = rsum(dkds[h] * kh)
                dgc_last = jnp.sum(dr * rr, axis=0, keepdims=True) + dgls[h] * q["gl"]
                dgc = dgcs[h] + dgam * gam - dr * rr + jnp.where(rowi == C - 1, dgc_last, 0.0)
                dbeta = rsum(dvbs[h] * vh) + rsum(dkb * kh)
                dqkv_ref[:, cs[h]] = dq
                dqkv_ref[:, W + h * Dh:W + (h + 1) * Dh] = dk
                dqkv_ref[:, 2 * W + h * Dh:2 * W + (h + 1) * Dh] = dvbs[h] * b
                dbeta_acc = jnp.where(lane == h, dbeta, dbeta_acc)
                dgc_acc = jnp.where(lane == h, dgc, dgc_acc)
        dbeta_ref[...] = dbeta_acc
        dg_ref[...] = _dot(jnp.where(incl, 1.0, 0.0), dgc_acc, TN, HI)
        dng_ref[...] += dng_acc

    rev = lambda n: N - 1 - n
    blk = lambda j: pl.BlockSpec((C, W), lambda n: (rev(n), j))
    small = pl.BlockSpec((C, LANES), lambda n: (rev(n), 0))
    return pl.pallas_call(
        body, name=name, grid=(N,),
        in_specs=[blk(0), blk(1), blk(2), small, small, blk(3), _full((1, Dh)), blk(0),
                  pl.BlockSpec((H, C, C), lambda n: (0, rev(n), 0)),
                  pl.BlockSpec((1, H, Dh, Dh), lambda n: (rev(n), 0, 0, 0)), blk(0)],
        out_specs=[pl.BlockSpec((C, 3 * W), lambda n: (rev(n), 0)), small, small, blk(0), _full((1, Dh))],
        out_shape=[jax.ShapeDtypeStruct((T, 3 * W), F32), jax.ShapeDtypeStruct((T, LANES), F32),
                   jax.ShapeDtypeStruct((T, LANES), F32), jax.ShapeDtypeStruct((T, W), MXU_DTYPE),
                   jax.ShapeDtypeStruct((1, Dh), F32)],
        scratch_shapes=[pltpu.VMEM((H, Dh, Dh), F32)],
        compiler_params=_cp("arbitrary"),
    )(qkv, qkv, qkv, beta, g, proj, norm_g, o, tinv, s_all, dog)


_INV_SQRT2 = 0.7071067811865476
_INV_SQRT_2PI = 0.3989422804014327


def _sg_recompute(zp_ref, bin_ref, lng_ref, lnb_ref):
    E = SG_WIDTH
    zin = zp_ref[...] + bin_ref[...]
    cdf = 0.5 * (1.0 + lax.erf(zin * _INV_SQRT2))
    zz = zin * cdf
    u = zz[:, :E]
    vp = zz[:, E:]
    mu = jnp.mean(vp, axis=-1, keepdims=True)
    xc = vp - mu
    rstd = lax.rsqrt(jnp.mean(xc * xc, axis=-1, keepdims=True) + LN_EPS)
    xhat = xc * rstd
    v = xhat * lng_ref[...] + lnb_ref[...]
    return zin, cdf, u, xhat, rstd, v


def _sg_masked_ws(ws_ref, g):
    return _mx(jnp.where(_tri(SG_CHUNK, False), ws_ref[g], 0.0))


def _sg_fwd(zpre, b_in, ln_g, ln_b, w_s, b_s_t, name):
    T = zpre.shape[0]
    E, G, C, GW = SG_WIDTH, SG_GROUPS, SG_CHUNK, SG_GROUP_W

    def body(zp_ref, bin_ref, lng_ref, lnb_ref, ws_ref, bst_ref, um_ref):
        _, _, u, _, _, v = _sg_recompute(zp_ref, bin_ref, lng_ref, lnb_ref)
        bst = bst_ref[...]
        for g in range(G):
            cs = slice(g * GW, (g + 1) * GW)
            mixed = _dot(_sg_masked_ws(ws_ref, g), _mx(v[:, cs]), NN) + bst[:, g:g + 1]
            um_ref[:, cs] = _mx(u[:, cs] * mixed)

    return pl.pallas_call(
        body, name=name, grid=(T // C,),
        in_specs=[pl.BlockSpec((C, 2 * E), lambda n: (n, 0)), _full((1, 2 * E)), _full((1, E)), _full((1, E)),
                  _full((G, C, C)), _full((C, LANES))],
        out_specs=pl.BlockSpec((C, E), lambda n: (n, 0)),
        out_shape=jax.ShapeDtypeStruct((T, E), MXU_DTYPE), compiler_params=_cp("parallel"),
    )(zpre, b_in, ln_g, ln_b, w_s, b_s_t)


def _sg_bwd(zpre, b_in, ln_g, ln_b, w_s, b_s_t, dum, name):
    T = zpre.shape[0]
    E, G, C, GW = SG_WIDTH, SG_GROUPS, SG_CHUNK, SG_GROUP_W

    def body(zp_ref, bin_ref, lng_ref, lnb_ref, ws_ref, bst_ref, dum_ref,
             dz_ref, dbin_ref, dlng_ref, dlnb_ref, dws_ref, dbst_ref):
        @pl.when(pl.program_id(0) == 0)
        def _():
            for r in (dbin_ref, dlng_ref, dlnb_ref, dws_ref, dbst_ref):
                r[...] = jnp.zeros_like(r)

        zin, cdf, u, xhat, rstd, v = _sg_recompute(zp_ref, bin_ref, lng_ref, lnb_ref)
        bst = bst_ref[...]
        lane = lax.broadcasted_iota(jnp.int32, (C, LANES), 1)
        dum_v = dum_ref[...]
        dbst = jnp.zeros((C, LANES), F32)
        du_parts, dv_parts = [], []
        for g in range(G):
            cs = slice(g * GW, (g + 1) * GW)
            wsm = _sg_masked_ws(ws_ref, g)
            vg = _mx(v[:, cs])
            mixed = _dot(wsm, vg, NN) + bst[:, g:g + 1]
            dumg = dum_v[:, cs]
            du_parts.append(dumg * mixed)
            dmixed = dumg * u[:, cs]
            dmb = _mx(dmixed)
            dv_parts.append(_dot(wsm, dmb, TN))
            dws_ref[g] += _dot(dmb, vg, NT)
            dbst = jnp.where(lane == g, jnp.sum(dmixed, axis=1, keepdims=True), dbst)
        dbst_ref[...] += dbst
        du = jnp.concatenate(du_parts, axis=1)
        dv = jnp.concatenate(dv_parts, axis=1)
        dlng_ref[...] += jnp.sum(dv * xhat, axis=0, keepdims=True)
        dlnb_ref[...] += jnp.sum(dv, axis=0, keepdims=True)
        dxh = dv * lng_ref[...]
        dvp = rstd * (dxh - jnp.mean(dxh, axis=-1, keepdims=True) - xhat * jnp.mean(dxh * xhat, axis=-1, keepdims=True))
        dzz = jnp.concatenate([du, dvp], axis=1)
        dzin = dzz * (cdf + zin * (_INV_SQRT_2PI * jnp.exp(-0.5 * zin * zin)))
        dz_ref[...] = _mx(dzin)
        dbin_ref[...] += jnp.sum(dzin, axis=0, keepdims=True)

    return pl.pallas_call(
        body, name=name, grid=(T // C,),
        in_specs=[pl.BlockSpec((C, 2 * E), lambda n: (n, 0)), _full((1, 2 * E)), _full((1, E)), _full((1, E)),
                  _full((G, C, C)), _full((C, LANES)), pl.BlockSpec((C, E), lambda n: (n, 0))],
        out_specs=[pl.BlockSpec((C, 2 * E), lambda n: (n, 0)), _full((1, 2 * E)), _full((1, E)), _full((1, E)),
                   _full((G, C, C)), _full((C, LANES))],
        out_shape=[jax.ShapeDtypeStruct((T, 2 * E), MXU_DTYPE), jax.ShapeDtypeStruct((1, 2 * E), F32),
                   jax.ShapeDtypeStruct((1, E), F32), jax.ShapeDtypeStruct((1, E), F32),
                   jax.ShapeDtypeStruct((G, C, C), F32), jax.ShapeDtypeStruct((C, LANES), F32)],
        compiler_params=_cp("arbitrary"),
    )(zpre, b_in, ln_g, ln_b, w_s, b_s_t, dum)


def _row(v):
    return v.reshape(1, -1)


def _pad_lanes(v):
    v = v.reshape(1, -1)
    return jnp.pad(v, ((0, 0), (0, LANES - v.shape[1])))


def _local_step(x, target, p, weights_for, grads_ready=None, small_ready=None):
    ng = p["norm_g"]
    grads = {}
    dng = [[None] * 6 for _ in range(2)]
    order = [jnp.zeros((), F32)]

    def tell(group):
        zero = grads_ready(group, grads) if grads_ready is not None else None
        if zero is not None:
            order[0] = zero

    def gain(i, s):
        return _row(ng[i, s]) + order[0]

    def ffn_f(xin, i, j, tag, **tail):
        wt = weights_for("ffn" + tag, xin)
        xo, h, gu, y, *rest = _ffn_fwd(xin, _row(ng[i, 4 * j]), _row(ng[i, 4 * j + 1]), wt, "ffn_fwd_" + tag, **tail)
        return (xo, *rest), (xin, h, gu, y, wt)

    (x1, hn0), sv_f00 = ffn_f(x, 0, 0, "00", next_gain=_row(ng[0, 2]))
    dnw = weights_for("dn", x1)
    proj = _mm(hn0, dnw["dn_wqkvz"], "nn", "dn_proj")
    ba = _mm(hn0, dnw["dn_wba"], "nn", "dn_proj_ba")
    a_log = _pad_lanes(p["dn_a_log"])
    dt_bias = _pad_lanes(p["dn_dt_bias"])
    dn_ng = _row(p["dn_norm_g"])
    qkv = _dn_prep_fwd(proj, p["dn_conv_w"], "dn_prep_fwd")
    beta, gdec = _dn_gate_fwd(ba, a_log, dt_bias, "dn_gate_fwd")
    og, o_raw, tinv, s_all = _dn_scan_fwd(qkv, beta, gdec, proj, dn_ng, "dn_scan_fwd")
    m0, x2 = _out_proj_postnorm(og, dnw["dn_wout"], x1, _row(ng[0, 3]), "dn_out")
    (x3,), sv_f01 = ffn_f(x2, 0, 1, "01")
    (x4, hn1), sv_f10 = ffn_f(x3, 1, 0, "10", next_gain=_row(ng[1, 2]))
    sgw = weights_for("sg", x4)
    zpre = _mm(hn1, sgw["sg_win"], "nn", "sg_proj")
    sg_bin = _row(p["sg_b_in"])
    sg_lng = _row(p["sg_ln_g"])
    sg_lnb = _row(p["sg_ln_b"])
    sg_bst = jnp.pad(p["sg_b_s"].T, ((0, 0), (0, LANES - SG_GROUPS)))
    um = _sg_fwd(zpre, sg_bin, sg_lng, sg_lnb, p["sg_w_s"], sg_bst, "sg_fwd")
    m1, x5 = _out_proj_postnorm(um, sgw["sg_wout"], x4, _row(ng[1, 3]), "sg_out")
    (_, loss_part, dx), sv_f11 = ffn_f(x5, 1, 1, "11", target=target)

    def ffn_b(dxo, sv, i, j, tag, last=False):
        xin, h, gu, y, wt = sv
        dxi, dy, a, dgu, dg0, dg1 = _ffn_bwd(dxo, xin, y, gu, gain(i, 4 * j), gain(i, 4 * j + 1), wt, "ffn_bwd_" + tag)
        dng[i][4 * j] = dg0
        dng[i][4 * j + 1] = dg1
        after = None
        if last:
            grads["norm_g"] = jnp.stack([jnp.concatenate(dng[t], axis=0) for t in range(2)])
            after = small_ready(grads, loss_part) if small_ready is not None else None
        grads["wd" + tag] = _mm(a, dy, "tn", "ffn_wgrad_down_" + tag, after=after)
        grads["wguT" + tag] = _mm(dgu, h, "tn", "ffn_wgrad_up_" + tag, after=after)
        tell("ffn" + tag)
        return dxi

    dx = ffn_b(dx, sv_f11, 1, 1, "11")
    dm1, dng[1][3], dum = _postnorm_bwd_dgrad(dx, m1, gain(1, 3), sgw["sg_wout"], "sg_dgrad_out")
    grads["sg_w_out"] = _mm(um, dm1, "tn", "sg_wgrad_out")
    dz1, dbin, dlng, dlnb, dws, dbst = _sg_bwd(zpre, sg_bin, sg_lng, sg_lnb, p["sg_w_s"], sg_bst, dum, "sg_bwd")
    grads["sg_w_inT"] = _mm(dz1, hn1, "tn", "sg_wgrad_in")
    tell("sg")
    dx, dng[1][2] = _dgrad_prenorm_bwd(dz1, sgw["sg_win"], None, dx, x4, gain(1, 2), "sg_dgrad_in")
    grads["sg_b_in"] = dbin.reshape(1, -1)
    grads["sg_ln_g"] = dlng.reshape(1, -1)
    grads["sg_ln_b"] = dlnb.reshape(1, -1)
    grads["sg_w_s"] = jnp.where(jnp.tril(jnp.ones((SG_CHUNK, SG_CHUNK), bool)), dws, 0.0)[None]
    grads["sg_b_s"] = dbst[:, :SG_GROUPS].T[None]
    dx = ffn_b(dx, sv_f10, 1, 0, "10")
    dx = ffn_b(dx, sv_f01, 0, 1, "01")
    dm0, dng[0][3], dog = _postnorm_bwd_dgrad(dx, m0, gain(0, 3), dnw["dn_wout"], "dn_dgrad_out")
    grads["dn_w_out"] = _mm(og, dm0, "tn", "dn_wgrad_out")
    dqkv, dbeta, dgdec, dz0, dnng = _dn_scan_bwd(qkv, beta, gdec, proj, dn_ng, o_raw, tinv, s_all, dog, "dn_scan_bwd")
    dqkv_pre, dconv = _dn_prep_bwd(proj, p["dn_conv_w"], dqkv, "dn_prep_bwd")
    dba, dal, ddt = _dn_gate_bwd(ba, a_log, dt_bias, dbeta, dgdec, "dn_gate_bwd")
    W3 = 3 * DN_HEADS * DN_HEAD_DIM
    dw_qkv = _mm(hn0, dqkv_pre, "tn", "dn_wgrad_qkv")
    dw_z = _mm(hn0, dz0, "tn", "dn_wgrad_z")
    dw_ba = _mm(hn0, dba, "tn", "dn_wgrad_ba")
    grads["dn_w_in"] = jnp.concatenate(
        [dw_qkv, dw_z, dw_ba[:, :DN_HEADS], dw_ba[:, LANES:LANES + DN_HEADS]], axis=1)
    tell("dn")
    dh0 = _mm(dqkv_pre, dnw["dn_wqkvz"][:, :W3], "nt", "dn_dgrad_qkv")
    dh0 = _mm(dz0, dnw["dn_wqkvz"][:, W3:], "nt", "dn_dgrad_z", add=dh0)
    dx, dng[0][2] = _dgrad_prenorm_bwd(dba, dnw["dn_wba"], dh0, dx, x1, gain(0, 2), "dn_dgrad_ba")
    grads["dn_conv_w"] = dconv[None]
    grads["dn_a_log"] = dal[:, :DN_HEADS]
    grads["dn_dt_bias"] = ddt[:, :DN_HEADS]
    grads["dn_norm_g"] = dnng
    dx = ffn_b(dx, sv_f00, 0, 0, "00", last=True)
    return loss_part, dx, grads


def _mesh_pos():
    return lax.axis_index("x"), lax.axis_index("y"), lax.axis_index("c")


def _other_chips(x, y):
    return [(1 - x, y), (x, 1 - y), (1 - x, 1 - y)]


def _allgather_chips(arrs, name):
    n = len(arrs)

    def body(*refs):
        ins, outs = refs[:n], refs[n:2 * n]
        ici_send, ici_recv, d2d_send, d2d_recv = refs[2 * n:]
        x, y, c = _mesh_pos()
        me = 2 * x + y
        chips = _other_chips(x, y)
        sibling = (x, y, 1 - c)

        def ici(i, j, k):
            cx, cy = chips[j]
            return pltpu.make_async_remote_copy(src_ref=ins[i].at[c], dst_ref=outs[i].at[k, c], send_sem=ici_send.at[3 * i + j],
                                                recv_sem=ici_recv.at[3 * i + j], device_id=(cx, cy, c), device_id_type=MESH)

        def d2d(i, j, h):
            cx, cy = chips[j]
            slot = outs[i].at[2 * cx + cy, h]
            return pltpu.make_async_remote_copy(src_ref=slot, dst_ref=slot, send_sem=d2d_send.at[3 * i + j],
                                                recv_sem=d2d_recv.at[3 * i + j], device_id=sibling, device_id_type=MESH)

        sends = [ici(i, j, me) for i in range(n) for j in range(3)]
        for cp in sends:
            cp.start()
        for i in range(n):
            for j, (cx, cy) in enumerate(chips):
                ici(i, j, 2 * cx + cy).wait_recv()
                fwd = d2d(i, j, c)
                fwd.start()
                sends.append(fwd)
        for i in range(n):
            for j in range(3):
                d2d(i, j, 1 - c).wait_recv()
        for cp in sends:
            cp.wait_send()

    return pl.pallas_call(
        body, name=name, in_specs=[ANY] * n, out_specs=[ANY] * n,
        out_shape=[jax.ShapeDtypeStruct((N_CHIPS,) + a.shape, a.dtype) for a in arrs],
        scratch_shapes=[pltpu.SemaphoreType.DMA((3 * n,))] * 4,
    )(*arrs)


HBM = pl.BlockSpec(memory_space=pltpu.HBM)
SEM = pl.BlockSpec(memory_space=pltpu.SEMAPHORE)
TOKEN = jax.ShapeDtypeStruct((SUBLANES, LANES), F32)


_PEERS = {"gather": 3, "scatter": 3, "swap": 1, "all": N_DEV - 1}


def _land_shape(kind, shape):
    if kind == "gather":
        return (N_CHIPS,) + shape
    if kind == "all":
        return (N_DEV,) + shape
    return (N_CHIPS,) + shape[2:] if kind == "swap" else shape


def _peer_copies(kind, flags, src_refs, land_refs, send_sems, recv_sems, receiving):
    x, y, c = _mesh_pos()
    me4, me8 = 2 * x + y, 4 * x + 2 * y + c
    np_ = _PEERS[kind]
    cps = []
    for i, (src, land) in enumerate(zip(src_refs, land_refs)):
        if kind == "swap":
            half = src.at[1 - c] if flags[i] else src.at[:, 1 - c]
            plan = [((x, y, 1 - c), half, land)]
        elif kind == "all":
            masks = [(mx, my, mc) for mx in (0, 1) for my in (0, 1) for mc in (0, 1)][1:]
            peers = [(jnp.where(mx, 1 - x, x), jnp.where(my, 1 - y, y), jnp.where(mc, 1 - c, c)) for mx, my, mc in masks]
            plan = [(p, src, land.at[4 * p[0] + 2 * p[1] + p[2] if receiving else me8]) for p in peers]
        else:
            plan = []
            for cx, cy in _other_chips(x, y):
                k = 2 * cx + cy
                s = src.at[me4 if receiving else k] if kind == "scatter" else src
                plan.append(((cx, cy, c), s, land.at[k if receiving else me4]))
        for j, (peer, s, d) in enumerate(plan):
            cps.append(pltpu.make_async_remote_copy(src_ref=s, dst_ref=d, send_sem=send_sems.at[np_ * i + j],
                                                    recv_sem=recv_sems.at[np_ * i + j], device_id=peer, device_id_type=MESH))
    return cps


def _copies_start(kind, srcs, after, name, flags=None):
    n = len(srcs)
    ns = _PEERS[kind] * n
    lands = [lax.empty(_land_shape(kind, s.shape), s.dtype) for s in srcs]
    after = [] if after is None else [after]

    def body(*refs):
        src_refs, land_refs = refs[:n], refs[n:2 * n]
        send_sems, recv_sems = refs[2 * n + len(after)], refs[2 * n + len(after) + 1]
        token = refs[-1]
        for cp in _peer_copies(kind, flags, src_refs, land_refs, send_sems, recv_sems, False):
            cp.start()
        token[...] = jnp.zeros_like(token)

    outs = pl.pallas_call(
        body, name=name,
        in_specs=[HBM] * (2 * n) + [ANY] * len(after),
        out_specs=(SEM, SEM) + (HBM,) * (2 * n) + (pl.BlockSpec(memory_space=pltpu.VMEM),),
        out_shape=(pltpu.SemaphoreType.DMA((ns,)), pltpu.SemaphoreType.DMA((ns,)))
        + tuple(pltpu.HBM(a.shape, a.dtype) for a in list(srcs) + lands) + (TOKEN,),
        input_output_aliases={i: 2 + i for i in range(2 * n)},
        compiler_params=pltpu.CompilerParams(has_side_effects=pltpu.SideEffectType.DATAFLOW_SIDE_EFFECTING),
    )(*[pltpu.with_memory_space_constraint(a, pltpu.HBM) for a in list(srcs) + lands], *after)
    return dict(sems=outs[:2], srcs=outs[2:2 + n], lands=outs[2 + n:2 + 2 * n], token=outs[-1], kind=kind, flags=flags)


def _copies_wait(started, after, name):
    n = len(started["srcs"])
    kind, flags = started["kind"], started["flags"]
    after = list(after) if isinstance(after, (list, tuple)) else [after]

    def body(*refs):
        src_refs, land_refs = refs[:n], refs[n:2 * n]
        send_sems, recv_sems = refs[2 * n], refs[2 * n + 1]
        for cp in _peer_copies(kind, flags, src_refs, land_refs, send_sems, recv_sems, True):
            cp.wait_send()
            cp.wait_recv()

    outs = pl.pallas_call(
        body, name=name,
        in_specs=[HBM] * (2 * n) + [SEM, SEM] + [ANY] * len(after),
        out_specs=(HBM,) * (2 * n),
        out_shape=tuple(pltpu.HBM(a.shape, a.dtype) for a in list(started["srcs"]) + list(started["lands"])),
        input_output_aliases={i: i for i in range(2 * n)},
        compiler_params=pltpu.CompilerParams(has_side_effects=pltpu.SideEffectType.DATAFLOW_SIDE_EFFECTING),
    )(*started["srcs"], *started["lands"], *started["sems"], *after)
    return outs[:n], outs[n:]


def _swap_whole(arrs, name):
    n = len(arrs)

    def body(*refs):
        ins, outs = refs[:n], refs[n:2 * n]
        send_sems, recv_sems = refs[2 * n:]
        x, y, c = _mesh_pos()
        cps = [pltpu.make_async_remote_copy(src_ref=ins[i], dst_ref=outs[i], send_sem=send_sems.at[i],
                                            recv_sem=recv_sems.at[i], device_id=(x, y, 1 - c), device_id_type=MESH)
               for i in range(n)]
        for cp in cps:
            cp.start()
        for cp in cps:
            cp.wait()

    return pl.pallas_call(
        body, name=name, in_specs=[ANY] * n, out_specs=[ANY] * n,
        out_shape=[jax.ShapeDtypeStruct(a.shape, a.dtype) for a in arrs],
        scratch_shapes=[pltpu.SemaphoreType.DMA((n,)), pltpu.SemaphoreType.DMA((n,))],
    )(*arrs)


def _as_rows(a, lead):
    shp = a.shape
    rows = 1
    for s in shp[lead:-1]:
        rows *= s
    return a.reshape(shp[:lead] + (rows, shp[-1]))


def _row_tile(rows, cols, n_bufs):
    budget = (24 * 1024 * 1024) // (n_bufs * 2 * 4 * cols)
    return _pick(rows, max(2 * SUBLANES, budget), 2 * SUBLANES)


def _sum_devices(own, got, dev, name):
    n, rows, cols = got.shape
    tr = _row_tile(rows, cols, n + 2)

    def body(dev_ref, own_ref, got_ref, o_ref):
        mine = own_ref[...]
        acc = jnp.where(dev_ref[0] == 0, mine, got_ref[0])
        for k in range(1, n):
            acc = acc + jnp.where(dev_ref[0] == k, mine, got_ref[k])
        o_ref[...] = acc

    return pl.pallas_call(
        body, name=name,
        grid_spec=pltpu.PrefetchScalarGridSpec(
            num_scalar_prefetch=1, grid=(rows // tr,),
            in_specs=[pl.BlockSpec((tr, cols), lambda i, d: (i, 0)), pl.BlockSpec((n, tr, cols), lambda i, d: (0, i, 0))],
            out_specs=pl.BlockSpec((tr, cols), lambda i, d: (i, 0))),
        out_shape=jax.ShapeDtypeStruct((rows, cols), F32), compiler_params=_cp("parallel"),
    )(_scalar(dev), own, got)


def _scalar(i):
    return jnp.reshape(i, (1,)).astype(jnp.int32)


def _add_own_half(g, other, c, half_first, name):
    _, rows, cols = other.shape
    tr = _row_tile(rows, cols, 3)

    def body(c_ref, g_ref, o_ref, out_ref):
        out_ref[0] = (g_ref[0, 0] + o_ref[0]).astype(out_ref.dtype)

    if half_first:
        g_map = lambda k, i, c_ref: (c_ref[0], k, i, 0)
    else:
        g_map = lambda k, i, c_ref: (k, c_ref[0], i, 0)
    flat = pl.BlockSpec((1, tr, cols), lambda k, i, c_ref: (k, i, 0))
    return pl.pallas_call(
        body, name=name,
        grid_spec=pltpu.PrefetchScalarGridSpec(
            num_scalar_prefetch=1, grid=(N_CHIPS, rows // tr),
            in_specs=[pl.BlockSpec((1, 1, tr, cols), g_map), flat], out_specs=flat),
        out_shape=jax.ShapeDtypeStruct(other.shape, COMM_DTYPE), compiler_params=_cp("parallel", "parallel"),
    )(_scalar(c), g, other)


def _sum_chips(own, got, chip, name, transpose=False):
    _, rows, cols = own.shape
    tr = rows if transpose else _row_tile(rows, cols, N_CHIPS + 2)

    def body(chip_ref, p_ref, b_ref, o_ref):
        mine = p_ref[0].astype(F32)
        acc = jnp.where(chip_ref[0] == 0, mine, b_ref[0].astype(F32))
        for k in range(1, N_CHIPS):
            acc = acc + jnp.where(chip_ref[0] == k, mine, b_ref[k].astype(F32))
        o_ref[...] = acc.T if transpose else acc

    if transpose:
        out_spec, out_shape = pl.BlockSpec((cols, rows), lambda i, k_ref: (0, 0)), (cols, rows)
    else:
        out_spec, out_shape = pl.BlockSpec((tr, cols), lambda i, k_ref: (i, 0)), (rows, cols)
    return pl.pallas_call(
        body, name=name,
        grid_spec=pltpu.PrefetchScalarGridSpec(
            num_scalar_prefetch=1, grid=(rows // tr,),
            in_specs=[pl.BlockSpec((1, tr, cols), lambda i, k_ref: (k_ref[0], i, 0)),
                      pl.BlockSpec((N_CHIPS, tr, cols), lambda i, k_ref: (0, i, 0))],
            out_specs=out_spec),
        out_shape=jax.ShapeDtypeStruct(out_shape, F32), compiler_params=_cp("parallel"),
    )(_scalar(chip), own, got)


def _adam_math(w, g, m, v):
    nm = ADAM_B1 * m + (1.0 - ADAM_B1) * g
    nv = ADAM_B2 * v + (1.0 - ADAM_B2) * (g * g)
    m_hat = nm / (1.0 - ADAM_B1 ** ADAM_STEP)
    v_hat = nv / (1.0 - ADAM_B2 ** ADAM_STEP)
    return -ADAM_LR * (m_hat / (jnp.sqrt(v_hat) + ADAM_EPS) + ADAM_WD * w), nm, nv


def _adamw_pieces(w, m, v, mine, theirs, c, kind, name):
    shape = w.shape
    P = len(mine)
    ws, ms, vs = (t.reshape((P, -1, t.shape[-1])) for t in (w, m, v))
    _, R, C = ws.shape
    if kind == "rows":
        tr = _pick(R // 2, 512, SUBLANES)
    else:
        tr = _pick(R, 256 if kind in ("lo", "hi") else 512, SUBLANES)
    nt = R // tr
    nh = nt // 2

    def body(c_ref, w_ref, m_ref, v_ref, *refs):
        mine_refs, theirs_refs = refs[:P], refs[P:2 * P]
        g_ref, d_ref, nm_ref, nv_ref = refs[2 * P:]
        p, i, core = pl.program_id(0), pl.program_id(1), c_ref[0]

        def pick(refs_):
            out = refs_[0][...]
            for q in range(1, P):
                out = jnp.where(p == q, refs_[q][...], out)
            return out

        a, b = pick(mine_refs), pick(theirs_refs)
        if kind == "cols":
            gv = jnp.where(core == 0, jnp.concatenate([a, b], axis=1), jnp.concatenate([b, a], axis=1))
        else:
            own = {"lo": core == 0, "hi": core == 1, "rows": (i >= nh) == (core == 1)}[kind]
            gv = jnp.where(own, a, b)
        g_ref[0] = gv
        d_ref[0], nm_ref[0], nv_ref[0] = _adam_math(w_ref[0], gv, m_ref[0], v_ref[0])

    def piece_spec(q):
        tile = (lambda i: i - jnp.where(i >= nh, nh, 0)) if kind == "rows" else (lambda i: i)
        return pl.BlockSpec((tr, mine[q].shape[1]), lambda p, i, c_ref: (jnp.where(p == q, tile(i), 0), 0))

    full = pl.BlockSpec((1, tr, C), lambda p, i, c_ref: (p, i, 0))
    outs = pl.pallas_call(
        body, name=name,
        grid_spec=pltpu.PrefetchScalarGridSpec(num_scalar_prefetch=1, grid=(P, nt),
                                               in_specs=[full] * 3 + [piece_spec(q) for q in range(P)] * 2,
                                               out_specs=[full] * 4),
        out_shape=[jax.ShapeDtypeStruct((P, R, C), F32)] * 4, compiler_params=_cp("parallel", "arbitrary"),
    )(_scalar(c), ws, ms, vs, *mine, *theirs)
    return tuple(o.reshape(shape) for o in outs)


def _adamw(w, g, m, v, name):
    shape = w.shape
    ws, gs, ms, vs = (_as_rows(t, 0) for t in (w, g, m, v))
    rows, cols = ws.shape
    tr = _row_tile(rows, cols, 7)

    def body(w_ref, g_ref, m_ref, v_ref, d_ref, nm_ref, nv_ref):
        d_ref[...], nm_ref[...], nv_ref[...] = _adam_math(w_ref[...], g_ref[...], m_ref[...], v_ref[...])

    spec = pl.BlockSpec((tr, cols), lambda i: (i, 0))
    outs = pl.pallas_call(body, name=name, grid=(rows // tr,), in_specs=[spec] * 4, out_specs=[spec] * 3,
                          out_shape=[jax.ShapeDtypeStruct((rows, cols), F32)] * 3, compiler_params=_cp("parallel"))(ws, gs, ms, vs)
    return tuple(o.reshape(shape) for o in outs)


_BIG = ["ffn_w_gate", "ffn_w_up", "ffn_w_down", "dn_w_in", "dn_w_out", "sg_w_in", "sg_w_out"]
_SMALL_SHARDED = ["norm_g", "dn_conv_w", "sg_b_in", "sg_ln_g", "sg_ln_b"]
_SMALL_REPL = ["dn_a_log", "dn_dt_bias", "dn_norm_g", "sg_w_s", "sg_b_s"]
_WEIGHTS = ["norm_g", "ffn_w_gate", "ffn_w_up", "ffn_w_down", "dn_w_in", "dn_conv_w", "dn_a_log", "dn_dt_bias",
            "dn_norm_g", "dn_w_out", "sg_w_in", "sg_b_in", "sg_ln_g", "sg_ln_b", "sg_w_s", "sg_b_s", "sg_w_out"]
PACK_COLS = 1024


def _pack(arrs):
    flat = jnp.concatenate([a.reshape(-1) for a in arrs])
    pad = (-flat.shape[0]) % (SUBLANES * PACK_COLS)
    return jnp.pad(flat, (0, pad)).reshape(-1, PACK_COLS)


def _unpack(buf, shapes):
    flat = buf.reshape(-1)
    out, off = [], 0
    for s in shapes:
        n = math.prod(s)
        out.append(flat[off:off + n].reshape(s))
        off += n
    return out


def _as_halves(a):
    if a.shape[0] == 2:
        return a
    if a.shape[0] == 1:
        return a.reshape((2, a.shape[1] // 2) + a.shape[2:])
    return a.reshape((2, a.shape[0] // 2) + a.shape[1:])


def _with_own(gathered, own, chip):
    g = gathered.reshape((N_CHIPS,) + own.shape)
    return [jnp.where(chip == k, own, g[k]) for k in range(N_CHIPS)]


def _cat_shards(g, axis):
    return jnp.concatenate(list(g), axis=axis)


_GROUP_ORDER = ["ffn00", "dn", "ffn01", "ffn10", "sg", "ffn11"]


def _weight_groups(w):
    cast = {k: _mx(w[k]) for k in _BIG}
    groups = {"ffn%d%d" % (i, j): [cast["ffn_w_gate"][i, j].T, cast["ffn_w_up"][i, j].T, cast["ffn_w_down"][i, j]]
              for i, j in [(0, 0), (0, 1), (1, 0), (1, 1)]}
    groups["dn"] = [cast["dn_w_in"][0], cast["dn_w_out"][0]]
    groups["sg"] = [cast["sg_w_in"][0], cast["sg_w_out"][0]]
    return groups


def _ffn_weights(chip, own, gathered):
    pairs = [(a, g.reshape((N_CHIPS,) + a.shape)) for a, g in zip(own, gathered)]
    return {"chip": chip, "gate": pairs[0], "up": pairs[1], "down": pairs[2]}


def _group_matrices(group, shards):
    if group == "sg":
        return {"sg_win": _cat_shards(shards[0], 1), "sg_wout": _cat_shards(shards[1], 0)}
    dn_full = _cat_shards(shards[0], 1)
    W4 = 4 * DN_HEADS * DN_HEAD_DIM
    wba = jnp.zeros((D_MODEL, 2 * LANES), dn_full.dtype)
    wba = wba.at[:, :DN_HEADS].set(dn_full[:, W4:W4 + DN_HEADS])
    wba = wba.at[:, LANES:LANES + DN_HEADS].set(dn_full[:, W4 + DN_HEADS:])
    return {"dn_wqkvz": dn_full[:, :W4], "dn_wba": wba, "dn_wout": _cat_shards(shards[1], 0)}


def _split_cols(a, n):
    w = a.shape[-1] // n
    return [a[..., k * w:(k + 1) * w] for k in range(n)]


def _split_rows(a, n):
    h = a.shape[-2] // n
    return [a[..., k * h:(k + 1) * h, :] for k in range(n)]


_IJ = [(0, 0), (0, 1), (1, 0), (1, 1)]


def _group_grads(group, grads):
    def rows_by_chip(a):
        return a.reshape(N_CHIPS, 2, a.shape[0] // (2 * N_CHIPS), a.shape[1])

    if group.startswith("ffn"):
        tag = group[3:]
        t = grads["wguT" + tag]
        return (["wguT" + tag, "wd" + tag],
                [t.reshape(2, N_CHIPS, t.shape[0] // (2 * N_CHIPS), t.shape[1]), rows_by_chip(grads["wd" + tag])], [True, False])
    if group == "sg":
        return ["sg_w_inT", "sg_w_out"], [rows_by_chip(grads["sg_w_inT"]), rows_by_chip(grads["sg_w_out"])], [False, False]
    dn_in = jnp.stack([jnp.stack(_split_cols(hf, N_CHIPS)) for hf in _split_rows(grads["dn_w_in"], 2)])
    return ["dn_w_in", "dn_w_out"], [dn_in, rows_by_chip(grads["dn_w_out"])], [True, False]


_SHARD_PIECES = {
    "ffn_w_gate": (["wguT%d%d" % ij for ij in _IJ], "lo"),
    "ffn_w_up": (["wguT%d%d" % ij for ij in _IJ], "hi"),
    "ffn_w_down": (["wd%d%d" % ij for ij in _IJ], "rows"),
    "dn_w_in": (["dn_w_in"], "rows"),
    "dn_w_out": (["dn_w_out"], "rows"),
    "sg_w_in": (["sg_w_inT"], "cols"),
    "sg_w_out": (["sg_w_out"], "rows"),
}


def kernel(x, norm_g, ffn_w_gate, ffn_w_up, ffn_w_down, dn_w_in, dn_conv_w, dn_a_log, dn_dt_bias, dn_norm_g, dn_w_out, sg_w_in, sg_b_in, sg_ln_g, sg_ln_b, sg_w_s, sg_b_s, sg_w_out, loss_target, m_norm_g, m_ffn_w_gate, m_ffn_w_up, m_ffn_w_down, m_dn_w_in, m_dn_conv_w, m_dn_a_log, m_dn_dt_bias, m_dn_norm_g, m_dn_w_out, m_sg_w_in, m_sg_b_in, m_sg_ln_g, m_sg_ln_b, m_sg_w_s, m_sg_b_s, m_sg_w_out, v_norm_g, v_ffn_w_gate, v_ffn_w_up, v_ffn_w_down, v_dn_w_in, v_dn_conv_w, v_dn_a_log, v_dn_dt_bias, v_dn_norm_g, v_dn_w_out, v_sg_w_in, v_sg_b_in, v_sg_ln_g, v_sg_ln_b, v_sg_w_s, v_sg_b_s, v_sg_w_out):
    args = dict(locals())
    w = {k: args[k] for k in _WEIGHTS}
    mom = {k: args["m_" + k] for k in _WEIGHTS}
    var = {k: args["v_" + k] for k in _WEIGHTS}
    cx, cy, cc = _mesh_pos()
    chip = 2 * cx + cy

    small_shapes = [w[k].shape for k in _SMALL_SHARDED]
    groups = _weight_groups(w)
    own = groups[_GROUP_ORDER[0]] + [_pack([w[k] for k in _SMALL_SHARDED])]
    first = _allgather_chips([_as_halves(a) for a in own], "gather_first")
    started, after = {}, first[0]
    for g in _GROUP_ORDER[1:]:
        started[g] = _copies_start("gather", groups[g], after, "gather_start_" + g)
        after = started[g]["token"]
    small_k = [_unpack(pack, small_shapes) for pack in _with_own(first[-1], own[-1], chip)]
    p = {name: jnp.concatenate([small_k[k][i] for k in range(N_CHIPS)], axis=-1) for i, name in enumerate(_SMALL_SHARDED)}
    p = {k: (v if k == "norm_g" else v[0]) for k, v in p.items()}
    p["norm_g"] = p["norm_g"] + after[0, 0]
    for k in _SMALL_REPL:
        p[k] = w[k][0]

    def weights_for(group, after):
        if group == _GROUP_ORDER[0]:
            return _ffn_weights(chip, own[:-1], first[:-1])
        srcs, lands = _copies_wait(started[group], after, "gather_wait_" + group)
        if group.startswith("ffn"):
            return _ffn_weights(chip, srcs, lands)
        return _group_matrices(group, [_with_own(l, a, chip) for l, a in zip(lands, srcs)])

    mine, theirs, to_core, to_chips = {}, {}, [], []

    def send_to_chips(after):
        group, names, flags, swap = to_core.pop(0)
        halves, got = _copies_wait(swap, after, "swap_wait_" + group)
        pair_sum = [_add_own_half(h, o, cc, hf, "pair_sum_" + n) for n, h, o, hf in zip(names, halves, got, flags)]
        scatter = _copies_start("scatter", pair_sum, got[0], "reduce_start_" + group)
        to_chips.append((group, names, scatter))
        return scatter["token"]

    def finish(after):
        group, names, scatter = to_chips.pop(0)
        pair_sum, got = _copies_wait(scatter, after, "reduce_wait_" + group)
        half_sum = [_sum_chips(a, b, chip, "chip_sum_" + n, transpose=n == "sg_w_inT")
                    for n, a, b in zip(names, pair_sum, got)]
        other = _swap_whole(half_sum, "gather_core_pair_" + group)
        mine.update(zip(names, half_sum))
        theirs.update(zip(names, other))

    def grads_ready(group, grads):
        names, halves, flags = _group_grads(group, grads)
        swap = _copies_start("swap", halves, None, "swap_start_" + group, flags)
        token = swap["token"]
        if to_core:
            token = send_to_chips(token)
            if len(to_chips) > 1:
                finish(token)
        to_core.append((group, names, flags, swap))
        return token[0, 0]

    small_names = _SMALL_SHARDED + _SMALL_REPL
    small = {}

    def small_ready(grads, loss_part):
        parts = [grads[k] for k in small_names]
        small["shapes"] = [g.shape for g in parts] + [(1,)]
        pack = _pack(parts + [loss_part[0, :1]])
        small["exchange"] = _copies_start("all", [pack], None, "small_start")
        return small["exchange"]["token"]

    loss_part, grad_x, grads = _local_step(x[0], loss_target[0], p, weights_for, grads_ready, small_ready)
    token = send_to_chips(to_core[0][3]["token"])
    finish(token)
    (pack,), (packs,) = _copies_wait(small["exchange"], list(theirs.values()), "small_wait")
    summed = _sum_devices(pack, packs, 4 * cx + 2 * cy + cc, "small_sum")
    parts = _unpack(summed, small["shapes"])
    loss = parts[-1][0]
    grad = {}
    for i, k in enumerate(small_names):
        g = parts[i]
        if k in _SMALL_SHARDED:
            n = w[k].shape[-1]
            g = lax.dynamic_slice_in_dim(g, chip * n, n, axis=g.ndim - 1)
        grad[k] = g

    delta, new_m, new_v = {}, {}, {}

    def update(keys):
        for k in keys:
            names, kind = _SHARD_PIECES[k]
            turn = (lambda a: jnp.swapaxes(a, -1, -2)) if names[0].startswith("wguT") else (lambda a: a)
            outs = _adamw_pieces(turn(w[k]), turn(mom[k]), turn(var[k]), [mine[n] for n in names], [theirs[n] for n in names],
                                 cc, kind, "adamw_" + k)
            grad[k], delta[k], new_m[k], new_v[k] = (turn(o) for o in outs)

    shapes = [w[k].shape for k in small_names]
    d, nm, nv = _adamw(_pack([w[k] for k in small_names]), _pack([grad[k] for k in small_names]),
                       _pack([mom[k] for k in small_names]), _pack([var[k] for k in small_names]), "adamw_small")
    for k, a, b, c_ in zip(small_names, _unpack(d, shapes), _unpack(nm, shapes), _unpack(nv, shapes)):
        delta[k], new_m[k], new_v[k] = a, b, c_
    mixers = [k for k in _BIG if not k.startswith("ffn")]
    update(mixers)
    finish([d] + [delta[k] for k in mixers] + list(theirs.values()))
    update([k for k in _BIG if k.startswith("ffn")])

    return (loss, grad_x[None], *[grad[k] for k in _WEIGHTS], *[delta[k] for k in _WEIGHTS],
            *[new_m[k] for k in _WEIGHTS], *[new_v[k] for k in _WEIGHTS])
```

```python
import math

import jax
import jax.numpy as jnp
from jax import lax
from jax.experimental import pallas as pl
from jax.experimental.pallas import tpu as pltpu

F32 = jnp.float32
MXU_DTYPE = jnp.bfloat16
COMM_DTYPE = jnp.bfloat16
HI = lax.Precision.HIGHEST
TRI_PREC = lax.Precision.HIGH

D_MODEL = 1024
RMS_EPS = 1e-6
LN_EPS = 1e-5
L2_EPS = 1e-6
DN_HEADS = 8
DN_HEAD_DIM = 128
DN_CONV = 4
DN_CHUNK = 64
SG_WIDTH = 2048
SG_GROUPS = 8
SG_CHUNK = 128
SG_GROUP_W = SG_WIDTH // SG_GROUPS
N_CHIPS = 4
N_DEV = 8
LANES = 128
SUBLANES = 8
VMEM_LIMIT = 56 * 1024 * 1024

ADAM_LR = 0.001
ADAM_B1 = 0.9
ADAM_B2 = 0.999
ADAM_EPS = 1e-08
ADAM_WD = 0.01
ADAM_STEP = 10

MESH = pl.DeviceIdType.MESH
ANY = pl.BlockSpec(memory_space=pl.ANY)


def _cp(*sem):
    return pltpu.CompilerParams(dimension_semantics=sem, vmem_limit_bytes=VMEM_LIMIT)


def _pick(n, pref, mult=LANES):
    best = None
    d = mult
    while d <= min(n, pref):
        if n % d == 0:
            best = d
        d += mult
    return best if best is not None else n


def _full(shape):
    nd = len(shape)
    return pl.BlockSpec(shape, lambda *_: (0,) * nd)


def _sigmoid(x):
    return 1.0 / (1.0 + jnp.exp(-x))


def _dot(a, b, dims, prec=None):
    return lax.dot_general(a, b, (dims, ((), ())), preferred_element_type=F32, precision=prec)


NN = ((1,), (0,))
NT = ((1,), (1,))
TN = ((0,), (0,))


def _mx(a):
    return a.astype(MXU_DTYPE)


def _rms_stat(x):
    return lax.rsqrt(jnp.mean(x * x, axis=-1, keepdims=True) + RMS_EPS)


def _rms_bwd(x, r, g, dy):
    xh = x * r
    dxh = dy * g
    dx = r * (dxh - xh * jnp.mean(dxh * xh, axis=-1, keepdims=True))
    return dx, jnp.sum(dy * xh, axis=0, keepdims=True)


def _mm(a, b, mode, name, out_dtype=F32, add=None, after=None):
    if mode == "tn":
        K, M = a.shape
        N = b.shape[1]
    elif mode == "nt":
        M, K = a.shape
        N = b.shape[0]
    else:
        M, K = a.shape
        N = b.shape[1]
    tn = _pick(N, 1024)
    if mode == "tn":
        tm = _pick(M, 1024 if tn <= 512 else 1408)
        tk = _pick(K, 1024, SUBLANES)
    else:
        tm = _pick(M, max(512, min(2048, (1024 * 1024) // tn)), SUBLANES)
        tk = _pick(K, 2048)
    nk = K // tk
    grid = (N // tn, M // tm, nk)
    if mode == "nn":
        a_spec = pl.BlockSpec((tm, tk), lambda j, i, k: (i, k))
        b_spec = pl.BlockSpec((tk, tn), lambda j, i, k: (k, j))
        dims = NN
    elif mode == "nt":
        a_spec = pl.BlockSpec((tm, tk), lambda j, i, k: (i, k))
        b_spec = pl.BlockSpec((tn, tk), lambda j, i, k: (j, k))
        dims = NT
    else:
        a_spec = pl.BlockSpec((tk, tm), lambda j, i, k: (k, i))
        b_spec = pl.BlockSpec((tk, tn), lambda j, i, k: (k, j))
        dims = TN
    o_spec = pl.BlockSpec((tm, tn), lambda j, i, k: (i, j))
    has_add = add is not None

    def body(*refs):
        a_ref, b_ref = refs[:2]
        add_ref = refs[2] if has_add else None
        o_ref, acc = refs[-2:]
        k = pl.program_id(2)

        @pl.when(k == 0)
        def _():
            acc[...] = add_ref[...] if has_add else jnp.zeros_like(acc)

        acc[...] += _dot(a_ref[...], b_ref[...], dims)

        @pl.when(k == nk - 1)
        def _():
            o_ref[...] = acc[...].astype(o_ref.dtype)

    ins = [a, b] + ([add] if has_add else []) + ([after] if after is not None else [])
    specs = [a_spec, b_spec] + ([o_spec] if has_add else []) + ([ANY] if after is not None else [])
    return pl.pallas_call(
        body, name=name, grid=grid, in_specs=specs, out_specs=o_spec,
        out_shape=jax.ShapeDtypeStruct((M, N), out_dtype),
        scratch_shapes=[pltpu.VMEM((tm, tn), F32)],
        compiler_params=_cp("parallel", "parallel", "arbitrary"),
    )(*ins)


def _ffn_weight_operands(wt):
    return [_scalar(wt["chip"])] , [wt["gate"][0], wt["gate"][1], wt["up"][0], wt["up"][1], wt["down"][0], wt["down"][1]]


def _load_ffn_weights(chip_ref, shard_refs, wgu_v, wd_v, sem):
    fs = wd_v.shape[0] // N_CHIPS

    @pl.when(pl.program_id(0) == 0)
    def _():
        me = chip_ref[0]
        waits = []
        for t, (dst, base) in enumerate([(wgu_v, 0), (wgu_v, wd_v.shape[0]), (wd_v, 0)]):
            own, gathered = shard_refs[2 * t], shard_refs[2 * t + 1]
            for k in range(N_CHIPS):
                slot = dst.at[pl.ds(base + k * fs, fs), :]
                s = sem.at[t * N_CHIPS + k]

                @pl.when(me == k)
                def _(own=own, slot=slot, s=s):
                    pltpu.make_async_copy(own, slot, s).start()

                @pl.when(me != k)
                def _(gathered=gathered, k=k, slot=slot, s=s):
                    pltpu.make_async_copy(gathered.at[k], slot, s).start()

                waits.append(pltpu.make_async_copy(own, slot, s))
        for cp in waits:
            cp.wait()


def _ffn_fwd(x, g0, g1, wt, name, next_gain=None, target=None):
    T, D = x.shape
    F = N_CHIPS * wt["down"][0].shape[0]
    F2 = 2 * F
    tm = _pick(T, 256, SUBLANES)
    prefetch, shards = _ffn_weight_operands(wt)
    extra = [a for a in (next_gain, target) if a is not None]
    n_tail = 1 if next_gain is not None else (2 if target is not None else 0)

    def body(chip_ref, x_ref, g0_ref, g1_ref, *refs):
        extra_ref = refs[0] if extra else None
        refs = refs[len(extra):]
        shard_refs = refs[:6]
        xo_ref, h_ref, gu_ref, y_ref = refs[6:10]
        tail_refs = refs[10:10 + n_tail]
        wgu_v, wd_v, sem = refs[10 + n_tail:]
        _load_ffn_weights(chip_ref, shard_refs, wgu_v, wd_v, sem)
        xv = x_ref[...]
        hb = _mx(xv * _rms_stat(xv) * g0_ref[...])
        h_ref[...] = hb
        gu = _dot(hb, wgu_v[...], NT)
        gu_ref[...] = gu.astype(gu_ref.dtype)
        g = gu[:, :F]
        u = gu[:, F:]
        a = _mx(g * _sigmoid(g) * u)
        y = _dot(a, wd_v[...], NN)
        y_ref[...] = y
        xo = xv + 0.5 * (y * _rms_stat(y) * g1_ref[...])
        xo_ref[...] = xo
        if next_gain is not None:
            tail_refs[0][...] = _mx(xo * _rms_stat(xo) * extra_ref[...])
        if target is not None:
            loss_ref, dy_ref = tail_refs

            @pl.when(pl.program_id(0) == 0)
            def _():
                loss_ref[...] = jnp.zeros_like(loss_ref)

            e = xo - extra_ref[...]
            dy_ref[...] = e * (1.0 / D)
            loss_ref[...] += 0.5 * jnp.sum(jnp.mean(e * e, axis=-1, keepdims=True), axis=0, keepdims=True)

    row = lambda w: pl.BlockSpec((tm, w), lambda i, c: (i, 0))
    one = pl.BlockSpec((1, D), lambda i, c: (0, 0))
    tail_specs, tail_shapes, extra_specs = [], [], []
    if next_gain is not None:
        extra_specs, tail_specs, tail_shapes = [one], [row(D)], [jax.ShapeDtypeStruct((T, D), MXU_DTYPE)]
    if target is not None:
        extra_specs = [row(D)]
        tail_specs = [pl.BlockSpec((SUBLANES, LANES), lambda i, c: (0, 0)), row(D)]
        tail_shapes = [jax.ShapeDtypeStruct((SUBLANES, LANES), F32), jax.ShapeDtypeStruct((T, D), F32)]
    return pl.pallas_call(
        body, name=name,
        grid_spec=pltpu.PrefetchScalarGridSpec(
            num_scalar_prefetch=1, grid=(T // tm,),
            in_specs=[row(D), one, one] + extra_specs + [ANY] * 6,
            out_specs=[row(D), row(D), row(F2), row(D)] + tail_specs,
            scratch_shapes=[pltpu.VMEM((F2, D), MXU_DTYPE), pltpu.VMEM((F, D), MXU_DTYPE),
                            pltpu.SemaphoreType.DMA((3 * N_CHIPS,))]),
        out_shape=[jax.ShapeDtypeStruct((T, D), F32), jax.ShapeDtypeStruct((T, D), MXU_DTYPE),
                   jax.ShapeDtypeStruct((T, F2), MXU_DTYPE), jax.ShapeDtypeStruct((T, D), F32)] + tail_shapes,
        compiler_params=_cp("arbitrary"),
    )(*prefetch, x, g0, g1, *extra, *shards)


FFN_BWD_CHUNK = 2816


def _ffn_bwd(dxo, x, y, gu, g0, g1, wt, name):
    T, D = x.shape
    F2 = gu.shape[1]
    F = F2 // 2
    tm = _pick(T, 256, SUBLANES)
    fc = _pick(F, FFN_BWD_CHUNK)
    prefetch, shards = _ffn_weight_operands(wt)

    def body(chip_ref, dxo_ref, x_ref, y_ref, gu_ref, g0_ref, g1_ref, *refs):
        shard_refs = refs[:6]
        dx_ref, dy_ref, a_ref, dgu_ref, dg0_ref, dg1_ref, wgu_v, wd_v, sem = refs[6:]
        _load_ffn_weights(chip_ref, shard_refs, wgu_v, wd_v, sem)

        @pl.when(pl.program_id(0) == 0)
        def _():
            dg0_ref[...] = jnp.zeros_like(dg0_ref)
            dg1_ref[...] = jnp.zeros_like(dg1_ref)

        dxo_v = dxo_ref[...]
        yv = y_ref[...]
        dy, dg1 = _rms_bwd(yv, _rms_stat(yv), g1_ref[...], 0.5 * dxo_v)
        dg1_ref[...] += dg1
        dyb = _mx(dy)
        dy_ref[...] = dyb
        dh = jnp.zeros((tm, D), F32)
        for c in range(F // fc):
            lo, hi = c * fc, (c + 1) * fc
            da = _dot(dyb, wd_v[lo:hi, :], NT)
            g = gu_ref[:, lo:hi].astype(F32)
            u = gu_ref[:, F + lo:F + hi].astype(F32)
            s = _sigmoid(g)
            sg = g * s
            a_ref[:, lo:hi] = _mx(sg * u)
            dg = _mx(da * u * (s * (1.0 + g * (1.0 - s))))
            du = _mx(da * sg)
            dgu_ref[:, lo:hi] = dg
            dgu_ref[:, F + lo:F + hi] = du
            dh = dh + _dot(dg, wgu_v[lo:hi, :], NN) + _dot(du, wgu_v[F + lo:F + hi, :], NN)
        xv = x_ref[...]
        dx, dg0 = _rms_bwd(xv, _rms_stat(xv), g0_ref[...], dh)
        dg0_ref[...] += dg0
        dx_ref[...] = dxo_v + dx

    row = lambda w: pl.BlockSpec((tm, w), lambda i, c: (i, 0))
    one = pl.BlockSpec((1, D), lambda i, c: (0, 0))
    return pl.pallas_call(
        body, name=name,
        grid_spec=pltpu.PrefetchScalarGridSpec(
            num_scalar_prefetch=1, grid=(T // tm,),
            in_specs=[row(D), row(D), row(D), row(F2), one, one] + [ANY] * 6,
            out_specs=[row(D), row(D), row(F), row(F2), one, one],
            scratch_shapes=[pltpu.VMEM((F2, D), MXU_DTYPE), pltpu.VMEM((F, D), MXU_DTYPE),
                            pltpu.SemaphoreType.DMA((3 * N_CHIPS,))]),
        out_shape=[jax.ShapeDtypeStruct((T, D), F32), jax.ShapeDtypeStruct((T, D), MXU_DTYPE),
                   jax.ShapeDtypeStruct((T, F), MXU_DTYPE), jax.ShapeDtypeStruct((T, F2), MXU_DTYPE),
                   jax.ShapeDtypeStruct((1, D), F32), jax.ShapeDtypeStruct((1, D), F32)],
        compiler_params=_cp("arbitrary"),
    )(*prefetch, dxo, x, y, gu, g0, g1, *shards)


def _out_proj_postnorm(a, b, x, g, name):
    T, K = a.shape
    D = b.shape[1]
    tm = _pick(T, 512, SUBLANES)

    def body(a_ref, b_ref, x_ref, g_ref, m_ref, o_ref):
        mv = _dot(a_ref[...], b_ref[...], NN)
        m_ref[...] = mv
        o_ref[...] = x_ref[...] + mv * _rms_stat(mv) * g_ref[...]

    row = lambda w: pl.BlockSpec((tm, w), lambda i: (i, 0))
    return pl.pallas_call(body, name=name, grid=(T // tm,),
                          in_specs=[row(K), _full((K, D)), row(D), _full((1, D))], out_specs=[row(D), row(D)],
                          out_shape=[jax.ShapeDtypeStruct((T, D), F32)] * 2, compiler_params=_cp("parallel"))(a, b, x, g)


def _postnorm_bwd_dgrad(dxo, m, g, b, name):
    T, D = m.shape
    K = b.shape[0]
    tm = _pick(T, 512, SUBLANES)

    def body(dxo_ref, m_ref, g_ref, b_ref, dm_ref, dg_ref, da_ref):
        @pl.when(pl.program_id(0) == 0)
        def _():
            dg_ref[...] = jnp.zeros_like(dg_ref)

        mv = m_ref[...]
        dm, dg = _rms_bwd(mv, _rms_stat(mv), g_ref[...], dxo_ref[...])
        dg_ref[...] += dg
        dmb = _mx(dm)
        dm_ref[...] = dmb
        da_ref[...] = _dot(dmb, b_ref[...], NT)

    row = lambda w: pl.BlockSpec((tm, w), lambda i: (i, 0))
    return pl.pallas_call(body, name=name, grid=(T // tm,), in_specs=[row(D), row(D), _full((1, D)), _full((K, D))],
                          out_specs=[row(D), _full((1, D)), row(K)],
                          out_shape=[jax.ShapeDtypeStruct((T, D), MXU_DTYPE), jax.ShapeDtypeStruct((1, D), F32),
                                     jax.ShapeDtypeStruct((T, K), F32)],
                          compiler_params=_cp("arbitrary"))(dxo, m, g, b)


def _dgrad_prenorm_bwd(a, b, add, dxo, x, g, name):
    T, K = a.shape
    D = b.shape[0]
    tm = _pick(T, 512, SUBLANES)
    tk = _pick(K, 2048)
    nk = K // tk
    has_add = add is not None

    def body(*refs):
        a_ref, b_ref = refs[:2]
        add_ref = refs[2] if has_add else None
        dxo_ref, x_ref, g_ref, dx_ref, dg_ref, acc = refs[-6:]
        i, k = pl.program_id(0), pl.program_id(1)

        @pl.when((i == 0) & (k == 0))
        def _():
            dg_ref[...] = jnp.zeros_like(dg_ref)

        @pl.when(k == 0)
        def _():
            acc[...] = add_ref[...] if has_add else jnp.zeros_like(acc)

        acc[...] += _dot(a_ref[...], b_ref[...], NT)

        @pl.when(k == nk - 1)
        def _():
            xv = x_ref[...]
            dx, dg = _rms_bwd(xv, _rms_stat(xv), g_ref[...], acc[...])
            dg_ref[...] += dg
            dx_ref[...] = dxo_ref[...] + dx

    row = pl.BlockSpec((tm, D), lambda i, k: (i, 0))
    one = pl.BlockSpec((1, D), lambda i, k: (0, 0))
    ins = [a, b] + ([add] if has_add else []) + [dxo, x, g]
    specs = ([pl.BlockSpec((tm, tk), lambda i, k: (i, k)), pl.BlockSpec((D, tk), lambda i, k: (0, k))]
             + ([row] if has_add else []) + [row, row, one])
    return pl.pallas_call(body, name=name, grid=(T // tm, nk), in_specs=specs, out_specs=[row, one],
                          out_shape=[jax.ShapeDtypeStruct((T, D), F32), jax.ShapeDtypeStruct((1, D), F32)],
                          scratch_shapes=[pltpu.VMEM((tm, D), F32)],
                          compiler_params=_cp("arbitrary", "arbitrary"))(*ins)


DN_ROWS = 512


def _shift_down(prev8, cur, s):
    n = cur.shape[0]
    xx = jnp.concatenate([prev8, cur], axis=0)
    return pltpu.roll(xx, s, 0)[SUBLANES:SUBLANES + n, :]


def _shift_up(cur, next8, s):
    n = cur.shape[0]
    xx = jnp.concatenate([cur, next8], axis=0)
    return pltpu.roll(xx, n + SUBLANES - s, 0)[:n, :]


def _tile_start(r, rows):
    return r * rows if isinstance(r, int) else pl.multiple_of(r * rows, SUBLANES)


def _conv_tile(x_ref, w, r, rows):
    start = _tile_start(r, rows)
    cur = x_ref[pl.ds(start, rows), :]
    if isinstance(r, int):
        prev8 = jnp.zeros((SUBLANES, cur.shape[1]), cur.dtype)
        taps = [_shift_down(prev8, cur, DN_CONV - 1 - j) if j < DN_CONV - 1 else cur for j in range(DN_CONV)]
    else:
        taps = [x_ref[pl.ds(start - (DN_CONV - 1 - j), rows), :] if j < DN_CONV - 1 else cur for j in range(DN_CONV)]
    c = taps[0] * w[0:1, :]
    for j in range(1, DN_CONV):
        c = c + taps[j] * w[j:j + 1, :]
    return c, taps


def _dn_prep_fwd(proj, conv_w, name):
    T = proj.shape[0]
    W = DN_HEADS * DN_HEAD_DIM
    rows = min(DN_ROWS, T)
    n_inner = T // rows
    scale = DN_HEAD_DIM ** -0.5

    def body(x_ref, w_ref, o_ref):
        cb = pl.program_id(0)
        w = w_ref[...]
        is_qk = cb < 2 * DN_HEADS
        post = jnp.where(cb < DN_HEADS, scale, 1.0)

        def step(r, carry):
            c, _ = _conv_tile(x_ref, w, r, rows)
            s = c * _sigmoid(c)
            rinv = lax.rsqrt(jnp.sum(s * s, axis=-1, keepdims=True) + L2_EPS)
            o_ref[pl.ds(_tile_start(r, rows), rows), :] = jnp.where(is_qk, s * rinv * post, s)
            return carry

        step(0, 0)
        lax.fori_loop(1, n_inner, step, 0)

    col = pl.BlockSpec((T, LANES), lambda j: (0, j))
    return pl.pallas_call(body, name=name, grid=(3 * W // LANES,),
                          in_specs=[col, pl.BlockSpec((DN_CONV, LANES), lambda j: (0, j))], out_specs=col,
                          out_shape=jax.ShapeDtypeStruct((T, 3 * W), F32), compiler_params=_cp("parallel"))(proj, conv_w)


def _dn_prep_bwd(proj, conv_w, dqkv, name):
    T = proj.shape[0]
    W = DN_HEADS * DN_HEAD_DIM
    rows = min(DN_ROWS, T)
    n_inner = T // rows
    scale = DN_HEAD_DIM ** -0.5

    def body(x_ref, w_ref, dy_ref, dx_ref, dw_ref, dc_scr):
        cb = pl.program_id(0)
        w = w_ref[...]
        is_qk = cb < 2 * DN_HEADS
        post = jnp.where(cb < DN_HEADS, scale, 1.0)

        def step1(r, dws):
            c, taps = _conv_tile(x_ref, w, r, rows)
            sg = _sigmoid(c)
            s = c * sg
            rinv = lax.rsqrt(jnp.sum(s * s, axis=-1, keepdims=True) + L2_EPS)
            dy = dy_ref[pl.ds(_tile_start(r, rows), rows), :]
            yn = s * rinv
            dyn = dy * post
            ds_qk = rinv * (dyn - yn * jnp.sum(dyn * yn, axis=-1, keepdims=True))
            ds = jnp.where(is_qk, ds_qk, dy)
            dc = ds * (sg * (1.0 + c * (1.0 - sg)))
            dc_scr[pl.ds(_tile_start(r, rows), rows), :] = dc
            return tuple(dws[j] + jnp.sum(dc * taps[j], axis=0, keepdims=True) for j in range(DN_CONV))

        zero = jnp.zeros((1, LANES), F32)
        dws = lax.fori_loop(1, n_inner, step1, step1(0, (zero,) * DN_CONV))
        for j in range(DN_CONV):
            dw_ref[j:j + 1, :] = dws[j]

        def step2(r, carry):
            start = _tile_start(r, rows)
            cur = dc_scr[pl.ds(start, rows), :]
            dx = cur * w[DN_CONV - 1:DN_CONV, :]
            for j in range(DN_CONV - 1):
                s = DN_CONV - 1 - j
                if isinstance(r, int):
                    up = _shift_up(cur, jnp.zeros((SUBLANES, LANES), F32), s)
                else:
                    up = dc_scr[pl.ds(start + s, rows), :]
                dx = dx + up * w[j:j + 1, :]
            dx_ref[pl.ds(start, rows), :] = _mx(dx)
            return carry

        lax.fori_loop(0, n_inner - 1, step2, 0)
        step2(n_inner - 1, 0)

    col = pl.BlockSpec((T, LANES), lambda j: (0, j))
    wspec = pl.BlockSpec((DN_CONV, LANES), lambda j: (0, j))
    return pl.pallas_call(body, name=name, grid=(3 * W // LANES,), in_specs=[col, wspec, col], out_specs=[col, wspec],
                          out_shape=[jax.ShapeDtypeStruct((T, 3 * W), MXU_DTYPE), jax.ShapeDtypeStruct((DN_CONV, 3 * W), F32)],
                          scratch_shapes=[pltpu.VMEM((T, LANES), F32)], compiler_params=_cp("parallel"))(proj, conv_w, dqkv)


def _softplus(x):
    return jnp.maximum(x, 0.0) + jnp.log(1.0 + jnp.exp(-jnp.abs(x)))


def _dn_gate_fwd(ba, a_log, dt_bias, name):
    T = ba.shape[0]
    tm = _pick(T, 1024, SUBLANES)

    def body(ba_ref, al_ref, dt_ref, beta_ref, g_ref):
        beta_ref[...] = _sigmoid(ba_ref[:, :LANES])
        g_ref[...] = -jnp.exp(al_ref[...]) * _softplus(ba_ref[:, LANES:] + dt_ref[...])

    row = lambda w: pl.BlockSpec((tm, w), lambda i: (i, 0))
    return pl.pallas_call(body, name=name, grid=(T // tm,), in_specs=[row(2 * LANES), _full((1, LANES)), _full((1, LANES))],
                          out_specs=[row(LANES), row(LANES)],
                          out_shape=[jax.ShapeDtypeStruct((T, LANES), F32)] * 2, compiler_params=_cp("parallel"))(ba, a_log, dt_bias)


def _dn_gate_bwd(ba, a_log, dt_bias, dbeta, dg, name):
    T = ba.shape[0]
    tm = _pick(T, 1024, SUBLANES)

    def body(ba_ref, al_ref, dt_ref, dbeta_ref, dg_ref, dba_ref, dal_ref, ddt_ref):
        @pl.when(pl.program_id(0) == 0)
        def _():
            dal_ref[...] = jnp.zeros_like(dal_ref)
            ddt_ref[...] = jnp.zeros_like(ddt_ref)

        beta = _sigmoid(ba_ref[:, :LANES])
        dba_ref[:, :LANES] = _mx(dbeta_ref[...] * beta * (1.0 - beta))
        pre = ba_ref[:, LANES:] + dt_ref[...]
        ea = jnp.exp(al_ref[...])
        dgv = dg_ref[...]
        da = dgv * (-ea) * _sigmoid(pre)
        dba_ref[:, LANES:] = _mx(da)
        ddt_ref[...] += jnp.sum(da, axis=0, keepdims=True)
        dal_ref[...] += jnp.sum(dgv * (-ea) * _softplus(pre), axis=0, keepdims=True)

    row = lambda w: pl.BlockSpec((tm, w), lambda i: (i, 0))
    one = _full((1, LANES))
    return pl.pallas_call(body, name=name, grid=(T // tm,), in_specs=[row(2 * LANES), one, one, row(LANES), row(LANES)],
                          out_specs=[row(2 * LANES), one, one],
                          out_shape=[jax.ShapeDtypeStruct((T, 2 * LANES), MXU_DTYPE), jax.ShapeDtypeStruct((1, LANES), F32),
                                     jax.ShapeDtypeStruct((1, LANES), F32)],
                          compiler_params=_cp("arbitrary"))(ba, a_log, dt_bias, dbeta, dg)


def _tri(c, strict):
    i = lax.broadcasted_iota(jnp.int32, (c, c), 0)
    j = lax.broadcasted_iota(jnp.int32, (c, c), 1)
    return (i > j) if strict else (i >= j)


def _inv_unit_lower(ls):
    c = ls[0].shape[0]
    i = lax.broadcasted_iota(jnp.int32, (c, c), 0)
    j = lax.broadcasted_iota(jnp.int32, (c, c), 1)
    eye = jnp.where(i == j, 1.0, 0.0)
    facs = [[eye - l for l in ls]]
    cur = ls
    for _ in range(int(math.log2(c)) - 1):
        cur = [_dot(p, p, NN, TRI_PREC) for p in cur]
        facs.append([eye + p for p in cur])
    while len(facs) > 1:
        nxt = [[_dot(a, b, NN, TRI_PREC) for a, b in zip(facs[t], facs[t + 1])] for t in range(0, len(facs) - 1, 2)]
        if len(facs) % 2:
            nxt.append(facs[-1])
        facs = nxt
    return facs[0]


def _chunk_gates(g_blk):
    c = g_blk.shape[0]
    gcs = _dot(jnp.where(_tri(c, False), 1.0, 0.0), g_blk, NN, HI)
    return gcs, gcs.T


def _head_chunk(h, qh, kh, vh, beta_blk, gcs, gcs_t):
    c = qh.shape[0]
    incl = _tri(c, False)
    gc_col = gcs[:, h:h + 1]
    gc_row = gcs_t[h:h + 1, :]
    gc_last = gcs_t[h:h + 1, c - 1:c]
    dec = jnp.where(incl, jnp.exp(jnp.where(incl, gc_col - gc_row, 0.0)), 0.0)
    gam = jnp.exp(gc_col)
    rr = jnp.exp(gc_last - gc_col)
    gl = jnp.exp(gc_last)
    b = beta_blk[:, h:h + 1]
    kb = kh * b
    vb = vh * b
    both = _dot(jnp.concatenate([_mx(kb), _mx(qh)], axis=0), _mx(kh), NT)
    lmat = jnp.where(_tri(c, True), both[:c] * dec, 0.0)
    pmat = jnp.where(incl, both[c:] * dec, 0.0)
    return dict(dec=dec, gam=gam, rr=rr, gl=gl, b=b, kb=kb, vb=vb, lmat=lmat, pmat=pmat)


def _solve_uw(tinv, q):
    return _dot(tinv, jnp.concatenate([q["vb"], q["kb"] * q["gam"]], axis=1), NN, TRI_PREC)


def _dn_scan_fwd(qkv, beta, g, proj, norm_g, name):
    T = qkv.shape[0]
    C, H, Dh = DN_CHUNK, DN_HEADS, DN_HEAD_DIM
    W = H * Dh
    N = T // C

    def body(q_ref, k_ref, v_ref, beta_ref, g_ref, z_ref, ng_ref, og_ref, o_ref, tinv_ref, s_ref, state):
        @pl.when(pl.program_id(0) == 0)
        def _():
            state[...] = jnp.zeros_like(state)

        gcs, gcs_t = _chunk_gates(g_ref[...])
        beta_blk = beta_ref[...]
        ng = ng_ref[...]
        heads = range(H)
        cs = [slice(h * Dh, (h + 1) * Dh) for h in heads]
        qs = [_head_chunk(h, q_ref[:, cs[h]], k_ref[:, cs[h]], v_ref[:, cs[h]], beta_blk, gcs, gcs_t) for h in heads]
        tinvs = _inv_unit_lower([q["lmat"] for q in qs])
        for h in heads:
            tinv_ref[h] = tinvs[h]
        uws = [_solve_uw(tinvs[h], qs[h]) for h in heads]
        ss = [state[h] for h in heads]
        for h in heads:
            s_ref[0, h] = ss[h]
        sbs = [_mx(s) for s in ss]
        vnbs = [_mx(uws[h][:, :Dh] - _dot(_mx(uws[h][:, Dh:]), sbs[h], NN)) for h in heads]
        os_ = [_dot(jnp.concatenate([_mx(q_ref[:, cs[h]] * qs[h]["gam"]), _mx(qs[h]["pmat"])], axis=1),
                    jnp.concatenate([sbs[h], vnbs[h]], axis=0), NN) for h in heads]
        for h in heads:
            state[h] = ss[h] * qs[h]["gl"] + _dot(_mx((k_ref[:, cs[h]] * qs[h]["rr"]).T), vnbs[h], NN)
        for h in heads:
            o = os_[h]
            o_ref[:, cs[h]] = o
            zh = z_ref[:, cs[h]]
            og_ref[:, cs[h]] = _mx(o * _rms_stat(o) * ng * (zh * _sigmoid(zh)))

    blk = lambda j: pl.BlockSpec((C, W), lambda n: (n, j))
    small = pl.BlockSpec((C, LANES), lambda n: (n, 0))
    return pl.pallas_call(
        body, name=name, grid=(N,),
        in_specs=[blk(0), blk(1), blk(2), small, small, blk(3), _full((1, Dh))],
        out_specs=[blk(0), blk(0), pl.BlockSpec((H, C, C), lambda n: (0, n, 0)),
                   pl.BlockSpec((1, H, Dh, Dh), lambda n: (n, 0, 0, 0))],
        out_shape=[jax.ShapeDtypeStruct((T, W), MXU_DTYPE), jax.ShapeDtypeStruct((T, W), F32),
                   jax.ShapeDtypeStruct((H, T, C), F32), jax.ShapeDtypeStruct((N, H, Dh, Dh), F32)],
        scratch_shapes=[pltpu.VMEM((H, Dh, Dh), F32)],
        compiler_params=_cp("arbitrary"),
    )(qkv, qkv, qkv, beta, g, proj, norm_g)


def _dn_scan_bwd(qkv, beta, g, proj, norm_g, o, tinv, s_all, dog, name):
    T = qkv.shape[0]
    C, H, Dh = DN_CHUNK, DN_HEADS, DN_HEAD_DIM
    W = H * Dh
    N = T // C

    def body(q_ref, k_ref, v_ref, beta_ref, g_ref, z_ref, ng_ref, o_ref, tinv_ref, s_ref, dog_ref,
             dqkv_ref, dbeta_ref, dg_ref, dz_ref, dng_ref, dstate):
        @pl.when(pl.program_id(0) == 0)
        def _():
            dstate[...] = jnp.zeros_like(dstate)
            dng_ref[...] = jnp.zeros_like(dng_ref)

        gcs, gcs_t = _chunk_gates(g_ref[...])
        beta_blk = beta_ref[...]
        ng = ng_ref[...]
        incl = _tri(C, False)
        strict = _tri(C, True)
        lane = lax.broadcasted_iota(jnp.int32, (C, LANES), 1)
        rowi = lax.broadcasted_iota(jnp.int32, (C, 1), 0)
        headrow = lax.broadcasted_iota(jnp.int32, (LANES, C), 0)
        colsums = jnp.zeros((LANES, C), F32)
        dbeta_acc = jnp.zeros((C, LANES), F32)
        dgc_acc = jnp.zeros((C, LANES), F32)
        dng_acc = jnp.zeros((1, Dh), F32)
        cs = [slice(h * Dh, (h + 1) * Dh) for h in range(H)]
        rsum = lambda t: jnp.sum(t, axis=1, keepdims=True)
        for heads in (range(0, H // 2), range(H // 2, H)):
            dobs = {}
            for h in heads:
                oh, zh, dogh = o_ref[:, cs[h]], z_ref[:, cs[h]], dog_ref[:, cs[h]]
                rstat = _rms_stat(oh)
                sz = _sigmoid(zh)
                dz_ref[:, cs[h]] = _mx(dogh * (oh * rstat * ng) * (sz * (1.0 + zh * (1.0 - sz))))
                do, dng = _rms_bwd(oh, rstat, ng, dogh * (zh * sz))
                dng_acc = dng_acc + dng
                dobs[h] = _mx(do)
            qs = {h: _head_chunk(h, q_ref[:, cs[h]], k_ref[:, cs[h]], v_ref[:, cs[h]], beta_blk, gcs, gcs_t) for h in heads}
            tms = {h: tinv_ref[h] for h in heads}
            uws = {h: _solve_uw(tms[h], qs[h]) for h in heads}
            ss = {h: s_ref[0, h] for h in heads}
            sbs = {h: _mx(ss[h]) for h in heads}
            wbs = {h: _mx(uws[h][:, Dh:]) for h in heads}
            vnbs = {h: _mx(uws[h][:, :Dh] - _dot(wbs[h], sbs[h], NN)) for h in heads}
            dsns = {h: dstate[h] for h in heads}
            dsbs = {h: _mx(dsns[h]) for h in heads}
            dvnews = {h: _dot(_mx(qs[h]["pmat"]), dobs[h], TN) + _dot(_mx(k_ref[:, cs[h]] * qs[h]["rr"]), dsbs[h], NN)
                      for h in heads}
            dvb16s = {h: _mx(dvnews[h]) for h in heads}
            dps = {h: jnp.where(incl, _dot(dobs[h], vnbs[h], NT), 0.0) for h in heads}
            dqds = {h: _dot(dobs[h], sbs[h], NT) for h in heads}
            dkds = {h: _dot(vnbs[h], dsbs[h], NT) for h in heads}
            dgls = {h: jnp.sum(rsum(ss[h] * dsns[h]), axis=0, keepdims=True) for h in heads}
            dws = {h: -_dot(dvb16s[h], sbs[h], NT) for h in heads}
            for h in heads:
                dstate[h] = qs[h]["gl"] * dsns[h] + _dot(
                    jnp.concatenate([_mx(q_ref[:, cs[h]] * qs[h]["gam"]), -wbs[h]], axis=0),
                    jnp.concatenate([dobs[h], dvb16s[h]], axis=0), TN)
            dsols = {h: _dot(tms[h], jnp.concatenate([dvnews[h], dws[h]], axis=1), TN, TRI_PREC) for h in heads}
            dvbs = {h: dsols[h][:, :Dh] for h in heads}
            dkbgs = {h: dsols[h][:, Dh:] for h in heads}
            dls = {h: jnp.where(strict, -_dot(dsols[h], uws[h], NT, TRI_PREC), 0.0) for h in heads}
            mmats = {h: dls[h] * qs[h]["lmat"] + dps[h] * qs[h]["pmat"] for h in heads}
            dgcs = {h: rsum(mmats[h]) for h in heads}
            for h in heads:
                colsums = jnp.where(headrow == h, jnp.sum(mmats[h], axis=0, keepdims=True), colsums)
            dboth = {h: jnp.concatenate([_mx(dls[h] * qs[h]["dec"]), _mx(dps[h] * qs[h]["dec"])], axis=0) for h in heads}
            for h in heads:
                q = qs[h]
                qh, kh, vh = q_ref[:, cs[h]], k_ref[:, cs[h]], v_ref[:, cs[h]]
                gam, rr, b, kb = q["gam"], q["rr"], q["b"], q["kb"]
                on_k = _dot(dboth[h], _mx(kh), NN)
                dkb = on_k[:C] + dkbgs[h] * gam
                dk = _dot(dboth[h], jnp.concatenate([_mx(kb), _mx(qh)], axis=0), TN) + dkb * b + dkds[h] * rr
                dq = on_k[C:] + dqds[h] * gam
                dgam = rsum(dkbgs[h] * kb) + rsum(dqds[h] * qh)
                dr = rsum(dkds[h] * kh)
                dgc_last = jnp.sum(dr * rr, axis=0, keepdims=True) + dgls[h] * q["gl"]
                dgc = dgcs[h] + dgam * gam - dr * rr + jnp.where(rowi == C - 1, dgc_last, 0.0)
                dbeta = rsum(dvbs[h] * vh) + rsum(dkb * kh)
                dqkv_ref[:, cs[h]] = dq
                dqkv_ref[:, W + h * Dh:W + (h + 1) * Dh] = dk
                dqkv_ref[:, 2 * W + h * Dh:2 * W + (h + 1) * Dh] = dvbs[h] * b
                dbeta_acc = jnp.where(lane == h, dbeta, dbeta_acc)
                dgc_acc = jnp.where(lane == h, dgc, dgc_acc)
        dbeta_ref[...] = dbeta_acc
        dg_ref[...] = _dot(jnp.where(incl, 1.0, 0.0), dgc_acc - colsums.T, TN, HI)
        dng_ref[...] += dng_acc

    rev = lambda n: N - 1 - n
    blk = lambda j: pl.BlockSpec((C, W), lambda n: (rev(n), j))
    small = pl.BlockSpec((C, LANES), lambda n: (rev(n), 0))
    return pl.pallas_call(
        body, name=name, grid=(N,),
        in_specs=[blk(0), blk(1), blk(2), small, small, blk(3), _full((1, Dh)), blk(0),
                  pl.BlockSpec((H, C, C), lambda n: (0, rev(n), 0)),
                  pl.BlockSpec((1, H, Dh, Dh), lambda n: (rev(n), 0, 0, 0)), blk(0)],
        out_specs=[pl.BlockSpec((C, 3 * W), lambda n: (rev(n), 0)), small, small, blk(0), _full((1, Dh))],
        out_shape=[jax.ShapeDtypeStruct((T, 3 * W), F32), jax.ShapeDtypeStruct((T, LANES), F32),
                   jax.ShapeDtypeStruct((T, LANES), F32), jax.ShapeDtypeStruct((T, W), MXU_DTYPE),
                   jax.ShapeDtypeStruct((1, Dh), F32)],
        scratch_shapes=[pltpu.VMEM((H, Dh, Dh), F32)],
        compiler_params=_cp("arbitrary"),
    )(qkv, qkv, qkv, beta, g, proj, norm_g, o, tinv, s_all, dog)


_INV_SQRT2 = 0.7071067811865476
_INV_SQRT_2PI = 0.3989422804014327


def _sg_recompute(zp_ref, bin_ref, lng_ref, lnb_ref):
    E = SG_WIDTH
    zin = zp_ref[...] + bin_ref[...]
    cdf = 0.5 * (1.0 + lax.erf(zin * _INV_SQRT2))
    zz = zin * cdf
    u = zz[:, :E]
    vp = zz[:, E:]
    mu = jnp.mean(vp, axis=-1, keepdims=True)
    xc = vp - mu
    rstd = lax.rsqrt(jnp.mean(xc * xc, axis=-1, keepdims=True) + LN_EPS)
    xhat = xc * rstd
    v = xhat * lng_ref[...] + lnb_ref[...]
    return zin, cdf, u, xhat, rstd, v


def _sg_masked_ws(ws_ref, g):
    return _mx(jnp.where(_tri(SG_CHUNK, False), ws_ref[g], 0.0))


def _sg_fwd(zpre, b_in, ln_g, ln_b, w_s, b_s_t, name):
    T = zpre.shape[0]
    E, G, C, GW = SG_WIDTH, SG_GROUPS, SG_CHUNK, SG_GROUP_W

    def body(zp_ref, bin_ref, lng_ref, lnb_ref, ws_ref, bst_ref, um_ref):
        _, _, u, _, _, v = _sg_recompute(zp_ref, bin_ref, lng_ref, lnb_ref)
        bst = bst_ref[...]
        for g in range(G):
            cs = slice(g * GW, (g + 1) * GW)
            mixed = _dot(_sg_masked_ws(ws_ref, g), _mx(v[:, cs]), NN) + bst[:, g:g + 1]
            um_ref[:, cs] = _mx(u[:, cs] * mixed)

    return pl.pallas_call(
        body, name=name, grid=(T // C,),
        in_specs=[pl.BlockSpec((C, 2 * E), lambda n: (n, 0)), _full((1, 2 * E)), _full((1, E)), _full((1, E)),
                  _full((G, C, C)), _full((C, LANES))],
        out_specs=pl.BlockSpec((C, E), lambda n: (n, 0)),
        out_shape=jax.ShapeDtypeStruct((T, E), MXU_DTYPE), compiler_params=_cp("parallel"),
    )(zpre, b_in, ln_g, ln_b, w_s, b_s_t)


def _sg_bwd(zpre, b_in, ln_g, ln_b, w_s, b_s_t, dum, name):
    T = zpre.shape[0]
    E, G, C, GW = SG_WIDTH, SG_GROUPS, SG_CHUNK, SG_GROUP_W

    def body(zp_ref, bin_ref, lng_ref, lnb_ref, ws_ref, bst_ref, dum_ref,
             dz_ref, dbin_ref, dlng_ref, dlnb_ref, dws_ref, dbst_ref):
        @pl.when(pl.program_id(0) == 0)
        def _():
            for r in (dbin_ref, dlng_ref, dlnb_ref, dws_ref, dbst_ref):
                r[...] = jnp.zeros_like(r)

        zin, cdf, u, xhat, rstd, v = _sg_recompute(zp_ref, bin_ref, lng_ref, lnb_ref)
        bst = bst_ref[...]
        lane = lax.broadcasted_iota(jnp.int32, (C, LANES), 1)
        dum_v = dum_ref[...]
        dbst = jnp.zeros((C, LANES), F32)
        du_parts, dv_parts = [], []
        for g in range(G):
            cs = slice(g * GW, (g + 1) * GW)
            wsm = _sg_masked_ws(ws_ref, g)
            vg = _mx(v[:, cs])
            mixed = _dot(wsm, vg, NN) + bst[:, g:g + 1]
            dumg = dum_v[:, cs]
            du_parts.append(dumg * mixed)
            dmixed = dumg * u[:, cs]
            dmb = _mx(dmixed)
            dv_parts.append(_dot(wsm, dmb, TN))
            dws_ref[g] += _dot(dmb, vg, NT)
            dbst = jnp.where(lane == g, jnp.sum(dmixed, axis=1, keepdims=True), dbst)
        dbst_ref[...] += dbst
        du = jnp.concatenate(du_parts, axis=1)
        dv = jnp.concatenate(dv_parts, axis=1)
        dlng_ref[...] += jnp.sum(dv * xhat, axis=0, keepdims=True)
        dlnb_ref[...] += jnp.sum(dv, axis=0, keepdims=True)
        dxh = dv * lng_ref[...]
        dvp = rstd * (dxh - jnp.mean(dxh, axis=-1, keepdims=True) - xhat * jnp.mean(dxh * xhat, axis=-1, keepdims=True))
        dzz = jnp.concatenate([du, dvp], axis=1)
        dzin = dzz * (cdf + zin * (_INV_SQRT_2PI * jnp.exp(-0.5 * zin * zin)))
        dz_ref[...] = _mx(dzin)
        dbin_ref[...] += jnp.sum(dzin, axis=0, keepdims=True)

    return pl.pallas_call(
        body, name=name, grid=(T // C,),
        in_specs=[pl.BlockSpec((C, 2 * E), lambda n: (n, 0)), _full((1, 2 * E)), _full((1, E)), _full((1, E)),
                  _full((G, C, C)), _full((C, LANES)), pl.BlockSpec((C, E), lambda n: (n, 0))],
        out_specs=[pl.BlockSpec((C, 2 * E), lambda n: (n, 0)), _full((1, 2 * E)), _full((1, E)), _full((1, E)),
                   _full((G, C, C)), _full((C, LANES))],
        out_shape=[jax.ShapeDtypeStruct((T, 2 * E), MXU_DTYPE), jax.ShapeDtypeStruct((1, 2 * E), F32),
                   jax.ShapeDtypeStruct((1, E), F32), jax.ShapeDtypeStruct((1, E), F32),
                   jax.ShapeDtypeStruct((G, C, C), F32), jax.ShapeDtypeStruct((C, LANES), F32)],
        compiler_params=_cp("arbitrary"),
    )(zpre, b_in, ln_g, ln_b, w_s, b_s_t, dum)


def _row(v):
    return v.reshape(1, -1)


def _pad_lanes(v):
    v = v.reshape(1, -1)
    return jnp.pad(v, ((0, 0), (0, LANES - v.shape[1])))


def _local_step(x, target, p, weights_for, grads_ready=None, small_ready=None):
    ng = p["norm_g"]
    grads = {}
    dng = [[None] * 6 for _ in range(2)]
    order = [jnp.zeros((), F32)]

    def tell(group):
        zero = grads_ready(group, grads) if grads_ready is not None else None
        if zero is not None:
            order[0] = zero

    def gain(i, s):
        return _row(ng[i, s]) + order[0]

    def ffn_f(xin, i, j, tag, **tail):
        wt = weights_for("ffn" + tag, xin)
        xo, h, gu, y, *rest = _ffn_fwd(xin, _row(ng[i, 4 * j]), _row(ng[i, 4 * j + 1]), wt, "ffn_fwd_" + tag, **tail)
        return (xo, *rest), (xin, h, gu, y, wt)

    (x1, hn0), sv_f00 = ffn_f(x, 0, 0, "00", next_gain=_row(ng[0, 2]))
    dnw = weights_for("dn", x1)
    proj = _mm(hn0, dnw["dn_wqkvz"], "nn", "dn_proj")
    ba = _mm(hn0, dnw["dn_wba"], "nn", "dn_proj_ba")
    a_log = _pad_lanes(p["dn_a_log"])
    dt_bias = _pad_lanes(p["dn_dt_bias"])
    dn_ng = _row(p["dn_norm_g"])
    qkv = _dn_prep_fwd(proj, p["dn_conv_w"], "dn_prep_fwd")
    beta, gdec = _dn_gate_fwd(ba, a_log, dt_bias, "dn_gate_fwd")
    og, o_raw, tinv, s_all = _dn_scan_fwd(qkv, beta, gdec, proj, dn_ng, "dn_scan_fwd")
    m0, x2 = _out_proj_postnorm(og, dnw["dn_wout"], x1, _row(ng[0, 3]), "dn_out")
    (x3,), sv_f01 = ffn_f(x2, 0, 1, "01")
    (x4, hn1), sv_f10 = ffn_f(x3, 1, 0, "10", next_gain=_row(ng[1, 2]))
    sgw = weights_for("sg", x4)
    zpre = _mm(hn1, sgw["sg_win"], "nn", "sg_proj")
    sg_bin = _row(p["sg_b_in"])
    sg_lng = _row(p["sg_ln_g"])
    sg_lnb = _row(p["sg_ln_b"])
    sg_bst = jnp.pad(p["sg_b_s"].T, ((0, 0), (0, LANES - SG_GROUPS)))
    um = _sg_fwd(zpre, sg_bin, sg_lng, sg_lnb, p["sg_w_s"], sg_bst, "sg_fwd")
    m1, x5 = _out_proj_postnorm(um, sgw["sg_wout"], x4, _row(ng[1, 3]), "sg_out")
    (_, loss_part, dx), sv_f11 = ffn_f(x5, 1, 1, "11", target=target)

    def ffn_b(dxo, sv, i, j, tag, last=False):
        xin, h, gu, y, wt = sv
        dxi, dy, a, dgu, dg0, dg1 = _ffn_bwd(dxo, xin, y, gu, gain(i, 4 * j), gain(i, 4 * j + 1), wt, "ffn_bwd_" + tag)
        dng[i][4 * j] = dg0
        dng[i][4 * j + 1] = dg1
        after = None
        if last:
            grads["norm_g"] = jnp.stack([jnp.concatenate(dng[t], axis=0) for t in range(2)])
            after = small_ready(grads, loss_part) if small_ready is not None else None
        grads["wd" + tag] = _mm(a, dy, "tn", "ffn_wgrad_down_" + tag, after=after)
        grads["wguT" + tag] = _mm(dgu, h, "tn", "ffn_wgrad_up_" + tag, after=after)
        tell("ffn" + tag)
        return dxi

    dx = ffn_b(dx, sv_f11, 1, 1, "11")
    dm1, dng[1][3], dum = _postnorm_bwd_dgrad(dx, m1, gain(1, 3), sgw["sg_wout"], "sg_dgrad_out")
    grads["sg_w_out"] = _mm(um, dm1, "tn", "sg_wgrad_out")
    dz1, dbin, dlng, dlnb, dws, dbst = _sg_bwd(zpre, sg_bin, sg_lng, sg_lnb, p["sg_w_s"], sg_bst, dum, "sg_bwd")
    grads["sg_w_inT"] = _mm(dz1, hn1, "tn", "sg_wgrad_in")
    tell("sg")
    dx, dng[1][2] = _dgrad_prenorm_bwd(dz1, sgw["sg_win"], None, dx, x4, gain(1, 2), "sg_dgrad_in")
    grads["sg_b_in"] = dbin.reshape(1, -1)
    grads["sg_ln_g"] = dlng.reshape(1, -1)
    grads["sg_ln_b"] = dlnb.reshape(1, -1)
    grads["sg_w_s"] = jnp.where(jnp.tril(jnp.ones((SG_CHUNK, SG_CHUNK), bool)), dws, 0.0)[None]
    grads["sg_b_s"] = dbst[:, :SG_GROUPS].T[None]
    dx = ffn_b(dx, sv_f10, 1, 0, "10")
    dx = ffn_b(dx, sv_f01, 0, 1, "01")
    dm0, dng[0][3], dog = _postnorm_bwd_dgrad(dx, m0, gain(0, 3), dnw["dn_wout"], "dn_dgrad_out")
    grads["dn_w_out"] = _mm(og, dm0, "tn", "dn_wgrad_out")
    dqkv, dbeta, dgdec, dz0, dnng = _dn_scan_bwd(qkv, beta, gdec, proj, dn_ng, o_raw, tinv, s_all, dog, "dn_scan_bwd")
    dqkv_pre, dconv = _dn_prep_bwd(proj, p["dn_conv_w"], dqkv, "dn_prep_bwd")
    dba, dal, ddt = _dn_gate_bwd(ba, a_log, dt_bias, dbeta, dgdec, "dn_gate_bwd")
    W3 = 3 * DN_HEADS * DN_HEAD_DIM
    dw_qkv = _mm(hn0, dqkv_pre, "tn", "dn_wgrad_qkv")
    dw_z = _mm(hn0, dz0, "tn", "dn_wgrad_z")
    dw_ba = _mm(hn0, dba, "tn", "dn_wgrad_ba")
    grads["dn_w_in"] = jnp.concatenate(
        [dw_qkv, dw_z, dw_ba[:, :DN_HEADS], dw_ba[:, LANES:LANES + DN_HEADS]], axis=1)
    tell("dn")
    dh0 = _mm(dqkv_pre, dnw["dn_wqkvz"][:, :W3], "nt", "dn_dgrad_qkv")
    dh0 = _mm(dz0, dnw["dn_wqkvz"][:, W3:], "nt", "dn_dgrad_z", add=dh0)
    dx, dng[0][2] = _dgrad_prenorm_bwd(dba, dnw["dn_wba"], dh0, dx, x1, gain(0, 2), "dn_dgrad_ba")
    grads["dn_conv_w"] = dconv[None]
    grads["dn_a_log"] = dal[:, :DN_HEADS]
    grads["dn_dt_bias"] = ddt[:, :DN_HEADS]
    grads["dn_norm_g"] = dnng
    dx = ffn_b(dx, sv_f00, 0, 0, "00", last=True)
    return loss_part, dx, grads


def _mesh_pos():
    return lax.axis_index("x"), lax.axis_index("y"), lax.axis_index("c")


def _other_chips(x, y):
    return [(1 - x, y), (x, 1 - y), (1 - x, 1 - y)]


def _allgather_chips(arrs, name):
    n = len(arrs)

    def body(*refs):
        ins, outs = refs[:n], refs[n:2 * n]
        ici_send, ici_recv, d2d_send, d2d_recv = refs[2 * n:]
        x, y, c = _mesh_pos()
        me = 2 * x + y
        chips = _other_chips(x, y)
        sibling = (x, y, 1 - c)

        def ici(i, j, k):
            cx, cy = chips[j]
            return pltpu.make_async_remote_copy(src_ref=ins[i].at[c], dst_ref=outs[i].at[k, c], send_sem=ici_send.at[3 * i + j],
                                                recv_sem=ici_recv.at[3 * i + j], device_id=(cx, cy, c), device_id_type=MESH)

        def d2d(i, j, h):
            cx, cy = chips[j]
            slot = outs[i].at[2 * cx + cy, h]
            return pltpu.make_async_remote_copy(src_ref=slot, dst_ref=slot, send_sem=d2d_send.at[3 * i + j],
                                                recv_sem=d2d_recv.at[3 * i + j], device_id=sibling, device_id_type=MESH)

        sends = [ici(i, j, me) for i in range(n) for j in range(3)]
        for cp in sends:
            cp.start()
        for i in range(n):
            for j, (cx, cy) in enumerate(chips):
                ici(i, j, 2 * cx + cy).wait_recv()
                fwd = d2d(i, j, c)
                fwd.start()
                sends.append(fwd)
        for i in range(n):
            for j in range(3):
                d2d(i, j, 1 - c).wait_recv()
        for cp in sends:
            cp.wait_send()

    return pl.pallas_call(
        body, name=name, in_specs=[ANY] * n, out_specs=[ANY] * n,
        out_shape=[jax.ShapeDtypeStruct((N_CHIPS,) + a.shape, a.dtype) for a in arrs],
        scratch_shapes=[pltpu.SemaphoreType.DMA((3 * n,))] * 4,
    )(*arrs)


HBM = pl.BlockSpec(memory_space=pltpu.HBM)
SEM = pl.BlockSpec(memory_space=pltpu.SEMAPHORE)
TOKEN = jax.ShapeDtypeStruct((SUBLANES, LANES), F32)


_PEERS = {"gather": 3, "scatter": 3, "swap": 1, "all": N_DEV - 1}


def _land_shape(kind, shape):
    if kind == "gather":
        return (N_CHIPS,) + shape
    if kind == "all":
        return (N_DEV,) + shape
    return (N_CHIPS,) + shape[2:] if kind == "swap" else shape


def _peer_copies(kind, flags, src_refs, land_refs, send_sems, recv_sems, receiving):
    x, y, c = _mesh_pos()
    me4, me8 = 2 * x + y, 4 * x + 2 * y + c
    np_ = _PEERS[kind]
    cps = []
    for i, (src, land) in enumerate(zip(src_refs, land_refs)):
        if kind == "swap":
            half = src.at[1 - c] if flags[i] else src.at[:, 1 - c]
            plan = [((x, y, 1 - c), half, land)]
        elif kind == "all":
            masks = [(mx, my, mc) for mx in (0, 1) for my in (0, 1) for mc in (0, 1)][1:]
            peers = [(jnp.where(mx, 1 - x, x), jnp.where(my, 1 - y, y), jnp.where(mc, 1 - c, c)) for mx, my, mc in masks]
            plan = [(p, src, land.at[4 * p[0] + 2 * p[1] + p[2] if receiving else me8]) for p in peers]
        else:
            plan = []
            for cx, cy in _other_chips(x, y):
                k = 2 * cx + cy
                s = src.at[me4 if receiving else k] if kind == "scatter" else src
                plan.append(((cx, cy, c), s, land.at[k if receiving else me4]))
        for j, (peer, s, d) in enumerate(plan):
            cps.append(pltpu.make_async_remote_copy(src_ref=s, dst_ref=d, send_sem=send_sems.at[np_ * i + j],
                                                    recv_sem=recv_sems.at[np_ * i + j], device_id=peer, device_id_type=MESH))
    return cps


def _copies_start(kind, srcs, after, name, flags=None):
    n = len(srcs)
    ns = _PEERS[kind] * n
    lands = [lax.empty(_land_shape(kind, s.shape), s.dtype) for s in srcs]
    after = [] if after is None else [after]

    def body(*refs):
        src_refs, land_refs = refs[:n], refs[n:2 * n]
        send_sems, recv_sems = refs[2 * n + len(after)], refs[2 * n + len(after) + 1]
        token = refs[-1]
        for cp in _peer_copies(kind, flags, src_refs, land_refs, send_sems, recv_sems, False):
            cp.start()
        token[...] = jnp.zeros_like(token)

    outs = pl.pallas_call(
        body, name=name,
        in_specs=[HBM] * (2 * n) + [ANY] * len(after),
        out_specs=(SEM, SEM) + (HBM,) * (2 * n) + (pl.BlockSpec(memory_space=pltpu.VMEM),),
        out_shape=(pltpu.SemaphoreType.DMA((ns,)), pltpu.SemaphoreType.DMA((ns,)))
        + tuple(pltpu.HBM(a.shape, a.dtype) for a in list(srcs) + lands) + (TOKEN,),
        input_output_aliases={i: 2 + i for i in range(2 * n)},
        compiler_params=pltpu.CompilerParams(has_side_effects=pltpu.SideEffectType.DATAFLOW_SIDE_EFFECTING),
    )(*[pltpu.with_memory_space_constraint(a, pltpu.HBM) for a in list(srcs) + lands], *after)
    return dict(sems=outs[:2], srcs=outs[2:2 + n], lands=outs[2 + n:2 + 2 * n], token=outs[-1], kind=kind, flags=flags)


def _copies_wait(started, after, name):
    n = len(started["srcs"])
    kind, flags = started["kind"], started["flags"]
    after = list(after) if isinstance(after, (list, tuple)) else [after]

    def body(*refs):
        src_refs, land_refs = refs[:n], refs[n:2 * n]
        send_sems, recv_sems = refs[2 * n], refs[2 * n + 1]
        for cp in _peer_copies(kind, flags, src_refs, land_refs, send_sems, recv_sems, True):
            cp.wait_send()
            cp.wait_recv()

    outs = pl.pallas_call(
        body, name=name,
        in_specs=[HBM] * (2 * n) + [SEM, SEM] + [ANY] * len(after),
        out_specs=(HBM,) * (2 * n),
        out_shape=tuple(pltpu.HBM(a.shape, a.dtype) for a in list(started["srcs"]) + list(started["lands"])),
        input_output_aliases={i: i for i in range(2 * n)},
        compiler_params=pltpu.CompilerParams(has_side_effects=pltpu.SideEffectType.DATAFLOW_SIDE_EFFECTING),
    )(*started["srcs"], *started["lands"], *started["sems"], *after)
    return outs[:n], outs[n:]


def _swap_whole(arrs, name):
    n = len(arrs)

    def body(*refs):
        ins, outs = refs[:n], refs[n:2 * n]
        send_sems, recv_sems = refs[2 * n:]
        x, y, c = _mesh_pos()
        cps = [pltpu.make_async_remote_copy(src_ref=ins[i], dst_ref=outs[i], send_sem=send_sems.at[i],
                                            recv_sem=recv_sems.at[i], device_id=(x, y, 1 - c), device_id_type=MESH)
               for i in range(n)]
        for cp in cps:
            cp.start()
        for cp in cps:
            cp.wait()

    return pl.pallas_call(
        body, name=name, in_specs=[ANY] * n, out_specs=[ANY] * n,
        out_shape=[jax.ShapeDtypeStruct(a.shape, a.dtype) for a in arrs],
        scratch_shapes=[pltpu.SemaphoreType.DMA((n,)), pltpu.SemaphoreType.DMA((n,))],
    )(*arrs)


def _as_rows(a, lead):
    shp = a.shape
    rows = 1
    for s in shp[lead:-1]:
        rows *= s
    return a.reshape(shp[:lead] + (rows, shp[-1]))


def _row_tile(rows, cols, n_bufs):
    budget = (24 * 1024 * 1024) // (n_bufs * 2 * 4 * cols)
    return _pick(rows, max(2 * SUBLANES, budget), 2 * SUBLANES)


def _sum_devices(own, got, dev, name):
    n, rows, cols = got.shape
    tr = _row_tile(rows, cols, n + 2)

    def body(dev_ref, own_ref, got_ref, o_ref):
        mine = own_ref[...]
        acc = jnp.where(dev_ref[0] == 0, mine, got_ref[0])
        for k in range(1, n):
            acc = acc + jnp.where(dev_ref[0] == k, mine, got_ref[k])
        o_ref[...] = acc

    return pl.pallas_call(
        body, name=name,
        grid_spec=pltpu.PrefetchScalarGridSpec(
            num_scalar_prefetch=1, grid=(rows // tr,),
            in_specs=[pl.BlockSpec((tr, cols), lambda i, d: (i, 0)), pl.BlockSpec((n, tr, cols), lambda i, d: (0, i, 0))],
            out_specs=pl.BlockSpec((tr, cols), lambda i, d: (i, 0))),
        out_shape=jax.ShapeDtypeStruct((rows, cols), F32), compiler_params=_cp("parallel"),
    )(_scalar(dev), own, got)


def _scalar(i):
    return jnp.reshape(i, (1,)).astype(jnp.int32)


def _add_own_half(g, other, c, half_first, name):
    _, rows, cols = other.shape
    tr = _row_tile(rows, cols, 3)

    def body(c_ref, g_ref, o_ref, out_ref):
        out_ref[0] = (g_ref[0, 0] + o_ref[0]).astype(out_ref.dtype)

    if half_first:
        g_map = lambda k, i, c_ref: (c_ref[0], k, i, 0)
    else:
        g_map = lambda k, i, c_ref: (k, c_ref[0], i, 0)
    flat = pl.BlockSpec((1, tr, cols), lambda k, i, c_ref: (k, i, 0))
    return pl.pallas_call(
        body, name=name,
        grid_spec=pltpu.PrefetchScalarGridSpec(
            num_scalar_prefetch=1, grid=(N_CHIPS, rows // tr),
            in_specs=[pl.BlockSpec((1, 1, tr, cols), g_map), flat], out_specs=flat),
        out_shape=jax.ShapeDtypeStruct(other.shape, COMM_DTYPE), compiler_params=_cp("parallel", "parallel"),
    )(_scalar(c), g, other)


def _sum_chips(own, got, chip, name, transpose=False):
    _, rows, cols = own.shape
    tr = rows if transpose else _row_tile(rows, cols, N_CHIPS + 2)

    def body(chip_ref, p_ref, b_ref, o_ref):
        mine = p_ref[0].astype(F32)
        acc = jnp.where(chip_ref[0] == 0, mine, b_ref[0].astype(F32))
        for k in range(1, N_CHIPS):
            acc = acc + jnp.where(chip_ref[0] == k, mine, b_ref[k].astype(F32))
        o_ref[...] = acc.T if transpose else acc

    if transpose:
        out_spec, out_shape = pl.BlockSpec((cols, rows), lambda i, k_ref: (0, 0)), (cols, rows)
    else:
        out_spec, out_shape = pl.BlockSpec((tr, cols), lambda i, k_ref: (i, 0)), (rows, cols)
    return pl.pallas_call(
        body, name=name,
        grid_spec=pltpu.PrefetchScalarGridSpec(
            num_scalar_prefetch=1, grid=(rows // tr,),
            in_specs=[pl.BlockSpec((1, tr, cols), lambda i, k_ref: (k_ref[0], i, 0)),
                      pl.BlockSpec((N_CHIPS, tr, cols), lambda i, k_ref: (0, i, 0))],
            out_specs=out_spec),
        out_shape=jax.ShapeDtypeStruct(out_shape, F32), compiler_params=_cp("parallel"),
    )(_scalar(chip), own, got)


def _adam_math(w, g, m, v):
    nm = ADAM_B1 * m + (1.0 - ADAM_B1) * g
    nv = ADAM_B2 * v + (1.0 - ADAM_B2) * (g * g)
    m_hat = nm / (1.0 - ADAM_B1 ** ADAM_STEP)
    v_hat = nv / (1.0 - ADAM_B2 ** ADAM_STEP)
    return -ADAM_LR * (m_hat / (jnp.sqrt(v_hat) + ADAM_EPS) + ADAM_WD * w), nm, nv


def _adamw_pieces(w, m, v, mine, theirs, c, kind, name):
    shape = w.shape
    P = len(mine)
    ws, ms, vs = (t.reshape((P, -1, t.shape[-1])) for t in (w, m, v))
    _, R, C = ws.shape
    if kind == "rows":
        tr = _pick(R // 2, 512, SUBLANES)
    else:
        tr = _pick(R, 256 if kind in ("lo", "hi") else 512, SUBLANES)
    nt = R // tr
    nh = nt // 2

    def body(c_ref, w_ref, m_ref, v_ref, *refs):
        mine_refs, theirs_refs = refs[:P], refs[P:2 * P]
        g_ref, d_ref, nm_ref, nv_ref = refs[2 * P:]
        p, i, core = pl.program_id(0), pl.program_id(1), c_ref[0]

        def pick(refs_):
            out = refs_[0][...]
            for q in range(1, P):
                out = jnp.where(p == q, refs_[q][...], out)
            return out

        a, b = pick(mine_refs), pick(theirs_refs)
        if kind == "cols":
            gv = jnp.where(core == 0, jnp.concatenate([a, b], axis=1), jnp.concatenate([b, a], axis=1))
        else:
            own = {"lo": core == 0, "hi": core == 1, "rows": (i >= nh) == (core == 1)}[kind]
            gv = jnp.where(own, a, b)
        g_ref[0] = gv
        d_ref[0], nm_ref[0], nv_ref[0] = _adam_math(w_ref[0], gv, m_ref[0], v_ref[0])

    def piece_spec(q):
        tile = (lambda i: i - jnp.where(i >= nh, nh, 0)) if kind == "rows" else (lambda i: i)
        return pl.BlockSpec((tr, mine[q].shape[1]), lambda p, i, c_ref: (jnp.where(p == q, tile(i), 0), 0))

    full = pl.BlockSpec((1, tr, C), lambda p, i, c_ref: (p, i, 0))
    outs = pl.pallas_call(
        body, name=name,
        grid_spec=pltpu.PrefetchScalarGridSpec(num_scalar_prefetch=1, grid=(P, nt),
                                               in_specs=[full] * 3 + [piece_spec(q) for q in range(P)] * 2,
                                               out_specs=[full] * 4),
        out_shape=[jax.ShapeDtypeStruct((P, R, C), F32)] * 4, compiler_params=_cp("parallel", "arbitrary"),
    )(_scalar(c), ws, ms, vs, *mine, *theirs)
    return tuple(o.reshape(shape) for o in outs)


def _adamw(w, g, m, v, name):
    shape = w.shape
    ws, gs, ms, vs = (_as_rows(t, 0) for t in (w, g, m, v))
    rows, cols = ws.shape
    tr = _row_tile(rows, cols, 7)

    def body(w_ref, g_ref, m_ref, v_ref, d_ref, nm_ref, nv_ref):
        d_ref[...], nm_ref[...], nv_ref[...] = _adam_math(w_ref[...], g_ref[...], m_ref[...], v_ref[...])

    spec = pl.BlockSpec((tr, cols), lambda i: (i, 0))
    outs = pl.pallas_call(body, name=name, grid=(rows // tr,), in_specs=[spec] * 4, out_specs=[spec] * 3,
                          out_shape=[jax.ShapeDtypeStruct((rows, cols), F32)] * 3, compiler_params=_cp("parallel"))(ws, gs, ms, vs)
    return tuple(o.reshape(shape) for o in outs)


_BIG = ["ffn_w_gate", "ffn_w_up", "ffn_w_down", "dn_w_in", "dn_w_out", "sg_w_in", "sg_w_out"]
_SMALL_SHARDED = ["norm_g", "dn_conv_w", "sg_b_in", "sg_ln_g", "sg_ln_b"]
_SMALL_REPL = ["dn_a_log", "dn_dt_bias", "dn_norm_g", "sg_w_s", "sg_b_s"]
_WEIGHTS = ["norm_g", "ffn_w_gate", "ffn_w_up", "ffn_w_down", "dn_w_in", "dn_conv_w", "dn_a_log", "dn_dt_bias",
            "dn_norm_g", "dn_w_out", "sg_w_in", "sg_b_in", "sg_ln_g", "sg_ln_b", "sg_w_s", "sg_b_s", "sg_w_out"]
PACK_COLS = 1024


def _pack(arrs):
    flat = jnp.concatenate([a.reshape(-1) for a in arrs])
    pad = (-flat.shape[0]) % (SUBLANES * PACK_COLS)
    return jnp.pad(flat, (0, pad)).reshape(-1, PACK_COLS)


def _unpack(buf, shapes):
    flat = buf.reshape(-1)
    out, off = [], 0
    for s in shapes:
        n = math.prod(s)
        out.append(flat[off:off + n].reshape(s))
        off += n
    return out


def _as_halves(a):
    if a.shape[0] == 2:
        return a
    if a.shape[0] == 1:
        return a.reshape((2, a.shape[1] // 2) + a.shape[2:])
    return a.reshape((2, a.shape[0] // 2) + a.shape[1:])


def _with_own(gathered, own, chip):
    g = gathered.reshape((N_CHIPS,) + own.shape)
    return [jnp.where(chip == k, own, g[k]) for k in range(N_CHIPS)]


def _cat_shards(g, axis):
    return jnp.concatenate(list(g), axis=axis)


_GROUP_ORDER = ["ffn00", "dn", "ffn01", "ffn10", "sg", "ffn11"]


def _weight_groups(w):
    cast = {k: _mx(w[k]) for k in _BIG}
    groups = {"ffn%d%d" % (i, j): [cast["ffn_w_gate"][i, j].T, cast["ffn_w_up"][i, j].T, cast["ffn_w_down"][i, j]]
              for i, j in [(0, 0), (0, 1), (1, 0), (1, 1)]}
    groups["dn"] = [cast["dn_w_in"][0], cast["dn_w_out"][0]]
    groups["sg"] = [cast["sg_w_in"][0], cast["sg_w_out"][0]]
    return groups


def _ffn_weights(chip, own, gathered):
    pairs = [(a, g.reshape((N_CHIPS,) + a.shape)) for a, g in zip(own, gathered)]
    return {"chip": chip, "gate": pairs[0], "up": pairs[1], "down": pairs[2]}


def _group_matrices(group, shards):
    if group == "sg":
        return {"sg_win": _cat_shards(shards[0], 1), "sg_wout": _cat_shards(shards[1], 0)}
    dn_full = _cat_shards(shards[0], 1)
    W4 = 4 * DN_HEADS * DN_HEAD_DIM
    wba = jnp.zeros((D_MODEL, 2 * LANES), dn_full.dtype)
    wba = wba.at[:, :DN_HEADS].set(dn_full[:, W4:W4 + DN_HEADS])
    wba = wba.at[:, LANES:LANES + DN_HEADS].set(dn_full[:, W4 + DN_HEADS:])
    return {"dn_wqkvz": dn_full[:, :W4], "dn_wba": wba, "dn_wout": _cat_shards(shards[1], 0)}


def _split_cols(a, n):
    w = a.shape[-1] // n
    return [a[..., k * w:(k + 1) * w] for k in range(n)]


def _split_rows(a, n):
    h = a.shape[-2] // n
    return [a[..., k * h:(k + 1) * h, :] for k in range(n)]


_IJ = [(0, 0), (0, 1), (1, 0), (1, 1)]


def _group_grads(group, grads):
    def rows_by_chip(a):
        return a.reshape(N_CHIPS, 2, a.shape[0] // (2 * N_CHIPS), a.shape[1])

    if group.startswith("ffn"):
        tag = group[3:]
        t = grads["wguT" + tag]
        return (["wguT" + tag, "wd" + tag],
                [t.reshape(2, N_CHIPS, t.shape[0] // (2 * N_CHIPS), t.shape[1]), rows_by_chip(grads["wd" + tag])], [True, False])
    if group == "sg":
        return ["sg_w_inT", "sg_w_out"], [rows_by_chip(grads["sg_w_inT"]), rows_by_chip(grads["sg_w_out"])], [False, False]
    dn_in = jnp.stack([jnp.stack(_split_cols(hf, N_CHIPS)) for hf in _split_rows(grads["dn_w_in"], 2)])
    return ["dn_w_in", "dn_w_out"], [dn_in, rows_by_chip(grads["dn_w_out"])], [True, False]


_SHARD_PIECES = {
    "ffn_w_gate": (["wguT%d%d" % ij for ij in _IJ], "lo"),
    "ffn_w_up": (["wguT%d%d" % ij for ij in _IJ], "hi"),
    "ffn_w_down": (["wd%d%d" % ij for ij in _IJ], "rows"),
    "dn_w_in": (["dn_w_in"], "rows"),
    "dn_w_out": (["dn_w_out"], "rows"),
    "sg_w_in": (["sg_w_inT"], "cols"),
    "sg_w_out": (["sg_w_out"], "rows"),
}


def kernel(x, norm_g, ffn_w_gate, ffn_w_up, ffn_w_down, dn_w_in, dn_conv_w, dn_a_log, dn_dt_bias, dn_norm_g, dn_w_out, sg_w_in, sg_b_in, sg_ln_g, sg_ln_b, sg_w_s, sg_b_s, sg_w_out, loss_target, m_norm_g, m_ffn_w_gate, m_ffn_w_up, m_ffn_w_down, m_dn_w_in, m_dn_conv_w, m_dn_a_log, m_dn_dt_bias, m_dn_norm_g, m_dn_w_out, m_sg_w_in, m_sg_b_in, m_sg_ln_g, m_sg_ln_b, m_sg_w_s, m_sg_b_s, m_sg_w_out, v_norm_g, v_ffn_w_gate, v_ffn_w_up, v_ffn_w_down, v_dn_w_in, v_dn_conv_w, v_dn_a_log, v_dn_dt_bias, v_dn_norm_g, v_dn_w_out, v_sg_w_in, v_sg_b_in, v_sg_ln_g, v_sg_ln_b, v_sg_w_s, v_sg_b_s, v_sg_w_out):
    args = dict(locals())
    w = {k: args[k] for k in _WEIGHTS}
    mom = {k: args["m_" + k] for k in _WEIGHTS}
    var = {k: args["v_" + k] for k in _WEIGHTS}
    cx, cy, cc = _mesh_pos()
    chip = 2 * cx + cy

    small_shapes = [w[k].shape for k in _SMALL_SHARDED]
    groups = _weight_groups(w)
    own = groups[_GROUP_ORDER[0]] + [_pack([w[k] for k in _SMALL_SHARDED])]
    first = _allgather_chips([_as_halves(a) for a in own], "gather_first")
    started, after = {}, first[0]
    for g in _GROUP_ORDER[1:]:
        started[g] = _copies_start("gather", groups[g], after, "gather_start_" + g)
        after = started[g]["token"]
    small_k = [_unpack(pack, small_shapes) for pack in _with_own(first[-1], own[-1], chip)]
    p = {name: jnp.concatenate([small_k[k][i] for k in range(N_CHIPS)], axis=-1) for i, name in enumerate(_SMALL_SHARDED)}
    p = {k: (v if k == "norm_g" else v[0]) for k, v in p.items()}
    p["norm_g"] = p["norm_g"] + after[0, 0]
    for k in _SMALL_REPL:
        p[k] = w[k][0]

    def weights_for(group, after):
        if group == _GROUP_ORDER[0]:
            return _ffn_weights(chip, own[:-1], first[:-1])
        srcs, lands = _copies_wait(started[group], after, "gather_wait_" + group)
        if group.startswith("ffn"):
            return _ffn_weights(chip, srcs, lands)
        return _group_matrices(group, [_with_own(l, a, chip) for l, a in zip(lands, srcs)])

    mine, theirs, to_core, to_chips = {}, {}, [], []

    def send_to_chips(after):
        group, names, flags, swap = to_core.pop(0)
        halves, got = _copies_wait(swap, after, "swap_wait_" + group)
        pair_sum = [_add_own_half(h, o, cc, hf, "pair_sum_" + n) for n, h, o, hf in zip(names, halves, got, flags)]
        scatter = _copies_start("scatter", pair_sum, got[0], "reduce_start_" + group)
        to_chips.append((group, names, scatter))
        return scatter["token"]

    def finish(after):
        group, names, scatter = to_chips.pop(0)
        pair_sum, got = _copies_wait(scatter, after, "reduce_wait_" + group)
        half_sum = [_sum_chips(a, b, chip, "chip_sum_" + n, transpose=n == "sg_w_inT")
                    for n, a, b in zip(names, pair_sum, got)]
        other = _swap_whole(half_sum, "gather_core_pair_" + group)
        mine.update(zip(names, half_sum))
        theirs.update(zip(names, other))

    def grads_ready(group, grads):
        names, halves, flags = _group_grads(group, grads)
        swap = _copies_start("swap", halves, None, "swap_start_" + group, flags)
        token = swap["token"]
        if to_core:
            token = send_to_chips(token)
            if len(to_chips) > 1:
                finish(token)
        to_core.append((group, names, flags, swap))
        return token[0, 0]

    small_names = _SMALL_SHARDED + _SMALL_REPL
    small = {}

    def small_ready(grads, loss_part):
        parts = [grads[k] for k in small_names]
        small["shapes"] = [g.shape for g in parts] + [(1,)]
        pack = _pack(parts + [loss_part[0, :1]])
        small["exchange"] = _copies_start("all", [pack], None, "small_start")
        return small["exchange"]["token"]

    loss_part, grad_x, grads = _local_step(x[0], loss_target[0], p, weights_for, grads_ready, small_ready)
    token = send_to_chips(to_core[0][3]["token"])
    finish(token)
    (pack,), (packs,) = _copies_wait(small["exchange"], list(theirs.values()), "small_wait")
    summed = _sum_devices(pack, packs, 4 * cx + 2 * cy + cc, "small_sum")
    parts = _unpack(summed, small["shapes"])
    loss = parts[-1][0]
    grad = {}
    for i, k in enumerate(small_names):
        g = parts[i]
        if k in _SMALL_SHARDED:
            n = w[k].shape[-1]
            g = lax.dynamic_slice_in_dim(g, chip * n, n, axis=g.ndim - 1)
        grad[k] = g

    delta, new_m, new_v = {}, {}, {}

    def update(keys):
        for k in keys:
            names, kind = _SHARD_PIECES[k]
            turn = (lambda a: jnp.swapaxes(a, -1, -2)) if names[0].startswith("wguT") else (lambda a: a)
            outs = _adamw_pieces(turn(w[k]), turn(mom[k]), turn(var[k]), [mine[n] for n in names], [theirs[n] for n in names],
                                 cc, kind, "adamw_" + k)
            grad[k], delta[k], new_m[k], new_v[k] = (turn(o) for o in outs)

    shapes = [w[k].shape for k in small_names]
    d, nm, nv = _adamw(_pack([w[k] for k in small_names]), _pack([grad[k] for k in small_names]),
                       _pack([mom[k] for k in small_names]), _pack([var[k] for k in small_names]), "adamw_small")
    for k, a, b, c_ in zip(small_names, _unpack(d, shapes), _unpack(nm, shapes), _unpack(nv, shapes)):
        delta[k], new_m[k], new_v[k] = a, b, c_
    mixers = [k for k in _BIG if not k.startswith("ffn")]
    update(mixers)
    finish([d] + [delta[k] for k in mixers] + list(theirs.values()))
    update([k for k in _BIG if k.startswith("ffn")])

    return (loss, grad_x[None], *[grad[k] for k in _WEIGHTS], *[delta[k] for k in _WEIGHTS],
            *[new_m[k] for k in _WEIGHTS], *[new_v[k] for k in _WEIGHTS])
```

```python
import math

import jax
import jax.numpy as jnp
from jax import lax
from jax.experimental import pallas as pl
from jax.experimental.pallas import tpu as pltpu

F32 = jnp.float32
MXU_DTYPE = jnp.bfloat16
COMM_DTYPE = jnp.bfloat16
HI = lax.Precision.HIGHEST
TRI_PREC = lax.Precision.HIGH

D_MODEL = 1024
RMS_EPS = 1e-6
LN_EPS = 1e-5
L2_EPS = 1e-6
DN_HEADS = 8
DN_HEAD_DIM = 128
DN_CONV = 4
DN_CHUNK = 64
SG_WIDTH = 2048
SG_GROUPS = 8
SG_CHUNK = 128
SG_GROUP_W = SG_WIDTH // SG_GROUPS
N_CHIPS = 4
N_DEV = 8
LANES = 128
SUBLANES = 8
VMEM_LIMIT = 56 * 1024 * 1024

ADAM_LR = 0.001
ADAM_B1 = 0.9
ADAM_B2 = 0.999
ADAM_EPS = 1e-08
ADAM_WD = 0.01
ADAM_STEP = 10

MESH = pl.DeviceIdType.MESH
ANY = pl.BlockSpec(memory_space=pl.ANY)


def _cp(*sem):
    return pltpu.CompilerParams(dimension_semantics=sem, vmem_limit_bytes=VMEM_LIMIT)


def _pick(n, pref, mult=LANES):
    best = None
    d = mult
    while d <= min(n, pref):
        if n % d == 0:
            best = d
        d += mult
    return best if best is not None else n


def _full(shape):
    nd = len(shape)
    return pl.BlockSpec(shape, lambda *_: (0,) * nd)


def _sigmoid(x):
    return 1.0 / (1.0 + jnp.exp(-x))


def _dot(a, b, dims, prec=None):
    return lax.dot_general(a, b, (dims, ((), ())), preferred_element_type=F32, precision=prec)


NN = ((1,), (0,))
NT = ((1,), (1,))
TN = ((0,), (0,))


def _mx(a):
    return a.astype(MXU_DTYPE)


def _rms_stat(x):
    return lax.rsqrt(jnp.mean(x * x, axis=-1, keepdims=True) + RMS_EPS)


def _rms_bwd(x, r, g, dy):
    xh = x * r
    dxh = dy * g
    dx = r * (dxh - xh * jnp.mean(dxh * xh, axis=-1, keepdims=True))
    return dx, jnp.sum(dy * xh, axis=0, keepdims=True)


def _mm(a, b, mode, name, out_dtype=F32, add=None, after=None):
    if mode == "tn":
        K, M = a.shape
        N = b.shape[1]
    elif mode == "nt":
        M, K = a.shape
        N = b.shape[0]
    else:
        M, K = a.shape
        N = b.shape[1]
    tn = _pick(N, 1024)
    if mode == "tn":
        tm = _pick(M, 1024 if tn <= 512 else 1408)
        tk = _pick(K, 1024, SUBLANES)
    else:
        tm = _pick(M, max(512, min(2048, (1024 * 1024) // tn)), SUBLANES)
        tk = _pick(K, 2048)
    nk = K // tk
    grid = (N // tn, M // tm, nk)
    if mode == "nn":
        a_spec = pl.BlockSpec((tm, tk), lambda j, i, k: (i, k))
        b_spec = pl.BlockSpec((tk, tn), lambda j, i, k: (k, j))
        dims = NN
    elif mode == "nt":
        a_spec = pl.BlockSpec((tm, tk), lambda j, i, k: (i, k))
        b_spec = pl.BlockSpec((tn, tk), lambda j, i, k: (j, k))
        dims = NT
    else:
        a_spec = pl.BlockSpec((tk, tm), lambda j, i, k: (k, i))
        b_spec = pl.BlockSpec((tk, tn), lambda j, i, k: (k, j))
        dims = TN
    o_spec = pl.BlockSpec((tm, tn), lambda j, i, k: (i, j))
    has_add = add is not None

    def body(*refs):
        a_ref, b_ref = refs[:2]
        add_ref = refs[2] if has_add else None
        o_ref, acc = refs[-2:]
        k = pl.program_id(2)

        @pl.when(k == 0)
        def _():
            acc[...] = add_ref[...] if has_add else jnp.zeros_like(acc)

        acc[...] += _dot(a_ref[...], b_ref[...], dims)

        @pl.when(k == nk - 1)
        def _():
            o_ref[...] = acc[...].astype(o_ref.dtype)

    ins = [a, b] + ([add] if has_add else []) + ([after] if after is not None else [])
    specs = [a_spec, b_spec] + ([o_spec] if has_add else []) + ([ANY] if after is not None else [])
    return pl.pallas_call(
        body, name=name, grid=grid, in_specs=specs, out_specs=o_spec,
        out_shape=jax.ShapeDtypeStruct((M, N), out_dtype),
        scratch_shapes=[pltpu.VMEM((tm, tn), F32)],
        compiler_params=_cp("parallel", "parallel", "arbitrary"),
    )(*ins)


def _ffn_weight_operands(wt):
    return [_scalar(wt["chip"])] , [wt["gate"][0], wt["gate"][1], wt["up"][0], wt["up"][1], wt["down"][0], wt["down"][1]]


def _load_ffn_weights(chip_ref, shard_refs, wgu_v, wd_v, sem):
    fs = wd_v.shape[0] // N_CHIPS

    @pl.when(pl.program_id(0) == 0)
    def _():
        me = chip_ref[0]
        waits = []
        for t, (dst, base) in enumerate([(wgu_v, 0), (wgu_v, wd_v.shape[0]), (wd_v, 0)]):
            own, gathered = shard_refs[2 * t], shard_refs[2 * t + 1]
            for k in range(N_CHIPS):
                slot = dst.at[pl.ds(base + k * fs, fs), :]
                s = sem.at[t * N_CHIPS + k]

                @pl.when(me == k)
                def _(own=own, slot=slot, s=s):
                    pltpu.make_async_copy(own, slot, s).start()

                @pl.when(me != k)
                def _(gathered=gathered, k=k, slot=slot, s=s):
                    pltpu.make_async_copy(gathered.at[k], slot, s).start()

                waits.append(pltpu.make_async_copy(own, slot, s))
        for cp in waits:
            cp.wait()


def _ffn_fwd(x, g0, g1, wt, name, next_gain=None, target=None):
    T, D = x.shape
    F = N_CHIPS * wt["down"][0].shape[0]
    F2 = 2 * F
    tm = _pick(T, 256, SUBLANES)
    prefetch, shards = _ffn_weight_operands(wt)
    extra = [a for a in (next_gain, target) if a is not None]
    n_tail = 1 if next_gain is not None else (2 if target is not None else 0)

    def body(chip_ref, x_ref, g0_ref, g1_ref, *refs):
        extra_ref = refs[0] if extra else None
        refs = refs[len(extra):]
        shard_refs = refs[:6]
        xo_ref, h_ref, gu_ref, y_ref = refs[6:10]
        tail_refs = refs[10:10 + n_tail]
        wgu_v, wd_v, sem = refs[10 + n_tail:]
        _load_ffn_weights(chip_ref, shard_refs, wgu_v, wd_v, sem)
        xv = x_ref[...]
        hb = _mx(xv * _rms_stat(xv) * g0_ref[...])
        h_ref[...] = hb
        gu = _dot(hb, wgu_v[...], NT)
        gu_ref[...] = gu.astype(gu_ref.dtype)
        g = gu[:, :F]
        u = gu[:, F:]
        a = _mx(g * _sigmoid(g) * u)
        y = _dot(a, wd_v[...], NN)
        y_ref[...] = y
        xo = xv + 0.5 * (y * _rms_stat(y) * g1_ref[...])
        xo_ref[...] = xo
        if next_gain is not None:
            tail_refs[0][...] = _mx(xo * _rms_stat(xo) * extra_ref[...])
        if target is not None:
            loss_ref, dy_ref = tail_refs

            @pl.when(pl.program_id(0) == 0)
            def _():
                loss_ref[...] = jnp.zeros_like(loss_ref)

            e = xo - extra_ref[...]
            dy_ref[...] = e * (1.0 / D)
            loss_ref[...] += 0.5 * jnp.sum(jnp.mean(e * e, axis=-1, keepdims=True), axis=0, keepdims=True)

    row = lambda w: pl.BlockSpec((tm, w), lambda i, c: (i, 0))
    one = pl.BlockSpec((1, D), lambda i, c: (0, 0))
    tail_specs, tail_shapes, extra_specs = [], [], []
    if next_gain is not None:
        extra_specs, tail_specs, tail_shapes = [one], [row(D)], [jax.ShapeDtypeStruct((T, D), MXU_DTYPE)]
    if target is not None:
        extra_specs = [row(D)]
        tail_specs = [pl.BlockSpec((SUBLANES, LANES), lambda i, c: (0, 0)), row(D)]
        tail_shapes = [jax.ShapeDtypeStruct((SUBLANES, LANES), F32), jax.ShapeDtypeStruct((T, D), F32)]
    return pl.pallas_call(
        body, name=name,
        grid_spec=pltpu.PrefetchScalarGridSpec(
            num_scalar_prefetch=1, grid=(T // tm,),
            in_specs=[row(D), one, one] + extra_specs + [ANY] * 6,
            out_specs=[row(D), row(D), row(F2), row(D)] + tail_specs,
            scratch_shapes=[pltpu.VMEM((F2, D), MXU_DTYPE), pltpu.VMEM((F, D), MXU_DTYPE),
                            pltpu.SemaphoreType.DMA((3 * N_CHIPS,))]),
        out_shape=[jax.ShapeDtypeStruct((T, D), F32), jax.ShapeDtypeStruct((T, D), MXU_DTYPE),
                   jax.ShapeDtypeStruct((T, F2), MXU_DTYPE), jax.ShapeDtypeStruct((T, D), F32)] + tail_shapes,
        compiler_params=_cp("arbitrary"),
    )(*prefetch, x, g0, g1, *extra, *shards)


FFN_BWD_CHUNK = 2816


def _ffn_bwd(dxo, x, y, gu, g0, g1, wt, name):
    T, D = x.shape
    F2 = gu.shape[1]
    F = F2 // 2
    tm = _pick(T, 256, SUBLANES)
    fc = _pick(F, FFN_BWD_CHUNK)
    prefetch, shards = _ffn_weight_operands(wt)

    def body(chip_ref, dxo_ref, x_ref, y_ref, gu_ref, g0_ref, g1_ref, *refs):
        shard_refs = refs[:6]
        dx_ref, dy_ref, a_ref, dgu_ref, dg0_ref, dg1_ref, wgu_v, wd_v, sem = refs[6:]
        _load_ffn_weights(chip_ref, shard_refs, wgu_v, wd_v, sem)

        @pl.when(pl.program_id(0) == 0)
        def _():
            dg0_ref[...] = jnp.zeros_like(dg0_ref)
            dg1_ref[...] = jnp.zeros_like(dg1_ref)

        dxo_v = dxo_ref[...]
        yv = y_ref[...]
        dy, dg1 = _rms_bwd(yv, _rms_stat(yv), g1_ref[...], 0.5 * dxo_v)
        dg1_ref[...] += dg1
        dyb = _mx(dy)
        dy_ref[...] = dyb
        dh = jnp.zeros((tm, D), F32)
        for c in range(F // fc):
            lo, hi = c * fc, (c + 1) * fc
            da = _dot(dyb, wd_v[lo:hi, :], NT)
            g = gu_ref[:, lo:hi].astype(F32)
            u = gu_ref[:, F + lo:F + hi].astype(F32)
            s = _sigmoid(g)
            sg = g * s
            a_ref[:, lo:hi] = _mx(sg * u)
            dg = _mx(da * u * (s * (1.0 + g * (1.0 - s))))
            du = _mx(da * sg)
            dgu_ref[:, lo:hi] = dg
            dgu_ref[:, F + lo:F + hi] = du
            dh = dh + _dot(dg, wgu_v[lo:hi, :], NN) + _dot(du, wgu_v[F + lo:F + hi, :], NN)
        xv = x_ref[...]
        dx, dg0 = _rms_bwd(xv, _rms_stat(xv), g0_ref[...], dh)
        dg0_ref[...] += dg0
        dx_ref[...] = dxo_v + dx

    row = lambda w: pl.BlockSpec((tm, w), lambda i, c: (i, 0))
    one = pl.BlockSpec((1, D), lambda i, c: (0, 0))
    return pl.pallas_call(
        body, name=name,
        grid_spec=pltpu.PrefetchScalarGridSpec(
            num_scalar_prefetch=1, grid=(T // tm,),
            in_specs=[row(D), row(D), row(D), row(F2), one, one] + [ANY] * 6,
            out_specs=[row(D), row(D), row(F), row(F2), one, one],
            scratch_shapes=[pltpu.VMEM((F2, D), MXU_DTYPE), pltpu.VMEM((F, D), MXU_DTYPE),
                            pltpu.SemaphoreType.DMA((3 * N_CHIPS,))]),
        out_shape=[jax.ShapeDtypeStruct((T, D), F32), jax.ShapeDtypeStruct((T, D), MXU_DTYPE),
                   jax.ShapeDtypeStruct((T, F), MXU_DTYPE), jax.ShapeDtypeStruct((T, F2), MXU_DTYPE),
                   jax.ShapeDtypeStruct((1, D), F32), jax.ShapeDtypeStruct((1, D), F32)],
        compiler_params=_cp("arbitrary"),
    )(*prefetch, dxo, x, y, gu, g0, g1, *shards)


def _out_proj_postnorm(a, b, x, g, name):
    T, K = a.shape
    D = b.shape[1]
    tm = _pick(T, 512, SUBLANES)

    def body(a_ref, b_ref, x_ref, g_ref, m_ref, o_ref):
        mv = _dot(a_ref[...], b_ref[...], NN)
        m_ref[...] = mv
        o_ref[...] = x_ref[...] + mv * _rms_stat(mv) * g_ref[...]

    row = lambda w: pl.BlockSpec((tm, w), lambda i: (i, 0))
    return pl.pallas_call(body, name=name, grid=(T // tm,),
                          in_specs=[row(K), _full((K, D)), row(D), _full((1, D))], out_specs=[row(D), row(D)],
                          out_shape=[jax.ShapeDtypeStruct((T, D), F32)] * 2, compiler_params=_cp("parallel"))(a, b, x, g)


def _postnorm_bwd_dgrad(dxo, m, g, b, name):
    T, D = m.shape
    K = b.shape[0]
    tm = _pick(T, 512, SUBLANES)

    def body(dxo_ref, m_ref, g_ref, b_ref, dm_ref, dg_ref, da_ref):
        @pl.when(pl.program_id(0) == 0)
        def _():
            dg_ref[...] = jnp.zeros_like(dg_ref)

        mv = m_ref[...]
        dm, dg = _rms_bwd(mv, _rms_stat(mv), g_ref[...], dxo_ref[...])
        dg_ref[...] += dg
        dmb = _mx(dm)
        dm_ref[...] = dmb
        da_ref[...] = _dot(dmb, b_ref[...], NT)

    row = lambda w: pl.BlockSpec((tm, w), lambda i: (i, 0))
    return pl.pallas_call(body, name=name, grid=(T // tm,), in_specs=[row(D), row(D), _full((1, D)), _full((K, D))],
                          out_specs=[row(D), _full((1, D)), row(K)],
                          out_shape=[jax.ShapeDtypeStruct((T, D), MXU_DTYPE), jax.ShapeDtypeStruct((1, D), F32),
                                     jax.ShapeDtypeStruct((T, K), F32)],
                          compiler_params=_cp("arbitrary"))(dxo, m, g, b)


def _dgrad_prenorm_bwd(a, b, add, dxo, x, g, name):
    T, K = a.shape
    D = b.shape[0]
    tm = _pick(T, 512, SUBLANES)
    tk = _pick(K, 2048)
    nk = K // tk
    has_add = add is not None

    def body(*refs):
        a_ref, b_ref = refs[:2]
        add_ref = refs[2] if has_add else None
        dxo_ref, x_ref, g_ref, dx_ref, dg_ref, acc = refs[-6:]
        i, k = pl.program_id(0), pl.program_id(1)

        @pl.when((i == 0) & (k == 0))
        def _():
            dg_ref[...] = jnp.zeros_like(dg_ref)

        @pl.when(k == 0)
        def _():
            acc[...] = add_ref[...] if has_add else jnp.zeros_like(acc)

        acc[...] += _dot(a_ref[...], b_ref[...], NT)

        @pl.when(k == nk - 1)
        def _():
            xv = x_ref[...]
            dx, dg = _rms_bwd(xv, _rms_stat(xv), g_ref[...], acc[...])
            dg_ref[...] += dg
            dx_ref[...] = dxo_ref[...] + dx

    row = pl.BlockSpec((tm, D), lambda i, k: (i, 0))
    one = pl.BlockSpec((1, D), lambda i, k: (0, 0))
    ins = [a, b] + ([add] if has_add else []) + [dxo, x, g]
    specs = ([pl.BlockSpec((tm, tk), lambda i, k: (i, k)), pl.BlockSpec((D, tk), lambda i, k: (0, k))]
             + ([row] if has_add else []) + [row, row, one])
    return pl.pallas_call(body, name=name, grid=(T // tm, nk), in_specs=specs, out_specs=[row, one],
                          out_shape=[jax.ShapeDtypeStruct((T, D), F32), jax.ShapeDtypeStruct((1, D), F32)],
                          scratch_shapes=[pltpu.VMEM((tm, D), F32)],
                          compiler_params=_cp("arbitrary", "arbitrary"))(*ins)


DN_ROWS = 512


def _shift_down(prev8, cur, s):
    n = cur.shape[0]
    xx = jnp.concatenate([prev8, cur], axis=0)
    return pltpu.roll(xx, s, 0)[SUBLANES:SUBLANES + n, :]


def _shift_up(cur, next8, s):
    n = cur.shape[0]
    xx = jnp.concatenate([cur, next8], axis=0)
    return pltpu.roll(xx, n + SUBLANES - s, 0)[:n, :]


def _tile_start(r, rows):
    return r * rows if isinstance(r, int) else pl.multiple_of(r * rows, SUBLANES)


def _conv_tile(x_ref, w, r, rows):
    start = _tile_start(r, rows)
    cur = x_ref[pl.ds(start, rows), :]
    if isinstance(r, int):
        prev8 = jnp.zeros((SUBLANES, cur.shape[1]), cur.dtype)
        taps = [_shift_down(prev8, cur, DN_CONV - 1 - j) if j < DN_CONV - 1 else cur for j in range(DN_CONV)]
    else:
        taps = [x_ref[pl.ds(start - (DN_CONV - 1 - j), rows), :] if j < DN_CONV - 1 else cur for j in range(DN_CONV)]
    c = taps[0] * w[0:1, :]
    for j in range(1, DN_CONV):
        c = c + taps[j] * w[j:j + 1, :]
    return c, taps


def _dn_prep_fwd(proj, conv_w, name):
    T = proj.shape[0]
    W = DN_HEADS * DN_HEAD_DIM
    rows = min(DN_ROWS, T)
    n_inner = T // rows
    scale = DN_HEAD_DIM ** -0.5

    def body(x_ref, w_ref, o_ref):
        cb = pl.program_id(0)
        w = w_ref[...]
        is_qk = cb < 2 * DN_HEADS
        post = jnp.where(cb < DN_HEADS, scale, 1.0)

        def step(r, carry):
            c, _ = _conv_tile(x_ref, w, r, rows)
            s = c * _sigmoid(c)
            rinv = lax.rsqrt(jnp.sum(s * s, axis=-1, keepdims=True) + L2_EPS)
            o_ref[pl.ds(_tile_start(r, rows), rows), :] = jnp.where(is_qk, s * rinv * post, s)
            return carry

        step(0, 0)
        lax.fori_loop(1, n_inner, step, 0)

    col = pl.BlockSpec((T, LANES), lambda j: (0, j))
    return pl.pallas_call(body, name=name, grid=(3 * W // LANES,),
                          in_specs=[col, pl.BlockSpec((DN_CONV, LANES), lambda j: (0, j))], out_specs=col,
                          out_shape=jax.ShapeDtypeStruct((T, 3 * W), F32), compiler_params=_cp("parallel"))(proj, conv_w)


def _dn_prep_bwd(proj, conv_w, dqkv, name):
    T = proj.shape[0]
    W = DN_HEADS * DN_HEAD_DIM
    rows = min(DN_ROWS, T)
    n_inner = T // rows
    scale = DN_HEAD_DIM ** -0.5

    def body(x_ref, w_ref, dy_ref, dx_ref, dw_ref, dc_scr):
        cb = pl.program_id(0)
        w = w_ref[...]
        is_qk = cb < 2 * DN_HEADS
        post = jnp.where(cb < DN_HEADS, scale, 1.0)

        def step1(r, dws):
            c, taps = _conv_tile(x_ref, w, r, rows)
            sg = _sigmoid(c)
            s = c * sg
            rinv = lax.rsqrt(jnp.sum(s * s, axis=-1, keepdims=True) + L2_EPS)
            dy = dy_ref[pl.ds(_tile_start(r, rows), rows), :]
            yn = s * rinv
            dyn = dy * post
            ds_qk = rinv * (dyn - yn * jnp.sum(dyn * yn, axis=-1, keepdims=True))
            ds = jnp.where(is_qk, ds_qk, dy)
            dc = ds * (sg * (1.0 + c * (1.0 - sg)))
            dc_scr[pl.ds(_tile_start(r, rows), rows), :] = dc
            return tuple(dws[j] + jnp.sum(dc * taps[j], axis=0, keepdims=True) for j in range(DN_CONV))

        zero = jnp.zeros((1, LANES), F32)
        dws = lax.fori_loop(1, n_inner, step1, step1(0, (zero,) * DN_CONV))
        for j in range(DN_CONV):
            dw_ref[j:j + 1, :] = dws[j]

        def step2(r, carry):
            start = _tile_start(r, rows)
            cur = dc_scr[pl.ds(start, rows), :]
            dx = cur * w[DN_CONV - 1:DN_CONV, :]
            for j in range(DN_CONV - 1):
                s = DN_CONV - 1 - j
                if isinstance(r, int):
                    up = _shift_up(cur, jnp.zeros((SUBLANES, LANES), F32), s)
                else:
                    up = dc_scr[pl.ds(start + s, rows), :]
                dx = dx + up * w[j:j + 1, :]
            dx_ref[pl.ds(start, rows), :] = _mx(dx)
            return carry

        lax.fori_loop(0, n_inner - 1, step2, 0)
        step2(n_inner - 1, 0)

    col = pl.BlockSpec((T, LANES), lambda j: (0, j))
    wspec = pl.BlockSpec((DN_CONV, LANES), lambda j: (0, j))
    return pl.pallas_call(body, name=name, grid=(3 * W // LANES,), in_specs=[col, wspec, col], out_specs=[col, wspec],
                          out_shape=[jax.ShapeDtypeStruct((T, 3 * W), MXU_DTYPE), jax.ShapeDtypeStruct((DN_CONV, 3 * W), F32)],
                          scratch_shapes=[pltpu.VMEM((T, LANES), F32)], compiler_params=_cp("parallel"))(proj, conv_w, dqkv)


def _softplus(x):
    return jnp.maximum(x, 0.0) + jnp.log(1.0 + jnp.exp(-jnp.abs(x)))


def _dn_gate_fwd(ba, a_log, dt_bias, name):
    T = ba.shape[0]
    tm = _pick(T, 1024, SUBLANES)

    def body(ba_ref, al_ref, dt_ref, beta_ref, g_ref):
        beta_ref[...] = _sigmoid(ba_ref[:, :LANES])
        g_ref[...] = -jnp.exp(al_ref[...]) * _softplus(ba_ref[:, LANES:] + dt_ref[...])

    row = lambda w: pl.BlockSpec((tm, w), lambda i: (i, 0))
    return pl.pallas_call(body, name=name, grid=(T // tm,), in_specs=[row(2 * LANES), _full((1, LANES)), _full((1, LANES))],
                          out_specs=[row(LANES), row(LANES)],
                          out_shape=[jax.ShapeDtypeStruct((T, LANES), F32)] * 2, compiler_params=_cp("parallel"))(ba, a_log, dt_bias)


def _dn_gate_bwd(ba, a_log, dt_bias, dbeta, dg, name):
    T = ba.shape[0]
    tm = _pick(T, 1024, SUBLANES)

    def body(ba_ref, al_ref, dt_ref, dbeta_ref, dg_ref, dba_ref, dal_ref, ddt_ref):
        @pl.when(pl.program_id(0) == 0)
        def _():
            dal_ref[...] = jnp.zeros_like(dal_ref)
            ddt_ref[...] = jnp.zeros_like(ddt_ref)

        beta = _sigmoid(ba_ref[:, :LANES])
        dba_ref[:, :LANES] = _mx(dbeta_ref[...] * beta * (1.0 - beta))
        pre = ba_ref[:, LANES:] + dt_ref[...]
        ea = jnp.exp(al_ref[...])
        dgv = dg_ref[...]
        da = dgv * (-ea) * _sigmoid(pre)
        dba_ref[:, LANES:] = _mx(da)
        ddt_ref[...] += jnp.sum(da, axis=0, keepdims=True)
        dal_ref[...] += jnp.sum(dgv * (-ea) * _softplus(pre), axis=0, keepdims=True)

    row = lambda w: pl.BlockSpec((tm, w), lambda i: (i, 0))
    one = _full((1, LANES))
    return pl.pallas_call(body, name=name, grid=(T // tm,), in_specs=[row(2 * LANES), one, one, row(LANES), row(LANES)],
                          out_specs=[row(2 * LANES), one, one],
                          out_shape=[jax.ShapeDtypeStruct((T, 2 * LANES), MXU_DTYPE), jax.ShapeDtypeStruct((1, LANES), F32),
                                     jax.ShapeDtypeStruct((1, LANES), F32)],
                          compiler_params=_cp("arbitrary"))(ba, a_log, dt_bias, dbeta, dg)


def _tri(c, strict):
    i = lax.broadcasted_iota(jnp.int32, (c, c), 0)
    j = lax.broadcasted_iota(jnp.int32, (c, c), 1)
    return (i > j) if strict else (i >= j)


def _inv_unit_lower(ls):
    c = ls[0].shape[0]
    i = lax.broadcasted_iota(jnp.int32, (c, c), 0)
    j = lax.broadcasted_iota(jnp.int32, (c, c), 1)
    eye = jnp.where(i == j, 1.0, 0.0)
    facs = [[eye - l for l in ls]]
    cur = ls
    for _ in range(int(math.log2(c)) - 1):
        cur = [_dot(p, p, NN, TRI_PREC) for p in cur]
        facs.append([eye + p for p in cur])
    while len(facs) > 1:
        nxt = [[_dot(a, b, NN, TRI_PREC) for a, b in zip(facs[t], facs[t + 1])] for t in range(0, len(facs) - 1, 2)]
        if len(facs) % 2:
            nxt.append(facs[-1])
        facs = nxt
    return facs[0]


def _chunk_gates(g_blk):
    c = g_blk.shape[0]
    gcs = _dot(jnp.where(_tri(c, False), 1.0, 0.0), g_blk, NN, HI)
    return gcs, gcs.T


def _head_chunk(h, qh, kh, vh, beta_blk, gcs, gcs_t):
    c = qh.shape[0]
    incl = _tri(c, False)
    gc_col = gcs[:, h:h + 1]
    gc_row = gcs_t[h:h + 1, :]
    gc_last = gcs_t[h:h + 1, c - 1:c]
    dec = jnp.where(incl, jnp.exp(jnp.where(incl, gc_col - gc_row, 0.0)), 0.0)
    gam = jnp.exp(gc_col)
    rr = jnp.exp(gc_last - gc_col)
    gl = jnp.exp(gc_last)
    b = beta_blk[:, h:h + 1]
    kb = kh * b
    vb = vh * b
    both = _dot(jnp.concatenate([_mx(kb), _mx(qh)], axis=0), _mx(kh), NT)
    lmat = jnp.where(_tri(c, True), both[:c] * dec, 0.0)
    pmat = jnp.where(incl, both[c:] * dec, 0.0)
    return dict(dec=dec, gam=gam, rr=rr, gl=gl, b=b, kb=kb, vb=vb, lmat=lmat, pmat=pmat)


def _solve_uw(tinv, q):
    return _dot(tinv, jnp.concatenate([q["vb"], q["kb"] * q["gam"]], axis=1), NN, TRI_PREC)


def _dn_scan_fwd(qkv, beta, g, proj, norm_g, name):
    T = qkv.shape[0]
    C, H, Dh = DN_CHUNK, DN_HEADS, DN_HEAD_DIM
    W = H * Dh
    N = T // C

    def body(q_ref, k_ref, v_ref, beta_ref, g_ref, z_ref, ng_ref, og_ref, o_ref, tinv_ref, s_ref, state):
        @pl.when(pl.program_id(0) == 0)
        def _():
            state[...] = jnp.zeros_like(state)

        gcs, gcs_t = _chunk_gates(g_ref[...])
        beta_blk = beta_ref[...]
        ng = ng_ref[...]
        heads = range(H)
        cs = [slice(h * Dh, (h + 1) * Dh) for h in heads]
        qs = [_head_chunk(h, q_ref[:, cs[h]], k_ref[:, cs[h]], v_ref[:, cs[h]], beta_blk, gcs, gcs_t) for h in heads]
        tinvs = _inv_unit_lower([q["lmat"] for q in qs])
        for h in heads:
            tinv_ref[h] = tinvs[h]
        uws = [_solve_uw(tinvs[h], qs[h]) for h in heads]
        ss = [state[h] for h in heads]
        for h in heads:
            s_ref[0, h] = ss[h]
        sbs = [_mx(s) for s in ss]
        vnbs = [_mx(uws[h][:, :Dh] - _dot(_mx(uws[h][:, Dh:]), sbs[h], NN)) for h in heads]
        os_ = [_dot(jnp.concatenate([_mx(q_ref[:, cs[h]] * qs[h]["gam"]), _mx(qs[h]["pmat"])], axis=1),
                    jnp.concatenate([sbs[h], vnbs[h]], axis=0), NN) for h in heads]
        for h in heads:
            state[h] = ss[h] * qs[h]["gl"] + _dot(_mx((k_ref[:, cs[h]] * qs[h]["rr"]).T), vnbs[h], NN)
        for h in heads:
            o = os_[h]
            o_ref[:, cs[h]] = o
            zh = z_ref[:, cs[h]]
            og_ref[:, cs[h]] = _mx(o * _rms_stat(o) * ng * (zh * _sigmoid(zh)))

    blk = lambda j: pl.BlockSpec((C, W), lambda n: (n, j))
    small = pl.BlockSpec((C, LANES), lambda n: (n, 0))
    return pl.pallas_call(
        body, name=name, grid=(N,),
        in_specs=[blk(0), blk(1), blk(2), small, small, blk(3), _full((1, Dh))],
        out_specs=[blk(0), blk(0), pl.BlockSpec((H, C, C), lambda n: (0, n, 0)),
                   pl.BlockSpec((1, H, Dh, Dh), lambda n: (n, 0, 0, 0))],
        out_shape=[jax.ShapeDtypeStruct((T, W), MXU_DTYPE), jax.ShapeDtypeStruct((T, W), F32),
                   jax.ShapeDtypeStruct((H, T, C), F32), jax.ShapeDtypeStruct((N, H, Dh, Dh), F32)],
        scratch_shapes=[pltpu.VMEM((H, Dh, Dh), F32)],
        compiler_params=_cp("arbitrary"),
    )(qkv, qkv, qkv, beta, g, proj, norm_g)


def _dn_scan_bwd(qkv, beta, g, proj, norm_g, o, tinv, s_all, dog, name):
    T = qkv.shape[0]
    C, H, Dh = DN_CHUNK, DN_HEADS, DN_HEAD_DIM
    W = H * Dh
    N = T // C

    def body(q_ref, k_ref, v_ref, beta_ref, g_ref, z_ref, ng_ref, o_ref, tinv_ref, s_ref, dog_ref,
             dqkv_ref, dbeta_ref, dg_ref, dz_ref, dng_ref, dstate):
        @pl.when(pl.program_id(0) == 0)
        def _():
            dstate[...] = jnp.zeros_like(dstate)
            dng_ref[...] = jnp.zeros_like(dng_ref)

        gcs, gcs_t = _chunk_gates(g_ref[...])
        beta_blk = beta_ref[...]
        ng = ng_ref[...]
        incl = _tri(C, False)
        strict = _tri(C, True)
        lane = lax.broadcasted_iota(jnp.int32, (C, LANES), 1)
        rowi = lax.broadcasted_iota(jnp.int32, (C, 1), 0)
        headrow = lax.broadcasted_iota(jnp.int32, (LANES, C), 0)
        colsums = jnp.zeros((LANES, C), F32)
        dbeta_acc = jnp.zeros((C, LANES), F32)
        dgc_acc = jnp.zeros((C, LANES), F32)
        dng_acc = jnp.zeros((1, Dh), F32)
        cs = [slice(h * Dh, (h + 1) * Dh) for h in range(H)]
        rsum = lambda t: jnp.sum(t, axis=1, keepdims=True)
        for heads in (range(0, H // 2), range(H // 2, H)):
            dobs = {}
            for h in heads:
                oh, zh, dogh = o_ref[:, cs[h]], z_ref[:, cs[h]], dog_ref[:, cs[h]]
                rstat = _rms_stat(oh)
                sz = _sigmoid(zh)
                dz_ref[:, cs[h]] = _mx(dogh * (oh * rstat * ng) * (sz * (1.0 + zh * (1.0 - sz))))
                do, dng = _rms_bwd(oh, rstat, ng, dogh * (zh * sz))
                dng_acc = dng_acc + dng
                dobs[h] = _mx(do)
            qs = {h: _head_chunk(h, q_ref[:, cs[h]], k_ref[:, cs[h]], v_ref[:, cs[h]], beta_blk, gcs, gcs_t) for h in heads}
            tms = {h: tinv_ref[h] for h in heads}
            uws = {h: _solve_uw(tms[h], qs[h]) for h in heads}
            ss = {h: s_ref[0, h] for h in heads}
            sbs = {h: _mx(ss[h]) for h in heads}
            wbs = {h: _mx(uws[h][:, Dh:]) for h in heads}
            vnbs = {h: _mx(uws[h][:, :Dh] - _dot(wbs[h], sbs[h], NN)) for h in heads}
            dsns = {h: dstate[h] for h in heads}
            dsbs = {h: _mx(dsns[h]) for h in heads}
            dvnews = {h: _dot(_mx(qs[h]["pmat"]), dobs[h], TN) + _dot(_mx(k_ref[:, cs[h]] * qs[h]["rr"]), dsbs[h], NN)
                      for h in heads}
            dvb16s = {h: _mx(dvnews[h]) for h in heads}
            dps = {h: jnp.where(incl, _dot(dobs[h], vnbs[h], NT), 0.0) for h in heads}
            dqds = {h: _dot(dobs[h], sbs[h], NT) for h in heads}
            dkds = {h: _dot(vnbs[h], dsbs[h], NT) for h in heads}
            dgls = {h: jnp.sum(rsum(ss[h] * dsns[h]), axis=0, keepdims=True) for h in heads}
            dws = {h: -_dot(dvb16s[h], sbs[h], NT) for h in heads}
            for h in heads:
                dstate[h] = qs[h]["gl"] * dsns[h] + _dot(
                    jnp.concatenate([_mx(q_ref[:, cs[h]] * qs[h]["gam"]), -wbs[h]], axis=0),
                    jnp.concatenate([dobs[h], dvb16s[h]], axis=0), TN)
            dsols = {h: _dot(tms[h], jnp.concatenate([dvnews[h], dws[h]], axis=1), TN, TRI_PREC) for h in heads}
            dvbs = {h: dsols[h][:, :Dh] for h in heads}
            dkbgs = {h: dsols[h][:, Dh:] for h in heads}
            dls = {h: jnp.where(strict, -_dot(dsols[h], uws[h], NT, TRI_PREC), 0.0) for h in heads}
            mmats = {h: dls[h] * qs[h]["lmat"] + dps[h] * qs[h]["pmat"] for h in heads}
            dgcs = {h: rsum(mmats[h]) for h in heads}
            for h in heads:
                colsums = jnp.where(headrow == h, jnp.sum(mmats[h], axis=0, keepdims=True), colsums)
            dboth = {h: jnp.concatenate([_mx(dls[h] * qs[h]["dec"]), _mx(dps[h] * qs[h]["dec"])], axis=0) for h in heads}
            for h in heads:
                q = qs[h]
                qh, kh, vh = q_ref[:, cs[h]], k_ref[:, cs[h]], v_ref[:, cs[h]]
                gam, rr, b, kb = q["gam"], q["rr"], q["b"], q["kb"]
                on_k = _dot(dboth[h], _mx(kh), NN)
                dkb = on_k[:C] + dkbgs[h] * gam
                dk = _dot(dboth[h], jnp.concatenate([_mx(kb), _mx(qh)], axis=0), TN) + dkb * b + dkds[h] * rr
                dq = on_k[C:] + dqds[h] * gam
                dgam = rsum(dkbgs[h] * kb) + rsum(dqds[h] * qh)
                dr = rsum(dkds[h] * kh)
                dgc_last = jnp.sum(dr * rr, axis=0, keepdims=True) + dgls[h] * q["gl"]
                dgc = dgcs[h] + dgam * gam - dr * rr + jnp.where(rowi == C - 1, dgc_last, 0.0)
                dbeta = rsum(dvbs[h] * vh) + rsum(dkb * kh)
                dqkv_ref[:, cs[h]] = dq
                dqkv_ref[:, W + h * Dh:W + (h + 1) * Dh] = dk
                dqkv_ref[:, 2 * W + h * Dh:2 * W + (h + 1) * Dh] = dvbs[h] * b
                dbeta_acc = jnp.where(lane == h, dbeta, dbeta_acc)
                dgc_acc = jnp.where(lane == h, dgc, dgc_acc)
        dbeta_ref[...] = dbeta_acc
        dg_ref[...] = _dot(jnp.where(incl, 1.0, 0.0), dgc_acc - colsums.T, TN, HI)
        dng_ref[...] += dng_acc

    rev = lambda n: N - 1 - n
    blk = lambda j: pl.BlockSpec((C, W), lambda n: (rev(n), j))
    small = pl.BlockSpec((C, LANES), lambda n: (rev(n), 0))
    return pl.pallas_call(
        body, name=name, grid=(N,),
        in_specs=[blk(0), blk(1), blk(2), small, small, blk(3), _full((1, Dh)), blk(0),
                  pl.BlockSpec((H, C, C), lambda n: (0, rev(n), 0)),
                  pl.BlockSpec((1, H, Dh, Dh), lambda n: (rev(n), 0, 0, 0)), blk(0)],
        out_specs=[pl.BlockSpec((C, 3 * W), lambda n: (rev(n), 0)), small, small, blk(0), _full((1, Dh))],
        out_shape=[jax.ShapeDtypeStruct((T, 3 * W), F32), jax.ShapeDtypeStruct((T, LANES), F32),
                   jax.ShapeDtypeStruct((T, LANES), F32), jax.ShapeDtypeStruct((T, W), MXU_DTYPE),
                   jax.ShapeDtypeStruct((1, Dh), F32)],
        scratch_shapes=[pltpu.VMEM((H, Dh, Dh), F32)],
        compiler_params=_cp("arbitrary"),
    )(qkv, qkv, qkv, beta, g, proj, norm_g, o, tinv, s_all, dog)


_INV_SQRT2 = 0.7071067811865476
_INV_SQRT_2PI = 0.3989422804014327


def _sg_recompute(zp_ref, bin_ref, lng_ref, lnb_ref):
    E = SG_WIDTH
    zin = zp_ref[...] + bin_ref[...]
    cdf = 0.5 * (1.0 + lax.erf(zin * _INV_SQRT2))
    zz = zin * cdf
    u = zz[:, :E]
    vp = zz[:, E:]
    mu = jnp.mean(vp, axis=-1, keepdims=True)
    xc = vp - mu
    rstd = lax.rsqrt(jnp.mean(xc * xc, axis=-1, keepdims=True) + LN_EPS)
    xhat = xc * rstd
    v = xhat * lng_ref[...] + lnb_ref[...]
    return zin, cdf, u, xhat, rstd, v


def _sg_masked_ws(ws_ref, g):
    return _mx(jnp.where(_tri(SG_CHUNK, False), ws_ref[g], 0.0))


def _sg_fwd(zpre, b_in, ln_g, ln_b, w_s, b_s_t, name):
    T = zpre.shape[0]
    E, G, C, GW = SG_WIDTH, SG_GROUPS, SG_CHUNK, SG_GROUP_W

    def body(zp_ref, bin_ref, lng_ref, lnb_ref, ws_ref, bst_ref, um_ref):
        _, _, u, _, _, v = _sg_recompute(zp_ref, bin_ref, lng_ref, lnb_ref)
        bst = bst_ref[...]
        for g in range(G):
            cs = slice(g * GW, (g + 1) * GW)
            mixed = _dot(_sg_masked_ws(ws_ref, g), _mx(v[:, cs]), NN) + bst[:, g:g + 1]
            um_ref[:, cs] = _mx(u[:, cs] * mixed)

    return pl.pallas_call(
        body, name=name, grid=(T // C,),
        in_specs=[pl.BlockSpec((C, 2 * E), lambda n: (n, 0)), _full((1, 2 * E)), _full((1, E)), _full((1, E)),
                  _full((G, C, C)), _full((C, LANES))],
        out_specs=pl.BlockSpec((C, E), lambda n: (n, 0)),
        out_shape=jax.ShapeDtypeStruct((T, E), MXU_DTYPE), compiler_params=_cp("parallel"),
    )(zpre, b_in, ln_g, ln_b, w_s, b_s_t)


def _sg_bwd(zpre, b_in, ln_g, ln_b, w_s, b_s_t, dum, name):
    T = zpre.shape[0]
    E, G, C, GW = SG_WIDTH, SG_GROUPS, SG_CHUNK, SG_GROUP_W

    def body(zp_ref, bin_ref, lng_ref, lnb_ref, ws_ref, bst_ref, dum_ref,
             dz_ref, dbin_ref, dlng_ref, dlnb_ref, dws_ref, dbst_ref):
        @pl.when(pl.program_id(0) == 0)
        def _():
            for r in (dbin_ref, dlng_ref, dlnb_ref, dws_ref, dbst_ref):
                r[...] = jnp.zeros_like(r)

        zin, cdf, u, xhat, rstd, v = _sg_recompute(zp_ref, bin_ref, lng_ref, lnb_ref)
        bst = bst_ref[...]
        lane = lax.broadcasted_iota(jnp.int32, (C, LANES), 1)
        dum_v = dum_ref[...]
        dbst = jnp.zeros((C, LANES), F32)
        du_parts, dv_parts = [], []
        for g in range(G):
            cs = slice(g * GW, (g + 1) * GW)
            wsm = _sg_masked_ws(ws_ref, g)
            vg = _mx(v[:, cs])
            mixed = _dot(wsm, vg, NN) + bst[:, g:g + 1]
            dumg = dum_v[:, cs]
            du_parts.append(dumg * mixed)
            dmixed = dumg * u[:, cs]
            dmb = _mx(dmixed)
            dv_parts.append(_dot(wsm, dmb, TN))
            dws_ref[g] += _dot(dmb, vg, NT)
            dbst = jnp.where(lane == g, jnp.sum(dmixed, axis=1, keepdims=True), dbst)
        dbst_ref[...] += dbst
        du = jnp.concatenate(du_parts, axis=1)
        dv = jnp.concatenate(dv_parts, axis=1)
        dlng_ref[...] += jnp.sum(dv * xhat, axis=0, keepdims=True)
        dlnb_ref[...] += jnp.sum(dv, axis=0, keepdims=True)
        dxh = dv * lng_ref[...]
        dvp = rstd * (dxh - jnp.mean(dxh, axis=-1, keepdims=True) - xhat * jnp.mean(dxh * xhat, axis=-1, keepdims=True))
        dzz = jnp.concatenate([du, dvp], axis=1)
        dzin = dzz * (cdf + zin * (_INV_SQRT_2PI * jnp.exp(-0.5 * zin * zin)))
        dz_ref[...] = _mx(dzin)
        dbin_ref[...] += jnp.sum(dzin, axis=0, keepdims=True)

    return pl.pallas_call(
        body, name=name, grid=(T // C,),
        in_specs=[pl.BlockSpec((C, 2 * E), lambda n: (n, 0)), _full((1, 2 * E)), _full((1, E)), _full((1, E)),
                  _full((G, C, C)), _full((C, LANES)), pl.BlockSpec((C, E), lambda n: (n, 0))],
        out_specs=[pl.BlockSpec((C, 2 * E), lambda n: (n, 0)), _full((1, 2 * E)), _full((1, E)), _full((1, E)),
                   _full((G, C, C)), _full((C, LANES))],
        out_shape=[jax.ShapeDtypeStruct((T, 2 * E), MXU_DTYPE), jax.ShapeDtypeStruct((1, 2 * E), F32),
                   jax.ShapeDtypeStruct((1, E), F32), jax.ShapeDtypeStruct((1, E), F32),
                   jax.ShapeDtypeStruct((G, C, C), F32), jax.ShapeDtypeStruct((C, LANES), F32)],
        compiler_params=_cp("arbitrary"),
    )(zpre, b_in, ln_g, ln_b, w_s, b_s_t, dum)


def _row(v):
    return v.reshape(1, -1)


def _pad_lanes(v):
    v = v.reshape(1, -1)
    return jnp.pad(v, ((0, 0), (0, LANES - v.shape[1])))


def _local_step(x, target, p, weights_for, grads_ready=None, small_ready=None):
    ng = p["norm_g"]
    grads = {}
    dng = [[None] * 6 for _ in range(2)]
    order = [jnp.zeros((), F32)]

    def tell(group):
        zero = grads_ready(group, grads) if grads_ready is not None else None
        if zero is not None:
            order[0] = zero

    def gain(i, s):
        return _row(ng[i, s]) + order[0]

    def ffn_f(xin, i, j, tag, **tail):
        wt = weights_for("ffn" + tag, xin)
        xo, h, gu, y, *rest = _ffn_fwd(xin, _row(ng[i, 4 * j]), _row(ng[i, 4 * j + 1]), wt, "ffn_fwd_" + tag, **tail)
        return (xo, *rest), (xin, h, gu, y, wt)

    (x1, hn0), sv_f00 = ffn_f(x, 0, 0, "00", next_gain=_row(ng[0, 2]))
    dnw = weights_for("dn", x1)
    proj = _mm(hn0, dnw["dn_wqkvz"], "nn", "dn_proj")
    ba = _mm(hn0, dnw["dn_wba"], "nn", "dn_proj_ba")
    a_log = _pad_lanes(p["dn_a_log"])
    dt_bias = _pad_lanes(p["dn_dt_bias"])
    dn_ng = _row(p["dn_norm_g"])
    qkv = _dn_prep_fwd(proj, p["dn_conv_w"], "dn_prep_fwd")
    beta, gdec = _dn_gate_fwd(ba, a_log, dt_bias, "dn_gate_fwd")
    og, o_raw, tinv, s_all = _dn_scan_fwd(qkv, beta, gdec, proj, dn_ng, "dn_scan_fwd")
    m0, x2 = _out_proj_postnorm(og, dnw["dn_wout"], x1, _row(ng[0, 3]), "dn_out")
    (x3,), sv_f01 = ffn_f(x2, 0, 1, "01")
    (x4, hn1), sv_f10 = ffn_f(x3, 1, 0, "10", next_gain=_row(ng[1, 2]))
    sgw = weights_for("sg", x4)
    zpre = _mm(hn1, sgw["sg_win"], "nn", "sg_proj")
    sg_bin = _row(p["sg_b_in"])
    sg_lng = _row(p["sg_ln_g"])
    sg_lnb = _row(p["sg_ln_b"])
    sg_bst = jnp.pad(p["sg_b_s"].T, ((0, 0), (0, LANES - SG_GROUPS)))
    um = _sg_fwd(zpre, sg_bin, sg_lng, sg_lnb, p["sg_w_s"], sg_bst, "sg_fwd")
    m1, x5 = _out_proj_postnorm(um, sgw["sg_wout"], x4, _row(ng[1, 3]), "sg_out")
    (_, loss_part, dx), sv_f11 = ffn_f(x5, 1, 1, "11", target=target)

    def ffn_b(dxo, sv, i, j, tag, last=False):
        xin, h, gu, y, wt = sv
        dxi, dy, a, dgu, dg0, dg1 = _ffn_bwd(dxo, xin, y, gu, gain(i, 4 * j), gain(i, 4 * j + 1), wt, "ffn_bwd_" + tag)
        dng[i][4 * j] = dg0
        dng[i][4 * j + 1] = dg1
        after = None
        if last:
            grads["norm_g"] = jnp.stack([jnp.concatenate(dng[t], axis=0) for t in range(2)])
            after = small_ready(grads, loss_part) if small_ready is not None else None
        grads["wd" + tag] = _mm(a, dy, "tn", "ffn_wgrad_down_" + tag, after=after)
        grads["wguT" + tag] = _mm(dgu, h, "tn", "ffn_wgrad_up_" + tag, after=after)
        tell("ffn" + tag)
        return dxi

    dx = ffn_b(dx, sv_f11, 1, 1, "11")
    dm1, dng[1][3], dum = _postnorm_bwd_dgrad(dx, m1, gain(1, 3), sgw["sg_wout"], "sg_dgrad_out")
    grads["sg_w_out"] = _mm(um, dm1, "tn", "sg_wgrad_out")
    dz1, dbin, dlng, dlnb, dws, dbst = _sg_bwd(zpre, sg_bin, sg_lng, sg_lnb, p["sg_w_s"], sg_bst, dum, "sg_bwd")
    grads["sg_w_inT"] = _mm(dz1, hn1, "tn", "sg_wgrad_in")
    tell("sg")
    dx, dng[1][2] = _dgrad_prenorm_bwd(dz1, sgw["sg_win"], None, dx, x4, gain(1, 2), "sg_dgrad_in")
    grads["sg_b_in"] = dbin.reshape(1, -1)
    grads["sg_ln_g"] = dlng.reshape(1, -1)
    grads["sg_ln_b"] = dlnb.reshape(1, -1)
    grads["sg_w_s"] = jnp.where(jnp.tril(jnp.ones((SG_CHUNK, SG_CHUNK), bool)), dws, 0.0)[None]
    grads["sg_b_s"] = dbst[:, :SG_GROUPS].T[None]
    dx = ffn_b(dx, sv_f10, 1, 0, "10")
    dx = ffn_b(dx, sv_f01, 0, 1, "01")
    dm0, dng[0][3], dog = _postnorm_bwd_dgrad(dx, m0, gain(0, 3), dnw["dn_wout"], "dn_dgrad_out")
    grads["dn_w_out"] = _mm(og, dm0, "tn", "dn_wgrad_out")
    dqkv, dbeta, dgdec, dz0, dnng = _dn_scan_bwd(qkv, beta, gdec, proj, dn_ng, o_raw, tinv, s_all, dog, "dn_scan_bwd")
    dqkv_pre, dconv = _dn_prep_bwd(proj, p["dn_conv_w"], dqkv, "dn_prep_bwd")
    dba, dal, ddt = _dn_gate_bwd(ba, a_log, dt_bias, dbeta, dgdec, "dn_gate_bwd")
    W3 = 3 * DN_HEADS * DN_HEAD_DIM
    dw_qkv = _mm(hn0, dqkv_pre, "tn", "dn_wgrad_qkv")
    dw_z = _mm(hn0, dz0, "tn", "dn_wgrad_z")
    dw_ba = _mm(hn0, dba, "tn", "dn_wgrad_ba")
    grads["dn_w_in"] = jnp.concatenate(
        [dw_qkv, dw_z, dw_ba[:, :DN_HEADS], dw_ba[:, LANES:LANES + DN_HEADS]], axis=1)
    tell("dn")
    dh0 = _mm(dqkv_pre, dnw["dn_wqkvz"][:, :W3], "nt", "dn_dgrad_qkv")
    dh0 = _mm(dz0, dnw["dn_wqkvz"][:, W3:], "nt", "dn_dgrad_z", add=dh0)
    dx, dng[0][2] = _dgrad_prenorm_bwd(dba, dnw["dn_wba"], dh0, dx, x1, gain(0, 2), "dn_dgrad_ba")
    grads["dn_conv_w"] = dconv[None]
    grads["dn_a_log"] = dal[:, :DN_HEADS]
    grads["dn_dt_bias"] = ddt[:, :DN_HEADS]
    grads["dn_norm_g"] = dnng
    dx = ffn_b(dx, sv_f00, 0, 0, "00", last=True)
    return loss_part, dx, grads


def _mesh_pos():
    return lax.axis_index("x"), lax.axis_index("y"), lax.axis_index("c")


def _other_chips(x, y):
    return [(1 - x, y), (x, 1 - y), (1 - x, 1 - y)]


def _allgather_chips(arrs, name):
    n = len(arrs)

    def body(*refs):
        ins, outs = refs[:n], refs[n:2 * n]
        ici_send, ici_recv, d2d_send, d2d_recv = refs[2 * n:]
        x, y, c = _mesh_pos()
        me = 2 * x + y
        chips = _other_chips(x, y)
        sibling = (x, y, 1 - c)

        def ici(i, j, k):
            cx, cy = chips[j]
            return pltpu.make_async_remote_copy(src_ref=ins[i].at[c], dst_ref=outs[i].at[k, c], send_sem=ici_send.at[3 * i + j],
                                                recv_sem=ici_recv.at[3 * i + j], device_id=(cx, cy, c), device_id_type=MESH)

        def d2d(i, j, h):
            cx, cy = chips[j]
            slot = outs[i].at[2 * cx + cy, h]
            return pltpu.make_async_remote_copy(src_ref=slot, dst_ref=slot, send_sem=d2d_send.at[3 * i + j],
                                                recv_sem=d2d_recv.at[3 * i + j], device_id=sibling, device_id_type=MESH)

        sends = [ici(i, j, me) for i in range(n) for j in range(3)]
        for cp in sends:
            cp.start()
        for i in range(n):
            for j, (cx, cy) in enumerate(chips):
                ici(i, j, 2 * cx + cy).wait_recv()
                fwd = d2d(i, j, c)
                fwd.start()
                sends.append(fwd)
        for i in range(n):
            for j in range(3):
                d2d(i, j, 1 - c).wait_recv()
        for cp in sends:
            cp.wait_send()

    return pl.pallas_call(
        body, name=name, in_specs=[ANY] * n, out_specs=[ANY] * n,
        out_shape=[jax.ShapeDtypeStruct((N_CHIPS,) + a.shape, a.dtype) for a in arrs],
        scratch_shapes=[pltpu.SemaphoreType.DMA((3 * n,))] * 4,
    )(*arrs)


HBM = pl.BlockSpec(memory_space=pltpu.HBM)
SEM = pl.BlockSpec(memory_space=pltpu.SEMAPHORE)
TOKEN = jax.ShapeDtypeStruct((SUBLANES, LANES), F32)


_PEERS = {"gather": 3, "scatter": 3, "swap": 1, "all": N_DEV - 1}


def _land_shape(kind, shape):
    if kind == "gather":
        return (N_CHIPS,) + shape
    if kind == "all":
        return (N_DEV,) + shape
    return (N_CHIPS,) + shape[2:] if kind == "swap" else shape


def _peer_copies(kind, flags, src_refs, land_refs, send_sems, recv_sems, receiving):
    x, y, c = _mesh_pos()
    me4, me8 = 2 * x + y, 4 * x + 2 * y + c
    np_ = _PEERS[kind]
    cps = []
    for i, (src, land) in enumerate(zip(src_refs, land_refs)):
        if kind == "swap":
            half = src.at[1 - c] if flags[i] else src.at[:, 1 - c]
            plan = [((x, y, 1 - c), half, land)]
        elif kind == "all":
            masks = [(mx, my, mc) for mx in (0, 1) for my in (0, 1) for mc in (0, 1)][1:]
            peers = [(jnp.where(mx, 1 - x, x), jnp.where(my, 1 - y, y), jnp.where(mc, 1 - c, c)) for mx, my, mc in masks]
            plan = [(p, src, land.at[4 * p[0] + 2 * p[1] + p[2] if receiving else me8]) for p in peers]
        else:
            plan = []
            for cx, cy in _other_chips(x, y):
                k = 2 * cx + cy
                s = src.at[me4 if receiving else k] if kind == "scatter" else src
                plan.append(((cx, cy, c), s, land.at[k if receiving else me4]))
        for j, (peer, s, d) in enumerate(plan):
            cps.append(pltpu.make_async_remote_copy(src_ref=s, dst_ref=d, send_sem=send_sems.at[np_ * i + j],
                                                    recv_sem=recv_sems.at[np_ * i + j], device_id=peer, device_id_type=MESH))
    return cps


def _copies_start(kind, srcs, after, name, flags=None):
    n = len(srcs)
    ns = _PEERS[kind] * n
    lands = [lax.empty(_land_shape(kind, s.shape), s.dtype) for s in srcs]
    after = [] if after is None else [after]

    def body(*refs):
        src_refs, land_refs = refs[:n], refs[n:2 * n]
        send_sems, recv_sems = refs[2 * n + len(after)], refs[2 * n + len(after) + 1]
        token = refs[-1]
        for cp in _peer_copies(kind, flags, src_refs, land_refs, send_sems, recv_sems, False):
            cp.start()
        token[...] = jnp.zeros_like(token)

    outs = pl.pallas_call(
        body, name=name,
        in_specs=[HBM] * (2 * n) + [ANY] * len(after),
        out_specs=(SEM, SEM) + (HBM,) * (2 * n) + (pl.BlockSpec(memory_space=pltpu.VMEM),),
        out_shape=(pltpu.SemaphoreType.DMA((ns,)), pltpu.SemaphoreType.DMA((ns,)))
        + tuple(pltpu.HBM(a.shape, a.dtype) for a in list(srcs) + lands) + (TOKEN,),
        input_output_aliases={i: 2 + i for i in range(2 * n)},
        compiler_params=pltpu.CompilerParams(has_side_effects=pltpu.SideEffectType.DATAFLOW_SIDE_EFFECTING),
    )(*[pltpu.with_memory_space_constraint(a, pltpu.HBM) for a in list(srcs) + lands], *after)
    return dict(sems=outs[:2], srcs=outs[2:2 + n], lands=outs[2 + n:2 + 2 * n], token=outs[-1], kind=kind, flags=flags)


def _copies_wait(started, after, name):
    n = len(started["srcs"])
    kind, flags = started["kind"], started["flags"]
    after = list(after) if isinstance(after, (list, tuple)) else [after]

    def body(*refs):
        src_refs, land_refs = refs[:n], refs[n:2 * n]
        send_sems, recv_sems = refs[2 * n], refs[2 * n + 1]
        for cp in _peer_copies(kind, flags, src_refs, land_refs, send_sems, recv_sems, True):
            cp.wait_send()
            cp.wait_recv()

    outs = pl.pallas_call(
        body, name=name,
        in_specs=[HBM] * (2 * n) + [SEM, SEM] + [ANY] * len(after),
        out_specs=(HBM,) * (2 * n),
        out_shape=tuple(pltpu.HBM(a.shape, a.dtype) for a in list(started["srcs"]) + list(started["lands"])),
        input_output_aliases={i: i for i in range(2 * n)},
        compiler_params=pltpu.CompilerParams(has_side_effects=pltpu.SideEffectType.DATAFLOW_SIDE_EFFECTING),
    )(*started["srcs"], *started["lands"], *started["sems"], *after)
    return outs[:n], outs[n:]


def _swap_whole(arrs, name):
    n = len(arrs)

    def body(*refs):
        ins, outs = refs[:n], refs[n:2 * n]
        send_sems, recv_sems = refs[2 * n:]
        x, y, c = _mesh_pos()
        cps = [pltpu.make_async_remote_copy(src_ref=ins[i], dst_ref=outs[i], send_sem=send_sems.at[i],
                                            recv_sem=recv_sems.at[i], device_id=(x, y, 1 - c), device_id_type=MESH)
               for i in range(n)]
        for cp in cps:
            cp.start()
        for cp in cps:
            cp.wait()

    return pl.pallas_call(
        body, name=name, in_specs=[ANY] * n, out_specs=[ANY] * n,
        out_shape=[jax.ShapeDtypeStruct(a.shape, a.dtype) for a in arrs],
        scratch_shapes=[pltpu.SemaphoreType.DMA((n,)), pltpu.SemaphoreType.DMA((n,))],
    )(*arrs)


def _as_rows(a, lead):
    shp = a.shape
    rows = 1
    for s in shp[lead:-1]:
        rows *= s
    return a.reshape(shp[:lead] + (rows, shp[-1]))


def _row_tile(rows, cols, n_bufs):
    budget = (24 * 1024 * 1024) // (n_bufs * 2 * 4 * cols)
    return _pick(rows, max(2 * SUBLANES, budget), 2 * SUBLANES)


def _sum_devices(own, got, dev, name):
    n, rows, cols = got.shape
    tr = _row_tile(rows, cols, n + 2)

    def body(dev_ref, own_ref, got_ref, o_ref):
        mine = own_ref[...]
        acc = jnp.where(dev_ref[0] == 0, mine, got_ref[0])
        for k in range(1, n):
            acc = acc + jnp.where(dev_ref[0] == k, mine, got_ref[k])
        o_ref[...] = acc

    return pl.pallas_call(
        body, name=name,
        grid_spec=pltpu.PrefetchScalarGridSpec(
            num_scalar_prefetch=1, grid=(rows // tr,),
            in_specs=[pl.BlockSpec((tr, cols), lambda i, d: (i, 0)), pl.BlockSpec((n, tr, cols), lambda i, d: (0, i, 0))],
            out_specs=pl.BlockSpec((tr, cols), lambda i, d: (i, 0))),
        out_shape=jax.ShapeDtypeStruct((rows, cols), F32), compiler_params=_cp("parallel"),
    )(_scalar(dev), own, got)


def _scalar(i):
    return jnp.reshape(i, (1,)).astype(jnp.int32)


def _add_own_half(g, other, c, half_first, name):
    _, rows, cols = other.shape
    tr = _row_tile(rows, cols, 3)

    def body(c_ref, g_ref, o_ref, out_ref):
        out_ref[0] = (g_ref[0, 0] + o_ref[0]).astype(out_ref.dtype)

    if half_first:
        g_map = lambda k, i, c_ref: (c_ref[0], k, i, 0)
    else:
        g_map = lambda k, i, c_ref: (k, c_ref[0], i, 0)
    flat = pl.BlockSpec((1, tr, cols), lambda k, i, c_ref: (k, i, 0))
    return pl.pallas_call(
        body, name=name,
        grid_spec=pltpu.PrefetchScalarGridSpec(
            num_scalar_prefetch=1, grid=(N_CHIPS, rows // tr),
            in_specs=[pl.BlockSpec((1, 1, tr, cols), g_map), flat], out_specs=flat),
        out_shape=jax.ShapeDtypeStruct(other.shape, COMM_DTYPE), compiler_params=_cp("parallel", "parallel"),
    )(_scalar(c), g, other)


def _sum_chips(own, got, chip, name, transpose=False):
    _, rows, cols = own.shape
    tr = rows if transpose else _row_tile(rows, cols, N_CHIPS + 2)

    def body(chip_ref, p_ref, b_ref, o_ref):
        mine = p_ref[0].astype(F32)
        acc = jnp.where(chip_ref[0] == 0, mine, b_ref[0].astype(F32))
        for k in range(1, N_CHIPS):
            acc = acc + jnp.where(chip_ref[0] == k, mine, b_ref[k].astype(F32))
        o_ref[...] = acc.T if transpose else acc

    if transpose:
        out_spec, out_shape = pl.BlockSpec((cols, rows), lambda i, k_ref: (0, 0)), (cols, rows)
    else:
        out_spec, out_shape = pl.BlockSpec((tr, cols), lambda i, k_ref: (i, 0)), (rows, cols)
    return pl.pallas_call(
        body, name=name,
        grid_spec=pltpu.PrefetchScalarGridSpec(
            num_scalar_prefetch=1, grid=(rows // tr,),
            in_specs=[pl.BlockSpec((1, tr, cols), lambda i, k_ref: (k_ref[0], i, 0)),
                      pl.BlockSpec((N_CHIPS, tr, cols), lambda i, k_ref: (0, i, 0))],
            out_specs=out_spec),
        out_shape=jax.ShapeDtypeStruct(out_shape, F32), compiler_params=_cp("parallel"),
    )(_scalar(chip), own, got)


def _adam_math(w, g, m, v):
    nm = ADAM_B1 * m + (1.0 - ADAM_B1) * g
    nv = ADAM_B2 * v + (1.0 - ADAM_B2) * (g * g)
    m_hat = nm / (1.0 - ADAM_B1 ** ADAM_STEP)
    v_hat = nv / (1.0 - ADAM_B2 ** ADAM_STEP)
    return -ADAM_LR * (m_hat / (jnp.sqrt(v_hat) + ADAM_EPS) + ADAM_WD * w), nm, nv


def _adamw_pieces(w, m, v, mine, theirs, c, kind, name):
    shape = w.shape
    P = len(mine)
    ws, ms, vs = (t.reshape((P, -1, t.shape[-1])) for t in (w, m, v))
    _, R, C = ws.shape
    if kind == "rows":
        tr = _pick(R // 2, 512, SUBLANES)
    else:
        tr = _pick(R, 256 if kind in ("lo", "hi") else 512, SUBLANES)
    nt = R // tr
    nh = nt // 2

    def body(c_ref, w_ref, m_ref, v_ref, *refs):
        mine_refs, theirs_refs = refs[:P], refs[P:2 * P]
        g_ref, d_ref, nm_ref, nv_ref = refs[2 * P:]
        p, i, core = pl.program_id(0), pl.program_id(1), c_ref[0]

        def pick(refs_):
            out = refs_[0][...]
            for q in range(1, P):
                out = jnp.where(p == q, refs_[q][...], out)
            return out

        a, b = pick(mine_refs), pick(theirs_refs)
        if kind == "cols":
            gv = jnp.where(core == 0, jnp.concatenate([a, b], axis=1), jnp.concatenate([b, a], axis=1))
        else:
            own = {"lo": core == 0, "hi": core == 1, "rows": (i >= nh) == (core == 1)}[kind]
            gv = jnp.where(own, a, b)
        g_ref[0] = gv
        d_ref[0], nm_ref[0], nv_ref[0] = _adam_math(w_ref[0], gv, m_ref[0], v_ref[0])

    def piece_spec(q):
        tile = (lambda i: i - jnp.where(i >= nh, nh, 0)) if kind == "rows" else (lambda i: i)
        return pl.BlockSpec((tr, mine[q].shape[1]), lambda p, i, c_ref: (jnp.where(p == q, tile(i), 0), 0))

    full = pl.BlockSpec((1, tr, C), lambda p, i, c_ref: (p, i, 0))
    outs = pl.pallas_call(
        body, name=name,
        grid_spec=pltpu.PrefetchScalarGridSpec(num_scalar_prefetch=1, grid=(P, nt),
                                               in_specs=[full] * 3 + [piece_spec(q) for q in range(P)] * 2,
                                               out_specs=[full] * 4),
        out_shape=[jax.ShapeDtypeStruct((P, R, C), F32)] * 4, compiler_params=_cp("parallel", "arbitrary"),
    )(_scalar(c), ws, ms, vs, *mine, *theirs)
    return tuple(o.reshape(shape) for o in outs)


def _adamw(w, g, m, v, name):
    shape = w.shape
    ws, gs, ms, vs = (_as_rows(t, 0) for t in (w, g, m, v))
    rows, cols = ws.shape
    tr = _row_tile(rows, cols, 7)

    def body(w_ref, g_ref, m_ref, v_ref, d_ref, nm_ref, nv_ref):
        d_ref[...], nm_ref[...], nv_ref[...] = _adam_math(w_ref[...], g_ref[...], m_ref[...], v_ref[...])

    spec = pl.BlockSpec((tr, cols), lambda i: (i, 0))
    outs = pl.pallas_call(body, name=name, grid=(rows // tr,), in_specs=[spec] * 4, out_specs=[spec] * 3,
                          out_shape=[jax.ShapeDtypeStruct((rows, cols), F32)] * 3, compiler_params=_cp("parallel"))(ws, gs, ms, vs)
    return tuple(o.reshape(shape) for o in outs)


_BIG = ["ffn_w_gate", "ffn_w_up", "ffn_w_down", "dn_w_in", "dn_w_out", "sg_w_in", "sg_w_out"]
_SMALL_SHARDED = ["norm_g", "dn_conv_w", "sg_b_in", "sg_ln_g", "sg_ln_b"]
_SMALL_REPL = ["dn_a_log", "dn_dt_bias", "dn_norm_g", "sg_w_s", "sg_b_s"]
_WEIGHTS = ["norm_g", "ffn_w_gate", "ffn_w_up", "ffn_w_down", "dn_w_in", "dn_conv_w", "dn_a_log", "dn_dt_bias",
            "dn_norm_g", "dn_w_out", "sg_w_in", "sg_b_in", "sg_ln_g", "sg_ln_b", "sg_w_s", "sg_b_s", "sg_w_out"]
PACK_COLS = 1024


def _pack(arrs):
    flat = jnp.concatenate([a.reshape(-1) for a in arrs])
    pad = (-flat.shape[0]) % (SUBLANES * PACK_COLS)
    return jnp.pad(flat, (0, pad)).reshape(-1, PACK_COLS)


def _unpack(buf, shapes):
    flat = buf.reshape(-1)
    out, off = [], 0
    for s in shapes:
        n = math.prod(s)
        out.append(flat[off:off + n].reshape(s))
        off += n
    return out


def _as_halves(a):
    if a.shape[0] == 2:
        return a
    if a.shape[0] == 1:
        return a.reshape((2, a.shape[1] // 2) + a.shape[2:])
    return a.reshape((2, a.shape[0] // 2) + a.shape[1:])


def _with_own(gathered, own, chip):
    g = gathered.reshape((N_CHIPS,) + own.shape)
    return [jnp.where(chip == k, own, g[k]) for k in range(N_CHIPS)]


def _cat_shards(g, axis):
    return jnp.concatenate(list(g), axis=axis)


_GROUP_ORDER = ["ffn00", "dn", "ffn01", "ffn10", "sg", "ffn11"]
_REDUCE_AFTER = ("sg", "ffn01", "ffn00")


def _weight_groups(w):
    cast = {k: _mx(w[k]) for k in _BIG}
    groups = {"ffn%d%d" % (i, j): [cast["ffn_w_gate"][i, j].T, cast["ffn_w_up"][i, j].T, cast["ffn_w_down"][i, j]]
              for i, j in [(0, 0), (0, 1), (1, 0), (1, 1)]}
    groups["dn"] = [cast["dn_w_in"][0], cast["dn_w_out"][0]]
    groups["sg"] = [cast["sg_w_in"][0], cast["sg_w_out"][0]]
    return groups


def _ffn_weights(chip, own, gathered):
    pairs = [(a, g.reshape((N_CHIPS,) + a.shape)) for a, g in zip(own, gathered)]
    return {"chip": chip, "gate": pairs[0], "up": pairs[1], "down": pairs[2]}


def _group_matrices(group, shards):
    if group == "sg":
        return {"sg_win": _cat_shards(shards[0], 1), "sg_wout": _cat_shards(shards[1], 0)}
    dn_full = _cat_shards(shards[0], 1)
    W4 = 4 * DN_HEADS * DN_HEAD_DIM
    wba = jnp.zeros((D_MODEL, 2 * LANES), dn_full.dtype)
    wba = wba.at[:, :DN_HEADS].set(dn_full[:, W4:W4 + DN_HEADS])
    wba = wba.at[:, LANES:LANES + DN_HEADS].set(dn_full[:, W4 + DN_HEADS:])
    return {"dn_wqkvz": dn_full[:, :W4], "dn_wba": wba, "dn_wout": _cat_shards(shards[1], 0)}


def _split_cols(a, n):
    w = a.shape[-1] // n
    return [a[..., k * w:(k + 1) * w] for k in range(n)]


def _split_rows(a, n):
    h = a.shape[-2] // n
    return [a[..., k * h:(k + 1) * h, :] for k in range(n)]


_IJ = [(0, 0), (0, 1), (1, 0), (1, 1)]


def _group_grads(group, grads):
    def rows_by_chip(a):
        return a.reshape(N_CHIPS, 2, a.shape[0] // (2 * N_CHIPS), a.shape[1])

    if group.startswith("ffn"):
        tag = group[3:]
        t = grads["wguT" + tag]
        return (["wguT" + tag, "wd" + tag],
                [t.reshape(2, N_CHIPS, t.shape[0] // (2 * N_CHIPS), t.shape[1]), rows_by_chip(grads["wd" + tag])], [True, False])
    if group == "sg":
        return ["sg_w_inT", "sg_w_out"], [rows_by_chip(grads["sg_w_inT"]), rows_by_chip(grads["sg_w_out"])], [False, False]
    dn_in = jnp.stack([jnp.stack(_split_cols(hf, N_CHIPS)) for hf in _split_rows(grads["dn_w_in"], 2)])
    return ["dn_w_in", "dn_w_out"], [dn_in, rows_by_chip(grads["dn_w_out"])], [True, False]


_SHARD_PIECES = {
    "ffn_w_gate": (["wguT%d%d" % ij for ij in _IJ], "lo"),
    "ffn_w_up": (["wguT%d%d" % ij for ij in _IJ], "hi"),
    "ffn_w_down": (["wd%d%d" % ij for ij in _IJ], "rows"),
    "dn_w_in": (["dn_w_in"], "rows"),
    "dn_w_out": (["dn_w_out"], "rows"),
    "sg_w_in": (["sg_w_inT"], "cols"),
    "sg_w_out": (["sg_w_out"], "rows"),
}


def kernel(x, norm_g, ffn_w_gate, ffn_w_up, ffn_w_down, dn_w_in, dn_conv_w, dn_a_log, dn_dt_bias, dn_norm_g, dn_w_out, sg_w_in, sg_b_in, sg_ln_g, sg_ln_b, sg_w_s, sg_b_s, sg_w_out, loss_target, m_norm_g, m_ffn_w_gate, m_ffn_w_up, m_ffn_w_down, m_dn_w_in, m_dn_conv_w, m_dn_a_log, m_dn_dt_bias, m_dn_norm_g, m_dn_w_out, m_sg_w_in, m_sg_b_in, m_sg_ln_g, m_sg_ln_b, m_sg_w_s, m_sg_b_s, m_sg_w_out, v_norm_g, v_ffn_w_gate, v_ffn_w_up, v_ffn_w_down, v_dn_w_in, v_dn_conv_w, v_dn_a_log, v_dn_dt_bias, v_dn_norm_g, v_dn_w_out, v_sg_w_in, v_sg_b_in, v_sg_ln_g, v_sg_ln_b, v_sg_w_s, v_sg_b_s, v_sg_w_out):
    args = dict(locals())
    w = {k: args[k] for k in _WEIGHTS}
    mom = {k: args["m_" + k] for k in _WEIGHTS}
    var = {k: args["v_" + k] for k in _WEIGHTS}
    cx, cy, cc = _mesh_pos()
    chip = 2 * cx + cy

    small_shapes = [w[k].shape for k in _SMALL_SHARDED]
    groups = _weight_groups(w)
    own = groups[_GROUP_ORDER[0]] + [_pack([w[k] for k in _SMALL_SHARDED])]
    first = _allgather_chips([_as_halves(a) for a in own], "gather_first")
    started, after = {}, first[0]
    for g in _GROUP_ORDER[1:]:
        started[g] = _copies_start("gather", groups[g], after, "gather_start_" + g)
        after = started[g]["token"]
    small_k = [_unpack(pack, small_shapes) for pack in _with_own(first[-1], own[-1], chip)]
    p = {name: jnp.concatenate([small_k[k][i] for k in range(N_CHIPS)], axis=-1) for i, name in enumerate(_SMALL_SHARDED)}
    p = {k: (v if k == "norm_g" else v[0]) for k, v in p.items()}
    p["norm_g"] = p["norm_g"] + after[0, 0]
    for k in _SMALL_REPL:
        p[k] = w[k][0]

    def weights_for(group, after):
        if group == _GROUP_ORDER[0]:
            return _ffn_weights(chip, own[:-1], first[:-1])
        srcs, lands = _copies_wait(started[group], after, "gather_wait_" + group)
        if group.startswith("ffn"):
            return _ffn_weights(chip, srcs, lands)
        return _group_matrices(group, [_with_own(l, a, chip) for l, a in zip(lands, srcs)])

    mine, theirs, to_core, to_chips = {}, {}, [], []

    def send_to_chips(after):
        group, names, flags, swap = to_core.pop(0)
        halves, got = _copies_wait(swap, after, "swap_wait_" + group)
        pair_sum = [_add_own_half(h, o, cc, hf, "pair_sum_" + n) for n, h, o, hf in zip(names, halves, got, flags)]
        scatter = _copies_start("scatter", pair_sum, got[0], "reduce_start_" + group)
        to_chips.append((group, names, scatter))
        return scatter["token"]

    def finish(after):
        group, names, scatter = to_chips.pop(0)
        pair_sum, got = _copies_wait(scatter, after, "reduce_wait_" + group)
        half_sum = [_sum_chips(a, b, chip, "chip_sum_" + n, transpose=n == "sg_w_inT")
                    for n, a, b in zip(names, pair_sum, got)]
        other = _swap_whole(half_sum, "gather_core_pair_" + group)
        mine.update(zip(names, half_sum))
        theirs.update(zip(names, other))

    bucket = [[], [], []]

    def grads_ready(group, grads):
        for acc, new in zip(bucket, _group_grads(group, grads)):
            acc.extend(new)
        if group not in _REDUCE_AFTER:
            return None
        names, halves, flags = (list(b) for b in bucket)
        for b in bucket:
            b.clear()
        swap = _copies_start("swap", halves, None, "swap_start_" + group, flags)
        token = swap["token"]
        if to_core:
            token = send_to_chips(token)
            if len(to_chips) > 1:
                finish(token)
        to_core.append((group, names, flags, swap))
        return token[0, 0]

    small_names = _SMALL_SHARDED + _SMALL_REPL
    small = {}

    def small_ready(grads, loss_part):
        parts = [grads[k] for k in small_names]
        small["shapes"] = [g.shape for g in parts] + [(1,)]
        pack = _pack(parts + [loss_part[0, :1]])
        small["exchange"] = _copies_start("all", [pack], None, "small_start")
        return small["exchange"]["token"]

    loss_part, grad_x, grads = _local_step(x[0], loss_target[0], p, weights_for, grads_ready, small_ready)
    token = send_to_chips(to_core[0][3]["token"])
    finish(token)
    (pack,), (packs,) = _copies_wait(small["exchange"], list(theirs.values()), "small_wait")
    summed = _sum_devices(pack, packs, 4 * cx + 2 * cy + cc, "small_sum")
    parts = _unpack(summed, small["shapes"])
    loss = parts[-1][0]
    grad = {}
    for i, k in enumerate(small_names):
        g = parts[i]
        if k in _SMALL_SHARDED:
            n = w[k].shape[-1]
            g = lax.dynamic_slice_in_dim(g, chip * n, n, axis=g.ndim - 1)
        grad[k] = g

    delta, new_m, new_v = {}, {}, {}

    def update(keys):
        for k in keys:
            names, kind = _SHARD_PIECES[k]
            turn = (lambda a: jnp.swapaxes(a, -1, -2)) if names[0].startswith("wguT") else (lambda a: a)
            outs = _adamw_pieces(turn(w[k]), turn(mom[k]), turn(var[k]), [mine[n] for n in names], [theirs[n] for n in names],
                                 cc, kind, "adamw_" + k)
            grad[k], delta[k], new_m[k], new_v[k] = (turn(o) for o in outs)

    shapes = [w[k].shape for k in small_names]
    d, nm, nv = _adamw(_pack([w[k] for k in small_names]), _pack([grad[k] for k in small_names]),
                       _pack([mom[k] for k in small_names]), _pack([var[k] for k in small_names]), "adamw_small")
    for k, a, b, c_ in zip(small_names, _unpack(d, shapes), _unpack(nm, shapes), _unpack(nv, shapes)):
        delta[k], new_m[k], new_v[k] = a, b, c_
    early = [k for k in _BIG if all(n in mine for n in _SHARD_PIECES[k][0])]
    update(early)
    finish([d] + [delta[k] for k in early] + list(theirs.values()))
    update([k for k in _BIG if k not in early])

    return (loss, grad_x[None], *[grad[k] for k in _WEIGHTS], *[delta[k] for k in _WEIGHTS],
            *[new_m[k] for k in _WEIGHTS], *[new_v[k] for k in _WEIGHTS])
```

```python
import math

import jax
import jax.numpy as jnp
from jax import lax
from jax.experimental import pallas as pl
from jax.experimental.pallas import tpu as pltpu

F32 = jnp.float32
MXU_DTYPE = jnp.bfloat16
COMM_DTYPE = jnp.bfloat16
HI = lax.Precision.HIGHEST
TRI_PREC = lax.Precision.HIGH

D_MODEL = 1024
RMS_EPS = 1e-6
LN_EPS = 1e-5
L2_EPS = 1e-6
DN_HEADS = 8
DN_HEAD_DIM = 128
DN_CONV = 4
DN_CHUNK = 64
SG_WIDTH = 2048
SG_GROUPS = 8
SG_CHUNK = 128
SG_GROUP_W = SG_WIDTH // SG_GROUPS
N_CHIPS = 4
N_DEV = 8
LANES = 128
SUBLANES = 8
VMEM_LIMIT = 56 * 1024 * 1024

ADAM_LR = 0.001
ADAM_B1 = 0.9
ADAM_B2 = 0.999
ADAM_EPS = 1e-08
ADAM_WD = 0.01
ADAM_STEP = 10

MESH = pl.DeviceIdType.MESH
ANY = pl.BlockSpec(memory_space=pl.ANY)


def _cp(*sem):
    return pltpu.CompilerParams(dimension_semantics=sem, vmem_limit_bytes=VMEM_LIMIT)


def _pick(n, pref, mult=LANES):
    best = None
    d = mult
    while d <= min(n, pref):
        if n % d == 0:
            best = d
        d += mult
    return best if best is not None else n


def _full(shape):
    nd = len(shape)
    return pl.BlockSpec(shape, lambda *_: (0,) * nd)


def _sigmoid(x):
    return 1.0 / (1.0 + jnp.exp(-x))


def _dot(a, b, dims, prec=None):
    return lax.dot_general(a, b, (dims, ((), ())), preferred_element_type=F32, precision=prec)


NN = ((1,), (0,))
NT = ((1,), (1,))
TN = ((0,), (0,))


def _mx(a):
    return a.astype(MXU_DTYPE)


def _rms_stat(x):
    return lax.rsqrt(jnp.mean(x * x, axis=-1, keepdims=True) + RMS_EPS)


def _rms_bwd(x, r, g, dy):
    xh = x * r
    dxh = dy * g
    dx = r * (dxh - xh * jnp.mean(dxh * xh, axis=-1, keepdims=True))
    return dx, jnp.sum(dy * xh, axis=0, keepdims=True)


def _mm(a, b, mode, name, out_dtype=F32, add=None, after=None):
    if mode == "tn":
        K, M = a.shape
        N = b.shape[1]
    elif mode == "nt":
        M, K = a.shape
        N = b.shape[0]
    else:
        M, K = a.shape
        N = b.shape[1]
    tn = _pick(N, 1024)
    if mode == "tn":
        tm = _pick(M, 1024 if tn <= 512 else 1408)
        tk = _pick(K, 1024, SUBLANES)
    else:
        tm = _pick(M, max(512, min(2048, (1024 * 1024) // tn)), SUBLANES)
        tk = _pick(K, 2048)
    nk = K // tk
    grid = (N // tn, M // tm, nk)
    if mode == "nn":
        a_spec = pl.BlockSpec((tm, tk), lambda j, i, k: (i, k))
        b_spec = pl.BlockSpec((tk, tn), lambda j, i, k: (k, j))
        dims = NN
    elif mode == "nt":
        a_spec = pl.BlockSpec((tm, tk), lambda j, i, k: (i, k))
        b_spec = pl.BlockSpec((tn, tk), lambda j, i, k: (j, k))
        dims = NT
    else:
        a_spec = pl.BlockSpec((tk, tm), lambda j, i, k: (k, i))
        b_spec = pl.BlockSpec((tk, tn), lambda j, i, k: (k, j))
        dims = TN
    o_spec = pl.BlockSpec((tm, tn), lambda j, i, k: (i, j))
    has_add = add is not None

    def body(*refs):
        a_ref, b_ref = refs[:2]
        add_ref = refs[2] if has_add else None
        o_ref, acc = refs[-2:]
        k = pl.program_id(2)

        @pl.when(k == 0)
        def _():
            acc[...] = add_ref[...] if has_add else jnp.zeros_like(acc)

        acc[...] += _dot(a_ref[...], b_ref[...], dims)

        @pl.when(k == nk - 1)
        def _():
            o_ref[...] = acc[...].astype(o_ref.dtype)

    ins = [a, b] + ([add] if has_add else []) + ([after] if after is not None else [])
    specs = [a_spec, b_spec] + ([o_spec] if has_add else []) + ([ANY] if after is not None else [])
    return pl.pallas_call(
        body, name=name, grid=grid, in_specs=specs, out_specs=o_spec,
        out_shape=jax.ShapeDtypeStruct((M, N), out_dtype),
        scratch_shapes=[pltpu.VMEM((tm, tn), F32)],
        compiler_params=_cp("parallel", "parallel", "arbitrary"),
    )(*ins)


def _ffn_weight_operands(wt):
    return [_scalar(wt["chip"])] , [wt["gate"][0], wt["gate"][1], wt["up"][0], wt["up"][1], wt["down"][0], wt["down"][1]]


def _load_ffn_weights(chip_ref, shard_refs, wgu_v, wd_v, sem):
    fs = wd_v.shape[0] // N_CHIPS

    @pl.when(pl.program_id(0) == 0)
    def _():
        me = chip_ref[0]
        waits = []
        for t, (dst, base) in enumerate([(wgu_v, 0), (wgu_v, wd_v.shape[0]), (wd_v, 0)]):
            own, gathered = shard_refs[2 * t], shard_refs[2 * t + 1]
            for k in range(N_CHIPS):
                slot = dst.at[pl.ds(base + k * fs, fs), :]
                s = sem.at[t * N_CHIPS + k]

                @pl.when(me == k)
                def _(own=own, slot=slot, s=s):
                    pltpu.make_async_copy(own, slot, s).start()

                @pl.when(me != k)
                def _(gathered=gathered, k=k, slot=slot, s=s):
                    pltpu.make_async_copy(gathered.at[k], slot, s).start()

                waits.append(pltpu.make_async_copy(own, slot, s))
        for cp in waits:
            cp.wait()


def _ffn_fwd(x, g0, g1, wt, name, next_gain=None, target=None):
    T, D = x.shape
    F = N_CHIPS * wt["down"][0].shape[0]
    F2 = 2 * F
    tm = _pick(T, 256, SUBLANES)
    prefetch, shards = _ffn_weight_operands(wt)
    extra = [a for a in (next_gain, target) if a is not None]
    n_tail = 1 if next_gain is not None else (2 if target is not None else 0)

    def body(chip_ref, x_ref, g0_ref, g1_ref, *refs):
        extra_ref = refs[0] if extra else None
        refs = refs[len(extra):]
        shard_refs = refs[:6]
        xo_ref, h_ref, gu_ref, y_ref = refs[6:10]
        tail_refs = refs[10:10 + n_tail]
        wgu_v, wd_v, sem = refs[10 + n_tail:]
        _load_ffn_weights(chip_ref, shard_refs, wgu_v, wd_v, sem)
        xv = x_ref[...]
        hb = _mx(xv * _rms_stat(xv) * g0_ref[...])
        h_ref[...] = hb
        gu = _dot(hb, wgu_v[...], NT)
        gu_ref[...] = gu.astype(gu_ref.dtype)
        g = gu[:, :F]
        u = gu[:, F:]
        a = _mx(g * _sigmoid(g) * u)
        y = _dot(a, wd_v[...], NN)
        y_ref[...] = y
        xo = xv + 0.5 * (y * _rms_stat(y) * g1_ref[...])
        xo_ref[...] = xo
        if next_gain is not None:
            tail_refs[0][...] = _mx(xo * _rms_stat(xo) * extra_ref[...])
        if target is not None:
            loss_ref, dy_ref = tail_refs

            @pl.when(pl.program_id(0) == 0)
            def _():
                loss_ref[...] = jnp.zeros_like(loss_ref)

            e = xo - extra_ref[...]
            dy_ref[...] = e * (1.0 / D)
            loss_ref[...] += 0.5 * jnp.sum(jnp.mean(e * e, axis=-1, keepdims=True), axis=0, keepdims=True)

    row = lambda w: pl.BlockSpec((tm, w), lambda i, c: (i, 0))
    one = pl.BlockSpec((1, D), lambda i, c: (0, 0))
    tail_specs, tail_shapes, extra_specs = [], [], []
    if next_gain is not None:
        extra_specs, tail_specs, tail_shapes = [one], [row(D)], [jax.ShapeDtypeStruct((T, D), MXU_DTYPE)]
    if target is not None:
        extra_specs = [row(D)]
        tail_specs = [pl.BlockSpec((SUBLANES, LANES), lambda i, c: (0, 0)), row(D)]
        tail_shapes = [jax.ShapeDtypeStruct((SUBLANES, LANES), F32), jax.ShapeDtypeStruct((T, D), F32)]
    return pl.pallas_call(
        body, name=name,
        grid_spec=pltpu.PrefetchScalarGridSpec(
            num_scalar_prefetch=1, grid=(T // tm,),
            in_specs=[row(D), one, one] + extra_specs + [ANY] * 6,
            out_specs=[row(D), row(D), row(F2), row(D)] + tail_specs,
            scratch_shapes=[pltpu.VMEM((F2, D), MXU_DTYPE), pltpu.VMEM((F, D), MXU_DTYPE),
                            pltpu.SemaphoreType.DMA((3 * N_CHIPS,))]),
        out_shape=[jax.ShapeDtypeStruct((T, D), F32), jax.ShapeDtypeStruct((T, D), MXU_DTYPE),
                   jax.ShapeDtypeStruct((T, F2), MXU_DTYPE), jax.ShapeDtypeStruct((T, D), F32)] + tail_shapes,
        compiler_params=_cp("arbitrary"),
    )(*prefetch, x, g0, g1, *extra, *shards)


FFN_BWD_CHUNK = 2816


def _ffn_bwd(dxo, x, y, gu, g0, g1, wt, name):
    T, D = x.shape
    F2 = gu.shape[1]
    F = F2 // 2
    tm = _pick(T, 256, SUBLANES)
    fc = _pick(F, FFN_BWD_CHUNK)
    prefetch, shards = _ffn_weight_operands(wt)

    def body(chip_ref, dxo_ref, x_ref, y_ref, gu_ref, g0_ref, g1_ref, *refs):
        shard_refs = refs[:6]
        dx_ref, dy_ref, a_ref, dgu_ref, dg0_ref, dg1_ref, wgu_v, wd_v, sem = refs[6:]
        _load_ffn_weights(chip_ref, shard_refs, wgu_v, wd_v, sem)

        @pl.when(pl.program_id(0) == 0)
        def _():
            dg0_ref[...] = jnp.zeros_like(dg0_ref)
            dg1_ref[...] = jnp.zeros_like(dg1_ref)

        dxo_v = dxo_ref[...]
        yv = y_ref[...]
        dy, dg1 = _rms_bwd(yv, _rms_stat(yv), g1_ref[...], 0.5 * dxo_v)
        dg1_ref[...] += dg1
        dyb = _mx(dy)
        dy_ref[...] = dyb
        dh = jnp.zeros((tm, D), F32)
        for c in range(F // fc):
            lo, hi = c * fc, (c + 1) * fc
            da = _dot(dyb, wd_v[lo:hi, :], NT)
            g = gu_ref[:, lo:hi].astype(F32)
            u = gu_ref[:, F + lo:F + hi].astype(F32)
            s = _sigmoid(g)
            sg = g * s
            a_ref[:, lo:hi] = _mx(sg * u)
            dg = _mx(da * u * (s * (1.0 + g * (1.0 - s))))
            du = _mx(da * sg)
            dgu_ref[:, lo:hi] = dg
            dgu_ref[:, F + lo:F + hi] = du
            dh = dh + _dot(dg, wgu_v[lo:hi, :], NN) + _dot(du, wgu_v[F + lo:F + hi, :], NN)
        xv = x_ref[...]
        dx, dg0 = _rms_bwd(xv, _rms_stat(xv), g0_ref[...], dh)
        dg0_ref[...] += dg0
        dx_ref[...] = dxo_v + dx

    row = lambda w: pl.BlockSpec((tm, w), lambda i, c: (i, 0))
    one = pl.BlockSpec((1, D), lambda i, c: (0, 0))
    return pl.pallas_call(
        body, name=name,
        grid_spec=pltpu.PrefetchScalarGridSpec(
            num_scalar_prefetch=1, grid=(T // tm,),
            in_specs=[row(D), row(D), row(D), row(F2), one, one] + [ANY] * 6,
            out_specs=[row(D), row(D), row(F), row(F2), one, one],
            scratch_shapes=[pltpu.VMEM((F2, D), MXU_DTYPE), pltpu.VMEM((F, D), MXU_DTYPE),
                            pltpu.SemaphoreType.DMA((3 * N_CHIPS,))]),
        out_shape=[jax.ShapeDtypeStruct((T, D), F32), jax.ShapeDtypeStruct((T, D), MXU_DTYPE),
                   jax.ShapeDtypeStruct((T, F), MXU_DTYPE), jax.ShapeDtypeStruct((T, F2), MXU_DTYPE),
                   jax.ShapeDtypeStruct((1, D), F32), jax.ShapeDtypeStruct((1, D), F32)],
        compiler_params=_cp("arbitrary"),
    )(*prefetch, dxo, x, y, gu, g0, g1, *shards)


def _out_proj_postnorm(a, b, x, g, name):
    T, K = a.shape
    D = b.shape[1]
    tm = _pick(T, 512, SUBLANES)

    def body(a_ref, b_ref, x_ref, g_ref, m_ref, o_ref):
        mv = _dot(a_ref[...], b_ref[...], NN)
        m_ref[...] = mv
        o_ref[...] = x_ref[...] + mv * _rms_stat(mv) * g_ref[...]

    row = lambda w: pl.BlockSpec((tm, w), lambda i: (i, 0))
    return pl.pallas_call(body, name=name, grid=(T // tm,),
                          in_specs=[row(K), _full((K, D)), row(D), _full((1, D))], out_specs=[row(D), row(D)],
                          out_shape=[jax.ShapeDtypeStruct((T, D), F32)] * 2, compiler_params=_cp("parallel"))(a, b, x, g)


def _postnorm_bwd_dgrad(dxo, m, g, b, name):
    T, D = m.shape
    K = b.shape[0]
    tm = _pick(T, 512, SUBLANES)

    def body(dxo_ref, m_ref, g_ref, b_ref, dm_ref, dg_ref, da_ref):
        @pl.when(pl.program_id(0) == 0)
        def _():
            dg_ref[...] = jnp.zeros_like(dg_ref)

        mv = m_ref[...]
        dm, dg = _rms_bwd(mv, _rms_stat(mv), g_ref[...], dxo_ref[...])
        dg_ref[...] += dg
        dmb = _mx(dm)
        dm_ref[...] = dmb
        da_ref[...] = _dot(dmb, b_ref[...], NT)

    row = lambda w: pl.BlockSpec((tm, w), lambda i: (i, 0))
    return pl.pallas_call(body, name=name, grid=(T // tm,), in_specs=[row(D), row(D), _full((1, D)), _full((K, D))],
                          out_specs=[row(D), _full((1, D)), row(K)],
                          out_shape=[jax.ShapeDtypeStruct((T, D), MXU_DTYPE), jax.ShapeDtypeStruct((1, D), F32),
                                     jax.ShapeDtypeStruct((T, K), F32)],
                          compiler_params=_cp("arbitrary"))(dxo, m, g, b)


def _dgrad_prenorm_bwd(a, b, add, dxo, x, g, name):
    T, K = a.shape
    D = b.shape[0]
    tm = _pick(T, 512, SUBLANES)
    tk = _pick(K, 2048)
    nk = K // tk
    has_add = add is not None

    def body(*refs):
        a_ref, b_ref = refs[:2]
        add_ref = refs[2] if has_add else None
        dxo_ref, x_ref, g_ref, dx_ref, dg_ref, acc = refs[-6:]
        i, k = pl.program_id(0), pl.program_id(1)

        @pl.when((i == 0) & (k == 0))
        def _():
            dg_ref[...] = jnp.zeros_like(dg_ref)

        @pl.when(k == 0)
        def _():
            acc[...] = add_ref[...] if has_add else jnp.zeros_like(acc)

        acc[...] += _dot(a_ref[...], b_ref[...], NT)

        @pl.when(k == nk - 1)
        def _():
            xv = x_ref[...]
            dx, dg = _rms_bwd(xv, _rms_stat(xv), g_ref[...], acc[...])
            dg_ref[...] += dg
            dx_ref[...] = dxo_ref[...] + dx

    row = pl.BlockSpec((tm, D), lambda i, k: (i, 0))
    one = pl.BlockSpec((1, D), lambda i, k: (0, 0))
    ins = [a, b] + ([add] if has_add else []) + [dxo, x, g]
    specs = ([pl.BlockSpec((tm, tk), lambda i, k: (i, k)), pl.BlockSpec((D, tk), lambda i, k: (0, k))]
             + ([row] if has_add else []) + [row, row, one])
    return pl.pallas_call(body, name=name, grid=(T // tm, nk), in_specs=specs, out_specs=[row, one],
                          out_shape=[jax.ShapeDtypeStruct((T, D), F32), jax.ShapeDtypeStruct((1, D), F32)],
                          scratch_shapes=[pltpu.VMEM((tm, D), F32)],
                          compiler_params=_cp("arbitrary", "arbitrary"))(*ins)


DN_ROWS = 512


def _shift_down(prev8, cur, s):
    n = cur.shape[0]
    xx = jnp.concatenate([prev8, cur], axis=0)
    return pltpu.roll(xx, s, 0)[SUBLANES:SUBLANES + n, :]


def _shift_up(cur, next8, s):
    n = cur.shape[0]
    xx = jnp.concatenate([cur, next8], axis=0)
    return pltpu.roll(xx, n + SUBLANES - s, 0)[:n, :]


def _tile_start(r, rows):
    return r * rows if isinstance(r, int) else pl.multiple_of(r * rows, SUBLANES)


def _conv_tile(x_ref, w, r, rows):
    start = _tile_start(r, rows)
    cur = x_ref[pl.ds(start, rows), :]
    if isinstance(r, int):
        prev8 = jnp.zeros((SUBLANES, cur.shape[1]), cur.dtype)
        taps = [_shift_down(prev8, cur, DN_CONV - 1 - j) if j < DN_CONV - 1 else cur for j in range(DN_CONV)]
    else:
        taps = [x_ref[pl.ds(start - (DN_CONV - 1 - j), rows), :] if j < DN_CONV - 1 else cur for j in range(DN_CONV)]
    c = taps[0] * w[0:1, :]
    for j in range(1, DN_CONV):
        c = c + taps[j] * w[j:j + 1, :]
    return c, taps


def _dn_prep_fwd(proj, conv_w, name):
    T = proj.shape[0]
    W = DN_HEADS * DN_HEAD_DIM
    rows = min(DN_ROWS, T)
    n_inner = T // rows
    scale = DN_HEAD_DIM ** -0.5

    def body(x_ref, w_ref, o_ref):
        cb = pl.program_id(0)
        w = w_ref[...]
        is_qk = cb < 2 * DN_HEADS
        post = jnp.where(cb < DN_HEADS, scale, 1.0)

        def step(r, carry):
            c, _ = _conv_tile(x_ref, w, r, rows)
            s = c * _sigmoid(c)
            rinv = lax.rsqrt(jnp.sum(s * s, axis=-1, keepdims=True) + L2_EPS)
            o_ref[pl.ds(_tile_start(r, rows), rows), :] = jnp.where(is_qk, s * rinv * post, s)
            return carry

        step(0, 0)
        lax.fori_loop(1, n_inner, step, 0)

    col = pl.BlockSpec((T, LANES), lambda j: (0, j))
    return pl.pallas_call(body, name=name, grid=(3 * W // LANES,),
                          in_specs=[col, pl.BlockSpec((DN_CONV, LANES), lambda j: (0, j))], out_specs=col,
                          out_shape=jax.ShapeDtypeStruct((T, 3 * W), F32), compiler_params=_cp("parallel"))(proj, conv_w)


def _dn_prep_bwd(proj, conv_w, dqkv, name):
    T = proj.shape[0]
    W = DN_HEADS * DN_HEAD_DIM
    rows = min(DN_ROWS, T)
    n_inner = T // rows
    scale = DN_HEAD_DIM ** -0.5

    def body(x_ref, w_ref, dy_ref, dx_ref, dw_ref, dc_scr):
        cb = pl.program_id(0)
        w = w_ref[...]
        is_qk = cb < 2 * DN_HEADS
        post = jnp.where(cb < DN_HEADS, scale, 1.0)

        def step1(r, dws):
            c, taps = _conv_tile(x_ref, w, r, rows)
            sg = _sigmoid(c)
            s = c * sg
            rinv = lax.rsqrt(jnp.sum(s * s, axis=-1, keepdims=True) + L2_EPS)
            dy = dy_ref[pl.ds(_tile_start(r, rows), rows), :]
            yn = s * rinv
            dyn = dy * post
            ds_qk = rinv * (dyn - yn * jnp.sum(dyn * yn, axis=-1, keepdims=True))
            ds = jnp.where(is_qk, ds_qk, dy)
            dc = ds * (sg * (1.0 + c * (1.0 - sg)))
            dc_scr[pl.ds(_tile_start(r, rows), rows), :] = dc
            return tuple(dws[j] + jnp.sum(dc * taps[j], axis=0, keepdims=True) for j in range(DN_CONV))

        zero = jnp.zeros((1, LANES), F32)
        dws = lax.fori_loop(1, n_inner, step1, step1(0, (zero,) * DN_CONV))
        for j in range(DN_CONV):
            dw_ref[j:j + 1, :] = dws[j]

        def step2(r, carry):
            start = _tile_start(r, rows)
            cur = dc_scr[pl.ds(start, rows), :]
            dx = cur * w[DN_CONV - 1:DN_CONV, :]
            for j in range(DN_CONV - 1):
                s = DN_CONV - 1 - j
                if isinstance(r, int):
                    up = _shift_up(cur, jnp.zeros((SUBLANES, LANES), F32), s)
                else:
                    up = dc_scr[pl.ds(start + s, rows), :]
                dx = dx + up * w[j:j + 1, :]
            dx_ref[pl.ds(start, rows), :] = _mx(dx)
            return carry

        lax.fori_loop(0, n_inner - 1, step2, 0)
        step2(n_inner - 1, 0)

    col = pl.BlockSpec((T, LANES), lambda j: (0, j))
    wspec = pl.BlockSpec((DN_CONV, LANES), lambda j: (0, j))
    return pl.pallas_call(body, name=name, grid=(3 * W // LANES,), in_specs=[col, wspec, col], out_specs=[col, wspec],
                          out_shape=[jax.ShapeDtypeStruct((T, 3 * W), MXU_DTYPE), jax.ShapeDtypeStruct((DN_CONV, 3 * W), F32)],
                          scratch_shapes=[pltpu.VMEM((T, LANES), F32)], compiler_params=_cp("parallel"))(proj, conv_w, dqkv)


def _softplus(x):
    return jnp.maximum(x, 0.0) + jnp.log(1.0 + jnp.exp(-jnp.abs(x)))


def _dn_gate_fwd(ba, a_log, dt_bias, name):
    T = ba.shape[0]
    tm = _pick(T, 1024, SUBLANES)

    def body(ba_ref, al_ref, dt_ref, beta_ref, g_ref):
        beta_ref[...] = _sigmoid(ba_ref[:, :LANES])
        g_ref[...] = -jnp.exp(al_ref[...]) * _softplus(ba_ref[:, LANES:] + dt_ref[...])

    row = lambda w: pl.BlockSpec((tm, w), lambda i: (i, 0))
    return pl.pallas_call(body, name=name, grid=(T // tm,), in_specs=[row(2 * LANES), _full((1, LANES)), _full((1, LANES))],
                          out_specs=[row(LANES), row(LANES)],
                          out_shape=[jax.ShapeDtypeStruct((T, LANES), F32)] * 2, compiler_params=_cp("parallel"))(ba, a_log, dt_bias)


def _dn_gate_bwd(ba, a_log, dt_bias, dbeta, dg, name):
    T = ba.shape[0]
    tm = _pick(T, 1024, SUBLANES)

    def body(ba_ref, al_ref, dt_ref, dbeta_ref, dg_ref, dba_ref, dal_ref, ddt_ref):
        @pl.when(pl.program_id(0) == 0)
        def _():
            dal_ref[...] = jnp.zeros_like(dal_ref)
            ddt_ref[...] = jnp.zeros_like(ddt_ref)

        beta = _sigmoid(ba_ref[:, :LANES])
        dba_ref[:, :LANES] = _mx(dbeta_ref[...] * beta * (1.0 - beta))
        pre = ba_ref[:, LANES:] + dt_ref[...]
        ea = jnp.exp(al_ref[...])
        dgv = dg_ref[...]
        da = dgv * (-ea) * _sigmoid(pre)
        dba_ref[:, LANES:] = _mx(da)
        ddt_ref[...] += jnp.sum(da, axis=0, keepdims=True)
        dal_ref[...] += jnp.sum(dgv * (-ea) * _softplus(pre), axis=0, keepdims=True)

    row = lambda w: pl.BlockSpec((tm, w), lambda i: (i, 0))
    one = _full((1, LANES))
    return pl.pallas_call(body, name=name, grid=(T // tm,), in_specs=[row(2 * LANES), one, one, row(LANES), row(LANES)],
                          out_specs=[row(2 * LANES), one, one],
                          out_shape=[jax.ShapeDtypeStruct((T, 2 * LANES), MXU_DTYPE), jax.ShapeDtypeStruct((1, LANES), F32),
                                     jax.ShapeDtypeStruct((1, LANES), F32)],
                          compiler_params=_cp("arbitrary"))(ba, a_log, dt_bias, dbeta, dg)


def _tri(c, strict):
    i = lax.broadcasted_iota(jnp.int32, (c, c), 0)
    j = lax.broadcasted_iota(jnp.int32, (c, c), 1)
    return (i > j) if strict else (i >= j)


def _inv_unit_lower(ls):
    c = ls[0].shape[0]
    i = lax.broadcasted_iota(jnp.int32, (c, c), 0)
    j = lax.broadcasted_iota(jnp.int32, (c, c), 1)
    eye = jnp.where(i == j, 1.0, 0.0)
    facs = [[eye - l for l in ls]]
    cur = ls
    for _ in range(int(math.log2(c)) - 1):
        cur = [_dot(p, p, NN, TRI_PREC) for p in cur]
        facs.append([eye + p for p in cur])
    while len(facs) > 1:
        nxt = [[_dot(a, b, NN, TRI_PREC) for a, b in zip(facs[t], facs[t + 1])] for t in range(0, len(facs) - 1, 2)]
        if len(facs) % 2:
            nxt.append(facs[-1])
        facs = nxt
    return facs[0]


def _chunk_gates(g_blk):
    c = g_blk.shape[0]
    gcs = _dot(jnp.where(_tri(c, False), 1.0, 0.0), g_blk, NN, HI)
    return gcs, gcs.T


def _head_chunk(h, qh, kh, vh, beta_blk, gcs, gcs_t):
    c = qh.shape[0]
    incl = _tri(c, False)
    gc_col = gcs[:, h:h + 1]
    gc_row = gcs_t[h:h + 1, :]
    gc_last = gcs_t[h:h + 1, c - 1:c]
    dec = jnp.where(incl, jnp.exp(jnp.where(incl, gc_col - gc_row, 0.0)), 0.0)
    gam = jnp.exp(gc_col)
    rr = jnp.exp(gc_last - gc_col)
    gl = jnp.exp(gc_last)
    b = beta_blk[:, h:h + 1]
    kb = kh * b
    vb = vh * b
    both = _dot(jnp.concatenate([_mx(kb), _mx(qh)], axis=0), _mx(kh), NT)
    lmat = jnp.where(_tri(c, True), both[:c] * dec, 0.0)
    pmat = jnp.where(incl, both[c:] * dec, 0.0)
    return dict(dec=dec, gam=gam, rr=rr, gl=gl, b=b, kb=kb, vb=vb, lmat=lmat, pmat=pmat)


def _solve_uw(tinv, q):
    return _dot(tinv, jnp.concatenate([q["vb"], q["kb"] * q["gam"]], axis=1), NN, TRI_PREC)


def _dn_scan_fwd(qkv, beta, g, proj, norm_g, name):
    T = qkv.shape[0]
    C, H, Dh = DN_CHUNK, DN_HEADS, DN_HEAD_DIM
    W = H * Dh
    N = T // C

    def body(q_ref, k_ref, v_ref, beta_ref, g_ref, z_ref, ng_ref, og_ref, o_ref, tinv_ref, s_ref, state):
        @pl.when(pl.program_id(0) == 0)
        def _():
            state[...] = jnp.zeros_like(state)

        ng = ng_ref[...]
        heads = range(H)
        cs = [slice(h * Dh, (h + 1) * Dh) for h in heads]

        def chunk(j, carry):
            rows = pl.ds(pl.multiple_of(j * C, C), C)
            gcs, gcs_t = _chunk_gates(g_ref[rows, :])
            beta_blk = beta_ref[rows, :]
            qs = [_head_chunk(h, q_ref[rows, cs[h]], k_ref[rows, cs[h]], v_ref[rows, cs[h]], beta_blk, gcs, gcs_t)
                  for h in heads]
            tinvs = _inv_unit_lower([q["lmat"] for q in qs])
            for h in heads:
                tinv_ref[h, rows, :] = tinvs[h]
            uws = [_solve_uw(tinvs[h], qs[h]) for h in heads]
            ss = [state[h] for h in heads]
            for h in heads:
                s_ref[j, h] = ss[h]
            sbs = [_mx(s) for s in ss]
            vnbs = [_mx(uws[h][:, :Dh] - _dot(_mx(uws[h][:, Dh:]), sbs[h], NN)) for h in heads]
            os_ = [_dot(jnp.concatenate([_mx(q_ref[rows, cs[h]] * qs[h]["gam"]), _mx(qs[h]["pmat"])], axis=1),
                        jnp.concatenate([sbs[h], vnbs[h]], axis=0), NN) for h in heads]
            for h in heads:
                state[h] = ss[h] * qs[h]["gl"] + _dot(_mx((k_ref[rows, cs[h]] * qs[h]["rr"]).T), vnbs[h], NN)
            for h in heads:
                o = os_[h]
                o_ref[rows, cs[h]] = o
                zh = z_ref[rows, cs[h]]
                og_ref[rows, cs[h]] = _mx(o * _rms_stat(o) * ng * (zh * _sigmoid(zh)))
            return carry

        lax.fori_loop(0, PER, chunk, 0)

    PER = 2 if N % 2 == 0 else 1
    blk = lambda j: pl.BlockSpec((PER * C, W), lambda n: (n, j))
    small = pl.BlockSpec((PER * C, LANES), lambda n: (n, 0))
    return pl.pallas_call(
        body, name=name, grid=(N // PER,),
        in_specs=[blk(0), blk(1), blk(2), small, small, blk(3), _full((1, Dh))],
        out_specs=[blk(0), blk(0), pl.BlockSpec((H, PER * C, C), lambda n: (0, n, 0)),
                   pl.BlockSpec((PER, H, Dh, Dh), lambda n: (n, 0, 0, 0))],
        out_shape=[jax.ShapeDtypeStruct((T, W), MXU_DTYPE), jax.ShapeDtypeStruct((T, W), F32),
                   jax.ShapeDtypeStruct((H, T, C), F32), jax.ShapeDtypeStruct((N, H, Dh, Dh), F32)],
        scratch_shapes=[pltpu.VMEM((H, Dh, Dh), F32)],
        compiler_params=_cp("arbitrary"),
    )(qkv, qkv, qkv, beta, g, proj, norm_g)


def _dn_scan_bwd(qkv, beta, g, proj, norm_g, o, tinv, s_all, dog, name):
    T = qkv.shape[0]
    C, H, Dh = DN_CHUNK, DN_HEADS, DN_HEAD_DIM
    W = H * Dh
    N = T // C

    def body(q_ref, k_ref, v_ref, beta_ref, g_ref, z_ref, ng_ref, o_ref, tinv_ref, s_ref, dog_ref,
             dqkv_ref, dbeta_ref, dg_ref, dz_ref, dng_ref, dstate):
        @pl.when(pl.program_id(0) == 0)
        def _():
            dstate[...] = jnp.zeros_like(dstate)
            dng_ref[...] = jnp.zeros_like(dng_ref)

        gcs, gcs_t = _chunk_gates(g_ref[...])
        beta_blk = beta_ref[...]
        ng = ng_ref[...]
        incl = _tri(C, False)
        strict = _tri(C, True)
        lane = lax.broadcasted_iota(jnp.int32, (C, LANES), 1)
        rowi = lax.broadcasted_iota(jnp.int32, (C, 1), 0)
        headrow = lax.broadcasted_iota(jnp.int32, (LANES, C), 0)
        colsums = jnp.zeros((LANES, C), F32)
        dbeta_acc = jnp.zeros((C, LANES), F32)
        dgc_acc = jnp.zeros((C, LANES), F32)
        dng_acc = jnp.zeros((1, Dh), F32)
        cs = [slice(h * Dh, (h + 1) * Dh) for h in range(H)]
        rsum = lambda t: jnp.sum(t, axis=1, keepdims=True)
        for heads in (range(0, H // 2), range(H // 2, H)):
            dobs = {}
            for h in heads:
                oh, zh, dogh = o_ref[:, cs[h]], z_ref[:, cs[h]], dog_ref[:, cs[h]]
                rstat = _rms_stat(oh)
                sz = _sigmoid(zh)
                dz_ref[:, cs[h]] = _mx(dogh * (oh * rstat * ng) * (sz * (1.0 + zh * (1.0 - sz))))
                do, dng = _rms_bwd(oh, rstat, ng, dogh * (zh * sz))
                dng_acc = dng_acc + dng
                dobs[h] = _mx(do)
            qs = {h: _head_chunk(h, q_ref[:, cs[h]], k_ref[:, cs[h]], v_ref[:, cs[h]], beta_blk, gcs, gcs_t) for h in heads}
            tms = {h: tinv_ref[h] for h in heads}
            uws = {h: _solve_uw(tms[h], qs[h]) for h in heads}
            ss = {h: s_ref[0, h] for h in heads}
            sbs = {h: _mx(ss[h]) for h in heads}
            wbs = {h: _mx(uws[h][:, Dh:]) for h in heads}
            vnbs = {h: _mx(uws[h][:, :Dh] - _dot(wbs[h], sbs[h], NN)) for h in heads}
            dsns = {h: dstate[h] for h in heads}
            dsbs = {h: _mx(dsns[h]) for h in heads}
            dvnews = {h: _dot(_mx(qs[h]["pmat"]), dobs[h], TN) + _dot(_mx(k_ref[:, cs[h]] * qs[h]["rr"]), dsbs[h], NN)
                      for h in heads}
            dvb16s = {h: _mx(dvnews[h]) for h in heads}
            dps = {h: jnp.where(incl, _dot(dobs[h], vnbs[h], NT), 0.0) for h in heads}
            dqds = {h: _dot(dobs[h], sbs[h], NT) for h in heads}
            dkds = {h: _dot(vnbs[h], dsbs[h], NT) for h in heads}
            dgls = {h: jnp.sum(rsum(ss[h] * dsns[h]), axis=0, keepdims=True) for h in heads}
            dws = {h: -_dot(dvb16s[h], sbs[h], NT) for h in heads}
            for h in heads:
                dstate[h] = qs[h]["gl"] * dsns[h] + _dot(
                    jnp.concatenate([_mx(q_ref[:, cs[h]] * qs[h]["gam"]), -wbs[h]], axis=0),
                    jnp.concatenate([dobs[h], dvb16s[h]], axis=0), TN)
            dsols = {h: _dot(tms[h], jnp.concatenate([dvnews[h], dws[h]], axis=1), TN, TRI_PREC) for h in heads}
            dvbs = {h: dsols[h][:, :Dh] for h in heads}
            dkbgs = {h: dsols[h][:, Dh:] for h in heads}
            dls = {h: jnp.where(strict, -_dot(dsols[h], uws[h], NT, TRI_PREC), 0.0) for h in heads}
            mmats = {h: dls[h] * qs[h]["lmat"] + dps[h] * qs[h]["pmat"] for h in heads}
            dgcs = {h: rsum(mmats[h]) for h in heads}
            for h in heads:
                colsums = jnp.where(headrow == h, jnp.sum(mmats[h], axis=0, keepdims=True), colsums)
            dboth = {h: jnp.concatenate([_mx(dls[h] * qs[h]["dec"]), _mx(dps[h] * qs[h]["dec"])], axis=0) for h in heads}
            for h in heads:
                q = qs[h]
                qh, kh, vh = q_ref[:, cs[h]], k_ref[:, cs[h]], v_ref[:, cs[h]]
                gam, rr, b, kb = q["gam"], q["rr"], q["b"], q["kb"]
                on_k = _dot(dboth[h], _mx(kh), NN)
                dkb = on_k[:C] + dkbgs[h] * gam
                dk = _dot(dboth[h], jnp.concatenate([_mx(kb), _mx(qh)], axis=0), TN) + dkb * b + dkds[h] * rr
                dq = on_k[C:] + dqds[h] * gam
                dgam = rsum(dkbgs[h] * kb) + rsum(dqds[h] * qh)
                dr = rsum(dkds[h] * kh)
                dgc_last = jnp.sum(dr * rr, axis=0, keepdims=True) + dgls[h] * q["gl"]
                dgc = dgcs[h] + dgam * gam - dr * rr + jnp.where(rowi == C - 1, dgc_last, 0.0)
                dbeta = rsum(dvbs[h] * vh) + rsum(dkb * kh)
                dqkv_ref[:, cs[h]] = dq
                dqkv_ref[:, W + h * Dh:W + (h + 1) * Dh] = dk
                dqkv_ref[:, 2 * W + h * Dh:2 * W + (h + 1) * Dh] = dvbs[h] * b
                dbeta_acc = jnp.where(lane == h, dbeta, dbeta_acc)
                dgc_acc = jnp.where(lane == h, dgc, dgc_acc)
        dbeta_ref[...] = dbeta_acc
        dg_ref[...] = _dot(jnp.where(incl, 1.0, 0.0), dgc_acc - colsums.T, TN, HI)
        dng_ref[...] += dng_acc

    rev = lambda n: N - 1 - n
    blk = lambda j: pl.BlockSpec((C, W), lambda n: (rev(n), j))
    small = pl.BlockSpec((C, LANES), lambda n: (rev(n), 0))
    return pl.pallas_call(
        body, name=name, grid=(N,),
        in_specs=[blk(0), blk(1), blk(2), small, small, blk(3), _full((1, Dh)), blk(0),
                  pl.BlockSpec((H, C, C), lambda n: (0, rev(n), 0)),
                  pl.BlockSpec((1, H, Dh, Dh), lambda n: (rev(n), 0, 0, 0)), blk(0)],
        out_specs=[pl.BlockSpec((C, 3 * W), lambda n: (rev(n), 0)), small, small, blk(0), _full((1, Dh))],
        out_shape=[jax.ShapeDtypeStruct((T, 3 * W), F32), jax.ShapeDtypeStruct((T, LANES), F32),
                   jax.ShapeDtypeStruct((T, LANES), F32), jax.ShapeDtypeStruct((T, W), MXU_DTYPE),
                   jax.ShapeDtypeStruct((1, Dh), F32)],
        scratch_shapes=[pltpu.VMEM((H, Dh, Dh), F32)],
        compiler_params=_cp("arbitrary"),
    )(qkv, qkv, qkv, beta, g, proj, norm_g, o, tinv, s_all, dog)


_INV_SQRT2 = 0.7071067811865476
_INV_SQRT_2PI = 0.3989422804014327


def _sg_recompute(zp_ref, bin_ref, lng_ref, lnb_ref):
    E = SG_WIDTH
    zin = zp_ref[...] + bin_ref[...]
    cdf = 0.5 * (1.0 + lax.erf(zin * _INV_SQRT2))
    zz = zin * cdf
    u = zz[:, :E]
    vp = zz[:, E:]
    mu = jnp.mean(vp, axis=-1, keepdims=True)
    xc = vp - mu
    rstd = lax.rsqrt(jnp.mean(xc * xc, axis=-1, keepdims=True) + LN_EPS)
    xhat = xc * rstd
    v = xhat * lng_ref[...] + lnb_ref[...]
    return zin, cdf, u, xhat, rstd, v


def _sg_masked_ws(ws_ref, g):
    return _mx(jnp.where(_tri(SG_CHUNK, False), ws_ref[g], 0.0))


def _sg_fwd(zpre, b_in, ln_g, ln_b, w_s, b_s_t, name):
    T = zpre.shape[0]
    E, G, C, GW = SG_WIDTH, SG_GROUPS, SG_CHUNK, SG_GROUP_W

    def body(zp_ref, bin_ref, lng_ref, lnb_ref, ws_ref, bst_ref, um_ref):
        _, _, u, _, _, v = _sg_recompute(zp_ref, bin_ref, lng_ref, lnb_ref)
        bst = bst_ref[...]
        for g in range(G):
            cs = slice(g * GW, (g + 1) * GW)
            mixed = _dot(_sg_masked_ws(ws_ref, g), _mx(v[:, cs]), NN) + bst[:, g:g + 1]
            um_ref[:, cs] = _mx(u[:, cs] * mixed)

    return pl.pallas_call(
        body, name=name, grid=(T // C,),
        in_specs=[pl.BlockSpec((C, 2 * E), lambda n: (n, 0)), _full((1, 2 * E)), _full((1, E)), _full((1, E)),
                  _full((G, C, C)), _full((C, LANES))],
        out_specs=pl.BlockSpec((C, E), lambda n: (n, 0)),
        out_shape=jax.ShapeDtypeStruct((T, E), MXU_DTYPE), compiler_params=_cp("parallel"),
    )(zpre, b_in, ln_g, ln_b, w_s, b_s_t)


def _sg_bwd(zpre, b_in, ln_g, ln_b, w_s, b_s_t, dum, name):
    T = zpre.shape[0]
    E, G, C, GW = SG_WIDTH, SG_GROUPS, SG_CHUNK, SG_GROUP_W

    def body(zp_ref, bin_ref, lng_ref, lnb_ref, ws_ref, bst_ref, dum_ref,
             dz_ref, dbin_ref, dlng_ref, dlnb_ref, dws_ref, dbst_ref):
        @pl.when(pl.program_id(0) == 0)
        def _():
            for r in (dbin_ref, dlng_ref, dlnb_ref, dws_ref, dbst_ref):
                r[...] = jnp.zeros_like(r)

        zin, cdf, u, xhat, rstd, v = _sg_recompute(zp_ref, bin_ref, lng_ref, lnb_ref)
        bst = bst_ref[...]
        lane = lax.broadcasted_iota(jnp.int32, (C, LANES), 1)
        dum_v = dum_ref[...]
        dbst = jnp.zeros((C, LANES), F32)
        du_parts, dv_parts = [], []
        for g in range(G):
            cs = slice(g * GW, (g + 1) * GW)
            wsm = _sg_masked_ws(ws_ref, g)
            vg = _mx(v[:, cs])
            mixed = _dot(wsm, vg, NN) + bst[:, g:g + 1]
            dumg = dum_v[:, cs]
            du_parts.append(dumg * mixed)
            dmixed = dumg * u[:, cs]
            dmb = _mx(dmixed)
            dv_parts.append(_dot(wsm, dmb, TN))
            dws_ref[g] += _dot(dmb, vg, NT)
            dbst = jnp.where(lane == g, jnp.sum(dmixed, axis=1, keepdims=True), dbst)
        dbst_ref[...] += dbst
        du = jnp.concatenate(du_parts, axis=1)
        dv = jnp.concatenate(dv_parts, axis=1)
        dlng_ref[...] += jnp.sum(dv * xhat, axis=0, keepdims=True)
        dlnb_ref[...] += jnp.sum(dv, axis=0, keepdims=True)
        dxh = dv * lng_ref[...]
        dvp = rstd * (dxh - jnp.mean(dxh, axis=-1, keepdims=True) - xhat * jnp.mean(dxh * xhat, axis=-1, keepdims=True))
        dzz = jnp.concatenate([du, dvp], axis=1)
        dzin = dzz * (cdf + zin * (_INV_SQRT_2PI * jnp.exp(-0.5 * zin * zin)))
        dz_ref[...] = _mx(dzin)
        dbin_ref[...] += jnp.sum(dzin, axis=0, keepdims=True)

    return pl.pallas_call(
        body, name=name, grid=(T // C,),
        in_specs=[pl.BlockSpec((C, 2 * E), lambda n: (n, 0)), _full((1, 2 * E)), _full((1, E)), _full((1, E)),
                  _full((G, C, C)), _full((C, LANES)), pl.BlockSpec((C, E), lambda n: (n, 0))],
        out_specs=[pl.BlockSpec((C, 2 * E), lambda n: (n, 0)), _full((1, 2 * E)), _full((1, E)), _full((1, E)),
                   _full((G, C, C)), _full((C, LANES))],
        out_shape=[jax.ShapeDtypeStruct((T, 2 * E), MXU_DTYPE), jax.ShapeDtypeStruct((1, 2 * E), F32),
                   jax.ShapeDtypeStruct((1, E), F32), jax.ShapeDtypeStruct((1, E), F32),
                   jax.ShapeDtypeStruct((G, C, C), F32), jax.ShapeDtypeStruct((C, LANES), F32)],
        compiler_params=_cp("arbitrary"),
    )(zpre, b_in, ln_g, ln_b, w_s, b_s_t, dum)


def _row(v):
    return v.reshape(1, -1)


def _pad_lanes(v):
    v = v.reshape(1, -1)
    return jnp.pad(v, ((0, 0), (0, LANES - v.shape[1])))


def _local_step(x, target, p, weights_for, grads_ready=None, small_ready=None):
    ng = p["norm_g"]
    grads = {}
    dng = [[None] * 6 for _ in range(2)]
    order = [jnp.zeros((), F32)]

    def tell(group):
        zero = grads_ready(group, grads) if grads_ready is not None else None
        if zero is not None:
            order[0] = zero

    def gain(i, s):
        return _row(ng[i, s]) + order[0]

    def ffn_f(xin, i, j, tag, **tail):
        wt = weights_for("ffn" + tag, xin)
        xo, h, gu, y, *rest = _ffn_fwd(xin, _row(ng[i, 4 * j]), _row(ng[i, 4 * j + 1]), wt, "ffn_fwd_" + tag, **tail)
        return (xo, *rest), (xin, h, gu, y, wt)

    (x1, hn0), sv_f00 = ffn_f(x, 0, 0, "00", next_gain=_row(ng[0, 2]))
    dnw = weights_for("dn", x1)
    proj = _mm(hn0, dnw["dn_wqkvz"], "nn", "dn_proj")
    ba = _mm(hn0, dnw["dn_wba"], "nn", "dn_proj_ba")
    a_log = _pad_lanes(p["dn_a_log"])
    dt_bias = _pad_lanes(p["dn_dt_bias"])
    dn_ng = _row(p["dn_norm_g"])
    qkv = _dn_prep_fwd(proj, p["dn_conv_w"], "dn_prep_fwd")
    beta, gdec = _dn_gate_fwd(ba, a_log, dt_bias, "dn_gate_fwd")
    og, o_raw, tinv, s_all = _dn_scan_fwd(qkv, beta, gdec, proj, dn_ng, "dn_scan_fwd")
    m0, x2 = _out_proj_postnorm(og, dnw["dn_wout"], x1, _row(ng[0, 3]), "dn_out")
    (x3,), sv_f01 = ffn_f(x2, 0, 1, "01")
    (x4, hn1), sv_f10 = ffn_f(x3, 1, 0, "10", next_gain=_row(ng[1, 2]))
    sgw = weights_for("sg", x4)
    zpre = _mm(hn1, sgw["sg_win"], "nn", "sg_proj")
    sg_bin = _row(p["sg_b_in"])
    sg_lng = _row(p["sg_ln_g"])
    sg_lnb = _row(p["sg_ln_b"])
    sg_bst = jnp.pad(p["sg_b_s"].T, ((0, 0), (0, LANES - SG_GROUPS)))
    um = _sg_fwd(zpre, sg_bin, sg_lng, sg_lnb, p["sg_w_s"], sg_bst, "sg_fwd")
    m1, x5 = _out_proj_postnorm(um, sgw["sg_wout"], x4, _row(ng[1, 3]), "sg_out")
    (_, loss_part, dx), sv_f11 = ffn_f(x5, 1, 1, "11", target=target)

    def ffn_b(dxo, sv, i, j, tag, last=False):
        xin, h, gu, y, wt = sv
        dxi, dy, a, dgu, dg0, dg1 = _ffn_bwd(dxo, xin, y, gu, gain(i, 4 * j), gain(i, 4 * j + 1), wt, "ffn_bwd_" + tag)
        dng[i][4 * j] = dg0
        dng[i][4 * j + 1] = dg1
        after = None
        if last:
            grads["norm_g"] = jnp.stack([jnp.concatenate(dng[t], axis=0) for t in range(2)])
            after = small_ready(grads, loss_part) if small_ready is not None else None
        grads["wd" + tag] = _mm(a, dy, "tn", "ffn_wgrad_down_" + tag, after=after)
        grads["wguT" + tag] = _mm(dgu, h, "tn", "ffn_wgrad_up_" + tag, after=after)
        tell("ffn" + tag)
        return dxi

    dx = ffn_b(dx, sv_f11, 1, 1, "11")
    dm1, dng[1][3], dum = _postnorm_bwd_dgrad(dx, m1, gain(1, 3), sgw["sg_wout"], "sg_dgrad_out")
    grads["sg_w_out"] = _mm(um, dm1, "tn", "sg_wgrad_out")
    dz1, dbin, dlng, dlnb, dws, dbst = _sg_bwd(zpre, sg_bin, sg_lng, sg_lnb, p["sg_w_s"], sg_bst, dum, "sg_bwd")
    grads["sg_w_inT"] = _mm(dz1, hn1, "tn", "sg_wgrad_in")
    tell("sg")
    dx, dng[1][2] = _dgrad_prenorm_bwd(dz1, sgw["sg_win"], None, dx, x4, gain(1, 2), "sg_dgrad_in")
    grads["sg_b_in"] = dbin.reshape(1, -1)
    grads["sg_ln_g"] = dlng.reshape(1, -1)
    grads["sg_ln_b"] = dlnb.reshape(1, -1)
    grads["sg_w_s"] = jnp.where(jnp.tril(jnp.ones((SG_CHUNK, SG_CHUNK), bool)), dws, 0.0)[None]
    grads["sg_b_s"] = dbst[:, :SG_GROUPS].T[None]
    dx = ffn_b(dx, sv_f10, 1, 0, "10")
    dx = ffn_b(dx, sv_f01, 0, 1, "01")
    dm0, dng[0][3], dog = _postnorm_bwd_dgrad(dx, m0, gain(0, 3), dnw["dn_wout"], "dn_dgrad_out")
    grads["dn_w_out"] = _mm(og, dm0, "tn", "dn_wgrad_out")
    dqkv, dbeta, dgdec, dz0, dnng = _dn_scan_bwd(qkv, beta, gdec, proj, dn_ng, o_raw, tinv, s_all, dog, "dn_scan_bwd")
    dqkv_pre, dconv = _dn_prep_bwd(proj, p["dn_conv_w"], dqkv, "dn_prep_bwd")
    dba, dal, ddt = _dn_gate_bwd(ba, a_log, dt_bias, dbeta, dgdec, "dn_gate_bwd")
    W3 = 3 * DN_HEADS * DN_HEAD_DIM
    dw_qkv = _mm(hn0, dqkv_pre, "tn", "dn_wgrad_qkv")
    dw_z = _mm(hn0, dz0, "tn", "dn_wgrad_z")
    dw_ba = _mm(hn0, dba, "tn", "dn_wgrad_ba")
    grads["dn_w_in"] = jnp.concatenate(
        [dw_qkv, dw_z, dw_ba[:, :DN_HEADS], dw_ba[:, LANES:LANES + DN_HEADS]], axis=1)
    tell("dn")
    dh0 = _mm(dqkv_pre, dnw["dn_wqkvz"][:, :W3], "nt", "dn_dgrad_qkv")
    dh0 = _mm(dz0, dnw["dn_wqkvz"][:, W3:], "nt", "dn_dgrad_z", add=dh0)
    dx, dng[0][2] = _dgrad_prenorm_bwd(dba, dnw["dn_wba"], dh0, dx, x1, gain(0, 2), "dn_dgrad_ba")
    grads["dn_conv_w"] = dconv[None]
    grads["dn_a_log"] = dal[:, :DN_HEADS]
    grads["dn_dt_bias"] = ddt[:, :DN_HEADS]
    grads["dn_norm_g"] = dnng
    dx = ffn_b(dx, sv_f00, 0, 0, "00", last=True)
    return loss_part, dx, grads


def _mesh_pos():
    return lax.axis_index("x"), lax.axis_index("y"), lax.axis_index("c")


def _other_chips(x, y):
    return [(1 - x, y), (x, 1 - y), (1 - x, 1 - y)]


def _allgather_chips(arrs, name):
    n = len(arrs)

    def body(*refs):
        ins, outs = refs[:n], refs[n:2 * n]
        ici_send, ici_recv, d2d_send, d2d_recv = refs[2 * n:]
        x, y, c = _mesh_pos()
        me = 2 * x + y
        chips = _other_chips(x, y)
        sibling = (x, y, 1 - c)

        def ici(i, j, k):
            cx, cy = chips[j]
            return pltpu.make_async_remote_copy(src_ref=ins[i].at[c], dst_ref=outs[i].at[k, c], send_sem=ici_send.at[3 * i + j],
                                                recv_sem=ici_recv.at[3 * i + j], device_id=(cx, cy, c), device_id_type=MESH)

        def d2d(i, j, h):
            cx, cy = chips[j]
            slot = outs[i].at[2 * cx + cy, h]
            return pltpu.make_async_remote_copy(src_ref=slot, dst_ref=slot, send_sem=d2d_send.at[3 * i + j],
                                                recv_sem=d2d_recv.at[3 * i + j], device_id=sibling, device_id_type=MESH)

        sends = [ici(i, j, me) for i in range(n) for j in range(3)]
        for cp in sends:
            cp.start()
        for i in range(n):
            for j, (cx, cy) in enumerate(chips):
                ici(i, j, 2 * cx + cy).wait_recv()
                fwd = d2d(i, j, c)
                fwd.start()
                sends.append(fwd)
        for i in range(n):
            for j in range(3):
                d2d(i, j, 1 - c).wait_recv()
        for cp in sends:
            cp.wait_send()

    return pl.pallas_call(
        body, name=name, in_specs=[ANY] * n, out_specs=[ANY] * n,
        out_shape=[jax.ShapeDtypeStruct((N_CHIPS,) + a.shape, a.dtype) for a in arrs],
        scratch_shapes=[pltpu.SemaphoreType.DMA((3 * n,))] * 4,
    )(*arrs)


HBM = pl.BlockSpec(memory_space=pltpu.HBM)
SEM = pl.BlockSpec(memory_space=pltpu.SEMAPHORE)
TOKEN = jax.ShapeDtypeStruct((SUBLANES, LANES), F32)


_PEERS = {"gather": 3, "scatter": 3, "swap": 1, "all": N_DEV - 1}


def _land_shape(kind, shape):
    if kind == "gather":
        return (N_CHIPS,) + shape
    if kind == "all":
        return (N_DEV,) + shape
    return (N_CHIPS,) + shape[2:] if kind == "swap" else shape


def _peer_copies(kind, flags, src_refs, land_refs, send_sems, recv_sems, receiving):
    x, y, c = _mesh_pos()
    me4, me8 = 2 * x + y, 4 * x + 2 * y + c
    np_ = _PEERS[kind]
    cps = []
    for i, (src, land) in enumerate(zip(src_refs, land_refs)):
        if kind == "swap":
            half = src.at[1 - c] if flags[i] else src.at[:, 1 - c]
            plan = [((x, y, 1 - c), half, land)]
        elif kind == "all":
            masks = [(mx, my, mc) for mx in (0, 1) for my in (0, 1) for mc in (0, 1)][1:]
            peers = [(jnp.where(mx, 1 - x, x), jnp.where(my, 1 - y, y), jnp.where(mc, 1 - c, c)) for mx, my, mc in masks]
            plan = [(p, src, land.at[4 * p[0] + 2 * p[1] + p[2] if receiving else me8]) for p in peers]
        else:
            plan = []
            for cx, cy in _other_chips(x, y):
                k = 2 * cx + cy
                s = src.at[me4 if receiving else k] if kind == "scatter" else src
                plan.append(((cx, cy, c), s, land.at[k if receiving else me4]))
        for j, (peer, s, d) in enumerate(plan):
            cps.append(pltpu.make_async_remote_copy(src_ref=s, dst_ref=d, send_sem=send_sems.at[np_ * i + j],
                                                    recv_sem=recv_sems.at[np_ * i + j], device_id=peer, device_id_type=MESH))
    return cps


def _copies_start(kind, srcs, after, name, flags=None):
    n = len(srcs)
    ns = _PEERS[kind] * n
    lands = [lax.empty(_land_shape(kind, s.shape), s.dtype) for s in srcs]
    after = [] if after is None else [after]

    def body(*refs):
        src_refs, land_refs = refs[:n], refs[n:2 * n]
        send_sems, recv_sems = refs[2 * n + len(after)], refs[2 * n + len(after) + 1]
        token = refs[-1]
        for cp in _peer_copies(kind, flags, src_refs, land_refs, send_sems, recv_sems, False):
            cp.start()
        token[...] = jnp.zeros_like(token)

    outs = pl.pallas_call(
        body, name=name,
        in_specs=[HBM] * (2 * n) + [ANY] * len(after),
        out_specs=(SEM, SEM) + (HBM,) * (2 * n) + (pl.BlockSpec(memory_space=pltpu.VMEM),),
        out_shape=(pltpu.SemaphoreType.DMA((ns,)), pltpu.SemaphoreType.DMA((ns,)))
        + tuple(pltpu.HBM(a.shape, a.dtype) for a in list(srcs) + lands) + (TOKEN,),
        input_output_aliases={i: 2 + i for i in range(2 * n)},
        compiler_params=pltpu.CompilerParams(has_side_effects=pltpu.SideEffectType.DATAFLOW_SIDE_EFFECTING),
    )(*[pltpu.with_memory_space_constraint(a, pltpu.HBM) for a in list(srcs) + lands], *after)
    return dict(sems=outs[:2], srcs=outs[2:2 + n], lands=outs[2 + n:2 + 2 * n], token=outs[-1], kind=kind, flags=flags)


def _copies_wait(started, after, name):
    n = len(started["srcs"])
    kind, flags = started["kind"], started["flags"]
    after = list(after) if isinstance(after, (list, tuple)) else [after]

    def body(*refs):
        src_refs, land_refs = refs[:n], refs[n:2 * n]
        send_sems, recv_sems = refs[2 * n], refs[2 * n + 1]
        for cp in _peer_copies(kind, flags, src_refs, land_refs, send_sems, recv_sems, True):
            cp.wait_send()
            cp.wait_recv()

    outs = pl.pallas_call(
        body, name=name,
        in_specs=[HBM] * (2 * n) + [SEM, SEM] + [ANY] * len(after),
        out_specs=(HBM,) * (2 * n),
        out_shape=tuple(pltpu.HBM(a.shape, a.dtype) for a in list(started["srcs"]) + list(started["lands"])),
        input_output_aliases={i: i for i in range(2 * n)},
        compiler_params=pltpu.CompilerParams(has_side_effects=pltpu.SideEffectType.DATAFLOW_SIDE_EFFECTING),
    )(*started["srcs"], *started["lands"], *started["sems"], *after)
    return outs[:n], outs[n:]


def _swap_whole(arrs, name):
    n = len(arrs)

    def body(*refs):
        ins, outs = refs[:n], refs[n:2 * n]
        send_sems, recv_sems = refs[2 * n:]
        x, y, c = _mesh_pos()
        cps = [pltpu.make_async_remote_copy(src_ref=ins[i], dst_ref=outs[i], send_sem=send_sems.at[i],
                                            recv_sem=recv_sems.at[i], device_id=(x, y, 1 - c), device_id_type=MESH)
               for i in range(n)]
        for cp in cps:
            cp.start()
        for cp in cps:
            cp.wait()

    return pl.pallas_call(
        body, name=name, in_specs=[ANY] * n, out_specs=[ANY] * n,
        out_shape=[jax.ShapeDtypeStruct(a.shape, a.dtype) for a in arrs],
        scratch_shapes=[pltpu.SemaphoreType.DMA((n,)), pltpu.SemaphoreType.DMA((n,))],
    )(*arrs)


def _as_rows(a, lead):
    shp = a.shape
    rows = 1
    for s in shp[lead:-1]:
        rows *= s
    return a.reshape(shp[:lead] + (rows, shp[-1]))


def _row_tile(rows, cols, n_bufs):
    budget = (24 * 1024 * 1024) // (n_bufs * 2 * 4 * cols)
    return _pick(rows, max(2 * SUBLANES, budget), 2 * SUBLANES)


def _sum_devices(own, got, dev, name):
    n, rows, cols = got.shape
    tr = _row_tile(rows, cols, n + 2)

    def body(dev_ref, own_ref, got_ref, o_ref):
        mine = own_ref[...]
        acc = jnp.where(dev_ref[0] == 0, mine, got_ref[0])
        for k in range(1, n):
            acc = acc + jnp.where(dev_ref[0] == k, mine, got_ref[k])
        o_ref[...] = acc

    return pl.pallas_call(
        body, name=name,
        grid_spec=pltpu.PrefetchScalarGridSpec(
            num_scalar_prefetch=1, grid=(rows // tr,),
            in_specs=[pl.BlockSpec((tr, cols), lambda i, d: (i, 0)), pl.BlockSpec((n, tr, cols), lambda i, d: (0, i, 0))],
            out_specs=pl.BlockSpec((tr, cols), lambda i, d: (i, 0))),
        out_shape=jax.ShapeDtypeStruct((rows, cols), F32), compiler_params=_cp("parallel"),
    )(_scalar(dev), own, got)


def _scalar(i):
    return jnp.reshape(i, (1,)).astype(jnp.int32)


def _add_own_half(g, other, c, half_first, name):
    _, rows, cols = other.shape
    tr = _row_tile(rows, cols, 3)

    def body(c_ref, g_ref, o_ref, out_ref):
        out_ref[0] = (g_ref[0, 0] + o_ref[0]).astype(out_ref.dtype)

    if half_first:
        g_map = lambda k, i, c_ref: (c_ref[0], k, i, 0)
    else:
        g_map = lambda k, i, c_ref: (k, c_ref[0], i, 0)
    flat = pl.BlockSpec((1, tr, cols), lambda k, i, c_ref: (k, i, 0))
    return pl.pallas_call(
        body, name=name,
        grid_spec=pltpu.PrefetchScalarGridSpec(
            num_scalar_prefetch=1, grid=(N_CHIPS, rows // tr),
            in_specs=[pl.BlockSpec((1, 1, tr, cols), g_map), flat], out_specs=flat),
        out_shape=jax.ShapeDtypeStruct(other.shape, COMM_DTYPE), compiler_params=_cp("parallel", "parallel"),
    )(_scalar(c), g, other)


def _sum_chips(own, got, chip, name, transpose=False):
    _, rows, cols = own.shape
    tr = rows if transpose else _row_tile(rows, cols, N_CHIPS + 2)

    def body(chip_ref, p_ref, b_ref, o_ref):
        mine = p_ref[0].astype(F32)
        acc = jnp.where(chip_ref[0] == 0, mine, b_ref[0].astype(F32))
        for k in range(1, N_CHIPS):
            acc = acc + jnp.where(chip_ref[0] == k, mine, b_ref[k].astype(F32))
        o_ref[...] = acc.T if transpose else acc

    if transpose:
        out_spec, out_shape = pl.BlockSpec((cols, rows), lambda i, k_ref: (0, 0)), (cols, rows)
    else:
        out_spec, out_shape = pl.BlockSpec((tr, cols), lambda i, k_ref: (i, 0)), (rows, cols)
    return pl.pallas_call(
        body, name=name,
        grid_spec=pltpu.PrefetchScalarGridSpec(
            num_scalar_prefetch=1, grid=(rows // tr,),
            in_specs=[pl.BlockSpec((1, tr, cols), lambda i, k_ref: (k_ref[0], i, 0)),
                      pl.BlockSpec((N_CHIPS, tr, cols), lambda i, k_ref: (0, i, 0))],
            out_specs=out_spec),
        out_shape=jax.ShapeDtypeStruct(out_shape, F32), compiler_params=_cp("parallel"),
    )(_scalar(chip), own, got)


def _adam_math(w, g, m, v):
    nm = ADAM_B1 * m + (1.0 - ADAM_B1) * g
    nv = ADAM_B2 * v + (1.0 - ADAM_B2) * (g * g)
    m_hat = nm / (1.0 - ADAM_B1 ** ADAM_STEP)
    v_hat = nv / (1.0 - ADAM_B2 ** ADAM_STEP)
    return -ADAM_LR * (m_hat / (jnp.sqrt(v_hat) + ADAM_EPS) + ADAM_WD * w), nm, nv


def _adamw_pieces(w, m, v, mine, theirs, c, kind, name):
    shape = w.shape
    P = len(mine)
    ws, ms, vs = (t.reshape((P, -1, t.shape[-1])) for t in (w, m, v))
    _, R, C = ws.shape
    if kind == "rows":
        tr = _pick(R // 2, 512, SUBLANES)
    else:
        tr = _pick(R, 256 if kind in ("lo", "hi") else 512, SUBLANES)
    nt = R // tr
    nh = nt // 2

    def body(c_ref, w_ref, m_ref, v_ref, *refs):
        mine_refs, theirs_refs = refs[:P], refs[P:2 * P]
        g_ref, d_ref, nm_ref, nv_ref = refs[2 * P:]
        p, i, core = pl.program_id(0), pl.program_id(1), c_ref[0]

        def pick(refs_):
            out = refs_[0][...]
            for q in range(1, P):
                out = jnp.where(p == q, refs_[q][...], out)
            return out

        a, b = pick(mine_refs), pick(theirs_refs)
        if kind == "cols":
            gv = jnp.where(core == 0, jnp.concatenate([a, b], axis=1), jnp.concatenate([b, a], axis=1))
        else:
            own = {"lo": core == 0, "hi": core == 1, "rows": (i >= nh) == (core == 1)}[kind]
            gv = jnp.where(own, a, b)
        g_ref[0] = gv
        d_ref[0], nm_ref[0], nv_ref[0] = _adam_math(w_ref[0], gv, m_ref[0], v_ref[0])

    def piece_spec(q):
        tile = (lambda i: i - jnp.where(i >= nh, nh, 0)) if kind == "rows" else (lambda i: i)
        return pl.BlockSpec((tr, mine[q].shape[1]), lambda p, i, c_ref: (jnp.where(p == q, tile(i), 0), 0))

    full = pl.BlockSpec((1, tr, C), lambda p, i, c_ref: (p, i, 0))
    outs = pl.pallas_call(
        body, name=name,
        grid_spec=pltpu.PrefetchScalarGridSpec(num_scalar_prefetch=1, grid=(P, nt),
                                               in_specs=[full] * 3 + [piece_spec(q) for q in range(P)] * 2,
                                               out_specs=[full] * 4),
        out_shape=[jax.ShapeDtypeStruct((P, R, C), F32)] * 4, compiler_params=_cp("parallel", "arbitrary"),
    )(_scalar(c), ws, ms, vs, *mine, *theirs)
    return tuple(o.reshape(shape) for o in outs)


def _adamw(w, g, m, v, name):
    shape = w.shape
    ws, gs, ms, vs = (_as_rows(t, 0) for t in (w, g, m, v))
    rows, cols = ws.shape
    tr = _row_tile(rows, cols, 7)

    def body(w_ref, g_ref, m_ref, v_ref, d_ref, nm_ref, nv_ref):
        d_ref[...], nm_ref[...], nv_ref[...] = _adam_math(w_ref[...], g_ref[...], m_ref[...], v_ref[...])

    spec = pl.BlockSpec((tr, cols), lambda i: (i, 0))
    outs = pl.pallas_call(body, name=name, grid=(rows // tr,), in_specs=[spec] * 4, out_specs=[spec] * 3,
                          out_shape=[jax.ShapeDtypeStruct((rows, cols), F32)] * 3, compiler_params=_cp("parallel"))(ws, gs, ms, vs)
    return tuple(o.reshape(shape) for o in outs)


_BIG = ["ffn_w_gate", "ffn_w_up", "ffn_w_down", "dn_w_in", "dn_w_out", "sg_w_in", "sg_w_out"]
_SMALL_SHARDED = ["norm_g", "dn_conv_w", "sg_b_in", "sg_ln_g", "sg_ln_b"]
_SMALL_REPL = ["dn_a_log", "dn_dt_bias", "dn_norm_g", "sg_w_s", "sg_b_s"]
_WEIGHTS = ["norm_g", "ffn_w_gate", "ffn_w_up", "ffn_w_down", "dn_w_in", "dn_conv_w", "dn_a_log", "dn_dt_bias",
            "dn_norm_g", "dn_w_out", "sg_w_in", "sg_b_in", "sg_ln_g", "sg_ln_b", "sg_w_s", "sg_b_s", "sg_w_out"]
PACK_COLS = 1024


def _pack(arrs):
    flat = jnp.concatenate([a.reshape(-1) for a in arrs])
    pad = (-flat.shape[0]) % (SUBLANES * PACK_COLS)
    return jnp.pad(flat, (0, pad)).reshape(-1, PACK_COLS)


def _unpack(buf, shapes):
    flat = buf.reshape(-1)
    out, off = [], 0
    for s in shapes:
        n = math.prod(s)
        out.append(flat[off:off + n].reshape(s))
        off += n
    return out


def _as_halves(a):
    if a.shape[0] == 2:
        return a
    if a.shape[0] == 1:
        return a.reshape((2, a.shape[1] // 2) + a.shape[2:])
    return a.reshape((2, a.shape[0] // 2) + a.shape[1:])


def _with_own(gathered, own, chip):
    g = gathered.reshape((N_CHIPS,) + own.shape)
    return [jnp.where(chip == k, own, g[k]) for k in range(N_CHIPS)]


def _cat_shards(g, axis):
    return jnp.concatenate(list(g), axis=axis)


_GROUP_ORDER = ["ffn00", "dn", "ffn01", "ffn10", "sg", "ffn11"]


def _weight_groups(w):
    cast = {k: _mx(w[k]) for k in _BIG}
    groups = {"ffn%d%d" % (i, j): [cast["ffn_w_gate"][i, j].T, cast["ffn_w_up"][i, j].T, cast["ffn_w_down"][i, j]]
              for i, j in [(0, 0), (0, 1), (1, 0), (1, 1)]}
    groups["dn"] = [cast["dn_w_in"][0], cast["dn_w_out"][0]]
    groups["sg"] = [cast["sg_w_in"][0], cast["sg_w_out"][0]]
    return groups


def _ffn_weights(chip, own, gathered):
    pairs = [(a, g.reshape((N_CHIPS,) + a.shape)) for a, g in zip(own, gathered)]
    return {"chip": chip, "gate": pairs[0], "up": pairs[1], "down": pairs[2]}


def _group_matrices(group, shards):
    if group == "sg":
        return {"sg_win": _cat_shards(shards[0], 1), "sg_wout": _cat_shards(shards[1], 0)}
    dn_full = _cat_shards(shards[0], 1)
    W4 = 4 * DN_HEADS * DN_HEAD_DIM
    wba = jnp.zeros((D_MODEL, 2 * LANES), dn_full.dtype)
    wba = wba.at[:, :DN_HEADS].set(dn_full[:, W4:W4 + DN_HEADS])
    wba = wba.at[:, LANES:LANES + DN_HEADS].set(dn_full[:, W4 + DN_HEADS:])
    return {"dn_wqkvz": dn_full[:, :W4], "dn_wba": wba, "dn_wout": _cat_shards(shards[1], 0)}


def _split_cols(a, n):
    w = a.shape[-1] // n
    return [a[..., k * w:(k + 1) * w] for k in range(n)]


def _split_rows(a, n):
    h = a.shape[-2] // n
    return [a[..., k * h:(k + 1) * h, :] for k in range(n)]


_IJ = [(0, 0), (0, 1), (1, 0), (1, 1)]


def _group_grads(group, grads):
    def rows_by_chip(a):
        return a.reshape(N_CHIPS, 2, a.shape[0] // (2 * N_CHIPS), a.shape[1])

    if group.startswith("ffn"):
        tag = group[3:]
        t = grads["wguT" + tag]
        return (["wguT" + tag, "wd" + tag],
                [t.reshape(2, N_CHIPS, t.shape[0] // (2 * N_CHIPS), t.shape[1]), rows_by_chip(grads["wd" + tag])], [True, False])
    if group == "sg":
        return ["sg_w_inT", "sg_w_out"], [rows_by_chip(grads["sg_w_inT"]), rows_by_chip(grads["sg_w_out"])], [False, False]
    dn_in = jnp.stack([jnp.stack(_split_cols(hf, N_CHIPS)) for hf in _split_rows(grads["dn_w_in"], 2)])
    return ["dn_w_in", "dn_w_out"], [dn_in, rows_by_chip(grads["dn_w_out"])], [True, False]


_SHARD_PIECES = {
    "ffn_w_gate": (["wguT%d%d" % ij for ij in _IJ], "lo"),
    "ffn_w_up": (["wguT%d%d" % ij for ij in _IJ], "hi"),
    "ffn_w_down": (["wd%d%d" % ij for ij in _IJ], "rows"),
    "dn_w_in": (["dn_w_in"], "rows"),
    "dn_w_out": (["dn_w_out"], "rows"),
    "sg_w_in": (["sg_w_inT"], "cols"),
    "sg_w_out": (["sg_w_out"], "rows"),
}


def kernel(x, norm_g, ffn_w_gate, ffn_w_up, ffn_w_down, dn_w_in, dn_conv_w, dn_a_log, dn_dt_bias, dn_norm_g, dn_w_out, sg_w_in, sg_b_in, sg_ln_g, sg_ln_b, sg_w_s, sg_b_s, sg_w_out, loss_target, m_norm_g, m_ffn_w_gate, m_ffn_w_up, m_ffn_w_down, m_dn_w_in, m_dn_conv_w, m_dn_a_log, m_dn_dt_bias, m_dn_norm_g, m_dn_w_out, m_sg_w_in, m_sg_b_in, m_sg_ln_g, m_sg_ln_b, m_sg_w_s, m_sg_b_s, m_sg_w_out, v_norm_g, v_ffn_w_gate, v_ffn_w_up, v_ffn_w_down, v_dn_w_in, v_dn_conv_w, v_dn_a_log, v_dn_dt_bias, v_dn_norm_g, v_dn_w_out, v_sg_w_in, v_sg_b_in, v_sg_ln_g, v_sg_ln_b, v_sg_w_s, v_sg_b_s, v_sg_w_out):
    args = dict(locals())
    w = {k: args[k] for k in _WEIGHTS}
    mom = {k: args["m_" + k] for k in _WEIGHTS}
    var = {k: args["v_" + k] for k in _WEIGHTS}
    cx, cy, cc = _mesh_pos()
    chip = 2 * cx + cy

    small_shapes = [w[k].shape for k in _SMALL_SHARDED]
    groups = _weight_groups(w)
    own = groups[_GROUP_ORDER[0]] + [_pack([w[k] for k in _SMALL_SHARDED])]
    first = _allgather_chips([_as_halves(a) for a in own], "gather_first")
    started, after = {}, first[0]
    for g in _GROUP_ORDER[1:]:
        started[g] = _copies_start("gather", groups[g], after, "gather_start_" + g)
        after = started[g]["token"]
    small_k = [_unpack(pack, small_shapes) for pack in _with_own(first[-1], own[-1], chip)]
    p = {name: jnp.concatenate([small_k[k][i] for k in range(N_CHIPS)], axis=-1) for i, name in enumerate(_SMALL_SHARDED)}
    p = {k: (v if k == "norm_g" else v[0]) for k, v in p.items()}
    p["norm_g"] = p["norm_g"] + after[0, 0]
    for k in _SMALL_REPL:
        p[k] = w[k][0]

    def weights_for(group, after):
        if group == _GROUP_ORDER[0]:
            return _ffn_weights(chip, own[:-1], first[:-1])
        srcs, lands = _copies_wait(started[group], after, "gather_wait_" + group)
        if group.startswith("ffn"):
            return _ffn_weights(chip, srcs, lands)
        return _group_matrices(group, [_with_own(l, a, chip) for l, a in zip(lands, srcs)])

    mine, theirs, to_core, to_chips = {}, {}, [], []

    def send_to_chips(after):
        group, names, flags, swap = to_core.pop(0)
        halves, got = _copies_wait(swap, after, "swap_wait_" + group)
        pair_sum = [_add_own_half(h, o, cc, hf, "pair_sum_" + n) for n, h, o, hf in zip(names, halves, got, flags)]
        scatter = _copies_start("scatter", pair_sum, got[0], "reduce_start_" + group)
        to_chips.append((group, names, scatter))
        return scatter["token"]

    def finish(after):
        group, names, scatter = to_chips.pop(0)
        pair_sum, got = _copies_wait(scatter, after, "reduce_wait_" + group)
        half_sum = [_sum_chips(a, b, chip, "chip_sum_" + n, transpose=n == "sg_w_inT")
                    for n, a, b in zip(names, pair_sum, got)]
        other = _swap_whole(half_sum, "gather_core_pair_" + group)
        mine.update(zip(names, half_sum))
        theirs.update(zip(names, other))

    def grads_ready(group, grads):
        names, halves, flags = _group_grads(group, grads)
        swap = _copies_start("swap", halves, None, "swap_start_" + group, flags)
        token = swap["token"]
        if to_core:
            token = send_to_chips(token)
            if len(to_chips) > 1:
                finish(token)
        to_core.append((group, names, flags, swap))
        return token[0, 0]

    small_names = _SMALL_SHARDED + _SMALL_REPL
    small = {}

    def small_ready(grads, loss_part):
        parts = [grads[k] for k in small_names]
        small["shapes"] = [g.shape for g in parts] + [(1,)]
        pack = _pack(parts + [loss_part[0, :1]])
        small["exchange"] = _copies_start("all", [pack], None, "small_start")
        return small["exchange"]["token"]

    loss_part, grad_x, grads = _local_step(x[0], loss_target[0], p, weights_for, grads_ready, small_ready)
    token = send_to_chips(to_core[0][3]["token"])
    finish(token)
    (pack,), (packs,) = _copies_wait(small["exchange"], list(theirs.values()), "small_wait")
    summed = _sum_devices(pack, packs, 4 * cx + 2 * cy + cc, "small_sum")
    parts = _unpack(summed, small["shapes"])
    loss = parts[-1][0]
    grad = {}
    for i, k in enumerate(small_names):
        g = parts[i]
        if k in _SMALL_SHARDED:
            n = w[k].shape[-1]
            g = lax.dynamic_slice_in_dim(g, chip * n, n, axis=g.ndim - 1)
        grad[k] = g

    delta, new_m, new_v = {}, {}, {}

    def update(keys):
        for k in keys:
            names, kind = _SHARD_PIECES[k]
            turn = (lambda a: jnp.swapaxes(a, -1, -2)) if names[0].startswith("wguT") else (lambda a: a)
            outs = _adamw_pieces(turn(w[k]), turn(mom[k]), turn(var[k]), [mine[n] for n in names], [theirs[n] for n in names],
                                 cc, kind, "adamw_" + k)
            grad[k], delta[k], new_m[k], new_v[k] = (turn(o) for o in outs)

    shapes = [w[k].shape for k in small_names]
    d, nm, nv = _adamw(_pack([w[k] for k in small_names]), _pack([grad[k] for k in small_names]),
                       _pack([mom[k] for k in small_names]), _pack([var[k] for k in small_names]), "adamw_small")
    for k, a, b, c_ in zip(small_names, _unpack(d, shapes), _unpack(nm, shapes), _unpack(nv, shapes)):
        delta[k], new_m[k], new_v[k] = a, b, c_
    mixers = [k for k in _BIG if not k.startswith("ffn")]
    update(mixers)
    finish([d] + [delta[k] for k in mixers] + list(theirs.values()))
    update([k for k in _BIG if k.startswith("ffn")])

    return (loss, grad_x[None], *[grad[k] for k in _WEIGHTS], *[delta[k] for k in _WEIGHTS],
            *[new_m[k] for k in _WEIGHTS], *[new_v[k] for k in _WEIGHTS])
```

```python
import math

import jax
import jax.numpy as jnp
from jax import lax
from jax.experimental import pallas as pl
from jax.experimental.pallas import tpu as pltpu

F32 = jnp.float32
MXU_DTYPE = jnp.bfloat16
COMM_DTYPE = jnp.bfloat16
HI = lax.Precision.HIGHEST
TRI_PREC = lax.Precision.HIGH

D_MODEL = 1024
RMS_EPS = 1e-6
LN_EPS = 1e-5
L2_EPS = 1e-6
DN_HEADS = 8
DN_HEAD_DIM = 128
DN_CONV = 4
DN_CHUNK = 64
SG_WIDTH = 2048
SG_GROUPS = 8
SG_CHUNK = 128
SG_GROUP_W = SG_WIDTH // SG_GROUPS
N_CHIPS = 4
N_DEV = 8
LANES = 128
SUBLANES = 8
VMEM_LIMIT = 56 * 1024 * 1024

ADAM_LR = 0.001
ADAM_B1 = 0.9
ADAM_B2 = 0.999
ADAM_EPS = 1e-08
ADAM_WD = 0.01
ADAM_STEP = 10

MESH = pl.DeviceIdType.MESH
ANY = pl.BlockSpec(memory_space=pl.ANY)


def _cp(*sem):
    return pltpu.CompilerParams(dimension_semantics=sem, vmem_limit_bytes=VMEM_LIMIT)


def _pick(n, pref, mult=LANES):
    best = None
    d = mult
    while d <= min(n, pref):
        if n % d == 0:
            best = d
        d += mult
    return best if best is not None else n


def _full(shape):
    nd = len(shape)
    return pl.BlockSpec(shape, lambda *_: (0,) * nd)


def _sigmoid(x):
    return 1.0 / (1.0 + jnp.exp(-x))


def _dot(a, b, dims, prec=None):
    return lax.dot_general(a, b, (dims, ((), ())), preferred_element_type=F32, precision=prec)


NN = ((1,), (0,))
NT = ((1,), (1,))
TN = ((0,), (0,))


def _mx(a):
    return a.astype(MXU_DTYPE)


def _rms_stat(x):
    return lax.rsqrt(jnp.mean(x * x, axis=-1, keepdims=True) + RMS_EPS)


def _rms_bwd(x, r, g, dy):
    xh = x * r
    dxh = dy * g
    dx = r * (dxh - xh * jnp.mean(dxh * xh, axis=-1, keepdims=True))
    return dx, jnp.sum(dy * xh, axis=0, keepdims=True)


def _mm(a, b, mode, name, out_dtype=F32, add=None, after=None):
    if mode == "tn":
        K, M = a.shape
        N = b.shape[1]
    elif mode == "nt":
        M, K = a.shape
        N = b.shape[0]
    else:
        M, K = a.shape
        N = b.shape[1]
    tn = _pick(N, 1024)
    if mode == "tn":
        tm = _pick(M, 1024 if tn <= 512 else 1408)
        tk = _pick(K, 1024, SUBLANES)
    else:
        tm = _pick(M, max(512, min(2048, (1024 * 1024) // tn)), SUBLANES)
        tk = _pick(K, 2048)
    nk = K // tk
    grid = (N // tn, M // tm, nk)
    if mode == "nn":
        a_spec = pl.BlockSpec((tm, tk), lambda j, i, k: (i, k))
        b_spec = pl.BlockSpec((tk, tn), lambda j, i, k: (k, j))
        dims = NN
    elif mode == "nt":
        a_spec = pl.BlockSpec((tm, tk), lambda j, i, k: (i, k))
        b_spec = pl.BlockSpec((tn, tk), lambda j, i, k: (j, k))
        dims = NT
    else:
        a_spec = pl.BlockSpec((tk, tm), lambda j, i, k: (k, i))
        b_spec = pl.BlockSpec((tk, tn), lambda j, i, k: (k, j))
        dims = TN
    o_spec = pl.BlockSpec((tm, tn), lambda j, i, k: (i, j))
    has_add = add is not None

    def body(*refs):
        a_ref, b_ref = refs[:2]
        add_ref = refs[2] if has_add else None
        o_ref, acc = refs[-2:]
        k = pl.program_id(2)

        @pl.when(k == 0)
        def _():
            acc[...] = add_ref[...] if has_add else jnp.zeros_like(acc)

        acc[...] += _dot(a_ref[...], b_ref[...], dims)

        @pl.when(k == nk - 1)
        def _():
            o_ref[...] = acc[...].astype(o_ref.dtype)

    ins = [a, b] + ([add] if has_add else []) + ([after] if after is not None else [])
    specs = [a_spec, b_spec] + ([o_spec] if has_add else []) + ([ANY] if after is not None else [])
    return pl.pallas_call(
        body, name=name, grid=grid, in_specs=specs, out_specs=o_spec,
        out_shape=jax.ShapeDtypeStruct((M, N), out_dtype),
        scratch_shapes=[pltpu.VMEM((tm, tn), F32)],
        compiler_params=_cp("parallel", "parallel", "arbitrary"),
    )(*ins)


def _ffn_weight_operands(wt):
    return [_scalar(wt["chip"])] , [wt["gate"][0], wt["gate"][1], wt["up"][0], wt["up"][1], wt["down"][0], wt["down"][1]]


def _load_ffn_weights(chip_ref, shard_refs, wgu_v, wd_v, sem):
    fs = wd_v.shape[0] // N_CHIPS

    @pl.when(pl.program_id(0) == 0)
    def _():
        me = chip_ref[0]
        waits = []
        for t, (dst, base) in enumerate([(wgu_v, 0), (wgu_v, wd_v.shape[0]), (wd_v, 0)]):
            own, gathered = shard_refs[2 * t], shard_refs[2 * t + 1]
            for k in range(N_CHIPS):
                slot = dst.at[pl.ds(base + k * fs, fs), :]
                s = sem.at[t * N_CHIPS + k]

                @pl.when(me == k)
                def _(own=own, slot=slot, s=s):
                    pltpu.make_async_copy(own, slot, s).start()

                @pl.when(me != k)
                def _(gathered=gathered, k=k, slot=slot, s=s):
                    pltpu.make_async_copy(gathered.at[k], slot, s).start()

                waits.append(pltpu.make_async_copy(own, slot, s))
        for cp in waits:
            cp.wait()


def _ffn_fwd(x, g0, g1, wt, name, next_gain=None, target=None):
    T, D = x.shape
    F = N_CHIPS * wt["down"][0].shape[0]
    F2 = 2 * F
    tm = _pick(T, 256, SUBLANES)
    prefetch, shards = _ffn_weight_operands(wt)
    extra = [a for a in (next_gain, target) if a is not None]
    n_tail = 1 if next_gain is not None else (2 if target is not None else 0)

    def body(chip_ref, x_ref, g0_ref, g1_ref, *refs):
        extra_ref = refs[0] if extra else None
        refs = refs[len(extra):]
        shard_refs = refs[:6]
        xo_ref, h_ref, gu_ref, y_ref = refs[6:10]
        tail_refs = refs[10:10 + n_tail]
        wgu_v, wd_v, sem = refs[10 + n_tail:]
        _load_ffn_weights(chip_ref, shard_refs, wgu_v, wd_v, sem)
        xv = x_ref[...]
        hb = _mx(xv * _rms_stat(xv) * g0_ref[...])
        h_ref[...] = hb
        gu = _dot(hb, wgu_v[...], NT)
        gu_ref[...] = gu.astype(gu_ref.dtype)
        g = gu[:, :F]
        u = gu[:, F:]
        a = _mx(g * _sigmoid(g) * u)
        y = _dot(a, wd_v[...], NN)
        y_ref[...] = y
        xo = xv + 0.5 * (y * _rms_stat(y) * g1_ref[...])
        xo_ref[...] = xo
        if next_gain is not None:
            tail_refs[0][...] = _mx(xo * _rms_stat(xo) * extra_ref[...])
        if target is not None:
            loss_ref, dy_ref = tail_refs

            @pl.when(pl.program_id(0) == 0)
            def _():
                loss_ref[...] = jnp.zeros_like(loss_ref)

            e = xo - extra_ref[...]
            dy_ref[...] = e * (1.0 / D)
            loss_ref[...] += 0.5 * jnp.sum(jnp.mean(e * e, axis=-1, keepdims=True), axis=0, keepdims=True)

    row = lambda w: pl.BlockSpec((tm, w), lambda i, c: (i, 0))
    one = pl.BlockSpec((1, D), lambda i, c: (0, 0))
    tail_specs, tail_shapes, extra_specs = [], [], []
    if next_gain is not None:
        extra_specs, tail_specs, tail_shapes = [one], [row(D)], [jax.ShapeDtypeStruct((T, D), MXU_DTYPE)]
    if target is not None:
        extra_specs = [row(D)]
        tail_specs = [pl.BlockSpec((SUBLANES, LANES), lambda i, c: (0, 0)), row(D)]
        tail_shapes = [jax.ShapeDtypeStruct((SUBLANES, LANES), F32), jax.ShapeDtypeStruct((T, D), F32)]
    return pl.pallas_call(
        body, name=name,
        grid_spec=pltpu.PrefetchScalarGridSpec(
            num_scalar_prefetch=1, grid=(T // tm,),
            in_specs=[row(D), one, one] + extra_specs + [ANY] * 6,
            out_specs=[row(D), row(D), row(F2), row(D)] + tail_specs,
            scratch_shapes=[pltpu.VMEM((F2, D), MXU_DTYPE), pltpu.VMEM((F, D), MXU_DTYPE),
                            pltpu.SemaphoreType.DMA((3 * N_CHIPS,))]),
        out_shape=[jax.ShapeDtypeStruct((T, D), F32), jax.ShapeDtypeStruct((T, D), MXU_DTYPE),
                   jax.ShapeDtypeStruct((T, F2), MXU_DTYPE), jax.ShapeDtypeStruct((T, D), F32)] + tail_shapes,
        compiler_params=_cp("arbitrary"),
    )(*prefetch, x, g0, g1, *extra, *shards)


FFN_BWD_CHUNK = 2816


def _ffn_bwd(dxo, x, y, gu, g0, g1, wt, name):
    T, D = x.shape
    F2 = gu.shape[1]
    F = F2 // 2
    tm = _pick(T, 256, SUBLANES)
    fc = _pick(F, FFN_BWD_CHUNK)
    prefetch, shards = _ffn_weight_operands(wt)

    def body(chip_ref, dxo_ref, x_ref, y_ref, gu_ref, g0_ref, g1_ref, *refs):
        shard_refs = refs[:6]
        dx_ref, dy_ref, a_ref, dgu_ref, dg0_ref, dg1_ref, wgu_v, wd_v, sem = refs[6:]
        _load_ffn_weights(chip_ref, shard_refs, wgu_v, wd_v, sem)

        @pl.when(pl.program_id(0) == 0)
        def _():
            dg0_ref[...] = jnp.zeros_like(dg0_ref)
            dg1_ref[...] = jnp.zeros_like(dg1_ref)

        dxo_v = dxo_ref[...]
        yv = y_ref[...]
        dy, dg1 = _rms_bwd(yv, _rms_stat(yv), g1_ref[...], 0.5 * dxo_v)
        dg1_ref[...] += dg1
        dyb = _mx(dy)
        dy_ref[...] = dyb
        dh = jnp.zeros((tm, D), F32)
        for c in range(F // fc):
            lo, hi = c * fc, (c + 1) * fc
            da = _dot(dyb, wd_v[lo:hi, :], NT)
            g = gu_ref[:, lo:hi].astype(F32)
            u = gu_ref[:, F + lo:F + hi].astype(F32)
            s = _sigmoid(g)
            sg = g * s
            a_ref[:, lo:hi] = _mx(sg * u)
            dg = _mx(da * u * (s * (1.0 + g * (1.0 - s))))
            du = _mx(da * sg)
            dgu_ref[:, lo:hi] = dg
            dgu_ref[:, F + lo:F + hi] = du
            dh = dh + _dot(dg, wgu_v[lo:hi, :], NN) + _dot(du, wgu_v[F + lo:F + hi, :], NN)
        xv = x_ref[...]
        dx, dg0 = _rms_bwd(xv, _rms_stat(xv), g0_ref[...], dh)
        dg0_ref[...] += dg0
        dx_ref[...] = dxo_v + dx

    row = lambda w: pl.BlockSpec((tm, w), lambda i, c: (i, 0))
    one = pl.BlockSpec((1, D), lambda i, c: (0, 0))
    return pl.pallas_call(
        body, name=name,
        grid_spec=pltpu.PrefetchScalarGridSpec(
            num_scalar_prefetch=1, grid=(T // tm,),
            in_specs=[row(D), row(D), row(D), row(F2), one, one] + [ANY] * 6,
            out_specs=[row(D), row(D), row(F), row(F2), one, one],
            scratch_shapes=[pltpu.VMEM((F2, D), MXU_DTYPE), pltpu.VMEM((F, D), MXU_DTYPE),
                            pltpu.SemaphoreType.DMA((3 * N_CHIPS,))]),
        out_shape=[jax.ShapeDtypeStruct((T, D), F32), jax.ShapeDtypeStruct((T, D), MXU_DTYPE),
                   jax.ShapeDtypeStruct((T, F), MXU_DTYPE), jax.ShapeDtypeStruct((T, F2), MXU_DTYPE),
                   jax.ShapeDtypeStruct((1, D), F32), jax.ShapeDtypeStruct((1, D), F32)],
        compiler_params=_cp("arbitrary"),
    )(*prefetch, dxo, x, y, gu, g0, g1, *shards)


def _out_proj_postnorm(a, b, x, g, name):
    T, K = a.shape
    D = b.shape[1]
    tm = _pick(T, 512, SUBLANES)

    def body(a_ref, b_ref, x_ref, g_ref, m_ref, o_ref):
        mv = _dot(a_ref[...], b_ref[...], NN)
        m_ref[...] = mv
        o_ref[...] = x_ref[...] + mv * _rms_stat(mv) * g_ref[...]

    row = lambda w: pl.BlockSpec((tm, w), lambda i: (i, 0))
    return pl.pallas_call(body, name=name, grid=(T // tm,),
                          in_specs=[row(K), _full((K, D)), row(D), _full((1, D))], out_specs=[row(D), row(D)],
                          out_shape=[jax.ShapeDtypeStruct((T, D), F32)] * 2, compiler_params=_cp("parallel"))(a, b, x, g)


def _postnorm_bwd_dgrad(dxo, m, g, b, name):
    T, D = m.shape
    K = b.shape[0]
    tm = _pick(T, 512, SUBLANES)

    def body(dxo_ref, m_ref, g_ref, b_ref, dm_ref, dg_ref, da_ref):
        @pl.when(pl.program_id(0) == 0)
        def _():
            dg_ref[...] = jnp.zeros_like(dg_ref)

        mv = m_ref[...]
        dm, dg = _rms_bwd(mv, _rms_stat(mv), g_ref[...], dxo_ref[...])
        dg_ref[...] += dg
        dmb = _mx(dm)
        dm_ref[...] = dmb
        da_ref[...] = _dot(dmb, b_ref[...], NT)

    row = lambda w: pl.BlockSpec((tm, w), lambda i: (i, 0))
    return pl.pallas_call(body, name=name, grid=(T // tm,), in_specs=[row(D), row(D), _full((1, D)), _full((K, D))],
                          out_specs=[row(D), _full((1, D)), row(K)],
                          out_shape=[jax.ShapeDtypeStruct((T, D), MXU_DTYPE), jax.ShapeDtypeStruct((1, D), F32),
                                     jax.ShapeDtypeStruct((T, K), F32)],
                          compiler_params=_cp("arbitrary"))(dxo, m, g, b)


def _dgrad_prenorm_bwd(a, b, add, dxo, x, g, name):
    T, K = a.shape
    D = b.shape[0]
    tm = _pick(T, 512, SUBLANES)
    tk = _pick(K, 2048)
    nk = K // tk
    has_add = add is not None

    def body(*refs):
        a_ref, b_ref = refs[:2]
        add_ref = refs[2] if has_add else None
        dxo_ref, x_ref, g_ref, dx_ref, dg_ref, acc = refs[-6:]
        i, k = pl.program_id(0), pl.program_id(1)

        @pl.when((i == 0) & (k == 0))
        def _():
            dg_ref[...] = jnp.zeros_like(dg_ref)

        @pl.when(k == 0)
        def _():
            acc[...] = add_ref[...] if has_add else jnp.zeros_like(acc)

        acc[...] += _dot(a_ref[...], b_ref[...], NT)

        @pl.when(k == nk - 1)
        def _():
            xv = x_ref[...]
            dx, dg = _rms_bwd(xv, _rms_stat(xv), g_ref[...], acc[...])
            dg_ref[...] += dg
            dx_ref[...] = dxo_ref[...] + dx

    row = pl.BlockSpec((tm, D), lambda i, k: (i, 0))
    one = pl.BlockSpec((1, D), lambda i, k: (0, 0))
    ins = [a, b] + ([add] if has_add else []) + [dxo, x, g]
    specs = ([pl.BlockSpec((tm, tk), lambda i, k: (i, k)), pl.BlockSpec((D, tk), lambda i, k: (0, k))]
             + ([row] if has_add else []) + [row, row, one])
    return pl.pallas_call(body, name=name, grid=(T // tm, nk), in_specs=specs, out_specs=[row, one],
                          out_shape=[jax.ShapeDtypeStruct((T, D), F32), jax.ShapeDtypeStruct((1, D), F32)],
                          scratch_shapes=[pltpu.VMEM((tm, D), F32)],
                          compiler_params=_cp("arbitrary", "arbitrary"))(*ins)


DN_ROWS = 512


def _shift_down(prev8, cur, s):
    n = cur.shape[0]
    xx = jnp.concatenate([prev8, cur], axis=0)
    return pltpu.roll(xx, s, 0)[SUBLANES:SUBLANES + n, :]


def _shift_up(cur, next8, s):
    n = cur.shape[0]
    xx = jnp.concatenate([cur, next8], axis=0)
    return pltpu.roll(xx, n + SUBLANES - s, 0)[:n, :]


def _tile_start(r, rows):
    return r * rows if isinstance(r, int) else pl.multiple_of(r * rows, SUBLANES)


def _conv_tile(x_ref, w, r, rows):
    start = _tile_start(r, rows)
    cur = x_ref[pl.ds(start, rows), :]
    if isinstance(r, int):
        prev8 = jnp.zeros((SUBLANES, cur.shape[1]), cur.dtype)
        taps = [_shift_down(prev8, cur, DN_CONV - 1 - j) if j < DN_CONV - 1 else cur for j in range(DN_CONV)]
    else:
        taps = [x_ref[pl.ds(start - (DN_CONV - 1 - j), rows), :] if j < DN_CONV - 1 else cur for j in range(DN_CONV)]
    c = taps[0] * w[0:1, :]
    for j in range(1, DN_CONV):
        c = c + taps[j] * w[j:j + 1, :]
    return c, taps


def _dn_prep_fwd(proj, conv_w, name):
    T = proj.shape[0]
    W = DN_HEADS * DN_HEAD_DIM
    rows = min(DN_ROWS, T)
    n_inner = T // rows
    scale = DN_HEAD_DIM ** -0.5

    def body(x_ref, w_ref, o_ref):
        cb = pl.program_id(0)
        w = w_ref[...]
        is_qk = cb < 2 * DN_HEADS
        post = jnp.where(cb < DN_HEADS, scale, 1.0)

        def step(r, carry):
            c, _ = _conv_tile(x_ref, w, r, rows)
            s = c * _sigmoid(c)
            rinv = lax.rsqrt(jnp.sum(s * s, axis=-1, keepdims=True) + L2_EPS)
            o_ref[pl.ds(_tile_start(r, rows), rows), :] = jnp.where(is_qk, s * rinv * post, s)
            return carry

        step(0, 0)
        lax.fori_loop(1, n_inner, step, 0)

    col = pl.BlockSpec((T, LANES), lambda j: (0, j))
    return pl.pallas_call(body, name=name, grid=(3 * W // LANES,),
                          in_specs=[col, pl.BlockSpec((DN_CONV, LANES), lambda j: (0, j))], out_specs=col,
                          out_shape=jax.ShapeDtypeStruct((T, 3 * W), F32), compiler_params=_cp("parallel"))(proj, conv_w)


def _dn_prep_bwd(proj, conv_w, dqkv, name):
    T = proj.shape[0]
    W = DN_HEADS * DN_HEAD_DIM
    rows = min(DN_ROWS, T)
    n_inner = T // rows
    scale = DN_HEAD_DIM ** -0.5

    def body(x_ref, w_ref, dy_ref, dx_ref, dw_ref, dc_scr):
        cb = pl.program_id(0)
        w = w_ref[...]
        is_qk = cb < 2 * DN_HEADS
        post = jnp.where(cb < DN_HEADS, scale, 1.0)

        def step1(r, dws):
            c, taps = _conv_tile(x_ref, w, r, rows)
            sg = _sigmoid(c)
            s = c * sg
            rinv = lax.rsqrt(jnp.sum(s * s, axis=-1, keepdims=True) + L2_EPS)
            dy = dy_ref[pl.ds(_tile_start(r, rows), rows), :]
            yn = s * rinv
            dyn = dy * post
            ds_qk = rinv * (dyn - yn * jnp.sum(dyn * yn, axis=-1, keepdims=True))
            ds = jnp.where(is_qk, ds_qk, dy)
            dc = ds * (sg * (1.0 + c * (1.0 - sg)))
            dc_scr[pl.ds(_tile_start(r, rows), rows), :] = dc
            return tuple(dws[j] + jnp.sum(dc * taps[j], axis=0, keepdims=True) for j in range(DN_CONV))

        zero = jnp.zeros((1, LANES), F32)
        dws = lax.fori_loop(1, n_inner, step1, step1(0, (zero,) * DN_CONV))
        for j in range(DN_CONV):
            dw_ref[j:j + 1, :] = dws[j]

        def step2(r, carry):
            start = _tile_start(r, rows)
            cur = dc_scr[pl.ds(start, rows), :]
            dx = cur * w[DN_CONV - 1:DN_CONV, :]
            for j in range(DN_CONV - 1):
                s = DN_CONV - 1 - j
                if isinstance(r, int):
                    up = _shift_up(cur, jnp.zeros((SUBLANES, LANES), F32), s)
                else:
                    up = dc_scr[pl.ds(start + s, rows), :]
                dx = dx + up * w[j:j + 1, :]
            dx_ref[pl.ds(start, rows), :] = _mx(dx)
            return carry

        lax.fori_loop(0, n_inner - 1, step2, 0)
        step2(n_inner - 1, 0)

    col = pl.BlockSpec((T, LANES), lambda j: (0, j))
    wspec = pl.BlockSpec((DN_CONV, LANES), lambda j: (0, j))
    return pl.pallas_call(body, name=name, grid=(3 * W // LANES,), in_specs=[col, wspec, col], out_specs=[col, wspec],
                          out_shape=[jax.ShapeDtypeStruct((T, 3 * W), MXU_DTYPE), jax.ShapeDtypeStruct((DN_CONV, 3 * W), F32)],
                          scratch_shapes=[pltpu.VMEM((T, LANES), F32)], compiler_params=_cp("parallel"))(proj, conv_w, dqkv)


def _softplus(x):
    return jnp.maximum(x, 0.0) + jnp.log(1.0 + jnp.exp(-jnp.abs(x)))


def _dn_gate_fwd(ba, a_log, dt_bias, name):
    T = ba.shape[0]
    tm = _pick(T, 1024, SUBLANES)

    def body(ba_ref, al_ref, dt_ref, beta_ref, g_ref):
        beta_ref[...] = _sigmoid(ba_ref[:, :LANES])
        g_ref[...] = -jnp.exp(al_ref[...]) * _softplus(ba_ref[:, LANES:] + dt_ref[...])

    row = lambda w: pl.BlockSpec((tm, w), lambda i: (i, 0))
    return pl.pallas_call(body, name=name, grid=(T // tm,), in_specs=[row(2 * LANES), _full((1, LANES)), _full((1, LANES))],
                          out_specs=[row(LANES), row(LANES)],
                          out_shape=[jax.ShapeDtypeStruct((T, LANES), F32)] * 2, compiler_params=_cp("parallel"))(ba, a_log, dt_bias)


def _dn_gate_bwd(ba, a_log, dt_bias, dbeta, dg, name):
    T = ba.shape[0]
    tm = _pick(T, 1024, SUBLANES)

    def body(ba_ref, al_ref, dt_ref, dbeta_ref, dg_ref, dba_ref, dal_ref, ddt_ref):
        @pl.when(pl.program_id(0) == 0)
        def _():
            dal_ref[...] = jnp.zeros_like(dal_ref)
            ddt_ref[...] = jnp.zeros_like(ddt_ref)

        beta = _sigmoid(ba_ref[:, :LANES])
        dba_ref[:, :LANES] = _mx(dbeta_ref[...] * beta * (1.0 - beta))
        pre = ba_ref[:, LANES:] + dt_ref[...]
        ea = jnp.exp(al_ref[...])
        dgv = dg_ref[...]
        da = dgv * (-ea) * _sigmoid(pre)
        dba_ref[:, LANES:] = _mx(da)
        ddt_ref[...] += jnp.sum(da, axis=0, keepdims=True)
        dal_ref[...] += jnp.sum(dgv * (-ea) * _softplus(pre), axis=0, keepdims=True)

    row = lambda w: pl.BlockSpec((tm, w), lambda i: (i, 0))
    one = _full((1, LANES))
    return pl.pallas_call(body, name=name, grid=(T // tm,), in_specs=[row(2 * LANES), one, one, row(LANES), row(LANES)],
                          out_specs=[row(2 * LANES), one, one],
                          out_shape=[jax.ShapeDtypeStruct((T, 2 * LANES), MXU_DTYPE), jax.ShapeDtypeStruct((1, LANES), F32),
                                     jax.ShapeDtypeStruct((1, LANES), F32)],
                          compiler_params=_cp("arbitrary"))(ba, a_log, dt_bias, dbeta, dg)


def _tri(c, strict):
    i = lax.broadcasted_iota(jnp.int32, (c, c), 0)
    j = lax.broadcasted_iota(jnp.int32, (c, c), 1)
    return (i > j) if strict else (i >= j)


def _inv_unit_lower(ls):
    c = ls[0].shape[0]
    i = lax.broadcasted_iota(jnp.int32, (c, c), 0)
    j = lax.broadcasted_iota(jnp.int32, (c, c), 1)
    eye = jnp.where(i == j, 1.0, 0.0)
    facs = [[eye - l for l in ls]]
    cur = ls
    for _ in range(int(math.log2(c)) - 1):
        cur = [_dot(p, p, NN, TRI_PREC) for p in cur]
        facs.append([eye + p for p in cur])
    while len(facs) > 1:
        nxt = [[_dot(a, b, NN, TRI_PREC) for a, b in zip(facs[t], facs[t + 1])] for t in range(0, len(facs) - 1, 2)]
        if len(facs) % 2:
            nxt.append(facs[-1])
        facs = nxt
    return facs[0]


def _chunk_gates(g_blk):
    c = g_blk.shape[0]
    gcs = _dot(jnp.where(_tri(c, False), 1.0, 0.0), g_blk, NN, HI)
    return gcs, gcs.T


def _head_chunk(h, qh, kh, vh, beta_blk, gcs, gcs_t):
    c = qh.shape[0]
    incl = _tri(c, False)
    gc_col = gcs[:, h:h + 1]
    gc_row = gcs_t[h:h + 1, :]
    gc_last = gcs_t[h:h + 1, c - 1:c]
    dec = jnp.where(incl, jnp.exp(jnp.where(incl, gc_col - gc_row, 0.0)), 0.0)
    gam = jnp.exp(gc_col)
    rr = jnp.exp(gc_last - gc_col)
    gl = jnp.exp(gc_last)
    b = beta_blk[:, h:h + 1]
    kb = kh * b
    vb = vh * b
    both = _dot(jnp.concatenate([_mx(kb), _mx(qh)], axis=0), _mx(kh), NT)
    lmat = jnp.where(_tri(c, True), both[:c] * dec, 0.0)
    pmat = jnp.where(incl, both[c:] * dec, 0.0)
    return dict(dec=dec, gam=gam, rr=rr, gl=gl, b=b, kb=kb, vb=vb, lmat=lmat, pmat=pmat)


def _solve_uw(tinv, q):
    return _dot(tinv, jnp.concatenate([q["vb"], q["kb"] * q["gam"]], axis=1), NN, TRI_PREC)


def _dn_scan_fwd(qkv, beta, g, proj, norm_g, name):
    T = qkv.shape[0]
    C, H, Dh = DN_CHUNK, DN_HEADS, DN_HEAD_DIM
    W = H * Dh
    N = T // C

    def body(q_ref, k_ref, v_ref, beta_ref, g_ref, z_ref, ng_ref, og_ref, o_ref, tinv_ref, s_ref, state):
        @pl.when(pl.program_id(0) == 0)
        def _():
            state[...] = jnp.zeros_like(state)

        ng = ng_ref[...]
        heads = range(H)
        cs = [slice(h * Dh, (h + 1) * Dh) for h in heads]

        def chunk(j, carry):
            rows = pl.ds(pl.multiple_of(j * C, C), C)
            gcs, gcs_t = _chunk_gates(g_ref[rows, :])
            beta_blk = beta_ref[rows, :]
            qs = [_head_chunk(h, q_ref[rows, cs[h]], k_ref[rows, cs[h]], v_ref[rows, cs[h]], beta_blk, gcs, gcs_t)
                  for h in heads]
            tinvs = _inv_unit_lower([q["lmat"] for q in qs])
            for h in heads:
                tinv_ref[h, rows, :] = tinvs[h]
            uws = [_solve_uw(tinvs[h], qs[h]) for h in heads]
            ss = [state[h] for h in heads]
            for h in heads:
                s_ref[j, h] = ss[h]
            sbs = [_mx(s) for s in ss]
            vnbs = [_mx(uws[h][:, :Dh] - _dot(_mx(uws[h][:, Dh:]), sbs[h], NN)) for h in heads]
            os_ = [_dot(jnp.concatenate([_mx(q_ref[rows, cs[h]] * qs[h]["gam"]), _mx(qs[h]["pmat"])], axis=1),
                        jnp.concatenate([sbs[h], vnbs[h]], axis=0), NN) for h in heads]
            for h in heads:
                state[h] = ss[h] * qs[h]["gl"] + _dot(_mx((k_ref[rows, cs[h]] * qs[h]["rr"]).T), vnbs[h], NN)
            for h in heads:
                o = os_[h]
                o_ref[rows, cs[h]] = o
                zh = z_ref[rows, cs[h]]
                og_ref[rows, cs[h]] = _mx(o * _rms_stat(o) * ng * (zh * _sigmoid(zh)))
            return carry

        lax.fori_loop(0, PER, chunk, 0)

    PER = 2 if N % 2 == 0 else 1
    blk = lambda j: pl.BlockSpec((PER * C, W), lambda n: (n, j))
    small = pl.BlockSpec((PER * C, LANES), lambda n: (n, 0))
    return pl.pallas_call(
        body, name=name, grid=(N // PER,),
        in_specs=[blk(0), blk(1), blk(2), small, small, blk(3), _full((1, Dh))],
        out_specs=[blk(0), blk(0), pl.BlockSpec((H, PER * C, C), lambda n: (0, n, 0)),
                   pl.BlockSpec((PER, H, Dh, Dh), lambda n: (n, 0, 0, 0))],
        out_shape=[jax.ShapeDtypeStruct((T, W), MXU_DTYPE), jax.ShapeDtypeStruct((T, W), F32),
                   jax.ShapeDtypeStruct((H, T, C), F32), jax.ShapeDtypeStruct((N, H, Dh, Dh), F32)],
        scratch_shapes=[pltpu.VMEM((H, Dh, Dh), F32)],
        compiler_params=_cp("arbitrary"),
    )(qkv, qkv, qkv, beta, g, proj, norm_g)


def _dn_scan_bwd(qkv, beta, g, proj, norm_g, o, tinv, s_all, dog, name):
    T = qkv.shape[0]
    C, H, Dh = DN_CHUNK, DN_HEADS, DN_HEAD_DIM
    W = H * Dh
    N = T // C

    def body(q_ref, k_ref, v_ref, beta_ref, g_ref, z_ref, ng_ref, o_ref, tinv_ref, s_ref, dog_ref,
             dqkv_ref, dbeta_ref, dg_ref, dz_ref, dng_ref, dstate):
        @pl.when(pl.program_id(0) == 0)
        def _():
            dstate[...] = jnp.zeros_like(dstate)
            dng_ref[...] = jnp.zeros_like(dng_ref)

        gcs, gcs_t = _chunk_gates(g_ref[...])
        beta_blk = beta_ref[...]
        ng = ng_ref[...]
        incl = _tri(C, False)
        strict = _tri(C, True)
        lane = lax.broadcasted_iota(jnp.int32, (C, LANES), 1)
        rowi = lax.broadcasted_iota(jnp.int32, (C, 1), 0)
        headrow = lax.broadcasted_iota(jnp.int32, (LANES, C), 0)
        colsums = jnp.zeros((LANES, C), F32)
        dbeta_acc = jnp.zeros((C, LANES), F32)
        dgc_acc = jnp.zeros((C, LANES), F32)
        dng_acc = jnp.zeros((1, Dh), F32)
        cs = [slice(h * Dh, (h + 1) * Dh) for h in range(H)]
        rsum = lambda t: jnp.sum(t, axis=1, keepdims=True)
        for heads in (range(0, H // 2), range(H // 2, H)):
            dobs = {}
            for h in heads:
                oh, zh, dogh = o_ref[:, cs[h]], z_ref[:, cs[h]], dog_ref[:, cs[h]]
                rstat = _rms_stat(oh)
                sz = _sigmoid(zh)
                dz_ref[:, cs[h]] = _mx(dogh * (oh * rstat * ng) * (sz * (1.0 + zh * (1.0 - sz))))
                do, dng = _rms_bwd(oh, rstat, ng, dogh * (zh * sz))
                dng_acc = dng_acc + dng
                dobs[h] = _mx(do)
            qs = {h: _head_chunk(h, q_ref[:, cs[h]], k_ref[:, cs[h]], v_ref[:, cs[h]], beta_blk, gcs, gcs_t) for h in heads}
            tms = {h: tinv_ref[h] for h in heads}
            uws = {h: _solve_uw(tms[h], qs[h]) for h in heads}
            ss = {h: s_ref[0, h] for h in heads}
            sbs = {h: _mx(ss[h]) for h in heads}
            wbs = {h: _mx(uws[h][:, Dh:]) for h in heads}
            vnbs = {h: _mx(uws[h][:, :Dh] - _dot(wbs[h], sbs[h], NN)) for h in heads}
            dsns = {h: dstate[h] for h in heads}
            dsbs = {h: _mx(dsns[h]) for h in heads}
            dvnews = {h: _dot(_mx(qs[h]["pmat"]), dobs[h], TN) + _dot(_mx(k_ref[:, cs[h]] * qs[h]["rr"]), dsbs[h], NN)
                      for h in heads}
            dvb16s = {h: _mx(dvnews[h]) for h in heads}
            dps = {h: jnp.where(incl, _dot(dobs[h], vnbs[h], NT), 0.0) for h in heads}
            dqds = {h: _dot(dobs[h], sbs[h], NT) for h in heads}
            dkds = {h: _dot(vnbs[h], dsbs[h], NT) for h in heads}
            dgls = {h: jnp.sum(rsum(ss[h] * dsns[h]), axis=0, keepdims=True) for h in heads}
            dws = {h: -_dot(dvb16s[h], sbs[h], NT) for h in heads}
            for h in heads:
                dstate[h] = qs[h]["gl"] * dsns[h] + _dot(
                    jnp.concatenate([_mx(q_ref[:, cs[h]] * qs[h]["gam"]), -wbs[h]], axis=0),
                    jnp.concatenate([dobs[h], dvb16s[h]], axis=0), TN)
            dsols = {h: _dot(tms[h], jnp.concatenate([dvnews[h], dws[h]], axis=1), TN, TRI_PREC) for h in heads}
            dvbs = {h: dsols[h][:, :Dh] for h in heads}
            dkbgs = {h: dsols[h][:, Dh:] for h in heads}
            dls = {h: jnp.where(strict, -_dot(dsols[h], uws[h], NT, TRI_PREC), 0.0) for h in heads}
            mmats = {h: dls[h] * qs[h]["lmat"] + dps[h] * qs[h]["pmat"] for h in heads}
            dgcs = {h: rsum(mmats[h]) for h in heads}
            for h in heads:
                colsums = jnp.where(headrow == h, jnp.sum(mmats[h], axis=0, keepdims=True), colsums)
            dboth = {h: jnp.concatenate([_mx(dls[h] * qs[h]["dec"]), _mx(dps[h] * qs[h]["dec"])], axis=0) for h in heads}
            for h in heads:
                q = qs[h]
                qh, kh, vh = q_ref[:, cs[h]], k_ref[:, cs[h]], v_ref[:, cs[h]]
                gam, rr, b, kb = q["gam"], q["rr"], q["b"], q["kb"]
                on_k = _dot(dboth[h], _mx(kh), NN)
                dkb = on_k[:C] + dkbgs[h] * gam
                dk = _dot(dboth[h], jnp.concatenate([_mx(kb), _mx(qh)], axis=0), TN) + dkb * b + dkds[h] * rr
                dq = on_k[C:] + dqds[h] * gam
                dgam = rsum(dkbgs[h] * kb) + rsum(dqds[h] * qh)
                dr = rsum(dkds[h] * kh)
                dgc_last = jnp.sum(dr * rr, axis=0, keepdims=True) + dgls[h] * q["gl"]
                dgc = dgcs[h] + dgam * gam - dr * rr + jnp.where(rowi == C - 1, dgc_last, 0.0)
                dbeta = rsum(dvbs[h] * vh) + rsum(dkb * kh)
                dqkv_ref[:, cs[h]] = dq
                dqkv_ref[:, W + h * Dh:W + (h + 1) * Dh] = dk
                dqkv_ref[:, 2 * W + h * Dh:2 * W + (h + 1) * Dh] = dvbs[h] * b
                dbeta_acc = jnp.where(lane == h, dbeta, dbeta_acc)
                dgc_acc = jnp.where(lane == h, dgc, dgc_acc)
        dbeta_ref[...] = dbeta_acc
        dg_ref[...] = _dot(jnp.where(incl, 1.0, 0.0), dgc_acc - colsums.T, TN, HI)
        dng_ref[...] += dng_acc

    rev = lambda n: N - 1 - n
    blk = lambda j: pl.BlockSpec((C, W), lambda n: (rev(n), j))
    small = pl.BlockSpec((C, LANES), lambda n: (rev(n), 0))
    return pl.pallas_call(
        body, name=name, grid=(N,),
        in_specs=[blk(0), blk(1), blk(2), small, small, blk(3), _full((1, Dh)), blk(0),
                  pl.BlockSpec((H, C, C), lambda n: (0, rev(n), 0)),
                  pl.BlockSpec((1, H, Dh, Dh), lambda n: (rev(n), 0, 0, 0)), blk(0)],
        out_specs=[pl.BlockSpec((C, 3 * W), lambda n: (rev(n), 0)), small, small, blk(0), _full((1, Dh))],
        out_shape=[jax.ShapeDtypeStruct((T, 3 * W), F32), jax.ShapeDtypeStruct((T, LANES), F32),
                   jax.ShapeDtypeStruct((T, LANES), F32), jax.ShapeDtypeStruct((T, W), MXU_DTYPE),
                   jax.ShapeDtypeStruct((1, Dh), F32)],
        scratch_shapes=[pltpu.VMEM((H, Dh, Dh), F32)],
        compiler_params=_cp("arbitrary"),
    )(qkv, qkv, qkv, beta, g, proj, norm_g, o, tinv, s_all, dog)


_INV_SQRT2 = 0.7071067811865476
_INV_SQRT_2PI = 0.3989422804014327


def _sg_recompute(zp_ref, bin_ref, lng_ref, lnb_ref):
    E = SG_WIDTH
    zin = zp_ref[...] + bin_ref[...]
    cdf = 0.5 * (1.0 + lax.erf(zin * _INV_SQRT2))
    zz = zin * cdf
    u = zz[:, :E]
    vp = zz[:, E:]
    mu = jnp.mean(vp, axis=-1, keepdims=True)
    xc = vp - mu
    rstd = lax.rsqrt(jnp.mean(xc * xc, axis=-1, keepdims=True) + LN_EPS)
    xhat = xc * rstd
    v = xhat * lng_ref[...] + lnb_ref[...]
    return zin, cdf, u, xhat, rstd, v


def _sg_masked_ws(ws_ref, g):
    return _mx(jnp.where(_tri(SG_CHUNK, False), ws_ref[g], 0.0))


def _sg_fwd(zpre, b_in, ln_g, ln_b, w_s, b_s_t, name):
    T = zpre.shape[0]
    E, G, C, GW = SG_WIDTH, SG_GROUPS, SG_CHUNK, SG_GROUP_W

    def body(zp_ref, bin_ref, lng_ref, lnb_ref, ws_ref, bst_ref, um_ref):
        _, _, u, _, _, v = _sg_recompute(zp_ref, bin_ref, lng_ref, lnb_ref)
        bst = bst_ref[...]
        for g in range(G):
            cs = slice(g * GW, (g + 1) * GW)
            mixed = _dot(_sg_masked_ws(ws_ref, g), _mx(v[:, cs]), NN) + bst[:, g:g + 1]
            um_ref[:, cs] = _mx(u[:, cs] * mixed)

    return pl.pallas_call(
        body, name=name, grid=(T // C,),
        in_specs=[pl.BlockSpec((C, 2 * E), lambda n: (n, 0)), _full((1, 2 * E)), _full((1, E)), _full((1, E)),
                  _full((G, C, C)), _full((C, LANES))],
        out_specs=pl.BlockSpec((C, E), lambda n: (n, 0)),
        out_shape=jax.ShapeDtypeStruct((T, E), MXU_DTYPE), compiler_params=_cp("parallel"),
    )(zpre, b_in, ln_g, ln_b, w_s, b_s_t)


def _sg_bwd(zpre, b_in, ln_g, ln_b, w_s, b_s_t, dum, name):
    T = zpre.shape[0]
    E, G, C, GW = SG_WIDTH, SG_GROUPS, SG_CHUNK, SG_GROUP_W

    def body(zp_ref, bin_ref, lng_ref, lnb_ref, ws_ref, bst_ref, dum_ref,
             dz_ref, dbin_ref, dlng_ref, dlnb_ref, dws_ref, dbst_ref):
        @pl.when(pl.program_id(0) == 0)
        def _():
            for r in (dbin_ref, dlng_ref, dlnb_ref, dws_ref, dbst_ref):
                r[...] = jnp.zeros_like(r)

        zin, cdf, u, xhat, rstd, v = _sg_recompute(zp_ref, bin_ref, lng_ref, lnb_ref)
        bst = bst_ref[...]
        lane = lax.broadcasted_iota(jnp.int32, (C, LANES), 1)
        dum_v = dum_ref[...]
        dbst = jnp.zeros((C, LANES), F32)
        du_parts, dv_parts = [], []
        for g in range(G):
            cs = slice(g * GW, (g + 1) * GW)
            wsm = _sg_masked_ws(ws_ref, g)
            vg = _mx(v[:, cs])
            mixed = _dot(wsm, vg, NN) + bst[:, g:g + 1]
            dumg = dum_v[:, cs]
            du_parts.append(dumg * mixed)
            dmixed = dumg * u[:, cs]
            dmb = _mx(dmixed)
            dv_parts.append(_dot(wsm, dmb, TN))
            dws_ref[g] += _dot(dmb, vg, NT)
            dbst = jnp.where(lane == g, jnp.sum(dmixed, axis=1, keepdims=True), dbst)
        dbst_ref[...] += dbst
        du = jnp.concatenate(du_parts, axis=1)
        dv = jnp.concatenate(dv_parts, axis=1)
        dlng_ref[...] += jnp.sum(dv * xhat, axis=0, keepdims=True)
        dlnb_ref[...] += jnp.sum(dv, axis=0, keepdims=True)
        dxh = dv * lng_ref[...]
        dvp = rstd * (dxh - jnp.mean(dxh, axis=-1, keepdims=True) - xhat * jnp.mean(dxh * xhat, axis=-1, keepdims=True))
        dzz = jnp.concatenate([du, dvp], axis=1)
        dzin = dzz * (cdf + zin * (_INV_SQRT_2PI * jnp.exp(-0.5 * zin * zin)))
        dz_ref[...] = _mx(dzin)
        dbin_ref[...] += jnp.sum(dzin, axis=0, keepdims=True)

    return pl.pallas_call(
        body, name=name, grid=(T // C,),
        in_specs=[pl.BlockSpec((C, 2 * E), lambda n: (n, 0)), _full((1, 2 * E)), _full((1, E)), _full((1, E)),
                  _full((G, C, C)), _full((C, LANES)), pl.BlockSpec((C, E), lambda n: (n, 0))],
        out_specs=[pl.BlockSpec((C, 2 * E), lambda n: (n, 0)), _full((1, 2 * E)), _full((1, E)), _full((1, E)),
                   _full((G, C, C)), _full((C, LANES))],
        out_shape=[jax.ShapeDtypeStruct((T, 2 * E), MXU_DTYPE), jax.ShapeDtypeStruct((1, 2 * E), F32),
                   jax.ShapeDtypeStruct((1, E), F32), jax.ShapeDtypeStruct((1, E), F32),
                   jax.ShapeDtypeStruct((G, C, C), F32), jax.ShapeDtypeStruct((C, LANES), F32)],
        compiler_params=_cp("arbitrary"),
    )(zpre, b_in, ln_g, ln_b, w_s, b_s_t, dum)


def _row(v):
    return v.reshape(1, -1)


def _pad_lanes(v):
    v = v.reshape(1, -1)
    return jnp.pad(v, ((0, 0), (0, LANES - v.shape[1])))


def _local_step(x, target, p, weights_for, grads_ready=None, small_ready=None):
    ng = p["norm_g"]
    grads = {}
    dng = [[None] * 6 for _ in range(2)]
    order = [jnp.zeros((), F32)]

    def tell(group):
        zero = grads_ready(group, grads) if grads_ready is not None else None
        if zero is not None:
            order[0] = zero

    def gain(i, s):
        return _row(ng[i, s]) + order[0]

    def ffn_f(xin, i, j, tag, **tail):
        wt = weights_for("ffn" + tag, xin)
        xo, h, gu, y, *rest = _ffn_fwd(xin, _row(ng[i, 4 * j]), _row(ng[i, 4 * j + 1]), wt, "ffn_fwd_" + tag, **tail)
        return (xo, *rest), (xin, h, gu, y, wt)

    (x1, hn0), sv_f00 = ffn_f(x, 0, 0, "00", next_gain=_row(ng[0, 2]))
    dnw = weights_for("dn", x1)
    proj = _mm(hn0, dnw["dn_wqkvz"], "nn", "dn_proj")
    ba = _mm(hn0, dnw["dn_wba"], "nn", "dn_proj_ba")
    a_log = _pad_lanes(p["dn_a_log"])
    dt_bias = _pad_lanes(p["dn_dt_bias"])
    dn_ng = _row(p["dn_norm_g"])
    qkv = _dn_prep_fwd(proj, p["dn_conv_w"], "dn_prep_fwd")
    beta, gdec = _dn_gate_fwd(ba, a_log, dt_bias, "dn_gate_fwd")
    og, o_raw, tinv, s_all = _dn_scan_fwd(qkv, beta, gdec, proj, dn_ng, "dn_scan_fwd")
    m0, x2 = _out_proj_postnorm(og, dnw["dn_wout"], x1, _row(ng[0, 3]), "dn_out")
    (x3,), sv_f01 = ffn_f(x2, 0, 1, "01")
    (x4, hn1), sv_f10 = ffn_f(x3, 1, 0, "10", next_gain=_row(ng[1, 2]))
    sgw = weights_for("sg", x4)
    zpre = _mm(hn1, sgw["sg_win"], "nn", "sg_proj")
    sg_bin = _row(p["sg_b_in"])
    sg_lng = _row(p["sg_ln_g"])
    sg_lnb = _row(p["sg_ln_b"])
    sg_bst = jnp.pad(p["sg_b_s"].T, ((0, 0), (0, LANES - SG_GROUPS)))
    um = _sg_fwd(zpre, sg_bin, sg_lng, sg_lnb, p["sg_w_s"], sg_bst, "sg_fwd")
    m1, x5 = _out_proj_postnorm(um, sgw["sg_wout"], x4, _row(ng[1, 3]), "sg_out")
    (_, loss_part, dx), sv_f11 = ffn_f(x5, 1, 1, "11", target=target)

    def ffn_b(dxo, sv, i, j, tag, last=False):
        xin, h, gu, y, wt = sv
        dxi, dy, a, dgu, dg0, dg1 = _ffn_bwd(dxo, xin, y, gu, gain(i, 4 * j), gain(i, 4 * j + 1), wt, "ffn_bwd_" + tag)
        dng[i][4 * j] = dg0
        dng[i][4 * j + 1] = dg1
        after = None
        if last:
            grads["norm_g"] = jnp.stack([jnp.concatenate(dng[t], axis=0) for t in range(2)])
            after = small_ready(grads, loss_part) if small_ready is not None else None
        grads["wd" + tag] = _mm(a, dy, "tn", "ffn_wgrad_down_" + tag, after=after)
        grads["wguT" + tag] = _mm(dgu, h, "tn", "ffn_wgrad_up_" + tag, after=after)
        tell("ffn" + tag)
        return dxi

    dx = ffn_b(dx, sv_f11, 1, 1, "11")
    dm1, dng[1][3], dum = _postnorm_bwd_dgrad(dx, m1, gain(1, 3), sgw["sg_wout"], "sg_dgrad_out")
    grads["sg_w_out"] = _mm(um, dm1, "tn", "sg_wgrad_out")
    dz1, dbin, dlng, dlnb, dws, dbst = _sg_bwd(zpre, sg_bin, sg_lng, sg_lnb, p["sg_w_s"], sg_bst, dum, "sg_bwd")
    grads["sg_w_inT"] = _mm(dz1, hn1, "tn", "sg_wgrad_in")
    tell("sg")
    dx, dng[1][2] = _dgrad_prenorm_bwd(dz1, sgw["sg_win"], None, dx, x4, gain(1, 2), "sg_dgrad_in")
    grads["sg_b_in"] = dbin.reshape(1, -1)
    grads["sg_ln_g"] = dlng.reshape(1, -1)
    grads["sg_ln_b"] = dlnb.reshape(1, -1)
    grads["sg_w_s"] = jnp.where(jnp.tril(jnp.ones((SG_CHUNK, SG_CHUNK), bool)), dws, 0.0)[None]
    grads["sg_b_s"] = dbst[:, :SG_GROUPS].T[None]
    dx = ffn_b(dx, sv_f10, 1, 0, "10")
    dx = ffn_b(dx, sv_f01, 0, 1, "01")
    dm0, dng[0][3], dog = _postnorm_bwd_dgrad(dx, m0, gain(0, 3), dnw["dn_wout"], "dn_dgrad_out")
    grads["dn_w_out"] = _mm(og, dm0, "tn", "dn_wgrad_out")
    dqkv, dbeta, dgdec, dz0, dnng = _dn_scan_bwd(qkv, beta, gdec, proj, dn_ng, o_raw, tinv, s_all, dog, "dn_scan_bwd")
    dqkv_pre, dconv = _dn_prep_bwd(proj, p["dn_conv_w"], dqkv, "dn_prep_bwd")
    dba, dal, ddt = _dn_gate_bwd(ba, a_log, dt_bias, dbeta, dgdec, "dn_gate_bwd")
    W3 = 3 * DN_HEADS * DN_HEAD_DIM
    dw_qkv = _mm(hn0, dqkv_pre, "tn", "dn_wgrad_qkv")
    dw_z = _mm(hn0, dz0, "tn", "dn_wgrad_z")
    dw_ba = _mm(hn0, dba, "tn", "dn_wgrad_ba")
    grads["dn_w_in"] = jnp.concatenate(
        [dw_qkv, dw_z, dw_ba[:, :DN_HEADS], dw_ba[:, LANES:LANES + DN_HEADS]], axis=1)
    tell("dn")
    dh0 = _mm(dqkv_pre, dnw["dn_wqkvz"][:, :W3], "nt", "dn_dgrad_qkv")
    dh0 = _mm(dz0, dnw["dn_wqkvz"][:, W3:], "nt", "dn_dgrad_z", add=dh0)
    dx, dng[0][2] = _dgrad_prenorm_bwd(dba, dnw["dn_wba"], dh0, dx, x1, gain(0, 2), "dn_dgrad_ba")
    grads["dn_conv_w"] = dconv[None]
    grads["dn_a_log"] = dal[:, :DN_HEADS]
    grads["dn_dt_bias"] = ddt[:, :DN_HEADS]
    grads["dn_norm_g"] = dnng
    dx = ffn_b(dx, sv_f00, 0, 0, "00", last=True)
    return loss_part, dx, grads


def _mesh_pos():
    return lax.axis_index("x"), lax.axis_index("y"), lax.axis_index("c")


def _other_chips(x, y):
    return [(1 - x, y), (x, 1 - y), (1 - x, 1 - y)]


def _allgather_chips(arrs, name):
    n = len(arrs)

    def body(*refs):
        ins, outs = refs[:n], refs[n:2 * n]
        ici_send, ici_recv, d2d_send, d2d_recv = refs[2 * n:]
        x, y, c = _mesh_pos()
        me = 2 * x + y
        chips = _other_chips(x, y)
        sibling = (x, y, 1 - c)

        def ici(i, j, k):
            cx, cy = chips[j]
            return pltpu.make_async_remote_copy(src_ref=ins[i].at[c], dst_ref=outs[i].at[k, c], send_sem=ici_send.at[3 * i + j],
                                                recv_sem=ici_recv.at[3 * i + j], device_id=(cx, cy, c), device_id_type=MESH)

        def d2d(i, j, h):
            cx, cy = chips[j]
            slot = outs[i].at[2 * cx + cy, h]
            return pltpu.make_async_remote_copy(src_ref=slot, dst_ref=slot, send_sem=d2d_send.at[3 * i + j],
                                                recv_sem=d2d_recv.at[3 * i + j], device_id=sibling, device_id_type=MESH)

        sends = [ici(i, j, me) for i in range(n) for j in range(3)]
        for cp in sends:
            cp.start()
        for i in range(n):
            for j, (cx, cy) in enumerate(chips):
                ici(i, j, 2 * cx + cy).wait_recv()
                fwd = d2d(i, j, c)
                fwd.start()
                sends.append(fwd)
        for i in range(n):
            for j in range(3):
                d2d(i, j, 1 - c).wait_recv()
        for cp in sends:
            cp.wait_send()

    return pl.pallas_call(
        body, name=name, in_specs=[ANY] * n, out_specs=[ANY] * n,
        out_shape=[jax.ShapeDtypeStruct((N_CHIPS,) + a.shape, a.dtype) for a in arrs],
        scratch_shapes=[pltpu.SemaphoreType.DMA((3 * n,))] * 4,
    )(*arrs)


HBM = pl.BlockSpec(memory_space=pltpu.HBM)
SEM = pl.BlockSpec(memory_space=pltpu.SEMAPHORE)
TOKEN = jax.ShapeDtypeStruct((SUBLANES, LANES), F32)


_PEERS = {"gather": 3, "scatter": 3, "swap": 1, "all": N_DEV - 1}


def _land_shape(kind, shape):
    if kind == "gather":
        return (N_CHIPS,) + shape
    if kind == "all":
        return (N_DEV,) + shape
    return (N_CHIPS,) + shape[2:] if kind == "swap" else shape


def _peer_copies(kind, flags, src_refs, land_refs, send_sems, recv_sems, receiving):
    x, y, c = _mesh_pos()
    me4, me8 = 2 * x + y, 4 * x + 2 * y + c
    np_ = _PEERS[kind]
    cps = []
    for i, (src, land) in enumerate(zip(src_refs, land_refs)):
        if kind == "swap":
            half = src.at[1 - c] if flags[i] else src.at[:, 1 - c]
            plan = [((x, y, 1 - c), half, land)]
        elif kind == "all":
            masks = [(mx, my, mc) for mx in (0, 1) for my in (0, 1) for mc in (0, 1)][1:]
            peers = [(jnp.where(mx, 1 - x, x), jnp.where(my, 1 - y, y), jnp.where(mc, 1 - c, c)) for mx, my, mc in masks]
            plan = [(p, src, land.at[4 * p[0] + 2 * p[1] + p[2] if receiving else me8]) for p in peers]
        else:
            plan = []
            for cx, cy in _other_chips(x, y):
                k = 2 * cx + cy
                s = src.at[me4 if receiving else k] if kind == "scatter" else src
                plan.append(((cx, cy, c), s, land.at[k if receiving else me4]))
        for j, (peer, s, d) in enumerate(plan):
            cps.append(pltpu.make_async_remote_copy(src_ref=s, dst_ref=d, send_sem=send_sems.at[np_ * i + j],
                                                    recv_sem=recv_sems.at[np_ * i + j], device_id=peer, device_id_type=MESH))
    return cps


def _copies_start(kind, srcs, after, name, flags=None):
    n = len(srcs)
    ns = _PEERS[kind] * n
    lands = [lax.empty(_land_shape(kind, s.shape), s.dtype) for s in srcs]
    after = [] if after is None else [after]

    def body(*refs):
        src_refs, land_refs = refs[:n], refs[n:2 * n]
        send_sems, recv_sems = refs[2 * n + len(after)], refs[2 * n + len(after) + 1]
        token = refs[-1]
        for cp in _peer_copies(kind, flags, src_refs, land_refs, send_sems, recv_sems, False):
            cp.start()
        token[...] = jnp.zeros_like(token)

    outs = pl.pallas_call(
        body, name=name,
        in_specs=[HBM] * (2 * n) + [ANY] * len(after),
        out_specs=(SEM, SEM) + (HBM,) * (2 * n) + (pl.BlockSpec(memory_space=pltpu.VMEM),),
        out_shape=(pltpu.SemaphoreType.DMA((ns,)), pltpu.SemaphoreType.DMA((ns,)))
        + tuple(pltpu.HBM(a.shape, a.dtype) for a in list(srcs) + lands) + (TOKEN,),
        input_output_aliases={i: 2 + i for i in range(2 * n)},
        compiler_params=pltpu.CompilerParams(has_side_effects=pltpu.SideEffectType.DATAFLOW_SIDE_EFFECTING),
    )(*[pltpu.with_memory_space_constraint(a, pltpu.HBM) for a in list(srcs) + lands], *after)
    return dict(sems=outs[:2], srcs=outs[2:2 + n], lands=outs[2 + n:2 + 2 * n], token=outs[-1], kind=kind, flags=flags)


def _copies_wait(started, after, name):
    n = len(started["srcs"])
    kind, flags = started["kind"], started["flags"]
    after = list(after) if isinstance(after, (list, tuple)) else [after]

    def body(*refs):
        src_refs, land_refs = refs[:n], refs[n:2 * n]
        send_sems, recv_sems = refs[2 * n], refs[2 * n + 1]
        for cp in _peer_copies(kind, flags, src_refs, land_refs, send_sems, recv_sems, True):
            cp.wait_send()
            cp.wait_recv()

    outs = pl.pallas_call(
        body, name=name,
        in_specs=[HBM] * (2 * n) + [SEM, SEM] + [ANY] * len(after),
        out_specs=(HBM,) * (2 * n),
        out_shape=tuple(pltpu.HBM(a.shape, a.dtype) for a in list(started["srcs"]) + list(started["lands"])),
        input_output_aliases={i: i for i in range(2 * n)},
        compiler_params=pltpu.CompilerParams(has_side_effects=pltpu.SideEffectType.DATAFLOW_SIDE_EFFECTING),
    )(*started["srcs"], *started["lands"], *started["sems"], *after)
    return outs[:n], outs[n:]


def _swap_whole(arrs, name):
    n = len(arrs)

    def body(*refs):
        ins, outs = refs[:n], refs[n:2 * n]
        send_sems, recv_sems = refs[2 * n:]
        x, y, c = _mesh_pos()
        cps = [pltpu.make_async_remote_copy(src_ref=ins[i], dst_ref=outs[i], send_sem=send_sems.at[i],
                                            recv_sem=recv_sems.at[i], device_id=(x, y, 1 - c), device_id_type=MESH)
               for i in range(n)]
        for cp in cps:
            cp.start()
        for cp in cps:
            cp.wait()

    return pl.pallas_call(
        body, name=name, in_specs=[ANY] * n, out_specs=[ANY] * n,
        out_shape=[jax.ShapeDtypeStruct(a.shape, a.dtype) for a in arrs],
        scratch_shapes=[pltpu.SemaphoreType.DMA((n,)), pltpu.SemaphoreType.DMA((n,))],
    )(*arrs)


def _as_rows(a, lead):
    shp = a.shape
    rows = 1
    for s in shp[lead:-1]:
        rows *= s
    return a.reshape(shp[:lead] + (rows, shp[-1]))


def _row_tile(rows, cols, n_bufs):
    budget = (24 * 1024 * 1024) // (n_bufs * 2 * 4 * cols)
    return _pick(rows, max(2 * SUBLANES, budget), 2 * SUBLANES)


def _sum_devices(own, got, dev, name):
    n, rows, cols = got.shape
    tr = _row_tile(rows, cols, n + 2)

    def body(dev_ref, own_ref, got_ref, o_ref):
        mine = own_ref[...]
        acc = jnp.where(dev_ref[0] == 0, mine, got_ref[0])
        for k in range(1, n):
            acc = acc + jnp.where(dev_ref[0] == k, mine, got_ref[k])
        o_ref[...] = acc

    return pl.pallas_call(
        body, name=name,
        grid_spec=pltpu.PrefetchScalarGridSpec(
            num_scalar_prefetch=1, grid=(rows // tr,),
            in_specs=[pl.BlockSpec((tr, cols), lambda i, d: (i, 0)), pl.BlockSpec((n, tr, cols), lambda i, d: (0, i, 0))],
            out_specs=pl.BlockSpec((tr, cols), lambda i, d: (i, 0))),
        out_shape=jax.ShapeDtypeStruct((rows, cols), F32), compiler_params=_cp("parallel"),
    )(_scalar(dev), own, got)


def _scalar(i):
    return jnp.reshape(i, (1,)).astype(jnp.int32)


def _add_own_half(g, other, c, half_first, name):
    _, rows, cols = other.shape
    tr = _pick(rows, 256, 2 * SUBLANES)

    def body(c_ref, g_ref, o_ref, out_ref):
        out_ref[0] = (g_ref[0, 0] + o_ref[0]).astype(out_ref.dtype)

    if half_first:
        g_map = lambda k, i, c_ref: (c_ref[0], k, i, 0)
    else:
        g_map = lambda k, i, c_ref: (k, c_ref[0], i, 0)
    flat = pl.BlockSpec((1, tr, cols), lambda k, i, c_ref: (k, i, 0))
    return pl.pallas_call(
        body, name=name,
        grid_spec=pltpu.PrefetchScalarGridSpec(
            num_scalar_prefetch=1, grid=(N_CHIPS, rows // tr),
            in_specs=[pl.BlockSpec((1, 1, tr, cols), g_map), flat], out_specs=flat),
        out_shape=jax.ShapeDtypeStruct(other.shape, COMM_DTYPE), compiler_params=_cp("parallel", "parallel"),
    )(_scalar(c), g, other)


def _sum_chips(own, got, chip, name, transpose=False):
    _, rows, cols = own.shape
    tr = rows if transpose else _pick(rows, 256, 2 * SUBLANES)

    def body(chip_ref, p_ref, b_ref, o_ref):
        mine = p_ref[0].astype(F32)
        acc = jnp.where(chip_ref[0] == 0, mine, b_ref[0].astype(F32))
        for k in range(1, N_CHIPS):
            acc = acc + jnp.where(chip_ref[0] == k, mine, b_ref[k].astype(F32))
        o_ref[...] = acc.T if transpose else acc

    if transpose:
        out_spec, out_shape = pl.BlockSpec((cols, rows), lambda i, k_ref: (0, 0)), (cols, rows)
    else:
        out_spec, out_shape = pl.BlockSpec((tr, cols), lambda i, k_ref: (i, 0)), (rows, cols)
    return pl.pallas_call(
        body, name=name,
        grid_spec=pltpu.PrefetchScalarGridSpec(
            num_scalar_prefetch=1, grid=(rows // tr,),
            in_specs=[pl.BlockSpec((1, tr, cols), lambda i, k_ref: (k_ref[0], i, 0)),
                      pl.BlockSpec((N_CHIPS, tr, cols), lambda i, k_ref: (0, i, 0))],
            out_specs=out_spec),
        out_shape=jax.ShapeDtypeStruct(out_shape, F32), compiler_params=_cp("parallel"),
    )(_scalar(chip), own, got)


def _adam_math(w, g, m, v):
    nm = ADAM_B1 * m + (1.0 - ADAM_B1) * g
    nv = ADAM_B2 * v + (1.0 - ADAM_B2) * (g * g)
    m_hat = nm / (1.0 - ADAM_B1 ** ADAM_STEP)
    v_hat = nv / (1.0 - ADAM_B2 ** ADAM_STEP)
    return -ADAM_LR * (m_hat / (jnp.sqrt(v_hat) + ADAM_EPS) + ADAM_WD * w), nm, nv


def _adamw_pieces(w, m, v, mine, theirs, c, kind, name):
    shape = w.shape
    P = len(mine)
    ws, ms, vs = (t.reshape((P, -1, t.shape[-1])) for t in (w, m, v))
    _, R, C = ws.shape
    if kind == "rows":
        tr = _pick(R // 2, 512, SUBLANES)
    else:
        tr = _pick(R, 256 if kind in ("lo", "hi") else 512, SUBLANES)
    nt = R // tr
    nh = nt // 2

    def body(c_ref, w_ref, m_ref, v_ref, *refs):
        mine_refs, theirs_refs = refs[:P], refs[P:2 * P]
        g_ref, d_ref, nm_ref, nv_ref = refs[2 * P:]
        p, i, core = pl.program_id(0), pl.program_id(1), c_ref[0]

        def pick(refs_):
            out = refs_[0][...]
            for q in range(1, P):
                out = jnp.where(p == q, refs_[q][...], out)
            return out

        a, b = pick(mine_refs), pick(theirs_refs)
        if kind == "cols":
            gv = jnp.where(core == 0, jnp.concatenate([a, b], axis=1), jnp.concatenate([b, a], axis=1))
        else:
            own = {"lo": core == 0, "hi": core == 1, "rows": (i >= nh) == (core == 1)}[kind]
            gv = jnp.where(own, a, b)
        g_ref[0] = gv
        d_ref[0], nm_ref[0], nv_ref[0] = _adam_math(w_ref[0], gv, m_ref[0], v_ref[0])

    def piece_spec(q):
        tile = (lambda i: i - jnp.where(i >= nh, nh, 0)) if kind == "rows" else (lambda i: i)
        return pl.BlockSpec((tr, mine[q].shape[1]), lambda p, i, c_ref: (jnp.where(p == q, tile(i), 0), 0))

    full = pl.BlockSpec((1, tr, C), lambda p, i, c_ref: (p, i, 0))
    outs = pl.pallas_call(
        body, name=name,
        grid_spec=pltpu.PrefetchScalarGridSpec(num_scalar_prefetch=1, grid=(P, nt),
                                               in_specs=[full] * 3 + [piece_spec(q) for q in range(P)] * 2,
                                               out_specs=[full] * 4),
        out_shape=[jax.ShapeDtypeStruct((P, R, C), F32)] * 4, compiler_params=_cp("parallel", "arbitrary"),
    )(_scalar(c), ws, ms, vs, *mine, *theirs)
    return tuple(o.reshape(shape) for o in outs)


def _adamw(w, g, m, v, name):
    shape = w.shape
    ws, gs, ms, vs = (_as_rows(t, 0) for t in (w, g, m, v))
    rows, cols = ws.shape
    tr = _row_tile(rows, cols, 7)

    def body(w_ref, g_ref, m_ref, v_ref, d_ref, nm_ref, nv_ref):
        d_ref[...], nm_ref[...], nv_ref[...] = _adam_math(w_ref[...], g_ref[...], m_ref[...], v_ref[...])

    spec = pl.BlockSpec((tr, cols), lambda i: (i, 0))
    outs = pl.pallas_call(body, name=name, grid=(rows // tr,), in_specs=[spec] * 4, out_specs=[spec] * 3,
                          out_shape=[jax.ShapeDtypeStruct((rows, cols), F32)] * 3, compiler_params=_cp("parallel"))(ws, gs, ms, vs)
    return tuple(o.reshape(shape) for o in outs)


_BIG = ["ffn_w_gate", "ffn_w_up", "ffn_w_down", "dn_w_in", "dn_w_out", "sg_w_in", "sg_w_out"]
_SMALL_SHARDED = ["norm_g", "dn_conv_w", "sg_b_in", "sg_ln_g", "sg_ln_b"]
_SMALL_REPL = ["dn_a_log", "dn_dt_bias", "dn_norm_g", "sg_w_s", "sg_b_s"]
_WEIGHTS = ["norm_g", "ffn_w_gate", "ffn_w_up", "ffn_w_down", "dn_w_in", "dn_conv_w", "dn_a_log", "dn_dt_bias",
            "dn_norm_g", "dn_w_out", "sg_w_in", "sg_b_in", "sg_ln_g", "sg_ln_b", "sg_w_s", "sg_b_s", "sg_w_out"]
PACK_COLS = 1024


def _pack(arrs):
    flat = jnp.concatenate([a.reshape(-1) for a in arrs])
    pad = (-flat.shape[0]) % (SUBLANES * PACK_COLS)
    return jnp.pad(flat, (0, pad)).reshape(-1, PACK_COLS)


def _unpack(buf, shapes):
    flat = buf.reshape(-1)
    out, off = [], 0
    for s in shapes:
        n = math.prod(s)
        out.append(flat[off:off + n].reshape(s))
        off += n
    return out


def _as_halves(a):
    if a.shape[0] == 2:
        return a
    if a.shape[0] == 1:
        return a.reshape((2, a.shape[1] // 2) + a.shape[2:])
    return a.reshape((2, a.shape[0] // 2) + a.shape[1:])


def _with_own(gathered, own, chip):
    g = gathered.reshape((N_CHIPS,) + own.shape)
    return [jnp.where(chip == k, own, g[k]) for k in range(N_CHIPS)]


def _cat_shards(g, axis):
    return jnp.concatenate(list(g), axis=axis)


_GROUP_ORDER = ["ffn00", "dn", "ffn01", "ffn10", "sg", "ffn11"]


def _weight_groups(w):
    cast = {k: _mx(w[k]) for k in _BIG}
    groups = {"ffn%d%d" % (i, j): [cast["ffn_w_gate"][i, j].T, cast["ffn_w_up"][i, j].T, cast["ffn_w_down"][i, j]]
              for i, j in [(0, 0), (0, 1), (1, 0), (1, 1)]}
    groups["dn"] = [cast["dn_w_in"][0], cast["dn_w_out"][0]]
    groups["sg"] = [cast["sg_w_in"][0], cast["sg_w_out"][0]]
    return groups


def _ffn_weights(chip, own, gathered):
    pairs = [(a, g.reshape((N_CHIPS,) + a.shape)) for a, g in zip(own, gathered)]
    return {"chip": chip, "gate": pairs[0], "up": pairs[1], "down": pairs[2]}


def _group_matrices(group, shards):
    if group == "sg":
        return {"sg_win": _cat_shards(shards[0], 1), "sg_wout": _cat_shards(shards[1], 0)}
    dn_full = _cat_shards(shards[0], 1)
    W4 = 4 * DN_HEADS * DN_HEAD_DIM
    wba = jnp.zeros((D_MODEL, 2 * LANES), dn_full.dtype)
    wba = wba.at[:, :DN_HEADS].set(dn_full[:, W4:W4 + DN_HEADS])
    wba = wba.at[:, LANES:LANES + DN_HEADS].set(dn_full[:, W4 + DN_HEADS:])
    return {"dn_wqkvz": dn_full[:, :W4], "dn_wba": wba, "dn_wout": _cat_shards(shards[1], 0)}


def _split_cols(a, n):
    w = a.shape[-1] // n
    return [a[..., k * w:(k + 1) * w] for k in range(n)]


def _split_rows(a, n):
    h = a.shape[-2] // n
    return [a[..., k * h:(k + 1) * h, :] for k in range(n)]


_IJ = [(0, 0), (0, 1), (1, 0), (1, 1)]


def _group_grads(group, grads):
    def rows_by_chip(a):
        return a.reshape(N_CHIPS, 2, a.shape[0] // (2 * N_CHIPS), a.shape[1])

    if group.startswith("ffn"):
        tag = group[3:]
        t = grads["wguT" + tag]
        return (["wguT" + tag, "wd" + tag],
                [t.reshape(2, N_CHIPS, t.shape[0] // (2 * N_CHIPS), t.shape[1]), rows_by_chip(grads["wd" + tag])], [True, False])
    if group == "sg":
        return ["sg_w_inT", "sg_w_out"], [rows_by_chip(grads["sg_w_inT"]), rows_by_chip(grads["sg_w_out"])], [False, False]
    dn_in = jnp.stack([jnp.stack(_split_cols(hf, N_CHIPS)) for hf in _split_rows(grads["dn_w_in"], 2)])
    return ["dn_w_in", "dn_w_out"], [dn_in, rows_by_chip(grads["dn_w_out"])], [True, False]


_SHARD_PIECES = {
    "ffn_w_gate": (["wguT%d%d" % ij for ij in _IJ], "lo"),
    "ffn_w_up": (["wguT%d%d" % ij for ij in _IJ], "hi"),
    "ffn_w_down": (["wd%d%d" % ij for ij in _IJ], "rows"),
    "dn_w_in": (["dn_w_in"], "rows"),
    "dn_w_out": (["dn_w_out"], "rows"),
    "sg_w_in": (["sg_w_inT"], "cols"),
    "sg_w_out": (["sg_w_out"], "rows"),
}


def kernel(x, norm_g, ffn_w_gate, ffn_w_up, ffn_w_down, dn_w_in, dn_conv_w, dn_a_log, dn_dt_bias, dn_norm_g, dn_w_out, sg_w_in, sg_b_in, sg_ln_g, sg_ln_b, sg_w_s, sg_b_s, sg_w_out, loss_target, m_norm_g, m_ffn_w_gate, m_ffn_w_up, m_ffn_w_down, m_dn_w_in, m_dn_conv_w, m_dn_a_log, m_dn_dt_bias, m_dn_norm_g, m_dn_w_out, m_sg_w_in, m_sg_b_in, m_sg_ln_g, m_sg_ln_b, m_sg_w_s, m_sg_b_s, m_sg_w_out, v_norm_g, v_ffn_w_gate, v_ffn_w_up, v_ffn_w_down, v_dn_w_in, v_dn_conv_w, v_dn_a_log, v_dn_dt_bias, v_dn_norm_g, v_dn_w_out, v_sg_w_in, v_sg_b_in, v_sg_ln_g, v_sg_ln_b, v_sg_w_s, v_sg_b_s, v_sg_w_out):
    args = dict(locals())
    w = {k: args[k] for k in _WEIGHTS}
    mom = {k: args["m_" + k] for k in _WEIGHTS}
    var = {k: args["v_" + k] for k in _WEIGHTS}
    cx, cy, cc = _mesh_pos()
    chip = 2 * cx + cy

    small_shapes = [w[k].shape for k in _SMALL_SHARDED]
    groups = _weight_groups(w)
    own = groups[_GROUP_ORDER[0]] + [_pack([w[k] for k in _SMALL_SHARDED])]
    first = _allgather_chips([_as_halves(a) for a in own], "gather_first")
    started, after = {}, first[0]
    for g in _GROUP_ORDER[1:]:
        started[g] = _copies_start("gather", groups[g], after, "gather_start_" + g)
        after = started[g]["token"]
    small_k = [_unpack(pack, small_shapes) for pack in _with_own(first[-1], own[-1], chip)]
    p = {name: jnp.concatenate([small_k[k][i] for k in range(N_CHIPS)], axis=-1) for i, name in enumerate(_SMALL_SHARDED)}
    p = {k: (v if k == "norm_g" else v[0]) for k, v in p.items()}
    p["norm_g"] = p["norm_g"] + after[0, 0]
    for k in _SMALL_REPL:
        p[k] = w[k][0]

    def weights_for(group, after):
        if group == _GROUP_ORDER[0]:
            return _ffn_weights(chip, own[:-1], first[:-1])
        srcs, lands = _copies_wait(started[group], after, "gather_wait_" + group)
        if group.startswith("ffn"):
            return _ffn_weights(chip, srcs, lands)
        return _group_matrices(group, [_with_own(l, a, chip) for l, a in zip(lands, srcs)])

    mine, theirs, to_core, to_chips = {}, {}, [], []

    def send_to_chips(after):
        group, names, flags, swap = to_core.pop(0)
        halves, got = _copies_wait(swap, after, "swap_wait_" + group)
        pair_sum = [_add_own_half(h, o, cc, hf, "pair_sum_" + n) for n, h, o, hf in zip(names, halves, got, flags)]
        scatter = _copies_start("scatter", pair_sum, got[0], "reduce_start_" + group)
        to_chips.append((group, names, scatter))
        return scatter["token"]

    def finish(after):
        group, names, scatter = to_chips.pop(0)
        pair_sum, got = _copies_wait(scatter, after, "reduce_wait_" + group)
        half_sum = [_sum_chips(a, b, chip, "chip_sum_" + n, transpose=n == "sg_w_inT")
                    for n, a, b in zip(names, pair_sum, got)]
        other = _swap_whole(half_sum, "gather_core_pair_" + group)
        mine.update(zip(names, half_sum))
        theirs.update(zip(names, other))

    def grads_ready(group, grads):
        names, halves, flags = _group_grads(group, grads)
        swap = _copies_start("swap", halves, None, "swap_start_" + group, flags)
        token = swap["token"]
        if to_core:
            token = send_to_chips(token)
            if len(to_chips) > 1:
                finish(token)
        to_core.append((group, names, flags, swap))
        return token[0, 0]

    small_names = _SMALL_SHARDED + _SMALL_REPL
    small = {}

    def small_ready(grads, loss_part):
        parts = [grads[k] for k in small_names]
        small["shapes"] = [g.shape for g in parts] + [(1,)]
        pack = _pack(parts + [loss_part[0, :1]])
        small["exchange"] = _copies_start("all", [pack], None, "small_start")
        return small["exchange"]["token"]

    loss_part, grad_x, grads = _local_step(x[0], loss_target[0], p, weights_for, grads_ready, small_ready)
    token = send_to_chips(to_core[0][3]["token"])
    finish(token)
    (pack,), (packs,) = _copies_wait(small["exchange"], list(theirs.values()), "small_wait")
    summed = _sum_devices(pack, packs, 4 * cx + 2 * cy + cc, "small_sum")
    parts = _unpack(summed, small["shapes"])
    loss = parts[-1][0]
    grad = {}
    for i, k in enumerate(small_names):
        g = parts[i]
        if k in _SMALL_SHARDED:
            n = w[k].shape[-1]
            g = lax.dynamic_slice_in_dim(g, chip * n, n, axis=g.ndim - 1)
        grad[k] = g

    delta, new_m, new_v = {}, {}, {}

    def update(keys):
        for k in keys:
            names, kind = _SHARD_PIECES[k]
            turn = (lambda a: jnp.swapaxes(a, -1, -2)) if names[0].startswith("wguT") else (lambda a: a)
            outs = _adamw_pieces(turn(w[k]), turn(mom[k]), turn(var[k]), [mine[n] for n in names], [theirs[n] for n in names],
                                 cc, kind, "adamw_" + k)
            grad[k], delta[k], new_m[k], new_v[k] = (turn(o) for o in outs)

    shapes = [w[k].shape for k in small_names]
    d, nm, nv = _adamw(_pack([w[k] for k in small_names]), _pack([grad[k] for k in small_names]),
                       _pack([mom[k] for k in small_names]), _pack([var[k] for k in small_names]), "adamw_small")
    for k, a, b, c_ in zip(small_names, _unpack(d, shapes), _unpack(nm, shapes), _unpack(nv, shapes)):
        delta[k], new_m[k], new_v[k] = a, b, c_
    mixers = [k for k in _BIG if not k.startswith("ffn")]
    update(mixers)
    finish([d] + [delta[k] for k in mixers] + list(theirs.values()))
    update([k for k in _BIG if k.startswith("ffn")])

    return (loss, grad_x[None], *[grad[k] for k in _WEIGHTS], *[delta[k] for k in _WEIGHTS],
            *[new_m[k] for k in _WEIGHTS], *[new_v[k] for k in _WEIGHTS])
```

```python
import math

import jax
import jax.numpy as jnp
from jax import lax
from jax.experimental import pallas as pl
from jax.experimental.pallas import tpu as pltpu

F32 = jnp.float32
MXU_DTYPE = jnp.bfloat16
COMM_DTYPE = jnp.bfloat16
HI = lax.Precision.HIGHEST
TRI_PREC = lax.Precision.HIGH

D_MODEL = 1024
RMS_EPS = 1e-6
LN_EPS = 1e-5
L2_EPS = 1e-6
DN_HEADS = 8
DN_HEAD_DIM = 128
DN_CONV = 4
DN_CHUNK = 64
SG_WIDTH = 2048
SG_GROUPS = 8
SG_CHUNK = 128
SG_GROUP_W = SG_WIDTH // SG_GROUPS
N_CHIPS = 4
N_DEV = 8
LANES = 128
SUBLANES = 8
VMEM_LIMIT = 56 * 1024 * 1024

ADAM_LR = 0.001
ADAM_B1 = 0.9
ADAM_B2 = 0.999
ADAM_EPS = 1e-08
ADAM_WD = 0.01
ADAM_STEP = 10

MESH = pl.DeviceIdType.MESH
ANY = pl.BlockSpec(memory_space=pl.ANY)


def _cp(*sem):
    return pltpu.CompilerParams(dimension_semantics=sem, vmem_limit_bytes=VMEM_LIMIT)


def _pick(n, pref, mult=LANES):
    best = None
    d = mult
    while d <= min(n, pref):
        if n % d == 0:
            best = d
        d += mult
    return best if best is not None else n


def _full(shape):
    nd = len(shape)
    return pl.BlockSpec(shape, lambda *_: (0,) * nd)


def _sigmoid(x):
    return 1.0 / (1.0 + jnp.exp(-x))


def _dot(a, b, dims, prec=None):
    return lax.dot_general(a, b, (dims, ((), ())), preferred_element_type=F32, precision=prec)


NN = ((1,), (0,))
NT = ((1,), (1,))
TN = ((0,), (0,))


def _mx(a):
    return a.astype(MXU_DTYPE)


def _rms_stat(x):
    return lax.rsqrt(jnp.mean(x * x, axis=-1, keepdims=True) + RMS_EPS)


def _rms_bwd(x, r, g, dy):
    xh = x * r
    dxh = dy * g
    dx = r * (dxh - xh * jnp.mean(dxh * xh, axis=-1, keepdims=True))
    return dx, jnp.sum(dy * xh, axis=0, keepdims=True)


def _mm(a, b, mode, name, out_dtype=F32, add=None, after=None):
    if mode == "tn":
        K, M = a.shape
        N = b.shape[1]
    elif mode == "nt":
        M, K = a.shape
        N = b.shape[0]
    else:
        M, K = a.shape
        N = b.shape[1]
    tn = _pick(N, 1024)
    if mode == "tn":
        tm = _pick(M, 1024 if tn <= 512 else 1408)
        tk = _pick(K, 1024, SUBLANES)
    else:
        tm = _pick(M, max(512, min(2048, (1024 * 1024) // tn)), SUBLANES)
        tk = _pick(K, 2048)
    nk = K // tk
    grid = (N // tn, M // tm, nk)
    if mode == "nn":
        a_spec = pl.BlockSpec((tm, tk), lambda j, i, k: (i, k))
        b_spec = pl.BlockSpec((tk, tn), lambda j, i, k: (k, j))
        dims = NN
    elif mode == "nt":
        a_spec = pl.BlockSpec((tm, tk), lambda j, i, k: (i, k))
        b_spec = pl.BlockSpec((tn, tk), lambda j, i, k: (j, k))
        dims = NT
    else:
        a_spec = pl.BlockSpec((tk, tm), lambda j, i, k: (k, i))
        b_spec = pl.BlockSpec((tk, tn), lambda j, i, k: (k, j))
        dims = TN
    o_spec = pl.BlockSpec((tm, tn), lambda j, i, k: (i, j))
    has_add = add is not None

    def body(*refs):
        a_ref, b_ref = refs[:2]
        add_ref = refs[2] if has_add else None
        o_ref, acc = refs[-2:]
        k = pl.program_id(2)

        @pl.when(k == 0)
        def _():
            acc[...] = add_ref[...] if has_add else jnp.zeros_like(acc)

        acc[...] += _dot(a_ref[...], b_ref[...], dims)

        @pl.when(k == nk - 1)
        def _():
            o_ref[...] = acc[...].astype(o_ref.dtype)

    ins = [a, b] + ([add] if has_add else []) + ([after] if after is not None else [])
    specs = [a_spec, b_spec] + ([o_spec] if has_add else []) + ([ANY] if after is not None else [])
    return pl.pallas_call(
        body, name=name, grid=grid, in_specs=specs, out_specs=o_spec,
        out_shape=jax.ShapeDtypeStruct((M, N), out_dtype),
        scratch_shapes=[pltpu.VMEM((tm, tn), F32)],
        compiler_params=_cp("parallel", "parallel", "arbitrary"),
    )(*ins)


def _ffn_weight_operands(wt):
    return [_scalar(wt["chip"])] , [wt["gate"][0], wt["gate"][1], wt["up"][0], wt["up"][1], wt["down"][0], wt["down"][1]]


def _load_ffn_weights(chip_ref, shard_refs, wgu_v, wd_v, sem):
    fs = wd_v.shape[0] // N_CHIPS

    @pl.when(pl.program_id(0) == 0)
    def _():
        me = chip_ref[0]
        waits = []
        for t, (dst, base) in enumerate([(wgu_v, 0), (wgu_v, wd_v.shape[0]), (wd_v, 0)]):
            own, gathered = shard_refs[2 * t], shard_refs[2 * t + 1]
            for k in range(N_CHIPS):
                slot = dst.at[pl.ds(base + k * fs, fs), :]
                s = sem.at[t * N_CHIPS + k]

                @pl.when(me == k)
                def _(own=own, slot=slot, s=s):
                    pltpu.make_async_copy(own, slot, s).start()

                @pl.when(me != k)
                def _(gathered=gathered, k=k, slot=slot, s=s):
                    pltpu.make_async_copy(gathered.at[k], slot, s).start()

                waits.append(pltpu.make_async_copy(own, slot, s))
        for cp in waits:
            cp.wait()


def _ffn_fwd(x, g0, g1, wt, name, next_gain=None, target=None):
    T, D = x.shape
    F = N_CHIPS * wt["down"][0].shape[0]
    F2 = 2 * F
    tm = _pick(T, 256, SUBLANES)
    prefetch, shards = _ffn_weight_operands(wt)
    extra = [a for a in (next_gain, target) if a is not None]
    n_tail = 1 if next_gain is not None else (2 if target is not None else 0)

    def body(chip_ref, x_ref, g0_ref, g1_ref, *refs):
        extra_ref = refs[0] if extra else None
        refs = refs[len(extra):]
        shard_refs = refs[:6]
        xo_ref, h_ref, gu_ref, y_ref = refs[6:10]
        tail_refs = refs[10:10 + n_tail]
        wgu_v, wd_v, sem = refs[10 + n_tail:]
        _load_ffn_weights(chip_ref, shard_refs, wgu_v, wd_v, sem)
        xv = x_ref[...]
        hb = _mx(xv * _rms_stat(xv) * g0_ref[...])
        h_ref[...] = hb
        gu = _dot(hb, wgu_v[...], NT)
        gu_ref[...] = gu.astype(gu_ref.dtype)
        g = gu[:, :F]
        u = gu[:, F:]
        a = _mx(g * _sigmoid(g) * u)
        y = _dot(a, wd_v[...], NN)
        y_ref[...] = y
        xo = xv + 0.5 * (y * _rms_stat(y) * g1_ref[...])
        xo_ref[...] = xo
        if next_gain is not None:
            tail_refs[0][...] = _mx(xo * _rms_stat(xo) * extra_ref[...])
        if target is not None:
            loss_ref, dy_ref = tail_refs

            @pl.when(pl.program_id(0) == 0)
            def _():
                loss_ref[...] = jnp.zeros_like(loss_ref)

            e = xo - extra_ref[...]
            dy_ref[...] = e * (1.0 / D)
            loss_ref[...] += 0.5 * jnp.sum(jnp.mean(e * e, axis=-1, keepdims=True), axis=0, keepdims=True)

    row = lambda w: pl.BlockSpec((tm, w), lambda i, c: (i, 0))
    one = pl.BlockSpec((1, D), lambda i, c: (0, 0))
    tail_specs, tail_shapes, extra_specs = [], [], []
    if next_gain is not None:
        extra_specs, tail_specs, tail_shapes = [one], [row(D)], [jax.ShapeDtypeStruct((T, D), MXU_DTYPE)]
    if target is not None:
        extra_specs = [row(D)]
        tail_specs = [pl.BlockSpec((SUBLANES, LANES), lambda i, c: (0, 0)), row(D)]
        tail_shapes = [jax.ShapeDtypeStruct((SUBLANES, LANES), F32), jax.ShapeDtypeStruct((T, D), F32)]
    return pl.pallas_call(
        body, name=name,
        grid_spec=pltpu.PrefetchScalarGridSpec(
            num_scalar_prefetch=1, grid=(T // tm,),
            in_specs=[row(D), one, one] + extra_specs + [ANY] * 6,
            out_specs=[row(D), row(D), row(F2), row(D)] + tail_specs,
            scratch_shapes=[pltpu.VMEM((F2, D), MXU_DTYPE), pltpu.VMEM((F, D), MXU_DTYPE),
                            pltpu.SemaphoreType.DMA((3 * N_CHIPS,))]),
        out_shape=[jax.ShapeDtypeStruct((T, D), F32), jax.ShapeDtypeStruct((T, D), MXU_DTYPE),
                   jax.ShapeDtypeStruct((T, F2), MXU_DTYPE), jax.ShapeDtypeStruct((T, D), F32)] + tail_shapes,
        compiler_params=_cp("arbitrary"),
    )(*prefetch, x, g0, g1, *extra, *shards)


FFN_BWD_CHUNK = 2816


def _ffn_bwd(dxo, x, y, gu, g0, g1, wt, name):
    T, D = x.shape
    F2 = gu.shape[1]
    F = F2 // 2
    tm = _pick(T, 256, SUBLANES)
    fc = _pick(F, FFN_BWD_CHUNK)
    prefetch, shards = _ffn_weight_operands(wt)

    def body(chip_ref, dxo_ref, x_ref, y_ref, gu_ref, g0_ref, g1_ref, *refs):
        shard_refs = refs[:6]
        dx_ref, dy_ref, a_ref, dgu_ref, dg0_ref, dg1_ref, wgu_v, wd_v, sem = refs[6:]
        _load_ffn_weights(chip_ref, shard_refs, wgu_v, wd_v, sem)

        @pl.when(pl.program_id(0) == 0)
        def _():
            dg0_ref[...] = jnp.zeros_like(dg0_ref)
            dg1_ref[...] = jnp.zeros_like(dg1_ref)

        dxo_v = dxo_ref[...]
        yv = y_ref[...]
        dy, dg1 = _rms_bwd(yv, _rms_stat(yv), g1_ref[...], 0.5 * dxo_v)
        dg1_ref[...] += dg1
        dyb = _mx(dy)
        dy_ref[...] = dyb
        dh = jnp.zeros((tm, D), F32)
        for c in range(F // fc):
            lo, hi = c * fc, (c + 1) * fc
            da = _dot(dyb, wd_v[lo:hi, :], NT)
            g = gu_ref[:, lo:hi].astype(F32)
            u = gu_ref[:, F + lo:F + hi].astype(F32)
            s = _sigmoid(g)
            sg = g * s
            a_ref[:, lo:hi] = _mx(sg * u)
            dg = _mx(da * u * (s * (1.0 + g * (1.0 - s))))
            du = _mx(da * sg)
            dgu_ref[:, lo:hi] = dg
            dgu_ref[:, F + lo:F + hi] = du
            dh = dh + _dot(dg, wgu_v[lo:hi, :], NN) + _dot(du, wgu_v[F + lo:F + hi, :], NN)
        xv = x_ref[...]
        dx, dg0 = _rms_bwd(xv, _rms_stat(xv), g0_ref[...], dh)
        dg0_ref[...] += dg0
        dx_ref[...] = dxo_v + dx

    row = lambda w: pl.BlockSpec((tm, w), lambda i, c: (i, 0))
    one = pl.BlockSpec((1, D), lambda i, c: (0, 0))
    return pl.pallas_call(
        body, name=name,
        grid_spec=pltpu.PrefetchScalarGridSpec(
            num_scalar_prefetch=1, grid=(T // tm,),
            in_specs=[row(D), row(D), row(D), row(F2), one, one] + [ANY] * 6,
            out_specs=[row(D), row(D), row(F), row(F2), one, one],
            scratch_shapes=[pltpu.VMEM((F2, D), MXU_DTYPE), pltpu.VMEM((F, D), MXU_DTYPE),
                            pltpu.SemaphoreType.DMA((3 * N_CHIPS,))]),
        out_shape=[jax.ShapeDtypeStruct((T, D), F32), jax.ShapeDtypeStruct((T, D), MXU_DTYPE),
                   jax.ShapeDtypeStruct((T, F), MXU_DTYPE), jax.ShapeDtypeStruct((T, F2), MXU_DTYPE),
                   jax.ShapeDtypeStruct((1, D), F32), jax.ShapeDtypeStruct((1, D), F32)],
        compiler_params=_cp("arbitrary"),
    )(*prefetch, dxo, x, y, gu, g0, g1, *shards)


def _out_proj_postnorm(a, b, x, g, name):
    T, K = a.shape
    D = b.shape[1]
    tm = _pick(T, 1024, SUBLANES)

    def body(a_ref, b_ref, x_ref, g_ref, m_ref, o_ref):
        mv = _dot(a_ref[...], b_ref[...], NN)
        m_ref[...] = mv
        o_ref[...] = x_ref[...] + mv * _rms_stat(mv) * g_ref[...]

    row = lambda w: pl.BlockSpec((tm, w), lambda i: (i, 0))
    return pl.pallas_call(body, name=name, grid=(T // tm,),
                          in_specs=[row(K), _full((K, D)), row(D), _full((1, D))], out_specs=[row(D), row(D)],
                          out_shape=[jax.ShapeDtypeStruct((T, D), F32)] * 2, compiler_params=_cp("parallel"))(a, b, x, g)


def _postnorm_bwd_dgrad(dxo, m, g, b, name):
    T, D = m.shape
    K = b.shape[0]
    tm = _pick(T, 1024, SUBLANES)

    def body(dxo_ref, m_ref, g_ref, b_ref, dm_ref, dg_ref, da_ref):
        @pl.when(pl.program_id(0) == 0)
        def _():
            dg_ref[...] = jnp.zeros_like(dg_ref)

        mv = m_ref[...]
        dm, dg = _rms_bwd(mv, _rms_stat(mv), g_ref[...], dxo_ref[...])
        dg_ref[...] += dg
        dmb = _mx(dm)
        dm_ref[...] = dmb
        da_ref[...] = _dot(dmb, b_ref[...], NT)

    row = lambda w: pl.BlockSpec((tm, w), lambda i: (i, 0))
    return pl.pallas_call(body, name=name, grid=(T // tm,), in_specs=[row(D), row(D), _full((1, D)), _full((K, D))],
                          out_specs=[row(D), _full((1, D)), row(K)],
                          out_shape=[jax.ShapeDtypeStruct((T, D), MXU_DTYPE), jax.ShapeDtypeStruct((1, D), F32),
                                     jax.ShapeDtypeStruct((T, K), F32)],
                          compiler_params=_cp("arbitrary"))(dxo, m, g, b)


def _dgrad_prenorm_bwd(a, b, add, dxo, x, g, name):
    T, K = a.shape
    D = b.shape[0]
    tm = _pick(T, 1024, SUBLANES)
    tk = _pick(K, 2048)
    nk = K // tk
    has_add = add is not None

    def body(*refs):
        a_ref, b_ref = refs[:2]
        add_ref = refs[2] if has_add else None
        dxo_ref, x_ref, g_ref, dx_ref, dg_ref, acc = refs[-6:]
        i, k = pl.program_id(0), pl.program_id(1)

        @pl.when((i == 0) & (k == 0))
        def _():
            dg_ref[...] = jnp.zeros_like(dg_ref)

        @pl.when(k == 0)
        def _():
            acc[...] = add_ref[...] if has_add else jnp.zeros_like(acc)

        acc[...] += _dot(a_ref[...], b_ref[...], NT)

        @pl.when(k == nk - 1)
        def _():
            xv = x_ref[...]
            dx, dg = _rms_bwd(xv, _rms_stat(xv), g_ref[...], acc[...])
            dg_ref[...] += dg
            dx_ref[...] = dxo_ref[...] + dx

    row = pl.BlockSpec((tm, D), lambda i, k: (i, 0))
    one = pl.BlockSpec((1, D), lambda i, k: (0, 0))
    ins = [a, b] + ([add] if has_add else []) + [dxo, x, g]
    specs = ([pl.BlockSpec((tm, tk), lambda i, k: (i, k)), pl.BlockSpec((D, tk), lambda i, k: (0, k))]
             + ([row] if has_add else []) + [row, row, one])
    return pl.pallas_call(body, name=name, grid=(T // tm, nk), in_specs=specs, out_specs=[row, one],
                          out_shape=[jax.ShapeDtypeStruct((T, D), F32), jax.ShapeDtypeStruct((1, D), F32)],
                          scratch_shapes=[pltpu.VMEM((tm, D), F32)],
                          compiler_params=_cp("arbitrary", "arbitrary"))(*ins)


DN_ROWS = 512


def _shift_down(prev8, cur, s):
    n = cur.shape[0]
    xx = jnp.concatenate([prev8, cur], axis=0)
    return pltpu.roll(xx, s, 0)[SUBLANES:SUBLANES + n, :]


def _shift_up(cur, next8, s):
    n = cur.shape[0]
    xx = jnp.concatenate([cur, next8], axis=0)
    return pltpu.roll(xx, n + SUBLANES - s, 0)[:n, :]


def _tile_start(r, rows):
    return r * rows if isinstance(r, int) else pl.multiple_of(r * rows, SUBLANES)


def _conv_tile(x_ref, w, r, rows):
    start = _tile_start(r, rows)
    cur = x_ref[pl.ds(start, rows), :]
    if isinstance(r, int):
        prev8 = jnp.zeros((SUBLANES, cur.shape[1]), cur.dtype)
        taps = [_shift_down(prev8, cur, DN_CONV - 1 - j) if j < DN_CONV - 1 else cur for j in range(DN_CONV)]
    else:
        taps = [x_ref[pl.ds(start - (DN_CONV - 1 - j), rows), :] if j < DN_CONV - 1 else cur for j in range(DN_CONV)]
    c = taps[0] * w[0:1, :]
    for j in range(1, DN_CONV):
        c = c + taps[j] * w[j:j + 1, :]
    return c, taps


def _dn_prep_fwd(proj, conv_w, name):
    T = proj.shape[0]
    W = DN_HEADS * DN_HEAD_DIM
    rows = min(DN_ROWS, T)
    n_inner = T // rows
    scale = DN_HEAD_DIM ** -0.5

    def body(x_ref, w_ref, o_ref):
        cb = pl.program_id(0)
        w = w_ref[...]
        is_qk = cb < 2 * DN_HEADS
        post = jnp.where(cb < DN_HEADS, scale, 1.0)

        def step(r, carry):
            c, _ = _conv_tile(x_ref, w, r, rows)
            s = c * _sigmoid(c)
            rinv = lax.rsqrt(jnp.sum(s * s, axis=-1, keepdims=True) + L2_EPS)
            o_ref[pl.ds(_tile_start(r, rows), rows), :] = jnp.where(is_qk, s * rinv * post, s)
            return carry

        step(0, 0)
        lax.fori_loop(1, n_inner, step, 0)

    col = pl.BlockSpec((T, LANES), lambda j: (0, j))
    return pl.pallas_call(body, name=name, grid=(3 * W // LANES,),
                          in_specs=[col, pl.BlockSpec((DN_CONV, LANES), lambda j: (0, j))], out_specs=col,
                          out_shape=jax.ShapeDtypeStruct((T, 3 * W), F32), compiler_params=_cp("parallel"))(proj, conv_w)


def _dn_prep_bwd(proj, conv_w, dqkv, name):
    T = proj.shape[0]
    W = DN_HEADS * DN_HEAD_DIM
    rows = min(DN_ROWS, T)
    n_inner = T // rows
    scale = DN_HEAD_DIM ** -0.5

    def body(x_ref, w_ref, dy_ref, dx_ref, dw_ref, dc_scr):
        cb = pl.program_id(0)
        w = w_ref[...]
        is_qk = cb < 2 * DN_HEADS
        post = jnp.where(cb < DN_HEADS, scale, 1.0)

        def step1(r, dws):
            c, taps = _conv_tile(x_ref, w, r, rows)
            sg = _sigmoid(c)
            s = c * sg
            rinv = lax.rsqrt(jnp.sum(s * s, axis=-1, keepdims=True) + L2_EPS)
            dy = dy_ref[pl.ds(_tile_start(r, rows), rows), :]
            yn = s * rinv
            dyn = dy * post
            ds_qk = rinv * (dyn - yn * jnp.sum(dyn * yn, axis=-1, keepdims=True))
            ds = jnp.where(is_qk, ds_qk, dy)
            dc = ds * (sg * (1.0 + c * (1.0 - sg)))
            dc_scr[pl.ds(_tile_start(r, rows), rows), :] = dc
            return tuple(dws[j] + jnp.sum(dc * taps[j], axis=0, keepdims=True) for j in range(DN_CONV))

        zero = jnp.zeros((1, LANES), F32)
        dws = lax.fori_loop(1, n_inner, step1, step1(0, (zero,) * DN_CONV))
        for j in range(DN_CONV):
            dw_ref[j:j + 1, :] = dws[j]

        def step2(r, carry):
            start = _tile_start(r, rows)
            cur = dc_scr[pl.ds(start, rows), :]
            dx = cur * w[DN_CONV - 1:DN_CONV, :]
            for j in range(DN_CONV - 1):
                s = DN_CONV - 1 - j
                if isinstance(r, int):
                    up = _shift_up(cur, jnp.zeros((SUBLANES, LANES), F32), s)
                else:
                    up = dc_scr[pl.ds(start + s, rows), :]
                dx = dx + up * w[j:j + 1, :]
            dx_ref[pl.ds(start, rows), :] = _mx(dx)
            return carry

        lax.fori_loop(0, n_inner - 1, step2, 0)
        step2(n_inner - 1, 0)

    col = pl.BlockSpec((T, LANES), lambda j: (0, j))
    wspec = pl.BlockSpec((DN_CONV, LANES), lambda j: (0, j))
    return pl.pallas_call(body, name=name, grid=(3 * W // LANES,), in_specs=[col, wspec, col], out_specs=[col, wspec],
                          out_shape=[jax.ShapeDtypeStruct((T, 3 * W), MXU_DTYPE), jax.ShapeDtypeStruct((DN_CONV, 3 * W), F32)],
                          scratch_shapes=[pltpu.VMEM((T, LANES), F32)], compiler_params=_cp("parallel"))(proj, conv_w, dqkv)


def _softplus(x):
    return jnp.maximum(x, 0.0) + jnp.log(1.0 + jnp.exp(-jnp.abs(x)))


def _dn_gate_fwd(ba, a_log, dt_bias, name):
    T = ba.shape[0]
    tm = _pick(T, 1024, SUBLANES)

    def body(ba_ref, al_ref, dt_ref, beta_ref, g_ref):
        beta_ref[...] = _sigmoid(ba_ref[:, :LANES])
        g_ref[...] = -jnp.exp(al_ref[...]) * _softplus(ba_ref[:, LANES:] + dt_ref[...])

    row = lambda w: pl.BlockSpec((tm, w), lambda i: (i, 0))
    return pl.pallas_call(body, name=name, grid=(T // tm,), in_specs=[row(2 * LANES), _full((1, LANES)), _full((1, LANES))],
                          out_specs=[row(LANES), row(LANES)],
                          out_shape=[jax.ShapeDtypeStruct((T, LANES), F32)] * 2, compiler_params=_cp("parallel"))(ba, a_log, dt_bias)


def _dn_gate_bwd(ba, a_log, dt_bias, dbeta, dg, name):
    T = ba.shape[0]
    tm = _pick(T, 1024, SUBLANES)

    def body(ba_ref, al_ref, dt_ref, dbeta_ref, dg_ref, dba_ref, dal_ref, ddt_ref):
        @pl.when(pl.program_id(0) == 0)
        def _():
            dal_ref[...] = jnp.zeros_like(dal_ref)
            ddt_ref[...] = jnp.zeros_like(ddt_ref)

        beta = _sigmoid(ba_ref[:, :LANES])
        dba_ref[:, :LANES] = _mx(dbeta_ref[...] * beta * (1.0 - beta))
        pre = ba_ref[:, LANES:] + dt_ref[...]
        ea = jnp.exp(al_ref[...])
        dgv = dg_ref[...]
        da = dgv * (-ea) * _sigmoid(pre)
        dba_ref[:, LANES:] = _mx(da)
        ddt_ref[...] += jnp.sum(da, axis=0, keepdims=True)
        dal_ref[...] += jnp.sum(dgv * (-ea) * _softplus(pre), axis=0, keepdims=True)

    row = lambda w: pl.BlockSpec((tm, w), lambda i: (i, 0))
    one = _full((1, LANES))
    return pl.pallas_call(body, name=name, grid=(T // tm,), in_specs=[row(2 * LANES), one, one, row(LANES), row(LANES)],
                          out_specs=[row(2 * LANES), one, one],
                          out_shape=[jax.ShapeDtypeStruct((T, 2 * LANES), MXU_DTYPE), jax.ShapeDtypeStruct((1, LANES), F32),
                                     jax.ShapeDtypeStruct((1, LANES), F32)],
                          compiler_params=_cp("arbitrary"))(ba, a_log, dt_bias, dbeta, dg)


def _tri(c, strict):
    i = lax.broadcasted_iota(jnp.int32, (c, c), 0)
    j = lax.broadcasted_iota(jnp.int32, (c, c), 1)
    return (i > j) if strict else (i >= j)


def _inv_unit_lower(ls):
    c = ls[0].shape[0]
    i = lax.broadcasted_iota(jnp.int32, (c, c), 0)
    j = lax.broadcasted_iota(jnp.int32, (c, c), 1)
    eye = jnp.where(i == j, 1.0, 0.0)
    facs = [[eye - l for l in ls]]
    cur = ls
    for _ in range(int(math.log2(c)) - 1):
        cur = [_dot(p, p, NN, TRI_PREC) for p in cur]
        facs.append([eye + p for p in cur])
    while len(facs) > 1:
        nxt = [[_dot(a, b, NN, TRI_PREC) for a, b in zip(facs[t], facs[t + 1])] for t in range(0, len(facs) - 1, 2)]
        if len(facs) % 2:
            nxt.append(facs[-1])
        facs = nxt
    return facs[0]


def _chunk_gates(g_blk):
    c = g_blk.shape[0]
    gcs = _dot(jnp.where(_tri(c, False), 1.0, 0.0), g_blk, NN, HI)
    return gcs, gcs.T


def _head_chunk(h, qh, kh, vh, beta_blk, gcs, gcs_t):
    c = qh.shape[0]
    incl = _tri(c, False)
    gc_col = gcs[:, h:h + 1]
    gc_row = gcs_t[h:h + 1, :]
    gc_last = gcs_t[h:h + 1, c - 1:c]
    dec = jnp.where(incl, jnp.exp(jnp.where(incl, gc_col - gc_row, 0.0)), 0.0)
    gam = jnp.exp(gc_col)
    rr = jnp.exp(gc_last - gc_col)
    gl = jnp.exp(gc_last)
    b = beta_blk[:, h:h + 1]
    kb = kh * b
    vb = vh * b
    both = _dot(jnp.concatenate([_mx(kb), _mx(qh)], axis=0), _mx(kh), NT)
    lmat = jnp.where(_tri(c, True), both[:c] * dec, 0.0)
    pmat = jnp.where(incl, both[c:] * dec, 0.0)
    return dict(dec=dec, gam=gam, rr=rr, gl=gl, b=b, kb=kb, vb=vb, lmat=lmat, pmat=pmat)


def _solve_uw(tinv, q):
    return _dot(tinv, jnp.concatenate([q["vb"], q["kb"] * q["gam"]], axis=1), NN, TRI_PREC)


def _dn_scan_fwd(qkv, beta, g, proj, norm_g, name):
    T = qkv.shape[0]
    C, H, Dh = DN_CHUNK, DN_HEADS, DN_HEAD_DIM
    W = H * Dh
    N = T // C

    def body(q_ref, k_ref, v_ref, beta_ref, g_ref, z_ref, ng_ref, og_ref, o_ref, tinv_ref, s_ref, state):
        @pl.when(pl.program_id(0) == 0)
        def _():
            state[...] = jnp.zeros_like(state)

        ng = ng_ref[...]
        heads = range(H)
        cs = [slice(h * Dh, (h + 1) * Dh) for h in heads]

        def chunk(j, carry):
            rows = pl.ds(pl.multiple_of(j * C, C), C)
            gcs, gcs_t = _chunk_gates(g_ref[rows, :])
            beta_blk = beta_ref[rows, :]
            qs = [_head_chunk(h, q_ref[rows, cs[h]], k_ref[rows, cs[h]], v_ref[rows, cs[h]], beta_blk, gcs, gcs_t)
                  for h in heads]
            tinvs = _inv_unit_lower([q["lmat"] for q in qs])
            for h in heads:
                tinv_ref[h, rows, :] = tinvs[h]
            uws = [_solve_uw(tinvs[h], qs[h]) for h in heads]
            ss = [state[h] for h in heads]
            for h in heads:
                s_ref[j, h] = ss[h]
            sbs = [_mx(s) for s in ss]
            vnbs = [_mx(uws[h][:, :Dh] - _dot(_mx(uws[h][:, Dh:]), sbs[h], NN)) for h in heads]
            os_ = [_dot(jnp.concatenate([_mx(q_ref[rows, cs[h]] * qs[h]["gam"]), _mx(qs[h]["pmat"])], axis=1),
                        jnp.concatenate([sbs[h], vnbs[h]], axis=0), NN) for h in heads]
            for h in heads:
                state[h] = ss[h] * qs[h]["gl"] + _dot(_mx((k_ref[rows, cs[h]] * qs[h]["rr"]).T), vnbs[h], NN)
            for h in heads:
                o = os_[h]
                o_ref[rows, cs[h]] = o
                zh = z_ref[rows, cs[h]]
                og_ref[rows, cs[h]] = _mx(o * _rms_stat(o) * ng * (zh * _sigmoid(zh)))
            return carry

        lax.fori_loop(0, PER, chunk, 0)

    PER = 2 if N % 2 == 0 else 1
    blk = lambda j: pl.BlockSpec((PER * C, W), lambda n: (n, j))
    small = pl.BlockSpec((PER * C, LANES), lambda n: (n, 0))
    return pl.pallas_call(
        body, name=name, grid=(N // PER,),
        in_specs=[blk(0), blk(1), blk(2), small, small, blk(3), _full((1, Dh))],
        out_specs=[blk(0), blk(0), pl.BlockSpec((H, PER * C, C), lambda n: (0, n, 0)),
                   pl.BlockSpec((PER, H, Dh, Dh), lambda n: (n, 0, 0, 0))],
        out_shape=[jax.ShapeDtypeStruct((T, W), MXU_DTYPE), jax.ShapeDtypeStruct((T, W), F32),
                   jax.ShapeDtypeStruct((H, T, C), F32), jax.ShapeDtypeStruct((N, H, Dh, Dh), F32)],
        scratch_shapes=[pltpu.VMEM((H, Dh, Dh), F32)],
        compiler_params=_cp("arbitrary"),
    )(qkv, qkv, qkv, beta, g, proj, norm_g)


def _dn_scan_bwd(qkv, beta, g, proj, norm_g, o, tinv, s_all, dog, name):
    T = qkv.shape[0]
    C, H, Dh = DN_CHUNK, DN_HEADS, DN_HEAD_DIM
    W = H * Dh
    N = T // C

    def body(q_ref, k_ref, v_ref, beta_ref, g_ref, z_ref, ng_ref, o_ref, tinv_ref, s_ref, dog_ref,
             dqkv_ref, dbeta_ref, dg_ref, dz_ref, dng_ref, dstate):
        @pl.when(pl.program_id(0) == 0)
        def _():
            dstate[...] = jnp.zeros_like(dstate)
            dng_ref[...] = jnp.zeros_like(dng_ref)

        gcs, gcs_t = _chunk_gates(g_ref[...])
        beta_blk = beta_ref[...]
        ng = ng_ref[...]
        incl = _tri(C, False)
        strict = _tri(C, True)
        lane = lax.broadcasted_iota(jnp.int32, (C, LANES), 1)
        rowi = lax.broadcasted_iota(jnp.int32, (C, 1), 0)
        headrow = lax.broadcasted_iota(jnp.int32, (LANES, C), 0)
        colsums = jnp.zeros((LANES, C), F32)
        dbeta_acc = jnp.zeros((C, LANES), F32)
        dgc_acc = jnp.zeros((C, LANES), F32)
        dng_acc = jnp.zeros((1, Dh), F32)
        cs = [slice(h * Dh, (h + 1) * Dh) for h in range(H)]
        rsum = lambda t: jnp.sum(t, axis=1, keepdims=True)
        for heads in (range(0, H // 2), range(H // 2, H)):
            dobs = {}
            for h in heads:
                oh, zh, dogh = o_ref[:, cs[h]], z_ref[:, cs[h]], dog_ref[:, cs[h]]
                rstat = _rms_stat(oh)
                sz = _sigmoid(zh)
                dz_ref[:, cs[h]] = _mx(dogh * (oh * rstat * ng) * (sz * (1.0 + zh * (1.0 - sz))))
                do, dng = _rms_bwd(oh, rstat, ng, dogh * (zh * sz))
                dng_acc = dng_acc + dng
                dobs[h] = _mx(do)
            qs = {h: _head_chunk(h, q_ref[:, cs[h]], k_ref[:, cs[h]], v_ref[:, cs[h]], beta_blk, gcs, gcs_t) for h in heads}
            tms = {h: tinv_ref[h] for h in heads}
            uws = {h: _solve_uw(tms[h], qs[h]) for h in heads}
            ss = {h: s_ref[0, h] for h in heads}
            sbs = {h: _mx(ss[h]) for h in heads}
            wbs = {h: _mx(uws[h][:, Dh:]) for h in heads}
            vnbs = {h: _mx(uws[h][:, :Dh] - _dot(wbs[h], sbs[h], NN)) for h in heads}
            dsns = {h: dstate[h] for h in heads}
            dsbs = {h: _mx(dsns[h]) for h in heads}
            dvnews = {h: _dot(_mx(qs[h]["pmat"]), dobs[h], TN) + _dot(_mx(k_ref[:, cs[h]] * qs[h]["rr"]), dsbs[h], NN)
                      for h in heads}
            dvb16s = {h: _mx(dvnews[h]) for h in heads}
            dps = {h: jnp.where(incl, _dot(dobs[h], vnbs[h], NT), 0.0) for h in heads}
            dqds = {h: _dot(dobs[h], sbs[h], NT) for h in heads}
            dkds = {h: _dot(vnbs[h], dsbs[h], NT) for h in heads}
            dgls = {h: jnp.sum(rsum(ss[h] * dsns[h]), axis=0, keepdims=True) for h in heads}
            dws = {h: -_dot(dvb16s[h], sbs[h], NT) for h in heads}
            for h in heads:
                dstate[h] = qs[h]["gl"] * dsns[h] + _dot(
                    jnp.concatenate([_mx(q_ref[:, cs[h]] * qs[h]["gam"]), -wbs[h]], axis=0),
                    jnp.concatenate([dobs[h], dvb16s[h]], axis=0), TN)
            dsols = {h: _dot(tms[h], jnp.concatenate([dvnews[h], dws[h]], axis=1), TN, TRI_PREC) for h in heads}
            dvbs = {h: dsols[h][:, :Dh] for h in heads}
            dkbgs = {h: dsols[h][:, Dh:] for h in heads}
            dls = {h: jnp.where(strict, -_dot(dsols[h], uws[h], NT, TRI_PREC), 0.0) for h in heads}
            mmats = {h: dls[h] * qs[h]["lmat"] + dps[h] * qs[h]["pmat"] for h in heads}
            dgcs = {h: rsum(mmats[h]) for h in heads}
            for h in heads:
                colsums = jnp.where(headrow == h, jnp.sum(mmats[h], axis=0, keepdims=True), colsums)
            dboth = {h: jnp.concatenate([_mx(dls[h] * qs[h]["dec"]), _mx(dps[h] * qs[h]["dec"])], axis=0) for h in heads}
            for h in heads:
                q = qs[h]
                qh, kh, vh = q_ref[:, cs[h]], k_ref[:, cs[h]], v_ref[:, cs[h]]
                gam, rr, b, kb = q["gam"], q["rr"], q["b"], q["kb"]
                on_k = _dot(dboth[h], _mx(kh), NN)
                dkb = on_k[:C] + dkbgs[h] * gam
                dk = _dot(dboth[h], jnp.concatenate([_mx(kb), _mx(qh)], axis=0), TN) + dkb * b + dkds[h] * rr
                dq = on_k[C:] + dqds[h] * gam
                dgam = rsum(dkbgs[h] * kb) + rsum(dqds[h] * qh)
                dr = rsum(dkds[h] * kh)
                dgc_last = jnp.sum(dr * rr, axis=0, keepdims=True) + dgls[h] * q["gl"]
                dgc = dgcs[h] + dgam * gam - dr * rr + jnp.where(rowi == C - 1, dgc_last, 0.0)
                dbeta = rsum(dvbs[h] * vh) + rsum(dkb * kh)
                dqkv_ref[:, cs[h]] = dq
                dqkv_ref[:, W + h * Dh:W + (h + 1) * Dh] = dk
                dqkv_ref[:, 2 * W + h * Dh:2 * W + (h + 1) * Dh] = dvbs[h] * b
                dbeta_acc = jnp.where(lane == h, dbeta, dbeta_acc)
                dgc_acc = jnp.where(lane == h, dgc, dgc_acc)
        dbeta_ref[...] = dbeta_acc
        dg_ref[...] = _dot(jnp.where(incl, 1.0, 0.0), dgc_acc - colsums.T, TN, HI)
        dng_ref[...] += dng_acc

    rev = lambda n: N - 1 - n
    blk = lambda j: pl.BlockSpec((C, W), lambda n: (rev(n), j))
    small = pl.BlockSpec((C, LANES), lambda n: (rev(n), 0))
    return pl.pallas_call(
        body, name=name, grid=(N,),
        in_specs=[blk(0), blk(1), blk(2), small, small, blk(3), _full((1, Dh)), blk(0),
                  pl.BlockSpec((H, C, C), lambda n: (0, rev(n), 0)),
                  pl.BlockSpec((1, H, Dh, Dh), lambda n: (rev(n), 0, 0, 0)), blk(0)],
        out_specs=[pl.BlockSpec((C, 3 * W), lambda n: (rev(n), 0)), small, small, blk(0), _full((1, Dh))],
        out_shape=[jax.ShapeDtypeStruct((T, 3 * W), F32), jax.ShapeDtypeStruct((T, LANES), F32),
                   jax.ShapeDtypeStruct((T, LANES), F32), jax.ShapeDtypeStruct((T, W), MXU_DTYPE),
                   jax.ShapeDtypeStruct((1, Dh), F32)],
        scratch_shapes=[pltpu.VMEM((H, Dh, Dh), F32)],
        compiler_params=_cp("arbitrary"),
    )(qkv, qkv, qkv, beta, g, proj, norm_g, o, tinv, s_all, dog)


_INV_SQRT2 = 0.7071067811865476
_INV_SQRT_2PI = 0.3989422804014327


def _sg_recompute(zp_ref, bin_ref, lng_ref, lnb_ref):
    E = SG_WIDTH
    zin = zp_ref[...] + bin_ref[...]
    cdf = 0.5 * (1.0 + lax.erf(zin * _INV_SQRT2))
    zz = zin * cdf
    u = zz[:, :E]
    vp = zz[:, E:]
    mu = jnp.mean(vp, axis=-1, keepdims=True)
    xc = vp - mu
    rstd = lax.rsqrt(jnp.mean(xc * xc, axis=-1, keepdims=True) + LN_EPS)
    xhat = xc * rstd
    v = xhat * lng_ref[...] + lnb_ref[...]
    return zin, cdf, u, xhat, rstd, v


def _sg_masked_ws(ws_ref, g):
    return _mx(jnp.where(_tri(SG_CHUNK, False), ws_ref[g], 0.0))


def _sg_fwd(zpre, b_in, ln_g, ln_b, w_s, b_s_t, name):
    T = zpre.shape[0]
    E, G, C, GW = SG_WIDTH, SG_GROUPS, SG_CHUNK, SG_GROUP_W

    def body(zp_ref, bin_ref, lng_ref, lnb_ref, ws_ref, bst_ref, um_ref):
        _, _, u, _, _, v = _sg_recompute(zp_ref, bin_ref, lng_ref, lnb_ref)
        bst = bst_ref[...]
        for g in range(G):
            cs = slice(g * GW, (g + 1) * GW)
            mixed = _dot(_sg_masked_ws(ws_ref, g), _mx(v[:, cs]), NN) + bst[:, g:g + 1]
            um_ref[:, cs] = _mx(u[:, cs] * mixed)

    return pl.pallas_call(
        body, name=name, grid=(T // C,),
        in_specs=[pl.BlockSpec((C, 2 * E), lambda n: (n, 0)), _full((1, 2 * E)), _full((1, E)), _full((1, E)),
                  _full((G, C, C)), _full((C, LANES))],
        out_specs=pl.BlockSpec((C, E), lambda n: (n, 0)),
        out_shape=jax.ShapeDtypeStruct((T, E), MXU_DTYPE), compiler_params=_cp("parallel"),
    )(zpre, b_in, ln_g, ln_b, w_s, b_s_t)


def _sg_bwd(zpre, b_in, ln_g, ln_b, w_s, b_s_t, dum, name):
    T = zpre.shape[0]
    E, G, C, GW = SG_WIDTH, SG_GROUPS, SG_CHUNK, SG_GROUP_W

    def body(zp_ref, bin_ref, lng_ref, lnb_ref, ws_ref, bst_ref, dum_ref,
             dz_ref, dbin_ref, dlng_ref, dlnb_ref, dws_ref, dbst_ref):
        @pl.when(pl.program_id(0) == 0)
        def _():
            for r in (dbin_ref, dlng_ref, dlnb_ref, dws_ref, dbst_ref):
                r[...] = jnp.zeros_like(r)

        zin, cdf, u, xhat, rstd, v = _sg_recompute(zp_ref, bin_ref, lng_ref, lnb_ref)
        bst = bst_ref[...]
        lane = lax.broadcasted_iota(jnp.int32, (C, LANES), 1)
        dum_v = dum_ref[...]
        dbst = jnp.zeros((C, LANES), F32)
        du_parts, dv_parts = [], []
        for g in range(G):
            cs = slice(g * GW, (g + 1) * GW)
            wsm = _sg_masked_ws(ws_ref, g)
            vg = _mx(v[:, cs])
            mixed = _dot(wsm, vg, NN) + bst[:, g:g + 1]
            dumg = dum_v[:, cs]
            du_parts.append(dumg * mixed)
            dmixed = dumg * u[:, cs]
            dmb = _mx(dmixed)
            dv_parts.append(_dot(wsm, dmb, TN))
            dws_ref[g] += _dot(dmb, vg, NT)
            dbst = jnp.where(lane == g, jnp.sum(dmixed, axis=1, keepdims=True), dbst)
        dbst_ref[...] += dbst
        du = jnp.concatenate(du_parts, axis=1)
        dv = jnp.concatenate(dv_parts, axis=1)
        dlng_ref[...] += jnp.sum(dv * xhat, axis=0, keepdims=True)
        dlnb_ref[...] += jnp.sum(dv, axis=0, keepdims=True)
        dxh = dv * lng_ref[...]
        dvp = rstd * (dxh - jnp.mean(dxh, axis=-1, keepdims=True) - xhat * jnp.mean(dxh * xhat, axis=-1, keepdims=True))
        dzz = jnp.concatenate([du, dvp], axis=1)
        dzin = dzz * (cdf + zin * (_INV_SQRT_2PI * jnp.exp(-0.5 * zin * zin)))
        dz_ref[...] = _mx(dzin)
        dbin_ref[...] += jnp.sum(dzin, axis=0, keepdims=True)

    return pl.pallas_call(
        body, name=name, grid=(T // C,),
        in_specs=[pl.BlockSpec((C, 2 * E), lambda n: (n, 0)), _full((1, 2 * E)), _full((1, E)), _full((1, E)),
                  _full((G, C, C)), _full((C, LANES)), pl.BlockSpec((C, E), lambda n: (n, 0))],
        out_specs=[pl.BlockSpec((C, 2 * E), lambda n: (n, 0)), _full((1, 2 * E)), _full((1, E)), _full((1, E)),
                   _full((G, C, C)), _full((C, LANES))],
        out_shape=[jax.ShapeDtypeStruct((T, 2 * E), MXU_DTYPE), jax.ShapeDtypeStruct((1, 2 * E), F32),
                   jax.ShapeDtypeStruct((1, E), F32), jax.ShapeDtypeStruct((1, E), F32),
                   jax.ShapeDtypeStruct((G, C, C), F32), jax.ShapeDtypeStruct((C, LANES), F32)],
        compiler_params=_cp("arbitrary"),
    )(zpre, b_in, ln_g, ln_b, w_s, b_s_t, dum)


def _row(v):
    return v.reshape(1, -1)


def _pad_lanes(v):
    v = v.reshape(1, -1)
    return jnp.pad(v, ((0, 0), (0, LANES - v.shape[1])))


def _local_step(x, target, p, weights_for, grads_ready=None, small_ready=None):
    ng = p["norm_g"]
    grads = {}
    dng = [[None] * 6 for _ in range(2)]
    order = [jnp.zeros((), F32)]

    def tell(group):
        zero = grads_ready(group, grads) if grads_ready is not None else None
        if zero is not None:
            order[0] = zero

    def gain(i, s):
        return _row(ng[i, s]) + order[0]

    def ffn_f(xin, i, j, tag, **tail):
        wt = weights_for("ffn" + tag, xin)
        xo, h, gu, y, *rest = _ffn_fwd(xin, _row(ng[i, 4 * j]), _row(ng[i, 4 * j + 1]), wt, "ffn_fwd_" + tag, **tail)
        return (xo, *rest), (xin, h, gu, y, wt)

    (x1, hn0), sv_f00 = ffn_f(x, 0, 0, "00", next_gain=_row(ng[0, 2]))
    dnw = weights_for("dn", x1)
    proj = _mm(hn0, dnw["dn_wqkvz"], "nn", "dn_proj")
    ba = _mm(hn0, dnw["dn_wba"], "nn", "dn_proj_ba")
    a_log = _pad_lanes(p["dn_a_log"])
    dt_bias = _pad_lanes(p["dn_dt_bias"])
    dn_ng = _row(p["dn_norm_g"])
    qkv = _dn_prep_fwd(proj, p["dn_conv_w"], "dn_prep_fwd")
    beta, gdec = _dn_gate_fwd(ba, a_log, dt_bias, "dn_gate_fwd")
    og, o_raw, tinv, s_all = _dn_scan_fwd(qkv, beta, gdec, proj, dn_ng, "dn_scan_fwd")
    m0, x2 = _out_proj_postnorm(og, dnw["dn_wout"], x1, _row(ng[0, 3]), "dn_out")
    (x3,), sv_f01 = ffn_f(x2, 0, 1, "01")
    (x4, hn1), sv_f10 = ffn_f(x3, 1, 0, "10", next_gain=_row(ng[1, 2]))
    sgw = weights_for("sg", x4)
    zpre = _mm(hn1, sgw["sg_win"], "nn", "sg_proj")
    sg_bin = _row(p["sg_b_in"])
    sg_lng = _row(p["sg_ln_g"])
    sg_lnb = _row(p["sg_ln_b"])
    sg_bst = jnp.pad(p["sg_b_s"].T, ((0, 0), (0, LANES - SG_GROUPS)))
    um = _sg_fwd(zpre, sg_bin, sg_lng, sg_lnb, p["sg_w_s"], sg_bst, "sg_fwd")
    m1, x5 = _out_proj_postnorm(um, sgw["sg_wout"], x4, _row(ng[1, 3]), "sg_out")
    (_, loss_part, dx), sv_f11 = ffn_f(x5, 1, 1, "11", target=target)

    def ffn_b(dxo, sv, i, j, tag, last=False):
        xin, h, gu, y, wt = sv
        dxi, dy, a, dgu, dg0, dg1 = _ffn_bwd(dxo, xin, y, gu, gain(i, 4 * j), gain(i, 4 * j + 1), wt, "ffn_bwd_" + tag)
        dng[i][4 * j] = dg0
        dng[i][4 * j + 1] = dg1
        after = None
        if last:
            grads["norm_g"] = jnp.stack([jnp.concatenate(dng[t], axis=0) for t in range(2)])
            after = small_ready(grads, loss_part) if small_ready is not None else None
        grads["wd" + tag] = _mm(a, dy, "tn", "ffn_wgrad_down_" + tag, after=after)
        grads["wguT" + tag] = _mm(dgu, h, "tn", "ffn_wgrad_up_" + tag, after=after)
        tell("ffn" + tag)
        return dxi

    dx = ffn_b(dx, sv_f11, 1, 1, "11")
    dm1, dng[1][3], dum = _postnorm_bwd_dgrad(dx, m1, gain(1, 3), sgw["sg_wout"], "sg_dgrad_out")
    grads["sg_w_out"] = _mm(um, dm1, "tn", "sg_wgrad_out")
    dz1, dbin, dlng, dlnb, dws, dbst = _sg_bwd(zpre, sg_bin, sg_lng, sg_lnb, p["sg_w_s"], sg_bst, dum, "sg_bwd")
    grads["sg_w_inT"] = _mm(dz1, hn1, "tn", "sg_wgrad_in")
    tell("sg")
    dx, dng[1][2] = _dgrad_prenorm_bwd(dz1, sgw["sg_win"], None, dx, x4, gain(1, 2), "sg_dgrad_in")
    grads["sg_b_in"] = dbin.reshape(1, -1)
    grads["sg_ln_g"] = dlng.reshape(1, -1)
    grads["sg_ln_b"] = dlnb.reshape(1, -1)
    grads["sg_w_s"] = jnp.where(jnp.tril(jnp.ones((SG_CHUNK, SG_CHUNK), bool)), dws, 0.0)[None]
    grads["sg_b_s"] = dbst[:, :SG_GROUPS].T[None]
    dx = ffn_b(dx, sv_f10, 1, 0, "10")
    dx = ffn_b(dx, sv_f01, 0, 1, "01")
    dm0, dng[0][3], dog = _postnorm_bwd_dgrad(dx, m0, gain(0, 3), dnw["dn_wout"], "dn_dgrad_out")
    grads["dn_w_out"] = _mm(og, dm0, "tn", "dn_wgrad_out")
    dqkv, dbeta, dgdec, dz0, dnng = _dn_scan_bwd(qkv, beta, gdec, proj, dn_ng, o_raw, tinv, s_all, dog, "dn_scan_bwd")
    dqkv_pre, dconv = _dn_prep_bwd(proj, p["dn_conv_w"], dqkv, "dn_prep_bwd")
    dba, dal, ddt = _dn_gate_bwd(ba, a_log, dt_bias, dbeta, dgdec, "dn_gate_bwd")
    W3 = 3 * DN_HEADS * DN_HEAD_DIM
    dw_qkv = _mm(hn0, dqkv_pre, "tn", "dn_wgrad_qkv")
    dw_z = _mm(hn0, dz0, "tn", "dn_wgrad_z")
    dw_ba = _mm(hn0, dba, "tn", "dn_wgrad_ba")
    grads["dn_w_in"] = jnp.concatenate(
        [dw_qkv, dw_z, dw_ba[:, :DN_HEADS], dw_ba[:, LANES:LANES + DN_HEADS]], axis=1)
    tell("dn")
    dh0 = _mm(dqkv_pre, dnw["dn_wqkvz"][:, :W3], "nt", "dn_dgrad_qkv")
    dh0 = _mm(dz0, dnw["dn_wqkvz"][:, W3:], "nt", "dn_dgrad_z", add=dh0)
    dx, dng[0][2] = _dgrad_prenorm_bwd(dba, dnw["dn_wba"], dh0, dx, x1, gain(0, 2), "dn_dgrad_ba")
    grads["dn_conv_w"] = dconv[None]
    grads["dn_a_log"] = dal[:, :DN_HEADS]
    grads["dn_dt_bias"] = ddt[:, :DN_HEADS]
    grads["dn_norm_g"] = dnng
    dx = ffn_b(dx, sv_f00, 0, 0, "00", last=True)
    return loss_part, dx, grads


def _mesh_pos():
    return lax.axis_index("x"), lax.axis_index("y"), lax.axis_index("c")


def _other_chips(x, y):
    return [(1 - x, y), (x, 1 - y), (1 - x, 1 - y)]


def _allgather_chips(arrs, name):
    n = len(arrs)

    def body(*refs):
        ins, outs = refs[:n], refs[n:2 * n]
        ici_send, ici_recv, d2d_send, d2d_recv = refs[2 * n:]
        x, y, c = _mesh_pos()
        me = 2 * x + y
        chips = _other_chips(x, y)
        sibling = (x, y, 1 - c)

        def ici(i, j, k):
            cx, cy = chips[j]
            return pltpu.make_async_remote_copy(src_ref=ins[i].at[c], dst_ref=outs[i].at[k, c], send_sem=ici_send.at[3 * i + j],
                                                recv_sem=ici_recv.at[3 * i + j], device_id=(cx, cy, c), device_id_type=MESH)

        def d2d(i, j, h):
            cx, cy = chips[j]
            slot = outs[i].at[2 * cx + cy, h]
            return pltpu.make_async_remote_copy(src_ref=slot, dst_ref=slot, send_sem=d2d_send.at[3 * i + j],
                                                recv_sem=d2d_recv.at[3 * i + j], device_id=sibling, device_id_type=MESH)

        sends = [ici(i, j, me) for i in range(n) for j in range(3)]
        for cp in sends:
            cp.start()
        for i in range(n):
            for j, (cx, cy) in enumerate(chips):
                ici(i, j, 2 * cx + cy).wait_recv()
                fwd = d2d(i, j, c)
                fwd.start()
                sends.append(fwd)
        for i in range(n):
            for j in range(3):
                d2d(i, j, 1 - c).wait_recv()
        for cp in sends:
            cp.wait_send()

    return pl.pallas_call(
        body, name=name, in_specs=[ANY] * n, out_specs=[ANY] * n,
        out_shape=[jax.ShapeDtypeStruct((N_CHIPS,) + a.shape, a.dtype) for a in arrs],
        scratch_shapes=[pltpu.SemaphoreType.DMA((3 * n,))] * 4,
    )(*arrs)


HBM = pl.BlockSpec(memory_space=pltpu.HBM)
SEM = pl.BlockSpec(memory_space=pltpu.SEMAPHORE)
TOKEN = jax.ShapeDtypeStruct((SUBLANES, LANES), F32)


_PEERS = {"gather": 3, "scatter": 3, "swap": 1, "all": N_DEV - 1}


def _land_shape(kind, shape):
    if kind == "gather":
        return (N_CHIPS,) + shape
    if kind == "all":
        return (N_DEV,) + shape
    return (N_CHIPS,) + shape[2:] if kind == "swap" else shape


def _peer_copies(kind, flags, src_refs, land_refs, send_sems, recv_sems, receiving):
    x, y, c = _mesh_pos()
    me4, me8 = 2 * x + y, 4 * x + 2 * y + c
    np_ = _PEERS[kind]
    cps = []
    for i, (src, land) in enumerate(zip(src_refs, land_refs)):
        if kind == "swap":
            half = src.at[1 - c] if flags[i] else src.at[:, 1 - c]
            plan = [((x, y, 1 - c), half, land)]
        elif kind == "all":
            masks = [(mx, my, mc) for mx in (0, 1) for my in (0, 1) for mc in (0, 1)][1:]
            peers = [(jnp.where(mx, 1 - x, x), jnp.where(my, 1 - y, y), jnp.where(mc, 1 - c, c)) for mx, my, mc in masks]
            plan = [(p, src, land.at[4 * p[0] + 2 * p[1] + p[2] if receiving else me8]) for p in peers]
        else:
            plan = []
            for cx, cy in _other_chips(x, y):
                k = 2 * cx + cy
                s = src.at[me4 if receiving else k] if kind == "scatter" else src
                plan.append(((cx, cy, c), s, land.at[k if receiving else me4]))
        for j, (peer, s, d) in enumerate(plan):
            cps.append(pltpu.make_async_remote_copy(src_ref=s, dst_ref=d, send_sem=send_sems.at[np_ * i + j],
                                                    recv_sem=recv_sems.at[np_ * i + j], device_id=peer, device_id_type=MESH))
    return cps


def _copies_start(kind, srcs, after, name, flags=None):
    n = len(srcs)
    ns = _PEERS[kind] * n
    lands = [lax.empty(_land_shape(kind, s.shape), s.dtype) for s in srcs]
    after = [] if after is None else [after]

    def body(*refs):
        src_refs, land_refs = refs[:n], refs[n:2 * n]
        send_sems, recv_sems = refs[2 * n + len(after)], refs[2 * n + len(after) + 1]
        token = refs[-1]
        for cp in _peer_copies(kind, flags, src_refs, land_refs, send_sems, recv_sems, False):
            cp.start()
        token[...] = jnp.zeros_like(token)

    outs = pl.pallas_call(
        body, name=name,
        in_specs=[HBM] * (2 * n) + [ANY] * len(after),
        out_specs=(SEM, SEM) + (HBM,) * (2 * n) + (pl.BlockSpec(memory_space=pltpu.VMEM),),
        out_shape=(pltpu.SemaphoreType.DMA((ns,)), pltpu.SemaphoreType.DMA((ns,)))
        + tuple(pltpu.HBM(a.shape, a.dtype) for a in list(srcs) + lands) + (TOKEN,),
        input_output_aliases={i: 2 + i for i in range(2 * n)},
        compiler_params=pltpu.CompilerParams(has_side_effects=pltpu.SideEffectType.DATAFLOW_SIDE_EFFECTING),
    )(*[pltpu.with_memory_space_constraint(a, pltpu.HBM) for a in list(srcs) + lands], *after)
    return dict(sems=outs[:2], srcs=outs[2:2 + n], lands=outs[2 + n:2 + 2 * n], token=outs[-1], kind=kind, flags=flags)


def _copies_wait(started, after, name):
    n = len(started["srcs"])
    kind, flags = started["kind"], started["flags"]
    after = list(after) if isinstance(after, (list, tuple)) else [after]

    def body(*refs):
        src_refs, land_refs = refs[:n], refs[n:2 * n]
        send_sems, recv_sems = refs[2 * n], refs[2 * n + 1]
        for cp in _peer_copies(kind, flags, src_refs, land_refs, send_sems, recv_sems, True):
            cp.wait_send()
            cp.wait_recv()

    outs = pl.pallas_call(
        body, name=name,
        in_specs=[HBM] * (2 * n) + [SEM, SEM] + [ANY] * len(after),
        out_specs=(HBM,) * (2 * n),
        out_shape=tuple(pltpu.HBM(a.shape, a.dtype) for a in list(started["srcs"]) + list(started["lands"])),
        input_output_aliases={i: i for i in range(2 * n)},
        compiler_params=pltpu.CompilerParams(has_side_effects=pltpu.SideEffectType.DATAFLOW_SIDE_EFFECTING),
    )(*started["srcs"], *started["lands"], *started["sems"], *after)
    return outs[:n], outs[n:]


def _swap_whole(arrs, name):
    n = len(arrs)

    def body(*refs):
        ins, outs = refs[:n], refs[n:2 * n]
        send_sems, recv_sems = refs[2 * n:]
        x, y, c = _mesh_pos()
        cps = [pltpu.make_async_remote_copy(src_ref=ins[i], dst_ref=outs[i], send_sem=send_sems.at[i],
                                            recv_sem=recv_sems.at[i], device_id=(x, y, 1 - c), device_id_type=MESH)
               for i in range(n)]
        for cp in cps:
            cp.start()
        for cp in cps:
            cp.wait()

    return pl.pallas_call(
        body, name=name, in_specs=[ANY] * n, out_specs=[ANY] * n,
        out_shape=[jax.ShapeDtypeStruct(a.shape, a.dtype) for a in arrs],
        scratch_shapes=[pltpu.SemaphoreType.DMA((n,)), pltpu.SemaphoreType.DMA((n,))],
    )(*arrs)


def _as_rows(a, lead):
    shp = a.shape
    rows = 1
    for s in shp[lead:-1]:
        rows *= s
    return a.reshape(shp[:lead] + (rows, shp[-1]))


def _row_tile(rows, cols, n_bufs):
    budget = (24 * 1024 * 1024) // (n_bufs * 2 * 4 * cols)
    return _pick(rows, max(2 * SUBLANES, budget), 2 * SUBLANES)


def _sum_devices(own, got, dev, name):
    n, rows, cols = got.shape
    tr = _row_tile(rows, cols, n + 2)

    def body(dev_ref, own_ref, got_ref, o_ref):
        mine = own_ref[...]
        acc = jnp.where(dev_ref[0] == 0, mine, got_ref[0])
        for k in range(1, n):
            acc = acc + jnp.where(dev_ref[0] == k, mine, got_ref[k])
        o_ref[...] = acc

    return pl.pallas_call(
        body, name=name,
        grid_spec=pltpu.PrefetchScalarGridSpec(
            num_scalar_prefetch=1, grid=(rows // tr,),
            in_specs=[pl.BlockSpec((tr, cols), lambda i, d: (i, 0)), pl.BlockSpec((n, tr, cols), lambda i, d: (0, i, 0))],
            out_specs=pl.BlockSpec((tr, cols), lambda i, d: (i, 0))),
        out_shape=jax.ShapeDtypeStruct((rows, cols), F32), compiler_params=_cp("parallel"),
    )(_scalar(dev), own, got)


def _scalar(i):
    return jnp.reshape(i, (1,)).astype(jnp.int32)


def _add_own_half(g, other, c, half_first, name):
    _, rows, cols = other.shape
    tr = _row_tile(rows, cols, 3)

    def body(c_ref, g_ref, o_ref, out_ref):
        out_ref[0] = (g_ref[0, 0] + o_ref[0]).astype(out_ref.dtype)

    if half_first:
        g_map = lambda k, i, c_ref: (c_ref[0], k, i, 0)
    else:
        g_map = lambda k, i, c_ref: (k, c_ref[0], i, 0)
    flat = pl.BlockSpec((1, tr, cols), lambda k, i, c_ref: (k, i, 0))
    return pl.pallas_call(
        body, name=name,
        grid_spec=pltpu.PrefetchScalarGridSpec(
            num_scalar_prefetch=1, grid=(N_CHIPS, rows // tr),
            in_specs=[pl.BlockSpec((1, 1, tr, cols), g_map), flat], out_specs=flat),
        out_shape=jax.ShapeDtypeStruct(other.shape, COMM_DTYPE), compiler_params=_cp("parallel", "parallel"),
    )(_scalar(c), g, other)


def _sum_chips(own, got, chip, name, transpose=False):
    _, rows, cols = own.shape
    tr = rows if transpose else _row_tile(rows, cols, N_CHIPS + 2)

    def body(chip_ref, p_ref, b_ref, o_ref):
        mine = p_ref[0].astype(F32)
        acc = jnp.where(chip_ref[0] == 0, mine, b_ref[0].astype(F32))
        for k in range(1, N_CHIPS):
            acc = acc + jnp.where(chip_ref[0] == k, mine, b_ref[k].astype(F32))
        o_ref[...] = acc.T if transpose else acc

    if transpose:
        out_spec, out_shape = pl.BlockSpec((cols, rows), lambda i, k_ref: (0, 0)), (cols, rows)
    else:
        out_spec, out_shape = pl.BlockSpec((tr, cols), lambda i, k_ref: (i, 0)), (rows, cols)
    return pl.pallas_call(
        body, name=name,
        grid_spec=pltpu.PrefetchScalarGridSpec(
            num_scalar_prefetch=1, grid=(rows // tr,),
            in_specs=[pl.BlockSpec((1, tr, cols), lambda i, k_ref: (k_ref[0], i, 0)),
                      pl.BlockSpec((N_CHIPS, tr, cols), lambda i, k_ref: (0, i, 0))],
            out_specs=out_spec),
        out_shape=jax.ShapeDtypeStruct(out_shape, F32), compiler_params=_cp("parallel"),
    )(_scalar(chip), own, got)


def _adam_math(w, g, m, v):
    nm = ADAM_B1 * m + (1.0 - ADAM_B1) * g
    nv = ADAM_B2 * v + (1.0 - ADAM_B2) * (g * g)
    m_hat = nm / (1.0 - ADAM_B1 ** ADAM_STEP)
    v_hat = nv / (1.0 - ADAM_B2 ** ADAM_STEP)
    return -ADAM_LR * (m_hat / (jnp.sqrt(v_hat) + ADAM_EPS) + ADAM_WD * w), nm, nv


def _adamw_pieces(w, m, v, mine, theirs, c, kind, name):
    shape = w.shape
    P = len(mine)
    ws, ms, vs = (t.reshape((P, -1, t.shape[-1])) for t in (w, m, v))
    _, R, C = ws.shape
    if kind == "rows":
        tr = _pick(R // 2, 512, SUBLANES)
    else:
        tr = _pick(R, 256 if kind in ("lo", "hi") else 512, SUBLANES)
    nt = R // tr
    nh = nt // 2

    def body(c_ref, w_ref, m_ref, v_ref, *refs):
        mine_refs, theirs_refs = refs[:P], refs[P:2 * P]
        g_ref, d_ref, nm_ref, nv_ref = refs[2 * P:]
        p, i, core = pl.program_id(0), pl.program_id(1), c_ref[0]

        def pick(refs_):
            out = refs_[0][...]
            for q in range(1, P):
                out = jnp.where(p == q, refs_[q][...], out)
            return out

        a, b = pick(mine_refs), pick(theirs_refs)
        if kind == "cols":
            gv = jnp.where(core == 0, jnp.concatenate([a, b], axis=1), jnp.concatenate([b, a], axis=1))
        else:
            own = {"lo": core == 0, "hi": core == 1, "rows": (i >= nh) == (core == 1)}[kind]
            gv = jnp.where(own, a, b)
        g_ref[0] = gv
        d_ref[0], nm_ref[0], nv_ref[0] = _adam_math(w_ref[0], gv, m_ref[0], v_ref[0])

    def piece_spec(q):
        tile = (lambda i: i - jnp.where(i >= nh, nh, 0)) if kind == "rows" else (lambda i: i)
        return pl.BlockSpec((tr, mine[q].shape[1]), lambda p, i, c_ref: (jnp.where(p == q, tile(i), 0), 0))

    full = pl.BlockSpec((1, tr, C), lambda p, i, c_ref: (p, i, 0))
    outs = pl.pallas_call(
        body, name=name,
        grid_spec=pltpu.PrefetchScalarGridSpec(num_scalar_prefetch=1, grid=(P, nt),
                                               in_specs=[full] * 3 + [piece_spec(q) for q in range(P)] * 2,
                                               out_specs=[full] * 4),
        out_shape=[jax.ShapeDtypeStruct((P, R, C), F32)] * 4, compiler_params=_cp("parallel", "arbitrary"),
    )(_scalar(c), ws, ms, vs, *mine, *theirs)
    return tuple(o.reshape(shape) for o in outs)


def _adamw(w, g, m, v, name):
    shape = w.shape
    ws, gs, ms, vs = (_as_rows(t, 0) for t in (w, g, m, v))
    rows, cols = ws.shape
    tr = _row_tile(rows, cols, 7)

    def body(w_ref, g_ref, m_ref, v_ref, d_ref, nm_ref, nv_ref):
        d_ref[...], nm_ref[...], nv_ref[...] = _adam_math(w_ref[...], g_ref[...], m_ref[...], v_ref[...])

    spec = pl.BlockSpec((tr, cols), lambda i: (i, 0))
    outs = pl.pallas_call(body, name=name, grid=(rows // tr,), in_specs=[spec] * 4, out_specs=[spec] * 3,
                          out_shape=[jax.ShapeDtypeStruct((rows, cols), F32)] * 3, compiler_params=_cp("parallel"))(ws, gs, ms, vs)
    return tuple(o.reshape(shape) for o in outs)


_BIG = ["ffn_w_gate", "ffn_w_up", "ffn_w_down", "dn_w_in", "dn_w_out", "sg_w_in", "sg_w_out"]
_SMALL_SHARDED = ["norm_g", "dn_conv_w", "sg_b_in", "sg_ln_g", "sg_ln_b"]
_SMALL_REPL = ["dn_a_log", "dn_dt_bias", "dn_norm_g", "sg_w_s", "sg_b_s"]
_WEIGHTS = ["norm_g", "ffn_w_gate", "ffn_w_up", "ffn_w_down", "dn_w_in", "dn_conv_w", "dn_a_log", "dn_dt_bias",
            "dn_norm_g", "dn_w_out", "sg_w_in", "sg_b_in", "sg_ln_g", "sg_ln_b", "sg_w_s", "sg_b_s", "sg_w_out"]
PACK_COLS = 1024


def _pack(arrs):
    flat = jnp.concatenate([a.reshape(-1) for a in arrs])
    pad = (-flat.shape[0]) % (SUBLANES * PACK_COLS)
    return jnp.pad(flat, (0, pad)).reshape(-1, PACK_COLS)


def _unpack(buf, shapes):
    flat = buf.reshape(-1)
    out, off = [], 0
    for s in shapes:
        n = math.prod(s)
        out.append(flat[off:off + n].reshape(s))
        off += n
    return out


def _as_halves(a):
    if a.shape[0] == 2:
        return a
    if a.shape[0] == 1:
        return a.reshape((2, a.shape[1] // 2) + a.shape[2:])
    return a.reshape((2, a.shape[0] // 2) + a.shape[1:])


def _with_own(gathered, own, chip):
    g = gathered.reshape((N_CHIPS,) + own.shape)
    return [jnp.where(chip == k, own, g[k]) for k in range(N_CHIPS)]


def _cat_shards(g, axis):
    return jnp.concatenate(list(g), axis=axis)


_GROUP_ORDER = ["ffn00", "dn", "ffn01", "ffn10", "sg", "ffn11"]


def _weight_groups(w):
    cast = {k: _mx(w[k]) for k in _BIG}
    groups = {"ffn%d%d" % (i, j): [cast["ffn_w_gate"][i, j].T, cast["ffn_w_up"][i, j].T, cast["ffn_w_down"][i, j]]
              for i, j in [(0, 0), (0, 1), (1, 0), (1, 1)]}
    groups["dn"] = [cast["dn_w_in"][0], cast["dn_w_out"][0]]
    groups["sg"] = [cast["sg_w_in"][0], cast["sg_w_out"][0]]
    return groups


def _ffn_weights(chip, own, gathered):
    pairs = [(a, g.reshape((N_CHIPS,) + a.shape)) for a, g in zip(own, gathered)]
    return {"chip": chip, "gate": pairs[0], "up": pairs[1], "down": pairs[2]}


def _group_matrices(group, shards):
    if group == "sg":
        return {"sg_win": _cat_shards(shards[0], 1), "sg_wout": _cat_shards(shards[1], 0)}
    dn_full = _cat_shards(shards[0], 1)
    W4 = 4 * DN_HEADS * DN_HEAD_DIM
    wba = jnp.zeros((D_MODEL, 2 * LANES), dn_full.dtype)
    wba = wba.at[:, :DN_HEADS].set(dn_full[:, W4:W4 + DN_HEADS])
    wba = wba.at[:, LANES:LANES + DN_HEADS].set(dn_full[:, W4 + DN_HEADS:])
    return {"dn_wqkvz": dn_full[:, :W4], "dn_wba": wba, "dn_wout": _cat_shards(shards[1], 0)}


def _split_cols(a, n):
    w = a.shape[-1] // n
    return [a[..., k * w:(k + 1) * w] for k in range(n)]


def _split_rows(a, n):
    h = a.shape[-2] // n
    return [a[..., k * h:(k + 1) * h, :] for k in range(n)]


_IJ = [(0, 0), (0, 1), (1, 0), (1, 1)]


def _group_grads(group, grads):
    def rows_by_chip(a):
        return a.reshape(N_CHIPS, 2, a.shape[0] // (2 * N_CHIPS), a.shape[1])

    if group.startswith("ffn"):
        tag = group[3:]
        t = grads["wguT" + tag]
        return (["wguT" + tag, "wd" + tag],
                [t.reshape(2, N_CHIPS, t.shape[0] // (2 * N_CHIPS), t.shape[1]), rows_by_chip(grads["wd" + tag])], [True, False])
    if group == "sg":
        return ["sg_w_inT", "sg_w_out"], [rows_by_chip(grads["sg_w_inT"]), rows_by_chip(grads["sg_w_out"])], [False, False]
    dn_in = jnp.stack([jnp.stack(_split_cols(hf, N_CHIPS)) for hf in _split_rows(grads["dn_w_in"], 2)])
    return ["dn_w_in", "dn_w_out"], [dn_in, rows_by_chip(grads["dn_w_out"])], [True, False]


_SHARD_PIECES = {
    "ffn_w_gate": (["wguT%d%d" % ij for ij in _IJ], "lo"),
    "ffn_w_up": (["wguT%d%d" % ij for ij in _IJ], "hi"),
    "ffn_w_down": (["wd%d%d" % ij for ij in _IJ], "rows"),
    "dn_w_in": (["dn_w_in"], "rows"),
    "dn_w_out": (["dn_w_out"], "rows"),
    "sg_w_in": (["sg_w_inT"], "cols"),
    "sg_w_out": (["sg_w_out"], "rows"),
}


def kernel(x, norm_g, ffn_w_gate, ffn_w_up, ffn_w_down, dn_w_in, dn_conv_w, dn_a_log, dn_dt_bias, dn_norm_g, dn_w_out, sg_w_in, sg_b_in, sg_ln_g, sg_ln_b, sg_w_s, sg_b_s, sg_w_out, loss_target, m_norm_g, m_ffn_w_gate, m_ffn_w_up, m_ffn_w_down, m_dn_w_in, m_dn_conv_w, m_dn_a_log, m_dn_dt_bias, m_dn_norm_g, m_dn_w_out, m_sg_w_in, m_sg_b_in, m_sg_ln_g, m_sg_ln_b, m_sg_w_s, m_sg_b_s, m_sg_w_out, v_norm_g, v_ffn_w_gate, v_ffn_w_up, v_ffn_w_down, v_dn_w_in, v_dn_conv_w, v_dn_a_log, v_dn_dt_bias, v_dn_norm_g, v_dn_w_out, v_sg_w_in, v_sg_b_in, v_sg_ln_g, v_sg_ln_b, v_sg_w_s, v_sg_b_s, v_sg_w_out):
    args = dict(locals())
    w = {k: args[k] for k in _WEIGHTS}
    mom = {k: args["m_" + k] for k in _WEIGHTS}
    var = {k: args["v_" + k] for k in _WEIGHTS}
    cx, cy, cc = _mesh_pos()
    chip = 2 * cx + cy

    small_shapes = [w[k].shape for k in _SMALL_SHARDED]
    groups = _weight_groups(w)
    own = groups[_GROUP_ORDER[0]] + [_pack([w[k] for k in _SMALL_SHARDED])]
    first = _allgather_chips([_as_halves(a) for a in own], "gather_first")
    started, after = {}, first[0]
    for g in _GROUP_ORDER[1:]:
        started[g] = _copies_start("gather", groups[g], after, "gather_start_" + g)
        after = started[g]["token"]
    small_k = [_unpack(pack, small_shapes) for pack in _with_own(first[-1], own[-1], chip)]
    p = {name: jnp.concatenate([small_k[k][i] for k in range(N_CHIPS)], axis=-1) for i, name in enumerate(_SMALL_SHARDED)}
    p = {k: (v if k == "norm_g" else v[0]) for k, v in p.items()}
    p["norm_g"] = p["norm_g"] + after[0, 0]
    for k in _SMALL_REPL:
        p[k] = w[k][0]

    def weights_for(group, after):
        if group == _GROUP_ORDER[0]:
            return _ffn_weights(chip, own[:-1], first[:-1])
        srcs, lands = _copies_wait(started[group], after, "gather_wait_" + group)
        if group.startswith("ffn"):
            return _ffn_weights(chip, srcs, lands)
        return _group_matrices(group, [_with_own(l, a, chip) for l, a in zip(lands, srcs)])

    mine, theirs, to_core, to_chips = {}, {}, [], []

    def send_to_chips(after):
        group, names, flags, swap = to_core.pop(0)
        halves, got = _copies_wait(swap, after, "swap_wait_" + group)
        pair_sum = [_add_own_half(h, o, cc, hf, "pair_sum_" + n) for n, h, o, hf in zip(names, halves, got, flags)]
        scatter = _copies_start("scatter", pair_sum, got[0], "reduce_start_" + group)
        to_chips.append((group, names, scatter))
        return scatter["token"]

    def finish(after):
        group, names, scatter = to_chips.pop(0)
        pair_sum, got = _copies_wait(scatter, after, "reduce_wait_" + group)
        half_sum = [_sum_chips(a, b, chip, "chip_sum_" + n, transpose=n == "sg_w_inT")
                    for n, a, b in zip(names, pair_sum, got)]
        other = _swap_whole(half_sum, "gather_core_pair_" + group)
        mine.update(zip(names, half_sum))
        theirs.update(zip(names, other))

    def grads_ready(group, grads):
        names, halves, flags = _group_grads(group, grads)
        swap = _copies_start("swap", halves, None, "swap_start_" + group, flags)
        token = swap["token"]
        if to_core:
            token = send_to_chips(token)
            if len(to_chips) > 1:
                finish(token)
        to_core.append((group, names, flags, swap))
        return token[0, 0]

    small_names = _SMALL_SHARDED + _SMALL_REPL
    small = {}

    def small_ready(grads, loss_part):
        parts = [grads[k] for k in small_names]
        small["shapes"] = [g.shape for g in parts] + [(1,)]
        pack = _pack(parts + [loss_part[0, :1]])
        small["exchange"] = _copies_start("all", [pack], None, "small_start")
        return small["exchange"]["token"]

    loss_part, grad_x, grads = _local_step(x[0], loss_target[0], p, weights_for, grads_ready, small_ready)
    token = send_to_chips(to_core[0][3]["token"])
    finish(token)
    (pack,), (packs,) = _copies_wait(small["exchange"], list(theirs.values()), "small_wait")
    summed = _sum_devices(pack, packs, 4 * cx + 2 * cy + cc, "small_sum")
    parts = _unpack(summed, small["shapes"])
    loss = parts[-1][0]
    grad = {}
    for i, k in enumerate(small_names):
        g = parts[i]
        if k in _SMALL_SHARDED:
            n = w[k].shape[-1]
            g = lax.dynamic_slice_in_dim(g, chip * n, n, axis=g.ndim - 1)
        grad[k] = g

    delta, new_m, new_v = {}, {}, {}

    def update(keys):
        for k in keys:
            names, kind = _SHARD_PIECES[k]
            turn = (lambda a: jnp.swapaxes(a, -1, -2)) if names[0].startswith("wguT") else (lambda a: a)
            outs = _adamw_pieces(turn(w[k]), turn(mom[k]), turn(var[k]), [mine[n] for n in names], [theirs[n] for n in names],
                                 cc, kind, "adamw_" + k)
            grad[k], delta[k], new_m[k], new_v[k] = (turn(o) for o in outs)

    shapes = [w[k].shape for k in small_names]
    d, nm, nv = _adamw(_pack([w[k] for k in small_names]), _pack([grad[k] for k in small_names]),
                       _pack([mom[k] for k in small_names]), _pack([var[k] for k in small_names]), "adamw_small")
    for k, a, b, c_ in zip(small_names, _unpack(d, shapes), _unpack(nm, shapes), _unpack(nv, shapes)):
        delta[k], new_m[k], new_v[k] = a, b, c_
    mixers = [k for k in _BIG if not k.startswith("ffn")]
    update(mixers)
    finish([d] + [delta[k] for k in mixers] + list(theirs.values()))
    update([k for k in _BIG if k.startswith("ffn")])

    return (loss, grad_x[None], *[grad[k] for k in _WEIGHTS], *[delta[k] for k in _WEIGHTS],
            *[new_m[k] for k in _WEIGHTS], *[new_v[k] for k in _WEIGHTS])
```

```python
import math

import jax
import jax.numpy as jnp
from jax import lax
from jax.experimental import pallas as pl
from jax.experimental.pallas import tpu as pltpu

F32 = jnp.float32
MXU_DTYPE = jnp.bfloat16
COMM_DTYPE = jnp.bfloat16
HI = lax.Precision.HIGHEST
TRI_PREC = lax.Precision.HIGH

D_MODEL = 1024
RMS_EPS = 1e-6
LN_EPS = 1e-5
L2_EPS = 1e-6
DN_HEADS = 8
DN_HEAD_DIM = 128
DN_CONV = 4
DN_CHUNK = 64
SG_WIDTH = 2048
SG_GROUPS = 8
SG_CHUNK = 128
SG_GROUP_W = SG_WIDTH // SG_GROUPS
N_CHIPS = 4
N_DEV = 8
LANES = 128
SUBLANES = 8
VMEM_LIMIT = 56 * 1024 * 1024

ADAM_LR = 0.001
ADAM_B1 = 0.9
ADAM_B2 = 0.999
ADAM_EPS = 1e-08
ADAM_WD = 0.01
ADAM_STEP = 10

MESH = pl.DeviceIdType.MESH
ANY = pl.BlockSpec(memory_space=pl.ANY)


def _cp(*sem):
    return pltpu.CompilerParams(dimension_semantics=sem, vmem_limit_bytes=VMEM_LIMIT)


def _pick(n, pref, mult=LANES):
    best = None
    d = mult
    while d <= min(n, pref):
        if n % d == 0:
            best = d
        d += mult
    return best if best is not None else n


def _full(shape):
    nd = len(shape)
    return pl.BlockSpec(shape, lambda *_: (0,) * nd)


def _sigmoid(x):
    return 1.0 / (1.0 + jnp.exp(-x))


def _dot(a, b, dims, prec=None):
    return lax.dot_general(a, b, (dims, ((), ())), preferred_element_type=F32, precision=prec)


NN = ((1,), (0,))
NT = ((1,), (1,))
TN = ((0,), (0,))


def _mx(a):
    return a.astype(MXU_DTYPE)


def _rms_stat(x):
    return lax.rsqrt(jnp.mean(x * x, axis=-1, keepdims=True) + RMS_EPS)


def _rms_bwd(x, r, g, dy):
    xh = x * r
    dxh = dy * g
    dx = r * (dxh - xh * jnp.mean(dxh * xh, axis=-1, keepdims=True))
    return dx, jnp.sum(dy * xh, axis=0, keepdims=True)


def _mm(a, b, mode, name, out_dtype=F32, add=None, after=None):
    if mode == "tn":
        K, M = a.shape
        N = b.shape[1]
    elif mode == "nt":
        M, K = a.shape
        N = b.shape[0]
    else:
        M, K = a.shape
        N = b.shape[1]
    tn = _pick(N, 1024)
    if mode == "tn":
        tm = _pick(M, 1024 if tn <= 512 else 1408)
        tk = _pick(K, 2048, SUBLANES)
    else:
        tm = _pick(M, max(512, min(2048, (1024 * 1024) // tn)), SUBLANES)
        tk = _pick(K, 2048)
    nk = K // tk
    grid = (N // tn, M // tm, nk)
    if mode == "nn":
        a_spec = pl.BlockSpec((tm, tk), lambda j, i, k: (i, k))
        b_spec = pl.BlockSpec((tk, tn), lambda j, i, k: (k, j))
        dims = NN
    elif mode == "nt":
        a_spec = pl.BlockSpec((tm, tk), lambda j, i, k: (i, k))
        b_spec = pl.BlockSpec((tn, tk), lambda j, i, k: (j, k))
        dims = NT
    else:
        a_spec = pl.BlockSpec((tk, tm), lambda j, i, k: (k, i))
        b_spec = pl.BlockSpec((tk, tn), lambda j, i, k: (k, j))
        dims = TN
    o_spec = pl.BlockSpec((tm, tn), lambda j, i, k: (i, j))
    has_add = add is not None

    def body(*refs):
        a_ref, b_ref = refs[:2]
        add_ref = refs[2] if has_add else None
        o_ref, acc = refs[-2:]
        k = pl.program_id(2)

        @pl.when(k == 0)
        def _():
            acc[...] = add_ref[...] if has_add else jnp.zeros_like(acc)

        acc[...] += _dot(a_ref[...], b_ref[...], dims)

        @pl.when(k == nk - 1)
        def _():
            o_ref[...] = acc[...].astype(o_ref.dtype)

    ins = [a, b] + ([add] if has_add else []) + ([after] if after is not None else [])
    specs = [a_spec, b_spec] + ([o_spec] if has_add else []) + ([ANY] if after is not None else [])
    return pl.pallas_call(
        body, name=name, grid=grid, in_specs=specs, out_specs=o_spec,
        out_shape=jax.ShapeDtypeStruct((M, N), out_dtype),
        scratch_shapes=[pltpu.VMEM((tm, tn), F32)],
        compiler_params=_cp("parallel", "parallel", "arbitrary"),
    )(*ins)


def _ffn_weight_operands(wt):
    return [_scalar(wt["chip"])] , [wt["gate"][0], wt["gate"][1], wt["up"][0], wt["up"][1], wt["down"][0], wt["down"][1]]


def _load_ffn_weights(chip_ref, shard_refs, wgu_v, wd_v, sem):
    fs = wd_v.shape[0] // N_CHIPS

    @pl.when(pl.program_id(0) == 0)
    def _():
        me = chip_ref[0]
        waits = []
        for t, (dst, base) in enumerate([(wgu_v, 0), (wgu_v, wd_v.shape[0]), (wd_v, 0)]):
            own, gathered = shard_refs[2 * t], shard_refs[2 * t + 1]
            for k in range(N_CHIPS):
                slot = dst.at[pl.ds(base + k * fs, fs), :]
                s = sem.at[t * N_CHIPS + k]

                @pl.when(me == k)
                def _(own=own, slot=slot, s=s):
                    pltpu.make_async_copy(own, slot, s).start()

                @pl.when(me != k)
                def _(gathered=gathered, k=k, slot=slot, s=s):
                    pltpu.make_async_copy(gathered.at[k], slot, s).start()

                waits.append(pltpu.make_async_copy(own, slot, s))
        for cp in waits:
            cp.wait()


def _ffn_fwd(x, g0, g1, wt, name, next_gain=None, target=None):
    T, D = x.shape
    F = N_CHIPS * wt["down"][0].shape[0]
    F2 = 2 * F
    tm = _pick(T, 256, SUBLANES)
    prefetch, shards = _ffn_weight_operands(wt)
    extra = [a for a in (next_gain, target) if a is not None]
    n_tail = 1 if next_gain is not None else (2 if target is not None else 0)

    def body(chip_ref, x_ref, g0_ref, g1_ref, *refs):
        extra_ref = refs[0] if extra else None
        refs = refs[len(extra):]
        shard_refs = refs[:6]
        xo_ref, h_ref, gu_ref, y_ref = refs[6:10]
        tail_refs = refs[10:10 + n_tail]
        wgu_v, wd_v, sem = refs[10 + n_tail:]
        _load_ffn_weights(chip_ref, shard_refs, wgu_v, wd_v, sem)
        xv = x_ref[...]
        hb = _mx(xv * _rms_stat(xv) * g0_ref[...])
        h_ref[...] = hb
        gu = _dot(hb, wgu_v[...], NT)
        gu_ref[...] = gu.astype(gu_ref.dtype)
        g = gu[:, :F]
        u = gu[:, F:]
        a = _mx(g * _sigmoid(g) * u)
        y = _dot(a, wd_v[...], NN)
        y_ref[...] = y
        xo = xv + 0.5 * (y * _rms_stat(y) * g1_ref[...])
        xo_ref[...] = xo
        if next_gain is not None:
            tail_refs[0][...] = _mx(xo * _rms_stat(xo) * extra_ref[...])
        if target is not None:
            loss_ref, dy_ref = tail_refs

            @pl.when(pl.program_id(0) == 0)
            def _():
                loss_ref[...] = jnp.zeros_like(loss_ref)

            e = xo - extra_ref[...]
            dy_ref[...] = e * (1.0 / D)
            loss_ref[...] += 0.5 * jnp.sum(jnp.mean(e * e, axis=-1, keepdims=True), axis=0, keepdims=True)

    row = lambda w: pl.BlockSpec((tm, w), lambda i, c: (i, 0))
    one = pl.BlockSpec((1, D), lambda i, c: (0, 0))
    tail_specs, tail_shapes, extra_specs = [], [], []
    if next_gain is not None:
        extra_specs, tail_specs, tail_shapes = [one], [row(D)], [jax.ShapeDtypeStruct((T, D), MXU_DTYPE)]
    if target is not None:
        extra_specs = [row(D)]
        tail_specs = [pl.BlockSpec((SUBLANES, LANES), lambda i, c: (0, 0)), row(D)]
        tail_shapes = [jax.ShapeDtypeStruct((SUBLANES, LANES), F32), jax.ShapeDtypeStruct((T, D), F32)]
    return pl.pallas_call(
        body, name=name,
        grid_spec=pltpu.PrefetchScalarGridSpec(
            num_scalar_prefetch=1, grid=(T // tm,),
            in_specs=[row(D), one, one] + extra_specs + [ANY] * 6,
            out_specs=[row(D), row(D), row(F2), row(D)] + tail_specs,
            scratch_shapes=[pltpu.VMEM((F2, D), MXU_DTYPE), pltpu.VMEM((F, D), MXU_DTYPE),
                            pltpu.SemaphoreType.DMA((3 * N_CHIPS,))]),
        out_shape=[jax.ShapeDtypeStruct((T, D), F32), jax.ShapeDtypeStruct((T, D), MXU_DTYPE),
                   jax.ShapeDtypeStruct((T, F2), MXU_DTYPE), jax.ShapeDtypeStruct((T, D), F32)] + tail_shapes,
        compiler_params=_cp("arbitrary"),
    )(*prefetch, x, g0, g1, *extra, *shards)


FFN_BWD_CHUNK = 2816


def _ffn_bwd(dxo, x, y, gu, g0, g1, wt, name):
    T, D = x.shape
    F2 = gu.shape[1]
    F = F2 // 2
    tm = _pick(T, 256, SUBLANES)
    fc = _pick(F, FFN_BWD_CHUNK)
    prefetch, shards = _ffn_weight_operands(wt)

    def body(chip_ref, dxo_ref, x_ref, y_ref, gu_ref, g0_ref, g1_ref, *refs):
        shard_refs = refs[:6]
        dx_ref, dy_ref, a_ref, dgu_ref, dg0_ref, dg1_ref, wgu_v, wd_v, sem = refs[6:]
        _load_ffn_weights(chip_ref, shard_refs, wgu_v, wd_v, sem)

        @pl.when(pl.program_id(0) == 0)
        def _():
            dg0_ref[...] = jnp.zeros_like(dg0_ref)
            dg1_ref[...] = jnp.zeros_like(dg1_ref)

        dxo_v = dxo_ref[...]
        yv = y_ref[...]
        dy, dg1 = _rms_bwd(yv, _rms_stat(yv), g1_ref[...], 0.5 * dxo_v)
        dg1_ref[...] += dg1
        dyb = _mx(dy)
        dy_ref[...] = dyb
        dh = jnp.zeros((tm, D), F32)
        for c in range(F // fc):
            lo, hi = c * fc, (c + 1) * fc
            da = _dot(dyb, wd_v[lo:hi, :], NT)
            g = gu_ref[:, lo:hi].astype(F32)
            u = gu_ref[:, F + lo:F + hi].astype(F32)
            s = _sigmoid(g)
            sg = g * s
            a_ref[:, lo:hi] = _mx(sg * u)
            dg = _mx(da * u * (s * (1.0 + g * (1.0 - s))))
            du = _mx(da * sg)
            dgu_ref[:, lo:hi] = dg
            dgu_ref[:, F + lo:F + hi] = du
            dh = dh + _dot(dg, wgu_v[lo:hi, :], NN) + _dot(du, wgu_v[F + lo:F + hi, :], NN)
        xv = x_ref[...]
        dx, dg0 = _rms_bwd(xv, _rms_stat(xv), g0_ref[...], dh)
        dg0_ref[...] += dg0
        dx_ref[...] = dxo_v + dx

    row = lambda w: pl.BlockSpec((tm, w), lambda i, c: (i, 0))
    one = pl.BlockSpec((1, D), lambda i, c: (0, 0))
    return pl.pallas_call(
        body, name=name,
        grid_spec=pltpu.PrefetchScalarGridSpec(
            num_scalar_prefetch=1, grid=(T // tm,),
            in_specs=[row(D), row(D), row(D), row(F2), one, one] + [ANY] * 6,
            out_specs=[row(D), row(D), row(F), row(F2), one, one],
            scratch_shapes=[pltpu.VMEM((F2, D), MXU_DTYPE), pltpu.VMEM((F, D), MXU_DTYPE),
                            pltpu.SemaphoreType.DMA((3 * N_CHIPS,))]),
        out_shape=[jax.ShapeDtypeStruct((T, D), F32), jax.ShapeDtypeStruct((T, D), MXU_DTYPE),
                   jax.ShapeDtypeStruct((T, F), MXU_DTYPE), jax.ShapeDtypeStruct((T, F2), MXU_DTYPE),
                   jax.ShapeDtypeStruct((1, D), F32), jax.ShapeDtypeStruct((1, D), F32)],
        compiler_params=_cp("arbitrary"),
    )(*prefetch, dxo, x, y, gu, g0, g1, *shards)


def _out_proj_postnorm(a, b, x, g, name):
    T, K = a.shape
    D = b.shape[1]
    tm = _pick(T, 1024, SUBLANES)

    def body(a_ref, b_ref, x_ref, g_ref, m_ref, o_ref):
        mv = _dot(a_ref[...], b_ref[...], NN)
        m_ref[...] = mv
        o_ref[...] = x_ref[...] + mv * _rms_stat(mv) * g_ref[...]

    row = lambda w: pl.BlockSpec((tm, w), lambda i: (i, 0))
    return pl.pallas_call(body, name=name, grid=(T // tm,),
                          in_specs=[row(K), _full((K, D)), row(D), _full((1, D))], out_specs=[row(D), row(D)],
                          out_shape=[jax.ShapeDtypeStruct((T, D), F32)] * 2, compiler_params=_cp("parallel"))(a, b, x, g)


def _postnorm_bwd_dgrad(dxo, m, g, b, name):
    T, D = m.shape
    K = b.shape[0]
    tm = _pick(T, 1024, SUBLANES)

    def body(dxo_ref, m_ref, g_ref, b_ref, dm_ref, dg_ref, da_ref):
        @pl.when(pl.program_id(0) == 0)
        def _():
            dg_ref[...] = jnp.zeros_like(dg_ref)

        mv = m_ref[...]
        dm, dg = _rms_bwd(mv, _rms_stat(mv), g_ref[...], dxo_ref[...])
        dg_ref[...] += dg
        dmb = _mx(dm)
        dm_ref[...] = dmb
        da_ref[...] = _dot(dmb, b_ref[...], NT)

    row = lambda w: pl.BlockSpec((tm, w), lambda i: (i, 0))
    return pl.pallas_call(body, name=name, grid=(T // tm,), in_specs=[row(D), row(D), _full((1, D)), _full((K, D))],
                          out_specs=[row(D), _full((1, D)), row(K)],
                          out_shape=[jax.ShapeDtypeStruct((T, D), MXU_DTYPE), jax.ShapeDtypeStruct((1, D), F32),
                                     jax.ShapeDtypeStruct((T, K), F32)],
                          compiler_params=_cp("arbitrary"))(dxo, m, g, b)


def _dgrad_prenorm_bwd(a, b, add, dxo, x, g, name):
    T, K = a.shape
    D = b.shape[0]
    tm = _pick(T, 1024, SUBLANES)
    tk = _pick(K, 2048)
    nk = K // tk
    has_add = add is not None

    def body(*refs):
        a_ref, b_ref = refs[:2]
        add_ref = refs[2] if has_add else None
        dxo_ref, x_ref, g_ref, dx_ref, dg_ref, acc = refs[-6:]
        i, k = pl.program_id(0), pl.program_id(1)

        @pl.when((i == 0) & (k == 0))
        def _():
            dg_ref[...] = jnp.zeros_like(dg_ref)

        @pl.when(k == 0)
        def _():
            acc[...] = add_ref[...] if has_add else jnp.zeros_like(acc)

        acc[...] += _dot(a_ref[...], b_ref[...], NT)

        @pl.when(k == nk - 1)
        def _():
            xv = x_ref[...]
            dx, dg = _rms_bwd(xv, _rms_stat(xv), g_ref[...], acc[...])
            dg_ref[...] += dg
            dx_ref[...] = dxo_ref[...] + dx

    row = pl.BlockSpec((tm, D), lambda i, k: (i, 0))
    one = pl.BlockSpec((1, D), lambda i, k: (0, 0))
    ins = [a, b] + ([add] if has_add else []) + [dxo, x, g]
    specs = ([pl.BlockSpec((tm, tk), lambda i, k: (i, k)), pl.BlockSpec((D, tk), lambda i, k: (0, k))]
             + ([row] if has_add else []) + [row, row, one])
    return pl.pallas_call(body, name=name, grid=(T // tm, nk), in_specs=specs, out_specs=[row, one],
                          out_shape=[jax.ShapeDtypeStruct((T, D), F32), jax.ShapeDtypeStruct((1, D), F32)],
                          scratch_shapes=[pltpu.VMEM((tm, D), F32)],
                          compiler_params=_cp("arbitrary", "arbitrary"))(*ins)


DN_ROWS = 512


def _shift_down(prev8, cur, s):
    n = cur.shape[0]
    xx = jnp.concatenate([prev8, cur], axis=0)
    return pltpu.roll(xx, s, 0)[SUBLANES:SUBLANES + n, :]


def _shift_up(cur, next8, s):
    n = cur.shape[0]
    xx = jnp.concatenate([cur, next8], axis=0)
    return pltpu.roll(xx, n + SUBLANES - s, 0)[:n, :]


def _tile_start(r, rows):
    return r * rows if isinstance(r, int) else pl.multiple_of(r * rows, SUBLANES)


def _conv_tile(x_ref, w, r, rows):
    start = _tile_start(r, rows)
    cur = x_ref[pl.ds(start, rows), :]
    if isinstance(r, int):
        prev8 = jnp.zeros((SUBLANES, cur.shape[1]), cur.dtype)
        taps = [_shift_down(prev8, cur, DN_CONV - 1 - j) if j < DN_CONV - 1 else cur for j in range(DN_CONV)]
    else:
        taps = [x_ref[pl.ds(start - (DN_CONV - 1 - j), rows), :] if j < DN_CONV - 1 else cur for j in range(DN_CONV)]
    c = taps[0] * w[0:1, :]
    for j in range(1, DN_CONV):
        c = c + taps[j] * w[j:j + 1, :]
    return c, taps


def _dn_prep_fwd(proj, conv_w, name):
    T = proj.shape[0]
    W = DN_HEADS * DN_HEAD_DIM
    rows = min(DN_ROWS, T)
    n_inner = T // rows
    scale = DN_HEAD_DIM ** -0.5

    def body(x_ref, w_ref, o_ref):
        cb = pl.program_id(0)
        w = w_ref[...]
        is_qk = cb < 2 * DN_HEADS
        post = jnp.where(cb < DN_HEADS, scale, 1.0)

        def step(r, carry):
            c, _ = _conv_tile(x_ref, w, r, rows)
            s = c * _sigmoid(c)
            rinv = lax.rsqrt(jnp.sum(s * s, axis=-1, keepdims=True) + L2_EPS)
            o_ref[pl.ds(_tile_start(r, rows), rows), :] = jnp.where(is_qk, s * rinv * post, s)
            return carry

        step(0, 0)
        lax.fori_loop(1, n_inner, step, 0)

    col = pl.BlockSpec((T, LANES), lambda j: (0, j))
    return pl.pallas_call(body, name=name, grid=(3 * W // LANES,),
                          in_specs=[col, pl.BlockSpec((DN_CONV, LANES), lambda j: (0, j))], out_specs=col,
                          out_shape=jax.ShapeDtypeStruct((T, 3 * W), F32), compiler_params=_cp("parallel"))(proj, conv_w)


def _dn_prep_bwd(proj, conv_w, dqkv, name):
    T = proj.shape[0]
    W = DN_HEADS * DN_HEAD_DIM
    rows = min(DN_ROWS, T)
    n_inner = T // rows
    scale = DN_HEAD_DIM ** -0.5

    def body(x_ref, w_ref, dy_ref, dx_ref, dw_ref, dc_scr):
        cb = pl.program_id(0)
        w = w_ref[...]
        is_qk = cb < 2 * DN_HEADS
        post = jnp.where(cb < DN_HEADS, scale, 1.0)

        def step1(r, dws):
            c, taps = _conv_tile(x_ref, w, r, rows)
            sg = _sigmoid(c)
            s = c * sg
            rinv = lax.rsqrt(jnp.sum(s * s, axis=-1, keepdims=True) + L2_EPS)
            dy = dy_ref[pl.ds(_tile_start(r, rows), rows), :]
            yn = s * rinv
            dyn = dy * post
            ds_qk = rinv * (dyn - yn * jnp.sum(dyn * yn, axis=-1, keepdims=True))
            ds = jnp.where(is_qk, ds_qk, dy)
            dc = ds * (sg * (1.0 + c * (1.0 - sg)))
            dc_scr[pl.ds(_tile_start(r, rows), rows), :] = dc
            return tuple(dws[j] + jnp.sum(dc * taps[j], axis=0, keepdims=True) for j in range(DN_CONV))

        zero = jnp.zeros((1, LANES), F32)
        dws = lax.fori_loop(1, n_inner, step1, step1(0, (zero,) * DN_CONV))
        for j in range(DN_CONV):
            dw_ref[j:j + 1, :] = dws[j]

        def step2(r, carry):
            start = _tile_start(r, rows)
            cur = dc_scr[pl.ds(start, rows), :]
            dx = cur * w[DN_CONV - 1:DN_CONV, :]
            for j in range(DN_CONV - 1):
                s = DN_CONV - 1 - j
                if isinstance(r, int):
                    up = _shift_up(cur, jnp.zeros((SUBLANES, LANES), F32), s)
                else:
                    up = dc_scr[pl.ds(start + s, rows), :]
                dx = dx + up * w[j:j + 1, :]
            dx_ref[pl.ds(start, rows), :] = _mx(dx)
            return carry

        lax.fori_loop(0, n_inner - 1, step2, 0)
        step2(n_inner - 1, 0)

    col = pl.BlockSpec((T, LANES), lambda j: (0, j))
    wspec = pl.BlockSpec((DN_CONV, LANES), lambda j: (0, j))
    return pl.pallas_call(body, name=name, grid=(3 * W // LANES,), in_specs=[col, wspec, col], out_specs=[col, wspec],
                          out_shape=[jax.ShapeDtypeStruct((T, 3 * W), MXU_DTYPE), jax.ShapeDtypeStruct((DN_CONV, 3 * W), F32)],
                          scratch_shapes=[pltpu.VMEM((T, LANES), F32)], compiler_params=_cp("parallel"))(proj, conv_w, dqkv)


def _softplus(x):
    return jnp.maximum(x, 0.0) + jnp.log(1.0 + jnp.exp(-jnp.abs(x)))


def _dn_gate_fwd(ba, a_log, dt_bias, name):
    T = ba.shape[0]
    tm = _pick(T, 1024, SUBLANES)

    def body(ba_ref, al_ref, dt_ref, beta_ref, g_ref):
        beta_ref[...] = _sigmoid(ba_ref[:, :LANES])
        g_ref[...] = -jnp.exp(al_ref[...]) * _softplus(ba_ref[:, LANES:] + dt_ref[...])

    row = lambda w: pl.BlockSpec((tm, w), lambda i: (i, 0))
    return pl.pallas_call(body, name=name, grid=(T // tm,), in_specs=[row(2 * LANES), _full((1, LANES)), _full((1, LANES))],
                          out_specs=[row(LANES), row(LANES)],
                          out_shape=[jax.ShapeDtypeStruct((T, LANES), F32)] * 2, compiler_params=_cp("parallel"))(ba, a_log, dt_bias)


def _dn_gate_bwd(ba, a_log, dt_bias, dbeta, dg, name):
    T = ba.shape[0]
    tm = _pick(T, 1024, SUBLANES)

    def body(ba_ref, al_ref, dt_ref, dbeta_ref, dg_ref, dba_ref, dal_ref, ddt_ref):
        @pl.when(pl.program_id(0) == 0)
        def _():
            dal_ref[...] = jnp.zeros_like(dal_ref)
            ddt_ref[...] = jnp.zeros_like(ddt_ref)

        beta = _sigmoid(ba_ref[:, :LANES])
        dba_ref[:, :LANES] = _mx(dbeta_ref[...] * beta * (1.0 - beta))
        pre = ba_ref[:, LANES:] + dt_ref[...]
        ea = jnp.exp(al_ref[...])
        dgv = dg_ref[...]
        da = dgv * (-ea) * _sigmoid(pre)
        dba_ref[:, LANES:] = _mx(da)
        ddt_ref[...] += jnp.sum(da, axis=0, keepdims=True)
        dal_ref[...] += jnp.sum(dgv * (-ea) * _softplus(pre), axis=0, keepdims=True)

    row = lambda w: pl.BlockSpec((tm, w), lambda i: (i, 0))
    one = _full((1, LANES))
    return pl.pallas_call(body, name=name, grid=(T // tm,), in_specs=[row(2 * LANES), one, one, row(LANES), row(LANES)],
                          out_specs=[row(2 * LANES), one, one],
                          out_shape=[jax.ShapeDtypeStruct((T, 2 * LANES), MXU_DTYPE), jax.ShapeDtypeStruct((1, LANES), F32),
                                     jax.ShapeDtypeStruct((1, LANES), F32)],
                          compiler_params=_cp("arbitrary"))(ba, a_log, dt_bias, dbeta, dg)


def _tri(c, strict):
    i = lax.broadcasted_iota(jnp.int32, (c, c), 0)
    j = lax.broadcasted_iota(jnp.int32, (c, c), 1)
    return (i > j) if strict else (i >= j)


def _inv_unit_lower(ls):
    c = ls[0].shape[0]
    i = lax.broadcasted_iota(jnp.int32, (c, c), 0)
    j = lax.broadcasted_iota(jnp.int32, (c, c), 1)
    eye = jnp.where(i == j, 1.0, 0.0)
    facs = [[eye - l for l in ls]]
    cur = ls
    for _ in range(int(math.log2(c)) - 1):
        cur = [_dot(p, p, NN, TRI_PREC) for p in cur]
        facs.append([eye + p for p in cur])
    while len(facs) > 1:
        nxt = [[_dot(a, b, NN, TRI_PREC) for a, b in zip(facs[t], facs[t + 1])] for t in range(0, len(facs) - 1, 2)]
        if len(facs) % 2:
            nxt.append(facs[-1])
        facs = nxt
    return facs[0]


def _chunk_gates(g_blk):
    c = g_blk.shape[0]
    gcs = _dot(jnp.where(_tri(c, False), 1.0, 0.0), g_blk, NN, HI)
    return gcs, gcs.T


def _head_chunk(h, qh, kh, vh, beta_blk, gcs, gcs_t):
    c = qh.shape[0]
    incl = _tri(c, False)
    gc_col = gcs[:, h:h + 1]
    gc_row = gcs_t[h:h + 1, :]
    gc_last = gcs_t[h:h + 1, c - 1:c]
    dec = jnp.where(incl, jnp.exp(jnp.where(incl, gc_col - gc_row, 0.0)), 0.0)
    gam = jnp.exp(gc_col)
    rr = jnp.exp(gc_last - gc_col)
    gl = jnp.exp(gc_last)
    b = beta_blk[:, h:h + 1]
    kb = kh * b
    vb = vh * b
    both = _dot(jnp.concatenate([_mx(kb), _mx(qh)], axis=0), _mx(kh), NT)
    lmat = jnp.where(_tri(c, True), both[:c] * dec, 0.0)
    pmat = jnp.where(incl, both[c:] * dec, 0.0)
    return dict(dec=dec, gam=gam, rr=rr, gl=gl, b=b, kb=kb, vb=vb, lmat=lmat, pmat=pmat)


def _solve_uw(tinv, q):
    return _dot(tinv, jnp.concatenate([q["vb"], q["kb"] * q["gam"]], axis=1), NN, TRI_PREC)


def _dn_scan_fwd(qkv, beta, g, proj, norm_g, name):
    T = qkv.shape[0]
    C, H, Dh = DN_CHUNK, DN_HEADS, DN_HEAD_DIM
    W = H * Dh
    N = T // C

    def body(q_ref, k_ref, v_ref, beta_ref, g_ref, z_ref, ng_ref, og_ref, o_ref, tinv_ref, s_ref, state):
        @pl.when(pl.program_id(0) == 0)
        def _():
            state[...] = jnp.zeros_like(state)

        ng = ng_ref[...]
        heads = range(H)
        cs = [slice(h * Dh, (h + 1) * Dh) for h in heads]

        def chunk(j, carry):
            rows = pl.ds(pl.multiple_of(j * C, C), C)
            gcs, gcs_t = _chunk_gates(g_ref[rows, :])
            beta_blk = beta_ref[rows, :]
            qs = [_head_chunk(h, q_ref[rows, cs[h]], k_ref[rows, cs[h]], v_ref[rows, cs[h]], beta_blk, gcs, gcs_t)
                  for h in heads]
            tinvs = _inv_unit_lower([q["lmat"] for q in qs])
            for h in heads:
                tinv_ref[h, rows, :] = tinvs[h]
            uws = [_solve_uw(tinvs[h], qs[h]) for h in heads]
            ss = [state[h] for h in heads]
            for h in heads:
                s_ref[j, h] = ss[h]
            sbs = [_mx(s) for s in ss]
            vnbs = [_mx(uws[h][:, :Dh] - _dot(_mx(uws[h][:, Dh:]), sbs[h], NN)) for h in heads]
            os_ = [_dot(jnp.concatenate([_mx(q_ref[rows, cs[h]] * qs[h]["gam"]), _mx(qs[h]["pmat"])], axis=1),
                        jnp.concatenate([sbs[h], vnbs[h]], axis=0), NN) for h in heads]
            for h in heads:
                state[h] = ss[h] * qs[h]["gl"] + _dot(_mx((k_ref[rows, cs[h]] * qs[h]["rr"]).T), vnbs[h], NN)
            for h in heads:
                o = os_[h]
                o_ref[rows, cs[h]] = o
                zh = z_ref[rows, cs[h]]
                og_ref[rows, cs[h]] = _mx(o * _rms_stat(o) * ng * (zh * _sigmoid(zh)))
            return carry

        lax.fori_loop(0, PER, chunk, 0)

    PER = 2 if N % 2 == 0 else 1
    blk = lambda j: pl.BlockSpec((PER * C, W), lambda n: (n, j))
    small = pl.BlockSpec((PER * C, LANES), lambda n: (n, 0))
    return pl.pallas_call(
        body, name=name, grid=(N // PER,),
        in_specs=[blk(0), blk(1), blk(2), small, small, blk(3), _full((1, Dh))],
        out_specs=[blk(0), blk(0), pl.BlockSpec((H, PER * C, C), lambda n: (0, n, 0)),
                   pl.BlockSpec((PER, H, Dh, Dh), lambda n: (n, 0, 0, 0))],
        out_shape=[jax.ShapeDtypeStruct((T, W), MXU_DTYPE), jax.ShapeDtypeStruct((T, W), F32),
                   jax.ShapeDtypeStruct((H, T, C), F32), jax.ShapeDtypeStruct((N, H, Dh, Dh), F32)],
        scratch_shapes=[pltpu.VMEM((H, Dh, Dh), F32)],
        compiler_params=_cp("arbitrary"),
    )(qkv, qkv, qkv, beta, g, proj, norm_g)


def _dn_scan_bwd(qkv, beta, g, proj, norm_g, o, tinv, s_all, dog, name):
    T = qkv.shape[0]
    C, H, Dh = DN_CHUNK, DN_HEADS, DN_HEAD_DIM
    W = H * Dh
    N = T // C

    def body(q_ref, k_ref, v_ref, beta_ref, g_ref, z_ref, ng_ref, o_ref, tinv_ref, s_ref, dog_ref,
             dqkv_ref, dbeta_ref, dg_ref, dz_ref, dng_ref, dstate):
        @pl.when(pl.program_id(0) == 0)
        def _():
            dstate[...] = jnp.zeros_like(dstate)
            dng_ref[...] = jnp.zeros_like(dng_ref)

        gcs, gcs_t = _chunk_gates(g_ref[...])
        beta_blk = beta_ref[...]
        ng = ng_ref[...]
        incl = _tri(C, False)
        strict = _tri(C, True)
        lane = lax.broadcasted_iota(jnp.int32, (C, LANES), 1)
        rowi = lax.broadcasted_iota(jnp.int32, (C, 1), 0)
        headrow = lax.broadcasted_iota(jnp.int32, (LANES, C), 0)
        colsums = jnp.zeros((LANES, C), F32)
        dbeta_acc = jnp.zeros((C, LANES), F32)
        dgc_acc = jnp.zeros((C, LANES), F32)
        dng_acc = jnp.zeros((1, Dh), F32)
        cs = [slice(h * Dh, (h + 1) * Dh) for h in range(H)]
        rsum = lambda t: jnp.sum(t, axis=1, keepdims=True)
        for heads in (range(0, H // 2), range(H // 2, H)):
            dobs = {}
            for h in heads:
                oh, zh, dogh = o_ref[:, cs[h]], z_ref[:, cs[h]], dog_ref[:, cs[h]]
                rstat = _rms_stat(oh)
                sz = _sigmoid(zh)
                dz_ref[:, cs[h]] = _mx(dogh * (oh * rstat * ng) * (sz * (1.0 + zh * (1.0 - sz))))
                do, dng = _rms_bwd(oh, rstat, ng, dogh * (zh * sz))
                dng_acc = dng_acc + dng
                dobs[h] = _mx(do)
            qs = {h: _head_chunk(h, q_ref[:, cs[h]], k_ref[:, cs[h]], v_ref[:, cs[h]], beta_blk, gcs, gcs_t) for h in heads}
            tms = {h: tinv_ref[h] for h in heads}
            uws = {h: _solve_uw(tms[h], qs[h]) for h in heads}
            ss = {h: s_ref[0, h] for h in heads}
            sbs = {h: _mx(ss[h]) for h in heads}
            wbs = {h: _mx(uws[h][:, Dh:]) for h in heads}
            vnbs = {h: _mx(uws[h][:, :Dh] - _dot(wbs[h], sbs[h], NN)) for h in heads}
            dsns = {h: dstate[h] for h in heads}
            dsbs = {h: _mx(dsns[h]) for h in heads}
            dvnews = {h: _dot(_mx(qs[h]["pmat"]), dobs[h], TN) + _dot(_mx(k_ref[:, cs[h]] * qs[h]["rr"]), dsbs[h], NN)
                      for h in heads}
            dvb16s = {h: _mx(dvnews[h]) for h in heads}
            dps = {h: jnp.where(incl, _dot(dobs[h], vnbs[h], NT), 0.0) for h in heads}
            dqds = {h: _dot(dobs[h], sbs[h], NT) for h in heads}
            dkds = {h: _dot(vnbs[h], dsbs[h], NT) for h in heads}
            dgls = {h: jnp.sum(rsum(ss[h] * dsns[h]), axis=0, keepdims=True) for h in heads}
            dws = {h: -_dot(dvb16s[h], sbs[h], NT) for h in heads}
            for h in heads:
                dstate[h] = qs[h]["gl"] * dsns[h] + _dot(
                    jnp.concatenate([_mx(q_ref[:, cs[h]] * qs[h]["gam"]), -wbs[h]], axis=0),
                    jnp.concatenate([dobs[h], dvb16s[h]], axis=0), TN)
            dsols = {h: _dot(tms[h], jnp.concatenate([dvnews[h], dws[h]], axis=1), TN, TRI_PREC) for h in heads}
            dvbs = {h: dsols[h][:, :Dh] for h in heads}
            dkbgs = {h: dsols[h][:, Dh:] for h in heads}
            dls = {h: jnp.where(strict, -_dot(dsols[h], uws[h], NT, TRI_PREC), 0.0) for h in heads}
            mmats = {h: dls[h] * qs[h]["lmat"] + dps[h] * qs[h]["pmat"] for h in heads}
            dgcs = {h: rsum(mmats[h]) for h in heads}
            for h in heads:
                colsums = jnp.where(headrow == h, jnp.sum(mmats[h], axis=0, keepdims=True), colsums)
            dboth = {h: jnp.concatenate([_mx(dls[h] * qs[h]["dec"]), _mx(dps[h] * qs[h]["dec"])], axis=0) for h in heads}
            for h in heads:
                q = qs[h]
                qh, kh, vh = q_ref[:, cs[h]], k_ref[:, cs[h]], v_ref[:, cs[h]]
                gam, rr, b, kb = q["gam"], q["rr"], q["b"], q["kb"]
                on_k = _dot(dboth[h], _mx(kh), NN)
                dkb = on_k[:C] + dkbgs[h] * gam
                dk = _dot(dboth[h], jnp.concatenate([_mx(kb), _mx(qh)], axis=0), TN) + dkb * b + dkds[h] * rr
                dq = on_k[C:] + dqds[h] * gam
                dgam = rsum(dkbgs[h] * kb) + rsum(dqds[h] * qh)
                dr = rsum(dkds[h] * kh)
                dgc_last = jnp.sum(dr * rr, axis=0, keepdims=True) + dgls[h] * q["gl"]
                dgc = dgcs[h] + dgam * gam - dr * rr + jnp.where(rowi == C - 1, dgc_last, 0.0)
                dbeta = rsum(dvbs[h] * vh) + rsum(dkb * kh)
                dqkv_ref[:, cs[h]] = dq
                dqkv_ref[:, W + h * Dh:W + (h + 1) * Dh] = dk
                dqkv_ref[:, 2 * W + h * Dh:2 * W + (h + 1) * Dh] = dvbs[h] * b
                dbeta_acc = jnp.where(lane == h, dbeta, dbeta_acc)
                dgc_acc = jnp.where(lane == h, dgc, dgc_acc)
        dbeta_ref[...] = dbeta_acc
        dg_ref[...] = _dot(jnp.where(incl, 1.0, 0.0), dgc_acc - colsums.T, TN, HI)
        dng_ref[...] += dng_acc

    rev = lambda n: N - 1 - n
    blk = lambda j: pl.BlockSpec((C, W), lambda n: (rev(n), j))
    small = pl.BlockSpec((C, LANES), lambda n: (rev(n), 0))
    return pl.pallas_call(
        body, name=name, grid=(N,),
        in_specs=[blk(0), blk(1), blk(2), small, small, blk(3), _full((1, Dh)), blk(0),
                  pl.BlockSpec((H, C, C), lambda n: (0, rev(n), 0)),
                  pl.BlockSpec((1, H, Dh, Dh), lambda n: (rev(n), 0, 0, 0)), blk(0)],
        out_specs=[pl.BlockSpec((C, 3 * W), lambda n: (rev(n), 0)), small, small, blk(0), _full((1, Dh))],
        out_shape=[jax.ShapeDtypeStruct((T, 3 * W), F32), jax.ShapeDtypeStruct((T, LANES), F32),
                   jax.ShapeDtypeStruct((T, LANES), F32), jax.ShapeDtypeStruct((T, W), MXU_DTYPE),
                   jax.ShapeDtypeStruct((1, Dh), F32)],
        scratch_shapes=[pltpu.VMEM((H, Dh, Dh), F32)],
        compiler_params=_cp("arbitrary"),
    )(qkv, qkv, qkv, beta, g, proj, norm_g, o, tinv, s_all, dog)


_INV_SQRT2 = 0.7071067811865476
_INV_SQRT_2PI = 0.3989422804014327


def _sg_recompute(zp_ref, bin_ref, lng_ref, lnb_ref):
    E = SG_WIDTH
    zin = zp_ref[...] + bin_ref[...]
    cdf = 0.5 * (1.0 + lax.erf(zin * _INV_SQRT2))
    zz = zin * cdf
    u = zz[:, :E]
    vp = zz[:, E:]
    mu = jnp.mean(vp, axis=-1, keepdims=True)
    xc = vp - mu
    rstd = lax.rsqrt(jnp.mean(xc * xc, axis=-1, keepdims=True) + LN_EPS)
    xhat = xc * rstd
    v = xhat * lng_ref[...] + lnb_ref[...]
    return zin, cdf, u, xhat, rstd, v


def _sg_masked_ws(ws_ref, g):
    return _mx(jnp.where(_tri(SG_CHUNK, False), ws_ref[g], 0.0))


def _sg_fwd(zpre, b_in, ln_g, ln_b, w_s, b_s_t, name):
    T = zpre.shape[0]
    E, G, C, GW = SG_WIDTH, SG_GROUPS, SG_CHUNK, SG_GROUP_W

    def body(zp_ref, bin_ref, lng_ref, lnb_ref, ws_ref, bst_ref, um_ref):
        _, _, u, _, _, v = _sg_recompute(zp_ref, bin_ref, lng_ref, lnb_ref)
        bst = bst_ref[...]
        for g in range(G):
            cs = slice(g * GW, (g + 1) * GW)
            mixed = _dot(_sg_masked_ws(ws_ref, g), _mx(v[:, cs]), NN) + bst[:, g:g + 1]
            um_ref[:, cs] = _mx(u[:, cs] * mixed)

    return pl.pallas_call(
        body, name=name, grid=(T // C,),
        in_specs=[pl.BlockSpec((C, 2 * E), lambda n: (n, 0)), _full((1, 2 * E)), _full((1, E)), _full((1, E)),
                  _full((G, C, C)), _full((C, LANES))],
        out_specs=pl.BlockSpec((C, E), lambda n: (n, 0)),
        out_shape=jax.ShapeDtypeStruct((T, E), MXU_DTYPE), compiler_params=_cp("parallel"),
    )(zpre, b_in, ln_g, ln_b, w_s, b_s_t)


def _sg_bwd(zpre, b_in, ln_g, ln_b, w_s, b_s_t, dum, name):
    T = zpre.shape[0]
    E, G, C, GW = SG_WIDTH, SG_GROUPS, SG_CHUNK, SG_GROUP_W

    def body(zp_ref, bin_ref, lng_ref, lnb_ref, ws_ref, bst_ref, dum_ref,
             dz_ref, dbin_ref, dlng_ref, dlnb_ref, dws_ref, dbst_ref):
        @pl.when(pl.program_id(0) == 0)
        def _():
            for r in (dbin_ref, dlng_ref, dlnb_ref, dws_ref, dbst_ref):
                r[...] = jnp.zeros_like(r)

        zin, cdf, u, xhat, rstd, v = _sg_recompute(zp_ref, bin_ref, lng_ref, lnb_ref)
        bst = bst_ref[...]
        lane = lax.broadcasted_iota(jnp.int32, (C, LANES), 1)
        dum_v = dum_ref[...]
        dbst = jnp.zeros((C, LANES), F32)
        du_parts, dv_parts = [], []
        for g in range(G):
            cs = slice(g * GW, (g + 1) * GW)
            wsm = _sg_masked_ws(ws_ref, g)
            vg = _mx(v[:, cs])
            mixed = _dot(wsm, vg, NN) + bst[:, g:g + 1]
            dumg = dum_v[:, cs]
            du_parts.append(dumg * mixed)
            dmixed = dumg * u[:, cs]
            dmb = _mx(dmixed)
            dv_parts.append(_dot(wsm, dmb, TN))
            dws_ref[g] += _dot(dmb, vg, NT)
            dbst = jnp.where(lane == g, jnp.sum(dmixed, axis=1, keepdims=True), dbst)
        dbst_ref[...] += dbst
        du = jnp.concatenate(du_parts, axis=1)
        dv = jnp.concatenate(dv_parts, axis=1)
        dlng_ref[...] += jnp.sum(dv * xhat, axis=0, keepdims=True)
        dlnb_ref[...] += jnp.sum(dv, axis=0, keepdims=True)
        dxh = dv * lng_ref[...]
        dvp = rstd * (dxh - jnp.mean(dxh, axis=-1, keepdims=True) - xhat * jnp.mean(dxh * xhat, axis=-1, keepdims=True))
        dzz = jnp.concatenate([du, dvp], axis=1)
        dzin = dzz * (cdf + zin * (_INV_SQRT_2PI * jnp.exp(-0.5 * zin * zin)))
        dz_ref[...] = _mx(dzin)
        dbin_ref[...] += jnp.sum(dzin, axis=0, keepdims=True)

    return pl.pallas_call(
        body, name=name, grid=(T // C,),
        in_specs=[pl.BlockSpec((C, 2 * E), lambda n: (n, 0)), _full((1, 2 * E)), _full((1, E)), _full((1, E)),
                  _full((G, C, C)), _full((C, LANES)), pl.BlockSpec((C, E), lambda n: (n, 0))],
        out_specs=[pl.BlockSpec((C, 2 * E), lambda n: (n, 0)), _full((1, 2 * E)), _full((1, E)), _full((1, E)),
                   _full((G, C, C)), _full((C, LANES))],
        out_shape=[jax.ShapeDtypeStruct((T, 2 * E), MXU_DTYPE), jax.ShapeDtypeStruct((1, 2 * E), F32),
                   jax.ShapeDtypeStruct((1, E), F32), jax.ShapeDtypeStruct((1, E), F32),
                   jax.ShapeDtypeStruct((G, C, C), F32), jax.ShapeDtypeStruct((C, LANES), F32)],
        compiler_params=_cp("arbitrary"),
    )(zpre, b_in, ln_g, ln_b, w_s, b_s_t, dum)


def _row(v):
    return v.reshape(1, -1)


def _pad_lanes(v):
    v = v.reshape(1, -1)
    return jnp.pad(v, ((0, 0), (0, LANES - v.shape[1])))


def _local_step(x, target, p, weights_for, grads_ready=None, small_ready=None):
    ng = p["norm_g"]
    grads = {}
    dng = [[None] * 6 for _ in range(2)]
    order = [jnp.zeros((), F32)]

    def tell(group):
        zero = grads_ready(group, grads) if grads_ready is not None else None
        if zero is not None:
            order[0] = zero

    def gain(i, s):
        return _row(ng[i, s]) + order[0]

    def ffn_f(xin, i, j, tag, **tail):
        wt = weights_for("ffn" + tag, xin)
        xo, h, gu, y, *rest = _ffn_fwd(xin, _row(ng[i, 4 * j]), _row(ng[i, 4 * j + 1]), wt, "ffn_fwd_" + tag, **tail)
        return (xo, *rest), (xin, h, gu, y, wt)

    (x1, hn0), sv_f00 = ffn_f(x, 0, 0, "00", next_gain=_row(ng[0, 2]))
    dnw = weights_for("dn", x1)
    proj = _mm(hn0, dnw["dn_wqkvz"], "nn", "dn_proj")
    ba = _mm(hn0, dnw["dn_wba"], "nn", "dn_proj_ba")
    a_log = _pad_lanes(p["dn_a_log"])
    dt_bias = _pad_lanes(p["dn_dt_bias"])
    dn_ng = _row(p["dn_norm_g"])
    qkv = _dn_prep_fwd(proj, p["dn_conv_w"], "dn_prep_fwd")
    beta, gdec = _dn_gate_fwd(ba, a_log, dt_bias, "dn_gate_fwd")
    og, o_raw, tinv, s_all = _dn_scan_fwd(qkv, beta, gdec, proj, dn_ng, "dn_scan_fwd")
    m0, x2 = _out_proj_postnorm(og, dnw["dn_wout"], x1, _row(ng[0, 3]), "dn_out")
    (x3,), sv_f01 = ffn_f(x2, 0, 1, "01")
    (x4, hn1), sv_f10 = ffn_f(x3, 1, 0, "10", next_gain=_row(ng[1, 2]))
    sgw = weights_for("sg", x4)
    zpre = _mm(hn1, sgw["sg_win"], "nn", "sg_proj")
    sg_bin = _row(p["sg_b_in"])
    sg_lng = _row(p["sg_ln_g"])
    sg_lnb = _row(p["sg_ln_b"])
    sg_bst = jnp.pad(p["sg_b_s"].T, ((0, 0), (0, LANES - SG_GROUPS)))
    um = _sg_fwd(zpre, sg_bin, sg_lng, sg_lnb, p["sg_w_s"], sg_bst, "sg_fwd")
    m1, x5 = _out_proj_postnorm(um, sgw["sg_wout"], x4, _row(ng[1, 3]), "sg_out")
    (_, loss_part, dx), sv_f11 = ffn_f(x5, 1, 1, "11", target=target)

    def ffn_b(dxo, sv, i, j, tag, last=False):
        xin, h, gu, y, wt = sv
        dxi, dy, a, dgu, dg0, dg1 = _ffn_bwd(dxo, xin, y, gu, gain(i, 4 * j), gain(i, 4 * j + 1), wt, "ffn_bwd_" + tag)
        dng[i][4 * j] = dg0
        dng[i][4 * j + 1] = dg1
        after = None
        if last:
            grads["norm_g"] = jnp.stack([jnp.concatenate(dng[t], axis=0) for t in range(2)])
            after = small_ready(grads, loss_part) if small_ready is not None else None
        grads["wd" + tag] = _mm(a, dy, "tn", "ffn_wgrad_down_" + tag, after=after)
        grads["wguT" + tag] = _mm(dgu, h, "tn", "ffn_wgrad_up_" + tag, after=after)
        tell("ffn" + tag)
        return dxi

    dx = ffn_b(dx, sv_f11, 1, 1, "11")
    dm1, dng[1][3], dum = _postnorm_bwd_dgrad(dx, m1, gain(1, 3), sgw["sg_wout"], "sg_dgrad_out")
    grads["sg_w_out"] = _mm(um, dm1, "tn", "sg_wgrad_out")
    dz1, dbin, dlng, dlnb, dws, dbst = _sg_bwd(zpre, sg_bin, sg_lng, sg_lnb, p["sg_w_s"], sg_bst, dum, "sg_bwd")
    grads["sg_w_inT"] = _mm(dz1, hn1, "tn", "sg_wgrad_in")
    tell("sg")
    dx, dng[1][2] = _dgrad_prenorm_bwd(dz1, sgw["sg_win"], None, dx, x4, gain(1, 2), "sg_dgrad_in")
    grads["sg_b_in"] = dbin.reshape(1, -1)
    grads["sg_ln_g"] = dlng.reshape(1, -1)
    grads["sg_ln_b"] = dlnb.reshape(1, -1)
    grads["sg_w_s"] = jnp.where(jnp.tril(jnp.ones((SG_CHUNK, SG_CHUNK), bool)), dws, 0.0)[None]
    grads["sg_b_s"] = dbst[:, :SG_GROUPS].T[None]
    dx = ffn_b(dx, sv_f10, 1, 0, "10")
    dx = ffn_b(dx, sv_f01, 0, 1, "01")
    dm0, dng[0][3], dog = _postnorm_bwd_dgrad(dx, m0, gain(0, 3), dnw["dn_wout"], "dn_dgrad_out")
    grads["dn_w_out"] = _mm(og, dm0, "tn", "dn_wgrad_out")
    dqkv, dbeta, dgdec, dz0, dnng = _dn_scan_bwd(qkv, beta, gdec, proj, dn_ng, o_raw, tinv, s_all, dog, "dn_scan_bwd")
    dqkv_pre, dconv = _dn_prep_bwd(proj, p["dn_conv_w"], dqkv, "dn_prep_bwd")
    dba, dal, ddt = _dn_gate_bwd(ba, a_log, dt_bias, dbeta, dgdec, "dn_gate_bwd")
    W3 = 3 * DN_HEADS * DN_HEAD_DIM
    dw_qkv = _mm(hn0, dqkv_pre, "tn", "dn_wgrad_qkv")
    dw_z = _mm(hn0, dz0, "tn", "dn_wgrad_z")
    dw_ba = _mm(hn0, dba, "tn", "dn_wgrad_ba")
    grads["dn_w_in"] = jnp.concatenate(
        [dw_qkv, dw_z, dw_ba[:, :DN_HEADS], dw_ba[:, LANES:LANES + DN_HEADS]], axis=1)
    tell("dn")
    dh0 = _mm(dqkv_pre, dnw["dn_wqkvz"][:, :W3], "nt", "dn_dgrad_qkv")
    dh0 = _mm(dz0, dnw["dn_wqkvz"][:, W3:], "nt", "dn_dgrad_z", add=dh0)
    dx, dng[0][2] = _dgrad_prenorm_bwd(dba, dnw["dn_wba"], dh0, dx, x1, gain(0, 2), "dn_dgrad_ba")
    grads["dn_conv_w"] = dconv[None]
    grads["dn_a_log"] = dal[:, :DN_HEADS]
    grads["dn_dt_bias"] = ddt[:, :DN_HEADS]
    grads["dn_norm_g"] = dnng
    dx = ffn_b(dx, sv_f00, 0, 0, "00", last=True)
    return loss_part, dx, grads


def _mesh_pos():
    return lax.axis_index("x"), lax.axis_index("y"), lax.axis_index("c")


def _other_chips(x, y):
    return [(1 - x, y), (x, 1 - y), (1 - x, 1 - y)]


def _allgather_chips(arrs, name):
    n = len(arrs)

    def body(*refs):
        ins, outs = refs[:n], refs[n:2 * n]
        ici_send, ici_recv, d2d_send, d2d_recv = refs[2 * n:]
        x, y, c = _mesh_pos()
        me = 2 * x + y
        chips = _other_chips(x, y)
        sibling = (x, y, 1 - c)

        def ici(i, j, k):
            cx, cy = chips[j]
            return pltpu.make_async_remote_copy(src_ref=ins[i].at[c], dst_ref=outs[i].at[k, c], send_sem=ici_send.at[3 * i + j],
                                                recv_sem=ici_recv.at[3 * i + j], device_id=(cx, cy, c), device_id_type=MESH)

        def d2d(i, j, h):
            cx, cy = chips[j]
            slot = outs[i].at[2 * cx + cy, h]
            return pltpu.make_async_remote_copy(src_ref=slot, dst_ref=slot, send_sem=d2d_send.at[3 * i + j],
                                                recv_sem=d2d_recv.at[3 * i + j], device_id=sibling, device_id_type=MESH)

        sends = [ici(i, j, me) for i in range(n) for j in range(3)]
        for cp in sends:
            cp.start()
        for i in range(n):
            for j, (cx, cy) in enumerate(chips):
                ici(i, j, 2 * cx + cy).wait_recv()
                fwd = d2d(i, j, c)
                fwd.start()
                sends.append(fwd)
        for i in range(n):
            for j in range(3):
                d2d(i, j, 1 - c).wait_recv()
        for cp in sends:
            cp.wait_send()

    return pl.pallas_call(
        body, name=name, in_specs=[ANY] * n, out_specs=[ANY] * n,
        out_shape=[jax.ShapeDtypeStruct((N_CHIPS,) + a.shape, a.dtype) for a in arrs],
        scratch_shapes=[pltpu.SemaphoreType.DMA((3 * n,))] * 4,
    )(*arrs)


HBM = pl.BlockSpec(memory_space=pltpu.HBM)
SEM = pl.BlockSpec(memory_space=pltpu.SEMAPHORE)
TOKEN = jax.ShapeDtypeStruct((SUBLANES, LANES), F32)


_PEERS = {"gather": 3, "scatter": 3, "swap": 1, "all": N_DEV - 1}


def _land_shape(kind, shape):
    if kind == "gather":
        return (N_CHIPS,) + shape
    if kind == "all":
        return (N_DEV,) + shape
    return (N_CHIPS,) + shape[2:] if kind == "swap" else shape


def _peer_copies(kind, flags, src_refs, land_refs, send_sems, recv_sems, receiving):
    x, y, c = _mesh_pos()
    me4, me8 = 2 * x + y, 4 * x + 2 * y + c
    np_ = _PEERS[kind]
    cps = []
    for i, (src, land) in enumerate(zip(src_refs, land_refs)):
        if kind == "swap":
            half = src.at[1 - c] if flags[i] else src.at[:, 1 - c]
            plan = [((x, y, 1 - c), half, land)]
        elif kind == "all":
            masks = [(mx, my, mc) for mx in (0, 1) for my in (0, 1) for mc in (0, 1)][1:]
            peers = [(jnp.where(mx, 1 - x, x), jnp.where(my, 1 - y, y), jnp.where(mc, 1 - c, c)) for mx, my, mc in masks]
            plan = [(p, src, land.at[4 * p[0] + 2 * p[1] + p[2] if receiving else me8]) for p in peers]
        else:
            plan = []
            for cx, cy in _other_chips(x, y):
                k = 2 * cx + cy
                s = src.at[me4 if receiving else k] if kind == "scatter" else src
                plan.append(((cx, cy, c), s, land.at[k if receiving else me4]))
        for j, (peer, s, d) in enumerate(plan):
            cps.append(pltpu.make_async_remote_copy(src_ref=s, dst_ref=d, send_sem=send_sems.at[np_ * i + j],
                                                    recv_sem=recv_sems.at[np_ * i + j], device_id=peer, device_id_type=MESH))
    return cps


def _copies_start(kind, srcs, after, name, flags=None):
    n = len(srcs)
    ns = _PEERS[kind] * n
    lands = [lax.empty(_land_shape(kind, s.shape), s.dtype) for s in srcs]
    after = [] if after is None else [after]

    def body(*refs):
        src_refs, land_refs = refs[:n], refs[n:2 * n]
        send_sems, recv_sems = refs[2 * n + len(after)], refs[2 * n + len(after) + 1]
        token = refs[-1]
        for cp in _peer_copies(kind, flags, src_refs, land_refs, send_sems, recv_sems, False):
            cp.start()
        token[...] = jnp.zeros_like(token)

    outs = pl.pallas_call(
        body, name=name,
        in_specs=[HBM] * (2 * n) + [ANY] * len(after),
        out_specs=(SEM, SEM) + (HBM,) * (2 * n) + (pl.BlockSpec(memory_space=pltpu.VMEM),),
        out_shape=(pltpu.SemaphoreType.DMA((ns,)), pltpu.SemaphoreType.DMA((ns,)))
        + tuple(pltpu.HBM(a.shape, a.dtype) for a in list(srcs) + lands) + (TOKEN,),
        input_output_aliases={i: 2 + i for i in range(2 * n)},
        compiler_params=pltpu.CompilerParams(has_side_effects=pltpu.SideEffectType.DATAFLOW_SIDE_EFFECTING),
    )(*[pltpu.with_memory_space_constraint(a, pltpu.HBM) for a in list(srcs) + lands], *after)
    return dict(sems=outs[:2], srcs=outs[2:2 + n], lands=outs[2 + n:2 + 2 * n], token=outs[-1], kind=kind, flags=flags)


def _copies_wait(started, after, name):
    n = len(started["srcs"])
    kind, flags = started["kind"], started["flags"]
    after = list(after) if isinstance(after, (list, tuple)) else [after]

    def body(*refs):
        src_refs, land_refs = refs[:n], refs[n:2 * n]
        send_sems, recv_sems = refs[2 * n], refs[2 * n + 1]
        for cp in _peer_copies(kind, flags, src_refs, land_refs, send_sems, recv_sems, True):
            cp.wait_send()
            cp.wait_recv()

    outs = pl.pallas_call(
        body, name=name,
        in_specs=[HBM] * (2 * n) + [SEM, SEM] + [ANY] * len(after),
        out_specs=(HBM,) * (2 * n),
        out_shape=tuple(pltpu.HBM(a.shape, a.dtype) for a in list(started["srcs"]) + list(started["lands"])),
        input_output_aliases={i: i for i in range(2 * n)},
        compiler_params=pltpu.CompilerParams(has_side_effects=pltpu.SideEffectType.DATAFLOW_SIDE_EFFECTING),
    )(*started["srcs"], *started["lands"], *started["sems"], *after)
    return outs[:n], outs[n:]


def _swap_whole(arrs, name):
    n = len(arrs)

    def body(*refs):
        ins, outs = refs[:n], refs[n:2 * n]
        send_sems, recv_sems = refs[2 * n:]
        x, y, c = _mesh_pos()
        cps = [pltpu.make_async_remote_copy(src_ref=ins[i], dst_ref=outs[i], send_sem=send_sems.at[i],
                                            recv_sem=recv_sems.at[i], device_id=(x, y, 1 - c), device_id_type=MESH)
               for i in range(n)]
        for cp in cps:
            cp.start()
        for cp in cps:
            cp.wait()

    return pl.pallas_call(
        body, name=name, in_specs=[ANY] * n, out_specs=[ANY] * n,
        out_shape=[jax.ShapeDtypeStruct(a.shape, a.dtype) for a in arrs],
        scratch_shapes=[pltpu.SemaphoreType.DMA((n,)), pltpu.SemaphoreType.DMA((n,))],
    )(*arrs)


def _as_rows(a, lead):
    shp = a.shape
    rows = 1
    for s in shp[lead:-1]:
        rows *= s
    return a.reshape(shp[:lead] + (rows, shp[-1]))


def _row_tile(rows, cols, n_bufs):
    budget = (24 * 1024 * 1024) // (n_bufs * 2 * 4 * cols)
    return _pick(rows, max(2 * SUBLANES, budget), 2 * SUBLANES)


def _sum_devices(own, got, dev, name):
    n, rows, cols = got.shape
    tr = _row_tile(rows, cols, n + 2)

    def body(dev_ref, own_ref, got_ref, o_ref):
        mine = own_ref[...]
        acc = jnp.where(dev_ref[0] == 0, mine, got_ref[0])
        for k in range(1, n):
            acc = acc + jnp.where(dev_ref[0] == k, mine, got_ref[k])
        o_ref[...] = acc

    return pl.pallas_call(
        body, name=name,
        grid_spec=pltpu.PrefetchScalarGridSpec(
            num_scalar_prefetch=1, grid=(rows // tr,),
            in_specs=[pl.BlockSpec((tr, cols), lambda i, d: (i, 0)), pl.BlockSpec((n, tr, cols), lambda i, d: (0, i, 0))],
            out_specs=pl.BlockSpec((tr, cols), lambda i, d: (i, 0))),
        out_shape=jax.ShapeDtypeStruct((rows, cols), F32), compiler_params=_cp("parallel"),
    )(_scalar(dev), own, got)


def _scalar(i):
    return jnp.reshape(i, (1,)).astype(jnp.int32)


def _add_own_half(g, other, c, half_first, name):
    _, rows, cols = other.shape
    tr = _row_tile(rows, cols, 3)

    def body(c_ref, g_ref, o_ref, out_ref):
        out_ref[0] = (g_ref[0, 0] + o_ref[0]).astype(out_ref.dtype)

    if half_first:
        g_map = lambda k, i, c_ref: (c_ref[0], k, i, 0)
    else:
        g_map = lambda k, i, c_ref: (k, c_ref[0], i, 0)
    flat = pl.BlockSpec((1, tr, cols), lambda k, i, c_ref: (k, i, 0))
    return pl.pallas_call(
        body, name=name,
        grid_spec=pltpu.PrefetchScalarGridSpec(
            num_scalar_prefetch=1, grid=(N_CHIPS, rows // tr),
            in_specs=[pl.BlockSpec((1, 1, tr, cols), g_map), flat], out_specs=flat),
        out_shape=jax.ShapeDtypeStruct(other.shape, COMM_DTYPE), compiler_params=_cp("parallel", "parallel"),
    )(_scalar(c), g, other)


def _sum_chips(own, got, chip, name, transpose=False):
    _, rows, cols = own.shape
    tr = rows if transpose else _row_tile(rows, cols, N_CHIPS + 2)

    def body(chip_ref, p_ref, b_ref, o_ref):
        mine = p_ref[0].astype(F32)
        acc = jnp.where(chip_ref[0] == 0, mine, b_ref[0].astype(F32))
        for k in range(1, N_CHIPS):
            acc = acc + jnp.where(chip_ref[0] == k, mine, b_ref[k].astype(F32))
        o_ref[...] = acc.T if transpose else acc

    if transpose:
        out_spec, out_shape = pl.BlockSpec((cols, rows), lambda i, k_ref: (0, 0)), (cols, rows)
    else:
        out_spec, out_shape = pl.BlockSpec((tr, cols), lambda i, k_ref: (i, 0)), (rows, cols)
    return pl.pallas_call(
        body, name=name,
        grid_spec=pltpu.PrefetchScalarGridSpec(
            num_scalar_prefetch=1, grid=(rows // tr,),
            in_specs=[pl.BlockSpec((1, tr, cols), lambda i, k_ref: (k_ref[0], i, 0)),
                      pl.BlockSpec((N_CHIPS, tr, cols), lambda i, k_ref: (0, i, 0))],
            out_specs=out_spec),
        out_shape=jax.ShapeDtypeStruct(out_shape, F32), compiler_params=_cp("parallel"),
    )(_scalar(chip), own, got)


def _adam_math(w, g, m, v):
    nm = ADAM_B1 * m + (1.0 - ADAM_B1) * g
    nv = ADAM_B2 * v + (1.0 - ADAM_B2) * (g * g)
    m_hat = nm / (1.0 - ADAM_B1 ** ADAM_STEP)
    v_hat = nv / (1.0 - ADAM_B2 ** ADAM_STEP)
    return -ADAM_LR * (m_hat / (jnp.sqrt(v_hat) + ADAM_EPS) + ADAM_WD * w), nm, nv


def _adamw_pieces(w, m, v, mine, theirs, c, kind, name):
    shape = w.shape
    P = len(mine)
    ws, ms, vs = (t.reshape((P, -1, t.shape[-1])) for t in (w, m, v))
    _, R, C = ws.shape
    if kind == "rows":
        tr = _pick(R // 2, 512, SUBLANES)
    else:
        tr = _pick(R, 256 if kind in ("lo", "hi") else 512, SUBLANES)
    nt = R // tr
    nh = nt // 2

    def body(c_ref, w_ref, m_ref, v_ref, *refs):
        mine_refs, theirs_refs = refs[:P], refs[P:2 * P]
        g_ref, d_ref, nm_ref, nv_ref = refs[2 * P:]
        p, i, core = pl.program_id(0), pl.program_id(1), c_ref[0]

        def pick(refs_):
            out = refs_[0][...]
            for q in range(1, P):
                out = jnp.where(p == q, refs_[q][...], out)
            return out

        a, b = pick(mine_refs), pick(theirs_refs)
        if kind == "cols":
            gv = jnp.where(core == 0, jnp.concatenate([a, b], axis=1), jnp.concatenate([b, a], axis=1))
        else:
            own = {"lo": core == 0, "hi": core == 1, "rows": (i >= nh) == (core == 1)}[kind]
            gv = jnp.where(own, a, b)
        g_ref[0] = gv
        d_ref[0], nm_ref[0], nv_ref[0] = _adam_math(w_ref[0], gv, m_ref[0], v_ref[0])

    def piece_spec(q):
        tile = (lambda i: i - jnp.where(i >= nh, nh, 0)) if kind == "rows" else (lambda i: i)
        return pl.BlockSpec((tr, mine[q].shape[1]), lambda p, i, c_ref: (jnp.where(p == q, tile(i), 0), 0))

    full = pl.BlockSpec((1, tr, C), lambda p, i, c_ref: (p, i, 0))
    outs = pl.pallas_call(
        body, name=name,
        grid_spec=pltpu.PrefetchScalarGridSpec(num_scalar_prefetch=1, grid=(P, nt),
                                               in_specs=[full] * 3 + [piece_spec(q) for q in range(P)] * 2,
                                               out_specs=[full] * 4),
        out_shape=[jax.ShapeDtypeStruct((P, R, C), F32)] * 4, compiler_params=_cp("parallel", "arbitrary"),
    )(_scalar(c), ws, ms, vs, *mine, *theirs)
    return tuple(o.reshape(shape) for o in outs)


def _adamw(w, g, m, v, name):
    shape = w.shape
    ws, gs, ms, vs = (_as_rows(t, 0) for t in (w, g, m, v))
    rows, cols = ws.shape
    tr = _row_tile(rows, cols, 7)

    def body(w_ref, g_ref, m_ref, v_ref, d_ref, nm_ref, nv_ref):
        d_ref[...], nm_ref[...], nv_ref[...] = _adam_math(w_ref[...], g_ref[...], m_ref[...], v_ref[...])

    spec = pl.BlockSpec((tr, cols), lambda i: (i, 0))
    outs = pl.pallas_call(body, name=name, grid=(rows // tr,), in_specs=[spec] * 4, out_specs=[spec] * 3,
                          out_shape=[jax.ShapeDtypeStruct((rows, cols), F32)] * 3, compiler_params=_cp("parallel"))(ws, gs, ms, vs)
    return tuple(o.reshape(shape) for o in outs)


_BIG = ["ffn_w_gate", "ffn_w_up", "ffn_w_down", "dn_w_in", "dn_w_out", "sg_w_in", "sg_w_out"]
_SMALL_SHARDED = ["norm_g", "dn_conv_w", "sg_b_in", "sg_ln_g", "sg_ln_b"]
_SMALL_REPL = ["dn_a_log", "dn_dt_bias", "dn_norm_g", "sg_w_s", "sg_b_s"]
_WEIGHTS = ["norm_g", "ffn_w_gate", "ffn_w_up", "ffn_w_down", "dn_w_in", "dn_conv_w", "dn_a_log", "dn_dt_bias",
            "dn_norm_g", "dn_w_out", "sg_w_in", "sg_b_in", "sg_ln_g", "sg_ln_b", "sg_w_s", "sg_b_s", "sg_w_out"]
PACK_COLS = 1024


def _pack(arrs):
    flat = jnp.concatenate([a.reshape(-1) for a in arrs])
    pad = (-flat.shape[0]) % (SUBLANES * PACK_COLS)
    return jnp.pad(flat, (0, pad)).reshape(-1, PACK_COLS)


def _unpack(buf, shapes):
    flat = buf.reshape(-1)
    out, off = [], 0
    for s in shapes:
        n = math.prod(s)
        out.append(flat[off:off + n].reshape(s))
        off += n
    return out


def _as_halves(a):
    if a.shape[0] == 2:
        return a
    if a.shape[0] == 1:
        return a.reshape((2, a.shape[1] // 2) + a.shape[2:])
    return a.reshape((2, a.shape[0] // 2) + a.shape[1:])


def _with_own(gathered, own, chip):
    g = gathered.reshape((N_CHIPS,) + own.shape)
    return [jnp.where(chip == k, own, g[k]) for k in range(N_CHIPS)]


def _cat_shards(g, axis):
    return jnp.concatenate(list(g), axis=axis)


_GROUP_ORDER = ["ffn00", "dn", "ffn01", "ffn10", "sg", "ffn11"]


def _weight_groups(w):
    cast = {k: _mx(w[k]) for k in _BIG}
    groups = {"ffn%d%d" % (i, j): [cast["ffn_w_gate"][i, j].T, cast["ffn_w_up"][i, j].T, cast["ffn_w_down"][i, j]]
              for i, j in [(0, 0), (0, 1), (1, 0), (1, 1)]}
    groups["dn"] = [cast["dn_w_in"][0], cast["dn_w_out"][0]]
    groups["sg"] = [cast["sg_w_in"][0], cast["sg_w_out"][0]]
    return groups


def _ffn_weights(chip, own, gathered):
    pairs = [(a, g.reshape((N_CHIPS,) + a.shape)) for a, g in zip(own, gathered)]
    return {"chip": chip, "gate": pairs[0], "up": pairs[1], "down": pairs[2]}


def _group_matrices(group, shards):
    if group == "sg":
        return {"sg_win": _cat_shards(shards[0], 1), "sg_wout": _cat_shards(shards[1], 0)}
    dn_full = _cat_shards(shards[0], 1)
    W4 = 4 * DN_HEADS * DN_HEAD_DIM
    wba = jnp.zeros((D_MODEL, 2 * LANES), dn_full.dtype)
    wba = wba.at[:, :DN_HEADS].set(dn_full[:, W4:W4 + DN_HEADS])
    wba = wba.at[:, LANES:LANES + DN_HEADS].set(dn_full[:, W4 + DN_HEADS:])
    return {"dn_wqkvz": dn_full[:, :W4], "dn_wba": wba, "dn_wout": _cat_shards(shards[1], 0)}


def _split_cols(a, n):
    w = a.shape[-1] // n
    return [a[..., k * w:(k + 1) * w] for k in range(n)]


def _split_rows(a, n):
    h = a.shape[-2] // n
    return [a[..., k * h:(k + 1) * h, :] for k in range(n)]


_IJ = [(0, 0), (0, 1), (1, 0), (1, 1)]


def _group_grads(group, grads):
    def rows_by_chip(a):
        return a.reshape(N_CHIPS, 2, a.shape[0] // (2 * N_CHIPS), a.shape[1])

    if group.startswith("ffn"):
        tag = group[3:]
        t = grads["wguT" + tag]
        return (["wguT" + tag, "wd" + tag],
                [t.reshape(2, N_CHIPS, t.shape[0] // (2 * N_CHIPS), t.shape[1]), rows_by_chip(grads["wd" + tag])], [True, False])
    if group == "sg":
        return ["sg_w_inT", "sg_w_out"], [rows_by_chip(grads["sg_w_inT"]), rows_by_chip(grads["sg_w_out"])], [False, False]
    dn_in = jnp.stack([jnp.stack(_split_cols(hf, N_CHIPS)) for hf in _split_rows(grads["dn_w_in"], 2)])
    return ["dn_w_in", "dn_w_out"], [dn_in, rows_by_chip(grads["dn_w_out"])], [True, False]


_SHARD_PIECES = {
    "ffn_w_gate": (["wguT%d%d" % ij for ij in _IJ], "lo"),
    "ffn_w_up": (["wguT%d%d" % ij for ij in _IJ], "hi"),
    "ffn_w_down": (["wd%d%d" % ij for ij in _IJ], "rows"),
    "dn_w_in": (["dn_w_in"], "rows"),
    "dn_w_out": (["dn_w_out"], "rows"),
    "sg_w_in": (["sg_w_inT"], "cols"),
    "sg_w_out": (["sg_w_out"], "rows"),
}


def kernel(x, norm_g, ffn_w_gate, ffn_w_up, ffn_w_down, dn_w_in, dn_conv_w, dn_a_log, dn_dt_bias, dn_norm_g, dn_w_out, sg_w_in, sg_b_in, sg_ln_g, sg_ln_b, sg_w_s, sg_b_s, sg_w_out, loss_target, m_norm_g, m_ffn_w_gate, m_ffn_w_up, m_ffn_w_down, m_dn_w_in, m_dn_conv_w, m_dn_a_log, m_dn_dt_bias, m_dn_norm_g, m_dn_w_out, m_sg_w_in, m_sg_b_in, m_sg_ln_g, m_sg_ln_b, m_sg_w_s, m_sg_b_s, m_sg_w_out, v_norm_g, v_ffn_w_gate, v_ffn_w_up, v_ffn_w_down, v_dn_w_in, v_dn_conv_w, v_dn_a_log, v_dn_dt_bias, v_dn_norm_g, v_dn_w_out, v_sg_w_in, v_sg_b_in, v_sg_ln_g, v_sg_ln_b, v_sg_w_s, v_sg_b_s, v_sg_w_out):
    args = dict(locals())
    w = {k: args[k] for k in _WEIGHTS}
    mom = {k: args["m_" + k] for k in _WEIGHTS}
    var = {k: args["v_" + k] for k in _WEIGHTS}
    cx, cy, cc = _mesh_pos()
    chip = 2 * cx + cy

    small_shapes = [w[k].shape for k in _SMALL_SHARDED]
    groups = _weight_groups(w)
    own = groups[_GROUP_ORDER[0]] + [_pack([w[k] for k in _SMALL_SHARDED])]
    first = _allgather_chips([_as_halves(a) for a in own], "gather_first")
    started, after = {}, first[0]
    for g in _GROUP_ORDER[1:]:
        started[g] = _copies_start("gather", groups[g], after, "gather_start_" + g)
        after = started[g]["token"]
    small_k = [_unpack(pack, small_shapes) for pack in _with_own(first[-1], own[-1], chip)]
    p = {name: jnp.concatenate([small_k[k][i] for k in range(N_CHIPS)], axis=-1) for i, name in enumerate(_SMALL_SHARDED)}
    p = {k: (v if k == "norm_g" else v[0]) for k, v in p.items()}
    p["norm_g"] = p["norm_g"] + after[0, 0]
    for k in _SMALL_REPL:
        p[k] = w[k][0]

    def weights_for(group, after):
        if group == _GROUP_ORDER[0]:
            return _ffn_weights(chip, own[:-1], first[:-1])
        srcs, lands = _copies_wait(started[group], after, "gather_wait_" + group)
        if group.startswith("ffn"):
            return _ffn_weights(chip, srcs, lands)
        return _group_matrices(group, [_with_own(l, a, chip) for l, a in zip(lands, srcs)])

    mine, theirs, to_core, to_chips = {}, {}, [], []

    def send_to_chips(after):
        group, names, flags, swap = to_core.pop(0)
        halves, got = _copies_wait(swap, after, "swap_wait_" + group)
        pair_sum = [_add_own_half(h, o, cc, hf, "pair_sum_" + n) for n, h, o, hf in zip(names, halves, got, flags)]
        scatter = _copies_start("scatter", pair_sum, got[0], "reduce_start_" + group)
        to_chips.append((group, names, scatter))
        return scatter["token"]

    def finish(after):
        group, names, scatter = to_chips.pop(0)
        pair_sum, got = _copies_wait(scatter, after, "reduce_wait_" + group)
        half_sum = [_sum_chips(a, b, chip, "chip_sum_" + n, transpose=n == "sg_w_inT")
                    for n, a, b in zip(names, pair_sum, got)]
        other = _swap_whole(half_sum, "gather_core_pair_" + group)
        mine.update(zip(names, half_sum))
        theirs.update(zip(names, other))

    def grads_ready(group, grads):
        names, halves, flags = _group_grads(group, grads)
        swap = _copies_start("swap", halves, None, "swap_start_" + group, flags)
        token = swap["token"]
        if to_core:
            token = send_to_chips(token)
            if len(to_chips) > 1:
                finish(token)
        to_core.append((group, names, flags, swap))
        return token[0, 0]

    small_names = _SMALL_SHARDED + _SMALL_REPL
    small = {}

    def small_ready(grads, loss_part):
        parts = [grads[k] for k in small_names]
        small["shapes"] = [g.shape for g in parts] + [(1,)]
        pack = _pack(parts + [loss_part[0, :1]])
        small["exchange"] = _copies_start("all", [pack], None, "small_start")
        return small["exchange"]["token"]

    loss_part, grad_x, grads = _local_step(x[0], loss_target[0], p, weights_for, grads_ready, small_ready)
    token = send_to_chips(to_core[0][3]["token"])
    finish(token)
    (pack,), (packs,) = _copies_wait(small["exchange"], list(theirs.values()), "small_wait")
    summed = _sum_devices(pack, packs, 4 * cx + 2 * cy + cc, "small_sum")
    parts = _unpack(summed, small["shapes"])
    loss = parts[-1][0]
    grad = {}
    for i, k in enumerate(small_names):
        g = parts[i]
        if k in _SMALL_SHARDED:
            n = w[k].shape[-1]
            g = lax.dynamic_slice_in_dim(g, chip * n, n, axis=g.ndim - 1)
        grad[k] = g

    delta, new_m, new_v = {}, {}, {}

    def update(keys):
        for k in keys:
            names, kind = _SHARD_PIECES[k]
            turn = (lambda a: jnp.swapaxes(a, -1, -2)) if names[0].startswith("wguT") else (lambda a: a)
            outs = _adamw_pieces(turn(w[k]), turn(mom[k]), turn(var[k]), [mine[n] for n in names], [theirs[n] for n in names],
                                 cc, kind, "adamw_" + k)
            grad[k], delta[k], new_m[k], new_v[k] = (turn(o) for o in outs)

    shapes = [w[k].shape for k in small_names]
    d, nm, nv = _adamw(_pack([w[k] for k in small_names]), _pack([grad[k] for k in small_names]),
                       _pack([mom[k] for k in small_names]), _pack([var[k] for k in small_names]), "adamw_small")
    for k, a, b, c_ in zip(small_names, _unpack(d, shapes), _unpack(nm, shapes), _unpack(nv, shapes)):
        delta[k], new_m[k], new_v[k] = a, b, c_
    mixers = [k for k in _BIG if not k.startswith("ffn")]
    update(mixers)
    finish([d] + [delta[k] for k in mixers] + list(theirs.values()))
    update([k for k in _BIG if k.startswith("ffn")])

    return (loss, grad_x[None], *[grad[k] for k in _WEIGHTS], *[delta[k] for k in _WEIGHTS],
            *[new_m[k] for k in _WEIGHTS], *[new_v[k] for k in _WEIGHTS])
```

```python
import math

import jax
import jax.numpy as jnp
from jax import lax
from jax.experimental import pallas as pl
from jax.experimental.pallas import tpu as pltpu

F32 = jnp.float32
MXU_DTYPE = jnp.bfloat16
COMM_DTYPE = jnp.bfloat16
HI = lax.Precision.HIGHEST
TRI_PREC = lax.Precision.HIGH

D_MODEL = 1024
RMS_EPS = 1e-6
LN_EPS = 1e-5
L2_EPS = 1e-6
DN_HEADS = 8
DN_HEAD_DIM = 128
DN_CONV = 4
DN_CHUNK = 64
SG_WIDTH = 2048
SG_GROUPS = 8
SG_CHUNK = 128
SG_GROUP_W = SG_WIDTH // SG_GROUPS
N_CHIPS = 4
N_DEV = 8
LANES = 128
SUBLANES = 8
VMEM_LIMIT = 56 * 1024 * 1024

ADAM_LR = 0.001
ADAM_B1 = 0.9
ADAM_B2 = 0.999
ADAM_EPS = 1e-08
ADAM_WD = 0.01
ADAM_STEP = 10

MESH = pl.DeviceIdType.MESH
ANY = pl.BlockSpec(memory_space=pl.ANY)


def _cp(*sem):
    return pltpu.CompilerParams(dimension_semantics=sem, vmem_limit_bytes=VMEM_LIMIT)


def _pick(n, pref, mult=LANES):
    best = None
    d = mult
    while d <= min(n, pref):
        if n % d == 0:
            best = d
        d += mult
    return best if best is not None else n


def _full(shape):
    nd = len(shape)
    return pl.BlockSpec(shape, lambda *_: (0,) * nd)


def _sigmoid(x):
    return 1.0 / (1.0 + jnp.exp(-x))


def _dot(a, b, dims, prec=None):
    return lax.dot_general(a, b, (dims, ((), ())), preferred_element_type=F32, precision=prec)


NN = ((1,), (0,))
NT = ((1,), (1,))
TN = ((0,), (0,))


def _mx(a):
    return a.astype(MXU_DTYPE)


def _rms_stat(x):
    return lax.rsqrt(jnp.mean(x * x, axis=-1, keepdims=True) + RMS_EPS)


def _rms_bwd(x, r, g, dy):
    xh = x * r
    dxh = dy * g
    dx = r * (dxh - xh * jnp.mean(dxh * xh, axis=-1, keepdims=True))
    return dx, jnp.sum(dy * xh, axis=0, keepdims=True)


def _mm(a, b, mode, name, out_dtype=F32, add=None, after=None):
    if mode == "tn":
        K, M = a.shape
        N = b.shape[1]
    elif mode == "nt":
        M, K = a.shape
        N = b.shape[0]
    else:
        M, K = a.shape
        N = b.shape[1]
    tn = _pick(N, 1024)
    if mode == "tn":
        tm = _pick(M, 1024 if tn <= 512 else 1408)
        tk = _pick(K, 2048, SUBLANES)
    else:
        tm = _pick(M, max(512, min(2048, (1024 * 1024) // tn)), SUBLANES)
        tk = _pick(K, 2048)
    nk = K // tk
    grid = (N // tn, M // tm, nk)
    if mode == "nn":
        a_spec = pl.BlockSpec((tm, tk), lambda j, i, k: (i, k))
        b_spec = pl.BlockSpec((tk, tn), lambda j, i, k: (k, j))
        dims = NN
    elif mode == "nt":
        a_spec = pl.BlockSpec((tm, tk), lambda j, i, k: (i, k))
        b_spec = pl.BlockSpec((tn, tk), lambda j, i, k: (j, k))
        dims = NT
    else:
        a_spec = pl.BlockSpec((tk, tm), lambda j, i, k: (k, i))
        b_spec = pl.BlockSpec((tk, tn), lambda j, i, k: (k, j))
        dims = TN
    o_spec = pl.BlockSpec((tm, tn), lambda j, i, k: (i, j))
    has_add = add is not None

    def body(*refs):
        a_ref, b_ref = refs[:2]
        add_ref = refs[2] if has_add else None
        o_ref, acc = refs[-2:]
        k = pl.program_id(2)

        @pl.when(k == 0)
        def _():
            acc[...] = add_ref[...] if has_add else jnp.zeros_like(acc)

        acc[...] += _dot(a_ref[...], b_ref[...], dims)

        @pl.when(k == nk - 1)
        def _():
            o_ref[...] = acc[...].astype(o_ref.dtype)

    ins = [a, b] + ([add] if has_add else []) + ([after] if after is not None else [])
    specs = [a_spec, b_spec] + ([o_spec] if has_add else []) + ([ANY] if after is not None else [])
    return pl.pallas_call(
        body, name=name, grid=grid, in_specs=specs, out_specs=o_spec,
        out_shape=jax.ShapeDtypeStruct((M, N), out_dtype),
        scratch_shapes=[pltpu.VMEM((tm, tn), F32)],
        compiler_params=_cp("parallel", "parallel", "arbitrary"),
    )(*ins)


def _ffn_weight_operands(wt):
    return [_scalar(wt["chip"])] , [wt["gate"][0], wt["gate"][1], wt["up"][0], wt["up"][1], wt["down"][0], wt["down"][1]]


def _load_ffn_weights(chip_ref, shard_refs, wgu_v, wd_v, sem, first):
    fs = wd_v.shape[0] // N_CHIPS

    @pl.when(pl.program_id(0) == 0)
    def _():
        me = chip_ref[0]
        waits = []
        for t, (dst, base) in enumerate([(wgu_v, 0), (wgu_v, wd_v.shape[0]), (wd_v, 0)]):
            own, gathered = shard_refs[2 * t], shard_refs[2 * t + 1]
            for k in range(N_CHIPS):
                slot = dst.at[pl.ds(base + k * fs, fs), :]
                s = sem.at[t * N_CHIPS + k]

                @pl.when(me == k)
                def _(own=own, slot=slot, s=s):
                    pltpu.make_async_copy(own, slot, s).start()

                @pl.when(me != k)
                def _(gathered=gathered, k=k, slot=slot, s=s):
                    pltpu.make_async_copy(gathered.at[k], slot, s).start()

                waits.append(pltpu.make_async_copy(own, slot, s))
        for cp in waits[:2 * N_CHIPS] if first == "gate_up" else waits[2 * N_CHIPS:]:
            cp.wait()


def _await_ffn_weights(shard_refs, wgu_v, wd_v, sem, which):
    fs = wd_v.shape[0] // N_CHIPS

    @pl.when(pl.program_id(0) == 0)
    def _():
        plan = [(wgu_v, 0, 0), (wgu_v, wd_v.shape[0], 1)] if which == "gate_up" else [(wd_v, 0, 2)]
        for dst, base, t in plan:
            for k in range(N_CHIPS):
                pltpu.make_async_copy(shard_refs[2 * t], dst.at[pl.ds(base + k * fs, fs), :], sem.at[t * N_CHIPS + k]).wait()


def _ffn_fwd(x, g0, g1, wt, name, next_gain=None, target=None):
    T, D = x.shape
    F = N_CHIPS * wt["down"][0].shape[0]
    F2 = 2 * F
    tm = _pick(T, 256, SUBLANES)
    prefetch, shards = _ffn_weight_operands(wt)
    extra = [a for a in (next_gain, target) if a is not None]
    n_tail = 1 if next_gain is not None else (2 if target is not None else 0)

    def body(chip_ref, x_ref, g0_ref, g1_ref, *refs):
        extra_ref = refs[0] if extra else None
        refs = refs[len(extra):]
        shard_refs = refs[:6]
        xo_ref, h_ref, gu_ref, y_ref = refs[6:10]
        tail_refs = refs[10:10 + n_tail]
        wgu_v, wd_v, sem = refs[10 + n_tail:]
        _load_ffn_weights(chip_ref, shard_refs, wgu_v, wd_v, sem, "gate_up")
        xv = x_ref[...]
        hb = _mx(xv * _rms_stat(xv) * g0_ref[...])
        h_ref[...] = hb
        gu = _dot(hb, wgu_v[...], NT)
        gu_ref[...] = gu.astype(gu_ref.dtype)
        g = gu[:, :F]
        u = gu[:, F:]
        a = _mx(g * _sigmoid(g) * u)
        _await_ffn_weights(shard_refs, wgu_v, wd_v, sem, "down")
        y = _dot(a, wd_v[...], NN)
        y_ref[...] = y
        xo = xv + 0.5 * (y * _rms_stat(y) * g1_ref[...])
        xo_ref[...] = xo
        if next_gain is not None:
            tail_refs[0][...] = _mx(xo * _rms_stat(xo) * extra_ref[...])
        if target is not None:
            loss_ref, dy_ref = tail_refs

            @pl.when(pl.program_id(0) == 0)
            def _():
                loss_ref[...] = jnp.zeros_like(loss_ref)

            e = xo - extra_ref[...]
            dy_ref[...] = e * (1.0 / D)
            loss_ref[...] += 0.5 * jnp.sum(jnp.mean(e * e, axis=-1, keepdims=True), axis=0, keepdims=True)

    row = lambda w: pl.BlockSpec((tm, w), lambda i, c: (i, 0))
    one = pl.BlockSpec((1, D), lambda i, c: (0, 0))
    tail_specs, tail_shapes, extra_specs = [], [], []
    if next_gain is not None:
        extra_specs, tail_specs, tail_shapes = [one], [row(D)], [jax.ShapeDtypeStruct((T, D), MXU_DTYPE)]
    if target is not None:
        extra_specs = [row(D)]
        tail_specs = [pl.BlockSpec((SUBLANES, LANES), lambda i, c: (0, 0)), row(D)]
        tail_shapes = [jax.ShapeDtypeStruct((SUBLANES, LANES), F32), jax.ShapeDtypeStruct((T, D), F32)]
    return pl.pallas_call(
        body, name=name,
        grid_spec=pltpu.PrefetchScalarGridSpec(
            num_scalar_prefetch=1, grid=(T // tm,),
            in_specs=[row(D), one, one] + extra_specs + [ANY] * 6,
            out_specs=[row(D), row(D), row(F2), row(D)] + tail_specs,
            scratch_shapes=[pltpu.VMEM((F2, D), MXU_DTYPE), pltpu.VMEM((F, D), MXU_DTYPE),
                            pltpu.SemaphoreType.DMA((3 * N_CHIPS,))]),
        out_shape=[jax.ShapeDtypeStruct((T, D), F32), jax.ShapeDtypeStruct((T, D), MXU_DTYPE),
                   jax.ShapeDtypeStruct((T, F2), MXU_DTYPE), jax.ShapeDtypeStruct((T, D), F32)] + tail_shapes,
        compiler_params=_cp("arbitrary"),
    )(*prefetch, x, g0, g1, *extra, *shards)


FFN_BWD_CHUNK = 2816


def _ffn_bwd(dxo, x, y, gu, g0, g1, wt, name):
    T, D = x.shape
    F2 = gu.shape[1]
    F = F2 // 2
    tm = _pick(T, 256, SUBLANES)
    fc = _pick(F, FFN_BWD_CHUNK)
    prefetch, shards = _ffn_weight_operands(wt)

    def body(chip_ref, dxo_ref, x_ref, y_ref, gu_ref, g0_ref, g1_ref, *refs):
        shard_refs = refs[:6]
        dx_ref, dy_ref, a_ref, dgu_ref, dg0_ref, dg1_ref, wgu_v, wd_v, sem = refs[6:]
        _load_ffn_weights(chip_ref, shard_refs, wgu_v, wd_v, sem, "down")

        @pl.when(pl.program_id(0) == 0)
        def _():
            dg0_ref[...] = jnp.zeros_like(dg0_ref)
            dg1_ref[...] = jnp.zeros_like(dg1_ref)

        dxo_v = dxo_ref[...]
        yv = y_ref[...]
        dy, dg1 = _rms_bwd(yv, _rms_stat(yv), g1_ref[...], 0.5 * dxo_v)
        dg1_ref[...] += dg1
        dyb = _mx(dy)
        dy_ref[...] = dyb
        dh = jnp.zeros((tm, D), F32)
        for c in range(F // fc):
            lo, hi = c * fc, (c + 1) * fc
            da = _dot(dyb, wd_v[lo:hi, :], NT)
            g = gu_ref[:, lo:hi].astype(F32)
            u = gu_ref[:, F + lo:F + hi].astype(F32)
            s = _sigmoid(g)
            sg = g * s
            a_ref[:, lo:hi] = _mx(sg * u)
            dg = _mx(da * u * (s * (1.0 + g * (1.0 - s))))
            du = _mx(da * sg)
            dgu_ref[:, lo:hi] = dg
            dgu_ref[:, F + lo:F + hi] = du
            if c == 0:
                _await_ffn_weights(shard_refs, wgu_v, wd_v, sem, "gate_up")
            dh = dh + _dot(dg, wgu_v[lo:hi, :], NN) + _dot(du, wgu_v[F + lo:F + hi, :], NN)
        xv = x_ref[...]
        dx, dg0 = _rms_bwd(xv, _rms_stat(xv), g0_ref[...], dh)
        dg0_ref[...] += dg0
        dx_ref[...] = dxo_v + dx

    row = lambda w: pl.BlockSpec((tm, w), lambda i, c: (i, 0))
    one = pl.BlockSpec((1, D), lambda i, c: (0, 0))
    return pl.pallas_call(
        body, name=name,
        grid_spec=pltpu.PrefetchScalarGridSpec(
            num_scalar_prefetch=1, grid=(T // tm,),
            in_specs=[row(D), row(D), row(D), row(F2), one, one] + [ANY] * 6,
            out_specs=[row(D), row(D), row(F), row(F2), one, one],
            scratch_shapes=[pltpu.VMEM((F2, D), MXU_DTYPE), pltpu.VMEM((F, D), MXU_DTYPE),
                            pltpu.SemaphoreType.DMA((3 * N_CHIPS,))]),
        out_shape=[jax.ShapeDtypeStruct((T, D), F32), jax.ShapeDtypeStruct((T, D), MXU_DTYPE),
                   jax.ShapeDtypeStruct((T, F), MXU_DTYPE), jax.ShapeDtypeStruct((T, F2), MXU_DTYPE),
                   jax.ShapeDtypeStruct((1, D), F32), jax.ShapeDtypeStruct((1, D), F32)],
        compiler_params=_cp("arbitrary"),
    )(*prefetch, dxo, x, y, gu, g0, g1, *shards)


def _out_proj_postnorm(a, b, x, g, name):
    T, K = a.shape
    D = b.shape[1]
    tm = _pick(T, 1024, SUBLANES)

    def body(a_ref, b_ref, x_ref, g_ref, m_ref, o_ref):
        mv = _dot(a_ref[...], b_ref[...], NN)
        m_ref[...] = mv
        o_ref[...] = x_ref[...] + mv * _rms_stat(mv) * g_ref[...]

    row = lambda w: pl.BlockSpec((tm, w), lambda i: (i, 0))
    return pl.pallas_call(body, name=name, grid=(T // tm,),
                          in_specs=[row(K), _full((K, D)), row(D), _full((1, D))], out_specs=[row(D), row(D)],
                          out_shape=[jax.ShapeDtypeStruct((T, D), F32)] * 2, compiler_params=_cp("parallel"))(a, b, x, g)


def _postnorm_bwd_dgrad(dxo, m, g, b, name):
    T, D = m.shape
    K = b.shape[0]
    tm = _pick(T, 1024, SUBLANES)

    def body(dxo_ref, m_ref, g_ref, b_ref, dm_ref, dg_ref, da_ref):
        @pl.when(pl.program_id(0) == 0)
        def _():
            dg_ref[...] = jnp.zeros_like(dg_ref)

        mv = m_ref[...]
        dm, dg = _rms_bwd(mv, _rms_stat(mv), g_ref[...], dxo_ref[...])
        dg_ref[...] += dg
        dmb = _mx(dm)
        dm_ref[...] = dmb
        da_ref[...] = _dot(dmb, b_ref[...], NT)

    row = lambda w: pl.BlockSpec((tm, w), lambda i: (i, 0))
    return pl.pallas_call(body, name=name, grid=(T // tm,), in_specs=[row(D), row(D), _full((1, D)), _full((K, D))],
                          out_specs=[row(D), _full((1, D)), row(K)],
                          out_shape=[jax.ShapeDtypeStruct((T, D), MXU_DTYPE), jax.ShapeDtypeStruct((1, D), F32),
                                     jax.ShapeDtypeStruct((T, K), F32)],
                          compiler_params=_cp("arbitrary"))(dxo, m, g, b)


def _dgrad_prenorm_bwd(a, b, add, dxo, x, g, name):
    T, K = a.shape
    D = b.shape[0]
    tm = _pick(T, 1024, SUBLANES)
    tk = _pick(K, 2048)
    nk = K // tk
    has_add = add is not None

    def body(*refs):
        a_ref, b_ref = refs[:2]
        add_ref = refs[2] if has_add else None
        dxo_ref, x_ref, g_ref, dx_ref, dg_ref, acc = refs[-6:]
        i, k = pl.program_id(0), pl.program_id(1)

        @pl.when((i == 0) & (k == 0))
        def _():
            dg_ref[...] = jnp.zeros_like(dg_ref)

        @pl.when(k == 0)
        def _():
            acc[...] = add_ref[...] if has_add else jnp.zeros_like(acc)

        acc[...] += _dot(a_ref[...], b_ref[...], NT)

        @pl.when(k == nk - 1)
        def _():
            xv = x_ref[...]
            dx, dg = _rms_bwd(xv, _rms_stat(xv), g_ref[...], acc[...])
            dg_ref[...] += dg
            dx_ref[...] = dxo_ref[...] + dx

    row = pl.BlockSpec((tm, D), lambda i, k: (i, 0))
    one = pl.BlockSpec((1, D), lambda i, k: (0, 0))
    ins = [a, b] + ([add] if has_add else []) + [dxo, x, g]
    specs = ([pl.BlockSpec((tm, tk), lambda i, k: (i, k)), pl.BlockSpec((D, tk), lambda i, k: (0, k))]
             + ([row] if has_add else []) + [row, row, one])
    return pl.pallas_call(body, name=name, grid=(T // tm, nk), in_specs=specs, out_specs=[row, one],
                          out_shape=[jax.ShapeDtypeStruct((T, D), F32), jax.ShapeDtypeStruct((1, D), F32)],
                          scratch_shapes=[pltpu.VMEM((tm, D), F32)],
                          compiler_params=_cp("arbitrary", "arbitrary"))(*ins)


DN_ROWS = 512


def _shift_down(prev8, cur, s):
    n = cur.shape[0]
    xx = jnp.concatenate([prev8, cur], axis=0)
    return pltpu.roll(xx, s, 0)[SUBLANES:SUBLANES + n, :]


def _shift_up(cur, next8, s):
    n = cur.shape[0]
    xx = jnp.concatenate([cur, next8], axis=0)
    return pltpu.roll(xx, n + SUBLANES - s, 0)[:n, :]


def _tile_start(r, rows):
    return r * rows if isinstance(r, int) else pl.multiple_of(r * rows, SUBLANES)


def _conv_tile(x_ref, w, r, rows):
    start = _tile_start(r, rows)
    cur = x_ref[pl.ds(start, rows), :]
    if isinstance(r, int):
        prev8 = jnp.zeros((SUBLANES, cur.shape[1]), cur.dtype)
        taps = [_shift_down(prev8, cur, DN_CONV - 1 - j) if j < DN_CONV - 1 else cur for j in range(DN_CONV)]
    else:
        taps = [x_ref[pl.ds(start - (DN_CONV - 1 - j), rows), :] if j < DN_CONV - 1 else cur for j in range(DN_CONV)]
    c = taps[0] * w[0:1, :]
    for j in range(1, DN_CONV):
        c = c + taps[j] * w[j:j + 1, :]
    return c, taps


def _dn_prep_fwd(proj, conv_w, name):
    T = proj.shape[0]
    W = DN_HEADS * DN_HEAD_DIM
    rows = min(DN_ROWS, T)
    n_inner = T // rows
    scale = DN_HEAD_DIM ** -0.5

    def body(x_ref, w_ref, o_ref):
        cb = pl.program_id(0)
        w = w_ref[...]
        is_qk = cb < 2 * DN_HEADS
        post = jnp.where(cb < DN_HEADS, scale, 1.0)

        def step(r, carry):
            c, _ = _conv_tile(x_ref, w, r, rows)
            s = c * _sigmoid(c)
            rinv = lax.rsqrt(jnp.sum(s * s, axis=-1, keepdims=True) + L2_EPS)
            o_ref[pl.ds(_tile_start(r, rows), rows), :] = jnp.where(is_qk, s * rinv * post, s)
            return carry

        step(0, 0)
        lax.fori_loop(1, n_inner, step, 0)

    col = pl.BlockSpec((T, LANES), lambda j: (0, j))
    return pl.pallas_call(body, name=name, grid=(3 * W // LANES,),
                          in_specs=[col, pl.BlockSpec((DN_CONV, LANES), lambda j: (0, j))], out_specs=col,
                          out_shape=jax.ShapeDtypeStruct((T, 3 * W), F32), compiler_params=_cp("parallel"))(proj, conv_w)


def _dn_prep_bwd(proj, conv_w, dqkv, name):
    T = proj.shape[0]
    W = DN_HEADS * DN_HEAD_DIM
    rows = min(DN_ROWS, T)
    n_inner = T // rows
    scale = DN_HEAD_DIM ** -0.5

    def body(x_ref, w_ref, dy_ref, dx_ref, dw_ref, dc_scr):
        cb = pl.program_id(0)
        w = w_ref[...]
        is_qk = cb < 2 * DN_HEADS
        post = jnp.where(cb < DN_HEADS, scale, 1.0)

        def step1(r, dws):
            c, taps = _conv_tile(x_ref, w, r, rows)
            sg = _sigmoid(c)
            s = c * sg
            rinv = lax.rsqrt(jnp.sum(s * s, axis=-1, keepdims=True) + L2_EPS)
            dy = dy_ref[pl.ds(_tile_start(r, rows), rows), :]
            yn = s * rinv
            dyn = dy * post
            ds_qk = rinv * (dyn - yn * jnp.sum(dyn * yn, axis=-1, keepdims=True))
            ds = jnp.where(is_qk, ds_qk, dy)
            dc = ds * (sg * (1.0 + c * (1.0 - sg)))
            dc_scr[pl.ds(_tile_start(r, rows), rows), :] = dc
            return tuple(dws[j] + jnp.sum(dc * taps[j], axis=0, keepdims=True) for j in range(DN_CONV))

        zero = jnp.zeros((1, LANES), F32)
        dws = lax.fori_loop(1, n_inner, step1, step1(0, (zero,) * DN_CONV))
        for j in range(DN_CONV):
            dw_ref[j:j + 1, :] = dws[j]

        def step2(r, carry):
            start = _tile_start(r, rows)
            cur = dc_scr[pl.ds(start, rows), :]
            dx = cur * w[DN_CONV - 1:DN_CONV, :]
            for j in range(DN_CONV - 1):
                s = DN_CONV - 1 - j
                if isinstance(r, int):
                    up = _shift_up(cur, jnp.zeros((SUBLANES, LANES), F32), s)
                else:
                    up = dc_scr[pl.ds(start + s, rows), :]
                dx = dx + up * w[j:j + 1, :]
            dx_ref[pl.ds(start, rows), :] = _mx(dx)
            return carry

        lax.fori_loop(0, n_inner - 1, step2, 0)
        step2(n_inner - 1, 0)

    col = pl.BlockSpec((T, LANES), lambda j: (0, j))
    wspec = pl.BlockSpec((DN_CONV, LANES), lambda j: (0, j))
    return pl.pallas_call(body, name=name, grid=(3 * W // LANES,), in_specs=[col, wspec, col], out_specs=[col, wspec],
                          out_shape=[jax.ShapeDtypeStruct((T, 3 * W), MXU_DTYPE), jax.ShapeDtypeStruct((DN_CONV, 3 * W), F32)],
                          scratch_shapes=[pltpu.VMEM((T, LANES), F32)], compiler_params=_cp("parallel"))(proj, conv_w, dqkv)


def _softplus(x):
    return jnp.maximum(x, 0.0) + jnp.log(1.0 + jnp.exp(-jnp.abs(x)))


def _dn_gate_fwd(ba, a_log, dt_bias, name):
    T = ba.shape[0]
    tm = _pick(T, 1024, SUBLANES)

    def body(ba_ref, al_ref, dt_ref, beta_ref, g_ref):
        beta_ref[...] = _sigmoid(ba_ref[:, :LANES])
        g_ref[...] = -jnp.exp(al_ref[...]) * _softplus(ba_ref[:, LANES:] + dt_ref[...])

    row = lambda w: pl.BlockSpec((tm, w), lambda i: (i, 0))
    return pl.pallas_call(body, name=name, grid=(T // tm,), in_specs=[row(2 * LANES), _full((1, LANES)), _full((1, LANES))],
                          out_specs=[row(LANES), row(LANES)],
                          out_shape=[jax.ShapeDtypeStruct((T, LANES), F32)] * 2, compiler_params=_cp("parallel"))(ba, a_log, dt_bias)


def _dn_gate_bwd(ba, a_log, dt_bias, dbeta, dg, name):
    T = ba.shape[0]
    tm = _pick(T, 1024, SUBLANES)

    def body(ba_ref, al_ref, dt_ref, dbeta_ref, dg_ref, dba_ref, dal_ref, ddt_ref):
        @pl.when(pl.program_id(0) == 0)
        def _():
            dal_ref[...] = jnp.zeros_like(dal_ref)
            ddt_ref[...] = jnp.zeros_like(ddt_ref)

        beta = _sigmoid(ba_ref[:, :LANES])
        dba_ref[:, :LANES] = _mx(dbeta_ref[...] * beta * (1.0 - beta))
        pre = ba_ref[:, LANES:] + dt_ref[...]
        ea = jnp.exp(al_ref[...])
        dgv = dg_ref[...]
        da = dgv * (-ea) * _sigmoid(pre)
        dba_ref[:, LANES:] = _mx(da)
        ddt_ref[...] += jnp.sum(da, axis=0, keepdims=True)
        dal_ref[...] += jnp.sum(dgv * (-ea) * _softplus(pre), axis=0, keepdims=True)

    row = lambda w: pl.BlockSpec((tm, w), lambda i: (i, 0))
    one = _full((1, LANES))
    return pl.pallas_call(body, name=name, grid=(T // tm,), in_specs=[row(2 * LANES), one, one, row(LANES), row(LANES)],
                          out_specs=[row(2 * LANES), one, one],
                          out_shape=[jax.ShapeDtypeStruct((T, 2 * LANES), MXU_DTYPE), jax.ShapeDtypeStruct((1, LANES), F32),
                                     jax.ShapeDtypeStruct((1, LANES), F32)],
                          compiler_params=_cp("arbitrary"))(ba, a_log, dt_bias, dbeta, dg)


def _tri(c, strict):
    i = lax.broadcasted_iota(jnp.int32, (c, c), 0)
    j = lax.broadcasted_iota(jnp.int32, (c, c), 1)
    return (i > j) if strict else (i >= j)


def _inv_unit_lower(ls):
    c = ls[0].shape[0]
    i = lax.broadcasted_iota(jnp.int32, (c, c), 0)
    j = lax.broadcasted_iota(jnp.int32, (c, c), 1)
    eye = jnp.where(i == j, 1.0, 0.0)
    facs = [[eye - l for l in ls]]
    cur = ls
    for _ in range(int(math.log2(c)) - 1):
        cur = [_dot(p, p, NN, TRI_PREC) for p in cur]
        facs.append([eye + p for p in cur])
    while len(facs) > 1:
        nxt = [[_dot(a, b, NN, TRI_PREC) for a, b in zip(facs[t], facs[t + 1])] for t in range(0, len(facs) - 1, 2)]
        if len(facs) % 2:
            nxt.append(facs[-1])
        facs = nxt
    return facs[0]


def _chunk_gates(g_blk):
    c = g_blk.shape[0]
    gcs = _dot(jnp.where(_tri(c, False), 1.0, 0.0), g_blk, NN, HI)
    return gcs, gcs.T


def _head_chunk(h, qh, kh, vh, beta_blk, gcs, gcs_t):
    c = qh.shape[0]
    incl = _tri(c, False)
    gc_col = gcs[:, h:h + 1]
    gc_row = gcs_t[h:h + 1, :]
    gc_last = gcs_t[h:h + 1, c - 1:c]
    dec = jnp.where(incl, jnp.exp(jnp.where(incl, gc_col - gc_row, 0.0)), 0.0)
    gam = jnp.exp(gc_col)
    rr = jnp.exp(gc_last - gc_col)
    gl = jnp.exp(gc_last)
    b = beta_blk[:, h:h + 1]
    kb = kh * b
    vb = vh * b
    both = _dot(jnp.concatenate([_mx(kb), _mx(qh)], axis=0), _mx(kh), NT)
    lmat = jnp.where(_tri(c, True), both[:c] * dec, 0.0)
    pmat = jnp.where(incl, both[c:] * dec, 0.0)
    return dict(dec=dec, gam=gam, rr=rr, gl=gl, b=b, kb=kb, vb=vb, lmat=lmat, pmat=pmat)


def _solve_uw(tinv, q):
    return _dot(tinv, jnp.concatenate([q["vb"], q["kb"] * q["gam"]], axis=1), NN, TRI_PREC)


def _dn_scan_fwd(qkv, beta, g, proj, norm_g, name):
    T = qkv.shape[0]
    C, H, Dh = DN_CHUNK, DN_HEADS, DN_HEAD_DIM
    W = H * Dh
    N = T // C

    def body(q_ref, k_ref, v_ref, beta_ref, g_ref, z_ref, ng_ref, og_ref, o_ref, tinv_ref, s_ref, state):
        @pl.when(pl.program_id(0) == 0)
        def _():
            state[...] = jnp.zeros_like(state)

        ng = ng_ref[...]
        heads = range(H)
        cs = [slice(h * Dh, (h + 1) * Dh) for h in heads]

        def chunk(j, carry):
            rows = pl.ds(pl.multiple_of(j * C, C), C)
            gcs, gcs_t = _chunk_gates(g_ref[rows, :])
            beta_blk = beta_ref[rows, :]
            qs = [_head_chunk(h, q_ref[rows, cs[h]], k_ref[rows, cs[h]], v_ref[rows, cs[h]], beta_blk, gcs, gcs_t)
                  for h in heads]
            tinvs = _inv_unit_lower([q["lmat"] for q in qs])
            for h in heads:
                tinv_ref[h, rows, :] = tinvs[h]
            uws = [_solve_uw(tinvs[h], qs[h]) for h in heads]
            ss = [state[h] for h in heads]
            for h in heads:
                s_ref[j, h] = ss[h]
            sbs = [_mx(s) for s in ss]
            vnbs = [_mx(uws[h][:, :Dh] - _dot(_mx(uws[h][:, Dh:]), sbs[h], NN)) for h in heads]
            os_ = [_dot(jnp.concatenate([_mx(q_ref[rows, cs[h]] * qs[h]["gam"]), _mx(qs[h]["pmat"])], axis=1),
                        jnp.concatenate([sbs[h], vnbs[h]], axis=0), NN) for h in heads]
            for h in heads:
                state[h] = ss[h] * qs[h]["gl"] + _dot(_mx((k_ref[rows, cs[h]] * qs[h]["rr"]).T), vnbs[h], NN)
            for h in heads:
                o = os_[h]
                o_ref[rows, cs[h]] = o
                zh = z_ref[rows, cs[h]]
                og_ref[rows, cs[h]] = _mx(o * _rms_stat(o) * ng * (zh * _sigmoid(zh)))
            return carry

        lax.fori_loop(0, PER, chunk, 0)

    PER = 2 if N % 2 == 0 else 1
    blk = lambda j: pl.BlockSpec((PER * C, W), lambda n: (n, j))
    small = pl.BlockSpec((PER * C, LANES), lambda n: (n, 0))
    return pl.pallas_call(
        body, name=name, grid=(N // PER,),
        in_specs=[blk(0), blk(1), blk(2), small, small, blk(3), _full((1, Dh))],
        out_specs=[blk(0), blk(0), pl.BlockSpec((H, PER * C, C), lambda n: (0, n, 0)),
                   pl.BlockSpec((PER, H, Dh, Dh), lambda n: (n, 0, 0, 0))],
        out_shape=[jax.ShapeDtypeStruct((T, W), MXU_DTYPE), jax.ShapeDtypeStruct((T, W), F32),
                   jax.ShapeDtypeStruct((H, T, C), F32), jax.ShapeDtypeStruct((N, H, Dh, Dh), F32)],
        scratch_shapes=[pltpu.VMEM((H, Dh, Dh), F32)],
        compiler_params=_cp("arbitrary"),
    )(qkv, qkv, qkv, beta, g, proj, norm_g)


def _dn_scan_bwd(qkv, beta, g, proj, norm_g, o, tinv, s_all, dog, name):
    T = qkv.shape[0]
    C, H, Dh = DN_CHUNK, DN_HEADS, DN_HEAD_DIM
    W = H * Dh
    N = T // C

    def body(q_ref, k_ref, v_ref, beta_ref, g_ref, z_ref, ng_ref, o_ref, tinv_ref, s_ref, dog_ref,
             dqkv_ref, dbeta_ref, dg_ref, dz_ref, dng_ref, dstate):
        @pl.when(pl.program_id(0) == 0)
        def _():
            dstate[...] = jnp.zeros_like(dstate)
            dng_ref[...] = jnp.zeros_like(dng_ref)

        gcs, gcs_t = _chunk_gates(g_ref[...])
        beta_blk = beta_ref[...]
        ng = ng_ref[...]
        incl = _tri(C, False)
        strict = _tri(C, True)
        lane = lax.broadcasted_iota(jnp.int32, (C, LANES), 1)
        rowi = lax.broadcasted_iota(jnp.int32, (C, 1), 0)
        headrow = lax.broadcasted_iota(jnp.int32, (LANES, C), 0)
        colsums = jnp.zeros((LANES, C), F32)
        dbeta_acc = jnp.zeros((C, LANES), F32)
        dgc_acc = jnp.zeros((C, LANES), F32)
        dng_acc = jnp.zeros((1, Dh), F32)
        cs = [slice(h * Dh, (h + 1) * Dh) for h in range(H)]
        rsum = lambda t: jnp.sum(t, axis=1, keepdims=True)
        for heads in (range(0, H // 2), range(H // 2, H)):
            dobs = {}
            for h in heads:
                oh, zh, dogh = o_ref[:, cs[h]], z_ref[:, cs[h]], dog_ref[:, cs[h]]
                rstat = _rms_stat(oh)
                sz = _sigmoid(zh)
                dz_ref[:, cs[h]] = _mx(dogh * (oh * rstat * ng) * (sz * (1.0 + zh * (1.0 - sz))))
                do, dng = _rms_bwd(oh, rstat, ng, dogh * (zh * sz))
                dng_acc = dng_acc + dng
                dobs[h] = _mx(do)
            qs = {h: _head_chunk(h, q_ref[:, cs[h]], k_ref[:, cs[h]], v_ref[:, cs[h]], beta_blk, gcs, gcs_t) for h in heads}
            tms = {h: tinv_ref[h] for h in heads}
            uws = {h: _solve_uw(tms[h], qs[h]) for h in heads}
            ss = {h: s_ref[0, h] for h in heads}
            sbs = {h: _mx(ss[h]) for h in heads}
            wbs = {h: _mx(uws[h][:, Dh:]) for h in heads}
            vnbs = {h: _mx(uws[h][:, :Dh] - _dot(wbs[h], sbs[h], NN)) for h in heads}
            dsns = {h: dstate[h] for h in heads}
            dsbs = {h: _mx(dsns[h]) for h in heads}
            dvnews = {h: _dot(_mx(qs[h]["pmat"]), dobs[h], TN) + _dot(_mx(k_ref[:, cs[h]] * qs[h]["rr"]), dsbs[h], NN)
                      for h in heads}
            dvb16s = {h: _mx(dvnews[h]) for h in heads}
            dps = {h: jnp.where(incl, _dot(dobs[h], vnbs[h], NT), 0.0) for h in heads}
            dqds = {h: _dot(dobs[h], sbs[h], NT) for h in heads}
            dkds = {h: _dot(vnbs[h], dsbs[h], NT) for h in heads}
            dgls = {h: jnp.sum(rsum(ss[h] * dsns[h]), axis=0, keepdims=True) for h in heads}
            dws = {h: -_dot(dvb16s[h], sbs[h], NT) for h in heads}
            for h in heads:
                dstate[h] = qs[h]["gl"] * dsns[h] + _dot(
                    jnp.concatenate([_mx(q_ref[:, cs[h]] * qs[h]["gam"]), -wbs[h]], axis=0),
                    jnp.concatenate([dobs[h], dvb16s[h]], axis=0), TN)
            dsols = {h: _dot(tms[h], jnp.concatenate([dvnews[h], dws[h]], axis=1), TN, TRI_PREC) for h in heads}
            dvbs = {h: dsols[h][:, :Dh] for h in heads}
            dkbgs = {h: dsols[h][:, Dh:] for h in heads}
            dls = {h: jnp.where(strict, -_dot(dsols[h], uws[h], NT, TRI_PREC), 0.0) for h in heads}
            mmats = {h: dls[h] * qs[h]["lmat"] + dps[h] * qs[h]["pmat"] for h in heads}
            dgcs = {h: rsum(mmats[h]) for h in heads}
            for h in heads:
                colsums = jnp.where(headrow == h, jnp.sum(mmats[h], axis=0, keepdims=True), colsums)
            dboth = {h: jnp.concatenate([_mx(dls[h] * qs[h]["dec"]), _mx(dps[h] * qs[h]["dec"])], axis=0) for h in heads}
            for h in heads:
                q = qs[h]
                qh, kh, vh = q_ref[:, cs[h]], k_ref[:, cs[h]], v_ref[:, cs[h]]
                gam, rr, b, kb = q["gam"], q["rr"], q["b"], q["kb"]
                on_k = _dot(dboth[h], _mx(kh), NN)
                dkb = on_k[:C] + dkbgs[h] * gam
                dk = _dot(dboth[h], jnp.concatenate([_mx(kb), _mx(qh)], axis=0), TN) + dkb * b + dkds[h] * rr
                dq = on_k[C:] + dqds[h] * gam
                dgam = rsum(dkbgs[h] * kb) + rsum(dqds[h] * qh)
                dr = rsum(dkds[h] * kh)
                dgc_last = jnp.sum(dr * rr, axis=0, keepdims=True) + dgls[h] * q["gl"]
                dgc = dgcs[h] + dgam * gam - dr * rr + jnp.where(rowi == C - 1, dgc_last, 0.0)
                dbeta = rsum(dvbs[h] * vh) + rsum(dkb * kh)
                dqkv_ref[:, cs[h]] = dq
                dqkv_ref[:, W + h * Dh:W + (h + 1) * Dh] = dk
                dqkv_ref[:, 2 * W + h * Dh:2 * W + (h + 1) * Dh] = dvbs[h] * b
                dbeta_acc = jnp.where(lane == h, dbeta, dbeta_acc)
                dgc_acc = jnp.where(lane == h, dgc, dgc_acc)
        dbeta_ref[...] = dbeta_acc
        dg_ref[...] = _dot(jnp.where(incl, 1.0, 0.0), dgc_acc - colsums.T, TN, HI)
        dng_ref[...] += dng_acc

    rev = lambda n: N - 1 - n
    blk = lambda j: pl.BlockSpec((C, W), lambda n: (rev(n), j))
    small = pl.BlockSpec((C, LANES), lambda n: (rev(n), 0))
    return pl.pallas_call(
        body, name=name, grid=(N,),
        in_specs=[blk(0), blk(1), blk(2), small, small, blk(3), _full((1, Dh)), blk(0),
                  pl.BlockSpec((H, C, C), lambda n: (0, rev(n), 0)),
                  pl.BlockSpec((1, H, Dh, Dh), lambda n: (rev(n), 0, 0, 0)), blk(0)],
        out_specs=[pl.BlockSpec((C, 3 * W), lambda n: (rev(n), 0)), small, small, blk(0), _full((1, Dh))],
        out_shape=[jax.ShapeDtypeStruct((T, 3 * W), F32), jax.ShapeDtypeStruct((T, LANES), F32),
                   jax.ShapeDtypeStruct((T, LANES), F32), jax.ShapeDtypeStruct((T, W), MXU_DTYPE),
                   jax.ShapeDtypeStruct((1, Dh), F32)],
        scratch_shapes=[pltpu.VMEM((H, Dh, Dh), F32)],
        compiler_params=_cp("arbitrary"),
    )(qkv, qkv, qkv, beta, g, proj, norm_g, o, tinv, s_all, dog)


_INV_SQRT2 = 0.7071067811865476
_INV_SQRT_2PI = 0.3989422804014327


def _sg_recompute(zp_ref, bin_ref, lng_ref, lnb_ref):
    E = SG_WIDTH
    zin = zp_ref[...] + bin_ref[...]
    cdf = 0.5 * (1.0 + lax.erf(zin * _INV_SQRT2))
    zz = zin * cdf
    u = zz[:, :E]
    vp = zz[:, E:]
    mu = jnp.mean(vp, axis=-1, keepdims=True)
    xc = vp - mu
    rstd = lax.rsqrt(jnp.mean(xc * xc, axis=-1, keepdims=True) + LN_EPS)
    xhat = xc * rstd
    v = xhat * lng_ref[...] + lnb_ref[...]
    return zin, cdf, u, xhat, rstd, v


def _sg_masked_ws(ws_ref, g):
    return _mx(jnp.where(_tri(SG_CHUNK, False), ws_ref[g], 0.0))


def _sg_fwd(zpre, b_in, ln_g, ln_b, w_s, b_s_t, name):
    T = zpre.shape[0]
    E, G, C, GW = SG_WIDTH, SG_GROUPS, SG_CHUNK, SG_GROUP_W

    def body(zp_ref, bin_ref, lng_ref, lnb_ref, ws_ref, bst_ref, um_ref):
        _, _, u, _, _, v = _sg_recompute(zp_ref, bin_ref, lng_ref, lnb_ref)
        bst = bst_ref[...]
        for g in range(G):
            cs = slice(g * GW, (g + 1) * GW)
            mixed = _dot(_sg_masked_ws(ws_ref, g), _mx(v[:, cs]), NN) + bst[:, g:g + 1]
            um_ref[:, cs] = _mx(u[:, cs] * mixed)

    return pl.pallas_call(
        body, name=name, grid=(T // C,),
        in_specs=[pl.BlockSpec((C, 2 * E), lambda n: (n, 0)), _full((1, 2 * E)), _full((1, E)), _full((1, E)),
                  _full((G, C, C)), _full((C, LANES))],
        out_specs=pl.BlockSpec((C, E), lambda n: (n, 0)),
        out_shape=jax.ShapeDtypeStruct((T, E), MXU_DTYPE), compiler_params=_cp("parallel"),
    )(zpre, b_in, ln_g, ln_b, w_s, b_s_t)


def _sg_bwd(zpre, b_in, ln_g, ln_b, w_s, b_s_t, dum, name):
    T = zpre.shape[0]
    E, G, C, GW = SG_WIDTH, SG_GROUPS, SG_CHUNK, SG_GROUP_W

    def body(zp_ref, bin_ref, lng_ref, lnb_ref, ws_ref, bst_ref, dum_ref,
             dz_ref, dbin_ref, dlng_ref, dlnb_ref, dws_ref, dbst_ref):
        @pl.when(pl.program_id(0) == 0)
        def _():
            for r in (dbin_ref, dlng_ref, dlnb_ref, dws_ref, dbst_ref):
                r[...] = jnp.zeros_like(r)

        zin, cdf, u, xhat, rstd, v = _sg_recompute(zp_ref, bin_ref, lng_ref, lnb_ref)
        bst = bst_ref[...]
        lane = lax.broadcasted_iota(jnp.int32, (C, LANES), 1)
        dum_v = dum_ref[...]
        dbst = jnp.zeros((C, LANES), F32)
        du_parts, dv_parts = [], []
        for g in range(G):
            cs = slice(g * GW, (g + 1) * GW)
            wsm = _sg_masked_ws(ws_ref, g)
            vg = _mx(v[:, cs])
            mixed = _dot(wsm, vg, NN) + bst[:, g:g + 1]
            dumg = dum_v[:, cs]
            du_parts.append(dumg * mixed)
            dmixed = dumg * u[:, cs]
            dmb = _mx(dmixed)
            dv_parts.append(_dot(wsm, dmb, TN))
            dws_ref[g] += _dot(dmb, vg, NT)
            dbst = jnp.where(lane == g, jnp.sum(dmixed, axis=1, keepdims=True), dbst)
        dbst_ref[...] += dbst
        du = jnp.concatenate(du_parts, axis=1)
        dv = jnp.concatenate(dv_parts, axis=1)
        dlng_ref[...] += jnp.sum(dv * xhat, axis=0, keepdims=True)
        dlnb_ref[...] += jnp.sum(dv, axis=0, keepdims=True)
        dxh = dv * lng_ref[...]
        dvp = rstd * (dxh - jnp.mean(dxh, axis=-1, keepdims=True) - xhat * jnp.mean(dxh * xhat, axis=-1, keepdims=True))
        dzz = jnp.concatenate([du, dvp], axis=1)
        dzin = dzz * (cdf + zin * (_INV_SQRT_2PI * jnp.exp(-0.5 * zin * zin)))
        dz_ref[...] = _mx(dzin)
        dbin_ref[...] += jnp.sum(dzin, axis=0, keepdims=True)

    return pl.pallas_call(
        body, name=name, grid=(T // C,),
        in_specs=[pl.BlockSpec((C, 2 * E), lambda n: (n, 0)), _full((1, 2 * E)), _full((1, E)), _full((1, E)),
                  _full((G, C, C)), _full((C, LANES)), pl.BlockSpec((C, E), lambda n: (n, 0))],
        out_specs=[pl.BlockSpec((C, 2 * E), lambda n: (n, 0)), _full((1, 2 * E)), _full((1, E)), _full((1, E)),
                   _full((G, C, C)), _full((C, LANES))],
        out_shape=[jax.ShapeDtypeStruct((T, 2 * E), MXU_DTYPE), jax.ShapeDtypeStruct((1, 2 * E), F32),
                   jax.ShapeDtypeStruct((1, E), F32), jax.ShapeDtypeStruct((1, E), F32),
                   jax.ShapeDtypeStruct((G, C, C), F32), jax.ShapeDtypeStruct((C, LANES), F32)],
        compiler_params=_cp("arbitrary"),
    )(zpre, b_in, ln_g, ln_b, w_s, b_s_t, dum)


def _row(v):
    return v.reshape(1, -1)


def _pad_lanes(v):
    v = v.reshape(1, -1)
    return jnp.pad(v, ((0, 0), (0, LANES - v.shape[1])))


def _local_step(x, target, p, weights_for, grads_ready=None, small_ready=None):
    ng = p["norm_g"]
    grads = {}
    dng = [[None] * 6 for _ in range(2)]
    order = [jnp.zeros((), F32)]

    def tell(group):
        zero = grads_ready(group, grads) if grads_ready is not None else None
        if zero is not None:
            order[0] = zero

    def gain(i, s):
        return _row(ng[i, s]) + order[0]

    def ffn_f(xin, i, j, tag, **tail):
        wt = weights_for("ffn" + tag, xin)
        xo, h, gu, y, *rest = _ffn_fwd(xin, _row(ng[i, 4 * j]), _row(ng[i, 4 * j + 1]), wt, "ffn_fwd_" + tag, **tail)
        return (xo, *rest), (xin, h, gu, y, wt)

    (x1, hn0), sv_f00 = ffn_f(x, 0, 0, "00", next_gain=_row(ng[0, 2]))
    dnw = weights_for("dn", x1)
    proj = _mm(hn0, dnw["dn_wqkvz"], "nn", "dn_proj")
    ba = _mm(hn0, dnw["dn_wba"], "nn", "dn_proj_ba")
    a_log = _pad_lanes(p["dn_a_log"])
    dt_bias = _pad_lanes(p["dn_dt_bias"])
    dn_ng = _row(p["dn_norm_g"])
    qkv = _dn_prep_fwd(proj, p["dn_conv_w"], "dn_prep_fwd")
    beta, gdec = _dn_gate_fwd(ba, a_log, dt_bias, "dn_gate_fwd")
    og, o_raw, tinv, s_all = _dn_scan_fwd(qkv, beta, gdec, proj, dn_ng, "dn_scan_fwd")
    m0, x2 = _out_proj_postnorm(og, dnw["dn_wout"], x1, _row(ng[0, 3]), "dn_out")
    (x3,), sv_f01 = ffn_f(x2, 0, 1, "01")
    (x4, hn1), sv_f10 = ffn_f(x3, 1, 0, "10", next_gain=_row(ng[1, 2]))
    sgw = weights_for("sg", x4)
    zpre = _mm(hn1, sgw["sg_win"], "nn", "sg_proj")
    sg_bin = _row(p["sg_b_in"])
    sg_lng = _row(p["sg_ln_g"])
    sg_lnb = _row(p["sg_ln_b"])
    sg_bst = jnp.pad(p["sg_b_s"].T, ((0, 0), (0, LANES - SG_GROUPS)))
    um = _sg_fwd(zpre, sg_bin, sg_lng, sg_lnb, p["sg_w_s"], sg_bst, "sg_fwd")
    m1, x5 = _out_proj_postnorm(um, sgw["sg_wout"], x4, _row(ng[1, 3]), "sg_out")
    (_, loss_part, dx), sv_f11 = ffn_f(x5, 1, 1, "11", target=target)

    def ffn_b(dxo, sv, i, j, tag, last=False):
        xin, h, gu, y, wt = sv
        dxi, dy, a, dgu, dg0, dg1 = _ffn_bwd(dxo, xin, y, gu, gain(i, 4 * j), gain(i, 4 * j + 1), wt, "ffn_bwd_" + tag)
        dng[i][4 * j] = dg0
        dng[i][4 * j + 1] = dg1
        after = None
        if last:
            grads["norm_g"] = jnp.stack([jnp.concatenate(dng[t], axis=0) for t in range(2)])
            after = small_ready(grads, loss_part) if small_ready is not None else None
        grads["wd" + tag] = _mm(a, dy, "tn", "ffn_wgrad_down_" + tag, after=after)
        grads["wguT" + tag] = _mm(dgu, h, "tn", "ffn_wgrad_up_" + tag, after=after)
        tell("ffn" + tag)
        return dxi

    dx = ffn_b(dx, sv_f11, 1, 1, "11")
    dm1, dng[1][3], dum = _postnorm_bwd_dgrad(dx, m1, gain(1, 3), sgw["sg_wout"], "sg_dgrad_out")
    grads["sg_w_out"] = _mm(um, dm1, "tn", "sg_wgrad_out")
    dz1, dbin, dlng, dlnb, dws, dbst = _sg_bwd(zpre, sg_bin, sg_lng, sg_lnb, p["sg_w_s"], sg_bst, dum, "sg_bwd")
    grads["sg_w_inT"] = _mm(dz1, hn1, "tn", "sg_wgrad_in")
    tell("sg")
    dx, dng[1][2] = _dgrad_prenorm_bwd(dz1, sgw["sg_win"], None, dx, x4, gain(1, 2), "sg_dgrad_in")
    grads["sg_b_in"] = dbin.reshape(1, -1)
    grads["sg_ln_g"] = dlng.reshape(1, -1)
    grads["sg_ln_b"] = dlnb.reshape(1, -1)
    grads["sg_w_s"] = jnp.where(jnp.tril(jnp.ones((SG_CHUNK, SG_CHUNK), bool)), dws, 0.0)[None]
    grads["sg_b_s"] = dbst[:, :SG_GROUPS].T[None]
    dx = ffn_b(dx, sv_f10, 1, 0, "10")
    dx = ffn_b(dx, sv_f01, 0, 1, "01")
    dm0, dng[0][3], dog = _postnorm_bwd_dgrad(dx, m0, gain(0, 3), dnw["dn_wout"], "dn_dgrad_out")
    grads["dn_w_out"] = _mm(og, dm0, "tn", "dn_wgrad_out")
    dqkv, dbeta, dgdec, dz0, dnng = _dn_scan_bwd(qkv, beta, gdec, proj, dn_ng, o_raw, tinv, s_all, dog, "dn_scan_bwd")
    dqkv_pre, dconv = _dn_prep_bwd(proj, p["dn_conv_w"], dqkv, "dn_prep_bwd")
    dba, dal, ddt = _dn_gate_bwd(ba, a_log, dt_bias, dbeta, dgdec, "dn_gate_bwd")
    W3 = 3 * DN_HEADS * DN_HEAD_DIM
    dw_qkv = _mm(hn0, dqkv_pre, "tn", "dn_wgrad_qkv")
    dw_z = _mm(hn0, dz0, "tn", "dn_wgrad_z")
    dw_ba = _mm(hn0, dba, "tn", "dn_wgrad_ba")
    grads["dn_w_in"] = jnp.concatenate(
        [dw_qkv, dw_z, dw_ba[:, :DN_HEADS], dw_ba[:, LANES:LANES + DN_HEADS]], axis=1)
    tell("dn")
    dh0 = _mm(dqkv_pre, dnw["dn_wqkvz"][:, :W3], "nt", "dn_dgrad_qkv")
    dh0 = _mm(dz0, dnw["dn_wqkvz"][:, W3:], "nt", "dn_dgrad_z", add=dh0)
    dx, dng[0][2] = _dgrad_prenorm_bwd(dba, dnw["dn_wba"], dh0, dx, x1, gain(0, 2), "dn_dgrad_ba")
    grads["dn_conv_w"] = dconv[None]
    grads["dn_a_log"] = dal[:, :DN_HEADS]
    grads["dn_dt_bias"] = ddt[:, :DN_HEADS]
    grads["dn_norm_g"] = dnng
    dx = ffn_b(dx, sv_f00, 0, 0, "00", last=True)
    return loss_part, dx, grads


def _mesh_pos():
    return lax.axis_index("x"), lax.axis_index("y"), lax.axis_index("c")


def _other_chips(x, y):
    return [(1 - x, y), (x, 1 - y), (1 - x, 1 - y)]


def _allgather_chips(arrs, name):
    n = len(arrs)

    def body(*refs):
        ins, outs = refs[:n], refs[n:2 * n]
        ici_send, ici_recv, d2d_send, d2d_recv = refs[2 * n:]
        x, y, c = _mesh_pos()
        me = 2 * x + y
        chips = _other_chips(x, y)
        sibling = (x, y, 1 - c)

        def ici(i, j, k):
            cx, cy = chips[j]
            return pltpu.make_async_remote_copy(src_ref=ins[i].at[c], dst_ref=outs[i].at[k, c], send_sem=ici_send.at[3 * i + j],
                                                recv_sem=ici_recv.at[3 * i + j], device_id=(cx, cy, c), device_id_type=MESH)

        def d2d(i, j, h):
            cx, cy = chips[j]
            slot = outs[i].at[2 * cx + cy, h]
            return pltpu.make_async_remote_copy(src_ref=slot, dst_ref=slot, send_sem=d2d_send.at[3 * i + j],
                                                recv_sem=d2d_recv.at[3 * i + j], device_id=sibling, device_id_type=MESH)

        sends = [ici(i, j, me) for i in range(n) for j in range(3)]
        for cp in sends:
            cp.start()
        for i in range(n):
            for j, (cx, cy) in enumerate(chips):
                ici(i, j, 2 * cx + cy).wait_recv()
                fwd = d2d(i, j, c)
                fwd.start()
                sends.append(fwd)
        for i in range(n):
            for j in range(3):
                d2d(i, j, 1 - c).wait_recv()
        for cp in sends:
            cp.wait_send()

    return pl.pallas_call(
        body, name=name, in_specs=[ANY] * n, out_specs=[ANY] * n,
        out_shape=[jax.ShapeDtypeStruct((N_CHIPS,) + a.shape, a.dtype) for a in arrs],
        scratch_shapes=[pltpu.SemaphoreType.DMA((3 * n,))] * 4,
    )(*arrs)


HBM = pl.BlockSpec(memory_space=pltpu.HBM)
SEM = pl.BlockSpec(memory_space=pltpu.SEMAPHORE)
TOKEN = jax.ShapeDtypeStruct((SUBLANES, LANES), F32)


_PEERS = {"gather": 3, "scatter": 3, "swap": 1, "all": N_DEV - 1}


def _land_shape(kind, shape):
    if kind == "gather":
        return (N_CHIPS,) + shape
    if kind == "all":
        return (N_DEV,) + shape
    return (N_CHIPS,) + shape[2:] if kind == "swap" else shape


def _peer_copies(kind, flags, src_refs, land_refs, send_sems, recv_sems, receiving):
    x, y, c = _mesh_pos()
    me4, me8 = 2 * x + y, 4 * x + 2 * y + c
    np_ = _PEERS[kind]
    cps = []
    for i, (src, land) in enumerate(zip(src_refs, land_refs)):
        if kind == "swap":
            half = src.at[1 - c] if flags[i] else src.at[:, 1 - c]
            plan = [((x, y, 1 - c), half, land)]
        elif kind == "all":
            masks = [(mx, my, mc) for mx in (0, 1) for my in (0, 1) for mc in (0, 1)][1:]
            peers = [(jnp.where(mx, 1 - x, x), jnp.where(my, 1 - y, y), jnp.where(mc, 1 - c, c)) for mx, my, mc in masks]
            plan = [(p, src, land.at[4 * p[0] + 2 * p[1] + p[2] if receiving else me8]) for p in peers]
        else:
            plan = []
            for cx, cy in _other_chips(x, y):
                k = 2 * cx + cy
                s = src.at[me4 if receiving else k] if kind == "scatter" else src
                plan.append(((cx, cy, c), s, land.at[k if receiving else me4]))
        for j, (peer, s, d) in enumerate(plan):
            cps.append(pltpu.make_async_remote_copy(src_ref=s, dst_ref=d, send_sem=send_sems.at[np_ * i + j],
                                                    recv_sem=recv_sems.at[np_ * i + j], device_id=peer, device_id_type=MESH))
    return cps


def _copies_start(kind, srcs, after, name, flags=None):
    n = len(srcs)
    ns = _PEERS[kind] * n
    lands = [lax.empty(_land_shape(kind, s.shape), s.dtype) for s in srcs]
    after = [] if after is None else [after]

    def body(*refs):
        src_refs, land_refs = refs[:n], refs[n:2 * n]
        send_sems, recv_sems = refs[2 * n + len(after)], refs[2 * n + len(after) + 1]
        token = refs[-1]
        for cp in _peer_copies(kind, flags, src_refs, land_refs, send_sems, recv_sems, False):
            cp.start()
        token[...] = jnp.zeros_like(token)

    outs = pl.pallas_call(
        body, name=name,
        in_specs=[HBM] * (2 * n) + [ANY] * len(after),
        out_specs=(SEM, SEM) + (HBM,) * (2 * n) + (pl.BlockSpec(memory_space=pltpu.VMEM),),
        out_shape=(pltpu.SemaphoreType.DMA((ns,)), pltpu.SemaphoreType.DMA((ns,)))
        + tuple(pltpu.HBM(a.shape, a.dtype) for a in list(srcs) + lands) + (TOKEN,),
        input_output_aliases={i: 2 + i for i in range(2 * n)},
        compiler_params=pltpu.CompilerParams(has_side_effects=pltpu.SideEffectType.DATAFLOW_SIDE_EFFECTING),
    )(*[pltpu.with_memory_space_constraint(a, pltpu.HBM) for a in list(srcs) + lands], *after)
    return dict(sems=outs[:2], srcs=outs[2:2 + n], lands=outs[2 + n:2 + 2 * n], token=outs[-1], kind=kind, flags=flags)


def _copies_wait(started, after, name):
    n = len(started["srcs"])
    kind, flags = started["kind"], started["flags"]
    after = list(after) if isinstance(after, (list, tuple)) else [after]

    def body(*refs):
        src_refs, land_refs = refs[:n], refs[n:2 * n]
        send_sems, recv_sems = refs[2 * n], refs[2 * n + 1]
        for cp in _peer_copies(kind, flags, src_refs, land_refs, send_sems, recv_sems, True):
            cp.wait_send()
            cp.wait_recv()

    outs = pl.pallas_call(
        body, name=name,
        in_specs=[HBM] * (2 * n) + [SEM, SEM] + [ANY] * len(after),
        out_specs=(HBM,) * (2 * n),
        out_shape=tuple(pltpu.HBM(a.shape, a.dtype) for a in list(started["srcs"]) + list(started["lands"])),
        input_output_aliases={i: i for i in range(2 * n)},
        compiler_params=pltpu.CompilerParams(has_side_effects=pltpu.SideEffectType.DATAFLOW_SIDE_EFFECTING),
    )(*started["srcs"], *started["lands"], *started["sems"], *after)
    return outs[:n], outs[n:]


def _swap_whole(arrs, name):
    n = len(arrs)

    def body(*refs):
        ins, outs = refs[:n], refs[n:2 * n]
        send_sems, recv_sems = refs[2 * n:]
        x, y, c = _mesh_pos()
        cps = [pltpu.make_async_remote_copy(src_ref=ins[i], dst_ref=outs[i], send_sem=send_sems.at[i],
                                            recv_sem=recv_sems.at[i], device_id=(x, y, 1 - c), device_id_type=MESH)
               for i in range(n)]
        for cp in cps:
            cp.start()
        for cp in cps:
            cp.wait()

    return pl.pallas_call(
        body, name=name, in_specs=[ANY] * n, out_specs=[ANY] * n,
        out_shape=[jax.ShapeDtypeStruct(a.shape, a.dtype) for a in arrs],
        scratch_shapes=[pltpu.SemaphoreType.DMA((n,)), pltpu.SemaphoreType.DMA((n,))],
    )(*arrs)


def _as_rows(a, lead):
    shp = a.shape
    rows = 1
    for s in shp[lead:-1]:
        rows *= s
    return a.reshape(shp[:lead] + (rows, shp[-1]))


def _row_tile(rows, cols, n_bufs):
    budget = (24 * 1024 * 1024) // (n_bufs * 2 * 4 * cols)
    return _pick(rows, max(2 * SUBLANES, budget), 2 * SUBLANES)


def _sum_devices(own, got, dev, name):
    n, rows, cols = got.shape
    tr = _row_tile(rows, cols, n + 2)

    def body(dev_ref, own_ref, got_ref, o_ref):
        mine = own_ref[...]
        acc = jnp.where(dev_ref[0] == 0, mine, got_ref[0])
        for k in range(1, n):
            acc = acc + jnp.where(dev_ref[0] == k, mine, got_ref[k])
        o_ref[...] = acc

    return pl.pallas_call(
        body, name=name,
        grid_spec=pltpu.PrefetchScalarGridSpec(
            num_scalar_prefetch=1, grid=(rows // tr,),
            in_specs=[pl.BlockSpec((tr, cols), lambda i, d: (i, 0)), pl.BlockSpec((n, tr, cols), lambda i, d: (0, i, 0))],
            out_specs=pl.BlockSpec((tr, cols), lambda i, d: (i, 0))),
        out_shape=jax.ShapeDtypeStruct((rows, cols), F32), compiler_params=_cp("parallel"),
    )(_scalar(dev), own, got)


def _scalar(i):
    return jnp.reshape(i, (1,)).astype(jnp.int32)


def _add_own_half(g, other, c, half_first, name):
    _, rows, cols = other.shape
    tr = _row_tile(rows, cols, 3)

    def body(c_ref, g_ref, o_ref, out_ref):
        out_ref[0] = (g_ref[0, 0] + o_ref[0]).astype(out_ref.dtype)

    if half_first:
        g_map = lambda k, i, c_ref: (c_ref[0], k, i, 0)
    else:
        g_map = lambda k, i, c_ref: (k, c_ref[0], i, 0)
    flat = pl.BlockSpec((1, tr, cols), lambda k, i, c_ref: (k, i, 0))
    return pl.pallas_call(
        body, name=name,
        grid_spec=pltpu.PrefetchScalarGridSpec(
            num_scalar_prefetch=1, grid=(N_CHIPS, rows // tr),
            in_specs=[pl.BlockSpec((1, 1, tr, cols), g_map), flat], out_specs=flat),
        out_shape=jax.ShapeDtypeStruct(other.shape, COMM_DTYPE), compiler_params=_cp("parallel", "parallel"),
    )(_scalar(c), g, other)


def _sum_chips(own, got, chip, name, transpose=False):
    _, rows, cols = own.shape
    tr = rows if transpose else _row_tile(rows, cols, N_CHIPS + 2)

    def body(chip_ref, p_ref, b_ref, o_ref):
        mine = p_ref[0].astype(F32)
        acc = jnp.where(chip_ref[0] == 0, mine, b_ref[0].astype(F32))
        for k in range(1, N_CHIPS):
            acc = acc + jnp.where(chip_ref[0] == k, mine, b_ref[k].astype(F32))
        o_ref[...] = acc.T if transpose else acc

    if transpose:
        out_spec, out_shape = pl.BlockSpec((cols, rows), lambda i, k_ref: (0, 0)), (cols, rows)
    else:
        out_spec, out_shape = pl.BlockSpec((tr, cols), lambda i, k_ref: (i, 0)), (rows, cols)
    return pl.pallas_call(
        body, name=name,
        grid_spec=pltpu.PrefetchScalarGridSpec(
            num_scalar_prefetch=1, grid=(rows // tr,),
            in_specs=[pl.BlockSpec((1, tr, cols), lambda i, k_ref: (k_ref[0], i, 0)),
                      pl.BlockSpec((N_CHIPS, tr, cols), lambda i, k_ref: (0, i, 0))],
            out_specs=out_spec),
        out_shape=jax.ShapeDtypeStruct(out_shape, F32), compiler_params=_cp("parallel"),
    )(_scalar(chip), own, got)


def _adam_math(w, g, m, v):
    nm = ADAM_B1 * m + (1.0 - ADAM_B1) * g
    nv = ADAM_B2 * v + (1.0 - ADAM_B2) * (g * g)
    m_hat = nm / (1.0 - ADAM_B1 ** ADAM_STEP)
    v_hat = nv / (1.0 - ADAM_B2 ** ADAM_STEP)
    return -ADAM_LR * (m_hat / (jnp.sqrt(v_hat) + ADAM_EPS) + ADAM_WD * w), nm, nv


def _adamw_pieces(w, m, v, mine, theirs, c, kind, name):
    shape = w.shape
    P = len(mine)
    ws, ms, vs = (t.reshape((P, -1, t.shape[-1])) for t in (w, m, v))
    _, R, C = ws.shape
    if kind == "rows":
        tr = _pick(R // 2, 512, SUBLANES)
    else:
        tr = _pick(R, 256 if kind in ("lo", "hi") else 512, SUBLANES)
    nt = R // tr
    nh = nt // 2

    def body(c_ref, w_ref, m_ref, v_ref, *refs):
        mine_refs, theirs_refs = refs[:P], refs[P:2 * P]
        g_ref, d_ref, nm_ref, nv_ref = refs[2 * P:]
        p, i, core = pl.program_id(0), pl.program_id(1), c_ref[0]

        def pick(refs_):
            out = refs_[0][...]
            for q in range(1, P):
                out = jnp.where(p == q, refs_[q][...], out)
            return out

        a, b = pick(mine_refs), pick(theirs_refs)
        if kind == "cols":
            gv = jnp.where(core == 0, jnp.concatenate([a, b], axis=1), jnp.concatenate([b, a], axis=1))
        else:
            own = {"lo": core == 0, "hi": core == 1, "rows": (i >= nh) == (core == 1)}[kind]
            gv = jnp.where(own, a, b)
        g_ref[0] = gv
        d_ref[0], nm_ref[0], nv_ref[0] = _adam_math(w_ref[0], gv, m_ref[0], v_ref[0])

    def piece_spec(q):
        tile = (lambda i: i - jnp.where(i >= nh, nh, 0)) if kind == "rows" else (lambda i: i)
        return pl.BlockSpec((tr, mine[q].shape[1]), lambda p, i, c_ref: (jnp.where(p == q, tile(i), 0), 0))

    full = pl.BlockSpec((1, tr, C), lambda p, i, c_ref: (p, i, 0))
    outs = pl.pallas_call(
        body, name=name,
        grid_spec=pltpu.PrefetchScalarGridSpec(num_scalar_prefetch=1, grid=(P, nt),
                                               in_specs=[full] * 3 + [piece_spec(q) for q in range(P)] * 2,
                                               out_specs=[full] * 4),
        out_shape=[jax.ShapeDtypeStruct((P, R, C), F32)] * 4, compiler_params=_cp("parallel", "arbitrary"),
    )(_scalar(c), ws, ms, vs, *mine, *theirs)
    return tuple(o.reshape(shape) for o in outs)


def _adamw(w, g, m, v, name):
    shape = w.shape
    ws, gs, ms, vs = (_as_rows(t, 0) for t in (w, g, m, v))
    rows, cols = ws.shape
    tr = _row_tile(rows, cols, 7)

    def body(w_ref, g_ref, m_ref, v_ref, d_ref, nm_ref, nv_ref):
        d_ref[...], nm_ref[...], nv_ref[...] = _adam_math(w_ref[...], g_ref[...], m_ref[...], v_ref[...])

    spec = pl.BlockSpec((tr, cols), lambda i: (i, 0))
    outs = pl.pallas_call(body, name=name, grid=(rows // tr,), in_specs=[spec] * 4, out_specs=[spec] * 3,
                          out_shape=[jax.ShapeDtypeStruct((rows, cols), F32)] * 3, compiler_params=_cp("parallel"))(ws, gs, ms, vs)
    return tuple(o.reshape(shape) for o in outs)


_BIG = ["ffn_w_gate", "ffn_w_up", "ffn_w_down", "dn_w_in", "dn_w_out", "sg_w_in", "sg_w_out"]
_SMALL_SHARDED = ["norm_g", "dn_conv_w", "sg_b_in", "sg_ln_g", "sg_ln_b"]
_SMALL_REPL = ["dn_a_log", "dn_dt_bias", "dn_norm_g", "sg_w_s", "sg_b_s"]
_WEIGHTS = ["norm_g", "ffn_w_gate", "ffn_w_up", "ffn_w_down", "dn_w_in", "dn_conv_w", "dn_a_log", "dn_dt_bias",
            "dn_norm_g", "dn_w_out", "sg_w_in", "sg_b_in", "sg_ln_g", "sg_ln_b", "sg_w_s", "sg_b_s", "sg_w_out"]
PACK_COLS = 1024


def _pack(arrs):
    flat = jnp.concatenate([a.reshape(-1) for a in arrs])
    pad = (-flat.shape[0]) % (SUBLANES * PACK_COLS)
    return jnp.pad(flat, (0, pad)).reshape(-1, PACK_COLS)


def _unpack(buf, shapes):
    flat = buf.reshape(-1)
    out, off = [], 0
    for s in shapes:
        n = math.prod(s)
        out.append(flat[off:off + n].reshape(s))
        off += n
    return out


def _as_halves(a):
    if a.shape[0] == 2:
        return a
    if a.shape[0] == 1:
        return a.reshape((2, a.shape[1] // 2) + a.shape[2:])
    return a.reshape((2, a.shape[0] // 2) + a.shape[1:])


def _with_own(gathered, own, chip):
    g = gathered.reshape((N_CHIPS,) + own.shape)
    return [jnp.where(chip == k, own, g[k]) for k in range(N_CHIPS)]


def _cat_shards(g, axis):
    return jnp.concatenate(list(g), axis=axis)


_GROUP_ORDER = ["ffn00", "dn", "ffn01", "ffn10", "sg", "ffn11"]


def _weight_groups(w):
    cast = {k: _mx(w[k]) for k in _BIG}
    groups = {"ffn%d%d" % (i, j): [cast["ffn_w_gate"][i, j].T, cast["ffn_w_up"][i, j].T, cast["ffn_w_down"][i, j]]
              for i, j in [(0, 0), (0, 1), (1, 0), (1, 1)]}
    groups["dn"] = [cast["dn_w_in"][0], cast["dn_w_out"][0]]
    groups["sg"] = [cast["sg_w_in"][0], cast["sg_w_out"][0]]
    return groups


def _ffn_weights(chip, own, gathered):
    pairs = [(a, g.reshape((N_CHIPS,) + a.shape)) for a, g in zip(own, gathered)]
    return {"chip": chip, "gate": pairs[0], "up": pairs[1], "down": pairs[2]}


def _group_matrices(group, shards):
    if group == "sg":
        return {"sg_win": _cat_shards(shards[0], 1), "sg_wout": _cat_shards(shards[1], 0)}
    dn_full = _cat_shards(shards[0], 1)
    W4 = 4 * DN_HEADS * DN_HEAD_DIM
    wba = jnp.zeros((D_MODEL, 2 * LANES), dn_full.dtype)
    wba = wba.at[:, :DN_HEADS].set(dn_full[:, W4:W4 + DN_HEADS])
    wba = wba.at[:, LANES:LANES + DN_HEADS].set(dn_full[:, W4 + DN_HEADS:])
    return {"dn_wqkvz": dn_full[:, :W4], "dn_wba": wba, "dn_wout": _cat_shards(shards[1], 0)}


def _split_cols(a, n):
    w = a.shape[-1] // n
    return [a[..., k * w:(k + 1) * w] for k in range(n)]


def _split_rows(a, n):
    h = a.shape[-2] // n
    return [a[..., k * h:(k + 1) * h, :] for k in range(n)]


_IJ = [(0, 0), (0, 1), (1, 0), (1, 1)]


def _group_grads(group, grads):
    def rows_by_chip(a):
        return a.reshape(N_CHIPS, 2, a.shape[0] // (2 * N_CHIPS), a.shape[1])

    if group.startswith("ffn"):
        tag = group[3:]
        t = grads["wguT" + tag]
        return (["wguT" + tag, "wd" + tag],
                [t.reshape(2, N_CHIPS, t.shape[0] // (2 * N_CHIPS), t.shape[1]), rows_by_chip(grads["wd" + tag])], [True, False])
    if group == "sg":
        return ["sg_w_inT", "sg_w_out"], [rows_by_chip(grads["sg_w_inT"]), rows_by_chip(grads["sg_w_out"])], [False, False]
    dn_in = jnp.stack([jnp.stack(_split_cols(hf, N_CHIPS)) for hf in _split_rows(grads["dn_w_in"], 2)])
    return ["dn_w_in", "dn_w_out"], [dn_in, rows_by_chip(grads["dn_w_out"])], [True, False]


_SHARD_PIECES = {
    "ffn_w_gate": (["wguT%d%d" % ij for ij in _IJ], "lo"),
    "ffn_w_up": (["wguT%d%d" % ij for ij in _IJ], "hi"),
    "ffn_w_down": (["wd%d%d" % ij for ij in _IJ], "rows"),
    "dn_w_in": (["dn_w_in"], "rows"),
    "dn_w_out": (["dn_w_out"], "rows"),
    "sg_w_in": (["sg_w_inT"], "cols"),
    "sg_w_out": (["sg_w_out"], "rows"),
}


def kernel(x, norm_g, ffn_w_gate, ffn_w_up, ffn_w_down, dn_w_in, dn_conv_w, dn_a_log, dn_dt_bias, dn_norm_g, dn_w_out, sg_w_in, sg_b_in, sg_ln_g, sg_ln_b, sg_w_s, sg_b_s, sg_w_out, loss_target, m_norm_g, m_ffn_w_gate, m_ffn_w_up, m_ffn_w_down, m_dn_w_in, m_dn_conv_w, m_dn_a_log, m_dn_dt_bias, m_dn_norm_g, m_dn_w_out, m_sg_w_in, m_sg_b_in, m_sg_ln_g, m_sg_ln_b, m_sg_w_s, m_sg_b_s, m_sg_w_out, v_norm_g, v_ffn_w_gate, v_ffn_w_up, v_ffn_w_down, v_dn_w_in, v_dn_conv_w, v_dn_a_log, v_dn_dt_bias, v_dn_norm_g, v_dn_w_out, v_sg_w_in, v_sg_b_in, v_sg_ln_g, v_sg_ln_b, v_sg_w_s, v_sg_b_s, v_sg_w_out):
    args = dict(locals())
    w = {k: args[k] for k in _WEIGHTS}
    mom = {k: args["m_" + k] for k in _WEIGHTS}
    var = {k: args["v_" + k] for k in _WEIGHTS}
    cx, cy, cc = _mesh_pos()
    chip = 2 * cx + cy

    small_shapes = [w[k].shape for k in _SMALL_SHARDED]
    groups = _weight_groups(w)
    own = groups[_GROUP_ORDER[0]] + [_pack([w[k] for k in _SMALL_SHARDED])]
    first = _allgather_chips([_as_halves(a) for a in own], "gather_first")
    started, after = {}, first[0]
    for g in _GROUP_ORDER[1:]:
        started[g] = _copies_start("gather", groups[g], after, "gather_start_" + g)
        after = started[g]["token"]
    small_k = [_unpack(pack, small_shapes) for pack in _with_own(first[-1], own[-1], chip)]
    p = {name: jnp.concatenate([small_k[k][i] for k in range(N_CHIPS)], axis=-1) for i, name in enumerate(_SMALL_SHARDED)}
    p = {k: (v if k == "norm_g" else v[0]) for k, v in p.items()}
    p["norm_g"] = p["norm_g"] + after[0, 0]
    for k in _SMALL_REPL:
        p[k] = w[k][0]

    def weights_for(group, after):
        if group == _GROUP_ORDER[0]:
            return _ffn_weights(chip, own[:-1], first[:-1])
        srcs, lands = _copies_wait(started[group], after, "gather_wait_" + group)
        if group.startswith("ffn"):
            return _ffn_weights(chip, srcs, lands)
        return _group_matrices(group, [_with_own(l, a, chip) for l, a in zip(lands, srcs)])

    mine, theirs, to_core, to_chips = {}, {}, [], []

    def send_to_chips(after):
        group, names, flags, swap = to_core.pop(0)
        halves, got = _copies_wait(swap, after, "swap_wait_" + group)
        pair_sum = [_add_own_half(h, o, cc, hf, "pair_sum_" + n) for n, h, o, hf in zip(names, halves, got, flags)]
        scatter = _copies_start("scatter", pair_sum, got[0], "reduce_start_" + group)
        to_chips.append((group, names, scatter))
        return scatter["token"]

    def finish(after):
        group, names, scatter = to_chips.pop(0)
        pair_sum, got = _copies_wait(scatter, after, "reduce_wait_" + group)
        half_sum = [_sum_chips(a, b, chip, "chip_sum_" + n, transpose=n == "sg_w_inT")
                    for n, a, b in zip(names, pair_sum, got)]
        other = _swap_whole(half_sum, "gather_core_pair_" + group)
        mine.update(zip(names, half_sum))
        theirs.update(zip(names, other))

    def grads_ready(group, grads):
        names, halves, flags = _group_grads(group, grads)
        swap = _copies_start("swap", halves, None, "swap_start_" + group, flags)
        token = swap["token"]
        if to_core:
            token = send_to_chips(token)
            if len(to_chips) > 1:
                finish(token)
        to_core.append((group, names, flags, swap))
        return token[0, 0]

    small_names = _SMALL_SHARDED + _SMALL_REPL
    small = {}

    def small_ready(grads, loss_part):
        parts = [grads[k] for k in small_names]
        small["shapes"] = [g.shape for g in parts] + [(1,)]
        pack = _pack(parts + [loss_part[0, :1]])
        small["exchange"] = _copies_start("all", [pack], None, "small_start")
        return small["exchange"]["token"]

    loss_part, grad_x, grads = _local_step(x[0], loss_target[0], p, weights_for, grads_ready, small_ready)
    token = send_to_chips(to_core[0][3]["token"])
    finish(token)
    (pack,), (packs,) = _copies_wait(small["exchange"], list(theirs.values()), "small_wait")
    summed = _sum_devices(pack, packs, 4 * cx + 2 * cy + cc, "small_sum")
    parts = _unpack(summed, small["shapes"])
    loss = parts[-1][0]
    grad = {}
    for i, k in enumerate(small_names):
        g = parts[i]
        if k in _SMALL_SHARDED:
            n = w[k].shape[-1]
            g = lax.dynamic_slice_in_dim(g, chip * n, n, axis=g.ndim - 1)
        grad[k] = g

    delta, new_m, new_v = {}, {}, {}

    def update(keys):
        for k in keys:
            names, kind = _SHARD_PIECES[k]
            turn = (lambda a: jnp.swapaxes(a, -1, -2)) if names[0].startswith("wguT") else (lambda a: a)
            outs = _adamw_pieces(turn(w[k]), turn(mom[k]), turn(var[k]), [mine[n] for n in names], [theirs[n] for n in names],
                                 cc, kind, "adamw_" + k)
            grad[k], delta[k], new_m[k], new_v[k] = (turn(o) for o in outs)

    shapes = [w[k].shape for k in small_names]
    d, nm, nv = _adamw(_pack([w[k] for k in small_names]), _pack([grad[k] for k in small_names]),
                       _pack([mom[k] for k in small_names]), _pack([var[k] for k in small_names]), "adamw_small")
    for k, a, b, c_ in zip(small_names, _unpack(d, shapes), _unpack(nm, shapes), _unpack(nv, shapes)):
        delta[k], new_m[k], new_v[k] = a, b, c_
    mixers = [k for k in _BIG if not k.startswith("ffn")]
    update(mixers)
    finish([d] + [delta[k] for k in mixers] + list(theirs.values()))
    update([k for k in _BIG if k.startswith("ffn")])

    return (loss, grad_x[None], *[grad[k] for k in _WEIGHTS], *[delta[k] for k in _WEIGHTS],
            *[new_m[k] for k in _WEIGHTS], *[new_v[k] for k in _WEIGHTS])
```
